```python
import jax, jax.numpy as jnp
from jax import lax
import numpy as np

D_MODEL = 1024
BATCH = 8
SEQ = 2048
DEPTH = 1

GRID_W = 64
CTX_LEN = 256
LRU_WIDTH = D_MODEL
LRU_HEADS = 8
LRU_HEAD_DIM = LRU_WIDTH // LRU_HEADS
CONV_WIDTH = 4
CONV_PAD_LEFT = 1
LRU_C = 8.0
SGU_WIDTH = D_MODEL
SGU_GROUPS = 8
SGU_GROUP_DIM = SGU_WIDTH // SGU_GROUPS
CHUNK = 128
D_MIX = LRU_WIDTH + SGU_WIDTH
D_IN = 2 * LRU_WIDTH + 3 * SGU_WIDTH
NORM_EPS = 1e-6
LN_EPS = 1e-5

kernel_name = "hybrid_rglru_chunk_sgu_dit_block"


def rmsnorm(x, g):
    xf = x.astype(jnp.float32)
    y = xf * lax.rsqrt(jnp.mean(xf * xf, axis=-1, keepdims=True) + NORM_EPS)
    return (y * g.astype(jnp.float32)).astype(x.dtype)


def ada_mod(cond, w, b):
    m = jax.nn.silu(cond) @ w + b
    return jnp.split(m, 3, axis=-1)


def project(h, shift, scale, norm_g, w_in):
    hn = rmsnorm(h, norm_g) * (1.0 + scale) + shift
    return hn @ w_in


def short_conv(xa, w, b):
    L = xa.shape[1]
    xp = jnp.pad(xa, ((0, 0), (CONV_PAD_LEFT, CONV_WIDTH - 1 - CONV_PAD_LEFT), (0, 0)))
    y = xp[:, 0:L] * w[0]
    for k in range(1, CONV_WIDTH):
        y = y + xp[:, k:k + L] * w[k]
    return y + b


def _lin_combine(e1, e2):
    a1, b1 = e1
    a2, b2 = e2
    return a1 * a2, a2 * b1 + b2


def rglru_direction(xc, h0, wa, ba, wx, bx, lam, reverse):
    Bn, L, _ = xc.shape
    xh = xc.reshape(Bn, L, LRU_HEADS, LRU_HEAD_DIM)
    r = jax.nn.sigmoid(jnp.einsum('blhi,hij->blhj', xh, wa) + ba).reshape(Bn, L, LRU_WIDTH)
    i = jax.nn.sigmoid(jnp.einsum('blhi,hij->blhj', xh, wx) + bx).reshape(Bn, L, LRU_WIDTH)
    log_a = -LRU_C * r * jax.nn.softplus(-lam.astype(jnp.float32))
    a = jnp.exp(log_a)
    u = jnp.sqrt(-jnp.expm1(2.0 * log_a)) * (i * xc)
    a_cum, h = lax.associative_scan(_lin_combine, (a, u), reverse=reverse, axis=1)
    h = h + a_cum * h0[:, None, :]
    final = h[:, 0] if reverse else h[:, -1]
    return h, final


def rglru_bidir(xc, h0f, h0b, wa, ba, wx, bx, lam):
    hf, ff = rglru_direction(xc, h0f, wa[0], ba[0], wx[0], bx[0], lam[0], False)
    hb, fb = rglru_direction(xc, h0b, wa[1], ba[1], wx[1], bx[1], lam[1], True)
    return hf + hb, ff, fb


def chunk_sgu(u, v, ln_g, ln_b, w_s, b_s, n_chunks):
    Bn = u.shape[0]
    vf = v.astype(jnp.float32)
    mu = jnp.mean(vf, axis=-1, keepdims=True)
    var = jnp.mean(jnp.square(vf - mu), axis=-1, keepdims=True)
    vn = (vf - mu) * lax.rsqrt(var + LN_EPS) * ln_g + ln_b
    vc = vn.reshape(Bn, n_chunks, CHUNK, SGU_GROUPS, SGU_GROUP_DIM)
    mixed = jnp.einsum('gpq,bnqgc->bnpgc', w_s, vc) + b_s.T[None, None, :, :, None]
    return u * mixed.reshape(Bn, n_chunks * CHUNK, SGU_WIDTH).astype(u.dtype)


def split_proj(z):
    W, S = LRU_WIDTH, SGU_WIDTH
    return (z[..., :W], z[..., W:2 * W], z[..., 2 * W:2 * W + S],
            z[..., 2 * W + S:2 * W + 2 * S], z[..., 2 * W + 2 * S:])


def mixer_out(y_lru, ga, y_sgu, gb, w_out):
    y = jnp.concatenate([y_lru * jax.nn.silu(ga), y_sgu * jax.nn.silu(gb)], axis=-1)
    return y @ w_out


def _fwd_setup_inputs(seed: int = 0) -> dict:
    key = jax.random.key(seed)
    ks = jax.random.split(key, 24)
    nrm = jax.random.normal
    a_c = jax.random.uniform(ks[13], (DEPTH, 2, LRU_WIDTH), minval=0.9, maxval=0.999)
    s = a_c ** (1.0 / LRU_C)
    return {
        "x": nrm(ks[0], (BATCH, SEQ, D_MODEL)),
        "c": nrm(ks[1], (BATCH, D_MODEL)),
        "ctx": nrm(ks[2], (BATCH, CTX_LEN, D_MODEL)),
        "c_ctx": nrm(ks[3], (D_MODEL,)),
        "ada_w": nrm(ks[4], (DEPTH, D_MODEL, 3 * D_MODEL)) * (0.5 * D_MODEL ** -0.5),
        "ada_b": 0.01 * nrm(ks[5], (DEPTH, 3 * D_MODEL)),
        "norm_g": 1.0 + 0.05 * nrm(ks[6], (DEPTH, D_MODEL)),
        "w_in": nrm(ks[7], (DEPTH, D_MODEL, D_IN)) * D_MODEL ** -0.5,
        "conv_w": nrm(ks[8], (DEPTH, CONV_WIDTH, LRU_WIDTH)) * CONV_WIDTH ** -0.5,
        "conv_b": 0.01 * nrm(ks[9], (DEPTH, LRU_WIDTH)),
        "lru_wa": nrm(ks[10], (DEPTH, 2, LRU_HEADS, LRU_HEAD_DIM, LRU_HEAD_DIM)) * LRU_HEAD_DIM ** -0.5,
        "lru_ba": 0.01 * nrm(ks[11], (DEPTH, 2, LRU_HEADS, LRU_HEAD_DIM)),
        "lru_wx": nrm(ks[12], (DEPTH, 2, LRU_HEADS, LRU_HEAD_DIM, LRU_HEAD_DIM)) * LRU_HEAD_DIM ** -0.5,
        "lru_bx": 0.01 * nrm(ks[14], (DEPTH, 2, LRU_HEADS, LRU_HEAD_DIM)),
        "lru_lambda": jnp.log(s) - jnp.log1p(-s),
        "sgu_ln_g": 1.0 + 0.05 * nrm(ks[15], (DEPTH, SGU_WIDTH)),
        "sgu_ln_b": 0.01 * nrm(ks[16], (DEPTH, SGU_WIDTH)),
        "sgu_w": nrm(ks[17], (DEPTH, SGU_GROUPS, CHUNK, CHUNK)) * (0.5 * CHUNK ** -0.5),
        "sgu_b": 1.0 + 0.1 * nrm(ks[18], (DEPTH, SGU_GROUPS, CHUNK)),
        "w_out": nrm(ks[19], (DEPTH, D_MIX, D_MODEL)) * D_MIX ** -0.5,
        "final_g": 1.0 + 0.05 * nrm(ks[20], (D_MODEL,)),
    }


def _fwd_reference(x, c, ctx, c_ctx, ada_w, ada_b, norm_g, w_in, conv_w, conv_b, lru_wa, lru_ba,
              lru_wx, lru_bx, lru_lambda, sgu_ln_g, sgu_ln_b, sgu_w, sgu_b, w_out, final_g):
    Bn, L, _ = x.shape
    rows = L // GRID_W
    n_chunks = rows * GRID_W // CHUNK
    n_ctx_chunks = ctx.shape[1] // CHUNK
    zeros = jnp.zeros((Bn, LRU_WIDTH), jnp.float32)
    for layer in range(DEPTH):
        sh_x, sc_x, g_x = ada_mod(c[:, None, :], ada_w[layer], ada_b[layer])
        sh_c, sc_c, g_c = ada_mod(c_ctx[None, None, :], ada_w[layer], ada_b[layer])
        lru_p = (lru_wa[layer], lru_ba[layer], lru_wx[layer], lru_bx[layer], lru_lambda[layer])
        last = layer == DEPTH - 1

        ctx_cols = LRU_WIDTH if last else D_IN
        zc = project(ctx, sh_c, sc_c, norm_g[layer], w_in[layer][:, :ctx_cols])
        xc_c = short_conv(zc[..., :LRU_WIDTH], conv_w[layer], conv_b[layer]).astype(jnp.float32)
        y_c, hf_c, hb_c = rglru_bidir(xc_c, zeros, zeros, *lru_p)

        zx = project(x, sh_x, sc_x, norm_g[layer], w_in[layer])
        xa_x, ga_x, u_x, v_x, gb_x = split_proj(zx)
        xc_x = short_conv(xa_x, conv_w[layer], conv_b[layer]).astype(jnp.float32)
        y_l, _, _ = rglru_bidir(xc_x, hf_c, hb_c, *lru_p)
        y_s = chunk_sgu(jax.nn.gelu(u_x), jax.nn.gelu(v_x), sgu_ln_g[layer], sgu_ln_b[layer],
                        sgu_w[layer], sgu_b[layer], n_chunks)
        x_new = x + g_x * mixer_out(y_l.astype(x.dtype), ga_x, y_s, gb_x, w_out[layer])

        if not last:
            _, ga_c, u_c, v_c, gb_c = split_proj(zc)
            y_sc = chunk_sgu(jax.nn.gelu(u_c), jax.nn.gelu(v_c), sgu_ln_g[layer], sgu_ln_b[layer],
                             sgu_w[layer], sgu_b[layer], n_ctx_chunks)
            ctx = ctx + g_c * mixer_out(y_c.astype(ctx.dtype), ga_c, y_sc, gb_c, w_out[layer])
        x = x_new
    return rmsnorm(x, final_g)


import jax as _jax
import jax.numpy as _jnp

TWIN_FORMAT = 'train_step'
FWD_PARAMS = ['x', 'c', 'ctx', 'c_ctx', 'ada_w', 'ada_b', 'norm_g', 'w_in', 'conv_w', 'conv_b', 'lru_wa', 'lru_ba', 'lru_wx', 'lru_bx', 'lru_lambda', 'sgu_ln_g', 'sgu_ln_b', 'sgu_w', 'sgu_b', 'w_out', 'final_g']
TWIN_WEIGHTS = ['c_ctx', 'ada_w', 'ada_b', 'norm_g', 'w_in', 'conv_w', 'conv_b', 'lru_wa', 'lru_ba', 'lru_wx', 'lru_bx', 'lru_lambda', 'sgu_ln_g', 'sgu_ln_b', 'sgu_w', 'sgu_b', 'w_out', 'final_g']
TWIN_DIFF_INPUT = 'x'
TWIN_INPUTS = ['x', 'c', 'ctx', 'c_ctx', 'ada_w', 'ada_b', 'norm_g', 'w_in', 'conv_w', 'conv_b', 'lru_wa', 'lru_ba', 'lru_wx', 'lru_bx', 'lru_lambda', 'sgu_ln_g', 'sgu_ln_b', 'sgu_w', 'sgu_b', 'w_out', 'final_g', 'loss_target', 'm_c_ctx', 'm_ada_w', 'm_ada_b', 'm_norm_g', 'm_w_in', 'm_conv_w', 'm_conv_b', 'm_lru_wa', 'm_lru_ba', 'm_lru_wx', 'm_lru_bx', 'm_lru_lambda', 'm_sgu_ln_g', 'm_sgu_ln_b', 'm_sgu_w', 'm_sgu_b', 'm_w_out', 'm_final_g', 'v_c_ctx', 'v_ada_w', 'v_ada_b', 'v_norm_g', 'v_w_in', 'v_conv_w', 'v_conv_b', 'v_lru_wa', 'v_lru_ba', 'v_lru_wx', 'v_lru_bx', 'v_lru_lambda', 'v_sgu_ln_g', 'v_sgu_ln_b', 'v_sgu_w', 'v_sgu_b', 'v_w_out', 'v_final_g']
TWIN_OUTPUTS = ['loss', 'grad_x', 'grad_c_ctx', 'grad_ada_w', 'grad_ada_b', 'grad_norm_g', 'grad_w_in', 'grad_conv_w', 'grad_conv_b', 'grad_lru_wa', 'grad_lru_ba', 'grad_lru_wx', 'grad_lru_bx', 'grad_lru_lambda', 'grad_sgu_ln_g', 'grad_sgu_ln_b', 'grad_sgu_w', 'grad_sgu_b', 'grad_w_out', 'grad_final_g', 'delta_c_ctx', 'delta_ada_w', 'delta_ada_b', 'delta_norm_g', 'delta_w_in', 'delta_conv_w', 'delta_conv_b', 'delta_lru_wa', 'delta_lru_ba', 'delta_lru_wx', 'delta_lru_bx', 'delta_lru_lambda', 'delta_sgu_ln_g', 'delta_sgu_ln_b', 'delta_sgu_w', 'delta_sgu_b', 'delta_w_out', 'delta_final_g', 'new_m_c_ctx', 'new_m_ada_w', 'new_m_ada_b', 'new_m_norm_g', 'new_m_w_in', 'new_m_conv_w', 'new_m_conv_b', 'new_m_lru_wa', 'new_m_lru_ba', 'new_m_lru_wx', 'new_m_lru_bx', 'new_m_lru_lambda', 'new_m_sgu_ln_g', 'new_m_sgu_ln_b', 'new_m_sgu_w', 'new_m_sgu_b', 'new_m_w_out', 'new_m_final_g', 'new_v_c_ctx', 'new_v_ada_w', 'new_v_ada_b', 'new_v_norm_g', 'new_v_w_in', 'new_v_conv_w', 'new_v_conv_b', 'new_v_lru_wa', 'new_v_lru_ba', 'new_v_lru_wx', 'new_v_lru_bx', 'new_v_lru_lambda', 'new_v_sgu_ln_g', 'new_v_sgu_ln_b', 'new_v_sgu_w', 'new_v_sgu_b', 'new_v_w_out', 'new_v_final_g']
TWIN_LEAF_KINDS = {'loss': 'loss', 'grad_x': 'grad_x', 'grad_c_ctx': 'grad_w', 'grad_ada_w': 'grad_w', 'grad_ada_b': 'grad_w', 'grad_norm_g': 'grad_w', 'grad_w_in': 'grad_w', 'grad_conv_w': 'grad_w', 'grad_conv_b': 'grad_w', 'grad_lru_wa': 'grad_w', 'grad_lru_ba': 'grad_w', 'grad_lru_wx': 'grad_w', 'grad_lru_bx': 'grad_w', 'grad_lru_lambda': 'grad_w', 'grad_sgu_ln_g': 'grad_w', 'grad_sgu_ln_b': 'grad_w', 'grad_sgu_w': 'grad_w', 'grad_sgu_b': 'grad_w', 'grad_w_out': 'grad_w', 'grad_final_g': 'grad_w', 'delta_c_ctx': 'delta_w', 'delta_ada_w': 'delta_w', 'delta_ada_b': 'delta_w', 'delta_norm_g': 'delta_w', 'delta_w_in': 'delta_w', 'delta_conv_w': 'delta_w', 'delta_conv_b': 'delta_w', 'delta_lru_wa': 'delta_w', 'delta_lru_ba': 'delta_w', 'delta_lru_wx': 'delta_w', 'delta_lru_bx': 'delta_w', 'delta_lru_lambda': 'delta_w', 'delta_sgu_ln_g': 'delta_w', 'delta_sgu_ln_b': 'delta_w', 'delta_sgu_w': 'delta_w', 'delta_sgu_b': 'delta_w', 'delta_w_out': 'delta_w', 'delta_final_g': 'delta_w', 'new_m_c_ctx': 'new_m', 'new_m_ada_w': 'new_m', 'new_m_ada_b': 'new_m', 'new_m_norm_g': 'new_m', 'new_m_w_in': 'new_m', 'new_m_conv_w': 'new_m', 'new_m_conv_b': 'new_m', 'new_m_lru_wa': 'new_m', 'new_m_lru_ba': 'new_m', 'new_m_lru_wx': 'new_m', 'new_m_lru_bx': 'new_m', 'new_m_lru_lambda': 'new_m', 'new_m_sgu_ln_g': 'new_m', 'new_m_sgu_ln_b': 'new_m', 'new_m_sgu_w': 'new_m', 'new_m_sgu_b': 'new_m', 'new_m_w_out': 'new_m', 'new_m_final_g': 'new_m', 'new_v_c_ctx': 'new_v', 'new_v_ada_w': 'new_v', 'new_v_ada_b': 'new_v', 'new_v_norm_g': 'new_v', 'new_v_w_in': 'new_v', 'new_v_conv_w': 'new_v', 'new_v_conv_b': 'new_v', 'new_v_lru_wa': 'new_v', 'new_v_lru_ba': 'new_v', 'new_v_lru_wx': 'new_v', 'new_v_lru_bx': 'new_v', 'new_v_lru_lambda': 'new_v', 'new_v_sgu_ln_g': 'new_v', 'new_v_sgu_ln_b': 'new_v', 'new_v_sgu_w': 'new_v', 'new_v_sgu_b': 'new_v', 'new_v_w_out': 'new_v', 'new_v_final_g': 'new_v'}


def _forward(args):
    return _fwd_reference(*[args[k] for k in FWD_PARAMS])


def _output_shape():
    out = _jax.eval_shape(lambda: _forward(_fwd_setup_inputs(0)))
    return out.shape, out.dtype

N_MICROBATCH = 1
ADAM_LR = 0.001
ADAM_B1 = 0.9
ADAM_B2 = 0.999
ADAM_EPS = 1e-08
ADAM_WD = 0.01
ADAM_STEP = 10
PER_EXAMPLE_BATCH_AXIS = {'x': 0, 'c': 0, 'ctx': 0, 'loss_target': 0}
SHARED_INPUTS = []
_WEIGHT_DTYPES = {'c_ctx': _jnp.float32, 'ada_w': _jnp.float32, 'ada_b': _jnp.float32, 'norm_g': _jnp.float32, 'w_in': _jnp.float32, 'conv_w': _jnp.float32, 'conv_b': _jnp.float32, 'lru_wa': _jnp.float32, 'lru_ba': _jnp.float32, 'lru_wx': _jnp.float32, 'lru_bx': _jnp.float32, 'lru_lambda': _jnp.float32, 'sgu_ln_g': _jnp.float32, 'sgu_ln_b': _jnp.float32, 'sgu_w': _jnp.float32, 'sgu_b': _jnp.float32, 'w_out': _jnp.float32, 'final_g': _jnp.float32}
MOMENT_SCALE = {'c_ctx': 1.686017e-02, 'ada_w': 1.871556e-01, 'ada_b': 3.345774e-01, 'norm_g': 6.965875e-02, 'w_in': 4.397714e-02, 'conv_w': 7.596534e-02, 'conv_b': 2.357405e-01, 'lru_wa': 4.097201e-03, 'lru_ba': 6.230832e-03, 'lru_wx': 8.502193e-03, 'lru_bx': 1.624558e-02, 'lru_lambda': 1.521435e-02, 'sgu_ln_g': 6.343277e-03, 'sgu_ln_b': 6.334371e-03, 'sgu_w': 1.233489e-02, 'sgu_b': 1.237540e-02, 'w_out': 6.971815e-02, 'final_g': 1.619384e+01}


def _to_microbatches(a, axis):
    t = _jnp.moveaxis(a, axis, 0)
    t = t.reshape((N_MICROBATCH, t.shape[0] // N_MICROBATCH) + t.shape[1:])
    return _jnp.moveaxis(t, 1, axis + 1)


def setup_inputs(seed: int = 0) -> dict:
    inp = _fwd_setup_inputs(seed)
    key = _jax.random.fold_in(_jax.random.key(seed), 7919)
    shape, _ = _output_shape()
    out = dict(inp)
    out["loss_target"] = _jax.random.normal(_jax.random.fold_in(key, 0), shape, _jnp.float32)
    for i, name in enumerate(TWIN_WEIGHTS):
        w = inp[name].astype(_jnp.float32)
        if MOMENT_SCALE is None:
            s = _jnp.sqrt(_jnp.mean(_jnp.square(w)) + 1e-30)
        else:
            s = MOMENT_SCALE[name]
        km, kv = _jax.random.split(_jax.random.fold_in(key, i + 1))
        out[name] = w
        out["m_" + name] = s * _jax.random.normal(km, w.shape, _jnp.float32)
        out["v_" + name] = (s * s) * _jax.random.uniform(kv, w.shape, _jnp.float32, 0.5, 1.5)
    if N_MICROBATCH > 1:
        for name, axis in PER_EXAMPLE_BATCH_AXIS.items():
            out[name] = _to_microbatches(out[name], axis)
    return {'x': out['x'], 'c': out['c'], 'ctx': out['ctx'], 'c_ctx': out['c_ctx'], 'ada_w': out['ada_w'], 'ada_b': out['ada_b'], 'norm_g': out['norm_g'], 'w_in': out['w_in'], 'conv_w': out['conv_w'], 'conv_b': out['conv_b'], 'lru_wa': out['lru_wa'], 'lru_ba': out['lru_ba'], 'lru_wx': out['lru_wx'], 'lru_bx': out['lru_bx'], 'lru_lambda': out['lru_lambda'], 'sgu_ln_g': out['sgu_ln_g'], 'sgu_ln_b': out['sgu_ln_b'], 'sgu_w': out['sgu_w'], 'sgu_b': out['sgu_b'], 'w_out': out['w_out'], 'final_g': out['final_g'], 'loss_target': out['loss_target'], 'm_c_ctx': out['m_c_ctx'], 'm_ada_w': out['m_ada_w'], 'm_ada_b': out['m_ada_b'], 'm_norm_g': out['m_norm_g'], 'm_w_in': out['m_w_in'], 'm_conv_w': out['m_conv_w'], 'm_conv_b': out['m_conv_b'], 'm_lru_wa': out['m_lru_wa'], 'm_lru_ba': out['m_lru_ba'], 'm_lru_wx': out['m_lru_wx'], 'm_lru_bx': out['m_lru_bx'], 'm_lru_lambda': out['m_lru_lambda'], 'm_sgu_ln_g': out['m_sgu_ln_g'], 'm_sgu_ln_b': out['m_sgu_ln_b'], 'm_sgu_w': out['m_sgu_w'], 'm_sgu_b': out['m_sgu_b'], 'm_w_out': out['m_w_out'], 'm_final_g': out['m_final_g'], 'v_c_ctx': out['v_c_ctx'], 'v_ada_w': out['v_ada_w'], 'v_ada_b': out['v_ada_b'], 'v_norm_g': out['v_norm_g'], 'v_w_in': out['v_w_in'], 'v_conv_w': out['v_conv_w'], 'v_conv_b': out['v_conv_b'], 'v_lru_wa': out['v_lru_wa'], 'v_lru_ba': out['v_lru_ba'], 'v_lru_wx': out['v_lru_wx'], 'v_lru_bx': out['v_lru_bx'], 'v_lru_lambda': out['v_lru_lambda'], 'v_sgu_ln_g': out['v_sgu_ln_g'], 'v_sgu_ln_b': out['v_sgu_ln_b'], 'v_sgu_w': out['v_sgu_w'], 'v_sgu_b': out['v_sgu_b'], 'v_w_out': out['v_w_out'], 'v_final_g': out['v_final_g']}


def _loss(weights, diff, rest, loss_target):
    with _jax.named_scope("forward"):
        args = {**rest, TWIN_DIFF_INPUT: diff, **{k: w.astype(_WEIGHT_DTYPES[k]) for k, w in weights.items()}}
        y = _forward(args)
    with _jax.named_scope("loss_head"):
        err = _jnp.square(y.astype(_jnp.float32) - loss_target)
        return 0.5 * _jnp.sum(_jnp.mean(err, axis=-1)) if err.ndim else 0.5 * err


def _adamw(w, g, m, v):
    m = ADAM_B1 * m + (1.0 - ADAM_B1) * g
    v = ADAM_B2 * v + (1.0 - ADAM_B2) * _jnp.square(g)
    m_hat = m / (1.0 - ADAM_B1 ** ADAM_STEP)
    v_hat = v / (1.0 - ADAM_B2 ** ADAM_STEP)
    delta = -ADAM_LR * (m_hat / (_jnp.sqrt(v_hat) + ADAM_EPS) + ADAM_WD * w)
    return delta, m, v


def reference(x, c, ctx, c_ctx, ada_w, ada_b, norm_g, w_in, conv_w, conv_b, lru_wa, lru_ba, lru_wx, lru_bx, lru_lambda, sgu_ln_g, sgu_ln_b, sgu_w, sgu_b, w_out, final_g, loss_target, m_c_ctx, m_ada_w, m_ada_b, m_norm_g, m_w_in, m_conv_w, m_conv_b, m_lru_wa, m_lru_ba, m_lru_wx, m_lru_bx, m_lru_lambda, m_sgu_ln_g, m_sgu_ln_b, m_sgu_w, m_sgu_b, m_w_out, m_final_g, v_c_ctx, v_ada_w, v_ada_b, v_norm_g, v_w_in, v_conv_w, v_conv_b, v_lru_wa, v_lru_ba, v_lru_wx, v_lru_bx, v_lru_lambda, v_sgu_ln_g, v_sgu_ln_b, v_sgu_w, v_sgu_b, v_w_out, v_final_g):
    given = dict(x=x, c=c, ctx=ctx, c_ctx=c_ctx, ada_w=ada_w, ada_b=ada_b, norm_g=norm_g, w_in=w_in, conv_w=conv_w, conv_b=conv_b, lru_wa=lru_wa, lru_ba=lru_ba, lru_wx=lru_wx, lru_bx=lru_bx, lru_lambda=lru_lambda, sgu_ln_g=sgu_ln_g, sgu_ln_b=sgu_ln_b, sgu_w=sgu_w, sgu_b=sgu_b, w_out=w_out, final_g=final_g, loss_target=loss_target, m_c_ctx=m_c_ctx, m_ada_w=m_ada_w, m_ada_b=m_ada_b, m_norm_g=m_norm_g, m_w_in=m_w_in, m_conv_w=m_conv_w, m_conv_b=m_conv_b, m_lru_wa=m_lru_wa, m_lru_ba=m_lru_ba, m_lru_wx=m_lru_wx, m_lru_bx=m_lru_bx, m_lru_lambda=m_lru_lambda, m_sgu_ln_g=m_sgu_ln_g, m_sgu_ln_b=m_sgu_ln_b, m_sgu_w=m_sgu_w, m_sgu_b=m_sgu_b, m_w_out=m_w_out, m_final_g=m_final_g, v_c_ctx=v_c_ctx, v_ada_w=v_ada_w, v_ada_b=v_ada_b, v_norm_g=v_norm_g, v_w_in=v_w_in, v_conv_w=v_conv_w, v_conv_b=v_conv_b, v_lru_wa=v_lru_wa, v_lru_ba=v_lru_ba, v_lru_wx=v_lru_wx, v_lru_bx=v_lru_bx, v_lru_lambda=v_lru_lambda, v_sgu_ln_g=v_sgu_ln_g, v_sgu_ln_b=v_sgu_ln_b, v_sgu_w=v_sgu_w, v_sgu_b=v_sgu_b, v_w_out=v_w_out, v_final_g=v_final_g)
    weights = {n: given[n] for n in TWIN_WEIGHTS}
    shared = {n: given[n] for n in SHARED_INPUTS}
    per_example = {n: given[n] for n in ['x', 'c', 'ctx']}
    grad_fn = _jax.value_and_grad(_loss, argnums=(0, 1))

    def one_microbatch(ex, loss_target):
        ex = dict(ex)
        diff = ex.pop(TWIN_DIFF_INPUT)
        return grad_fn(weights, diff, {**shared, **ex}, loss_target)

    if N_MICROBATCH == 1:
        loss, (grad_w, grad_x) = one_microbatch(per_example, given["loss_target"])
    else:
        def body(carry, xs):
            loss_sum, grad_sum = carry
            l_k, (gw_k, gx_k) = one_microbatch(xs[0], xs[1])
            with _jax.named_scope("update"):
                return (loss_sum + l_k, _jax.tree.map(_jnp.add, grad_sum, gw_k)), gx_k

        init = (_jnp.zeros((), _jnp.float32), _jax.tree.map(_jnp.zeros_like, weights))
        (loss, grad_w), grad_x = _jax.lax.scan(body, init, (per_example, given["loss_target"]))
    with _jax.named_scope("update"):
        delta_w, new_m, new_v = {}, {}, {}
        for n in TWIN_WEIGHTS:
            delta_w[n], new_m[n], new_v[n] = _adamw(weights[n], grad_w[n], given["m_" + n], given["v_" + n])
    return (loss, grad_x, *[grad_w[n] for n in TWIN_WEIGHTS], *[delta_w[n] for n in TWIN_WEIGHTS],
            *[new_m[n] for n in TWIN_WEIGHTS], *[new_v[n] for n in TWIN_WEIGHTS])
```

```python
import functools

import jax
import jax.numpy as jnp
from jax import lax
from jax.experimental import pallas as pl
from jax.experimental.pallas import tpu as pltpu

F32 = jnp.float32
BF16 = jnp.bfloat16

N_DEV = 8
D = 1024
L = 2048
LC = 256
HEADS = 8
HD = 128
CHUNK = 128
D_IN = 5 * D
D_MIX = 2 * D
CONV_W = 4
LRU_C = 8.0
NORM_EPS = 1e-6
LN_EPS = 1e-5
ADAM_LR, ADAM_B1, ADAM_B2, ADAM_EPS, ADAM_WD, ADAM_STEP = 0.001, 0.9, 0.999, 1e-08, 0.01, 10

VMEM_LIMIT = 56 * 1024 * 1024

HBM = pl.BlockSpec(memory_space=pltpu.HBM)
VMEM = pl.BlockSpec(memory_space=pltpu.VMEM)
MESH = pl.DeviceIdType.MESH


def _call(body, **kw):
    return pl.pallas_call(body, **kw)


def _params(*sem):
    return pltpu.CompilerParams(dimension_semantics=sem, vmem_limit_bytes=VMEM_LIMIT)


def _sigmoid(x):
    return 1.0 / (1.0 + jnp.exp(-x))


def _silu_and_grad(x):
    s = _sigmoid(x)
    return x * s, s * (1.0 + x * (1.0 - s))


_G0 = 0.7978845608028654
_G1 = 0.044715


def _gelu_and_grad(x):
    x2 = x * x
    t = jnp.tanh(_G0 * (x + _G1 * x * x2))
    cdf = 0.5 * (1.0 + t)
    return x * cdf, cdf + 0.5 * x * (1.0 - t * t) * (_G0 * (1.0 + 3.0 * _G1 * x2))


def _gelu(x):
    return 0.5 * x * (1.0 + jnp.tanh(_G0 * (x + _G1 * x * x * x)))


def _expm1(x):
    series = x * (1.0 + x * (0.5 + x * (1.0 / 6.0 + x * (1.0 / 24.0 + x * (1.0 / 120.0)))))
    return jnp.where(jnp.abs(x) < 0.05, series, jnp.exp(x) - 1.0)


def _softplus(z):
    t = jnp.exp(-jnp.abs(z))
    u = 1.0 + t
    log1p = jnp.where(u == 1.0, t, jnp.log(u) * t / jnp.where(u == 1.0, 1.0, u - 1.0))
    return jnp.maximum(z, 0.0) + log1p


def _dot(a, b):
    return jnp.dot(a, b, preferred_element_type=F32)


def _dot_nt(a, b):
    return lax.dot_general(a, b, (((1,), (1,)), ((), ())), preferred_element_type=F32)


def _dot_tn(a, b):
    return lax.dot_general(a, b, (((0,), (0,)), ((), ())), preferred_element_type=F32)


def _rows(shape):
    return lax.broadcasted_iota(jnp.int32, shape, 0)


def _shift_down(x, first):
    return jnp.where(_rows(x.shape) == 0, first, pltpu.roll(x, 1, 0))


def _shift_up(x, last):
    n = x.shape[0]
    return jnp.where(_rows(x.shape) == n - 1, last, pltpu.roll(x, n - 1, 0))


def _exchange(arrays, modes, name):
    n = len(arrays)
    out_shape = []
    for a, m in zip(arrays, modes):
        if m == "ag":
            out_shape.append(jax.ShapeDtypeStruct((N_DEV,) + a.shape, a.dtype))
        elif m == "agc":
            out_shape.append(jax.ShapeDtypeStruct((a.shape[0], N_DEV * a.shape[1]), a.dtype))
        elif m == "a2ac":
            out_shape.append(jax.ShapeDtypeStruct((N_DEV, a.shape[0], a.shape[1] // N_DEV), a.dtype))
        else:
            out_shape.append(jax.ShapeDtypeStruct(a.shape, a.dtype))

    def body(*refs):
        ins, outs = refs[:n], refs[n:2 * n]
        send_sems, recv_sems, local_sems = refs[2 * n:]
        x, y, c = lax.axis_index("x"), lax.axis_index("y"), lax.axis_index("c")
        me = 4 * x + 2 * y + c

        def mine(j):
            if modes[j] == "agc":
                w = ins[j].shape[1]
                return outs[j].at[:, pl.ds(pl.multiple_of(me * w, 128), w)]
            return outs[j].at[me]

        def piece(j, dev):
            if modes[j] == "a2a":
                return ins[j].at[dev]
            if modes[j] == "a2ac":
                w = ins[j].shape[1] // N_DEV
                return ins[j].at[:, pl.ds(pl.multiple_of(dev * w, 128), w)]
            return ins[j]

        copies = []
        for j in range(n):
            for k in range(1, N_DEV):
                px, py, pc = x ^ (k >> 2), y ^ ((k >> 1) & 1), c ^ (k & 1)
                cp = pltpu.make_async_remote_copy(src_ref=piece(j, 4 * px + 2 * py + pc), dst_ref=mine(j),
                                                  send_sem=send_sems.at[j, k - 1], recv_sem=recv_sems.at[j, k - 1],
                                                  device_id=(px, py, pc), device_id_type=MESH)
                cp.start()
                copies.append(cp)
            own = pltpu.make_async_copy(piece(j, me), mine(j), local_sems.at[j])
            own.start()
            copies.append(own)
        for cp in copies:
            cp.wait()

    return _call(
        body, name=name, out_shape=out_shape, in_specs=[HBM] * n, out_specs=[HBM] * n,
        scratch_shapes=[pltpu.SemaphoreType.DMA((n, N_DEV - 1)), pltpu.SemaphoreType.DMA((n, N_DEV - 1)),
                        pltpu.SemaphoreType.DMA((n,))],
        compiler_params=pltpu.CompilerParams(has_side_effects=True),
    )(*[pltpu.with_memory_space_constraint(a, pltpu.HBM) for a in arrays])


def _ada_forward(c_all, c_ctx, ada_w, ada_b, me):
    nloc = ada_w.shape[1]

    def body(me_ref, c_ref, cc_ref, w_ref, b_ref, o_ref):
        off = pl.multiple_of(me_ref[0] * nloc, 128)
        b = b_ref[:, pl.ds(off, nloc)]
        w = w_ref[...]
        sx, _ = _silu_and_grad(c_ref[...])
        sc, _ = _silu_and_grad(jnp.broadcast_to(cc_ref[...], (8, D)))
        o_ref[0:8, :] = _dot(sx, w) + b
        o_ref[8:16, :] = _dot(sc, w) + b

    return _call(
        body, name="ada_forward", out_shape=jax.ShapeDtypeStruct((16, nloc), F32),
        in_specs=[pl.BlockSpec(memory_space=pltpu.SMEM), VMEM, VMEM, VMEM, VMEM], out_specs=VMEM,
    )(me, c_all, c_ctx, ada_w, ada_b)


def _cast_weights(w_in, w_out):
    def body(a_ref, b_ref, ao_ref, bo_ref):
        ao_ref[...] = a_ref[...].astype(BF16)
        bo_ref[...] = b_ref[...].astype(BF16)

    return _call(
        body, name="cast_weights",
        out_shape=[jax.ShapeDtypeStruct(w_in.shape, BF16), jax.ShapeDtypeStruct(w_out.shape, BF16)],
        in_specs=[VMEM, VMEM], out_specs=[VMEM, VMEM], compiler_params=_params(),
    )(w_in, w_out)


def _project(xr, mod, ng, w, ncols, tm, name):
    rows = xr.shape[0]

    def body(x_ref, sh_ref, sc_ref, ng_ref, w_ref, z_ref, hn_ref):
        x = x_ref[...]
        rs = lax.rsqrt(jnp.mean(x * x, axis=-1, keepdims=True) + NORM_EPS)
        hn = (x * rs * ng_ref[...]) * (1.0 + sc_ref[...]) + sh_ref[...]
        hb = hn.astype(BF16)
        hn_ref[...] = hb
        for n in range(ncols // D):
            z_ref[:, n * D:(n + 1) * D] = _dot(hb, w_ref[:, n * D:(n + 1) * D])

    vec = pl.BlockSpec((1, D), lambda i: (0, 0))
    return _call(
        body, name=name, grid=(rows // tm,),
        out_shape=[jax.ShapeDtypeStruct((rows, ncols), F32), jax.ShapeDtypeStruct((rows, D), BF16)],
        in_specs=[pl.BlockSpec((tm, D), lambda i: (i, 0)), vec, pl.BlockSpec((1, D), lambda i: (0, 1)), vec,
                  pl.BlockSpec((D, ncols), lambda i: (0, 0), pipeline_mode=pl.Buffered(1))],
        out_specs=[pl.BlockSpec((tm, ncols), lambda i: (i, 0)), pl.BlockSpec((tm, D), lambda i: (i, 0))],
        compiler_params=_params("arbitrary"),
    )(xr, mod, mod, ng, w)


def _scan_pair(af_ref, uf_ref, hf_ref, h0f, ab_ref, ub_ref, hb_ref, h0b, t_len):
    nb = t_len // 8
    rows = _rows((8, HD))

    def block_f(a, b, carry):
        for s in (1, 2, 4):
            m = rows >= s
            b = a * jnp.where(m, pltpu.roll(b, s, 0), 0.0) + b
            a = a * jnp.where(m, pltpu.roll(a, s, 0), 1.0)
        h = b + a * carry
        return h, h[7:8, :]

    def block_b(a, b, carry):
        for s in (1, 2, 4):
            m = rows < 8 - s
            b = a * jnp.where(m, pltpu.roll(b, 8 - s, 0), 0.0) + b
            a = a * jnp.where(m, pltpu.roll(a, 8 - s, 0), 1.0)
        h = b + a * carry
        return h, h[0:1, :]

    def body(k, carry):
        cf, cb = carry
        of = pl.multiple_of(k * 8, 8)
        ob = pl.multiple_of((nb - 1 - k) * 8, 8)
        hf, cf = block_f(af_ref[pl.ds(of, 8), :], uf_ref[pl.ds(of, 8), :], cf)
        hf_ref[pl.ds(of, 8), :] = hf
        hb, cb = block_b(ab_ref[pl.ds(ob, 8), :], ub_ref[pl.ds(ob, 8), :], cb)
        hb_ref[pl.ds(ob, 8), :] = hb
        return cf, cb

    return lax.fori_loop(0, nb, body, (h0f, h0b))


def _conv(xa, cw, cb):
    z = jnp.zeros((1, HD), F32)
    xm1 = _shift_down(xa, z)
    xp1 = _shift_up(xa, z)
    xp2 = _shift_up(xp1, z)
    return xm1 * cw[0:1, :] + xa * cw[1:2, :] + xp1 * cw[2:3, :] + xp2 * cw[3:4, :] + cb


def _gates(xc, wa, wx, ba, bx, sp):
    xb = xc.astype(BF16)
    r = _sigmoid(_dot(xb, wa) + ba)
    i = _sigmoid(_dot(xb, wx) + bx)
    log_a = (-LRU_C) * r * sp
    a = jnp.exp(log_a)
    gamma = jnp.sqrt(-_expm1(2.0 * log_a))
    return r, i, a, gamma


def _lru_param_specs():
    h4 = pl.BlockSpec((2, 1, HD, HD), lambda h: (0, h, 0, 0))
    v2 = pl.BlockSpec((2, HD), lambda h: (0, h))
    return dict(
        xa=pl.BlockSpec((L, HD), lambda h: (0, h)), xac=pl.BlockSpec((LC, HD), lambda h: (0, h)),
        cw=pl.BlockSpec((CONV_W, HD), lambda h: (0, h)), cb=pl.BlockSpec((1, HD), lambda h: (0, h)), h4=h4, v2=v2)


def _lru_forward(zx, zc, cw, cb, wa, wx, ba, bx, lam):
    def body(xa_ref, xac_ref, cw_ref, cb_ref, wa_ref, wx_ref, ba_ref, bx_ref, lam_ref, yl_ref,
             af, uf, hf, ab, ub, hb):
        cwv, cbv = cw_ref[...], cb_ref[...]
        sp = _softplus(-lam_ref[...])

        def forward(xa, t_len, h0f, h0b):
            xc = _conv(xa, cwv, cbv)
            for d, (a_ref, u_ref) in enumerate(((af, uf), (ab, ub))):
                _, i, a, gamma = _gates(xc, wa_ref[d, 0].astype(BF16), wx_ref[d, 0].astype(BF16),
                                        ba_ref[d:d + 1, :], bx_ref[d:d + 1, :], sp[d:d + 1, :])
                a_ref[0:t_len, :] = a
                u_ref[0:t_len, :] = gamma * (i * xc)
            return _scan_pair(af, uf, hf, h0f, ab, ub, hb, h0b, t_len)

        z = jnp.zeros((1, HD), F32)
        h0f, h0b = forward(xac_ref[...], LC, z, z)
        forward(xa_ref[...], L, h0f, h0b)
        yl_ref[...] = hf[...] + hb[...]

    s = _lru_param_specs()
    return _call(
        body, name="lru_forward", grid=(HEADS,), out_shape=jax.ShapeDtypeStruct((L, D), F32),
        in_specs=[s["xa"], s["xac"], s["cw"], s["cb"], s["h4"], s["h4"], s["v2"], s["v2"], s["v2"]],
        out_specs=pl.BlockSpec((L, HD), lambda h: (0, h)),
        scratch_shapes=[pltpu.VMEM((L, HD), F32)] * 6,
        compiler_params=_params("arbitrary"),
    )(zx, zc, cw, cb, wa, wx, ba, bx, lam)


def _lru_backward(zx, zc, dyl, dz, cw, cb, wa, wx, ba, bx, lam):
    def body(xa_ref, xac_ref, dyl_ref, dz_in, cw_ref, cb_ref, wa_ref, wx_ref, ba_ref, bx_ref, lam_ref,
             dxa_ref, dxac_ref, dwa_ref, dwx_ref, dba_ref, dbx_ref, dlam_ref, dcw_ref, dcb_ref,
             af, uf, hf, ab, ub, hb, rf, rb):
        del dz_in
        cwv, cbv = cw_ref[...], cb_ref[...]
        lamv = lam_ref[...]
        sp = _softplus(-lamv)
        z = jnp.zeros((1, HD), F32)

        def wmat(ref, d):
            return ref[d, 0].astype(BF16)

        def forward(xa, t_len, h0f, h0b):
            xc = _conv(xa, cwv, cbv)
            for d, (a_ref, u_ref) in enumerate(((af, uf), (ab, ub))):
                _, i, a, gamma = _gates(xc, wmat(wa_ref, d), wmat(wx_ref, d),
                                        ba_ref[d:d + 1, :], bx_ref[d:d + 1, :], sp[d:d + 1, :])
                a_ref[0:t_len, :] = a
                u_ref[0:t_len, :] = gamma * (i * xc)
            return _scan_pair(af, uf, hf, h0f, ab, ub, hb, h0b, t_len)

        def backward(xa, t_len, h0f, h0b, dhf, dhb, first):
            xc = _conv(xa, cwv, cbv)
            sl = slice(0, t_len)
            uf[sl, :] = ab[sl, :] * dhb
            ub[sl, :] = af[sl, :] * dhf
            rho_b_last, rho_f_first = _scan_pair(ab, uf, rb, z, af, ub, rf, z, t_len)
            dxc = jnp.zeros((t_len, HD), F32)
            dsp = []
            for d in (0, 1):
                r, i, a, gamma = _gates(xc, wmat(wa_ref, d), wmat(wx_ref, d),
                                        ba_ref[d:d + 1, :], bx_ref[d:d + 1, :], sp[d:d + 1, :])
                if d == 0:
                    lam_t = dhf + _shift_up(rf[sl, :], z)
                    h_prev = _shift_down(hf[sl, :], h0f)
                else:
                    lam_t = dhb + _shift_down(rb[sl, :], z)
                    h_prev = _shift_up(hb[sl, :], h0b)
                da = lam_t * h_prev
                lx = lam_t * xc
                d_i = lx * gamma
                d_gamma = lx * i
                dxc = dxc + lam_t * (gamma * i)
                d_log_a = a * (da - d_gamma * (a / gamma))
                dsp.append(jnp.sum(d_log_a * r, axis=0, keepdims=True) * (-LRU_C))
                d_pre_r = d_log_a * ((-LRU_C) * sp[d:d + 1, :]) * (r * (1.0 - r))
                d_pre_i = d_i * (i * (1.0 - i))
                prb, pib, xb = d_pre_r.astype(BF16), d_pre_i.astype(BF16), xc.astype(BF16)
                dxc = dxc + _dot_nt(prb, wmat(wa_ref, d)) + _dot_nt(pib, wmat(wx_ref, d))
                g_wa, g_wx = _dot_tn(xb, prb), _dot_tn(xb, pib)
                g_ba = jnp.sum(d_pre_r, axis=0, keepdims=True)
                g_bx = jnp.sum(d_pre_i, axis=0, keepdims=True)
                if first:
                    dwa_ref[d, 0] = g_wa
                    dwx_ref[d, 0] = g_wx
                    dba_ref[d:d + 1, :] = g_ba
                    dbx_ref[d:d + 1, :] = g_bx
                else:
                    dwa_ref[d, 0] += g_wa
                    dwx_ref[d, 0] += g_wx
                    dba_ref[d:d + 1, :] += g_ba
                    dbx_ref[d:d + 1, :] += g_bx
            g_lam = jnp.concatenate(dsp, axis=0) * (-_sigmoid(-lamv))
            dm1 = _shift_down(dxc, z)
            dp1 = _shift_up(dxc, z)
            dm2 = _shift_down(dm1, z)
            dxa = dp1 * cwv[0:1, :] + dxc * cwv[1:2, :] + dm1 * cwv[2:3, :] + dm2 * cwv[3:4, :]
            xm1 = _shift_down(xa, z)
            xp1 = _shift_up(xa, z)
            xp2 = _shift_up(xp1, z)
            g_cw = jnp.concatenate([jnp.sum(dxc * v, axis=0, keepdims=True) for v in (xm1, xa, xp1, xp2)], axis=0)
            g_cb = jnp.sum(dxc, axis=0, keepdims=True)
            if first:
                dlam_ref[...] = g_lam
                dcw_ref[...] = g_cw
                dcb_ref[...] = g_cb
            else:
                dlam_ref[...] += g_lam
                dcw_ref[...] += g_cw
                dcb_ref[...] += g_cb
            return dxa, rho_f_first, rho_b_last

        h0f, h0b = forward(xac_ref[...], LC, z, z)
        forward(xa_ref[...], L, h0f, h0b)
        dh = dyl_ref[...]
        dxa, dh0f, dh0b = backward(xa_ref[...], L, h0f, h0b, dh, dh, True)
        dxa_ref[...] = dxa.astype(BF16)
        forward(xac_ref[...], LC, z, z)
        rc = _rows((LC, HD))
        dxac, _, _ = backward(xac_ref[...], LC, z, z, jnp.where(rc == LC - 1, dh0f, 0.0),
                              jnp.where(rc == 0, dh0b, 0.0), False)
        dxac_ref[...] = dxac.astype(BF16)

    s = _lru_param_specs()
    col = lambda r: pl.BlockSpec((r, HD), lambda h: (0, h))
    return _call(
        body, name="lru_backward", grid=(HEADS,),
        out_shape=[jax.ShapeDtypeStruct((L, D_IN), BF16), jax.ShapeDtypeStruct((LC, D), BF16),
                   jax.ShapeDtypeStruct((2, HEADS, HD, HD), F32), jax.ShapeDtypeStruct((2, HEADS, HD, HD), F32),
                   jax.ShapeDtypeStruct((2, D), F32), jax.ShapeDtypeStruct((2, D), F32),
                   jax.ShapeDtypeStruct((2, D), F32), jax.ShapeDtypeStruct((CONV_W, D), F32),
                   jax.ShapeDtypeStruct((1, D), F32)],
        in_specs=[s["xa"], s["xac"], col(L), pl.BlockSpec(memory_space=pl.ANY), s["cw"], s["cb"], s["h4"], s["h4"],
                  s["v2"], s["v2"], s["v2"]],
        out_specs=[col(L), col(LC), s["h4"], s["h4"], s["v2"], s["v2"], s["v2"], col(CONV_W), col(1)],
        scratch_shapes=[pltpu.VMEM((L, HD), F32)] * 8,
        input_output_aliases={3: 0},
        compiler_params=_params("arbitrary"),
    )(zx, zc, dyl, dz, cw, cb, wa, wx, ba, bx, lam)


def _mixer_loss(x, tgt, zx, yl, gx, fg, lng, lnb, ws, wst, bst, wout, tm):
    ncht = tm // CHUNK

    def body(x_ref, t_ref, ga_ref, u_ref, v_ref, gb_ref, yl_ref, gx_ref, fg_ref, lng_ref, lnb_ref, ws_ref, wst_ref,
             bst_ref, wout_ref,
             dz_ref, dyl_ref, dxn_ref, y_s, do_ref, dws_ref, dbst_ref, vec_ref, loss_ref,
             vn_s, mix_s, dm_s, dvn_s):
        step = pl.program_id(0)

        @pl.when(step == 0)
        def _():
            dws_ref[...] = jnp.zeros_like(dws_ref)
            dbst_ref[...] = jnp.zeros_like(dbst_ref)
            vec_ref[...] = jnp.zeros_like(vec_ref)
            loss_ref[...] = jnp.zeros_like(loss_ref)

        u, v = u_ref[...], v_ref[...]
        ug, dug_du = _gelu_and_grad(u)
        vg, dvg_dv = _gelu_and_grad(v)
        mu = jnp.mean(vg, axis=-1, keepdims=True)
        vc = vg - mu
        rstd = lax.rsqrt(jnp.mean(vc * vc, axis=-1, keepdims=True) + LN_EPS)
        vhat = vc * rstd
        lngv = lng_ref[...]
        vn_s[...] = (vhat * lngv + lnb_ref[...]).astype(BF16)
        for ch in range(ncht):
            rs = slice(ch * CHUNK, (ch + 1) * CHUNK)
            for g in range(HEADS):
                cs = slice(g * HD, (g + 1) * HD)
                mix_s[rs, cs] = _dot(ws_ref[g], vn_s[rs, cs]) + bst_ref[:, g:g + 1]
        mixed = mix_s[...]
        ga, gb, yl = ga_ref[...], gb_ref[...], yl_ref[...]
        sga, dsga = _silu_and_grad(ga)
        sgb, dsgb = _silu_and_grad(gb)
        ys = ug * mixed
        y_s[:, 0:D] = (yl * sga).astype(BF16)
        y_s[:, D:D_MIX] = (ys * sgb).astype(BF16)
        o = _dot(y_s[...], wout_ref[...])
        gxv, fgv = gx_ref[...], fg_ref[...]
        xn = x_ref[...] + gxv * o
        rs2 = lax.rsqrt(jnp.mean(xn * xn, axis=-1, keepdims=True) + NORM_EPS)
        xh = xn * rs2
        diff = xh * fgv - t_ref[...]
        loss_ref[...] += jnp.full(loss_ref.shape, jnp.sum(diff * diff) * (0.5 / D), F32)
        dout = diff * (1.0 / D)
        w = dout * fgv
        dxn = rs2 * (w - xh * jnp.mean(w * xh, axis=-1, keepdims=True))
        dxn_ref[...] = dxn
        vec_ref[0:1, :] += jnp.sum(dxn * o, axis=0, keepdims=True)
        vec_ref[1:2, :] += jnp.sum(dout * xh, axis=0, keepdims=True)
        dob = (dxn * gxv).astype(BF16)
        do_ref[...] = dob
        dy = _dot_nt(dob, wout_ref[...])
        dya, dyb = dy[:, 0:D], dy[:, D:D_MIX]
        dyl_ref[...] = dya * sga
        dys = dyb * sgb
        dz_ref[:, 0:D] = jnp.zeros((tm, D), BF16)
        dz_ref[:, D:2 * D] = (dya * yl * dsga).astype(BF16)
        dz_ref[:, 2 * D:3 * D] = (dys * mixed * dug_du).astype(BF16)
        dz_ref[:, 4 * D:5 * D] = (dyb * ys * dsgb).astype(BF16)
        dm = dys * ug
        dm_s[...] = dm.astype(BF16)
        for g in range(HEADS):
            cs = slice(g * HD, (g + 1) * HD)
            dbst_ref[:, g:g + 1] += sum(jnp.sum(dm[ch * CHUNK:(ch + 1) * CHUNK, cs], axis=1, keepdims=True)
                                        for ch in range(ncht))
            for ch in range(ncht):
                rs = slice(ch * CHUNK, (ch + 1) * CHUNK)
                dws_ref[g] += _dot_nt(dm_s[rs, cs], vn_s[rs, cs])
                dvn_s[rs, cs] = _dot(wst_ref[g], dm_s[rs, cs])
        dvn = dvn_s[...]
        vec_ref[2:3, :] += jnp.sum(dvn * vhat, axis=0, keepdims=True)
        vec_ref[3:4, :] += jnp.sum(dvn, axis=0, keepdims=True)
        dvh = dvn * lngv
        dvg = rstd * (dvh - jnp.mean(dvh, axis=-1, keepdims=True) - vhat * jnp.mean(dvh * vhat, axis=-1, keepdims=True))
        dz_ref[:, 3 * D:4 * D] = (dvg * dvg_dv).astype(BF16)

    tile = pl.BlockSpec((tm, D), lambda i: (i, 0))
    zcol = lambda n: pl.BlockSpec((tm, D), lambda i: (i, n))
    vec = pl.BlockSpec((1, D), lambda i: (0, 0))
    full = lambda *s: pl.BlockSpec(s, lambda i: (0,) * len(s))
    return _call(
        body, name="mixer_loss", grid=(L // tm,),
        out_shape=[jax.ShapeDtypeStruct((L, D_IN), BF16), jax.ShapeDtypeStruct((L, D), F32),
                   jax.ShapeDtypeStruct((L, D), F32), jax.ShapeDtypeStruct((L, D_MIX), BF16),
                   jax.ShapeDtypeStruct((L, D), BF16),
                   jax.ShapeDtypeStruct((HEADS, CHUNK, CHUNK), F32), jax.ShapeDtypeStruct((CHUNK, HEADS), F32),
                   jax.ShapeDtypeStruct((8, D), F32), jax.ShapeDtypeStruct((8, 128), F32)],
        in_specs=[tile, tile, zcol(1), zcol(2), zcol(3), zcol(4), tile, pl.BlockSpec((1, D), lambda i: (0, 2)),
                  vec, vec, vec,
                  full(HEADS, CHUNK, CHUNK), full(HEADS, CHUNK, CHUNK), full(CHUNK, HEADS),
                  pl.BlockSpec((D_MIX, D), lambda i: (0, 0), pipeline_mode=pl.Buffered(1))],
        out_specs=[pl.BlockSpec((tm, D_IN), lambda i: (i, 0)), tile, tile,
                   pl.BlockSpec((tm, D_MIX), lambda i: (i, 0)), tile,
                   full(HEADS, CHUNK, CHUNK), full(CHUNK, HEADS), full(8, D), full(8, 128)],
        scratch_shapes=[pltpu.VMEM((tm, D), BF16), pltpu.VMEM((tm, D), F32),
                        pltpu.VMEM((tm, D), BF16), pltpu.VMEM((tm, D), F32)],
        compiler_params=_params("arbitrary"),
    )(x, tgt, zx, zx, zx, zx, yl, gx, fg, lng, lnb, ws, wst, bst, wout)


def _grad_w(a, b, a2, b2, tk, name):
    nk = a.shape[0] // tk
    m, ncols = a.shape[1], b.shape[1]
    with_ctx = a2 is not None

    def body(*refs):
        if with_ctx:
            a_ref, b_ref, a2_ref, b2_ref, o_ref, acc = refs
        else:
            a_ref, b_ref, o_ref, acc = refs
        n, k = pl.program_id(0), pl.program_id(1)

        @pl.when(k == 0)
        def _():
            acc[...] = jnp.zeros_like(acc)

        acc[...] += _dot_tn(a_ref[...], b_ref[...])

        if with_ctx:
            @pl.when(jnp.logical_and(k == nk - 1, n == 0))
            def _():
                acc[...] += _dot_tn(a2_ref[...], b2_ref[...])

        @pl.when(k == nk - 1)
        def _():
            o_ref[...] = acc[...].astype(BF16)

    in_specs = [pl.BlockSpec((tk, m), lambda n, k: (k, 0)), pl.BlockSpec((tk, D), lambda n, k: (k, n))]
    args = [a, b]
    if with_ctx:
        in_specs += [pl.BlockSpec(a2.shape, lambda n, k: (0, 0)), pl.BlockSpec(b2.shape, lambda n, k: (0, 0))]
        args += [a2, b2]
    return _call(
        body, name=name, grid=(ncols // D, nk), out_shape=jax.ShapeDtypeStruct((m, ncols), BF16),
        in_specs=in_specs, out_specs=pl.BlockSpec((m, D), lambda n, k: (0, n)),
        scratch_shapes=[pltpu.VMEM((m, D), F32)],
        compiler_params=_params("arbitrary", "arbitrary"),
    )(*args)


def _grad_rows(xr, dz, w, mod, ng, dres, ncols, tm, name):
    rows = xr.shape[0]
    with_dx = dres is not None

    def body(*refs):
        if with_dx:
            x_ref, dz_ref, w_ref, sc_ref, ng_ref, dres_ref, dx_ref, vec_ref = refs
        else:
            x_ref, dz_ref, w_ref, sc_ref, ng_ref, vec_ref = refs

        @pl.when(pl.program_id(0) == 0)
        def _():
            vec_ref[...] = jnp.zeros_like(vec_ref)

        dhn = _dot_nt(dz_ref[...], w_ref[...])
        x = x_ref[...]
        rs = lax.rsqrt(jnp.mean(x * x, axis=-1, keepdims=True) + NORM_EPS)
        xh = x * rs
        ngv = ng_ref[...]
        y = xh * ngv
        vec_ref[0:1, :] += jnp.sum(dhn, axis=0, keepdims=True)
        vec_ref[1:2, :] += jnp.sum(dhn * y, axis=0, keepdims=True)
        dy = dhn * (1.0 + sc_ref[...])
        vec_ref[2:3, :] += jnp.sum(dy * xh, axis=0, keepdims=True)
        if with_dx:
            dxh = dy * ngv
            dx_ref[...] = dres_ref[...] + rs * (dxh - xh * jnp.mean(dxh * xh, axis=-1, keepdims=True))

    tile = pl.BlockSpec((tm, D), lambda i: (i, 0))
    vec = pl.BlockSpec((1, D), lambda i: (0, 0))
    in_specs = [tile, pl.BlockSpec((tm, ncols), lambda i: (i, 0)),
                pl.BlockSpec((D, ncols), lambda i: (0, 0), pipeline_mode=pl.Buffered(1)),
                pl.BlockSpec((1, D), lambda i: (0, 1)), vec]
    out_shape = [jax.ShapeDtypeStruct((8, D), F32)]
    out_specs = [pl.BlockSpec((8, D), lambda i: (0, 0))]
    args = [xr, dz, w, mod, ng]
    if with_dx:
        in_specs.append(tile)
        out_shape.insert(0, jax.ShapeDtypeStruct((rows, D), F32))
        out_specs.insert(0, tile)
        args.append(dres)
    return _call(body, name=name, grid=(rows // tm,), out_shape=out_shape, in_specs=in_specs, out_specs=out_specs,
                 compiler_params=_params("arbitrary"))(*args)


def _adamw(w, g, m, v):
    m = ADAM_B1 * m + (1.0 - ADAM_B1) * g
    v = ADAM_B2 * v + (1.0 - ADAM_B2) * (g * g)
    m_hat = m / (1.0 - ADAM_B1 ** ADAM_STEP)
    v_hat = v / (1.0 - ADAM_B2 ** ADAM_STEP)
    delta = -ADAM_LR * (m_hat / (jnp.sqrt(v_hat) + ADAM_EPS) + ADAM_WD * w)
    return delta, m, v


def _adamw_reduced(parts, w, m, v, tr, name):
    r, n = w.shape

    def body(p_ref, w_ref, m_ref, v_ref, g_ref, d_ref, mo_ref, vo_ref):
        g = p_ref[0].astype(F32)
        for i in range(1, N_DEV):
            g = g + p_ref[i].astype(F32)
        g_ref[...] = g
        d_ref[...], mo_ref[...], vo_ref[...] = _adamw(w_ref[...], g, m_ref[...], v_ref[...])

    tile = pl.BlockSpec((tr, n), lambda i: (i, 0))
    sds = jax.ShapeDtypeStruct((r, n), F32)
    return _call(
        body, name=name, grid=(r // tr,), out_shape=[sds] * 4,
        in_specs=[pl.BlockSpec((N_DEV, tr, n), lambda i: (0, i, 0)), tile, tile, tile], out_specs=[tile] * 4,
        compiler_params=_params("arbitrary"),
    )(parts, w, m, v)


R_GATE, R_FINAL_G, R_LN_G, R_LN_B = 0, 1, 2, 3
R_SH_X, R_SC_X, R_NG_X = 4, 5, 6
R_SH_C, R_SC_C, R_NG_C = 7, 8, 9
R_BA, R_BX, R_LAM, R_CW, R_CB, R_SGU_B = 10, 12, 14, 16, 20, 21
PACK_ROWS = 32
MAT_ROWS = 2 * (2 * HEADS * HD) + HEADS * CHUNK


def _reduce_small(vp_all, mat_parts, ada_w, me):
    nloc = ada_w.shape[1]

    def body(me_ref, vp_ref, mp_ref, w_ref, red_ref, mat_ref, dmod_ref, gab_ref, cpart_ref, dmc_s):
        red = vp_ref[0]
        mat = mp_ref[0]
        for i in range(1, N_DEV):
            red = red + vp_ref[i]
            mat = mat + mp_ref[i]
        red_ref[...] = red
        mat_ref[...] = mat
        for e in range(N_DEV):
            dmod_ref[e:e + 1, 0:D] = vp_ref[e, R_SH_X:R_SH_X + 1, :]
            dmod_ref[e:e + 1, D:2 * D] = vp_ref[e, R_SC_X:R_SC_X + 1, :]
            dmod_ref[e:e + 1, 2 * D:3 * D] = vp_ref[e, R_GATE:R_GATE + 1, :]
        dmod_ref[8:9, 0:D] = red[R_SH_C:R_SH_C + 1, :]
        dmod_ref[8:9, D:2 * D] = red[R_SC_C:R_SC_C + 1, :]
        dmod_ref[8:9, 2 * D:3 * D] = jnp.zeros((1, D), F32)
        dmod_ref[9:16, :] = jnp.zeros((7, 3 * D), F32)
        gab_ref[:, 0:D] = red[R_SH_X:R_SH_X + 1, :] + red[R_SH_C:R_SH_C + 1, :]
        gab_ref[:, D:2 * D] = red[R_SC_X:R_SC_X + 1, :] + red[R_SC_C:R_SC_C + 1, :]
        gab_ref[:, 2 * D:3 * D] = red[R_GATE:R_GATE + 1, :]
        dmc_s[...] = jnp.broadcast_to(dmod_ref[8:9, :], (8, 3 * D))
        off = pl.multiple_of(me_ref[0] * nloc, 128)
        cpart_ref[...] = _dot_nt(dmc_s[:, pl.ds(off, nloc)], w_ref[...])

    return _call(
        body, name="reduce_small",
        out_shape=[jax.ShapeDtypeStruct((PACK_ROWS, D), F32), jax.ShapeDtypeStruct(mat_parts.shape[1:], F32),
                   jax.ShapeDtypeStruct((16, 3 * D), F32), jax.ShapeDtypeStruct((1, 3 * D), F32),
                   jax.ShapeDtypeStruct((8, D), F32)],
        in_specs=[pl.BlockSpec(memory_space=pltpu.SMEM), VMEM, VMEM, VMEM], out_specs=[VMEM] * 5,
        scratch_shapes=[pltpu.VMEM((8, 3 * D), F32)], compiler_params=_params(),
    )(me, vp_all, mat_parts, ada_w)


def _adamw_ada(c_all, c_ctx, dmod, w, m, v, me):
    nloc = w.shape[1]

    def body(me_ref, c_ref, cc_ref, dm_ref, w_ref, m_ref, v_ref, g_ref, d_ref, mo_ref, vo_ref):
        off = pl.multiple_of(me_ref[0] * nloc, 128)
        dm = dm_ref[:, pl.ds(off, nloc)]
        sx, _ = _silu_and_grad(c_ref[...])
        sc, _ = _silu_and_grad(cc_ref[...])
        g = _dot_tn(sx, dm[0:8, :]) + _dot_tn(jnp.broadcast_to(sc, (8, D)), dm[8:16, :])
        g_ref[...] = g
        d_ref[...], mo_ref[...], vo_ref[...] = _adamw(w_ref[...], g, m_ref[...], v_ref[...])

    sds = jax.ShapeDtypeStruct(w.shape, F32)
    return _call(
        body, name="adamw_ada_w", out_shape=[sds] * 4,
        in_specs=[pl.BlockSpec(memory_space=pltpu.SMEM)] + [VMEM] * 6, out_specs=[VMEM] * 4,
        compiler_params=_params(),
    )(me, c_all, c_ctx, dmod, w, m, v)


_SMALL = ("c_ctx", "ada_b", "norm_g", "conv_w", "conv_b", "lru_wa", "lru_ba", "lru_wx", "lru_bx", "lru_lambda",
          "sgu_ln_g", "sgu_ln_b", "sgu_w", "sgu_b", "final_g")


def _adamw_small(red, mat, cparts, gab, ws, ms, vs, me):
    n = len(_SMALL)
    nw = 2 * HEADS * HD

    def body(me_ref, red_ref, mat_ref, cp_ref, gab_ref, *refs):
        w_refs, m_refs, v_refs = refs[:n], refs[n:2 * n], refs[2 * n:3 * n]
        outs = refs[3 * n:]
        off = pl.multiple_of(me_ref[0] * HD, 128)

        def row(r, k=1):
            return red_ref[r:r + k, :]

        cc = w_refs[0][...]
        dcc = cp_ref[0, 0:1, :]
        for i in range(1, N_DEV):
            dcc = dcc + cp_ref[i, 0:1, :]
        grads = dict(
            c_ctx=dcc * _silu_and_grad(cc)[1], ada_b=gab_ref[...], norm_g=row(R_NG_X) + row(R_NG_C),
            conv_w=red_ref[R_CW:R_CW + CONV_W, pl.ds(off, HD)], conv_b=row(R_CB),
            lru_wa=mat_ref[0:nw, :], lru_ba=row(R_BA, 2), lru_wx=mat_ref[nw:2 * nw, :], lru_bx=row(R_BX, 2),
            lru_lambda=red_ref[R_LAM:R_LAM + 2, pl.ds(off, HD)], sgu_ln_g=row(R_LN_G), sgu_ln_b=row(R_LN_B),
            sgu_w=mat_ref[2 * nw:MAT_ROWS, :], sgu_b=row(R_SGU_B), final_g=row(R_FINAL_G))
        for j, name in enumerate(_SMALL):
            g = grads[name]
            outs[j][...] = g
            outs[n + j][...], outs[2 * n + j][...], outs[3 * n + j][...] = _adamw(w_refs[j][...], g, m_refs[j][...],
                                                                                 v_refs[j][...])

    sds = [jax.ShapeDtypeStruct(ws[k].shape, F32) for k in _SMALL]
    outs = _call(
        body, name="adamw_small", out_shape=sds * 4,
        in_specs=[pl.BlockSpec(memory_space=pltpu.SMEM)] + [VMEM] * (4 + 3 * n), out_specs=[VMEM] * (4 * n),
        compiler_params=_params(),
    )(me, red, mat, cparts, gab, *[ws[k] for k in _SMALL], *[ms[k] for k in _SMALL], *[vs[k] for k in _SMALL])
    return [dict(zip(_SMALL, outs[i * n:(i + 1) * n])) for i in range(4)]


def kernel(x, c, ctx, c_ctx, ada_w, ada_b, norm_g, w_in, conv_w, conv_b, lru_wa, lru_ba, lru_wx, lru_bx, lru_lambda, sgu_ln_g, sgu_ln_b, sgu_w, sgu_b, w_out, final_g, loss_target, m_c_ctx, m_ada_w, m_ada_b, m_norm_g, m_w_in, m_conv_w, m_conv_b, m_lru_wa, m_lru_ba, m_lru_wx, m_lru_bx, m_lru_lambda, m_sgu_ln_g, m_sgu_ln_b, m_sgu_w, m_sgu_b, m_w_out, m_final_g, v_c_ctx, v_ada_w, v_ada_b, v_norm_g, v_w_in, v_conv_w, v_conv_b, v_lru_wa, v_lru_ba, v_lru_wx, v_lru_bx, v_lru_lambda, v_sgu_ln_g, v_sgu_ln_b, v_sgu_w, v_sgu_b, v_w_out, v_final_g):
    args = dict(locals())
    me_s = 4 * lax.axis_index("x") + 2 * lax.axis_index("y") + lax.axis_index("c")
    me = me_s.astype(jnp.int32).reshape(1)
    xr, ctxr, tgt = x[0], ctx[0], loss_target[0]
    cc = c_ctx.reshape(1, D)
    nw = 2 * HEADS * HD
    view = dict(c_ctx=(1, D), ada_b=(1, 3 * D), norm_g=(1, D), conv_w=(CONV_W, HD), conv_b=(1, D), lru_wa=(nw, HD),
                lru_ba=(2, D), lru_wx=(nw, HD), lru_bx=(2, D), lru_lambda=(2, HD), sgu_ln_g=(1, D), sgu_ln_b=(1, D),
                sgu_w=(HEADS * CHUNK, CHUNK), sgu_b=(1, D), final_g=(1, D))

    (c_all,) = _exchange([c], ["ag"], "gather_c")
    c_all = c_all.reshape(N_DEV, D)
    part = _ada_forward(c_all, cc, ada_w[0], ada_b, me)
    (parts,) = _exchange([part], ["ag"], "gather_mod")
    modx = lax.dynamic_index_in_dim(parts, me_s, axis=1, keepdims=False).reshape(1, 3 * D)
    modc = parts[:, 8, :].reshape(1, 3 * D)

    w_in_b, w_out_b = _cast_weights(w_in[0], w_out[0])
    w_full, wout_all, cw_full, lam_full = _exchange(
        [w_in_b, w_out_b, conv_w[0], lru_lambda[0]], ["agc", "ag", "agc", "agc"], "gather_weights")
    wout_full = wout_all.reshape(D_MIX, D)

    zx, hn = _project(xr, modx, norm_g, w_full, D_IN, 256, "project_x")
    zc, hnc = _project(ctxr, modc, norm_g, w_full, D, LC, "project_ctx")
    ba, bx = lru_ba.reshape(2, D), lru_bx.reshape(2, D)
    yl = _lru_forward(zx, zc, cw_full, conv_b, lru_wa[0], lru_wx[0], ba, bx, lam_full)
    ws_b = sgu_w[0].astype(BF16)
    dz, dyl, dxn, ycat, dob, dws, dbst, mvec, lossv = _mixer_loss(
        xr, tgt, zx, yl, modx, final_g.reshape(1, D), sgu_ln_g, sgu_ln_b, ws_b, jnp.swapaxes(ws_b, 1, 2),
        sgu_b[0].T, wout_full, 128)

    dz, dxac, dwa, dwx, dba, dbx, dlam, dcw, dcb = _lru_backward(
        zx, zc, dyl, dz, cw_full, conv_b, lru_wa[0], lru_wx[0], ba, bx, lam_full)
    gw_in = _grad_w(hn, dz, hnc, dxac, 512, "grad_w_in")
    gw_out = _grad_w(ycat, dob, None, None, 512, "grad_w_out")
    gx, xvec = _grad_rows(xr, dz, w_full, modx, norm_g, dxn, D_IN, 256, "grad_rows_x")
    (cvec,) = _grad_rows(ctxr, dxac, w_full, modc, norm_g, None, D, LC, "grad_rows_ctx")

    pack = jnp.concatenate([mvec[0:4], xvec[0:3], cvec[0:3], dba, dbx, dlam, dcw, dcb, dbst.T.reshape(1, D),
                            jnp.zeros((PACK_ROWS - R_SGU_B - 1, D), F32)], axis=0)
    matpack = jnp.concatenate([dwa.reshape(nw, HD), dwx.reshape(nw, HD), dws.reshape(HEADS * CHUNK, CHUNK)], axis=0)
    win_parts, wout_parts, mat_parts, vp_all = _exchange(
        [gw_in, gw_out.reshape(N_DEV, D_MIX // N_DEV, D), matpack.reshape(N_DEV, MAT_ROWS // N_DEV, HD), pack],
        ["a2ac", "a2a", "a2a", "ag"], "reduce_grads")
    red, matpiece, dmod, gab, cpart = _reduce_small(vp_all, mat_parts, ada_w[0], me)
    mat_all, cparts = _exchange([matpiece, cpart], ["ag", "ag"], "gather_small")

    g_w_in, d_w_in, nm_w_in, nv_w_in = _adamw_reduced(win_parts, w_in[0], m_w_in[0], v_w_in[0], 256, "adamw_w_in")
    g_w_out, d_w_out, nm_w_out, nv_w_out = _adamw_reduced(wout_parts, w_out[0], m_w_out[0], v_w_out[0], 128,
                                                          "adamw_w_out")
    g_ada, d_ada, nm_ada, nv_ada = _adamw_ada(c_all, cc, dmod, ada_w[0], m_ada_w[0], v_ada_w[0], me)
    ws = {k: args[k].reshape(view[k]) for k in _SMALL}
    ms = {k: args["m_" + k].reshape(view[k]) for k in _SMALL}
    vs = {k: args["v_" + k].reshape(view[k]) for k in _SMALL}
    small = _adamw_small(red, mat_all.reshape(MAT_ROWS, HD), cparts, gab, ws, ms, vs, me)
    big = dict(w_in=(g_w_in, d_w_in, nm_w_in, nv_w_in), w_out=(g_w_out, d_w_out, nm_w_out, nv_w_out),
               ada_w=(g_ada, d_ada, nm_ada, nv_ada))

    loss = lax.psum(lossv[0, 0], ("x", "y", "c"))
    names = ("c_ctx", "ada_w", "ada_b", "norm_g", "w_in", "conv_w", "conv_b", "lru_wa", "lru_ba", "lru_wx", "lru_bx",
             "lru_lambda", "sgu_ln_g", "sgu_ln_b", "sgu_w", "sgu_b", "w_out", "final_g")
    outs = [loss, gx.reshape(x.shape)]
    for kind in range(4):
        for k in names:
            val = big[k][kind] if k in big else small[kind][k]
            outs.append(val.reshape(args[k].shape))
    return tuple(outs)
```

```python
import functools

import jax
import jax.numpy as jnp
from jax import lax
from jax.experimental import pallas as pl
from jax.experimental.pallas import tpu as pltpu

F32 = jnp.float32
BF16 = jnp.bfloat16

N_DEV = 8
D = 1024
L = 2048
LC = 256
HEADS = 8
HD = 128
CHUNK = 128
D_IN = 5 * D
D_MIX = 2 * D
CONV_W = 4
LRU_C = 8.0
NORM_EPS = 1e-6
LN_EPS = 1e-5
ADAM_LR, ADAM_B1, ADAM_B2, ADAM_EPS, ADAM_WD, ADAM_STEP = 0.001, 0.9, 0.999, 1e-08, 0.01, 10

VMEM_LIMIT = 56 * 1024 * 1024

HBM = pl.BlockSpec(memory_space=pltpu.HBM)
VMEM = pl.BlockSpec(memory_space=pltpu.VMEM)
MESH = pl.DeviceIdType.MESH


def _call(body, **kw):
    return pl.pallas_call(body, **kw)


def _params(*sem):
    return pltpu.CompilerParams(dimension_semantics=sem, vmem_limit_bytes=VMEM_LIMIT)


def _sigmoid(x):
    return 1.0 / (1.0 + jnp.exp(-x))


def _silu_and_grad(x):
    s = _sigmoid(x)
    return x * s, s * (1.0 + x * (1.0 - s))


_G0 = 0.7978845608028654
_G1 = 0.044715


def _gelu_and_grad(x):
    x2 = x * x
    t = jnp.tanh(_G0 * (x + _G1 * x * x2))
    cdf = 0.5 * (1.0 + t)
    return x * cdf, cdf + 0.5 * x * (1.0 - t * t) * (_G0 * (1.0 + 3.0 * _G1 * x2))


def _gelu(x):
    return 0.5 * x * (1.0 + jnp.tanh(_G0 * (x + _G1 * x * x * x)))


def _expm1(x):
    series = x * (1.0 + x * (0.5 + x * (1.0 / 6.0 + x * (1.0 / 24.0 + x * (1.0 / 120.0)))))
    return jnp.where(jnp.abs(x) < 0.05, series, jnp.exp(x) - 1.0)


def _softplus(z):
    t = jnp.exp(-jnp.abs(z))
    u = 1.0 + t
    log1p = jnp.where(u == 1.0, t, jnp.log(u) * t / jnp.where(u == 1.0, 1.0, u - 1.0))
    return jnp.maximum(z, 0.0) + log1p


def _dot(a, b):
    return jnp.dot(a, b, preferred_element_type=F32)


def _dot_nt(a, b):
    return lax.dot_general(a, b, (((1,), (1,)), ((), ())), preferred_element_type=F32)


def _dot_tn(a, b):
    return lax.dot_general(a, b, (((0,), (0,)), ((), ())), preferred_element_type=F32)


def _rows(shape):
    return lax.broadcasted_iota(jnp.int32, shape, 0)


def _shift_down(x, first):
    return jnp.where(_rows(x.shape) == 0, first, pltpu.roll(x, 1, 0))


def _shift_up(x, last):
    n = x.shape[0]
    return jnp.where(_rows(x.shape) == n - 1, last, pltpu.roll(x, n - 1, 0))


def _gather2(arrays, modes, name):
    n = len(arrays)
    out_shape = [jax.ShapeDtypeStruct((N_DEV,) + a.shape if m == "ag" else (a.shape[0], N_DEV * a.shape[1]), a.dtype)
                 for a, m in zip(arrays, modes)]

    def body(*refs):
        ins, outs = refs[:n], refs[n:2 * n]
        send_sems, recv_sems, local_sems = refs[2 * n:]
        x, y, c = lax.axis_index("x"), lax.axis_index("y"), lax.axis_index("c")
        sibling = (x, y, 1 - c)
        chips = [(x ^ (k >> 1), y ^ (k & 1)) for k in (1, 2, 3)]

        def slot(j, px, py, pc):
            dev = 4 * px + 2 * py + pc
            if modes[j] == "agc":
                w = ins[j].shape[1]
                return outs[j].at[:, pl.ds(pl.multiple_of(dev * w, 128), w)]
            return outs[j].at[dev]

        def copy(j, k, block, to, src=None):
            return pltpu.make_async_remote_copy(
                src_ref=slot(j, *block) if src is None else src, dst_ref=slot(j, *block),
                send_sem=send_sems.at[j, k], recv_sem=recv_sems.at[j, k], device_id=to, device_id_type=MESH)

        me = (x, y, c)
        sends, locals_ = [], []
        for j in range(n):
            own = pltpu.make_async_copy(ins[j], slot(j, *me), local_sems.at[j])
            own.start()
            locals_.append(own)
            first = [copy(j, 0, me, sibling, src=ins[j])]
            first += [copy(j, 1 + i, me, (*chip, c), src=ins[j]) for i, chip in enumerate(chips)]
            for cp in first:
                cp.start()
            sends += first
        for i, chip in enumerate(chips):
            for j in range(n):
                copy(j, 1 + i, (*chip, c), me).wait_recv()
                fwd = copy(j, 4 + i, (*chip, c), sibling)
                fwd.start()
                sends.append(fwd)
        for j in range(n):
            copy(j, 0, sibling, me).wait_recv()
            for i, chip in enumerate(chips):
                copy(j, 4 + i, (*chip, 1 - c), me).wait_recv()
        for cp in sends:
            cp.wait_send()
        for cp in locals_:
            cp.wait()

    return _call(
        body, name=name, out_shape=out_shape, in_specs=[HBM] * n, out_specs=[HBM] * n,
        scratch_shapes=[pltpu.SemaphoreType.DMA((n, N_DEV - 1)), pltpu.SemaphoreType.DMA((n, N_DEV - 1)),
                        pltpu.SemaphoreType.DMA((n,))],
        compiler_params=pltpu.CompilerParams(has_side_effects=True),
    )(*[pltpu.with_memory_space_constraint(a, pltpu.HBM) for a in arrays])


def _reduce2(arrays, modes, me, name):
    n = len(arrays)
    shapes = [(a.shape[1], a.shape[2]) if m == "a2a" else (a.shape[0], a.shape[1] // N_DEV)
              for a, m in zip(arrays, modes)]
    staged = [jax.ShapeDtypeStruct((4,) + s, a.dtype) for s, a in zip(shapes, arrays)]

    def piece(ref, mode, dev, w):
        return ref.at[dev] if mode == "a2a" else ref.at[:, pl.ds(pl.multiple_of(dev * w, 128), w)]

    def to_sibling(*refs):
        ins, outs = refs[:n], refs[n:2 * n]
        send_sems, recv_sems = refs[2 * n:]
        x, y, c = lax.axis_index("x"), lax.axis_index("y"), lax.axis_index("c")
        copies = []
        for j in range(n):
            for q in range(4):
                cp = pltpu.make_async_remote_copy(
                    src_ref=piece(ins[j], modes[j], 2 * q + (1 - c), shapes[j][1]), dst_ref=outs[j].at[q],
                    send_sem=send_sems.at[j, q], recv_sem=recv_sems.at[j, q], device_id=(x, y, 1 - c),
                    device_id_type=MESH)
                cp.start()
                copies.append(cp)
        for cp in copies:
            cp.wait()

    stage = _call(
        to_sibling, name=name + "_d2d", out_shape=staged, in_specs=[HBM] * n, out_specs=[HBM] * n,
        scratch_shapes=[pltpu.SemaphoreType.DMA((n, 4)), pltpu.SemaphoreType.DMA((n, 4))],
        compiler_params=pltpu.CompilerParams(has_side_effects=True),
    )(*[pltpu.with_memory_space_constraint(a, pltpu.HBM) for a in arrays])

    def add(me_ref, *refs):
        del me_ref
        own, got, outs = refs[:n], refs[n:2 * n], refs[2 * n:]
        for j in range(n):
            mine = own[j][0] if modes[j] == "a2a" else own[j][...]
            outs[j][0] = (mine.astype(F32) + got[j][0].astype(F32)).astype(outs[j].dtype)

    in_specs = []
    for (r, w), m in zip(shapes, modes):
        if m == "a2a":
            in_specs.append(pl.BlockSpec((1, r, w), lambda q, me_ref: (2 * q + me_ref[0] % 2, 0, 0)))
        else:
            in_specs.append(pl.BlockSpec((r, w), lambda q, me_ref: (0, 2 * q + me_ref[0] % 2)))
    slot_specs = [pl.BlockSpec((1, r, w), lambda q, me_ref: (q, 0, 0)) for r, w in shapes]
    sums = _call(
        add, name=name + "_add", out_shape=staged,
        grid_spec=pltpu.PrefetchScalarGridSpec(num_scalar_prefetch=1, grid=(4,), in_specs=in_specs + slot_specs,
                                               out_specs=slot_specs),
        compiler_params=_params("arbitrary"),
    )(me, *arrays, *stage)

    def to_chips(*refs):
        ins, outs = refs[:n], refs[n:2 * n]
        send_sems, recv_sems, local_sems = refs[2 * n:]
        x, y, c = lax.axis_index("x"), lax.axis_index("y"), lax.axis_index("c")
        qm = 2 * x + y
        copies = []
        for j in range(n):
            for k in (1, 2, 3):
                px, py = x ^ (k >> 1), y ^ (k & 1)
                cp = pltpu.make_async_remote_copy(
                    src_ref=ins[j].at[2 * px + py], dst_ref=outs[j].at[qm], send_sem=send_sems.at[j, k - 1],
                    recv_sem=recv_sems.at[j, k - 1], device_id=(px, py, c), device_id_type=MESH)
                cp.start()
                copies.append(cp)
            own = pltpu.make_async_copy(ins[j].at[qm], outs[j].at[qm], local_sems.at[j])
            own.start()
            copies.append(own)
        for cp in copies:
            cp.wait()

    return _call(
        to_chips, name=name + "_ici", out_shape=staged, in_specs=[HBM] * n, out_specs=[HBM] * n,
        scratch_shapes=[pltpu.SemaphoreType.DMA((n, 3)), pltpu.SemaphoreType.DMA((n, 3)),
                        pltpu.SemaphoreType.DMA((n,))],
        compiler_params=pltpu.CompilerParams(has_side_effects=True),
    )(*[pltpu.with_memory_space_constraint(a, pltpu.HBM) for a in sums])


def _ada_forward(c_all, c_ctx, ada_w, ada_b, me):
    nloc = ada_w.shape[1]

    def body(me_ref, c_ref, cc_ref, w_ref, b_ref, o_ref):
        off = pl.multiple_of(me_ref[0] * nloc, 128)
        b = b_ref[:, pl.ds(off, nloc)]
        w = w_ref[...]
        sx, _ = _silu_and_grad(c_ref[...])
        sc, _ = _silu_and_grad(jnp.broadcast_to(cc_ref[...], (8, D)))
        o_ref[0:8, :] = _dot(sx, w) + b
        o_ref[8:16, :] = _dot(sc, w) + b

    return _call(
        body, name="ada_forward", out_shape=jax.ShapeDtypeStruct((16, nloc), F32),
        in_specs=[pl.BlockSpec(memory_space=pltpu.SMEM), VMEM, VMEM, VMEM, VMEM], out_specs=VMEM,
    )(me, c_all, c_ctx, ada_w, ada_b)


def _cast_weights(w_in, w_out):
    def body(a_ref, b_ref, ao_ref, bo_ref):
        ao_ref[...] = a_ref[...].astype(BF16)
        bo_ref[...] = b_ref[...].astype(BF16)

    return _call(
        body, name="cast_weights",
        out_shape=[jax.ShapeDtypeStruct(w_in.shape, BF16), jax.ShapeDtypeStruct(w_out.shape, BF16)],
        in_specs=[VMEM, VMEM], out_specs=[VMEM, VMEM], compiler_params=_params(),
    )(w_in, w_out)


def _project(xr, mod, ng, w, ncols, tm, name):
    rows = xr.shape[0]

    def body(x_ref, sh_ref, sc_ref, ng_ref, w_ref, z_ref, hn_ref):
        x = x_ref[...]
        rs = lax.rsqrt(jnp.mean(x * x, axis=-1, keepdims=True) + NORM_EPS)
        hn = (x * rs * ng_ref[...]) * (1.0 + sc_ref[...]) + sh_ref[...]
        hb = hn.astype(BF16)
        hn_ref[...] = hb
        for n in range(ncols // D):
            z_ref[:, n * D:(n + 1) * D] = _dot(hb, w_ref[:, n * D:(n + 1) * D])

    vec = pl.BlockSpec((1, D), lambda i: (0, 0))
    return _call(
        body, name=name, grid=(rows // tm,),
        out_shape=[jax.ShapeDtypeStruct((rows, ncols), F32), jax.ShapeDtypeStruct((rows, D), BF16)],
        in_specs=[pl.BlockSpec((tm, D), lambda i: (i, 0)), vec, pl.BlockSpec((1, D), lambda i: (0, 1)), vec,
                  pl.BlockSpec((D, ncols), lambda i: (0, 0), pipeline_mode=pl.Buffered(1))],
        out_specs=[pl.BlockSpec((tm, ncols), lambda i: (i, 0)), pl.BlockSpec((tm, D), lambda i: (i, 0))],
        compiler_params=_params("arbitrary"),
    )(xr, mod, mod, ng, w)


def _scan_pair(af_ref, uf_ref, hf_ref, h0f, ab_ref, ub_ref, hb_ref, h0b, t_len):
    nb = t_len // 8
    rows = _rows((8, HD))

    def block_f(a, b, carry):
        for s in (1, 2, 4):
            m = rows >= s
            b = a * jnp.where(m, pltpu.roll(b, s, 0), 0.0) + b
            a = a * jnp.where(m, pltpu.roll(a, s, 0), 1.0)
        h = b + a * carry
        return h, h[7:8, :]

    def block_b(a, b, carry):
        for s in (1, 2, 4):
            m = rows < 8 - s
            b = a * jnp.where(m, pltpu.roll(b, 8 - s, 0), 0.0) + b
            a = a * jnp.where(m, pltpu.roll(a, 8 - s, 0), 1.0)
        h = b + a * carry
        return h, h[0:1, :]

    def body(k, carry):
        cf, cb = carry
        of = pl.multiple_of(k * 8, 8)
        ob = pl.multiple_of((nb - 1 - k) * 8, 8)
        hf, cf = block_f(af_ref[pl.ds(of, 8), :], uf_ref[pl.ds(of, 8), :], cf)
        hf_ref[pl.ds(of, 8), :] = hf
        hb, cb = block_b(ab_ref[pl.ds(ob, 8), :], ub_ref[pl.ds(ob, 8), :], cb)
        hb_ref[pl.ds(ob, 8), :] = hb
        return cf, cb

    return lax.fori_loop(0, nb, body, (h0f, h0b))


def _conv(xa, cw, cb):
    z = jnp.zeros((1, HD), F32)
    xm1 = _shift_down(xa, z)
    xp1 = _shift_up(xa, z)
    xp2 = _shift_up(xp1, z)
    return xm1 * cw[0:1, :] + xa * cw[1:2, :] + xp1 * cw[2:3, :] + xp2 * cw[3:4, :] + cb


def _gates(xc, wa, wx, ba, bx, sp):
    xb = xc.astype(BF16)
    r = _sigmoid(_dot(xb, wa) + ba)
    i = _sigmoid(_dot(xb, wx) + bx)
    log_a = (-LRU_C) * r * sp
    a = jnp.exp(log_a)
    gamma = jnp.sqrt(-_expm1(2.0 * log_a))
    return r, i, a, gamma


def _lru_param_specs():
    h4 = pl.BlockSpec((2, 1, HD, HD), lambda h: (0, h, 0, 0))
    v2 = pl.BlockSpec((2, HD), lambda h: (0, h))
    return dict(
        xa=pl.BlockSpec((L, HD), lambda h: (0, h)), xac=pl.BlockSpec((LC, HD), lambda h: (0, h)),
        cw=pl.BlockSpec((CONV_W, HD), lambda h: (0, h)), cb=pl.BlockSpec((1, HD), lambda h: (0, h)), h4=h4, v2=v2)


def _lru_forward(zx, zc, cw, cb, wa, wx, ba, bx, lam):
    def body(xa_ref, xac_ref, cw_ref, cb_ref, wa_ref, wx_ref, ba_ref, bx_ref, lam_ref, yl_ref,
             af, uf, hf, ab, ub, hb):
        cwv, cbv = cw_ref[...], cb_ref[...]
        sp = _softplus(-lam_ref[...])

        def forward(xa, t_len, h0f, h0b):
            xc = _conv(xa, cwv, cbv)
            for d, (a_ref, u_ref) in enumerate(((af, uf), (ab, ub))):
                _, i, a, gamma = _gates(xc, wa_ref[d, 0].astype(BF16), wx_ref[d, 0].astype(BF16),
                                        ba_ref[d:d + 1, :], bx_ref[d:d + 1, :], sp[d:d + 1, :])
                a_ref[0:t_len, :] = a
                u_ref[0:t_len, :] = gamma * (i * xc)
            return _scan_pair(af, uf, hf, h0f, ab, ub, hb, h0b, t_len)

        z = jnp.zeros((1, HD), F32)
        h0f, h0b = forward(xac_ref[...], LC, z, z)
        forward(xa_ref[...], L, h0f, h0b)
        yl_ref[...] = hf[...] + hb[...]

    s = _lru_param_specs()
    return _call(
        body, name="lru_forward", grid=(HEADS,), out_shape=jax.ShapeDtypeStruct((L, D), F32),
        in_specs=[s["xa"], s["xac"], s["cw"], s["cb"], s["h4"], s["h4"], s["v2"], s["v2"], s["v2"]],
        out_specs=pl.BlockSpec((L, HD), lambda h: (0, h)),
        scratch_shapes=[pltpu.VMEM((L, HD), F32)] * 6,
        compiler_params=_params("arbitrary"),
    )(zx, zc, cw, cb, wa, wx, ba, bx, lam)


def _lru_backward(zx, zc, dyl, dz, cw, cb, wa, wx, ba, bx, lam):
    def body(xa_ref, xac_ref, dyl_ref, dz_in, cw_ref, cb_ref, wa_ref, wx_ref, ba_ref, bx_ref, lam_ref,
             dxa_ref, dxac_ref, dwa_ref, dwx_ref, dba_ref, dbx_ref, dlam_ref, dcw_ref, dcb_ref,
             af, uf, hf, ab, ub, hb, rf, rb):
        del dz_in
        cwv, cbv = cw_ref[...], cb_ref[...]
        lamv = lam_ref[...]
        sp = _softplus(-lamv)
        z = jnp.zeros((1, HD), F32)

        def wmat(ref, d):
            return ref[d, 0].astype(BF16)

        def forward(xa, t_len, h0f, h0b):
            xc = _conv(xa, cwv, cbv)
            for d, (a_ref, u_ref) in enumerate(((af, uf), (ab, ub))):
                _, i, a, gamma = _gates(xc, wmat(wa_ref, d), wmat(wx_ref, d),
                                        ba_ref[d:d + 1, :], bx_ref[d:d + 1, :], sp[d:d + 1, :])
                a_ref[0:t_len, :] = a
                u_ref[0:t_len, :] = gamma * (i * xc)
            return _scan_pair(af, uf, hf, h0f, ab, ub, hb, h0b, t_len)

        def backward(xa, t_len, h0f, h0b, dhf, dhb, first):
            xc = _conv(xa, cwv, cbv)
            sl = slice(0, t_len)
            uf[sl, :] = ab[sl, :] * dhb
            ub[sl, :] = af[sl, :] * dhf
            rho_b_last, rho_f_first = _scan_pair(ab, uf, rb, z, af, ub, rf, z, t_len)
            dxc = jnp.zeros((t_len, HD), F32)
            dsp = []
            for d in (0, 1):
                r, i, a, gamma = _gates(xc, wmat(wa_ref, d), wmat(wx_ref, d),
                                        ba_ref[d:d + 1, :], bx_ref[d:d + 1, :], sp[d:d + 1, :])
                if d == 0:
                    lam_t = dhf + _shift_up(rf[sl, :], z)
                    h_prev = _shift_down(hf[sl, :], h0f)
                else:
                    lam_t = dhb + _shift_down(rb[sl, :], z)
                    h_prev = _shift_up(hb[sl, :], h0b)
                da = lam_t * h_prev
                lx = lam_t * xc
                d_i = lx * gamma
                d_gamma = lx * i
                dxc = dxc + lam_t * (gamma * i)
                d_log_a = a * (da - d_gamma * (a / gamma))
                dsp.append(jnp.sum(d_log_a * r, axis=0, keepdims=True) * (-LRU_C))
                d_pre_r = d_log_a * ((-LRU_C) * sp[d:d + 1, :]) * (r * (1.0 - r))
                d_pre_i = d_i * (i * (1.0 - i))
                prb, pib, xb = d_pre_r.astype(BF16), d_pre_i.astype(BF16), xc.astype(BF16)
                dxc = dxc + _dot_nt(prb, wmat(wa_ref, d)) + _dot_nt(pib, wmat(wx_ref, d))
                g_wa, g_wx = _dot_tn(xb, prb), _dot_tn(xb, pib)
                g_ba = jnp.sum(d_pre_r, axis=0, keepdims=True)
                g_bx = jnp.sum(d_pre_i, axis=0, keepdims=True)
                if first:
                    dwa_ref[d, 0] = g_wa
                    dwx_ref[d, 0] = g_wx
                    dba_ref[d:d + 1, :] = g_ba
                    dbx_ref[d:d + 1, :] = g_bx
                else:
                    dwa_ref[d, 0] += g_wa
                    dwx_ref[d, 0] += g_wx
                    dba_ref[d:d + 1, :] += g_ba
                    dbx_ref[d:d + 1, :] += g_bx
            g_lam = jnp.concatenate(dsp, axis=0) * (-_sigmoid(-lamv))
            dm1 = _shift_down(dxc, z)
            dp1 = _shift_up(dxc, z)
            dm2 = _shift_down(dm1, z)
            dxa = dp1 * cwv[0:1, :] + dxc * cwv[1:2, :] + dm1 * cwv[2:3, :] + dm2 * cwv[3:4, :]
            xm1 = _shift_down(xa, z)
            xp1 = _shift_up(xa, z)
            xp2 = _shift_up(xp1, z)
            g_cw = jnp.concatenate([jnp.sum(dxc * v, axis=0, keepdims=True) for v in (xm1, xa, xp1, xp2)], axis=0)
            g_cb = jnp.sum(dxc, axis=0, keepdims=True)
            if first:
                dlam_ref[...] = g_lam
                dcw_ref[...] = g_cw
                dcb_ref[...] = g_cb
            else:
                dlam_ref[...] += g_lam
                dcw_ref[...] += g_cw
                dcb_ref[...] += g_cb
            return dxa, rho_f_first, rho_b_last

        h0f, h0b = forward(xac_ref[...], LC, z, z)
        forward(xa_ref[...], L, h0f, h0b)
        dh = dyl_ref[...]
        dxa, dh0f, dh0b = backward(xa_ref[...], L, h0f, h0b, dh, dh, True)
        dxa_ref[...] = dxa.astype(BF16)
        forward(xac_ref[...], LC, z, z)
        rc = _rows((LC, HD))
        dxac, _, _ = backward(xac_ref[...], LC, z, z, jnp.where(rc == LC - 1, dh0f, 0.0),
                              jnp.where(rc == 0, dh0b, 0.0), False)
        dxac_ref[...] = dxac.astype(BF16)

    s = _lru_param_specs()
    col = lambda r: pl.BlockSpec((r, HD), lambda h: (0, h))
    return _call(
        body, name="lru_backward", grid=(HEADS,),
        out_shape=[jax.ShapeDtypeStruct((L, D_IN), BF16), jax.ShapeDtypeStruct((LC, D), BF16),
                   jax.ShapeDtypeStruct((2, HEADS, HD, HD), F32), jax.ShapeDtypeStruct((2, HEADS, HD, HD), F32),
                   jax.ShapeDtypeStruct((2, D), F32), jax.ShapeDtypeStruct((2, D), F32),
                   jax.ShapeDtypeStruct((2, D), F32), jax.ShapeDtypeStruct((CONV_W, D), F32),
                   jax.ShapeDtypeStruct((1, D), F32)],
        in_specs=[s["xa"], s["xac"], col(L), pl.BlockSpec(memory_space=pl.ANY), s["cw"], s["cb"], s["h4"], s["h4"],
                  s["v2"], s["v2"], s["v2"]],
        out_specs=[col(L), col(LC), s["h4"], s["h4"], s["v2"], s["v2"], s["v2"], col(CONV_W), col(1)],
        scratch_shapes=[pltpu.VMEM((L, HD), F32)] * 8,
        input_output_aliases={3: 0},
        compiler_params=_params("arbitrary"),
    )(zx, zc, dyl, dz, cw, cb, wa, wx, ba, bx, lam)


def _mixer_loss(x, tgt, zx, yl, gx, fg, lng, lnb, ws, wst, bst, wout, tm):
    ncht = tm // CHUNK

    def body(x_ref, t_ref, ga_ref, u_ref, v_ref, gb_ref, yl_ref, gx_ref, fg_ref, lng_ref, lnb_ref, ws_ref, wst_ref,
             bst_ref, wout_ref,
             dz_ref, dyl_ref, dxn_ref, y_s, do_ref, dws_ref, dbst_ref, vec_ref, loss_ref,
             vn_s, mix_s, dm_s, dvn_s):
        step = pl.program_id(0)

        @pl.when(step == 0)
        def _():
            dws_ref[...] = jnp.zeros_like(dws_ref)
            dbst_ref[...] = jnp.zeros_like(dbst_ref)
            vec_ref[...] = jnp.zeros_like(vec_ref)
            loss_ref[...] = jnp.zeros_like(loss_ref)

        u, v = u_ref[...], v_ref[...]
        ug, dug_du = _gelu_and_grad(u)
        vg, dvg_dv = _gelu_and_grad(v)
        mu = jnp.mean(vg, axis=-1, keepdims=True)
        vc = vg - mu
        rstd = lax.rsqrt(jnp.mean(vc * vc, axis=-1, keepdims=True) + LN_EPS)
        vhat = vc * rstd
        lngv = lng_ref[...]
        vn_s[...] = (vhat * lngv + lnb_ref[...]).astype(BF16)
        for ch in range(ncht):
            rs = slice(ch * CHUNK, (ch + 1) * CHUNK)
            for g in range(HEADS):
                cs = slice(g * HD, (g + 1) * HD)
                mix_s[rs, cs] = _dot(ws_ref[g], vn_s[rs, cs]) + bst_ref[:, g:g + 1]
        mixed = mix_s[...]
        ga, gb, yl = ga_ref[...], gb_ref[...], yl_ref[...]
        sga, dsga = _silu_and_grad(ga)
        sgb, dsgb = _silu_and_grad(gb)
        ys = ug * mixed
        y_s[:, 0:D] = (yl * sga).astype(BF16)
        y_s[:, D:D_MIX] = (ys * sgb).astype(BF16)
        o = _dot(y_s[...], wout_ref[...])
        gxv, fgv = gx_ref[...], fg_ref[...]
        xn = x_ref[...] + gxv * o
        rs2 = lax.rsqrt(jnp.mean(xn * xn, axis=-1, keepdims=True) + NORM_EPS)
        xh = xn * rs2
        diff = xh * fgv - t_ref[...]
        loss_ref[...] += jnp.full(loss_ref.shape, jnp.sum(diff * diff) * (0.5 / D), F32)
        dout = diff * (1.0 / D)
        w = dout * fgv
        dxn = rs2 * (w - xh * jnp.mean(w * xh, axis=-1, keepdims=True))
        dxn_ref[...] = dxn
        vec_ref[0:1, :] += jnp.sum(dxn * o, axis=0, keepdims=True)
        vec_ref[1:2, :] += jnp.sum(dout * xh, axis=0, keepdims=True)
        dob = (dxn * gxv).astype(BF16)
        do_ref[...] = dob
        dy = _dot_nt(dob, wout_ref[...])
        dya, dyb = dy[:, 0:D], dy[:, D:D_MIX]
        dyl_ref[...] = dya * sga
        dys = dyb * sgb
        dz_ref[:, 0:D] = jnp.zeros((tm, D), BF16)
        dz_ref[:, D:2 * D] = (dya * yl * dsga).astype(BF16)
        dz_ref[:, 2 * D:3 * D] = (dys * mixed * dug_du).astype(BF16)
        dz_ref[:, 4 * D:5 * D] = (dyb * ys * dsgb).astype(BF16)
        dm = dys * ug
        dm_s[...] = dm.astype(BF16)
        for g in range(HEADS):
            cs = slice(g * HD, (g + 1) * HD)
            dbst_ref[:, g:g + 1] += sum(jnp.sum(dm[ch * CHUNK:(ch + 1) * CHUNK, cs], axis=1, keepdims=True)
                                        for ch in range(ncht))
            for ch in range(ncht):
                rs = slice(ch * CHUNK, (ch + 1) * CHUNK)
                dws_ref[g] += _dot_nt(dm_s[rs, cs], vn_s[rs, cs])
                dvn_s[rs, cs] = _dot(wst_ref[g], dm_s[rs, cs])
        dvn = dvn_s[...]
        vec_ref[2:3, :] += jnp.sum(dvn * vhat, axis=0, keepdims=True)
        vec_ref[3:4, :] += jnp.sum(dvn, axis=0, keepdims=True)
        dvh = dvn * lngv
        dvg = rstd * (dvh - jnp.mean(dvh, axis=-1, keepdims=True) - vhat * jnp.mean(dvh * vhat, axis=-1, keepdims=True))
        dz_ref[:, 3 * D:4 * D] = (dvg * dvg_dv).astype(BF16)

    tile = pl.BlockSpec((tm, D), lambda i: (i, 0))
    zcol = lambda n: pl.BlockSpec((tm, D), lambda i: (i, n))
    vec = pl.BlockSpec((1, D), lambda i: (0, 0))
    full = lambda *s: pl.BlockSpec(s, lambda i: (0,) * len(s))
    return _call(
        body, name="mixer_loss", grid=(L // tm,),
        out_shape=[jax.ShapeDtypeStruct((L, D_IN), BF16), jax.ShapeDtypeStruct((L, D), F32),
                   jax.ShapeDtypeStruct((L, D), F32), jax.ShapeDtypeStruct((L, D_MIX), BF16),
                   jax.ShapeDtypeStruct((L, D), BF16),
                   jax.ShapeDtypeStruct((HEADS, CHUNK, CHUNK), F32), jax.ShapeDtypeStruct((CHUNK, HEADS), F32),
                   jax.ShapeDtypeStruct((8, D), F32), jax.ShapeDtypeStruct((8, 128), F32)],
        in_specs=[tile, tile, zcol(1), zcol(2), zcol(3), zcol(4), tile, pl.BlockSpec((1, D), lambda i: (0, 2)),
                  vec, vec, vec,
                  full(HEADS, CHUNK, CHUNK), full(HEADS, CHUNK, CHUNK), full(CHUNK, HEADS),
                  pl.BlockSpec((D_MIX, D), lambda i: (0, 0), pipeline_mode=pl.Buffered(1))],
        out_specs=[pl.BlockSpec((tm, D_IN), lambda i: (i, 0)), tile, tile,
                   pl.BlockSpec((tm, D_MIX), lambda i: (i, 0)), tile,
                   full(HEADS, CHUNK, CHUNK), full(CHUNK, HEADS), full(8, D), full(8, 128)],
        scratch_shapes=[pltpu.VMEM((tm, D), BF16), pltpu.VMEM((tm, D), F32),
                        pltpu.VMEM((tm, D), BF16), pltpu.VMEM((tm, D), F32)],
        compiler_params=_params("arbitrary"),
    )(x, tgt, zx, zx, zx, zx, yl, gx, fg, lng, lnb, ws, wst, bst, wout)


def _grad_w(a, b, a2, b2, tk, name):
    nk = a.shape[0] // tk
    m, ncols = a.shape[1], b.shape[1]
    with_ctx = a2 is not None

    def body(*refs):
        if with_ctx:
            a_ref, b_ref, a2_ref, b2_ref, o_ref, acc = refs
        else:
            a_ref, b_ref, o_ref, acc = refs
        n, k = pl.program_id(0), pl.program_id(1)

        @pl.when(k == 0)
        def _():
            acc[...] = jnp.zeros_like(acc)

        acc[...] += _dot_tn(a_ref[...], b_ref[...])

        if with_ctx:
            @pl.when(jnp.logical_and(k == nk - 1, n == 0))
            def _():
                acc[...] += _dot_tn(a2_ref[...], b2_ref[...])

        @pl.when(k == nk - 1)
        def _():
            o_ref[...] = acc[...].astype(BF16)

    in_specs = [pl.BlockSpec((tk, m), lambda n, k: (k, 0)), pl.BlockSpec((tk, D), lambda n, k: (k, n))]
    args = [a, b]
    if with_ctx:
        in_specs += [pl.BlockSpec(a2.shape, lambda n, k: (0, 0)), pl.BlockSpec(b2.shape, lambda n, k: (0, 0))]
        args += [a2, b2]
    return _call(
        body, name=name, grid=(ncols // D, nk), out_shape=jax.ShapeDtypeStruct((m, ncols), BF16),
        in_specs=in_specs, out_specs=pl.BlockSpec((m, D), lambda n, k: (0, n)),
        scratch_shapes=[pltpu.VMEM((m, D), F32)],
        compiler_params=_params("arbitrary", "arbitrary"),
    )(*args)


def _grad_rows(xr, dz, w, mod, ng, dres, ncols, tm, name):
    rows = xr.shape[0]
    with_dx = dres is not None

    def body(*refs):
        if with_dx:
            x_ref, dz_ref, w_ref, sc_ref, ng_ref, dres_ref, dx_ref, vec_ref = refs
        else:
            x_ref, dz_ref, w_ref, sc_ref, ng_ref, vec_ref = refs

        @pl.when(pl.program_id(0) == 0)
        def _():
            vec_ref[...] = jnp.zeros_like(vec_ref)

        dhn = _dot_nt(dz_ref[...], w_ref[...])
        x = x_ref[...]
        rs = lax.rsqrt(jnp.mean(x * x, axis=-1, keepdims=True) + NORM_EPS)
        xh = x * rs
        ngv = ng_ref[...]
        y = xh * ngv
        vec_ref[0:1, :] += jnp.sum(dhn, axis=0, keepdims=True)
        vec_ref[1:2, :] += jnp.sum(dhn * y, axis=0, keepdims=True)
        dy = dhn * (1.0 + sc_ref[...])
        vec_ref[2:3, :] += jnp.sum(dy * xh, axis=0, keepdims=True)
        if with_dx:
            dxh = dy * ngv
            dx_ref[...] = dres_ref[...] + rs * (dxh - xh * jnp.mean(dxh * xh, axis=-1, keepdims=True))

    tile = pl.BlockSpec((tm, D), lambda i: (i, 0))
    vec = pl.BlockSpec((1, D), lambda i: (0, 0))
    in_specs = [tile, pl.BlockSpec((tm, ncols), lambda i: (i, 0)),
                pl.BlockSpec((D, ncols), lambda i: (0, 0), pipeline_mode=pl.Buffered(1)),
                pl.BlockSpec((1, D), lambda i: (0, 1)), vec]
    out_shape = [jax.ShapeDtypeStruct((8, D), F32)]
    out_specs = [pl.BlockSpec((8, D), lambda i: (0, 0))]
    args = [xr, dz, w, mod, ng]
    if with_dx:
        in_specs.append(tile)
        out_shape.insert(0, jax.ShapeDtypeStruct((rows, D), F32))
        out_specs.insert(0, tile)
        args.append(dres)
    return _call(body, name=name, grid=(rows // tm,), out_shape=out_shape, in_specs=in_specs, out_specs=out_specs,
                 compiler_params=_params("arbitrary"))(*args)


def _adamw(w, g, m, v):
    m = ADAM_B1 * m + (1.0 - ADAM_B1) * g
    v = ADAM_B2 * v + (1.0 - ADAM_B2) * (g * g)
    m_hat = m / (1.0 - ADAM_B1 ** ADAM_STEP)
    v_hat = v / (1.0 - ADAM_B2 ** ADAM_STEP)
    delta = -ADAM_LR * (m_hat / (jnp.sqrt(v_hat) + ADAM_EPS) + ADAM_WD * w)
    return delta, m, v


def _adamw_reduced(parts, w, m, v, tr, name):
    r, n = w.shape
    nparts = parts.shape[0]

    def body(p_ref, w_ref, m_ref, v_ref, g_ref, d_ref, mo_ref, vo_ref):
        g = p_ref[0].astype(F32)
        for i in range(1, nparts):
            g = g + p_ref[i].astype(F32)
        g_ref[...] = g
        d_ref[...], mo_ref[...], vo_ref[...] = _adamw(w_ref[...], g, m_ref[...], v_ref[...])

    tile = pl.BlockSpec((tr, n), lambda i: (i, 0))
    sds = jax.ShapeDtypeStruct((r, n), F32)
    return _call(
        body, name=name, grid=(r // tr,), out_shape=[sds] * 4,
        in_specs=[pl.BlockSpec((nparts, tr, n), lambda i: (0, i, 0)), tile, tile, tile], out_specs=[tile] * 4,
        compiler_params=_params("arbitrary"),
    )(parts, w, m, v)


R_GATE, R_FINAL_G, R_LN_G, R_LN_B = 0, 1, 2, 3
R_SH_X, R_SC_X, R_NG_X = 4, 5, 6
R_SH_C, R_SC_C, R_NG_C = 7, 8, 9
R_BA, R_BX, R_LAM, R_CW, R_CB, R_SGU_B = 10, 12, 14, 16, 20, 21
PACK_ROWS = 32
MAT_ROWS = 2 * (2 * HEADS * HD) + HEADS * CHUNK


def _reduce_small(vp_all, mat_parts, ada_w, me):
    nloc = ada_w.shape[1]

    def body(me_ref, vp_ref, mp_ref, w_ref, red_ref, mat_ref, dmod_ref, gab_ref, cpart_ref, dmc_s):
        red = vp_ref[0]
        for i in range(1, N_DEV):
            red = red + vp_ref[i]
        mat = mp_ref[0]
        for i in range(1, mp_ref.shape[0]):
            mat = mat + mp_ref[i]
        red_ref[...] = red
        mat_ref[...] = mat
        for e in range(N_DEV):
            dmod_ref[e:e + 1, 0:D] = vp_ref[e, R_SH_X:R_SH_X + 1, :]
            dmod_ref[e:e + 1, D:2 * D] = vp_ref[e, R_SC_X:R_SC_X + 1, :]
            dmod_ref[e:e + 1, 2 * D:3 * D] = vp_ref[e, R_GATE:R_GATE + 1, :]
        dmod_ref[8:9, 0:D] = red[R_SH_C:R_SH_C + 1, :]
        dmod_ref[8:9, D:2 * D] = red[R_SC_C:R_SC_C + 1, :]
        dmod_ref[8:9, 2 * D:3 * D] = jnp.zeros((1, D), F32)
        dmod_ref[9:16, :] = jnp.zeros((7, 3 * D), F32)
        gab_ref[:, 0:D] = red[R_SH_X:R_SH_X + 1, :] + red[R_SH_C:R_SH_C + 1, :]
        gab_ref[:, D:2 * D] = red[R_SC_X:R_SC_X + 1, :] + red[R_SC_C:R_SC_C + 1, :]
        gab_ref[:, 2 * D:3 * D] = red[R_GATE:R_GATE + 1, :]
        dmc_s[...] = jnp.broadcast_to(dmod_ref[8:9, :], (8, 3 * D))
        off = pl.multiple_of(me_ref[0] * nloc, 128)
        cpart_ref[...] = _dot_nt(dmc_s[:, pl.ds(off, nloc)], w_ref[...])

    return _call(
        body, name="reduce_small",
        out_shape=[jax.ShapeDtypeStruct((PACK_ROWS, D), F32), jax.ShapeDtypeStruct(mat_parts.shape[1:], F32),
                   jax.ShapeDtypeStruct((16, 3 * D), F32), jax.ShapeDtypeStruct((1, 3 * D), F32),
                   jax.ShapeDtypeStruct((8, D), F32)],
        in_specs=[pl.BlockSpec(memory_space=pltpu.SMEM), VMEM, VMEM, VMEM], out_specs=[VMEM] * 5,
        scratch_shapes=[pltpu.VMEM((8, 3 * D), F32)], compiler_params=_params(),
    )(me, vp_all, mat_parts, ada_w)


def _adamw_ada(c_all, c_ctx, dmod, w, m, v, me):
    nloc = w.shape[1]

    def body(me_ref, c_ref, cc_ref, dm_ref, w_ref, m_ref, v_ref, g_ref, d_ref, mo_ref, vo_ref):
        off = pl.multiple_of(me_ref[0] * nloc, 128)
        dm = dm_ref[:, pl.ds(off, nloc)]
        sx, _ = _silu_and_grad(c_ref[...])
        sc, _ = _silu_and_grad(cc_ref[...])
        g = _dot_tn(sx, dm[0:8, :]) + _dot_tn(jnp.broadcast_to(sc, (8, D)), dm[8:16, :])
        g_ref[...] = g
        d_ref[...], mo_ref[...], vo_ref[...] = _adamw(w_ref[...], g, m_ref[...], v_ref[...])

    sds = jax.ShapeDtypeStruct(w.shape, F32)
    return _call(
        body, name="adamw_ada_w", out_shape=[sds] * 4,
        in_specs=[pl.BlockSpec(memory_space=pltpu.SMEM)] + [VMEM] * 6, out_specs=[VMEM] * 4,
        compiler_params=_params(),
    )(me, c_all, c_ctx, dmod, w, m, v)


_SMALL = ("c_ctx", "ada_b", "norm_g", "conv_w", "conv_b", "lru_wa", "lru_ba", "lru_wx", "lru_bx", "lru_lambda",
          "sgu_ln_g", "sgu_ln_b", "sgu_w", "sgu_b", "final_g")


def _adamw_small(red, mat, cparts, gab, ws, ms, vs, me):
    n = len(_SMALL)
    nw = 2 * HEADS * HD

    def body(me_ref, red_ref, mat_ref, cp_ref, gab_ref, *refs):
        w_refs, m_refs, v_refs = refs[:n], refs[n:2 * n], refs[2 * n:3 * n]
        outs = refs[3 * n:]
        off = pl.multiple_of(me_ref[0] * HD, 128)

        def row(r, k=1):
            return red_ref[r:r + k, :]

        cc = w_refs[0][...]
        dcc = cp_ref[0, 0:1, :]
        for i in range(1, N_DEV):
            dcc = dcc + cp_ref[i, 0:1, :]
        grads = dict(
            c_ctx=dcc * _silu_and_grad(cc)[1], ada_b=gab_ref[...], norm_g=row(R_NG_X) + row(R_NG_C),
            conv_w=red_ref[R_CW:R_CW + CONV_W, pl.ds(off, HD)], conv_b=row(R_CB),
            lru_wa=mat_ref[0:nw, :], lru_ba=row(R_BA, 2), lru_wx=mat_ref[nw:2 * nw, :], lru_bx=row(R_BX, 2),
            lru_lambda=red_ref[R_LAM:R_LAM + 2, pl.ds(off, HD)], sgu_ln_g=row(R_LN_G), sgu_ln_b=row(R_LN_B),
            sgu_w=mat_ref[2 * nw:MAT_ROWS, :], sgu_b=row(R_SGU_B), final_g=row(R_FINAL_G))
        for j, name in enumerate(_SMALL):
            g = grads[name]
            outs[j][...] = g
            outs[n + j][...], outs[2 * n + j][...], outs[3 * n + j][...] = _adamw(w_refs[j][...], g, m_refs[j][...],
                                                                                 v_refs[j][...])

    sds = [jax.ShapeDtypeStruct(ws[k].shape, F32) for k in _SMALL]
    outs = _call(
        body, name="adamw_small", out_shape=sds * 4,
        in_specs=[pl.BlockSpec(memory_space=pltpu.SMEM)] + [VMEM] * (4 + 3 * n), out_specs=[VMEM] * (4 * n),
        compiler_params=_params(),
    )(me, red, mat, cparts, gab, *[ws[k] for k in _SMALL], *[ms[k] for k in _SMALL], *[vs[k] for k in _SMALL])
    return [dict(zip(_SMALL, outs[i * n:(i + 1) * n])) for i in range(4)]


def kernel(x, c, ctx, c_ctx, ada_w, ada_b, norm_g, w_in, conv_w, conv_b, lru_wa, lru_ba, lru_wx, lru_bx, lru_lambda, sgu_ln_g, sgu_ln_b, sgu_w, sgu_b, w_out, final_g, loss_target, m_c_ctx, m_ada_w, m_ada_b, m_norm_g, m_w_in, m_conv_w, m_conv_b, m_lru_wa, m_lru_ba, m_lru_wx, m_lru_bx, m_lru_lambda, m_sgu_ln_g, m_sgu_ln_b, m_sgu_w, m_sgu_b, m_w_out, m_final_g, v_c_ctx, v_ada_w, v_ada_b, v_norm_g, v_w_in, v_conv_w, v_conv_b, v_lru_wa, v_lru_ba, v_lru_wx, v_lru_bx, v_lru_lambda, v_sgu_ln_g, v_sgu_ln_b, v_sgu_w, v_sgu_b, v_w_out, v_final_g):
    args = dict(locals())
    me_s = 4 * lax.axis_index("x") + 2 * lax.axis_index("y") + lax.axis_index("c")
    me = me_s.astype(jnp.int32).reshape(1)
    xr, ctxr, tgt = x[0], ctx[0], loss_target[0]
    cc = c_ctx.reshape(1, D)
    nw = 2 * HEADS * HD
    view = dict(c_ctx=(1, D), ada_b=(1, 3 * D), norm_g=(1, D), conv_w=(CONV_W, HD), conv_b=(1, D), lru_wa=(nw, HD),
                lru_ba=(2, D), lru_wx=(nw, HD), lru_bx=(2, D), lru_lambda=(2, HD), sgu_ln_g=(1, D), sgu_ln_b=(1, D),
                sgu_w=(HEADS * CHUNK, CHUNK), sgu_b=(1, D), final_g=(1, D))

    (c_all,) = _gather2([c], ["ag"], "gather_c")
    c_all = c_all.reshape(N_DEV, D)
    part = _ada_forward(c_all, cc, ada_w[0], ada_b, me)
    (parts,) = _gather2([part], ["ag"], "gather_mod")
    modx = lax.dynamic_index_in_dim(parts, me_s, axis=1, keepdims=False).reshape(1, 3 * D)
    modc = parts[:, 8, :].reshape(1, 3 * D)

    w_in_b, w_out_b = _cast_weights(w_in[0], w_out[0])
    w_full, wout_all, cw_full, lam_full = _gather2(
        [w_in_b, w_out_b, conv_w[0], lru_lambda[0]], ["agc", "ag", "agc", "agc"], "gather_weights")
    wout_full = wout_all.reshape(D_MIX, D)

    zx, hn = _project(xr, modx, norm_g, w_full, D_IN, 256, "project_x")
    zc, hnc = _project(ctxr, modc, norm_g, w_full, D, LC, "project_ctx")
    ba, bx = lru_ba.reshape(2, D), lru_bx.reshape(2, D)
    yl = _lru_forward(zx, zc, cw_full, conv_b, lru_wa[0], lru_wx[0], ba, bx, lam_full)
    ws_b = sgu_w[0].astype(BF16)
    dz, dyl, dxn, ycat, dob, dws, dbst, mvec, lossv = _mixer_loss(
        xr, tgt, zx, yl, modx, final_g.reshape(1, D), sgu_ln_g, sgu_ln_b, ws_b, jnp.swapaxes(ws_b, 1, 2),
        sgu_b[0].T, wout_full, 128)

    dz, dxac, dwa, dwx, dba, dbx, dlam, dcw, dcb = _lru_backward(
        zx, zc, dyl, dz, cw_full, conv_b, lru_wa[0], lru_wx[0], ba, bx, lam_full)
    gw_in = _grad_w(hn, dz, hnc, dxac, 512, "grad_w_in")
    gw_out = _grad_w(ycat, dob, None, None, 512, "grad_w_out")
    gx, xvec = _grad_rows(xr, dz, w_full, modx, norm_g, dxn, D_IN, 256, "grad_rows_x")
    (cvec,) = _grad_rows(ctxr, dxac, w_full, modc, norm_g, None, D, LC, "grad_rows_ctx")

    pack = jnp.concatenate([mvec[0:4], xvec[0:3], cvec[0:3], dba, dbx, dlam, dcw, dcb, dbst.T.reshape(1, D),
                            jnp.zeros((PACK_ROWS - R_SGU_B - 1, D), F32)], axis=0)
    matpack = jnp.concatenate([dwa.reshape(nw, HD), dwx.reshape(nw, HD), dws.reshape(HEADS * CHUNK, CHUNK)], axis=0)
    win_parts, wout_parts, mat_parts = _reduce2(
        [gw_in, gw_out.reshape(N_DEV, D_MIX // N_DEV, D), matpack.reshape(N_DEV, MAT_ROWS // N_DEV, HD)],
        ["a2ac", "a2a", "a2a"], me, "reduce_grads")
    (vp_all,) = _gather2([pack], ["ag"], "gather_pack")
    red, matpiece, dmod, gab, cpart = _reduce_small(vp_all, mat_parts, ada_w[0], me)
    mat_all, cparts = _gather2([matpiece, cpart], ["ag", "ag"], "gather_small")

    g_w_in, d_w_in, nm_w_in, nv_w_in = _adamw_reduced(win_parts, w_in[0], m_w_in[0], v_w_in[0], 256, "adamw_w_in")
    g_w_out, d_w_out, nm_w_out, nv_w_out = _adamw_reduced(wout_parts, w_out[0], m_w_out[0], v_w_out[0], 128,
                                                          "adamw_w_out")
    g_ada, d_ada, nm_ada, nv_ada = _adamw_ada(c_all, cc, dmod, ada_w[0], m_ada_w[0], v_ada_w[0], me)
    ws = {k: args[k].reshape(view[k]) for k in _SMALL}
    ms = {k: args["m_" + k].reshape(view[k]) for k in _SMALL}
    vs = {k: args["v_" + k].reshape(view[k]) for k in _SMALL}
    small = _adamw_small(red, mat_all.reshape(MAT_ROWS, HD), cparts, gab, ws, ms, vs, me)
    big = dict(w_in=(g_w_in, d_w_in, nm_w_in, nv_w_in), w_out=(g_w_out, d_w_out, nm_w_out, nv_w_out),
               ada_w=(g_ada, d_ada, nm_ada, nv_ada))

    loss = lax.psum(lossv[0, 0], ("x", "y", "c"))
    names = ("c_ctx", "ada_w", "ada_b", "norm_g", "w_in", "conv_w", "conv_b", "lru_wa", "lru_ba", "lru_wx", "lru_bx",
             "lru_lambda", "sgu_ln_g", "sgu_ln_b", "sgu_w", "sgu_b", "w_out", "final_g")
    outs = [loss, gx.reshape(x.shape)]
    for kind in range(4):
        for k in names:
            val = big[k][kind] if k in big else small[kind][k]
            outs.append(val.reshape(args[k].shape))
    return tuple(outs)
```

```python
import functools

import jax
import jax.numpy as jnp
from jax import lax
from jax.experimental import pallas as pl
from jax.experimental.pallas import tpu as pltpu

F32 = jnp.float32
BF16 = jnp.bfloat16

N_DEV = 8
D = 1024
L = 2048
LC = 256
HEADS = 8
HD = 128
CHUNK = 128
D_IN = 5 * D
D_MIX = 2 * D
CONV_W = 4
LRU_C = 8.0
NORM_EPS = 1e-6
LN_EPS = 1e-5
ADAM_LR, ADAM_B1, ADAM_B2, ADAM_EPS, ADAM_WD, ADAM_STEP = 0.001, 0.9, 0.999, 1e-08, 0.01, 10

VMEM_LIMIT = 56 * 1024 * 1024

HBM = pl.BlockSpec(memory_space=pltpu.HBM)
VMEM = pl.BlockSpec(memory_space=pltpu.VMEM)
MESH = pl.DeviceIdType.MESH


def _call(body, **kw):
    return pl.pallas_call(body, **kw)


def _params(*sem):
    return pltpu.CompilerParams(dimension_semantics=sem, vmem_limit_bytes=VMEM_LIMIT)


def _sigmoid(x):
    return 0.5 * jnp.tanh(0.5 * x) + 0.5


def _silu_and_grad(x):
    s = _sigmoid(x)
    return x * s, s * (1.0 + x * (1.0 - s))


_G0 = 0.7978845608028654
_G1 = 0.044715


def _gelu_and_grad(x):
    x2 = x * x
    t = jnp.tanh(_G0 * (x + _G1 * x * x2))
    cdf = 0.5 * (1.0 + t)
    return x * cdf, cdf + 0.5 * x * (1.0 - t * t) * (_G0 * (1.0 + 3.0 * _G1 * x2))


def _gelu(x):
    return 0.5 * x * (1.0 + jnp.tanh(_G0 * (x + _G1 * x * x * x)))


def _softplus(z):
    t = jnp.exp(-jnp.abs(z))
    u = 1.0 + t
    log1p = jnp.where(u == 1.0, t, jnp.log(u) * t / jnp.where(u == 1.0, 1.0, u - 1.0))
    return jnp.maximum(z, 0.0) + log1p


def _dot(a, b):
    return jnp.dot(a, b, preferred_element_type=F32)


def _dot_nt(a, b):
    return lax.dot_general(a, b, (((1,), (1,)), ((), ())), preferred_element_type=F32)


def _dot_tn(a, b):
    return lax.dot_general(a, b, (((0,), (0,)), ((), ())), preferred_element_type=F32)


def _rows(shape):
    return lax.broadcasted_iota(jnp.int32, shape, 0)


def _shift_down(x, first):
    return jnp.where(_rows(x.shape) == 0, first, pltpu.roll(x, 1, 0))


def _shift_up(x, last):
    n = x.shape[0]
    return jnp.where(_rows(x.shape) == n - 1, last, pltpu.roll(x, n - 1, 0))


def _gather2(arrays, modes, name):
    n = len(arrays)
    out_shape = [jax.ShapeDtypeStruct((N_DEV,) + a.shape if m == "ag" else (a.shape[0], N_DEV * a.shape[1]), a.dtype)
                 for a, m in zip(arrays, modes)]

    def body(*refs):
        ins, outs = refs[:n], refs[n:2 * n]
        send_sems, recv_sems, local_sems = refs[2 * n:]
        x, y, c = lax.axis_index("x"), lax.axis_index("y"), lax.axis_index("c")
        sibling = (x, y, 1 - c)
        chips = [(x ^ (k >> 1), y ^ (k & 1)) for k in (1, 2, 3)]

        def slot(j, px, py, pc):
            dev = 4 * px + 2 * py + pc
            if modes[j] == "agc":
                w = ins[j].shape[1]
                return outs[j].at[:, pl.ds(pl.multiple_of(dev * w, 128), w)]
            return outs[j].at[dev]

        def copy(j, k, block, to, src=None):
            return pltpu.make_async_remote_copy(
                src_ref=slot(j, *block) if src is None else src, dst_ref=slot(j, *block),
                send_sem=send_sems.at[j, k], recv_sem=recv_sems.at[j, k], device_id=to, device_id_type=MESH)

        me = (x, y, c)
        sends, locals_ = [], []
        for j in range(n):
            own = pltpu.make_async_copy(ins[j], slot(j, *me), local_sems.at[j])
            own.start()
            locals_.append(own)
            first = [copy(j, 0, me, sibling, src=ins[j])]
            first += [copy(j, 1 + i, me, (*chip, c), src=ins[j]) for i, chip in enumerate(chips)]
            for cp in first:
                cp.start()
            sends += first
        for i, chip in enumerate(chips):
            for j in range(n):
                copy(j, 1 + i, (*chip, c), me).wait_recv()
                fwd = copy(j, 4 + i, (*chip, c), sibling)
                fwd.start()
                sends.append(fwd)
        for j in range(n):
            copy(j, 0, sibling, me).wait_recv()
            for i, chip in enumerate(chips):
                copy(j, 4 + i, (*chip, 1 - c), me).wait_recv()
        for cp in sends:
            cp.wait_send()
        for cp in locals_:
            cp.wait()

    return _call(
        body, name=name, out_shape=out_shape, in_specs=[HBM] * n, out_specs=[HBM] * n,
        scratch_shapes=[pltpu.SemaphoreType.DMA((n, N_DEV - 1)), pltpu.SemaphoreType.DMA((n, N_DEV - 1)),
                        pltpu.SemaphoreType.DMA((n,))],
        compiler_params=pltpu.CompilerParams(has_side_effects=True),
    )(*[pltpu.with_memory_space_constraint(a, pltpu.HBM) for a in arrays])


def _reduce2(arrays, modes, me, name):
    n = len(arrays)
    shapes = [(a.shape[1], a.shape[2]) if m == "a2a" else (a.shape[0], a.shape[1] // N_DEV)
              for a, m in zip(arrays, modes)]
    staged = [jax.ShapeDtypeStruct((4,) + s, a.dtype) for s, a in zip(shapes, arrays)]

    def piece(ref, mode, dev, w):
        return ref.at[dev] if mode == "a2a" else ref.at[:, pl.ds(pl.multiple_of(dev * w, 128), w)]

    def to_sibling(*refs):
        ins, outs = refs[:n], refs[n:2 * n]
        send_sems, recv_sems = refs[2 * n:]
        x, y, c = lax.axis_index("x"), lax.axis_index("y"), lax.axis_index("c")
        copies = []
        for j in range(n):
            for q in range(4):
                cp = pltpu.make_async_remote_copy(
                    src_ref=piece(ins[j], modes[j], 2 * q + (1 - c), shapes[j][1]), dst_ref=outs[j].at[q],
                    send_sem=send_sems.at[j, q], recv_sem=recv_sems.at[j, q], device_id=(x, y, 1 - c),
                    device_id_type=MESH)
                cp.start()
                copies.append(cp)
        for cp in copies:
            cp.wait()

    stage = _call(
        to_sibling, name=name + "_d2d", out_shape=staged, in_specs=[HBM] * n, out_specs=[HBM] * n,
        scratch_shapes=[pltpu.SemaphoreType.DMA((n, 4)), pltpu.SemaphoreType.DMA((n, 4))],
        compiler_params=pltpu.CompilerParams(has_side_effects=True),
    )(*[pltpu.with_memory_space_constraint(a, pltpu.HBM) for a in arrays])

    def add(me_ref, *refs):
        del me_ref
        own, got, outs = refs[:n], refs[n:2 * n], refs[2 * n:]
        for j in range(n):
            mine = own[j][0] if modes[j] == "a2a" else own[j][...]
            outs[j][0] = (mine.astype(F32) + got[j][0].astype(F32)).astype(outs[j].dtype)

    in_specs = []
    for (r, w), m in zip(shapes, modes):
        if m == "a2a":
            in_specs.append(pl.BlockSpec((1, r, w), lambda q, me_ref: (2 * q + me_ref[0] % 2, 0, 0)))
        else:
            in_specs.append(pl.BlockSpec((r, w), lambda q, me_ref: (0, 2 * q + me_ref[0] % 2)))
    slot_specs = [pl.BlockSpec((1, r, w), lambda q, me_ref: (q, 0, 0)) for r, w in shapes]
    sums = _call(
        add, name=name + "_add", out_shape=staged,
        grid_spec=pltpu.PrefetchScalarGridSpec(num_scalar_prefetch=1, grid=(4,), in_specs=in_specs + slot_specs,
                                               out_specs=slot_specs),
        compiler_params=_params("arbitrary"),
    )(me, *arrays, *stage)

    def to_chips(*refs):
        ins, outs = refs[:n], refs[n:2 * n]
        send_sems, recv_sems, local_sems = refs[2 * n:]
        x, y, c = lax.axis_index("x"), lax.axis_index("y"), lax.axis_index("c")
        qm = 2 * x + y
        copies = []
        for j in range(n):
            for k in (1, 2, 3):
                px, py = x ^ (k >> 1), y ^ (k & 1)
                cp = pltpu.make_async_remote_copy(
                    src_ref=ins[j].at[2 * px + py], dst_ref=outs[j].at[qm], send_sem=send_sems.at[j, k - 1],
                    recv_sem=recv_sems.at[j, k - 1], device_id=(px, py, c), device_id_type=MESH)
                cp.start()
                copies.append(cp)
            own = pltpu.make_async_copy(ins[j].at[qm], outs[j].at[qm], local_sems.at[j])
            own.start()
            copies.append(own)
        for cp in copies:
            cp.wait()

    return _call(
        to_chips, name=name + "_ici", out_shape=staged, in_specs=[HBM] * n, out_specs=[HBM] * n,
        scratch_shapes=[pltpu.SemaphoreType.DMA((n, 3)), pltpu.SemaphoreType.DMA((n, 3)),
                        pltpu.SemaphoreType.DMA((n,))],
        compiler_params=pltpu.CompilerParams(has_side_effects=True),
    )(*[pltpu.with_memory_space_constraint(a, pltpu.HBM) for a in sums])


def _ada_forward(c_all, c_ctx, ada_w, ada_b, me):
    nloc = ada_w.shape[1]

    def body(me_ref, c_ref, cc_ref, w_ref, b_ref, o_ref):
        off = pl.multiple_of(me_ref[0] * nloc, 128)
        b = b_ref[:, pl.ds(off, nloc)]
        w = w_ref[...]
        sx, _ = _silu_and_grad(c_ref[...])
        sc, _ = _silu_and_grad(jnp.broadcast_to(cc_ref[...], (8, D)))
        o_ref[0:8, :] = _dot(sx, w) + b
        o_ref[8:16, :] = _dot(sc, w) + b

    return _call(
        body, name="ada_forward", out_shape=jax.ShapeDtypeStruct((16, nloc), F32),
        in_specs=[pl.BlockSpec(memory_space=pltpu.SMEM), VMEM, VMEM, VMEM, VMEM], out_specs=VMEM,
    )(me, c_all, c_ctx, ada_w, ada_b)


def _cast_weights(w_in, w_out):
    def body(a_ref, b_ref, ao_ref, bo_ref):
        ao_ref[...] = a_ref[...].astype(BF16)
        bo_ref[...] = b_ref[...].astype(BF16)

    return _call(
        body, name="cast_weights",
        out_shape=[jax.ShapeDtypeStruct(w_in.shape, BF16), jax.ShapeDtypeStruct(w_out.shape, BF16)],
        in_specs=[VMEM, VMEM], out_specs=[VMEM, VMEM], compiler_params=_params(),
    )(w_in, w_out)


def _project(xr, mod, ng, w, ncols, tm, name):
    rows = xr.shape[0]

    def body(x_ref, sh_ref, sc_ref, ng_ref, w_ref, z_ref, hn_ref):
        x = x_ref[...]
        rs = lax.rsqrt(jnp.mean(x * x, axis=-1, keepdims=True) + NORM_EPS)
        hn = (x * rs * ng_ref[...]) * (1.0 + sc_ref[...]) + sh_ref[...]
        hb = hn.astype(BF16)
        hn_ref[...] = hb
        for n in range(ncols // D):
            z_ref[:, n * D:(n + 1) * D] = _dot(hb, w_ref[:, n * D:(n + 1) * D])

    vec = pl.BlockSpec((1, D), lambda i: (0, 0))
    return _call(
        body, name=name, grid=(rows // tm,),
        out_shape=[jax.ShapeDtypeStruct((rows, ncols), F32), jax.ShapeDtypeStruct((rows, D), BF16)],
        in_specs=[pl.BlockSpec((tm, D), lambda i: (i, 0)), vec, pl.BlockSpec((1, D), lambda i: (0, 1)), vec,
                  pl.BlockSpec((D, ncols), lambda i: (0, 0), pipeline_mode=pl.Buffered(1))],
        out_specs=[pl.BlockSpec((tm, ncols), lambda i: (i, 0)), pl.BlockSpec((tm, D), lambda i: (i, 0))],
        compiler_params=_params("arbitrary"),
    )(xr, mod, mod, ng, w)


def _scan_pair(af_ref, uf_ref, hf_ref, h0f, ab_ref, ub_ref, hb_ref, h0b, t_len):
    span = 8 * SCAN_BLOCKS
    nit = t_len // span
    rows = _rows((8, HD))

    def local_scan(a, b, forward):
        for s in (1, 2, 4):
            sh = s if forward else 8 - s
            m = rows >= s if forward else rows < 8 - s
            b = a * jnp.where(m, pltpu.roll(b, sh, 0), 0.0) + b
            a = a * jnp.where(m, pltpu.roll(a, sh, 0), 1.0)
        return a, b

    def span_scan(a_ref, u_ref, h_ref, off, carry, forward):
        order = range(SCAN_BLOCKS) if forward else range(SCAN_BLOCKS - 1, -1, -1)
        last = slice(7, 8) if forward else slice(0, 1)
        for q in order:
            rs = pl.ds(off + 8 * q, 8)
            a, b = local_scan(a_ref[rs, :], u_ref[rs, :], forward)
            h_ref[rs, :] = b + a * carry
            carry = a[last, :] * carry + b[last, :]
        return carry

    def body(k, carry):
        cf, cb = carry
        cf = span_scan(af_ref, uf_ref, hf_ref, pl.multiple_of(k * span, span), cf, True)
        cb = span_scan(ab_ref, ub_ref, hb_ref, pl.multiple_of((nit - 1 - k) * span, span), cb, False)
        return cf, cb

    return lax.fori_loop(0, nit, body, (h0f, h0b))


SCAN_BLOCKS = 4


def _conv(xa, cw, cb):
    z = jnp.zeros((1, HD), F32)
    xm1 = _shift_down(xa, z)
    xp1 = _shift_up(xa, z)
    xp2 = _shift_up(xp1, z)
    return xm1 * cw[0:1, :] + xa * cw[1:2, :] + xp1 * cw[2:3, :] + xp2 * cw[3:4, :] + cb


def _gates(xc, wa, wx, ba, bx, nsp):
    xb = xc.astype(BF16)
    r = _sigmoid(_dot(xb, wa) + ba)
    i = _sigmoid(_dot(xb, wx) + bx)
    log_a = r * nsp
    a = jnp.exp(log_a)
    g2 = jnp.tanh(log_a) * (-1.0 - a * a)
    rg = lax.rsqrt(jnp.maximum(g2, 1e-30))
    return r, i, a, g2 * rg, rg


def _lru_param_specs():
    h4 = pl.BlockSpec((2, 1, HD, HD), lambda h: (0, h, 0, 0))
    v2 = pl.BlockSpec((2, HD), lambda h: (0, h))
    return dict(
        xa=pl.BlockSpec((L, HD), lambda h: (0, h)), xac=pl.BlockSpec((LC, HD), lambda h: (0, h)),
        cw=pl.BlockSpec((CONV_W, HD), lambda h: (0, h)), cb=pl.BlockSpec((1, HD), lambda h: (0, h)), h4=h4, v2=v2)


def _lru_forward(zx, zc, cw, cb, wa, wx, ba, bx, lam):
    def body(xa_ref, xac_ref, cw_ref, cb_ref, wa_ref, wx_ref, ba_ref, bx_ref, lam_ref, yl_ref,
             af, uf, hf, ab, ub, hb):
        cwv, cbv = cw_ref[...], cb_ref[...]
        nsp = (-LRU_C) * _softplus(-lam_ref[...])

        def forward(xa, t_len, h0f, h0b):
            xc = _conv(xa, cwv, cbv)
            for d, (a_ref, u_ref) in enumerate(((af, uf), (ab, ub))):
                _, i, a, gamma, _ = _gates(xc, wa_ref[d, 0].astype(BF16), wx_ref[d, 0].astype(BF16),
                                           ba_ref[d:d + 1, :], bx_ref[d:d + 1, :], nsp[d:d + 1, :])
                a_ref[0:t_len, :] = a
                u_ref[0:t_len, :] = gamma * (i * xc)
            return _scan_pair(af, uf, hf, h0f, ab, ub, hb, h0b, t_len)

        z = jnp.zeros((1, HD), F32)
        h0f, h0b = forward(xac_ref[...], LC, z, z)
        forward(xa_ref[...], L, h0f, h0b)
        yl_ref[...] = hf[...] + hb[...]

    s = _lru_param_specs()
    return _call(
        body, name="lru_forward", grid=(HEADS,), out_shape=jax.ShapeDtypeStruct((L, D), F32),
        in_specs=[s["xa"], s["xac"], s["cw"], s["cb"], s["h4"], s["h4"], s["v2"], s["v2"], s["v2"]],
        out_specs=pl.BlockSpec((L, HD), lambda h: (0, h)),
        scratch_shapes=[pltpu.VMEM((L, HD), F32)] * 6,
        compiler_params=_params("arbitrary"),
    )(zx, zc, cw, cb, wa, wx, ba, bx, lam)


def _lru_backward(zx, zc, dyl, dz, cw, cb, wa, wx, ba, bx, lam):
    def body(xa_ref, xac_ref, dyl_ref, dz_in, cw_ref, cb_ref, wa_ref, wx_ref, ba_ref, bx_ref, lam_ref,
             dxa_ref, dxac_ref, dwa_ref, dwx_ref, dba_ref, dbx_ref, dlam_ref, dcw_ref, dcb_ref,
             main_s, ctx_s):
        del dz_in
        cwv, cbv = cw_ref[...], cb_ref[...]
        lamv = lam_ref[...]
        sp = _softplus(-lamv)
        nsp = (-LRU_C) * sp
        z = jnp.zeros((1, HD), F32)

        def wmat(ref, d):
            return ref[d, 0].astype(BF16)

        def workspace(s):
            return dict(a=(s.at[0], s.at[1]), u=(s.at[2], s.at[3]), h=(s.at[4], s.at[5]), rho=(s.at[6], s.at[7]),
                        saved=(tuple(s.at[8 + k] for k in range(4)), tuple(s.at[12 + k] for k in range(4))),
                        xc=s.at[16])

        def forward(ws, xa, t_len, h0f, h0b):
            xc = _conv(xa, cwv, cbv)
            ws["xc"][...] = xc
            for d in (0, 1):
                vals = _gates(xc, wmat(wa_ref, d), wmat(wx_ref, d), ba_ref[d:d + 1, :], bx_ref[d:d + 1, :],
                              nsp[d:d + 1, :])
                r, i, a, gamma, rg = vals
                ws["a"][d][...] = a
                ws["u"][d][...] = gamma * (i * xc)
                for ref, val in zip(ws["saved"][d], (r, i, gamma, rg)):
                    ref[...] = val
            return _scan_pair(ws["a"][0], ws["u"][0], ws["h"][0], h0f, ws["a"][1], ws["u"][1], ws["h"][1], h0b,
                              t_len)

        def backward(ws, xa, t_len, h0f, h0b, dhf, dhb, first):
            xc = ws["xc"][...]
            (af, ab), (uf, ub), (hf, hb), (rf, rb) = ws["a"], ws["u"], ws["h"], ws["rho"]
            uf[...] = ab[...] * dhb
            ub[...] = af[...] * dhf
            rho_b_last, rho_f_first = _scan_pair(ab, uf, rb, z, af, ub, rf, z, t_len)
            dxc = jnp.zeros((t_len, HD), F32)
            dsp = []
            for d in (0, 1):
                r, i, gamma, rg = (ref[...] for ref in ws["saved"][d])
                a = ws["a"][d][...]
                if d == 0:
                    lam_t = dhf + _shift_up(rf[...], z)
                    h_prev = _shift_down(hf[...], h0f)
                else:
                    lam_t = dhb + _shift_down(rb[...], z)
                    h_prev = _shift_up(hb[...], h0b)
                da = lam_t * h_prev
                lx = lam_t * xc
                d_i = lx * gamma
                d_gamma = lx * i
                dxc = dxc + lam_t * (gamma * i)
                d_log_a = a * (da - d_gamma * (a * rg))
                dsp.append(jnp.sum(d_log_a * r, axis=0, keepdims=True) * (-LRU_C))
                d_pre_r = d_log_a * nsp[d:d + 1, :] * (r * (1.0 - r))
                d_pre_i = d_i * (i * (1.0 - i))
                prb, pib, xb = d_pre_r.astype(BF16), d_pre_i.astype(BF16), xc.astype(BF16)
                dxc = dxc + _dot_nt(prb, wmat(wa_ref, d)) + _dot_nt(pib, wmat(wx_ref, d))
                g_wa, g_wx = _dot_tn(xb, prb), _dot_tn(xb, pib)
                g_ba = jnp.sum(d_pre_r, axis=0, keepdims=True)
                g_bx = jnp.sum(d_pre_i, axis=0, keepdims=True)
                if first:
                    dwa_ref[d, 0] = g_wa
                    dwx_ref[d, 0] = g_wx
                    dba_ref[d:d + 1, :] = g_ba
                    dbx_ref[d:d + 1, :] = g_bx
                else:
                    dwa_ref[d, 0] += g_wa
                    dwx_ref[d, 0] += g_wx
                    dba_ref[d:d + 1, :] += g_ba
                    dbx_ref[d:d + 1, :] += g_bx
            g_lam = jnp.concatenate(dsp, axis=0) * (-_sigmoid(-lamv))
            dm1 = _shift_down(dxc, z)
            dp1 = _shift_up(dxc, z)
            dm2 = _shift_down(dm1, z)
            dxa = dp1 * cwv[0:1, :] + dxc * cwv[1:2, :] + dm1 * cwv[2:3, :] + dm2 * cwv[3:4, :]
            xm1 = _shift_down(xa, z)
            xp1 = _shift_up(xa, z)
            xp2 = _shift_up(xp1, z)
            g_cw = jnp.concatenate([jnp.sum(dxc * v, axis=0, keepdims=True) for v in (xm1, xa, xp1, xp2)], axis=0)
            g_cb = jnp.sum(dxc, axis=0, keepdims=True)
            if first:
                dlam_ref[...] = g_lam
                dcw_ref[...] = g_cw
                dcb_ref[...] = g_cb
            else:
                dlam_ref[...] += g_lam
                dcw_ref[...] += g_cw
                dcb_ref[...] += g_cb
            return dxa, rho_f_first, rho_b_last

        ws_x, ws_c = workspace(main_s), workspace(ctx_s)
        h0f, h0b = forward(ws_c, xac_ref[...], LC, z, z)
        forward(ws_x, xa_ref[...], L, h0f, h0b)
        dh = dyl_ref[...]
        dxa, dh0f, dh0b = backward(ws_x, xa_ref[...], L, h0f, h0b, dh, dh, True)
        dxa_ref[...] = dxa.astype(BF16)
        rc = _rows((LC, HD))
        dxac, _, _ = backward(ws_c, xac_ref[...], LC, z, z, jnp.where(rc == LC - 1, dh0f, 0.0),
                              jnp.where(rc == 0, dh0b, 0.0), False)
        dxac_ref[...] = dxac.astype(BF16)

    s = _lru_param_specs()
    col = lambda r: pl.BlockSpec((r, HD), lambda h: (0, h))
    return _call(
        body, name="lru_backward", grid=(HEADS,),
        out_shape=[jax.ShapeDtypeStruct((L, D_IN), BF16), jax.ShapeDtypeStruct((LC, D), BF16),
                   jax.ShapeDtypeStruct((2, HEADS, HD, HD), F32), jax.ShapeDtypeStruct((2, HEADS, HD, HD), F32),
                   jax.ShapeDtypeStruct((2, D), F32), jax.ShapeDtypeStruct((2, D), F32),
                   jax.ShapeDtypeStruct((2, D), F32), jax.ShapeDtypeStruct((CONV_W, D), F32),
                   jax.ShapeDtypeStruct((1, D), F32)],
        in_specs=[s["xa"], s["xac"], col(L), pl.BlockSpec(memory_space=pl.ANY), s["cw"], s["cb"], s["h4"], s["h4"],
                  s["v2"], s["v2"], s["v2"]],
        out_specs=[col(L), col(LC), s["h4"], s["h4"], s["v2"], s["v2"], s["v2"], col(CONV_W), col(1)],
        scratch_shapes=[pltpu.VMEM((17, L, HD), F32), pltpu.VMEM((17, LC, HD), F32)],
        input_output_aliases={3: 0},
        compiler_params=_params("arbitrary"),
    )(zx, zc, dyl, dz, cw, cb, wa, wx, ba, bx, lam)


def _mixer_loss(x, tgt, zx, yl, gx, fg, lng, lnb, ws, wst, bst, wout, tm):
    ncht = tm // CHUNK

    def body(x_ref, t_ref, ga_ref, u_ref, v_ref, gb_ref, yl_ref, gx_ref, fg_ref, lng_ref, lnb_ref, ws_ref, wst_ref,
             bst_ref, wout_ref,
             dz_ref, dyl_ref, dxn_ref, y_s, do_ref, dws_ref, dbst_ref, vec_ref, loss_ref,
             vn_s, mix_s, dm_s, dvn_s):
        step = pl.program_id(0)

        @pl.when(step == 0)
        def _():
            dws_ref[...] = jnp.zeros_like(dws_ref)
            dbst_ref[...] = jnp.zeros_like(dbst_ref)
            vec_ref[...] = jnp.zeros_like(vec_ref)
            loss_ref[...] = jnp.zeros_like(loss_ref)

        u, v = u_ref[...], v_ref[...]
        ug, dug_du = _gelu_and_grad(u)
        vg, dvg_dv = _gelu_and_grad(v)
        mu = jnp.mean(vg, axis=-1, keepdims=True)
        vc = vg - mu
        rstd = lax.rsqrt(jnp.mean(vc * vc, axis=-1, keepdims=True) + LN_EPS)
        vhat = vc * rstd
        lngv = lng_ref[...]
        vn_s[...] = (vhat * lngv + lnb_ref[...]).astype(BF16)
        for ch in range(ncht):
            rs = slice(ch * CHUNK, (ch + 1) * CHUNK)
            for g in range(HEADS):
                cs = slice(g * HD, (g + 1) * HD)
                mix_s[rs, cs] = _dot(ws_ref[g], vn_s[rs, cs]) + bst_ref[:, g:g + 1]
        mixed = mix_s[...]
        ga, gb, yl = ga_ref[...], gb_ref[...], yl_ref[...]
        sga, dsga = _silu_and_grad(ga)
        sgb, dsgb = _silu_and_grad(gb)
        ys = ug * mixed
        y_s[:, 0:D] = (yl * sga).astype(BF16)
        y_s[:, D:D_MIX] = (ys * sgb).astype(BF16)
        o = _dot(y_s[...], wout_ref[...])
        gxv, fgv = gx_ref[...], fg_ref[...]
        xn = x_ref[...] + gxv * o
        rs2 = lax.rsqrt(jnp.mean(xn * xn, axis=-1, keepdims=True) + NORM_EPS)
        xh = xn * rs2
        diff = xh * fgv - t_ref[...]
        loss_ref[...] += jnp.full(loss_ref.shape, jnp.sum(diff * diff) * (0.5 / D), F32)
        dout = diff * (1.0 / D)
        w = dout * fgv
        dxn = rs2 * (w - xh * jnp.mean(w * xh, axis=-1, keepdims=True))
        dxn_ref[...] = dxn
        vec_ref[0:1, :] += jnp.sum(dxn * o, axis=0, keepdims=True)
        vec_ref[1:2, :] += jnp.sum(dout * xh, axis=0, keepdims=True)
        dob = (dxn * gxv).astype(BF16)
        do_ref[...] = dob
        dy = _dot_nt(dob, wout_ref[...])
        dya, dyb = dy[:, 0:D], dy[:, D:D_MIX]
        dyl_ref[...] = dya * sga
        dys = dyb * sgb
        dz_ref[:, 0:D] = jnp.zeros((tm, D), BF16)
        dz_ref[:, D:2 * D] = (dya * yl * dsga).astype(BF16)
        dz_ref[:, 2 * D:3 * D] = (dys * mixed * dug_du).astype(BF16)
        dz_ref[:, 4 * D:5 * D] = (dyb * ys * dsgb).astype(BF16)
        dm = dys * ug
        dm_s[...] = dm.astype(BF16)
        for g in range(HEADS):
            cs = slice(g * HD, (g + 1) * HD)
            dbst_ref[:, g:g + 1] += sum(jnp.sum(dm[ch * CHUNK:(ch + 1) * CHUNK, cs], axis=1, keepdims=True)
                                        for ch in range(ncht))
            for ch in range(ncht):
                rs = slice(ch * CHUNK, (ch + 1) * CHUNK)
                dws_ref[g] += _dot_nt(dm_s[rs, cs], vn_s[rs, cs])
                dvn_s[rs, cs] = _dot(wst_ref[g], dm_s[rs, cs])
        dvn = dvn_s[...]
        vec_ref[2:3, :] += jnp.sum(dvn * vhat, axis=0, keepdims=True)
        vec_ref[3:4, :] += jnp.sum(dvn, axis=0, keepdims=True)
        dvh = dvn * lngv
        dvg = rstd * (dvh - jnp.mean(dvh, axis=-1, keepdims=True) - vhat * jnp.mean(dvh * vhat, axis=-1, keepdims=True))
        dz_ref[:, 3 * D:4 * D] = (dvg * dvg_dv).astype(BF16)

    tile = pl.BlockSpec((tm, D), lambda i: (i, 0))
    zcol = lambda n: pl.BlockSpec((tm, D), lambda i: (i, n))
    vec = pl.BlockSpec((1, D), lambda i: (0, 0))
    full = lambda *s: pl.BlockSpec(s, lambda i: (0,) * len(s))
    return _call(
        body, name="mixer_loss", grid=(L // tm,),
        out_shape=[jax.ShapeDtypeStruct((L, D_IN), BF16), jax.ShapeDtypeStruct((L, D), F32),
                   jax.ShapeDtypeStruct((L, D), F32), jax.ShapeDtypeStruct((L, D_MIX), BF16),
                   jax.ShapeDtypeStruct((L, D), BF16),
                   jax.ShapeDtypeStruct((HEADS, CHUNK, CHUNK), F32), jax.ShapeDtypeStruct((CHUNK, HEADS), F32),
                   jax.ShapeDtypeStruct((8, D), F32), jax.ShapeDtypeStruct((8, 128), F32)],
        in_specs=[tile, tile, zcol(1), zcol(2), zcol(3), zcol(4), tile, pl.BlockSpec((1, D), lambda i: (0, 2)),
                  vec, vec, vec,
                  full(HEADS, CHUNK, CHUNK), full(HEADS, CHUNK, CHUNK), full(CHUNK, HEADS),
                  pl.BlockSpec((D_MIX, D), lambda i: (0, 0), pipeline_mode=pl.Buffered(1))],
        out_specs=[pl.BlockSpec((tm, D_IN), lambda i: (i, 0)), tile, tile,
                   pl.BlockSpec((tm, D_MIX), lambda i: (i, 0)), tile,
                   full(HEADS, CHUNK, CHUNK), full(CHUNK, HEADS), full(8, D), full(8, 128)],
        scratch_shapes=[pltpu.VMEM((tm, D), BF16), pltpu.VMEM((tm, D), F32),
                        pltpu.VMEM((tm, D), BF16), pltpu.VMEM((tm, D), F32)],
        compiler_params=_params("arbitrary"),
    )(x, tgt, zx, zx, zx, zx, yl, gx, fg, lng, lnb, ws, wst, bst, wout)


def _grad_w(a, b, a2, b2, tk, name):
    nk = a.shape[0] // tk
    m, ncols = a.shape[1], b.shape[1]
    with_ctx = a2 is not None

    def body(*refs):
        if with_ctx:
            a_ref, b_ref, a2_ref, b2_ref, o_ref, acc = refs
        else:
            a_ref, b_ref, o_ref, acc = refs
        n, k = pl.program_id(0), pl.program_id(1)

        @pl.when(k == 0)
        def _():
            acc[...] = jnp.zeros_like(acc)

        acc[...] += _dot_tn(a_ref[...], b_ref[...])

        if with_ctx:
            @pl.when(jnp.logical_and(k == nk - 1, n == 0))
            def _():
                acc[...] += _dot_tn(a2_ref[...], b2_ref[...])

        @pl.when(k == nk - 1)
        def _():
            o_ref[...] = acc[...].astype(BF16)

    in_specs = [pl.BlockSpec((tk, m), lambda n, k: (k, 0)), pl.BlockSpec((tk, D), lambda n, k: (k, n))]
    args = [a, b]
    if with_ctx:
        in_specs += [pl.BlockSpec(a2.shape, lambda n, k: (0, 0)), pl.BlockSpec(b2.shape, lambda n, k: (0, 0))]
        args += [a2, b2]
    return _call(
        body, name=name, grid=(ncols // D, nk), out_shape=jax.ShapeDtypeStruct((m, ncols), BF16),
        in_specs=in_specs, out_specs=pl.BlockSpec((m, D), lambda n, k: (0, n)),
        scratch_shapes=[pltpu.VMEM((m, D), F32)],
        compiler_params=_params("arbitrary", "arbitrary"),
    )(*args)


def _grad_rows(xr, dz, w, mod, ng, dres, ncols, tm, name):
    rows = xr.shape[0]
    with_dx = dres is not None

    def body(*refs):
        if with_dx:
            x_ref, dz_ref, w_ref, sc_ref, ng_ref, dres_ref, dx_ref, vec_ref = refs
        else:
            x_ref, dz_ref, w_ref, sc_ref, ng_ref, vec_ref = refs

        @pl.when(pl.program_id(0) == 0)
        def _():
            vec_ref[...] = jnp.zeros_like(vec_ref)

        dhn = _dot_nt(dz_ref[...], w_ref[...])
        x = x_ref[...]
        rs = lax.rsqrt(jnp.mean(x * x, axis=-1, keepdims=True) + NORM_EPS)
        xh = x * rs
        ngv = ng_ref[...]
        y = xh * ngv
        vec_ref[0:1, :] += jnp.sum(dhn, axis=0, keepdims=True)
        vec_ref[1:2, :] += jnp.sum(dhn * y, axis=0, keepdims=True)
        dy = dhn * (1.0 + sc_ref[...])
        vec_ref[2:3, :] += jnp.sum(dy * xh, axis=0, keepdims=True)
        if with_dx:
            dxh = dy * ngv
            dx_ref[...] = dres_ref[...] + rs * (dxh - xh * jnp.mean(dxh * xh, axis=-1, keepdims=True))

    tile = pl.BlockSpec((tm, D), lambda i: (i, 0))
    vec = pl.BlockSpec((1, D), lambda i: (0, 0))
    in_specs = [tile, pl.BlockSpec((tm, ncols), lambda i: (i, 0)),
                pl.BlockSpec((D, ncols), lambda i: (0, 0), pipeline_mode=pl.Buffered(1)),
                pl.BlockSpec((1, D), lambda i: (0, 1)), vec]
    out_shape = [jax.ShapeDtypeStruct((8, D), F32)]
    out_specs = [pl.BlockSpec((8, D), lambda i: (0, 0))]
    args = [xr, dz, w, mod, ng]
    if with_dx:
        in_specs.append(tile)
        out_shape.insert(0, jax.ShapeDtypeStruct((rows, D), F32))
        out_specs.insert(0, tile)
        args.append(dres)
    return _call(body, name=name, grid=(rows // tm,), out_shape=out_shape, in_specs=in_specs, out_specs=out_specs,
                 compiler_params=_params("arbitrary"))(*args)


def _adamw(w, g, m, v):
    m = ADAM_B1 * m + (1.0 - ADAM_B1) * g
    v = ADAM_B2 * v + (1.0 - ADAM_B2) * (g * g)
    m_hat = m / (1.0 - ADAM_B1 ** ADAM_STEP)
    v_hat = v / (1.0 - ADAM_B2 ** ADAM_STEP)
    delta = -ADAM_LR * (m_hat / (jnp.sqrt(v_hat) + ADAM_EPS) + ADAM_WD * w)
    return delta, m, v


def _adamw_reduced(parts, w, m, v, tr, name):
    r, n = w.shape
    nparts = parts.shape[0]

    def body(p_ref, w_ref, m_ref, v_ref, g_ref, d_ref, mo_ref, vo_ref):
        g = p_ref[0].astype(F32)
        for i in range(1, nparts):
            g = g + p_ref[i].astype(F32)
        g_ref[...] = g
        d_ref[...], mo_ref[...], vo_ref[...] = _adamw(w_ref[...], g, m_ref[...], v_ref[...])

    tile = pl.BlockSpec((tr, n), lambda i: (i, 0))
    sds = jax.ShapeDtypeStruct((r, n), F32)
    return _call(
        body, name=name, grid=(r // tr,), out_shape=[sds] * 4,
        in_specs=[pl.BlockSpec((nparts, tr, n), lambda i: (0, i, 0)), tile, tile, tile], out_specs=[tile] * 4,
        compiler_params=_params("arbitrary"),
    )(parts, w, m, v)


R_GATE, R_FINAL_G, R_LN_G, R_LN_B = 0, 1, 2, 3
R_SH_X, R_SC_X, R_NG_X = 4, 5, 6
R_SH_C, R_SC_C, R_NG_C = 7, 8, 9
R_BA, R_BX, R_LAM, R_CW, R_CB, R_SGU_B = 10, 12, 14, 16, 20, 21
PACK_ROWS = 32
MAT_ROWS = 2 * (2 * HEADS * HD) + HEADS * CHUNK


def _reduce_small(vp_all, mat_parts, ada_w, me):
    nloc = ada_w.shape[1]

    def body(me_ref, vp_ref, mp_ref, w_ref, red_ref, mat_ref, dmod_ref, gab_ref, cpart_ref, dmc_s):
        red = vp_ref[0]
        for i in range(1, N_DEV):
            red = red + vp_ref[i]
        mat = mp_ref[0]
        for i in range(1, mp_ref.shape[0]):
            mat = mat + mp_ref[i]
        red_ref[...] = red
        mat_ref[...] = mat
        for e in range(N_DEV):
            dmod_ref[e:e + 1, 0:D] = vp_ref[e, R_SH_X:R_SH_X + 1, :]
            dmod_ref[e:e + 1, D:2 * D] = vp_ref[e, R_SC_X:R_SC_X + 1, :]
            dmod_ref[e:e + 1, 2 * D:3 * D] = vp_ref[e, R_GATE:R_GATE + 1, :]
        dmod_ref[8:9, 0:D] = red[R_SH_C:R_SH_C + 1, :]
        dmod_ref[8:9, D:2 * D] = red[R_SC_C:R_SC_C + 1, :]
        dmod_ref[8:9, 2 * D:3 * D] = jnp.zeros((1, D), F32)
        dmod_ref[9:16, :] = jnp.zeros((7, 3 * D), F32)
        gab_ref[:, 0:D] = red[R_SH_X:R_SH_X + 1, :] + red[R_SH_C:R_SH_C + 1, :]
        gab_ref[:, D:2 * D] = red[R_SC_X:R_SC_X + 1, :] + red[R_SC_C:R_SC_C + 1, :]
        gab_ref[:, 2 * D:3 * D] = red[R_GATE:R_GATE + 1, :]
        dmc_s[...] = jnp.broadcast_to(dmod_ref[8:9, :], (8, 3 * D))
        off = pl.multiple_of(me_ref[0] * nloc, 128)
        cpart_ref[...] = _dot_nt(dmc_s[:, pl.ds(off, nloc)], w_ref[...])

    return _call(
        body, name="reduce_small",
        out_shape=[jax.ShapeDtypeStruct((PACK_ROWS, D), F32), jax.ShapeDtypeStruct(mat_parts.shape[1:], F32),
                   jax.ShapeDtypeStruct((16, 3 * D), F32), jax.ShapeDtypeStruct((1, 3 * D), F32),
                   jax.ShapeDtypeStruct((8, D), F32)],
        in_specs=[pl.BlockSpec(memory_space=pltpu.SMEM), VMEM, VMEM, VMEM], out_specs=[VMEM] * 5,
        scratch_shapes=[pltpu.VMEM((8, 3 * D), F32)], compiler_params=_params(),
    )(me, vp_all, mat_parts, ada_w)


def _adamw_ada(c_all, c_ctx, dmod, w, m, v, me):
    nloc = w.shape[1]

    def body(me_ref, c_ref, cc_ref, dm_ref, w_ref, m_ref, v_ref, g_ref, d_ref, mo_ref, vo_ref):
        off = pl.multiple_of(me_ref[0] * nloc, 128)
        dm = dm_ref[:, pl.ds(off, nloc)]
        sx, _ = _silu_and_grad(c_ref[...])
        sc, _ = _silu_and_grad(cc_ref[...])
        g = _dot_tn(sx, dm[0:8, :]) + _dot_tn(jnp.broadcast_to(sc, (8, D)), dm[8:16, :])
        g_ref[...] = g
        d_ref[...], mo_ref[...], vo_ref[...] = _adamw(w_ref[...], g, m_ref[...], v_ref[...])

    sds = jax.ShapeDtypeStruct(w.shape, F32)
    return _call(
        body, name="adamw_ada_w", out_shape=[sds] * 4,
        in_specs=[pl.BlockSpec(memory_space=pltpu.SMEM)] + [VMEM] * 6, out_specs=[VMEM] * 4,
        compiler_params=_params(),
    )(me, c_all, c_ctx, dmod, w, m, v)


_SMALL = ("c_ctx", "ada_b", "norm_g", "conv_w", "conv_b", "lru_wa", "lru_ba", "lru_wx", "lru_bx", "lru_lambda",
          "sgu_ln_g", "sgu_ln_b", "sgu_w", "sgu_b", "final_g")


def _adamw_small(red, mat, cparts, gab, ws, ms, vs, me):
    n = len(_SMALL)
    nw = 2 * HEADS * HD

    def body(me_ref, red_ref, mat_ref, cp_ref, gab_ref, *refs):
        w_refs, m_refs, v_refs = refs[:n], refs[n:2 * n], refs[2 * n:3 * n]
        outs = refs[3 * n:]
        off = pl.multiple_of(me_ref[0] * HD, 128)

        def row(r, k=1):
            return red_ref[r:r + k, :]

        cc = w_refs[0][...]
        dcc = cp_ref[0, 0:1, :]
        for i in range(1, N_DEV):
            dcc = dcc + cp_ref[i, 0:1, :]
        grads = dict(
            c_ctx=dcc * _silu_and_grad(cc)[1], ada_b=gab_ref[...], norm_g=row(R_NG_X) + row(R_NG_C),
            conv_w=red_ref[R_CW:R_CW + CONV_W, pl.ds(off, HD)], conv_b=row(R_CB),
            lru_wa=mat_ref[0:nw, :], lru_ba=row(R_BA, 2), lru_wx=mat_ref[nw:2 * nw, :], lru_bx=row(R_BX, 2),
            lru_lambda=red_ref[R_LAM:R_LAM + 2, pl.ds(off, HD)], sgu_ln_g=row(R_LN_G), sgu_ln_b=row(R_LN_B),
            sgu_w=mat_ref[2 * nw:MAT_ROWS, :], sgu_b=row(R_SGU_B), final_g=row(R_FINAL_G))
        for j, name in enumerate(_SMALL):
            g = grads[name]
            outs[j][...] = g
            outs[n + j][...], outs[2 * n + j][...], outs[3 * n + j][...] = _adamw(w_refs[j][...], g, m_refs[j][...],
                                                                                 v_refs[j][...])

    sds = [jax.ShapeDtypeStruct(ws[k].shape, F32) for k in _SMALL]
    outs = _call(
        body, name="adamw_small", out_shape=sds * 4,
        in_specs=[pl.BlockSpec(memory_space=pltpu.SMEM)] + [VMEM] * (4 + 3 * n), out_specs=[VMEM] * (4 * n),
        compiler_params=_params(),
    )(me, red, mat, cparts, gab, *[ws[k] for k in _SMALL], *[ms[k] for k in _SMALL], *[vs[k] for k in _SMALL])
    return [dict(zip(_SMALL, outs[i * n:(i + 1) * n])) for i in range(4)]


def kernel(x, c, ctx, c_ctx, ada_w, ada_b, norm_g, w_in, conv_w, conv_b, lru_wa, lru_ba, lru_wx, lru_bx, lru_lambda, sgu_ln_g, sgu_ln_b, sgu_w, sgu_b, w_out, final_g, loss_target, m_c_ctx, m_ada_w, m_ada_b, m_norm_g, m_w_in, m_conv_w, m_conv_b, m_lru_wa, m_lru_ba, m_lru_wx, m_lru_bx, m_lru_lambda, m_sgu_ln_g, m_sgu_ln_b, m_sgu_w, m_sgu_b, m_w_out, m_final_g, v_c_ctx, v_ada_w, v_ada_b, v_norm_g, v_w_in, v_conv_w, v_conv_b, v_lru_wa, v_lru_ba, v_lru_wx, v_lru_bx, v_lru_lambda, v_sgu_ln_g, v_sgu_ln_b, v_sgu_w, v_sgu_b, v_w_out, v_final_g):
    args = dict(locals())
    me_s = 4 * lax.axis_index("x") + 2 * lax.axis_index("y") + lax.axis_index("c")
    me = me_s.astype(jnp.int32).reshape(1)
    xr, ctxr, tgt = x[0], ctx[0], loss_target[0]
    cc = c_ctx.reshape(1, D)
    nw = 2 * HEADS * HD
    view = dict(c_ctx=(1, D), ada_b=(1, 3 * D), norm_g=(1, D), conv_w=(CONV_W, HD), conv_b=(1, D), lru_wa=(nw, HD),
                lru_ba=(2, D), lru_wx=(nw, HD), lru_bx=(2, D), lru_lambda=(2, HD), sgu_ln_g=(1, D), sgu_ln_b=(1, D),
                sgu_w=(HEADS * CHUNK, CHUNK), sgu_b=(1, D), final_g=(1, D))

    (c_all,) = _gather2([c], ["ag"], "gather_c")
    c_all = c_all.reshape(N_DEV, D)
    part = _ada_forward(c_all, cc, ada_w[0], ada_b, me)
    (parts,) = _gather2([part], ["ag"], "gather_mod")
    modx = lax.dynamic_index_in_dim(parts, me_s, axis=1, keepdims=False).reshape(1, 3 * D)
    modc = parts[:, 8, :].reshape(1, 3 * D)

    w_in_b, w_out_b = _cast_weights(w_in[0], w_out[0])
    w_full, wout_all, cw_full, lam_full = _gather2(
        [w_in_b, w_out_b, conv_w[0], lru_lambda[0]], ["agc", "ag", "agc", "agc"], "gather_weights")
    wout_full = wout_all.reshape(D_MIX, D)

    zx, hn = _project(xr, modx, norm_g, w_full, D_IN, 256, "project_x")
    zc, hnc = _project(ctxr, modc, norm_g, w_full, D, LC, "project_ctx")
    ba, bx = lru_ba.reshape(2, D), lru_bx.reshape(2, D)
    yl = _lru_forward(zx, zc, cw_full, conv_b, lru_wa[0], lru_wx[0], ba, bx, lam_full)
    ws_b = sgu_w[0].astype(BF16)
    dz, dyl, dxn, ycat, dob, dws, dbst, mvec, lossv = _mixer_loss(
        xr, tgt, zx, yl, modx, final_g.reshape(1, D), sgu_ln_g, sgu_ln_b, ws_b, jnp.swapaxes(ws_b, 1, 2),
        sgu_b[0].T, wout_full, 128)

    dz, dxac, dwa, dwx, dba, dbx, dlam, dcw, dcb = _lru_backward(
        zx, zc, dyl, dz, cw_full, conv_b, lru_wa[0], lru_wx[0], ba, bx, lam_full)
    gw_in = _grad_w(hn, dz, hnc, dxac, 512, "grad_w_in")
    gw_out = _grad_w(ycat, dob, None, None, 512, "grad_w_out")
    gx, xvec = _grad_rows(xr, dz, w_full, modx, norm_g, dxn, D_IN, 256, "grad_rows_x")
    (cvec,) = _grad_rows(ctxr, dxac, w_full, modc, norm_g, None, D, LC, "grad_rows_ctx")

    pack = jnp.concatenate([mvec[0:4], xvec[0:3], cvec[0:3], dba, dbx, dlam, dcw, dcb, dbst.T.reshape(1, D),
                            jnp.zeros((PACK_ROWS - R_SGU_B - 1, D), F32)], axis=0)
    matpack = jnp.concatenate([dwa.reshape(nw, HD), dwx.reshape(nw, HD), dws.reshape(HEADS * CHUNK, CHUNK)], axis=0)
    win_parts, wout_parts, mat_parts = _reduce2(
        [gw_in, gw_out.reshape(N_DEV, D_MIX // N_DEV, D), matpack.reshape(N_DEV, MAT_ROWS // N_DEV, HD)],
        ["a2ac", "a2a", "a2a"], me, "reduce_grads")
    (vp_all,) = _gather2([pack], ["ag"], "gather_pack")
    red, matpiece, dmod, gab, cpart = _reduce_small(vp_all, mat_parts, ada_w[0], me)
    mat_all, cparts = _gather2([matpiece, cpart], ["ag", "ag"], "gather_small")

    g_w_in, d_w_in, nm_w_in, nv_w_in = _adamw_reduced(win_parts, w_in[0], m_w_in[0], v_w_in[0], 256, "adamw_w_in")
    g_w_out, d_w_out, nm_w_out, nv_w_out = _adamw_reduced(wout_parts, w_out[0], m_w_out[0], v_w_out[0], 128,
                                                          "adamw_w_out")
    g_ada, d_ada, nm_ada, nv_ada = _adamw_ada(c_all, cc, dmod, ada_w[0], m_ada_w[0], v_ada_w[0], me)
    ws = {k: args[k].reshape(view[k]) for k in _SMALL}
    ms = {k: args["m_" + k].reshape(view[k]) for k in _SMALL}
    vs = {k: args["v_" + k].reshape(view[k]) for k in _SMALL}
    small = _adamw_small(red, mat_all.reshape(MAT_ROWS, HD), cparts, gab, ws, ms, vs, me)
    big = dict(w_in=(g_w_in, d_w_in, nm_w_in, nv_w_in), w_out=(g_w_out, d_w_out, nm_w_out, nv_w_out),
               ada_w=(g_ada, d_ada, nm_ada, nv_ada))

    loss = lax.psum(lossv[0, 0], ("x", "y", "c"))
    names = ("c_ctx", "ada_w", "ada_b", "norm_g", "w_in", "conv_w", "conv_b", "lru_wa", "lru_ba", "lru_wx", "lru_bx",
             "lru_lambda", "sgu_ln_g", "sgu_ln_b", "sgu_w", "sgu_b", "w_out", "final_g")
    outs = [loss, gx.reshape(x.shape)]
    for kind in range(4):
        for k in names:
            val = big[k][kind] if k in big else small[kind][k]
            outs.append(val.reshape(args[k].shape))
    return tuple(outs)
```

```python
import functools

import jax
import jax.numpy as jnp
from jax import lax
from jax.experimental import pallas as pl
from jax.experimental.pallas import tpu as pltpu

F32 = jnp.float32
BF16 = jnp.bfloat16

N_DEV = 8
D = 1024
L = 2048
LC = 256
HEADS = 8
HD = 128
CHUNK = 128
D_IN = 5 * D
D_MIX = 2 * D
CONV_W = 4
LRU_C = 8.0
NORM_EPS = 1e-6
LN_EPS = 1e-5
ADAM_LR, ADAM_B1, ADAM_B2, ADAM_EPS, ADAM_WD, ADAM_STEP = 0.001, 0.9, 0.999, 1e-08, 0.01, 10

VMEM_LIMIT = 56 * 1024 * 1024

HBM = pl.BlockSpec(memory_space=pltpu.HBM)
VMEM = pl.BlockSpec(memory_space=pltpu.VMEM)
MESH = pl.DeviceIdType.MESH


def _call(body, **kw):
    return pl.pallas_call(body, **kw)


def _params(*sem):
    return pltpu.CompilerParams(dimension_semantics=sem, vmem_limit_bytes=VMEM_LIMIT)


def _sigmoid(x):
    return 0.5 * jnp.tanh(0.5 * x) + 0.5


def _silu_and_grad(x):
    s = _sigmoid(x)
    return x * s, s * (1.0 + x * (1.0 - s))


_G0 = 0.7978845608028654
_G1 = 0.044715


def _gelu_and_grad(x):
    x2 = x * x
    t = jnp.tanh(_G0 * (x + _G1 * x * x2))
    cdf = 0.5 * (1.0 + t)
    return x * cdf, cdf + 0.5 * x * (1.0 - t * t) * (_G0 * (1.0 + 3.0 * _G1 * x2))


def _gelu(x):
    return 0.5 * x * (1.0 + jnp.tanh(_G0 * (x + _G1 * x * x * x)))


def _softplus(z):
    t = jnp.exp(-jnp.abs(z))
    u = 1.0 + t
    log1p = jnp.where(u == 1.0, t, jnp.log(u) * t / jnp.where(u == 1.0, 1.0, u - 1.0))
    return jnp.maximum(z, 0.0) + log1p


def _dot(a, b):
    return jnp.dot(a, b, preferred_element_type=F32)


def _dot_nt(a, b):
    return lax.dot_general(a, b, (((1,), (1,)), ((), ())), preferred_element_type=F32)


def _dot_tn(a, b):
    return lax.dot_general(a, b, (((0,), (0,)), ((), ())), preferred_element_type=F32)


def _rows(shape):
    return lax.broadcasted_iota(jnp.int32, shape, 0)


def _shift_down(x, first):
    return jnp.where(_rows(x.shape) == 0, first, pltpu.roll(x, 1, 0))


def _shift_up(x, last):
    n = x.shape[0]
    return jnp.where(_rows(x.shape) == n - 1, last, pltpu.roll(x, n - 1, 0))


def _gather2(arrays, modes, name):
    n = len(arrays)

    def body(*refs):
        start, forward, finish = _gather2_ops(refs[:n], refs[n:2 * n], modes, *refs[2 * n:])
        start()
        forward()
        finish()

    return _call(
        body, name=name, out_shape=_gather2_shapes(arrays, modes), in_specs=[HBM] * n, out_specs=[HBM] * n,
        scratch_shapes=_gather2_sems(n), compiler_params=pltpu.CompilerParams(has_side_effects=True),
    )(*[pltpu.with_memory_space_constraint(a, pltpu.HBM) for a in arrays])


def _gather2_shapes(arrays, modes):
    return [jax.ShapeDtypeStruct((N_DEV,) + a.shape if m == "ag" else (a.shape[0], N_DEV * a.shape[1]), a.dtype)
            for a, m in zip(arrays, modes)]


def _gather2_sems(n):
    return [pltpu.SemaphoreType.DMA((n, N_DEV - 1)), pltpu.SemaphoreType.DMA((n, N_DEV - 1)),
            pltpu.SemaphoreType.DMA((n,))]


def _gather2_ops(ins, outs, modes, send_sems, recv_sems, local_sems):
    n = len(ins)
    x, y, c = lax.axis_index("x"), lax.axis_index("y"), lax.axis_index("c")
    me, sibling = (x, y, c), (x, y, 1 - c)
    chips = [(x ^ (k >> 1), y ^ (k & 1)) for k in (1, 2, 3)]

    def slot(j, px, py, pc):
        dev = 4 * px + 2 * py + pc
        if modes[j] == "agc":
            w = ins[j].shape[1]
            return outs[j].at[:, pl.ds(pl.multiple_of(dev * w, 128), w)]
        return outs[j].at[dev]

    def copy(j, k, block, to, src=None):
        return pltpu.make_async_remote_copy(
            src_ref=slot(j, *block) if src is None else src, dst_ref=slot(j, *block),
            send_sem=send_sems.at[j, k], recv_sem=recv_sems.at[j, k], device_id=to, device_id_type=MESH)

    def own(j):
        return pltpu.make_async_copy(ins[j], slot(j, *me), local_sems.at[j])

    def first(j):
        return [copy(j, 0, me, sibling, src=ins[j])] + [copy(j, 1 + i, me, (*chip, c), src=ins[j])
                                                        for i, chip in enumerate(chips)]

    def passed(j, i):
        return copy(j, 4 + i, (*chips[i], c), sibling)

    def start():
        for j in range(n):
            own(j).start()
            for cp in first(j):
                cp.start()

    def forward():
        for i, chip in enumerate(chips):
            for j in range(n):
                copy(j, 1 + i, (*chip, c), me).wait_recv()
                passed(j, i).start()

    def finish():
        for j in range(n):
            copy(j, 0, sibling, me).wait_recv()
            for i, chip in enumerate(chips):
                copy(j, 4 + i, (*chip, 1 - c), me).wait_recv()
            for cp in first(j) + [passed(j, i) for i in range(3)]:
                cp.wait_send()
            own(j).wait()

    return start, forward, finish


def _reduce2_local(arrays, modes, me, name):
    n = len(arrays)
    shapes = [(a.shape[1], a.shape[2]) if m == "a2a" else (a.shape[0], a.shape[1] // N_DEV)
              for a, m in zip(arrays, modes)]
    staged = [jax.ShapeDtypeStruct((4,) + s, a.dtype) for s, a in zip(shapes, arrays)]

    def piece(ref, mode, dev, w):
        return ref.at[dev] if mode == "a2a" else ref.at[:, pl.ds(pl.multiple_of(dev * w, 128), w)]

    def to_sibling(*refs):
        ins, outs = refs[:n], refs[n:2 * n]
        send_sems, recv_sems = refs[2 * n:]
        x, y, c = lax.axis_index("x"), lax.axis_index("y"), lax.axis_index("c")
        copies = []
        for j in range(n):
            for q in range(4):
                cp = pltpu.make_async_remote_copy(
                    src_ref=piece(ins[j], modes[j], 2 * q + (1 - c), shapes[j][1]), dst_ref=outs[j].at[q],
                    send_sem=send_sems.at[j, q], recv_sem=recv_sems.at[j, q], device_id=(x, y, 1 - c),
                    device_id_type=MESH)
                cp.start()
                copies.append(cp)
        for cp in copies:
            cp.wait()

    stage = _call(
        to_sibling, name=name + "_d2d", out_shape=staged, in_specs=[HBM] * n, out_specs=[HBM] * n,
        scratch_shapes=[pltpu.SemaphoreType.DMA((n, 4)), pltpu.SemaphoreType.DMA((n, 4))],
        compiler_params=pltpu.CompilerParams(has_side_effects=True),
    )(*[pltpu.with_memory_space_constraint(a, pltpu.HBM) for a in arrays])

    def add(me_ref, *refs):
        del me_ref
        own, got, outs = refs[:n], refs[n:2 * n], refs[2 * n:]
        for j in range(n):
            mine = own[j][0] if modes[j] == "a2a" else own[j][...]
            outs[j][0] = (mine.astype(F32) + got[j][0].astype(F32)).astype(outs[j].dtype)

    in_specs = []
    for (r, w), m in zip(shapes, modes):
        if m == "a2a":
            in_specs.append(pl.BlockSpec((1, r, w), lambda q, me_ref: (2 * q + me_ref[0] % 2, 0, 0)))
        else:
            in_specs.append(pl.BlockSpec((r, w), lambda q, me_ref: (0, 2 * q + me_ref[0] % 2)))
    slot_specs = [pl.BlockSpec((1, r, w), lambda q, me_ref: (q, 0, 0)) for r, w in shapes]
    sums = _call(
        add, name=name + "_add", out_shape=staged,
        grid_spec=pltpu.PrefetchScalarGridSpec(num_scalar_prefetch=1, grid=(4,), in_specs=in_specs + slot_specs,
                                               out_specs=slot_specs),
        compiler_params=_params("arbitrary"),
    )(me, *arrays, *stage)

    return sums


def _chips_sems(n):
    return [pltpu.SemaphoreType.DMA((n, 3)), pltpu.SemaphoreType.DMA((n, 3)), pltpu.SemaphoreType.DMA((n,))]


def _chips_ops(ins, outs, send_sems, recv_sems, local_sems):
    x, y, c = lax.axis_index("x"), lax.axis_index("y"), lax.axis_index("c")
    qm = 2 * x + y

    def copies():
        out = []
        for j in range(len(ins)):
            for k in (1, 2, 3):
                px, py = x ^ (k >> 1), y ^ (k & 1)
                out.append(pltpu.make_async_remote_copy(
                    src_ref=ins[j].at[2 * px + py], dst_ref=outs[j].at[qm], send_sem=send_sems.at[j, k - 1],
                    recv_sem=recv_sems.at[j, k - 1], device_id=(px, py, c), device_id_type=MESH))
            out.append(pltpu.make_async_copy(ins[j].at[qm], outs[j].at[qm], local_sems.at[j]))
        return out

    def start():
        for cp in copies():
            cp.start()

    def finish():
        for cp in copies():
            cp.wait()

    return start, finish


def _ada_forward(c_all, c_ctx, ada_w, ada_b, me):
    nloc = ada_w.shape[1]

    def body(me_ref, c_ref, cc_ref, w_ref, b_ref, o_ref):
        off = pl.multiple_of(me_ref[0] * nloc, 128)
        b = b_ref[:, pl.ds(off, nloc)]
        w = w_ref[...]
        sx, _ = _silu_and_grad(c_ref[...])
        sc, _ = _silu_and_grad(jnp.broadcast_to(cc_ref[...], (8, D)))
        o_ref[0:8, :] = _dot(sx, w) + b
        o_ref[8:16, :] = _dot(sc, w) + b

    return _call(
        body, name="ada_forward", out_shape=jax.ShapeDtypeStruct((16, nloc), F32),
        in_specs=[pl.BlockSpec(memory_space=pltpu.SMEM), VMEM, VMEM, VMEM, VMEM], out_specs=VMEM,
    )(me, c_all, c_ctx, ada_w, ada_b)


def _cast_weights(w_in, w_out):
    def body(a_ref, b_ref, ao_ref, bo_ref):
        ao_ref[...] = a_ref[...].astype(BF16)
        bo_ref[...] = b_ref[...].astype(BF16)

    return _call(
        body, name="cast_weights",
        out_shape=[jax.ShapeDtypeStruct(w_in.shape, BF16), jax.ShapeDtypeStruct(w_out.shape, BF16)],
        in_specs=[VMEM, VMEM], out_specs=[VMEM, VMEM], compiler_params=_params(),
    )(w_in, w_out)


def _project(xr, mod, ng, w, ncols, tm, name, gather=None, gather_modes=()):
    rows = xr.shape[0]
    steps = rows // tm
    ng_ = len(gather or ())

    def body(x_ref, sh_ref, sc_ref, ng_ref, w_ref, *rest):
        z_ref, hn_ref = rest[ng_:ng_ + 2]
        if ng_:
            start, forward, finish = _gather2_ops(rest[:ng_], rest[ng_ + 2:2 * ng_ + 2], gather_modes,
                                                  *rest[2 * ng_ + 2:])
            pl.when(pl.program_id(0) == 0)(start)
            pl.when(pl.program_id(0) == steps // 2)(forward)
        x = x_ref[...]
        rs = lax.rsqrt(jnp.mean(x * x, axis=-1, keepdims=True) + NORM_EPS)
        hn = (x * rs * ng_ref[...]) * (1.0 + sc_ref[...]) + sh_ref[...]
        hb = hn.astype(BF16)
        hn_ref[...] = hb
        for n in range(ncols // D):
            z_ref[:, n * D:(n + 1) * D] = _dot(hb, w_ref[:, n * D:(n + 1) * D])
        if ng_:
            pl.when(pl.program_id(0) == steps - 1)(finish)

    vec = pl.BlockSpec((1, D), lambda i: (0, 0))
    gathered = _gather2_shapes(gather, gather_modes) if ng_ else []
    return _call(
        body, name=name, grid=(steps,),
        out_shape=[jax.ShapeDtypeStruct((rows, ncols), F32), jax.ShapeDtypeStruct((rows, D), BF16)] + gathered,
        in_specs=[pl.BlockSpec((tm, D), lambda i: (i, 0)), vec, pl.BlockSpec((1, D), lambda i: (0, 1)), vec,
                  pl.BlockSpec((D, ncols), lambda i: (0, 0), pipeline_mode=pl.Buffered(1))] + [HBM] * ng_,
        out_specs=[pl.BlockSpec((tm, ncols), lambda i: (i, 0)), pl.BlockSpec((tm, D), lambda i: (i, 0))] + [HBM] * ng_,
        scratch_shapes=_gather2_sems(ng_) if ng_ else [],
        compiler_params=pltpu.CompilerParams(dimension_semantics=("arbitrary",), vmem_limit_bytes=VMEM_LIMIT,
                                             has_side_effects=bool(ng_)),
    )(xr, mod, mod, ng, w, *[pltpu.with_memory_space_constraint(a, pltpu.HBM) for a in gather or ()])


def _scan_pair(af_ref, uf_ref, hf_ref, h0f, ab_ref, ub_ref, hb_ref, h0b, t_len):
    span = 8 * SCAN_BLOCKS
    nit = t_len // span
    rows = _rows((8, HD))

    def local_scan(a, b, forward):
        for s in (1, 2, 4):
            sh = s if forward else 8 - s
            m = rows >= s if forward else rows < 8 - s
            b = a * jnp.where(m, pltpu.roll(b, sh, 0), 0.0) + b
            a = a * jnp.where(m, pltpu.roll(a, sh, 0), 1.0)
        return a, b

    def span_scan(a_ref, u_ref, h_ref, off, carry, forward):
        order = range(SCAN_BLOCKS) if forward else range(SCAN_BLOCKS - 1, -1, -1)
        last = slice(7, 8) if forward else slice(0, 1)
        for q in order:
            rs = pl.ds(off + 8 * q, 8)
            a, b = local_scan(a_ref[rs, :], u_ref[rs, :], forward)
            h_ref[rs, :] = b + a * carry
            carry = a[last, :] * carry + b[last, :]
        return carry

    def body(k, carry):
        cf, cb = carry
        cf = span_scan(af_ref, uf_ref, hf_ref, pl.multiple_of(k * span, span), cf, True)
        cb = span_scan(ab_ref, ub_ref, hb_ref, pl.multiple_of((nit - 1 - k) * span, span), cb, False)
        return cf, cb

    return lax.fori_loop(0, nit, body, (h0f, h0b))


SCAN_BLOCKS = 4


def _conv(xa, cw, cb):
    z = jnp.zeros((1, HD), F32)
    xm1 = _shift_down(xa, z)
    xp1 = _shift_up(xa, z)
    xp2 = _shift_up(xp1, z)
    return xm1 * cw[0:1, :] + xa * cw[1:2, :] + xp1 * cw[2:3, :] + xp2 * cw[3:4, :] + cb


def _gates(xc, wa, wx, ba, bx, nsp):
    xb = xc.astype(BF16)
    r = _sigmoid(_dot(xb, wa) + ba)
    i = _sigmoid(_dot(xb, wx) + bx)
    log_a = r * nsp
    a = jnp.exp(log_a)
    g2 = jnp.tanh(log_a) * (-1.0 - a * a)
    rg = lax.rsqrt(jnp.maximum(g2, 1e-30))
    return r, i, a, g2 * rg, rg


def _lru_param_specs():
    h4 = pl.BlockSpec((2, 1, HD, HD), lambda h: (0, h, 0, 0))
    v2 = pl.BlockSpec((2, HD), lambda h: (0, h))
    return dict(
        xa=pl.BlockSpec((L, HD), lambda h: (0, h)), xac=pl.BlockSpec((LC, HD), lambda h: (0, h)),
        cw=pl.BlockSpec((CONV_W, HD), lambda h: (0, h)), cb=pl.BlockSpec((1, HD), lambda h: (0, h)), h4=h4, v2=v2)


def _lru_forward(zx, zc, cw, cb, wa, wx, ba, bx, lam):
    def body(xa_ref, xac_ref, cw_ref, cb_ref, wa_ref, wx_ref, ba_ref, bx_ref, lam_ref, yl_ref,
             af, uf, hf, ab, ub, hb):
        cwv, cbv = cw_ref[...], cb_ref[...]
        nsp = (-LRU_C) * _softplus(-lam_ref[...])

        def forward(xa, t_len, h0f, h0b):
            xc = _conv(xa, cwv, cbv)
            for d, (a_ref, u_ref) in enumerate(((af, uf), (ab, ub))):
                _, i, a, gamma, _ = _gates(xc, wa_ref[d, 0].astype(BF16), wx_ref[d, 0].astype(BF16),
                                           ba_ref[d:d + 1, :], bx_ref[d:d + 1, :], nsp[d:d + 1, :])
                a_ref[0:t_len, :] = a
                u_ref[0:t_len, :] = gamma * (i * xc)
            return _scan_pair(af, uf, hf, h0f, ab, ub, hb, h0b, t_len)

        z = jnp.zeros((1, HD), F32)
        h0f, h0b = forward(xac_ref[...], LC, z, z)
        forward(xa_ref[...], L, h0f, h0b)
        yl_ref[...] = hf[...] + hb[...]

    s = _lru_param_specs()
    return _call(
        body, name="lru_forward", grid=(HEADS,), out_shape=jax.ShapeDtypeStruct((L, D), F32),
        in_specs=[s["xa"], s["xac"], s["cw"], s["cb"], s["h4"], s["h4"], s["v2"], s["v2"], s["v2"]],
        out_specs=pl.BlockSpec((L, HD), lambda h: (0, h)),
        scratch_shapes=[pltpu.VMEM((L, HD), F32)] * 6,
        compiler_params=_params("arbitrary"),
    )(zx, zc, cw, cb, wa, wx, ba, bx, lam)


def _lru_backward(zx, zc, dyl, dz, cw, cb, wa, wx, ba, bx, lam, chip_sums):
    nr = len(chip_sums)

    def body(xa_ref, xac_ref, dyl_ref, dz_in, cw_ref, cb_ref, wa_ref, wx_ref, ba_ref, bx_ref, lam_ref, *rest):
        (dxa_ref, dxac_ref, dwa_ref, dwx_ref, dba_ref, dbx_ref, dlam_ref, dcw_ref,
         dcb_ref) = rest[nr:nr + 9]
        main_s, ctx_s = rest[2 * nr + 9:2 * nr + 11]
        if nr:
            start, finish = _chips_ops(rest[:nr], rest[nr + 9:2 * nr + 9], *rest[2 * nr + 11:])
            pl.when(pl.program_id(0) == 0)(start)
            pl.when(pl.program_id(0) == HEADS - 1)(finish)
        del dz_in
        cwv, cbv = cw_ref[...], cb_ref[...]
        lamv = lam_ref[...]
        sp = _softplus(-lamv)
        nsp = (-LRU_C) * sp
        z = jnp.zeros((1, HD), F32)

        def wmat(ref, d):
            return ref[d, 0].astype(BF16)

        def workspace(s):
            return dict(a=(s.at[0], s.at[1]), u=(s.at[2], s.at[3]), h=(s.at[4], s.at[5]), rho=(s.at[6], s.at[7]),
                        saved=(tuple(s.at[8 + k] for k in range(4)), tuple(s.at[12 + k] for k in range(4))),
                        xc=s.at[16])

        def forward(ws, xa, t_len, h0f, h0b):
            xc = _conv(xa, cwv, cbv)
            ws["xc"][...] = xc
            for d in (0, 1):
                vals = _gates(xc, wmat(wa_ref, d), wmat(wx_ref, d), ba_ref[d:d + 1, :], bx_ref[d:d + 1, :],
                              nsp[d:d + 1, :])
                r, i, a, gamma, rg = vals
                ws["a"][d][...] = a
                ws["u"][d][...] = gamma * (i * xc)
                for ref, val in zip(ws["saved"][d], (r, i, gamma, rg)):
                    ref[...] = val
            return _scan_pair(ws["a"][0], ws["u"][0], ws["h"][0], h0f, ws["a"][1], ws["u"][1], ws["h"][1], h0b,
                              t_len)

        def backward(ws, xa, t_len, h0f, h0b, dhf, dhb, first):
            xc = ws["xc"][...]
            (af, ab), (uf, ub), (hf, hb), (rf, rb) = ws["a"], ws["u"], ws["h"], ws["rho"]
            uf[...] = ab[...] * dhb
            ub[...] = af[...] * dhf
            rho_b_last, rho_f_first = _scan_pair(ab, uf, rb, z, af, ub, rf, z, t_len)
            dxc = jnp.zeros((t_len, HD), F32)
            dsp = []
            for d in (0, 1):
                r, i, gamma, rg = (ref[...] for ref in ws["saved"][d])
                a = ws["a"][d][...]
                if d == 0:
                    lam_t = dhf + _shift_up(rf[...], z)
                    h_prev = _shift_down(hf[...], h0f)
                else:
                    lam_t = dhb + _shift_down(rb[...], z)
                    h_prev = _shift_up(hb[...], h0b)
                da = lam_t * h_prev
                lx = lam_t * xc
                d_i = lx * gamma
                d_gamma = lx * i
                dxc = dxc + lam_t * (gamma * i)
                d_log_a = a * (da - d_gamma * (a * rg))
                dsp.append(jnp.sum(d_log_a * r, axis=0, keepdims=True) * (-LRU_C))
                d_pre_r = d_log_a * nsp[d:d + 1, :] * (r * (1.0 - r))
                d_pre_i = d_i * (i * (1.0 - i))
                prb, pib, xb = d_pre_r.astype(BF16), d_pre_i.astype(BF16), xc.astype(BF16)
                dxc = dxc + _dot_nt(prb, wmat(wa_ref, d)) + _dot_nt(pib, wmat(wx_ref, d))
                g_wa, g_wx = _dot_tn(xb, prb), _dot_tn(xb, pib)
                g_ba = jnp.sum(d_pre_r, axis=0, keepdims=True)
                g_bx = jnp.sum(d_pre_i, axis=0, keepdims=True)
                if first:
                    dwa_ref[d, 0] = g_wa
                    dwx_ref[d, 0] = g_wx
                    dba_ref[d:d + 1, :] = g_ba
                    dbx_ref[d:d + 1, :] = g_bx
                else:
                    dwa_ref[d, 0] += g_wa
                    dwx_ref[d, 0] += g_wx
                    dba_ref[d:d + 1, :] += g_ba
                    dbx_ref[d:d + 1, :] += g_bx
            g_lam = jnp.concatenate(dsp, axis=0) * (-_sigmoid(-lamv))
            dm1 = _shift_down(dxc, z)
            dp1 = _shift_up(dxc, z)
            dm2 = _shift_down(dm1, z)
            dxa = dp1 * cwv[0:1, :] + dxc * cwv[1:2, :] + dm1 * cwv[2:3, :] + dm2 * cwv[3:4, :]
            xm1 = _shift_down(xa, z)
            xp1 = _shift_up(xa, z)
            xp2 = _shift_up(xp1, z)
            g_cw = jnp.concatenate([jnp.sum(dxc * v, axis=0, keepdims=True) for v in (xm1, xa, xp1, xp2)], axis=0)
            g_cb = jnp.sum(dxc, axis=0, keepdims=True)
            if first:
                dlam_ref[...] = g_lam
                dcw_ref[...] = g_cw
                dcb_ref[...] = g_cb
            else:
                dlam_ref[...] += g_lam
                dcw_ref[...] += g_cw
                dcb_ref[...] += g_cb
            return dxa, rho_f_first, rho_b_last

        ws_x, ws_c = workspace(main_s), workspace(ctx_s)
        h0f, h0b = forward(ws_c, xac_ref[...], LC, z, z)
        forward(ws_x, xa_ref[...], L, h0f, h0b)
        dh = dyl_ref[...]
        dxa, dh0f, dh0b = backward(ws_x, xa_ref[...], L, h0f, h0b, dh, dh, True)
        dxa_ref[...] = dxa.astype(BF16)
        rc = _rows((LC, HD))
        dxac, _, _ = backward(ws_c, xac_ref[...], LC, z, z, jnp.where(rc == LC - 1, dh0f, 0.0),
                              jnp.where(rc == 0, dh0b, 0.0), False)
        dxac_ref[...] = dxac.astype(BF16)

    s = _lru_param_specs()
    col = lambda r: pl.BlockSpec((r, HD), lambda h: (0, h))
    return _call(
        body, name="lru_backward", grid=(HEADS,),
        out_shape=[jax.ShapeDtypeStruct((L, D_IN), BF16), jax.ShapeDtypeStruct((LC, D), BF16),
                   jax.ShapeDtypeStruct((2, HEADS, HD, HD), F32), jax.ShapeDtypeStruct((2, HEADS, HD, HD), F32),
                   jax.ShapeDtypeStruct((2, D), F32), jax.ShapeDtypeStruct((2, D), F32),
                   jax.ShapeDtypeStruct((2, D), F32), jax.ShapeDtypeStruct((CONV_W, D), F32),
                   jax.ShapeDtypeStruct((1, D), F32)] + [jax.ShapeDtypeStruct(a.shape, a.dtype) for a in chip_sums],
        in_specs=[s["xa"], s["xac"], col(L), pl.BlockSpec(memory_space=pl.ANY), s["cw"], s["cb"], s["h4"], s["h4"],
                  s["v2"], s["v2"], s["v2"]] + [HBM] * nr,
        out_specs=[col(L), col(LC), s["h4"], s["h4"], s["v2"], s["v2"], s["v2"], col(CONV_W), col(1)] + [HBM] * nr,
        scratch_shapes=[pltpu.VMEM((17, L, HD), F32), pltpu.VMEM((17, LC, HD), F32)] + (_chips_sems(nr) if nr else []),
        input_output_aliases={3: 0},
        compiler_params=pltpu.CompilerParams(dimension_semantics=("arbitrary",), vmem_limit_bytes=VMEM_LIMIT,
                                             has_side_effects=True),
    )(zx, zc, dyl, dz, cw, cb, wa, wx, ba, bx, lam, *[pltpu.with_memory_space_constraint(a, pltpu.HBM)
                                                       for a in chip_sums])


def _mixer_loss(x, tgt, zx, yl, gx, fg, lng, lnb, ws, wst, bst, wout, tm):
    ncht = tm // CHUNK

    def body(x_ref, t_ref, ga_ref, u_ref, v_ref, gb_ref, yl_ref, gx_ref, fg_ref, lng_ref, lnb_ref, ws_ref, wst_ref,
             bst_ref, wout_ref,
             dz_ref, dyl_ref, dxn_ref, y_s, do_ref, dws_ref, dbst_ref, vec_ref,
             vn_s, mix_s, dm_s, dvn_s):
        step = pl.program_id(0)

        @pl.when(step == 0)
        def _():
            dws_ref[...] = jnp.zeros_like(dws_ref)
            dbst_ref[...] = jnp.zeros_like(dbst_ref)
            vec_ref[...] = jnp.zeros_like(vec_ref)

        u, v = u_ref[...], v_ref[...]
        ug, dug_du = _gelu_and_grad(u)
        vg, dvg_dv = _gelu_and_grad(v)
        mu = jnp.mean(vg, axis=-1, keepdims=True)
        vc = vg - mu
        rstd = lax.rsqrt(jnp.mean(vc * vc, axis=-1, keepdims=True) + LN_EPS)
        vhat = vc * rstd
        lngv = lng_ref[...]
        vn_s[...] = (vhat * lngv + lnb_ref[...]).astype(BF16)
        for ch in range(ncht):
            rs = slice(ch * CHUNK, (ch + 1) * CHUNK)
            for g in range(HEADS):
                cs = slice(g * HD, (g + 1) * HD)
                mix_s[rs, cs] = _dot(ws_ref[g], vn_s[rs, cs]) + bst_ref[:, g:g + 1]
        mixed = mix_s[...]
        ga, gb, yl = ga_ref[...], gb_ref[...], yl_ref[...]
        sga, dsga = _silu_and_grad(ga)
        sgb, dsgb = _silu_and_grad(gb)
        ys = ug * mixed
        y_s[:, 0:D] = (yl * sga).astype(BF16)
        y_s[:, D:D_MIX] = (ys * sgb).astype(BF16)
        o = _dot(y_s[...], wout_ref[...])
        gxv, fgv = gx_ref[...], fg_ref[...]
        xn = x_ref[...] + gxv * o
        rs2 = lax.rsqrt(jnp.mean(xn * xn, axis=-1, keepdims=True) + NORM_EPS)
        xh = xn * rs2
        diff = xh * fgv - t_ref[...]
        vec_ref[R_LOSS:R_LOSS + 1, :] += jnp.full((1, D), jnp.sum(diff * diff) * (0.5 / D), F32)
        dout = diff * (1.0 / D)
        w = dout * fgv
        dxn = rs2 * (w - xh * jnp.mean(w * xh, axis=-1, keepdims=True))
        dxn_ref[...] = dxn
        vec_ref[0:1, :] += jnp.sum(dxn * o, axis=0, keepdims=True)
        vec_ref[1:2, :] += jnp.sum(dout * xh, axis=0, keepdims=True)
        dob = (dxn * gxv).astype(BF16)
        do_ref[...] = dob
        dy = _dot_nt(dob, wout_ref[...])
        dya, dyb = dy[:, 0:D], dy[:, D:D_MIX]
        dyl_ref[...] = dya * sga
        dys = dyb * sgb
        dz_ref[:, 0:D] = jnp.zeros((tm, D), BF16)
        dz_ref[:, D:2 * D] = (dya * yl * dsga).astype(BF16)
        dz_ref[:, 2 * D:3 * D] = (dys * mixed * dug_du).astype(BF16)
        dz_ref[:, 4 * D:5 * D] = (dyb * ys * dsgb).astype(BF16)
        dm = dys * ug
        dm_s[...] = dm.astype(BF16)
        for g in range(HEADS):
            cs = slice(g * HD, (g + 1) * HD)
            dbst_ref[:, g:g + 1] += sum(jnp.sum(dm[ch * CHUNK:(ch + 1) * CHUNK, cs], axis=1, keepdims=True)
                                        for ch in range(ncht))
            for ch in range(ncht):
                rs = slice(ch * CHUNK, (ch + 1) * CHUNK)
                dws_ref[g] += _dot_nt(dm_s[rs, cs], vn_s[rs, cs])
                dvn_s[rs, cs] = _dot(wst_ref[g], dm_s[rs, cs])
        dvn = dvn_s[...]
        vec_ref[2:3, :] += jnp.sum(dvn * vhat, axis=0, keepdims=True)
        vec_ref[3:4, :] += jnp.sum(dvn, axis=0, keepdims=True)
        dvh = dvn * lngv
        dvg = rstd * (dvh - jnp.mean(dvh, axis=-1, keepdims=True) - vhat * jnp.mean(dvh * vhat, axis=-1, keepdims=True))
        dz_ref[:, 3 * D:4 * D] = (dvg * dvg_dv).astype(BF16)

    tile = pl.BlockSpec((tm, D), lambda i: (i, 0))
    zcol = lambda n: pl.BlockSpec((tm, D), lambda i: (i, n))
    vec = pl.BlockSpec((1, D), lambda i: (0, 0))
    full = lambda *s: pl.BlockSpec(s, lambda i: (0,) * len(s))
    return _call(
        body, name="mixer_loss", grid=(L // tm,),
        out_shape=[jax.ShapeDtypeStruct((L, D_IN), BF16), jax.ShapeDtypeStruct((L, D), F32),
                   jax.ShapeDtypeStruct((L, D), F32), jax.ShapeDtypeStruct((L, D_MIX), BF16),
                   jax.ShapeDtypeStruct((L, D), BF16),
                   jax.ShapeDtypeStruct((HEADS, CHUNK, CHUNK), F32), jax.ShapeDtypeStruct((CHUNK, HEADS), F32),
                   jax.ShapeDtypeStruct((8, D), F32)],
        in_specs=[tile, tile, zcol(1), zcol(2), zcol(3), zcol(4), tile, pl.BlockSpec((1, D), lambda i: (0, 2)),
                  vec, vec, vec,
                  full(HEADS, CHUNK, CHUNK), full(HEADS, CHUNK, CHUNK), full(CHUNK, HEADS),
                  pl.BlockSpec((D_MIX, D), lambda i: (0, 0), pipeline_mode=pl.Buffered(1))],
        out_specs=[pl.BlockSpec((tm, D_IN), lambda i: (i, 0)), tile, tile,
                   pl.BlockSpec((tm, D_MIX), lambda i: (i, 0)), tile,
                   full(HEADS, CHUNK, CHUNK), full(CHUNK, HEADS), full(8, D)],
        scratch_shapes=[pltpu.VMEM((tm, D), BF16), pltpu.VMEM((tm, D), F32),
                        pltpu.VMEM((tm, D), BF16), pltpu.VMEM((tm, D), F32)],
        compiler_params=_params("arbitrary"),
    )(x, tgt, zx, zx, zx, zx, yl, gx, fg, lng, lnb, ws, wst, bst, wout)


def _grad_w(a, b, a2, b2, tk, name):
    nk = a.shape[0] // tk
    m, ncols = a.shape[1], b.shape[1]
    with_ctx = a2 is not None

    def body(*refs):
        if with_ctx:
            a_ref, b_ref, a2_ref, b2_ref, o_ref, acc = refs
        else:
            a_ref, b_ref, o_ref, acc = refs
        n, k = pl.program_id(0), pl.program_id(1)

        @pl.when(k == 0)
        def _():
            acc[...] = jnp.zeros_like(acc)

        acc[...] += _dot_tn(a_ref[...], b_ref[...])

        if with_ctx:
            @pl.when(jnp.logical_and(k == nk - 1, n == 0))
            def _():
                acc[...] += _dot_tn(a2_ref[...], b2_ref[...])

        @pl.when(k == nk - 1)
        def _():
            o_ref[...] = acc[...].astype(BF16)

    in_specs = [pl.BlockSpec((tk, m), lambda n, k: (k, 0)), pl.BlockSpec((tk, D), lambda n, k: (k, n))]
    args = [a, b]
    if with_ctx:
        in_specs += [pl.BlockSpec(a2.shape, lambda n, k: (0, 0)), pl.BlockSpec(b2.shape, lambda n, k: (0, 0))]
        args += [a2, b2]
    return _call(
        body, name=name, grid=(ncols // D, nk), out_shape=jax.ShapeDtypeStruct((m, ncols), BF16),
        in_specs=in_specs, out_specs=pl.BlockSpec((m, D), lambda n, k: (0, n)),
        scratch_shapes=[pltpu.VMEM((m, D), F32)],
        compiler_params=_params("arbitrary", "arbitrary"),
    )(*args)


def _grad_rows(xr, dz, w, mod, ng, dres, ncols, tm, name, chip_sums=()):
    rows = xr.shape[0]
    steps = rows // tm
    with_dx = dres is not None
    nr = len(chip_sums)
    nin = 6 if with_dx else 5
    nout = 2 if with_dx else 1

    def body(*refs):
        if with_dx:
            x_ref, dz_ref, w_ref, sc_ref, ng_ref, dres_ref = refs[:nin]
            dx_ref, vec_ref = refs[nin + nr:nin + nr + nout]
        else:
            x_ref, dz_ref, w_ref, sc_ref, ng_ref = refs[:nin]
            (vec_ref,) = refs[nin + nr:nin + nr + nout]
        if nr:
            start, finish = _chips_ops(refs[nin:nin + nr], refs[nin + nr + nout:nin + 2 * nr + nout],
                                       *refs[nin + 2 * nr + nout:])
            pl.when(pl.program_id(0) == 0)(start)
            pl.when(pl.program_id(0) == steps - 1)(finish)

        @pl.when(pl.program_id(0) == 0)
        def _():
            vec_ref[...] = jnp.zeros_like(vec_ref)

        dhn = _dot_nt(dz_ref[...], w_ref[...])
        x = x_ref[...]
        rs = lax.rsqrt(jnp.mean(x * x, axis=-1, keepdims=True) + NORM_EPS)
        xh = x * rs
        ngv = ng_ref[...]
        y = xh * ngv
        vec_ref[0:1, :] += jnp.sum(dhn, axis=0, keepdims=True)
        vec_ref[1:2, :] += jnp.sum(dhn * y, axis=0, keepdims=True)
        dy = dhn * (1.0 + sc_ref[...])
        vec_ref[2:3, :] += jnp.sum(dy * xh, axis=0, keepdims=True)
        if with_dx:
            dxh = dy * ngv
            dx_ref[...] = dres_ref[...] + rs * (dxh - xh * jnp.mean(dxh * xh, axis=-1, keepdims=True))

    tile = pl.BlockSpec((tm, D), lambda i: (i, 0))
    vec = pl.BlockSpec((1, D), lambda i: (0, 0))
    in_specs = [tile, pl.BlockSpec((tm, ncols), lambda i: (i, 0)),
                pl.BlockSpec((D, ncols), lambda i: (0, 0), pipeline_mode=pl.Buffered(1)),
                pl.BlockSpec((1, D), lambda i: (0, 1)), vec]
    out_shape = [jax.ShapeDtypeStruct((8, D), F32)]
    out_specs = [pl.BlockSpec((8, D), lambda i: (0, 0))]
    args = [xr, dz, w, mod, ng]
    if with_dx:
        in_specs.append(tile)
        out_shape.insert(0, jax.ShapeDtypeStruct((rows, D), F32))
        out_specs.insert(0, tile)
        args.append(dres)
    in_specs += [HBM] * nr
    out_specs += [HBM] * nr
    out_shape += [jax.ShapeDtypeStruct(a.shape, a.dtype) for a in chip_sums]
    args += [pltpu.with_memory_space_constraint(a, pltpu.HBM) for a in chip_sums]
    return _call(body, name=name, grid=(steps,), out_shape=out_shape, in_specs=in_specs, out_specs=out_specs,
                 scratch_shapes=_chips_sems(nr) if nr else [],
                 compiler_params=pltpu.CompilerParams(dimension_semantics=("arbitrary",),
                                                      vmem_limit_bytes=VMEM_LIMIT, has_side_effects=bool(nr)))(*args)


def _adamw(w, g, m, v):
    m = ADAM_B1 * m + (1.0 - ADAM_B1) * g
    v = ADAM_B2 * v + (1.0 - ADAM_B2) * (g * g)
    m_hat = m / (1.0 - ADAM_B1 ** ADAM_STEP)
    v_hat = v / (1.0 - ADAM_B2 ** ADAM_STEP)
    delta = -ADAM_LR * (m_hat / (jnp.sqrt(v_hat) + ADAM_EPS) + ADAM_WD * w)
    return delta, m, v


def _adamw_reduced(parts, w, m, v, tr, name):
    r, n = w.shape
    nparts = parts.shape[0]

    def body(p_ref, w_ref, m_ref, v_ref, g_ref, d_ref, mo_ref, vo_ref):
        g = p_ref[0].astype(F32)
        for i in range(1, nparts):
            g = g + p_ref[i].astype(F32)
        g_ref[...] = g
        d_ref[...], mo_ref[...], vo_ref[...] = _adamw(w_ref[...], g, m_ref[...], v_ref[...])

    tile = pl.BlockSpec((tr, n), lambda i: (i, 0))
    sds = jax.ShapeDtypeStruct((r, n), F32)
    return _call(
        body, name=name, grid=(r // tr,), out_shape=[sds] * 4,
        in_specs=[pl.BlockSpec((nparts, tr, n), lambda i: (0, i, 0)), tile, tile, tile], out_specs=[tile] * 4,
        compiler_params=_params("arbitrary"),
    )(parts, w, m, v)


R_GATE, R_FINAL_G, R_LN_G, R_LN_B, R_LOSS = 0, 1, 2, 3, 4
R_SH_X, R_SC_X, R_NG_X = 5, 6, 7
R_SH_C, R_SC_C, R_NG_C = 8, 9, 10
R_BA, R_BX, R_LAM, R_CW, R_CB, R_SGU_B = 11, 13, 15, 17, 21, 22
PACK_ROWS = 32
MAT_ROWS = 2 * (2 * HEADS * HD) + HEADS * CHUNK


def _reduce_small(vp_all, mat_parts, ada_w, me):
    nloc = ada_w.shape[1]

    def body(me_ref, vp_ref, mp_ref, w_ref, red_ref, mat_ref, dmod_ref, gab_ref, cpart_ref, dmc_s):
        red = vp_ref[0]
        for i in range(1, N_DEV):
            red = red + vp_ref[i]
        mat = mp_ref[0]
        for i in range(1, mp_ref.shape[0]):
            mat = mat + mp_ref[i]
        red_ref[...] = red
        mat_ref[...] = mat
        for e in range(N_DEV):
            dmod_ref[e:e + 1, 0:D] = vp_ref[e, R_SH_X:R_SH_X + 1, :]
            dmod_ref[e:e + 1, D:2 * D] = vp_ref[e, R_SC_X:R_SC_X + 1, :]
            dmod_ref[e:e + 1, 2 * D:3 * D] = vp_ref[e, R_GATE:R_GATE + 1, :]
        dmod_ref[8:9, 0:D] = red[R_SH_C:R_SH_C + 1, :]
        dmod_ref[8:9, D:2 * D] = red[R_SC_C:R_SC_C + 1, :]
        dmod_ref[8:9, 2 * D:3 * D] = jnp.zeros((1, D), F32)
        dmod_ref[9:16, :] = jnp.zeros((7, 3 * D), F32)
        gab_ref[:, 0:D] = red[R_SH_X:R_SH_X + 1, :] + red[R_SH_C:R_SH_C + 1, :]
        gab_ref[:, D:2 * D] = red[R_SC_X:R_SC_X + 1, :] + red[R_SC_C:R_SC_C + 1, :]
        gab_ref[:, 2 * D:3 * D] = red[R_GATE:R_GATE + 1, :]
        dmc_s[...] = jnp.broadcast_to(dmod_ref[8:9, :], (8, 3 * D))
        off = pl.multiple_of(me_ref[0] * nloc, 128)
        cpart_ref[...] = _dot_nt(dmc_s[:, pl.ds(off, nloc)], w_ref[...])

    return _call(
        body, name="reduce_small",
        out_shape=[jax.ShapeDtypeStruct((PACK_ROWS, D), F32), jax.ShapeDtypeStruct(mat_parts.shape[1:], F32),
                   jax.ShapeDtypeStruct((16, 3 * D), F32), jax.ShapeDtypeStruct((1, 3 * D), F32),
                   jax.ShapeDtypeStruct((8, D), F32)],
        in_specs=[pl.BlockSpec(memory_space=pltpu.SMEM), VMEM, VMEM, VMEM], out_specs=[VMEM] * 5,
        scratch_shapes=[pltpu.VMEM((8, 3 * D), F32)], compiler_params=_params(),
    )(me, vp_all, mat_parts, ada_w)


def _adamw_ada(c_all, c_ctx, dmod, w, m, v, me):
    nloc = w.shape[1]

    def body(me_ref, c_ref, cc_ref, dm_ref, w_ref, m_ref, v_ref, g_ref, d_ref, mo_ref, vo_ref):
        off = pl.multiple_of(me_ref[0] * nloc, 128)
        dm = dm_ref[:, pl.ds(off, nloc)]
        sx, _ = _silu_and_grad(c_ref[...])
        sc, _ = _silu_and_grad(cc_ref[...])
        g = _dot_tn(sx, dm[0:8, :]) + _dot_tn(jnp.broadcast_to(sc, (8, D)), dm[8:16, :])
        g_ref[...] = g
        d_ref[...], mo_ref[...], vo_ref[...] = _adamw(w_ref[...], g, m_ref[...], v_ref[...])

    sds = jax.ShapeDtypeStruct(w.shape, F32)
    return _call(
        body, name="adamw_ada_w", out_shape=[sds] * 4,
        in_specs=[pl.BlockSpec(memory_space=pltpu.SMEM)] + [VMEM] * 6, out_specs=[VMEM] * 4,
        compiler_params=_params(),
    )(me, c_all, c_ctx, dmod, w, m, v)


_SMALL = ("c_ctx", "ada_b", "norm_g", "conv_w", "conv_b", "lru_wa", "lru_ba", "lru_wx", "lru_bx", "lru_lambda",
          "sgu_ln_g", "sgu_ln_b", "sgu_w", "sgu_b", "final_g")


def _adamw_small(red, mat, cparts, gab, ws, ms, vs, me):
    n = len(_SMALL)
    nw = 2 * HEADS * HD

    def body(me_ref, red_ref, mat_ref, cp_ref, gab_ref, *refs):
        w_refs, m_refs, v_refs = refs[:n], refs[n:2 * n], refs[2 * n:3 * n]
        outs = refs[3 * n:]
        off = pl.multiple_of(me_ref[0] * HD, 128)

        def row(r, k=1):
            return red_ref[r:r + k, :]

        cc = w_refs[0][...]
        dcc = cp_ref[0, 0:1, :]
        for i in range(1, N_DEV):
            dcc = dcc + cp_ref[i, 0:1, :]
        grads = dict(
            c_ctx=dcc * _silu_and_grad(cc)[1], ada_b=gab_ref[...], norm_g=row(R_NG_X) + row(R_NG_C),
            conv_w=red_ref[R_CW:R_CW + CONV_W, pl.ds(off, HD)], conv_b=row(R_CB),
            lru_wa=mat_ref[0:nw, :], lru_ba=row(R_BA, 2), lru_wx=mat_ref[nw:2 * nw, :], lru_bx=row(R_BX, 2),
            lru_lambda=red_ref[R_LAM:R_LAM + 2, pl.ds(off, HD)], sgu_ln_g=row(R_LN_G), sgu_ln_b=row(R_LN_B),
            sgu_w=mat_ref[2 * nw:MAT_ROWS, :], sgu_b=row(R_SGU_B), final_g=row(R_FINAL_G))
        for j, name in enumerate(_SMALL):
            g = grads[name]
            outs[j][...] = g
            outs[n + j][...], outs[2 * n + j][...], outs[3 * n + j][...] = _adamw(w_refs[j][...], g, m_refs[j][...],
                                                                                 v_refs[j][...])

    sds = [jax.ShapeDtypeStruct(ws[k].shape, F32) for k in _SMALL]
    outs = _call(
        body, name="adamw_small", out_shape=sds * 4,
        in_specs=[pl.BlockSpec(memory_space=pltpu.SMEM)] + [VMEM] * (4 + 3 * n), out_specs=[VMEM] * (4 * n),
        compiler_params=_params(),
    )(me, red, mat, cparts, gab, *[ws[k] for k in _SMALL], *[ms[k] for k in _SMALL], *[vs[k] for k in _SMALL])
    return [dict(zip(_SMALL, outs[i * n:(i + 1) * n])) for i in range(4)]


def kernel(x, c, ctx, c_ctx, ada_w, ada_b, norm_g, w_in, conv_w, conv_b, lru_wa, lru_ba, lru_wx, lru_bx, lru_lambda, sgu_ln_g, sgu_ln_b, sgu_w, sgu_b, w_out, final_g, loss_target, m_c_ctx, m_ada_w, m_ada_b, m_norm_g, m_w_in, m_conv_w, m_conv_b, m_lru_wa, m_lru_ba, m_lru_wx, m_lru_bx, m_lru_lambda, m_sgu_ln_g, m_sgu_ln_b, m_sgu_w, m_sgu_b, m_w_out, m_final_g, v_c_ctx, v_ada_w, v_ada_b, v_norm_g, v_w_in, v_conv_w, v_conv_b, v_lru_wa, v_lru_ba, v_lru_wx, v_lru_bx, v_lru_lambda, v_sgu_ln_g, v_sgu_ln_b, v_sgu_w, v_sgu_b, v_w_out, v_final_g):
    args = dict(locals())
    me_s = 4 * lax.axis_index("x") + 2 * lax.axis_index("y") + lax.axis_index("c")
    me = me_s.astype(jnp.int32).reshape(1)
    xr, ctxr, tgt = x[0], ctx[0], loss_target[0]
    cc = c_ctx.reshape(1, D)
    nw = 2 * HEADS * HD
    view = dict(c_ctx=(1, D), ada_b=(1, 3 * D), norm_g=(1, D), conv_w=(CONV_W, HD), conv_b=(1, D), lru_wa=(nw, HD),
                lru_ba=(2, D), lru_wx=(nw, HD), lru_bx=(2, D), lru_lambda=(2, HD), sgu_ln_g=(1, D), sgu_ln_b=(1, D),
                sgu_w=(HEADS * CHUNK, CHUNK), sgu_b=(1, D), final_g=(1, D))

    (c_all,) = _gather2([c], ["ag"], "gather_c")
    c_all = c_all.reshape(N_DEV, D)
    part = _ada_forward(c_all, cc, ada_w[0], ada_b, me)
    (parts,) = _gather2([part], ["ag"], "gather_mod")
    modx = lax.dynamic_index_in_dim(parts, me_s, axis=1, keepdims=False).reshape(1, 3 * D)
    modc = parts[:, 8, :].reshape(1, 3 * D)

    w_in_b, w_out_b = _cast_weights(w_in[0], w_out[0])
    (w_full,) = _gather2([w_in_b], ["agc"], "gather_w_in")

    zx, hn, wout_all, cw_full, lam_full = _project(
        xr, modx, norm_g, w_full, D_IN, 256, "project_x", gather=[w_out_b, conv_w[0], lru_lambda[0]],
        gather_modes=["ag", "agc", "agc"])
    wout_full = wout_all.reshape(D_MIX, D)
    zc, hnc = _project(ctxr, modc, norm_g, w_full, D, LC, "project_ctx")
    ba, bx = lru_ba.reshape(2, D), lru_bx.reshape(2, D)
    yl = _lru_forward(zx, zc, cw_full, conv_b, lru_wa[0], lru_wx[0], ba, bx, lam_full)
    ws_b = sgu_w[0].astype(BF16)
    dz, dyl, dxn, ycat, dob, dws, dbst, mvec = _mixer_loss(
        xr, tgt, zx, yl, modx, final_g.reshape(1, D), sgu_ln_g, sgu_ln_b, ws_b, jnp.swapaxes(ws_b, 1, 2),
        sgu_b[0].T, wout_full, 128)

    gw_out = _grad_w(ycat, dob, None, None, 512, "grad_w_out")
    (wout_sums,) = _reduce2_local([gw_out.reshape(N_DEV, D_MIX // N_DEV, D)], ["a2a"], me, "reduce_w_out")
    dz, dxac, dwa, dwx, dba, dbx, dlam, dcw, dcb, wout_parts = _lru_backward(
        zx, zc, dyl, dz, cw_full, conv_b, lru_wa[0], lru_wx[0], ba, bx, lam_full, [wout_sums])
    gw_in = _grad_w(hn, dz, hnc, dxac, 512, "grad_w_in")
    matpack = jnp.concatenate([dwa.reshape(nw, HD), dwx.reshape(nw, HD), dws.reshape(HEADS * CHUNK, CHUNK)], axis=0)
    win_sums, mat_sums = _reduce2_local([gw_in, matpack.reshape(N_DEV, MAT_ROWS // N_DEV, HD)], ["a2ac", "a2a"], me,
                                        "reduce_w_in")
    gx, xvec, win_parts, mat_parts = _grad_rows(xr, dz, w_full, modx, norm_g, dxn, D_IN, 256, "grad_rows_x",
                                                chip_sums=[win_sums, mat_sums])
    (cvec,) = _grad_rows(ctxr, dxac, w_full, modc, norm_g, None, D, LC, "grad_rows_ctx")
    pack = jnp.concatenate([mvec[0:5], xvec[0:3], cvec[0:3], dba, dbx, dlam, dcw, dcb, dbst.T.reshape(1, D),
                            jnp.zeros((PACK_ROWS - R_SGU_B - 1, D), F32)], axis=0)
    (vp_all,) = _gather2([pack], ["ag"], "gather_pack")
    red, matpiece, dmod, gab, cpart = _reduce_small(vp_all, mat_parts, ada_w[0], me)
    mat_all, cparts = _gather2([matpiece, cpart], ["ag", "ag"], "gather_small")

    g_w_in, d_w_in, nm_w_in, nv_w_in = _adamw_reduced(win_parts, w_in[0], m_w_in[0], v_w_in[0], 256, "adamw_w_in")
    g_w_out, d_w_out, nm_w_out, nv_w_out = _adamw_reduced(wout_parts, w_out[0], m_w_out[0], v_w_out[0], 128,
                                                          "adamw_w_out")
    g_ada, d_ada, nm_ada, nv_ada = _adamw_ada(c_all, cc, dmod, ada_w[0], m_ada_w[0], v_ada_w[0], me)
    ws = {k: args[k].reshape(view[k]) for k in _SMALL}
    ms = {k: args["m_" + k].reshape(view[k]) for k in _SMALL}
    vs = {k: args["v_" + k].reshape(view[k]) for k in _SMALL}
    small = _adamw_small(red, mat_all.reshape(MAT_ROWS, HD), cparts, gab, ws, ms, vs, me)
    big = dict(w_in=(g_w_in, d_w_in, nm_w_in, nv_w_in), w_out=(g_w_out, d_w_out, nm_w_out, nv_w_out),
               ada_w=(g_ada, d_ada, nm_ada, nv_ada))

    loss = red[R_LOSS, 0]
    names = ("c_ctx", "ada_w", "ada_b", "norm_g", "w_in", "conv_w", "conv_b", "lru_wa", "lru_ba", "lru_wx", "lru_bx",
             "lru_lambda", "sgu_ln_g", "sgu_ln_b", "sgu_w", "sgu_b", "w_out", "final_g")
    outs = [loss, gx.reshape(x.shape)]
    for kind in range(4):
        for k in names:
            val = big[k][kind] if k in big else small[kind][k]
            outs.append(val.reshape(args[k].shape))
    return tuple(outs)
```

```python
import functools

import jax
import jax.numpy as jnp
from jax import lax
from jax.experimental import pallas as pl
from jax.experimental.pallas import tpu as pltpu

F32 = jnp.float32
BF16 = jnp.bfloat16

N_DEV = 8
D = 1024
L = 2048
LC = 256
HEADS = 8
HD = 128
CHUNK = 128
D_IN = 5 * D
W_IN_SHARD = D_IN // N_DEV
D_MIX = 2 * D
CONV_W = 4
LRU_C = 8.0
NORM_EPS = 1e-6
LN_EPS = 1e-5
ADAM_LR, ADAM_B1, ADAM_B2, ADAM_EPS, ADAM_WD, ADAM_STEP = 0.001, 0.9, 0.999, 1e-08, 0.01, 10

VMEM_LIMIT = 56 * 1024 * 1024

HBM = pl.BlockSpec(memory_space=pltpu.HBM)
VMEM = pl.BlockSpec(memory_space=pltpu.VMEM)
MESH = pl.DeviceIdType.MESH


def _call(body, **kw):
    return pl.pallas_call(body, **kw)


def _params(*sem):
    return pltpu.CompilerParams(dimension_semantics=sem, vmem_limit_bytes=VMEM_LIMIT)


def _sigmoid(x):
    return 0.5 * jnp.tanh(0.5 * x) + 0.5


def _silu_and_grad(x):
    s = _sigmoid(x)
    return x * s, s * (1.0 + x * (1.0 - s))


_G0 = 0.7978845608028654
_G1 = 0.044715


def _gelu_and_grad(x):
    x2 = x * x
    t = jnp.tanh(_G0 * (x + _G1 * x * x2))
    cdf = 0.5 * (1.0 + t)
    return x * cdf, cdf + 0.5 * x * (1.0 - t * t) * (_G0 * (1.0 + 3.0 * _G1 * x2))


def _gelu(x):
    return 0.5 * x * (1.0 + jnp.tanh(_G0 * (x + _G1 * x * x * x)))


def _softplus(z):
    t = jnp.exp(-jnp.abs(z))
    u = 1.0 + t
    log1p = jnp.where(u == 1.0, t, jnp.log(u) * t / jnp.where(u == 1.0, 1.0, u - 1.0))
    return jnp.maximum(z, 0.0) + log1p


def _dot(a, b):
    return jnp.dot(a, b, preferred_element_type=F32)


def _dot_nt(a, b):
    return lax.dot_general(a, b, (((1,), (1,)), ((), ())), preferred_element_type=F32)


def _dot_tn(a, b):
    return lax.dot_general(a, b, (((0,), (0,)), ((), ())), preferred_element_type=F32)


def _rows(shape):
    return lax.broadcasted_iota(jnp.int32, shape, 0)


def _shift_down(x, first):
    return jnp.where(_rows(x.shape) == 0, first, pltpu.roll(x, 1, 0))


def _shift_up(x, last):
    n = x.shape[0]
    return jnp.where(_rows(x.shape) == n - 1, last, pltpu.roll(x, n - 1, 0))


def _gather2(arrays, modes, name):
    n = len(arrays)

    def body(*refs):
        start, forward, finish = _gather2_ops(refs[:n], refs[n:2 * n], modes, *refs[2 * n:])
        start()
        forward()
        finish()

    return _call(
        body, name=name, out_shape=_gather2_shapes(arrays, modes), in_specs=[HBM] * n, out_specs=[HBM] * n,
        scratch_shapes=_gather2_sems(n), compiler_params=pltpu.CompilerParams(has_side_effects=True),
    )(*[pltpu.with_memory_space_constraint(a, pltpu.HBM) for a in arrays])


def _gather2_shapes(arrays, modes):
    return [jax.ShapeDtypeStruct((N_DEV,) + a.shape if m == "ag" else (a.shape[0], N_DEV * a.shape[1]), a.dtype)
            for a, m in zip(arrays, modes)]


def _gather2_sems(n):
    return [pltpu.SemaphoreType.DMA((n, N_DEV - 1)), pltpu.SemaphoreType.DMA((n, N_DEV - 1)),
            pltpu.SemaphoreType.DMA((n,))]


def _gather2_ops(ins, outs, modes, send_sems, recv_sems, local_sems):
    n = len(ins)
    x, y, c = lax.axis_index("x"), lax.axis_index("y"), lax.axis_index("c")
    me, sibling = (x, y, c), (x, y, 1 - c)
    chips = [(x ^ (k >> 1), y ^ (k & 1)) for k in (1, 2, 3)]

    def slot(j, px, py, pc):
        dev = 4 * px + 2 * py + pc
        if modes[j] == "agc":
            w = ins[j].shape[1]
            return outs[j].at[:, pl.ds(pl.multiple_of(dev * w, 128), w)]
        return outs[j].at[dev]

    def copy(j, k, block, to, src=None):
        return pltpu.make_async_remote_copy(
            src_ref=slot(j, *block) if src is None else src, dst_ref=slot(j, *block),
            send_sem=send_sems.at[j, k], recv_sem=recv_sems.at[j, k], device_id=to, device_id_type=MESH)

    def own(j):
        return pltpu.make_async_copy(ins[j], slot(j, *me), local_sems.at[j])

    def first(j):
        return [copy(j, 0, me, sibling, src=ins[j])] + [copy(j, 1 + i, me, (*chip, c), src=ins[j])
                                                        for i, chip in enumerate(chips)]

    def passed(j, i):
        return copy(j, 4 + i, (*chips[i], c), sibling)

    def start():
        for j in range(n):
            own(j).start()
            for cp in first(j):
                cp.start()

    def forward():
        for i, chip in enumerate(chips):
            for j in range(n):
                copy(j, 1 + i, (*chip, c), me).wait_recv()
                passed(j, i).start()

    def finish():
        for j in range(n):
            copy(j, 0, sibling, me).wait_recv()
            for i, chip in enumerate(chips):
                copy(j, 4 + i, (*chip, 1 - c), me).wait_recv()
            for cp in first(j) + [passed(j, i) for i in range(3)]:
                cp.wait_send()
            own(j).wait()

    return start, forward, finish


def _reduce2_local(arrays, modes, me, name, counts=None):
    n = len(arrays)
    counts = counts or [4] * n
    shapes = [(a.shape[1], a.shape[2]) if m == "a2a" else (a.shape[0], a.shape[1] // (2 * cnt))
              for a, m, cnt in zip(arrays, modes, counts)]
    staged = [jax.ShapeDtypeStruct((cnt,) + s, a.dtype) for s, a, cnt in zip(shapes, arrays, counts)]

    def piece(ref, mode, dev, w):
        return ref.at[dev] if mode == "a2a" else ref.at[:, pl.ds(pl.multiple_of(dev * w, 128), w)]

    def to_sibling(*refs):
        ins, outs = refs[:n], refs[n:2 * n]
        send_sems, recv_sems = refs[2 * n:]
        x, y, c = lax.axis_index("x"), lax.axis_index("y"), lax.axis_index("c")
        copies = []
        for j in range(n):
            for q in range(counts[j]):
                cp = pltpu.make_async_remote_copy(
                    src_ref=piece(ins[j], modes[j], 2 * q + (1 - c), shapes[j][1]), dst_ref=outs[j].at[q],
                    send_sem=send_sems.at[j, q], recv_sem=recv_sems.at[j, q], device_id=(x, y, 1 - c),
                    device_id_type=MESH)
                cp.start()
                copies.append(cp)
        for cp in copies:
            cp.wait()

    stage = _call(
        to_sibling, name=name + "_d2d", out_shape=staged, in_specs=[HBM] * n, out_specs=[HBM] * n,
        scratch_shapes=[pltpu.SemaphoreType.DMA((n, 4)), pltpu.SemaphoreType.DMA((n, 4))],
        compiler_params=pltpu.CompilerParams(has_side_effects=True),
    )(*[pltpu.with_memory_space_constraint(a, pltpu.HBM) for a in arrays])

    def add(me_ref, *refs):
        del me_ref
        own, got, outs = refs[:n], refs[n:2 * n], refs[2 * n:]
        for j in range(n):
            mine = own[j][0] if modes[j] == "a2a" else own[j][...]
            outs[j][0] = (mine.astype(F32) + got[j][0].astype(F32)).astype(outs[j].dtype)

    in_specs, slot_specs = [], []
    for (r, w), m, cnt in zip(shapes, modes, counts):
        if m == "a2a":
            in_specs.append(pl.BlockSpec(
                (1, r, w), lambda q, me_ref, cnt=cnt: (2 * jnp.minimum(q, cnt - 1) + me_ref[0] % 2, 0, 0)))
        else:
            in_specs.append(pl.BlockSpec(
                (r, w), lambda q, me_ref, cnt=cnt: (0, 2 * jnp.minimum(q, cnt - 1) + me_ref[0] % 2)))
        slot_specs.append(pl.BlockSpec((1, r, w), lambda q, me_ref, cnt=cnt: (jnp.minimum(q, cnt - 1), 0, 0)))
    return _call(
        add, name=name + "_add", out_shape=staged,
        grid_spec=pltpu.PrefetchScalarGridSpec(num_scalar_prefetch=1, grid=(max(counts),),
                                               in_specs=in_specs + slot_specs, out_specs=slot_specs),
        compiler_params=_params("arbitrary"),
    )(me, *arrays, *stage)


def _chips_sems(n):
    return [pltpu.SemaphoreType.DMA((n, 3)), pltpu.SemaphoreType.DMA((n, 3)), pltpu.SemaphoreType.DMA((n,))]


def _chips_ops(ins, outs, send_sems, recv_sems, local_sems, first_chips=None):
    x, y, c = lax.axis_index("x"), lax.axis_index("y"), lax.axis_index("c")
    qm = 2 * x + y
    first_chips = first_chips or [0] * len(ins)

    def owns(j, chip):
        lo, cnt = first_chips[j], ins[j].shape[0]
        if lo == 0 and cnt == 4:
            return None
        return jnp.logical_and(chip >= lo, chip < lo + cnt)

    def guarded(cond, fn):
        if cond is None:
            fn()
        else:
            pl.when(cond)(fn)

    def remote(j, k):
        px, py = x ^ (k >> 1), y ^ (k & 1)
        return pltpu.make_async_remote_copy(
            src_ref=ins[j].at[slot(j, 2 * px + py)], dst_ref=outs[j].at[qm], send_sem=send_sems.at[j, k - 1],
            recv_sem=recv_sems.at[j, k - 1], device_id=(px, py, c), device_id_type=MESH)

    def slot(j, chip):
        return jnp.clip(chip - first_chips[j], 0, ins[j].shape[0] - 1)

    def local(j):
        return pltpu.make_async_copy(ins[j].at[slot(j, qm)], outs[j].at[qm], local_sems.at[j])

    def start():
        for j in range(len(ins)):
            for k in (1, 2, 3):
                guarded(owns(j, qm ^ k), lambda j=j, k=k: remote(j, k).start())
            guarded(owns(j, qm), lambda j=j: local(j).start())

    def finish():
        for j in range(len(ins)):
            for k in (1, 2, 3):
                guarded(owns(j, qm ^ k), lambda j=j, k=k: remote(j, k).wait_send())
                guarded(owns(j, qm), lambda j=j, k=k: remote(j, k).wait_recv())
            guarded(owns(j, qm), lambda j=j: local(j).wait())

    return start, finish


def _ada_forward(c_all, c_ctx, ada_w, ada_b, me):
    nloc = ada_w.shape[1]

    def body(me_ref, c_ref, cc_ref, w_ref, b_ref, o_ref):
        off = pl.multiple_of(me_ref[0] * nloc, 128)
        b = b_ref[:, pl.ds(off, nloc)]
        w = w_ref[...]
        sx, _ = _silu_and_grad(c_ref[...])
        sc, _ = _silu_and_grad(jnp.broadcast_to(cc_ref[...], (8, D)))
        o_ref[0:8, :] = _dot(sx, w) + b
        o_ref[8:16, :] = _dot(sc, w) + b

    return _call(
        body, name="ada_forward", out_shape=jax.ShapeDtypeStruct((16, nloc), F32),
        in_specs=[pl.BlockSpec(memory_space=pltpu.SMEM), VMEM, VMEM, VMEM, VMEM], out_specs=VMEM,
    )(me, c_all, c_ctx, ada_w, ada_b)


def _cast_weights(w_in, w_out):
    def body(a_ref, b_ref, ao_ref, bo_ref):
        ao_ref[...] = a_ref[...].astype(BF16)
        bo_ref[...] = b_ref[...].astype(BF16)

    return _call(
        body, name="cast_weights",
        out_shape=[jax.ShapeDtypeStruct(w_in.shape, BF16), jax.ShapeDtypeStruct(w_out.shape, BF16)],
        in_specs=[VMEM, VMEM], out_specs=[VMEM, VMEM], compiler_params=_params(),
    )(w_in, w_out)


def _project(xr, mod, ng, w, ncols, tm, name, gather=None, gather_modes=()):
    rows = xr.shape[0]
    steps = rows // tm
    ng_ = len(gather or ())

    def body(x_ref, sh_ref, sc_ref, ng_ref, w_ref, *rest):
        z_ref, hn_ref = rest[ng_:ng_ + 2]
        if ng_:
            start, forward, finish = _gather2_ops(rest[:ng_], rest[ng_ + 2:2 * ng_ + 2], gather_modes,
                                                  *rest[2 * ng_ + 2:])
            pl.when(pl.program_id(0) == 0)(start)
            pl.when(pl.program_id(0) == steps // 2)(forward)
        x = x_ref[...]
        rs = lax.rsqrt(jnp.mean(x * x, axis=-1, keepdims=True) + NORM_EPS)
        hn = (x * rs * ng_ref[...]) * (1.0 + sc_ref[...]) + sh_ref[...]
        hb = hn.astype(BF16)
        hn_ref[...] = hb
        for n in range(ncols // D):
            z_ref[:, n * D:(n + 1) * D] = _dot(hb, w_ref[:, n * D:(n + 1) * D])
        if ng_:
            pl.when(pl.program_id(0) == steps - 1)(finish)

    vec = pl.BlockSpec((1, D), lambda i: (0, 0))
    gathered = _gather2_shapes(gather, gather_modes) if ng_ else []
    return _call(
        body, name=name, grid=(steps,),
        out_shape=[jax.ShapeDtypeStruct((rows, ncols), F32), jax.ShapeDtypeStruct((rows, D), BF16)] + gathered,
        in_specs=[pl.BlockSpec((tm, D), lambda i: (i, 0)), vec, pl.BlockSpec((1, D), lambda i: (0, 1)), vec,
                  pl.BlockSpec((D, ncols), lambda i: (0, 0), pipeline_mode=pl.Buffered(1))] + [HBM] * ng_,
        out_specs=[pl.BlockSpec((tm, ncols), lambda i: (i, 0)), pl.BlockSpec((tm, D), lambda i: (i, 0))] + [HBM] * ng_,
        scratch_shapes=_gather2_sems(ng_) if ng_ else [],
        compiler_params=pltpu.CompilerParams(dimension_semantics=("arbitrary",), vmem_limit_bytes=VMEM_LIMIT,
                                             has_side_effects=bool(ng_)),
    )(xr, mod, mod, ng, w, *[pltpu.with_memory_space_constraint(a, pltpu.HBM) for a in gather or ()])


def _scan_pair(af_ref, uf_ref, hf_ref, h0f, ab_ref, ub_ref, hb_ref, h0b, t_len):
    span = 8 * SCAN_BLOCKS
    nit = t_len // span
    rows = _rows((8, HD))

    def local_scan(a, b, forward):
        for s in (1, 2, 4):
            sh = s if forward else 8 - s
            m = rows >= s if forward else rows < 8 - s
            b = a * jnp.where(m, pltpu.roll(b, sh, 0), 0.0) + b
            a = a * jnp.where(m, pltpu.roll(a, sh, 0), 1.0)
        return a, b

    def span_scan(a_ref, u_ref, h_ref, off, carry, forward):
        order = range(SCAN_BLOCKS) if forward else range(SCAN_BLOCKS - 1, -1, -1)
        last = slice(7, 8) if forward else slice(0, 1)
        for q in order:
            rs = pl.ds(off + 8 * q, 8)
            a, b = local_scan(a_ref[rs, :], u_ref[rs, :], forward)
            h_ref[rs, :] = b + a * carry
            carry = a[last, :] * carry + b[last, :]
        return carry

    def body(k, carry):
        cf, cb = carry
        cf = span_scan(af_ref, uf_ref, hf_ref, pl.multiple_of(k * span, span), cf, True)
        cb = span_scan(ab_ref, ub_ref, hb_ref, pl.multiple_of((nit - 1 - k) * span, span), cb, False)
        return cf, cb

    return lax.fori_loop(0, nit, body, (h0f, h0b))


SCAN_BLOCKS = 4


def _conv(xa, cw, cb):
    z = jnp.zeros((1, HD), F32)
    xm1 = _shift_down(xa, z)
    xp1 = _shift_up(xa, z)
    xp2 = _shift_up(xp1, z)
    return xm1 * cw[0:1, :] + xa * cw[1:2, :] + xp1 * cw[2:3, :] + xp2 * cw[3:4, :] + cb


def _gates(xc, wa, wx, ba, bx, nsp):
    xb = xc.astype(BF16)
    r = _sigmoid(_dot(xb, wa) + ba)
    i = _sigmoid(_dot(xb, wx) + bx)
    log_a = r * nsp
    a = jnp.exp(log_a)
    g2 = jnp.tanh(log_a) * (-1.0 - a * a)
    rg = lax.rsqrt(jnp.maximum(g2, 1e-30))
    return r, i, a, g2 * rg, rg


def _lru_param_specs():
    h4 = pl.BlockSpec((2, 1, HD, HD), lambda h: (0, h, 0, 0))
    v2 = pl.BlockSpec((2, HD), lambda h: (0, h))
    return dict(
        xa=pl.BlockSpec((L, HD), lambda h: (0, h)), xac=pl.BlockSpec((LC, HD), lambda h: (0, h)),
        cw=pl.BlockSpec((CONV_W, HD), lambda h: (0, h)), cb=pl.BlockSpec((1, HD), lambda h: (0, h)), h4=h4, v2=v2)


def _lru_forward(zx, zc, cw, cb, wa, wx, ba, bx, lam):
    def body(xa_ref, xac_ref, cw_ref, cb_ref, wa_ref, wx_ref, ba_ref, bx_ref, lam_ref, yl_ref,
             af, uf, hf, ab, ub, hb):
        cwv, cbv = cw_ref[...], cb_ref[...]
        nsp = (-LRU_C) * _softplus(-lam_ref[...])

        def forward(xa, t_len, h0f, h0b):
            xc = _conv(xa, cwv, cbv)
            for d, (a_ref, u_ref) in enumerate(((af, uf), (ab, ub))):
                _, i, a, gamma, _ = _gates(xc, wa_ref[d, 0].astype(BF16), wx_ref[d, 0].astype(BF16),
                                           ba_ref[d:d + 1, :], bx_ref[d:d + 1, :], nsp[d:d + 1, :])
                a_ref[0:t_len, :] = a
                u_ref[0:t_len, :] = gamma * (i * xc)
            return _scan_pair(af, uf, hf, h0f, ab, ub, hb, h0b, t_len)

        z = jnp.zeros((1, HD), F32)
        h0f, h0b = forward(xac_ref[...], LC, z, z)
        forward(xa_ref[...], L, h0f, h0b)
        yl_ref[...] = hf[...] + hb[...]

    s = _lru_param_specs()
    return _call(
        body, name="lru_forward", grid=(HEADS,), out_shape=jax.ShapeDtypeStruct((L, D), F32),
        in_specs=[s["xa"], s["xac"], s["cw"], s["cb"], s["h4"], s["h4"], s["v2"], s["v2"], s["v2"]],
        out_specs=pl.BlockSpec((L, HD), lambda h: (0, h)),
        scratch_shapes=[pltpu.VMEM((L, HD), F32)] * 6,
        compiler_params=_params("arbitrary"),
    )(zx, zc, cw, cb, wa, wx, ba, bx, lam)


def _lru_backward(zx, zc, dyl, dz, cw, cb, wa, wx, ba, bx, lam, chip_sums, first_chips=None):
    nr = len(chip_sums)

    def body(xa_ref, xac_ref, dyl_ref, dz_in, cw_ref, cb_ref, wa_ref, wx_ref, ba_ref, bx_ref, lam_ref, *rest):
        (dxa_ref, dxac_ref, dwa_ref, dwx_ref, dba_ref, dbx_ref, dlam_ref, dcw_ref,
         dcb_ref) = rest[nr:nr + 9]
        main_s, ctx_s = rest[2 * nr + 9:2 * nr + 11]
        if nr:
            start, finish = _chips_ops(rest[:nr], rest[nr + 9:2 * nr + 9], *rest[2 * nr + 11:],
                                       first_chips=first_chips)
            pl.when(pl.program_id(0) == 0)(start)
            pl.when(pl.program_id(0) == HEADS - 1)(finish)
        del dz_in
        cwv, cbv = cw_ref[...], cb_ref[...]
        lamv = lam_ref[...]
        sp = _softplus(-lamv)
        nsp = (-LRU_C) * sp
        z = jnp.zeros((1, HD), F32)

        def wmat(ref, d):
            return ref[d, 0].astype(BF16)

        def workspace(s):
            return dict(a=(s.at[0], s.at[1]), u=(s.at[2], s.at[3]), h=(s.at[4], s.at[5]), rho=(s.at[6], s.at[7]),
                        saved=(tuple(s.at[8 + k] for k in range(4)), tuple(s.at[12 + k] for k in range(4))),
                        xc=s.at[16])

        def forward(ws, xa, t_len, h0f, h0b):
            xc = _conv(xa, cwv, cbv)
            ws["xc"][...] = xc
            for d in (0, 1):
                vals = _gates(xc, wmat(wa_ref, d), wmat(wx_ref, d), ba_ref[d:d + 1, :], bx_ref[d:d + 1, :],
                              nsp[d:d + 1, :])
                r, i, a, gamma, rg = vals
                ws["a"][d][...] = a
                ws["u"][d][...] = gamma * (i * xc)
                for ref, val in zip(ws["saved"][d], (r, i, gamma, rg)):
                    ref[...] = val
            return _scan_pair(ws["a"][0], ws["u"][0], ws["h"][0], h0f, ws["a"][1], ws["u"][1], ws["h"][1], h0b,
                              t_len)

        def backward(ws, xa, t_len, h0f, h0b, dhf, dhb, first):
            xc = ws["xc"][...]
            (af, ab), (uf, ub), (hf, hb), (rf, rb) = ws["a"], ws["u"], ws["h"], ws["rho"]
            uf[...] = ab[...] * dhb
            ub[...] = af[...] * dhf
            rho_b_last, rho_f_first = _scan_pair(ab, uf, rb, z, af, ub, rf, z, t_len)
            dxc = jnp.zeros((t_len, HD), F32)
            dsp = []
            for d in (0, 1):
                r, i, gamma, rg = (ref[...] for ref in ws["saved"][d])
                a = ws["a"][d][...]
                if d == 0:
                    lam_t = dhf + _shift_up(rf[...], z)
                    h_prev = _shift_down(hf[...], h0f)
                else:
                    lam_t = dhb + _shift_down(rb[...], z)
                    h_prev = _shift_up(hb[...], h0b)
                da = lam_t * h_prev
                lx = lam_t * xc
                d_i = lx * gamma
                d_gamma = lx * i
                dxc = dxc + lam_t * (gamma * i)
                d_log_a = a * (da - d_gamma * (a * rg))
                dsp.append(jnp.sum(d_log_a * r, axis=0, keepdims=True) * (-LRU_C))
                d_pre_r = d_log_a * nsp[d:d + 1, :] * (r * (1.0 - r))
                d_pre_i = d_i * (i * (1.0 - i))
                prb, pib, xb = d_pre_r.astype(BF16), d_pre_i.astype(BF16), xc.astype(BF16)
                dxc = dxc + _dot_nt(prb, wmat(wa_ref, d)) + _dot_nt(pib, wmat(wx_ref, d))
                g_wa, g_wx = _dot_tn(xb, prb), _dot_tn(xb, pib)
                g_ba = jnp.sum(d_pre_r, axis=0, keepdims=True)
                g_bx = jnp.sum(d_pre_i, axis=0, keepdims=True)
                if first:
                    dwa_ref[d, 0] = g_wa
                    dwx_ref[d, 0] = g_wx
                    dba_ref[d:d + 1, :] = g_ba
                    dbx_ref[d:d + 1, :] = g_bx
                else:
                    dwa_ref[d, 0] += g_wa
                    dwx_ref[d, 0] += g_wx
                    dba_ref[d:d + 1, :] += g_ba
                    dbx_ref[d:d + 1, :] += g_bx
            g_lam = jnp.concatenate(dsp, axis=0) * (-_sigmoid(-lamv))
            dm1 = _shift_down(dxc, z)
            dp1 = _shift_up(dxc, z)
            dm2 = _shift_down(dm1, z)
            dxa = dp1 * cwv[0:1, :] + dxc * cwv[1:2, :] + dm1 * cwv[2:3, :] + dm2 * cwv[3:4, :]
            xm1 = _shift_down(xa, z)
            xp1 = _shift_up(xa, z)
            xp2 = _shift_up(xp1, z)
            g_cw = jnp.concatenate([jnp.sum(dxc * v, axis=0, keepdims=True) for v in (xm1, xa, xp1, xp2)], axis=0)
            g_cb = jnp.sum(dxc, axis=0, keepdims=True)
            if first:
                dlam_ref[...] = g_lam
                dcw_ref[...] = g_cw
                dcb_ref[...] = g_cb
            else:
                dlam_ref[...] += g_lam
                dcw_ref[...] += g_cw
                dcb_ref[...] += g_cb
            return dxa, rho_f_first, rho_b_last

        ws_x, ws_c = workspace(main_s), workspace(ctx_s)
        h0f, h0b = forward(ws_c, xac_ref[...], LC, z, z)
        forward(ws_x, xa_ref[...], L, h0f, h0b)
        dh = dyl_ref[...]
        dxa, dh0f, dh0b = backward(ws_x, xa_ref[...], L, h0f, h0b, dh, dh, True)
        dxa_ref[...] = dxa.astype(BF16)
        rc = _rows((LC, HD))
        dxac, _, _ = backward(ws_c, xac_ref[...], LC, z, z, jnp.where(rc == LC - 1, dh0f, 0.0),
                              jnp.where(rc == 0, dh0b, 0.0), False)
        dxac_ref[...] = dxac.astype(BF16)

    s = _lru_param_specs()
    col = lambda r: pl.BlockSpec((r, HD), lambda h: (0, h))
    return _call(
        body, name="lru_backward", grid=(HEADS,),
        out_shape=[jax.ShapeDtypeStruct((L, D_IN), BF16), jax.ShapeDtypeStruct((LC, D), BF16),
                   jax.ShapeDtypeStruct((2, HEADS, HD, HD), F32), jax.ShapeDtypeStruct((2, HEADS, HD, HD), F32),
                   jax.ShapeDtypeStruct((2, D), F32), jax.ShapeDtypeStruct((2, D), F32),
                   jax.ShapeDtypeStruct((2, D), F32), jax.ShapeDtypeStruct((CONV_W, D), F32),
                   jax.ShapeDtypeStruct((1, D), F32)] + [jax.ShapeDtypeStruct((4,) + a.shape[1:], a.dtype)
                                                          for a in chip_sums],
        in_specs=[s["xa"], s["xac"], col(L), pl.BlockSpec(memory_space=pl.ANY), s["cw"], s["cb"], s["h4"], s["h4"],
                  s["v2"], s["v2"], s["v2"]] + [HBM] * nr,
        out_specs=[col(L), col(LC), s["h4"], s["h4"], s["v2"], s["v2"], s["v2"], col(CONV_W), col(1)] + [HBM] * nr,
        scratch_shapes=[pltpu.VMEM((17, L, HD), F32), pltpu.VMEM((17, LC, HD), F32)] + (_chips_sems(nr) if nr else []),
        input_output_aliases={3: 0},
        compiler_params=pltpu.CompilerParams(dimension_semantics=("arbitrary",), vmem_limit_bytes=VMEM_LIMIT,
                                             has_side_effects=True),
    )(zx, zc, dyl, dz, cw, cb, wa, wx, ba, bx, lam, *[pltpu.with_memory_space_constraint(a, pltpu.HBM)
                                                       for a in chip_sums])


def _mixer_loss(x, tgt, zx, yl, gx, fg, lng, lnb, ws, wst, bst, wout, tm):
    ncht = tm // CHUNK

    def body(x_ref, t_ref, ga_ref, u_ref, v_ref, gb_ref, yl_ref, gx_ref, fg_ref, lng_ref, lnb_ref, ws_ref, wst_ref,
             bst_ref, wout_ref,
             dz_ref, dyl_ref, dxn_ref, y_s, do_ref, dws_ref, dbst_ref, vec_ref,
             vn_s, mix_s, dm_s, dvn_s):
        step = pl.program_id(0)

        @pl.when(step == 0)
        def _():
            dws_ref[...] = jnp.zeros_like(dws_ref)
            dbst_ref[...] = jnp.zeros_like(dbst_ref)
            vec_ref[...] = jnp.zeros_like(vec_ref)

        u, v = u_ref[...], v_ref[...]
        ug, dug_du = _gelu_and_grad(u)
        vg, dvg_dv = _gelu_and_grad(v)
        mu = jnp.mean(vg, axis=-1, keepdims=True)
        vc = vg - mu
        rstd = lax.rsqrt(jnp.mean(vc * vc, axis=-1, keepdims=True) + LN_EPS)
        vhat = vc * rstd
        lngv = lng_ref[...]
        vn_s[...] = (vhat * lngv + lnb_ref[...]).astype(BF16)
        for ch in range(ncht):
            rs = slice(ch * CHUNK, (ch + 1) * CHUNK)
            for g in range(HEADS):
                cs = slice(g * HD, (g + 1) * HD)
                mix_s[rs, cs] = _dot(ws_ref[g], vn_s[rs, cs]) + bst_ref[:, g:g + 1]
        mixed = mix_s[...]
        ga, gb, yl = ga_ref[...], gb_ref[...], yl_ref[...]
        sga, dsga = _silu_and_grad(ga)
        sgb, dsgb = _silu_and_grad(gb)
        ys = ug * mixed
        y_s[:, 0:D] = (yl * sga).astype(BF16)
        y_s[:, D:D_MIX] = (ys * sgb).astype(BF16)
        o = _dot(y_s[...], wout_ref[...])
        gxv, fgv = gx_ref[...], fg_ref[...]
        xn = x_ref[...] + gxv * o
        rs2 = lax.rsqrt(jnp.mean(xn * xn, axis=-1, keepdims=True) + NORM_EPS)
        xh = xn * rs2
        diff = xh * fgv - t_ref[...]
        vec_ref[R_LOSS:R_LOSS + 1, :] += jnp.full((1, D), jnp.sum(diff * diff) * (0.5 / D), F32)
        dout = diff * (1.0 / D)
        w = dout * fgv
        dxn = rs2 * (w - xh * jnp.mean(w * xh, axis=-1, keepdims=True))
        dxn_ref[...] = dxn
        vec_ref[0:1, :] += jnp.sum(dxn * o, axis=0, keepdims=True)
        vec_ref[1:2, :] += jnp.sum(dout * xh, axis=0, keepdims=True)
        dob = (dxn * gxv).astype(BF16)
        do_ref[...] = dob
        dy = _dot_nt(dob, wout_ref[...])
        dya, dyb = dy[:, 0:D], dy[:, D:D_MIX]
        dyl_ref[...] = dya * sga
        dys = dyb * sgb
        dz_ref[:, 0:D] = jnp.zeros((tm, D), BF16)
        dz_ref[:, D:2 * D] = (dya * yl * dsga).astype(BF16)
        dz_ref[:, 2 * D:3 * D] = (dys * mixed * dug_du).astype(BF16)
        dz_ref[:, 4 * D:5 * D] = (dyb * ys * dsgb).astype(BF16)
        dm = dys * ug
        dm_s[...] = dm.astype(BF16)
        for g in range(HEADS):
            cs = slice(g * HD, (g + 1) * HD)
            dbst_ref[:, g:g + 1] += sum(jnp.sum(dm[ch * CHUNK:(ch + 1) * CHUNK, cs], axis=1, keepdims=True)
                                        for ch in range(ncht))
            for ch in range(ncht):
                rs = slice(ch * CHUNK, (ch + 1) * CHUNK)
                dws_ref[g] += _dot_nt(dm_s[rs, cs], vn_s[rs, cs])
                dvn_s[rs, cs] = _dot(wst_ref[g], dm_s[rs, cs])
        dvn = dvn_s[...]
        vec_ref[2:3, :] += jnp.sum(dvn * vhat, axis=0, keepdims=True)
        vec_ref[3:4, :] += jnp.sum(dvn, axis=0, keepdims=True)
        dvh = dvn * lngv
        dvg = rstd * (dvh - jnp.mean(dvh, axis=-1, keepdims=True) - vhat * jnp.mean(dvh * vhat, axis=-1, keepdims=True))
        dz_ref[:, 3 * D:4 * D] = (dvg * dvg_dv).astype(BF16)

    tile = pl.BlockSpec((tm, D), lambda i: (i, 0))
    zcol = lambda n: pl.BlockSpec((tm, D), lambda i: (i, n))
    vec = pl.BlockSpec((1, D), lambda i: (0, 0))
    full = lambda *s: pl.BlockSpec(s, lambda i: (0,) * len(s))
    return _call(
        body, name="mixer_loss", grid=(L // tm,),
        out_shape=[jax.ShapeDtypeStruct((L, D_IN), BF16), jax.ShapeDtypeStruct((L, D), F32),
                   jax.ShapeDtypeStruct((L, D), F32), jax.ShapeDtypeStruct((L, D_MIX), BF16),
                   jax.ShapeDtypeStruct((L, D), BF16),
                   jax.ShapeDtypeStruct((HEADS, CHUNK, CHUNK), F32), jax.ShapeDtypeStruct((CHUNK, HEADS), F32),
                   jax.ShapeDtypeStruct((8, D), F32)],
        in_specs=[tile, tile, zcol(1), zcol(2), zcol(3), zcol(4), tile, pl.BlockSpec((1, D), lambda i: (0, 2)),
                  vec, vec, vec,
                  full(HEADS, CHUNK, CHUNK), full(HEADS, CHUNK, CHUNK), full(CHUNK, HEADS),
                  pl.BlockSpec((D_MIX, D), lambda i: (0, 0), pipeline_mode=pl.Buffered(1))],
        out_specs=[pl.BlockSpec((tm, D_IN), lambda i: (i, 0)), tile, tile,
                   pl.BlockSpec((tm, D_MIX), lambda i: (i, 0)), tile,
                   full(HEADS, CHUNK, CHUNK), full(CHUNK, HEADS), full(8, D)],
        scratch_shapes=[pltpu.VMEM((tm, D), BF16), pltpu.VMEM((tm, D), F32),
                        pltpu.VMEM((tm, D), BF16), pltpu.VMEM((tm, D), F32)],
        compiler_params=_params("arbitrary"),
    )(x, tgt, zx, zx, zx, zx, yl, gx, fg, lng, lnb, ws, wst, bst, wout)


def _grad_w(a, b, a2, b2, tk, name, bw=D, first=0, nblocks=None):
    nk = a.shape[0] // tk
    m = a.shape[1]
    nblocks = nblocks or b.shape[1] // bw
    with_ctx = a2 is not None

    def body(*refs):
        if with_ctx:
            a_ref, b_ref, a2_ref, b2_ref, o_ref, acc = refs
        else:
            a_ref, b_ref, o_ref, acc = refs
        n, k = pl.program_id(0), pl.program_id(1)

        @pl.when(k == 0)
        def _():
            acc[...] = jnp.zeros_like(acc)

        acc[...] += _dot_tn(a_ref[...], b_ref[...])

        if with_ctx:
            @pl.when(jnp.logical_and(k == nk - 1, n == 0))
            def _():
                acc[:, 0:b2_ref.shape[1]] += _dot_tn(a2_ref[...], b2_ref[...])

        @pl.when(k == nk - 1)
        def _():
            o_ref[...] = acc[...].astype(BF16)

    in_specs = [pl.BlockSpec((tk, m), lambda n, k: (k, 0)), pl.BlockSpec((tk, bw), lambda n, k: (k, n + first))]
    args = [a, b]
    if with_ctx:
        in_specs += [pl.BlockSpec(a2.shape, lambda n, k: (0, 0)), pl.BlockSpec(b2.shape, lambda n, k: (0, 0))]
        args += [a2, b2]
    return _call(
        body, name=name, grid=(nblocks, nk), out_shape=jax.ShapeDtypeStruct((m, nblocks * bw), BF16),
        in_specs=in_specs, out_specs=pl.BlockSpec((m, bw), lambda n, k: (0, n)),
        scratch_shapes=[pltpu.VMEM((m, bw), F32)],
        compiler_params=_params("arbitrary", "arbitrary"),
    )(*args)


def _grad_rows(xr, dz, w, mod, ng, dres, ncols, tm, name, chip_sums=(), first_chips=None, dests=None):
    rows = xr.shape[0]
    steps = rows // tm
    with_dx = dres is not None
    nr = len(chip_sums)
    dests = [d for d in (dests or [None] * nr)]
    nd = sum(d is not None for d in dests)
    nin = 6 if with_dx else 5
    nout = 2 if with_dx else 1

    def body(*refs):
        if with_dx:
            x_ref, dz_ref, w_ref, sc_ref, ng_ref, dres_ref = refs[:nin]
            dx_ref, vec_ref = refs[nin + nr + nd:nin + nr + nd + nout]
        else:
            x_ref, dz_ref, w_ref, sc_ref, ng_ref = refs[:nin]
            (vec_ref,) = refs[nin + nr + nd:nin + nr + nd + nout]
        if nr:
            o0 = nin + nr + nd + nout
            start, finish = _chips_ops(refs[nin:nin + nr], refs[o0:o0 + nr], *refs[o0 + nr:], first_chips=first_chips)
            pl.when(pl.program_id(0) == 0)(start)
            pl.when(pl.program_id(0) == steps - 1)(finish)

        @pl.when(pl.program_id(0) == 0)
        def _():
            vec_ref[...] = jnp.zeros_like(vec_ref)

        dhn = _dot_nt(dz_ref[...], w_ref[...])
        x = x_ref[...]
        rs = lax.rsqrt(jnp.mean(x * x, axis=-1, keepdims=True) + NORM_EPS)
        xh = x * rs
        ngv = ng_ref[...]
        y = xh * ngv
        vec_ref[0:1, :] += jnp.sum(dhn, axis=0, keepdims=True)
        vec_ref[1:2, :] += jnp.sum(dhn * y, axis=0, keepdims=True)
        dy = dhn * (1.0 + sc_ref[...])
        vec_ref[2:3, :] += jnp.sum(dy * xh, axis=0, keepdims=True)
        if with_dx:
            dxh = dy * ngv
            dx_ref[...] = dres_ref[...] + rs * (dxh - xh * jnp.mean(dxh * xh, axis=-1, keepdims=True))

    tile = pl.BlockSpec((tm, D), lambda i: (i, 0))
    vec = pl.BlockSpec((1, D), lambda i: (0, 0))
    in_specs = [tile, pl.BlockSpec((tm, ncols), lambda i: (i, 0)),
                pl.BlockSpec((D, ncols), lambda i: (0, 0), pipeline_mode=pl.Buffered(1)),
                pl.BlockSpec((1, D), lambda i: (0, 1)), vec]
    out_shape = [jax.ShapeDtypeStruct((8, D), F32)]
    out_specs = [pl.BlockSpec((8, D), lambda i: (0, 0))]
    args = [xr, dz, w, mod, ng]
    if with_dx:
        in_specs.append(tile)
        out_shape.insert(0, jax.ShapeDtypeStruct((rows, D), F32))
        out_specs.insert(0, tile)
        args.append(dres)
    aliases = {}
    for j, d in enumerate(dests):
        if d is not None:
            aliases[len(args) + nr + len(aliases)] = len(out_shape) + j
    in_specs += [HBM] * (nr + nd)
    out_specs += [HBM] * nr
    out_shape += [jax.ShapeDtypeStruct((4,) + a.shape[1:], a.dtype) for a in chip_sums]
    args += [pltpu.with_memory_space_constraint(a, pltpu.HBM) for a in chip_sums]
    args += [pltpu.with_memory_space_constraint(d, pltpu.HBM) for d in dests if d is not None]
    return _call(body, name=name, grid=(steps,), out_shape=out_shape, in_specs=in_specs, out_specs=out_specs,
                 scratch_shapes=_chips_sems(nr) if nr else [], input_output_aliases=aliases,
                 compiler_params=pltpu.CompilerParams(dimension_semantics=("arbitrary",),
                                                      vmem_limit_bytes=VMEM_LIMIT, has_side_effects=bool(nr)))(*args)


def _adamw(w, g, m, v):
    m = ADAM_B1 * m + (1.0 - ADAM_B1) * g
    v = ADAM_B2 * v + (1.0 - ADAM_B2) * (g * g)
    m_hat = m / (1.0 - ADAM_B1 ** ADAM_STEP)
    v_hat = v / (1.0 - ADAM_B2 ** ADAM_STEP)
    delta = -ADAM_LR * (m_hat / (jnp.sqrt(v_hat) + ADAM_EPS) + ADAM_WD * w)
    return delta, m, v


def _adamw_reduced(parts, w, m, v, tr, name):
    r, n = w.shape
    nparts = parts.shape[0]

    def body(p_ref, w_ref, m_ref, v_ref, g_ref, d_ref, mo_ref, vo_ref):
        g = p_ref[0].astype(F32)
        for i in range(1, nparts):
            g = g + p_ref[i].astype(F32)
        g_ref[...] = g
        d_ref[...], mo_ref[...], vo_ref[...] = _adamw(w_ref[...], g, m_ref[...], v_ref[...])

    tile = pl.BlockSpec((tr, n), lambda i: (i, 0))
    sds = jax.ShapeDtypeStruct((r, n), F32)
    return _call(
        body, name=name, grid=(r // tr,), out_shape=[sds] * 4,
        in_specs=[pl.BlockSpec((nparts, tr, n), lambda i: (0, i, 0)), tile, tile, tile], out_specs=[tile] * 4,
        compiler_params=_params("arbitrary"),
    )(parts, w, m, v)


R_GATE, R_FINAL_G, R_LN_G, R_LN_B, R_LOSS = 0, 1, 2, 3, 4
R_SH_X, R_SC_X, R_NG_X = 5, 6, 7
R_SH_C, R_SC_C, R_NG_C = 8, 9, 10
R_BA, R_BX, R_LAM, R_CW, R_CB, R_SGU_B = 11, 13, 15, 17, 21, 22
PACK_ROWS = 32
MAT_ROWS = 2 * (2 * HEADS * HD) + HEADS * CHUNK


def _reduce_small(vp_all, mat_parts, ada_w, me):
    nloc = ada_w.shape[1]

    def body(me_ref, vp_ref, mp_ref, w_ref, red_ref, mat_ref, dmod_ref, gab_ref, cpart_ref, dmc_s):
        red = vp_ref[0]
        for i in range(1, N_DEV):
            red = red + vp_ref[i]
        mat = mp_ref[0]
        for i in range(1, mp_ref.shape[0]):
            mat = mat + mp_ref[i]
        red_ref[...] = red
        mat_ref[...] = mat
        for e in range(N_DEV):
            dmod_ref[e:e + 1, 0:D] = vp_ref[e, R_SH_X:R_SH_X + 1, :]
            dmod_ref[e:e + 1, D:2 * D] = vp_ref[e, R_SC_X:R_SC_X + 1, :]
            dmod_ref[e:e + 1, 2 * D:3 * D] = vp_ref[e, R_GATE:R_GATE + 1, :]
        dmod_ref[8:9, 0:D] = red[R_SH_C:R_SH_C + 1, :]
        dmod_ref[8:9, D:2 * D] = red[R_SC_C:R_SC_C + 1, :]
        dmod_ref[8:9, 2 * D:3 * D] = jnp.zeros((1, D), F32)
        dmod_ref[9:16, :] = jnp.zeros((7, 3 * D), F32)
        gab_ref[:, 0:D] = red[R_SH_X:R_SH_X + 1, :] + red[R_SH_C:R_SH_C + 1, :]
        gab_ref[:, D:2 * D] = red[R_SC_X:R_SC_X + 1, :] + red[R_SC_C:R_SC_C + 1, :]
        gab_ref[:, 2 * D:3 * D] = red[R_GATE:R_GATE + 1, :]
        dmc_s[...] = jnp.broadcast_to(dmod_ref[8:9, :], (8, 3 * D))
        off = pl.multiple_of(me_ref[0] * nloc, 128)
        cpart_ref[...] = _dot_nt(dmc_s[:, pl.ds(off, nloc)], w_ref[...])

    return _call(
        body, name="reduce_small",
        out_shape=[jax.ShapeDtypeStruct((PACK_ROWS, D), F32), jax.ShapeDtypeStruct(mat_parts.shape[1:], F32),
                   jax.ShapeDtypeStruct((16, 3 * D), F32), jax.ShapeDtypeStruct((1, 3 * D), F32),
                   jax.ShapeDtypeStruct((8, D), F32)],
        in_specs=[pl.BlockSpec(memory_space=pltpu.SMEM), VMEM, VMEM, VMEM], out_specs=[VMEM] * 5,
        scratch_shapes=[pltpu.VMEM((8, 3 * D), F32)], compiler_params=_params(),
    )(me, vp_all, mat_parts, ada_w)


def _adamw_ada(c_all, c_ctx, dmod, w, m, v, me):
    nloc = w.shape[1]

    def body(me_ref, c_ref, cc_ref, dm_ref, w_ref, m_ref, v_ref, g_ref, d_ref, mo_ref, vo_ref):
        off = pl.multiple_of(me_ref[0] * nloc, 128)
        dm = dm_ref[:, pl.ds(off, nloc)]
        sx, _ = _silu_and_grad(c_ref[...])
        sc, _ = _silu_and_grad(cc_ref[...])
        g = _dot_tn(sx, dm[0:8, :]) + _dot_tn(jnp.broadcast_to(sc, (8, D)), dm[8:16, :])
        g_ref[...] = g
        d_ref[...], mo_ref[...], vo_ref[...] = _adamw(w_ref[...], g, m_ref[...], v_ref[...])

    sds = jax.ShapeDtypeStruct(w.shape, F32)
    return _call(
        body, name="adamw_ada_w", out_shape=[sds] * 4,
        in_specs=[pl.BlockSpec(memory_space=pltpu.SMEM)] + [VMEM] * 6, out_specs=[VMEM] * 4,
        compiler_params=_params(),
    )(me, c_all, c_ctx, dmod, w, m, v)


_SMALL = ("c_ctx", "ada_b", "norm_g", "conv_w", "conv_b", "lru_wa", "lru_ba", "lru_wx", "lru_bx", "lru_lambda",
          "sgu_ln_g", "sgu_ln_b", "sgu_w", "sgu_b", "final_g")


def _adamw_small(red, mat, cparts, gab, ws, ms, vs, me):
    n = len(_SMALL)
    nw = 2 * HEADS * HD

    def body(me_ref, red_ref, mat_ref, cp_ref, gab_ref, *refs):
        w_refs, m_refs, v_refs = refs[:n], refs[n:2 * n], refs[2 * n:3 * n]
        outs = refs[3 * n:]
        off = pl.multiple_of(me_ref[0] * HD, 128)

        def row(r, k=1):
            return red_ref[r:r + k, :]

        cc = w_refs[0][...]
        dcc = cp_ref[0, 0:1, :]
        for i in range(1, N_DEV):
            dcc = dcc + cp_ref[i, 0:1, :]
        grads = dict(
            c_ctx=dcc * _silu_and_grad(cc)[1], ada_b=gab_ref[...], norm_g=row(R_NG_X) + row(R_NG_C),
            conv_w=red_ref[R_CW:R_CW + CONV_W, pl.ds(off, HD)], conv_b=row(R_CB),
            lru_wa=mat_ref[0:nw, :], lru_ba=row(R_BA, 2), lru_wx=mat_ref[nw:2 * nw, :], lru_bx=row(R_BX, 2),
            lru_lambda=red_ref[R_LAM:R_LAM + 2, pl.ds(off, HD)], sgu_ln_g=row(R_LN_G), sgu_ln_b=row(R_LN_B),
            sgu_w=mat_ref[2 * nw:MAT_ROWS, :], sgu_b=row(R_SGU_B), final_g=row(R_FINAL_G))
        for j, name in enumerate(_SMALL):
            g = grads[name]
            outs[j][...] = g
            outs[n + j][...], outs[2 * n + j][...], outs[3 * n + j][...] = _adamw(w_refs[j][...], g, m_refs[j][...],
                                                                                 v_refs[j][...])

    sds = [jax.ShapeDtypeStruct(ws[k].shape, F32) for k in _SMALL]
    outs = _call(
        body, name="adamw_small", out_shape=sds * 4,
        in_specs=[pl.BlockSpec(memory_space=pltpu.SMEM)] + [VMEM] * (4 + 3 * n), out_specs=[VMEM] * (4 * n),
        compiler_params=_params(),
    )(me, red, mat, cparts, gab, *[ws[k] for k in _SMALL], *[ms[k] for k in _SMALL], *[vs[k] for k in _SMALL])
    return [dict(zip(_SMALL, outs[i * n:(i + 1) * n])) for i in range(4)]


def kernel(x, c, ctx, c_ctx, ada_w, ada_b, norm_g, w_in, conv_w, conv_b, lru_wa, lru_ba, lru_wx, lru_bx, lru_lambda, sgu_ln_g, sgu_ln_b, sgu_w, sgu_b, w_out, final_g, loss_target, m_c_ctx, m_ada_w, m_ada_b, m_norm_g, m_w_in, m_conv_w, m_conv_b, m_lru_wa, m_lru_ba, m_lru_wx, m_lru_bx, m_lru_lambda, m_sgu_ln_g, m_sgu_ln_b, m_sgu_w, m_sgu_b, m_w_out, m_final_g, v_c_ctx, v_ada_w, v_ada_b, v_norm_g, v_w_in, v_conv_w, v_conv_b, v_lru_wa, v_lru_ba, v_lru_wx, v_lru_bx, v_lru_lambda, v_sgu_ln_g, v_sgu_ln_b, v_sgu_w, v_sgu_b, v_w_out, v_final_g):
    args = dict(locals())
    me_s = 4 * lax.axis_index("x") + 2 * lax.axis_index("y") + lax.axis_index("c")
    me = me_s.astype(jnp.int32).reshape(1)
    xr, ctxr, tgt = x[0], ctx[0], loss_target[0]
    cc = c_ctx.reshape(1, D)
    nw = 2 * HEADS * HD
    view = dict(c_ctx=(1, D), ada_b=(1, 3 * D), norm_g=(1, D), conv_w=(CONV_W, HD), conv_b=(1, D), lru_wa=(nw, HD),
                lru_ba=(2, D), lru_wx=(nw, HD), lru_bx=(2, D), lru_lambda=(2, HD), sgu_ln_g=(1, D), sgu_ln_b=(1, D),
                sgu_w=(HEADS * CHUNK, CHUNK), sgu_b=(1, D), final_g=(1, D))

    (c_all,) = _gather2([c], ["ag"], "gather_c")
    c_all = c_all.reshape(N_DEV, D)
    part = _ada_forward(c_all, cc, ada_w[0], ada_b, me)
    (parts,) = _gather2([part], ["ag"], "gather_mod")
    modx = lax.dynamic_index_in_dim(parts, me_s, axis=1, keepdims=False).reshape(1, 3 * D)
    modc = parts[:, 8, :].reshape(1, 3 * D)

    w_in_b, w_out_b = _cast_weights(w_in[0], w_out[0])
    (w_full,) = _gather2([w_in_b], ["agc"], "gather_w_in")

    zx, hn, wout_all, cw_full, lam_full = _project(
        xr, modx, norm_g, w_full, D_IN, 256, "project_x", gather=[w_out_b, conv_w[0], lru_lambda[0]],
        gather_modes=["ag", "agc", "agc"])
    wout_full = wout_all.reshape(D_MIX, D)
    zc, hnc = _project(ctxr, modc, norm_g, w_full, D, LC, "project_ctx")
    ba, bx = lru_ba.reshape(2, D), lru_bx.reshape(2, D)
    yl = _lru_forward(zx, zc, cw_full, conv_b, lru_wa[0], lru_wx[0], ba, bx, lam_full)
    ws_b = sgu_w[0].astype(BF16)
    dz, dyl, dxn, ycat, dob, dws, dbst, mvec = _mixer_loss(
        xr, tgt, zx, yl, modx, final_g.reshape(1, D), sgu_ln_g, sgu_ln_b, ws_b, jnp.swapaxes(ws_b, 1, 2),
        sgu_b[0].T, wout_full, 128)

    gw_out = _grad_w(ycat, dob, None, None, 512, "grad_w_out")
    gw_rest = _grad_w(hn, dz, None, None, 512, "grad_w_in_rest", bw=2 * W_IN_SHARD, first=1, nblocks=3)
    rest_sums, wout_sums = _reduce2_local([gw_rest, gw_out.reshape(N_DEV, D_MIX // N_DEV, D)], ["a2ac", "a2a"], me,
                                          "reduce_early", counts=[3, 4])
    dz, dxac, dwa, dwx, dba, dbx, dlam, dcw, dcb, win_parts, wout_parts = _lru_backward(
        zx, zc, dyl, dz, cw_full, conv_b, lru_wa[0], lru_wx[0], ba, bx, lam_full, [rest_sums, wout_sums],
        first_chips=[1, 0])
    gw_first = _grad_w(hn, dz, hnc, dxac, 512, "grad_w_in_first", bw=2 * W_IN_SHARD, first=0, nblocks=1)
    matpack = jnp.concatenate([dwa.reshape(nw, HD), dwx.reshape(nw, HD), dws.reshape(HEADS * CHUNK, CHUNK)], axis=0)
    first_sums, mat_sums = _reduce2_local([gw_first, matpack.reshape(N_DEV, MAT_ROWS // N_DEV, HD)], ["a2ac", "a2a"],
                                          me, "reduce_late", counts=[1, 4])
    gx, xvec, win_parts, mat_parts = _grad_rows(
        xr, dz, w_full, modx, norm_g, dxn, D_IN, 256, "grad_rows_x", chip_sums=[first_sums, mat_sums],
        first_chips=[0, 0], dests=[win_parts, None])
    (cvec,) = _grad_rows(ctxr, dxac, w_full, modc, norm_g, None, D, LC, "grad_rows_ctx")
    pack = jnp.concatenate([mvec[0:5], xvec[0:3], cvec[0:3], dba, dbx, dlam, dcw, dcb, dbst.T.reshape(1, D),
                            jnp.zeros((PACK_ROWS - R_SGU_B - 1, D), F32)], axis=0)
    (vp_all,) = _gather2([pack], ["ag"], "gather_pack")
    red, matpiece, dmod, gab, cpart = _reduce_small(vp_all, mat_parts, ada_w[0], me)
    mat_all, cparts = _gather2([matpiece, cpart], ["ag", "ag"], "gather_small")

    g_w_in, d_w_in, nm_w_in, nv_w_in = _adamw_reduced(win_parts, w_in[0], m_w_in[0], v_w_in[0], 256, "adamw_w_in")
    g_w_out, d_w_out, nm_w_out, nv_w_out = _adamw_reduced(wout_parts, w_out[0], m_w_out[0], v_w_out[0], 128,
                                                          "adamw_w_out")
    g_ada, d_ada, nm_ada, nv_ada = _adamw_ada(c_all, cc, dmod, ada_w[0], m_ada_w[0], v_ada_w[0], me)
    ws = {k: args[k].reshape(view[k]) for k in _SMALL}
    ms = {k: args["m_" + k].reshape(view[k]) for k in _SMALL}
    vs = {k: args["v_" + k].reshape(view[k]) for k in _SMALL}
    small = _adamw_small(red, mat_all.reshape(MAT_ROWS, HD), cparts, gab, ws, ms, vs, me)
    big = dict(w_in=(g_w_in, d_w_in, nm_w_in, nv_w_in), w_out=(g_w_out, d_w_out, nm_w_out, nv_w_out),
               ada_w=(g_ada, d_ada, nm_ada, nv_ada))

    loss = red[R_LOSS, 0]
    names = ("c_ctx", "ada_w", "ada_b", "norm_g", "w_in", "conv_w", "conv_b", "lru_wa", "lru_ba", "lru_wx", "lru_bx",
             "lru_lambda", "sgu_ln_g", "sgu_ln_b", "sgu_w", "sgu_b", "w_out", "final_g")
    outs = [loss, gx.reshape(x.shape)]
    for kind in range(4):
        for k in names:
            val = big[k][kind] if k in big else small[kind][k]
            outs.append(val.reshape(args[k].shape))
    return tuple(outs)
```

```python
import functools

import jax
import jax.numpy as jnp
from jax import lax
from jax.experimental import pallas as pl
from jax.experimental.pallas import tpu as pltpu

F32 = jnp.float32
BF16 = jnp.bfloat16

N_DEV = 8
D = 1024
L = 2048
LC = 256
HEADS = 8
HD = 128
CHUNK = 128
D_IN = 5 * D
W_IN_SHARD = D_IN // N_DEV
D_MIX = 2 * D
CONV_W = 4
LRU_C = 8.0
NORM_EPS = 1e-6
LN_EPS = 1e-5
ADAM_LR, ADAM_B1, ADAM_B2, ADAM_EPS, ADAM_WD, ADAM_STEP = 0.001, 0.9, 0.999, 1e-08, 0.01, 10

VMEM_LIMIT = 56 * 1024 * 1024

HBM = pl.BlockSpec(memory_space=pltpu.HBM)
VMEM = pl.BlockSpec(memory_space=pltpu.VMEM)
MESH = pl.DeviceIdType.MESH


def _call(body, **kw):
    return pl.pallas_call(body, **kw)


def _params(*sem):
    return pltpu.CompilerParams(dimension_semantics=sem, vmem_limit_bytes=VMEM_LIMIT)


def _sigmoid(x):
    return 0.5 * jnp.tanh(0.5 * x) + 0.5


def _silu_and_grad(x):
    s = _sigmoid(x)
    return x * s, s * (1.0 + x * (1.0 - s))


_G0 = 0.7978845608028654
_G1 = 0.044715


def _gelu_and_grad(x):
    x2 = x * x
    t = jnp.tanh(_G0 * (x + _G1 * x * x2))
    cdf = 0.5 * (1.0 + t)
    return x * cdf, cdf + 0.5 * x * (1.0 - t * t) * (_G0 * (1.0 + 3.0 * _G1 * x2))


def _gelu(x):
    return 0.5 * x * (1.0 + jnp.tanh(_G0 * (x + _G1 * x * x * x)))


def _softplus(z):
    t = jnp.exp(-jnp.abs(z))
    u = 1.0 + t
    log1p = jnp.where(u == 1.0, t, jnp.log(u) * t / jnp.where(u == 1.0, 1.0, u - 1.0))
    return jnp.maximum(z, 0.0) + log1p


def _dot(a, b):
    return jnp.dot(a, b, preferred_element_type=F32)


def _dot_nt(a, b):
    return lax.dot_general(a, b, (((1,), (1,)), ((), ())), preferred_element_type=F32)


def _dot_tn(a, b):
    return lax.dot_general(a, b, (((0,), (0,)), ((), ())), preferred_element_type=F32)


def _rows(shape):
    return lax.broadcasted_iota(jnp.int32, shape, 0)


def _shift_down(x, first):
    return jnp.where(_rows(x.shape) == 0, first, pltpu.roll(x, 1, 0))


def _shift_up(x, last):
    n = x.shape[0]
    return jnp.where(_rows(x.shape) == n - 1, last, pltpu.roll(x, n - 1, 0))


def _gather2(arrays, modes, name):
    n = len(arrays)

    def body(*refs):
        start, forward, finish = _gather2_ops(refs[:n], refs[n:2 * n], modes, *refs[2 * n:])
        start()
        forward()
        finish()

    return _call(
        body, name=name, out_shape=_gather2_shapes(arrays, modes), in_specs=[HBM] * n, out_specs=[HBM] * n,
        scratch_shapes=_gather2_sems(n), compiler_params=pltpu.CompilerParams(has_side_effects=True),
    )(*[pltpu.with_memory_space_constraint(a, pltpu.HBM) for a in arrays])


def _gather2_shapes(arrays, modes):
    return [jax.ShapeDtypeStruct((N_DEV,) + a.shape if m == "ag" else (a.shape[0], N_DEV * a.shape[1]), a.dtype)
            for a, m in zip(arrays, modes)]


def _gather2_sems(n):
    return [pltpu.SemaphoreType.DMA((n, N_DEV - 1)), pltpu.SemaphoreType.DMA((n, N_DEV - 1)),
            pltpu.SemaphoreType.DMA((n,))]


def _gather2_ops(ins, outs, modes, send_sems, recv_sems, local_sems):
    n = len(ins)
    x, y, c = lax.axis_index("x"), lax.axis_index("y"), lax.axis_index("c")
    me, sibling = (x, y, c), (x, y, 1 - c)
    chips = [(x ^ (k >> 1), y ^ (k & 1)) for k in (1, 2, 3)]

    def slot(j, px, py, pc):
        dev = 4 * px + 2 * py + pc
        if modes[j] == "agc":
            w = ins[j].shape[1]
            return outs[j].at[:, pl.ds(pl.multiple_of(dev * w, 128), w)]
        return outs[j].at[dev]

    def copy(j, k, block, to, src=None):
        return pltpu.make_async_remote_copy(
            src_ref=slot(j, *block) if src is None else src, dst_ref=slot(j, *block),
            send_sem=send_sems.at[j, k], recv_sem=recv_sems.at[j, k], device_id=to, device_id_type=MESH)

    def own(j):
        return pltpu.make_async_copy(ins[j], slot(j, *me), local_sems.at[j])

    def first(j):
        return [copy(j, 0, me, sibling, src=ins[j])] + [copy(j, 1 + i, me, (*chip, c), src=ins[j])
                                                        for i, chip in enumerate(chips)]

    def passed(j, i):
        return copy(j, 4 + i, (*chips[i], c), sibling)

    def start():
        for j in range(n):
            own(j).start()
            for cp in first(j):
                cp.start()

    def forward():
        for i, chip in enumerate(chips):
            for j in range(n):
                copy(j, 1 + i, (*chip, c), me).wait_recv()
                passed(j, i).start()

    def finish():
        for j in range(n):
            copy(j, 0, sibling, me).wait_recv()
            for i, chip in enumerate(chips):
                copy(j, 4 + i, (*chip, 1 - c), me).wait_recv()
            for cp in first(j) + [passed(j, i) for i in range(3)]:
                cp.wait_send()
            own(j).wait()

    return start, forward, finish


def _reduce2_local(arrays, modes, me, name, counts=None):
    n = len(arrays)
    counts = counts or [4] * n
    shapes = [(a.shape[1], a.shape[2]) if m == "a2a" else (a.shape[0], a.shape[1] // (2 * cnt))
              for a, m, cnt in zip(arrays, modes, counts)]
    staged = [jax.ShapeDtypeStruct((cnt,) + s, a.dtype) for s, a, cnt in zip(shapes, arrays, counts)]

    def piece(ref, mode, dev, w):
        return ref.at[dev] if mode == "a2a" else ref.at[:, pl.ds(pl.multiple_of(dev * w, 128), w)]

    def to_sibling(*refs):
        ins, outs = refs[:n], refs[n:2 * n]
        send_sems, recv_sems = refs[2 * n:]
        x, y, c = lax.axis_index("x"), lax.axis_index("y"), lax.axis_index("c")
        copies = []
        for j in range(n):
            for q in range(counts[j]):
                cp = pltpu.make_async_remote_copy(
                    src_ref=piece(ins[j], modes[j], 2 * q + (1 - c), shapes[j][1]), dst_ref=outs[j].at[q],
                    send_sem=send_sems.at[j, q], recv_sem=recv_sems.at[j, q], device_id=(x, y, 1 - c),
                    device_id_type=MESH)
                cp.start()
                copies.append(cp)
        for cp in copies:
            cp.wait()

    stage = _call(
        to_sibling, name=name + "_d2d", out_shape=staged, in_specs=[HBM] * n, out_specs=[HBM] * n,
        scratch_shapes=[pltpu.SemaphoreType.DMA((n, 4)), pltpu.SemaphoreType.DMA((n, 4))],
        compiler_params=pltpu.CompilerParams(has_side_effects=True),
    )(*[pltpu.with_memory_space_constraint(a, pltpu.HBM) for a in arrays])

    def add(me_ref, *refs):
        del me_ref
        own, got, outs = refs[:n], refs[n:2 * n], refs[2 * n:]
        for j in range(n):
            mine = own[j][0] if modes[j] == "a2a" else own[j][...]
            outs[j][0] = (mine.astype(F32) + got[j][0].astype(F32)).astype(outs[j].dtype)

    in_specs, slot_specs = [], []
    for (r, w), m, cnt in zip(shapes, modes, counts):
        if m == "a2a":
            in_specs.append(pl.BlockSpec(
                (1, r, w), lambda q, me_ref, cnt=cnt: (2 * jnp.minimum(q, cnt - 1) + me_ref[0] % 2, 0, 0)))
        else:
            in_specs.append(pl.BlockSpec(
                (r, w), lambda q, me_ref, cnt=cnt: (0, 2 * jnp.minimum(q, cnt - 1) + me_ref[0] % 2)))
        slot_specs.append(pl.BlockSpec((1, r, w), lambda q, me_ref, cnt=cnt: (jnp.minimum(q, cnt - 1), 0, 0)))
    return _call(
        add, name=name + "_add", out_shape=staged,
        grid_spec=pltpu.PrefetchScalarGridSpec(num_scalar_prefetch=1, grid=(max(counts),),
                                               in_specs=in_specs + slot_specs, out_specs=slot_specs),
        compiler_params=_params("arbitrary"),
    )(me, *arrays, *stage)


def _chips_sems(n):
    return [pltpu.SemaphoreType.DMA((n, 3)), pltpu.SemaphoreType.DMA((n, 3)), pltpu.SemaphoreType.DMA((n,))]


def _chips_ops(ins, outs, send_sems, recv_sems, local_sems, first_chips=None):
    x, y, c = lax.axis_index("x"), lax.axis_index("y"), lax.axis_index("c")
    qm = 2 * x + y
    first_chips = first_chips or [0] * len(ins)

    def owns(j, chip):
        lo, cnt = first_chips[j], ins[j].shape[0]
        if lo == 0 and cnt == 4:
            return None
        return jnp.logical_and(chip >= lo, chip < lo + cnt)

    def guarded(cond, fn):
        if cond is None:
            fn()
        else:
            pl.when(cond)(fn)

    def remote(j, k):
        px, py = x ^ (k >> 1), y ^ (k & 1)
        return pltpu.make_async_remote_copy(
            src_ref=ins[j].at[slot(j, 2 * px + py)], dst_ref=outs[j].at[qm], send_sem=send_sems.at[j, k - 1],
            recv_sem=recv_sems.at[j, k - 1], device_id=(px, py, c), device_id_type=MESH)

    def slot(j, chip):
        return jnp.clip(chip - first_chips[j], 0, ins[j].shape[0] - 1)

    def local(j):
        return pltpu.make_async_copy(ins[j].at[slot(j, qm)], outs[j].at[qm], local_sems.at[j])

    def start():
        for j in range(len(ins)):
            for k in (1, 2, 3):
                guarded(owns(j, qm ^ k), lambda j=j, k=k: remote(j, k).start())
            guarded(owns(j, qm), lambda j=j: local(j).start())

    def finish():
        for j in range(len(ins)):
            for k in (1, 2, 3):
                guarded(owns(j, qm ^ k), lambda j=j, k=k: remote(j, k).wait_send())
                guarded(owns(j, qm), lambda j=j, k=k: remote(j, k).wait_recv())
            guarded(owns(j, qm), lambda j=j: local(j).wait())

    return start, finish


def _front(c, c_ctx, ada_w, ada_b, w_in, w_out, me):
    nloc = ada_w.shape[1]

    def body(me_ref, c_ref, cc_ref, aw_ref, ab_ref, win_ref, wout_ref,
             wfull_ref, woutb_ref, modx_ref, modc_ref, call_ref,
             wb_s, part_s, parts_s, w_send, w_recv, w_local, s_send, s_recv):
        x, y, cidx = lax.axis_index("x"), lax.axis_index("y"), lax.axis_index("c")
        me = me_ref[0]
        wb_s[...] = win_ref[...].astype(BF16)
        woutb_ref[...] = wout_ref[...].astype(BF16)
        start, forward, finish = _gather2_ops([wb_s], [wfull_ref], ["agc"], w_send, w_recv, w_local)
        start()

        def small_gather(src, my_slot, stage):
            copies = []
            for k in range(1, N_DEV):
                peer = (x ^ (k >> 2), y ^ ((k >> 1) & 1), cidx ^ (k & 1))
                cp = pltpu.make_async_remote_copy(src_ref=src, dst_ref=my_slot, send_sem=s_send.at[stage, k - 1],
                                                  recv_sem=s_recv.at[stage, k - 1], device_id=peer,
                                                  device_id_type=MESH)
                cp.start()
                copies.append(cp)
            pltpu.sync_copy(src, my_slot)
            for cp in copies:
                cp.wait()

        small_gather(c_ref, call_ref.at[pl.ds(me, 1), :], 0)
        off = pl.multiple_of(me * nloc, 128)
        b = ab_ref[:, pl.ds(off, nloc)]
        w = aw_ref[...]
        sx, _ = _silu_and_grad(call_ref[...])
        sc, _ = _silu_and_grad(jnp.broadcast_to(cc_ref[...], (8, D)))
        part_s[0:8, :] = _dot(sx, w) + b
        part_s[8:16, :] = _dot(sc, w) + b
        small_gather(part_s, parts_s.at[me], 1)
        mine = _rows((16, nloc)) == me
        for j in range(N_DEV):
            pj = parts_s[j]
            modx_ref[:, j * nloc:(j + 1) * nloc] = jnp.sum(jnp.where(mine, pj, 0.0), axis=0, keepdims=True)
            modc_ref[:, j * nloc:(j + 1) * nloc] = pj[8:9, :]
        forward()
        finish()

    return _call(
        body, name="front",
        out_shape=[jax.ShapeDtypeStruct((D, D_IN), BF16), jax.ShapeDtypeStruct(w_out.shape, BF16),
                   jax.ShapeDtypeStruct((1, 3 * D), F32), jax.ShapeDtypeStruct((1, 3 * D), F32),
                   jax.ShapeDtypeStruct((N_DEV, D), F32)],
        in_specs=[pl.BlockSpec(memory_space=pltpu.SMEM)] + [VMEM] * 6, out_specs=[HBM, VMEM, VMEM, VMEM, VMEM],
        scratch_shapes=[pltpu.VMEM(w_in.shape, BF16), pltpu.VMEM((16, nloc), F32),
                        pltpu.VMEM((N_DEV, 16, nloc), F32)] + _gather2_sems(1) +
                       [pltpu.SemaphoreType.DMA((2, N_DEV - 1)), pltpu.SemaphoreType.DMA((2, N_DEV - 1))],
        compiler_params=pltpu.CompilerParams(vmem_limit_bytes=VMEM_LIMIT, has_side_effects=True),
    )(me, c, c_ctx, ada_w, ada_b, w_in, w_out)


def _project(xr, mod, ng, w, ncols, tm, name, gather=None, gather_modes=()):
    rows = xr.shape[0]
    steps = rows // tm
    ng_ = len(gather or ())

    def body(x_ref, sh_ref, sc_ref, ng_ref, w_ref, *rest):
        z_ref, hn_ref = rest[ng_:ng_ + 2]
        if ng_:
            start, forward, finish = _gather2_ops(rest[:ng_], rest[ng_ + 2:2 * ng_ + 2], gather_modes,
                                                  *rest[2 * ng_ + 2:])
            pl.when(pl.program_id(0) == 0)(start)
            pl.when(pl.program_id(0) == steps // 2)(forward)
        x = x_ref[...]
        rs = lax.rsqrt(jnp.mean(x * x, axis=-1, keepdims=True) + NORM_EPS)
        hn = (x * rs * ng_ref[...]) * (1.0 + sc_ref[...]) + sh_ref[...]
        hb = hn.astype(BF16)
        hn_ref[...] = hb
        for n in range(ncols // D):
            z_ref[:, n * D:(n + 1) * D] = _dot(hb, w_ref[:, n * D:(n + 1) * D])
        if ng_:
            pl.when(pl.program_id(0) == steps - 1)(finish)

    vec = pl.BlockSpec((1, D), lambda i: (0, 0))
    gathered = _gather2_shapes(gather, gather_modes) if ng_ else []
    return _call(
        body, name=name, grid=(steps,),
        out_shape=[jax.ShapeDtypeStruct((rows, ncols), F32), jax.ShapeDtypeStruct((rows, D), BF16)] + gathered,
        in_specs=[pl.BlockSpec((tm, D), lambda i: (i, 0)), vec, pl.BlockSpec((1, D), lambda i: (0, 1)), vec,
                  pl.BlockSpec((D, ncols), lambda i: (0, 0), pipeline_mode=pl.Buffered(1))] + [HBM] * ng_,
        out_specs=[pl.BlockSpec((tm, ncols), lambda i: (i, 0)), pl.BlockSpec((tm, D), lambda i: (i, 0))] + [HBM] * ng_,
        scratch_shapes=_gather2_sems(ng_) if ng_ else [],
        compiler_params=pltpu.CompilerParams(dimension_semantics=("arbitrary",), vmem_limit_bytes=VMEM_LIMIT,
                                             has_side_effects=bool(ng_)),
    )(xr, mod, mod, ng, w, *[pltpu.with_memory_space_constraint(a, pltpu.HBM) for a in gather or ()])


def _scan_pair(af_ref, uf_ref, hf_ref, h0f, ab_ref, ub_ref, hb_ref, h0b, t_len):
    span = 8 * SCAN_BLOCKS
    nit = t_len // span
    rows = _rows((8, HD))

    def local_scan(a, b, forward):
        for s in (1, 2, 4):
            sh = s if forward else 8 - s
            m = rows >= s if forward else rows < 8 - s
            b = a * jnp.where(m, pltpu.roll(b, sh, 0), 0.0) + b
            a = a * jnp.where(m, pltpu.roll(a, sh, 0), 1.0)
        return a, b

    def span_scan(a_ref, u_ref, h_ref, off, carry, forward):
        order = range(SCAN_BLOCKS) if forward else range(SCAN_BLOCKS - 1, -1, -1)
        last = slice(7, 8) if forward else slice(0, 1)
        for q in order:
            rs = pl.ds(off + 8 * q, 8)
            a, b = local_scan(a_ref[rs, :], u_ref[rs, :], forward)
            h_ref[rs, :] = b + a * carry
            carry = a[last, :] * carry + b[last, :]
        return carry

    def body(k, carry):
        cf, cb = carry
        cf = span_scan(af_ref, uf_ref, hf_ref, pl.multiple_of(k * span, span), cf, True)
        cb = span_scan(ab_ref, ub_ref, hb_ref, pl.multiple_of((nit - 1 - k) * span, span), cb, False)
        return cf, cb

    return lax.fori_loop(0, nit, body, (h0f, h0b))


SCAN_BLOCKS = 4


def _conv(xa, cw, cb):
    z = jnp.zeros((1, HD), F32)
    xm1 = _shift_down(xa, z)
    xp1 = _shift_up(xa, z)
    xp2 = _shift_up(xp1, z)
    return xm1 * cw[0:1, :] + xa * cw[1:2, :] + xp1 * cw[2:3, :] + xp2 * cw[3:4, :] + cb


def _gates(xc, wa, wx, ba, bx, nsp):
    xb = xc.astype(BF16)
    r = _sigmoid(_dot(xb, wa) + ba)
    i = _sigmoid(_dot(xb, wx) + bx)
    log_a = r * nsp
    a = jnp.exp(log_a)
    g2 = jnp.tanh(log_a) * (-1.0 - a * a)
    rg = lax.rsqrt(jnp.maximum(g2, 1e-30))
    return r, i, a, g2 * rg, rg


def _lru_param_specs():
    h4 = pl.BlockSpec((2, 1, HD, HD), lambda h: (0, h, 0, 0))
    v2 = pl.BlockSpec((2, HD), lambda h: (0, h))
    return dict(
        xa=pl.BlockSpec((L, HD), lambda h: (0, h)), xac=pl.BlockSpec((LC, HD), lambda h: (0, h)),
        cw=pl.BlockSpec((CONV_W, HD), lambda h: (0, h)), cb=pl.BlockSpec((1, HD), lambda h: (0, h)), h4=h4, v2=v2)


def _lru_forward(zx, zc, cw, cb, wa, wx, ba, bx, lam):
    def body(xa_ref, xac_ref, cw_ref, cb_ref, wa_ref, wx_ref, ba_ref, bx_ref, lam_ref, yl_ref,
             af, uf, hf, ab, ub, hb):
        cwv, cbv = cw_ref[...], cb_ref[...]
        nsp = (-LRU_C) * _softplus(-lam_ref[...])

        def forward(xa, t_len, h0f, h0b):
            xc = _conv(xa, cwv, cbv)
            for d, (a_ref, u_ref) in enumerate(((af, uf), (ab, ub))):
                _, i, a, gamma, _ = _gates(xc, wa_ref[d, 0].astype(BF16), wx_ref[d, 0].astype(BF16),
                                           ba_ref[d:d + 1, :], bx_ref[d:d + 1, :], nsp[d:d + 1, :])
                a_ref[0:t_len, :] = a
                u_ref[0:t_len, :] = gamma * (i * xc)
            return _scan_pair(af, uf, hf, h0f, ab, ub, hb, h0b, t_len)

        z = jnp.zeros((1, HD), F32)
        h0f, h0b = forward(xac_ref[...], LC, z, z)
        forward(xa_ref[...], L, h0f, h0b)
        yl_ref[...] = hf[...] + hb[...]

    s = _lru_param_specs()
    return _call(
        body, name="lru_forward", grid=(HEADS,), out_shape=jax.ShapeDtypeStruct((L, D), F32),
        in_specs=[s["xa"], s["xac"], s["cw"], s["cb"], s["h4"], s["h4"], s["v2"], s["v2"], s["v2"]],
        out_specs=pl.BlockSpec((L, HD), lambda h: (0, h)),
        scratch_shapes=[pltpu.VMEM((L, HD), F32)] * 6,
        compiler_params=_params("arbitrary"),
    )(zx, zc, cw, cb, wa, wx, ba, bx, lam)


def _lru_backward(zx, zc, dyl, dz, cw, cb, wa, wx, ba, bx, lam, chip_sums, first_chips=None):
    nr = len(chip_sums)

    def body(xa_ref, xac_ref, dyl_ref, dz_in, cw_ref, cb_ref, wa_ref, wx_ref, ba_ref, bx_ref, lam_ref, *rest):
        (dxa_ref, dxac_ref, dwa_ref, dwx_ref, dba_ref, dbx_ref, dlam_ref, dcw_ref,
         dcb_ref) = rest[nr:nr + 9]
        main_s, ctx_s = rest[2 * nr + 9:2 * nr + 11]
        if nr:
            start, finish = _chips_ops(rest[:nr], rest[nr + 9:2 * nr + 9], *rest[2 * nr + 11:],
                                       first_chips=first_chips)
            pl.when(pl.program_id(0) == 0)(start)
            pl.when(pl.program_id(0) == HEADS - 1)(finish)
        del dz_in
        cwv, cbv = cw_ref[...], cb_ref[...]
        lamv = lam_ref[...]
        sp = _softplus(-lamv)
        nsp = (-LRU_C) * sp
        z = jnp.zeros((1, HD), F32)

        def wmat(ref, d):
            return ref[d, 0].astype(BF16)

        def workspace(s):
            return dict(a=(s.at[0], s.at[1]), u=(s.at[2], s.at[3]), h=(s.at[4], s.at[5]), rho=(s.at[6], s.at[7]),
                        saved=(tuple(s.at[8 + k] for k in range(4)), tuple(s.at[12 + k] for k in range(4))),
                        xc=s.at[16])

        def forward(ws, xa, t_len, h0f, h0b):
            xc = _conv(xa, cwv, cbv)
            ws["xc"][...] = xc
            for d in (0, 1):
                vals = _gates(xc, wmat(wa_ref, d), wmat(wx_ref, d), ba_ref[d:d + 1, :], bx_ref[d:d + 1, :],
                              nsp[d:d + 1, :])
                r, i, a, gamma, rg = vals
                ws["a"][d][...] = a
                ws["u"][d][...] = gamma * (i * xc)
                for ref, val in zip(ws["saved"][d], (r, i, gamma, rg)):
                    ref[...] = val
            return _scan_pair(ws["a"][0], ws["u"][0], ws["h"][0], h0f, ws["a"][1], ws["u"][1], ws["h"][1], h0b,
                              t_len)

        def backward(ws, xa, t_len, h0f, h0b, dhf, dhb, first):
            xc = ws["xc"][...]
            (af, ab), (uf, ub), (hf, hb), (rf, rb) = ws["a"], ws["u"], ws["h"], ws["rho"]
            uf[...] = ab[...] * dhb
            ub[...] = af[...] * dhf
            rho_b_last, rho_f_first = _scan_pair(ab, uf, rb, z, af, ub, rf, z, t_len)
            dxc = jnp.zeros((t_len, HD), F32)
            dsp = []
            for d in (0, 1):
                r, i, gamma, rg = (ref[...] for ref in ws["saved"][d])
                a = ws["a"][d][...]
                if d == 0:
                    lam_t = dhf + _shift_up(rf[...], z)
                    h_prev = _shift_down(hf[...], h0f)
                else:
                    lam_t = dhb + _shift_down(rb[...], z)
                    h_prev = _shift_up(hb[...], h0b)
                da = lam_t * h_prev
                lx = lam_t * xc
                d_i = lx * gamma
                d_gamma = lx * i
                dxc = dxc + lam_t * (gamma * i)
                d_log_a = a * (da - d_gamma * (a * rg))
                dsp.append(jnp.sum(d_log_a * r, axis=0, keepdims=True) * (-LRU_C))
                d_pre_r = d_log_a * nsp[d:d + 1, :] * (r * (1.0 - r))
                d_pre_i = d_i * (i * (1.0 - i))
                prb, pib, xb = d_pre_r.astype(BF16), d_pre_i.astype(BF16), xc.astype(BF16)
                dxc = dxc + _dot_nt(prb, wmat(wa_ref, d)) + _dot_nt(pib, wmat(wx_ref, d))
                g_wa, g_wx = _dot_tn(xb, prb), _dot_tn(xb, pib)
                g_ba = jnp.sum(d_pre_r, axis=0, keepdims=True)
                g_bx = jnp.sum(d_pre_i, axis=0, keepdims=True)
                if first:
                    dwa_ref[d, 0] = g_wa
                    dwx_ref[d, 0] = g_wx
                    dba_ref[d:d + 1, :] = g_ba
                    dbx_ref[d:d + 1, :] = g_bx
                else:
                    dwa_ref[d, 0] += g_wa
                    dwx_ref[d, 0] += g_wx
                    dba_ref[d:d + 1, :] += g_ba
                    dbx_ref[d:d + 1, :] += g_bx
            g_lam = jnp.concatenate(dsp, axis=0) * (-_sigmoid(-lamv))
            dm1 = _shift_down(dxc, z)
            dp1 = _shift_up(dxc, z)
            dm2 = _shift_down(dm1, z)
            dxa = dp1 * cwv[0:1, :] + dxc * cwv[1:2, :] + dm1 * cwv[2:3, :] + dm2 * cwv[3:4, :]
            xm1 = _shift_down(xa, z)
            xp1 = _shift_up(xa, z)
            xp2 = _shift_up(xp1, z)
            g_cw = jnp.concatenate([jnp.sum(dxc * v, axis=0, keepdims=True) for v in (xm1, xa, xp1, xp2)], axis=0)
            g_cb = jnp.sum(dxc, axis=0, keepdims=True)
            if first:
                dlam_ref[...] = g_lam
                dcw_ref[...] = g_cw
                dcb_ref[...] = g_cb
            else:
                dlam_ref[...] += g_lam
                dcw_ref[...] += g_cw
                dcb_ref[...] += g_cb
            return dxa, rho_f_first, rho_b_last

        ws_x, ws_c = workspace(main_s), workspace(ctx_s)
        h0f, h0b = forward(ws_c, xac_ref[...], LC, z, z)
        forward(ws_x, xa_ref[...], L, h0f, h0b)
        dh = dyl_ref[...]
        dxa, dh0f, dh0b = backward(ws_x, xa_ref[...], L, h0f, h0b, dh, dh, True)
        dxa_ref[...] = dxa.astype(BF16)
        rc = _rows((LC, HD))
        dxac, _, _ = backward(ws_c, xac_ref[...], LC, z, z, jnp.where(rc == LC - 1, dh0f, 0.0),
                              jnp.where(rc == 0, dh0b, 0.0), False)
        dxac_ref[...] = dxac.astype(BF16)

    s = _lru_param_specs()
    col = lambda r: pl.BlockSpec((r, HD), lambda h: (0, h))
    return _call(
        body, name="lru_backward", grid=(HEADS,),
        out_shape=[jax.ShapeDtypeStruct((L, D_IN), BF16), jax.ShapeDtypeStruct((LC, D), BF16),
                   jax.ShapeDtypeStruct((2, HEADS, HD, HD), F32), jax.ShapeDtypeStruct((2, HEADS, HD, HD), F32),
                   jax.ShapeDtypeStruct((2, D), F32), jax.ShapeDtypeStruct((2, D), F32),
                   jax.ShapeDtypeStruct((2, D), F32), jax.ShapeDtypeStruct((CONV_W, D), F32),
                   jax.ShapeDtypeStruct((1, D), F32)] + [jax.ShapeDtypeStruct((4,) + a.shape[1:], a.dtype)
                                                          for a in chip_sums],
        in_specs=[s["xa"], s["xac"], col(L), pl.BlockSpec(memory_space=pl.ANY), s["cw"], s["cb"], s["h4"], s["h4"],
                  s["v2"], s["v2"], s["v2"]] + [HBM] * nr,
        out_specs=[col(L), col(LC), s["h4"], s["h4"], s["v2"], s["v2"], s["v2"], col(CONV_W), col(1)] + [HBM] * nr,
        scratch_shapes=[pltpu.VMEM((17, L, HD), F32), pltpu.VMEM((17, LC, HD), F32)] + (_chips_sems(nr) if nr else []),
        input_output_aliases={3: 0},
        compiler_params=pltpu.CompilerParams(dimension_semantics=("arbitrary",), vmem_limit_bytes=VMEM_LIMIT,
                                             has_side_effects=True),
    )(zx, zc, dyl, dz, cw, cb, wa, wx, ba, bx, lam, *[pltpu.with_memory_space_constraint(a, pltpu.HBM)
                                                       for a in chip_sums])


def _mixer_loss(x, tgt, zx, yl, gx, fg, lng, lnb, ws, wst, bst, wout, tm):
    ncht = tm // CHUNK

    def body(x_ref, t_ref, ga_ref, u_ref, v_ref, gb_ref, yl_ref, gx_ref, fg_ref, lng_ref, lnb_ref, ws_ref, wst_ref,
             bst_ref, wout_ref,
             dz_ref, dyl_ref, dxn_ref, y_s, do_ref, dws_ref, dbst_ref, vec_ref,
             vn_s, mix_s, dm_s, dvn_s):
        step = pl.program_id(0)

        @pl.when(step == 0)
        def _():
            dws_ref[...] = jnp.zeros_like(dws_ref)
            dbst_ref[...] = jnp.zeros_like(dbst_ref)
            vec_ref[...] = jnp.zeros_like(vec_ref)

        u, v = u_ref[...], v_ref[...]
        ug, dug_du = _gelu_and_grad(u)
        vg, dvg_dv = _gelu_and_grad(v)
        mu = jnp.mean(vg, axis=-1, keepdims=True)
        vc = vg - mu
        rstd = lax.rsqrt(jnp.mean(vc * vc, axis=-1, keepdims=True) + LN_EPS)
        vhat = vc * rstd
        lngv = lng_ref[...]
        vn_s[...] = (vhat * lngv + lnb_ref[...]).astype(BF16)
        for ch in range(ncht):
            rs = slice(ch * CHUNK, (ch + 1) * CHUNK)
            for g in range(HEADS):
                cs = slice(g * HD, (g + 1) * HD)
                mix_s[rs, cs] = _dot(ws_ref[g], vn_s[rs, cs]) + bst_ref[:, g:g + 1]
        mixed = mix_s[...]
        ga, gb, yl = ga_ref[...], gb_ref[...], yl_ref[...]
        sga, dsga = _silu_and_grad(ga)
        sgb, dsgb = _silu_and_grad(gb)
        ys = ug * mixed
        y_s[:, 0:D] = (yl * sga).astype(BF16)
        y_s[:, D:D_MIX] = (ys * sgb).astype(BF16)
        o = _dot(y_s[...], wout_ref[...])
        gxv, fgv = gx_ref[...], fg_ref[...]
        xn = x_ref[...] + gxv * o
        rs2 = lax.rsqrt(jnp.mean(xn * xn, axis=-1, keepdims=True) + NORM_EPS)
        xh = xn * rs2
        diff = xh * fgv - t_ref[...]
        vec_ref[R_LOSS:R_LOSS + 1, :] += jnp.full((1, D), jnp.sum(diff * diff) * (0.5 / D), F32)
        dout = diff * (1.0 / D)
        w = dout * fgv
        dxn = rs2 * (w - xh * jnp.mean(w * xh, axis=-1, keepdims=True))
        dxn_ref[...] = dxn
        vec_ref[0:1, :] += jnp.sum(dxn * o, axis=0, keepdims=True)
        vec_ref[1:2, :] += jnp.sum(dout * xh, axis=0, keepdims=True)
        dob = (dxn * gxv).astype(BF16)
        do_ref[...] = dob
        dy = _dot_nt(dob, wout_ref[...])
        dya, dyb = dy[:, 0:D], dy[:, D:D_MIX]
        dyl_ref[...] = dya * sga
        dys = dyb * sgb
        dz_ref[:, 0:D] = jnp.zeros((tm, D), BF16)
        dz_ref[:, D:2 * D] = (dya * yl * dsga).astype(BF16)
        dz_ref[:, 2 * D:3 * D] = (dys * mixed * dug_du).astype(BF16)
        dz_ref[:, 4 * D:5 * D] = (dyb * ys * dsgb).astype(BF16)
        dm = dys * ug
        dm_s[...] = dm.astype(BF16)
        for g in range(HEADS):
            cs = slice(g * HD, (g + 1) * HD)
            dbst_ref[:, g:g + 1] += sum(jnp.sum(dm[ch * CHUNK:(ch + 1) * CHUNK, cs], axis=1, keepdims=True)
                                        for ch in range(ncht))
            for ch in range(ncht):
                rs = slice(ch * CHUNK, (ch + 1) * CHUNK)
                dws_ref[g] += _dot_nt(dm_s[rs, cs], vn_s[rs, cs])
                dvn_s[rs, cs] = _dot(wst_ref[g], dm_s[rs, cs])
        dvn = dvn_s[...]
        vec_ref[2:3, :] += jnp.sum(dvn * vhat, axis=0, keepdims=True)
        vec_ref[3:4, :] += jnp.sum(dvn, axis=0, keepdims=True)
        dvh = dvn * lngv
        dvg = rstd * (dvh - jnp.mean(dvh, axis=-1, keepdims=True) - vhat * jnp.mean(dvh * vhat, axis=-1, keepdims=True))
        dz_ref[:, 3 * D:4 * D] = (dvg * dvg_dv).astype(BF16)

    tile = pl.BlockSpec((tm, D), lambda i: (i, 0))
    zcol = lambda n: pl.BlockSpec((tm, D), lambda i: (i, n))
    vec = pl.BlockSpec((1, D), lambda i: (0, 0))
    full = lambda *s: pl.BlockSpec(s, lambda i: (0,) * len(s))
    return _call(
        body, name="mixer_loss", grid=(L // tm,),
        out_shape=[jax.ShapeDtypeStruct((L, D_IN), BF16), jax.ShapeDtypeStruct((L, D), F32),
                   jax.ShapeDtypeStruct((L, D), F32), jax.ShapeDtypeStruct((L, D_MIX), BF16),
                   jax.ShapeDtypeStruct((L, D), BF16),
                   jax.ShapeDtypeStruct((HEADS, CHUNK, CHUNK), F32), jax.ShapeDtypeStruct((CHUNK, HEADS), F32),
                   jax.ShapeDtypeStruct((8, D), F32)],
        in_specs=[tile, tile, zcol(1), zcol(2), zcol(3), zcol(4), tile, pl.BlockSpec((1, D), lambda i: (0, 2)),
                  vec, vec, vec,
                  full(HEADS, CHUNK, CHUNK), full(HEADS, CHUNK, CHUNK), full(CHUNK, HEADS),
                  pl.BlockSpec((D_MIX, D), lambda i: (0, 0), pipeline_mode=pl.Buffered(1))],
        out_specs=[pl.BlockSpec((tm, D_IN), lambda i: (i, 0)), tile, tile,
                   pl.BlockSpec((tm, D_MIX), lambda i: (i, 0)), tile,
                   full(HEADS, CHUNK, CHUNK), full(CHUNK, HEADS), full(8, D)],
        scratch_shapes=[pltpu.VMEM((tm, D), BF16), pltpu.VMEM((tm, D), F32),
                        pltpu.VMEM((tm, D), BF16), pltpu.VMEM((tm, D), F32)],
        compiler_params=_params("arbitrary"),
    )(x, tgt, zx, zx, zx, zx, yl, gx, fg, lng, lnb, ws, wst, bst, wout)


def _grad_w(a, b, a2, b2, tk, name, bw=D, first=0, nblocks=None):
    nk = a.shape[0] // tk
    m = a.shape[1]
    nblocks = nblocks or b.shape[1] // bw
    with_ctx = a2 is not None

    def body(*refs):
        if with_ctx:
            a_ref, b_ref, a2_ref, b2_ref, o_ref, acc = refs
        else:
            a_ref, b_ref, o_ref, acc = refs
        n, k = pl.program_id(0), pl.program_id(1)

        @pl.when(k == 0)
        def _():
            acc[...] = jnp.zeros_like(acc)

        acc[...] += _dot_tn(a_ref[...], b_ref[...])

        if with_ctx:
            @pl.when(jnp.logical_and(k == nk - 1, n == 0))
            def _():
                acc[:, 0:b2_ref.shape[1]] += _dot_tn(a2_ref[...], b2_ref[...])

        @pl.when(k == nk - 1)
        def _():
            o_ref[...] = acc[...].astype(BF16)

    in_specs = [pl.BlockSpec((tk, m), lambda n, k: (k, 0)), pl.BlockSpec((tk, bw), lambda n, k: (k, n + first))]
    args = [a, b]
    if with_ctx:
        in_specs += [pl.BlockSpec(a2.shape, lambda n, k: (0, 0)), pl.BlockSpec(b2.shape, lambda n, k: (0, 0))]
        args += [a2, b2]
    return _call(
        body, name=name, grid=(nblocks, nk), out_shape=jax.ShapeDtypeStruct((m, nblocks * bw), BF16),
        in_specs=in_specs, out_specs=pl.BlockSpec((m, bw), lambda n, k: (0, n)),
        scratch_shapes=[pltpu.VMEM((m, bw), F32)],
        compiler_params=_params("arbitrary", "arbitrary"),
    )(*args)


def _grad_rows(xr, dz, w, mod, ng, dres, ncols, tm, name, chip_sums=(), first_chips=None, dests=None):
    rows = xr.shape[0]
    steps = rows // tm
    with_dx = dres is not None
    nr = len(chip_sums)
    dests = [d for d in (dests or [None] * nr)]
    nd = sum(d is not None for d in dests)
    nin = 6 if with_dx else 5
    nout = 2 if with_dx else 1

    def body(*refs):
        if with_dx:
            x_ref, dz_ref, w_ref, sc_ref, ng_ref, dres_ref = refs[:nin]
            dx_ref, vec_ref = refs[nin + nr + nd:nin + nr + nd + nout]
        else:
            x_ref, dz_ref, w_ref, sc_ref, ng_ref = refs[:nin]
            (vec_ref,) = refs[nin + nr + nd:nin + nr + nd + nout]
        if nr:
            o0 = nin + nr + nd + nout
            start, finish = _chips_ops(refs[nin:nin + nr], refs[o0:o0 + nr], *refs[o0 + nr:], first_chips=first_chips)
            pl.when(pl.program_id(0) == 0)(start)
            pl.when(pl.program_id(0) == steps - 1)(finish)

        @pl.when(pl.program_id(0) == 0)
        def _():
            vec_ref[...] = jnp.zeros_like(vec_ref)

        dhn = _dot_nt(dz_ref[...], w_ref[...])
        x = x_ref[...]
        rs = lax.rsqrt(jnp.mean(x * x, axis=-1, keepdims=True) + NORM_EPS)
        xh = x * rs
        ngv = ng_ref[...]
        y = xh * ngv
        vec_ref[0:1, :] += jnp.sum(dhn, axis=0, keepdims=True)
        vec_ref[1:2, :] += jnp.sum(dhn * y, axis=0, keepdims=True)
        dy = dhn * (1.0 + sc_ref[...])
        vec_ref[2:3, :] += jnp.sum(dy * xh, axis=0, keepdims=True)
        if with_dx:
            dxh = dy * ngv
            dx_ref[...] = dres_ref[...] + rs * (dxh - xh * jnp.mean(dxh * xh, axis=-1, keepdims=True))

    tile = pl.BlockSpec((tm, D), lambda i: (i, 0))
    vec = pl.BlockSpec((1, D), lambda i: (0, 0))
    in_specs = [tile, pl.BlockSpec((tm, ncols), lambda i: (i, 0)),
                pl.BlockSpec((D, ncols), lambda i: (0, 0), pipeline_mode=pl.Buffered(1)),
                pl.BlockSpec((1, D), lambda i: (0, 1)), vec]
    out_shape = [jax.ShapeDtypeStruct((8, D), F32)]
    out_specs = [pl.BlockSpec((8, D), lambda i: (0, 0))]
    args = [xr, dz, w, mod, ng]
    if with_dx:
        in_specs.append(tile)
        out_shape.insert(0, jax.ShapeDtypeStruct((rows, D), F32))
        out_specs.insert(0, tile)
        args.append(dres)
    aliases = {}
    for j, d in enumerate(dests):
        if d is not None:
            aliases[len(args) + nr + len(aliases)] = len(out_shape) + j
    in_specs += [HBM] * (nr + nd)
    out_specs += [HBM] * nr
    out_shape += [jax.ShapeDtypeStruct((4,) + a.shape[1:], a.dtype) for a in chip_sums]
    args += [pltpu.with_memory_space_constraint(a, pltpu.HBM) for a in chip_sums]
    args += [pltpu.with_memory_space_constraint(d, pltpu.HBM) for d in dests if d is not None]
    return _call(body, name=name, grid=(steps,), out_shape=out_shape, in_specs=in_specs, out_specs=out_specs,
                 scratch_shapes=_chips_sems(nr) if nr else [], input_output_aliases=aliases,
                 compiler_params=pltpu.CompilerParams(dimension_semantics=("arbitrary",),
                                                      vmem_limit_bytes=VMEM_LIMIT, has_side_effects=bool(nr)))(*args)


def _adamw(w, g, m, v):
    m = ADAM_B1 * m + (1.0 - ADAM_B1) * g
    v = ADAM_B2 * v + (1.0 - ADAM_B2) * (g * g)
    m_hat = m / (1.0 - ADAM_B1 ** ADAM_STEP)
    v_hat = v / (1.0 - ADAM_B2 ** ADAM_STEP)
    delta = -ADAM_LR * (m_hat / (jnp.sqrt(v_hat) + ADAM_EPS) + ADAM_WD * w)
    return delta, m, v


def _adamw_reduced(parts, w, m, v, tr, name):
    r, n = w.shape
    nparts = parts.shape[0]

    def body(p_ref, w_ref, m_ref, v_ref, g_ref, d_ref, mo_ref, vo_ref):
        g = p_ref[0].astype(F32)
        for i in range(1, nparts):
            g = g + p_ref[i].astype(F32)
        g_ref[...] = g
        d_ref[...], mo_ref[...], vo_ref[...] = _adamw(w_ref[...], g, m_ref[...], v_ref[...])

    tile = pl.BlockSpec((tr, n), lambda i: (i, 0))
    sds = jax.ShapeDtypeStruct((r, n), F32)
    return _call(
        body, name=name, grid=(r // tr,), out_shape=[sds] * 4,
        in_specs=[pl.BlockSpec((nparts, tr, n), lambda i: (0, i, 0)), tile, tile, tile], out_specs=[tile] * 4,
        compiler_params=_params("arbitrary"),
    )(parts, w, m, v)


R_GATE, R_FINAL_G, R_LN_G, R_LN_B, R_LOSS = 0, 1, 2, 3, 4
R_SH_X, R_SC_X, R_NG_X = 5, 6, 7
R_SH_C, R_SC_C, R_NG_C = 8, 9, 10
R_BA, R_BX, R_LAM, R_CW, R_CB, R_SGU_B = 11, 13, 15, 17, 21, 22
PACK_ROWS = 32
MAT_ROWS = 2 * (2 * HEADS * HD) + HEADS * CHUNK


def _reduce_small(vp_all, mat_parts, ada_w, me):
    nloc = ada_w.shape[1]

    def body(me_ref, vp_ref, mp_ref, w_ref, red_ref, mat_ref, dmod_ref, gab_ref, cpart_ref, dmc_s):
        red = vp_ref[0]
        for i in range(1, N_DEV):
            red = red + vp_ref[i]
        mat = mp_ref[0]
        for i in range(1, mp_ref.shape[0]):
            mat = mat + mp_ref[i]
        red_ref[...] = red
        mat_ref[...] = mat
        for e in range(N_DEV):
            dmod_ref[e:e + 1, 0:D] = vp_ref[e, R_SH_X:R_SH_X + 1, :]
            dmod_ref[e:e + 1, D:2 * D] = vp_ref[e, R_SC_X:R_SC_X + 1, :]
            dmod_ref[e:e + 1, 2 * D:3 * D] = vp_ref[e, R_GATE:R_GATE + 1, :]
        dmod_ref[8:9, 0:D] = red[R_SH_C:R_SH_C + 1, :]
        dmod_ref[8:9, D:2 * D] = red[R_SC_C:R_SC_C + 1, :]
        dmod_ref[8:9, 2 * D:3 * D] = jnp.zeros((1, D), F32)
        dmod_ref[9:16, :] = jnp.zeros((7, 3 * D), F32)
        gab_ref[:, 0:D] = red[R_SH_X:R_SH_X + 1, :] + red[R_SH_C:R_SH_C + 1, :]
        gab_ref[:, D:2 * D] = red[R_SC_X:R_SC_X + 1, :] + red[R_SC_C:R_SC_C + 1, :]
        gab_ref[:, 2 * D:3 * D] = red[R_GATE:R_GATE + 1, :]
        dmc_s[...] = jnp.broadcast_to(dmod_ref[8:9, :], (8, 3 * D))
        off = pl.multiple_of(me_ref[0] * nloc, 128)
        cpart_ref[...] = _dot_nt(dmc_s[:, pl.ds(off, nloc)], w_ref[...])

    return _call(
        body, name="reduce_small",
        out_shape=[jax.ShapeDtypeStruct((PACK_ROWS, D), F32), jax.ShapeDtypeStruct(mat_parts.shape[1:], F32),
                   jax.ShapeDtypeStruct((16, 3 * D), F32), jax.ShapeDtypeStruct((1, 3 * D), F32),
                   jax.ShapeDtypeStruct((8, D), F32)],
        in_specs=[pl.BlockSpec(memory_space=pltpu.SMEM), VMEM, VMEM, VMEM], out_specs=[VMEM] * 5,
        scratch_shapes=[pltpu.VMEM((8, 3 * D), F32)], compiler_params=_params(),
    )(me, vp_all, mat_parts, ada_w)


def _adamw_ada(c_all, c_ctx, dmod, w, m, v, me):
    nloc = w.shape[1]

    def body(me_ref, c_ref, cc_ref, dm_ref, w_ref, m_ref, v_ref, g_ref, d_ref, mo_ref, vo_ref):
        off = pl.multiple_of(me_ref[0] * nloc, 128)
        dm = dm_ref[:, pl.ds(off, nloc)]
        sx, _ = _silu_and_grad(c_ref[...])
        sc, _ = _silu_and_grad(cc_ref[...])
        g = _dot_tn(sx, dm[0:8, :]) + _dot_tn(jnp.broadcast_to(sc, (8, D)), dm[8:16, :])
        g_ref[...] = g
        d_ref[...], mo_ref[...], vo_ref[...] = _adamw(w_ref[...], g, m_ref[...], v_ref[...])

    sds = jax.ShapeDtypeStruct(w.shape, F32)
    return _call(
        body, name="adamw_ada_w", out_shape=[sds] * 4,
        in_specs=[pl.BlockSpec(memory_space=pltpu.SMEM)] + [VMEM] * 6, out_specs=[VMEM] * 4,
        compiler_params=_params(),
    )(me, c_all, c_ctx, dmod, w, m, v)


_SMALL = ("c_ctx", "ada_b", "norm_g", "conv_w", "conv_b", "lru_wa", "lru_ba", "lru_wx", "lru_bx", "lru_lambda",
          "sgu_ln_g", "sgu_ln_b", "sgu_w", "sgu_b", "final_g")


def _adamw_small(red, mat, cparts, gab, ws, ms, vs, me):
    n = len(_SMALL)
    nw = 2 * HEADS * HD

    def body(me_ref, red_ref, mat_ref, cp_ref, gab_ref, *refs):
        w_refs, m_refs, v_refs = refs[:n], refs[n:2 * n], refs[2 * n:3 * n]
        outs = refs[3 * n:]
        off = pl.multiple_of(me_ref[0] * HD, 128)

        def row(r, k=1):
            return red_ref[r:r + k, :]

        cc = w_refs[0][...]
        dcc = cp_ref[0, 0:1, :]
        for i in range(1, N_DEV):
            dcc = dcc + cp_ref[i, 0:1, :]
        grads = dict(
            c_ctx=dcc * _silu_and_grad(cc)[1], ada_b=gab_ref[...], norm_g=row(R_NG_X) + row(R_NG_C),
            conv_w=red_ref[R_CW:R_CW + CONV_W, pl.ds(off, HD)], conv_b=row(R_CB),
            lru_wa=mat_ref[0:nw, :], lru_ba=row(R_BA, 2), lru_wx=mat_ref[nw:2 * nw, :], lru_bx=row(R_BX, 2),
            lru_lambda=red_ref[R_LAM:R_LAM + 2, pl.ds(off, HD)], sgu_ln_g=row(R_LN_G), sgu_ln_b=row(R_LN_B),
            sgu_w=mat_ref[2 * nw:MAT_ROWS, :], sgu_b=row(R_SGU_B), final_g=row(R_FINAL_G))
        for j, name in enumerate(_SMALL):
            g = grads[name]
            outs[j][...] = g
            outs[n + j][...], outs[2 * n + j][...], outs[3 * n + j][...] = _adamw(w_refs[j][...], g, m_refs[j][...],
                                                                                 v_refs[j][...])

    sds = [jax.ShapeDtypeStruct(ws[k].shape, F32) for k in _SMALL]
    outs = _call(
        body, name="adamw_small", out_shape=sds * 4,
        in_specs=[pl.BlockSpec(memory_space=pltpu.SMEM)] + [VMEM] * (4 + 3 * n), out_specs=[VMEM] * (4 * n),
        compiler_params=_params(),
    )(me, red, mat, cparts, gab, *[ws[k] for k in _SMALL], *[ms[k] for k in _SMALL], *[vs[k] for k in _SMALL])
    return [dict(zip(_SMALL, outs[i * n:(i + 1) * n])) for i in range(4)]


def kernel(x, c, ctx, c_ctx, ada_w, ada_b, norm_g, w_in, conv_w, conv_b, lru_wa, lru_ba, lru_wx, lru_bx, lru_lambda, sgu_ln_g, sgu_ln_b, sgu_w, sgu_b, w_out, final_g, loss_target, m_c_ctx, m_ada_w, m_ada_b, m_norm_g, m_w_in, m_conv_w, m_conv_b, m_lru_wa, m_lru_ba, m_lru_wx, m_lru_bx, m_lru_lambda, m_sgu_ln_g, m_sgu_ln_b, m_sgu_w, m_sgu_b, m_w_out, m_final_g, v_c_ctx, v_ada_w, v_ada_b, v_norm_g, v_w_in, v_conv_w, v_conv_b, v_lru_wa, v_lru_ba, v_lru_wx, v_lru_bx, v_lru_lambda, v_sgu_ln_g, v_sgu_ln_b, v_sgu_w, v_sgu_b, v_w_out, v_final_g):
    args = dict(locals())
    me_s = 4 * lax.axis_index("x") + 2 * lax.axis_index("y") + lax.axis_index("c")
    me = me_s.astype(jnp.int32).reshape(1)
    xr, ctxr, tgt = x[0], ctx[0], loss_target[0]
    cc = c_ctx.reshape(1, D)
    nw = 2 * HEADS * HD
    view = dict(c_ctx=(1, D), ada_b=(1, 3 * D), norm_g=(1, D), conv_w=(CONV_W, HD), conv_b=(1, D), lru_wa=(nw, HD),
                lru_ba=(2, D), lru_wx=(nw, HD), lru_bx=(2, D), lru_lambda=(2, HD), sgu_ln_g=(1, D), sgu_ln_b=(1, D),
                sgu_w=(HEADS * CHUNK, CHUNK), sgu_b=(1, D), final_g=(1, D))

    w_full, w_out_b, modx, modc, c_all = _front(c, cc, ada_w[0], ada_b, w_in[0], w_out[0], me)

    zx, hn, wout_all, cw_full, lam_full = _project(
        xr, modx, norm_g, w_full, D_IN, 256, "project_x", gather=[w_out_b, conv_w[0], lru_lambda[0]],
        gather_modes=["ag", "agc", "agc"])
    wout_full = wout_all.reshape(D_MIX, D)
    zc, hnc = _project(ctxr, modc, norm_g, w_full, D, LC, "project_ctx")
    ba, bx = lru_ba.reshape(2, D), lru_bx.reshape(2, D)
    yl = _lru_forward(zx, zc, cw_full, conv_b, lru_wa[0], lru_wx[0], ba, bx, lam_full)
    ws_b = sgu_w[0].astype(BF16)
    dz, dyl, dxn, ycat, dob, dws, dbst, mvec = _mixer_loss(
        xr, tgt, zx, yl, modx, final_g.reshape(1, D), sgu_ln_g, sgu_ln_b, ws_b, jnp.swapaxes(ws_b, 1, 2),
        sgu_b[0].T, wout_full, 128)

    gw_out = _grad_w(ycat, dob, None, None, 512, "grad_w_out")
    gw_rest = _grad_w(hn, dz, None, None, 512, "grad_w_in_rest", bw=2 * W_IN_SHARD, first=1, nblocks=3)
    rest_sums, wout_sums = _reduce2_local([gw_rest, gw_out.reshape(N_DEV, D_MIX // N_DEV, D)], ["a2ac", "a2a"], me,
                                          "reduce_early", counts=[3, 4])
    dz, dxac, dwa, dwx, dba, dbx, dlam, dcw, dcb, win_parts, wout_parts = _lru_backward(
        zx, zc, dyl, dz, cw_full, conv_b, lru_wa[0], lru_wx[0], ba, bx, lam_full, [rest_sums, wout_sums],
        first_chips=[1, 0])
    gw_first = _grad_w(hn, dz, hnc, dxac, 512, "grad_w_in_first", bw=2 * W_IN_SHARD, first=0, nblocks=1)
    matpack = jnp.concatenate([dwa.reshape(nw, HD), dwx.reshape(nw, HD), dws.reshape(HEADS * CHUNK, CHUNK)], axis=0)
    first_sums, mat_sums = _reduce2_local([gw_first, matpack.reshape(N_DEV, MAT_ROWS // N_DEV, HD)], ["a2ac", "a2a"],
                                          me, "reduce_late", counts=[1, 4])
    gx, xvec, win_parts, mat_parts = _grad_rows(
        xr, dz, w_full, modx, norm_g, dxn, D_IN, 256, "grad_rows_x", chip_sums=[first_sums, mat_sums],
        first_chips=[0, 0], dests=[win_parts, None])
    (cvec,) = _grad_rows(ctxr, dxac, w_full, modc, norm_g, None, D, LC, "grad_rows_ctx")
    pack = jnp.concatenate([mvec[0:5], xvec[0:3], cvec[0:3], dba, dbx, dlam, dcw, dcb, dbst.T.reshape(1, D),
                            jnp.zeros((PACK_ROWS - R_SGU_B - 1, D), F32)], axis=0)
    (vp_all,) = _gather2([pack], ["ag"], "gather_pack")
    red, matpiece, dmod, gab, cpart = _reduce_small(vp_all, mat_parts, ada_w[0], me)
    mat_all, cparts = _gather2([matpiece, cpart], ["ag", "ag"], "gather_small")

    g_w_in, d_w_in, nm_w_in, nv_w_in = _adamw_reduced(win_parts, w_in[0], m_w_in[0], v_w_in[0], 256, "adamw_w_in")
    g_w_out, d_w_out, nm_w_out, nv_w_out = _adamw_reduced(wout_parts, w_out[0], m_w_out[0], v_w_out[0], 128,
                                                          "adamw_w_out")
    g_ada, d_ada, nm_ada, nv_ada = _adamw_ada(c_all, cc, dmod, ada_w[0], m_ada_w[0], v_ada_w[0], me)
    ws = {k: args[k].reshape(view[k]) for k in _SMALL}
    ms = {k: args["m_" + k].reshape(view[k]) for k in _SMALL}
    vs = {k: args["v_" + k].reshape(view[k]) for k in _SMALL}
    small = _adamw_small(red, mat_all.reshape(MAT_ROWS, HD), cparts, gab, ws, ms, vs, me)
    big = dict(w_in=(g_w_in, d_w_in, nm_w_in, nv_w_in), w_out=(g_w_out, d_w_out, nm_w_out, nv_w_out),
               ada_w=(g_ada, d_ada, nm_ada, nv_ada))

    loss = red[R_LOSS, 0]
    names = ("c_ctx", "ada_w", "ada_b", "norm_g", "w_in", "conv_w", "conv_b", "lru_wa", "lru_ba", "lru_wx", "lru_bx",
             "lru_lambda", "sgu_ln_g", "sgu_ln_b", "sgu_w", "sgu_b", "w_out", "final_g")
    outs = [loss, gx.reshape(x.shape)]
    for kind in range(4):
        for k in names:
            val = big[k][kind] if k in big else small[kind][k]
            outs.append(val.reshape(args[k].shape))
    return tuple(outs)
```

```python
import functools

import jax
import jax.numpy as jnp
from jax import lax
from jax.experimental import pallas as pl
from jax.experimental.pallas import tpu as pltpu

F32 = jnp.float32
BF16 = jnp.bfloat16

N_DEV = 8
D = 1024
L = 2048
LC = 256
HEADS = 8
HD = 128
CHUNK = 128
D_IN = 5 * D
W_IN_SHARD = D_IN // N_DEV
D_MIX = 2 * D
CONV_W = 4
LRU_C = 8.0
NORM_EPS = 1e-6
LN_EPS = 1e-5
ADAM_LR, ADAM_B1, ADAM_B2, ADAM_EPS, ADAM_WD, ADAM_STEP = 0.001, 0.9, 0.999, 1e-08, 0.01, 10

VMEM_LIMIT = 56 * 1024 * 1024

HBM = pl.BlockSpec(memory_space=pltpu.HBM)
VMEM = pl.BlockSpec(memory_space=pltpu.VMEM)
MESH = pl.DeviceIdType.MESH


def _call(body, **kw):
    return pl.pallas_call(body, **kw)


def _params(*sem):
    return pltpu.CompilerParams(dimension_semantics=sem, vmem_limit_bytes=VMEM_LIMIT)


def _sigmoid(x):
    return 0.5 * jnp.tanh(0.5 * x) + 0.5


def _silu_and_grad(x):
    s = _sigmoid(x)
    return x * s, s * (1.0 + x * (1.0 - s))


_G0 = 0.7978845608028654
_G1 = 0.044715


def _gelu_and_grad(x):
    x2 = x * x
    t = jnp.tanh(_G0 * (x + _G1 * x * x2))
    cdf = 0.5 * (1.0 + t)
    return x * cdf, cdf + 0.5 * x * (1.0 - t * t) * (_G0 * (1.0 + 3.0 * _G1 * x2))


def _gelu(x):
    return 0.5 * x * (1.0 + jnp.tanh(_G0 * (x + _G1 * x * x * x)))


def _softplus(z):
    t = jnp.exp(-jnp.abs(z))
    u = 1.0 + t
    log1p = jnp.where(u == 1.0, t, jnp.log(u) * t / jnp.where(u == 1.0, 1.0, u - 1.0))
    return jnp.maximum(z, 0.0) + log1p


def _dot(a, b):
    return jnp.dot(a, b, preferred_element_type=F32)


def _dot_nt(a, b):
    return lax.dot_general(a, b, (((1,), (1,)), ((), ())), preferred_element_type=F32)


def _dot_tn(a, b):
    return lax.dot_general(a, b, (((0,), (0,)), ((), ())), preferred_element_type=F32)


def _rows(shape):
    return lax.broadcasted_iota(jnp.int32, shape, 0)


def _shift_down(x, first):
    return jnp.where(_rows(x.shape) == 0, first, pltpu.roll(x, 1, 0))


def _shift_up(x, last):
    n = x.shape[0]
    return jnp.where(_rows(x.shape) == n - 1, last, pltpu.roll(x, n - 1, 0))


def _gather2(arrays, modes, name):
    n = len(arrays)

    def body(*refs):
        start, forward, finish = _gather2_ops(refs[:n], refs[n:2 * n], modes, *refs[2 * n:])
        start()
        forward()
        finish()

    return _call(
        body, name=name, out_shape=_gather2_shapes(arrays, modes), in_specs=[HBM] * n, out_specs=[HBM] * n,
        scratch_shapes=_gather2_sems(n), compiler_params=pltpu.CompilerParams(has_side_effects=True),
    )(*[pltpu.with_memory_space_constraint(a, pltpu.HBM) for a in arrays])


def _gather2_shapes(arrays, modes):
    return [jax.ShapeDtypeStruct((N_DEV,) + a.shape if m == "ag" else (a.shape[0], N_DEV * a.shape[1]), a.dtype)
            for a, m in zip(arrays, modes)]


def _gather2_sems(n):
    return [pltpu.SemaphoreType.DMA((n, N_DEV - 1)), pltpu.SemaphoreType.DMA((n, N_DEV - 1)),
            pltpu.SemaphoreType.DMA((n,))]


def _gather2_ops(ins, outs, modes, send_sems, recv_sems, local_sems):
    n = len(ins)
    x, y, c = lax.axis_index("x"), lax.axis_index("y"), lax.axis_index("c")
    me, sibling = (x, y, c), (x, y, 1 - c)
    chips = [(x ^ (k >> 1), y ^ (k & 1)) for k in (1, 2, 3)]

    def slot(j, px, py, pc):
        dev = 4 * px + 2 * py + pc
        if modes[j] == "agc":
            w = ins[j].shape[1]
            return outs[j].at[:, pl.ds(pl.multiple_of(dev * w, 128), w)]
        return outs[j].at[dev]

    def copy(j, k, block, to, src=None):
        return pltpu.make_async_remote_copy(
            src_ref=slot(j, *block) if src is None else src, dst_ref=slot(j, *block),
            send_sem=send_sems.at[j, k], recv_sem=recv_sems.at[j, k], device_id=to, device_id_type=MESH)

    def own(j):
        return pltpu.make_async_copy(ins[j], slot(j, *me), local_sems.at[j])

    def first(j):
        return [copy(j, 0, me, sibling, src=ins[j])] + [copy(j, 1 + i, me, (*chip, c), src=ins[j])
                                                        for i, chip in enumerate(chips)]

    def passed(j, i):
        return copy(j, 4 + i, (*chips[i], c), sibling)

    def start():
        for j in range(n):
            own(j).start()
            for cp in first(j):
                cp.start()

    def forward():
        for i, chip in enumerate(chips):
            for j in range(n):
                copy(j, 1 + i, (*chip, c), me).wait_recv()
                passed(j, i).start()

    def finish():
        for j in range(n):
            copy(j, 0, sibling, me).wait_recv()
            for i, chip in enumerate(chips):
                copy(j, 4 + i, (*chip, 1 - c), me).wait_recv()
            for cp in first(j) + [passed(j, i) for i in range(3)]:
                cp.wait_send()
            own(j).wait()

    return start, forward, finish


def _reduce2_local(arrays, modes, me, name, counts=None):
    n = len(arrays)
    counts = counts or [4] * n
    shapes = [(a.shape[1], a.shape[2]) if m == "a2a" else (a.shape[0], a.shape[1] // (2 * cnt))
              for a, m, cnt in zip(arrays, modes, counts)]
    staged = [jax.ShapeDtypeStruct((cnt,) + s, a.dtype) for s, a, cnt in zip(shapes, arrays, counts)]

    def piece(ref, mode, dev, w):
        return ref.at[dev] if mode == "a2a" else ref.at[:, pl.ds(pl.multiple_of(dev * w, 128), w)]

    def to_sibling(*refs):
        ins, outs = refs[:n], refs[n:2 * n]
        send_sems, recv_sems = refs[2 * n:]
        x, y, c = lax.axis_index("x"), lax.axis_index("y"), lax.axis_index("c")
        copies = []
        for j in range(n):
            for q in range(counts[j]):
                cp = pltpu.make_async_remote_copy(
                    src_ref=piece(ins[j], modes[j], 2 * q + (1 - c), shapes[j][1]), dst_ref=outs[j].at[q],
                    send_sem=send_sems.at[j, q], recv_sem=recv_sems.at[j, q], device_id=(x, y, 1 - c),
                    device_id_type=MESH)
                cp.start()
                copies.append(cp)
        for cp in copies:
            cp.wait()

    stage = _call(
        to_sibling, name=name + "_d2d", out_shape=staged, in_specs=[HBM] * n, out_specs=[HBM] * n,
        scratch_shapes=[pltpu.SemaphoreType.DMA((n, 4)), pltpu.SemaphoreType.DMA((n, 4))],
        compiler_params=pltpu.CompilerParams(has_side_effects=True),
    )(*[pltpu.with_memory_space_constraint(a, pltpu.HBM) for a in arrays])

    def add(me_ref, *refs):
        del me_ref
        own, got, outs = refs[:n], refs[n:2 * n], refs[2 * n:]
        for j in range(n):
            mine = own[j][0] if modes[j] == "a2a" else own[j][...]
            outs[j][0] = (mine.astype(F32) + got[j][0].astype(F32)).astype(outs[j].dtype)

    in_specs, slot_specs = [], []
    for (r, w), m, cnt in zip(shapes, modes, counts):
        if m == "a2a":
            in_specs.append(pl.BlockSpec(
                (1, r, w), lambda q, me_ref, cnt=cnt: (2 * jnp.minimum(q, cnt - 1) + me_ref[0] % 2, 0, 0)))
        else:
            in_specs.append(pl.BlockSpec(
                (r, w), lambda q, me_ref, cnt=cnt: (0, 2 * jnp.minimum(q, cnt - 1) + me_ref[0] % 2)))
        slot_specs.append(pl.BlockSpec((1, r, w), lambda q, me_ref, cnt=cnt: (jnp.minimum(q, cnt - 1), 0, 0)))
    return _call(
        add, name=name + "_add", out_shape=staged,
        grid_spec=pltpu.PrefetchScalarGridSpec(num_scalar_prefetch=1, grid=(max(counts),),
                                               in_specs=in_specs + slot_specs, out_specs=slot_specs),
        compiler_params=_params("arbitrary"),
    )(me, *arrays, *stage)


def _chips_sems(n):
    return [pltpu.SemaphoreType.DMA((n, 6)), pltpu.SemaphoreType.DMA((n, 6)), pltpu.SemaphoreType.DMA((n,))]


def _chips_stage_shapes(chip_sums):
    return [jax.ShapeDtypeStruct((2, a.shape[1] // 2, a.shape[2]), a.dtype) for a in chip_sums]


def _chips_ops(ins, outs, stages, send_sems, recv_sems, local_sems, first_chips=None):
    x, y, c = lax.axis_index("x"), lax.axis_index("y"), lax.axis_index("c")
    qm = 2 * x + y
    first_chips = first_chips or [0] * len(ins)

    def owns(j, chip):
        lo, cnt = first_chips[j], ins[j].shape[0]
        if lo == 0 and cnt == 4:
            return None
        return jnp.logical_and(chip >= lo, chip < lo + cnt)

    def guarded(cond, fn):
        if cond is None:
            fn()
        else:
            pl.when(cond)(fn)

    def slot(j, chip):
        return jnp.clip(chip - first_chips[j], 0, ins[j].shape[0] - 1)

    def half(j, i):
        h = ins[j].shape[1] // 2
        return pl.ds(i * h, h)

    def copy(j, sem, src, dst, k):
        return pltpu.make_async_remote_copy(
            src_ref=src, dst_ref=dst, send_sem=send_sems.at[j, sem], recv_sem=recv_sems.at[j, sem],
            device_id=(x ^ (k >> 1), y ^ (k & 1), c), device_id_type=MESH)

    def direct(j, k):
        return copy(j, k - 1, ins[j].at[slot(j, qm ^ k)], outs[j].at[qm], k)

    def first_hop(j, k):
        return copy(j, 1 + k, ins[j].at[slot(j, qm ^ 3), half(j, k - 1)], stages[j].at[k - 1], k)

    def second_hop(j, k):
        return copy(j, 3 + k, stages[j].at[2 - k], outs[j].at[qm ^ (3 - k), half(j, 2 - k)], k)

    def local(j):
        return pltpu.make_async_copy(ins[j].at[slot(j, qm)], outs[j].at[qm], local_sems.at[j])

    def start():
        for j in range(len(ins)):
            for k in (1, 2):
                guarded(owns(j, qm ^ k), lambda j=j, k=k: direct(j, k).start())
                guarded(owns(j, qm ^ 3), lambda j=j, k=k: first_hop(j, k).start())
            guarded(owns(j, qm), lambda j=j: local(j).start())

    def forward():
        for j in range(len(ins)):
            for k in (1, 2):
                def pass_on(j=j, k=k):
                    first_hop(j, 3 - k).wait_recv()
                    second_hop(j, k).start()
                guarded(owns(j, qm ^ k), pass_on)

    def finish():
        for j in range(len(ins)):
            for k in (1, 2):
                guarded(owns(j, qm ^ k), lambda j=j, k=k: direct(j, k).wait_send())
                guarded(owns(j, qm ^ k), lambda j=j, k=k: second_hop(j, k).wait_send())
                guarded(owns(j, qm ^ 3), lambda j=j, k=k: first_hop(j, k).wait_send())
                guarded(owns(j, qm), lambda j=j, k=k: direct(j, k).wait_recv())
                guarded(owns(j, qm), lambda j=j, k=k: second_hop(j, k).wait_recv())
            guarded(owns(j, qm), lambda j=j: local(j).wait())

    return start, forward, finish


def _front(c, c_ctx, ada_w, ada_b, w_in, w_out, me):
    nloc = ada_w.shape[1]

    def body(me_ref, c_ref, cc_ref, aw_ref, ab_ref, win_ref, wout_ref,
             wfull_ref, woutb_ref, modx_ref, modc_ref, call_ref,
             wb_s, part_s, parts_s, w_send, w_recv, w_local, s_send, s_recv):
        x, y, cidx = lax.axis_index("x"), lax.axis_index("y"), lax.axis_index("c")
        me = me_ref[0]
        wb_s[...] = win_ref[...].astype(BF16)
        woutb_ref[...] = wout_ref[...].astype(BF16)
        start, forward, finish = _gather2_ops([wb_s], [wfull_ref], ["agc"], w_send, w_recv, w_local)
        start()

        def small_gather(src, my_slot, stage):
            copies = []
            for k in range(1, N_DEV):
                peer = (x ^ (k >> 2), y ^ ((k >> 1) & 1), cidx ^ (k & 1))
                cp = pltpu.make_async_remote_copy(src_ref=src, dst_ref=my_slot, send_sem=s_send.at[stage, k - 1],
                                                  recv_sem=s_recv.at[stage, k - 1], device_id=peer,
                                                  device_id_type=MESH)
                cp.start()
                copies.append(cp)
            pltpu.sync_copy(src, my_slot)
            for cp in copies:
                cp.wait()

        small_gather(c_ref, call_ref.at[pl.ds(me, 1), :], 0)
        off = pl.multiple_of(me * nloc, 128)
        b = ab_ref[:, pl.ds(off, nloc)]
        w = aw_ref[...]
        sx, _ = _silu_and_grad(call_ref[...])
        sc, _ = _silu_and_grad(jnp.broadcast_to(cc_ref[...], (8, D)))
        part_s[0:8, :] = _dot(sx, w) + b
        part_s[8:16, :] = _dot(sc, w) + b
        small_gather(part_s, parts_s.at[me], 1)
        mine = _rows((16, nloc)) == me
        for j in range(N_DEV):
            pj = parts_s[j]
            modx_ref[:, j * nloc:(j + 1) * nloc] = jnp.sum(jnp.where(mine, pj, 0.0), axis=0, keepdims=True)
            modc_ref[:, j * nloc:(j + 1) * nloc] = pj[8:9, :]
        forward()
        finish()

    return _call(
        body, name="front",
        out_shape=[jax.ShapeDtypeStruct((D, D_IN), BF16), jax.ShapeDtypeStruct(w_out.shape, BF16),
                   jax.ShapeDtypeStruct((1, 3 * D), F32), jax.ShapeDtypeStruct((1, 3 * D), F32),
                   jax.ShapeDtypeStruct((N_DEV, D), F32)],
        in_specs=[pl.BlockSpec(memory_space=pltpu.SMEM)] + [VMEM] * 6, out_specs=[HBM, VMEM, VMEM, VMEM, VMEM],
        scratch_shapes=[pltpu.VMEM(w_in.shape, BF16), pltpu.VMEM((16, nloc), F32),
                        pltpu.VMEM((N_DEV, 16, nloc), F32)] + _gather2_sems(1) +
                       [pltpu.SemaphoreType.DMA((2, N_DEV - 1)), pltpu.SemaphoreType.DMA((2, N_DEV - 1))],
        compiler_params=pltpu.CompilerParams(vmem_limit_bytes=VMEM_LIMIT, has_side_effects=True),
    )(me, c, c_ctx, ada_w, ada_b, w_in, w_out)


def _project(xr, mod, ng, w, ncols, tm, name, gather=None, gather_modes=()):
    rows = xr.shape[0]
    steps = rows // tm
    ng_ = len(gather or ())

    def body(x_ref, sh_ref, sc_ref, ng_ref, w_ref, *rest):
        z_ref, hn_ref = rest[ng_:ng_ + 2]
        if ng_:
            start, forward, finish = _gather2_ops(rest[:ng_], rest[ng_ + 2:2 * ng_ + 2], gather_modes,
                                                  *rest[2 * ng_ + 2:])
            pl.when(pl.program_id(0) == 0)(start)
            pl.when(pl.program_id(0) == steps // 2)(forward)
        x = x_ref[...]
        rs = lax.rsqrt(jnp.mean(x * x, axis=-1, keepdims=True) + NORM_EPS)
        hn = (x * rs * ng_ref[...]) * (1.0 + sc_ref[...]) + sh_ref[...]
        hb = hn.astype(BF16)
        hn_ref[...] = hb
        for n in range(ncols // D):
            z_ref[:, n * D:(n + 1) * D] = _dot(hb, w_ref[:, n * D:(n + 1) * D])
        if ng_:
            pl.when(pl.program_id(0) == steps - 1)(finish)

    vec = pl.BlockSpec((1, D), lambda i: (0, 0))
    gathered = _gather2_shapes(gather, gather_modes) if ng_ else []
    return _call(
        body, name=name, grid=(steps,),
        out_shape=[jax.ShapeDtypeStruct((rows, ncols), F32), jax.ShapeDtypeStruct((rows, D), BF16)] + gathered,
        in_specs=[pl.BlockSpec((tm, D), lambda i: (i, 0)), vec, pl.BlockSpec((1, D), lambda i: (0, 1)), vec,
                  pl.BlockSpec((D, ncols), lambda i: (0, 0), pipeline_mode=pl.Buffered(1))] + [HBM] * ng_,
        out_specs=[pl.BlockSpec((tm, ncols), lambda i: (i, 0)), pl.BlockSpec((tm, D), lambda i: (i, 0))] + [HBM] * ng_,
        scratch_shapes=_gather2_sems(ng_) if ng_ else [],
        compiler_params=pltpu.CompilerParams(dimension_semantics=("arbitrary",), vmem_limit_bytes=VMEM_LIMIT,
                                             has_side_effects=bool(ng_)),
    )(xr, mod, mod, ng, w, *[pltpu.with_memory_space_constraint(a, pltpu.HBM) for a in gather or ()])


def _scan_pair(af_ref, uf_ref, hf_ref, h0f, ab_ref, ub_ref, hb_ref, h0b, t_len):
    span = 8 * SCAN_BLOCKS
    nit = t_len // span
    rows = _rows((8, HD))

    def local_scan(a, b, forward):
        for s in (1, 2, 4):
            sh = s if forward else 8 - s
            m = rows >= s if forward else rows < 8 - s
            b = a * jnp.where(m, pltpu.roll(b, sh, 0), 0.0) + b
            a = a * jnp.where(m, pltpu.roll(a, sh, 0), 1.0)
        return a, b

    def span_scan(a_ref, u_ref, h_ref, off, carry, forward):
        order = range(SCAN_BLOCKS) if forward else range(SCAN_BLOCKS - 1, -1, -1)
        last = slice(7, 8) if forward else slice(0, 1)
        for q in order:
            rs = pl.ds(off + 8 * q, 8)
            a, b = local_scan(a_ref[rs, :], u_ref[rs, :], forward)
            h_ref[rs, :] = b + a * carry
            carry = a[last, :] * carry + b[last, :]
        return carry

    def body(k, carry):
        cf, cb = carry
        cf = span_scan(af_ref, uf_ref, hf_ref, pl.multiple_of(k * span, span), cf, True)
        cb = span_scan(ab_ref, ub_ref, hb_ref, pl.multiple_of((nit - 1 - k) * span, span), cb, False)
        return cf, cb

    return lax.fori_loop(0, nit, body, (h0f, h0b))


SCAN_BLOCKS = 4


def _conv(xa, cw, cb):
    z = jnp.zeros((1, HD), F32)
    xm1 = _shift_down(xa, z)
    xp1 = _shift_up(xa, z)
    xp2 = _shift_up(xp1, z)
    return xm1 * cw[0:1, :] + xa * cw[1:2, :] + xp1 * cw[2:3, :] + xp2 * cw[3:4, :] + cb


def _gates(xc, wa, wx, ba, bx, nsp):
    xb = xc.astype(BF16)
    r = _sigmoid(_dot(xb, wa) + ba)
    i = _sigmoid(_dot(xb, wx) + bx)
    log_a = r * nsp
    a = jnp.exp(log_a)
    g2 = jnp.tanh(log_a) * (-1.0 - a * a)
    rg = lax.rsqrt(jnp.maximum(g2, 1e-30))
    return r, i, a, g2 * rg, rg


def _lru_param_specs():
    h4 = pl.BlockSpec((2, 1, HD, HD), lambda h: (0, h, 0, 0))
    v2 = pl.BlockSpec((2, HD), lambda h: (0, h))
    return dict(
        xa=pl.BlockSpec((L, HD), lambda h: (0, h)), xac=pl.BlockSpec((LC, HD), lambda h: (0, h)),
        cw=pl.BlockSpec((CONV_W, HD), lambda h: (0, h)), cb=pl.BlockSpec((1, HD), lambda h: (0, h)), h4=h4, v2=v2)


def _lru_forward(zx, zc, cw, cb, wa, wx, ba, bx, lam):
    def body(xa_ref, xac_ref, cw_ref, cb_ref, wa_ref, wx_ref, ba_ref, bx_ref, lam_ref, yl_ref,
             af, uf, hf, ab, ub, hb):
        cwv, cbv = cw_ref[...], cb_ref[...]
        nsp = (-LRU_C) * _softplus(-lam_ref[...])

        def forward(xa, t_len, h0f, h0b):
            xc = _conv(xa, cwv, cbv)
            for d, (a_ref, u_ref) in enumerate(((af, uf), (ab, ub))):
                _, i, a, gamma, _ = _gates(xc, wa_ref[d, 0].astype(BF16), wx_ref[d, 0].astype(BF16),
                                           ba_ref[d:d + 1, :], bx_ref[d:d + 1, :], nsp[d:d + 1, :])
                a_ref[0:t_len, :] = a
                u_ref[0:t_len, :] = gamma * (i * xc)
            return _scan_pair(af, uf, hf, h0f, ab, ub, hb, h0b, t_len)

        z = jnp.zeros((1, HD), F32)
        h0f, h0b = forward(xac_ref[...], LC, z, z)
        forward(xa_ref[...], L, h0f, h0b)
        yl_ref[...] = hf[...] + hb[...]

    s = _lru_param_specs()
    return _call(
        body, name="lru_forward", grid=(HEADS,), out_shape=jax.ShapeDtypeStruct((L, D), F32),
        in_specs=[s["xa"], s["xac"], s["cw"], s["cb"], s["h4"], s["h4"], s["v2"], s["v2"], s["v2"]],
        out_specs=pl.BlockSpec((L, HD), lambda h: (0, h)),
        scratch_shapes=[pltpu.VMEM((L, HD), F32)] * 6,
        compiler_params=_params("arbitrary"),
    )(zx, zc, cw, cb, wa, wx, ba, bx, lam)


def _lru_backward(zx, zc, dyl, dz, cw, cb, wa, wx, ba, bx, lam, chip_sums, first_chips=None):
    nr = len(chip_sums)

    def body(xa_ref, xac_ref, dyl_ref, dz_in, cw_ref, cb_ref, wa_ref, wx_ref, ba_ref, bx_ref, lam_ref, *rest):
        (dxa_ref, dxac_ref, dwa_ref, dwx_ref, dba_ref, dbx_ref, dlam_ref, dcw_ref,
         dcb_ref) = rest[nr:nr + 9]
        main_s, ctx_s = rest[3 * nr + 9:3 * nr + 11]
        if nr:
            start, forward, finish = _chips_ops(rest[:nr], rest[nr + 9:2 * nr + 9], rest[2 * nr + 9:3 * nr + 9],
                                                *rest[3 * nr + 11:], first_chips=first_chips)
            pl.when(pl.program_id(0) == 0)(start)
            pl.when(pl.program_id(0) == 2)(forward)
            pl.when(pl.program_id(0) == HEADS - 1)(finish)
        del dz_in
        cwv, cbv = cw_ref[...], cb_ref[...]
        lamv = lam_ref[...]
        sp = _softplus(-lamv)
        nsp = (-LRU_C) * sp
        z = jnp.zeros((1, HD), F32)

        def wmat(ref, d):
            return ref[d, 0].astype(BF16)

        def workspace(s):
            return dict(a=(s.at[0], s.at[1]), u=(s.at[2], s.at[3]), h=(s.at[4], s.at[5]), rho=(s.at[6], s.at[7]),
                        saved=(tuple(s.at[8 + k] for k in range(4)), tuple(s.at[12 + k] for k in range(4))),
                        xc=s.at[16])

        def forward(ws, xa, t_len, h0f, h0b):
            xc = _conv(xa, cwv, cbv)
            ws["xc"][...] = xc
            for d in (0, 1):
                vals = _gates(xc, wmat(wa_ref, d), wmat(wx_ref, d), ba_ref[d:d + 1, :], bx_ref[d:d + 1, :],
                              nsp[d:d + 1, :])
                r, i, a, gamma, rg = vals
                ws["a"][d][...] = a
                ws["u"][d][...] = gamma * (i * xc)
                for ref, val in zip(ws["saved"][d], (r, i, gamma, rg)):
                    ref[...] = val
            return _scan_pair(ws["a"][0], ws["u"][0], ws["h"][0], h0f, ws["a"][1], ws["u"][1], ws["h"][1], h0b,
                              t_len)

        def backward(ws, xa, t_len, h0f, h0b, dhf, dhb, first):
            xc = ws["xc"][...]
            (af, ab), (uf, ub), (hf, hb), (rf, rb) = ws["a"], ws["u"], ws["h"], ws["rho"]
            uf[...] = ab[...] * dhb
            ub[...] = af[...] * dhf
            rho_b_last, rho_f_first = _scan_pair(ab, uf, rb, z, af, ub, rf, z, t_len)
            dxc = jnp.zeros((t_len, HD), F32)
            dsp = []
            for d in (0, 1):
                r, i, gamma, rg = (ref[...] for ref in ws["saved"][d])
                a = ws["a"][d][...]
                if d == 0:
                    lam_t = dhf + _shift_up(rf[...], z)
                    h_prev = _shift_down(hf[...], h0f)
                else:
                    lam_t = dhb + _shift_down(rb[...], z)
                    h_prev = _shift_up(hb[...], h0b)
                da = lam_t * h_prev
                lx = lam_t * xc
                d_i = lx * gamma
                d_gamma = lx * i
                dxc = dxc + lam_t * (gamma * i)
                d_log_a = a * (da - d_gamma * (a * rg))
                dsp.append(jnp.sum(d_log_a * r, axis=0, keepdims=True) * (-LRU_C))
                d_pre_r = d_log_a * nsp[d:d + 1, :] * (r * (1.0 - r))
                d_pre_i = d_i * (i * (1.0 - i))
                prb, pib, xb = d_pre_r.astype(BF16), d_pre_i.astype(BF16), xc.astype(BF16)
                dxc = dxc + _dot_nt(prb, wmat(wa_ref, d)) + _dot_nt(pib, wmat(wx_ref, d))
                g_wa, g_wx = _dot_tn(xb, prb), _dot_tn(xb, pib)
                g_ba = jnp.sum(d_pre_r, axis=0, keepdims=True)
                g_bx = jnp.sum(d_pre_i, axis=0, keepdims=True)
                if first:
                    dwa_ref[d, 0] = g_wa
                    dwx_ref[d, 0] = g_wx
                    dba_ref[d:d + 1, :] = g_ba
                    dbx_ref[d:d + 1, :] = g_bx
                else:
                    dwa_ref[d, 0] += g_wa
                    dwx_ref[d, 0] += g_wx
                    dba_ref[d:d + 1, :] += g_ba
                    dbx_ref[d:d + 1, :] += g_bx
            g_lam = jnp.concatenate(dsp, axis=0) * (-_sigmoid(-lamv))
            dm1 = _shift_down(dxc, z)
            dp1 = _shift_up(dxc, z)
            dm2 = _shift_down(dm1, z)
            dxa = dp1 * cwv[0:1, :] + dxc * cwv[1:2, :] + dm1 * cwv[2:3, :] + dm2 * cwv[3:4, :]
            xm1 = _shift_down(xa, z)
            xp1 = _shift_up(xa, z)
            xp2 = _shift_up(xp1, z)
            g_cw = jnp.concatenate([jnp.sum(dxc * v, axis=0, keepdims=True) for v in (xm1, xa, xp1, xp2)], axis=0)
            g_cb = jnp.sum(dxc, axis=0, keepdims=True)
            if first:
                dlam_ref[...] = g_lam
                dcw_ref[...] = g_cw
                dcb_ref[...] = g_cb
            else:
                dlam_ref[...] += g_lam
                dcw_ref[...] += g_cw
                dcb_ref[...] += g_cb
            return dxa, rho_f_first, rho_b_last

        ws_x, ws_c = workspace(main_s), workspace(ctx_s)
        h0f, h0b = forward(ws_c, xac_ref[...], LC, z, z)
        forward(ws_x, xa_ref[...], L, h0f, h0b)
        dh = dyl_ref[...]
        dxa, dh0f, dh0b = backward(ws_x, xa_ref[...], L, h0f, h0b, dh, dh, True)
        dxa_ref[...] = dxa.astype(BF16)
        rc = _rows((LC, HD))
        dxac, _, _ = backward(ws_c, xac_ref[...], LC, z, z, jnp.where(rc == LC - 1, dh0f, 0.0),
                              jnp.where(rc == 0, dh0b, 0.0), False)
        dxac_ref[...] = dxac.astype(BF16)

    s = _lru_param_specs()
    col = lambda r: pl.BlockSpec((r, HD), lambda h: (0, h))
    return _call(
        body, name="lru_backward", grid=(HEADS,),
        out_shape=[jax.ShapeDtypeStruct((L, D_IN), BF16), jax.ShapeDtypeStruct((LC, D), BF16),
                   jax.ShapeDtypeStruct((2, HEADS, HD, HD), F32), jax.ShapeDtypeStruct((2, HEADS, HD, HD), F32),
                   jax.ShapeDtypeStruct((2, D), F32), jax.ShapeDtypeStruct((2, D), F32),
                   jax.ShapeDtypeStruct((2, D), F32), jax.ShapeDtypeStruct((CONV_W, D), F32),
                   jax.ShapeDtypeStruct((1, D), F32)] + [jax.ShapeDtypeStruct((4,) + a.shape[1:], a.dtype)
                                                          for a in chip_sums] + _chips_stage_shapes(chip_sums),
        in_specs=[s["xa"], s["xac"], col(L), pl.BlockSpec(memory_space=pl.ANY), s["cw"], s["cb"], s["h4"], s["h4"],
                  s["v2"], s["v2"], s["v2"]] + [HBM] * nr,
        out_specs=[col(L), col(LC), s["h4"], s["h4"], s["v2"], s["v2"], s["v2"], col(CONV_W), col(1)]
        + [HBM] * (2 * nr),
        scratch_shapes=[pltpu.VMEM((17, L, HD), F32), pltpu.VMEM((17, LC, HD), F32)] + (_chips_sems(nr) if nr else []),
        input_output_aliases={3: 0},
        compiler_params=pltpu.CompilerParams(dimension_semantics=("arbitrary",), vmem_limit_bytes=VMEM_LIMIT,
                                             has_side_effects=True),
    )(zx, zc, dyl, dz, cw, cb, wa, wx, ba, bx, lam, *[pltpu.with_memory_space_constraint(a, pltpu.HBM)
                                                       for a in chip_sums])


def _mixer_loss(x, tgt, zx, yl, gx, fg, lng, lnb, ws, wst, bst, wout, tm):
    ncht = tm // CHUNK

    def body(x_ref, t_ref, ga_ref, u_ref, v_ref, gb_ref, yl_ref, gx_ref, fg_ref, lng_ref, lnb_ref, ws_ref, wst_ref,
             bst_ref, wout_ref,
             dz_ref, dyl_ref, dxn_ref, y_s, do_ref, dws_ref, dbst_ref, vec_ref,
             vn_s, mix_s, dm_s, dvn_s):
        step = pl.program_id(0)

        @pl.when(step == 0)
        def _():
            dws_ref[...] = jnp.zeros_like(dws_ref)
            dbst_ref[...] = jnp.zeros_like(dbst_ref)
            vec_ref[...] = jnp.zeros_like(vec_ref)

        u, v = u_ref[...], v_ref[...]
        ug, dug_du = _gelu_and_grad(u)
        vg, dvg_dv = _gelu_and_grad(v)
        mu = jnp.mean(vg, axis=-1, keepdims=True)
        vc = vg - mu
        rstd = lax.rsqrt(jnp.mean(vc * vc, axis=-1, keepdims=True) + LN_EPS)
        vhat = vc * rstd
        lngv = lng_ref[...]
        vn_s[...] = (vhat * lngv + lnb_ref[...]).astype(BF16)
        for ch in range(ncht):
            rs = slice(ch * CHUNK, (ch + 1) * CHUNK)
            for g in range(HEADS):
                cs = slice(g * HD, (g + 1) * HD)
                mix_s[rs, cs] = _dot(ws_ref[g], vn_s[rs, cs]) + bst_ref[:, g:g + 1]
        mixed = mix_s[...]
        ga, gb, yl = ga_ref[...], gb_ref[...], yl_ref[...]
        sga, dsga = _silu_and_grad(ga)
        sgb, dsgb = _silu_and_grad(gb)
        ys = ug * mixed
        y_s[:, 0:D] = (yl * sga).astype(BF16)
        y_s[:, D:D_MIX] = (ys * sgb).astype(BF16)
        o = _dot(y_s[...], wout_ref[...])
        gxv, fgv = gx_ref[...], fg_ref[...]
        xn = x_ref[...] + gxv * o
        rs2 = lax.rsqrt(jnp.mean(xn * xn, axis=-1, keepdims=True) + NORM_EPS)
        xh = xn * rs2
        diff = xh * fgv - t_ref[...]
        vec_ref[R_LOSS:R_LOSS + 1, :] += jnp.full((1, D), jnp.sum(diff * diff) * (0.5 / D), F32)
        dout = diff * (1.0 / D)
        w = dout * fgv
        dxn = rs2 * (w - xh * jnp.mean(w * xh, axis=-1, keepdims=True))
        dxn_ref[...] = dxn
        vec_ref[0:1, :] += jnp.sum(dxn * o, axis=0, keepdims=True)
        vec_ref[1:2, :] += jnp.sum(dout * xh, axis=0, keepdims=True)
        dob = (dxn * gxv).astype(BF16)
        do_ref[...] = dob
        dy = _dot_nt(dob, wout_ref[...])
        dya, dyb = dy[:, 0:D], dy[:, D:D_MIX]
        dyl_ref[...] = dya * sga
        dys = dyb * sgb
        dz_ref[:, 0:D] = jnp.zeros((tm, D), BF16)
        dz_ref[:, D:2 * D] = (dya * yl * dsga).astype(BF16)
        dz_ref[:, 2 * D:3 * D] = (dys * mixed * dug_du).astype(BF16)
        dz_ref[:, 4 * D:5 * D] = (dyb * ys * dsgb).astype(BF16)
        dm = dys * ug
        dm_s[...] = dm.astype(BF16)
        for g in range(HEADS):
            cs = slice(g * HD, (g + 1) * HD)
            dbst_ref[:, g:g + 1] += sum(jnp.sum(dm[ch * CHUNK:(ch + 1) * CHUNK, cs], axis=1, keepdims=True)
                                        for ch in range(ncht))
            for ch in range(ncht):
                rs = slice(ch * CHUNK, (ch + 1) * CHUNK)
                dws_ref[g] += _dot_nt(dm_s[rs, cs], vn_s[rs, cs])
                dvn_s[rs, cs] = _dot(wst_ref[g], dm_s[rs, cs])
        dvn = dvn_s[...]
        vec_ref[2:3, :] += jnp.sum(dvn * vhat, axis=0, keepdims=True)
        vec_ref[3:4, :] += jnp.sum(dvn, axis=0, keepdims=True)
        dvh = dvn * lngv
        dvg = rstd * (dvh - jnp.mean(dvh, axis=-1, keepdims=True) - vhat * jnp.mean(dvh * vhat, axis=-1, keepdims=True))
        dz_ref[:, 3 * D:4 * D] = (dvg * dvg_dv).astype(BF16)

    tile = pl.BlockSpec((tm, D), lambda i: (i, 0))
    zcol = lambda n: pl.BlockSpec((tm, D), lambda i: (i, n))
    vec = pl.BlockSpec((1, D), lambda i: (0, 0))
    full = lambda *s: pl.BlockSpec(s, lambda i: (0,) * len(s))
    return _call(
        body, name="mixer_loss", grid=(L // tm,),
        out_shape=[jax.ShapeDtypeStruct((L, D_IN), BF16), jax.ShapeDtypeStruct((L, D), F32),
                   jax.ShapeDtypeStruct((L, D), F32), jax.ShapeDtypeStruct((L, D_MIX), BF16),
                   jax.ShapeDtypeStruct((L, D), BF16),
                   jax.ShapeDtypeStruct((HEADS, CHUNK, CHUNK), F32), jax.ShapeDtypeStruct((CHUNK, HEADS), F32),
                   jax.ShapeDtypeStruct((8, D), F32)],
        in_specs=[tile, tile, zcol(1), zcol(2), zcol(3), zcol(4), tile, pl.BlockSpec((1, D), lambda i: (0, 2)),
                  vec, vec, vec,
                  full(HEADS, CHUNK, CHUNK), full(HEADS, CHUNK, CHUNK), full(CHUNK, HEADS),
                  pl.BlockSpec((D_MIX, D), lambda i: (0, 0), pipeline_mode=pl.Buffered(1))],
        out_specs=[pl.BlockSpec((tm, D_IN), lambda i: (i, 0)), tile, tile,
                   pl.BlockSpec((tm, D_MIX), lambda i: (i, 0)), tile,
                   full(HEADS, CHUNK, CHUNK), full(CHUNK, HEADS), full(8, D)],
        scratch_shapes=[pltpu.VMEM((tm, D), BF16), pltpu.VMEM((tm, D), F32),
                        pltpu.VMEM((tm, D), BF16), pltpu.VMEM((tm, D), F32)],
        compiler_params=_params("arbitrary"),
    )(x, tgt, zx, zx, zx, zx, yl, gx, fg, lng, lnb, ws, wst, bst, wout)


def _grad_w(a, b, a2, b2, tk, name, bw=D, first=0, nblocks=None):
    nk = a.shape[0] // tk
    m = a.shape[1]
    nblocks = nblocks or b.shape[1] // bw
    with_ctx = a2 is not None

    def body(*refs):
        if with_ctx:
            a_ref, b_ref, a2_ref, b2_ref, o_ref, acc = refs
        else:
            a_ref, b_ref, o_ref, acc = refs
        n, k = pl.program_id(0), pl.program_id(1)

        @pl.when(k == 0)
        def _():
            acc[...] = jnp.zeros_like(acc)

        acc[...] += _dot_tn(a_ref[...], b_ref[...])

        if with_ctx:
            @pl.when(jnp.logical_and(k == nk - 1, n == 0))
            def _():
                acc[:, 0:b2_ref.shape[1]] += _dot_tn(a2_ref[...], b2_ref[...])

        @pl.when(k == nk - 1)
        def _():
            o_ref[...] = acc[...].astype(BF16)

    in_specs = [pl.BlockSpec((tk, m), lambda n, k: (k, 0)), pl.BlockSpec((tk, bw), lambda n, k: (k, n + first))]
    args = [a, b]
    if with_ctx:
        in_specs += [pl.BlockSpec(a2.shape, lambda n, k: (0, 0)), pl.BlockSpec(b2.shape, lambda n, k: (0, 0))]
        args += [a2, b2]
    return _call(
        body, name=name, grid=(nblocks, nk), out_shape=jax.ShapeDtypeStruct((m, nblocks * bw), BF16),
        in_specs=in_specs, out_specs=pl.BlockSpec((m, bw), lambda n, k: (0, n)),
        scratch_shapes=[pltpu.VMEM((m, bw), F32)],
        compiler_params=_params("arbitrary", "arbitrary"),
    )(*args)


def _grad_rows(xr, dz, w, mod, ng, dres, ncols, tm, name, chip_sums=(), first_chips=None, dests=None):
    rows = xr.shape[0]
    steps = rows // tm
    with_dx = dres is not None
    nr = len(chip_sums)
    dests = [d for d in (dests or [None] * nr)]
    nd = sum(d is not None for d in dests)
    nin = 6 if with_dx else 5
    nout = 2 if with_dx else 1

    def body(*refs):
        if with_dx:
            x_ref, dz_ref, w_ref, sc_ref, ng_ref, dres_ref = refs[:nin]
            dx_ref, vec_ref = refs[nin + nr + nd:nin + nr + nd + nout]
        else:
            x_ref, dz_ref, w_ref, sc_ref, ng_ref = refs[:nin]
            (vec_ref,) = refs[nin + nr + nd:nin + nr + nd + nout]
        if nr:
            o0 = nin + nr + nd + nout
            start, forward, finish = _chips_ops(refs[nin:nin + nr], refs[o0:o0 + nr], refs[o0 + nr:o0 + 2 * nr],
                                                *refs[o0 + 2 * nr:], first_chips=first_chips)
            pl.when(pl.program_id(0) == 0)(start)
            pl.when(pl.program_id(0) == 2)(forward)
            pl.when(pl.program_id(0) == steps - 1)(finish)

        @pl.when(pl.program_id(0) == 0)
        def _():
            vec_ref[...] = jnp.zeros_like(vec_ref)

        dhn = _dot_nt(dz_ref[...], w_ref[...])
        x = x_ref[...]
        rs = lax.rsqrt(jnp.mean(x * x, axis=-1, keepdims=True) + NORM_EPS)
        xh = x * rs
        ngv = ng_ref[...]
        y = xh * ngv
        vec_ref[0:1, :] += jnp.sum(dhn, axis=0, keepdims=True)
        vec_ref[1:2, :] += jnp.sum(dhn * y, axis=0, keepdims=True)
        dy = dhn * (1.0 + sc_ref[...])
        vec_ref[2:3, :] += jnp.sum(dy * xh, axis=0, keepdims=True)
        if with_dx:
            dxh = dy * ngv
            dx_ref[...] = dres_ref[...] + rs * (dxh - xh * jnp.mean(dxh * xh, axis=-1, keepdims=True))

    tile = pl.BlockSpec((tm, D), lambda i: (i, 0))
    vec = pl.BlockSpec((1, D), lambda i: (0, 0))
    in_specs = [tile, pl.BlockSpec((tm, ncols), lambda i: (i, 0)),
                pl.BlockSpec((D, ncols), lambda i: (0, 0), pipeline_mode=pl.Buffered(1)),
                pl.BlockSpec((1, D), lambda i: (0, 1)), vec]
    out_shape = [jax.ShapeDtypeStruct((8, D), F32)]
    out_specs = [pl.BlockSpec((8, D), lambda i: (0, 0))]
    args = [xr, dz, w, mod, ng]
    if with_dx:
        in_specs.append(tile)
        out_shape.insert(0, jax.ShapeDtypeStruct((rows, D), F32))
        out_specs.insert(0, tile)
        args.append(dres)
    aliases = {}
    for j, d in enumerate(dests):
        if d is not None:
            aliases[len(args) + nr + len(aliases)] = len(out_shape) + j
    in_specs += [HBM] * (nr + nd)
    out_specs += [HBM] * (2 * nr)
    out_shape += [jax.ShapeDtypeStruct((4,) + a.shape[1:], a.dtype) for a in chip_sums]
    out_shape += _chips_stage_shapes(chip_sums)
    args += [pltpu.with_memory_space_constraint(a, pltpu.HBM) for a in chip_sums]
    args += [pltpu.with_memory_space_constraint(d, pltpu.HBM) for d in dests if d is not None]
    return _call(body, name=name, grid=(steps,), out_shape=out_shape, in_specs=in_specs, out_specs=out_specs,
                 scratch_shapes=_chips_sems(nr) if nr else [], input_output_aliases=aliases,
                 compiler_params=pltpu.CompilerParams(dimension_semantics=("arbitrary",),
                                                      vmem_limit_bytes=VMEM_LIMIT, has_side_effects=bool(nr)))(*args)


def _adamw(w, g, m, v):
    m = ADAM_B1 * m + (1.0 - ADAM_B1) * g
    v = ADAM_B2 * v + (1.0 - ADAM_B2) * (g * g)
    m_hat = m / (1.0 - ADAM_B1 ** ADAM_STEP)
    v_hat = v / (1.0 - ADAM_B2 ** ADAM_STEP)
    delta = -ADAM_LR * (m_hat / (jnp.sqrt(v_hat) + ADAM_EPS) + ADAM_WD * w)
    return delta, m, v


def _adamw_reduced(parts, w, m, v, tr, name):
    r, n = w.shape
    nparts = parts.shape[0]

    def body(p_ref, w_ref, m_ref, v_ref, g_ref, d_ref, mo_ref, vo_ref):
        g = p_ref[0].astype(F32)
        for i in range(1, nparts):
            g = g + p_ref[i].astype(F32)
        g_ref[...] = g
        d_ref[...], mo_ref[...], vo_ref[...] = _adamw(w_ref[...], g, m_ref[...], v_ref[...])

    tile = pl.BlockSpec((tr, n), lambda i: (i, 0))
    sds = jax.ShapeDtypeStruct((r, n), F32)
    return _call(
        body, name=name, grid=(r // tr,), out_shape=[sds] * 4,
        in_specs=[pl.BlockSpec((nparts, tr, n), lambda i: (0, i, 0)), tile, tile, tile], out_specs=[tile] * 4,
        compiler_params=_params("arbitrary"),
    )(parts, w, m, v)


R_GATE, R_FINAL_G, R_LN_G, R_LN_B, R_LOSS = 0, 1, 2, 3, 4
R_SH_X, R_SC_X, R_NG_X = 5, 6, 7
R_SH_C, R_SC_C, R_NG_C = 8, 9, 10
R_BA, R_BX, R_LAM, R_CW, R_CB, R_SGU_B = 11, 13, 15, 17, 21, 22
PACK_ROWS = 32
MAT_ROWS = 2 * (2 * HEADS * HD) + HEADS * CHUNK


def _reduce_small(vp_all, mat_parts, ada_w, me):
    nloc = ada_w.shape[1]

    def body(me_ref, vp_ref, mp_ref, w_ref, red_ref, mat_ref, dmod_ref, gab_ref, cpart_ref, dmc_s):
        red = vp_ref[0]
        for i in range(1, N_DEV):
            red = red + vp_ref[i]
        mat = mp_ref[0].astype(F32)
        for i in range(1, mp_ref.shape[0]):
            mat = mat + mp_ref[i].astype(F32)
        red_ref[...] = red
        mat_ref[...] = mat
        for e in range(N_DEV):
            dmod_ref[e:e + 1, 0:D] = vp_ref[e, R_SH_X:R_SH_X + 1, :]
            dmod_ref[e:e + 1, D:2 * D] = vp_ref[e, R_SC_X:R_SC_X + 1, :]
            dmod_ref[e:e + 1, 2 * D:3 * D] = vp_ref[e, R_GATE:R_GATE + 1, :]
        dmod_ref[8:9, 0:D] = red[R_SH_C:R_SH_C + 1, :]
        dmod_ref[8:9, D:2 * D] = red[R_SC_C:R_SC_C + 1, :]
        dmod_ref[8:9, 2 * D:3 * D] = jnp.zeros((1, D), F32)
        dmod_ref[9:16, :] = jnp.zeros((7, 3 * D), F32)
        gab_ref[:, 0:D] = red[R_SH_X:R_SH_X + 1, :] + red[R_SH_C:R_SH_C + 1, :]
        gab_ref[:, D:2 * D] = red[R_SC_X:R_SC_X + 1, :] + red[R_SC_C:R_SC_C + 1, :]
        gab_ref[:, 2 * D:3 * D] = red[R_GATE:R_GATE + 1, :]
        dmc_s[...] = jnp.broadcast_to(dmod_ref[8:9, :], (8, 3 * D))
        off = pl.multiple_of(me_ref[0] * nloc, 128)
        cpart_ref[...] = _dot_nt(dmc_s[:, pl.ds(off, nloc)], w_ref[...])

    return _call(
        body, name="reduce_small",
        out_shape=[jax.ShapeDtypeStruct((PACK_ROWS, D), F32), jax.ShapeDtypeStruct(mat_parts.shape[1:], F32),
                   jax.ShapeDtypeStruct((16, 3 * D), F32), jax.ShapeDtypeStruct((1, 3 * D), F32),
                   jax.ShapeDtypeStruct((8, D), F32)],
        in_specs=[pl.BlockSpec(memory_space=pltpu.SMEM), VMEM, VMEM, VMEM], out_specs=[VMEM] * 5,
        scratch_shapes=[pltpu.VMEM((8, 3 * D), F32)], compiler_params=_params(),
    )(me, vp_all, mat_parts, ada_w)


def _adamw_ada(c_all, c_ctx, dmod, w, m, v, me):
    nloc = w.shape[1]

    def body(me_ref, c_ref, cc_ref, dm_ref, w_ref, m_ref, v_ref, g_ref, d_ref, mo_ref, vo_ref):
        off = pl.multiple_of(me_ref[0] * nloc, 128)
        dm = dm_ref[:, pl.ds(off, nloc)]
        sx, _ = _silu_and_grad(c_ref[...])
        sc, _ = _silu_and_grad(cc_ref[...])
        g = _dot_tn(sx, dm[0:8, :]) + _dot_tn(jnp.broadcast_to(sc, (8, D)), dm[8:16, :])
        g_ref[...] = g
        d_ref[...], mo_ref[...], vo_ref[...] = _adamw(w_ref[...], g, m_ref[...], v_ref[...])

    sds = jax.ShapeDtypeStruct(w.shape, F32)
    return _call(
        body, name="adamw_ada_w", out_shape=[sds] * 4,
        in_specs=[pl.BlockSpec(memory_space=pltpu.SMEM)] + [VMEM] * 6, out_specs=[VMEM] * 4,
        compiler_params=_params(),
    )(me, c_all, c_ctx, dmod, w, m, v)


_SMALL = ("c_ctx", "ada_b", "norm_g", "conv_w", "conv_b", "lru_wa", "lru_ba", "lru_wx", "lru_bx", "lru_lambda",
          "sgu_ln_g", "sgu_ln_b", "sgu_w", "sgu_b", "final_g")


def _adamw_small(red, mat, cparts, gab, ws, ms, vs, me):
    n = len(_SMALL)
    nw = 2 * HEADS * HD

    def body(me_ref, red_ref, mat_ref, cp_ref, gab_ref, *refs):
        w_refs, m_refs, v_refs = refs[:n], refs[n:2 * n], refs[2 * n:3 * n]
        outs = refs[3 * n:]
        off = pl.multiple_of(me_ref[0] * HD, 128)

        def row(r, k=1):
            return red_ref[r:r + k, :]

        cc = w_refs[0][...]
        dcc = cp_ref[0, 0:1, :]
        for i in range(1, N_DEV):
            dcc = dcc + cp_ref[i, 0:1, :]
        grads = dict(
            c_ctx=dcc * _silu_and_grad(cc)[1], ada_b=gab_ref[...], norm_g=row(R_NG_X) + row(R_NG_C),
            conv_w=red_ref[R_CW:R_CW + CONV_W, pl.ds(off, HD)], conv_b=row(R_CB),
            lru_wa=mat_ref[0:nw, :], lru_ba=row(R_BA, 2), lru_wx=mat_ref[nw:2 * nw, :], lru_bx=row(R_BX, 2),
            lru_lambda=red_ref[R_LAM:R_LAM + 2, pl.ds(off, HD)], sgu_ln_g=row(R_LN_G), sgu_ln_b=row(R_LN_B),
            sgu_w=mat_ref[2 * nw:MAT_ROWS, :], sgu_b=row(R_SGU_B), final_g=row(R_FINAL_G))
        for j, name in enumerate(_SMALL):
            g = grads[name]
            outs[j][...] = g
            outs[n + j][...], outs[2 * n + j][...], outs[3 * n + j][...] = _adamw(w_refs[j][...], g, m_refs[j][...],
                                                                                 v_refs[j][...])

    sds = [jax.ShapeDtypeStruct(ws[k].shape, F32) for k in _SMALL]
    outs = _call(
        body, name="adamw_small", out_shape=sds * 4,
        in_specs=[pl.BlockSpec(memory_space=pltpu.SMEM)] + [VMEM] * (4 + 3 * n), out_specs=[VMEM] * (4 * n),
        compiler_params=_params(),
    )(me, red, mat, cparts, gab, *[ws[k] for k in _SMALL], *[ms[k] for k in _SMALL], *[vs[k] for k in _SMALL])
    return [dict(zip(_SMALL, outs[i * n:(i + 1) * n])) for i in range(4)]


def kernel(x, c, ctx, c_ctx, ada_w, ada_b, norm_g, w_in, conv_w, conv_b, lru_wa, lru_ba, lru_wx, lru_bx, lru_lambda, sgu_ln_g, sgu_ln_b, sgu_w, sgu_b, w_out, final_g, loss_target, m_c_ctx, m_ada_w, m_ada_b, m_norm_g, m_w_in, m_conv_w, m_conv_b, m_lru_wa, m_lru_ba, m_lru_wx, m_lru_bx, m_lru_lambda, m_sgu_ln_g, m_sgu_ln_b, m_sgu_w, m_sgu_b, m_w_out, m_final_g, v_c_ctx, v_ada_w, v_ada_b, v_norm_g, v_w_in, v_conv_w, v_conv_b, v_lru_wa, v_lru_ba, v_lru_wx, v_lru_bx, v_lru_lambda, v_sgu_ln_g, v_sgu_ln_b, v_sgu_w, v_sgu_b, v_w_out, v_final_g):
    args = dict(locals())
    me_s = 4 * lax.axis_index("x") + 2 * lax.axis_index("y") + lax.axis_index("c")
    me = me_s.astype(jnp.int32).reshape(1)
    xr, ctxr, tgt = x[0], ctx[0], loss_target[0]
    cc = c_ctx.reshape(1, D)
    nw = 2 * HEADS * HD
    view = dict(c_ctx=(1, D), ada_b=(1, 3 * D), norm_g=(1, D), conv_w=(CONV_W, HD), conv_b=(1, D), lru_wa=(nw, HD),
                lru_ba=(2, D), lru_wx=(nw, HD), lru_bx=(2, D), lru_lambda=(2, HD), sgu_ln_g=(1, D), sgu_ln_b=(1, D),
                sgu_w=(HEADS * CHUNK, CHUNK), sgu_b=(1, D), final_g=(1, D))

    w_full, w_out_b, modx, modc, c_all = _front(c, cc, ada_w[0], ada_b, w_in[0], w_out[0], me)

    zx, hn, wout_all, cw_full, lam_full = _project(
        xr, modx, norm_g, w_full, D_IN, 256, "project_x", gather=[w_out_b, conv_w[0], lru_lambda[0]],
        gather_modes=["ag", "agc", "agc"])
    wout_full = wout_all.reshape(D_MIX, D)
    zc, hnc = _project(ctxr, modc, norm_g, w_full, D, LC, "project_ctx")
    ba, bx = lru_ba.reshape(2, D), lru_bx.reshape(2, D)
    yl = _lru_forward(zx, zc, cw_full, conv_b, lru_wa[0], lru_wx[0], ba, bx, lam_full)
    ws_b = sgu_w[0].astype(BF16)
    dz, dyl, dxn, ycat, dob, dws, dbst, mvec = _mixer_loss(
        xr, tgt, zx, yl, modx, final_g.reshape(1, D), sgu_ln_g, sgu_ln_b, ws_b, jnp.swapaxes(ws_b, 1, 2),
        sgu_b[0].T, wout_full, 256)

    gw_out = _grad_w(ycat, dob, None, None, 512, "grad_w_out")
    gw_rest = _grad_w(hn, dz, None, None, 512, "grad_w_in_rest", bw=2 * W_IN_SHARD, first=1, nblocks=3)
    rest_sums, wout_sums = _reduce2_local([gw_rest, gw_out.reshape(N_DEV, D_MIX // N_DEV, D)], ["a2ac", "a2a"], me,
                                          "reduce_early", counts=[3, 4])
    dz, dxac, dwa, dwx, dba, dbx, dlam, dcw, dcb, win_parts, wout_parts, _, _ = _lru_backward(
        zx, zc, dyl, dz, cw_full, conv_b, lru_wa[0], lru_wx[0], ba, bx, lam_full, [rest_sums, wout_sums],
        first_chips=[1, 0])
    gw_first = _grad_w(hn, dz, hnc, dxac, 512, "grad_w_in_first", bw=2 * W_IN_SHARD, first=0, nblocks=1)
    matpack = jnp.concatenate([dwa.reshape(nw, HD), dwx.reshape(nw, HD), dws.reshape(HEADS * CHUNK, CHUNK)],
                              axis=0).astype(BF16)
    first_sums, mat_sums = _reduce2_local([gw_first, matpack.reshape(N_DEV, MAT_ROWS // N_DEV, HD)], ["a2ac", "a2a"],
                                          me, "reduce_late", counts=[1, 4])
    gx, xvec, win_parts, mat_parts, _, _ = _grad_rows(
        xr, dz, w_full, modx, norm_g, dxn, D_IN, 256, "grad_rows_x", chip_sums=[first_sums, mat_sums],
        first_chips=[0, 0], dests=[win_parts, None])
    (cvec,) = _grad_rows(ctxr, dxac, w_full, modc, norm_g, None, D, LC, "grad_rows_ctx")
    pack = jnp.concatenate([mvec[0:5], xvec[0:3], cvec[0:3], dba, dbx, dlam, dcw, dcb, dbst.T.reshape(1, D),
                            jnp.zeros((PACK_ROWS - R_SGU_B - 1, D), F32)], axis=0)
    (vp_all,) = _gather2([pack], ["ag"], "gather_pack")
    red, matpiece, dmod, gab, cpart = _reduce_small(vp_all, mat_parts, ada_w[0], me)
    mat_all, cparts = _gather2([matpiece, cpart], ["ag", "ag"], "gather_small")

    g_w_in, d_w_in, nm_w_in, nv_w_in = _adamw_reduced(win_parts, w_in[0], m_w_in[0], v_w_in[0], 256, "adamw_w_in")
    g_w_out, d_w_out, nm_w_out, nv_w_out = _adamw_reduced(wout_parts, w_out[0], m_w_out[0], v_w_out[0], 128,
                                                          "adamw_w_out")
    g_ada, d_ada, nm_ada, nv_ada = _adamw_ada(c_all, cc, dmod, ada_w[0], m_ada_w[0], v_ada_w[0], me)
    ws = {k: args[k].reshape(view[k]) for k in _SMALL}
    ms = {k: args["m_" + k].reshape(view[k]) for k in _SMALL}
    vs = {k: args["v_" + k].reshape(view[k]) for k in _SMALL}
    small = _adamw_small(red, mat_all.reshape(MAT_ROWS, HD), cparts, gab, ws, ms, vs, me)
    big = dict(w_in=(g_w_in, d_w_in, nm_w_in, nv_w_in), w_out=(g_w_out, d_w_out, nm_w_out, nv_w_out),
               ada_w=(g_ada, d_ada, nm_ada, nv_ada))

    loss = red[R_LOSS, 0]
    names = ("c_ctx", "ada_w", "ada_b", "norm_g", "w_in", "conv_w", "conv_b", "lru_wa", "lru_ba", "lru_wx", "lru_bx",
             "lru_lambda", "sgu_ln_g", "sgu_ln_b", "sgu_w", "sgu_b", "w_out", "final_g")
    outs = [loss, gx.reshape(x.shape)]
    for kind in range(4):
        for k in names:
            val = big[k][kind] if k in big else small[kind][k]
            outs.append(val.reshape(args[k].shape))
    return tuple(outs)
```

```python
import functools

import jax
import jax.numpy as jnp
from jax import lax
from jax.experimental import pallas as pl
from jax.experimental.pallas import tpu as pltpu

F32 = jnp.float32
BF16 = jnp.bfloat16

N_DEV = 8
D = 1024
L = 2048
LC = 256
HEADS = 8
HD = 128
CHUNK = 128
D_IN = 5 * D
W_IN_SHARD = D_IN // N_DEV
ROWS = 256
D_MIX = 2 * D
CONV_W = 4
LRU_C = 8.0
NORM_EPS = 1e-6
LN_EPS = 1e-5
ADAM_LR, ADAM_B1, ADAM_B2, ADAM_EPS, ADAM_WD, ADAM_STEP = 0.001, 0.9, 0.999, 1e-08, 0.01, 10

VMEM_LIMIT = 56 * 1024 * 1024

HBM = pl.BlockSpec(memory_space=pltpu.HBM)
VMEM = pl.BlockSpec(memory_space=pltpu.VMEM)
MESH = pl.DeviceIdType.MESH


def _call(body, **kw):
    return pl.pallas_call(body, **kw)


def _params(*sem):
    return pltpu.CompilerParams(dimension_semantics=sem, vmem_limit_bytes=VMEM_LIMIT)


def _sigmoid(x):
    return 0.5 * jnp.tanh(0.5 * x) + 0.5


def _silu_and_grad(x):
    s = _sigmoid(x)
    return x * s, s * (1.0 + x * (1.0 - s))


_G0 = 0.7978845608028654
_G1 = 0.044715


def _gelu_and_grad(x):
    x2 = x * x
    t = jnp.tanh(_G0 * (x + _G1 * x * x2))
    cdf = 0.5 * (1.0 + t)
    return x * cdf, cdf + 0.5 * x * (1.0 - t * t) * (_G0 * (1.0 + 3.0 * _G1 * x2))


def _gelu(x):
    return 0.5 * x * (1.0 + jnp.tanh(_G0 * (x + _G1 * x * x * x)))


def _softplus(z):
    t = jnp.exp(-jnp.abs(z))
    u = 1.0 + t
    log1p = jnp.where(u == 1.0, t, jnp.log(u) * t / jnp.where(u == 1.0, 1.0, u - 1.0))
    return jnp.maximum(z, 0.0) + log1p


def _dot(a, b):
    return jnp.dot(a, b, preferred_element_type=F32)


def _dot_nt(a, b):
    return lax.dot_general(a, b, (((1,), (1,)), ((), ())), preferred_element_type=F32)


def _dot_tn(a, b):
    return lax.dot_general(a, b, (((0,), (0,)), ((), ())), preferred_element_type=F32)


def _rows(shape):
    return lax.broadcasted_iota(jnp.int32, shape, 0)


def _shift_down(x, first):
    return jnp.where(_rows(x.shape) == 0, first, pltpu.roll(x, 1, 0))


def _shift_up(x, last):
    n = x.shape[0]
    return jnp.where(_rows(x.shape) == n - 1, last, pltpu.roll(x, n - 1, 0))


def _gather2(arrays, modes, name):
    n = len(arrays)

    def body(*refs):
        start, forward, finish = _gather2_ops(refs[:n], refs[n:2 * n], modes, *refs[2 * n:])
        start()
        forward()
        finish()

    return _call(
        body, name=name, out_shape=_gather2_shapes(arrays, modes), in_specs=[HBM] * n, out_specs=[HBM] * n,
        scratch_shapes=_gather2_sems(n), compiler_params=pltpu.CompilerParams(has_side_effects=True),
    )(*[pltpu.with_memory_space_constraint(a, pltpu.HBM) for a in arrays])


def _gather2_shapes(arrays, modes):
    return [jax.ShapeDtypeStruct((N_DEV,) + a.shape if m == "ag" else (a.shape[0], N_DEV * a.shape[1]), a.dtype)
            for a, m in zip(arrays, modes)]


def _gather2_sems(n):
    return [pltpu.SemaphoreType.DMA((n, N_DEV - 1)), pltpu.SemaphoreType.DMA((n, N_DEV - 1)),
            pltpu.SemaphoreType.DMA((n,))]


def _gather2_ops(ins, outs, modes, send_sems, recv_sems, local_sems):
    n = len(ins)
    x, y, c = lax.axis_index("x"), lax.axis_index("y"), lax.axis_index("c")
    me, sibling = (x, y, c), (x, y, 1 - c)
    chips = [(x ^ (k >> 1), y ^ (k & 1)) for k in (1, 2, 3)]

    def slot(j, px, py, pc):
        dev = 4 * px + 2 * py + pc
        if modes[j] == "agc":
            w = ins[j].shape[1]
            return outs[j].at[:, pl.ds(pl.multiple_of(dev * w, 128), w)]
        return outs[j].at[dev]

    def copy(j, k, block, to, src=None):
        return pltpu.make_async_remote_copy(
            src_ref=slot(j, *block) if src is None else src, dst_ref=slot(j, *block),
            send_sem=send_sems.at[j, k], recv_sem=recv_sems.at[j, k], device_id=to, device_id_type=MESH)

    def own(j):
        return pltpu.make_async_copy(ins[j], slot(j, *me), local_sems.at[j])

    def first(j):
        return [copy(j, 0, me, sibling, src=ins[j])] + [copy(j, 1 + i, me, (*chip, c), src=ins[j])
                                                        for i, chip in enumerate(chips)]

    def passed(j, i):
        return copy(j, 4 + i, (*chips[i], c), sibling)

    def start():
        for j in range(n):
            own(j).start()
            for cp in first(j):
                cp.start()

    def forward():
        for i, chip in enumerate(chips):
            for j in range(n):
                copy(j, 1 + i, (*chip, c), me).wait_recv()
                passed(j, i).start()

    def finish():
        for j in range(n):
            copy(j, 0, sibling, me).wait_recv()
            for i, chip in enumerate(chips):
                copy(j, 4 + i, (*chip, 1 - c), me).wait_recv()
            for cp in first(j) + [passed(j, i) for i in range(3)]:
                cp.wait_send()
            own(j).wait()

    return start, forward, finish


def _reduce2_local(arrays, modes, me, name, counts=None):
    n = len(arrays)
    counts = counts or [4] * n
    shapes = [(a.shape[1], a.shape[2]) if m == "a2a" else (a.shape[0], a.shape[1] // (2 * cnt))
              for a, m, cnt in zip(arrays, modes, counts)]
    staged = [jax.ShapeDtypeStruct((cnt,) + s, a.dtype) for s, a, cnt in zip(shapes, arrays, counts)]

    def piece(ref, mode, dev, w):
        return ref.at[dev] if mode == "a2a" else ref.at[:, pl.ds(pl.multiple_of(dev * w, 128), w)]

    def to_sibling(*refs):
        ins, outs = refs[:n], refs[n:2 * n]
        send_sems, recv_sems = refs[2 * n:]
        x, y, c = lax.axis_index("x"), lax.axis_index("y"), lax.axis_index("c")
        copies = []
        for j in range(n):
            for q in range(counts[j]):
                cp = pltpu.make_async_remote_copy(
                    src_ref=piece(ins[j], modes[j], 2 * q + (1 - c), shapes[j][1]), dst_ref=outs[j].at[q],
                    send_sem=send_sems.at[j, q], recv_sem=recv_sems.at[j, q], device_id=(x, y, 1 - c),
                    device_id_type=MESH)
                cp.start()
                copies.append(cp)
        for cp in copies:
            cp.wait()

    stage = _call(
        to_sibling, name=name + "_d2d", out_shape=staged, in_specs=[HBM] * n, out_specs=[HBM] * n,
        scratch_shapes=[pltpu.SemaphoreType.DMA((n, 4)), pltpu.SemaphoreType.DMA((n, 4))],
        compiler_params=pltpu.CompilerParams(has_side_effects=True),
    )(*[pltpu.with_memory_space_constraint(a, pltpu.HBM) for a in arrays])

    def add(me_ref, *refs):
        del me_ref
        own, got, outs = refs[:n], refs[n:2 * n], refs[2 * n:]
        for j in range(n):
            mine = own[j][0] if modes[j] == "a2a" else own[j][...]
            outs[j][0] = (mine.astype(F32) + got[j][0].astype(F32)).astype(outs[j].dtype)

    in_specs, slot_specs = [], []
    for (r, w), m, cnt in zip(shapes, modes, counts):
        if m == "a2a":
            in_specs.append(pl.BlockSpec(
                (1, r, w), lambda q, me_ref, cnt=cnt: (2 * jnp.minimum(q, cnt - 1) + me_ref[0] % 2, 0, 0)))
        else:
            in_specs.append(pl.BlockSpec(
                (r, w), lambda q, me_ref, cnt=cnt: (0, 2 * jnp.minimum(q, cnt - 1) + me_ref[0] % 2)))
        slot_specs.append(pl.BlockSpec((1, r, w), lambda q, me_ref, cnt=cnt: (jnp.minimum(q, cnt - 1), 0, 0)))
    return _call(
        add, name=name + "_add", out_shape=staged,
        grid_spec=pltpu.PrefetchScalarGridSpec(num_scalar_prefetch=1, grid=(max(counts),),
                                               in_specs=in_specs + slot_specs, out_specs=slot_specs),
        compiler_params=_params("arbitrary"),
    )(me, *arrays, *stage)


def _chips_sems(n):
    return [pltpu.SemaphoreType.DMA((n, 6)), pltpu.SemaphoreType.DMA((n, 6)), pltpu.SemaphoreType.DMA((n,))]


def _chips_stage_shapes(chip_sums):
    return [jax.ShapeDtypeStruct((2, a.shape[1] // 2, a.shape[2]), a.dtype) for a in chip_sums]


def _chips_ops(ins, outs, stages, send_sems, recv_sems, local_sems, first_chips=None):
    x, y, c = lax.axis_index("x"), lax.axis_index("y"), lax.axis_index("c")
    qm = 2 * x + y
    first_chips = first_chips or [0] * len(ins)

    def owns(j, chip):
        lo, cnt = first_chips[j], ins[j].shape[0]
        if lo == 0 and cnt == 4:
            return None
        return jnp.logical_and(chip >= lo, chip < lo + cnt)

    def guarded(cond, fn):
        if cond is None:
            fn()
        else:
            pl.when(cond)(fn)

    def slot(j, chip):
        return jnp.clip(chip - first_chips[j], 0, ins[j].shape[0] - 1)

    def half(j, i):
        h = ins[j].shape[1] // 2
        return pl.ds(i * h, h)

    def copy(j, sem, src, dst, k):
        return pltpu.make_async_remote_copy(
            src_ref=src, dst_ref=dst, send_sem=send_sems.at[j, sem], recv_sem=recv_sems.at[j, sem],
            device_id=(x ^ (k >> 1), y ^ (k & 1), c), device_id_type=MESH)

    def direct(j, k):
        return copy(j, k - 1, ins[j].at[slot(j, qm ^ k)], outs[j].at[qm], k)

    def first_hop(j, k):
        return copy(j, 1 + k, ins[j].at[slot(j, qm ^ 3), half(j, k - 1)], stages[j].at[k - 1], k)

    def second_hop(j, k):
        return copy(j, 3 + k, stages[j].at[2 - k], outs[j].at[qm ^ (3 - k), half(j, 2 - k)], k)

    def local(j):
        return pltpu.make_async_copy(ins[j].at[slot(j, qm)], outs[j].at[qm], local_sems.at[j])

    def start():
        for j in range(len(ins)):
            for k in (1, 2):
                guarded(owns(j, qm ^ 3), lambda j=j, k=k: first_hop(j, k).start())
        for j in range(len(ins)):
            for k in (1, 2):
                guarded(owns(j, qm ^ k), lambda j=j, k=k: direct(j, k).start())
            guarded(owns(j, qm), lambda j=j: local(j).start())

    def forward():
        for j in range(len(ins)):
            for k in (1, 2):
                def pass_on(j=j, k=k):
                    first_hop(j, 3 - k).wait_recv()
                    second_hop(j, k).start()
                guarded(owns(j, qm ^ k), pass_on)

    def finish():
        for j in range(len(ins)):
            for k in (1, 2):
                guarded(owns(j, qm ^ k), lambda j=j, k=k: direct(j, k).wait_send())
                guarded(owns(j, qm ^ k), lambda j=j, k=k: second_hop(j, k).wait_send())
                guarded(owns(j, qm ^ 3), lambda j=j, k=k: first_hop(j, k).wait_send())
                guarded(owns(j, qm), lambda j=j, k=k: direct(j, k).wait_recv())
                guarded(owns(j, qm), lambda j=j, k=k: second_hop(j, k).wait_recv())
            guarded(owns(j, qm), lambda j=j: local(j).wait())

    return start, forward, finish


def _front(c, c_ctx, ada_w, ada_b, w_in, w_out, me):
    nloc = ada_w.shape[1]

    def body(me_ref, c_ref, cc_ref, aw_ref, ab_ref, win_ref, wout_ref,
             wfull_ref, woutb_ref, modx_ref, modc_ref, call_ref,
             wb_s, part_s, parts_s, w_send, w_recv, w_local, s_send, s_recv):
        x, y, cidx = lax.axis_index("x"), lax.axis_index("y"), lax.axis_index("c")
        me = me_ref[0]
        wb_s[...] = win_ref[...].astype(BF16)
        woutb_ref[...] = wout_ref[...].astype(BF16)
        start, forward, finish = _gather2_ops([wb_s], [wfull_ref], ["agc"], w_send, w_recv, w_local)
        start()

        def small_gather(src, my_slot, stage):
            copies = []
            for k in range(1, N_DEV):
                peer = (x ^ (k >> 2), y ^ ((k >> 1) & 1), cidx ^ (k & 1))
                cp = pltpu.make_async_remote_copy(src_ref=src, dst_ref=my_slot, send_sem=s_send.at[stage, k - 1],
                                                  recv_sem=s_recv.at[stage, k - 1], device_id=peer,
                                                  device_id_type=MESH)
                cp.start()
                copies.append(cp)
            pltpu.sync_copy(src, my_slot)
            for cp in copies:
                cp.wait()

        small_gather(c_ref, call_ref.at[pl.ds(me, 1), :], 0)
        off = pl.multiple_of(me * nloc, 128)
        b = ab_ref[:, pl.ds(off, nloc)]
        w = aw_ref[...]
        sx, _ = _silu_and_grad(call_ref[...])
        sc, _ = _silu_and_grad(jnp.broadcast_to(cc_ref[...], (8, D)))
        part_s[0:8, :] = _dot(sx, w) + b
        part_s[8:16, :] = _dot(sc, w) + b
        small_gather(part_s, parts_s.at[me], 1)
        mine = _rows((16, nloc)) == me
        for j in range(N_DEV):
            pj = parts_s[j]
            modx_ref[:, j * nloc:(j + 1) * nloc] = jnp.sum(jnp.where(mine, pj, 0.0), axis=0, keepdims=True)
            modc_ref[:, j * nloc:(j + 1) * nloc] = pj[8:9, :]
        forward()
        finish()

    return _call(
        body, name="front",
        out_shape=[jax.ShapeDtypeStruct((D, D_IN), BF16), jax.ShapeDtypeStruct(w_out.shape, BF16),
                   jax.ShapeDtypeStruct((1, 3 * D), F32), jax.ShapeDtypeStruct((1, 3 * D), F32),
                   jax.ShapeDtypeStruct((N_DEV, D), F32)],
        in_specs=[pl.BlockSpec(memory_space=pltpu.SMEM)] + [VMEM] * 6, out_specs=[HBM, VMEM, VMEM, VMEM, VMEM],
        scratch_shapes=[pltpu.VMEM(w_in.shape, BF16), pltpu.VMEM((16, nloc), F32),
                        pltpu.VMEM((N_DEV, 16, nloc), F32)] + _gather2_sems(1) +
                       [pltpu.SemaphoreType.DMA((2, N_DEV - 1)), pltpu.SemaphoreType.DMA((2, N_DEV - 1))],
        compiler_params=pltpu.CompilerParams(vmem_limit_bytes=VMEM_LIMIT, has_side_effects=True),
    )(me, c, c_ctx, ada_w, ada_b, w_in, w_out)


ARRIVAL = (0, 1, 2, 4, 6, 3, 5, 7)


def _front_project(xr, c, c_ctx, ada_w, ada_b, ng, w_in, w_out, cw, lam, me):
    nloc = ada_w.shape[1]
    ws = W_IN_SHARD
    order = me[0] ^ jnp.asarray(ARRIVAL, jnp.int32)

    def body(ord_ref, x_ref, c_ref, cc_ref, aw_ref, ab_ref, ng_ref, win_ref, wout_ref, cw_ref, lam_ref,
             z_ref, hn_ref, wfull_ref, woutb_ref, modx_ref, modc_ref, call_ref, cwf_ref, lamf_ref,
             wv, call_s, part_s, parts_s, w_send, w_recv, hbm_sems, s_send, s_recv, g_send, g_recv, g_local):
        t = pl.program_id(0)
        x, y, cidx = lax.axis_index("x"), lax.axis_index("y"), lax.axis_index("c")
        me_i = ord_ref[0]
        sibling = (x, y, 1 - cidx)
        chips = [(x ^ (k >> 1), y ^ (k & 1)) for k in (1, 2, 3)]
        g_start, g_pass, g_finish = _gather2_ops([cw_ref, lam_ref], [cwf_ref, lamf_ref], ["agc", "agc"],
                                                 g_send, g_recv, g_local)

        def shard_copy(k, px, py, pc, to):
            slot = wv.at[4 * px + 2 * py + pc]
            return pltpu.make_async_remote_copy(src_ref=slot, dst_ref=slot, send_sem=w_send.at[k],
                                                recv_sem=w_recv.at[k], device_id=to, device_id_type=MESH)

        def small_gather(src, my_slot, stage):
            copies = []
            for k in range(1, N_DEV):
                peer = (x ^ (k >> 2), y ^ ((k >> 1) & 1), cidx ^ (k & 1))
                cp = pltpu.make_async_remote_copy(src_ref=src, dst_ref=my_slot, send_sem=s_send.at[stage, k - 1],
                                                  recv_sem=s_recv.at[stage, k - 1], device_id=peer,
                                                  device_id_type=MESH)
                cp.start()
                copies.append(cp)
            pltpu.sync_copy(src, my_slot)
            for cp in copies:
                cp.wait()

        @pl.when(t == 0)
        def _():
            g_start()
            wv[me_i] = win_ref[...].astype(BF16)
            woutb_ref[...] = wout_ref[...].astype(BF16)
            shard_copy(0, x, y, cidx, sibling).start()
            for i, chip in enumerate(chips):
                shard_copy(1 + i, x, y, cidx, (*chip, cidx)).start()
            small_gather(c_ref, call_s.at[pl.ds(me_i, 1), :], 0)
            call_ref[...] = call_s[...]
            off = pl.multiple_of(me_i * nloc, 128)
            b = ab_ref[:, pl.ds(off, nloc)]
            w = aw_ref[...]
            sx, _ = _silu_and_grad(call_s[...])
            sc, _ = _silu_and_grad(jnp.broadcast_to(cc_ref[...], (8, D)))
            part_s[0:8, :] = _dot(sx, w) + b
            part_s[8:16, :] = _dot(sc, w) + b
            small_gather(part_s, parts_s.at[me_i], 1)
            mine = _rows((16, nloc)) == me_i
            for j in range(N_DEV):
                pj = parts_s[j]
                modx_ref[:, j * nloc:(j + 1) * nloc] = jnp.sum(jnp.where(mine, pj, 0.0), axis=0, keepdims=True)
                modc_ref[:, j * nloc:(j + 1) * nloc] = pj[8:9, :]
            shift, scale1, ngv = modx_ref[:, 0:D], 1.0 + modx_ref[:, D:2 * D], ng_ref[...]
            for r in range(L // ROWS):
                rsl = slice(r * ROWS, (r + 1) * ROWS)
                xv = x_ref[rsl, :]
                rs = lax.rsqrt(jnp.mean(xv * xv, axis=-1, keepdims=True) + NORM_EPS)
                hn_ref[rsl, :] = ((xv * rs * ngv) * scale1 + shift).astype(BF16)

        @pl.when(t == 1)
        def _():
            shard_copy(0, x, y, 1 - cidx, sibling).wait_recv()
            g_pass()

        for i, chip in enumerate(chips):
            @pl.when(t == 2 + i)
            def _(i=i, chip=chip):
                shard_copy(1 + i, *chip, cidx, sibling).wait_recv()
                shard_copy(4 + i, *chip, cidx, sibling).start()

            @pl.when(t == 5 + i)
            def _(i=i, chip=chip):
                shard_copy(4 + i, *chip, 1 - cidx, sibling).wait_recv()

        @pl.when(t == 2)
        def _():
            g_finish()

        dev = ord_ref[t]
        for r in range(L // (2 * ROWS)):
            rsl = slice(r * 2 * ROWS, (r + 1) * 2 * ROWS)
            z_ref[rsl, :] = _dot(hn_ref[rsl, :], wv[dev])
        col = pl.ds(pl.multiple_of(dev * ws, 128), ws)
        pltpu.make_async_copy(wv.at[dev], wfull_ref.at[:, col], hbm_sems.at[t]).start()

        @pl.when(t == N_DEV - 1)
        def _():
            for k in range(7):
                shard_copy(k, x, y, cidx, sibling).wait_send()
            for s in range(N_DEV):
                pltpu.make_async_copy(wv.at[0], wfull_ref.at[:, pl.ds(0, ws)], hbm_sems.at[s]).wait()

    const = lambda *shape: pl.BlockSpec(shape, lambda t, o: (0,) * len(shape))
    once = lambda *shape: pl.BlockSpec(shape, lambda t, o: (0,) * len(shape), pipeline_mode=pl.Buffered(1))
    return _call(
        body, name="front_project",
        out_shape=[jax.ShapeDtypeStruct((L, D_IN), F32), jax.ShapeDtypeStruct((L, D), BF16),
                   jax.ShapeDtypeStruct((D, D_IN), BF16), jax.ShapeDtypeStruct(w_out.shape, BF16),
                   jax.ShapeDtypeStruct((1, 3 * D), F32), jax.ShapeDtypeStruct((1, 3 * D), F32),
                   jax.ShapeDtypeStruct((N_DEV, D), F32), jax.ShapeDtypeStruct((CONV_W, D), F32),
                   jax.ShapeDtypeStruct((2, D), F32)],
        grid_spec=pltpu.PrefetchScalarGridSpec(
            num_scalar_prefetch=1, grid=(N_DEV,),
            in_specs=[once(L, D), const(1, D), const(1, D), once(D, nloc), const(1, 3 * D), const(1, D),
                      once(D, ws), once(*w_out.shape), HBM, HBM],
            out_specs=[pl.BlockSpec((L, ws), lambda t, o: (0, o[t])), const(L, D), HBM, const(*w_out.shape),
                       const(1, 3 * D), const(1, 3 * D), const(N_DEV, D), HBM, HBM],
            scratch_shapes=[pltpu.VMEM((N_DEV, D, ws), BF16), pltpu.VMEM((N_DEV, D), F32), pltpu.VMEM((16, nloc), F32),
                            pltpu.VMEM((N_DEV, 16, nloc), F32), pltpu.SemaphoreType.DMA((7,)),
                            pltpu.SemaphoreType.DMA((7,)), pltpu.SemaphoreType.DMA((N_DEV,)),
                            pltpu.SemaphoreType.DMA((2, N_DEV - 1)), pltpu.SemaphoreType.DMA((2, N_DEV - 1))]
            + _gather2_sems(2)),
        compiler_params=pltpu.CompilerParams(dimension_semantics=("arbitrary",), vmem_limit_bytes=VMEM_LIMIT,
                                             has_side_effects=True),
    )(order, xr, c, c_ctx, ada_w, ada_b, ng, w_in, w_out, pltpu.with_memory_space_constraint(cw, pltpu.HBM),
      pltpu.with_memory_space_constraint(lam, pltpu.HBM))


def _project(xr, mod, ng, w, ncols, tm, name, gather=None, gather_modes=()):
    rows = xr.shape[0]
    steps = rows // tm
    ng_ = len(gather or ())

    def body(x_ref, sh_ref, sc_ref, ng_ref, w_ref, *rest):
        z_ref, hn_ref = rest[ng_:ng_ + 2]
        if ng_:
            start, forward, finish = _gather2_ops(rest[:ng_], rest[ng_ + 2:2 * ng_ + 2], gather_modes,
                                                  *rest[2 * ng_ + 2:])
            pl.when(pl.program_id(0) == 0)(start)
            pl.when(pl.program_id(0) == steps // 2)(forward)
        x = x_ref[...]
        rs = lax.rsqrt(jnp.mean(x * x, axis=-1, keepdims=True) + NORM_EPS)
        hn = (x * rs * ng_ref[...]) * (1.0 + sc_ref[...]) + sh_ref[...]
        hb = hn.astype(BF16)
        hn_ref[...] = hb
        for n in range(ncols // D):
            z_ref[:, n * D:(n + 1) * D] = _dot(hb, w_ref[:, n * D:(n + 1) * D])
        if ng_:
            pl.when(pl.program_id(0) == steps - 1)(finish)

    vec = pl.BlockSpec((1, D), lambda i: (0, 0))
    gathered = _gather2_shapes(gather, gather_modes) if ng_ else []
    return _call(
        body, name=name, grid=(steps,),
        out_shape=[jax.ShapeDtypeStruct((rows, ncols), F32), jax.ShapeDtypeStruct((rows, D), BF16)] + gathered,
        in_specs=[pl.BlockSpec((tm, D), lambda i: (i, 0)), vec, pl.BlockSpec((1, D), lambda i: (0, 1)), vec,
                  pl.BlockSpec((D, ncols), lambda i: (0, 0), pipeline_mode=pl.Buffered(1))] + [HBM] * ng_,
        out_specs=[pl.BlockSpec((tm, ncols), lambda i: (i, 0)), pl.BlockSpec((tm, D), lambda i: (i, 0))] + [HBM] * ng_,
        scratch_shapes=_gather2_sems(ng_) if ng_ else [],
        compiler_params=pltpu.CompilerParams(dimension_semantics=("arbitrary",), vmem_limit_bytes=VMEM_LIMIT,
                                             has_side_effects=bool(ng_)),
    )(xr, mod, mod, ng, w, *[pltpu.with_memory_space_constraint(a, pltpu.HBM) for a in gather or ()])


def _scan_pair(af_ref, uf_ref, hf_ref, h0f, ab_ref, ub_ref, hb_ref, h0b, t_len):
    span = 8 * SCAN_BLOCKS
    nit = t_len // span
    rows = _rows((8, HD))

    def local_scan(a, b, forward):
        for s in (1, 2, 4):
            sh = s if forward else 8 - s
            m = rows >= s if forward else rows < 8 - s
            b = a * jnp.where(m, pltpu.roll(b, sh, 0), 0.0) + b
            a = a * jnp.where(m, pltpu.roll(a, sh, 0), 1.0)
        return a, b

    def span_scan(a_ref, u_ref, h_ref, off, carry, forward):
        order = range(SCAN_BLOCKS) if forward else range(SCAN_BLOCKS - 1, -1, -1)
        last = slice(7, 8) if forward else slice(0, 1)
        for q in order:
            rs = pl.ds(off + 8 * q, 8)
            a, b = local_scan(a_ref[rs, :], u_ref[rs, :], forward)
            h_ref[rs, :] = b + a * carry
            carry = a[last, :] * carry + b[last, :]
        return carry

    def body(k, carry):
        cf, cb = carry
        cf = span_scan(af_ref, uf_ref, hf_ref, pl.multiple_of(k * span, span), cf, True)
        cb = span_scan(ab_ref, ub_ref, hb_ref, pl.multiple_of((nit - 1 - k) * span, span), cb, False)
        return cf, cb

    return lax.fori_loop(0, nit, body, (h0f, h0b))


SCAN_BLOCKS = 4


def _conv(xa, cw, cb):
    z = jnp.zeros((1, HD), F32)
    xm1 = _shift_down(xa, z)
    xp1 = _shift_up(xa, z)
    xp2 = _shift_up(xp1, z)
    return xm1 * cw[0:1, :] + xa * cw[1:2, :] + xp1 * cw[2:3, :] + xp2 * cw[3:4, :] + cb


def _gates(xc, wa, wx, ba, bx, nsp):
    xb = xc.astype(BF16)
    r = _sigmoid(_dot(xb, wa) + ba)
    i = _sigmoid(_dot(xb, wx) + bx)
    log_a = r * nsp
    a = jnp.exp(log_a)
    g2 = jnp.tanh(log_a) * (-1.0 - a * a)
    rg = lax.rsqrt(jnp.maximum(g2, 1e-30))
    return r, i, a, g2 * rg, rg


def _lru_param_specs():
    h4 = pl.BlockSpec((2, 1, HD, HD), lambda h: (0, h, 0, 0))
    v2 = pl.BlockSpec((2, HD), lambda h: (0, h))
    return dict(
        xa=pl.BlockSpec((L, HD), lambda h: (0, h)), xac=pl.BlockSpec((LC, HD), lambda h: (0, h)),
        cw=pl.BlockSpec((CONV_W, HD), lambda h: (0, h)), cb=pl.BlockSpec((1, HD), lambda h: (0, h)), h4=h4, v2=v2)


def _lru_forward(zx, zc, cw, cb, wa, wx, ba, bx, lam, gather, gather_modes):
    ng_ = len(gather)

    def body(xa_ref, xac_ref, cw_ref, cb_ref, wa_ref, wx_ref, ba_ref, bx_ref, lam_ref, *rest):
        yl_ref = rest[ng_]
        af, uf, hf, ab, ub, hb = rest[2 * ng_ + 1:2 * ng_ + 7]
        start, pass_on, finish = _gather2_ops(rest[:ng_], rest[ng_ + 1:2 * ng_ + 1], gather_modes,
                                              *rest[2 * ng_ + 7:])
        pl.when(pl.program_id(0) == 0)(start)
        pl.when(pl.program_id(0) == HEADS // 2)(pass_on)
        pl.when(pl.program_id(0) == HEADS - 1)(finish)
        cwv, cbv = cw_ref[...], cb_ref[...]
        nsp = (-LRU_C) * _softplus(-lam_ref[...])

        def forward(xa, t_len, h0f, h0b):
            xc = _conv(xa, cwv, cbv)
            for d, (a_ref, u_ref) in enumerate(((af, uf), (ab, ub))):
                _, i, a, gamma, _ = _gates(xc, wa_ref[d, 0].astype(BF16), wx_ref[d, 0].astype(BF16),
                                           ba_ref[d:d + 1, :], bx_ref[d:d + 1, :], nsp[d:d + 1, :])
                a_ref[0:t_len, :] = a
                u_ref[0:t_len, :] = gamma * (i * xc)
            return _scan_pair(af, uf, hf, h0f, ab, ub, hb, h0b, t_len)

        z = jnp.zeros((1, HD), F32)
        h0f, h0b = forward(xac_ref[...], LC, z, z)
        forward(xa_ref[...], L, h0f, h0b)
        yl_ref[...] = hf[...] + hb[...]

    s = _lru_param_specs()
    return _call(
        body, name="lru_forward", grid=(HEADS,),
        out_shape=[jax.ShapeDtypeStruct((L, D), F32)] + _gather2_shapes(gather, gather_modes),
        in_specs=[s["xa"], s["xac"], s["cw"], s["cb"], s["h4"], s["h4"], s["v2"], s["v2"], s["v2"]] + [HBM] * ng_,
        out_specs=[pl.BlockSpec((L, HD), lambda h: (0, h))] + [HBM] * ng_,
        scratch_shapes=[pltpu.VMEM((L, HD), F32)] * 6 + _gather2_sems(ng_),
        compiler_params=pltpu.CompilerParams(dimension_semantics=("arbitrary",), vmem_limit_bytes=VMEM_LIMIT,
                                             has_side_effects=True),
    )(zx, zc, cw, cb, wa, wx, ba, bx, lam, *[pltpu.with_memory_space_constraint(a, pltpu.HBM) for a in gather])


def _lru_backward(zx, zc, dyl, dz, cw, cb, wa, wx, ba, bx, lam, chip_sums, first_chips=None):
    nr = len(chip_sums)

    def body(xa_ref, xac_ref, dyl_ref, dz_in, cw_ref, cb_ref, wa_ref, wx_ref, ba_ref, bx_ref, lam_ref, *rest):
        (dxa_ref, dxac_ref, dwa_ref, dwx_ref, dba_ref, dbx_ref, dlam_ref, dcw_ref,
         dcb_ref) = rest[nr:nr + 9]
        main_s, ctx_s = rest[3 * nr + 9:3 * nr + 11]
        if nr:
            start, forward, finish = _chips_ops(rest[:nr], rest[nr + 9:2 * nr + 9], rest[2 * nr + 9:3 * nr + 9],
                                                *rest[3 * nr + 11:], first_chips=first_chips)
            pl.when(pl.program_id(0) == 0)(start)
            pl.when(pl.program_id(0) == HEADS // 2)(forward)
            pl.when(pl.program_id(0) == HEADS - 1)(finish)
        del dz_in
        cwv, cbv = cw_ref[...], cb_ref[...]
        lamv = lam_ref[...]
        sp = _softplus(-lamv)
        nsp = (-LRU_C) * sp
        z = jnp.zeros((1, HD), F32)

        def wmat(ref, d):
            return ref[d, 0].astype(BF16)

        def workspace(s):
            return dict(a=(s.at[0], s.at[1]), u=(s.at[2], s.at[3]), h=(s.at[4], s.at[5]), rho=(s.at[6], s.at[7]),
                        saved=(tuple(s.at[8 + k] for k in range(4)), tuple(s.at[12 + k] for k in range(4))),
                        xc=s.at[16])

        def forward(ws, xa, t_len, h0f, h0b):
            xc = _conv(xa, cwv, cbv)
            ws["xc"][...] = xc
            for d in (0, 1):
                vals = _gates(xc, wmat(wa_ref, d), wmat(wx_ref, d), ba_ref[d:d + 1, :], bx_ref[d:d + 1, :],
                              nsp[d:d + 1, :])
                r, i, a, gamma, rg = vals
                ws["a"][d][...] = a
                ws["u"][d][...] = gamma * (i * xc)
                for ref, val in zip(ws["saved"][d], (r, i, gamma, rg)):
                    ref[...] = val
            return _scan_pair(ws["a"][0], ws["u"][0], ws["h"][0], h0f, ws["a"][1], ws["u"][1], ws["h"][1], h0b,
                              t_len)

        def backward(ws, xa, t_len, h0f, h0b, dhf, dhb, first):
            xc = ws["xc"][...]
            (af, ab), (uf, ub), (hf, hb), (rf, rb) = ws["a"], ws["u"], ws["h"], ws["rho"]
            uf[...] = ab[...] * dhb
            ub[...] = af[...] * dhf
            rho_b_last, rho_f_first = _scan_pair(ab, uf, rb, z, af, ub, rf, z, t_len)
            dxc = jnp.zeros((t_len, HD), F32)
            dsp = []
            for d in (0, 1):
                r, i, gamma, rg = (ref[...] for ref in ws["saved"][d])
                a = ws["a"][d][...]
                if d == 0:
                    lam_t = dhf + _shift_up(rf[...], z)
                    h_prev = _shift_down(hf[...], h0f)
                else:
                    lam_t = dhb + _shift_down(rb[...], z)
                    h_prev = _shift_up(hb[...], h0b)
                da = lam_t * h_prev
                lx = lam_t * xc
                d_i = lx * gamma
                d_gamma = lx * i
                dxc = dxc + lam_t * (gamma * i)
                d_log_a = a * (da - d_gamma * (a * rg))
                dsp.append(jnp.sum(d_log_a * r, axis=0, keepdims=True) * (-LRU_C))
                d_pre_r = d_log_a * nsp[d:d + 1, :] * (r * (1.0 - r))
                d_pre_i = d_i * (i * (1.0 - i))
                prb, pib, xb = d_pre_r.astype(BF16), d_pre_i.astype(BF16), xc.astype(BF16)
                dxc = dxc + _dot_nt(prb, wmat(wa_ref, d)) + _dot_nt(pib, wmat(wx_ref, d))
                g_wa, g_wx = _dot_tn(xb, prb), _dot_tn(xb, pib)
                g_ba = jnp.sum(d_pre_r, axis=0, keepdims=True)
                g_bx = jnp.sum(d_pre_i, axis=0, keepdims=True)
                if first:
                    dwa_ref[d, 0] = g_wa
                    dwx_ref[d, 0] = g_wx
                    dba_ref[d:d + 1, :] = g_ba
                    dbx_ref[d:d + 1, :] = g_bx
                else:
                    dwa_ref[d, 0] += g_wa
                    dwx_ref[d, 0] += g_wx
                    dba_ref[d:d + 1, :] += g_ba
                    dbx_ref[d:d + 1, :] += g_bx
            g_lam = jnp.concatenate(dsp, axis=0) * (-_sigmoid(-lamv))
            dm1 = _shift_down(dxc, z)
            dp1 = _shift_up(dxc, z)
            dm2 = _shift_down(dm1, z)
            dxa = dp1 * cwv[0:1, :] + dxc * cwv[1:2, :] + dm1 * cwv[2:3, :] + dm2 * cwv[3:4, :]
            xm1 = _shift_down(xa, z)
            xp1 = _shift_up(xa, z)
            xp2 = _shift_up(xp1, z)
            g_cw = jnp.concatenate([jnp.sum(dxc * v, axis=0, keepdims=True) for v in (xm1, xa, xp1, xp2)], axis=0)
            g_cb = jnp.sum(dxc, axis=0, keepdims=True)
            if first:
                dlam_ref[...] = g_lam
                dcw_ref[...] = g_cw
                dcb_ref[...] = g_cb
            else:
                dlam_ref[...] += g_lam
                dcw_ref[...] += g_cw
                dcb_ref[...] += g_cb
            return dxa, rho_f_first, rho_b_last

        ws_x, ws_c = workspace(main_s), workspace(ctx_s)
        h0f, h0b = forward(ws_c, xac_ref[...], LC, z, z)
        forward(ws_x, xa_ref[...], L, h0f, h0b)
        dh = dyl_ref[...]
        dxa, dh0f, dh0b = backward(ws_x, xa_ref[...], L, h0f, h0b, dh, dh, True)
        dxa_ref[...] = dxa.astype(BF16)
        rc = _rows((LC, HD))
        dxac, _, _ = backward(ws_c, xac_ref[...], LC, z, z, jnp.where(rc == LC - 1, dh0f, 0.0),
                              jnp.where(rc == 0, dh0b, 0.0), False)
        dxac_ref[...] = dxac.astype(BF16)

    s = _lru_param_specs()
    col = lambda r: pl.BlockSpec((r, HD), lambda h: (0, h))
    return _call(
        body, name="lru_backward", grid=(HEADS,),
        out_shape=[jax.ShapeDtypeStruct((L, D_IN), BF16), jax.ShapeDtypeStruct((LC, D), BF16),
                   jax.ShapeDtypeStruct((2, HEADS, HD, HD), F32), jax.ShapeDtypeStruct((2, HEADS, HD, HD), F32),
                   jax.ShapeDtypeStruct((2, D), F32), jax.ShapeDtypeStruct((2, D), F32),
                   jax.ShapeDtypeStruct((2, D), F32), jax.ShapeDtypeStruct((CONV_W, D), F32),
                   jax.ShapeDtypeStruct((1, D), F32)] + [jax.ShapeDtypeStruct((4,) + a.shape[1:], a.dtype)
                                                          for a in chip_sums] + _chips_stage_shapes(chip_sums),
        in_specs=[s["xa"], s["xac"], col(L), pl.BlockSpec(memory_space=pl.ANY), s["cw"], s["cb"], s["h4"], s["h4"],
                  s["v2"], s["v2"], s["v2"]] + [HBM] * nr,
        out_specs=[col(L), col(LC), s["h4"], s["h4"], s["v2"], s["v2"], s["v2"], col(CONV_W), col(1)]
        + [HBM] * (2 * nr),
        scratch_shapes=[pltpu.VMEM((17, L, HD), F32), pltpu.VMEM((17, LC, HD), F32)] + (_chips_sems(nr) if nr else []),
        input_output_aliases={3: 0},
        compiler_params=pltpu.CompilerParams(dimension_semantics=("arbitrary",), vmem_limit_bytes=VMEM_LIMIT,
                                             has_side_effects=True),
    )(zx, zc, dyl, dz, cw, cb, wa, wx, ba, bx, lam, *[pltpu.with_memory_space_constraint(a, pltpu.HBM)
                                                       for a in chip_sums])


def _mixer_loss(x, tgt, zx, yl, gx, fg, lng, lnb, ws, wst, bst, wout, tm):
    ncht = tm // CHUNK

    def body(x_ref, t_ref, ga_ref, u_ref, v_ref, gb_ref, yl_ref, gx_ref, fg_ref, lng_ref, lnb_ref, ws_ref, wst_ref,
             bst_ref, wout_ref,
             dz_ref, dyl_ref, dxn_ref, y_s, do_ref, dws_ref, dbst_ref, vec_ref,
             vn_s, mix_s, dm_s, dvn_s):
        step = pl.program_id(0)

        @pl.when(step == 0)
        def _():
            dws_ref[...] = jnp.zeros_like(dws_ref)
            dbst_ref[...] = jnp.zeros_like(dbst_ref)
            vec_ref[...] = jnp.zeros_like(vec_ref)

        u, v = u_ref[...], v_ref[...]
        ug, dug_du = _gelu_and_grad(u)
        vg, dvg_dv = _gelu_and_grad(v)
        mu = jnp.mean(vg, axis=-1, keepdims=True)
        vc = vg - mu
        rstd = lax.rsqrt(jnp.mean(vc * vc, axis=-1, keepdims=True) + LN_EPS)
        vhat = vc * rstd
        lngv = lng_ref[...]
        vn_s[...] = (vhat * lngv + lnb_ref[...]).astype(BF16)
        for ch in range(ncht):
            rs = slice(ch * CHUNK, (ch + 1) * CHUNK)
            for g in range(HEADS):
                cs = slice(g * HD, (g + 1) * HD)
                mix_s[rs, cs] = _dot(ws_ref[g], vn_s[rs, cs]) + bst_ref[:, g:g + 1]
        mixed = mix_s[...]
        ga, gb, yl = ga_ref[...], gb_ref[...], yl_ref[...]
        sga, dsga = _silu_and_grad(ga)
        sgb, dsgb = _silu_and_grad(gb)
        ys = ug * mixed
        y_s[:, 0:D] = (yl * sga).astype(BF16)
        y_s[:, D:D_MIX] = (ys * sgb).astype(BF16)
        o = _dot(y_s[...], wout_ref[...])
        gxv, fgv = gx_ref[...], fg_ref[...]
        xn = x_ref[...] + gxv * o
        rs2 = lax.rsqrt(jnp.mean(xn * xn, axis=-1, keepdims=True) + NORM_EPS)
        xh = xn * rs2
        diff = xh * fgv - t_ref[...]
        vec_ref[R_LOSS:R_LOSS + 1, :] += jnp.full((1, D), jnp.sum(diff * diff) * (0.5 / D), F32)
        dout = diff * (1.0 / D)
        w = dout * fgv
        dxn = rs2 * (w - xh * jnp.mean(w * xh, axis=-1, keepdims=True))
        dxn_ref[...] = dxn
        vec_ref[0:1, :] += jnp.sum(dxn * o, axis=0, keepdims=True)
        vec_ref[1:2, :] += jnp.sum(dout * xh, axis=0, keepdims=True)
        dob = (dxn * gxv).astype(BF16)
        do_ref[...] = dob
        dy = _dot_nt(dob, wout_ref[...])
        dya, dyb = dy[:, 0:D], dy[:, D:D_MIX]
        dyl_ref[...] = dya * sga
        dys = dyb * sgb
        dz_ref[:, 0:D] = jnp.zeros((tm, D), BF16)
        dz_ref[:, D:2 * D] = (dya * yl * dsga).astype(BF16)
        dz_ref[:, 2 * D:3 * D] = (dys * mixed * dug_du).astype(BF16)
        dz_ref[:, 4 * D:5 * D] = (dyb * ys * dsgb).astype(BF16)
        dm = dys * ug
        dm_s[...] = dm.astype(BF16)
        for g in range(HEADS):
            cs = slice(g * HD, (g + 1) * HD)
            dbst_ref[:, g:g + 1] += sum(jnp.sum(dm[ch * CHUNK:(ch + 1) * CHUNK, cs], axis=1, keepdims=True)
                                        for ch in range(ncht))
            for ch in range(ncht):
                rs = slice(ch * CHUNK, (ch + 1) * CHUNK)
                dws_ref[g] += _dot_nt(dm_s[rs, cs], vn_s[rs, cs])
                dvn_s[rs, cs] = _dot(wst_ref[g], dm_s[rs, cs])
        dvn = dvn_s[...]
        vec_ref[2:3, :] += jnp.sum(dvn * vhat, axis=0, keepdims=True)
        vec_ref[3:4, :] += jnp.sum(dvn, axis=0, keepdims=True)
        dvh = dvn * lngv
        dvg = rstd * (dvh - jnp.mean(dvh, axis=-1, keepdims=True) - vhat * jnp.mean(dvh * vhat, axis=-1, keepdims=True))
        dz_ref[:, 3 * D:4 * D] = (dvg * dvg_dv).astype(BF16)

    tile = pl.BlockSpec((tm, D), lambda i: (i, 0))
    zcol = lambda n: pl.BlockSpec((tm, D), lambda i: (i, n))
    vec = pl.BlockSpec((1, D), lambda i: (0, 0))
    full = lambda *s: pl.BlockSpec(s, lambda i: (0,) * len(s))
    return _call(
        body, name="mixer_loss", grid=(L // tm,),
        out_shape=[jax.ShapeDtypeStruct((L, D_IN), BF16), jax.ShapeDtypeStruct((L, D), F32),
                   jax.ShapeDtypeStruct((L, D), F32), jax.ShapeDtypeStruct((L, D_MIX), BF16),
                   jax.ShapeDtypeStruct((L, D), BF16),
                   jax.ShapeDtypeStruct((HEADS, CHUNK, CHUNK), F32), jax.ShapeDtypeStruct((CHUNK, HEADS), F32),
                   jax.ShapeDtypeStruct((8, D), F32)],
        in_specs=[tile, tile, zcol(1), zcol(2), zcol(3), zcol(4), tile, pl.BlockSpec((1, D), lambda i: (0, 2)),
                  vec, vec, vec,
                  full(HEADS, CHUNK, CHUNK), full(HEADS, CHUNK, CHUNK), full(CHUNK, HEADS),
                  pl.BlockSpec((D_MIX, D), lambda i: (0, 0), pipeline_mode=pl.Buffered(1))],
        out_specs=[pl.BlockSpec((tm, D_IN), lambda i: (i, 0)), tile, tile,
                   pl.BlockSpec((tm, D_MIX), lambda i: (i, 0)), tile,
                   full(HEADS, CHUNK, CHUNK), full(CHUNK, HEADS), full(8, D)],
        scratch_shapes=[pltpu.VMEM((tm, D), BF16), pltpu.VMEM((tm, D), F32),
                        pltpu.VMEM((tm, D), BF16), pltpu.VMEM((tm, D), F32)],
        compiler_params=_params("arbitrary"),
    )(x, tgt, zx, zx, zx, zx, yl, gx, fg, lng, lnb, ws, wst, bst, wout)


def _grad_w(a, b, a2, b2, tk, name, bw=D, first=0, nblocks=None):
    nk = a.shape[0] // tk
    m = a.shape[1]
    nblocks = nblocks or b.shape[1] // bw
    with_ctx = a2 is not None

    def body(*refs):
        if with_ctx:
            a_ref, b_ref, a2_ref, b2_ref, o_ref, acc = refs
        else:
            a_ref, b_ref, o_ref, acc = refs
        n, k = pl.program_id(0), pl.program_id(1)

        @pl.when(k == 0)
        def _():
            acc[...] = jnp.zeros_like(acc)

        acc[...] += _dot_tn(a_ref[...], b_ref[...])

        if with_ctx:
            @pl.when(jnp.logical_and(k == nk - 1, n == 0))
            def _():
                acc[:, 0:b2_ref.shape[1]] += _dot_tn(a2_ref[...], b2_ref[...])

        @pl.when(k == nk - 1)
        def _():
            o_ref[...] = acc[...].astype(BF16)

    in_specs = [pl.BlockSpec((tk, m), lambda n, k: (k, 0)), pl.BlockSpec((tk, bw), lambda n, k: (k, n + first))]
    args = [a, b]
    if with_ctx:
        in_specs += [pl.BlockSpec(a2.shape, lambda n, k: (0, 0)), pl.BlockSpec(b2.shape, lambda n, k: (0, 0))]
        args += [a2, b2]
    return _call(
        body, name=name, grid=(nblocks, nk), out_shape=jax.ShapeDtypeStruct((m, nblocks * bw), BF16),
        in_specs=in_specs, out_specs=pl.BlockSpec((m, bw), lambda n, k: (0, n)),
        scratch_shapes=[pltpu.VMEM((m, bw), F32)],
        compiler_params=_params("arbitrary", "arbitrary"),
    )(*args)


def _grad_rows(xr, dz, w, mod, ng, dres, ncols, tm, name, chip_sums=(), first_chips=None, dests=None):
    rows = xr.shape[0]
    steps = rows // tm
    with_dx = dres is not None
    nr = len(chip_sums)
    dests = [d for d in (dests or [None] * nr)]
    nd = sum(d is not None for d in dests)
    nin = 6 if with_dx else 5
    nout = 2 if with_dx else 1

    def body(*refs):
        if with_dx:
            x_ref, dz_ref, w_ref, sc_ref, ng_ref, dres_ref = refs[:nin]
            dx_ref, vec_ref = refs[nin + nr + nd:nin + nr + nd + nout]
        else:
            x_ref, dz_ref, w_ref, sc_ref, ng_ref = refs[:nin]
            (vec_ref,) = refs[nin + nr + nd:nin + nr + nd + nout]
        if nr:
            o0 = nin + nr + nd + nout
            start, forward, finish = _chips_ops(refs[nin:nin + nr], refs[o0:o0 + nr], refs[o0 + nr:o0 + 2 * nr],
                                                *refs[o0 + 2 * nr:], first_chips=first_chips)
            pl.when(pl.program_id(0) == 0)(start)
            pl.when(pl.program_id(0) == 2)(forward)
            pl.when(pl.program_id(0) == steps - 1)(finish)

        @pl.when(pl.program_id(0) == 0)
        def _():
            vec_ref[...] = jnp.zeros_like(vec_ref)

        dhn = _dot_nt(dz_ref[...], w_ref[...])
        x = x_ref[...]
        rs = lax.rsqrt(jnp.mean(x * x, axis=-1, keepdims=True) + NORM_EPS)
        xh = x * rs
        ngv = ng_ref[...]
        y = xh * ngv
        vec_ref[0:1, :] += jnp.sum(dhn, axis=0, keepdims=True)
        vec_ref[1:2, :] += jnp.sum(dhn * y, axis=0, keepdims=True)
        dy = dhn * (1.0 + sc_ref[...])
        vec_ref[2:3, :] += jnp.sum(dy * xh, axis=0, keepdims=True)
        if with_dx:
            dxh = dy * ngv
            dx_ref[...] = dres_ref[...] + rs * (dxh - xh * jnp.mean(dxh * xh, axis=-1, keepdims=True))

    tile = pl.BlockSpec((tm, D), lambda i: (i, 0))
    vec = pl.BlockSpec((1, D), lambda i: (0, 0))
    in_specs = [tile, pl.BlockSpec((tm, ncols), lambda i: (i, 0)),
                pl.BlockSpec((D, ncols), lambda i: (0, 0), pipeline_mode=pl.Buffered(1)),
                pl.BlockSpec((1, D), lambda i: (0, 1)), vec]
    out_shape = [jax.ShapeDtypeStruct((8, D), F32)]
    out_specs = [pl.BlockSpec((8, D), lambda i: (0, 0))]
    args = [xr, dz, w, mod, ng]
    if with_dx:
        in_specs.append(tile)
        out_shape.insert(0, jax.ShapeDtypeStruct((rows, D), F32))
        out_specs.insert(0, tile)
        args.append(dres)
    aliases = {}
    for j, d in enumerate(dests):
        if d is not None:
            aliases[len(args) + nr + len(aliases)] = len(out_shape) + j
    in_specs += [HBM] * (nr + nd)
    out_specs += [HBM] * (2 * nr)
    out_shape += [jax.ShapeDtypeStruct((4,) + a.shape[1:], a.dtype) for a in chip_sums]
    out_shape += _chips_stage_shapes(chip_sums)
    args += [pltpu.with_memory_space_constraint(a, pltpu.HBM) for a in chip_sums]
    args += [pltpu.with_memory_space_constraint(d, pltpu.HBM) for d in dests if d is not None]
    return _call(body, name=name, grid=(steps,), out_shape=out_shape, in_specs=in_specs, out_specs=out_specs,
                 scratch_shapes=_chips_sems(nr) if nr else [], input_output_aliases=aliases,
                 compiler_params=pltpu.CompilerParams(dimension_semantics=("arbitrary",),
                                                      vmem_limit_bytes=VMEM_LIMIT, has_side_effects=bool(nr)))(*args)


def _adamw(w, g, m, v):
    m = ADAM_B1 * m + (1.0 - ADAM_B1) * g
    v = ADAM_B2 * v + (1.0 - ADAM_B2) * (g * g)
    m_hat = m / (1.0 - ADAM_B1 ** ADAM_STEP)
    v_hat = v / (1.0 - ADAM_B2 ** ADAM_STEP)
    delta = -ADAM_LR * (m_hat / (jnp.sqrt(v_hat) + ADAM_EPS) + ADAM_WD * w)
    return delta, m, v


def _adamw_reduced(parts, w, m, v, tr, name):
    r, n = w.shape
    nparts = parts.shape[0]

    def body(p_ref, w_ref, m_ref, v_ref, g_ref, d_ref, mo_ref, vo_ref):
        g = p_ref[0].astype(F32)
        for i in range(1, nparts):
            g = g + p_ref[i].astype(F32)
        g_ref[...] = g
        d_ref[...], mo_ref[...], vo_ref[...] = _adamw(w_ref[...], g, m_ref[...], v_ref[...])

    tile = pl.BlockSpec((tr, n), lambda i: (i, 0))
    sds = jax.ShapeDtypeStruct((r, n), F32)
    return _call(
        body, name=name, grid=(r // tr,), out_shape=[sds] * 4,
        in_specs=[pl.BlockSpec((nparts, tr, n), lambda i: (0, i, 0)), tile, tile, tile], out_specs=[tile] * 4,
        compiler_params=_params("arbitrary"),
    )(parts, w, m, v)


R_GATE, R_FINAL_G, R_LN_G, R_LN_B, R_LOSS = 0, 1, 2, 3, 4
R_SH_X, R_SC_X, R_NG_X = 5, 6, 7
R_SH_C, R_SC_C, R_NG_C = 8, 9, 10
R_BA, R_BX, R_LAM, R_CW, R_CB, R_SGU_B = 11, 13, 15, 17, 21, 22
PACK_ROWS = 32
MAT_ROWS = 2 * (2 * HEADS * HD) + HEADS * CHUNK


def _reduce_small(vp_all, mat_parts, ada_w, me):
    nloc = ada_w.shape[1]

    def body(me_ref, vp_ref, mp_ref, w_ref, red_ref, mat_ref, dmod_ref, gab_ref, cpart_ref, dmc_s):
        red = vp_ref[0]
        for i in range(1, N_DEV):
            red = red + vp_ref[i]
        mat = mp_ref[0].astype(F32)
        for i in range(1, mp_ref.shape[0]):
            mat = mat + mp_ref[i].astype(F32)
        red_ref[...] = red
        mat_ref[...] = mat
        for e in range(N_DEV):
            dmod_ref[e:e + 1, 0:D] = vp_ref[e, R_SH_X:R_SH_X + 1, :]
            dmod_ref[e:e + 1, D:2 * D] = vp_ref[e, R_SC_X:R_SC_X + 1, :]
            dmod_ref[e:e + 1, 2 * D:3 * D] = vp_ref[e, R_GATE:R_GATE + 1, :]
        dmod_ref[8:9, 0:D] = red[R_SH_C:R_SH_C + 1, :]
        dmod_ref[8:9, D:2 * D] = red[R_SC_C:R_SC_C + 1, :]
        dmod_ref[8:9, 2 * D:3 * D] = jnp.zeros((1, D), F32)
        dmod_ref[9:16, :] = jnp.zeros((7, 3 * D), F32)
        gab_ref[:, 0:D] = red[R_SH_X:R_SH_X + 1, :] + red[R_SH_C:R_SH_C + 1, :]
        gab_ref[:, D:2 * D] = red[R_SC_X:R_SC_X + 1, :] + red[R_SC_C:R_SC_C + 1, :]
        gab_ref[:, 2 * D:3 * D] = red[R_GATE:R_GATE + 1, :]
        dmc_s[...] = jnp.broadcast_to(dmod_ref[8:9, :], (8, 3 * D))
        off = pl.multiple_of(me_ref[0] * nloc, 128)
        cpart_ref[...] = _dot_nt(dmc_s[:, pl.ds(off, nloc)], w_ref[...])

    return _call(
        body, name="reduce_small",
        out_shape=[jax.ShapeDtypeStruct((PACK_ROWS, D), F32), jax.ShapeDtypeStruct(mat_parts.shape[1:], F32),
                   jax.ShapeDtypeStruct((16, 3 * D), F32), jax.ShapeDtypeStruct((1, 3 * D), F32),
                   jax.ShapeDtypeStruct((8, D), F32)],
        in_specs=[pl.BlockSpec(memory_space=pltpu.SMEM), VMEM, VMEM, VMEM], out_specs=[VMEM] * 5,
        scratch_shapes=[pltpu.VMEM((8, 3 * D), F32)], compiler_params=_params(),
    )(me, vp_all, mat_parts, ada_w)


def _adamw_ada(c_all, c_ctx, dmod, w, m, v, me):
    nloc = w.shape[1]

    def body(me_ref, c_ref, cc_ref, dm_ref, w_ref, m_ref, v_ref, g_ref, d_ref, mo_ref, vo_ref):
        off = pl.multiple_of(me_ref[0] * nloc, 128)
        dm = dm_ref[:, pl.ds(off, nloc)]
        sx, _ = _silu_and_grad(c_ref[...])
        sc, _ = _silu_and_grad(cc_ref[...])
        g = _dot_tn(sx, dm[0:8, :]) + _dot_tn(jnp.broadcast_to(sc, (8, D)), dm[8:16, :])
        g_ref[...] = g
        d_ref[...], mo_ref[...], vo_ref[...] = _adamw(w_ref[...], g, m_ref[...], v_ref[...])

    sds = jax.ShapeDtypeStruct(w.shape, F32)
    return _call(
        body, name="adamw_ada_w", out_shape=[sds] * 4,
        in_specs=[pl.BlockSpec(memory_space=pltpu.SMEM)] + [VMEM] * 6, out_specs=[VMEM] * 4,
        compiler_params=_params(),
    )(me, c_all, c_ctx, dmod, w, m, v)


_SMALL = ("c_ctx", "ada_b", "norm_g", "conv_w", "conv_b", "lru_wa", "lru_ba", "lru_wx", "lru_bx", "lru_lambda",
          "sgu_ln_g", "sgu_ln_b", "sgu_w", "sgu_b", "final_g")


def _adamw_small(red, mat, cparts, gab, ws, ms, vs, me):
    n = len(_SMALL)
    nw = 2 * HEADS * HD

    def body(me_ref, red_ref, mat_ref, cp_ref, gab_ref, *refs):
        w_refs, m_refs, v_refs = refs[:n], refs[n:2 * n], refs[2 * n:3 * n]
        outs = refs[3 * n:]
        off = pl.multiple_of(me_ref[0] * HD, 128)

        def row(r, k=1):
            return red_ref[r:r + k, :]

        cc = w_refs[0][...]
        dcc = cp_ref[0, 0:1, :]
        for i in range(1, N_DEV):
            dcc = dcc + cp_ref[i, 0:1, :]
        grads = dict(
            c_ctx=dcc * _silu_and_grad(cc)[1], ada_b=gab_ref[...], norm_g=row(R_NG_X) + row(R_NG_C),
            conv_w=red_ref[R_CW:R_CW + CONV_W, pl.ds(off, HD)], conv_b=row(R_CB),
            lru_wa=mat_ref[0:nw, :], lru_ba=row(R_BA, 2), lru_wx=mat_ref[nw:2 * nw, :], lru_bx=row(R_BX, 2),
            lru_lambda=red_ref[R_LAM:R_LAM + 2, pl.ds(off, HD)], sgu_ln_g=row(R_LN_G), sgu_ln_b=row(R_LN_B),
            sgu_w=mat_ref[2 * nw:MAT_ROWS, :], sgu_b=row(R_SGU_B), final_g=row(R_FINAL_G))
        for j, name in enumerate(_SMALL):
            g = grads[name]
            outs[j][...] = g
            outs[n + j][...], outs[2 * n + j][...], outs[3 * n + j][...] = _adamw(w_refs[j][...], g, m_refs[j][...],
                                                                                 v_refs[j][...])

    sds = [jax.ShapeDtypeStruct(ws[k].shape, F32) for k in _SMALL]
    outs = _call(
        body, name="adamw_small", out_shape=sds * 4,
        in_specs=[pl.BlockSpec(memory_space=pltpu.SMEM)] + [VMEM] * (4 + 3 * n), out_specs=[VMEM] * (4 * n),
        compiler_params=_params(),
    )(me, red, mat, cparts, gab, *[ws[k] for k in _SMALL], *[ms[k] for k in _SMALL], *[vs[k] for k in _SMALL])
    return [dict(zip(_SMALL, outs[i * n:(i + 1) * n])) for i in range(4)]


def kernel(x, c, ctx, c_ctx, ada_w, ada_b, norm_g, w_in, conv_w, conv_b, lru_wa, lru_ba, lru_wx, lru_bx, lru_lambda, sgu_ln_g, sgu_ln_b, sgu_w, sgu_b, w_out, final_g, loss_target, m_c_ctx, m_ada_w, m_ada_b, m_norm_g, m_w_in, m_conv_w, m_conv_b, m_lru_wa, m_lru_ba, m_lru_wx, m_lru_bx, m_lru_lambda, m_sgu_ln_g, m_sgu_ln_b, m_sgu_w, m_sgu_b, m_w_out, m_final_g, v_c_ctx, v_ada_w, v_ada_b, v_norm_g, v_w_in, v_conv_w, v_conv_b, v_lru_wa, v_lru_ba, v_lru_wx, v_lru_bx, v_lru_lambda, v_sgu_ln_g, v_sgu_ln_b, v_sgu_w, v_sgu_b, v_w_out, v_final_g):
    args = dict(locals())
    me_s = 4 * lax.axis_index("x") + 2 * lax.axis_index("y") + lax.axis_index("c")
    me = me_s.astype(jnp.int32).reshape(1)
    xr, ctxr, tgt = x[0], ctx[0], loss_target[0]
    cc = c_ctx.reshape(1, D)
    nw = 2 * HEADS * HD
    view = dict(c_ctx=(1, D), ada_b=(1, 3 * D), norm_g=(1, D), conv_w=(CONV_W, HD), conv_b=(1, D), lru_wa=(nw, HD),
                lru_ba=(2, D), lru_wx=(nw, HD), lru_bx=(2, D), lru_lambda=(2, HD), sgu_ln_g=(1, D), sgu_ln_b=(1, D),
                sgu_w=(HEADS * CHUNK, CHUNK), sgu_b=(1, D), final_g=(1, D))

    zx, hn, w_full, w_out_b, modx, modc, c_all, cw_full, lam_full = _front_project(
        xr, c, cc, ada_w[0], ada_b, norm_g, w_in[0], w_out[0], conv_w[0], lru_lambda[0], me)
    zc, hnc = _project(ctxr, modc, norm_g, w_full, D, LC, "project_ctx")
    ba, bx = lru_ba.reshape(2, D), lru_bx.reshape(2, D)
    yl, wout_all = _lru_forward(zx, zc, cw_full, conv_b, lru_wa[0], lru_wx[0], ba, bx, lam_full, [w_out_b], ["ag"])
    wout_full = wout_all.reshape(D_MIX, D)
    ws_b = sgu_w[0].astype(BF16)
    dz, dyl, dxn, ycat, dob, dws, dbst, mvec = _mixer_loss(
        xr, tgt, zx, yl, modx, final_g.reshape(1, D), sgu_ln_g, sgu_ln_b, ws_b, jnp.swapaxes(ws_b, 1, 2),
        sgu_b[0].T, wout_full, 256)

    gw_out = _grad_w(ycat, dob, None, None, 1024, "grad_w_out")
    gw_rest = _grad_w(hn, dz, None, None, 1024, "grad_w_in_rest", bw=2 * W_IN_SHARD, first=1, nblocks=3)
    rest_sums, wout_sums = _reduce2_local([gw_rest, gw_out.reshape(N_DEV, D_MIX // N_DEV, D)], ["a2ac", "a2a"], me,
                                          "reduce_early", counts=[3, 4])
    dz, dxac, dwa, dwx, dba, dbx, dlam, dcw, dcb, win_parts, wout_parts, _, _ = _lru_backward(
        zx, zc, dyl, dz, cw_full, conv_b, lru_wa[0], lru_wx[0], ba, bx, lam_full, [rest_sums, wout_sums],
        first_chips=[1, 0])
    gw_first = _grad_w(hn, dz, hnc, dxac, 1024, "grad_w_in_first", bw=2 * W_IN_SHARD, first=0, nblocks=1)
    matpack = jnp.concatenate([dwa.reshape(nw, HD), dwx.reshape(nw, HD), dws.reshape(HEADS * CHUNK, CHUNK)],
                              axis=0).astype(BF16)
    first_sums, mat_sums = _reduce2_local([gw_first, matpack.reshape(N_DEV, MAT_ROWS // N_DEV, HD)], ["a2ac", "a2a"],
                                          me, "reduce_late", counts=[1, 4])
    gx, xvec, win_parts, mat_parts, _, _ = _grad_rows(
        xr, dz, w_full, modx, norm_g, dxn, D_IN, 256, "grad_rows_x", chip_sums=[first_sums, mat_sums],
        first_chips=[0, 0], dests=[win_parts, None])
    (cvec,) = _grad_rows(ctxr, dxac, w_full, modc, norm_g, None, D, LC, "grad_rows_ctx")
    pack = jnp.concatenate([mvec[0:5], xvec[0:3], cvec[0:3], dba, dbx, dlam, dcw, dcb, dbst.T.reshape(1, D),
                            jnp.zeros((PACK_ROWS - R_SGU_B - 1, D), F32)], axis=0)
    (vp_all,) = _gather2([pack], ["ag"], "gather_pack")
    red, matpiece, dmod, gab, cpart = _reduce_small(vp_all, mat_parts, ada_w[0], me)
    mat_all, cparts = _gather2([matpiece, cpart], ["ag", "ag"], "gather_small")

    g_w_in, d_w_in, nm_w_in, nv_w_in = _adamw_reduced(win_parts, w_in[0], m_w_in[0], v_w_in[0], 256, "adamw_w_in")
    g_w_out, d_w_out, nm_w_out, nv_w_out = _adamw_reduced(wout_parts, w_out[0], m_w_out[0], v_w_out[0], 128,
                                                          "adamw_w_out")
    g_ada, d_ada, nm_ada, nv_ada = _adamw_ada(c_all, cc, dmod, ada_w[0], m_ada_w[0], v_ada_w[0], me)
    ws = {k: args[k].reshape(view[k]) for k in _SMALL}
    ms = {k: args["m_" + k].reshape(view[k]) for k in _SMALL}
    vs = {k: args["v_" + k].reshape(view[k]) for k in _SMALL}
    small = _adamw_small(red, mat_all.reshape(MAT_ROWS, HD), cparts, gab, ws, ms, vs, me)
    big = dict(w_in=(g_w_in, d_w_in, nm_w_in, nv_w_in), w_out=(g_w_out, d_w_out, nm_w_out, nv_w_out),
               ada_w=(g_ada, d_ada, nm_ada, nv_ada))

    loss = red[R_LOSS, 0]
    names = ("c_ctx", "ada_w", "ada_b", "norm_g", "w_in", "conv_w", "conv_b", "lru_wa", "lru_ba", "lru_wx", "lru_bx",
             "lru_lambda", "sgu_ln_g", "sgu_ln_b", "sgu_w", "sgu_b", "w_out", "final_g")
    outs = [loss, gx.reshape(x.shape)]
    for kind in range(4):
        for k in names:
            val = big[k][kind] if k in big else small[kind][k]
            outs.append(val.reshape(args[k].shape))
    return tuple(outs)
```

```python
import functools

import jax
import jax.numpy as jnp
from jax import lax
from jax.experimental import pallas as pl
from jax.experimental.pallas import tpu as pltpu

F32 = jnp.float32
BF16 = jnp.bfloat16

N_DEV = 8
D = 1024
L = 2048
LC = 256
HEADS = 8
HD = 128
CHUNK = 128
D_IN = 5 * D
W_IN_SHARD = D_IN // N_DEV
ROWS = 256
D_MIX = 2 * D
CONV_W = 4
LRU_C = 8.0
NORM_EPS = 1e-6
LN_EPS = 1e-5
ADAM_LR, ADAM_B1, ADAM_B2, ADAM_EPS, ADAM_WD, ADAM_STEP = 0.001, 0.9, 0.999, 1e-08, 0.01, 10

VMEM_LIMIT = 56 * 1024 * 1024

HBM = pl.BlockSpec(memory_space=pltpu.HBM)
VMEM = pl.BlockSpec(memory_space=pltpu.VMEM)
MESH = pl.DeviceIdType.MESH


def _call(body, **kw):
    return pl.pallas_call(body, **kw)


def _params(*sem):
    return pltpu.CompilerParams(dimension_semantics=sem, vmem_limit_bytes=VMEM_LIMIT)


def _sigmoid(x):
    return 0.5 * jnp.tanh(0.5 * x) + 0.5


def _silu_and_grad(x):
    s = _sigmoid(x)
    return x * s, s * (1.0 + x * (1.0 - s))


_G0 = 0.7978845608028654
_G1 = 0.044715


def _gelu_and_grad(x):
    x2 = x * x
    t = jnp.tanh(_G0 * (x + _G1 * x * x2))
    cdf = 0.5 * (1.0 + t)
    return x * cdf, cdf + 0.5 * x * (1.0 - t * t) * (_G0 * (1.0 + 3.0 * _G1 * x2))


def _gelu(x):
    return 0.5 * x * (1.0 + jnp.tanh(_G0 * (x + _G1 * x * x * x)))


def _softplus(z):
    t = jnp.exp(-jnp.abs(z))
    u = 1.0 + t
    log1p = jnp.where(u == 1.0, t, jnp.log(u) * t / jnp.where(u == 1.0, 1.0, u - 1.0))
    return jnp.maximum(z, 0.0) + log1p


def _dot(a, b):
    return jnp.dot(a, b, preferred_element_type=F32)


def _dot_nt(a, b):
    return lax.dot_general(a, b, (((1,), (1,)), ((), ())), preferred_element_type=F32)


def _dot_tn(a, b):
    return lax.dot_general(a, b, (((0,), (0,)), ((), ())), preferred_element_type=F32)


def _rows(shape):
    return lax.broadcasted_iota(jnp.int32, shape, 0)


def _shift_down(x, first):
    return jnp.where(_rows(x.shape) == 0, first, pltpu.roll(x, 1, 0))


def _shift_up(x, last):
    n = x.shape[0]
    return jnp.where(_rows(x.shape) == n - 1, last, pltpu.roll(x, n - 1, 0))


def _gather2(arrays, modes, name):
    n = len(arrays)

    def body(*refs):
        start, forward, finish = _gather2_ops(refs[:n], refs[n:2 * n], modes, *refs[2 * n:])
        start()
        forward()
        finish()

    return _call(
        body, name=name, out_shape=_gather2_shapes(arrays, modes), in_specs=[HBM] * n, out_specs=[HBM] * n,
        scratch_shapes=_gather2_sems(n), compiler_params=pltpu.CompilerParams(has_side_effects=True),
    )(*[pltpu.with_memory_space_constraint(a, pltpu.HBM) for a in arrays])


def _gather2_shapes(arrays, modes):
    return [jax.ShapeDtypeStruct((N_DEV,) + a.shape if m == "ag" else (a.shape[0], N_DEV * a.shape[1]), a.dtype)
            for a, m in zip(arrays, modes)]


def _gather2_sems(n):
    return [pltpu.SemaphoreType.DMA((n, N_DEV - 1)), pltpu.SemaphoreType.DMA((n, N_DEV - 1)),
            pltpu.SemaphoreType.DMA((n,))]


def _gather2_ops(ins, outs, modes, send_sems, recv_sems, local_sems):
    n = len(ins)
    x, y, c = lax.axis_index("x"), lax.axis_index("y"), lax.axis_index("c")
    me, sibling = (x, y, c), (x, y, 1 - c)
    chips = [(x ^ (k >> 1), y ^ (k & 1)) for k in (1, 2, 3)]

    def slot(j, px, py, pc):
        dev = 4 * px + 2 * py + pc
        if modes[j] == "agc":
            w = ins[j].shape[1]
            return outs[j].at[:, pl.ds(pl.multiple_of(dev * w, 128), w)]
        return outs[j].at[dev]

    def copy(j, k, block, to, src=None):
        return pltpu.make_async_remote_copy(
            src_ref=slot(j, *block) if src is None else src, dst_ref=slot(j, *block),
            send_sem=send_sems.at[j, k], recv_sem=recv_sems.at[j, k], device_id=to, device_id_type=MESH)

    def own(j):
        return pltpu.make_async_copy(ins[j], slot(j, *me), local_sems.at[j])

    def first(j):
        return [copy(j, 0, me, sibling, src=ins[j])] + [copy(j, 1 + i, me, (*chip, c), src=ins[j])
                                                        for i, chip in enumerate(chips)]

    def passed(j, i):
        return copy(j, 4 + i, (*chips[i], c), sibling)

    def start():
        for j in range(n):
            own(j).start()
            for cp in first(j):
                cp.start()

    def forward():
        for i, chip in enumerate(chips):
            for j in range(n):
                copy(j, 1 + i, (*chip, c), me).wait_recv()
                passed(j, i).start()

    def finish():
        for j in range(n):
            copy(j, 0, sibling, me).wait_recv()
            for i, chip in enumerate(chips):
                copy(j, 4 + i, (*chip, 1 - c), me).wait_recv()
            for cp in first(j) + [passed(j, i) for i in range(3)]:
                cp.wait_send()
            own(j).wait()

    return start, forward, finish


def _reduce2_local(arrays, modes, me, name, counts=None):
    n = len(arrays)
    counts = counts or [4] * n
    shapes = [(a.shape[1], a.shape[2]) if m == "a2a" else (a.shape[0], a.shape[1] // (2 * cnt))
              for a, m, cnt in zip(arrays, modes, counts)]
    staged = [jax.ShapeDtypeStruct((cnt,) + s, a.dtype) for s, a, cnt in zip(shapes, arrays, counts)]

    def piece(ref, mode, dev, w):
        return ref.at[dev] if mode == "a2a" else ref.at[:, pl.ds(pl.multiple_of(dev * w, 128), w)]

    def to_sibling(*refs):
        ins, outs = refs[:n], refs[n:2 * n]
        send_sems, recv_sems = refs[2 * n:]
        x, y, c = lax.axis_index("x"), lax.axis_index("y"), lax.axis_index("c")
        copies = []
        for j in range(n):
            for q in range(counts[j]):
                cp = pltpu.make_async_remote_copy(
                    src_ref=piece(ins[j], modes[j], 2 * q + (1 - c), shapes[j][1]), dst_ref=outs[j].at[q],
                    send_sem=send_sems.at[j, q], recv_sem=recv_sems.at[j, q], device_id=(x, y, 1 - c),
                    device_id_type=MESH)
                cp.start()
                copies.append(cp)
        for cp in copies:
            cp.wait()

    stage = _call(
        to_sibling, name=name + "_d2d", out_shape=staged, in_specs=[HBM] * n, out_specs=[HBM] * n,
        scratch_shapes=[pltpu.SemaphoreType.DMA((n, 4)), pltpu.SemaphoreType.DMA((n, 4))],
        compiler_params=pltpu.CompilerParams(has_side_effects=True),
    )(*[pltpu.with_memory_space_constraint(a, pltpu.HBM) for a in arrays])

    def add(me_ref, *refs):
        del me_ref
        own, got, outs = refs[:n], refs[n:2 * n], refs[2 * n:]
        for j in range(n):
            mine = own[j][0] if modes[j] == "a2a" else own[j][...]
            outs[j][0] = (mine.astype(F32) + got[j][0].astype(F32)).astype(outs[j].dtype)

    in_specs, slot_specs = [], []
    for (r, w), m, cnt in zip(shapes, modes, counts):
        if m == "a2a":
            in_specs.append(pl.BlockSpec(
                (1, r, w), lambda q, me_ref, cnt=cnt: (2 * jnp.minimum(q, cnt - 1) + me_ref[0] % 2, 0, 0)))
        else:
            in_specs.append(pl.BlockSpec(
                (r, w), lambda q, me_ref, cnt=cnt: (0, 2 * jnp.minimum(q, cnt - 1) + me_ref[0] % 2)))
        slot_specs.append(pl.BlockSpec((1, r, w), lambda q, me_ref, cnt=cnt: (jnp.minimum(q, cnt - 1), 0, 0)))
    return _call(
        add, name=name + "_add", out_shape=staged,
        grid_spec=pltpu.PrefetchScalarGridSpec(num_scalar_prefetch=1, grid=(max(counts),),
                                               in_specs=in_specs + slot_specs, out_specs=slot_specs),
        compiler_params=_params("arbitrary"),
    )(me, *arrays, *stage)


def _chips_sems(n):
    return [pltpu.SemaphoreType.DMA((n, 6)), pltpu.SemaphoreType.DMA((n, 6)), pltpu.SemaphoreType.DMA((n,))]


def _chips_stage_shapes(chip_sums):
    return [jax.ShapeDtypeStruct((2, a.shape[1] // 2, a.shape[2]), a.dtype) for a in chip_sums]


def _chips_ops(ins, outs, stages, send_sems, recv_sems, local_sems, first_chips=None):
    x, y, c = lax.axis_index("x"), lax.axis_index("y"), lax.axis_index("c")
    qm = 2 * x + y
    first_chips = first_chips or [0] * len(ins)

    def owns(j, chip):
        lo, cnt = first_chips[j], ins[j].shape[0]
        if lo == 0 and cnt == 4:
            return None
        return jnp.logical_and(chip >= lo, chip < lo + cnt)

    def guarded(cond, fn):
        if cond is None:
            fn()
        else:
            pl.when(cond)(fn)

    def slot(j, chip):
        return jnp.clip(chip - first_chips[j], 0, ins[j].shape[0] - 1)

    def half(j, i):
        h = ins[j].shape[1] // 2
        return pl.ds(i * h, h)

    def copy(j, sem, src, dst, k):
        return pltpu.make_async_remote_copy(
            src_ref=src, dst_ref=dst, send_sem=send_sems.at[j, sem], recv_sem=recv_sems.at[j, sem],
            device_id=(x ^ (k >> 1), y ^ (k & 1), c), device_id_type=MESH)

    def direct(j, k):
        return copy(j, k - 1, ins[j].at[slot(j, qm ^ k)], outs[j].at[qm], k)

    def first_hop(j, k):
        return copy(j, 1 + k, ins[j].at[slot(j, qm ^ 3), half(j, k - 1)], stages[j].at[k - 1], k)

    def second_hop(j, k):
        return copy(j, 3 + k, stages[j].at[2 - k], outs[j].at[qm ^ (3 - k), half(j, 2 - k)], k)

    def local(j):
        return pltpu.make_async_copy(ins[j].at[slot(j, qm)], outs[j].at[qm], local_sems.at[j])

    def start():
        for j in range(len(ins)):
            for k in (1, 2):
                guarded(owns(j, qm ^ 3), lambda j=j, k=k: first_hop(j, k).start())
        for j in range(len(ins)):
            for k in (1, 2):
                guarded(owns(j, qm ^ k), lambda j=j, k=k: direct(j, k).start())
            guarded(owns(j, qm), lambda j=j: local(j).start())

    def forward():
        for j in range(len(ins)):
            for k in (1, 2):
                def pass_on(j=j, k=k):
                    first_hop(j, 3 - k).wait_recv()
                    second_hop(j, k).start()
                guarded(owns(j, qm ^ k), pass_on)

    def finish():
        for j in range(len(ins)):
            for k in (1, 2):
                guarded(owns(j, qm ^ k), lambda j=j, k=k: direct(j, k).wait_send())
                guarded(owns(j, qm ^ k), lambda j=j, k=k: second_hop(j, k).wait_send())
                guarded(owns(j, qm ^ 3), lambda j=j, k=k: first_hop(j, k).wait_send())
                guarded(owns(j, qm), lambda j=j, k=k: direct(j, k).wait_recv())
                guarded(owns(j, qm), lambda j=j, k=k: second_hop(j, k).wait_recv())
            guarded(owns(j, qm), lambda j=j: local(j).wait())

    return start, forward, finish


def _front(c, c_ctx, ada_w, ada_b, w_in, w_out, me):
    nloc = ada_w.shape[1]

    def body(me_ref, c_ref, cc_ref, aw_ref, ab_ref, win_ref, wout_ref,
             wfull_ref, woutb_ref, modx_ref, modc_ref, call_ref,
             wb_s, part_s, parts_s, w_send, w_recv, w_local, s_send, s_recv):
        x, y, cidx = lax.axis_index("x"), lax.axis_index("y"), lax.axis_index("c")
        me = me_ref[0]
        wb_s[...] = win_ref[...].astype(BF16)
        woutb_ref[...] = wout_ref[...].astype(BF16)
        start, forward, finish = _gather2_ops([wb_s], [wfull_ref], ["agc"], w_send, w_recv, w_local)
        start()

        def small_gather(src, my_slot, stage):
            copies = []
            for k in range(1, N_DEV):
                peer = (x ^ (k >> 2), y ^ ((k >> 1) & 1), cidx ^ (k & 1))
                cp = pltpu.make_async_remote_copy(src_ref=src, dst_ref=my_slot, send_sem=s_send.at[stage, k - 1],
                                                  recv_sem=s_recv.at[stage, k - 1], device_id=peer,
                                                  device_id_type=MESH)
                cp.start()
                copies.append(cp)
            pltpu.sync_copy(src, my_slot)
            for cp in copies:
                cp.wait()

        small_gather(c_ref, call_ref.at[pl.ds(me, 1), :], 0)
        off = pl.multiple_of(me * nloc, 128)
        b = ab_ref[:, pl.ds(off, nloc)]
        w = aw_ref[...]
        sx, _ = _silu_and_grad(call_ref[...])
        sc, _ = _silu_and_grad(jnp.broadcast_to(cc_ref[...], (8, D)))
        part_s[0:8, :] = _dot(sx, w) + b
        part_s[8:16, :] = _dot(sc, w) + b
        small_gather(part_s, parts_s.at[me], 1)
        mine = _rows((16, nloc)) == me
        for j in range(N_DEV):
            pj = parts_s[j]
            modx_ref[:, j * nloc:(j + 1) * nloc] = jnp.sum(jnp.where(mine, pj, 0.0), axis=0, keepdims=True)
            modc_ref[:, j * nloc:(j + 1) * nloc] = pj[8:9, :]
        forward()
        finish()

    return _call(
        body, name="front",
        out_shape=[jax.ShapeDtypeStruct((D, D_IN), BF16), jax.ShapeDtypeStruct(w_out.shape, BF16),
                   jax.ShapeDtypeStruct((1, 3 * D), F32), jax.ShapeDtypeStruct((1, 3 * D), F32),
                   jax.ShapeDtypeStruct((N_DEV, D), F32)],
        in_specs=[pl.BlockSpec(memory_space=pltpu.SMEM)] + [VMEM] * 6, out_specs=[HBM, VMEM, VMEM, VMEM, VMEM],
        scratch_shapes=[pltpu.VMEM(w_in.shape, BF16), pltpu.VMEM((16, nloc), F32),
                        pltpu.VMEM((N_DEV, 16, nloc), F32)] + _gather2_sems(1) +
                       [pltpu.SemaphoreType.DMA((2, N_DEV - 1)), pltpu.SemaphoreType.DMA((2, N_DEV - 1))],
        compiler_params=pltpu.CompilerParams(vmem_limit_bytes=VMEM_LIMIT, has_side_effects=True),
    )(me, c, c_ctx, ada_w, ada_b, w_in, w_out)


ARRIVAL = (0, 1, 2, 4, 3, 5, 6, 7)


def _front_project(xr, c, c_ctx, ada_w, ada_b, ng, w_in, w_out, cw, lam, me):
    nloc = ada_w.shape[1]
    ws = W_IN_SHARD
    order = me[0] ^ jnp.asarray(ARRIVAL, jnp.int32)

    def body(ord_ref, x_ref, c_ref, cc_ref, aw_ref, ab_ref, ng_ref, win_ref, wout_ref, cw_ref, lam_ref,
             z_ref, hn_ref, wfull_ref, woutb_ref, modx_ref, modc_ref, call_ref, cwf_ref, lamf_ref,
             wv, call_s, part_s, parts_s, w_send, w_recv, hbm_sems, s_send, s_recv, g_send, g_recv, g_local):
        t = pl.program_id(0)
        x, y, cidx = lax.axis_index("x"), lax.axis_index("y"), lax.axis_index("c")
        me_i = ord_ref[0]
        sibling = (x, y, 1 - cidx)
        chips = [(x ^ (k >> 1), y ^ (k & 1)) for k in (1, 2, 3)]
        g_start, g_pass, g_finish = _gather2_ops([cw_ref, lam_ref], [cwf_ref, lamf_ref], ["agc", "agc"],
                                                 g_send, g_recv, g_local)

        def shard_copy(k, px, py, pc, to):
            slot = wv.at[4 * px + 2 * py + pc]
            return pltpu.make_async_remote_copy(src_ref=slot, dst_ref=slot, send_sem=w_send.at[k],
                                                recv_sem=w_recv.at[k], device_id=to, device_id_type=MESH)

        def small_gather(src, my_slot, stage):
            copies = []
            for k in range(1, N_DEV):
                peer = (x ^ (k >> 2), y ^ ((k >> 1) & 1), cidx ^ (k & 1))
                cp = pltpu.make_async_remote_copy(src_ref=src, dst_ref=my_slot, send_sem=s_send.at[stage, k - 1],
                                                  recv_sem=s_recv.at[stage, k - 1], device_id=peer,
                                                  device_id_type=MESH)
                cp.start()
                copies.append(cp)
            pltpu.sync_copy(src, my_slot)
            for cp in copies:
                cp.wait()

        @pl.when(t == 0)
        def _():
            g_start()
            wv[me_i] = win_ref[...].astype(BF16)
            woutb_ref[...] = wout_ref[...].astype(BF16)
            shard_copy(0, x, y, cidx, sibling).start()
            for i, chip in enumerate(chips):
                shard_copy(1 + i, x, y, cidx, (*chip, cidx)).start()
            small_gather(c_ref, call_s.at[pl.ds(me_i, 1), :], 0)
            call_ref[...] = call_s[...]
            off = pl.multiple_of(me_i * nloc, 128)
            b = ab_ref[:, pl.ds(off, nloc)]
            w = aw_ref[...]
            sx, _ = _silu_and_grad(call_s[...])
            sc, _ = _silu_and_grad(jnp.broadcast_to(cc_ref[...], (8, D)))
            part_s[0:8, :] = _dot(sx, w) + b
            part_s[8:16, :] = _dot(sc, w) + b
            small_gather(part_s, parts_s.at[me_i], 1)
            mine = _rows((16, nloc)) == me_i
            for j in range(N_DEV):
                pj = parts_s[j]
                modx_ref[:, j * nloc:(j + 1) * nloc] = jnp.sum(jnp.where(mine, pj, 0.0), axis=0, keepdims=True)
                modc_ref[:, j * nloc:(j + 1) * nloc] = pj[8:9, :]
            shift, scale1, ngv = modx_ref[:, 0:D], 1.0 + modx_ref[:, D:2 * D], ng_ref[...]
            for r in range(L // ROWS):
                rsl = slice(r * ROWS, (r + 1) * ROWS)
                xv = x_ref[rsl, :]
                rs = lax.rsqrt(jnp.mean(xv * xv, axis=-1, keepdims=True) + NORM_EPS)
                hn_ref[rsl, :] = ((xv * rs * ngv) * scale1 + shift).astype(BF16)

        @pl.when(t == 1)
        def _():
            shard_copy(0, x, y, 1 - cidx, sibling).wait_recv()
            g_pass()

        for i, chip in enumerate(chips):
            @pl.when(t == ARRIVAL.index((2, 4, 6)[i]))
            def _(i=i, chip=chip):
                shard_copy(1 + i, *chip, cidx, sibling).wait_recv()
                shard_copy(4 + i, *chip, cidx, sibling).start()

            @pl.when(t == ARRIVAL.index((3, 5, 7)[i]))
            def _(i=i, chip=chip):
                shard_copy(4 + i, *chip, 1 - cidx, sibling).wait_recv()

        @pl.when(t == 2)
        def _():
            g_finish()

        dev = ord_ref[t]
        for r in range(L // (2 * ROWS)):
            rsl = slice(r * 2 * ROWS, (r + 1) * 2 * ROWS)
            z_ref[rsl, :] = _dot(hn_ref[rsl, :], wv[dev])
        col = pl.ds(pl.multiple_of(dev * ws, 128), ws)
        pltpu.make_async_copy(wv.at[dev], wfull_ref.at[:, col], hbm_sems.at[t]).start()

        @pl.when(t == N_DEV - 1)
        def _():
            for k in range(7):
                shard_copy(k, x, y, cidx, sibling).wait_send()
            for s in range(N_DEV):
                pltpu.make_async_copy(wv.at[0], wfull_ref.at[:, pl.ds(0, ws)], hbm_sems.at[s]).wait()

    const = lambda *shape: pl.BlockSpec(shape, lambda t, o: (0,) * len(shape))
    once = lambda *shape: pl.BlockSpec(shape, lambda t, o: (0,) * len(shape), pipeline_mode=pl.Buffered(1))
    return _call(
        body, name="front_project",
        out_shape=[jax.ShapeDtypeStruct((L, D_IN), F32), jax.ShapeDtypeStruct((L, D), BF16),
                   jax.ShapeDtypeStruct((D, D_IN), BF16), jax.ShapeDtypeStruct(w_out.shape, BF16),
                   jax.ShapeDtypeStruct((1, 3 * D), F32), jax.ShapeDtypeStruct((1, 3 * D), F32),
                   jax.ShapeDtypeStruct((N_DEV, D), F32), jax.ShapeDtypeStruct((CONV_W, D), F32),
                   jax.ShapeDtypeStruct((2, D), F32)],
        grid_spec=pltpu.PrefetchScalarGridSpec(
            num_scalar_prefetch=1, grid=(N_DEV,),
            in_specs=[once(L, D), const(1, D), const(1, D), once(D, nloc), const(1, 3 * D), const(1, D),
                      once(D, ws), once(*w_out.shape), HBM, HBM],
            out_specs=[pl.BlockSpec((L, ws), lambda t, o: (0, o[t])), const(L, D), HBM, const(*w_out.shape),
                       const(1, 3 * D), const(1, 3 * D), const(N_DEV, D), HBM, HBM],
            scratch_shapes=[pltpu.VMEM((N_DEV, D, ws), BF16), pltpu.VMEM((N_DEV, D), F32), pltpu.VMEM((16, nloc), F32),
                            pltpu.VMEM((N_DEV, 16, nloc), F32), pltpu.SemaphoreType.DMA((7,)),
                            pltpu.SemaphoreType.DMA((7,)), pltpu.SemaphoreType.DMA((N_DEV,)),
                            pltpu.SemaphoreType.DMA((2, N_DEV - 1)), pltpu.SemaphoreType.DMA((2, N_DEV - 1))]
            + _gather2_sems(2)),
        compiler_params=pltpu.CompilerParams(dimension_semantics=("arbitrary",), vmem_limit_bytes=VMEM_LIMIT,
                                             has_side_effects=True),
    )(order, xr, c, c_ctx, ada_w, ada_b, ng, w_in, w_out, pltpu.with_memory_space_constraint(cw, pltpu.HBM),
      pltpu.with_memory_space_constraint(lam, pltpu.HBM))


def _project(xr, mod, ng, w, ncols, tm, name, gather=None, gather_modes=()):
    rows = xr.shape[0]
    steps = rows // tm
    ng_ = len(gather or ())

    def body(x_ref, sh_ref, sc_ref, ng_ref, w_ref, *rest):
        z_ref, hn_ref = rest[ng_:ng_ + 2]
        if ng_:
            start, forward, finish = _gather2_ops(rest[:ng_], rest[ng_ + 2:2 * ng_ + 2], gather_modes,
                                                  *rest[2 * ng_ + 2:])
            pl.when(pl.program_id(0) == 0)(start)
            pl.when(pl.program_id(0) == steps // 2)(forward)
        x = x_ref[...]
        rs = lax.rsqrt(jnp.mean(x * x, axis=-1, keepdims=True) + NORM_EPS)
        hn = (x * rs * ng_ref[...]) * (1.0 + sc_ref[...]) + sh_ref[...]
        hb = hn.astype(BF16)
        hn_ref[...] = hb
        for n in range(ncols // D):
            z_ref[:, n * D:(n + 1) * D] = _dot(hb, w_ref[:, n * D:(n + 1) * D])
        if ng_:
            pl.when(pl.program_id(0) == steps - 1)(finish)

    vec = pl.BlockSpec((1, D), lambda i: (0, 0))
    gathered = _gather2_shapes(gather, gather_modes) if ng_ else []
    return _call(
        body, name=name, grid=(steps,),
        out_shape=[jax.ShapeDtypeStruct((rows, ncols), F32), jax.ShapeDtypeStruct((rows, D), BF16)] + gathered,
        in_specs=[pl.BlockSpec((tm, D), lambda i: (i, 0)), vec, pl.BlockSpec((1, D), lambda i: (0, 1)), vec,
                  pl.BlockSpec((D, ncols), lambda i: (0, 0), pipeline_mode=pl.Buffered(1))] + [HBM] * ng_,
        out_specs=[pl.BlockSpec((tm, ncols), lambda i: (i, 0)), pl.BlockSpec((tm, D), lambda i: (i, 0))] + [HBM] * ng_,
        scratch_shapes=_gather2_sems(ng_) if ng_ else [],
        compiler_params=pltpu.CompilerParams(dimension_semantics=("arbitrary",), vmem_limit_bytes=VMEM_LIMIT,
                                             has_side_effects=bool(ng_)),
    )(xr, mod, mod, ng, w, *[pltpu.with_memory_space_constraint(a, pltpu.HBM) for a in gather or ()])


def _scan_pair(af_ref, uf_ref, hf_ref, h0f, ab_ref, ub_ref, hb_ref, h0b, t_len):
    span = 8 * SCAN_BLOCKS
    nit = t_len // span
    rows = _rows((8, HD))

    def local_scan(a, b, forward):
        for s in (1, 2, 4):
            sh = s if forward else 8 - s
            m = rows >= s if forward else rows < 8 - s
            b = a * jnp.where(m, pltpu.roll(b, sh, 0), 0.0) + b
            a = a * jnp.where(m, pltpu.roll(a, sh, 0), 1.0)
        return a, b

    def span_scan(a_ref, u_ref, h_ref, off, carry, forward):
        order = range(SCAN_BLOCKS) if forward else range(SCAN_BLOCKS - 1, -1, -1)
        last = slice(7, 8) if forward else slice(0, 1)
        for q in order:
            rs = pl.ds(off + 8 * q, 8)
            a, b = local_scan(a_ref[rs, :], u_ref[rs, :], forward)
            h_ref[rs, :] = b + a * carry
            carry = a[last, :] * carry + b[last, :]
        return carry

    def body(k, carry):
        cf, cb = carry
        cf = span_scan(af_ref, uf_ref, hf_ref, pl.multiple_of(k * span, span), cf, True)
        cb = span_scan(ab_ref, ub_ref, hb_ref, pl.multiple_of((nit - 1 - k) * span, span), cb, False)
        return cf, cb

    return lax.fori_loop(0, nit, body, (h0f, h0b))


SCAN_BLOCKS = 4


def _conv(xa, cw, cb):
    z = jnp.zeros((1, HD), F32)
    xm1 = _shift_down(xa, z)
    xp1 = _shift_up(xa, z)
    xp2 = _shift_up(xp1, z)
    return xm1 * cw[0:1, :] + xa * cw[1:2, :] + xp1 * cw[2:3, :] + xp2 * cw[3:4, :] + cb


def _gates(xc, wa, wx, ba, bx, nsp):
    xb = xc.astype(BF16)
    r = _sigmoid(_dot(xb, wa) + ba)
    i = _sigmoid(_dot(xb, wx) + bx)
    log_a = r * nsp
    a = jnp.exp(log_a)
    g2 = jnp.tanh(log_a) * (-1.0 - a * a)
    rg = lax.rsqrt(jnp.maximum(g2, 1e-30))
    return r, i, a, g2 * rg, rg


def _lru_param_specs():
    h4 = pl.BlockSpec((2, 1, HD, HD), lambda h: (0, h, 0, 0))
    v2 = pl.BlockSpec((2, HD), lambda h: (0, h))
    return dict(
        xa=pl.BlockSpec((L, HD), lambda h: (0, h)), xac=pl.BlockSpec((LC, HD), lambda h: (0, h)),
        cw=pl.BlockSpec((CONV_W, HD), lambda h: (0, h)), cb=pl.BlockSpec((1, HD), lambda h: (0, h)), h4=h4, v2=v2)


def _lru_forward(zx, zc, cw, cb, wa, wx, ba, bx, lam, gather, gather_modes):
    ng_ = len(gather)

    def body(xa_ref, xac_ref, cw_ref, cb_ref, wa_ref, wx_ref, ba_ref, bx_ref, lam_ref, *rest):
        yl_ref = rest[ng_]
        af, uf, hf, ab, ub, hb = rest[2 * ng_ + 1:2 * ng_ + 7]
        start, pass_on, finish = _gather2_ops(rest[:ng_], rest[ng_ + 1:2 * ng_ + 1], gather_modes,
                                              *rest[2 * ng_ + 7:])
        pl.when(pl.program_id(0) == 0)(start)
        pl.when(pl.program_id(0) == HEADS // 2)(pass_on)
        pl.when(pl.program_id(0) == HEADS - 1)(finish)
        cwv, cbv = cw_ref[...], cb_ref[...]
        nsp = (-LRU_C) * _softplus(-lam_ref[...])

        def forward(xa, t_len, h0f, h0b):
            xc = _conv(xa, cwv, cbv)
            for d, (a_ref, u_ref) in enumerate(((af, uf), (ab, ub))):
                _, i, a, gamma, _ = _gates(xc, wa_ref[d, 0].astype(BF16), wx_ref[d, 0].astype(BF16),
                                           ba_ref[d:d + 1, :], bx_ref[d:d + 1, :], nsp[d:d + 1, :])
                a_ref[0:t_len, :] = a
                u_ref[0:t_len, :] = gamma * (i * xc)
            return _scan_pair(af, uf, hf, h0f, ab, ub, hb, h0b, t_len)

        z = jnp.zeros((1, HD), F32)
        h0f, h0b = forward(xac_ref[...], LC, z, z)
        forward(xa_ref[...], L, h0f, h0b)
        yl_ref[...] = hf[...] + hb[...]

    s = _lru_param_specs()
    return _call(
        body, name="lru_forward", grid=(HEADS,),
        out_shape=[jax.ShapeDtypeStruct((L, D), F32)] + _gather2_shapes(gather, gather_modes),
        in_specs=[s["xa"], s["xac"], s["cw"], s["cb"], s["h4"], s["h4"], s["v2"], s["v2"], s["v2"]] + [HBM] * ng_,
        out_specs=[pl.BlockSpec((L, HD), lambda h: (0, h))] + [HBM] * ng_,
        scratch_shapes=[pltpu.VMEM((L, HD), F32)] * 6 + _gather2_sems(ng_),
        compiler_params=pltpu.CompilerParams(dimension_semantics=("arbitrary",), vmem_limit_bytes=VMEM_LIMIT,
                                             has_side_effects=True),
    )(zx, zc, cw, cb, wa, wx, ba, bx, lam, *[pltpu.with_memory_space_constraint(a, pltpu.HBM) for a in gather])


def _lru_backward(zx, zc, dyl, dz, cw, cb, wa, wx, ba, bx, lam, chip_sums, first_chips=None):
    nr = len(chip_sums)

    def body(xa_ref, xac_ref, dyl_ref, dz_in, cw_ref, cb_ref, wa_ref, wx_ref, ba_ref, bx_ref, lam_ref, *rest):
        (dxa_ref, dxac_ref, dwa_ref, dwx_ref, dba_ref, dbx_ref, dlam_ref, dcw_ref,
         dcb_ref) = rest[nr:nr + 9]
        main_s, ctx_s = rest[3 * nr + 9:3 * nr + 11]
        if nr:
            start, forward, finish = _chips_ops(rest[:nr], rest[nr + 9:2 * nr + 9], rest[2 * nr + 9:3 * nr + 9],
                                                *rest[3 * nr + 11:], first_chips=first_chips)
            pl.when(pl.program_id(0) == 0)(start)
            pl.when(pl.program_id(0) == HEADS // 2)(forward)
            pl.when(pl.program_id(0) == HEADS - 1)(finish)
        del dz_in
        cwv, cbv = cw_ref[...], cb_ref[...]
        lamv = lam_ref[...]
        sp = _softplus(-lamv)
        nsp = (-LRU_C) * sp
        z = jnp.zeros((1, HD), F32)

        def wmat(ref, d):
            return ref[d, 0].astype(BF16)

        def workspace(s):
            return dict(a=(s.at[0], s.at[1]), u=(s.at[2], s.at[3]), h=(s.at[4], s.at[5]), rho=(s.at[6], s.at[7]),
                        saved=(tuple(s.at[8 + k] for k in range(4)), tuple(s.at[12 + k] for k in range(4))),
                        xc=s.at[16])

        def forward(ws, xa, t_len, h0f, h0b):
            xc = _conv(xa, cwv, cbv)
            ws["xc"][...] = xc
            for d in (0, 1):
                vals = _gates(xc, wmat(wa_ref, d), wmat(wx_ref, d), ba_ref[d:d + 1, :], bx_ref[d:d + 1, :],
                              nsp[d:d + 1, :])
                r, i, a, gamma, rg = vals
                ws["a"][d][...] = a
                ws["u"][d][...] = gamma * (i * xc)
                for ref, val in zip(ws["saved"][d], (r, i, gamma, rg)):
                    ref[...] = val
            return _scan_pair(ws["a"][0], ws["u"][0], ws["h"][0], h0f, ws["a"][1], ws["u"][1], ws["h"][1], h0b,
                              t_len)

        def backward(ws, xa, t_len, h0f, h0b, dhf, dhb, first):
            xc = ws["xc"][...]
            (af, ab), (uf, ub), (hf, hb), (rf, rb) = ws["a"], ws["u"], ws["h"], ws["rho"]
            uf[...] = ab[...] * dhb
            ub[...] = af[...] * dhf
            rho_b_last, rho_f_first = _scan_pair(ab, uf, rb, z, af, ub, rf, z, t_len)
            dxc = jnp.zeros((t_len, HD), F32)
            dsp = []
            for d in (0, 1):
                r, i, gamma, rg = (ref[...] for ref in ws["saved"][d])
                a = ws["a"][d][...]
                if d == 0:
                    lam_t = dhf + _shift_up(rf[...], z)
                    h_prev = _shift_down(hf[...], h0f)
                else:
                    lam_t = dhb + _shift_down(rb[...], z)
                    h_prev = _shift_up(hb[...], h0b)
                da = lam_t * h_prev
                lx = lam_t * xc
                d_i = lx * gamma
                d_gamma = lx * i
                dxc = dxc + lam_t * (gamma * i)
                d_log_a = a * (da - d_gamma * (a * rg))
                dsp.append(jnp.sum(d_log_a * r, axis=0, keepdims=True) * (-LRU_C))
                d_pre_r = d_log_a * nsp[d:d + 1, :] * (r * (1.0 - r))
                d_pre_i = d_i * (i * (1.0 - i))
                prb, pib, xb = d_pre_r.astype(BF16), d_pre_i.astype(BF16), xc.astype(BF16)
                dxc = dxc + _dot_nt(prb, wmat(wa_ref, d)) + _dot_nt(pib, wmat(wx_ref, d))
                g_wa, g_wx = _dot_tn(xb, prb), _dot_tn(xb, pib)
                g_ba = jnp.sum(d_pre_r, axis=0, keepdims=True)
                g_bx = jnp.sum(d_pre_i, axis=0, keepdims=True)
                if first:
                    dwa_ref[d, 0] = g_wa
                    dwx_ref[d, 0] = g_wx
                    dba_ref[d:d + 1, :] = g_ba
                    dbx_ref[d:d + 1, :] = g_bx
                else:
                    dwa_ref[d, 0] += g_wa
                    dwx_ref[d, 0] += g_wx
                    dba_ref[d:d + 1, :] += g_ba
                    dbx_ref[d:d + 1, :] += g_bx
            g_lam = jnp.concatenate(dsp, axis=0) * (-_sigmoid(-lamv))
            dm1 = _shift_down(dxc, z)
            dp1 = _shift_up(dxc, z)
            dm2 = _shift_down(dm1, z)
            dxa = dp1 * cwv[0:1, :] + dxc * cwv[1:2, :] + dm1 * cwv[2:3, :] + dm2 * cwv[3:4, :]
            xm1 = _shift_down(xa, z)
            xp1 = _shift_up(xa, z)
            xp2 = _shift_up(xp1, z)
            g_cw = jnp.concatenate([jnp.sum(dxc * v, axis=0, keepdims=True) for v in (xm1, xa, xp1, xp2)], axis=0)
            g_cb = jnp.sum(dxc, axis=0, keepdims=True)
            if first:
                dlam_ref[...] = g_lam
                dcw_ref[...] = g_cw
                dcb_ref[...] = g_cb
            else:
                dlam_ref[...] += g_lam
                dcw_ref[...] += g_cw
                dcb_ref[...] += g_cb
            return dxa, rho_f_first, rho_b_last

        ws_x, ws_c = workspace(main_s), workspace(ctx_s)
        h0f, h0b = forward(ws_c, xac_ref[...], LC, z, z)
        forward(ws_x, xa_ref[...], L, h0f, h0b)
        dh = dyl_ref[...]
        dxa, dh0f, dh0b = backward(ws_x, xa_ref[...], L, h0f, h0b, dh, dh, True)
        dxa_ref[...] = dxa.astype(BF16)
        rc = _rows((LC, HD))
        dxac, _, _ = backward(ws_c, xac_ref[...], LC, z, z, jnp.where(rc == LC - 1, dh0f, 0.0),
                              jnp.where(rc == 0, dh0b, 0.0), False)
        dxac_ref[...] = dxac.astype(BF16)

    s = _lru_param_specs()
    col = lambda r: pl.BlockSpec((r, HD), lambda h: (0, h))
    return _call(
        body, name="lru_backward", grid=(HEADS,),
        out_shape=[jax.ShapeDtypeStruct((L, D_IN), BF16), jax.ShapeDtypeStruct((LC, D), BF16),
                   jax.ShapeDtypeStruct((2, HEADS, HD, HD), F32), jax.ShapeDtypeStruct((2, HEADS, HD, HD), F32),
                   jax.ShapeDtypeStruct((2, D), F32), jax.ShapeDtypeStruct((2, D), F32),
                   jax.ShapeDtypeStruct((2, D), F32), jax.ShapeDtypeStruct((CONV_W, D), F32),
                   jax.ShapeDtypeStruct((1, D), F32)] + [jax.ShapeDtypeStruct((4,) + a.shape[1:], a.dtype)
                                                          for a in chip_sums] + _chips_stage_shapes(chip_sums),
        in_specs=[s["xa"], s["xac"], col(L), pl.BlockSpec(memory_space=pl.ANY), s["cw"], s["cb"], s["h4"], s["h4"],
                  s["v2"], s["v2"], s["v2"]] + [HBM] * nr,
        out_specs=[col(L), col(LC), s["h4"], s["h4"], s["v2"], s["v2"], s["v2"], col(CONV_W), col(1)]
        + [HBM] * (2 * nr),
        scratch_shapes=[pltpu.VMEM((17, L, HD), F32), pltpu.VMEM((17, LC, HD), F32)] + (_chips_sems(nr) if nr else []),
        input_output_aliases={3: 0},
        compiler_params=pltpu.CompilerParams(dimension_semantics=("arbitrary",), vmem_limit_bytes=VMEM_LIMIT,
                                             has_side_effects=True),
    )(zx, zc, dyl, dz, cw, cb, wa, wx, ba, bx, lam, *[pltpu.with_memory_space_constraint(a, pltpu.HBM)
                                                       for a in chip_sums])


def _mixer_loss(x, tgt, zx, yl, gx, fg, lng, lnb, ws, wst, bst, wout, tm):
    ncht = tm // CHUNK

    def body(x_ref, t_ref, ga_ref, u_ref, v_ref, gb_ref, yl_ref, gx_ref, fg_ref, lng_ref, lnb_ref, ws_ref, wst_ref,
             bst_ref, wout_ref,
             dz_ref, dyl_ref, dxn_ref, y_s, do_ref, dws_ref, dbst_ref, vec_ref,
             vn_s, mix_s, dm_s, dvn_s):
        step = pl.program_id(0)

        @pl.when(step == 0)
        def _():
            dws_ref[...] = jnp.zeros_like(dws_ref)
            dbst_ref[...] = jnp.zeros_like(dbst_ref)
            vec_ref[...] = jnp.zeros_like(vec_ref)

        u, v = u_ref[...], v_ref[...]
        ug, dug_du = _gelu_and_grad(u)
        vg, dvg_dv = _gelu_and_grad(v)
        mu = jnp.mean(vg, axis=-1, keepdims=True)
        vc = vg - mu
        rstd = lax.rsqrt(jnp.mean(vc * vc, axis=-1, keepdims=True) + LN_EPS)
        vhat = vc * rstd
        lngv = lng_ref[...]
        vn_s[...] = (vhat * lngv + lnb_ref[...]).astype(BF16)
        for ch in range(ncht):
            rs = slice(ch * CHUNK, (ch + 1) * CHUNK)
            for g in range(HEADS):
                cs = slice(g * HD, (g + 1) * HD)
                mix_s[rs, cs] = _dot(ws_ref[g], vn_s[rs, cs]) + bst_ref[:, g:g + 1]
        mixed = mix_s[...]
        ga, gb, yl = ga_ref[...], gb_ref[...], yl_ref[...]
        sga, dsga = _silu_and_grad(ga)
        sgb, dsgb = _silu_and_grad(gb)
        ys = ug * mixed
        y_s[:, 0:D] = (yl * sga).astype(BF16)
        y_s[:, D:D_MIX] = (ys * sgb).astype(BF16)
        o = _dot(y_s[...], wout_ref[...])
        gxv, fgv = gx_ref[...], fg_ref[...]
        xn = x_ref[...] + gxv * o
        rs2 = lax.rsqrt(jnp.mean(xn * xn, axis=-1, keepdims=True) + NORM_EPS)
        xh = xn * rs2
        diff = xh * fgv - t_ref[...]
        vec_ref[R_LOSS:R_LOSS + 1, :] += jnp.full((1, D), jnp.sum(diff * diff) * (0.5 / D), F32)
        dout = diff * (1.0 / D)
        w = dout * fgv
        dxn = rs2 * (w - xh * jnp.mean(w * xh, axis=-1, keepdims=True))
        dxn_ref[...] = dxn
        vec_ref[0:1, :] += jnp.sum(dxn * o, axis=0, keepdims=True)
        vec_ref[1:2, :] += jnp.sum(dout * xh, axis=0, keepdims=True)
        dob = (dxn * gxv).astype(BF16)
        do_ref[...] = dob
        dy = _dot_nt(dob, wout_ref[...])
        dya, dyb = dy[:, 0:D], dy[:, D:D_MIX]
        dyl_ref[...] = dya * sga
        dys = dyb * sgb
        dz_ref[:, 0:D] = jnp.zeros((tm, D), BF16)
        dz_ref[:, D:2 * D] = (dya * yl * dsga).astype(BF16)
        dz_ref[:, 2 * D:3 * D] = (dys * mixed * dug_du).astype(BF16)
        dz_ref[:, 4 * D:5 * D] = (dyb * ys * dsgb).astype(BF16)
        dm = dys * ug
        dm_s[...] = dm.astype(BF16)
        for g in range(HEADS):
            cs = slice(g * HD, (g + 1) * HD)
            dbst_ref[:, g:g + 1] += sum(jnp.sum(dm[ch * CHUNK:(ch + 1) * CHUNK, cs], axis=1, keepdims=True)
                                        for ch in range(ncht))
            for ch in range(ncht):
                rs = slice(ch * CHUNK, (ch + 1) * CHUNK)
                dws_ref[g] += _dot_nt(dm_s[rs, cs], vn_s[rs, cs])
                dvn_s[rs, cs] = _dot(wst_ref[g], dm_s[rs, cs])
        dvn = dvn_s[...]
        vec_ref[2:3, :] += jnp.sum(dvn * vhat, axis=0, keepdims=True)
        vec_ref[3:4, :] += jnp.sum(dvn, axis=0, keepdims=True)
        dvh = dvn * lngv
        dvg = rstd * (dvh - jnp.mean(dvh, axis=-1, keepdims=True) - vhat * jnp.mean(dvh * vhat, axis=-1, keepdims=True))
        dz_ref[:, 3 * D:4 * D] = (dvg * dvg_dv).astype(BF16)

    tile = pl.BlockSpec((tm, D), lambda i: (i, 0))
    zcol = lambda n: pl.BlockSpec((tm, D), lambda i: (i, n))
    vec = pl.BlockSpec((1, D), lambda i: (0, 0))
    full = lambda *s: pl.BlockSpec(s, lambda i: (0,) * len(s))
    return _call(
        body, name="mixer_loss", grid=(L // tm,),
        out_shape=[jax.ShapeDtypeStruct((L, D_IN), BF16), jax.ShapeDtypeStruct((L, D), F32),
                   jax.ShapeDtypeStruct((L, D), F32), jax.ShapeDtypeStruct((L, D_MIX), BF16),
                   jax.ShapeDtypeStruct((L, D), BF16),
                   jax.ShapeDtypeStruct((HEADS, CHUNK, CHUNK), F32), jax.ShapeDtypeStruct((CHUNK, HEADS), F32),
                   jax.ShapeDtypeStruct((8, D), F32)],
        in_specs=[tile, tile, zcol(1), zcol(2), zcol(3), zcol(4), tile, pl.BlockSpec((1, D), lambda i: (0, 2)),
                  vec, vec, vec,
                  full(HEADS, CHUNK, CHUNK), full(HEADS, CHUNK, CHUNK), full(CHUNK, HEADS),
                  pl.BlockSpec((D_MIX, D), lambda i: (0, 0), pipeline_mode=pl.Buffered(1))],
        out_specs=[pl.BlockSpec((tm, D_IN), lambda i: (i, 0)), tile, tile,
                   pl.BlockSpec((tm, D_MIX), lambda i: (i, 0)), tile,
                   full(HEADS, CHUNK, CHUNK), full(CHUNK, HEADS), full(8, D)],
        scratch_shapes=[pltpu.VMEM((tm, D), BF16), pltpu.VMEM((tm, D), F32),
                        pltpu.VMEM((tm, D), BF16), pltpu.VMEM((tm, D), F32)],
        compiler_params=_params("arbitrary"),
    )(x, tgt, zx, zx, zx, zx, yl, gx, fg, lng, lnb, ws, wst, bst, wout)


def _grad_w(a, b, a2, b2, tk, name, bw=D, first=0, nblocks=None):
    nk = a.shape[0] // tk
    m = a.shape[1]
    nblocks = nblocks or b.shape[1] // bw
    with_ctx = a2 is not None

    def body(*refs):
        if with_ctx:
            a_ref, b_ref, a2_ref, b2_ref, o_ref, acc = refs
        else:
            a_ref, b_ref, o_ref, acc = refs
        n, k = pl.program_id(0), pl.program_id(1)

        @pl.when(k == 0)
        def _():
            acc[...] = jnp.zeros_like(acc)

        acc[...] += _dot_tn(a_ref[...], b_ref[...])

        if with_ctx:
            @pl.when(jnp.logical_and(k == nk - 1, n == 0))
            def _():
                acc[:, 0:b2_ref.shape[1]] += _dot_tn(a2_ref[...], b2_ref[...])

        @pl.when(k == nk - 1)
        def _():
            o_ref[...] = acc[...].astype(BF16)

    in_specs = [pl.BlockSpec((tk, m), lambda n, k: (k, 0)), pl.BlockSpec((tk, bw), lambda n, k: (k, n + first))]
    args = [a, b]
    if with_ctx:
        in_specs += [pl.BlockSpec(a2.shape, lambda n, k: (0, 0)), pl.BlockSpec(b2.shape, lambda n, k: (0, 0))]
        args += [a2, b2]
    return _call(
        body, name=name, grid=(nblocks, nk), out_shape=jax.ShapeDtypeStruct((m, nblocks * bw), BF16),
        in_specs=in_specs, out_specs=pl.BlockSpec((m, bw), lambda n, k: (0, n)),
        scratch_shapes=[pltpu.VMEM((m, bw), F32)],
        compiler_params=_params("arbitrary", "arbitrary"),
    )(*args)


def _grad_rows(xr, dz, w, mod, ng, dres, ncols, tm, name, chip_sums=(), first_chips=None, dests=None):
    rows = xr.shape[0]
    steps = rows // tm
    with_dx = dres is not None
    nr = len(chip_sums)
    dests = [d for d in (dests or [None] * nr)]
    nd = sum(d is not None for d in dests)
    nin = 6 if with_dx else 5
    nout = 2 if with_dx else 1

    def body(*refs):
        if with_dx:
            x_ref, dz_ref, w_ref, sc_ref, ng_ref, dres_ref = refs[:nin]
            dx_ref, vec_ref = refs[nin + nr + nd:nin + nr + nd + nout]
        else:
            x_ref, dz_ref, w_ref, sc_ref, ng_ref = refs[:nin]
            (vec_ref,) = refs[nin + nr + nd:nin + nr + nd + nout]
        if nr:
            o0 = nin + nr + nd + nout
            start, forward, finish = _chips_ops(refs[nin:nin + nr], refs[o0:o0 + nr], refs[o0 + nr:o0 + 2 * nr],
                                                *refs[o0 + 2 * nr:], first_chips=first_chips)
            pl.when(pl.program_id(0) == 0)(start)
            pl.when(pl.program_id(0) == 2)(forward)
            pl.when(pl.program_id(0) == steps - 1)(finish)

        @pl.when(pl.program_id(0) == 0)
        def _():
            vec_ref[...] = jnp.zeros_like(vec_ref)

        dhn = _dot_nt(dz_ref[...], w_ref[...])
        x = x_ref[...]
        rs = lax.rsqrt(jnp.mean(x * x, axis=-1, keepdims=True) + NORM_EPS)
        xh = x * rs
        ngv = ng_ref[...]
        y = xh * ngv
        vec_ref[0:1, :] += jnp.sum(dhn, axis=0, keepdims=True)
        vec_ref[1:2, :] += jnp.sum(dhn * y, axis=0, keepdims=True)
        dy = dhn * (1.0 + sc_ref[...])
        vec_ref[2:3, :] += jnp.sum(dy * xh, axis=0, keepdims=True)
        if with_dx:
            dxh = dy * ngv
            dx_ref[...] = dres_ref[...] + rs * (dxh - xh * jnp.mean(dxh * xh, axis=-1, keepdims=True))

    tile = pl.BlockSpec((tm, D), lambda i: (i, 0))
    vec = pl.BlockSpec((1, D), lambda i: (0, 0))
    in_specs = [tile, pl.BlockSpec((tm, ncols), lambda i: (i, 0)),
                pl.BlockSpec((D, ncols), lambda i: (0, 0), pipeline_mode=pl.Buffered(1)),
                pl.BlockSpec((1, D), lambda i: (0, 1)), vec]
    out_shape = [jax.ShapeDtypeStruct((8, D), F32)]
    out_specs = [pl.BlockSpec((8, D), lambda i: (0, 0))]
    args = [xr, dz, w, mod, ng]
    if with_dx:
        in_specs.append(tile)
        out_shape.insert(0, jax.ShapeDtypeStruct((rows, D), F32))
        out_specs.insert(0, tile)
        args.append(dres)
    aliases = {}
    for j, d in enumerate(dests):
        if d is not None:
            aliases[len(args) + nr + len(aliases)] = len(out_shape) + j
    in_specs += [HBM] * (nr + nd)
    out_specs += [HBM] * (2 * nr)
    out_shape += [jax.ShapeDtypeStruct((4,) + a.shape[1:], a.dtype) for a in chip_sums]
    out_shape += _chips_stage_shapes(chip_sums)
    args += [pltpu.with_memory_space_constraint(a, pltpu.HBM) for a in chip_sums]
    args += [pltpu.with_memory_space_constraint(d, pltpu.HBM) for d in dests if d is not None]
    return _call(body, name=name, grid=(steps,), out_shape=out_shape, in_specs=in_specs, out_specs=out_specs,
                 scratch_shapes=_chips_sems(nr) if nr else [], input_output_aliases=aliases,
                 compiler_params=pltpu.CompilerParams(dimension_semantics=("arbitrary",),
                                                      vmem_limit_bytes=VMEM_LIMIT, has_side_effects=bool(nr)))(*args)


def _adamw(w, g, m, v):
    m = ADAM_B1 * m + (1.0 - ADAM_B1) * g
    v = ADAM_B2 * v + (1.0 - ADAM_B2) * (g * g)
    m_hat = m / (1.0 - ADAM_B1 ** ADAM_STEP)
    v_hat = v / (1.0 - ADAM_B2 ** ADAM_STEP)
    delta = -ADAM_LR * (m_hat / (jnp.sqrt(v_hat) + ADAM_EPS) + ADAM_WD * w)
    return delta, m, v


def _adamw_reduced(parts, w, m, v, tr, name):
    r, n = w.shape
    nparts = parts.shape[0]

    def body(p_ref, w_ref, m_ref, v_ref, g_ref, d_ref, mo_ref, vo_ref):
        g = p_ref[0].astype(F32)
        for i in range(1, nparts):
            g = g + p_ref[i].astype(F32)
        g_ref[...] = g
        d_ref[...], mo_ref[...], vo_ref[...] = _adamw(w_ref[...], g, m_ref[...], v_ref[...])

    tile = pl.BlockSpec((tr, n), lambda i: (i, 0))
    sds = jax.ShapeDtypeStruct((r, n), F32)
    return _call(
        body, name=name, grid=(r // tr,), out_shape=[sds] * 4,
        in_specs=[pl.BlockSpec((nparts, tr, n), lambda i: (0, i, 0)), tile, tile, tile], out_specs=[tile] * 4,
        compiler_params=_params("arbitrary"),
    )(parts, w, m, v)


R_GATE, R_FINAL_G, R_LN_G, R_LN_B, R_LOSS = 0, 1, 2, 3, 4
R_SH_X, R_SC_X, R_NG_X = 5, 6, 7
R_SH_C, R_SC_C, R_NG_C = 8, 9, 10
R_BA, R_BX, R_LAM, R_CW, R_CB, R_SGU_B = 11, 13, 15, 17, 21, 22
PACK_ROWS = 32
MAT_ROWS = 2 * (2 * HEADS * HD) + HEADS * CHUNK


def _reduce_small(vp_all, mat_parts, ada_w, me):
    nloc = ada_w.shape[1]

    def body(me_ref, vp_ref, mp_ref, w_ref, red_ref, mat_ref, dmod_ref, gab_ref, cpart_ref, dmc_s):
        red = vp_ref[0]
        for i in range(1, N_DEV):
            red = red + vp_ref[i]
        mat = mp_ref[0].astype(F32)
        for i in range(1, mp_ref.shape[0]):
            mat = mat + mp_ref[i].astype(F32)
        red_ref[...] = red
        mat_ref[...] = mat
        for e in range(N_DEV):
            dmod_ref[e:e + 1, 0:D] = vp_ref[e, R_SH_X:R_SH_X + 1, :]
            dmod_ref[e:e + 1, D:2 * D] = vp_ref[e, R_SC_X:R_SC_X + 1, :]
            dmod_ref[e:e + 1, 2 * D:3 * D] = vp_ref[e, R_GATE:R_GATE + 1, :]
        dmod_ref[8:9, 0:D] = red[R_SH_C:R_SH_C + 1, :]
        dmod_ref[8:9, D:2 * D] = red[R_SC_C:R_SC_C + 1, :]
        dmod_ref[8:9, 2 * D:3 * D] = jnp.zeros((1, D), F32)
        dmod_ref[9:16, :] = jnp.zeros((7, 3 * D), F32)
        gab_ref[:, 0:D] = red[R_SH_X:R_SH_X + 1, :] + red[R_SH_C:R_SH_C + 1, :]
        gab_ref[:, D:2 * D] = red[R_SC_X:R_SC_X + 1, :] + red[R_SC_C:R_SC_C + 1, :]
        gab_ref[:, 2 * D:3 * D] = red[R_GATE:R_GATE + 1, :]
        dmc_s[...] = jnp.broadcast_to(dmod_ref[8:9, :], (8, 3 * D))
        off = pl.multiple_of(me_ref[0] * nloc, 128)
        cpart_ref[...] = _dot_nt(dmc_s[:, pl.ds(off, nloc)], w_ref[...])

    return _call(
        body, name="reduce_small",
        out_shape=[jax.ShapeDtypeStruct((PACK_ROWS, D), F32), jax.ShapeDtypeStruct(mat_parts.shape[1:], F32),
                   jax.ShapeDtypeStruct((16, 3 * D), F32), jax.ShapeDtypeStruct((1, 3 * D), F32),
                   jax.ShapeDtypeStruct((8, D), F32)],
        in_specs=[pl.BlockSpec(memory_space=pltpu.SMEM), VMEM, VMEM, VMEM], out_specs=[VMEM] * 5,
        scratch_shapes=[pltpu.VMEM((8, 3 * D), F32)], compiler_params=_params(),
    )(me, vp_all, mat_parts, ada_w)


def _adamw_ada(c_all, c_ctx, dmod, w, m, v, me):
    nloc = w.shape[1]

    def body(me_ref, c_ref, cc_ref, dm_ref, w_ref, m_ref, v_ref, g_ref, d_ref, mo_ref, vo_ref):
        off = pl.multiple_of(me_ref[0] * nloc, 128)
        dm = dm_ref[:, pl.ds(off, nloc)]
        sx, _ = _silu_and_grad(c_ref[...])
        sc, _ = _silu_and_grad(cc_ref[...])
        g = _dot_tn(sx, dm[0:8, :]) + _dot_tn(jnp.broadcast_to(sc, (8, D)), dm[8:16, :])
        g_ref[...] = g
        d_ref[...], mo_ref[...], vo_ref[...] = _adamw(w_ref[...], g, m_ref[...], v_ref[...])

    sds = jax.ShapeDtypeStruct(w.shape, F32)
    return _call(
        body, name="adamw_ada_w", out_shape=[sds] * 4,
        in_specs=[pl.BlockSpec(memory_space=pltpu.SMEM)] + [VMEM] * 6, out_specs=[VMEM] * 4,
        compiler_params=_params(),
    )(me, c_all, c_ctx, dmod, w, m, v)


_SMALL = ("c_ctx", "ada_b", "norm_g", "conv_w", "conv_b", "lru_wa", "lru_ba", "lru_wx", "lru_bx", "lru_lambda",
          "sgu_ln_g", "sgu_ln_b", "sgu_w", "sgu_b", "final_g")


def _adamw_small(red, mat, cparts, gab, ws, ms, vs, me):
    n = len(_SMALL)
    nw = 2 * HEADS * HD

    def body(me_ref, red_ref, mat_ref, cp_ref, gab_ref, *refs):
        w_refs, m_refs, v_refs = refs[:n], refs[n:2 * n], refs[2 * n:3 * n]
        outs = refs[3 * n:]
        off = pl.multiple_of(me_ref[0] * HD, 128)

        def row(r, k=1):
            return red_ref[r:r + k, :]

        cc = w_refs[0][...]
        dcc = cp_ref[0, 0:1, :]
        for i in range(1, N_DEV):
            dcc = dcc + cp_ref[i, 0:1, :]
        grads = dict(
            c_ctx=dcc * _silu_and_grad(cc)[1], ada_b=gab_ref[...], norm_g=row(R_NG_X) + row(R_NG_C),
            conv_w=red_ref[R_CW:R_CW + CONV_W, pl.ds(off, HD)], conv_b=row(R_CB),
            lru_wa=mat_ref[0:nw, :], lru_ba=row(R_BA, 2), lru_wx=mat_ref[nw:2 * nw, :], lru_bx=row(R_BX, 2),
            lru_lambda=red_ref[R_LAM:R_LAM + 2, pl.ds(off, HD)], sgu_ln_g=row(R_LN_G), sgu_ln_b=row(R_LN_B),
            sgu_w=mat_ref[2 * nw:MAT_ROWS, :], sgu_b=row(R_SGU_B), final_g=row(R_FINAL_G))
        for j, name in enumerate(_SMALL):
            g = grads[name]
            outs[j][...] = g
            outs[n + j][...], outs[2 * n + j][...], outs[3 * n + j][...] = _adamw(w_refs[j][...], g, m_refs[j][...],
                                                                                 v_refs[j][...])

    sds = [jax.ShapeDtypeStruct(ws[k].shape, F32) for k in _SMALL]
    outs = _call(
        body, name="adamw_small", out_shape=sds * 4,
        in_specs=[pl.BlockSpec(memory_space=pltpu.SMEM)] + [VMEM] * (4 + 3 * n), out_specs=[VMEM] * (4 * n),
        compiler_params=_params(),
    )(me, red, mat, cparts, gab, *[ws[k] for k in _SMALL], *[ms[k] for k in _SMALL], *[vs[k] for k in _SMALL])
    return [dict(zip(_SMALL, outs[i * n:(i + 1) * n])) for i in range(4)]


def kernel(x, c, ctx, c_ctx, ada_w, ada_b, norm_g, w_in, conv_w, conv_b, lru_wa, lru_ba, lru_wx, lru_bx, lru_lambda, sgu_ln_g, sgu_ln_b, sgu_w, sgu_b, w_out, final_g, loss_target, m_c_ctx, m_ada_w, m_ada_b, m_norm_g, m_w_in, m_conv_w, m_conv_b, m_lru_wa, m_lru_ba, m_lru_wx, m_lru_bx, m_lru_lambda, m_sgu_ln_g, m_sgu_ln_b, m_sgu_w, m_sgu_b, m_w_out, m_final_g, v_c_ctx, v_ada_w, v_ada_b, v_norm_g, v_w_in, v_conv_w, v_conv_b, v_lru_wa, v_lru_ba, v_lru_wx, v_lru_bx, v_lru_lambda, v_sgu_ln_g, v_sgu_ln_b, v_sgu_w, v_sgu_b, v_w_out, v_final_g):
    args = dict(locals())
    me_s = 4 * lax.axis_index("x") + 2 * lax.axis_index("y") + lax.axis_index("c")
    me = me_s.astype(jnp.int32).reshape(1)
    xr, ctxr, tgt = x[0], ctx[0], loss_target[0]
    cc = c_ctx.reshape(1, D)
    nw = 2 * HEADS * HD
    view = dict(c_ctx=(1, D), ada_b=(1, 3 * D), norm_g=(1, D), conv_w=(CONV_W, HD), conv_b=(1, D), lru_wa=(nw, HD),
                lru_ba=(2, D), lru_wx=(nw, HD), lru_bx=(2, D), lru_lambda=(2, HD), sgu_ln_g=(1, D), sgu_ln_b=(1, D),
                sgu_w=(HEADS * CHUNK, CHUNK), sgu_b=(1, D), final_g=(1, D))

    zx, hn, w_full, w_out_b, modx, modc, c_all, cw_full, lam_full = _front_project(
        xr, c, cc, ada_w[0], ada_b, norm_g, w_in[0], w_out[0], conv_w[0], lru_lambda[0], me)
    zc, hnc = _project(ctxr, modc, norm_g, w_full, D, LC, "project_ctx")
    ba, bx = lru_ba.reshape(2, D), lru_bx.reshape(2, D)
    yl, wout_all = _lru_forward(zx, zc, cw_full, conv_b, lru_wa[0], lru_wx[0], ba, bx, lam_full, [w_out_b], ["ag"])
    wout_full = wout_all.reshape(D_MIX, D)
    ws_b = sgu_w[0].astype(BF16)
    dz, dyl, dxn, ycat, dob, dws, dbst, mvec = _mixer_loss(
        xr, tgt, zx, yl, modx, final_g.reshape(1, D), sgu_ln_g, sgu_ln_b, ws_b, jnp.swapaxes(ws_b, 1, 2),
        sgu_b[0].T, wout_full, 256)

    gw_out = _grad_w(ycat, dob, None, None, 1024, "grad_w_out")
    gw_rest = _grad_w(hn, dz, None, None, 1024, "grad_w_in_rest", bw=2 * W_IN_SHARD, first=1, nblocks=3)
    rest_sums, wout_sums = _reduce2_local([gw_rest, gw_out.reshape(N_DEV, D_MIX // N_DEV, D)], ["a2ac", "a2a"], me,
                                          "reduce_early", counts=[3, 4])
    dz, dxac, dwa, dwx, dba, dbx, dlam, dcw, dcb, win_parts, wout_parts, _, _ = _lru_backward(
        zx, zc, dyl, dz, cw_full, conv_b, lru_wa[0], lru_wx[0], ba, bx, lam_full, [rest_sums, wout_sums],
        first_chips=[1, 0])
    gw_first = _grad_w(hn, dz, hnc, dxac, 1024, "grad_w_in_first", bw=2 * W_IN_SHARD, first=0, nblocks=1)
    matpack = jnp.concatenate([dwa.reshape(nw, HD), dwx.reshape(nw, HD), dws.reshape(HEADS * CHUNK, CHUNK)],
                              axis=0).astype(BF16)
    first_sums, mat_sums = _reduce2_local([gw_first, matpack.reshape(N_DEV, MAT_ROWS // N_DEV, HD)], ["a2ac", "a2a"],
                                          me, "reduce_late", counts=[1, 4])
    gx, xvec, win_parts, mat_parts, _, _ = _grad_rows(
        xr, dz, w_full, modx, norm_g, dxn, D_IN, 256, "grad_rows_x", chip_sums=[first_sums, mat_sums],
        first_chips=[0, 0], dests=[win_parts, None])
    (cvec,) = _grad_rows(ctxr, dxac, w_full, modc, norm_g, None, D, LC, "grad_rows_ctx")
    pack = jnp.concatenate([mvec[0:5], xvec[0:3], cvec[0:3], dba, dbx, dlam, dcw, dcb, dbst.T.reshape(1, D),
                            jnp.zeros((PACK_ROWS - R_SGU_B - 1, D), F32)], axis=0)
    (vp_all,) = _gather2([pack], ["ag"], "gather_pack")
    red, matpiece, dmod, gab, cpart = _reduce_small(vp_all, mat_parts, ada_w[0], me)
    mat_all, cparts = _gather2([matpiece, cpart], ["ag", "ag"], "gather_small")

    g_w_in, d_w_in, nm_w_in, nv_w_in = _adamw_reduced(win_parts, w_in[0], m_w_in[0], v_w_in[0], 256, "adamw_w_in")
    g_w_out, d_w_out, nm_w_out, nv_w_out = _adamw_reduced(wout_parts, w_out[0], m_w_out[0], v_w_out[0], 128,
                                                          "adamw_w_out")
    g_ada, d_ada, nm_ada, nv_ada = _adamw_ada(c_all, cc, dmod, ada_w[0], m_ada_w[0], v_ada_w[0], me)
    ws = {k: args[k].reshape(view[k]) for k in _SMALL}
    ms = {k: args["m_" + k].reshape(view[k]) for k in _SMALL}
    vs = {k: args["v_" + k].reshape(view[k]) for k in _SMALL}
    small = _adamw_small(red, mat_all.reshape(MAT_ROWS, HD), cparts, gab, ws, ms, vs, me)
    big = dict(w_in=(g_w_in, d_w_in, nm_w_in, nv_w_in), w_out=(g_w_out, d_w_out, nm_w_out, nv_w_out),
               ada_w=(g_ada, d_ada, nm_ada, nv_ada))

    loss = red[R_LOSS, 0]
    names = ("c_ctx", "ada_w", "ada_b", "norm_g", "w_in", "conv_w", "conv_b", "lru_wa", "lru_ba", "lru_wx", "lru_bx",
             "lru_lambda", "sgu_ln_g", "sgu_ln_b", "sgu_w", "sgu_b", "w_out", "final_g")
    outs = [loss, gx.reshape(x.shape)]
    for kind in range(4):
        for k in names:
            val = big[k][kind] if k in big else small[kind][k]
            outs.append(val.reshape(args[k].shape))
    return tuple(outs)
```

```python
import functools

import jax
import jax.numpy as jnp
from jax import lax
from jax.experimental import pallas as pl
from jax.experimental.pallas import tpu as pltpu

F32 = jnp.float32
BF16 = jnp.bfloat16

N_DEV = 8
D = 1024
L = 2048
LC = 256
HEADS = 8
HD = 128
CHUNK = 128
D_IN = 5 * D
W_IN_SHARD = D_IN // N_DEV
ROWS = 256
D_MIX = 2 * D
CONV_W = 4
LRU_C = 8.0
NORM_EPS = 1e-6
LN_EPS = 1e-5
ADAM_LR, ADAM_B1, ADAM_B2, ADAM_EPS, ADAM_WD, ADAM_STEP = 0.001, 0.9, 0.999, 1e-08, 0.01, 10

VMEM_LIMIT = 56 * 1024 * 1024

HBM = pl.BlockSpec(memory_space=pltpu.HBM)
VMEM = pl.BlockSpec(memory_space=pltpu.VMEM)
MESH = pl.DeviceIdType.MESH


def _call(body, **kw):
    return pl.pallas_call(body, **kw)


def _params(*sem):
    return pltpu.CompilerParams(dimension_semantics=sem, vmem_limit_bytes=VMEM_LIMIT)


def _sigmoid(x):
    return 0.5 * jnp.tanh(0.5 * x) + 0.5


def _silu_and_grad(x):
    s = _sigmoid(x)
    return x * s, s * (1.0 + x * (1.0 - s))


_G0 = 0.7978845608028654
_G1 = 0.044715


def _gelu_and_grad(x):
    x2 = x * x
    t = jnp.tanh(_G0 * (x + _G1 * x * x2))
    cdf = 0.5 * (1.0 + t)
    return x * cdf, cdf + 0.5 * x * (1.0 - t * t) * (_G0 * (1.0 + 3.0 * _G1 * x2))


def _gelu(x):
    return 0.5 * x * (1.0 + jnp.tanh(_G0 * (x + _G1 * x * x * x)))


def _softplus(z):
    t = jnp.exp(-jnp.abs(z))
    u = 1.0 + t
    log1p = jnp.where(u == 1.0, t, jnp.log(u) * t / jnp.where(u == 1.0, 1.0, u - 1.0))
    return jnp.maximum(z, 0.0) + log1p


def _dot(a, b):
    return jnp.dot(a, b, preferred_element_type=F32)


def _dot_nt(a, b):
    return lax.dot_general(a, b, (((1,), (1,)), ((), ())), preferred_element_type=F32)


def _dot_tn(a, b):
    return lax.dot_general(a, b, (((0,), (0,)), ((), ())), preferred_element_type=F32)


def _rows(shape):
    return lax.broadcasted_iota(jnp.int32, shape, 0)


def _shift_down(x, first):
    return jnp.where(_rows(x.shape) == 0, first, pltpu.roll(x, 1, 0))


def _shift_up(x, last):
    n = x.shape[0]
    return jnp.where(_rows(x.shape) == n - 1, last, pltpu.roll(x, n - 1, 0))


def _gather2(arrays, modes, name):
    n = len(arrays)

    def body(*refs):
        start, forward, finish = _gather2_ops(refs[:n], refs[n:2 * n], modes, *refs[2 * n:])
        start()
        forward()
        finish()

    return _call(
        body, name=name, out_shape=_gather2_shapes(arrays, modes), in_specs=[HBM] * n, out_specs=[HBM] * n,
        scratch_shapes=_gather2_sems(n), compiler_params=pltpu.CompilerParams(has_side_effects=True),
    )(*[pltpu.with_memory_space_constraint(a, pltpu.HBM) for a in arrays])


def _gather2_shapes(arrays, modes):
    return [jax.ShapeDtypeStruct((N_DEV,) + a.shape if m == "ag" else (a.shape[0], N_DEV * a.shape[1]), a.dtype)
            for a, m in zip(arrays, modes)]


def _gather2_sems(n):
    return [pltpu.SemaphoreType.DMA((n, N_DEV - 1)), pltpu.SemaphoreType.DMA((n, N_DEV - 1)),
            pltpu.SemaphoreType.DMA((n,))]


def _gather2_ops(ins, outs, modes, send_sems, recv_sems, local_sems):
    n = len(ins)
    x, y, c = lax.axis_index("x"), lax.axis_index("y"), lax.axis_index("c")
    me, sibling = (x, y, c), (x, y, 1 - c)
    chips = [(x ^ (k >> 1), y ^ (k & 1)) for k in (1, 2, 3)]

    def slot(j, px, py, pc):
        dev = 4 * px + 2 * py + pc
        if modes[j] == "agc":
            w = ins[j].shape[1]
            return outs[j].at[:, pl.ds(pl.multiple_of(dev * w, 128), w)]
        return outs[j].at[dev]

    def copy(j, k, block, to, src=None):
        return pltpu.make_async_remote_copy(
            src_ref=slot(j, *block) if src is None else src, dst_ref=slot(j, *block),
            send_sem=send_sems.at[j, k], recv_sem=recv_sems.at[j, k], device_id=to, device_id_type=MESH)

    def own(j):
        return pltpu.make_async_copy(ins[j], slot(j, *me), local_sems.at[j])

    def first(j):
        return [copy(j, 0, me, sibling, src=ins[j])] + [copy(j, 1 + i, me, (*chip, c), src=ins[j])
                                                        for i, chip in enumerate(chips)]

    def passed(j, i):
        return copy(j, 4 + i, (*chips[i], c), sibling)

    def start():
        for j in range(n):
            own(j).start()
            for cp in first(j):
                cp.start()

    def forward():
        for i, chip in enumerate(chips):
            for j in range(n):
                copy(j, 1 + i, (*chip, c), me).wait_recv()
                passed(j, i).start()

    def finish():
        for j in range(n):
            copy(j, 0, sibling, me).wait_recv()
            for i, chip in enumerate(chips):
                copy(j, 4 + i, (*chip, 1 - c), me).wait_recv()
            for cp in first(j) + [passed(j, i) for i in range(3)]:
                cp.wait_send()
            own(j).wait()

    return start, forward, finish


def _reduce2_local(arrays, modes, me, name, counts=None):
    n = len(arrays)
    counts = counts or [4] * n
    shapes = [(a.shape[1], a.shape[2]) if m == "a2a" else (a.shape[0], a.shape[1] // (2 * cnt))
              for a, m, cnt in zip(arrays, modes, counts)]
    staged = [jax.ShapeDtypeStruct((cnt,) + s, a.dtype) for s, a, cnt in zip(shapes, arrays, counts)]

    def piece(ref, mode, dev, w):
        return ref.at[dev] if mode == "a2a" else ref.at[:, pl.ds(pl.multiple_of(dev * w, 128), w)]

    def to_sibling(*refs):
        ins, outs = refs[:n], refs[n:2 * n]
        send_sems, recv_sems = refs[2 * n:]
        x, y, c = lax.axis_index("x"), lax.axis_index("y"), lax.axis_index("c")
        copies = []
        for j in range(n):
            for q in range(counts[j]):
                cp = pltpu.make_async_remote_copy(
                    src_ref=piece(ins[j], modes[j], 2 * q + (1 - c), shapes[j][1]), dst_ref=outs[j].at[q],
                    send_sem=send_sems.at[j, q], recv_sem=recv_sems.at[j, q], device_id=(x, y, 1 - c),
                    device_id_type=MESH)
                cp.start()
                copies.append(cp)
        for cp in copies:
            cp.wait()

    stage = _call(
        to_sibling, name=name + "_d2d", out_shape=staged, in_specs=[HBM] * n, out_specs=[HBM] * n,
        scratch_shapes=[pltpu.SemaphoreType.DMA((n, 4)), pltpu.SemaphoreType.DMA((n, 4))],
        compiler_params=pltpu.CompilerParams(has_side_effects=True),
    )(*[pltpu.with_memory_space_constraint(a, pltpu.HBM) for a in arrays])

    def add(me_ref, *refs):
        del me_ref
        own, got, outs = refs[:n], refs[n:2 * n], refs[2 * n:]
        for j in range(n):
            mine = own[j][0] if modes[j] == "a2a" else own[j][...]
            outs[j][0] = (mine.astype(F32) + got[j][0].astype(F32)).astype(outs[j].dtype)

    in_specs, slot_specs = [], []
    for (r, w), m, cnt in zip(shapes, modes, counts):
        if m == "a2a":
            in_specs.append(pl.BlockSpec(
                (1, r, w), lambda q, me_ref, cnt=cnt: (2 * jnp.minimum(q, cnt - 1) + me_ref[0] % 2, 0, 0)))
        else:
            in_specs.append(pl.BlockSpec(
                (r, w), lambda q, me_ref, cnt=cnt: (0, 2 * jnp.minimum(q, cnt - 1) + me_ref[0] % 2)))
        slot_specs.append(pl.BlockSpec((1, r, w), lambda q, me_ref, cnt=cnt: (jnp.minimum(q, cnt - 1), 0, 0)))
    return _call(
        add, name=name + "_add", out_shape=staged,
        grid_spec=pltpu.PrefetchScalarGridSpec(num_scalar_prefetch=1, grid=(max(counts),),
                                               in_specs=in_specs + slot_specs, out_specs=slot_specs),
        compiler_params=_params("arbitrary"),
    )(me, *arrays, *stage)


def _chips_sems(n):
    return [pltpu.SemaphoreType.DMA((n, 6)), pltpu.SemaphoreType.DMA((n, 6)), pltpu.SemaphoreType.DMA((n,))]


def _chips_stage_shapes(chip_sums):
    return [jax.ShapeDtypeStruct((2, a.shape[1] // 2, a.shape[2]), a.dtype) for a in chip_sums]


def _chips_ops(ins, outs, stages, send_sems, recv_sems, local_sems, first_chips=None):
    x, y, c = lax.axis_index("x"), lax.axis_index("y"), lax.axis_index("c")
    qm = 2 * x + y
    first_chips = first_chips or [0] * len(ins)

    def owns(j, chip):
        lo, cnt = first_chips[j], ins[j].shape[0]
        if lo == 0 and cnt == 4:
            return None
        return jnp.logical_and(chip >= lo, chip < lo + cnt)

    def guarded(cond, fn):
        if cond is None:
            fn()
        else:
            pl.when(cond)(fn)

    def slot(j, chip):
        return jnp.clip(chip - first_chips[j], 0, ins[j].shape[0] - 1)

    def half(j, i):
        h = ins[j].shape[1] // 2
        return pl.ds(i * h, h)

    def copy(j, sem, src, dst, k):
        return pltpu.make_async_remote_copy(
            src_ref=src, dst_ref=dst, send_sem=send_sems.at[j, sem], recv_sem=recv_sems.at[j, sem],
            device_id=(x ^ (k >> 1), y ^ (k & 1), c), device_id_type=MESH)

    def direct(j, k):
        return copy(j, k - 1, ins[j].at[slot(j, qm ^ k)], outs[j].at[qm], k)

    def first_hop(j, k):
        return copy(j, 1 + k, ins[j].at[slot(j, qm ^ 3), half(j, k - 1)], stages[j].at[k - 1], k)

    def second_hop(j, k):
        return copy(j, 3 + k, stages[j].at[2 - k], outs[j].at[qm ^ (3 - k), half(j, 2 - k)], k)

    def local(j):
        return pltpu.make_async_copy(ins[j].at[slot(j, qm)], outs[j].at[qm], local_sems.at[j])

    def start():
        for j in range(len(ins)):
            for k in (1, 2):
                guarded(owns(j, qm ^ 3), lambda j=j, k=k: first_hop(j, k).start())
        for j in range(len(ins)):
            for k in (1, 2):
                guarded(owns(j, qm ^ k), lambda j=j, k=k: direct(j, k).start())
            guarded(owns(j, qm), lambda j=j: local(j).start())

    def forward():
        for j in range(len(ins)):
            for k in (1, 2):
                def pass_on(j=j, k=k):
                    first_hop(j, 3 - k).wait_recv()
                    second_hop(j, k).start()
                guarded(owns(j, qm ^ k), pass_on)

    def finish():
        for j in range(len(ins)):
            for k in (1, 2):
                guarded(owns(j, qm ^ k), lambda j=j, k=k: direct(j, k).wait_send())
                guarded(owns(j, qm ^ k), lambda j=j, k=k: second_hop(j, k).wait_send())
                guarded(owns(j, qm ^ 3), lambda j=j, k=k: first_hop(j, k).wait_send())
                guarded(owns(j, qm), lambda j=j, k=k: direct(j, k).wait_recv())
                guarded(owns(j, qm), lambda j=j, k=k: second_hop(j, k).wait_recv())
            guarded(owns(j, qm), lambda j=j: local(j).wait())

    return start, forward, finish


def _front(c, c_ctx, ada_w, ada_b, w_in, w_out, me):
    nloc = ada_w.shape[1]

    def body(me_ref, c_ref, cc_ref, aw_ref, ab_ref, win_ref, wout_ref,
             wfull_ref, woutb_ref, modx_ref, modc_ref, call_ref,
             wb_s, part_s, parts_s, w_send, w_recv, w_local, s_send, s_recv):
        x, y, cidx = lax.axis_index("x"), lax.axis_index("y"), lax.axis_index("c")
        me = me_ref[0]
        wb_s[...] = win_ref[...].astype(BF16)
        woutb_ref[...] = wout_ref[...].astype(BF16)
        start, forward, finish = _gather2_ops([wb_s], [wfull_ref], ["agc"], w_send, w_recv, w_local)
        start()

        def small_gather(src, my_slot, stage):
            copies = []
            for k in range(1, N_DEV):
                peer = (x ^ (k >> 2), y ^ ((k >> 1) & 1), cidx ^ (k & 1))
                cp = pltpu.make_async_remote_copy(src_ref=src, dst_ref=my_slot, send_sem=s_send.at[stage, k - 1],
                                                  recv_sem=s_recv.at[stage, k - 1], device_id=peer,
                                                  device_id_type=MESH)
                cp.start()
                copies.append(cp)
            pltpu.sync_copy(src, my_slot)
            for cp in copies:
                cp.wait()

        small_gather(c_ref, call_ref.at[pl.ds(me, 1), :], 0)
        off = pl.multiple_of(me * nloc, 128)
        b = ab_ref[:, pl.ds(off, nloc)]
        w = aw_ref[...]
        sx, _ = _silu_and_grad(call_ref[...])
        sc, _ = _silu_and_grad(jnp.broadcast_to(cc_ref[...], (8, D)))
        part_s[0:8, :] = _dot(sx, w) + b
        part_s[8:16, :] = _dot(sc, w) + b
        small_gather(part_s, parts_s.at[me], 1)
        mine = _rows((16, nloc)) == me
        for j in range(N_DEV):
            pj = parts_s[j]
            modx_ref[:, j * nloc:(j + 1) * nloc] = jnp.sum(jnp.where(mine, pj, 0.0), axis=0, keepdims=True)
            modc_ref[:, j * nloc:(j + 1) * nloc] = pj[8:9, :]
        forward()
        finish()

    return _call(
        body, name="front",
        out_shape=[jax.ShapeDtypeStruct((D, D_IN), BF16), jax.ShapeDtypeStruct(w_out.shape, BF16),
                   jax.ShapeDtypeStruct((1, 3 * D), F32), jax.ShapeDtypeStruct((1, 3 * D), F32),
                   jax.ShapeDtypeStruct((N_DEV, D), F32)],
        in_specs=[pl.BlockSpec(memory_space=pltpu.SMEM)] + [VMEM] * 6, out_specs=[HBM, VMEM, VMEM, VMEM, VMEM],
        scratch_shapes=[pltpu.VMEM(w_in.shape, BF16), pltpu.VMEM((16, nloc), F32),
                        pltpu.VMEM((N_DEV, 16, nloc), F32)] + _gather2_sems(1) +
                       [pltpu.SemaphoreType.DMA((2, N_DEV - 1)), pltpu.SemaphoreType.DMA((2, N_DEV - 1))],
        compiler_params=pltpu.CompilerParams(vmem_limit_bytes=VMEM_LIMIT, has_side_effects=True),
    )(me, c, c_ctx, ada_w, ada_b, w_in, w_out)


ARRIVAL = (0, 1, 2, 4, 3, 5, 6, 7)


def _front_project(xr, c, c_ctx, ada_w, ada_b, ng, w_in, w_out, cw, lam, me):
    nloc = ada_w.shape[1]
    ws = W_IN_SHARD
    order = me[0] ^ jnp.asarray(ARRIVAL, jnp.int32)

    def body(ord_ref, x_ref, c_ref, cc_ref, aw_ref, ab_ref, ng_ref, win_ref, wout_ref, cw_ref, lam_ref,
             z_ref, hn_ref, wfull_ref, woutb_ref, modx_ref, modc_ref, call_ref, cwf_ref, lamf_ref,
             wv, call_s, part_s, parts_s, w_send, w_recv, hbm_sems, s_send, s_recv, g_send, g_recv, g_local):
        t = pl.program_id(0)
        x, y, cidx = lax.axis_index("x"), lax.axis_index("y"), lax.axis_index("c")
        me_i = ord_ref[0]
        sibling = (x, y, 1 - cidx)
        chips = [(x ^ (k >> 1), y ^ (k & 1)) for k in (1, 2, 3)]
        g_start, g_pass, g_finish = _gather2_ops([cw_ref, lam_ref], [cwf_ref, lamf_ref], ["agc", "agc"],
                                                 g_send, g_recv, g_local)

        def shard_copy(k, px, py, pc, to):
            slot = wv.at[4 * px + 2 * py + pc]
            return pltpu.make_async_remote_copy(src_ref=slot, dst_ref=slot, send_sem=w_send.at[k],
                                                recv_sem=w_recv.at[k], device_id=to, device_id_type=MESH)

        def small_gather(src, my_slot, stage):
            copies = []
            for k in range(1, N_DEV):
                peer = (x ^ (k >> 2), y ^ ((k >> 1) & 1), cidx ^ (k & 1))
                cp = pltpu.make_async_remote_copy(src_ref=src, dst_ref=my_slot, send_sem=s_send.at[stage, k - 1],
                                                  recv_sem=s_recv.at[stage, k - 1], device_id=peer,
                                                  device_id_type=MESH)
                cp.start()
                copies.append(cp)
            pltpu.sync_copy(src, my_slot)
            for cp in copies:
                cp.wait()

        @pl.when(t == 0)
        def _():
            g_start()
            wv[me_i] = win_ref[...].astype(BF16)
            woutb_ref[...] = wout_ref[...].astype(BF16)
            shard_copy(0, x, y, cidx, sibling).start()
            small_gather(c_ref, call_s.at[pl.ds(me_i, 1), :], 0)
            call_ref[...] = call_s[...]
            off = pl.multiple_of(me_i * nloc, 128)
            b = ab_ref[:, pl.ds(off, nloc)]
            w = aw_ref[...]
            sx, _ = _silu_and_grad(call_s[...])
            sc, _ = _silu_and_grad(jnp.broadcast_to(cc_ref[...], (8, D)))
            part_s[0:8, :] = _dot(sx, w) + b
            part_s[8:16, :] = _dot(sc, w) + b
            small_gather(part_s, parts_s.at[me_i], 1)
            for i, chip in enumerate(chips):
                shard_copy(1 + i, x, y, cidx, (*chip, cidx)).start()
            mine = _rows((16, nloc)) == me_i
            for j in range(N_DEV):
                pj = parts_s[j]
                modx_ref[:, j * nloc:(j + 1) * nloc] = jnp.sum(jnp.where(mine, pj, 0.0), axis=0, keepdims=True)
                modc_ref[:, j * nloc:(j + 1) * nloc] = pj[8:9, :]
            shift, scale1, ngv = modx_ref[:, 0:D], 1.0 + modx_ref[:, D:2 * D], ng_ref[...]
            for r in range(L // ROWS):
                rsl = slice(r * ROWS, (r + 1) * ROWS)
                xv = x_ref[rsl, :]
                rs = lax.rsqrt(jnp.mean(xv * xv, axis=-1, keepdims=True) + NORM_EPS)
                hn_ref[rsl, :] = ((xv * rs * ngv) * scale1 + shift).astype(BF16)

        @pl.when(t == 1)
        def _():
            shard_copy(0, x, y, 1 - cidx, sibling).wait_recv()
            g_pass()

        for i, chip in enumerate(chips):
            @pl.when(t == ARRIVAL.index((2, 4, 6)[i]))
            def _(i=i, chip=chip):
                shard_copy(1 + i, *chip, cidx, sibling).wait_recv()
                shard_copy(4 + i, *chip, cidx, sibling).start()

            @pl.when(t == ARRIVAL.index((3, 5, 7)[i]))
            def _(i=i, chip=chip):
                shard_copy(4 + i, *chip, 1 - cidx, sibling).wait_recv()

        @pl.when(t == 2)
        def _():
            g_finish()

        dev = ord_ref[t]
        for r in range(L // (2 * ROWS)):
            rsl = slice(r * 2 * ROWS, (r + 1) * 2 * ROWS)
            z_ref[rsl, :] = _dot(hn_ref[rsl, :], wv[dev])
        col = pl.ds(pl.multiple_of(dev * ws, 128), ws)
        pltpu.make_async_copy(wv.at[dev], wfull_ref.at[:, col], hbm_sems.at[t]).start()

        @pl.when(t == N_DEV - 1)
        def _():
            for k in range(7):
                shard_copy(k, x, y, cidx, sibling).wait_send()
            for s in range(N_DEV):
                pltpu.make_async_copy(wv.at[0], wfull_ref.at[:, pl.ds(0, ws)], hbm_sems.at[s]).wait()

    const = lambda *shape: pl.BlockSpec(shape, lambda t, o: (0,) * len(shape))
    once = lambda *shape: pl.BlockSpec(shape, lambda t, o: (0,) * len(shape), pipeline_mode=pl.Buffered(1))
    return _call(
        body, name="front_project",
        out_shape=[jax.ShapeDtypeStruct((L, D_IN), F32), jax.ShapeDtypeStruct((L, D), BF16),
                   jax.ShapeDtypeStruct((D, D_IN), BF16), jax.ShapeDtypeStruct(w_out.shape, BF16),
                   jax.ShapeDtypeStruct((1, 3 * D), F32), jax.ShapeDtypeStruct((1, 3 * D), F32),
                   jax.ShapeDtypeStruct((N_DEV, D), F32), jax.ShapeDtypeStruct((CONV_W, D), F32),
                   jax.ShapeDtypeStruct((2, D), F32)],
        grid_spec=pltpu.PrefetchScalarGridSpec(
            num_scalar_prefetch=1, grid=(N_DEV,),
            in_specs=[once(L, D), const(1, D), const(1, D), once(D, nloc), const(1, 3 * D), const(1, D),
                      once(D, ws), once(*w_out.shape), HBM, HBM],
            out_specs=[pl.BlockSpec((L, ws), lambda t, o: (0, o[t])), const(L, D), HBM, const(*w_out.shape),
                       const(1, 3 * D), const(1, 3 * D), const(N_DEV, D), HBM, HBM],
            scratch_shapes=[pltpu.VMEM((N_DEV, D, ws), BF16), pltpu.VMEM((N_DEV, D), F32), pltpu.VMEM((16, nloc), F32),
                            pltpu.VMEM((N_DEV, 16, nloc), F32), pltpu.SemaphoreType.DMA((7,)),
                            pltpu.SemaphoreType.DMA((7,)), pltpu.SemaphoreType.DMA((N_DEV,)),
                            pltpu.SemaphoreType.DMA((2, N_DEV - 1)), pltpu.SemaphoreType.DMA((2, N_DEV - 1))]
            + _gather2_sems(2)),
        compiler_params=pltpu.CompilerParams(dimension_semantics=("arbitrary",), vmem_limit_bytes=VMEM_LIMIT,
                                             has_side_effects=True),
    )(order, xr, c, c_ctx, ada_w, ada_b, ng, w_in, w_out, pltpu.with_memory_space_constraint(cw, pltpu.HBM),
      pltpu.with_memory_space_constraint(lam, pltpu.HBM))


def _project(xr, mod, ng, w, ncols, tm, name, gather=None, gather_modes=()):
    rows = xr.shape[0]
    steps = rows // tm
    ng_ = len(gather or ())

    def body(x_ref, sh_ref, sc_ref, ng_ref, w_ref, *rest):
        z_ref, hn_ref = rest[ng_:ng_ + 2]
        if ng_:
            start, forward, finish = _gather2_ops(rest[:ng_], rest[ng_ + 2:2 * ng_ + 2], gather_modes,
                                                  *rest[2 * ng_ + 2:])
            pl.when(pl.program_id(0) == 0)(start)
            pl.when(pl.program_id(0) == steps // 2)(forward)
        x = x_ref[...]
        rs = lax.rsqrt(jnp.mean(x * x, axis=-1, keepdims=True) + NORM_EPS)
        hn = (x * rs * ng_ref[...]) * (1.0 + sc_ref[...]) + sh_ref[...]
        hb = hn.astype(BF16)
        hn_ref[...] = hb
        for n in range(ncols // D):
            z_ref[:, n * D:(n + 1) * D] = _dot(hb, w_ref[:, n * D:(n + 1) * D])
        if ng_:
            pl.when(pl.program_id(0) == steps - 1)(finish)

    vec = pl.BlockSpec((1, D), lambda i: (0, 0))
    gathered = _gather2_shapes(gather, gather_modes) if ng_ else []
    return _call(
        body, name=name, grid=(steps,),
        out_shape=[jax.ShapeDtypeStruct((rows, ncols), F32), jax.ShapeDtypeStruct((rows, D), BF16)] + gathered,
        in_specs=[pl.BlockSpec((tm, D), lambda i: (i, 0)), vec, pl.BlockSpec((1, D), lambda i: (0, 1)), vec,
                  pl.BlockSpec((D, ncols), lambda i: (0, 0), pipeline_mode=pl.Buffered(1))] + [HBM] * ng_,
        out_specs=[pl.BlockSpec((tm, ncols), lambda i: (i, 0)), pl.BlockSpec((tm, D), lambda i: (i, 0))] + [HBM] * ng_,
        scratch_shapes=_gather2_sems(ng_) if ng_ else [],
        compiler_params=pltpu.CompilerParams(dimension_semantics=("arbitrary",), vmem_limit_bytes=VMEM_LIMIT,
                                             has_side_effects=bool(ng_)),
    )(xr, mod, mod, ng, w, *[pltpu.with_memory_space_constraint(a, pltpu.HBM) for a in gather or ()])


def _scan_pair(af_ref, uf_ref, hf_ref, h0f, ab_ref, ub_ref, hb_ref, h0b, t_len):
    span = 8 * SCAN_BLOCKS
    nit = t_len // span
    rows = _rows((8, HD))

    def local_scan(a, b, forward):
        for s in (1, 2, 4):
            sh = s if forward else 8 - s
            m = rows >= s if forward else rows < 8 - s
            b = a * jnp.where(m, pltpu.roll(b, sh, 0), 0.0) + b
            a = a * jnp.where(m, pltpu.roll(a, sh, 0), 1.0)
        return a, b

    def span_scan(a_ref, u_ref, h_ref, off, carry, forward):
        order = range(SCAN_BLOCKS) if forward else range(SCAN_BLOCKS - 1, -1, -1)
        last = slice(7, 8) if forward else slice(0, 1)
        for q in order:
            rs = pl.ds(off + 8 * q, 8)
            a, b = local_scan(a_ref[rs, :], u_ref[rs, :], forward)
            h_ref[rs, :] = b + a * carry
            carry = a[last, :] * carry + b[last, :]
        return carry

    def body(k, carry):
        cf, cb = carry
        cf = span_scan(af_ref, uf_ref, hf_ref, pl.multiple_of(k * span, span), cf, True)
        cb = span_scan(ab_ref, ub_ref, hb_ref, pl.multiple_of((nit - 1 - k) * span, span), cb, False)
        return cf, cb

    return lax.fori_loop(0, nit, body, (h0f, h0b))


SCAN_BLOCKS = 4


def _conv(xa, cw, cb):
    z = jnp.zeros((1, HD), F32)
    xm1 = _shift_down(xa, z)
    xp1 = _shift_up(xa, z)
    xp2 = _shift_up(xp1, z)
    return xm1 * cw[0:1, :] + xa * cw[1:2, :] + xp1 * cw[2:3, :] + xp2 * cw[3:4, :] + cb


def _gates(xc, wa, wx, ba, bx, nsp):
    xb = xc.astype(BF16)
    r = _sigmoid(_dot(xb, wa) + ba)
    i = _sigmoid(_dot(xb, wx) + bx)
    log_a = r * nsp
    a = jnp.exp(log_a)
    g2 = jnp.tanh(log_a) * (-1.0 - a * a)
    rg = lax.rsqrt(jnp.maximum(g2, 1e-30))
    return r, i, a, g2 * rg, rg


def _lru_param_specs():
    h4 = pl.BlockSpec((2, 1, HD, HD), lambda h: (0, h, 0, 0))
    v2 = pl.BlockSpec((2, HD), lambda h: (0, h))
    return dict(
        xa=pl.BlockSpec((L, HD), lambda h: (0, h)), xac=pl.BlockSpec((LC, HD), lambda h: (0, h)),
        cw=pl.BlockSpec((CONV_W, HD), lambda h: (0, h)), cb=pl.BlockSpec((1, HD), lambda h: (0, h)), h4=h4, v2=v2)


def _lru_forward(zx, zc, cw, cb, wa, wx, ba, bx, lam, gather, gather_modes):
    ng_ = len(gather)

    def body(xa_ref, xac_ref, cw_ref, cb_ref, wa_ref, wx_ref, ba_ref, bx_ref, lam_ref, *rest):
        yl_ref = rest[ng_]
        af, uf, hf, ab, ub, hb = rest[2 * ng_ + 1:2 * ng_ + 7]
        start, pass_on, finish = _gather2_ops(rest[:ng_], rest[ng_ + 1:2 * ng_ + 1], gather_modes,
                                              *rest[2 * ng_ + 7:])
        pl.when(pl.program_id(0) == 0)(start)
        pl.when(pl.program_id(0) == HEADS // 2)(pass_on)
        pl.when(pl.program_id(0) == HEADS - 1)(finish)
        cwv, cbv = cw_ref[...], cb_ref[...]
        nsp = (-LRU_C) * _softplus(-lam_ref[...])

        def forward(xa, t_len, h0f, h0b):
            xc = _conv(xa, cwv, cbv)
            for d, (a_ref, u_ref) in enumerate(((af, uf), (ab, ub))):
                _, i, a, gamma, _ = _gates(xc, wa_ref[d, 0].astype(BF16), wx_ref[d, 0].astype(BF16),
                                           ba_ref[d:d + 1, :], bx_ref[d:d + 1, :], nsp[d:d + 1, :])
                a_ref[0:t_len, :] = a
                u_ref[0:t_len, :] = gamma * (i * xc)
            return _scan_pair(af, uf, hf, h0f, ab, ub, hb, h0b, t_len)

        z = jnp.zeros((1, HD), F32)
        h0f, h0b = forward(xac_ref[...], LC, z, z)
        forward(xa_ref[...], L, h0f, h0b)
        yl_ref[...] = hf[...] + hb[...]

    s = _lru_param_specs()
    return _call(
        body, name="lru_forward", grid=(HEADS,),
        out_shape=[jax.ShapeDtypeStruct((L, D), F32)] + _gather2_shapes(gather, gather_modes),
        in_specs=[s["xa"], s["xac"], s["cw"], s["cb"], s["h4"], s["h4"], s["v2"], s["v2"], s["v2"]] + [HBM] * ng_,
        out_specs=[pl.BlockSpec((L, HD), lambda h: (0, h))] + [HBM] * ng_,
        scratch_shapes=[pltpu.VMEM((L, HD), F32)] * 6 + _gather2_sems(ng_),
        compiler_params=pltpu.CompilerParams(dimension_semantics=("arbitrary",), vmem_limit_bytes=VMEM_LIMIT,
                                             has_side_effects=True),
    )(zx, zc, cw, cb, wa, wx, ba, bx, lam, *[pltpu.with_memory_space_constraint(a, pltpu.HBM) for a in gather])


def _lru_backward(zx, zc, dyl, dz, cw, cb, wa, wx, ba, bx, lam, chip_sums, first_chips=None):
    nr = len(chip_sums)

    def body(xa_ref, xac_ref, dyl_ref, dz_in, cw_ref, cb_ref, wa_ref, wx_ref, ba_ref, bx_ref, lam_ref, *rest):
        (dxa_ref, dxac_ref, dwa_ref, dwx_ref, dba_ref, dbx_ref, dlam_ref, dcw_ref,
         dcb_ref) = rest[nr:nr + 9]
        main_s, ctx_s = rest[3 * nr + 9:3 * nr + 11]
        if nr:
            start, forward, finish = _chips_ops(rest[:nr], rest[nr + 9:2 * nr + 9], rest[2 * nr + 9:3 * nr + 9],
                                                *rest[3 * nr + 11:], first_chips=first_chips)
            pl.when(pl.program_id(0) == 0)(start)
            pl.when(pl.program_id(0) == HEADS // 2)(forward)
            pl.when(pl.program_id(0) == HEADS - 1)(finish)
        del dz_in
        cwv, cbv = cw_ref[...], cb_ref[...]
        lamv = lam_ref[...]
        sp = _softplus(-lamv)
        nsp = (-LRU_C) * sp
        z = jnp.zeros((1, HD), F32)

        def wmat(ref, d):
            return ref[d, 0].astype(BF16)

        def workspace(s):
            return dict(a=(s.at[0], s.at[1]), u=(s.at[2], s.at[3]), h=(s.at[4], s.at[5]), rho=(s.at[6], s.at[7]),
                        saved=(tuple(s.at[8 + k] for k in range(4)), tuple(s.at[12 + k] for k in range(4))),
                        xc=s.at[16])

        def forward(ws, xa, t_len, h0f, h0b):
            xc = _conv(xa, cwv, cbv)
            ws["xc"][...] = xc
            for d in (0, 1):
                vals = _gates(xc, wmat(wa_ref, d), wmat(wx_ref, d), ba_ref[d:d + 1, :], bx_ref[d:d + 1, :],
                              nsp[d:d + 1, :])
                r, i, a, gamma, rg = vals
                ws["a"][d][...] = a
                ws["u"][d][...] = gamma * (i * xc)
                for ref, val in zip(ws["saved"][d], (r, i, gamma, rg)):
                    ref[...] = val
            return _scan_pair(ws["a"][0], ws["u"][0], ws["h"][0], h0f, ws["a"][1], ws["u"][1], ws["h"][1], h0b,
                              t_len)

        def backward(ws, xa, t_len, h0f, h0b, dhf, dhb, first):
            xc = ws["xc"][...]
            (af, ab), (uf, ub), (hf, hb), (rf, rb) = ws["a"], ws["u"], ws["h"], ws["rho"]
            uf[...] = ab[...] * dhb
            ub[...] = af[...] * dhf
            rho_b_last, rho_f_first = _scan_pair(ab, uf, rb, z, af, ub, rf, z, t_len)
            dxc = jnp.zeros((t_len, HD), F32)
            dsp = []
            for d in (0, 1):
                r, i, gamma, rg = (ref[...] for ref in ws["saved"][d])
                a = ws["a"][d][...]
                if d == 0:
                    lam_t = dhf + _shift_up(rf[...], z)
                    h_prev = _shift_down(hf[...], h0f)
                else:
                    lam_t = dhb + _shift_down(rb[...], z)
                    h_prev = _shift_up(hb[...], h0b)
                da = lam_t * h_prev
                lx = lam_t * xc
                d_i = lx * gamma
                d_gamma = lx * i
                dxc = dxc + lam_t * (gamma * i)
                d_log_a = a * (da - d_gamma * (a * rg))
                dsp.append(jnp.sum(d_log_a * r, axis=0, keepdims=True) * (-LRU_C))
                d_pre_r = d_log_a * nsp[d:d + 1, :] * (r * (1.0 - r))
                d_pre_i = d_i * (i * (1.0 - i))
                prb, pib, xb = d_pre_r.astype(BF16), d_pre_i.astype(BF16), xc.astype(BF16)
                dxc = dxc + _dot_nt(prb, wmat(wa_ref, d)) + _dot_nt(pib, wmat(wx_ref, d))
                g_wa, g_wx = _dot_tn(xb, prb), _dot_tn(xb, pib)
                g_ba = jnp.sum(d_pre_r, axis=0, keepdims=True)
                g_bx = jnp.sum(d_pre_i, axis=0, keepdims=True)
                if first:
                    dwa_ref[d, 0] = g_wa
                    dwx_ref[d, 0] = g_wx
                    dba_ref[d:d + 1, :] = g_ba
                    dbx_ref[d:d + 1, :] = g_bx
                else:
                    dwa_ref[d, 0] += g_wa
                    dwx_ref[d, 0] += g_wx
                    dba_ref[d:d + 1, :] += g_ba
                    dbx_ref[d:d + 1, :] += g_bx
            g_lam = jnp.concatenate(dsp, axis=0) * (-_sigmoid(-lamv))
            dm1 = _shift_down(dxc, z)
            dp1 = _shift_up(dxc, z)
            dm2 = _shift_down(dm1, z)
            dxa = dp1 * cwv[0:1, :] + dxc * cwv[1:2, :] + dm1 * cwv[2:3, :] + dm2 * cwv[3:4, :]
            xm1 = _shift_down(xa, z)
            xp1 = _shift_up(xa, z)
            xp2 = _shift_up(xp1, z)
            g_cw = jnp.concatenate([jnp.sum(dxc * v, axis=0, keepdims=True) for v in (xm1, xa, xp1, xp2)], axis=0)
            g_cb = jnp.sum(dxc, axis=0, keepdims=True)
            if first:
                dlam_ref[...] = g_lam
                dcw_ref[...] = g_cw
                dcb_ref[...] = g_cb
            else:
                dlam_ref[...] += g_lam
                dcw_ref[...] += g_cw
                dcb_ref[...] += g_cb
            return dxa, rho_f_first, rho_b_last

        ws_x, ws_c = workspace(main_s), workspace(ctx_s)
        h0f, h0b = forward(ws_c, xac_ref[...], LC, z, z)
        forward(ws_x, xa_ref[...], L, h0f, h0b)
        dh = dyl_ref[...]
        dxa, dh0f, dh0b = backward(ws_x, xa_ref[...], L, h0f, h0b, dh, dh, True)
        dxa_ref[...] = dxa.astype(BF16)
        rc = _rows((LC, HD))
        dxac, _, _ = backward(ws_c, xac_ref[...], LC, z, z, jnp.where(rc == LC - 1, dh0f, 0.0),
                              jnp.where(rc == 0, dh0b, 0.0), False)
        dxac_ref[...] = dxac.astype(BF16)

    s = _lru_param_specs()
    col = lambda r: pl.BlockSpec((r, HD), lambda h: (0, h))
    return _call(
        body, name="lru_backward", grid=(HEADS,),
        out_shape=[jax.ShapeDtypeStruct((L, D_IN), BF16), jax.ShapeDtypeStruct((LC, D), BF16),
                   jax.ShapeDtypeStruct((2, HEADS, HD, HD), F32), jax.ShapeDtypeStruct((2, HEADS, HD, HD), F32),
                   jax.ShapeDtypeStruct((2, D), F32), jax.ShapeDtypeStruct((2, D), F32),
                   jax.ShapeDtypeStruct((2, D), F32), jax.ShapeDtypeStruct((CONV_W, D), F32),
                   jax.ShapeDtypeStruct((1, D), F32)] + [jax.ShapeDtypeStruct((4,) + a.shape[1:], a.dtype)
                                                          for a in chip_sums] + _chips_stage_shapes(chip_sums),
        in_specs=[s["xa"], s["xac"], col(L), pl.BlockSpec(memory_space=pl.ANY), s["cw"], s["cb"], s["h4"], s["h4"],
                  s["v2"], s["v2"], s["v2"]] + [HBM] * nr,
        out_specs=[col(L), col(LC), s["h4"], s["h4"], s["v2"], s["v2"], s["v2"], col(CONV_W), col(1)]
        + [HBM] * (2 * nr),
        scratch_shapes=[pltpu.VMEM((17, L, HD), F32), pltpu.VMEM((17, LC, HD), F32)] + (_chips_sems(nr) if nr else []),
        input_output_aliases={3: 0},
        compiler_params=pltpu.CompilerParams(dimension_semantics=("arbitrary",), vmem_limit_bytes=VMEM_LIMIT,
                                             has_side_effects=True),
    )(zx, zc, dyl, dz, cw, cb, wa, wx, ba, bx, lam, *[pltpu.with_memory_space_constraint(a, pltpu.HBM)
                                                       for a in chip_sums])


def _mixer_loss(x, tgt, zx, yl, gx, fg, lng, lnb, ws, wst, bst, wout, tm):
    ncht = tm // CHUNK

    def body(x_ref, t_ref, ga_ref, u_ref, v_ref, gb_ref, yl_ref, gx_ref, fg_ref, lng_ref, lnb_ref, ws_ref, wst_ref,
             bst_ref, wout_ref,
             dz_ref, dyl_ref, dxn_ref, y_s, do_ref, dws_ref, dbst_ref, vec_ref,
             vn_s, mix_s, dm_s, dvn_s):
        step = pl.program_id(0)

        @pl.when(step == 0)
        def _():
            dws_ref[...] = jnp.zeros_like(dws_ref)
            dbst_ref[...] = jnp.zeros_like(dbst_ref)
            vec_ref[...] = jnp.zeros_like(vec_ref)

        u, v = u_ref[...], v_ref[...]
        ug, dug_du = _gelu_and_grad(u)
        vg, dvg_dv = _gelu_and_grad(v)
        mu = jnp.mean(vg, axis=-1, keepdims=True)
        vc = vg - mu
        rstd = lax.rsqrt(jnp.mean(vc * vc, axis=-1, keepdims=True) + LN_EPS)
        vhat = vc * rstd
        lngv = lng_ref[...]
        vn_s[...] = (vhat * lngv + lnb_ref[...]).astype(BF16)
        for ch in range(ncht):
            rs = slice(ch * CHUNK, (ch + 1) * CHUNK)
            for g in range(HEADS):
                cs = slice(g * HD, (g + 1) * HD)
                mix_s[rs, cs] = _dot(ws_ref[g], vn_s[rs, cs]) + bst_ref[:, g:g + 1]
        mixed = mix_s[...]
        ga, gb, yl = ga_ref[...], gb_ref[...], yl_ref[...]
        sga, dsga = _silu_and_grad(ga)
        sgb, dsgb = _silu_and_grad(gb)
        ys = ug * mixed
        y_s[:, 0:D] = (yl * sga).astype(BF16)
        y_s[:, D:D_MIX] = (ys * sgb).astype(BF16)
        o = _dot(y_s[...], wout_ref[...])
        gxv, fgv = gx_ref[...], fg_ref[...]
        xn = x_ref[...] + gxv * o
        rs2 = lax.rsqrt(jnp.mean(xn * xn, axis=-1, keepdims=True) + NORM_EPS)
        xh = xn * rs2
        diff = xh * fgv - t_ref[...]
        vec_ref[R_LOSS:R_LOSS + 1, :] += jnp.full((1, D), jnp.sum(diff * diff) * (0.5 / D), F32)
        dout = diff * (1.0 / D)
        w = dout * fgv
        dxn = rs2 * (w - xh * jnp.mean(w * xh, axis=-1, keepdims=True))
        dxn_ref[...] = dxn
        vec_ref[0:1, :] += jnp.sum(dxn * o, axis=0, keepdims=True)
        vec_ref[1:2, :] += jnp.sum(dout * xh, axis=0, keepdims=True)
        dob = (dxn * gxv).astype(BF16)
        do_ref[...] = dob
        dy = _dot_nt(dob, wout_ref[...])
        dya, dyb = dy[:, 0:D], dy[:, D:D_MIX]
        dyl_ref[...] = dya * sga
        dys = dyb * sgb
        dz_ref[:, 0:D] = jnp.zeros((tm, D), BF16)
        dz_ref[:, D:2 * D] = (dya * yl * dsga).astype(BF16)
        dz_ref[:, 2 * D:3 * D] = (dys * mixed * dug_du).astype(BF16)
        dz_ref[:, 4 * D:5 * D] = (dyb * ys * dsgb).astype(BF16)
        dm = dys * ug
        dm_s[...] = dm.astype(BF16)
        for g in range(HEADS):
            cs = slice(g * HD, (g + 1) * HD)
            dbst_ref[:, g:g + 1] += sum(jnp.sum(dm[ch * CHUNK:(ch + 1) * CHUNK, cs], axis=1, keepdims=True)
                                        for ch in range(ncht))
            for ch in range(ncht):
                rs = slice(ch * CHUNK, (ch + 1) * CHUNK)
                dws_ref[g] += _dot_nt(dm_s[rs, cs], vn_s[rs, cs])
                dvn_s[rs, cs] = _dot(wst_ref[g], dm_s[rs, cs])
        dvn = dvn_s[...]
        vec_ref[2:3, :] += jnp.sum(dvn * vhat, axis=0, keepdims=True)
        vec_ref[3:4, :] += jnp.sum(dvn, axis=0, keepdims=True)
        dvh = dvn * lngv
        dvg = rstd * (dvh - jnp.mean(dvh, axis=-1, keepdims=True) - vhat * jnp.mean(dvh * vhat, axis=-1, keepdims=True))
        dz_ref[:, 3 * D:4 * D] = (dvg * dvg_dv).astype(BF16)

    tile = pl.BlockSpec((tm, D), lambda i: (i, 0))
    zcol = lambda n: pl.BlockSpec((tm, D), lambda i: (i, n))
    vec = pl.BlockSpec((1, D), lambda i: (0, 0))
    full = lambda *s: pl.BlockSpec(s, lambda i: (0,) * len(s))
    return _call(
        body, name="mixer_loss", grid=(L // tm,),
        out_shape=[jax.ShapeDtypeStruct((L, D_IN), BF16), jax.ShapeDtypeStruct((L, D), F32),
                   jax.ShapeDtypeStruct((L, D), F32), jax.ShapeDtypeStruct((L, D_MIX), BF16),
                   jax.ShapeDtypeStruct((L, D), BF16),
                   jax.ShapeDtypeStruct((HEADS, CHUNK, CHUNK), F32), jax.ShapeDtypeStruct((CHUNK, HEADS), F32),
                   jax.ShapeDtypeStruct((8, D), F32)],
        in_specs=[tile, tile, zcol(1), zcol(2), zcol(3), zcol(4), tile, pl.BlockSpec((1, D), lambda i: (0, 2)),
                  vec, vec, vec,
                  full(HEADS, CHUNK, CHUNK), full(HEADS, CHUNK, CHUNK), full(CHUNK, HEADS),
                  pl.BlockSpec((D_MIX, D), lambda i: (0, 0), pipeline_mode=pl.Buffered(1))],
        out_specs=[pl.BlockSpec((tm, D_IN), lambda i: (i, 0)), tile, tile,
                   pl.BlockSpec((tm, D_MIX), lambda i: (i, 0)), tile,
                   full(HEADS, CHUNK, CHUNK), full(CHUNK, HEADS), full(8, D)],
        scratch_shapes=[pltpu.VMEM((tm, D), BF16), pltpu.VMEM((tm, D), F32),
                        pltpu.VMEM((tm, D), BF16), pltpu.VMEM((tm, D), F32)],
        compiler_params=_params("arbitrary"),
    )(x, tgt, zx, zx, zx, zx, yl, gx, fg, lng, lnb, ws, wst, bst, wout)


def _grad_w(a, b, a2, b2, tk, name, bw=D, first=0, nblocks=None):
    nk = a.shape[0] // tk
    m = a.shape[1]
    nblocks = nblocks or b.shape[1] // bw
    with_ctx = a2 is not None

    def body(*refs):
        if with_ctx:
            a_ref, b_ref, a2_ref, b2_ref, o_ref, acc = refs
        else:
            a_ref, b_ref, o_ref, acc = refs
        n, k = pl.program_id(0), pl.program_id(1)

        @pl.when(k == 0)
        def _():
            acc[...] = jnp.zeros_like(acc)

        acc[...] += _dot_tn(a_ref[...], b_ref[...])

        if with_ctx:
            @pl.when(jnp.logical_and(k == nk - 1, n == 0))
            def _():
                acc[:, 0:b2_ref.shape[1]] += _dot_tn(a2_ref[...], b2_ref[...])

        @pl.when(k == nk - 1)
        def _():
            o_ref[...] = acc[...].astype(BF16)

    in_specs = [pl.BlockSpec((tk, m), lambda n, k: (k, 0)), pl.BlockSpec((tk, bw), lambda n, k: (k, n + first))]
    args = [a, b]
    if with_ctx:
        in_specs += [pl.BlockSpec(a2.shape, lambda n, k: (0, 0)), pl.BlockSpec(b2.shape, lambda n, k: (0, 0))]
        args += [a2, b2]
    return _call(
        body, name=name, grid=(nblocks, nk), out_shape=jax.ShapeDtypeStruct((m, nblocks * bw), BF16),
        in_specs=in_specs, out_specs=pl.BlockSpec((m, bw), lambda n, k: (0, n)),
        scratch_shapes=[pltpu.VMEM((m, bw), F32)],
        compiler_params=_params("arbitrary", "arbitrary"),
    )(*args)


def _grad_rows(xr, dz, w, mod, ng, dres, ncols, tm, name, chip_sums=(), first_chips=None, dests=None):
    rows = xr.shape[0]
    steps = rows // tm
    with_dx = dres is not None
    nr = len(chip_sums)
    dests = [d for d in (dests or [None] * nr)]
    nd = sum(d is not None for d in dests)
    nin = 6 if with_dx else 5
    nout = 2 if with_dx else 1

    def body(*refs):
        if with_dx:
            x_ref, dz_ref, w_ref, sc_ref, ng_ref, dres_ref = refs[:nin]
            dx_ref, vec_ref = refs[nin + nr + nd:nin + nr + nd + nout]
        else:
            x_ref, dz_ref, w_ref, sc_ref, ng_ref = refs[:nin]
            (vec_ref,) = refs[nin + nr + nd:nin + nr + nd + nout]
        if nr:
            o0 = nin + nr + nd + nout
            start, forward, finish = _chips_ops(refs[nin:nin + nr], refs[o0:o0 + nr], refs[o0 + nr:o0 + 2 * nr],
                                                *refs[o0 + 2 * nr:], first_chips=first_chips)
            pl.when(pl.program_id(0) == 0)(start)
            pl.when(pl.program_id(0) == 2)(forward)
            pl.when(pl.program_id(0) == steps - 1)(finish)

        @pl.when(pl.program_id(0) == 0)
        def _():
            vec_ref[...] = jnp.zeros_like(vec_ref)

        dhn = _dot_nt(dz_ref[...], w_ref[...])
        x = x_ref[...]
        rs = lax.rsqrt(jnp.mean(x * x, axis=-1, keepdims=True) + NORM_EPS)
        xh = x * rs
        ngv = ng_ref[...]
        y = xh * ngv
        vec_ref[0:1, :] += jnp.sum(dhn, axis=0, keepdims=True)
        vec_ref[1:2, :] += jnp.sum(dhn * y, axis=0, keepdims=True)
        dy = dhn * (1.0 + sc_ref[...])
        vec_ref[2:3, :] += jnp.sum(dy * xh, axis=0, keepdims=True)
        if with_dx:
            dxh = dy * ngv
            dx_ref[...] = dres_ref[...] + rs * (dxh - xh * jnp.mean(dxh * xh, axis=-1, keepdims=True))

    tile = pl.BlockSpec((tm, D), lambda i: (i, 0))
    vec = pl.BlockSpec((1, D), lambda i: (0, 0))
    in_specs = [tile, pl.BlockSpec((tm, ncols), lambda i: (i, 0)),
                pl.BlockSpec((D, ncols), lambda i: (0, 0), pipeline_mode=pl.Buffered(1)),
                pl.BlockSpec((1, D), lambda i: (0, 1)), vec]
    out_shape = [jax.ShapeDtypeStruct((8, D), F32)]
    out_specs = [pl.BlockSpec((8, D), lambda i: (0, 0))]
    args = [xr, dz, w, mod, ng]
    if with_dx:
        in_specs.append(tile)
        out_shape.insert(0, jax.ShapeDtypeStruct((rows, D), F32))
        out_specs.insert(0, tile)
        args.append(dres)
    aliases = {}
    for j, d in enumerate(dests):
        if d is not None:
            aliases[len(args) + nr + len(aliases)] = len(out_shape) + j
    in_specs += [HBM] * (nr + nd)
    out_specs += [HBM] * (2 * nr)
    out_shape += [jax.ShapeDtypeStruct((4,) + a.shape[1:], a.dtype) for a in chip_sums]
    out_shape += _chips_stage_shapes(chip_sums)
    args += [pltpu.with_memory_space_constraint(a, pltpu.HBM) for a in chip_sums]
    args += [pltpu.with_memory_space_constraint(d, pltpu.HBM) for d in dests if d is not None]
    return _call(body, name=name, grid=(steps,), out_shape=out_shape, in_specs=in_specs, out_specs=out_specs,
                 scratch_shapes=_chips_sems(nr) if nr else [], input_output_aliases=aliases,
                 compiler_params=pltpu.CompilerParams(dimension_semantics=("arbitrary",),
                                                      vmem_limit_bytes=VMEM_LIMIT, has_side_effects=bool(nr)))(*args)


def _adamw(w, g, m, v):
    m = ADAM_B1 * m + (1.0 - ADAM_B1) * g
    v = ADAM_B2 * v + (1.0 - ADAM_B2) * (g * g)
    m_hat = m / (1.0 - ADAM_B1 ** ADAM_STEP)
    v_hat = v / (1.0 - ADAM_B2 ** ADAM_STEP)
    delta = -ADAM_LR * (m_hat / (jnp.sqrt(v_hat) + ADAM_EPS) + ADAM_WD * w)
    return delta, m, v


def _adamw_reduced(parts, w, m, v, tr, name):
    r, n = w.shape
    nparts = parts.shape[0]

    def body(p_ref, w_ref, m_ref, v_ref, g_ref, d_ref, mo_ref, vo_ref):
        g = p_ref[0].astype(F32)
        for i in range(1, nparts):
            g = g + p_ref[i].astype(F32)
        g_ref[...] = g
        d_ref[...], mo_ref[...], vo_ref[...] = _adamw(w_ref[...], g, m_ref[...], v_ref[...])

    tile = pl.BlockSpec((tr, n), lambda i: (i, 0))
    sds = jax.ShapeDtypeStruct((r, n), F32)
    return _call(
        body, name=name, grid=(r // tr,), out_shape=[sds] * 4,
        in_specs=[pl.BlockSpec((nparts, tr, n), lambda i: (0, i, 0)), tile, tile, tile], out_specs=[tile] * 4,
        compiler_params=_params("arbitrary"),
    )(parts, w, m, v)


R_GATE, R_FINAL_G, R_LN_G, R_LN_B, R_LOSS = 0, 1, 2, 3, 4
R_SH_X, R_SC_X, R_NG_X = 5, 6, 7
R_SH_C, R_SC_C, R_NG_C = 8, 9, 10
R_BA, R_BX, R_LAM, R_CW, R_CB, R_SGU_B = 11, 13, 15, 17, 21, 22
PACK_ROWS = 32
MAT_ROWS = 2 * (2 * HEADS * HD) + HEADS * CHUNK


def _reduce_small(vp_all, mat_parts, ada_w, me):
    nloc = ada_w.shape[1]

    def body(me_ref, vp_ref, mp_ref, w_ref, red_ref, mat_ref, dmod_ref, gab_ref, cpart_ref, dmc_s):
        red = vp_ref[0]
        for i in range(1, N_DEV):
            red = red + vp_ref[i]
        mat = mp_ref[0].astype(F32)
        for i in range(1, mp_ref.shape[0]):
            mat = mat + mp_ref[i].astype(F32)
        red_ref[...] = red
        mat_ref[...] = mat
        for e in range(N_DEV):
            dmod_ref[e:e + 1, 0:D] = vp_ref[e, R_SH_X:R_SH_X + 1, :]
            dmod_ref[e:e + 1, D:2 * D] = vp_ref[e, R_SC_X:R_SC_X + 1, :]
            dmod_ref[e:e + 1, 2 * D:3 * D] = vp_ref[e, R_GATE:R_GATE + 1, :]
        dmod_ref[8:9, 0:D] = red[R_SH_C:R_SH_C + 1, :]
        dmod_ref[8:9, D:2 * D] = red[R_SC_C:R_SC_C + 1, :]
        dmod_ref[8:9, 2 * D:3 * D] = jnp.zeros((1, D), F32)
        dmod_ref[9:16, :] = jnp.zeros((7, 3 * D), F32)
        gab_ref[:, 0:D] = red[R_SH_X:R_SH_X + 1, :] + red[R_SH_C:R_SH_C + 1, :]
        gab_ref[:, D:2 * D] = red[R_SC_X:R_SC_X + 1, :] + red[R_SC_C:R_SC_C + 1, :]
        gab_ref[:, 2 * D:3 * D] = red[R_GATE:R_GATE + 1, :]
        dmc_s[...] = jnp.broadcast_to(dmod_ref[8:9, :], (8, 3 * D))
        off = pl.multiple_of(me_ref[0] * nloc, 128)
        cpart_ref[...] = _dot_nt(dmc_s[:, pl.ds(off, nloc)], w_ref[...])

    return _call(
        body, name="reduce_small",
        out_shape=[jax.ShapeDtypeStruct((PACK_ROWS, D), F32), jax.ShapeDtypeStruct(mat_parts.shape[1:], F32),
                   jax.ShapeDtypeStruct((16, 3 * D), F32), jax.ShapeDtypeStruct((1, 3 * D), F32),
                   jax.ShapeDtypeStruct((8, D), F32)],
        in_specs=[pl.BlockSpec(memory_space=pltpu.SMEM), VMEM, VMEM, VMEM], out_specs=[VMEM] * 5,
        scratch_shapes=[pltpu.VMEM((8, 3 * D), F32)], compiler_params=_params(),
    )(me, vp_all, mat_parts, ada_w)


def _adamw_ada(c_all, c_ctx, dmod, w, m, v, me):
    nloc = w.shape[1]

    def body(me_ref, c_ref, cc_ref, dm_ref, w_ref, m_ref, v_ref, g_ref, d_ref, mo_ref, vo_ref):
        off = pl.multiple_of(me_ref[0] * nloc, 128)
        dm = dm_ref[:, pl.ds(off, nloc)]
        sx, _ = _silu_and_grad(c_ref[...])
        sc, _ = _silu_and_grad(cc_ref[...])
        g = _dot_tn(sx, dm[0:8, :]) + _dot_tn(jnp.broadcast_to(sc, (8, D)), dm[8:16, :])
        g_ref[...] = g
        d_ref[...], mo_ref[...], vo_ref[...] = _adamw(w_ref[...], g, m_ref[...], v_ref[...])

    sds = jax.ShapeDtypeStruct(w.shape, F32)
    return _call(
        body, name="adamw_ada_w", out_shape=[sds] * 4,
        in_specs=[pl.BlockSpec(memory_space=pltpu.SMEM)] + [VMEM] * 6, out_specs=[VMEM] * 4,
        compiler_params=_params(),
    )(me, c_all, c_ctx, dmod, w, m, v)


_SMALL = ("c_ctx", "ada_b", "norm_g", "conv_w", "conv_b", "lru_wa", "lru_ba", "lru_wx", "lru_bx", "lru_lambda",
          "sgu_ln_g", "sgu_ln_b", "sgu_w", "sgu_b", "final_g")


def _adamw_small(red, mat, cparts, gab, ws, ms, vs, me):
    n = len(_SMALL)
    nw = 2 * HEADS * HD

    def body(me_ref, red_ref, mat_ref, cp_ref, gab_ref, *refs):
        w_refs, m_refs, v_refs = refs[:n], refs[n:2 * n], refs[2 * n:3 * n]
        outs = refs[3 * n:]
        off = pl.multiple_of(me_ref[0] * HD, 128)

        def row(r, k=1):
            return red_ref[r:r + k, :]

        cc = w_refs[0][...]
        dcc = cp_ref[0, 0:1, :]
        for i in range(1, N_DEV):
            dcc = dcc + cp_ref[i, 0:1, :]
        grads = dict(
            c_ctx=dcc * _silu_and_grad(cc)[1], ada_b=gab_ref[...], norm_g=row(R_NG_X) + row(R_NG_C),
            conv_w=red_ref[R_CW:R_CW + CONV_W, pl.ds(off, HD)], conv_b=row(R_CB),
            lru_wa=mat_ref[0:nw, :], lru_ba=row(R_BA, 2), lru_wx=mat_ref[nw:2 * nw, :], lru_bx=row(R_BX, 2),
            lru_lambda=red_ref[R_LAM:R_LAM + 2, pl.ds(off, HD)], sgu_ln_g=row(R_LN_G), sgu_ln_b=row(R_LN_B),
            sgu_w=mat_ref[2 * nw:MAT_ROWS, :], sgu_b=row(R_SGU_B), final_g=row(R_FINAL_G))
        for j, name in enumerate(_SMALL):
            g = grads[name]
            outs[j][...] = g
            outs[n + j][...], outs[2 * n + j][...], outs[3 * n + j][...] = _adamw(w_refs[j][...], g, m_refs[j][...],
                                                                                 v_refs[j][...])

    sds = [jax.ShapeDtypeStruct(ws[k].shape, F32) for k in _SMALL]
    outs = _call(
        body, name="adamw_small", out_shape=sds * 4,
        in_specs=[pl.BlockSpec(memory_space=pltpu.SMEM)] + [VMEM] * (4 + 3 * n), out_specs=[VMEM] * (4 * n),
        compiler_params=_params(),
    )(me, red, mat, cparts, gab, *[ws[k] for k in _SMALL], *[ms[k] for k in _SMALL], *[vs[k] for k in _SMALL])
    return [dict(zip(_SMALL, outs[i * n:(i + 1) * n])) for i in range(4)]


def kernel(x, c, ctx, c_ctx, ada_w, ada_b, norm_g, w_in, conv_w, conv_b, lru_wa, lru_ba, lru_wx, lru_bx, lru_lambda, sgu_ln_g, sgu_ln_b, sgu_w, sgu_b, w_out, final_g, loss_target, m_c_ctx, m_ada_w, m_ada_b, m_norm_g, m_w_in, m_conv_w, m_conv_b, m_lru_wa, m_lru_ba, m_lru_wx, m_lru_bx, m_lru_lambda, m_sgu_ln_g, m_sgu_ln_b, m_sgu_w, m_sgu_b, m_w_out, m_final_g, v_c_ctx, v_ada_w, v_ada_b, v_norm_g, v_w_in, v_conv_w, v_conv_b, v_lru_wa, v_lru_ba, v_lru_wx, v_lru_bx, v_lru_lambda, v_sgu_ln_g, v_sgu_ln_b, v_sgu_w, v_sgu_b, v_w_out, v_final_g):
    args = dict(locals())
    me_s = 4 * lax.axis_index("x") + 2 * lax.axis_index("y") + lax.axis_index("c")
    me = me_s.astype(jnp.int32).reshape(1)
    xr, ctxr, tgt = x[0], ctx[0], loss_target[0]
    cc = c_ctx.reshape(1, D)
    nw = 2 * HEADS * HD
    view = dict(c_ctx=(1, D), ada_b=(1, 3 * D), norm_g=(1, D), conv_w=(CONV_W, HD), conv_b=(1, D), lru_wa=(nw, HD),
                lru_ba=(2, D), lru_wx=(nw, HD), lru_bx=(2, D), lru_lambda=(2, HD), sgu_ln_g=(1, D), sgu_ln_b=(1, D),
                sgu_w=(HEADS * CHUNK, CHUNK), sgu_b=(1, D), final_g=(1, D))

    zx, hn, w_full, w_out_b, modx, modc, c_all, cw_full, lam_full = _front_project(
        xr, c, cc, ada_w[0], ada_b, norm_g, w_in[0], w_out[0], conv_w[0], lru_lambda[0], me)
    zc, hnc = _project(ctxr, modc, norm_g, w_full, D, LC, "project_ctx")
    ba, bx = lru_ba.reshape(2, D), lru_bx.reshape(2, D)
    yl, wout_all = _lru_forward(zx, zc, cw_full, conv_b, lru_wa[0], lru_wx[0], ba, bx, lam_full, [w_out_b], ["ag"])
    wout_full = wout_all.reshape(D_MIX, D)
    ws_b = sgu_w[0].astype(BF16)
    dz, dyl, dxn, ycat, dob, dws, dbst, mvec = _mixer_loss(
        xr, tgt, zx, yl, modx, final_g.reshape(1, D), sgu_ln_g, sgu_ln_b, ws_b, jnp.swapaxes(ws_b, 1, 2),
        sgu_b[0].T, wout_full, 256)

    gw_out = _grad_w(ycat, dob, None, None, 1024, "grad_w_out")
    gw_rest = _grad_w(hn, dz, None, None, 1024, "grad_w_in_rest", bw=2 * W_IN_SHARD, first=1, nblocks=3)
    rest_sums, wout_sums = _reduce2_local([gw_rest, gw_out.reshape(N_DEV, D_MIX // N_DEV, D)], ["a2ac", "a2a"], me,
                                          "reduce_early", counts=[3, 4])
    dz, dxac, dwa, dwx, dba, dbx, dlam, dcw, dcb, win_parts, wout_parts, _, _ = _lru_backward(
        zx, zc, dyl, dz, cw_full, conv_b, lru_wa[0], lru_wx[0], ba, bx, lam_full, [rest_sums, wout_sums],
        first_chips=[1, 0])
    gw_first = _grad_w(hn, dz, hnc, dxac, 1024, "grad_w_in_first", bw=2 * W_IN_SHARD, first=0, nblocks=1)
    matpack = jnp.concatenate([dwa.reshape(nw, HD), dwx.reshape(nw, HD), dws.reshape(HEADS * CHUNK, CHUNK)],
                              axis=0).astype(BF16)
    first_sums, mat_sums = _reduce2_local([gw_first, matpack.reshape(N_DEV, MAT_ROWS // N_DEV, HD)], ["a2ac", "a2a"],
                                          me, "reduce_late", counts=[1, 4])
    gx, xvec, win_parts, mat_parts, _, _ = _grad_rows(
        xr, dz, w_full, modx, norm_g, dxn, D_IN, 256, "grad_rows_x", chip_sums=[first_sums, mat_sums],
        first_chips=[0, 0], dests=[win_parts, None])
    (cvec,) = _grad_rows(ctxr, dxac, w_full, modc, norm_g, None, D, LC, "grad_rows_ctx")
    pack = jnp.concatenate([mvec[0:5], xvec[0:3], cvec[0:3], dba, dbx, dlam, dcw, dcb, dbst.T.reshape(1, D),
                            jnp.zeros((PACK_ROWS - R_SGU_B - 1, D), F32)], axis=0)
    (vp_all,) = _gather2([pack], ["ag"], "gather_pack")
    red, matpiece, dmod, gab, cpart = _reduce_small(vp_all, mat_parts, ada_w[0], me)
    mat_all, cparts = _gather2([matpiece, cpart], ["ag", "ag"], "gather_small")

    g_w_in, d_w_in, nm_w_in, nv_w_in = _adamw_reduced(win_parts, w_in[0], m_w_in[0], v_w_in[0], 256, "adamw_w_in")
    g_w_out, d_w_out, nm_w_out, nv_w_out = _adamw_reduced(wout_parts, w_out[0], m_w_out[0], v_w_out[0], 128,
                                                          "adamw_w_out")
    g_ada, d_ada, nm_ada, nv_ada = _adamw_ada(c_all, cc, dmod, ada_w[0], m_ada_w[0], v_ada_w[0], me)
    ws = {k: args[k].reshape(view[k]) for k in _SMALL}
    ms = {k: args["m_" + k].reshape(view[k]) for k in _SMALL}
    vs = {k: args["v_" + k].reshape(view[k]) for k in _SMALL}
    small = _adamw_small(red, mat_all.reshape(MAT_ROWS, HD), cparts, gab, ws, ms, vs, me)
    big = dict(w_in=(g_w_in, d_w_in, nm_w_in, nv_w_in), w_out=(g_w_out, d_w_out, nm_w_out, nv_w_out),
               ada_w=(g_ada, d_ada, nm_ada, nv_ada))

    loss = red[R_LOSS, 0]
    names = ("c_ctx", "ada_w", "ada_b", "norm_g", "w_in", "conv_w", "conv_b", "lru_wa", "lru_ba", "lru_wx", "lru_bx",
             "lru_lambda", "sgu_ln_g", "sgu_ln_b", "sgu_w", "sgu_b", "w_out", "final_g")
    outs = [loss, gx.reshape(x.shape)]
    for kind in range(4):
        for k in names:
            val = big[k][kind] if k in big else small[kind][k]
            outs.append(val.reshape(args[k].shape))
    return tuple(outs)
```

```python
import functools

import jax
import jax.numpy as jnp
from jax import lax
from jax.experimental import pallas as pl
from jax.experimental.pallas import tpu as pltpu

F32 = jnp.float32
BF16 = jnp.bfloat16

N_DEV = 8
D = 1024
L = 2048
LC = 256
HEADS = 8
HD = 128
CHUNK = 128
D_IN = 5 * D
W_IN_SHARD = D_IN // N_DEV
ROWS = 256
D_MIX = 2 * D
CONV_W = 4
LRU_C = 8.0
NORM_EPS = 1e-6
LN_EPS = 1e-5
ADAM_LR, ADAM_B1, ADAM_B2, ADAM_EPS, ADAM_WD, ADAM_STEP = 0.001, 0.9, 0.999, 1e-08, 0.01, 10

VMEM_LIMIT = 56 * 1024 * 1024

HBM = pl.BlockSpec(memory_space=pltpu.HBM)
VMEM = pl.BlockSpec(memory_space=pltpu.VMEM)
MESH = pl.DeviceIdType.MESH


def _call(body, **kw):
    return pl.pallas_call(body, **kw)


def _params(*sem):
    return pltpu.CompilerParams(dimension_semantics=sem, vmem_limit_bytes=VMEM_LIMIT)


def _sigmoid(x):
    return 0.5 * jnp.tanh(0.5 * x) + 0.5


def _silu_and_grad(x):
    s = _sigmoid(x)
    return x * s, s * (1.0 + x * (1.0 - s))


_G0 = 0.7978845608028654
_G1 = 0.044715


def _gelu_and_grad(x):
    x2 = x * x
    t = jnp.tanh(_G0 * (x + _G1 * x * x2))
    cdf = 0.5 * (1.0 + t)
    return x * cdf, cdf + 0.5 * x * (1.0 - t * t) * (_G0 * (1.0 + 3.0 * _G1 * x2))


def _gelu(x):
    return 0.5 * x * (1.0 + jnp.tanh(_G0 * (x + _G1 * x * x * x)))


def _softplus(z):
    t = jnp.exp(-jnp.abs(z))
    u = 1.0 + t
    log1p = jnp.where(u == 1.0, t, jnp.log(u) * t / jnp.where(u == 1.0, 1.0, u - 1.0))
    return jnp.maximum(z, 0.0) + log1p


def _dot(a, b):
    return jnp.dot(a, b, preferred_element_type=F32)


def _dot_nt(a, b):
    return lax.dot_general(a, b, (((1,), (1,)), ((), ())), preferred_element_type=F32)


def _dot_tn(a, b):
    return lax.dot_general(a, b, (((0,), (0,)), ((), ())), preferred_element_type=F32)


def _rows(shape):
    return lax.broadcasted_iota(jnp.int32, shape, 0)


def _shift_down(x, first):
    return jnp.where(_rows(x.shape) == 0, first, pltpu.roll(x, 1, 0))


def _shift_up(x, last):
    n = x.shape[0]
    return jnp.where(_rows(x.shape) == n - 1, last, pltpu.roll(x, n - 1, 0))


def _gather2(arrays, modes, name):
    n = len(arrays)

    def body(*refs):
        start, forward, finish = _gather2_ops(refs[:n], refs[n:2 * n], modes, *refs[2 * n:])
        start()
        forward()
        finish()

    return _call(
        body, name=name, out_shape=_gather2_shapes(arrays, modes), in_specs=[HBM] * n, out_specs=[HBM] * n,
        scratch_shapes=_gather2_sems(n), compiler_params=pltpu.CompilerParams(has_side_effects=True),
    )(*[pltpu.with_memory_space_constraint(a, pltpu.HBM) for a in arrays])


def _gather2_shapes(arrays, modes):
    return [jax.ShapeDtypeStruct((N_DEV,) + a.shape if m == "ag" else (a.shape[0], N_DEV * a.shape[1]), a.dtype)
            for a, m in zip(arrays, modes)]


def _gather2_sems(n):
    return [pltpu.SemaphoreType.DMA((n, N_DEV - 1)), pltpu.SemaphoreType.DMA((n, N_DEV - 1)),
            pltpu.SemaphoreType.DMA((n,))]


def _gather2_ops(ins, outs, modes, send_sems, recv_sems, local_sems):
    n = len(ins)
    x, y, c = lax.axis_index("x"), lax.axis_index("y"), lax.axis_index("c")
    me, sibling = (x, y, c), (x, y, 1 - c)
    chips = [(x ^ (k >> 1), y ^ (k & 1)) for k in (1, 2, 3)]

    def slot(j, px, py, pc):
        dev = 4 * px + 2 * py + pc
        if modes[j] == "agc":
            w = ins[j].shape[1]
            return outs[j].at[:, pl.ds(pl.multiple_of(dev * w, 128), w)]
        return outs[j].at[dev]

    def copy(j, k, block, to, src=None):
        return pltpu.make_async_remote_copy(
            src_ref=slot(j, *block) if src is None else src, dst_ref=slot(j, *block),
            send_sem=send_sems.at[j, k], recv_sem=recv_sems.at[j, k], device_id=to, device_id_type=MESH)

    def own(j):
        return pltpu.make_async_copy(ins[j], slot(j, *me), local_sems.at[j])

    def first(j):
        return [copy(j, 0, me, sibling, src=ins[j])] + [copy(j, 1 + i, me, (*chip, c), src=ins[j])
                                                        for i, chip in enumerate(chips)]

    def passed(j, i):
        return copy(j, 4 + i, (*chips[i], c), sibling)

    def start():
        for j in range(n):
            own(j).start()
            for cp in first(j):
                cp.start()

    def forward():
        for i, chip in enumerate(chips):
            for j in range(n):
                copy(j, 1 + i, (*chip, c), me).wait_recv()
                passed(j, i).start()

    def finish():
        for j in range(n):
            copy(j, 0, sibling, me).wait_recv()
            for i, chip in enumerate(chips):
                copy(j, 4 + i, (*chip, 1 - c), me).wait_recv()
            for cp in first(j) + [passed(j, i) for i in range(3)]:
                cp.wait_send()
            own(j).wait()

    return start, forward, finish


def _reduce2_local(arrays, modes, me, name, counts=None):
    n = len(arrays)
    counts = counts or [4] * n
    shapes = [(a.shape[1], a.shape[2]) if m == "a2a" else (a.shape[0], a.shape[1] // (2 * cnt))
              for a, m, cnt in zip(arrays, modes, counts)]
    staged = [jax.ShapeDtypeStruct((cnt,) + s, a.dtype) for s, a, cnt in zip(shapes, arrays, counts)]

    def piece(ref, mode, dev, w):
        return ref.at[dev] if mode == "a2a" else ref.at[:, pl.ds(pl.multiple_of(dev * w, 128), w)]

    def to_sibling(*refs):
        ins, outs = refs[:n], refs[n:2 * n]
        send_sems, recv_sems = refs[2 * n:]
        x, y, c = lax.axis_index("x"), lax.axis_index("y"), lax.axis_index("c")
        copies = []
        for j in range(n):
            for q in range(counts[j]):
                cp = pltpu.make_async_remote_copy(
                    src_ref=piece(ins[j], modes[j], 2 * q + (1 - c), shapes[j][1]), dst_ref=outs[j].at[q],
                    send_sem=send_sems.at[j, q], recv_sem=recv_sems.at[j, q], device_id=(x, y, 1 - c),
                    device_id_type=MESH)
                cp.start()
                copies.append(cp)
        for cp in copies:
            cp.wait()

    stage = _call(
        to_sibling, name=name + "_d2d", out_shape=staged, in_specs=[HBM] * n, out_specs=[HBM] * n,
        scratch_shapes=[pltpu.SemaphoreType.DMA((n, 4)), pltpu.SemaphoreType.DMA((n, 4))],
        compiler_params=pltpu.CompilerParams(has_side_effects=True),
    )(*[pltpu.with_memory_space_constraint(a, pltpu.HBM) for a in arrays])

    def add(me_ref, *refs):
        del me_ref
        own, got, outs = refs[:n], refs[n:2 * n], refs[2 * n:]
        for j in range(n):
            mine = own[j][0] if modes[j] == "a2a" else own[j][...]
            outs[j][0] = (mine.astype(F32) + got[j][0].astype(F32)).astype(outs[j].dtype)

    in_specs, slot_specs = [], []
    for (r, w), m, cnt in zip(shapes, modes, counts):
        if m == "a2a":
            in_specs.append(pl.BlockSpec(
                (1, r, w), lambda q, me_ref, cnt=cnt: (2 * jnp.minimum(q, cnt - 1) + me_ref[0] % 2, 0, 0)))
        else:
            in_specs.append(pl.BlockSpec(
                (r, w), lambda q, me_ref, cnt=cnt: (0, 2 * jnp.minimum(q, cnt - 1) + me_ref[0] % 2)))
        slot_specs.append(pl.BlockSpec((1, r, w), lambda q, me_ref, cnt=cnt: (jnp.minimum(q, cnt - 1), 0, 0)))
    return _call(
        add, name=name + "_add", out_shape=staged,
        grid_spec=pltpu.PrefetchScalarGridSpec(num_scalar_prefetch=1, grid=(max(counts),),
                                               in_specs=in_specs + slot_specs, out_specs=slot_specs),
        compiler_params=_params("arbitrary"),
    )(me, *arrays, *stage)


def _chips_sems(n):
    return [pltpu.SemaphoreType.DMA((n, 6)), pltpu.SemaphoreType.DMA((n, 6)), pltpu.SemaphoreType.DMA((n,))]


def _chips_stage_shapes(chip_sums):
    return [jax.ShapeDtypeStruct((2, a.shape[1] // 2, a.shape[2]), a.dtype) for a in chip_sums]


def _chips_ops(ins, outs, stages, send_sems, recv_sems, local_sems, first_chips=None):
    x, y, c = lax.axis_index("x"), lax.axis_index("y"), lax.axis_index("c")
    qm = 2 * x + y
    first_chips = first_chips or [0] * len(ins)

    def owns(j, chip):
        lo, cnt = first_chips[j], ins[j].shape[0]
        if lo == 0 and cnt == 4:
            return None
        return jnp.logical_and(chip >= lo, chip < lo + cnt)

    def guarded(cond, fn):
        if cond is None:
            fn()
        else:
            pl.when(cond)(fn)

    def slot(j, chip):
        return jnp.clip(chip - first_chips[j], 0, ins[j].shape[0] - 1)

    def half(j, i):
        h = ins[j].shape[1] // 2
        return pl.ds(i * h, h)

    def copy(j, sem, src, dst, k):
        return pltpu.make_async_remote_copy(
            src_ref=src, dst_ref=dst, send_sem=send_sems.at[j, sem], recv_sem=recv_sems.at[j, sem],
            device_id=(x ^ (k >> 1), y ^ (k & 1), c), device_id_type=MESH)

    def direct(j, k):
        return copy(j, k - 1, ins[j].at[slot(j, qm ^ k)], outs[j].at[qm], k)

    def first_hop(j, k):
        return copy(j, 1 + k, ins[j].at[slot(j, qm ^ 3), half(j, k - 1)], stages[j].at[k - 1], k)

    def second_hop(j, k):
        return copy(j, 3 + k, stages[j].at[2 - k], outs[j].at[qm ^ (3 - k), half(j, 2 - k)], k)

    def local(j):
        return pltpu.make_async_copy(ins[j].at[slot(j, qm)], outs[j].at[qm], local_sems.at[j])

    def start():
        for j in range(len(ins)):
            for k in (1, 2):
                guarded(owns(j, qm ^ 3), lambda j=j, k=k: first_hop(j, k).start())
        for j in range(len(ins)):
            for k in (1, 2):
                guarded(owns(j, qm ^ k), lambda j=j, k=k: direct(j, k).start())
            guarded(owns(j, qm), lambda j=j: local(j).start())

    def forward():
        for j in range(len(ins)):
            for k in (1, 2):
                def pass_on(j=j, k=k):
                    first_hop(j, 3 - k).wait_recv()
                    second_hop(j, k).start()
                guarded(owns(j, qm ^ k), pass_on)

    def finish():
        for j in range(len(ins)):
            for k in (1, 2):
                guarded(owns(j, qm ^ k), lambda j=j, k=k: direct(j, k).wait_send())
                guarded(owns(j, qm ^ k), lambda j=j, k=k: second_hop(j, k).wait_send())
                guarded(owns(j, qm ^ 3), lambda j=j, k=k: first_hop(j, k).wait_send())
                guarded(owns(j, qm), lambda j=j, k=k: direct(j, k).wait_recv())
                guarded(owns(j, qm), lambda j=j, k=k: second_hop(j, k).wait_recv())
            guarded(owns(j, qm), lambda j=j: local(j).wait())

    return start, forward, finish


def _front(c, c_ctx, ada_w, ada_b, w_in, w_out, me):
    nloc = ada_w.shape[1]

    def body(me_ref, c_ref, cc_ref, aw_ref, ab_ref, win_ref, wout_ref,
             wfull_ref, woutb_ref, modx_ref, modc_ref, call_ref,
             wb_s, part_s, parts_s, w_send, w_recv, w_local, s_send, s_recv):
        x, y, cidx = lax.axis_index("x"), lax.axis_index("y"), lax.axis_index("c")
        me = me_ref[0]
        wb_s[...] = win_ref[...].astype(BF16)
        woutb_ref[...] = wout_ref[...].astype(BF16)
        start, forward, finish = _gather2_ops([wb_s], [wfull_ref], ["agc"], w_send, w_recv, w_local)
        start()

        def small_gather(src, my_slot, stage):
            copies = []
            for k in range(1, N_DEV):
                peer = (x ^ (k >> 2), y ^ ((k >> 1) & 1), cidx ^ (k & 1))
                cp = pltpu.make_async_remote_copy(src_ref=src, dst_ref=my_slot, send_sem=s_send.at[stage, k - 1],
                                                  recv_sem=s_recv.at[stage, k - 1], device_id=peer,
                                                  device_id_type=MESH)
                cp.start()
                copies.append(cp)
            pltpu.sync_copy(src, my_slot)
            for cp in copies:
                cp.wait()

        small_gather(c_ref, call_ref.at[pl.ds(me, 1), :], 0)
        off = pl.multiple_of(me * nloc, 128)
        b = ab_ref[:, pl.ds(off, nloc)]
        w = aw_ref[...]
        sx, _ = _silu_and_grad(call_ref[...])
        sc, _ = _silu_and_grad(jnp.broadcast_to(cc_ref[...], (8, D)))
        part_s[0:8, :] = _dot(sx, w) + b
        part_s[8:16, :] = _dot(sc, w) + b
        small_gather(part_s, parts_s.at[me], 1)
        mine = _rows((16, nloc)) == me
        for j in range(N_DEV):
            pj = parts_s[j]
            modx_ref[:, j * nloc:(j + 1) * nloc] = jnp.sum(jnp.where(mine, pj, 0.0), axis=0, keepdims=True)
            modc_ref[:, j * nloc:(j + 1) * nloc] = pj[8:9, :]
        forward()
        finish()

    return _call(
        body, name="front",
        out_shape=[jax.ShapeDtypeStruct((D, D_IN), BF16), jax.ShapeDtypeStruct(w_out.shape, BF16),
                   jax.ShapeDtypeStruct((1, 3 * D), F32), jax.ShapeDtypeStruct((1, 3 * D), F32),
                   jax.ShapeDtypeStruct((N_DEV, D), F32)],
        in_specs=[pl.BlockSpec(memory_space=pltpu.SMEM)] + [VMEM] * 6, out_specs=[HBM, VMEM, VMEM, VMEM, VMEM],
        scratch_shapes=[pltpu.VMEM(w_in.shape, BF16), pltpu.VMEM((16, nloc), F32),
                        pltpu.VMEM((N_DEV, 16, nloc), F32)] + _gather2_sems(1) +
                       [pltpu.SemaphoreType.DMA((2, N_DEV - 1)), pltpu.SemaphoreType.DMA((2, N_DEV - 1))],
        compiler_params=pltpu.CompilerParams(vmem_limit_bytes=VMEM_LIMIT, has_side_effects=True),
    )(me, c, c_ctx, ada_w, ada_b, w_in, w_out)


ARRIVAL = (0, 1, 2, 4, 3, 5, 6, 7)


def _front_project(xr, c, c_ctx, ada_w, ada_b, ng, w_in, w_out, cw, lam, me):
    nloc = ada_w.shape[1]
    ws = W_IN_SHARD
    order = me[0] ^ jnp.asarray(ARRIVAL, jnp.int32)

    def body(ord_ref, x_ref, c_ref, cc_ref, aw_ref, ab_ref, ng_ref, win_ref, wout_ref, cw_ref, lam_ref,
             z_ref, hn_ref, wfull_ref, woutb_ref, modx_ref, modc_ref, call_ref, cwf_ref, lamf_ref,
             wv, call_s, part_s, parts_s, w_send, w_recv, hbm_sems, s_send, s_recv, g_send, g_recv, g_local):
        t = pl.program_id(0)
        x, y, cidx = lax.axis_index("x"), lax.axis_index("y"), lax.axis_index("c")
        me_i = ord_ref[0]
        sibling = (x, y, 1 - cidx)
        chips = [(x ^ (k >> 1), y ^ (k & 1)) for k in (1, 2, 3)]
        g_start, g_pass, g_finish = _gather2_ops([cw_ref, lam_ref], [cwf_ref, lamf_ref], ["agc", "agc"],
                                                 g_send, g_recv, g_local)

        def shard_copy(k, px, py, pc, to):
            slot = wv.at[4 * px + 2 * py + pc]
            return pltpu.make_async_remote_copy(src_ref=slot, dst_ref=slot, send_sem=w_send.at[k],
                                                recv_sem=w_recv.at[k], device_id=to, device_id_type=MESH)

        def small_gather(src, my_slot, stage):
            copies = []
            for k in range(1, N_DEV):
                peer = (x ^ (k >> 2), y ^ ((k >> 1) & 1), cidx ^ (k & 1))
                cp = pltpu.make_async_remote_copy(src_ref=src, dst_ref=my_slot, send_sem=s_send.at[stage, k - 1],
                                                  recv_sem=s_recv.at[stage, k - 1], device_id=peer,
                                                  device_id_type=MESH)
                cp.start()
                copies.append(cp)
            pltpu.sync_copy(src, my_slot)
            for cp in copies:
                cp.wait()

        @pl.when(t == 0)
        def _():
            g_start()
            wv[me_i] = win_ref[...].astype(BF16)
            woutb_ref[...] = wout_ref[...].astype(BF16)
            shard_copy(0, x, y, cidx, sibling).start()
            small_gather(c_ref, call_s.at[pl.ds(me_i, 1), :], 0)
            call_ref[...] = call_s[...]
            off = pl.multiple_of(me_i * nloc, 128)
            b = ab_ref[:, pl.ds(off, nloc)]
            w = aw_ref[...]
            sx, _ = _silu_and_grad(call_s[...])
            sc, _ = _silu_and_grad(jnp.broadcast_to(cc_ref[...], (8, D)))
            part_s[0:8, :] = _dot(sx, w) + b
            part_s[8:16, :] = _dot(sc, w) + b
            small_gather(part_s, parts_s.at[me_i], 1)
            for i, chip in enumerate(chips):
                shard_copy(1 + i, x, y, cidx, (*chip, cidx)).start()
            mine = _rows((16, nloc)) == me_i
            for j in range(N_DEV):
                pj = parts_s[j]
                modx_ref[:, j * nloc:(j + 1) * nloc] = jnp.sum(jnp.where(mine, pj, 0.0), axis=0, keepdims=True)
                modc_ref[:, j * nloc:(j + 1) * nloc] = pj[8:9, :]
            shift, scale1, ngv = modx_ref[:, 0:D], 1.0 + modx_ref[:, D:2 * D], ng_ref[...]
            for r in range(L // ROWS):
                rsl = slice(r * ROWS, (r + 1) * ROWS)
                xv = x_ref[rsl, :]
                rs = lax.rsqrt(jnp.mean(xv * xv, axis=-1, keepdims=True) + NORM_EPS)
                hn_ref[rsl, :] = ((xv * rs * ngv) * scale1 + shift).astype(BF16)

        @pl.when(t == 1)
        def _():
            shard_copy(0, x, y, 1 - cidx, sibling).wait_recv()
            g_pass()

        for i, chip in enumerate(chips):
            @pl.when(t == ARRIVAL.index((2, 4, 6)[i]))
            def _(i=i, chip=chip):
                shard_copy(1 + i, *chip, cidx, sibling).wait_recv()
                shard_copy(4 + i, *chip, cidx, sibling).start()

            @pl.when(t == ARRIVAL.index((3, 5, 7)[i]))
            def _(i=i, chip=chip):
                shard_copy(4 + i, *chip, 1 - cidx, sibling).wait_recv()

        @pl.when(t == 2)
        def _():
            g_finish()

        dev = ord_ref[t]
        for r in range(L // (2 * ROWS)):
            rsl = slice(r * 2 * ROWS, (r + 1) * 2 * ROWS)
            z_ref[rsl, :] = _dot(hn_ref[rsl, :], wv[dev])
        col = pl.ds(pl.multiple_of(dev * ws, 128), ws)
        pltpu.make_async_copy(wv.at[dev], wfull_ref.at[:, col], hbm_sems.at[t]).start()

        @pl.when(t == N_DEV - 1)
        def _():
            for k in range(7):
                shard_copy(k, x, y, cidx, sibling).wait_send()
            for s in range(N_DEV):
                pltpu.make_async_copy(wv.at[0], wfull_ref.at[:, pl.ds(0, ws)], hbm_sems.at[s]).wait()

    const = lambda *shape: pl.BlockSpec(shape, lambda t, o: (0,) * len(shape))
    once = lambda *shape: pl.BlockSpec(shape, lambda t, o: (0,) * len(shape), pipeline_mode=pl.Buffered(1))
    return _call(
        body, name="front_project",
        out_shape=[jax.ShapeDtypeStruct((L, D_IN), F32), jax.ShapeDtypeStruct((L, D), BF16),
                   jax.ShapeDtypeStruct((D, D_IN), BF16), jax.ShapeDtypeStruct(w_out.shape, BF16),
                   jax.ShapeDtypeStruct((1, 3 * D), F32), jax.ShapeDtypeStruct((1, 3 * D), F32),
                   jax.ShapeDtypeStruct((N_DEV, D), F32), jax.ShapeDtypeStruct((CONV_W, D), F32),
                   jax.ShapeDtypeStruct((2, D), F32)],
        grid_spec=pltpu.PrefetchScalarGridSpec(
            num_scalar_prefetch=1, grid=(N_DEV,),
            in_specs=[once(L, D), const(1, D), const(1, D), once(D, nloc), const(1, 3 * D), const(1, D),
                      once(D, ws), once(*w_out.shape), HBM, HBM],
            out_specs=[pl.BlockSpec((L, ws), lambda t, o: (0, o[t])), const(L, D), HBM, const(*w_out.shape),
                       const(1, 3 * D), const(1, 3 * D), const(N_DEV, D), HBM, HBM],
            scratch_shapes=[pltpu.VMEM((N_DEV, D, ws), BF16), pltpu.VMEM((N_DEV, D), F32), pltpu.VMEM((16, nloc), F32),
                            pltpu.VMEM((N_DEV, 16, nloc), F32), pltpu.SemaphoreType.DMA((7,)),
                            pltpu.SemaphoreType.DMA((7,)), pltpu.SemaphoreType.DMA((N_DEV,)),
                            pltpu.SemaphoreType.DMA((2, N_DEV - 1)), pltpu.SemaphoreType.DMA((2, N_DEV - 1))]
            + _gather2_sems(2)),
        compiler_params=pltpu.CompilerParams(dimension_semantics=("arbitrary",), vmem_limit_bytes=VMEM_LIMIT,
                                             has_side_effects=True),
    )(order, xr, c, c_ctx, ada_w, ada_b, ng, w_in, w_out, pltpu.with_memory_space_constraint(cw, pltpu.HBM),
      pltpu.with_memory_space_constraint(lam, pltpu.HBM))


def _project(xr, mod, ng, w, ncols, tm, name, gather=None, gather_modes=()):
    rows = xr.shape[0]
    steps = rows // tm
    ng_ = len(gather or ())

    def body(x_ref, sh_ref, sc_ref, ng_ref, w_ref, *rest):
        z_ref, hn_ref = rest[ng_:ng_ + 2]
        if ng_:
            start, forward, finish = _gather2_ops(rest[:ng_], rest[ng_ + 2:2 * ng_ + 2], gather_modes,
                                                  *rest[2 * ng_ + 2:])
            pl.when(pl.program_id(0) == 0)(start)
            pl.when(pl.program_id(0) == steps // 2)(forward)
        x = x_ref[...]
        rs = lax.rsqrt(jnp.mean(x * x, axis=-1, keepdims=True) + NORM_EPS)
        hn = (x * rs * ng_ref[...]) * (1.0 + sc_ref[...]) + sh_ref[...]
        hb = hn.astype(BF16)
        hn_ref[...] = hb
        for n in range(ncols // D):
            z_ref[:, n * D:(n + 1) * D] = _dot(hb, w_ref[:, n * D:(n + 1) * D])
        if ng_:
            pl.when(pl.program_id(0) == steps - 1)(finish)

    vec = pl.BlockSpec((1, D), lambda i: (0, 0))
    gathered = _gather2_shapes(gather, gather_modes) if ng_ else []
    return _call(
        body, name=name, grid=(steps,),
        out_shape=[jax.ShapeDtypeStruct((rows, ncols), F32), jax.ShapeDtypeStruct((rows, D), BF16)] + gathered,
        in_specs=[pl.BlockSpec((tm, D), lambda i: (i, 0)), vec, pl.BlockSpec((1, D), lambda i: (0, 1)), vec,
                  pl.BlockSpec((D, ncols), lambda i: (0, 0), pipeline_mode=pl.Buffered(1))] + [HBM] * ng_,
        out_specs=[pl.BlockSpec((tm, ncols), lambda i: (i, 0)), pl.BlockSpec((tm, D), lambda i: (i, 0))] + [HBM] * ng_,
        scratch_shapes=_gather2_sems(ng_) if ng_ else [],
        compiler_params=pltpu.CompilerParams(dimension_semantics=("arbitrary",), vmem_limit_bytes=VMEM_LIMIT,
                                             has_side_effects=bool(ng_)),
    )(xr, mod, mod, ng, w, *[pltpu.with_memory_space_constraint(a, pltpu.HBM) for a in gather or ()])


def _scan_pair(af_ref, uf_ref, hf_ref, h0f, ab_ref, ub_ref, hb_ref, h0b, t_len):
    span = 8 * SCAN_BLOCKS
    nit = t_len // span
    rows = _rows((8, HD))

    def local_scan(a, b, forward):
        for s in (1, 2, 4):
            sh = s if forward else 8 - s
            m = rows >= s if forward else rows < 8 - s
            b = a * jnp.where(m, pltpu.roll(b, sh, 0), 0.0) + b
            a = a * jnp.where(m, pltpu.roll(a, sh, 0), 1.0)
        return a, b

    def span_scan(a_ref, u_ref, h_ref, off, carry, forward):
        order = range(SCAN_BLOCKS) if forward else range(SCAN_BLOCKS - 1, -1, -1)
        last = slice(7, 8) if forward else slice(0, 1)
        for q in order:
            rs = pl.ds(off + 8 * q, 8)
            a, b = local_scan(a_ref[rs, :], u_ref[rs, :], forward)
            h_ref[rs, :] = b + a * carry
            carry = a[last, :] * carry + b[last, :]
        return carry

    def body(k, carry):
        cf, cb = carry
        cf = span_scan(af_ref, uf_ref, hf_ref, pl.multiple_of(k * span, span), cf, True)
        cb = span_scan(ab_ref, ub_ref, hb_ref, pl.multiple_of((nit - 1 - k) * span, span), cb, False)
        return cf, cb

    return lax.fori_loop(0, nit, body, (h0f, h0b))


SCAN_BLOCKS = 4


def _conv(xa, cw, cb):
    z = jnp.zeros((1, HD), F32)
    xm1 = _shift_down(xa, z)
    xp1 = _shift_up(xa, z)
    xp2 = _shift_up(xp1, z)
    return xm1 * cw[0:1, :] + xa * cw[1:2, :] + xp1 * cw[2:3, :] + xp2 * cw[3:4, :] + cb


def _gates(xc, wa, wx, ba, bx, nsp):
    xb = xc.astype(BF16)
    r = _sigmoid(_dot(xb, wa) + ba)
    i = _sigmoid(_dot(xb, wx) + bx)
    log_a = r * nsp
    a = jnp.exp(log_a)
    g2 = jnp.tanh(log_a) * (-1.0 - a * a)
    rg = lax.rsqrt(jnp.maximum(g2, 1e-30))
    return r, i, a, g2 * rg, rg


def _lru_param_specs():
    h4 = pl.BlockSpec((2, 1, HD, HD), lambda h: (0, h, 0, 0))
    v2 = pl.BlockSpec((2, HD), lambda h: (0, h))
    return dict(
        xa=pl.BlockSpec((L, HD), lambda h: (0, h)), xac=pl.BlockSpec((LC, HD), lambda h: (0, h)),
        cw=pl.BlockSpec((CONV_W, HD), lambda h: (0, h)), cb=pl.BlockSpec((1, HD), lambda h: (0, h)), h4=h4, v2=v2)


def _lru_forward(zx, zc, cw, cb, wa, wx, ba, bx, lam, gather, gather_modes):
    ng_ = len(gather)

    def body(xa_ref, xac_ref, cw_ref, cb_ref, wa_ref, wx_ref, ba_ref, bx_ref, lam_ref, *rest):
        yl_ref = rest[ng_]
        af, uf, hf, ab, ub, hb = rest[2 * ng_ + 1:2 * ng_ + 7]
        start, pass_on, finish = _gather2_ops(rest[:ng_], rest[ng_ + 1:2 * ng_ + 1], gather_modes,
                                              *rest[2 * ng_ + 7:])
        pl.when(pl.program_id(0) == 0)(start)
        pl.when(pl.program_id(0) == HEADS // 2)(pass_on)
        pl.when(pl.program_id(0) == HEADS - 1)(finish)
        cwv, cbv = cw_ref[...], cb_ref[...]
        nsp = (-LRU_C) * _softplus(-lam_ref[...])

        def forward(xa, t_len, h0f, h0b):
            xc = _conv(xa, cwv, cbv)
            for d, (a_ref, u_ref) in enumerate(((af, uf), (ab, ub))):
                _, i, a, gamma, _ = _gates(xc, wa_ref[d, 0].astype(BF16), wx_ref[d, 0].astype(BF16),
                                           ba_ref[d:d + 1, :], bx_ref[d:d + 1, :], nsp[d:d + 1, :])
                a_ref[0:t_len, :] = a
                u_ref[0:t_len, :] = gamma * (i * xc)
            return _scan_pair(af, uf, hf, h0f, ab, ub, hb, h0b, t_len)

        z = jnp.zeros((1, HD), F32)
        h0f, h0b = forward(xac_ref[...], LC, z, z)
        forward(xa_ref[...], L, h0f, h0b)
        yl_ref[...] = hf[...] + hb[...]

    s = _lru_param_specs()
    return _call(
        body, name="lru_forward", grid=(HEADS,),
        out_shape=[jax.ShapeDtypeStruct((L, D), F32)] + _gather2_shapes(gather, gather_modes),
        in_specs=[s["xa"], s["xac"], s["cw"], s["cb"], s["h4"], s["h4"], s["v2"], s["v2"], s["v2"]] + [HBM] * ng_,
        out_specs=[pl.BlockSpec((L, HD), lambda h: (0, h))] + [HBM] * ng_,
        scratch_shapes=[pltpu.VMEM((L, HD), F32)] * 6 + _gather2_sems(ng_),
        compiler_params=pltpu.CompilerParams(dimension_semantics=("arbitrary",), vmem_limit_bytes=VMEM_LIMIT,
                                             has_side_effects=True),
    )(zx, zc, cw, cb, wa, wx, ba, bx, lam, *[pltpu.with_memory_space_constraint(a, pltpu.HBM) for a in gather])


def _lru_backward(zx, zc, dyl, dz, cw, cb, wa, wx, ba, bx, lam, chip_sums, first_chips=None):
    nr = len(chip_sums)

    def body(xa_ref, xac_ref, dyl_ref, dz_in, cw_ref, cb_ref, wa_ref, wx_ref, ba_ref, bx_ref, lam_ref, *rest):
        (dxa_ref, dxac_ref, dwa_ref, dwx_ref, dba_ref, dbx_ref, dlam_ref, dcw_ref,
         dcb_ref) = rest[nr:nr + 9]
        main_s, ctx_s = rest[3 * nr + 9:3 * nr + 11]
        if nr:
            start, forward, finish = _chips_ops(rest[:nr], rest[nr + 9:2 * nr + 9], rest[2 * nr + 9:3 * nr + 9],
                                                *rest[3 * nr + 11:], first_chips=first_chips)
            pl.when(pl.program_id(0) == 0)(start)
            pl.when(pl.program_id(0) == HEADS // 2)(forward)
            pl.when(pl.program_id(0) == HEADS - 1)(finish)
        del dz_in
        cwv, cbv = cw_ref[...], cb_ref[...]
        lamv = lam_ref[...]
        sp = _softplus(-lamv)
        nsp = (-LRU_C) * sp
        z = jnp.zeros((1, HD), F32)

        def wmat(ref, d):
            return ref[d, 0].astype(BF16)

        def workspace(s):
            return dict(a=(s.at[0], s.at[1]), u=(s.at[2], s.at[3]), h=(s.at[4], s.at[5]), rho=(s.at[6], s.at[7]),
                        saved=(tuple(s.at[8 + k] for k in range(4)), tuple(s.at[12 + k] for k in range(4))),
                        xc=s.at[16])

        def forward(ws, xa, t_len, h0f, h0b):
            xc = _conv(xa, cwv, cbv)
            ws["xc"][...] = xc
            for d in (0, 1):
                vals = _gates(xc, wmat(wa_ref, d), wmat(wx_ref, d), ba_ref[d:d + 1, :], bx_ref[d:d + 1, :],
                              nsp[d:d + 1, :])
                r, i, a, gamma, rg = vals
                ws["a"][d][...] = a
                ws["u"][d][...] = gamma * (i * xc)
                for ref, val in zip(ws["saved"][d], (r, i, gamma, rg)):
                    ref[...] = val
            return _scan_pair(ws["a"][0], ws["u"][0], ws["h"][0], h0f, ws["a"][1], ws["u"][1], ws["h"][1], h0b,
                              t_len)

        def backward(ws, xa, t_len, h0f, h0b, dhf, dhb, first):
            xc = ws["xc"][...]
            (af, ab), (uf, ub), (hf, hb), (rf, rb) = ws["a"], ws["u"], ws["h"], ws["rho"]
            uf[...] = ab[...] * dhb
            ub[...] = af[...] * dhf
            rho_b_last, rho_f_first = _scan_pair(ab, uf, rb, z, af, ub, rf, z, t_len)
            dxc = jnp.zeros((t_len, HD), F32)
            dsp = []
            for d in (0, 1):
                r, i, gamma, rg = (ref[...] for ref in ws["saved"][d])
                a = ws["a"][d][...]
                if d == 0:
                    lam_t = dhf + _shift_up(rf[...], z)
                    h_prev = _shift_down(hf[...], h0f)
                else:
                    lam_t = dhb + _shift_down(rb[...], z)
                    h_prev = _shift_up(hb[...], h0b)
                da = lam_t * h_prev
                lx = lam_t * xc
                d_i = lx * gamma
                d_gamma = lx * i
                dxc = dxc + lam_t * (gamma * i)
                d_log_a = a * (da - d_gamma * (a * rg))
                dsp.append(jnp.sum(d_log_a * r, axis=0, keepdims=True) * (-LRU_C))
                d_pre_r = d_log_a * nsp[d:d + 1, :] * (r * (1.0 - r))
                d_pre_i = d_i * (i * (1.0 - i))
                prb, pib, xb = d_pre_r.astype(BF16), d_pre_i.astype(BF16), xc.astype(BF16)
                dxc = dxc + _dot_nt(prb, wmat(wa_ref, d)) + _dot_nt(pib, wmat(wx_ref, d))
                g_wa, g_wx = _dot_tn(xb, prb), _dot_tn(xb, pib)
                g_ba = jnp.sum(d_pre_r, axis=0, keepdims=True)
                g_bx = jnp.sum(d_pre_i, axis=0, keepdims=True)
                if first:
                    dwa_ref[d, 0] = g_wa
                    dwx_ref[d, 0] = g_wx
                    dba_ref[d:d + 1, :] = g_ba
                    dbx_ref[d:d + 1, :] = g_bx
                else:
                    dwa_ref[d, 0] += g_wa
                    dwx_ref[d, 0] += g_wx
                    dba_ref[d:d + 1, :] += g_ba
                    dbx_ref[d:d + 1, :] += g_bx
            g_lam = jnp.concatenate(dsp, axis=0) * (-_sigmoid(-lamv))
            dm1 = _shift_down(dxc, z)
            dp1 = _shift_up(dxc, z)
            dm2 = _shift_down(dm1, z)
            dxa = dp1 * cwv[0:1, :] + dxc * cwv[1:2, :] + dm1 * cwv[2:3, :] + dm2 * cwv[3:4, :]
            xm1 = _shift_down(xa, z)
            xp1 = _shift_up(xa, z)
            xp2 = _shift_up(xp1, z)
            g_cw = jnp.concatenate([jnp.sum(dxc * v, axis=0, keepdims=True) for v in (xm1, xa, xp1, xp2)], axis=0)
            g_cb = jnp.sum(dxc, axis=0, keepdims=True)
            if first:
                dlam_ref[...] = g_lam
                dcw_ref[...] = g_cw
                dcb_ref[...] = g_cb
            else:
                dlam_ref[...] += g_lam
                dcw_ref[...] += g_cw
                dcb_ref[...] += g_cb
            return dxa, rho_f_first, rho_b_last

        ws_x, ws_c = workspace(main_s), workspace(ctx_s)
        h0f, h0b = forward(ws_c, xac_ref[...], LC, z, z)
        forward(ws_x, xa_ref[...], L, h0f, h0b)
        dh = dyl_ref[...]
        dxa, dh0f, dh0b = backward(ws_x, xa_ref[...], L, h0f, h0b, dh, dh, True)
        dxa_ref[...] = dxa.astype(BF16)
        rc = _rows((LC, HD))
        dxac, _, _ = backward(ws_c, xac_ref[...], LC, z, z, jnp.where(rc == LC - 1, dh0f, 0.0),
                              jnp.where(rc == 0, dh0b, 0.0), False)
        dxac_ref[...] = dxac.astype(BF16)

    s = _lru_param_specs()
    col = lambda r: pl.BlockSpec((r, HD), lambda h: (0, h))
    return _call(
        body, name="lru_backward", grid=(HEADS,),
        out_shape=[jax.ShapeDtypeStruct((L, D_IN), BF16), jax.ShapeDtypeStruct((LC, D), BF16),
                   jax.ShapeDtypeStruct((2, HEADS, HD, HD), F32), jax.ShapeDtypeStruct((2, HEADS, HD, HD), F32),
                   jax.ShapeDtypeStruct((2, D), F32), jax.ShapeDtypeStruct((2, D), F32),
                   jax.ShapeDtypeStruct((2, D), F32), jax.ShapeDtypeStruct((CONV_W, D), F32),
                   jax.ShapeDtypeStruct((1, D), F32)] + [jax.ShapeDtypeStruct((4,) + a.shape[1:], a.dtype)
                                                          for a in chip_sums] + _chips_stage_shapes(chip_sums),
        in_specs=[s["xa"], s["xac"], col(L), pl.BlockSpec(memory_space=pl.ANY), s["cw"], s["cb"], s["h4"], s["h4"],
                  s["v2"], s["v2"], s["v2"]] + [HBM] * nr,
        out_specs=[col(L), col(LC), s["h4"], s["h4"], s["v2"], s["v2"], s["v2"], col(CONV_W), col(1)]
        + [HBM] * (2 * nr),
        scratch_shapes=[pltpu.VMEM((17, L, HD), F32), pltpu.VMEM((17, LC, HD), F32)] + (_chips_sems(nr) if nr else []),
        input_output_aliases={3: 0},
        compiler_params=pltpu.CompilerParams(dimension_semantics=("arbitrary",), vmem_limit_bytes=VMEM_LIMIT,
                                             has_side_effects=True),
    )(zx, zc, dyl, dz, cw, cb, wa, wx, ba, bx, lam, *[pltpu.with_memory_space_constraint(a, pltpu.HBM)
                                                       for a in chip_sums])


def _mixer_loss(x, tgt, zx, yl, gx, fg, lng, lnb, ws, wst, bst, wout, tm):
    ncht = tm // CHUNK

    def body(x_ref, t_ref, ga_ref, u_ref, v_ref, gb_ref, yl_ref, gx_ref, fg_ref, lng_ref, lnb_ref, ws_ref, wst_ref,
             bst_ref, wout_ref,
             dz_ref, dyl_ref, dxn_ref, y_s, do_ref, dws_ref, dbst_ref, vec_ref,
             vn_s, mix_s, dm_s, dvn_s):
        step = pl.program_id(0)

        @pl.when(step == 0)
        def _():
            dws_ref[...] = jnp.zeros_like(dws_ref)
            dbst_ref[...] = jnp.zeros_like(dbst_ref)
            vec_ref[...] = jnp.zeros_like(vec_ref)

        u, v = u_ref[...], v_ref[...]
        ug, dug_du = _gelu_and_grad(u)
        vg, dvg_dv = _gelu_and_grad(v)
        mu = jnp.mean(vg, axis=-1, keepdims=True)
        vc = vg - mu
        rstd = lax.rsqrt(jnp.mean(vc * vc, axis=-1, keepdims=True) + LN_EPS)
        vhat = vc * rstd
        lngv = lng_ref[...]
        vn_s[...] = (vhat * lngv + lnb_ref[...]).astype(BF16)
        for ch in range(ncht):
            rs = slice(ch * CHUNK, (ch + 1) * CHUNK)
            for g in range(HEADS):
                cs = slice(g * HD, (g + 1) * HD)
                mix_s[rs, cs] = _dot(ws_ref[g], vn_s[rs, cs]) + bst_ref[:, g:g + 1]
        mixed = mix_s[...]
        ga, gb, yl = ga_ref[...], gb_ref[...], yl_ref[...]
        sga, dsga = _silu_and_grad(ga)
        sgb, dsgb = _silu_and_grad(gb)
        ys = ug * mixed
        y_s[:, 0:D] = (yl * sga).astype(BF16)
        y_s[:, D:D_MIX] = (ys * sgb).astype(BF16)
        o = _dot(y_s[...], wout_ref[...])
        gxv, fgv = gx_ref[...], fg_ref[...]
        xn = x_ref[...] + gxv * o
        rs2 = lax.rsqrt(jnp.mean(xn * xn, axis=-1, keepdims=True) + NORM_EPS)
        xh = xn * rs2
        diff = xh * fgv - t_ref[...]
        vec_ref[R_LOSS:R_LOSS + 1, :] += jnp.full((1, D), jnp.sum(diff * diff) * (0.5 / D), F32)
        dout = diff * (1.0 / D)
        w = dout * fgv
        dxn = rs2 * (w - xh * jnp.mean(w * xh, axis=-1, keepdims=True))
        dxn_ref[...] = dxn
        vec_ref[0:1, :] += jnp.sum(dxn * o, axis=0, keepdims=True)
        vec_ref[1:2, :] += jnp.sum(dout * xh, axis=0, keepdims=True)
        dob = (dxn * gxv).astype(BF16)
        do_ref[...] = dob
        dy = _dot_nt(dob, wout_ref[...])
        dya, dyb = dy[:, 0:D], dy[:, D:D_MIX]
        dyl_ref[...] = dya * sga
        dys = dyb * sgb
        dz_ref[:, 0:D] = jnp.zeros((tm, D), BF16)
        dz_ref[:, D:2 * D] = (dya * yl * dsga).astype(BF16)
        dz_ref[:, 2 * D:3 * D] = (dys * mixed * dug_du).astype(BF16)
        dz_ref[:, 4 * D:5 * D] = (dyb * ys * dsgb).astype(BF16)
        dm = dys * ug
        dm_s[...] = dm.astype(BF16)
        for g in range(HEADS):
            cs = slice(g * HD, (g + 1) * HD)
            dbst_ref[:, g:g + 1] += sum(jnp.sum(dm[ch * CHUNK:(ch + 1) * CHUNK, cs], axis=1, keepdims=True)
                                        for ch in range(ncht))
            for ch in range(ncht):
                rs = slice(ch * CHUNK, (ch + 1) * CHUNK)
                dws_ref[g] += _dot_nt(dm_s[rs, cs], vn_s[rs, cs])
                dvn_s[rs, cs] = _dot(wst_ref[g], dm_s[rs, cs])
        dvn = dvn_s[...]
        vec_ref[2:3, :] += jnp.sum(dvn * vhat, axis=0, keepdims=True)
        vec_ref[3:4, :] += jnp.sum(dvn, axis=0, keepdims=True)
        dvh = dvn * lngv
        dvg = rstd * (dvh - jnp.mean(dvh, axis=-1, keepdims=True) - vhat * jnp.mean(dvh * vhat, axis=-1, keepdims=True))
        dz_ref[:, 3 * D:4 * D] = (dvg * dvg_dv).astype(BF16)

    tile = pl.BlockSpec((tm, D), lambda i: (i, 0))
    zcol = lambda n: pl.BlockSpec((tm, D), lambda i: (i, n))
    vec = pl.BlockSpec((1, D), lambda i: (0, 0))
    full = lambda *s: pl.BlockSpec(s, lambda i: (0,) * len(s))
    return _call(
        body, name="mixer_loss", grid=(L // tm,),
        out_shape=[jax.ShapeDtypeStruct((L, D_IN), BF16), jax.ShapeDtypeStruct((L, D), F32),
                   jax.ShapeDtypeStruct((L, D), F32), jax.ShapeDtypeStruct((L, D_MIX), BF16),
                   jax.ShapeDtypeStruct((L, D), BF16),
                   jax.ShapeDtypeStruct((HEADS, CHUNK, CHUNK), F32), jax.ShapeDtypeStruct((CHUNK, HEADS), F32),
                   jax.ShapeDtypeStruct((8, D), F32)],
        in_specs=[tile, tile, zcol(1), zcol(2), zcol(3), zcol(4), tile, pl.BlockSpec((1, D), lambda i: (0, 2)),
                  vec, vec, vec,
                  full(HEADS, CHUNK, CHUNK), full(HEADS, CHUNK, CHUNK), full(CHUNK, HEADS),
                  pl.BlockSpec((D_MIX, D), lambda i: (0, 0), pipeline_mode=pl.Buffered(1))],
        out_specs=[pl.BlockSpec((tm, D_IN), lambda i: (i, 0)), tile, tile,
                   pl.BlockSpec((tm, D_MIX), lambda i: (i, 0)), tile,
                   full(HEADS, CHUNK, CHUNK), full(CHUNK, HEADS), full(8, D)],
        scratch_shapes=[pltpu.VMEM((tm, D), BF16), pltpu.VMEM((tm, D), F32),
                        pltpu.VMEM((tm, D), BF16), pltpu.VMEM((tm, D), F32)],
        compiler_params=_params("arbitrary"),
    )(x, tgt, zx, zx, zx, zx, yl, gx, fg, lng, lnb, ws, wst, bst, wout)


def _grad_w(a, b, a2, b2, tk, name, bw, first, nblocks, split):
    nk = a.shape[0] // tk
    m = a.shape[1]
    with_ctx = a2 is not None
    if split == "cols":
        slots, r, w = nblocks, m, bw // 2
        piece = lambda q, pc: (slice(None), slice(pc * w, (pc + 1) * w))
    else:
        slots, r, w = 4, m // 8, bw
        piece = lambda q, pc: (slice((2 * q + pc) * r, (2 * q + pc + 1) * r), slice(None))

    def body(*refs):
        a_ref, b_ref = refs[:2]
        a2_ref, b2_ref = refs[2:4] if with_ctx else (None, None)
        sums_ref, acc, mine_v, send_v, stage_v, send_sems, recv_sems = refs[4 if with_ctx else 2:]
        n, k = pl.program_id(0), pl.program_id(1)
        x, y, c = lax.axis_index("x"), lax.axis_index("y"), lax.axis_index("c")

        def to_sibling(s):
            return pltpu.make_async_remote_copy(src_ref=send_v.at[s], dst_ref=stage_v.at[s], send_sem=send_sems.at[s],
                                                recv_sem=recv_sems.at[s], device_id=(x, y, 1 - c),
                                                device_id_type=MESH)

        @pl.when(k == 0)
        def _():
            acc[...] = jnp.zeros_like(acc)

        acc[...] += _dot_tn(a_ref[...], b_ref[...])

        if with_ctx:
            @pl.when(jnp.logical_and(k == nk - 1, n == 0))
            def _():
                acc[:, 0:b2_ref.shape[1]] += _dot_tn(a2_ref[...], b2_ref[...])

        def hand_over(s, q):
            for pc in (0, 1):
                @pl.when(c == pc)
                def _(pc=pc):
                    mine_v[s] = acc[piece(q, pc)]
                    send_v[s] = acc[piece(q, 1 - pc)].astype(BF16)
            to_sibling(s).start()

        for i in range(nblocks):
            @pl.when(jnp.logical_and(k == nk - 1, n == i))
            def _(i=i):
                if split == "cols":
                    hand_over(i, 0)
                else:
                    for q in range(4):
                        hand_over(q, q)

        @pl.when(jnp.logical_and(k == nk - 1, n == nblocks - 1))
        def _():
            for s in range(slots):
                to_sibling(s).wait_recv()
                sums_ref[s] = (mine_v[s] + stage_v[s].astype(F32)).astype(BF16)
            for s in range(slots):
                to_sibling(s).wait_send()

    in_specs = [pl.BlockSpec((tk, m), lambda n, k: (k, 0)), pl.BlockSpec((tk, bw), lambda n, k: (k, n + first))]
    args = [a, b]
    if with_ctx:
        in_specs += [pl.BlockSpec(a2.shape, lambda n, k: (0, 0)), pl.BlockSpec(b2.shape, lambda n, k: (0, 0))]
        args += [a2, b2]
    return _call(
        body, name=name, grid=(nblocks, nk), out_shape=jax.ShapeDtypeStruct((slots, r, w), BF16),
        in_specs=in_specs, out_specs=pl.BlockSpec((slots, r, w), lambda n, k: (0, 0, 0)),
        scratch_shapes=[pltpu.VMEM((m, bw), F32), pltpu.VMEM((slots, r, w), F32), pltpu.VMEM((slots, r, w), BF16),
                        pltpu.VMEM((slots, r, w), BF16), pltpu.SemaphoreType.DMA((slots,)),
                        pltpu.SemaphoreType.DMA((slots,))],
        compiler_params=pltpu.CompilerParams(dimension_semantics=("arbitrary", "arbitrary"),
                                             vmem_limit_bytes=VMEM_LIMIT, has_side_effects=True),
    )(*args)


def _grad_rows(xr, dz, w, mod, ng, dres, ncols, tm, name, chip_sums=(), first_chips=None, dests=None):
    rows = xr.shape[0]
    steps = rows // tm
    with_dx = dres is not None
    nr = len(chip_sums)
    dests = [d for d in (dests or [None] * nr)]
    nd = sum(d is not None for d in dests)
    nin = 6 if with_dx else 5
    nout = 2 if with_dx else 1

    def body(*refs):
        if with_dx:
            x_ref, dz_ref, w_ref, sc_ref, ng_ref, dres_ref = refs[:nin]
            dx_ref, vec_ref = refs[nin + nr + nd:nin + nr + nd + nout]
        else:
            x_ref, dz_ref, w_ref, sc_ref, ng_ref = refs[:nin]
            (vec_ref,) = refs[nin + nr + nd:nin + nr + nd + nout]
        if nr:
            o0 = nin + nr + nd + nout
            start, forward, finish = _chips_ops(refs[nin:nin + nr], refs[o0:o0 + nr], refs[o0 + nr:o0 + 2 * nr],
                                                *refs[o0 + 2 * nr:], first_chips=first_chips)
            pl.when(pl.program_id(0) == 0)(start)
            pl.when(pl.program_id(0) == 2)(forward)
            pl.when(pl.program_id(0) == steps - 1)(finish)

        @pl.when(pl.program_id(0) == 0)
        def _():
            vec_ref[...] = jnp.zeros_like(vec_ref)

        dhn = _dot_nt(dz_ref[...], w_ref[...])
        x = x_ref[...]
        rs = lax.rsqrt(jnp.mean(x * x, axis=-1, keepdims=True) + NORM_EPS)
        xh = x * rs
        ngv = ng_ref[...]
        y = xh * ngv
        vec_ref[0:1, :] += jnp.sum(dhn, axis=0, keepdims=True)
        vec_ref[1:2, :] += jnp.sum(dhn * y, axis=0, keepdims=True)
        dy = dhn * (1.0 + sc_ref[...])
        vec_ref[2:3, :] += jnp.sum(dy * xh, axis=0, keepdims=True)
        if with_dx:
            dxh = dy * ngv
            dx_ref[...] = dres_ref[...] + rs * (dxh - xh * jnp.mean(dxh * xh, axis=-1, keepdims=True))

    tile = pl.BlockSpec((tm, D), lambda i: (i, 0))
    vec = pl.BlockSpec((1, D), lambda i: (0, 0))
    in_specs = [tile, pl.BlockSpec((tm, ncols), lambda i: (i, 0)),
                pl.BlockSpec((D, ncols), lambda i: (0, 0), pipeline_mode=pl.Buffered(1)),
                pl.BlockSpec((1, D), lambda i: (0, 1)), vec]
    out_shape = [jax.ShapeDtypeStruct((8, D), F32)]
    out_specs = [pl.BlockSpec((8, D), lambda i: (0, 0))]
    args = [xr, dz, w, mod, ng]
    if with_dx:
        in_specs.append(tile)
        out_shape.insert(0, jax.ShapeDtypeStruct((rows, D), F32))
        out_specs.insert(0, tile)
        args.append(dres)
    aliases = {}
    for j, d in enumerate(dests):
        if d is not None:
            aliases[len(args) + nr + len(aliases)] = len(out_shape) + j
    in_specs += [HBM] * (nr + nd)
    out_specs += [HBM] * (2 * nr)
    out_shape += [jax.ShapeDtypeStruct((4,) + a.shape[1:], a.dtype) for a in chip_sums]
    out_shape += _chips_stage_shapes(chip_sums)
    args += [pltpu.with_memory_space_constraint(a, pltpu.HBM) for a in chip_sums]
    args += [pltpu.with_memory_space_constraint(d, pltpu.HBM) for d in dests if d is not None]
    return _call(body, name=name, grid=(steps,), out_shape=out_shape, in_specs=in_specs, out_specs=out_specs,
                 scratch_shapes=_chips_sems(nr) if nr else [], input_output_aliases=aliases,
                 compiler_params=pltpu.CompilerParams(dimension_semantics=("arbitrary",),
                                                      vmem_limit_bytes=VMEM_LIMIT, has_side_effects=bool(nr)))(*args)


def _adamw(w, g, m, v):
    m = ADAM_B1 * m + (1.0 - ADAM_B1) * g
    v = ADAM_B2 * v + (1.0 - ADAM_B2) * (g * g)
    m_hat = m / (1.0 - ADAM_B1 ** ADAM_STEP)
    v_hat = v / (1.0 - ADAM_B2 ** ADAM_STEP)
    delta = -ADAM_LR * (m_hat / (jnp.sqrt(v_hat) + ADAM_EPS) + ADAM_WD * w)
    return delta, m, v


def _adamw_reduced(parts, w, m, v, tr, name):
    r, n = w.shape
    nparts = parts.shape[0]

    def body(p_ref, w_ref, m_ref, v_ref, g_ref, d_ref, mo_ref, vo_ref):
        g = p_ref[0].astype(F32)
        for i in range(1, nparts):
            g = g + p_ref[i].astype(F32)
        g_ref[...] = g
        d_ref[...], mo_ref[...], vo_ref[...] = _adamw(w_ref[...], g, m_ref[...], v_ref[...])

    tile = pl.BlockSpec((tr, n), lambda i: (i, 0))
    sds = jax.ShapeDtypeStruct((r, n), F32)
    return _call(
        body, name=name, grid=(r // tr,), out_shape=[sds] * 4,
        in_specs=[pl.BlockSpec((nparts, tr, n), lambda i: (0, i, 0)), tile, tile, tile], out_specs=[tile] * 4,
        compiler_params=_params("arbitrary"),
    )(parts, w, m, v)


R_GATE, R_FINAL_G, R_LN_G, R_LN_B, R_LOSS = 0, 1, 2, 3, 4
R_SH_X, R_SC_X, R_NG_X = 5, 6, 7
R_SH_C, R_SC_C, R_NG_C = 8, 9, 10
R_BA, R_BX, R_LAM, R_CW, R_CB, R_SGU_B = 11, 13, 15, 17, 21, 22
PACK_ROWS = 32
MAT_ROWS = 2 * (2 * HEADS * HD) + HEADS * CHUNK


def _reduce_small(vp_all, mat_parts, ada_w, me):
    nloc = ada_w.shape[1]

    def body(me_ref, vp_ref, mp_ref, w_ref, red_ref, mat_ref, dmod_ref, gab_ref, cpart_ref, dmc_s):
        red = vp_ref[0]
        for i in range(1, N_DEV):
            red = red + vp_ref[i]
        mat = mp_ref[0].astype(F32)
        for i in range(1, mp_ref.shape[0]):
            mat = mat + mp_ref[i].astype(F32)
        red_ref[...] = red
        mat_ref[...] = mat
        for e in range(N_DEV):
            dmod_ref[e:e + 1, 0:D] = vp_ref[e, R_SH_X:R_SH_X + 1, :]
            dmod_ref[e:e + 1, D:2 * D] = vp_ref[e, R_SC_X:R_SC_X + 1, :]
            dmod_ref[e:e + 1, 2 * D:3 * D] = vp_ref[e, R_GATE:R_GATE + 1, :]
        dmod_ref[8:9, 0:D] = red[R_SH_C:R_SH_C + 1, :]
        dmod_ref[8:9, D:2 * D] = red[R_SC_C:R_SC_C + 1, :]
        dmod_ref[8:9, 2 * D:3 * D] = jnp.zeros((1, D), F32)
        dmod_ref[9:16, :] = jnp.zeros((7, 3 * D), F32)
        gab_ref[:, 0:D] = red[R_SH_X:R_SH_X + 1, :] + red[R_SH_C:R_SH_C + 1, :]
        gab_ref[:, D:2 * D] = red[R_SC_X:R_SC_X + 1, :] + red[R_SC_C:R_SC_C + 1, :]
        gab_ref[:, 2 * D:3 * D] = red[R_GATE:R_GATE + 1, :]
        dmc_s[...] = jnp.broadcast_to(dmod_ref[8:9, :], (8, 3 * D))
        off = pl.multiple_of(me_ref[0] * nloc, 128)
        cpart_ref[...] = _dot_nt(dmc_s[:, pl.ds(off, nloc)], w_ref[...])

    return _call(
        body, name="reduce_small",
        out_shape=[jax.ShapeDtypeStruct((PACK_ROWS, D), F32), jax.ShapeDtypeStruct(mat_parts.shape[1:], F32),
                   jax.ShapeDtypeStruct((16, 3 * D), F32), jax.ShapeDtypeStruct((1, 3 * D), F32),
                   jax.ShapeDtypeStruct((8, D), F32)],
        in_specs=[pl.BlockSpec(memory_space=pltpu.SMEM), VMEM, VMEM, VMEM], out_specs=[VMEM] * 5,
        scratch_shapes=[pltpu.VMEM((8, 3 * D), F32)], compiler_params=_params(),
    )(me, vp_all, mat_parts, ada_w)


def _adamw_ada(c_all, c_ctx, dmod, w, m, v, me):
    nloc = w.shape[1]

    def body(me_ref, c_ref, cc_ref, dm_ref, w_ref, m_ref, v_ref, g_ref, d_ref, mo_ref, vo_ref):
        off = pl.multiple_of(me_ref[0] * nloc, 128)
        dm = dm_ref[:, pl.ds(off, nloc)]
        sx, _ = _silu_and_grad(c_ref[...])
        sc, _ = _silu_and_grad(cc_ref[...])
        g = _dot_tn(sx, dm[0:8, :]) + _dot_tn(jnp.broadcast_to(sc, (8, D)), dm[8:16, :])
        g_ref[...] = g
        d_ref[...], mo_ref[...], vo_ref[...] = _adamw(w_ref[...], g, m_ref[...], v_ref[...])

    sds = jax.ShapeDtypeStruct(w.shape, F32)
    return _call(
        body, name="adamw_ada_w", out_shape=[sds] * 4,
        in_specs=[pl.BlockSpec(memory_space=pltpu.SMEM)] + [VMEM] * 6, out_specs=[VMEM] * 4,
        compiler_params=_params(),
    )(me, c_all, c_ctx, dmod, w, m, v)


_SMALL = ("c_ctx", "ada_b", "norm_g", "conv_w", "conv_b", "lru_wa", "lru_ba", "lru_wx", "lru_bx", "lru_lambda",
          "sgu_ln_g", "sgu_ln_b", "sgu_w", "sgu_b", "final_g")


def _adamw_small(red, mat, cparts, gab, ws, ms, vs, me):
    n = len(_SMALL)
    nw = 2 * HEADS * HD

    def body(me_ref, red_ref, mat_ref, cp_ref, gab_ref, *refs):
        w_refs, m_refs, v_refs = refs[:n], refs[n:2 * n], refs[2 * n:3 * n]
        outs = refs[3 * n:]
        off = pl.multiple_of(me_ref[0] * HD, 128)

        def row(r, k=1):
            return red_ref[r:r + k, :]

        cc = w_refs[0][...]
        dcc = cp_ref[0, 0:1, :]
        for i in range(1, N_DEV):
            dcc = dcc + cp_ref[i, 0:1, :]
        grads = dict(
            c_ctx=dcc * _silu_and_grad(cc)[1], ada_b=gab_ref[...], norm_g=row(R_NG_X) + row(R_NG_C),
            conv_w=red_ref[R_CW:R_CW + CONV_W, pl.ds(off, HD)], conv_b=row(R_CB),
            lru_wa=mat_ref[0:nw, :], lru_ba=row(R_BA, 2), lru_wx=mat_ref[nw:2 * nw, :], lru_bx=row(R_BX, 2),
            lru_lambda=red_ref[R_LAM:R_LAM + 2, pl.ds(off, HD)], sgu_ln_g=row(R_LN_G), sgu_ln_b=row(R_LN_B),
            sgu_w=mat_ref[2 * nw:MAT_ROWS, :], sgu_b=row(R_SGU_B), final_g=row(R_FINAL_G))
        for j, name in enumerate(_SMALL):
            g = grads[name]
            outs[j][...] = g
            outs[n + j][...], outs[2 * n + j][...], outs[3 * n + j][...] = _adamw(w_refs[j][...], g, m_refs[j][...],
                                                                                 v_refs[j][...])

    sds = [jax.ShapeDtypeStruct(ws[k].shape, F32) for k in _SMALL]
    outs = _call(
        body, name="adamw_small", out_shape=sds * 4,
        in_specs=[pl.BlockSpec(memory_space=pltpu.SMEM)] + [VMEM] * (4 + 3 * n), out_specs=[VMEM] * (4 * n),
        compiler_params=_params(),
    )(me, red, mat, cparts, gab, *[ws[k] for k in _SMALL], *[ms[k] for k in _SMALL], *[vs[k] for k in _SMALL])
    return [dict(zip(_SMALL, outs[i * n:(i + 1) * n])) for i in range(4)]


def kernel(x, c, ctx, c_ctx, ada_w, ada_b, norm_g, w_in, conv_w, conv_b, lru_wa, lru_ba, lru_wx, lru_bx, lru_lambda, sgu_ln_g, sgu_ln_b, sgu_w, sgu_b, w_out, final_g, loss_target, m_c_ctx, m_ada_w, m_ada_b, m_norm_g, m_w_in, m_conv_w, m_conv_b, m_lru_wa, m_lru_ba, m_lru_wx, m_lru_bx, m_lru_lambda, m_sgu_ln_g, m_sgu_ln_b, m_sgu_w, m_sgu_b, m_w_out, m_final_g, v_c_ctx, v_ada_w, v_ada_b, v_norm_g, v_w_in, v_conv_w, v_conv_b, v_lru_wa, v_lru_ba, v_lru_wx, v_lru_bx, v_lru_lambda, v_sgu_ln_g, v_sgu_ln_b, v_sgu_w, v_sgu_b, v_w_out, v_final_g):
    args = dict(locals())
    me_s = 4 * lax.axis_index("x") + 2 * lax.axis_index("y") + lax.axis_index("c")
    me = me_s.astype(jnp.int32).reshape(1)
    xr, ctxr, tgt = x[0], ctx[0], loss_target[0]
    cc = c_ctx.reshape(1, D)
    nw = 2 * HEADS * HD
    view = dict(c_ctx=(1, D), ada_b=(1, 3 * D), norm_g=(1, D), conv_w=(CONV_W, HD), conv_b=(1, D), lru_wa=(nw, HD),
                lru_ba=(2, D), lru_wx=(nw, HD), lru_bx=(2, D), lru_lambda=(2, HD), sgu_ln_g=(1, D), sgu_ln_b=(1, D),
                sgu_w=(HEADS * CHUNK, CHUNK), sgu_b=(1, D), final_g=(1, D))

    zx, hn, w_full, w_out_b, modx, modc, c_all, cw_full, lam_full = _front_project(
        xr, c, cc, ada_w[0], ada_b, norm_g, w_in[0], w_out[0], conv_w[0], lru_lambda[0], me)
    zc, hnc = _project(ctxr, modc, norm_g, w_full, D, LC, "project_ctx")
    ba, bx = lru_ba.reshape(2, D), lru_bx.reshape(2, D)
    yl, wout_all = _lru_forward(zx, zc, cw_full, conv_b, lru_wa[0], lru_wx[0], ba, bx, lam_full, [w_out_b], ["ag"])
    wout_full = wout_all.reshape(D_MIX, D)
    ws_b = sgu_w[0].astype(BF16)
    dz, dyl, dxn, ycat, dob, dws, dbst, mvec = _mixer_loss(
        xr, tgt, zx, yl, modx, final_g.reshape(1, D), sgu_ln_g, sgu_ln_b, ws_b, jnp.swapaxes(ws_b, 1, 2),
        sgu_b[0].T, wout_full, 256)

    wout_sums = _grad_w(ycat, dob, None, None, 1024, "grad_w_out", D, 0, 1, "rows")
    rest_sums = _grad_w(hn, dz, None, None, 1024, "grad_w_in_rest", 2 * W_IN_SHARD, 1, 3, "cols")
    dz, dxac, dwa, dwx, dba, dbx, dlam, dcw, dcb, win_parts, wout_parts, _, _ = _lru_backward(
        zx, zc, dyl, dz, cw_full, conv_b, lru_wa[0], lru_wx[0], ba, bx, lam_full, [rest_sums, wout_sums],
        first_chips=[1, 0])
    first_sums = _grad_w(hn, dz, hnc, dxac, 1024, "grad_w_in_first", 2 * W_IN_SHARD, 0, 1, "cols")
    matpack = jnp.concatenate([dwa.reshape(nw, HD), dwx.reshape(nw, HD), dws.reshape(HEADS * CHUNK, CHUNK)],
                              axis=0).astype(BF16)
    (mat_sums,) = _reduce2_local([matpack.reshape(N_DEV, MAT_ROWS // N_DEV, HD)], ["a2a"], me, "reduce_mat")
    gx, xvec, win_parts, mat_parts, _, _ = _grad_rows(
        xr, dz, w_full, modx, norm_g, dxn, D_IN, 256, "grad_rows_x", chip_sums=[first_sums, mat_sums],
        first_chips=[0, 0], dests=[win_parts, None])
    (cvec,) = _grad_rows(ctxr, dxac, w_full, modc, norm_g, None, D, LC, "grad_rows_ctx")
    pack = jnp.concatenate([mvec[0:5], xvec[0:3], cvec[0:3], dba, dbx, dlam, dcw, dcb, dbst.T.reshape(1, D),
                            jnp.zeros((PACK_ROWS - R_SGU_B - 1, D), F32)], axis=0)
    (vp_all,) = _gather2([pack], ["ag"], "gather_pack")
    red, matpiece, dmod, gab, cpart = _reduce_small(vp_all, mat_parts, ada_w[0], me)
    mat_all, cparts = _gather2([matpiece, cpart], ["ag", "ag"], "gather_small")

    g_w_in, d_w_in, nm_w_in, nv_w_in = _adamw_reduced(win_parts, w_in[0], m_w_in[0], v_w_in[0], 256, "adamw_w_in")
    g_w_out, d_w_out, nm_w_out, nv_w_out = _adamw_reduced(wout_parts, w_out[0], m_w_out[0], v_w_out[0], 128,
                                                          "adamw_w_out")
    g_ada, d_ada, nm_ada, nv_ada = _adamw_ada(c_all, cc, dmod, ada_w[0], m_ada_w[0], v_ada_w[0], me)
    ws = {k: args[k].reshape(view[k]) for k in _SMALL}
    ms = {k: args["m_" + k].reshape(view[k]) for k in _SMALL}
    vs = {k: args["v_" + k].reshape(view[k]) for k in _SMALL}
    small = _adamw_small(red, mat_all.reshape(MAT_ROWS, HD), cparts, gab, ws, ms, vs, me)
    big = dict(w_in=(g_w_in, d_w_in, nm_w_in, nv_w_in), w_out=(g_w_out, d_w_out, nm_w_out, nv_w_out),
               ada_w=(g_ada, d_ada, nm_ada, nv_ada))

    loss = red[R_LOSS, 0]
    names = ("c_ctx", "ada_w", "ada_b", "norm_g", "w_in", "conv_w", "conv_b", "lru_wa", "lru_ba", "lru_wx", "lru_bx",
             "lru_lambda", "sgu_ln_g", "sgu_ln_b", "sgu_w", "sgu_b", "w_out", "final_g")
    outs = [loss, gx.reshape(x.shape)]
    for kind in range(4):
        for k in names:
            val = big[k][kind] if k in big else small[kind][k]
            outs.append(val.reshape(args[k].shape))
    return tuple(outs)
```

```python
import functools

import jax
import jax.numpy as jnp
from jax import lax
from jax.experimental import pallas as pl
from jax.experimental.pallas import tpu as pltpu

F32 = jnp.float32
BF16 = jnp.bfloat16

N_DEV = 8
D = 1024
L = 2048
LC = 256
HEADS = 8
HD = 128
CHUNK = 128
D_IN = 5 * D
W_IN_SHARD = D_IN // N_DEV
ROWS = 256
D_MIX = 2 * D
CONV_W = 4
LRU_C = 8.0
NORM_EPS = 1e-6
LN_EPS = 1e-5
ADAM_LR, ADAM_B1, ADAM_B2, ADAM_EPS, ADAM_WD, ADAM_STEP = 0.001, 0.9, 0.999, 1e-08, 0.01, 10

VMEM_LIMIT = 56 * 1024 * 1024

HBM = pl.BlockSpec(memory_space=pltpu.HBM)
VMEM = pl.BlockSpec(memory_space=pltpu.VMEM)
MESH = pl.DeviceIdType.MESH


def _call(body, **kw):
    return pl.pallas_call(body, **kw)


def _params(*sem):
    return pltpu.CompilerParams(dimension_semantics=sem, vmem_limit_bytes=VMEM_LIMIT)


def _sigmoid(x):
    return 0.5 * jnp.tanh(0.5 * x) + 0.5


def _silu_and_grad(x):
    s = _sigmoid(x)
    return x * s, s * (1.0 + x * (1.0 - s))


_G0 = 0.7978845608028654
_G1 = 0.044715


def _gelu_and_grad(x):
    x2 = x * x
    t = jnp.tanh(_G0 * (x + _G1 * x * x2))
    cdf = 0.5 * (1.0 + t)
    return x * cdf, cdf + 0.5 * x * (1.0 - t * t) * (_G0 * (1.0 + 3.0 * _G1 * x2))


def _gelu(x):
    return 0.5 * x * (1.0 + jnp.tanh(_G0 * (x + _G1 * x * x * x)))


def _softplus(z):
    t = jnp.exp(-jnp.abs(z))
    u = 1.0 + t
    log1p = jnp.where(u == 1.0, t, jnp.log(u) * t / jnp.where(u == 1.0, 1.0, u - 1.0))
    return jnp.maximum(z, 0.0) + log1p


def _dot(a, b):
    return jnp.dot(a, b, preferred_element_type=F32)


def _dot_nt(a, b):
    return lax.dot_general(a, b, (((1,), (1,)), ((), ())), preferred_element_type=F32)


def _dot_tn(a, b):
    return lax.dot_general(a, b, (((0,), (0,)), ((), ())), preferred_element_type=F32)


def _rows(shape):
    return lax.broadcasted_iota(jnp.int32, shape, 0)


def _shift_down(x, first):
    y = pltpu.roll(x, 1, 0)
    head = jnp.where(_rows((8, x.shape[1])) == 0, first, y[0:8])
    return jnp.concatenate([head, y[8:]], axis=0)


def _shift_up(x, last):
    n = x.shape[0]
    y = pltpu.roll(x, n - 1, 0)
    tail = jnp.where(_rows((8, x.shape[1])) == 7, last, y[n - 8:])
    return jnp.concatenate([y[:n - 8], tail], axis=0)


def _gather2(arrays, modes, name):
    n = len(arrays)

    def body(*refs):
        start, forward, finish = _gather2_ops(refs[:n], refs[n:2 * n], modes, *refs[2 * n:])
        start()
        forward()
        finish()

    return _call(
        body, name=name, out_shape=_gather2_shapes(arrays, modes), in_specs=[HBM] * n, out_specs=[HBM] * n,
        scratch_shapes=_gather2_sems(n), compiler_params=pltpu.CompilerParams(has_side_effects=True),
    )(*[pltpu.with_memory_space_constraint(a, pltpu.HBM) for a in arrays])


def _gather2_shapes(arrays, modes):
    return [jax.ShapeDtypeStruct((N_DEV,) + a.shape if m == "ag" else (a.shape[0], N_DEV * a.shape[1]), a.dtype)
            for a, m in zip(arrays, modes)]


def _gather2_sems(n):
    return [pltpu.SemaphoreType.DMA((n, N_DEV - 1)), pltpu.SemaphoreType.DMA((n, N_DEV - 1)),
            pltpu.SemaphoreType.DMA((n,))]


def _gather2_ops(ins, outs, modes, send_sems, recv_sems, local_sems):
    n = len(ins)
    x, y, c = lax.axis_index("x"), lax.axis_index("y"), lax.axis_index("c")
    me, sibling = (x, y, c), (x, y, 1 - c)
    chips = [(x ^ (k >> 1), y ^ (k & 1)) for k in (1, 2, 3)]

    def slot(j, px, py, pc):
        dev = 4 * px + 2 * py + pc
        if modes[j] == "agc":
            w = ins[j].shape[1]
            return outs[j].at[:, pl.ds(pl.multiple_of(dev * w, 128), w)]
        return outs[j].at[dev]

    def copy(j, k, block, to, src=None):
        return pltpu.make_async_remote_copy(
            src_ref=slot(j, *block) if src is None else src, dst_ref=slot(j, *block),
            send_sem=send_sems.at[j, k], recv_sem=recv_sems.at[j, k], device_id=to, device_id_type=MESH)

    def own(j):
        return pltpu.make_async_copy(ins[j], slot(j, *me), local_sems.at[j])

    def first(j):
        return [copy(j, 0, me, sibling, src=ins[j])] + [copy(j, 1 + i, me, (*chip, c), src=ins[j])
                                                        for i, chip in enumerate(chips)]

    def passed(j, i):
        return copy(j, 4 + i, (*chips[i], c), sibling)

    def start():
        for j in range(n):
            own(j).start()
            for cp in first(j):
                cp.start()

    def forward():
        for i, chip in enumerate(chips):
            for j in range(n):
                copy(j, 1 + i, (*chip, c), me).wait_recv()
                passed(j, i).start()

    def finish():
        for j in range(n):
            copy(j, 0, sibling, me).wait_recv()
            for i, chip in enumerate(chips):
                copy(j, 4 + i, (*chip, 1 - c), me).wait_recv()
            for cp in first(j) + [passed(j, i) for i in range(3)]:
                cp.wait_send()
            own(j).wait()

    return start, forward, finish


def _reduce2_local(arrays, modes, me, name, counts=None):
    n = len(arrays)
    counts = counts or [4] * n
    shapes = [(a.shape[1], a.shape[2]) if m == "a2a" else (a.shape[0], a.shape[1] // (2 * cnt))
              for a, m, cnt in zip(arrays, modes, counts)]
    staged = [jax.ShapeDtypeStruct((cnt,) + s, a.dtype) for s, a, cnt in zip(shapes, arrays, counts)]

    def piece(ref, mode, dev, w):
        return ref.at[dev] if mode == "a2a" else ref.at[:, pl.ds(pl.multiple_of(dev * w, 128), w)]

    def to_sibling(*refs):
        ins, outs = refs[:n], refs[n:2 * n]
        send_sems, recv_sems = refs[2 * n:]
        x, y, c = lax.axis_index("x"), lax.axis_index("y"), lax.axis_index("c")
        copies = []
        for j in range(n):
            for q in range(counts[j]):
                cp = pltpu.make_async_remote_copy(
                    src_ref=piece(ins[j], modes[j], 2 * q + (1 - c), shapes[j][1]), dst_ref=outs[j].at[q],
                    send_sem=send_sems.at[j, q], recv_sem=recv_sems.at[j, q], device_id=(x, y, 1 - c),
                    device_id_type=MESH)
                cp.start()
                copies.append(cp)
        for cp in copies:
            cp.wait()

    stage = _call(
        to_sibling, name=name + "_d2d", out_shape=staged, in_specs=[HBM] * n, out_specs=[HBM] * n,
        scratch_shapes=[pltpu.SemaphoreType.DMA((n, 4)), pltpu.SemaphoreType.DMA((n, 4))],
        compiler_params=pltpu.CompilerParams(has_side_effects=True),
    )(*[pltpu.with_memory_space_constraint(a, pltpu.HBM) for a in arrays])

    def add(me_ref, *refs):
        del me_ref
        own, got, outs = refs[:n], refs[n:2 * n], refs[2 * n:]
        for j in range(n):
            mine = own[j][0] if modes[j] == "a2a" else own[j][...]
            outs[j][0] = (mine.astype(F32) + got[j][0].astype(F32)).astype(outs[j].dtype)

    in_specs, slot_specs = [], []
    for (r, w), m, cnt in zip(shapes, modes, counts):
        if m == "a2a":
            in_specs.append(pl.BlockSpec(
                (1, r, w), lambda q, me_ref, cnt=cnt: (2 * jnp.minimum(q, cnt - 1) + me_ref[0] % 2, 0, 0)))
        else:
            in_specs.append(pl.BlockSpec(
                (r, w), lambda q, me_ref, cnt=cnt: (0, 2 * jnp.minimum(q, cnt - 1) + me_ref[0] % 2)))
        slot_specs.append(pl.BlockSpec((1, r, w), lambda q, me_ref, cnt=cnt: (jnp.minimum(q, cnt - 1), 0, 0)))
    return _call(
        add, name=name + "_add", out_shape=staged,
        grid_spec=pltpu.PrefetchScalarGridSpec(num_scalar_prefetch=1, grid=(max(counts),),
                                               in_specs=in_specs + slot_specs, out_specs=slot_specs),
        compiler_params=_params("arbitrary"),
    )(me, *arrays, *stage)


def _chips_sems(n):
    return [pltpu.SemaphoreType.DMA((n, 6)), pltpu.SemaphoreType.DMA((n, 6)), pltpu.SemaphoreType.DMA((n,))]


def _chips_stage_shapes(chip_sums):
    return [jax.ShapeDtypeStruct((2, a.shape[1] // 2, a.shape[2]), a.dtype) for a in chip_sums]


def _chips_ops(ins, outs, stages, send_sems, recv_sems, local_sems, first_chips=None):
    x, y, c = lax.axis_index("x"), lax.axis_index("y"), lax.axis_index("c")
    qm = 2 * x + y
    first_chips = first_chips or [0] * len(ins)

    def owns(j, chip):
        lo, cnt = first_chips[j], ins[j].shape[0]
        if lo == 0 and cnt == 4:
            return None
        return jnp.logical_and(chip >= lo, chip < lo + cnt)

    def guarded(cond, fn):
        if cond is None:
            fn()
        else:
            pl.when(cond)(fn)

    def slot(j, chip):
        return jnp.clip(chip - first_chips[j], 0, ins[j].shape[0] - 1)

    def half(j, i):
        h = ins[j].shape[1] // 2
        return pl.ds(i * h, h)

    def copy(j, sem, src, dst, k):
        return pltpu.make_async_remote_copy(
            src_ref=src, dst_ref=dst, send_sem=send_sems.at[j, sem], recv_sem=recv_sems.at[j, sem],
            device_id=(x ^ (k >> 1), y ^ (k & 1), c), device_id_type=MESH)

    def direct(j, k):
        return copy(j, k - 1, ins[j].at[slot(j, qm ^ k)], outs[j].at[qm], k)

    def first_hop(j, k):
        return copy(j, 1 + k, ins[j].at[slot(j, qm ^ 3), half(j, k - 1)], stages[j].at[k - 1], k)

    def second_hop(j, k):
        return copy(j, 3 + k, stages[j].at[2 - k], outs[j].at[qm ^ (3 - k), half(j, 2 - k)], k)

    def local(j):
        return pltpu.make_async_copy(ins[j].at[slot(j, qm)], outs[j].at[qm], local_sems.at[j])

    def start():
        for j in range(len(ins)):
            for k in (1, 2):
                guarded(owns(j, qm ^ 3), lambda j=j, k=k: first_hop(j, k).start())
        for j in range(len(ins)):
            for k in (1, 2):
                guarded(owns(j, qm ^ k), lambda j=j, k=k: direct(j, k).start())
            guarded(owns(j, qm), lambda j=j: local(j).start())

    def forward():
        for j in range(len(ins)):
            for k in (1, 2):
                def pass_on(j=j, k=k):
                    first_hop(j, 3 - k).wait_recv()
                    second_hop(j, k).start()
                guarded(owns(j, qm ^ k), pass_on)

    def finish():
        for j in range(len(ins)):
            for k in (1, 2):
                guarded(owns(j, qm ^ k), lambda j=j, k=k: direct(j, k).wait_send())
                guarded(owns(j, qm ^ k), lambda j=j, k=k: second_hop(j, k).wait_send())
                guarded(owns(j, qm ^ 3), lambda j=j, k=k: first_hop(j, k).wait_send())
                guarded(owns(j, qm), lambda j=j, k=k: direct(j, k).wait_recv())
                guarded(owns(j, qm), lambda j=j, k=k: second_hop(j, k).wait_recv())
            guarded(owns(j, qm), lambda j=j: local(j).wait())

    return start, forward, finish


def _front(c, c_ctx, ada_w, ada_b, w_in, w_out, me):
    nloc = ada_w.shape[1]

    def body(me_ref, c_ref, cc_ref, aw_ref, ab_ref, win_ref, wout_ref,
             wfull_ref, woutb_ref, modx_ref, modc_ref, call_ref,
             wb_s, part_s, parts_s, w_send, w_recv, w_local, s_send, s_recv):
        x, y, cidx = lax.axis_index("x"), lax.axis_index("y"), lax.axis_index("c")
        me = me_ref[0]
        wb_s[...] = win_ref[...].astype(BF16)
        woutb_ref[...] = wout_ref[...].astype(BF16)
        start, forward, finish = _gather2_ops([wb_s], [wfull_ref], ["agc"], w_send, w_recv, w_local)
        start()

        def small_gather(src, my_slot, stage):
            copies = []
            for k in range(1, N_DEV):
                peer = (x ^ (k >> 2), y ^ ((k >> 1) & 1), cidx ^ (k & 1))
                cp = pltpu.make_async_remote_copy(src_ref=src, dst_ref=my_slot, send_sem=s_send.at[stage, k - 1],
                                                  recv_sem=s_recv.at[stage, k - 1], device_id=peer,
                                                  device_id_type=MESH)
                cp.start()
                copies.append(cp)
            pltpu.sync_copy(src, my_slot)
            for cp in copies:
                cp.wait()

        small_gather(c_ref, call_ref.at[pl.ds(me, 1), :], 0)
        off = pl.multiple_of(me * nloc, 128)
        b = ab_ref[:, pl.ds(off, nloc)]
        w = aw_ref[...]
        sx, _ = _silu_and_grad(call_ref[...])
        sc, _ = _silu_and_grad(jnp.broadcast_to(cc_ref[...], (8, D)))
        part_s[0:8, :] = _dot(sx, w) + b
        part_s[8:16, :] = _dot(sc, w) + b
        small_gather(part_s, parts_s.at[me], 1)
        mine = _rows((16, nloc)) == me
        for j in range(N_DEV):
            pj = parts_s[j]
            modx_ref[:, j * nloc:(j + 1) * nloc] = jnp.sum(jnp.where(mine, pj, 0.0), axis=0, keepdims=True)
            modc_ref[:, j * nloc:(j + 1) * nloc] = pj[8:9, :]
        forward()
        finish()

    return _call(
        body, name="front",
        out_shape=[jax.ShapeDtypeStruct((D, D_IN), BF16), jax.ShapeDtypeStruct(w_out.shape, BF16),
                   jax.ShapeDtypeStruct((1, 3 * D), F32), jax.ShapeDtypeStruct((1, 3 * D), F32),
                   jax.ShapeDtypeStruct((N_DEV, D), F32)],
        in_specs=[pl.BlockSpec(memory_space=pltpu.SMEM)] + [VMEM] * 6, out_specs=[HBM, VMEM, VMEM, VMEM, VMEM],
        scratch_shapes=[pltpu.VMEM(w_in.shape, BF16), pltpu.VMEM((16, nloc), F32),
                        pltpu.VMEM((N_DEV, 16, nloc), F32)] + _gather2_sems(1) +
                       [pltpu.SemaphoreType.DMA((2, N_DEV - 1)), pltpu.SemaphoreType.DMA((2, N_DEV - 1))],
        compiler_params=pltpu.CompilerParams(vmem_limit_bytes=VMEM_LIMIT, has_side_effects=True),
    )(me, c, c_ctx, ada_w, ada_b, w_in, w_out)


ARRIVAL = (0, 1, 2, 4, 3, 5, 6, 7)


def _front_project(xr, c, c_ctx, ada_w, ada_b, ng, w_in, w_out, cw, lam, me):
    nloc = ada_w.shape[1]
    ws = W_IN_SHARD
    order = me[0] ^ jnp.asarray(ARRIVAL, jnp.int32)

    def body(ord_ref, x_ref, c_ref, cc_ref, aw_ref, ab_ref, ng_ref, win_ref, wout_ref, cw_ref, lam_ref,
             z_ref, hn_ref, wfull_ref, woutb_ref, modx_ref, modc_ref, call_ref, cwf_ref, lamf_ref,
             wv, call_s, part_s, parts_s, w_send, w_recv, hbm_sems, s_send, s_recv, g_send, g_recv, g_local):
        t = pl.program_id(0)
        x, y, cidx = lax.axis_index("x"), lax.axis_index("y"), lax.axis_index("c")
        me_i = ord_ref[0]
        sibling = (x, y, 1 - cidx)
        chips = [(x ^ (k >> 1), y ^ (k & 1)) for k in (1, 2, 3)]
        g_start, g_pass, g_finish = _gather2_ops([cw_ref, lam_ref], [cwf_ref, lamf_ref], ["agc", "agc"],
                                                 g_send, g_recv, g_local)

        def shard_copy(k, px, py, pc, to):
            slot = wv.at[4 * px + 2 * py + pc]
            return pltpu.make_async_remote_copy(src_ref=slot, dst_ref=slot, send_sem=w_send.at[k],
                                                recv_sem=w_recv.at[k], device_id=to, device_id_type=MESH)

        def small_gather(src, my_slot, stage):
            copies = []
            for k in range(1, N_DEV):
                peer = (x ^ (k >> 2), y ^ ((k >> 1) & 1), cidx ^ (k & 1))
                cp = pltpu.make_async_remote_copy(src_ref=src, dst_ref=my_slot, send_sem=s_send.at[stage, k - 1],
                                                  recv_sem=s_recv.at[stage, k - 1], device_id=peer,
                                                  device_id_type=MESH)
                cp.start()
                copies.append(cp)
            pltpu.sync_copy(src, my_slot)
            for cp in copies:
                cp.wait()

        @pl.when(t == 0)
        def _():
            g_start()
            wv[me_i] = win_ref[...].astype(BF16)
            woutb_ref[...] = wout_ref[...].astype(BF16)
            shard_copy(0, x, y, cidx, sibling).start()
            small_gather(c_ref, call_s.at[pl.ds(me_i, 1), :], 0)
            call_ref[...] = call_s[...]
            off = pl.multiple_of(me_i * nloc, 128)
            b = ab_ref[:, pl.ds(off, nloc)]
            w = aw_ref[...]
            sx, _ = _silu_and_grad(call_s[...])
            sc, _ = _silu_and_grad(jnp.broadcast_to(cc_ref[...], (8, D)))
            part_s[0:8, :] = _dot(sx, w) + b
            part_s[8:16, :] = _dot(sc, w) + b
            small_gather(part_s, parts_s.at[me_i], 1)
            for i, chip in enumerate(chips):
                shard_copy(1 + i, x, y, cidx, (*chip, cidx)).start()
            mine = _rows((16, nloc)) == me_i
            for j in range(N_DEV):
                pj = parts_s[j]
                modx_ref[:, j * nloc:(j + 1) * nloc] = jnp.sum(jnp.where(mine, pj, 0.0), axis=0, keepdims=True)
                modc_ref[:, j * nloc:(j + 1) * nloc] = pj[8:9, :]
            shift, scale1, ngv = modx_ref[:, 0:D], 1.0 + modx_ref[:, D:2 * D], ng_ref[...]
            for r in range(L // ROWS):
                rsl = slice(r * ROWS, (r + 1) * ROWS)
                xv = x_ref[rsl, :]
                rs = lax.rsqrt(jnp.mean(xv * xv, axis=-1, keepdims=True) + NORM_EPS)
                hn_ref[rsl, :] = ((xv * rs * ngv) * scale1 + shift).astype(BF16)

        @pl.when(t == 1)
        def _():
            shard_copy(0, x, y, 1 - cidx, sibling).wait_recv()
            g_pass()

        for i, chip in enumerate(chips):
            @pl.when(t == ARRIVAL.index((2, 4, 6)[i]))
            def _(i=i, chip=chip):
                shard_copy(1 + i, *chip, cidx, sibling).wait_recv()
                shard_copy(4 + i, *chip, cidx, sibling).start()

            @pl.when(t == ARRIVAL.index((3, 5, 7)[i]))
            def _(i=i, chip=chip):
                shard_copy(4 + i, *chip, 1 - cidx, sibling).wait_recv()

        @pl.when(t == 2)
        def _():
            g_finish()

        dev = ord_ref[t]
        for r in range(L // (2 * ROWS)):
            rsl = slice(r * 2 * ROWS, (r + 1) * 2 * ROWS)
            z_ref[rsl, :] = _dot(hn_ref[rsl, :], wv[dev])
        col = pl.ds(pl.multiple_of(dev * ws, 128), ws)
        pltpu.make_async_copy(wv.at[dev], wfull_ref.at[:, col], hbm_sems.at[t]).start()

        @pl.when(t == N_DEV - 1)
        def _():
            for k in range(7):
                shard_copy(k, x, y, cidx, sibling).wait_send()
            for s in range(N_DEV):
                pltpu.make_async_copy(wv.at[0], wfull_ref.at[:, pl.ds(0, ws)], hbm_sems.at[s]).wait()

    const = lambda *shape: pl.BlockSpec(shape, lambda t, o: (0,) * len(shape))
    once = lambda *shape: pl.BlockSpec(shape, lambda t, o: (0,) * len(shape), pipeline_mode=pl.Buffered(1))
    return _call(
        body, name="front_project",
        out_shape=[jax.ShapeDtypeStruct((L, D_IN), F32), jax.ShapeDtypeStruct((L, D), BF16),
                   jax.ShapeDtypeStruct((D, D_IN), BF16), jax.ShapeDtypeStruct(w_out.shape, BF16),
                   jax.ShapeDtypeStruct((1, 3 * D), F32), jax.ShapeDtypeStruct((1, 3 * D), F32),
                   jax.ShapeDtypeStruct((N_DEV, D), F32), jax.ShapeDtypeStruct((CONV_W, D), F32),
                   jax.ShapeDtypeStruct((2, D), F32)],
        grid_spec=pltpu.PrefetchScalarGridSpec(
            num_scalar_prefetch=1, grid=(N_DEV,),
            in_specs=[once(L, D), const(1, D), const(1, D), once(D, nloc), const(1, 3 * D), const(1, D),
                      once(D, ws), once(*w_out.shape), HBM, HBM],
            out_specs=[pl.BlockSpec((L, ws), lambda t, o: (0, o[t])), const(L, D), HBM, const(*w_out.shape),
                       const(1, 3 * D), const(1, 3 * D), const(N_DEV, D), HBM, HBM],
            scratch_shapes=[pltpu.VMEM((N_DEV, D, ws), BF16), pltpu.VMEM((N_DEV, D), F32), pltpu.VMEM((16, nloc), F32),
                            pltpu.VMEM((N_DEV, 16, nloc), F32), pltpu.SemaphoreType.DMA((7,)),
                            pltpu.SemaphoreType.DMA((7,)), pltpu.SemaphoreType.DMA((N_DEV,)),
                            pltpu.SemaphoreType.DMA((2, N_DEV - 1)), pltpu.SemaphoreType.DMA((2, N_DEV - 1))]
            + _gather2_sems(2)),
        compiler_params=pltpu.CompilerParams(dimension_semantics=("arbitrary",), vmem_limit_bytes=VMEM_LIMIT,
                                             has_side_effects=True),
    )(order, xr, c, c_ctx, ada_w, ada_b, ng, w_in, w_out, pltpu.with_memory_space_constraint(cw, pltpu.HBM),
      pltpu.with_memory_space_constraint(lam, pltpu.HBM))


def _project(xr, mod, ng, w, ncols, tm, name, gather=None, gather_modes=()):
    rows = xr.shape[0]
    steps = rows // tm
    ng_ = len(gather or ())

    def body(x_ref, sh_ref, sc_ref, ng_ref, w_ref, *rest):
        z_ref, hn_ref = rest[ng_:ng_ + 2]
        if ng_:
            start, forward, finish = _gather2_ops(rest[:ng_], rest[ng_ + 2:2 * ng_ + 2], gather_modes,
                                                  *rest[2 * ng_ + 2:])
            pl.when(pl.program_id(0) == 0)(start)
            pl.when(pl.program_id(0) == steps // 2)(forward)
        x = x_ref[...]
        rs = lax.rsqrt(jnp.mean(x * x, axis=-1, keepdims=True) + NORM_EPS)
        hn = (x * rs * ng_ref[...]) * (1.0 + sc_ref[...]) + sh_ref[...]
        hb = hn.astype(BF16)
        hn_ref[...] = hb
        for n in range(ncols // D):
            z_ref[:, n * D:(n + 1) * D] = _dot(hb, w_ref[:, n * D:(n + 1) * D])
        if ng_:
            pl.when(pl.program_id(0) == steps - 1)(finish)

    vec = pl.BlockSpec((1, D), lambda i: (0, 0))
    gathered = _gather2_shapes(gather, gather_modes) if ng_ else []
    return _call(
        body, name=name, grid=(steps,),
        out_shape=[jax.ShapeDtypeStruct((rows, ncols), F32), jax.ShapeDtypeStruct((rows, D), BF16)] + gathered,
        in_specs=[pl.BlockSpec((tm, D), lambda i: (i, 0)), vec, pl.BlockSpec((1, D), lambda i: (0, 1)), vec,
                  pl.BlockSpec((D, ncols), lambda i: (0, 0), pipeline_mode=pl.Buffered(1))] + [HBM] * ng_,
        out_specs=[pl.BlockSpec((tm, ncols), lambda i: (i, 0)), pl.BlockSpec((tm, D), lambda i: (i, 0))] + [HBM] * ng_,
        scratch_shapes=_gather2_sems(ng_) if ng_ else [],
        compiler_params=pltpu.CompilerParams(dimension_semantics=("arbitrary",), vmem_limit_bytes=VMEM_LIMIT,
                                             has_side_effects=bool(ng_)),
    )(xr, mod, mod, ng, w, *[pltpu.with_memory_space_constraint(a, pltpu.HBM) for a in gather or ()])


def _scan_pair(af_ref, uf_ref, hf_ref, h0f, ab_ref, ub_ref, hb_ref, h0b, t_len):
    span = 8 * SCAN_BLOCKS
    nit = t_len // span
    rows = _rows((8, HD))

    def local_scan(a, b, forward):
        for s in (1, 2, 4):
            sh = s if forward else 8 - s
            m = rows >= s if forward else rows < 8 - s
            b = a * jnp.where(m, pltpu.roll(b, sh, 0), 0.0) + b
            a = a * jnp.where(m, pltpu.roll(a, sh, 0), 1.0)
        return a, b

    def span_scan(a_ref, u_ref, h_ref, off, carry, forward):
        order = range(SCAN_BLOCKS) if forward else range(SCAN_BLOCKS - 1, -1, -1)
        last = slice(7, 8) if forward else slice(0, 1)
        for q in order:
            rs = pl.ds(off + 8 * q, 8)
            a, b = local_scan(a_ref[rs, :], u_ref[rs, :], forward)
            h_ref[rs, :] = b + a * carry
            carry = a[last, :] * carry + b[last, :]
        return carry

    def body(k, carry):
        cf, cb = carry
        cf = span_scan(af_ref, uf_ref, hf_ref, pl.multiple_of(k * span, span), cf, True)
        cb = span_scan(ab_ref, ub_ref, hb_ref, pl.multiple_of((nit - 1 - k) * span, span), cb, False)
        return cf, cb

    return lax.fori_loop(0, nit, body, (h0f, h0b))


SCAN_BLOCKS = 8


def _shifted(pad_ref, x, offsets, before=0.0, after=0.0):
    n = x.shape[0]
    pad_ref[0:8, :] = jnp.broadcast_to(jnp.asarray(before, F32), (8, x.shape[1]))
    pad_ref[8:8 + n, :] = x
    pad_ref[8 + n:16 + n, :] = jnp.broadcast_to(jnp.asarray(after, F32), (8, x.shape[1]))
    return [pad_ref[8 + o:8 + o + n, :] for o in offsets]


def _conv(xa, cw, cb, pad_ref):
    xm1, xp1, xp2 = _shifted(pad_ref, xa, (-1, 1, 2))
    return xm1 * cw[0:1, :] + xa * cw[1:2, :] + xp1 * cw[2:3, :] + xp2 * cw[3:4, :] + cb


def _gates(xc, wa, wx, ba, bx, nsp):
    xb = xc.astype(BF16)
    r = _sigmoid(_dot(xb, wa) + ba)
    i = _sigmoid(_dot(xb, wx) + bx)
    log_a = r * nsp
    a = jnp.exp(log_a)
    g2 = jnp.tanh(log_a) * (-1.0 - a * a)
    rg = lax.rsqrt(jnp.maximum(g2, 1e-30))
    return r, i, a, g2 * rg, rg


def _lru_param_specs():
    h4 = pl.BlockSpec((2, 1, HD, HD), lambda h: (0, h, 0, 0))
    v2 = pl.BlockSpec((2, HD), lambda h: (0, h))
    return dict(
        xa=pl.BlockSpec((L, HD), lambda h: (0, h)), xac=pl.BlockSpec((LC, HD), lambda h: (0, h)),
        cw=pl.BlockSpec((CONV_W, HD), lambda h: (0, h)), cb=pl.BlockSpec((1, HD), lambda h: (0, h)), h4=h4, v2=v2)


def _lru_forward(zx, zc, cw, cb, wa, wx, ba, bx, lam, gather, gather_modes):
    ng_ = len(gather)

    def body(xa_ref, xac_ref, cw_ref, cb_ref, wa_ref, wx_ref, ba_ref, bx_ref, lam_ref, *rest):
        yl_ref = rest[ng_]
        af, uf, hf, ab, ub, hb, pad_s = rest[2 * ng_ + 1:2 * ng_ + 8]
        start, pass_on, finish = _gather2_ops(rest[:ng_], rest[ng_ + 1:2 * ng_ + 1], gather_modes,
                                              *rest[2 * ng_ + 8:])
        pl.when(pl.program_id(0) == 0)(start)
        pl.when(pl.program_id(0) == HEADS // 2)(pass_on)
        pl.when(pl.program_id(0) == HEADS - 1)(finish)
        cwv, cbv = cw_ref[...], cb_ref[...]
        nsp = (-LRU_C) * _softplus(-lam_ref[...])

        def forward(xa, t_len, h0f, h0b):
            xc = _conv(xa, cwv, cbv, pad_s)
            for d, (a_ref, u_ref) in enumerate(((af, uf), (ab, ub))):
                _, i, a, gamma, _ = _gates(xc, wa_ref[d, 0].astype(BF16), wx_ref[d, 0].astype(BF16),
                                           ba_ref[d:d + 1, :], bx_ref[d:d + 1, :], nsp[d:d + 1, :])
                a_ref[0:t_len, :] = a
                u_ref[0:t_len, :] = gamma * (i * xc)
            return _scan_pair(af, uf, hf, h0f, ab, ub, hb, h0b, t_len)

        z = jnp.zeros((1, HD), F32)
        h0f, h0b = forward(xac_ref[...], LC, z, z)
        forward(xa_ref[...], L, h0f, h0b)
        yl_ref[...] = hf[...] + hb[...]

    s = _lru_param_specs()
    return _call(
        body, name="lru_forward", grid=(HEADS,),
        out_shape=[jax.ShapeDtypeStruct((L, D), F32)] + _gather2_shapes(gather, gather_modes),
        in_specs=[s["xa"], s["xac"], s["cw"], s["cb"], s["h4"], s["h4"], s["v2"], s["v2"], s["v2"]] + [HBM] * ng_,
        out_specs=[pl.BlockSpec((L, HD), lambda h: (0, h))] + [HBM] * ng_,
        scratch_shapes=[pltpu.VMEM((L, HD), F32)] * 6 + [pltpu.VMEM((L + 16, HD), F32)] + _gather2_sems(ng_),
        compiler_params=pltpu.CompilerParams(dimension_semantics=("arbitrary",), vmem_limit_bytes=VMEM_LIMIT,
                                             has_side_effects=True),
    )(zx, zc, cw, cb, wa, wx, ba, bx, lam, *[pltpu.with_memory_space_constraint(a, pltpu.HBM) for a in gather])


def _lru_backward(zx, zc, dyl, dz, cw, cb, wa, wx, ba, bx, lam, chip_sums, first_chips=None):
    nr = len(chip_sums)

    def body(xa_ref, xac_ref, dyl_ref, dz_in, cw_ref, cb_ref, wa_ref, wx_ref, ba_ref, bx_ref, lam_ref, *rest):
        (dxa_ref, dxac_ref, dwa_ref, dwx_ref, dba_ref, dbx_ref, dlam_ref, dcw_ref,
         dcb_ref) = rest[nr:nr + 9]
        main_s, ctx_s, pad_s = rest[3 * nr + 9:3 * nr + 12]
        if nr:
            start, forward, finish = _chips_ops(rest[:nr], rest[nr + 9:2 * nr + 9], rest[2 * nr + 9:3 * nr + 9],
                                                *rest[3 * nr + 12:], first_chips=first_chips)
            pl.when(pl.program_id(0) == 0)(start)
            pl.when(pl.program_id(0) == HEADS // 2)(forward)
            pl.when(pl.program_id(0) == HEADS - 1)(finish)
        del dz_in
        cwv, cbv = cw_ref[...], cb_ref[...]
        lamv = lam_ref[...]
        sp = _softplus(-lamv)
        nsp = (-LRU_C) * sp
        z = jnp.zeros((1, HD), F32)

        def wmat(ref, d):
            return ref[d, 0].astype(BF16)

        def workspace(s):
            return dict(a=(s.at[0], s.at[1]), u=(s.at[2], s.at[3]), h=(s.at[4], s.at[5]), rho=(s.at[6], s.at[7]),
                        saved=(tuple(s.at[8 + k] for k in range(4)), tuple(s.at[12 + k] for k in range(4))),
                        xc=s.at[16])

        def forward(ws, xa, t_len, h0f, h0b):
            xc = _conv(xa, cwv, cbv, pad_s)
            ws["xc"][...] = xc
            for d in (0, 1):
                vals = _gates(xc, wmat(wa_ref, d), wmat(wx_ref, d), ba_ref[d:d + 1, :], bx_ref[d:d + 1, :],
                              nsp[d:d + 1, :])
                r, i, a, gamma, rg = vals
                ws["a"][d][...] = a
                ws["u"][d][...] = gamma * (i * xc)
                for ref, val in zip(ws["saved"][d], (r, i, gamma, rg)):
                    ref[...] = val
            return _scan_pair(ws["a"][0], ws["u"][0], ws["h"][0], h0f, ws["a"][1], ws["u"][1], ws["h"][1], h0b,
                              t_len)

        def backward(ws, xa, t_len, h0f, h0b, dhf, dhb, first):
            xc = ws["xc"][...]
            (af, ab), (uf, ub), (hf, hb), (rf, rb) = ws["a"], ws["u"], ws["h"], ws["rho"]
            uf[...] = ab[...] * dhb
            ub[...] = af[...] * dhf
            rho_b_last, rho_f_first = _scan_pair(ab, uf, rb, z, af, ub, rf, z, t_len)
            dxc = jnp.zeros((t_len, HD), F32)
            dsp = []
            for d in (0, 1):
                r, i, gamma, rg = (ref[...] for ref in ws["saved"][d])
                a = ws["a"][d][...]
                if d == 0:
                    lam_t = dhf + _shifted(pad_s, rf[...], (1,))[0]
                    h_prev = _shifted(pad_s, hf[...], (-1,), before=h0f)[0]
                else:
                    lam_t = dhb + _shifted(pad_s, rb[...], (-1,))[0]
                    h_prev = _shifted(pad_s, hb[...], (1,), after=h0b)[0]
                da = lam_t * h_prev
                lx = lam_t * xc
                d_i = lx * gamma
                d_gamma = lx * i
                dxc = dxc + lam_t * (gamma * i)
                d_log_a = a * (da - d_gamma * (a * rg))
                dsp.append(jnp.sum(d_log_a * r, axis=0, keepdims=True) * (-LRU_C))
                d_pre_r = d_log_a * nsp[d:d + 1, :] * (r * (1.0 - r))
                d_pre_i = d_i * (i * (1.0 - i))
                prb, pib, xb = d_pre_r.astype(BF16), d_pre_i.astype(BF16), xc.astype(BF16)
                dxc = dxc + _dot_nt(prb, wmat(wa_ref, d)) + _dot_nt(pib, wmat(wx_ref, d))
                g_wa, g_wx = _dot_tn(xb, prb), _dot_tn(xb, pib)
                g_ba = jnp.sum(d_pre_r, axis=0, keepdims=True)
                g_bx = jnp.sum(d_pre_i, axis=0, keepdims=True)
                if first:
                    dwa_ref[d, 0] = g_wa
                    dwx_ref[d, 0] = g_wx
                    dba_ref[d:d + 1, :] = g_ba
                    dbx_ref[d:d + 1, :] = g_bx
                else:
                    dwa_ref[d, 0] += g_wa
                    dwx_ref[d, 0] += g_wx
                    dba_ref[d:d + 1, :] += g_ba
                    dbx_ref[d:d + 1, :] += g_bx
            g_lam = jnp.concatenate(dsp, axis=0) * (-_sigmoid(-lamv))
            dm1, dp1, dm2 = _shifted(pad_s, dxc, (-1, 1, -2))
            dxa = dp1 * cwv[0:1, :] + dxc * cwv[1:2, :] + dm1 * cwv[2:3, :] + dm2 * cwv[3:4, :]
            xm1, xp1, xp2 = _shifted(pad_s, xa, (-1, 1, 2))
            g_cw = jnp.concatenate([jnp.sum(dxc * v, axis=0, keepdims=True) for v in (xm1, xa, xp1, xp2)], axis=0)
            g_cb = jnp.sum(dxc, axis=0, keepdims=True)
            if first:
                dlam_ref[...] = g_lam
                dcw_ref[...] = g_cw
                dcb_ref[...] = g_cb
            else:
                dlam_ref[...] += g_lam
                dcw_ref[...] += g_cw
                dcb_ref[...] += g_cb
            return dxa, rho_f_first, rho_b_last

        ws_x, ws_c = workspace(main_s), workspace(ctx_s)
        h0f, h0b = forward(ws_c, xac_ref[...], LC, z, z)
        forward(ws_x, xa_ref[...], L, h0f, h0b)
        dh = dyl_ref[...]
        dxa, dh0f, dh0b = backward(ws_x, xa_ref[...], L, h0f, h0b, dh, dh, True)
        dxa_ref[...] = dxa.astype(BF16)
        rc = _rows((LC, HD))
        dxac, _, _ = backward(ws_c, xac_ref[...], LC, z, z, jnp.where(rc == LC - 1, dh0f, 0.0),
                              jnp.where(rc == 0, dh0b, 0.0), False)
        dxac_ref[...] = dxac.astype(BF16)

    s = _lru_param_specs()
    col = lambda r: pl.BlockSpec((r, HD), lambda h: (0, h))
    return _call(
        body, name="lru_backward", grid=(HEADS,),
        out_shape=[jax.ShapeDtypeStruct((L, D_IN), BF16), jax.ShapeDtypeStruct((LC, D), BF16),
                   jax.ShapeDtypeStruct((2, HEADS, HD, HD), F32), jax.ShapeDtypeStruct((2, HEADS, HD, HD), F32),
                   jax.ShapeDtypeStruct((2, D), F32), jax.ShapeDtypeStruct((2, D), F32),
                   jax.ShapeDtypeStruct((2, D), F32), jax.ShapeDtypeStruct((CONV_W, D), F32),
                   jax.ShapeDtypeStruct((1, D), F32)] + [jax.ShapeDtypeStruct((4,) + a.shape[1:], a.dtype)
                                                          for a in chip_sums] + _chips_stage_shapes(chip_sums),
        in_specs=[s["xa"], s["xac"], col(L), pl.BlockSpec(memory_space=pl.ANY), s["cw"], s["cb"], s["h4"], s["h4"],
                  s["v2"], s["v2"], s["v2"]] + [HBM] * nr,
        out_specs=[col(L), col(LC), s["h4"], s["h4"], s["v2"], s["v2"], s["v2"], col(CONV_W), col(1)]
        + [HBM] * (2 * nr),
        scratch_shapes=[pltpu.VMEM((17, L, HD), F32), pltpu.VMEM((17, LC, HD), F32), pltpu.VMEM((L + 16, HD), F32)]
        + (_chips_sems(nr) if nr else []),
        input_output_aliases={3: 0},
        compiler_params=pltpu.CompilerParams(dimension_semantics=("arbitrary",), vmem_limit_bytes=VMEM_LIMIT,
                                             has_side_effects=True),
    )(zx, zc, dyl, dz, cw, cb, wa, wx, ba, bx, lam, *[pltpu.with_memory_space_constraint(a, pltpu.HBM)
                                                       for a in chip_sums])


def _mixer_loss(x, tgt, zx, yl, gx, fg, lng, lnb, ws, wst, bst, wout, tm):
    ncht = tm // CHUNK

    def body(x_ref, t_ref, ga_ref, u_ref, v_ref, gb_ref, yl_ref, gx_ref, fg_ref, lng_ref, lnb_ref, ws_ref, wst_ref,
             bst_ref, wout_ref,
             dz_ref, dyl_ref, dxn_ref, y_s, do_ref, dws_ref, dbst_ref, vec_ref,
             vn_s, mix_s, dm_s, dvn_s):
        step = pl.program_id(0)

        @pl.when(step == 0)
        def _():
            dws_ref[...] = jnp.zeros_like(dws_ref)
            dbst_ref[...] = jnp.zeros_like(dbst_ref)
            vec_ref[...] = jnp.zeros_like(vec_ref)

        u, v = u_ref[...], v_ref[...]
        ug, dug_du = _gelu_and_grad(u)
        vg, dvg_dv = _gelu_and_grad(v)
        mu = jnp.mean(vg, axis=-1, keepdims=True)
        vc = vg - mu
        rstd = lax.rsqrt(jnp.mean(vc * vc, axis=-1, keepdims=True) + LN_EPS)
        vhat = vc * rstd
        lngv = lng_ref[...]
        vn_s[...] = (vhat * lngv + lnb_ref[...]).astype(BF16)
        for ch in range(ncht):
            rs = slice(ch * CHUNK, (ch + 1) * CHUNK)
            for g in range(HEADS):
                cs = slice(g * HD, (g + 1) * HD)
                mix_s[rs, cs] = _dot(ws_ref[g], vn_s[rs, cs]) + bst_ref[:, g:g + 1]
        mixed = mix_s[...]
        ga, gb, yl = ga_ref[...], gb_ref[...], yl_ref[...]
        sga, dsga = _silu_and_grad(ga)
        sgb, dsgb = _silu_and_grad(gb)
        ys = ug * mixed
        y_s[:, 0:D] = (yl * sga).astype(BF16)
        y_s[:, D:D_MIX] = (ys * sgb).astype(BF16)
        o = _dot(y_s[...], wout_ref[...])
        gxv, fgv = gx_ref[...], fg_ref[...]
        xn = x_ref[...] + gxv * o
        rs2 = lax.rsqrt(jnp.mean(xn * xn, axis=-1, keepdims=True) + NORM_EPS)
        xh = xn * rs2
        diff = xh * fgv - t_ref[...]
        vec_ref[R_LOSS:R_LOSS + 1, :] += jnp.full((1, D), jnp.sum(diff * diff) * (0.5 / D), F32)
        dout = diff * (1.0 / D)
        w = dout * fgv
        dxn = rs2 * (w - xh * jnp.mean(w * xh, axis=-1, keepdims=True))
        dxn_ref[...] = dxn
        vec_ref[0:1, :] += jnp.sum(dxn * o, axis=0, keepdims=True)
        vec_ref[1:2, :] += jnp.sum(dout * xh, axis=0, keepdims=True)
        dob = (dxn * gxv).astype(BF16)
        do_ref[...] = dob
        dy = _dot_nt(dob, wout_ref[...])
        dya, dyb = dy[:, 0:D], dy[:, D:D_MIX]
        dyl_ref[...] = dya * sga
        dys = dyb * sgb
        dz_ref[:, 0:D] = jnp.zeros((tm, D), BF16)
        dz_ref[:, D:2 * D] = (dya * yl * dsga).astype(BF16)
        dz_ref[:, 2 * D:3 * D] = (dys * mixed * dug_du).astype(BF16)
        dz_ref[:, 4 * D:5 * D] = (dyb * ys * dsgb).astype(BF16)
        dm = dys * ug
        dm_s[...] = dm.astype(BF16)
        for g in range(HEADS):
            cs = slice(g * HD, (g + 1) * HD)
            dbst_ref[:, g:g + 1] += sum(jnp.sum(dm[ch * CHUNK:(ch + 1) * CHUNK, cs], axis=1, keepdims=True)
                                        for ch in range(ncht))
            for ch in range(ncht):
                rs = slice(ch * CHUNK, (ch + 1) * CHUNK)
                dws_ref[g] += _dot_nt(dm_s[rs, cs], vn_s[rs, cs])
                dvn_s[rs, cs] = _dot(wst_ref[g], dm_s[rs, cs])
        dvn = dvn_s[...]
        vec_ref[2:3, :] += jnp.sum(dvn * vhat, axis=0, keepdims=True)
        vec_ref[3:4, :] += jnp.sum(dvn, axis=0, keepdims=True)
        dvh = dvn * lngv
        dvg = rstd * (dvh - jnp.mean(dvh, axis=-1, keepdims=True) - vhat * jnp.mean(dvh * vhat, axis=-1, keepdims=True))
        dz_ref[:, 3 * D:4 * D] = (dvg * dvg_dv).astype(BF16)

    tile = pl.BlockSpec((tm, D), lambda i: (i, 0))
    zcol = lambda n: pl.BlockSpec((tm, D), lambda i: (i, n))
    vec = pl.BlockSpec((1, D), lambda i: (0, 0))
    full = lambda *s: pl.BlockSpec(s, lambda i: (0,) * len(s))
    return _call(
        body, name="mixer_loss", grid=(L // tm,),
        out_shape=[jax.ShapeDtypeStruct((L, D_IN), BF16), jax.ShapeDtypeStruct((L, D), F32),
                   jax.ShapeDtypeStruct((L, D), F32), jax.ShapeDtypeStruct((L, D_MIX), BF16),
                   jax.ShapeDtypeStruct((L, D), BF16),
                   jax.ShapeDtypeStruct((HEADS, CHUNK, CHUNK), F32), jax.ShapeDtypeStruct((CHUNK, HEADS), F32),
                   jax.ShapeDtypeStruct((8, D), F32)],
        in_specs=[tile, tile, zcol(1), zcol(2), zcol(3), zcol(4), tile, pl.BlockSpec((1, D), lambda i: (0, 2)),
                  vec, vec, vec,
                  full(HEADS, CHUNK, CHUNK), full(HEADS, CHUNK, CHUNK), full(CHUNK, HEADS),
                  pl.BlockSpec((D_MIX, D), lambda i: (0, 0), pipeline_mode=pl.Buffered(1))],
        out_specs=[pl.BlockSpec((tm, D_IN), lambda i: (i, 0)), tile, tile,
                   pl.BlockSpec((tm, D_MIX), lambda i: (i, 0)), tile,
                   full(HEADS, CHUNK, CHUNK), full(CHUNK, HEADS), full(8, D)],
        scratch_shapes=[pltpu.VMEM((tm, D), BF16), pltpu.VMEM((tm, D), F32),
                        pltpu.VMEM((tm, D), BF16), pltpu.VMEM((tm, D), F32)],
        compiler_params=_params("arbitrary"),
    )(x, tgt, zx, zx, zx, zx, yl, gx, fg, lng, lnb, ws, wst, bst, wout)


def _grad_w(a, b, a2, b2, tk, name, bw, first, nblocks, split):
    nk = a.shape[0] // tk
    m = a.shape[1]
    with_ctx = a2 is not None
    if split == "cols":
        slots, r, w = nblocks, m, bw // 2
        piece = lambda q, pc: (slice(None), slice(pc * w, (pc + 1) * w))
    else:
        slots, r, w = 4, m // 8, bw
        piece = lambda q, pc: (slice((2 * q + pc) * r, (2 * q + pc + 1) * r), slice(None))

    def body(*refs):
        a_ref, b_ref = refs[:2]
        a2_ref, b2_ref = refs[2:4] if with_ctx else (None, None)
        sums_ref, acc, mine_v, send_v, stage_v, send_sems, recv_sems = refs[4 if with_ctx else 2:]
        n, k = pl.program_id(0), pl.program_id(1)
        x, y, c = lax.axis_index("x"), lax.axis_index("y"), lax.axis_index("c")

        def to_sibling(s):
            return pltpu.make_async_remote_copy(src_ref=send_v.at[s], dst_ref=stage_v.at[s], send_sem=send_sems.at[s],
                                                recv_sem=recv_sems.at[s], device_id=(x, y, 1 - c),
                                                device_id_type=MESH)

        @pl.when(k == 0)
        def _():
            acc[...] = jnp.zeros_like(acc)

        acc[...] += _dot_tn(a_ref[...], b_ref[...])

        if with_ctx:
            @pl.when(jnp.logical_and(k == nk - 1, n == 0))
            def _():
                acc[:, 0:b2_ref.shape[1]] += _dot_tn(a2_ref[...], b2_ref[...])

        def hand_over(s, q):
            for pc in (0, 1):
                @pl.when(c == pc)
                def _(pc=pc):
                    mine_v[s] = acc[piece(q, pc)]
                    send_v[s] = acc[piece(q, 1 - pc)].astype(BF16)
            to_sibling(s).start()

        for i in range(nblocks):
            @pl.when(jnp.logical_and(k == nk - 1, n == i))
            def _(i=i):
                if split == "cols":
                    hand_over(i, 0)
                else:
                    for q in range(4):
                        hand_over(q, q)

        @pl.when(jnp.logical_and(k == nk - 1, n == nblocks - 1))
        def _():
            for s in range(slots):
                to_sibling(s).wait_recv()
                sums_ref[s] = (mine_v[s] + stage_v[s].astype(F32)).astype(BF16)
            for s in range(slots):
                to_sibling(s).wait_send()

    in_specs = [pl.BlockSpec((tk, m), lambda n, k: (k, 0)), pl.BlockSpec((tk, bw), lambda n, k: (k, n + first))]
    args = [a, b]
    if with_ctx:
        in_specs += [pl.BlockSpec(a2.shape, lambda n, k: (0, 0)), pl.BlockSpec(b2.shape, lambda n, k: (0, 0))]
        args += [a2, b2]
    return _call(
        body, name=name, grid=(nblocks, nk), out_shape=jax.ShapeDtypeStruct((slots, r, w), BF16),
        in_specs=in_specs, out_specs=pl.BlockSpec((slots, r, w), lambda n, k: (0, 0, 0)),
        scratch_shapes=[pltpu.VMEM((m, bw), F32), pltpu.VMEM((slots, r, w), F32), pltpu.VMEM((slots, r, w), BF16),
                        pltpu.VMEM((slots, r, w), BF16), pltpu.SemaphoreType.DMA((slots,)),
                        pltpu.SemaphoreType.DMA((slots,))],
        compiler_params=pltpu.CompilerParams(dimension_semantics=("arbitrary", "arbitrary"),
                                             vmem_limit_bytes=VMEM_LIMIT, has_side_effects=True),
    )(*args)


def _grad_rows(xr, dz, w, mod, ng, dres, ncols, tm, name, chip_sums=(), first_chips=None, dests=None):
    rows = xr.shape[0]
    steps = rows // tm
    with_dx = dres is not None
    nr = len(chip_sums)
    dests = [d for d in (dests or [None] * nr)]
    nd = sum(d is not None for d in dests)
    nin = 6 if with_dx else 5
    nout = 2 if with_dx else 1

    def body(*refs):
        if with_dx:
            x_ref, dz_ref, w_ref, sc_ref, ng_ref, dres_ref = refs[:nin]
            dx_ref, vec_ref = refs[nin + nr + nd:nin + nr + nd + nout]
        else:
            x_ref, dz_ref, w_ref, sc_ref, ng_ref = refs[:nin]
            (vec_ref,) = refs[nin + nr + nd:nin + nr + nd + nout]
        if nr:
            o0 = nin + nr + nd + nout
            start, forward, finish = _chips_ops(refs[nin:nin + nr], refs[o0:o0 + nr], refs[o0 + nr:o0 + 2 * nr],
                                                *refs[o0 + 2 * nr:], first_chips=first_chips)
            pl.when(pl.program_id(0) == 0)(start)
            pl.when(pl.program_id(0) == 2)(forward)
            pl.when(pl.program_id(0) == steps - 1)(finish)

        @pl.when(pl.program_id(0) == 0)
        def _():
            vec_ref[...] = jnp.zeros_like(vec_ref)

        dhn = _dot_nt(dz_ref[...], w_ref[...])
        x = x_ref[...]
        rs = lax.rsqrt(jnp.mean(x * x, axis=-1, keepdims=True) + NORM_EPS)
        xh = x * rs
        ngv = ng_ref[...]
        y = xh * ngv
        vec_ref[0:1, :] += jnp.sum(dhn, axis=0, keepdims=True)
        vec_ref[1:2, :] += jnp.sum(dhn * y, axis=0, keepdims=True)
        dy = dhn * (1.0 + sc_ref[...])
        vec_ref[2:3, :] += jnp.sum(dy * xh, axis=0, keepdims=True)
        if with_dx:
            dxh = dy * ngv
            dx_ref[...] = dres_ref[...] + rs * (dxh - xh * jnp.mean(dxh * xh, axis=-1, keepdims=True))

    tile = pl.BlockSpec((tm, D), lambda i: (i, 0))
    vec = pl.BlockSpec((1, D), lambda i: (0, 0))
    in_specs = [tile, pl.BlockSpec((tm, ncols), lambda i: (i, 0)),
                pl.BlockSpec((D, ncols), lambda i: (0, 0), pipeline_mode=pl.Buffered(1)),
                pl.BlockSpec((1, D), lambda i: (0, 1)), vec]
    out_shape = [jax.ShapeDtypeStruct((8, D), F32)]
    out_specs = [pl.BlockSpec((8, D), lambda i: (0, 0))]
    args = [xr, dz, w, mod, ng]
    if with_dx:
        in_specs.append(tile)
        out_shape.insert(0, jax.ShapeDtypeStruct((rows, D), F32))
        out_specs.insert(0, tile)
        args.append(dres)
    aliases = {}
    for j, d in enumerate(dests):
        if d is not None:
            aliases[len(args) + nr + len(aliases)] = len(out_shape) + j
    in_specs += [HBM] * (nr + nd)
    out_specs += [HBM] * (2 * nr)
    out_shape += [jax.ShapeDtypeStruct((4,) + a.shape[1:], a.dtype) for a in chip_sums]
    out_shape += _chips_stage_shapes(chip_sums)
    args += [pltpu.with_memory_space_constraint(a, pltpu.HBM) for a in chip_sums]
    args += [pltpu.with_memory_space_constraint(d, pltpu.HBM) for d in dests if d is not None]
    return _call(body, name=name, grid=(steps,), out_shape=out_shape, in_specs=in_specs, out_specs=out_specs,
                 scratch_shapes=_chips_sems(nr) if nr else [], input_output_aliases=aliases,
                 compiler_params=pltpu.CompilerParams(dimension_semantics=("arbitrary",),
                                                      vmem_limit_bytes=VMEM_LIMIT, has_side_effects=bool(nr)))(*args)


def _adamw(w, g, m, v):
    m = ADAM_B1 * m + (1.0 - ADAM_B1) * g
    v = ADAM_B2 * v + (1.0 - ADAM_B2) * (g * g)
    m_hat = m / (1.0 - ADAM_B1 ** ADAM_STEP)
    v_hat = v / (1.0 - ADAM_B2 ** ADAM_STEP)
    delta = -ADAM_LR * (m_hat / (jnp.sqrt(v_hat) + ADAM_EPS) + ADAM_WD * w)
    return delta, m, v


def _adamw_reduced(parts, w, m, v, tr, name):
    r, n = w.shape
    nparts = parts.shape[0]

    def body(p_ref, w_ref, m_ref, v_ref, g_ref, d_ref, mo_ref, vo_ref):
        g = p_ref[0].astype(F32)
        for i in range(1, nparts):
            g = g + p_ref[i].astype(F32)
        g_ref[...] = g
        d_ref[...], mo_ref[...], vo_ref[...] = _adamw(w_ref[...], g, m_ref[...], v_ref[...])

    tile = pl.BlockSpec((tr, n), lambda i: (i, 0))
    sds = jax.ShapeDtypeStruct((r, n), F32)
    return _call(
        body, name=name, grid=(r // tr,), out_shape=[sds] * 4,
        in_specs=[pl.BlockSpec((nparts, tr, n), lambda i: (0, i, 0)), tile, tile, tile], out_specs=[tile] * 4,
        compiler_params=_params("arbitrary"),
    )(parts, w, m, v)


R_GATE, R_FINAL_G, R_LN_G, R_LN_B, R_LOSS = 0, 1, 2, 3, 4
R_SH_X, R_SC_X, R_NG_X = 5, 6, 7
R_SH_C, R_SC_C, R_NG_C = 8, 9, 10
R_BA, R_BX, R_LAM, R_CW, R_CB, R_SGU_B = 11, 13, 15, 17, 21, 22
PACK_ROWS = 32
MAT_ROWS = 2 * (2 * HEADS * HD) + HEADS * CHUNK


def _reduce_small(vp_all, mat_parts, ada_w, me):
    nloc = ada_w.shape[1]

    def body(me_ref, vp_ref, mp_ref, w_ref, red_ref, mat_ref, dmod_ref, gab_ref, cpart_ref, dmc_s):
        red = vp_ref[0]
        for i in range(1, N_DEV):
            red = red + vp_ref[i]
        mat = mp_ref[0].astype(F32)
        for i in range(1, mp_ref.shape[0]):
            mat = mat + mp_ref[i].astype(F32)
        red_ref[...] = red
        mat_ref[...] = mat
        for e in range(N_DEV):
            dmod_ref[e:e + 1, 0:D] = vp_ref[e, R_SH_X:R_SH_X + 1, :]
            dmod_ref[e:e + 1, D:2 * D] = vp_ref[e, R_SC_X:R_SC_X + 1, :]
            dmod_ref[e:e + 1, 2 * D:3 * D] = vp_ref[e, R_GATE:R_GATE + 1, :]
        dmod_ref[8:9, 0:D] = red[R_SH_C:R_SH_C + 1, :]
        dmod_ref[8:9, D:2 * D] = red[R_SC_C:R_SC_C + 1, :]
        dmod_ref[8:9, 2 * D:3 * D] = jnp.zeros((1, D), F32)
        dmod_ref[9:16, :] = jnp.zeros((7, 3 * D), F32)
        gab_ref[:, 0:D] = red[R_SH_X:R_SH_X + 1, :] + red[R_SH_C:R_SH_C + 1, :]
        gab_ref[:, D:2 * D] = red[R_SC_X:R_SC_X + 1, :] + red[R_SC_C:R_SC_C + 1, :]
        gab_ref[:, 2 * D:3 * D] = red[R_GATE:R_GATE + 1, :]
        dmc_s[...] = jnp.broadcast_to(dmod_ref[8:9, :], (8, 3 * D))
        off = pl.multiple_of(me_ref[0] * nloc, 128)
        cpart_ref[...] = _dot_nt(dmc_s[:, pl.ds(off, nloc)], w_ref[...])

    return _call(
        body, name="reduce_small",
        out_shape=[jax.ShapeDtypeStruct((PACK_ROWS, D), F32), jax.ShapeDtypeStruct(mat_parts.shape[1:], F32),
                   jax.ShapeDtypeStruct((16, 3 * D), F32), jax.ShapeDtypeStruct((1, 3 * D), F32),
                   jax.ShapeDtypeStruct((8, D), F32)],
        in_specs=[pl.BlockSpec(memory_space=pltpu.SMEM), VMEM, VMEM, VMEM], out_specs=[VMEM] * 5,
        scratch_shapes=[pltpu.VMEM((8, 3 * D), F32)], compiler_params=_params(),
    )(me, vp_all, mat_parts, ada_w)


def _adamw_ada(c_all, c_ctx, dmod, w, m, v, me):
    nloc = w.shape[1]

    def body(me_ref, c_ref, cc_ref, dm_ref, w_ref, m_ref, v_ref, g_ref, d_ref, mo_ref, vo_ref):
        off = pl.multiple_of(me_ref[0] * nloc, 128)
        dm = dm_ref[:, pl.ds(off, nloc)]
        sx, _ = _silu_and_grad(c_ref[...])
        sc, _ = _silu_and_grad(cc_ref[...])
        g = _dot_tn(sx, dm[0:8, :]) + _dot_tn(jnp.broadcast_to(sc, (8, D)), dm[8:16, :])
        g_ref[...] = g
        d_ref[...], mo_ref[...], vo_ref[...] = _adamw(w_ref[...], g, m_ref[...], v_ref[...])

    sds = jax.ShapeDtypeStruct(w.shape, F32)
    return _call(
        body, name="adamw_ada_w", out_shape=[sds] * 4,
        in_specs=[pl.BlockSpec(memory_space=pltpu.SMEM)] + [VMEM] * 6, out_specs=[VMEM] * 4,
        compiler_params=_params(),
    )(me, c_all, c_ctx, dmod, w, m, v)


_SMALL = ("c_ctx", "ada_b", "norm_g", "conv_w", "conv_b", "lru_wa", "lru_ba", "lru_wx", "lru_bx", "lru_lambda",
          "sgu_ln_g", "sgu_ln_b", "sgu_w", "sgu_b", "final_g")


def _adamw_small(red, mat, cparts, gab, ws, ms, vs, me):
    n = len(_SMALL)
    nw = 2 * HEADS * HD

    def body(me_ref, red_ref, mat_ref, cp_ref, gab_ref, *refs):
        w_refs, m_refs, v_refs = refs[:n], refs[n:2 * n], refs[2 * n:3 * n]
        outs = refs[3 * n:]
        off = pl.multiple_of(me_ref[0] * HD, 128)

        def row(r, k=1):
            return red_ref[r:r + k, :]

        cc = w_refs[0][...]
        dcc = cp_ref[0, 0:1, :]
        for i in range(1, N_DEV):
            dcc = dcc + cp_ref[i, 0:1, :]
        grads = dict(
            c_ctx=dcc * _silu_and_grad(cc)[1], ada_b=gab_ref[...], norm_g=row(R_NG_X) + row(R_NG_C),
            conv_w=red_ref[R_CW:R_CW + CONV_W, pl.ds(off, HD)], conv_b=row(R_CB),
            lru_wa=mat_ref[0:nw, :], lru_ba=row(R_BA, 2), lru_wx=mat_ref[nw:2 * nw, :], lru_bx=row(R_BX, 2),
            lru_lambda=red_ref[R_LAM:R_LAM + 2, pl.ds(off, HD)], sgu_ln_g=row(R_LN_G), sgu_ln_b=row(R_LN_B),
            sgu_w=mat_ref[2 * nw:MAT_ROWS, :], sgu_b=row(R_SGU_B), final_g=row(R_FINAL_G))
        for j, name in enumerate(_SMALL):
            g = grads[name]
            outs[j][...] = g
            outs[n + j][...], outs[2 * n + j][...], outs[3 * n + j][...] = _adamw(w_refs[j][...], g, m_refs[j][...],
                                                                                 v_refs[j][...])

    sds = [jax.ShapeDtypeStruct(ws[k].shape, F32) for k in _SMALL]
    outs = _call(
        body, name="adamw_small", out_shape=sds * 4,
        in_specs=[pl.BlockSpec(memory_space=pltpu.SMEM)] + [VMEM] * (4 + 3 * n), out_specs=[VMEM] * (4 * n),
        compiler_params=_params(),
    )(me, red, mat, cparts, gab, *[ws[k] for k in _SMALL], *[ms[k] for k in _SMALL], *[vs[k] for k in _SMALL])
    return [dict(zip(_SMALL, outs[i * n:(i + 1) * n])) for i in range(4)]


def kernel(x, c, ctx, c_ctx, ada_w, ada_b, norm_g, w_in, conv_w, conv_b, lru_wa, lru_ba, lru_wx, lru_bx, lru_lambda, sgu_ln_g, sgu_ln_b, sgu_w, sgu_b, w_out, final_g, loss_target, m_c_ctx, m_ada_w, m_ada_b, m_norm_g, m_w_in, m_conv_w, m_conv_b, m_lru_wa, m_lru_ba, m_lru_wx, m_lru_bx, m_lru_lambda, m_sgu_ln_g, m_sgu_ln_b, m_sgu_w, m_sgu_b, m_w_out, m_final_g, v_c_ctx, v_ada_w, v_ada_b, v_norm_g, v_w_in, v_conv_w, v_conv_b, v_lru_wa, v_lru_ba, v_lru_wx, v_lru_bx, v_lru_lambda, v_sgu_ln_g, v_sgu_ln_b, v_sgu_w, v_sgu_b, v_w_out, v_final_g):
    args = dict(locals())
    me_s = 4 * lax.axis_index("x") + 2 * lax.axis_index("y") + lax.axis_index("c")
    me = me_s.astype(jnp.int32).reshape(1)
    xr, ctxr, tgt = x[0], ctx[0], loss_target[0]
    cc = c_ctx.reshape(1, D)
    nw = 2 * HEADS * HD
    view = dict(c_ctx=(1, D), ada_b=(1, 3 * D), norm_g=(1, D), conv_w=(CONV_W, HD), conv_b=(1, D), lru_wa=(nw, HD),
                lru_ba=(2, D), lru_wx=(nw, HD), lru_bx=(2, D), lru_lambda=(2, HD), sgu_ln_g=(1, D), sgu_ln_b=(1, D),
                sgu_w=(HEADS * CHUNK, CHUNK), sgu_b=(1, D), final_g=(1, D))

    zx, hn, w_full, w_out_b, modx, modc, c_all, cw_full, lam_full = _front_project(
        xr, c, cc, ada_w[0], ada_b, norm_g, w_in[0], w_out[0], conv_w[0], lru_lambda[0], me)
    zc, hnc = _project(ctxr, modc, norm_g, w_full, D, LC, "project_ctx")
    ba, bx = lru_ba.reshape(2, D), lru_bx.reshape(2, D)
    yl, wout_all = _lru_forward(zx, zc, cw_full, conv_b, lru_wa[0], lru_wx[0], ba, bx, lam_full, [w_out_b], ["ag"])
    wout_full = wout_all.reshape(D_MIX, D)
    ws_b = sgu_w[0].astype(BF16)
    dz, dyl, dxn, ycat, dob, dws, dbst, mvec = _mixer_loss(
        xr, tgt, zx, yl, modx, final_g.reshape(1, D), sgu_ln_g, sgu_ln_b, ws_b, jnp.swapaxes(ws_b, 1, 2),
        sgu_b[0].T, wout_full, 256)

    wout_sums = _grad_w(ycat, dob, None, None, 1024, "grad_w_out", D, 0, 1, "rows")
    rest_sums = _grad_w(hn, dz, None, None, 1024, "grad_w_in_rest", 2 * W_IN_SHARD, 1, 3, "cols")
    dz, dxac, dwa, dwx, dba, dbx, dlam, dcw, dcb, win_parts, wout_parts, _, _ = _lru_backward(
        zx, zc, dyl, dz, cw_full, conv_b, lru_wa[0], lru_wx[0], ba, bx, lam_full, [rest_sums, wout_sums],
        first_chips=[1, 0])
    first_sums = _grad_w(hn, dz, hnc, dxac, 1024, "grad_w_in_first", 2 * W_IN_SHARD, 0, 1, "cols")
    matpack = jnp.concatenate([dwa.reshape(nw, HD), dwx.reshape(nw, HD), dws.reshape(HEADS * CHUNK, CHUNK)],
                              axis=0).astype(BF16)
    (mat_sums,) = _reduce2_local([matpack.reshape(N_DEV, MAT_ROWS // N_DEV, HD)], ["a2a"], me, "reduce_mat")
    gx, xvec, win_parts, mat_parts, _, _ = _grad_rows(
        xr, dz, w_full, modx, norm_g, dxn, D_IN, 256, "grad_rows_x", chip_sums=[first_sums, mat_sums],
        first_chips=[0, 0], dests=[win_parts, None])
    (cvec,) = _grad_rows(ctxr, dxac, w_full, modc, norm_g, None, D, LC, "grad_rows_ctx")
    pack = jnp.concatenate([mvec[0:5], xvec[0:3], cvec[0:3], dba, dbx, dlam, dcw, dcb, dbst.T.reshape(1, D),
                            jnp.zeros((PACK_ROWS - R_SGU_B - 1, D), F32)], axis=0)
    (vp_all,) = _gather2([pack], ["ag"], "gather_pack")
    red, matpiece, dmod, gab, cpart = _reduce_small(vp_all, mat_parts, ada_w[0], me)
    mat_all, cparts = _gather2([matpiece, cpart], ["ag", "ag"], "gather_small")

    g_w_in, d_w_in, nm_w_in, nv_w_in = _adamw_reduced(win_parts, w_in[0], m_w_in[0], v_w_in[0], 256, "adamw_w_in")
    g_w_out, d_w_out, nm_w_out, nv_w_out = _adamw_reduced(wout_parts, w_out[0], m_w_out[0], v_w_out[0], 128,
                                                          "adamw_w_out")
    g_ada, d_ada, nm_ada, nv_ada = _adamw_ada(c_all, cc, dmod, ada_w[0], m_ada_w[0], v_ada_w[0], me)
    ws = {k: args[k].reshape(view[k]) for k in _SMALL}
    ms = {k: args["m_" + k].reshape(view[k]) for k in _SMALL}
    vs = {k: args["v_" + k].reshape(view[k]) for k in _SMALL}
    small = _adamw_small(red, mat_all.reshape(MAT_ROWS, HD), cparts, gab, ws, ms, vs, me)
    big = dict(w_in=(g_w_in, d_w_in, nm_w_in, nv_w_in), w_out=(g_w_out, d_w_out, nm_w_out, nv_w_out),
               ada_w=(g_ada, d_ada, nm_ada, nv_ada))

    loss = red[R_LOSS, 0]
    names = ("c_ctx", "ada_w", "ada_b", "norm_g", "w_in", "conv_w", "conv_b", "lru_wa", "lru_ba", "lru_wx", "lru_bx",
             "lru_lambda", "sgu_ln_g", "sgu_ln_b", "sgu_w", "sgu_b", "w_out", "final_g")
    outs = [loss, gx.reshape(x.shape)]
    for kind in range(4):
        for k in names:
            val = big[k][kind] if k in big else small[kind][k]
            outs.append(val.reshape(args[k].shape))
    return tuple(outs)
```

```python
import functools

import jax
import jax.numpy as jnp
from jax import lax
from jax.experimental import pallas as pl
from jax.experimental.pallas import tpu as pltpu

F32 = jnp.float32
BF16 = jnp.bfloat16

N_DEV = 8
D = 1024
L = 2048
LC = 256
HEADS = 8
HD = 128
CHUNK = 128
D_IN = 5 * D
W_IN_SHARD = D_IN // N_DEV
ROWS = 256
D_MIX = 2 * D
CONV_W = 4
LRU_C = 8.0
NORM_EPS = 1e-6
LN_EPS = 1e-5
ADAM_LR, ADAM_B1, ADAM_B2, ADAM_EPS, ADAM_WD, ADAM_STEP = 0.001, 0.9, 0.999, 1e-08, 0.01, 10

VMEM_LIMIT = 56 * 1024 * 1024

HBM = pl.BlockSpec(memory_space=pltpu.HBM)
VMEM = pl.BlockSpec(memory_space=pltpu.VMEM)
MESH = pl.DeviceIdType.MESH


def _call(body, **kw):
    return pl.pallas_call(body, **kw)


def _params(*sem):
    return pltpu.CompilerParams(dimension_semantics=sem, vmem_limit_bytes=VMEM_LIMIT)


def _sigmoid(x):
    return 0.5 * jnp.tanh(0.5 * x) + 0.5


def _silu_and_grad(x):
    s = _sigmoid(x)
    return x * s, s * (1.0 + x * (1.0 - s))


_G0 = 0.7978845608028654
_G1 = 0.044715


def _gelu_and_grad(x):
    x2 = x * x
    t = jnp.tanh(_G0 * (x + _G1 * x * x2))
    cdf = 0.5 * (1.0 + t)
    return x * cdf, cdf + 0.5 * x * (1.0 - t * t) * (_G0 * (1.0 + 3.0 * _G1 * x2))


def _gelu(x):
    return 0.5 * x * (1.0 + jnp.tanh(_G0 * (x + _G1 * x * x * x)))


def _softplus(z):
    t = jnp.exp(-jnp.abs(z))
    u = 1.0 + t
    log1p = jnp.where(u == 1.0, t, jnp.log(u) * t / jnp.where(u == 1.0, 1.0, u - 1.0))
    return jnp.maximum(z, 0.0) + log1p


def _dot(a, b):
    return jnp.dot(a, b, preferred_element_type=F32)


def _dot_nt(a, b):
    return lax.dot_general(a, b, (((1,), (1,)), ((), ())), preferred_element_type=F32)


def _dot_tn(a, b):
    return lax.dot_general(a, b, (((0,), (0,)), ((), ())), preferred_element_type=F32)


def _rows(shape):
    return lax.broadcasted_iota(jnp.int32, shape, 0)


def _shift_down(x, first):
    y = pltpu.roll(x, 1, 0)
    head = jnp.where(_rows((8, x.shape[1])) == 0, first, y[0:8])
    return jnp.concatenate([head, y[8:]], axis=0)


def _shift_up(x, last):
    n = x.shape[0]
    y = pltpu.roll(x, n - 1, 0)
    tail = jnp.where(_rows((8, x.shape[1])) == 7, last, y[n - 8:])
    return jnp.concatenate([y[:n - 8], tail], axis=0)


def _gather2(arrays, modes, name):
    n = len(arrays)

    def body(*refs):
        start, forward, finish = _gather2_ops(refs[:n], refs[n:2 * n], modes, *refs[2 * n:])
        start()
        forward()
        finish()

    return _call(
        body, name=name, out_shape=_gather2_shapes(arrays, modes), in_specs=[HBM] * n, out_specs=[HBM] * n,
        scratch_shapes=_gather2_sems(n), compiler_params=pltpu.CompilerParams(has_side_effects=True),
    )(*[pltpu.with_memory_space_constraint(a, pltpu.HBM) for a in arrays])


def _gather2_shapes(arrays, modes):
    return [jax.ShapeDtypeStruct((N_DEV,) + a.shape if m == "ag" else (a.shape[0], N_DEV * a.shape[1]), a.dtype)
            for a, m in zip(arrays, modes)]


def _gather2_sems(n):
    return [pltpu.SemaphoreType.DMA((n, N_DEV - 1)), pltpu.SemaphoreType.DMA((n, N_DEV - 1)),
            pltpu.SemaphoreType.DMA((n,))]


def _gather2_ops(ins, outs, modes, send_sems, recv_sems, local_sems):
    n = len(ins)
    x, y, c = lax.axis_index("x"), lax.axis_index("y"), lax.axis_index("c")
    me, sibling = (x, y, c), (x, y, 1 - c)
    chips = [(x ^ (k >> 1), y ^ (k & 1)) for k in (1, 2, 3)]

    def slot(j, px, py, pc):
        dev = 4 * px + 2 * py + pc
        if modes[j] == "agc":
            w = ins[j].shape[1]
            return outs[j].at[:, pl.ds(pl.multiple_of(dev * w, 128), w)]
        return outs[j].at[dev]

    def copy(j, k, block, to, src=None):
        return pltpu.make_async_remote_copy(
            src_ref=slot(j, *block) if src is None else src, dst_ref=slot(j, *block),
            send_sem=send_sems.at[j, k], recv_sem=recv_sems.at[j, k], device_id=to, device_id_type=MESH)

    def own(j):
        return pltpu.make_async_copy(ins[j], slot(j, *me), local_sems.at[j])

    def first(j):
        return [copy(j, 0, me, sibling, src=ins[j])] + [copy(j, 1 + i, me, (*chip, c), src=ins[j])
                                                        for i, chip in enumerate(chips)]

    def passed(j, i):
        return copy(j, 4 + i, (*chips[i], c), sibling)

    def start():
        for j in range(n):
            own(j).start()
            for cp in first(j):
                cp.start()

    def forward():
        for i, chip in enumerate(chips):
            for j in range(n):
                copy(j, 1 + i, (*chip, c), me).wait_recv()
                passed(j, i).start()

    def finish():
        for j in range(n):
            copy(j, 0, sibling, me).wait_recv()
            for i, chip in enumerate(chips):
                copy(j, 4 + i, (*chip, 1 - c), me).wait_recv()
            for cp in first(j) + [passed(j, i) for i in range(3)]:
                cp.wait_send()
            own(j).wait()

    return start, forward, finish


def _reduce2_local(arrays, modes, me, name, counts=None):
    n = len(arrays)
    counts = counts or [4] * n
    shapes = [(a.shape[1], a.shape[2]) if m == "a2a" else (a.shape[0], a.shape[1] // (2 * cnt))
              for a, m, cnt in zip(arrays, modes, counts)]
    staged = [jax.ShapeDtypeStruct((cnt,) + s, a.dtype) for s, a, cnt in zip(shapes, arrays, counts)]

    def piece(ref, mode, dev, w):
        return ref.at[dev] if mode == "a2a" else ref.at[:, pl.ds(pl.multiple_of(dev * w, 128), w)]

    def to_sibling(*refs):
        ins, outs = refs[:n], refs[n:2 * n]
        send_sems, recv_sems = refs[2 * n:]
        x, y, c = lax.axis_index("x"), lax.axis_index("y"), lax.axis_index("c")
        copies = []
        for j in range(n):
            for q in range(counts[j]):
                cp = pltpu.make_async_remote_copy(
                    src_ref=piece(ins[j], modes[j], 2 * q + (1 - c), shapes[j][1]), dst_ref=outs[j].at[q],
                    send_sem=send_sems.at[j, q], recv_sem=recv_sems.at[j, q], device_id=(x, y, 1 - c),
                    device_id_type=MESH)
                cp.start()
                copies.append(cp)
        for cp in copies:
            cp.wait()

    stage = _call(
        to_sibling, name=name + "_d2d", out_shape=staged, in_specs=[HBM] * n, out_specs=[HBM] * n,
        scratch_shapes=[pltpu.SemaphoreType.DMA((n, 4)), pltpu.SemaphoreType.DMA((n, 4))],
        compiler_params=pltpu.CompilerParams(has_side_effects=True),
    )(*[pltpu.with_memory_space_constraint(a, pltpu.HBM) for a in arrays])

    def add(me_ref, *refs):
        del me_ref
        own, got, outs = refs[:n], refs[n:2 * n], refs[2 * n:]
        for j in range(n):
            mine = own[j][0] if modes[j] == "a2a" else own[j][...]
            outs[j][0] = (mine.astype(F32) + got[j][0].astype(F32)).astype(outs[j].dtype)

    in_specs, slot_specs = [], []
    for (r, w), m, cnt in zip(shapes, modes, counts):
        if m == "a2a":
            in_specs.append(pl.BlockSpec(
                (1, r, w), lambda q, me_ref, cnt=cnt: (2 * jnp.minimum(q, cnt - 1) + me_ref[0] % 2, 0, 0)))
        else:
            in_specs.append(pl.BlockSpec(
                (r, w), lambda q, me_ref, cnt=cnt: (0, 2 * jnp.minimum(q, cnt - 1) + me_ref[0] % 2)))
        slot_specs.append(pl.BlockSpec((1, r, w), lambda q, me_ref, cnt=cnt: (jnp.minimum(q, cnt - 1), 0, 0)))
    return _call(
        add, name=name + "_add", out_shape=staged,
        grid_spec=pltpu.PrefetchScalarGridSpec(num_scalar_prefetch=1, grid=(max(counts),),
                                               in_specs=in_specs + slot_specs, out_specs=slot_specs),
        compiler_params=_params("arbitrary"),
    )(me, *arrays, *stage)


def _chips_sems(n):
    return [pltpu.SemaphoreType.DMA((n, 6)), pltpu.SemaphoreType.DMA((n, 6)), pltpu.SemaphoreType.DMA((n,))]


def _chips_stage_shapes(chip_sums):
    return [jax.ShapeDtypeStruct((2, a.shape[1] // 2, a.shape[2]), a.dtype) for a in chip_sums]


def _chips_ops(ins, outs, stages, send_sems, recv_sems, local_sems, first_chips=None):
    x, y, c = lax.axis_index("x"), lax.axis_index("y"), lax.axis_index("c")
    qm = 2 * x + y
    first_chips = first_chips or [0] * len(ins)

    def owns(j, chip):
        lo, cnt = first_chips[j], ins[j].shape[0]
        if lo == 0 and cnt == 4:
            return None
        return jnp.logical_and(chip >= lo, chip < lo + cnt)

    def guarded(cond, fn):
        if cond is None:
            fn()
        else:
            pl.when(cond)(fn)

    def slot(j, chip):
        return jnp.clip(chip - first_chips[j], 0, ins[j].shape[0] - 1)

    def half(j, i):
        h = ins[j].shape[1] // 2
        return pl.ds(i * h, h)

    def copy(j, sem, src, dst, k):
        return pltpu.make_async_remote_copy(
            src_ref=src, dst_ref=dst, send_sem=send_sems.at[j, sem], recv_sem=recv_sems.at[j, sem],
            device_id=(x ^ (k >> 1), y ^ (k & 1), c), device_id_type=MESH)

    def direct(j, k):
        return copy(j, k - 1, ins[j].at[slot(j, qm ^ k)], outs[j].at[qm], k)

    def first_hop(j, k):
        return copy(j, 1 + k, ins[j].at[slot(j, qm ^ 3), half(j, k - 1)], stages[j].at[k - 1], k)

    def second_hop(j, k):
        return copy(j, 3 + k, stages[j].at[2 - k], outs[j].at[qm ^ (3 - k), half(j, 2 - k)], k)

    def local(j):
        return pltpu.make_async_copy(ins[j].at[slot(j, qm)], outs[j].at[qm], local_sems.at[j])

    def start():
        for j in range(len(ins)):
            for k in (1, 2):
                guarded(owns(j, qm ^ 3), lambda j=j, k=k: first_hop(j, k).start())
        for j in range(len(ins)):
            for k in (1, 2):
                guarded(owns(j, qm ^ k), lambda j=j, k=k: direct(j, k).start())
            guarded(owns(j, qm), lambda j=j: local(j).start())

    def forward():
        for j in range(len(ins)):
            for k in (1, 2):
                def pass_on(j=j, k=k):
                    first_hop(j, 3 - k).wait_recv()
                    second_hop(j, k).start()
                guarded(owns(j, qm ^ k), pass_on)

    def finish():
        for j in range(len(ins)):
            for k in (1, 2):
                guarded(owns(j, qm ^ k), lambda j=j, k=k: direct(j, k).wait_send())
                guarded(owns(j, qm ^ k), lambda j=j, k=k: second_hop(j, k).wait_send())
                guarded(owns(j, qm ^ 3), lambda j=j, k=k: first_hop(j, k).wait_send())
                guarded(owns(j, qm), lambda j=j, k=k: direct(j, k).wait_recv())
                guarded(owns(j, qm), lambda j=j, k=k: second_hop(j, k).wait_recv())
            guarded(owns(j, qm), lambda j=j: local(j).wait())

    return start, forward, finish


def _front(c, c_ctx, ada_w, ada_b, w_in, w_out, me):
    nloc = ada_w.shape[1]

    def body(me_ref, c_ref, cc_ref, aw_ref, ab_ref, win_ref, wout_ref,
             wfull_ref, woutb_ref, modx_ref, modc_ref, call_ref,
             wb_s, part_s, parts_s, w_send, w_recv, w_local, s_send, s_recv):
        x, y, cidx = lax.axis_index("x"), lax.axis_index("y"), lax.axis_index("c")
        me = me_ref[0]
        wb_s[...] = win_ref[...].astype(BF16)
        woutb_ref[...] = wout_ref[...].astype(BF16)
        start, forward, finish = _gather2_ops([wb_s], [wfull_ref], ["agc"], w_send, w_recv, w_local)
        start()

        def small_gather(src, my_slot, stage):
            copies = []
            for k in range(1, N_DEV):
                peer = (x ^ (k >> 2), y ^ ((k >> 1) & 1), cidx ^ (k & 1))
                cp = pltpu.make_async_remote_copy(src_ref=src, dst_ref=my_slot, send_sem=s_send.at[stage, k - 1],
                                                  recv_sem=s_recv.at[stage, k - 1], device_id=peer,
                                                  device_id_type=MESH)
                cp.start()
                copies.append(cp)
            pltpu.sync_copy(src, my_slot)
            for cp in copies:
                cp.wait()

        small_gather(c_ref, call_ref.at[pl.ds(me, 1), :], 0)
        off = pl.multiple_of(me * nloc, 128)
        b = ab_ref[:, pl.ds(off, nloc)]
        w = aw_ref[...]
        sx, _ = _silu_and_grad(call_ref[...])
        sc, _ = _silu_and_grad(jnp.broadcast_to(cc_ref[...], (8, D)))
        part_s[0:8, :] = _dot(sx, w) + b
        part_s[8:16, :] = _dot(sc, w) + b
        small_gather(part_s, parts_s.at[me], 1)
        mine = _rows((16, nloc)) == me
        for j in range(N_DEV):
            pj = parts_s[j]
            modx_ref[:, j * nloc:(j + 1) * nloc] = jnp.sum(jnp.where(mine, pj, 0.0), axis=0, keepdims=True)
            modc_ref[:, j * nloc:(j + 1) * nloc] = pj[8:9, :]
        forward()
        finish()

    return _call(
        body, name="front",
        out_shape=[jax.ShapeDtypeStruct((D, D_IN), BF16), jax.ShapeDtypeStruct(w_out.shape, BF16),
                   jax.ShapeDtypeStruct((1, 3 * D), F32), jax.ShapeDtypeStruct((1, 3 * D), F32),
                   jax.ShapeDtypeStruct((N_DEV, D), F32)],
        in_specs=[pl.BlockSpec(memory_space=pltpu.SMEM)] + [VMEM] * 6, out_specs=[HBM, VMEM, VMEM, VMEM, VMEM],
        scratch_shapes=[pltpu.VMEM(w_in.shape, BF16), pltpu.VMEM((16, nloc), F32),
                        pltpu.VMEM((N_DEV, 16, nloc), F32)] + _gather2_sems(1) +
                       [pltpu.SemaphoreType.DMA((2, N_DEV - 1)), pltpu.SemaphoreType.DMA((2, N_DEV - 1))],
        compiler_params=pltpu.CompilerParams(vmem_limit_bytes=VMEM_LIMIT, has_side_effects=True),
    )(me, c, c_ctx, ada_w, ada_b, w_in, w_out)


ARRIVAL = (0, 1, 2, 4, 3, 5, 6, 7)


def _front_project(xr, c, c_ctx, ada_w, ada_b, ng, w_in, w_out, cw, lam, me):
    nloc = ada_w.shape[1]
    ws = W_IN_SHARD
    order = me[0] ^ jnp.asarray(ARRIVAL, jnp.int32)

    def body(ord_ref, x_ref, c_ref, cc_ref, aw_ref, ab_ref, ng_ref, win_ref, wout_ref, cw_ref, lam_ref,
             z_ref, hn_ref, wfull_ref, woutb_ref, modx_ref, modc_ref, call_ref, cwf_ref, lamf_ref,
             wv, call_s, part_s, parts_s, w_send, w_recv, hbm_sems, s_send, s_recv, g_send, g_recv, g_local):
        t = pl.program_id(0)
        x, y, cidx = lax.axis_index("x"), lax.axis_index("y"), lax.axis_index("c")
        me_i = ord_ref[0]
        sibling = (x, y, 1 - cidx)
        chips = [(x ^ (k >> 1), y ^ (k & 1)) for k in (1, 2, 3)]
        g_start, g_pass, g_finish = _gather2_ops([cw_ref, lam_ref], [cwf_ref, lamf_ref], ["agc", "agc"],
                                                 g_send, g_recv, g_local)

        def shard_copy(k, px, py, pc, to, half=None):
            slot = wv.at[4 * px + 2 * py + pc]
            if half is not None:
                slot = slot.at[pl.ds(half * (D // 2), D // 2), :]
            return pltpu.make_async_remote_copy(src_ref=slot, dst_ref=slot, send_sem=w_send.at[k],
                                                recv_sem=w_recv.at[k], device_id=to, device_id_type=MESH)

        def small_gather(src, my_slot, stage):
            copies = []
            for k in range(1, N_DEV):
                peer = (x ^ (k >> 2), y ^ ((k >> 1) & 1), cidx ^ (k & 1))
                cp = pltpu.make_async_remote_copy(src_ref=src, dst_ref=my_slot, send_sem=s_send.at[stage, k - 1],
                                                  recv_sem=s_recv.at[stage, k - 1], device_id=peer,
                                                  device_id_type=MESH)
                cp.start()
                copies.append(cp)
            pltpu.sync_copy(src, my_slot)
            for cp in copies:
                cp.wait()

        @pl.when(t == 0)
        def _():
            g_start()
            wv[me_i] = win_ref[...].astype(BF16)
            woutb_ref[...] = wout_ref[...].astype(BF16)
            shard_copy(0, x, y, cidx, sibling).start()
            small_gather(c_ref, call_s.at[pl.ds(me_i, 1), :], 0)
            call_ref[...] = call_s[...]
            off = pl.multiple_of(me_i * nloc, 128)
            b = ab_ref[:, pl.ds(off, nloc)]
            w = aw_ref[...]
            sx, _ = _silu_and_grad(call_s[...])
            sc, _ = _silu_and_grad(jnp.broadcast_to(cc_ref[...], (8, D)))
            part_s[0:8, :] = _dot(sx, w) + b
            part_s[8:16, :] = _dot(sc, w) + b
            small_gather(part_s, parts_s.at[me_i], 1)
            for i in (0, 1):
                shard_copy(1 + i, x, y, cidx, (*chips[i], cidx)).start()
            mine = _rows((16, nloc)) == me_i
            for j in range(N_DEV):
                pj = parts_s[j]
                modx_ref[:, j * nloc:(j + 1) * nloc] = jnp.sum(jnp.where(mine, pj, 0.0), axis=0, keepdims=True)
                modc_ref[:, j * nloc:(j + 1) * nloc] = pj[8:9, :]
            shift, scale1, ngv = modx_ref[:, 0:D], 1.0 + modx_ref[:, D:2 * D], ng_ref[...]
            for r in range(L // ROWS):
                rsl = slice(r * ROWS, (r + 1) * ROWS)
                xv = x_ref[rsl, :]
                rs = lax.rsqrt(jnp.mean(xv * xv, axis=-1, keepdims=True) + NORM_EPS)
                hn_ref[rsl, :] = ((xv * rs * ngv) * scale1 + shift).astype(BF16)

        @pl.when(t == 1)
        def _():
            shard_copy(0, x, y, 1 - cidx, sibling).wait_recv()
            g_pass()

        for i in (0, 1):
            @pl.when(t == ARRIVAL.index((2, 4)[i]))
            def _(i=i):
                shard_copy(1 + i, *chips[i], cidx, sibling).wait_recv()
                shard_copy(4 + i, *chips[i], cidx, sibling).start()
                shard_copy((7, 3)[i], *chips[i], cidx, (*chips[1 - i], cidx), half=i).start()

        @pl.when(t == ARRIVAL.index(6))
        def _():
            shard_copy(3, *chips[2], cidx, sibling, half=1).wait_recv()
            shard_copy(7, *chips[2], cidx, sibling, half=0).wait_recv()
            shard_copy(6, *chips[2], cidx, sibling).start()

        for i in range(3):
            @pl.when(t == ARRIVAL.index((3, 5, 7)[i]))
            def _(i=i):
                shard_copy(4 + i, *chips[i], 1 - cidx, sibling).wait_recv()

        @pl.when(t == 2)
        def _():
            g_finish()

        dev = ord_ref[t]
        for r in range(L // (2 * ROWS)):
            rsl = slice(r * 2 * ROWS, (r + 1) * 2 * ROWS)
            z_ref[rsl, :] = _dot(hn_ref[rsl, :], wv[dev])
        col = pl.ds(pl.multiple_of(dev * ws, 128), ws)
        pltpu.make_async_copy(wv.at[dev], wfull_ref.at[:, col], hbm_sems.at[t]).start()

        @pl.when(t == N_DEV - 1)
        def _():
            for k in (0, 1, 2, 4, 5, 6):
                shard_copy(k, x, y, cidx, sibling).wait_send()
            for k in (3, 7):
                shard_copy(k, x, y, cidx, sibling, half=0).wait_send()
            for s in range(N_DEV):
                pltpu.make_async_copy(wv.at[0], wfull_ref.at[:, pl.ds(0, ws)], hbm_sems.at[s]).wait()

    const = lambda *shape: pl.BlockSpec(shape, lambda t, o: (0,) * len(shape))
    once = lambda *shape: pl.BlockSpec(shape, lambda t, o: (0,) * len(shape), pipeline_mode=pl.Buffered(1))
    return _call(
        body, name="front_project",
        out_shape=[jax.ShapeDtypeStruct((L, D_IN), F32), jax.ShapeDtypeStruct((L, D), BF16),
                   jax.ShapeDtypeStruct((D, D_IN), BF16), jax.ShapeDtypeStruct(w_out.shape, BF16),
                   jax.ShapeDtypeStruct((1, 3 * D), F32), jax.ShapeDtypeStruct((1, 3 * D), F32),
                   jax.ShapeDtypeStruct((N_DEV, D), F32), jax.ShapeDtypeStruct((CONV_W, D), F32),
                   jax.ShapeDtypeStruct((2, D), F32)],
        grid_spec=pltpu.PrefetchScalarGridSpec(
            num_scalar_prefetch=1, grid=(N_DEV,),
            in_specs=[once(L, D), const(1, D), const(1, D), once(D, nloc), const(1, 3 * D), const(1, D),
                      once(D, ws), once(*w_out.shape), HBM, HBM],
            out_specs=[pl.BlockSpec((L, ws), lambda t, o: (0, o[t])), const(L, D), HBM, const(*w_out.shape),
                       const(1, 3 * D), const(1, 3 * D), const(N_DEV, D), HBM, HBM],
            scratch_shapes=[pltpu.VMEM((N_DEV, D, ws), BF16), pltpu.VMEM((N_DEV, D), F32), pltpu.VMEM((16, nloc), F32),
                            pltpu.VMEM((N_DEV, 16, nloc), F32), pltpu.SemaphoreType.DMA((8,)),
                            pltpu.SemaphoreType.DMA((8,)), pltpu.SemaphoreType.DMA((N_DEV,)),
                            pltpu.SemaphoreType.DMA((2, N_DEV - 1)), pltpu.SemaphoreType.DMA((2, N_DEV - 1))]
            + _gather2_sems(2)),
        compiler_params=pltpu.CompilerParams(dimension_semantics=("arbitrary",), vmem_limit_bytes=VMEM_LIMIT,
                                             has_side_effects=True),
    )(order, xr, c, c_ctx, ada_w, ada_b, ng, w_in, w_out, pltpu.with_memory_space_constraint(cw, pltpu.HBM),
      pltpu.with_memory_space_constraint(lam, pltpu.HBM))


def _project(xr, mod, ng, w, ncols, tm, name, gather=None, gather_modes=()):
    rows = xr.shape[0]
    steps = rows // tm
    ng_ = len(gather or ())

    def body(x_ref, sh_ref, sc_ref, ng_ref, w_ref, *rest):
        z_ref, hn_ref = rest[ng_:ng_ + 2]
        if ng_:
            start, forward, finish = _gather2_ops(rest[:ng_], rest[ng_ + 2:2 * ng_ + 2], gather_modes,
                                                  *rest[2 * ng_ + 2:])
            pl.when(pl.program_id(0) == 0)(start)
            pl.when(pl.program_id(0) == steps // 2)(forward)
        x = x_ref[...]
        rs = lax.rsqrt(jnp.mean(x * x, axis=-1, keepdims=True) + NORM_EPS)
        hn = (x * rs * ng_ref[...]) * (1.0 + sc_ref[...]) + sh_ref[...]
        hb = hn.astype(BF16)
        hn_ref[...] = hb
        for n in range(ncols // D):
            z_ref[:, n * D:(n + 1) * D] = _dot(hb, w_ref[:, n * D:(n + 1) * D])
        if ng_:
            pl.when(pl.program_id(0) == steps - 1)(finish)

    vec = pl.BlockSpec((1, D), lambda i: (0, 0))
    gathered = _gather2_shapes(gather, gather_modes) if ng_ else []
    return _call(
        body, name=name, grid=(steps,),
        out_shape=[jax.ShapeDtypeStruct((rows, ncols), F32), jax.ShapeDtypeStruct((rows, D), BF16)] + gathered,
        in_specs=[pl.BlockSpec((tm, D), lambda i: (i, 0)), vec, pl.BlockSpec((1, D), lambda i: (0, 1)), vec,
                  pl.BlockSpec((D, ncols), lambda i: (0, 0), pipeline_mode=pl.Buffered(1))] + [HBM] * ng_,
        out_specs=[pl.BlockSpec((tm, ncols), lambda i: (i, 0)), pl.BlockSpec((tm, D), lambda i: (i, 0))] + [HBM] * ng_,
        scratch_shapes=_gather2_sems(ng_) if ng_ else [],
        compiler_params=pltpu.CompilerParams(dimension_semantics=("arbitrary",), vmem_limit_bytes=VMEM_LIMIT,
                                             has_side_effects=bool(ng_)),
    )(xr, mod, mod, ng, w, *[pltpu.with_memory_space_constraint(a, pltpu.HBM) for a in gather or ()])


def _scan_pair(af_ref, uf_ref, hf_ref, h0f, ab_ref, ub_ref, hb_ref, h0b, t_len):
    span = 8 * SCAN_BLOCKS
    nit = t_len // span
    rows = _rows((8, HD))

    def local_scan(a, b, forward):
        for s in (1, 2, 4):
            sh = s if forward else 8 - s
            m = rows >= s if forward else rows < 8 - s
            b = a * jnp.where(m, pltpu.roll(b, sh, 0), 0.0) + b
            a = a * jnp.where(m, pltpu.roll(a, sh, 0), 1.0)
        return a, b

    def span_scan(a_ref, u_ref, h_ref, off, carry, forward):
        order = range(SCAN_BLOCKS) if forward else range(SCAN_BLOCKS - 1, -1, -1)
        last = slice(7, 8) if forward else slice(0, 1)
        for q in order:
            rs = pl.ds(off + 8 * q, 8)
            a, b = local_scan(a_ref[rs, :], u_ref[rs, :], forward)
            h_ref[rs, :] = b + a * carry
            carry = a[last, :] * carry + b[last, :]
        return carry

    def body(k, carry):
        cf, cb = carry
        cf = span_scan(af_ref, uf_ref, hf_ref, pl.multiple_of(k * span, span), cf, True)
        cb = span_scan(ab_ref, ub_ref, hb_ref, pl.multiple_of((nit - 1 - k) * span, span), cb, False)
        return cf, cb

    return lax.fori_loop(0, nit, body, (h0f, h0b))


SCAN_BLOCKS = 8


def _shifted(pad_ref, x, offsets, before=0.0, after=0.0):
    n = x.shape[0]
    pad_ref[0:8, :] = jnp.broadcast_to(jnp.asarray(before, F32), (8, x.shape[1]))
    pad_ref[8:8 + n, :] = x
    pad_ref[8 + n:16 + n, :] = jnp.broadcast_to(jnp.asarray(after, F32), (8, x.shape[1]))
    return [pad_ref[8 + o:8 + o + n, :] for o in offsets]


def _conv(xa, cw, cb, pad_ref):
    xm1, xp1, xp2 = _shifted(pad_ref, xa, (-1, 1, 2))
    return xm1 * cw[0:1, :] + xa * cw[1:2, :] + xp1 * cw[2:3, :] + xp2 * cw[3:4, :] + cb


def _gates(xc, wa, wx, ba, bx, nsp):
    xb = xc.astype(BF16)
    r = _sigmoid(_dot(xb, wa) + ba)
    i = _sigmoid(_dot(xb, wx) + bx)
    log_a = r * nsp
    a = jnp.exp(log_a)
    g2 = jnp.tanh(log_a) * (-1.0 - a * a)
    rg = lax.rsqrt(jnp.maximum(g2, 1e-30))
    return r, i, a, g2 * rg, rg


def _lru_param_specs():
    h4 = pl.BlockSpec((2, 1, HD, HD), lambda h: (0, h, 0, 0))
    v2 = pl.BlockSpec((2, HD), lambda h: (0, h))
    return dict(
        xa=pl.BlockSpec((L, HD), lambda h: (0, h)), xac=pl.BlockSpec((LC, HD), lambda h: (0, h)),
        cw=pl.BlockSpec((CONV_W, HD), lambda h: (0, h)), cb=pl.BlockSpec((1, HD), lambda h: (0, h)), h4=h4, v2=v2)


def _lru_forward(zx, zc, cw, cb, wa, wx, ba, bx, lam, gather, gather_modes):
    ng_ = len(gather)

    def body(xa_ref, xac_ref, cw_ref, cb_ref, wa_ref, wx_ref, ba_ref, bx_ref, lam_ref, *rest):
        yl_ref = rest[ng_]
        af, uf, hf, ab, ub, hb, pad_s = rest[2 * ng_ + 1:2 * ng_ + 8]
        start, pass_on, finish = _gather2_ops(rest[:ng_], rest[ng_ + 1:2 * ng_ + 1], gather_modes,
                                              *rest[2 * ng_ + 8:])
        pl.when(pl.program_id(0) == 0)(start)
        pl.when(pl.program_id(0) == HEADS // 2)(pass_on)
        pl.when(pl.program_id(0) == HEADS - 1)(finish)
        cwv, cbv = cw_ref[...], cb_ref[...]
        nsp = (-LRU_C) * _softplus(-lam_ref[...])

        def forward(xa, t_len, h0f, h0b):
            xc = _conv(xa, cwv, cbv, pad_s)
            for d, (a_ref, u_ref) in enumerate(((af, uf), (ab, ub))):
                _, i, a, gamma, _ = _gates(xc, wa_ref[d, 0].astype(BF16), wx_ref[d, 0].astype(BF16),
                                           ba_ref[d:d + 1, :], bx_ref[d:d + 1, :], nsp[d:d + 1, :])
                a_ref[0:t_len, :] = a
                u_ref[0:t_len, :] = gamma * (i * xc)
            return _scan_pair(af, uf, hf, h0f, ab, ub, hb, h0b, t_len)

        z = jnp.zeros((1, HD), F32)
        h0f, h0b = forward(xac_ref[...], LC, z, z)
        forward(xa_ref[...], L, h0f, h0b)
        yl_ref[...] = hf[...] + hb[...]

    s = _lru_param_specs()
    return _call(
        body, name="lru_forward", grid=(HEADS,),
        out_shape=[jax.ShapeDtypeStruct((L, D), F32)] + _gather2_shapes(gather, gather_modes),
        in_specs=[s["xa"], s["xac"], s["cw"], s["cb"], s["h4"], s["h4"], s["v2"], s["v2"], s["v2"]] + [HBM] * ng_,
        out_specs=[pl.BlockSpec((L, HD), lambda h: (0, h))] + [HBM] * ng_,
        scratch_shapes=[pltpu.VMEM((L, HD), F32)] * 6 + [pltpu.VMEM((L + 16, HD), F32)] + _gather2_sems(ng_),
        compiler_params=pltpu.CompilerParams(dimension_semantics=("arbitrary",), vmem_limit_bytes=VMEM_LIMIT,
                                             has_side_effects=True),
    )(zx, zc, cw, cb, wa, wx, ba, bx, lam, *[pltpu.with_memory_space_constraint(a, pltpu.HBM) for a in gather])


def _lru_backward(zx, zc, dyl, dz, cw, cb, wa, wx, ba, bx, lam, chip_sums, first_chips=None):
    nr = len(chip_sums)

    def body(xa_ref, xac_ref, dyl_ref, dz_in, cw_ref, cb_ref, wa_ref, wx_ref, ba_ref, bx_ref, lam_ref, *rest):
        (dxa_ref, dxac_ref, dwa_ref, dwx_ref, dba_ref, dbx_ref, dlam_ref, dcw_ref,
         dcb_ref) = rest[nr:nr + 9]
        main_s, ctx_s, pad_s = rest[3 * nr + 9:3 * nr + 12]
        if nr:
            start, forward, finish = _chips_ops(rest[:nr], rest[nr + 9:2 * nr + 9], rest[2 * nr + 9:3 * nr + 9],
                                                *rest[3 * nr + 12:], first_chips=first_chips)
            pl.when(pl.program_id(0) == 0)(start)
            pl.when(pl.program_id(0) == HEADS // 2)(forward)
            pl.when(pl.program_id(0) == HEADS - 1)(finish)
        del dz_in
        cwv, cbv = cw_ref[...], cb_ref[...]
        lamv = lam_ref[...]
        sp = _softplus(-lamv)
        nsp = (-LRU_C) * sp
        z = jnp.zeros((1, HD), F32)

        def wmat(ref, d):
            return ref[d, 0].astype(BF16)

        def workspace(s):
            return dict(a=(s.at[0], s.at[1]), u=(s.at[2], s.at[3]), h=(s.at[4], s.at[5]), rho=(s.at[6], s.at[7]),
                        saved=(tuple(s.at[8 + k] for k in range(4)), tuple(s.at[12 + k] for k in range(4))),
                        xc=s.at[16])

        def forward(ws, xa, t_len, h0f, h0b):
            xc = _conv(xa, cwv, cbv, pad_s)
            ws["xc"][...] = xc
            for d in (0, 1):
                vals = _gates(xc, wmat(wa_ref, d), wmat(wx_ref, d), ba_ref[d:d + 1, :], bx_ref[d:d + 1, :],
                              nsp[d:d + 1, :])
                r, i, a, gamma, rg = vals
                ws["a"][d][...] = a
                ws["u"][d][...] = gamma * (i * xc)
                for ref, val in zip(ws["saved"][d], (r, i, gamma, rg)):
                    ref[...] = val
            return _scan_pair(ws["a"][0], ws["u"][0], ws["h"][0], h0f, ws["a"][1], ws["u"][1], ws["h"][1], h0b,
                              t_len)

        def backward(ws, xa, t_len, h0f, h0b, dhf, dhb, first):
            xc = ws["xc"][...]
            (af, ab), (uf, ub), (hf, hb), (rf, rb) = ws["a"], ws["u"], ws["h"], ws["rho"]
            uf[...] = ab[...] * dhb
            ub[...] = af[...] * dhf
            rho_b_last, rho_f_first = _scan_pair(ab, uf, rb, z, af, ub, rf, z, t_len)
            dxc = jnp.zeros((t_len, HD), F32)
            dsp = []
            for d in (0, 1):
                r, i, gamma, rg = (ref[...] for ref in ws["saved"][d])
                a = ws["a"][d][...]
                if d == 0:
                    lam_t = dhf + _shifted(pad_s, rf[...], (1,))[0]
                    h_prev = _shifted(pad_s, hf[...], (-1,), before=h0f)[0]
                else:
                    lam_t = dhb + _shifted(pad_s, rb[...], (-1,))[0]
                    h_prev = _shifted(pad_s, hb[...], (1,), after=h0b)[0]
                da = lam_t * h_prev
                lx = lam_t * xc
                d_i = lx * gamma
                d_gamma = lx * i
                dxc = dxc + lam_t * (gamma * i)
                d_log_a = a * (da - d_gamma * (a * rg))
                dsp.append(jnp.sum(d_log_a * r, axis=0, keepdims=True) * (-LRU_C))
                d_pre_r = d_log_a * nsp[d:d + 1, :] * (r * (1.0 - r))
                d_pre_i = d_i * (i * (1.0 - i))
                prb, pib, xb = d_pre_r.astype(BF16), d_pre_i.astype(BF16), xc.astype(BF16)
                dxc = dxc + _dot_nt(prb, wmat(wa_ref, d)) + _dot_nt(pib, wmat(wx_ref, d))
                g_wa, g_wx = _dot_tn(xb, prb), _dot_tn(xb, pib)
                g_ba = jnp.sum(d_pre_r, axis=0, keepdims=True)
                g_bx = jnp.sum(d_pre_i, axis=0, keepdims=True)
                if first:
                    dwa_ref[d, 0] = g_wa
                    dwx_ref[d, 0] = g_wx
                    dba_ref[d:d + 1, :] = g_ba
                    dbx_ref[d:d + 1, :] = g_bx
                else:
                    dwa_ref[d, 0] += g_wa
                    dwx_ref[d, 0] += g_wx
                    dba_ref[d:d + 1, :] += g_ba
                    dbx_ref[d:d + 1, :] += g_bx
            g_lam = jnp.concatenate(dsp, axis=0) * (-_sigmoid(-lamv))
            dm1, dp1, dm2 = _shifted(pad_s, dxc, (-1, 1, -2))
            dxa = dp1 * cwv[0:1, :] + dxc * cwv[1:2, :] + dm1 * cwv[2:3, :] + dm2 * cwv[3:4, :]
            xm1, xp1, xp2 = _shifted(pad_s, xa, (-1, 1, 2))
            g_cw = jnp.concatenate([jnp.sum(dxc * v, axis=0, keepdims=True) for v in (xm1, xa, xp1, xp2)], axis=0)
            g_cb = jnp.sum(dxc, axis=0, keepdims=True)
            if first:
                dlam_ref[...] = g_lam
                dcw_ref[...] = g_cw
                dcb_ref[...] = g_cb
            else:
                dlam_ref[...] += g_lam
                dcw_ref[...] += g_cw
                dcb_ref[...] += g_cb
            return dxa, rho_f_first, rho_b_last

        ws_x, ws_c = workspace(main_s), workspace(ctx_s)
        h0f, h0b = forward(ws_c, xac_ref[...], LC, z, z)
        forward(ws_x, xa_ref[...], L, h0f, h0b)
        dh = dyl_ref[...]
        dxa, dh0f, dh0b = backward(ws_x, xa_ref[...], L, h0f, h0b, dh, dh, True)
        dxa_ref[...] = dxa.astype(BF16)
        rc = _rows((LC, HD))
        dxac, _, _ = backward(ws_c, xac_ref[...], LC, z, z, jnp.where(rc == LC - 1, dh0f, 0.0),
                              jnp.where(rc == 0, dh0b, 0.0), False)
        dxac_ref[...] = dxac.astype(BF16)

    s = _lru_param_specs()
    col = lambda r: pl.BlockSpec((r, HD), lambda h: (0, h))
    return _call(
        body, name="lru_backward", grid=(HEADS,),
        out_shape=[jax.ShapeDtypeStruct((L, D_IN), BF16), jax.ShapeDtypeStruct((LC, D), BF16),
                   jax.ShapeDtypeStruct((2, HEADS, HD, HD), F32), jax.ShapeDtypeStruct((2, HEADS, HD, HD), F32),
                   jax.ShapeDtypeStruct((2, D), F32), jax.ShapeDtypeStruct((2, D), F32),
                   jax.ShapeDtypeStruct((2, D), F32), jax.ShapeDtypeStruct((CONV_W, D), F32),
                   jax.ShapeDtypeStruct((1, D), F32)] + [jax.ShapeDtypeStruct((4,) + a.shape[1:], a.dtype)
                                                          for a in chip_sums] + _chips_stage_shapes(chip_sums),
        in_specs=[s["xa"], s["xac"], col(L), pl.BlockSpec(memory_space=pl.ANY), s["cw"], s["cb"], s["h4"], s["h4"],
                  s["v2"], s["v2"], s["v2"]] + [HBM] * nr,
        out_specs=[col(L), col(LC), s["h4"], s["h4"], s["v2"], s["v2"], s["v2"], col(CONV_W), col(1)]
        + [HBM] * (2 * nr),
        scratch_shapes=[pltpu.VMEM((17, L, HD), F32), pltpu.VMEM((17, LC, HD), F32), pltpu.VMEM((L + 16, HD), F32)]
        + (_chips_sems(nr) if nr else []),
        input_output_aliases={3: 0},
        compiler_params=pltpu.CompilerParams(dimension_semantics=("arbitrary",), vmem_limit_bytes=VMEM_LIMIT,
                                             has_side_effects=True),
    )(zx, zc, dyl, dz, cw, cb, wa, wx, ba, bx, lam, *[pltpu.with_memory_space_constraint(a, pltpu.HBM)
                                                       for a in chip_sums])


def _mixer_loss(x, tgt, zx, yl, gx, fg, lng, lnb, ws, wst, bst, wout, tm):
    ncht = tm // CHUNK

    def body(x_ref, t_ref, ga_ref, u_ref, v_ref, gb_ref, yl_ref, gx_ref, fg_ref, lng_ref, lnb_ref, ws_ref, wst_ref,
             bst_ref, wout_ref,
             dz_ref, dyl_ref, dxn_ref, y_s, do_ref, dws_ref, dbst_ref, vec_ref,
             vn_s, mix_s, dm_s, dvn_s):
        step = pl.program_id(0)

        @pl.when(step == 0)
        def _():
            dws_ref[...] = jnp.zeros_like(dws_ref)
            dbst_ref[...] = jnp.zeros_like(dbst_ref)
            vec_ref[...] = jnp.zeros_like(vec_ref)

        u, v = u_ref[...], v_ref[...]
        ug, dug_du = _gelu_and_grad(u)
        vg, dvg_dv = _gelu_and_grad(v)
        mu = jnp.mean(vg, axis=-1, keepdims=True)
        vc = vg - mu
        rstd = lax.rsqrt(jnp.mean(vc * vc, axis=-1, keepdims=True) + LN_EPS)
        vhat = vc * rstd
        lngv = lng_ref[...]
        vn_s[...] = (vhat * lngv + lnb_ref[...]).astype(BF16)
        for ch in range(ncht):
            rs = slice(ch * CHUNK, (ch + 1) * CHUNK)
            for g in range(HEADS):
                cs = slice(g * HD, (g + 1) * HD)
                mix_s[rs, cs] = _dot(ws_ref[g], vn_s[rs, cs]) + bst_ref[:, g:g + 1]
        mixed = mix_s[...]
        ga, gb, yl = ga_ref[...], gb_ref[...], yl_ref[...]
        sga, dsga = _silu_and_grad(ga)
        sgb, dsgb = _silu_and_grad(gb)
        ys = ug * mixed
        y_s[:, 0:D] = (yl * sga).astype(BF16)
        y_s[:, D:D_MIX] = (ys * sgb).astype(BF16)
        o = _dot(y_s[...], wout_ref[...])
        gxv, fgv = gx_ref[...], fg_ref[...]
        xn = x_ref[...] + gxv * o
        rs2 = lax.rsqrt(jnp.mean(xn * xn, axis=-1, keepdims=True) + NORM_EPS)
        xh = xn * rs2
        diff = xh * fgv - t_ref[...]
        vec_ref[R_LOSS:R_LOSS + 1, :] += jnp.full((1, D), jnp.sum(diff * diff) * (0.5 / D), F32)
        dout = diff * (1.0 / D)
        w = dout * fgv
        dxn = rs2 * (w - xh * jnp.mean(w * xh, axis=-1, keepdims=True))
        dxn_ref[...] = dxn
        vec_ref[0:1, :] += jnp.sum(dxn * o, axis=0, keepdims=True)
        vec_ref[1:2, :] += jnp.sum(dout * xh, axis=0, keepdims=True)
        dob = (dxn * gxv).astype(BF16)
        do_ref[...] = dob
        dy = _dot_nt(dob, wout_ref[...])
        dya, dyb = dy[:, 0:D], dy[:, D:D_MIX]
        dyl_ref[...] = dya * sga
        dys = dyb * sgb
        dz_ref[:, 0:D] = jnp.zeros((tm, D), BF16)
        dz_ref[:, D:2 * D] = (dya * yl * dsga).astype(BF16)
        dz_ref[:, 2 * D:3 * D] = (dys * mixed * dug_du).astype(BF16)
        dz_ref[:, 4 * D:5 * D] = (dyb * ys * dsgb).astype(BF16)
        dm = dys * ug
        dm_s[...] = dm.astype(BF16)
        for g in range(HEADS):
            cs = slice(g * HD, (g + 1) * HD)
            dbst_ref[:, g:g + 1] += sum(jnp.sum(dm[ch * CHUNK:(ch + 1) * CHUNK, cs], axis=1, keepdims=True)
                                        for ch in range(ncht))
            for ch in range(ncht):
                rs = slice(ch * CHUNK, (ch + 1) * CHUNK)
                dws_ref[g] += _dot_nt(dm_s[rs, cs], vn_s[rs, cs])
                dvn_s[rs, cs] = _dot(wst_ref[g], dm_s[rs, cs])
        dvn = dvn_s[...]
        vec_ref[2:3, :] += jnp.sum(dvn * vhat, axis=0, keepdims=True)
        vec_ref[3:4, :] += jnp.sum(dvn, axis=0, keepdims=True)
        dvh = dvn * lngv
        dvg = rstd * (dvh - jnp.mean(dvh, axis=-1, keepdims=True) - vhat * jnp.mean(dvh * vhat, axis=-1, keepdims=True))
        dz_ref[:, 3 * D:4 * D] = (dvg * dvg_dv).astype(BF16)

    tile = pl.BlockSpec((tm, D), lambda i: (i, 0))
    zcol = lambda n: pl.BlockSpec((tm, D), lambda i: (i, n))
    vec = pl.BlockSpec((1, D), lambda i: (0, 0))
    full = lambda *s: pl.BlockSpec(s, lambda i: (0,) * len(s))
    return _call(
        body, name="mixer_loss", grid=(L // tm,),
        out_shape=[jax.ShapeDtypeStruct((L, D_IN), BF16), jax.ShapeDtypeStruct((L, D), F32),
                   jax.ShapeDtypeStruct((L, D), F32), jax.ShapeDtypeStruct((L, D_MIX), BF16),
                   jax.ShapeDtypeStruct((L, D), BF16),
                   jax.ShapeDtypeStruct((HEADS, CHUNK, CHUNK), F32), jax.ShapeDtypeStruct((CHUNK, HEADS), F32),
                   jax.ShapeDtypeStruct((8, D), F32)],
        in_specs=[tile, tile, zcol(1), zcol(2), zcol(3), zcol(4), tile, pl.BlockSpec((1, D), lambda i: (0, 2)),
                  vec, vec, vec,
                  full(HEADS, CHUNK, CHUNK), full(HEADS, CHUNK, CHUNK), full(CHUNK, HEADS),
                  pl.BlockSpec((D_MIX, D), lambda i: (0, 0), pipeline_mode=pl.Buffered(1))],
        out_specs=[pl.BlockSpec((tm, D_IN), lambda i: (i, 0)), tile, tile,
                   pl.BlockSpec((tm, D_MIX), lambda i: (i, 0)), tile,
                   full(HEADS, CHUNK, CHUNK), full(CHUNK, HEADS), full(8, D)],
        scratch_shapes=[pltpu.VMEM((tm, D), BF16), pltpu.VMEM((tm, D), F32),
                        pltpu.VMEM((tm, D), BF16), pltpu.VMEM((tm, D), F32)],
        compiler_params=_params("arbitrary"),
    )(x, tgt, zx, zx, zx, zx, yl, gx, fg, lng, lnb, ws, wst, bst, wout)


def _grad_w(a, b, a2, b2, tk, name, bw, first, nblocks, split):
    nk = a.shape[0] // tk
    m = a.shape[1]
    with_ctx = a2 is not None
    if split == "cols":
        slots, r, w = nblocks, m, bw // 2
        piece = lambda q, pc: (slice(None), slice(pc * w, (pc + 1) * w))
    else:
        slots, r, w = 4, m // 8, bw
        piece = lambda q, pc: (slice((2 * q + pc) * r, (2 * q + pc + 1) * r), slice(None))

    def body(*refs):
        a_ref, b_ref = refs[:2]
        a2_ref, b2_ref = refs[2:4] if with_ctx else (None, None)
        sums_ref, acc, mine_v, send_v, stage_v, send_sems, recv_sems = refs[4 if with_ctx else 2:]
        n, k = pl.program_id(0), pl.program_id(1)
        x, y, c = lax.axis_index("x"), lax.axis_index("y"), lax.axis_index("c")

        def to_sibling(s):
            return pltpu.make_async_remote_copy(src_ref=send_v.at[s], dst_ref=stage_v.at[s], send_sem=send_sems.at[s],
                                                recv_sem=recv_sems.at[s], device_id=(x, y, 1 - c),
                                                device_id_type=MESH)

        @pl.when(k == 0)
        def _():
            acc[...] = jnp.zeros_like(acc)

        acc[...] += _dot_tn(a_ref[...], b_ref[...])

        if with_ctx:
            @pl.when(jnp.logical_and(k == nk - 1, n == 0))
            def _():
                acc[:, 0:b2_ref.shape[1]] += _dot_tn(a2_ref[...], b2_ref[...])

        def hand_over(s, q):
            for pc in (0, 1):
                @pl.when(c == pc)
                def _(pc=pc):
                    mine_v[s] = acc[piece(q, pc)]
                    send_v[s] = acc[piece(q, 1 - pc)].astype(BF16)
            to_sibling(s).start()

        for i in range(nblocks):
            @pl.when(jnp.logical_and(k == nk - 1, n == i))
            def _(i=i):
                if split == "cols":
                    hand_over(i, 0)
                else:
                    for q in range(4):
                        hand_over(q, q)

        @pl.when(jnp.logical_and(k == nk - 1, n == nblocks - 1))
        def _():
            for s in range(slots):
                to_sibling(s).wait_recv()
                sums_ref[s] = (mine_v[s] + stage_v[s].astype(F32)).astype(BF16)
            for s in range(slots):
                to_sibling(s).wait_send()

    in_specs = [pl.BlockSpec((tk, m), lambda n, k: (k, 0)), pl.BlockSpec((tk, bw), lambda n, k: (k, n + first))]
    args = [a, b]
    if with_ctx:
        in_specs += [pl.BlockSpec(a2.shape, lambda n, k: (0, 0)), pl.BlockSpec(b2.shape, lambda n, k: (0, 0))]
        args += [a2, b2]
    return _call(
        body, name=name, grid=(nblocks, nk), out_shape=jax.ShapeDtypeStruct((slots, r, w), BF16),
        in_specs=in_specs, out_specs=pl.BlockSpec((slots, r, w), lambda n, k: (0, 0, 0)),
        scratch_shapes=[pltpu.VMEM((m, bw), F32), pltpu.VMEM((slots, r, w), F32), pltpu.VMEM((slots, r, w), BF16),
                        pltpu.VMEM((slots, r, w), BF16), pltpu.SemaphoreType.DMA((slots,)),
                        pltpu.SemaphoreType.DMA((slots,))],
        compiler_params=pltpu.CompilerParams(dimension_semantics=("arbitrary", "arbitrary"),
                                             vmem_limit_bytes=VMEM_LIMIT, has_side_effects=True),
    )(*args)


def _grad_rows(xr, dz, w, mod, ng, dres, ncols, tm, name, chip_sums=(), first_chips=None, dests=None):
    rows = xr.shape[0]
    steps = rows // tm
    with_dx = dres is not None
    nr = len(chip_sums)
    dests = [d for d in (dests or [None] * nr)]
    nd = sum(d is not None for d in dests)
    nin = 6 if with_dx else 5
    nout = 2 if with_dx else 1

    def body(*refs):
        if with_dx:
            x_ref, dz_ref, w_ref, sc_ref, ng_ref, dres_ref = refs[:nin]
            dx_ref, vec_ref = refs[nin + nr + nd:nin + nr + nd + nout]
        else:
            x_ref, dz_ref, w_ref, sc_ref, ng_ref = refs[:nin]
            (vec_ref,) = refs[nin + nr + nd:nin + nr + nd + nout]
        if nr:
            o0 = nin + nr + nd + nout
            start, forward, finish = _chips_ops(refs[nin:nin + nr], refs[o0:o0 + nr], refs[o0 + nr:o0 + 2 * nr],
                                                *refs[o0 + 2 * nr:], first_chips=first_chips)
            pl.when(pl.program_id(0) == 0)(start)
            pl.when(pl.program_id(0) == 2)(forward)
            pl.when(pl.program_id(0) == steps - 1)(finish)

        @pl.when(pl.program_id(0) == 0)
        def _():
            vec_ref[...] = jnp.zeros_like(vec_ref)

        dhn = _dot_nt(dz_ref[...], w_ref[...])
        x = x_ref[...]
        rs = lax.rsqrt(jnp.mean(x * x, axis=-1, keepdims=True) + NORM_EPS)
        xh = x * rs
        ngv = ng_ref[...]
        y = xh * ngv
        vec_ref[0:1, :] += jnp.sum(dhn, axis=0, keepdims=True)
        vec_ref[1:2, :] += jnp.sum(dhn * y, axis=0, keepdims=True)
        dy = dhn * (1.0 + sc_ref[...])
        vec_ref[2:3, :] += jnp.sum(dy * xh, axis=0, keepdims=True)
        if with_dx:
            dxh = dy * ngv
            dx_ref[...] = dres_ref[...] + rs * (dxh - xh * jnp.mean(dxh * xh, axis=-1, keepdims=True))

    tile = pl.BlockSpec((tm, D), lambda i: (i, 0))
    vec = pl.BlockSpec((1, D), lambda i: (0, 0))
    in_specs = [tile, pl.BlockSpec((tm, ncols), lambda i: (i, 0)),
                pl.BlockSpec((D, ncols), lambda i: (0, 0), pipeline_mode=pl.Buffered(1)),
                pl.BlockSpec((1, D), lambda i: (0, 1)), vec]
    out_shape = [jax.ShapeDtypeStruct((8, D), F32)]
    out_specs = [pl.BlockSpec((8, D), lambda i: (0, 0))]
    args = [xr, dz, w, mod, ng]
    if with_dx:
        in_specs.append(tile)
        out_shape.insert(0, jax.ShapeDtypeStruct((rows, D), F32))
        out_specs.insert(0, tile)
        args.append(dres)
    aliases = {}
    for j, d in enumerate(dests):
        if d is not None:
            aliases[len(args) + nr + len(aliases)] = len(out_shape) + j
    in_specs += [HBM] * (nr + nd)
    out_specs += [HBM] * (2 * nr)
    out_shape += [jax.ShapeDtypeStruct((4,) + a.shape[1:], a.dtype) for a in chip_sums]
    out_shape += _chips_stage_shapes(chip_sums)
    args += [pltpu.with_memory_space_constraint(a, pltpu.HBM) for a in chip_sums]
    args += [pltpu.with_memory_space_constraint(d, pltpu.HBM) for d in dests if d is not None]
    return _call(body, name=name, grid=(steps,), out_shape=out_shape, in_specs=in_specs, out_specs=out_specs,
                 scratch_shapes=_chips_sems(nr) if nr else [], input_output_aliases=aliases,
                 compiler_params=pltpu.CompilerParams(dimension_semantics=("arbitrary",),
                                                      vmem_limit_bytes=VMEM_LIMIT, has_side_effects=bool(nr)))(*args)


def _adamw(w, g, m, v):
    m = ADAM_B1 * m + (1.0 - ADAM_B1) * g
    v = ADAM_B2 * v + (1.0 - ADAM_B2) * (g * g)
    m_hat = m / (1.0 - ADAM_B1 ** ADAM_STEP)
    v_hat = v / (1.0 - ADAM_B2 ** ADAM_STEP)
    delta = -ADAM_LR * (m_hat / (jnp.sqrt(v_hat) + ADAM_EPS) + ADAM_WD * w)
    return delta, m, v


def _adamw_reduced(parts, w, m, v, tr, name):
    r, n = w.shape
    nparts = parts.shape[0]

    def body(p_ref, w_ref, m_ref, v_ref, g_ref, d_ref, mo_ref, vo_ref):
        g = p_ref[0].astype(F32)
        for i in range(1, nparts):
            g = g + p_ref[i].astype(F32)
        g_ref[...] = g
        d_ref[...], mo_ref[...], vo_ref[...] = _adamw(w_ref[...], g, m_ref[...], v_ref[...])

    tile = pl.BlockSpec((tr, n), lambda i: (i, 0))
    sds = jax.ShapeDtypeStruct((r, n), F32)
    return _call(
        body, name=name, grid=(r // tr,), out_shape=[sds] * 4,
        in_specs=[pl.BlockSpec((nparts, tr, n), lambda i: (0, i, 0)), tile, tile, tile], out_specs=[tile] * 4,
        compiler_params=_params("arbitrary"),
    )(parts, w, m, v)


R_GATE, R_FINAL_G, R_LN_G, R_LN_B, R_LOSS = 0, 1, 2, 3, 4
R_SH_X, R_SC_X, R_NG_X = 5, 6, 7
R_SH_C, R_SC_C, R_NG_C = 8, 9, 10
R_BA, R_BX, R_LAM, R_CW, R_CB, R_SGU_B = 11, 13, 15, 17, 21, 22
PACK_ROWS = 32
MAT_ROWS = 2 * (2 * HEADS * HD) + HEADS * CHUNK


def _reduce_small(vp_all, mat_parts, ada_w, me):
    nloc = ada_w.shape[1]

    def body(me_ref, vp_ref, mp_ref, w_ref, red_ref, mat_ref, dmod_ref, gab_ref, cpart_ref, dmc_s):
        red = vp_ref[0]
        for i in range(1, N_DEV):
            red = red + vp_ref[i]
        mat = mp_ref[0].astype(F32)
        for i in range(1, mp_ref.shape[0]):
            mat = mat + mp_ref[i].astype(F32)
        red_ref[...] = red
        mat_ref[...] = mat
        for e in range(N_DEV):
            dmod_ref[e:e + 1, 0:D] = vp_ref[e, R_SH_X:R_SH_X + 1, :]
            dmod_ref[e:e + 1, D:2 * D] = vp_ref[e, R_SC_X:R_SC_X + 1, :]
            dmod_ref[e:e + 1, 2 * D:3 * D] = vp_ref[e, R_GATE:R_GATE + 1, :]
        dmod_ref[8:9, 0:D] = red[R_SH_C:R_SH_C + 1, :]
        dmod_ref[8:9, D:2 * D] = red[R_SC_C:R_SC_C + 1, :]
        dmod_ref[8:9, 2 * D:3 * D] = jnp.zeros((1, D), F32)
        dmod_ref[9:16, :] = jnp.zeros((7, 3 * D), F32)
        gab_ref[:, 0:D] = red[R_SH_X:R_SH_X + 1, :] + red[R_SH_C:R_SH_C + 1, :]
        gab_ref[:, D:2 * D] = red[R_SC_X:R_SC_X + 1, :] + red[R_SC_C:R_SC_C + 1, :]
        gab_ref[:, 2 * D:3 * D] = red[R_GATE:R_GATE + 1, :]
        dmc_s[...] = jnp.broadcast_to(dmod_ref[8:9, :], (8, 3 * D))
        off = pl.multiple_of(me_ref[0] * nloc, 128)
        cpart_ref[...] = _dot_nt(dmc_s[:, pl.ds(off, nloc)], w_ref[...])

    return _call(
        body, name="reduce_small",
        out_shape=[jax.ShapeDtypeStruct((PACK_ROWS, D), F32), jax.ShapeDtypeStruct(mat_parts.shape[1:], F32),
                   jax.ShapeDtypeStruct((16, 3 * D), F32), jax.ShapeDtypeStruct((1, 3 * D), F32),
                   jax.ShapeDtypeStruct((8, D), F32)],
        in_specs=[pl.BlockSpec(memory_space=pltpu.SMEM), VMEM, VMEM, VMEM], out_specs=[VMEM] * 5,
        scratch_shapes=[pltpu.VMEM((8, 3 * D), F32)], compiler_params=_params(),
    )(me, vp_all, mat_parts, ada_w)


def _adamw_ada(c_all, c_ctx, dmod, w, m, v, me):
    nloc = w.shape[1]

    def body(me_ref, c_ref, cc_ref, dm_ref, w_ref, m_ref, v_ref, g_ref, d_ref, mo_ref, vo_ref):
        off = pl.multiple_of(me_ref[0] * nloc, 128)
        dm = dm_ref[:, pl.ds(off, nloc)]
        sx, _ = _silu_and_grad(c_ref[...])
        sc, _ = _silu_and_grad(cc_ref[...])
        g = _dot_tn(sx, dm[0:8, :]) + _dot_tn(jnp.broadcast_to(sc, (8, D)), dm[8:16, :])
        g_ref[...] = g
        d_ref[...], mo_ref[...], vo_ref[...] = _adamw(w_ref[...], g, m_ref[...], v_ref[...])

    sds = jax.ShapeDtypeStruct(w.shape, F32)
    return _call(
        body, name="adamw_ada_w", out_shape=[sds] * 4,
        in_specs=[pl.BlockSpec(memory_space=pltpu.SMEM)] + [VMEM] * 6, out_specs=[VMEM] * 4,
        compiler_params=_params(),
    )(me, c_all, c_ctx, dmod, w, m, v)


_SMALL = ("c_ctx", "ada_b", "norm_g", "conv_w", "conv_b", "lru_wa", "lru_ba", "lru_wx", "lru_bx", "lru_lambda",
          "sgu_ln_g", "sgu_ln_b", "sgu_w", "sgu_b", "final_g")


def _adamw_small(red, mat, cparts, gab, ws, ms, vs, me):
    n = len(_SMALL)
    nw = 2 * HEADS * HD

    def body(me_ref, red_ref, mat_ref, cp_ref, gab_ref, *refs):
        w_refs, m_refs, v_refs = refs[:n], refs[n:2 * n], refs[2 * n:3 * n]
        outs = refs[3 * n:]
        off = pl.multiple_of(me_ref[0] * HD, 128)

        def row(r, k=1):
            return red_ref[r:r + k, :]

        cc = w_refs[0][...]
        dcc = cp_ref[0, 0:1, :]
        for i in range(1, N_DEV):
            dcc = dcc + cp_ref[i, 0:1, :]
        grads = dict(
            c_ctx=dcc * _silu_and_grad(cc)[1], ada_b=gab_ref[...], norm_g=row(R_NG_X) + row(R_NG_C),
            conv_w=red_ref[R_CW:R_CW + CONV_W, pl.ds(off, HD)], conv_b=row(R_CB),
            lru_wa=mat_ref[0:nw, :], lru_ba=row(R_BA, 2), lru_wx=mat_ref[nw:2 * nw, :], lru_bx=row(R_BX, 2),
            lru_lambda=red_ref[R_LAM:R_LAM + 2, pl.ds(off, HD)], sgu_ln_g=row(R_LN_G), sgu_ln_b=row(R_LN_B),
            sgu_w=mat_ref[2 * nw:MAT_ROWS, :], sgu_b=row(R_SGU_B), final_g=row(R_FINAL_G))
        for j, name in enumerate(_SMALL):
            g = grads[name]
            outs[j][...] = g
            outs[n + j][...], outs[2 * n + j][...], outs[3 * n + j][...] = _adamw(w_refs[j][...], g, m_refs[j][...],
                                                                                 v_refs[j][...])

    sds = [jax.ShapeDtypeStruct(ws[k].shape, F32) for k in _SMALL]
    outs = _call(
        body, name="adamw_small", out_shape=sds * 4,
        in_specs=[pl.BlockSpec(memory_space=pltpu.SMEM)] + [VMEM] * (4 + 3 * n), out_specs=[VMEM] * (4 * n),
        compiler_params=_params(),
    )(me, red, mat, cparts, gab, *[ws[k] for k in _SMALL], *[ms[k] for k in _SMALL], *[vs[k] for k in _SMALL])
    return [dict(zip(_SMALL, outs[i * n:(i + 1) * n])) for i in range(4)]


def kernel(x, c, ctx, c_ctx, ada_w, ada_b, norm_g, w_in, conv_w, conv_b, lru_wa, lru_ba, lru_wx, lru_bx, lru_lambda, sgu_ln_g, sgu_ln_b, sgu_w, sgu_b, w_out, final_g, loss_target, m_c_ctx, m_ada_w, m_ada_b, m_norm_g, m_w_in, m_conv_w, m_conv_b, m_lru_wa, m_lru_ba, m_lru_wx, m_lru_bx, m_lru_lambda, m_sgu_ln_g, m_sgu_ln_b, m_sgu_w, m_sgu_b, m_w_out, m_final_g, v_c_ctx, v_ada_w, v_ada_b, v_norm_g, v_w_in, v_conv_w, v_conv_b, v_lru_wa, v_lru_ba, v_lru_wx, v_lru_bx, v_lru_lambda, v_sgu_ln_g, v_sgu_ln_b, v_sgu_w, v_sgu_b, v_w_out, v_final_g):
    args = dict(locals())
    me_s = 4 * lax.axis_index("x") + 2 * lax.axis_index("y") + lax.axis_index("c")
    me = me_s.astype(jnp.int32).reshape(1)
    xr, ctxr, tgt = x[0], ctx[0], loss_target[0]
    cc = c_ctx.reshape(1, D)
    nw = 2 * HEADS * HD
    view = dict(c_ctx=(1, D), ada_b=(1, 3 * D), norm_g=(1, D), conv_w=(CONV_W, HD), conv_b=(1, D), lru_wa=(nw, HD),
                lru_ba=(2, D), lru_wx=(nw, HD), lru_bx=(2, D), lru_lambda=(2, HD), sgu_ln_g=(1, D), sgu_ln_b=(1, D),
                sgu_w=(HEADS * CHUNK, CHUNK), sgu_b=(1, D), final_g=(1, D))

    zx, hn, w_full, w_out_b, modx, modc, c_all, cw_full, lam_full = _front_project(
        xr, c, cc, ada_w[0], ada_b, norm_g, w_in[0], w_out[0], conv_w[0], lru_lambda[0], me)
    zc, hnc = _project(ctxr, modc, norm_g, w_full, D, LC, "project_ctx")
    ba, bx = lru_ba.reshape(2, D), lru_bx.reshape(2, D)
    yl, wout_all = _lru_forward(zx, zc, cw_full, conv_b, lru_wa[0], lru_wx[0], ba, bx, lam_full, [w_out_b], ["ag"])
    wout_full = wout_all.reshape(D_MIX, D)
    ws_b = sgu_w[0].astype(BF16)
    dz, dyl, dxn, ycat, dob, dws, dbst, mvec = _mixer_loss(
        xr, tgt, zx, yl, modx, final_g.reshape(1, D), sgu_ln_g, sgu_ln_b, ws_b, jnp.swapaxes(ws_b, 1, 2),
        sgu_b[0].T, wout_full, 256)

    wout_sums = _grad_w(ycat, dob, None, None, 1024, "grad_w_out", D, 0, 1, "rows")
    rest_sums = _grad_w(hn, dz, None, None, 1024, "grad_w_in_rest", 2 * W_IN_SHARD, 1, 3, "cols")
    dz, dxac, dwa, dwx, dba, dbx, dlam, dcw, dcb, win_parts, wout_parts, _, _ = _lru_backward(
        zx, zc, dyl, dz, cw_full, conv_b, lru_wa[0], lru_wx[0], ba, bx, lam_full, [rest_sums, wout_sums],
        first_chips=[1, 0])
    first_sums = _grad_w(hn, dz, hnc, dxac, 1024, "grad_w_in_first", 2 * W_IN_SHARD, 0, 1, "cols")
    matpack = jnp.concatenate([dwa.reshape(nw, HD), dwx.reshape(nw, HD), dws.reshape(HEADS * CHUNK, CHUNK)],
                              axis=0).astype(BF16)
    (mat_sums,) = _reduce2_local([matpack.reshape(N_DEV, MAT_ROWS // N_DEV, HD)], ["a2a"], me, "reduce_mat")
    gx, xvec, win_parts, mat_parts, _, _ = _grad_rows(
        xr, dz, w_full, modx, norm_g, dxn, D_IN, 256, "grad_rows_x", chip_sums=[first_sums, mat_sums],
        first_chips=[0, 0], dests=[win_parts, None])
    (cvec,) = _grad_rows(ctxr, dxac, w_full, modc, norm_g, None, D, LC, "grad_rows_ctx")
    pack = jnp.concatenate([mvec[0:5], xvec[0:3], cvec[0:3], dba, dbx, dlam, dcw, dcb, dbst.T.reshape(1, D),
                            jnp.zeros((PACK_ROWS - R_SGU_B - 1, D), F32)], axis=0)
    (vp_all,) = _gather2([pack], ["ag"], "gather_pack")
    red, matpiece, dmod, gab, cpart = _reduce_small(vp_all, mat_parts, ada_w[0], me)
    mat_all, cparts = _gather2([matpiece, cpart], ["ag", "ag"], "gather_small")

    g_w_in, d_w_in, nm_w_in, nv_w_in = _adamw_reduced(win_parts, w_in[0], m_w_in[0], v_w_in[0], 256, "adamw_w_in")
    g_w_out, d_w_out, nm_w_out, nv_w_out = _adamw_reduced(wout_parts, w_out[0], m_w_out[0], v_w_out[0], 128,
                                                          "adamw_w_out")
    g_ada, d_ada, nm_ada, nv_ada = _adamw_ada(c_all, cc, dmod, ada_w[0], m_ada_w[0], v_ada_w[0], me)
    ws = {k: args[k].reshape(view[k]) for k in _SMALL}
    ms = {k: args["m_" + k].reshape(view[k]) for k in _SMALL}
    vs = {k: args["v_" + k].reshape(view[k]) for k in _SMALL}
    small = _adamw_small(red, mat_all.reshape(MAT_ROWS, HD), cparts, gab, ws, ms, vs, me)
    big = dict(w_in=(g_w_in, d_w_in, nm_w_in, nv_w_in), w_out=(g_w_out, d_w_out, nm_w_out, nv_w_out),
               ada_w=(g_ada, d_ada, nm_ada, nv_ada))

    loss = red[R_LOSS, 0]
    names = ("c_ctx", "ada_w", "ada_b", "norm_g", "w_in", "conv_w", "conv_b", "lru_wa", "lru_ba", "lru_wx", "lru_bx",
             "lru_lambda", "sgu_ln_g", "sgu_ln_b", "sgu_w", "sgu_b", "w_out", "final_g")
    outs = [loss, gx.reshape(x.shape)]
    for kind in range(4):
        for k in names:
            val = big[k][kind] if k in big else small[kind][k]
            outs.append(val.reshape(args[k].shape))
    return tuple(outs)
```

```python
import functools

import jax
import jax.numpy as jnp
from jax import lax
from jax.experimental import pallas as pl
from jax.experimental.pallas import tpu as pltpu

F32 = jnp.float32
BF16 = jnp.bfloat16

N_DEV = 8
D = 1024
L = 2048
LC = 256
HEADS = 8
HD = 128
CHUNK = 128
D_IN = 5 * D
W_IN_SHARD = D_IN // N_DEV
ROWS = 256
D_MIX = 2 * D
CONV_W = 4
LRU_C = 8.0
NORM_EPS = 1e-6
LN_EPS = 1e-5
ADAM_LR, ADAM_B1, ADAM_B2, ADAM_EPS, ADAM_WD, ADAM_STEP = 0.001, 0.9, 0.999, 1e-08, 0.01, 10

VMEM_LIMIT = 56 * 1024 * 1024

HBM = pl.BlockSpec(memory_space=pltpu.HBM)
VMEM = pl.BlockSpec(memory_space=pltpu.VMEM)
MESH = pl.DeviceIdType.MESH


def _call(body, **kw):
    return pl.pallas_call(body, **kw)


def _params(*sem):
    return pltpu.CompilerParams(dimension_semantics=sem, vmem_limit_bytes=VMEM_LIMIT)


def _sigmoid(x):
    return 0.5 * jnp.tanh(0.5 * x) + 0.5


def _silu_and_grad(x):
    s = _sigmoid(x)
    return x * s, s * (1.0 + x * (1.0 - s))


_G0 = 0.7978845608028654
_G1 = 0.044715


def _gelu_and_grad(x):
    x2 = x * x
    t = jnp.tanh(_G0 * (x + _G1 * x * x2))
    cdf = 0.5 * (1.0 + t)
    return x * cdf, cdf + 0.5 * x * (1.0 - t * t) * (_G0 * (1.0 + 3.0 * _G1 * x2))


def _gelu(x):
    return 0.5 * x * (1.0 + jnp.tanh(_G0 * (x + _G1 * x * x * x)))


def _softplus(z):
    t = jnp.exp(-jnp.abs(z))
    u = 1.0 + t
    log1p = jnp.where(u == 1.0, t, jnp.log(u) * t / jnp.where(u == 1.0, 1.0, u - 1.0))
    return jnp.maximum(z, 0.0) + log1p


def _dot(a, b):
    return jnp.dot(a, b, preferred_element_type=F32)


def _dot_nt(a, b):
    return lax.dot_general(a, b, (((1,), (1,)), ((), ())), preferred_element_type=F32)


def _dot_tn(a, b):
    return lax.dot_general(a, b, (((0,), (0,)), ((), ())), preferred_element_type=F32)


def _rows(shape):
    return lax.broadcasted_iota(jnp.int32, shape, 0)


def _shift_down(x, first):
    y = pltpu.roll(x, 1, 0)
    head = jnp.where(_rows((8, x.shape[1])) == 0, first, y[0:8])
    return jnp.concatenate([head, y[8:]], axis=0)


def _shift_up(x, last):
    n = x.shape[0]
    y = pltpu.roll(x, n - 1, 0)
    tail = jnp.where(_rows((8, x.shape[1])) == 7, last, y[n - 8:])
    return jnp.concatenate([y[:n - 8], tail], axis=0)


def _gather2(arrays, modes, name):
    n = len(arrays)

    def body(*refs):
        start, forward, finish = _gather2_ops(refs[:n], refs[n:2 * n], modes, *refs[2 * n:])
        start()
        forward()
        finish()

    return _call(
        body, name=name, out_shape=_gather2_shapes(arrays, modes), in_specs=[HBM] * n, out_specs=[HBM] * n,
        scratch_shapes=_gather2_sems(n), compiler_params=pltpu.CompilerParams(has_side_effects=True),
    )(*[pltpu.with_memory_space_constraint(a, pltpu.HBM) for a in arrays])


def _gather2_shapes(arrays, modes):
    return [jax.ShapeDtypeStruct((N_DEV,) + a.shape if m == "ag" else (a.shape[0], N_DEV * a.shape[1]), a.dtype)
            for a, m in zip(arrays, modes)]


def _gather2_sems(n):
    return [pltpu.SemaphoreType.DMA((n, N_DEV - 1)), pltpu.SemaphoreType.DMA((n, N_DEV - 1)),
            pltpu.SemaphoreType.DMA((n,))]


def _gather2_ops(ins, outs, modes, send_sems, recv_sems, local_sems):
    n = len(ins)
    x, y, c = lax.axis_index("x"), lax.axis_index("y"), lax.axis_index("c")
    me, sibling = (x, y, c), (x, y, 1 - c)
    chips = [(x ^ (k >> 1), y ^ (k & 1)) for k in (1, 2, 3)]

    def slot(j, px, py, pc):
        dev = 4 * px + 2 * py + pc
        if modes[j] == "agc":
            w = ins[j].shape[1]
            return outs[j].at[:, pl.ds(pl.multiple_of(dev * w, 128), w)]
        return outs[j].at[dev]

    def copy(j, k, block, to, src=None):
        return pltpu.make_async_remote_copy(
            src_ref=slot(j, *block) if src is None else src, dst_ref=slot(j, *block),
            send_sem=send_sems.at[j, k], recv_sem=recv_sems.at[j, k], device_id=to, device_id_type=MESH)

    def own(j):
        return pltpu.make_async_copy(ins[j], slot(j, *me), local_sems.at[j])

    def first(j):
        return [copy(j, 0, me, sibling, src=ins[j])] + [copy(j, 1 + i, me, (*chip, c), src=ins[j])
                                                        for i, chip in enumerate(chips)]

    def passed(j, i):
        return copy(j, 4 + i, (*chips[i], c), sibling)

    def start():
        for j in range(n):
            own(j).start()
            for cp in first(j):
                cp.start()

    def forward():
        for i, chip in enumerate(chips):
            for j in range(n):
                copy(j, 1 + i, (*chip, c), me).wait_recv()
                passed(j, i).start()

    def finish():
        for j in range(n):
            copy(j, 0, sibling, me).wait_recv()
            for i, chip in enumerate(chips):
                copy(j, 4 + i, (*chip, 1 - c), me).wait_recv()
            for cp in first(j) + [passed(j, i) for i in range(3)]:
                cp.wait_send()
            own(j).wait()

    return start, forward, finish


def _reduce2_local(arrays, modes, me, name, counts=None, out_dtype=None):
    n = len(arrays)
    counts = counts or [4] * n
    shapes = [(a.shape[1], a.shape[2]) if m == "a2a" else (a.shape[0], a.shape[1] // (2 * cnt))
              for a, m, cnt in zip(arrays, modes, counts)]
    staged = [jax.ShapeDtypeStruct((cnt,) + s, a.dtype) for s, a, cnt in zip(shapes, arrays, counts)]

    def piece(ref, mode, dev, w):
        return ref.at[dev] if mode == "a2a" else ref.at[:, pl.ds(pl.multiple_of(dev * w, 128), w)]

    def to_sibling(*refs):
        ins, outs = refs[:n], refs[n:2 * n]
        send_sems, recv_sems = refs[2 * n:]
        x, y, c = lax.axis_index("x"), lax.axis_index("y"), lax.axis_index("c")
        copies = []
        for j in range(n):
            for q in range(counts[j]):
                cp = pltpu.make_async_remote_copy(
                    src_ref=piece(ins[j], modes[j], 2 * q + (1 - c), shapes[j][1]), dst_ref=outs[j].at[q],
                    send_sem=send_sems.at[j, q], recv_sem=recv_sems.at[j, q], device_id=(x, y, 1 - c),
                    device_id_type=MESH)
                cp.start()
                copies.append(cp)
        for cp in copies:
            cp.wait()

    stage = _call(
        to_sibling, name=name + "_d2d", out_shape=staged, in_specs=[HBM] * n, out_specs=[HBM] * n,
        scratch_shapes=[pltpu.SemaphoreType.DMA((n, 4)), pltpu.SemaphoreType.DMA((n, 4))],
        compiler_params=pltpu.CompilerParams(has_side_effects=True),
    )(*[pltpu.with_memory_space_constraint(a, pltpu.HBM) for a in arrays])

    def add(me_ref, *refs):
        del me_ref
        own, got, outs = refs[:n], refs[n:2 * n], refs[2 * n:]
        for j in range(n):
            mine = own[j][0] if modes[j] == "a2a" else own[j][...]
            outs[j][0] = (mine.astype(F32) + got[j][0].astype(F32)).astype(outs[j].dtype)

    in_specs, slot_specs = [], []
    for (r, w), m, cnt in zip(shapes, modes, counts):
        if m == "a2a":
            in_specs.append(pl.BlockSpec(
                (1, r, w), lambda q, me_ref, cnt=cnt: (2 * jnp.minimum(q, cnt - 1) + me_ref[0] % 2, 0, 0)))
        else:
            in_specs.append(pl.BlockSpec(
                (r, w), lambda q, me_ref, cnt=cnt: (0, 2 * jnp.minimum(q, cnt - 1) + me_ref[0] % 2)))
        slot_specs.append(pl.BlockSpec((1, r, w), lambda q, me_ref, cnt=cnt: (jnp.minimum(q, cnt - 1), 0, 0)))
    return _call(
        add, name=name + "_add",
        out_shape=[jax.ShapeDtypeStruct(s.shape, out_dtype or s.dtype) for s in staged],
        grid_spec=pltpu.PrefetchScalarGridSpec(num_scalar_prefetch=1, grid=(max(counts),),
                                               in_specs=in_specs + slot_specs, out_specs=slot_specs),
        compiler_params=_params("arbitrary"),
    )(me, *arrays, *stage)


def _chips_sems(n):
    return [pltpu.SemaphoreType.DMA((n, 6)), pltpu.SemaphoreType.DMA((n, 6)), pltpu.SemaphoreType.DMA((n,))]


def _chips_stage_shapes(chip_sums):
    return [jax.ShapeDtypeStruct((2, a.shape[1] // 2, a.shape[2]), a.dtype) for a in chip_sums]


def _chips_ops(ins, outs, stages, send_sems, recv_sems, local_sems, first_chips=None):
    x, y, c = lax.axis_index("x"), lax.axis_index("y"), lax.axis_index("c")
    qm = 2 * x + y
    first_chips = first_chips or [0] * len(ins)

    def owns(j, chip):
        lo, cnt = first_chips[j], ins[j].shape[0]
        if lo == 0 and cnt == 4:
            return None
        return jnp.logical_and(chip >= lo, chip < lo + cnt)

    def guarded(cond, fn):
        if cond is None:
            fn()
        else:
            pl.when(cond)(fn)

    def slot(j, chip):
        return jnp.clip(chip - first_chips[j], 0, ins[j].shape[0] - 1)

    def half(j, i):
        h = ins[j].shape[1] // 2
        return pl.ds(i * h, h)

    def copy(j, sem, src, dst, k):
        return pltpu.make_async_remote_copy(
            src_ref=src, dst_ref=dst, send_sem=send_sems.at[j, sem], recv_sem=recv_sems.at[j, sem],
            device_id=(x ^ (k >> 1), y ^ (k & 1), c), device_id_type=MESH)

    def direct(j, k):
        return copy(j, k - 1, ins[j].at[slot(j, qm ^ k)], outs[j].at[qm], k)

    def first_hop(j, k):
        return copy(j, 1 + k, ins[j].at[slot(j, qm ^ 3), half(j, k - 1)], stages[j].at[k - 1], k)

    def second_hop(j, k):
        return copy(j, 3 + k, stages[j].at[2 - k], outs[j].at[qm ^ (3 - k), half(j, 2 - k)], k)

    def local(j):
        return pltpu.make_async_copy(ins[j].at[slot(j, qm)], outs[j].at[qm], local_sems.at[j])

    def start():
        for j in range(len(ins)):
            for k in (1, 2):
                guarded(owns(j, qm ^ 3), lambda j=j, k=k: first_hop(j, k).start())
        for j in range(len(ins)):
            for k in (1, 2):
                guarded(owns(j, qm ^ k), lambda j=j, k=k: direct(j, k).start())
            guarded(owns(j, qm), lambda j=j: local(j).start())

    def forward():
        for j in range(len(ins)):
            for k in (1, 2):
                def pass_on(j=j, k=k):
                    first_hop(j, 3 - k).wait_recv()
                    second_hop(j, k).start()
                guarded(owns(j, qm ^ k), pass_on)

    def finish():
        for j in range(len(ins)):
            for k in (1, 2):
                guarded(owns(j, qm ^ k), lambda j=j, k=k: direct(j, k).wait_send())
                guarded(owns(j, qm ^ k), lambda j=j, k=k: second_hop(j, k).wait_send())
                guarded(owns(j, qm ^ 3), lambda j=j, k=k: first_hop(j, k).wait_send())
                guarded(owns(j, qm), lambda j=j, k=k: direct(j, k).wait_recv())
                guarded(owns(j, qm), lambda j=j, k=k: second_hop(j, k).wait_recv())
            guarded(owns(j, qm), lambda j=j: local(j).wait())

    return start, forward, finish


def _front(c, c_ctx, ada_w, ada_b, w_in, w_out, me):
    nloc = ada_w.shape[1]

    def body(me_ref, c_ref, cc_ref, aw_ref, ab_ref, win_ref, wout_ref,
             wfull_ref, woutb_ref, modx_ref, modc_ref, call_ref,
             wb_s, part_s, parts_s, w_send, w_recv, w_local, s_send, s_recv):
        x, y, cidx = lax.axis_index("x"), lax.axis_index("y"), lax.axis_index("c")
        me = me_ref[0]
        wb_s[...] = win_ref[...].astype(BF16)
        woutb_ref[...] = wout_ref[...].astype(BF16)
        start, forward, finish = _gather2_ops([wb_s], [wfull_ref], ["agc"], w_send, w_recv, w_local)
        start()

        def small_gather(src, my_slot, stage):
            copies = []
            for k in range(1, N_DEV):
                peer = (x ^ (k >> 2), y ^ ((k >> 1) & 1), cidx ^ (k & 1))
                cp = pltpu.make_async_remote_copy(src_ref=src, dst_ref=my_slot, send_sem=s_send.at[stage, k - 1],
                                                  recv_sem=s_recv.at[stage, k - 1], device_id=peer,
                                                  device_id_type=MESH)
                cp.start()
                copies.append(cp)
            pltpu.sync_copy(src, my_slot)
            for cp in copies:
                cp.wait()

        small_gather(c_ref, call_ref.at[pl.ds(me, 1), :], 0)
        off = pl.multiple_of(me * nloc, 128)
        b = ab_ref[:, pl.ds(off, nloc)]
        w = aw_ref[...]
        sx, _ = _silu_and_grad(call_ref[...])
        sc, _ = _silu_and_grad(jnp.broadcast_to(cc_ref[...], (8, D)))
        part_s[0:8, :] = _dot(sx, w) + b
        part_s[8:16, :] = _dot(sc, w) + b
        small_gather(part_s, parts_s.at[me], 1)
        mine = _rows((16, nloc)) == me
        for j in range(N_DEV):
            pj = parts_s[j]
            modx_ref[:, j * nloc:(j + 1) * nloc] = jnp.sum(jnp.where(mine, pj, 0.0), axis=0, keepdims=True)
            modc_ref[:, j * nloc:(j + 1) * nloc] = pj[8:9, :]
        forward()
        finish()

    return _call(
        body, name="front",
        out_shape=[jax.ShapeDtypeStruct((D, D_IN), BF16), jax.ShapeDtypeStruct(w_out.shape, BF16),
                   jax.ShapeDtypeStruct((1, 3 * D), F32), jax.ShapeDtypeStruct((1, 3 * D), F32),
                   jax.ShapeDtypeStruct((N_DEV, D), F32)],
        in_specs=[pl.BlockSpec(memory_space=pltpu.SMEM)] + [VMEM] * 6, out_specs=[HBM, VMEM, VMEM, VMEM, VMEM],
        scratch_shapes=[pltpu.VMEM(w_in.shape, BF16), pltpu.VMEM((16, nloc), F32),
                        pltpu.VMEM((N_DEV, 16, nloc), F32)] + _gather2_sems(1) +
                       [pltpu.SemaphoreType.DMA((2, N_DEV - 1)), pltpu.SemaphoreType.DMA((2, N_DEV - 1))],
        compiler_params=pltpu.CompilerParams(vmem_limit_bytes=VMEM_LIMIT, has_side_effects=True),
    )(me, c, c_ctx, ada_w, ada_b, w_in, w_out)


ARRIVAL = (0, 1, 2, 4, 3, 5, 6, 7)


def _front_project(xr, c, c_ctx, ada_w, ada_b, ng, w_in, w_out, cw, lam, me):
    nloc = ada_w.shape[1]
    ws = W_IN_SHARD
    order = me[0] ^ jnp.asarray(ARRIVAL, jnp.int32)

    def body(ord_ref, x_ref, c_ref, cc_ref, aw_ref, ab_ref, ng_ref, win_ref, wout_ref, cw_ref, lam_ref,
             z_ref, hn_ref, wfull_ref, woutb_ref, modx_ref, modc_ref, call_ref, cwf_ref, lamf_ref,
             wv, call_s, part_s, parts_s, w_send, w_recv, hbm_sems, s_send, s_recv, g_send, g_recv, g_local):
        t = pl.program_id(0)
        x, y, cidx = lax.axis_index("x"), lax.axis_index("y"), lax.axis_index("c")
        me_i = ord_ref[0]
        sibling = (x, y, 1 - cidx)
        chips = [(x ^ (k >> 1), y ^ (k & 1)) for k in (1, 2, 3)]
        g_start, g_pass, g_finish = _gather2_ops([cw_ref, lam_ref], [cwf_ref, lamf_ref], ["agc", "agc"],
                                                 g_send, g_recv, g_local)

        def shard_copy(k, px, py, pc, to, half=None):
            slot = wv.at[4 * px + 2 * py + pc]
            if half is not None:
                slot = slot.at[pl.ds(half * (D // 2), D // 2), :]
            return pltpu.make_async_remote_copy(src_ref=slot, dst_ref=slot, send_sem=w_send.at[k],
                                                recv_sem=w_recv.at[k], device_id=to, device_id_type=MESH)

        def small_gather(src, my_slot, stage):
            copies = []
            for k in range(1, N_DEV):
                peer = (x ^ (k >> 2), y ^ ((k >> 1) & 1), cidx ^ (k & 1))
                cp = pltpu.make_async_remote_copy(src_ref=src, dst_ref=my_slot, send_sem=s_send.at[stage, k - 1],
                                                  recv_sem=s_recv.at[stage, k - 1], device_id=peer,
                                                  device_id_type=MESH)
                cp.start()
                copies.append(cp)
            pltpu.sync_copy(src, my_slot)
            for cp in copies:
                cp.wait()

        @pl.when(t == 0)
        def _():
            g_start()
            wv[me_i] = win_ref[...].astype(BF16)
            woutb_ref[...] = wout_ref[...].astype(BF16)
            shard_copy(0, x, y, cidx, sibling).start()
            small_gather(c_ref, call_s.at[pl.ds(me_i, 1), :], 0)
            call_ref[...] = call_s[...]
            off = pl.multiple_of(me_i * nloc, 128)
            b = ab_ref[:, pl.ds(off, nloc)]
            w = aw_ref[...]
            sx, _ = _silu_and_grad(call_s[...])
            sc, _ = _silu_and_grad(jnp.broadcast_to(cc_ref[...], (8, D)))
            part_s[0:8, :] = _dot(sx, w) + b
            part_s[8:16, :] = _dot(sc, w) + b
            small_gather(part_s, parts_s.at[me_i], 1)
            for i in (0, 1):
                shard_copy(1 + i, x, y, cidx, (*chips[i], cidx)).start()
            mine = _rows((16, nloc)) == me_i
            for j in range(N_DEV):
                pj = parts_s[j]
                modx_ref[:, j * nloc:(j + 1) * nloc] = jnp.sum(jnp.where(mine, pj, 0.0), axis=0, keepdims=True)
                modc_ref[:, j * nloc:(j + 1) * nloc] = pj[8:9, :]
            shift, scale1, ngv = modx_ref[:, 0:D], 1.0 + modx_ref[:, D:2 * D], ng_ref[...]
            for r in range(L // ROWS):
                rsl = slice(r * ROWS, (r + 1) * ROWS)
                xv = x_ref[rsl, :]
                rs = lax.rsqrt(jnp.mean(xv * xv, axis=-1, keepdims=True) + NORM_EPS)
                hn_ref[rsl, :] = ((xv * rs * ngv) * scale1 + shift).astype(BF16)

        @pl.when(t == 1)
        def _():
            shard_copy(0, x, y, 1 - cidx, sibling).wait_recv()
            g_pass()

        for i in (0, 1):
            @pl.when(t == ARRIVAL.index((2, 4)[i]))
            def _(i=i):
                shard_copy(1 + i, *chips[i], cidx, sibling).wait_recv()
                shard_copy(4 + i, *chips[i], cidx, sibling).start()
                shard_copy((7, 3)[i], *chips[i], cidx, (*chips[1 - i], cidx), half=i).start()

        @pl.when(t == ARRIVAL.index(6))
        def _():
            shard_copy(3, *chips[2], cidx, sibling, half=1).wait_recv()
            shard_copy(7, *chips[2], cidx, sibling, half=0).wait_recv()
            shard_copy(6, *chips[2], cidx, sibling).start()

        for i in range(3):
            @pl.when(t == ARRIVAL.index((3, 5, 7)[i]))
            def _(i=i):
                shard_copy(4 + i, *chips[i], 1 - cidx, sibling).wait_recv()

        @pl.when(t == 2)
        def _():
            g_finish()

        dev = ord_ref[t]
        for r in range(L // (2 * ROWS)):
            rsl = slice(r * 2 * ROWS, (r + 1) * 2 * ROWS)
            z_ref[rsl, :] = _dot(hn_ref[rsl, :], wv[dev])
        col = pl.ds(pl.multiple_of(dev * ws, 128), ws)
        pltpu.make_async_copy(wv.at[dev], wfull_ref.at[:, col], hbm_sems.at[t]).start()

        @pl.when(t == N_DEV - 1)
        def _():
            for k in (0, 1, 2, 4, 5, 6):
                shard_copy(k, x, y, cidx, sibling).wait_send()
            for k in (3, 7):
                shard_copy(k, x, y, cidx, sibling, half=0).wait_send()
            for s in range(N_DEV):
                pltpu.make_async_copy(wv.at[0], wfull_ref.at[:, pl.ds(0, ws)], hbm_sems.at[s]).wait()

    const = lambda *shape: pl.BlockSpec(shape, lambda t, o: (0,) * len(shape))
    once = lambda *shape: pl.BlockSpec(shape, lambda t, o: (0,) * len(shape), pipeline_mode=pl.Buffered(1))
    return _call(
        body, name="front_project",
        out_shape=[jax.ShapeDtypeStruct((L, D_IN), F32), jax.ShapeDtypeStruct((L, D), BF16),
                   jax.ShapeDtypeStruct((D, D_IN), BF16), jax.ShapeDtypeStruct(w_out.shape, BF16),
                   jax.ShapeDtypeStruct((1, 3 * D), F32), jax.ShapeDtypeStruct((1, 3 * D), F32),
                   jax.ShapeDtypeStruct((N_DEV, D), F32), jax.ShapeDtypeStruct((CONV_W, D), F32),
                   jax.ShapeDtypeStruct((2, D), F32)],
        grid_spec=pltpu.PrefetchScalarGridSpec(
            num_scalar_prefetch=1, grid=(N_DEV,),
            in_specs=[once(L, D), const(1, D), const(1, D), once(D, nloc), const(1, 3 * D), const(1, D),
                      once(D, ws), once(*w_out.shape), HBM, HBM],
            out_specs=[pl.BlockSpec((L, ws), lambda t, o: (0, o[t])), const(L, D), HBM, const(*w_out.shape),
                       const(1, 3 * D), const(1, 3 * D), const(N_DEV, D), HBM, HBM],
            scratch_shapes=[pltpu.VMEM((N_DEV, D, ws), BF16), pltpu.VMEM((N_DEV, D), F32), pltpu.VMEM((16, nloc), F32),
                            pltpu.VMEM((N_DEV, 16, nloc), F32), pltpu.SemaphoreType.DMA((8,)),
                            pltpu.SemaphoreType.DMA((8,)), pltpu.SemaphoreType.DMA((N_DEV,)),
                            pltpu.SemaphoreType.DMA((2, N_DEV - 1)), pltpu.SemaphoreType.DMA((2, N_DEV - 1))]
            + _gather2_sems(2)),
        compiler_params=pltpu.CompilerParams(dimension_semantics=("arbitrary",), vmem_limit_bytes=VMEM_LIMIT,
                                             has_side_effects=True),
    )(order, xr, c, c_ctx, ada_w, ada_b, ng, w_in, w_out, pltpu.with_memory_space_constraint(cw, pltpu.HBM),
      pltpu.with_memory_space_constraint(lam, pltpu.HBM))


def _project(xr, mod, ng, w, ncols, tm, name, gather=None, gather_modes=()):
    rows = xr.shape[0]
    steps = rows // tm
    ng_ = len(gather or ())

    def body(x_ref, sh_ref, sc_ref, ng_ref, w_ref, *rest):
        z_ref, hn_ref = rest[ng_:ng_ + 2]
        if ng_:
            start, forward, finish = _gather2_ops(rest[:ng_], rest[ng_ + 2:2 * ng_ + 2], gather_modes,
                                                  *rest[2 * ng_ + 2:])
            pl.when(pl.program_id(0) == 0)(start)
            pl.when(pl.program_id(0) == steps // 2)(forward)
        x = x_ref[...]
        rs = lax.rsqrt(jnp.mean(x * x, axis=-1, keepdims=True) + NORM_EPS)
        hn = (x * rs * ng_ref[...]) * (1.0 + sc_ref[...]) + sh_ref[...]
        hb = hn.astype(BF16)
        hn_ref[...] = hb
        for n in range(ncols // D):
            z_ref[:, n * D:(n + 1) * D] = _dot(hb, w_ref[:, n * D:(n + 1) * D])
        if ng_:
            pl.when(pl.program_id(0) == steps - 1)(finish)

    vec = pl.BlockSpec((1, D), lambda i: (0, 0))
    gathered = _gather2_shapes(gather, gather_modes) if ng_ else []
    return _call(
        body, name=name, grid=(steps,),
        out_shape=[jax.ShapeDtypeStruct((rows, ncols), F32), jax.ShapeDtypeStruct((rows, D), BF16)] + gathered,
        in_specs=[pl.BlockSpec((tm, D), lambda i: (i, 0)), vec, pl.BlockSpec((1, D), lambda i: (0, 1)), vec,
                  pl.BlockSpec((D, ncols), lambda i: (0, 0), pipeline_mode=pl.Buffered(1))] + [HBM] * ng_,
        out_specs=[pl.BlockSpec((tm, ncols), lambda i: (i, 0)), pl.BlockSpec((tm, D), lambda i: (i, 0))] + [HBM] * ng_,
        scratch_shapes=_gather2_sems(ng_) if ng_ else [],
        compiler_params=pltpu.CompilerParams(dimension_semantics=("arbitrary",), vmem_limit_bytes=VMEM_LIMIT,
                                             has_side_effects=bool(ng_)),
    )(xr, mod, mod, ng, w, *[pltpu.with_memory_space_constraint(a, pltpu.HBM) for a in gather or ()])


def _scan_pair(af_ref, uf_ref, hf_ref, h0f, ab_ref, ub_ref, hb_ref, h0b, t_len):
    span = 8 * SCAN_BLOCKS
    nit = t_len // span
    rows = _rows((8, HD))

    def local_scan(a, b, forward):
        for s in (1, 2, 4):
            sh = s if forward else 8 - s
            m = rows >= s if forward else rows < 8 - s
            b = a * jnp.where(m, pltpu.roll(b, sh, 0), 0.0) + b
            a = a * jnp.where(m, pltpu.roll(a, sh, 0), 1.0)
        return a, b

    def span_scan(a_ref, u_ref, h_ref, off, carry, forward):
        order = range(SCAN_BLOCKS) if forward else range(SCAN_BLOCKS - 1, -1, -1)
        last = slice(7, 8) if forward else slice(0, 1)
        for q in order:
            rs = pl.ds(off + 8 * q, 8)
            a, b = local_scan(a_ref[rs, :], u_ref[rs, :], forward)
            h_ref[rs, :] = b + a * carry
            carry = a[last, :] * carry + b[last, :]
        return carry

    def body(k, carry):
        cf, cb = carry
        cf = span_scan(af_ref, uf_ref, hf_ref, pl.multiple_of(k * span, span), cf, True)
        cb = span_scan(ab_ref, ub_ref, hb_ref, pl.multiple_of((nit - 1 - k) * span, span), cb, False)
        return cf, cb

    return lax.fori_loop(0, nit, body, (h0f, h0b))


SCAN_BLOCKS = 8


def _shifted(pad_ref, x, offsets, before=0.0, after=0.0):
    n = x.shape[0]
    pad_ref[0:8, :] = jnp.broadcast_to(jnp.asarray(before, F32), (8, x.shape[1]))
    pad_ref[8:8 + n, :] = x
    pad_ref[8 + n:16 + n, :] = jnp.broadcast_to(jnp.asarray(after, F32), (8, x.shape[1]))
    return [pad_ref[8 + o:8 + o + n, :] for o in offsets]


def _conv(xa, cw, cb, pad_ref):
    xm1, xp1, xp2 = _shifted(pad_ref, xa, (-1, 1, 2))
    return xm1 * cw[0:1, :] + xa * cw[1:2, :] + xp1 * cw[2:3, :] + xp2 * cw[3:4, :] + cb


def _gates(xc, wa, wx, ba, bx, nsp):
    xb = xc.astype(BF16)
    r = _sigmoid(_dot(xb, wa) + ba)
    i = _sigmoid(_dot(xb, wx) + bx)
    log_a = r * nsp
    a = jnp.exp(log_a)
    g2 = jnp.tanh(log_a) * (-1.0 - a * a)
    rg = lax.rsqrt(jnp.maximum(g2, 1e-30))
    return r, i, a, g2 * rg, rg


def _lru_param_specs():
    h4 = pl.BlockSpec((2, 1, HD, HD), lambda h: (0, h, 0, 0))
    v2 = pl.BlockSpec((2, HD), lambda h: (0, h))
    b16 = pl.BlockSpec((2 * HEADS, HD), lambda h: (0, 0))
    return dict(
        xa=pl.BlockSpec((L, HD), lambda h: (0, h)), xac=pl.BlockSpec((LC, HD), lambda h: (0, h)),
        cw=pl.BlockSpec((CONV_W, HD), lambda h: (0, h)), cb=pl.BlockSpec((1, HD), lambda h: (0, h)), h4=h4, v2=v2,
        b16=b16)


def _bias_row(ref, d):
    mask = _rows((2 * HEADS, HD)) == d * HEADS + pl.program_id(0)
    return jnp.sum(jnp.where(mask, ref[...], 0.0), axis=0, keepdims=True), mask


def _lru_forward(zx, zc, cw, cb, wa, wx, ba, bx, lam, gather, gather_modes):
    ng_ = len(gather)

    def body(xa_ref, xac_ref, cw_ref, cb_ref, wa_ref, wx_ref, ba_ref, bx_ref, lam_ref, *rest):
        yl_ref = rest[ng_]
        af, uf, hf, ab, ub, hb, pad_s = rest[2 * ng_ + 1:2 * ng_ + 8]
        start, pass_on, finish = _gather2_ops(rest[:ng_], rest[ng_ + 1:2 * ng_ + 1], gather_modes,
                                              *rest[2 * ng_ + 8:])
        pl.when(pl.program_id(0) == 0)(start)
        pl.when(pl.program_id(0) == HEADS // 2)(pass_on)
        pl.when(pl.program_id(0) == HEADS - 1)(finish)
        cwv, cbv = cw_ref[...], cb_ref[...]
        nsp = (-LRU_C) * _softplus(-lam_ref[...])

        def forward(xa, t_len, h0f, h0b):
            xc = _conv(xa, cwv, cbv, pad_s)
            for d, (a_ref, u_ref) in enumerate(((af, uf), (ab, ub))):
                _, i, a, gamma, _ = _gates(xc, wa_ref[d, 0].astype(BF16), wx_ref[d, 0].astype(BF16),
                                           _bias_row(ba_ref, d)[0], _bias_row(bx_ref, d)[0], nsp[d:d + 1, :])
                a_ref[0:t_len, :] = a
                u_ref[0:t_len, :] = gamma * (i * xc)
            return _scan_pair(af, uf, hf, h0f, ab, ub, hb, h0b, t_len)

        z = jnp.zeros((1, HD), F32)
        h0f, h0b = forward(xac_ref[...], LC, z, z)
        forward(xa_ref[...], L, h0f, h0b)
        yl_ref[...] = hf[...] + hb[...]

    s = _lru_param_specs()
    return _call(
        body, name="lru_forward", grid=(HEADS,),
        out_shape=[jax.ShapeDtypeStruct((L, D), F32)] + _gather2_shapes(gather, gather_modes),
        in_specs=[s["xa"], s["xac"], s["cw"], s["cb"], s["h4"], s["h4"], s["b16"], s["b16"], s["v2"]] + [HBM] * ng_,
        out_specs=[pl.BlockSpec((L, HD), lambda h: (0, h))] + [HBM] * ng_,
        scratch_shapes=[pltpu.VMEM((L, HD), F32)] * 6 + [pltpu.VMEM((L + 16, HD), F32)] + _gather2_sems(ng_),
        compiler_params=pltpu.CompilerParams(dimension_semantics=("arbitrary",), vmem_limit_bytes=VMEM_LIMIT,
                                             has_side_effects=True),
    )(zx, zc, cw, cb, wa, wx, ba, bx, lam, *[pltpu.with_memory_space_constraint(a, pltpu.HBM) for a in gather])


def _lru_backward(zx, zc, dyl, dz, cw, cb, wa, wx, ba, bx, lam, chip_sums, first_chips=None):
    nr = len(chip_sums)

    def body(xa_ref, xac_ref, dyl_ref, dz_in, cw_ref, cb_ref, wa_ref, wx_ref, ba_ref, bx_ref, lam_ref, *rest):
        (dxa_ref, dxac_ref, dwa_ref, dwx_ref, dba_ref, dbx_ref, dlam_ref, dcw_ref,
         dcb_ref) = rest[nr:nr + 9]
        main_s, ctx_s, pad_s = rest[3 * nr + 9:3 * nr + 12]
        if nr:
            start, forward, finish = _chips_ops(rest[:nr], rest[nr + 9:2 * nr + 9], rest[2 * nr + 9:3 * nr + 9],
                                                *rest[3 * nr + 12:], first_chips=first_chips)
            pl.when(pl.program_id(0) == 0)(start)
            pl.when(pl.program_id(0) == HEADS // 2)(forward)
            pl.when(pl.program_id(0) == HEADS - 1)(finish)
        del dz_in

        @pl.when(pl.program_id(0) == 0)
        def _():
            dba_ref[...] = jnp.zeros_like(dba_ref)
            dbx_ref[...] = jnp.zeros_like(dbx_ref)

        cwv, cbv = cw_ref[...], cb_ref[...]
        lamv = lam_ref[...]
        sp = _softplus(-lamv)
        nsp = (-LRU_C) * sp
        z = jnp.zeros((1, HD), F32)

        def wmat(ref, d):
            return ref[d, 0].astype(BF16)

        def workspace(s):
            return dict(a=(s.at[0], s.at[1]), u=(s.at[2], s.at[3]), h=(s.at[4], s.at[5]), rho=(s.at[6], s.at[7]),
                        saved=(tuple(s.at[8 + k] for k in range(4)), tuple(s.at[12 + k] for k in range(4))),
                        xc=s.at[16])

        def forward(ws, xa, t_len, h0f, h0b):
            xc = _conv(xa, cwv, cbv, pad_s)
            ws["xc"][...] = xc
            for d in (0, 1):
                vals = _gates(xc, wmat(wa_ref, d), wmat(wx_ref, d), _bias_row(ba_ref, d)[0],
                              _bias_row(bx_ref, d)[0], nsp[d:d + 1, :])
                r, i, a, gamma, rg = vals
                ws["a"][d][...] = a
                ws["u"][d][...] = gamma * (i * xc)
                for ref, val in zip(ws["saved"][d], (r, i, gamma, rg)):
                    ref[...] = val
            return _scan_pair(ws["a"][0], ws["u"][0], ws["h"][0], h0f, ws["a"][1], ws["u"][1], ws["h"][1], h0b,
                              t_len)

        def backward(ws, xa, t_len, h0f, h0b, dhf, dhb, first):
            xc = ws["xc"][...]
            (af, ab), (uf, ub), (hf, hb), (rf, rb) = ws["a"], ws["u"], ws["h"], ws["rho"]
            uf[...] = ab[...] * dhb
            ub[...] = af[...] * dhf
            rho_b_last, rho_f_first = _scan_pair(ab, uf, rb, z, af, ub, rf, z, t_len)
            dxc = jnp.zeros((t_len, HD), F32)
            dsp = []
            for d in (0, 1):
                r, i, gamma, rg = (ref[...] for ref in ws["saved"][d])
                a = ws["a"][d][...]
                if d == 0:
                    lam_t = dhf + _shifted(pad_s, rf[...], (1,))[0]
                    h_prev = _shifted(pad_s, hf[...], (-1,), before=h0f)[0]
                else:
                    lam_t = dhb + _shifted(pad_s, rb[...], (-1,))[0]
                    h_prev = _shifted(pad_s, hb[...], (1,), after=h0b)[0]
                da = lam_t * h_prev
                lx = lam_t * xc
                d_i = lx * gamma
                d_gamma = lx * i
                dxc = dxc + lam_t * (gamma * i)
                d_log_a = a * (da - d_gamma * (a * rg))
                dsp.append(jnp.sum(d_log_a * r, axis=0, keepdims=True) * (-LRU_C))
                d_pre_r = d_log_a * nsp[d:d + 1, :] * (r * (1.0 - r))
                d_pre_i = d_i * (i * (1.0 - i))
                prb, pib, xb = d_pre_r.astype(BF16), d_pre_i.astype(BF16), xc.astype(BF16)
                dxc = dxc + _dot_nt(prb, wmat(wa_ref, d)) + _dot_nt(pib, wmat(wx_ref, d))
                g_wa, g_wx = _dot_tn(xb, prb), _dot_tn(xb, pib)
                g_ba = jnp.sum(d_pre_r, axis=0, keepdims=True)
                g_bx = jnp.sum(d_pre_i, axis=0, keepdims=True)
                mask = _bias_row(ba_ref, d)[1]
                dba_ref[...] += jnp.where(mask, g_ba, 0.0)
                dbx_ref[...] += jnp.where(mask, g_bx, 0.0)
                if first:
                    dwa_ref[d, 0] = g_wa
                    dwx_ref[d, 0] = g_wx
                else:
                    dwa_ref[d, 0] += g_wa
                    dwx_ref[d, 0] += g_wx
            g_lam = jnp.concatenate(dsp, axis=0) * (-_sigmoid(-lamv))
            dm1, dp1, dm2 = _shifted(pad_s, dxc, (-1, 1, -2))
            dxa = dp1 * cwv[0:1, :] + dxc * cwv[1:2, :] + dm1 * cwv[2:3, :] + dm2 * cwv[3:4, :]
            xm1, xp1, xp2 = _shifted(pad_s, xa, (-1, 1, 2))
            g_cw = jnp.concatenate([jnp.sum(dxc * v, axis=0, keepdims=True) for v in (xm1, xa, xp1, xp2)], axis=0)
            g_cb = jnp.sum(dxc, axis=0, keepdims=True)
            if first:
                dlam_ref[...] = g_lam
                dcw_ref[...] = g_cw
                dcb_ref[...] = g_cb
            else:
                dlam_ref[...] += g_lam
                dcw_ref[...] += g_cw
                dcb_ref[...] += g_cb
            return dxa, rho_f_first, rho_b_last

        ws_x, ws_c = workspace(main_s), workspace(ctx_s)
        h0f, h0b = forward(ws_c, xac_ref[...], LC, z, z)
        forward(ws_x, xa_ref[...], L, h0f, h0b)
        dh = dyl_ref[...]
        dxa, dh0f, dh0b = backward(ws_x, xa_ref[...], L, h0f, h0b, dh, dh, True)
        dxa_ref[...] = dxa.astype(BF16)
        rc = _rows((LC, HD))
        dxac, _, _ = backward(ws_c, xac_ref[...], LC, z, z, jnp.where(rc == LC - 1, dh0f, 0.0),
                              jnp.where(rc == 0, dh0b, 0.0), False)
        dxac_ref[...] = dxac.astype(BF16)

    s = _lru_param_specs()
    col = lambda r: pl.BlockSpec((r, HD), lambda h: (0, h))
    return _call(
        body, name="lru_backward", grid=(HEADS,),
        out_shape=[jax.ShapeDtypeStruct((L, D_IN), BF16), jax.ShapeDtypeStruct((LC, D), BF16),
                   jax.ShapeDtypeStruct((2, HEADS, HD, HD), F32), jax.ShapeDtypeStruct((2, HEADS, HD, HD), F32),
                   jax.ShapeDtypeStruct((2 * HEADS, HD), F32), jax.ShapeDtypeStruct((2 * HEADS, HD), F32),
                   jax.ShapeDtypeStruct((2, D), F32), jax.ShapeDtypeStruct((CONV_W, D), F32),
                   jax.ShapeDtypeStruct((1, D), F32)] + [jax.ShapeDtypeStruct((4,) + a.shape[1:], a.dtype)
                                                          for a in chip_sums] + _chips_stage_shapes(chip_sums),
        in_specs=[s["xa"], s["xac"], col(L), pl.BlockSpec(memory_space=pl.ANY), s["cw"], s["cb"], s["h4"], s["h4"],
                  s["b16"], s["b16"], s["v2"]] + [HBM] * nr,
        out_specs=[col(L), col(LC), s["h4"], s["h4"], s["b16"], s["b16"], s["v2"], col(CONV_W), col(1)]
        + [HBM] * (2 * nr),
        scratch_shapes=[pltpu.VMEM((17, L, HD), F32), pltpu.VMEM((17, LC, HD), F32), pltpu.VMEM((L + 16, HD), F32)]
        + (_chips_sems(nr) if nr else []),
        input_output_aliases={3: 0},
        compiler_params=pltpu.CompilerParams(dimension_semantics=("arbitrary",), vmem_limit_bytes=VMEM_LIMIT,
                                             has_side_effects=True),
    )(zx, zc, dyl, dz, cw, cb, wa, wx, ba, bx, lam, *[pltpu.with_memory_space_constraint(a, pltpu.HBM)
                                                       for a in chip_sums])


def _mixer_loss(x, tgt, zx, yl, gx, fg, lng, lnb, ws, wst, bst, wout, tm):
    ncht = tm // CHUNK

    def body(x_ref, t_ref, ga_ref, u_ref, v_ref, gb_ref, yl_ref, gx_ref, fg_ref, lng_ref, lnb_ref, ws_ref, wst_ref,
             bst_ref, wout_ref,
             dz_ref, dyl_ref, dxn_ref, y_s, do_ref, dws_ref, dbst_ref, vec_ref,
             vn_s, mix_s, dm_s, dvn_s):
        step = pl.program_id(0)

        @pl.when(step == 0)
        def _():
            dws_ref[...] = jnp.zeros_like(dws_ref)
            dbst_ref[...] = jnp.zeros_like(dbst_ref)
            vec_ref[...] = jnp.zeros_like(vec_ref)

        u, v = u_ref[...], v_ref[...]
        ug, dug_du = _gelu_and_grad(u)
        vg, dvg_dv = _gelu_and_grad(v)
        mu = jnp.mean(vg, axis=-1, keepdims=True)
        vc = vg - mu
        rstd = lax.rsqrt(jnp.mean(vc * vc, axis=-1, keepdims=True) + LN_EPS)
        vhat = vc * rstd
        lngv = lng_ref[...]
        vn_s[...] = (vhat * lngv + lnb_ref[...]).astype(BF16)
        for ch in range(ncht):
            rs = slice(ch * CHUNK, (ch + 1) * CHUNK)
            for g in range(HEADS):
                cs = slice(g * HD, (g + 1) * HD)
                mix_s[rs, cs] = _dot(ws_ref[g], vn_s[rs, cs]) + bst_ref[:, g:g + 1]
        mixed = mix_s[...]
        ga, gb, yl = ga_ref[...], gb_ref[...], yl_ref[...]
        sga, dsga = _silu_and_grad(ga)
        sgb, dsgb = _silu_and_grad(gb)
        ys = ug * mixed
        y_s[:, 0:D] = (yl * sga).astype(BF16)
        y_s[:, D:D_MIX] = (ys * sgb).astype(BF16)
        o = _dot(y_s[...], wout_ref[...])
        gxv, fgv = gx_ref[...], fg_ref[...]
        xn = x_ref[...] + gxv * o
        rs2 = lax.rsqrt(jnp.mean(xn * xn, axis=-1, keepdims=True) + NORM_EPS)
        xh = xn * rs2
        diff = xh * fgv - t_ref[...]
        vec_ref[R_LOSS:R_LOSS + 1, :] += jnp.full((1, D), jnp.sum(diff * diff) * (0.5 / D), F32)
        dout = diff * (1.0 / D)
        w = dout * fgv
        dxn = rs2 * (w - xh * jnp.mean(w * xh, axis=-1, keepdims=True))
        dxn_ref[...] = dxn
        vec_ref[0:1, :] += jnp.sum(dxn * o, axis=0, keepdims=True)
        vec_ref[1:2, :] += jnp.sum(dout * xh, axis=0, keepdims=True)
        dob = (dxn * gxv).astype(BF16)
        do_ref[...] = dob
        dy = _dot_nt(dob, wout_ref[...])
        dya, dyb = dy[:, 0:D], dy[:, D:D_MIX]
        dyl_ref[...] = dya * sga
        dys = dyb * sgb
        dz_ref[:, 0:D] = jnp.zeros((tm, D), BF16)
        dz_ref[:, D:2 * D] = (dya * yl * dsga).astype(BF16)
        dz_ref[:, 2 * D:3 * D] = (dys * mixed * dug_du).astype(BF16)
        dz_ref[:, 4 * D:5 * D] = (dyb * ys * dsgb).astype(BF16)
        dm = dys * ug
        dm_s[...] = dm.astype(BF16)
        for g in range(HEADS):
            cs = slice(g * HD, (g + 1) * HD)
            dbst_ref[:, g:g + 1] += sum(jnp.sum(dm[ch * CHUNK:(ch + 1) * CHUNK, cs], axis=1, keepdims=True)
                                        for ch in range(ncht))
            for ch in range(ncht):
                rs = slice(ch * CHUNK, (ch + 1) * CHUNK)
                dws_ref[g] += _dot_nt(dm_s[rs, cs], vn_s[rs, cs])
                dvn_s[rs, cs] = _dot(wst_ref[g], dm_s[rs, cs])
        dvn = dvn_s[...]
        vec_ref[2:3, :] += jnp.sum(dvn * vhat, axis=0, keepdims=True)
        vec_ref[3:4, :] += jnp.sum(dvn, axis=0, keepdims=True)
        dvh = dvn * lngv
        dvg = rstd * (dvh - jnp.mean(dvh, axis=-1, keepdims=True) - vhat * jnp.mean(dvh * vhat, axis=-1, keepdims=True))
        dz_ref[:, 3 * D:4 * D] = (dvg * dvg_dv).astype(BF16)

    tile = pl.BlockSpec((tm, D), lambda i: (i, 0))
    zcol = lambda n: pl.BlockSpec((tm, D), lambda i: (i, n))
    vec = pl.BlockSpec((1, D), lambda i: (0, 0))
    full = lambda *s: pl.BlockSpec(s, lambda i: (0,) * len(s))
    return _call(
        body, name="mixer_loss", grid=(L // tm,),
        out_shape=[jax.ShapeDtypeStruct((L, D_IN), BF16), jax.ShapeDtypeStruct((L, D), F32),
                   jax.ShapeDtypeStruct((L, D), F32), jax.ShapeDtypeStruct((L, D_MIX), BF16),
                   jax.ShapeDtypeStruct((L, D), BF16),
                   jax.ShapeDtypeStruct((HEADS, CHUNK, CHUNK), F32), jax.ShapeDtypeStruct((CHUNK, HEADS), F32),
                   jax.ShapeDtypeStruct((8, D), F32)],
        in_specs=[tile, tile, zcol(1), zcol(2), zcol(3), zcol(4), tile, pl.BlockSpec((1, D), lambda i: (0, 2)),
                  vec, vec, vec,
                  full(HEADS, CHUNK, CHUNK), full(HEADS, CHUNK, CHUNK), full(CHUNK, HEADS),
                  pl.BlockSpec((D_MIX, D), lambda i: (0, 0), pipeline_mode=pl.Buffered(1))],
        out_specs=[pl.BlockSpec((tm, D_IN), lambda i: (i, 0)), tile, tile,
                   pl.BlockSpec((tm, D_MIX), lambda i: (i, 0)), tile,
                   full(HEADS, CHUNK, CHUNK), full(CHUNK, HEADS), full(8, D)],
        scratch_shapes=[pltpu.VMEM((tm, D), BF16), pltpu.VMEM((tm, D), F32),
                        pltpu.VMEM((tm, D), BF16), pltpu.VMEM((tm, D), F32)],
        compiler_params=_params("arbitrary"),
    )(x, tgt, zx, zx, zx, zx, yl, gx, fg, lng, lnb, ws, wst, bst, wout)


def _grad_w(a, b, a2, b2, tk, name, bw, first, nblocks, split):
    nk = a.shape[0] // tk
    m = a.shape[1]
    with_ctx = a2 is not None
    if split == "cols":
        slots, r, w = nblocks, m, bw // 2
        piece = lambda q, pc: (slice(None), slice(pc * w, (pc + 1) * w))
    else:
        slots, r, w = 4, m // 8, bw
        piece = lambda q, pc: (slice((2 * q + pc) * r, (2 * q + pc + 1) * r), slice(None))

    def body(*refs):
        a_ref, b_ref = refs[:2]
        a2_ref, b2_ref = refs[2:4] if with_ctx else (None, None)
        sums_ref, acc, mine_v, send_v, stage_v, send_sems, recv_sems = refs[4 if with_ctx else 2:]
        n, k = pl.program_id(0), pl.program_id(1)
        x, y, c = lax.axis_index("x"), lax.axis_index("y"), lax.axis_index("c")

        def to_sibling(s):
            return pltpu.make_async_remote_copy(src_ref=send_v.at[s], dst_ref=stage_v.at[s], send_sem=send_sems.at[s],
                                                recv_sem=recv_sems.at[s], device_id=(x, y, 1 - c),
                                                device_id_type=MESH)

        @pl.when(k == 0)
        def _():
            acc[...] = jnp.zeros_like(acc)

        acc[...] += _dot_tn(a_ref[...], b_ref[...])

        if with_ctx:
            @pl.when(jnp.logical_and(k == nk - 1, n == 0))
            def _():
                acc[:, 0:b2_ref.shape[1]] += _dot_tn(a2_ref[...], b2_ref[...])

        def hand_over(s, q):
            for pc in (0, 1):
                @pl.when(c == pc)
                def _(pc=pc):
                    mine_v[s] = acc[piece(q, pc)]
                    send_v[s] = acc[piece(q, 1 - pc)].astype(BF16)
            to_sibling(s).start()

        for i in range(nblocks):
            @pl.when(jnp.logical_and(k == nk - 1, n == i))
            def _(i=i):
                if split == "cols":
                    hand_over(i, 0)
                else:
                    for q in range(4):
                        hand_over(q, q)

        @pl.when(jnp.logical_and(k == nk - 1, n == nblocks - 1))
        def _():
            for s in range(slots):
                to_sibling(s).wait_recv()
                sums_ref[s] = (mine_v[s] + stage_v[s].astype(F32)).astype(BF16)
            for s in range(slots):
                to_sibling(s).wait_send()

    in_specs = [pl.BlockSpec((tk, m), lambda n, k: (k, 0)), pl.BlockSpec((tk, bw), lambda n, k: (k, n + first))]
    args = [a, b]
    if with_ctx:
        in_specs += [pl.BlockSpec(a2.shape, lambda n, k: (0, 0)), pl.BlockSpec(b2.shape, lambda n, k: (0, 0))]
        args += [a2, b2]
    return _call(
        body, name=name, grid=(nblocks, nk), out_shape=jax.ShapeDtypeStruct((slots, r, w), BF16),
        in_specs=in_specs, out_specs=pl.BlockSpec((slots, r, w), lambda n, k: (0, 0, 0)),
        scratch_shapes=[pltpu.VMEM((m, bw), F32), pltpu.VMEM((slots, r, w), F32), pltpu.VMEM((slots, r, w), BF16),
                        pltpu.VMEM((slots, r, w), BF16), pltpu.SemaphoreType.DMA((slots,)),
                        pltpu.SemaphoreType.DMA((slots,))],
        compiler_params=pltpu.CompilerParams(dimension_semantics=("arbitrary", "arbitrary"),
                                             vmem_limit_bytes=VMEM_LIMIT, has_side_effects=True),
    )(*args)


def _grad_rows(xr, dz, w, mod, ng, dres, ncols, tm, name, chip_sums=(), first_chips=None, dests=None):
    rows = xr.shape[0]
    steps = rows // tm
    with_dx = dres is not None
    nr = len(chip_sums)
    dests = [d for d in (dests or [None] * nr)]
    nd = sum(d is not None for d in dests)
    nin = 6 if with_dx else 5
    nout = 2 if with_dx else 1

    def body(*refs):
        if with_dx:
            x_ref, dz_ref, w_ref, sc_ref, ng_ref, dres_ref = refs[:nin]
            dx_ref, vec_ref = refs[nin + nr + nd:nin + nr + nd + nout]
        else:
            x_ref, dz_ref, w_ref, sc_ref, ng_ref = refs[:nin]
            (vec_ref,) = refs[nin + nr + nd:nin + nr + nd + nout]
        if nr:
            o0 = nin + nr + nd + nout
            start, forward, finish = _chips_ops(refs[nin:nin + nr], refs[o0:o0 + nr], refs[o0 + nr:o0 + 2 * nr],
                                                *refs[o0 + 2 * nr:], first_chips=first_chips)
            pl.when(pl.program_id(0) == 0)(start)
            pl.when(pl.program_id(0) == 2)(forward)
            pl.when(pl.program_id(0) == steps - 1)(finish)

        @pl.when(pl.program_id(0) == 0)
        def _():
            vec_ref[...] = jnp.zeros_like(vec_ref)

        dhn = _dot_nt(dz_ref[...], w_ref[...])
        x = x_ref[...]
        rs = lax.rsqrt(jnp.mean(x * x, axis=-1, keepdims=True) + NORM_EPS)
        xh = x * rs
        ngv = ng_ref[...]
        y = xh * ngv
        vec_ref[0:1, :] += jnp.sum(dhn, axis=0, keepdims=True)
        vec_ref[1:2, :] += jnp.sum(dhn * y, axis=0, keepdims=True)
        dy = dhn * (1.0 + sc_ref[...])
        vec_ref[2:3, :] += jnp.sum(dy * xh, axis=0, keepdims=True)
        if with_dx:
            dxh = dy * ngv
            dx_ref[...] = dres_ref[...] + rs * (dxh - xh * jnp.mean(dxh * xh, axis=-1, keepdims=True))

    tile = pl.BlockSpec((tm, D), lambda i: (i, 0))
    vec = pl.BlockSpec((1, D), lambda i: (0, 0))
    in_specs = [tile, pl.BlockSpec((tm, ncols), lambda i: (i, 0)),
                pl.BlockSpec((D, ncols), lambda i: (0, 0), pipeline_mode=pl.Buffered(1)),
                pl.BlockSpec((1, D), lambda i: (0, 1)), vec]
    out_shape = [jax.ShapeDtypeStruct((8, D), F32)]
    out_specs = [pl.BlockSpec((8, D), lambda i: (0, 0))]
    args = [xr, dz, w, mod, ng]
    if with_dx:
        in_specs.append(tile)
        out_shape.insert(0, jax.ShapeDtypeStruct((rows, D), F32))
        out_specs.insert(0, tile)
        args.append(dres)
    aliases = {}
    for j, d in enumerate(dests):
        if d is not None:
            aliases[len(args) + nr + len(aliases)] = len(out_shape) + j
    in_specs += [HBM] * (nr + nd)
    out_specs += [HBM] * (2 * nr)
    out_shape += [jax.ShapeDtypeStruct((4,) + a.shape[1:], a.dtype) for a in chip_sums]
    out_shape += _chips_stage_shapes(chip_sums)
    args += [pltpu.with_memory_space_constraint(a, pltpu.HBM) for a in chip_sums]
    args += [pltpu.with_memory_space_constraint(d, pltpu.HBM) for d in dests if d is not None]
    return _call(body, name=name, grid=(steps,), out_shape=out_shape, in_specs=in_specs, out_specs=out_specs,
                 scratch_shapes=_chips_sems(nr) if nr else [], input_output_aliases=aliases,
                 compiler_params=pltpu.CompilerParams(dimension_semantics=("arbitrary",),
                                                      vmem_limit_bytes=VMEM_LIMIT, has_side_effects=bool(nr)))(*args)


def _adamw(w, g, m, v):
    m = ADAM_B1 * m + (1.0 - ADAM_B1) * g
    v = ADAM_B2 * v + (1.0 - ADAM_B2) * (g * g)
    m_hat = m / (1.0 - ADAM_B1 ** ADAM_STEP)
    v_hat = v / (1.0 - ADAM_B2 ** ADAM_STEP)
    delta = -ADAM_LR * (m_hat / (jnp.sqrt(v_hat) + ADAM_EPS) + ADAM_WD * w)
    return delta, m, v


def _adamw_reduced(parts, w, m, v, tr, name):
    r, n = w.shape
    nparts = parts.shape[0]

    def body(p_ref, w_ref, m_ref, v_ref, g_ref, d_ref, mo_ref, vo_ref):
        g = p_ref[0].astype(F32)
        for i in range(1, nparts):
            g = g + p_ref[i].astype(F32)
        g_ref[...] = g
        d_ref[...], mo_ref[...], vo_ref[...] = _adamw(w_ref[...], g, m_ref[...], v_ref[...])

    tile = pl.BlockSpec((tr, n), lambda i: (i, 0))
    sds = jax.ShapeDtypeStruct((r, n), F32)
    return _call(
        body, name=name, grid=(r // tr,), out_shape=[sds] * 4,
        in_specs=[pl.BlockSpec((nparts, tr, n), lambda i: (0, i, 0)), tile, tile, tile], out_specs=[tile] * 4,
        compiler_params=_params("arbitrary"),
    )(parts, w, m, v)


R_GATE, R_FINAL_G, R_LN_G, R_LN_B, R_LOSS = 0, 1, 2, 3, 4
R_SH_X, R_SC_X, R_NG_X = 5, 6, 7
R_SH_C, R_SC_C, R_NG_C = 8, 9, 10
R_LAM, R_CW, R_CB = 11, 13, 17
PACK_ROWS = 24
Q_BA, Q_BX, Q_SGU_B, PACK128_ROWS = 0, 16, 32, 40


def _reduce_small(vp_all, vq_all, mat_parts, ada_w, me):
    nloc = ada_w.shape[1]
    nm = len(mat_parts)

    def body(me_ref, vp_ref, vq_ref, *refs):
        mp_refs, w_ref = refs[:nm], refs[nm]
        red_ref, redq_ref = refs[nm + 1:nm + 3]
        mat_refs = refs[nm + 3:2 * nm + 3]
        dmod_ref, gab_ref, cpart_ref, dmc_s = refs[2 * nm + 3:]
        red, redq = vp_ref[0], vq_ref[0]
        for i in range(1, N_DEV):
            red = red + vp_ref[i]
            redq = redq + vq_ref[i]
        red_ref[...] = red
        redq_ref[...] = redq
        for mp_ref, mat_ref in zip(mp_refs, mat_refs):
            mat = mp_ref[0].astype(F32)
            for i in range(1, mp_ref.shape[0]):
                mat = mat + mp_ref[i].astype(F32)
            mat_ref[...] = mat
        for e in range(N_DEV):
            dmod_ref[e:e + 1, 0:D] = vp_ref[e, R_SH_X:R_SH_X + 1, :]
            dmod_ref[e:e + 1, D:2 * D] = vp_ref[e, R_SC_X:R_SC_X + 1, :]
            dmod_ref[e:e + 1, 2 * D:3 * D] = vp_ref[e, R_GATE:R_GATE + 1, :]
        dmod_ref[8:9, 0:D] = red[R_SH_C:R_SH_C + 1, :]
        dmod_ref[8:9, D:2 * D] = red[R_SC_C:R_SC_C + 1, :]
        dmod_ref[8:9, 2 * D:3 * D] = jnp.zeros((1, D), F32)
        dmod_ref[9:16, :] = jnp.zeros((7, 3 * D), F32)
        gab_ref[:, 0:D] = red[R_SH_X:R_SH_X + 1, :] + red[R_SH_C:R_SH_C + 1, :]
        gab_ref[:, D:2 * D] = red[R_SC_X:R_SC_X + 1, :] + red[R_SC_C:R_SC_C + 1, :]
        gab_ref[:, 2 * D:3 * D] = red[R_GATE:R_GATE + 1, :]
        dmc_s[...] = jnp.broadcast_to(dmod_ref[8:9, :], (8, 3 * D))
        off = pl.multiple_of(me_ref[0] * nloc, 128)
        cpart_ref[...] = _dot_nt(dmc_s[:, pl.ds(off, nloc)], w_ref[...])

    return _call(
        body, name="reduce_small",
        out_shape=[jax.ShapeDtypeStruct((PACK_ROWS, D), F32), jax.ShapeDtypeStruct((PACK128_ROWS, HD), F32)]
        + [jax.ShapeDtypeStruct(p.shape[1:], F32) for p in mat_parts]
        + [jax.ShapeDtypeStruct((16, 3 * D), F32), jax.ShapeDtypeStruct((1, 3 * D), F32),
           jax.ShapeDtypeStruct((8, D), F32)],
        in_specs=[pl.BlockSpec(memory_space=pltpu.SMEM)] + [VMEM] * (nm + 3), out_specs=[VMEM] * (nm + 5),
        scratch_shapes=[pltpu.VMEM((8, 3 * D), F32)], compiler_params=_params(),
    )(me, vp_all, vq_all, *mat_parts, ada_w)


def _adamw_ada(c_all, c_ctx, dmod, w, m, v, me):
    nloc = w.shape[1]

    def body(me_ref, c_ref, cc_ref, dm_ref, w_ref, m_ref, v_ref, g_ref, d_ref, mo_ref, vo_ref):
        off = pl.multiple_of(me_ref[0] * nloc, 128)
        dm = dm_ref[:, pl.ds(off, nloc)]
        sx, _ = _silu_and_grad(c_ref[...])
        sc, _ = _silu_and_grad(cc_ref[...])
        g = _dot_tn(sx, dm[0:8, :]) + _dot_tn(jnp.broadcast_to(sc, (8, D)), dm[8:16, :])
        g_ref[...] = g
        d_ref[...], mo_ref[...], vo_ref[...] = _adamw(w_ref[...], g, m_ref[...], v_ref[...])

    sds = jax.ShapeDtypeStruct(w.shape, F32)
    return _call(
        body, name="adamw_ada_w", out_shape=[sds] * 4,
        in_specs=[pl.BlockSpec(memory_space=pltpu.SMEM)] + [VMEM] * 6, out_specs=[VMEM] * 4,
        compiler_params=_params(),
    )(me, c_all, c_ctx, dmod, w, m, v)


_SMALL = ("c_ctx", "ada_b", "norm_g", "conv_w", "conv_b", "lru_wa", "lru_ba", "lru_wx", "lru_bx", "lru_lambda",
          "sgu_ln_g", "sgu_ln_b", "sgu_w", "sgu_b", "final_g")


def _adamw_small(red, redq, mats, cparts, gab, ws, ms, vs, me):
    n = len(_SMALL)

    def body(me_ref, red_ref, redq_ref, wa_ref, wx_ref, sw_ref, cp_ref, gab_ref, *refs):
        w_refs, m_refs, v_refs = refs[:n], refs[n:2 * n], refs[2 * n:3 * n]
        outs = refs[3 * n:]
        off = pl.multiple_of(me_ref[0] * HD, 128)

        def row(r, k=1):
            return red_ref[r:r + k, :]

        cc = w_refs[0][...]
        dcc = cp_ref[0, 0:1, :]
        for i in range(1, N_DEV):
            dcc = dcc + cp_ref[i, 0:1, :]
        grads = dict(
            c_ctx=dcc * _silu_and_grad(cc)[1], ada_b=gab_ref[...], norm_g=row(R_NG_X) + row(R_NG_C),
            conv_w=red_ref[R_CW:R_CW + CONV_W, pl.ds(off, HD)], conv_b=row(R_CB),
            lru_wa=wa_ref[...], lru_ba=redq_ref[Q_BA:Q_BA + 2 * HEADS, :], lru_wx=wx_ref[...],
            lru_bx=redq_ref[Q_BX:Q_BX + 2 * HEADS, :], lru_lambda=red_ref[R_LAM:R_LAM + 2, pl.ds(off, HD)],
            sgu_ln_g=row(R_LN_G), sgu_ln_b=row(R_LN_B), sgu_w=sw_ref[...],
            sgu_b=redq_ref[Q_SGU_B:Q_SGU_B + HEADS, :], final_g=row(R_FINAL_G))
        for j, name in enumerate(_SMALL):
            g = grads[name]
            outs[j][...] = g
            outs[n + j][...], outs[2 * n + j][...], outs[3 * n + j][...] = _adamw(w_refs[j][...], g, m_refs[j][...],
                                                                                 v_refs[j][...])

    sds = [jax.ShapeDtypeStruct(ws[k].shape, F32) for k in _SMALL]
    outs = _call(
        body, name="adamw_small", out_shape=sds * 4,
        in_specs=[pl.BlockSpec(memory_space=pltpu.SMEM)] + [VMEM] * (7 + 3 * n), out_specs=[VMEM] * (4 * n),
        compiler_params=_params(),
    )(me, red, redq, *mats, cparts, gab, *[ws[k] for k in _SMALL], *[ms[k] for k in _SMALL],
      *[vs[k] for k in _SMALL])
    return [dict(zip(_SMALL, outs[i * n:(i + 1) * n])) for i in range(4)]


def kernel(x, c, ctx, c_ctx, ada_w, ada_b, norm_g, w_in, conv_w, conv_b, lru_wa, lru_ba, lru_wx, lru_bx, lru_lambda, sgu_ln_g, sgu_ln_b, sgu_w, sgu_b, w_out, final_g, loss_target, m_c_ctx, m_ada_w, m_ada_b, m_norm_g, m_w_in, m_conv_w, m_conv_b, m_lru_wa, m_lru_ba, m_lru_wx, m_lru_bx, m_lru_lambda, m_sgu_ln_g, m_sgu_ln_b, m_sgu_w, m_sgu_b, m_w_out, m_final_g, v_c_ctx, v_ada_w, v_ada_b, v_norm_g, v_w_in, v_conv_w, v_conv_b, v_lru_wa, v_lru_ba, v_lru_wx, v_lru_bx, v_lru_lambda, v_sgu_ln_g, v_sgu_ln_b, v_sgu_w, v_sgu_b, v_w_out, v_final_g):
    args = dict(locals())
    me_s = 4 * lax.axis_index("x") + 2 * lax.axis_index("y") + lax.axis_index("c")
    me = me_s.astype(jnp.int32).reshape(1)
    xr, ctxr, tgt = x[0], ctx[0], loss_target[0]
    cc = c_ctx.reshape(1, D)
    nw = 2 * HEADS * HD
    view = dict(c_ctx=(1, D), ada_b=(1, 3 * D), norm_g=(1, D), conv_w=(CONV_W, HD), conv_b=(1, D), lru_wa=(nw, HD),
                lru_ba=(2 * HEADS, HD), lru_wx=(nw, HD), lru_bx=(2 * HEADS, HD), lru_lambda=(2, HD), sgu_ln_g=(1, D),
                sgu_ln_b=(1, D), sgu_w=(HEADS * CHUNK, CHUNK), sgu_b=(HEADS, CHUNK), final_g=(1, D))

    zx, hn, w_full, w_out_b, modx, modc, c_all, cw_full, lam_full = _front_project(
        xr, c, cc, ada_w[0], ada_b, norm_g, w_in[0], w_out[0], conv_w[0], lru_lambda[0], me)
    zc, hnc = _project(ctxr, modc, norm_g, w_full, D, LC, "project_ctx")
    ba, bx = lru_ba.reshape(view["lru_ba"]), lru_bx.reshape(view["lru_bx"])
    yl, wout_all = _lru_forward(zx, zc, cw_full, conv_b, lru_wa[0], lru_wx[0], ba, bx, lam_full, [w_out_b], ["ag"])
    wout_full = wout_all.reshape(D_MIX, D)
    ws_b = sgu_w[0].astype(BF16)
    dz, dyl, dxn, ycat, dob, dws, dbst, mvec = _mixer_loss(
        xr, tgt, zx, yl, modx, final_g.reshape(1, D), sgu_ln_g, sgu_ln_b, ws_b, jnp.swapaxes(ws_b, 1, 2),
        sgu_b[0].T, wout_full, 256)

    wout_sums = _grad_w(ycat, dob, None, None, 1024, "grad_w_out", D, 0, 1, "rows")
    rest_sums = _grad_w(hn, dz, None, None, 1024, "grad_w_in_rest", 2 * W_IN_SHARD, 1, 3, "cols")
    dz, dxac, dwa, dwx, dba, dbx, dlam, dcw, dcb, win_parts, wout_parts, _, _ = _lru_backward(
        zx, zc, dyl, dz, cw_full, conv_b, lru_wa[0], lru_wx[0], ba, bx, lam_full, [rest_sums, wout_sums],
        first_chips=[1, 0])
    first_sums = _grad_w(hn, dz, hnc, dxac, 1024, "grad_w_in_first", 2 * W_IN_SHARD, 0, 1, "cols")
    mats = [dwa.reshape(N_DEV, nw // N_DEV, HD), dwx.reshape(N_DEV, nw // N_DEV, HD), dws]
    mat_sums = _reduce2_local(mats, ["a2a"] * 3, me, "reduce_mat", out_dtype=BF16)
    gx, xvec, win_parts, *mat_parts = _grad_rows(
        xr, dz, w_full, modx, norm_g, dxn, D_IN, 256, "grad_rows_x", chip_sums=[first_sums, *mat_sums],
        first_chips=[0, 0, 0, 0], dests=[win_parts, None, None, None])[:6]
    (cvec,) = _grad_rows(ctxr, dxac, w_full, modc, norm_g, None, D, LC, "grad_rows_ctx")
    pack = jnp.concatenate([mvec[0:5], xvec[0:3], cvec[0:3], dlam, dcw, dcb,
                            jnp.zeros((PACK_ROWS - R_CB - 1, D), F32)], axis=0)
    pack128 = jnp.concatenate([dba, dbx, dbst.T], axis=0)
    vp_all, vq_all = _gather2([pack, pack128], ["ag", "ag"], "gather_pack")
    red, redq, *rest = _reduce_small(vp_all, vq_all, mat_parts, ada_w[0], me)
    mat_pieces, (dmod, gab, cpart) = rest[:3], rest[3:]
    *mats_all, cparts = _gather2([*mat_pieces, cpart], ["ag"] * 4, "gather_small")

    g_w_in, d_w_in, nm_w_in, nv_w_in = _adamw_reduced(win_parts, w_in[0], m_w_in[0], v_w_in[0], 256, "adamw_w_in")
    g_w_out, d_w_out, nm_w_out, nv_w_out = _adamw_reduced(wout_parts, w_out[0], m_w_out[0], v_w_out[0], 128,
                                                          "adamw_w_out")
    g_ada, d_ada, nm_ada, nv_ada = _adamw_ada(c_all, cc, dmod, ada_w[0], m_ada_w[0], v_ada_w[0], me)
    ws = {k: args[k].reshape(view[k]) for k in _SMALL}
    ms = {k: args["m_" + k].reshape(view[k]) for k in _SMALL}
    vs = {k: args["v_" + k].reshape(view[k]) for k in _SMALL}
    small = _adamw_small(red, redq, [m.reshape(-1, HD) for m in mats_all], cparts, gab, ws, ms, vs, me)
    big = dict(w_in=(g_w_in, d_w_in, nm_w_in, nv_w_in), w_out=(g_w_out, d_w_out, nm_w_out, nv_w_out),
               ada_w=(g_ada, d_ada, nm_ada, nv_ada))

    loss = red[R_LOSS, 0]
    names = ("c_ctx", "ada_w", "ada_b", "norm_g", "w_in", "conv_w", "conv_b", "lru_wa", "lru_ba", "lru_wx", "lru_bx",
             "lru_lambda", "sgu_ln_g", "sgu_ln_b", "sgu_w", "sgu_b", "w_out", "final_g")
    outs = [loss, gx.reshape(x.shape)]
    for kind in range(4):
        for k in names:
            val = big[k][kind] if k in big else small[kind][k]
            outs.append(val.reshape(args[k].shape))
    return tuple(outs)
```

```python
import functools

import jax
import jax.numpy as jnp
from jax import lax
from jax.experimental import pallas as pl
from jax.experimental.pallas import tpu as pltpu

F32 = jnp.float32
BF16 = jnp.bfloat16

N_DEV = 8
D = 1024
L = 2048
LC = 256
HEADS = 8
HD = 128
CHUNK = 128
D_IN = 5 * D
W_IN_SHARD = D_IN // N_DEV
ROWS = 256
D_MIX = 2 * D
CONV_W = 4
LRU_C = 8.0
NORM_EPS = 1e-6
LN_EPS = 1e-5
ADAM_LR, ADAM_B1, ADAM_B2, ADAM_EPS, ADAM_WD, ADAM_STEP = 0.001, 0.9, 0.999, 1e-08, 0.01, 10

VMEM_LIMIT = 56 * 1024 * 1024

HBM = pl.BlockSpec(memory_space=pltpu.HBM)
VMEM = pl.BlockSpec(memory_space=pltpu.VMEM)
MESH = pl.DeviceIdType.MESH


def _call(body, **kw):
    return pl.pallas_call(body, **kw)


def _params(*sem):
    return pltpu.CompilerParams(dimension_semantics=sem, vmem_limit_bytes=VMEM_LIMIT)


def _sigmoid(x):
    return 0.5 * jnp.tanh(0.5 * x) + 0.5


def _silu_and_grad(x):
    s = _sigmoid(x)
    return x * s, s * (1.0 + x * (1.0 - s))


_G0 = 0.7978845608028654
_G1 = 0.044715


def _gelu_and_grad(x):
    x2 = x * x
    t = jnp.tanh(_G0 * (x + _G1 * x * x2))
    cdf = 0.5 * (1.0 + t)
    return x * cdf, cdf + 0.5 * x * (1.0 - t * t) * (_G0 * (1.0 + 3.0 * _G1 * x2))


def _gelu(x):
    return 0.5 * x * (1.0 + jnp.tanh(_G0 * (x + _G1 * x * x * x)))


def _softplus(z):
    t = jnp.exp(-jnp.abs(z))
    u = 1.0 + t
    log1p = jnp.where(u == 1.0, t, jnp.log(u) * t / jnp.where(u == 1.0, 1.0, u - 1.0))
    return jnp.maximum(z, 0.0) + log1p


def _dot(a, b):
    return jnp.dot(a, b, preferred_element_type=F32)


def _dot_nt(a, b):
    return lax.dot_general(a, b, (((1,), (1,)), ((), ())), preferred_element_type=F32)


def _dot_tn(a, b):
    return lax.dot_general(a, b, (((0,), (0,)), ((), ())), preferred_element_type=F32)


def _rows(shape):
    return lax.broadcasted_iota(jnp.int32, shape, 0)


def _shift_down(x, first):
    y = pltpu.roll(x, 1, 0)
    head = jnp.where(_rows((8, x.shape[1])) == 0, first, y[0:8])
    return jnp.concatenate([head, y[8:]], axis=0)


def _shift_up(x, last):
    n = x.shape[0]
    y = pltpu.roll(x, n - 1, 0)
    tail = jnp.where(_rows((8, x.shape[1])) == 7, last, y[n - 8:])
    return jnp.concatenate([y[:n - 8], tail], axis=0)


def _gather2(arrays, modes, name):
    n = len(arrays)

    def body(*refs):
        start, forward, finish = _gather2_ops(refs[:n], refs[n:2 * n], modes, *refs[2 * n:])
        start()
        forward()
        finish()

    return _call(
        body, name=name, out_shape=_gather2_shapes(arrays, modes), in_specs=[HBM] * n, out_specs=[HBM] * n,
        scratch_shapes=_gather2_sems(n), compiler_params=pltpu.CompilerParams(has_side_effects=True),
    )(*[pltpu.with_memory_space_constraint(a, pltpu.HBM) for a in arrays])


def _gather2_shapes(arrays, modes):
    return [jax.ShapeDtypeStruct((N_DEV,) + a.shape if m == "ag" else (a.shape[0], N_DEV * a.shape[1]), a.dtype)
            for a, m in zip(arrays, modes)]


def _gather2_sems(n):
    return [pltpu.SemaphoreType.DMA((n, N_DEV - 1)), pltpu.SemaphoreType.DMA((n, N_DEV - 1)),
            pltpu.SemaphoreType.DMA((n,))]


def _gather2_ops(ins, outs, modes, send_sems, recv_sems, local_sems):
    n = len(ins)
    x, y, c = lax.axis_index("x"), lax.axis_index("y"), lax.axis_index("c")
    me, sibling = (x, y, c), (x, y, 1 - c)
    chips = [(x ^ (k >> 1), y ^ (k & 1)) for k in (1, 2, 3)]

    def slot(j, px, py, pc):
        dev = 4 * px + 2 * py + pc
        if modes[j] == "agc":
            w = ins[j].shape[1]
            return outs[j].at[:, pl.ds(pl.multiple_of(dev * w, 128), w)]
        return outs[j].at[dev]

    def copy(j, k, block, to, src=None):
        return pltpu.make_async_remote_copy(
            src_ref=slot(j, *block) if src is None else src, dst_ref=slot(j, *block),
            send_sem=send_sems.at[j, k], recv_sem=recv_sems.at[j, k], device_id=to, device_id_type=MESH)

    def own(j):
        return pltpu.make_async_copy(ins[j], slot(j, *me), local_sems.at[j])

    def first(j):
        return [copy(j, 0, me, sibling, src=ins[j])] + [copy(j, 1 + i, me, (*chip, c), src=ins[j])
                                                        for i, chip in enumerate(chips)]

    def passed(j, i):
        return copy(j, 4 + i, (*chips[i], c), sibling)

    def start():
        for j in range(n):
            own(j).start()
            for cp in first(j):
                cp.start()

    def forward():
        for i, chip in enumerate(chips):
            for j in range(n):
                copy(j, 1 + i, (*chip, c), me).wait_recv()
                passed(j, i).start()

    def finish():
        for j in range(n):
            copy(j, 0, sibling, me).wait_recv()
            for i, chip in enumerate(chips):
                copy(j, 4 + i, (*chip, 1 - c), me).wait_recv()
            for cp in first(j) + [passed(j, i) for i in range(3)]:
                cp.wait_send()
            own(j).wait()

    return start, forward, finish


def _reduce2_local(arrays, modes, me, name, counts=None, out_dtype=None):
    n = len(arrays)
    counts = counts or [4] * n
    shapes = [(a.shape[1], a.shape[2]) if m == "a2a" else (a.shape[0], a.shape[1] // (2 * cnt))
              for a, m, cnt in zip(arrays, modes, counts)]
    staged = [jax.ShapeDtypeStruct((cnt,) + s, a.dtype) for s, a, cnt in zip(shapes, arrays, counts)]

    def piece(ref, mode, dev, w):
        return ref.at[dev] if mode == "a2a" else ref.at[:, pl.ds(pl.multiple_of(dev * w, 128), w)]

    def to_sibling(*refs):
        ins, outs = refs[:n], refs[n:2 * n]
        send_sems, recv_sems = refs[2 * n:]
        x, y, c = lax.axis_index("x"), lax.axis_index("y"), lax.axis_index("c")
        copies = []
        for j in range(n):
            for q in range(counts[j]):
                cp = pltpu.make_async_remote_copy(
                    src_ref=piece(ins[j], modes[j], 2 * q + (1 - c), shapes[j][1]), dst_ref=outs[j].at[q],
                    send_sem=send_sems.at[j, q], recv_sem=recv_sems.at[j, q], device_id=(x, y, 1 - c),
                    device_id_type=MESH)
                cp.start()
                copies.append(cp)
        for cp in copies:
            cp.wait()

    stage = _call(
        to_sibling, name=name + "_d2d", out_shape=staged, in_specs=[HBM] * n, out_specs=[HBM] * n,
        scratch_shapes=[pltpu.SemaphoreType.DMA((n, 4)), pltpu.SemaphoreType.DMA((n, 4))],
        compiler_params=pltpu.CompilerParams(has_side_effects=True),
    )(*[pltpu.with_memory_space_constraint(a, pltpu.HBM) for a in arrays])

    def add(me_ref, *refs):
        del me_ref
        own, got, outs = refs[:n], refs[n:2 * n], refs[2 * n:]
        for j in range(n):
            mine = own[j][0] if modes[j] == "a2a" else own[j][...]
            outs[j][0] = (mine.astype(F32) + got[j][0].astype(F32)).astype(outs[j].dtype)

    in_specs, slot_specs = [], []
    for (r, w), m, cnt in zip(shapes, modes, counts):
        if m == "a2a":
            in_specs.append(pl.BlockSpec(
                (1, r, w), lambda q, me_ref, cnt=cnt: (2 * jnp.minimum(q, cnt - 1) + me_ref[0] % 2, 0, 0)))
        else:
            in_specs.append(pl.BlockSpec(
                (r, w), lambda q, me_ref, cnt=cnt: (0, 2 * jnp.minimum(q, cnt - 1) + me_ref[0] % 2)))
        slot_specs.append(pl.BlockSpec((1, r, w), lambda q, me_ref, cnt=cnt: (jnp.minimum(q, cnt - 1), 0, 0)))
    return _call(
        add, name=name + "_add",
        out_shape=[jax.ShapeDtypeStruct(s.shape, out_dtype or s.dtype) for s in staged],
        grid_spec=pltpu.PrefetchScalarGridSpec(num_scalar_prefetch=1, grid=(max(counts),),
                                               in_specs=in_specs + slot_specs, out_specs=slot_specs),
        compiler_params=_params("arbitrary"),
    )(me, *arrays, *stage)


def _chips_sems(n):
    return [pltpu.SemaphoreType.DMA((n, 6)), pltpu.SemaphoreType.DMA((n, 6)), pltpu.SemaphoreType.DMA((n,))]


def _chips_stage_shapes(chip_sums):
    return [jax.ShapeDtypeStruct((2, a.shape[1] // 2, a.shape[2]), a.dtype) for a in chip_sums]


def _chips_ops(ins, outs, stages, send_sems, recv_sems, local_sems, first_chips=None):
    x, y, c = lax.axis_index("x"), lax.axis_index("y"), lax.axis_index("c")
    qm = 2 * x + y
    first_chips = first_chips or [0] * len(ins)

    def owns(j, chip):
        lo, cnt = first_chips[j], ins[j].shape[0]
        if lo == 0 and cnt == 4:
            return None
        return jnp.logical_and(chip >= lo, chip < lo + cnt)

    def guarded(cond, fn):
        if cond is None:
            fn()
        else:
            pl.when(cond)(fn)

    def slot(j, chip):
        return jnp.clip(chip - first_chips[j], 0, ins[j].shape[0] - 1)

    def half(j, i):
        h = ins[j].shape[1] // 2
        return pl.ds(i * h, h)

    def copy(j, sem, src, dst, k):
        return pltpu.make_async_remote_copy(
            src_ref=src, dst_ref=dst, send_sem=send_sems.at[j, sem], recv_sem=recv_sems.at[j, sem],
            device_id=(x ^ (k >> 1), y ^ (k & 1), c), device_id_type=MESH)

    def direct(j, k):
        return copy(j, k - 1, ins[j].at[slot(j, qm ^ k)], outs[j].at[qm], k)

    def first_hop(j, k):
        return copy(j, 1 + k, ins[j].at[slot(j, qm ^ 3), half(j, k - 1)], stages[j].at[k - 1], k)

    def second_hop(j, k):
        return copy(j, 3 + k, stages[j].at[2 - k], outs[j].at[qm ^ (3 - k), half(j, 2 - k)], k)

    def local(j):
        return pltpu.make_async_copy(ins[j].at[slot(j, qm)], outs[j].at[qm], local_sems.at[j])

    def start():
        for j in range(len(ins)):
            for k in (1, 2):
                guarded(owns(j, qm ^ 3), lambda j=j, k=k: first_hop(j, k).start())
        for j in range(len(ins)):
            for k in (1, 2):
                guarded(owns(j, qm ^ k), lambda j=j, k=k: direct(j, k).start())
            guarded(owns(j, qm), lambda j=j: local(j).start())

    def forward():
        for j in range(len(ins)):
            for k in (1, 2):
                def pass_on(j=j, k=k):
                    first_hop(j, 3 - k).wait_recv()
                    second_hop(j, k).start()
                guarded(owns(j, qm ^ k), pass_on)

    def finish():
        for j in range(len(ins)):
            for k in (1, 2):
                guarded(owns(j, qm ^ k), lambda j=j, k=k: direct(j, k).wait_send())
                guarded(owns(j, qm ^ k), lambda j=j, k=k: second_hop(j, k).wait_send())
                guarded(owns(j, qm ^ 3), lambda j=j, k=k: first_hop(j, k).wait_send())
                guarded(owns(j, qm), lambda j=j, k=k: direct(j, k).wait_recv())
                guarded(owns(j, qm), lambda j=j, k=k: second_hop(j, k).wait_recv())
            guarded(owns(j, qm), lambda j=j: local(j).wait())

    return start, forward, finish


def _front(c, c_ctx, ada_w, ada_b, w_in, w_out, me):
    nloc = ada_w.shape[1]

    def body(me_ref, c_ref, cc_ref, aw_ref, ab_ref, win_ref, wout_ref,
             wfull_ref, woutb_ref, modx_ref, modc_ref, call_ref,
             wb_s, part_s, parts_s, w_send, w_recv, w_local, s_send, s_recv):
        x, y, cidx = lax.axis_index("x"), lax.axis_index("y"), lax.axis_index("c")
        me = me_ref[0]
        wb_s[...] = win_ref[...].astype(BF16)
        woutb_ref[...] = wout_ref[...].astype(BF16)
        start, forward, finish = _gather2_ops([wb_s], [wfull_ref], ["agc"], w_send, w_recv, w_local)
        start()

        def small_gather(src, my_slot, stage):
            copies = []
            for k in range(1, N_DEV):
                peer = (x ^ (k >> 2), y ^ ((k >> 1) & 1), cidx ^ (k & 1))
                cp = pltpu.make_async_remote_copy(src_ref=src, dst_ref=my_slot, send_sem=s_send.at[stage, k - 1],
                                                  recv_sem=s_recv.at[stage, k - 1], device_id=peer,
                                                  device_id_type=MESH)
                cp.start()
                copies.append(cp)
            pltpu.sync_copy(src, my_slot)
            for cp in copies:
                cp.wait()

        small_gather(c_ref, call_ref.at[pl.ds(me, 1), :], 0)
        off = pl.multiple_of(me * nloc, 128)
        b = ab_ref[:, pl.ds(off, nloc)]
        w = aw_ref[...]
        sx, _ = _silu_and_grad(call_ref[...])
        sc, _ = _silu_and_grad(jnp.broadcast_to(cc_ref[...], (8, D)))
        part_s[0:8, :] = _dot(sx, w) + b
        part_s[8:16, :] = _dot(sc, w) + b
        small_gather(part_s, parts_s.at[me], 1)
        mine = _rows((16, nloc)) == me
        for j in range(N_DEV):
            pj = parts_s[j]
            modx_ref[:, j * nloc:(j + 1) * nloc] = jnp.sum(jnp.where(mine, pj, 0.0), axis=0, keepdims=True)
            modc_ref[:, j * nloc:(j + 1) * nloc] = pj[8:9, :]
        forward()
        finish()

    return _call(
        body, name="front",
        out_shape=[jax.ShapeDtypeStruct((D, D_IN), BF16), jax.ShapeDtypeStruct(w_out.shape, BF16),
                   jax.ShapeDtypeStruct((1, 3 * D), F32), jax.ShapeDtypeStruct((1, 3 * D), F32),
                   jax.ShapeDtypeStruct((N_DEV, D), F32)],
        in_specs=[pl.BlockSpec(memory_space=pltpu.SMEM)] + [VMEM] * 6, out_specs=[HBM, VMEM, VMEM, VMEM, VMEM],
        scratch_shapes=[pltpu.VMEM(w_in.shape, BF16), pltpu.VMEM((16, nloc), F32),
                        pltpu.VMEM((N_DEV, 16, nloc), F32)] + _gather2_sems(1) +
                       [pltpu.SemaphoreType.DMA((2, N_DEV - 1)), pltpu.SemaphoreType.DMA((2, N_DEV - 1))],
        compiler_params=pltpu.CompilerParams(vmem_limit_bytes=VMEM_LIMIT, has_side_effects=True),
    )(me, c, c_ctx, ada_w, ada_b, w_in, w_out)


ARRIVAL = (0, 1, 2, 4, 3, 5, 6, 7)


def _front_project(xr, c, c_ctx, ada_w, ada_b, ng, w_in, w_out, cw, lam, me):
    nloc = ada_w.shape[1]
    ws = W_IN_SHARD
    order = me[0] ^ jnp.asarray(ARRIVAL, jnp.int32)

    def body(ord_ref, x_ref, c_ref, cc_ref, aw_ref, ab_ref, ng_ref, win_ref, wout_ref, cw_ref, lam_ref,
             z_ref, hn_ref, wfull_ref, woutb_ref, modx_ref, modc_ref, call_ref, cwf_ref, lamf_ref,
             wv, call_s, part_s, parts_s, w_send, w_recv, hbm_sems, s_send, s_recv, g_send, g_recv, g_local):
        t = pl.program_id(0)
        x, y, cidx = lax.axis_index("x"), lax.axis_index("y"), lax.axis_index("c")
        me_i = ord_ref[0]
        sibling = (x, y, 1 - cidx)
        chips = [(x ^ (k >> 1), y ^ (k & 1)) for k in (1, 2, 3)]
        g_start, g_pass, g_finish = _gather2_ops([cw_ref, lam_ref], [cwf_ref, lamf_ref], ["agc", "agc"],
                                                 g_send, g_recv, g_local)

        def shard_copy(k, px, py, pc, to, half=None):
            slot = wv.at[4 * px + 2 * py + pc]
            if half is not None:
                slot = slot.at[pl.ds(half * (D // 2), D // 2), :]
            return pltpu.make_async_remote_copy(src_ref=slot, dst_ref=slot, send_sem=w_send.at[k],
                                                recv_sem=w_recv.at[k], device_id=to, device_id_type=MESH)

        def small_gather(src, my_slot, stage):
            copies = []
            for k in range(1, N_DEV):
                peer = (x ^ (k >> 2), y ^ ((k >> 1) & 1), cidx ^ (k & 1))
                cp = pltpu.make_async_remote_copy(src_ref=src, dst_ref=my_slot, send_sem=s_send.at[stage, k - 1],
                                                  recv_sem=s_recv.at[stage, k - 1], device_id=peer,
                                                  device_id_type=MESH)
                cp.start()
                copies.append(cp)
            pltpu.sync_copy(src, my_slot)
            return copies

        def finish_small(copies):
            for cp in copies:
                cp.wait()

        def to_neighbours(half):
            for i in (0, 1):
                shard_copy(1 + i, x, y, cidx, (*chips[i], cidx), half=half).start()

        @pl.when(t == 0)
        def _():
            g_start()
            wv[me_i] = win_ref[...].astype(BF16)
            woutb_ref[...] = wout_ref[...].astype(BF16)
            shard_copy(0, x, y, cidx, sibling).start()
            finish_small(small_gather(c_ref, call_s.at[pl.ds(me_i, 1), :], 0))
            to_neighbours(0)
            call_ref[...] = call_s[...]
            off = pl.multiple_of(me_i * nloc, 128)
            b = ab_ref[:, pl.ds(off, nloc)]
            w = aw_ref[...]
            sx, _ = _silu_and_grad(call_s[...])
            sc, _ = _silu_and_grad(jnp.broadcast_to(cc_ref[...], (8, D)))
            part_s[0:8, :] = _dot(sx, w) + b
            part_s[8:16, :] = _dot(sc, w) + b
            parts_sent = small_gather(part_s, parts_s.at[me_i], 1)
            to_neighbours(1)
            finish_small(parts_sent)
            mine = _rows((16, nloc)) == me_i
            for j in range(N_DEV):
                pj = parts_s[j]
                modx_ref[:, j * nloc:(j + 1) * nloc] = jnp.sum(jnp.where(mine, pj, 0.0), axis=0, keepdims=True)
                modc_ref[:, j * nloc:(j + 1) * nloc] = pj[8:9, :]
            shift, scale1, ngv = modx_ref[:, 0:D], 1.0 + modx_ref[:, D:2 * D], ng_ref[...]
            for r in range(L // ROWS):
                rsl = slice(r * ROWS, (r + 1) * ROWS)
                xv = x_ref[rsl, :]
                rs = lax.rsqrt(jnp.mean(xv * xv, axis=-1, keepdims=True) + NORM_EPS)
                hn_ref[rsl, :] = ((xv * rs * ngv) * scale1 + shift).astype(BF16)

        @pl.when(t == 1)
        def _():
            shard_copy(0, x, y, 1 - cidx, sibling).wait_recv()
            g_pass()

        for i in (0, 1):
            @pl.when(t == ARRIVAL.index((2, 4)[i]))
            def _(i=i):
                shard_copy(1 + i, *chips[i], cidx, sibling).wait_recv()
                shard_copy(4 + i, *chips[i], cidx, sibling).start()
                shard_copy((7, 3)[i], *chips[i], cidx, (*chips[1 - i], cidx), half=i).start()

        @pl.when(t == ARRIVAL.index(6))
        def _():
            shard_copy(3, *chips[2], cidx, sibling, half=1).wait_recv()
            shard_copy(7, *chips[2], cidx, sibling, half=0).wait_recv()
            shard_copy(6, *chips[2], cidx, sibling).start()

        for i in range(3):
            @pl.when(t == ARRIVAL.index((3, 5, 7)[i]))
            def _(i=i):
                shard_copy(4 + i, *chips[i], 1 - cidx, sibling).wait_recv()

        @pl.when(t == 2)
        def _():
            g_finish()

        dev = ord_ref[t]
        for r in range(L // (2 * ROWS)):
            rsl = slice(r * 2 * ROWS, (r + 1) * 2 * ROWS)
            z_ref[rsl, :] = _dot(hn_ref[rsl, :], wv[dev])
        col = pl.ds(pl.multiple_of(dev * ws, 128), ws)
        pltpu.make_async_copy(wv.at[dev], wfull_ref.at[:, col], hbm_sems.at[t]).start()

        @pl.when(t == N_DEV - 1)
        def _():
            for k in (0, 1, 2, 4, 5, 6):
                shard_copy(k, x, y, cidx, sibling).wait_send()
            for k in (3, 7):
                shard_copy(k, x, y, cidx, sibling, half=0).wait_send()
            for s in range(N_DEV):
                pltpu.make_async_copy(wv.at[0], wfull_ref.at[:, pl.ds(0, ws)], hbm_sems.at[s]).wait()

    const = lambda *shape: pl.BlockSpec(shape, lambda t, o: (0,) * len(shape))
    once = lambda *shape: pl.BlockSpec(shape, lambda t, o: (0,) * len(shape), pipeline_mode=pl.Buffered(1))
    return _call(
        body, name="front_project",
        out_shape=[jax.ShapeDtypeStruct((L, D_IN), F32), jax.ShapeDtypeStruct((L, D), BF16),
                   jax.ShapeDtypeStruct((D, D_IN), BF16), jax.ShapeDtypeStruct(w_out.shape, BF16),
                   jax.ShapeDtypeStruct((1, 3 * D), F32), jax.ShapeDtypeStruct((1, 3 * D), F32),
                   jax.ShapeDtypeStruct((N_DEV, D), F32), jax.ShapeDtypeStruct((CONV_W, D), F32),
                   jax.ShapeDtypeStruct((2, D), F32)],
        grid_spec=pltpu.PrefetchScalarGridSpec(
            num_scalar_prefetch=1, grid=(N_DEV,),
            in_specs=[once(L, D), const(1, D), const(1, D), once(D, nloc), const(1, 3 * D), const(1, D),
                      once(D, ws), once(*w_out.shape), HBM, HBM],
            out_specs=[pl.BlockSpec((L, ws), lambda t, o: (0, o[t])), const(L, D), HBM, const(*w_out.shape),
                       const(1, 3 * D), const(1, 3 * D), const(N_DEV, D), HBM, HBM],
            scratch_shapes=[pltpu.VMEM((N_DEV, D, ws), BF16), pltpu.VMEM((N_DEV, D), F32), pltpu.VMEM((16, nloc), F32),
                            pltpu.VMEM((N_DEV, 16, nloc), F32), pltpu.SemaphoreType.DMA((8,)),
                            pltpu.SemaphoreType.DMA((8,)), pltpu.SemaphoreType.DMA((N_DEV,)),
                            pltpu.SemaphoreType.DMA((2, N_DEV - 1)), pltpu.SemaphoreType.DMA((2, N_DEV - 1))]
            + _gather2_sems(2)),
        compiler_params=pltpu.CompilerParams(dimension_semantics=("arbitrary",), vmem_limit_bytes=VMEM_LIMIT,
                                             has_side_effects=True),
    )(order, xr, c, c_ctx, ada_w, ada_b, ng, w_in, w_out, pltpu.with_memory_space_constraint(cw, pltpu.HBM),
      pltpu.with_memory_space_constraint(lam, pltpu.HBM))


def _project(xr, mod, ng, w, ncols, tm, name, gather=None, gather_modes=()):
    rows = xr.shape[0]
    steps = rows // tm
    ng_ = len(gather or ())

    def body(x_ref, sh_ref, sc_ref, ng_ref, w_ref, *rest):
        z_ref, hn_ref = rest[ng_:ng_ + 2]
        if ng_:
            start, forward, finish = _gather2_ops(rest[:ng_], rest[ng_ + 2:2 * ng_ + 2], gather_modes,
                                                  *rest[2 * ng_ + 2:])
            pl.when(pl.program_id(0) == 0)(start)
            pl.when(pl.program_id(0) == steps // 2)(forward)
        x = x_ref[...]
        rs = lax.rsqrt(jnp.mean(x * x, axis=-1, keepdims=True) + NORM_EPS)
        hn = (x * rs * ng_ref[...]) * (1.0 + sc_ref[...]) + sh_ref[...]
        hb = hn.astype(BF16)
        hn_ref[...] = hb
        for n in range(ncols // D):
            z_ref[:, n * D:(n + 1) * D] = _dot(hb, w_ref[:, n * D:(n + 1) * D])
        if ng_:
            pl.when(pl.program_id(0) == steps - 1)(finish)

    vec = pl.BlockSpec((1, D), lambda i: (0, 0))
    gathered = _gather2_shapes(gather, gather_modes) if ng_ else []
    return _call(
        body, name=name, grid=(steps,),
        out_shape=[jax.ShapeDtypeStruct((rows, ncols), F32), jax.ShapeDtypeStruct((rows, D), BF16)] + gathered,
        in_specs=[pl.BlockSpec((tm, D), lambda i: (i, 0)), vec, pl.BlockSpec((1, D), lambda i: (0, 1)), vec,
                  pl.BlockSpec((D, ncols), lambda i: (0, 0), pipeline_mode=pl.Buffered(1))] + [HBM] * ng_,
        out_specs=[pl.BlockSpec((tm, ncols), lambda i: (i, 0)), pl.BlockSpec((tm, D), lambda i: (i, 0))] + [HBM] * ng_,
        scratch_shapes=_gather2_sems(ng_) if ng_ else [],
        compiler_params=pltpu.CompilerParams(dimension_semantics=("arbitrary",), vmem_limit_bytes=VMEM_LIMIT,
                                             has_side_effects=bool(ng_)),
    )(xr, mod, mod, ng, w, *[pltpu.with_memory_space_constraint(a, pltpu.HBM) for a in gather or ()])


def _scan_pair(af_ref, uf_ref, hf_ref, h0f, ab_ref, ub_ref, hb_ref, h0b, t_len):
    span = 8 * SCAN_BLOCKS
    nit = t_len // span
    rows = _rows((8, HD))

    def local_scan(a, b, forward):
        for s in (1, 2, 4):
            sh = s if forward else 8 - s
            m = rows >= s if forward else rows < 8 - s
            b = a * jnp.where(m, pltpu.roll(b, sh, 0), 0.0) + b
            a = a * jnp.where(m, pltpu.roll(a, sh, 0), 1.0)
        return a, b

    def span_scan(a_ref, u_ref, h_ref, off, carry, forward):
        order = range(SCAN_BLOCKS) if forward else range(SCAN_BLOCKS - 1, -1, -1)
        last = slice(7, 8) if forward else slice(0, 1)
        for q in order:
            rs = pl.ds(off + 8 * q, 8)
            a, b = local_scan(a_ref[rs, :], u_ref[rs, :], forward)
            h_ref[rs, :] = b + a * carry
            carry = a[last, :] * carry + b[last, :]
        return carry

    def body(k, carry):
        cf, cb = carry
        cf = span_scan(af_ref, uf_ref, hf_ref, pl.multiple_of(k * span, span), cf, True)
        cb = span_scan(ab_ref, ub_ref, hb_ref, pl.multiple_of((nit - 1 - k) * span, span), cb, False)
        return cf, cb

    return lax.fori_loop(0, nit, body, (h0f, h0b))


SCAN_BLOCKS = 8


def _shifted(pad_ref, x, offsets, before=0.0, after=0.0):
    n = x.shape[0]
    pad_ref[0:8, :] = jnp.broadcast_to(jnp.asarray(before, F32), (8, x.shape[1]))
    pad_ref[8:8 + n, :] = x
    pad_ref[8 + n:16 + n, :] = jnp.broadcast_to(jnp.asarray(after, F32), (8, x.shape[1]))
    return [pad_ref[8 + o:8 + o + n, :] for o in offsets]


def _conv(xa, cw, cb, pad_ref):
    xm1, xp1, xp2 = _shifted(pad_ref, xa, (-1, 1, 2))
    return xm1 * cw[0:1, :] + xa * cw[1:2, :] + xp1 * cw[2:3, :] + xp2 * cw[3:4, :] + cb


def _gates(xc, wa, wx, ba, bx, nsp):
    xb = xc.astype(BF16)
    r = _sigmoid(_dot(xb, wa) + ba)
    i = _sigmoid(_dot(xb, wx) + bx)
    log_a = r * nsp
    a = jnp.exp(log_a)
    g2 = jnp.tanh(log_a) * (-1.0 - a * a)
    rg = lax.rsqrt(jnp.maximum(g2, 1e-30))
    return r, i, a, g2 * rg, rg


def _lru_param_specs():
    h4 = pl.BlockSpec((2, 1, HD, HD), lambda h: (0, h, 0, 0))
    v2 = pl.BlockSpec((2, HD), lambda h: (0, h))
    b16 = pl.BlockSpec((2 * HEADS, HD), lambda h: (0, 0))
    return dict(
        xa=pl.BlockSpec((L, HD), lambda h: (0, h)), xac=pl.BlockSpec((LC, HD), lambda h: (0, h)),
        cw=pl.BlockSpec((CONV_W, HD), lambda h: (0, h)), cb=pl.BlockSpec((1, HD), lambda h: (0, h)), h4=h4, v2=v2,
        b16=b16)


def _bias_row(ref, d):
    mask = _rows((2 * HEADS, HD)) == d * HEADS + pl.program_id(0)
    return jnp.sum(jnp.where(mask, ref[...], 0.0), axis=0, keepdims=True), mask


def _lru_forward(zx, zc, cw, cb, wa, wx, ba, bx, lam, gather, gather_modes):
    ng_ = len(gather)

    def body(xa_ref, xac_ref, cw_ref, cb_ref, wa_ref, wx_ref, ba_ref, bx_ref, lam_ref, *rest):
        yl_ref = rest[ng_]
        af, uf, hf, ab, ub, hb, pad_s = rest[2 * ng_ + 1:2 * ng_ + 8]
        start, pass_on, finish = _gather2_ops(rest[:ng_], rest[ng_ + 1:2 * ng_ + 1], gather_modes,
                                              *rest[2 * ng_ + 8:])
        pl.when(pl.program_id(0) == 0)(start)
        pl.when(pl.program_id(0) == HEADS // 2)(pass_on)
        pl.when(pl.program_id(0) == HEADS - 1)(finish)
        cwv, cbv = cw_ref[...], cb_ref[...]
        nsp = (-LRU_C) * _softplus(-lam_ref[...])

        def forward(xa, t_len, h0f, h0b):
            xc = _conv(xa, cwv, cbv, pad_s)
            for d, (a_ref, u_ref) in enumerate(((af, uf), (ab, ub))):
                _, i, a, gamma, _ = _gates(xc, wa_ref[d, 0].astype(BF16), wx_ref[d, 0].astype(BF16),
                                           _bias_row(ba_ref, d)[0], _bias_row(bx_ref, d)[0], nsp[d:d + 1, :])
                a_ref[0:t_len, :] = a
                u_ref[0:t_len, :] = gamma * (i * xc)
            return _scan_pair(af, uf, hf, h0f, ab, ub, hb, h0b, t_len)

        z = jnp.zeros((1, HD), F32)
        h0f, h0b = forward(xac_ref[...], LC, z, z)
        forward(xa_ref[...], L, h0f, h0b)
        yl_ref[...] = hf[...] + hb[...]

    s = _lru_param_specs()
    return _call(
        body, name="lru_forward", grid=(HEADS,),
        out_shape=[jax.ShapeDtypeStruct((L, D), F32)] + _gather2_shapes(gather, gather_modes),
        in_specs=[s["xa"], s["xac"], s["cw"], s["cb"], s["h4"], s["h4"], s["b16"], s["b16"], s["v2"]] + [HBM] * ng_,
        out_specs=[pl.BlockSpec((L, HD), lambda h: (0, h))] + [HBM] * ng_,
        scratch_shapes=[pltpu.VMEM((L, HD), F32)] * 6 + [pltpu.VMEM((L + 16, HD), F32)] + _gather2_sems(ng_),
        compiler_params=pltpu.CompilerParams(dimension_semantics=("arbitrary",), vmem_limit_bytes=VMEM_LIMIT,
                                             has_side_effects=True),
    )(zx, zc, cw, cb, wa, wx, ba, bx, lam, *[pltpu.with_memory_space_constraint(a, pltpu.HBM) for a in gather])


def _lru_backward(zx, zc, dyl, dz, cw, cb, wa, wx, ba, bx, lam, chip_sums, first_chips=None):
    nr = len(chip_sums)

    def body(xa_ref, xac_ref, dyl_ref, dz_in, cw_ref, cb_ref, wa_ref, wx_ref, ba_ref, bx_ref, lam_ref, *rest):
        (dxa_ref, dxac_ref, dwa_ref, dwx_ref, dba_ref, dbx_ref, dlam_ref, dcw_ref,
         dcb_ref) = rest[nr:nr + 9]
        main_s, ctx_s, pad_s = rest[3 * nr + 9:3 * nr + 12]
        if nr:
            start, forward, finish = _chips_ops(rest[:nr], rest[nr + 9:2 * nr + 9], rest[2 * nr + 9:3 * nr + 9],
                                                *rest[3 * nr + 12:], first_chips=first_chips)
            pl.when(pl.program_id(0) == 0)(start)
            pl.when(pl.program_id(0) == HEADS // 2)(forward)
            pl.when(pl.program_id(0) == HEADS - 1)(finish)
        del dz_in

        @pl.when(pl.program_id(0) == 0)
        def _():
            dba_ref[...] = jnp.zeros_like(dba_ref)
            dbx_ref[...] = jnp.zeros_like(dbx_ref)

        cwv, cbv = cw_ref[...], cb_ref[...]
        lamv = lam_ref[...]
        sp = _softplus(-lamv)
        nsp = (-LRU_C) * sp
        z = jnp.zeros((1, HD), F32)

        def wmat(ref, d):
            return ref[d, 0].astype(BF16)

        def workspace(s):
            return dict(a=(s.at[0], s.at[1]), u=(s.at[2], s.at[3]), h=(s.at[4], s.at[5]), rho=(s.at[6], s.at[7]),
                        saved=(tuple(s.at[8 + k] for k in range(4)), tuple(s.at[12 + k] for k in range(4))),
                        xc=s.at[16])

        def forward(ws, xa, t_len, h0f, h0b):
            xc = _conv(xa, cwv, cbv, pad_s)
            ws["xc"][...] = xc
            for d in (0, 1):
                vals = _gates(xc, wmat(wa_ref, d), wmat(wx_ref, d), _bias_row(ba_ref, d)[0],
                              _bias_row(bx_ref, d)[0], nsp[d:d + 1, :])
                r, i, a, gamma, rg = vals
                ws["a"][d][...] = a
                ws["u"][d][...] = gamma * (i * xc)
                for ref, val in zip(ws["saved"][d], (r, i, gamma, rg)):
                    ref[...] = val
            return _scan_pair(ws["a"][0], ws["u"][0], ws["h"][0], h0f, ws["a"][1], ws["u"][1], ws["h"][1], h0b,
                              t_len)

        def backward(ws, xa, t_len, h0f, h0b, dhf, dhb, first):
            xc = ws["xc"][...]
            (af, ab), (uf, ub), (hf, hb), (rf, rb) = ws["a"], ws["u"], ws["h"], ws["rho"]
            uf[...] = ab[...] * dhb
            ub[...] = af[...] * dhf
            rho_b_last, rho_f_first = _scan_pair(ab, uf, rb, z, af, ub, rf, z, t_len)
            dxc = jnp.zeros((t_len, HD), F32)
            dsp = []
            for d in (0, 1):
                r, i, gamma, rg = (ref[...] for ref in ws["saved"][d])
                a = ws["a"][d][...]
                if d == 0:
                    lam_t = dhf + _shifted(pad_s, rf[...], (1,))[0]
                    h_prev = _shifted(pad_s, hf[...], (-1,), before=h0f)[0]
                else:
                    lam_t = dhb + _shifted(pad_s, rb[...], (-1,))[0]
                    h_prev = _shifted(pad_s, hb[...], (1,), after=h0b)[0]
                da = lam_t * h_prev
                lx = lam_t * xc
                d_i = lx * gamma
                d_gamma = lx * i
                dxc = dxc + lam_t * (gamma * i)
                d_log_a = a * (da - d_gamma * (a * rg))
                dsp.append(jnp.sum(d_log_a * r, axis=0, keepdims=True) * (-LRU_C))
                d_pre_r = d_log_a * nsp[d:d + 1, :] * (r * (1.0 - r))
                d_pre_i = d_i * (i * (1.0 - i))
                prb, pib, xb = d_pre_r.astype(BF16), d_pre_i.astype(BF16), xc.astype(BF16)
                dxc = dxc + _dot_nt(prb, wmat(wa_ref, d)) + _dot_nt(pib, wmat(wx_ref, d))
                g_wa, g_wx = _dot_tn(xb, prb), _dot_tn(xb, pib)
                g_ba = jnp.sum(d_pre_r, axis=0, keepdims=True)
                g_bx = jnp.sum(d_pre_i, axis=0, keepdims=True)
                mask = _bias_row(ba_ref, d)[1]
                dba_ref[...] += jnp.where(mask, g_ba, 0.0)
                dbx_ref[...] += jnp.where(mask, g_bx, 0.0)
                if first:
                    dwa_ref[d, 0] = g_wa
                    dwx_ref[d, 0] = g_wx
                else:
                    dwa_ref[d, 0] += g_wa
                    dwx_ref[d, 0] += g_wx
            g_lam = jnp.concatenate(dsp, axis=0) * (-_sigmoid(-lamv))
            dm1, dp1, dm2 = _shifted(pad_s, dxc, (-1, 1, -2))
            dxa = dp1 * cwv[0:1, :] + dxc * cwv[1:2, :] + dm1 * cwv[2:3, :] + dm2 * cwv[3:4, :]
            xm1, xp1, xp2 = _shifted(pad_s, xa, (-1, 1, 2))
            g_cw = jnp.concatenate([jnp.sum(dxc * v, axis=0, keepdims=True) for v in (xm1, xa, xp1, xp2)], axis=0)
            g_cb = jnp.sum(dxc, axis=0, keepdims=True)
            if first:
                dlam_ref[...] = g_lam
                dcw_ref[...] = g_cw
                dcb_ref[...] = g_cb
            else:
                dlam_ref[...] += g_lam
                dcw_ref[...] += g_cw
                dcb_ref[...] += g_cb
            return dxa, rho_f_first, rho_b_last

        ws_x, ws_c = workspace(main_s), workspace(ctx_s)
        h0f, h0b = forward(ws_c, xac_ref[...], LC, z, z)
        forward(ws_x, xa_ref[...], L, h0f, h0b)
        dh = dyl_ref[...]
        dxa, dh0f, dh0b = backward(ws_x, xa_ref[...], L, h0f, h0b, dh, dh, True)
        dxa_ref[...] = dxa.astype(BF16)
        rc = _rows((LC, HD))
        dxac, _, _ = backward(ws_c, xac_ref[...], LC, z, z, jnp.where(rc == LC - 1, dh0f, 0.0),
                              jnp.where(rc == 0, dh0b, 0.0), False)
        dxac_ref[...] = dxac.astype(BF16)

    s = _lru_param_specs()
    col = lambda r: pl.BlockSpec((r, HD), lambda h: (0, h))
    return _call(
        body, name="lru_backward", grid=(HEADS,),
        out_shape=[jax.ShapeDtypeStruct((L, D_IN), BF16), jax.ShapeDtypeStruct((LC, D), BF16),
                   jax.ShapeDtypeStruct((2, HEADS, HD, HD), F32), jax.ShapeDtypeStruct((2, HEADS, HD, HD), F32),
                   jax.ShapeDtypeStruct((2 * HEADS, HD), F32), jax.ShapeDtypeStruct((2 * HEADS, HD), F32),
                   jax.ShapeDtypeStruct((2, D), F32), jax.ShapeDtypeStruct((CONV_W, D), F32),
                   jax.ShapeDtypeStruct((1, D), F32)] + [jax.ShapeDtypeStruct((4,) + a.shape[1:], a.dtype)
                                                          for a in chip_sums] + _chips_stage_shapes(chip_sums),
        in_specs=[s["xa"], s["xac"], col(L), pl.BlockSpec(memory_space=pl.ANY), s["cw"], s["cb"], s["h4"], s["h4"],
                  s["b16"], s["b16"], s["v2"]] + [HBM] * nr,
        out_specs=[col(L), col(LC), s["h4"], s["h4"], s["b16"], s["b16"], s["v2"], col(CONV_W), col(1)]
        + [HBM] * (2 * nr),
        scratch_shapes=[pltpu.VMEM((17, L, HD), F32), pltpu.VMEM((17, LC, HD), F32), pltpu.VMEM((L + 16, HD), F32)]
        + (_chips_sems(nr) if nr else []),
        input_output_aliases={3: 0},
        compiler_params=pltpu.CompilerParams(dimension_semantics=("arbitrary",), vmem_limit_bytes=VMEM_LIMIT,
                                             has_side_effects=True),
    )(zx, zc, dyl, dz, cw, cb, wa, wx, ba, bx, lam, *[pltpu.with_memory_space_constraint(a, pltpu.HBM)
                                                       for a in chip_sums])


def _mixer_loss(x, tgt, zx, yl, gx, fg, lng, lnb, ws, wst, bst, wout, tm):
    ncht = tm // CHUNK

    def body(x_ref, t_ref, ga_ref, u_ref, v_ref, gb_ref, yl_ref, gx_ref, fg_ref, lng_ref, lnb_ref, ws_ref, wst_ref,
             bst_ref, wout_ref,
             dz_ref, dyl_ref, dxn_ref, y_s, do_ref, dws_ref, dbst_ref, vec_ref,
             vn_s, mix_s, dm_s, dvn_s):
        step = pl.program_id(0)

        @pl.when(step == 0)
        def _():
            dws_ref[...] = jnp.zeros_like(dws_ref)
            dbst_ref[...] = jnp.zeros_like(dbst_ref)
            vec_ref[...] = jnp.zeros_like(vec_ref)

        u, v = u_ref[...], v_ref[...]
        ug, dug_du = _gelu_and_grad(u)
        vg, dvg_dv = _gelu_and_grad(v)
        mu = jnp.mean(vg, axis=-1, keepdims=True)
        vc = vg - mu
        rstd = lax.rsqrt(jnp.mean(vc * vc, axis=-1, keepdims=True) + LN_EPS)
        vhat = vc * rstd
        lngv = lng_ref[...]
        vn_s[...] = (vhat * lngv + lnb_ref[...]).astype(BF16)
        for ch in range(ncht):
            rs = slice(ch * CHUNK, (ch + 1) * CHUNK)
            for g in range(HEADS):
                cs = slice(g * HD, (g + 1) * HD)
                mix_s[rs, cs] = _dot(ws_ref[g], vn_s[rs, cs]) + bst_ref[:, g:g + 1]
        mixed = mix_s[...]
        ga, gb, yl = ga_ref[...], gb_ref[...], yl_ref[...]
        sga, dsga = _silu_and_grad(ga)
        sgb, dsgb = _silu_and_grad(gb)
        ys = ug * mixed
        y_s[:, 0:D] = (yl * sga).astype(BF16)
        y_s[:, D:D_MIX] = (ys * sgb).astype(BF16)
        o = _dot(y_s[...], wout_ref[...])
        gxv, fgv = gx_ref[...], fg_ref[...]
        xn = x_ref[...] + gxv * o
        rs2 = lax.rsqrt(jnp.mean(xn * xn, axis=-1, keepdims=True) + NORM_EPS)
        xh = xn * rs2
        diff = xh * fgv - t_ref[...]
        vec_ref[R_LOSS:R_LOSS + 1, :] += jnp.full((1, D), jnp.sum(diff * diff) * (0.5 / D), F32)
        dout = diff * (1.0 / D)
        w = dout * fgv
        dxn = rs2 * (w - xh * jnp.mean(w * xh, axis=-1, keepdims=True))
        dxn_ref[...] = dxn
        vec_ref[0:1, :] += jnp.sum(dxn * o, axis=0, keepdims=True)
        vec_ref[1:2, :] += jnp.sum(dout * xh, axis=0, keepdims=True)
        dob = (dxn * gxv).astype(BF16)
        do_ref[...] = dob
        dy = _dot_nt(dob, wout_ref[...])
        dya, dyb = dy[:, 0:D], dy[:, D:D_MIX]
        dyl_ref[...] = dya * sga
        dys = dyb * sgb
        dz_ref[:, 0:D] = jnp.zeros((tm, D), BF16)
        dz_ref[:, D:2 * D] = (dya * yl * dsga).astype(BF16)
        dz_ref[:, 2 * D:3 * D] = (dys * mixed * dug_du).astype(BF16)
        dz_ref[:, 4 * D:5 * D] = (dyb * ys * dsgb).astype(BF16)
        dm = dys * ug
        dm_s[...] = dm.astype(BF16)
        for g in range(HEADS):
            cs = slice(g * HD, (g + 1) * HD)
            dbst_ref[:, g:g + 1] += sum(jnp.sum(dm[ch * CHUNK:(ch + 1) * CHUNK, cs], axis=1, keepdims=True)
                                        for ch in range(ncht))
            for ch in range(ncht):
                rs = slice(ch * CHUNK, (ch + 1) * CHUNK)
                dws_ref[g] += _dot_nt(dm_s[rs, cs], vn_s[rs, cs])
                dvn_s[rs, cs] = _dot(wst_ref[g], dm_s[rs, cs])
        dvn = dvn_s[...]
        vec_ref[2:3, :] += jnp.sum(dvn * vhat, axis=0, keepdims=True)
        vec_ref[3:4, :] += jnp.sum(dvn, axis=0, keepdims=True)
        dvh = dvn * lngv
        dvg = rstd * (dvh - jnp.mean(dvh, axis=-1, keepdims=True) - vhat * jnp.mean(dvh * vhat, axis=-1, keepdims=True))
        dz_ref[:, 3 * D:4 * D] = (dvg * dvg_dv).astype(BF16)

    tile = pl.BlockSpec((tm, D), lambda i: (i, 0))
    zcol = lambda n: pl.BlockSpec((tm, D), lambda i: (i, n))
    vec = pl.BlockSpec((1, D), lambda i: (0, 0))
    full = lambda *s: pl.BlockSpec(s, lambda i: (0,) * len(s))
    return _call(
        body, name="mixer_loss", grid=(L // tm,),
        out_shape=[jax.ShapeDtypeStruct((L, D_IN), BF16), jax.ShapeDtypeStruct((L, D), F32),
                   jax.ShapeDtypeStruct((L, D), F32), jax.ShapeDtypeStruct((L, D_MIX), BF16),
                   jax.ShapeDtypeStruct((L, D), BF16),
                   jax.ShapeDtypeStruct((HEADS, CHUNK, CHUNK), F32), jax.ShapeDtypeStruct((CHUNK, HEADS), F32),
                   jax.ShapeDtypeStruct((8, D), F32)],
        in_specs=[tile, tile, zcol(1), zcol(2), zcol(3), zcol(4), tile, pl.BlockSpec((1, D), lambda i: (0, 2)),
                  vec, vec, vec,
                  full(HEADS, CHUNK, CHUNK), full(HEADS, CHUNK, CHUNK), full(CHUNK, HEADS),
                  pl.BlockSpec((D_MIX, D), lambda i: (0, 0), pipeline_mode=pl.Buffered(1))],
        out_specs=[pl.BlockSpec((tm, D_IN), lambda i: (i, 0)), tile, tile,
                   pl.BlockSpec((tm, D_MIX), lambda i: (i, 0)), tile,
                   full(HEADS, CHUNK, CHUNK), full(CHUNK, HEADS), full(8, D)],
        scratch_shapes=[pltpu.VMEM((tm, D), BF16), pltpu.VMEM((tm, D), F32),
                        pltpu.VMEM((tm, D), BF16), pltpu.VMEM((tm, D), F32)],
        compiler_params=_params("arbitrary"),
    )(x, tgt, zx, zx, zx, zx, yl, gx, fg, lng, lnb, ws, wst, bst, wout)


def _grad_w(a, b, a2, b2, tk, name, bw, first, nblocks, split):
    nk = a.shape[0] // tk
    m = a.shape[1]
    with_ctx = a2 is not None
    if split == "cols":
        slots, r, w = nblocks, m, bw // 2
        piece = lambda q, pc: (slice(None), slice(pc * w, (pc + 1) * w))
    else:
        slots, r, w = 4, m // 8, bw
        piece = lambda q, pc: (slice((2 * q + pc) * r, (2 * q + pc + 1) * r), slice(None))

    def body(*refs):
        a_ref, b_ref = refs[:2]
        a2_ref, b2_ref = refs[2:4] if with_ctx else (None, None)
        sums_ref, acc, mine_v, send_v, stage_v, send_sems, recv_sems = refs[4 if with_ctx else 2:]
        n, k = pl.program_id(0), pl.program_id(1)
        x, y, c = lax.axis_index("x"), lax.axis_index("y"), lax.axis_index("c")

        def to_sibling(s):
            return pltpu.make_async_remote_copy(src_ref=send_v.at[s], dst_ref=stage_v.at[s], send_sem=send_sems.at[s],
                                                recv_sem=recv_sems.at[s], device_id=(x, y, 1 - c),
                                                device_id_type=MESH)

        @pl.when(k == 0)
        def _():
            acc[...] = jnp.zeros_like(acc)

        acc[...] += _dot_tn(a_ref[...], b_ref[...])

        if with_ctx:
            @pl.when(jnp.logical_and(k == nk - 1, n == 0))
            def _():
                acc[:, 0:b2_ref.shape[1]] += _dot_tn(a2_ref[...], b2_ref[...])

        def hand_over(s, q):
            for pc in (0, 1):
                @pl.when(c == pc)
                def _(pc=pc):
                    mine_v[s] = acc[piece(q, pc)]
                    send_v[s] = acc[piece(q, 1 - pc)].astype(BF16)
            to_sibling(s).start()

        for i in range(nblocks):
            @pl.when(jnp.logical_and(k == nk - 1, n == i))
            def _(i=i):
                if split == "cols":
                    hand_over(i, 0)
                else:
                    for q in range(4):
                        hand_over(q, q)

        @pl.when(jnp.logical_and(k == nk - 1, n == nblocks - 1))
        def _():
            for s in range(slots):
                to_sibling(s).wait_recv()
                sums_ref[s] = (mine_v[s] + stage_v[s].astype(F32)).astype(BF16)
            for s in range(slots):
                to_sibling(s).wait_send()

    in_specs = [pl.BlockSpec((tk, m), lambda n, k: (k, 0)), pl.BlockSpec((tk, bw), lambda n, k: (k, n + first))]
    args = [a, b]
    if with_ctx:
        in_specs += [pl.BlockSpec(a2.shape, lambda n, k: (0, 0)), pl.BlockSpec(b2.shape, lambda n, k: (0, 0))]
        args += [a2, b2]
    return _call(
        body, name=name, grid=(nblocks, nk), out_shape=jax.ShapeDtypeStruct((slots, r, w), BF16),
        in_specs=in_specs, out_specs=pl.BlockSpec((slots, r, w), lambda n, k: (0, 0, 0)),
        scratch_shapes=[pltpu.VMEM((m, bw), F32), pltpu.VMEM((slots, r, w), F32), pltpu.VMEM((slots, r, w), BF16),
                        pltpu.VMEM((slots, r, w), BF16), pltpu.SemaphoreType.DMA((slots,)),
                        pltpu.SemaphoreType.DMA((slots,))],
        compiler_params=pltpu.CompilerParams(dimension_semantics=("arbitrary", "arbitrary"),
                                             vmem_limit_bytes=VMEM_LIMIT, has_side_effects=True),
    )(*args)


def _grad_rows(xr, dz, w, mod, ng, dres, ncols, tm, name, chip_sums=(), first_chips=None, dests=None):
    rows = xr.shape[0]
    steps = rows // tm
    with_dx = dres is not None
    nr = len(chip_sums)
    dests = [d for d in (dests or [None] * nr)]
    nd = sum(d is not None for d in dests)
    nin = 6 if with_dx else 5
    nout = 2 if with_dx else 1

    def body(*refs):
        if with_dx:
            x_ref, dz_ref, w_ref, sc_ref, ng_ref, dres_ref = refs[:nin]
            dx_ref, vec_ref = refs[nin + nr + nd:nin + nr + nd + nout]
        else:
            x_ref, dz_ref, w_ref, sc_ref, ng_ref = refs[:nin]
            (vec_ref,) = refs[nin + nr + nd:nin + nr + nd + nout]
        if nr:
            o0 = nin + nr + nd + nout
            start, forward, finish = _chips_ops(refs[nin:nin + nr], refs[o0:o0 + nr], refs[o0 + nr:o0 + 2 * nr],
                                                *refs[o0 + 2 * nr:], first_chips=first_chips)
            pl.when(pl.program_id(0) == 0)(start)
            pl.when(pl.program_id(0) == steps // 2)(forward)
            pl.when(pl.program_id(0) == steps - 1)(finish)

        @pl.when(pl.program_id(0) == 0)
        def _():
            vec_ref[...] = jnp.zeros_like(vec_ref)

        dhn = _dot_nt(dz_ref[...], w_ref[...])
        x = x_ref[...]
        rs = lax.rsqrt(jnp.mean(x * x, axis=-1, keepdims=True) + NORM_EPS)
        xh = x * rs
        ngv = ng_ref[...]
        y = xh * ngv
        vec_ref[0:1, :] += jnp.sum(dhn, axis=0, keepdims=True)
        vec_ref[1:2, :] += jnp.sum(dhn * y, axis=0, keepdims=True)
        dy = dhn * (1.0 + sc_ref[...])
        vec_ref[2:3, :] += jnp.sum(dy * xh, axis=0, keepdims=True)
        if with_dx:
            dxh = dy * ngv
            dx_ref[...] = dres_ref[...] + rs * (dxh - xh * jnp.mean(dxh * xh, axis=-1, keepdims=True))

    tile = pl.BlockSpec((tm, D), lambda i: (i, 0))
    vec = pl.BlockSpec((1, D), lambda i: (0, 0))
    in_specs = [tile, pl.BlockSpec((tm, ncols), lambda i: (i, 0)),
                pl.BlockSpec((D, ncols), lambda i: (0, 0), pipeline_mode=pl.Buffered(1)),
                pl.BlockSpec((1, D), lambda i: (0, 1)), vec]
    out_shape = [jax.ShapeDtypeStruct((8, D), F32)]
    out_specs = [pl.BlockSpec((8, D), lambda i: (0, 0))]
    args = [xr, dz, w, mod, ng]
    if with_dx:
        in_specs.append(tile)
        out_shape.insert(0, jax.ShapeDtypeStruct((rows, D), F32))
        out_specs.insert(0, tile)
        args.append(dres)
    aliases = {}
    for j, d in enumerate(dests):
        if d is not None:
            aliases[len(args) + nr + len(aliases)] = len(out_shape) + j
    in_specs += [HBM] * (nr + nd)
    out_specs += [HBM] * (2 * nr)
    out_shape += [jax.ShapeDtypeStruct((4,) + a.shape[1:], a.dtype) for a in chip_sums]
    out_shape += _chips_stage_shapes(chip_sums)
    args += [pltpu.with_memory_space_constraint(a, pltpu.HBM) for a in chip_sums]
    args += [pltpu.with_memory_space_constraint(d, pltpu.HBM) for d in dests if d is not None]
    return _call(body, name=name, grid=(steps,), out_shape=out_shape, in_specs=in_specs, out_specs=out_specs,
                 scratch_shapes=_chips_sems(nr) if nr else [], input_output_aliases=aliases,
                 compiler_params=pltpu.CompilerParams(dimension_semantics=("arbitrary",),
                                                      vmem_limit_bytes=VMEM_LIMIT, has_side_effects=bool(nr)))(*args)


def _adamw(w, g, m, v):
    m = ADAM_B1 * m + (1.0 - ADAM_B1) * g
    v = ADAM_B2 * v + (1.0 - ADAM_B2) * (g * g)
    m_hat = m / (1.0 - ADAM_B1 ** ADAM_STEP)
    v_hat = v / (1.0 - ADAM_B2 ** ADAM_STEP)
    delta = -ADAM_LR * (m_hat / (jnp.sqrt(v_hat) + ADAM_EPS) + ADAM_WD * w)
    return delta, m, v


def _adamw_reduced(parts, w, m, v, tr, name):
    r, n = w.shape
    nparts = parts.shape[0]

    def body(p_ref, w_ref, m_ref, v_ref, g_ref, d_ref, mo_ref, vo_ref):
        g = p_ref[0].astype(F32)
        for i in range(1, nparts):
            g = g + p_ref[i].astype(F32)
        g_ref[...] = g
        d_ref[...], mo_ref[...], vo_ref[...] = _adamw(w_ref[...], g, m_ref[...], v_ref[...])

    tile = pl.BlockSpec((tr, n), lambda i: (i, 0))
    sds = jax.ShapeDtypeStruct((r, n), F32)
    return _call(
        body, name=name, grid=(r // tr,), out_shape=[sds] * 4,
        in_specs=[pl.BlockSpec((nparts, tr, n), lambda i: (0, i, 0)), tile, tile, tile], out_specs=[tile] * 4,
        compiler_params=_params("arbitrary"),
    )(parts, w, m, v)


R_GATE, R_FINAL_G, R_LN_G, R_LN_B, R_LOSS = 0, 1, 2, 3, 4
R_SH_X, R_SC_X, R_NG_X = 5, 6, 7
R_SH_C, R_SC_C, R_NG_C = 8, 9, 10
R_LAM, R_CW, R_CB = 11, 13, 17
PACK_ROWS = 24
Q_BA, Q_BX, Q_SGU_B, PACK128_ROWS = 0, 16, 32, 40


def _reduce_small(vp_all, vq_all, mat_parts, ada_w, me):
    nloc = ada_w.shape[1]
    nm = len(mat_parts)

    def body(me_ref, vp_ref, vq_ref, *refs):
        mp_refs, w_ref = refs[:nm], refs[nm]
        red_ref, redq_ref = refs[nm + 1:nm + 3]
        mat_refs = refs[nm + 3:2 * nm + 3]
        dmod_ref, gab_ref, cpart_ref, dmc_s = refs[2 * nm + 3:]
        red, redq = vp_ref[0], vq_ref[0]
        for i in range(1, N_DEV):
            red = red + vp_ref[i]
            redq = redq + vq_ref[i]
        red_ref[...] = red
        redq_ref[...] = redq
        for mp_ref, mat_ref in zip(mp_refs, mat_refs):
            mat = mp_ref[0].astype(F32)
            for i in range(1, mp_ref.shape[0]):
                mat = mat + mp_ref[i].astype(F32)
            mat_ref[...] = mat
        for e in range(N_DEV):
            dmod_ref[e:e + 1, 0:D] = vp_ref[e, R_SH_X:R_SH_X + 1, :]
            dmod_ref[e:e + 1, D:2 * D] = vp_ref[e, R_SC_X:R_SC_X + 1, :]
            dmod_ref[e:e + 1, 2 * D:3 * D] = vp_ref[e, R_GATE:R_GATE + 1, :]
        dmod_ref[8:9, 0:D] = red[R_SH_C:R_SH_C + 1, :]
        dmod_ref[8:9, D:2 * D] = red[R_SC_C:R_SC_C + 1, :]
        dmod_ref[8:9, 2 * D:3 * D] = jnp.zeros((1, D), F32)
        dmod_ref[9:16, :] = jnp.zeros((7, 3 * D), F32)
        gab_ref[:, 0:D] = red[R_SH_X:R_SH_X + 1, :] + red[R_SH_C:R_SH_C + 1, :]
        gab_ref[:, D:2 * D] = red[R_SC_X:R_SC_X + 1, :] + red[R_SC_C:R_SC_C + 1, :]
        gab_ref[:, 2 * D:3 * D] = red[R_GATE:R_GATE + 1, :]
        dmc_s[...] = jnp.broadcast_to(dmod_ref[8:9, :], (8, 3 * D))
        off = pl.multiple_of(me_ref[0] * nloc, 128)
        cpart_ref[...] = _dot_nt(dmc_s[:, pl.ds(off, nloc)], w_ref[...])

    return _call(
        body, name="reduce_small",
        out_shape=[jax.ShapeDtypeStruct((PACK_ROWS, D), F32), jax.ShapeDtypeStruct((PACK128_ROWS, HD), F32)]
        + [jax.ShapeDtypeStruct(p.shape[1:], F32) for p in mat_parts]
        + [jax.ShapeDtypeStruct((16, 3 * D), F32), jax.ShapeDtypeStruct((1, 3 * D), F32),
           jax.ShapeDtypeStruct((8, D), F32)],
        in_specs=[pl.BlockSpec(memory_space=pltpu.SMEM)] + [VMEM] * (nm + 3), out_specs=[VMEM] * (nm + 5),
        scratch_shapes=[pltpu.VMEM((8, 3 * D), F32)], compiler_params=_params(),
    )(me, vp_all, vq_all, *mat_parts, ada_w)


def _adamw_ada(c_all, c_ctx, dmod, w, m, v, me):
    nloc = w.shape[1]

    def body(me_ref, c_ref, cc_ref, dm_ref, w_ref, m_ref, v_ref, g_ref, d_ref, mo_ref, vo_ref):
        off = pl.multiple_of(me_ref[0] * nloc, 128)
        dm = dm_ref[:, pl.ds(off, nloc)]
        sx, _ = _silu_and_grad(c_ref[...])
        sc, _ = _silu_and_grad(cc_ref[...])
        g = _dot_tn(sx, dm[0:8, :]) + _dot_tn(jnp.broadcast_to(sc, (8, D)), dm[8:16, :])
        g_ref[...] = g
        d_ref[...], mo_ref[...], vo_ref[...] = _adamw(w_ref[...], g, m_ref[...], v_ref[...])

    sds = jax.ShapeDtypeStruct(w.shape, F32)
    return _call(
        body, name="adamw_ada_w", out_shape=[sds] * 4,
        in_specs=[pl.BlockSpec(memory_space=pltpu.SMEM)] + [VMEM] * 6, out_specs=[VMEM] * 4,
        compiler_params=_params(),
    )(me, c_all, c_ctx, dmod, w, m, v)


_SMALL = ("c_ctx", "ada_b", "norm_g", "conv_w", "conv_b", "lru_wa", "lru_ba", "lru_wx", "lru_bx", "lru_lambda",
          "sgu_ln_g", "sgu_ln_b", "sgu_w", "sgu_b", "final_g")


def _adamw_small(red, redq, mats, cparts, gab, ws, ms, vs, me):
    n = len(_SMALL)

    def body(me_ref, red_ref, redq_ref, wa_ref, wx_ref, sw_ref, cp_ref, gab_ref, *refs):
        w_refs, m_refs, v_refs = refs[:n], refs[n:2 * n], refs[2 * n:3 * n]
        outs = refs[3 * n:]
        off = pl.multiple_of(me_ref[0] * HD, 128)

        def row(r, k=1):
            return red_ref[r:r + k, :]

        cc = w_refs[0][...]
        dcc = cp_ref[0, 0:1, :]
        for i in range(1, N_DEV):
            dcc = dcc + cp_ref[i, 0:1, :]
        grads = dict(
            c_ctx=dcc * _silu_and_grad(cc)[1], ada_b=gab_ref[...], norm_g=row(R_NG_X) + row(R_NG_C),
            conv_w=red_ref[R_CW:R_CW + CONV_W, pl.ds(off, HD)], conv_b=row(R_CB),
            lru_wa=wa_ref[...], lru_ba=redq_ref[Q_BA:Q_BA + 2 * HEADS, :], lru_wx=wx_ref[...],
            lru_bx=redq_ref[Q_BX:Q_BX + 2 * HEADS, :], lru_lambda=red_ref[R_LAM:R_LAM + 2, pl.ds(off, HD)],
            sgu_ln_g=row(R_LN_G), sgu_ln_b=row(R_LN_B), sgu_w=sw_ref[...],
            sgu_b=redq_ref[Q_SGU_B:Q_SGU_B + HEADS, :], final_g=row(R_FINAL_G))
        for j, name in enumerate(_SMALL):
            g = grads[name]
            outs[j][...] = g
            outs[n + j][...], outs[2 * n + j][...], outs[3 * n + j][...] = _adamw(w_refs[j][...], g, m_refs[j][...],
                                                                                 v_refs[j][...])

    sds = [jax.ShapeDtypeStruct(ws[k].shape, F32) for k in _SMALL]
    outs = _call(
        body, name="adamw_small", out_shape=sds * 4,
        in_specs=[pl.BlockSpec(memory_space=pltpu.SMEM)] + [VMEM] * (7 + 3 * n), out_specs=[VMEM] * (4 * n),
        compiler_params=_params(),
    )(me, red, redq, *mats, cparts, gab, *[ws[k] for k in _SMALL], *[ms[k] for k in _SMALL],
      *[vs[k] for k in _SMALL])
    return [dict(zip(_SMALL, outs[i * n:(i + 1) * n])) for i in range(4)]


def kernel(x, c, ctx, c_ctx, ada_w, ada_b, norm_g, w_in, conv_w, conv_b, lru_wa, lru_ba, lru_wx, lru_bx, lru_lambda, sgu_ln_g, sgu_ln_b, sgu_w, sgu_b, w_out, final_g, loss_target, m_c_ctx, m_ada_w, m_ada_b, m_norm_g, m_w_in, m_conv_w, m_conv_b, m_lru_wa, m_lru_ba, m_lru_wx, m_lru_bx, m_lru_lambda, m_sgu_ln_g, m_sgu_ln_b, m_sgu_w, m_sgu_b, m_w_out, m_final_g, v_c_ctx, v_ada_w, v_ada_b, v_norm_g, v_w_in, v_conv_w, v_conv_b, v_lru_wa, v_lru_ba, v_lru_wx, v_lru_bx, v_lru_lambda, v_sgu_ln_g, v_sgu_ln_b, v_sgu_w, v_sgu_b, v_w_out, v_final_g):
    args = dict(locals())
    me_s = 4 * lax.axis_index("x") + 2 * lax.axis_index("y") + lax.axis_index("c")
    me = me_s.astype(jnp.int32).reshape(1)
    xr, ctxr, tgt = x[0], ctx[0], loss_target[0]
    cc = c_ctx.reshape(1, D)
    nw = 2 * HEADS * HD
    view = dict(c_ctx=(1, D), ada_b=(1, 3 * D), norm_g=(1, D), conv_w=(CONV_W, HD), conv_b=(1, D), lru_wa=(nw, HD),
                lru_ba=(2 * HEADS, HD), lru_wx=(nw, HD), lru_bx=(2 * HEADS, HD), lru_lambda=(2, HD), sgu_ln_g=(1, D),
                sgu_ln_b=(1, D), sgu_w=(HEADS * CHUNK, CHUNK), sgu_b=(HEADS, CHUNK), final_g=(1, D))

    zx, hn, w_full, w_out_b, modx, modc, c_all, cw_full, lam_full = _front_project(
        xr, c, cc, ada_w[0], ada_b, norm_g, w_in[0], w_out[0], conv_w[0], lru_lambda[0], me)
    zc, hnc = _project(ctxr, modc, norm_g, w_full, D, LC, "project_ctx")
    ba, bx = lru_ba.reshape(view["lru_ba"]), lru_bx.reshape(view["lru_bx"])
    yl, wout_all = _lru_forward(zx, zc, cw_full, conv_b, lru_wa[0], lru_wx[0], ba, bx, lam_full, [w_out_b], ["ag"])
    wout_full = wout_all.reshape(D_MIX, D)
    ws_b = sgu_w[0].astype(BF16)
    dz, dyl, dxn, ycat, dob, dws, dbst, mvec = _mixer_loss(
        xr, tgt, zx, yl, modx, final_g.reshape(1, D), sgu_ln_g, sgu_ln_b, ws_b, jnp.swapaxes(ws_b, 1, 2),
        sgu_b[0].T, wout_full, 256)

    wout_sums = _grad_w(ycat, dob, None, None, 1024, "grad_w_out", D, 0, 1, "rows")
    rest_sums = _grad_w(hn, dz, None, None, 1024, "grad_w_in_rest", 2 * W_IN_SHARD, 1, 3, "cols")
    dz, dxac, dwa, dwx, dba, dbx, dlam, dcw, dcb, win_parts, wout_parts, _, _ = _lru_backward(
        zx, zc, dyl, dz, cw_full, conv_b, lru_wa[0], lru_wx[0], ba, bx, lam_full, [rest_sums, wout_sums],
        first_chips=[1, 0])
    first_sums = _grad_w(hn, dz, hnc, dxac, 1024, "grad_w_in_first", 2 * W_IN_SHARD, 0, 1, "cols")
    mats = [dwa.reshape(N_DEV, nw // N_DEV, HD), dwx.reshape(N_DEV, nw // N_DEV, HD), dws]
    mat_sums = _reduce2_local(mats, ["a2a"] * 3, me, "reduce_mat", out_dtype=BF16)
    gx, xvec, win_parts, *mat_parts = _grad_rows(
        xr, dz, w_full, modx, norm_g, dxn, D_IN, 256, "grad_rows_x", chip_sums=[first_sums, *mat_sums],
        first_chips=[0, 0, 0, 0], dests=[win_parts, None, None, None])[:6]
    (cvec,) = _grad_rows(ctxr, dxac, w_full, modc, norm_g, None, D, LC, "grad_rows_ctx")
    pack = jnp.concatenate([mvec[0:5], xvec[0:3], cvec[0:3], dlam, dcw, dcb,
                            jnp.zeros((PACK_ROWS - R_CB - 1, D), F32)], axis=0)
    pack128 = jnp.concatenate([dba, dbx, dbst.T], axis=0)
    vp_all, vq_all = _gather2([pack, pack128], ["ag", "ag"], "gather_pack")
    red, redq, *rest = _reduce_small(vp_all, vq_all, mat_parts, ada_w[0], me)
    mat_pieces, (dmod, gab, cpart) = rest[:3], rest[3:]
    *mats_all, cparts = _gather2([*mat_pieces, cpart], ["ag"] * 4, "gather_small")

    g_w_in, d_w_in, nm_w_in, nv_w_in = _adamw_reduced(win_parts, w_in[0], m_w_in[0], v_w_in[0], 256, "adamw_w_in")
    g_w_out, d_w_out, nm_w_out, nv_w_out = _adamw_reduced(wout_parts, w_out[0], m_w_out[0], v_w_out[0], 128,
                                                          "adamw_w_out")
    g_ada, d_ada, nm_ada, nv_ada = _adamw_ada(c_all, cc, dmod, ada_w[0], m_ada_w[0], v_ada_w[0], me)
    ws = {k: args[k].reshape(view[k]) for k in _SMALL}
    ms = {k: args["m_" + k].reshape(view[k]) for k in _SMALL}
    vs = {k: args["v_" + k].reshape(view[k]) for k in _SMALL}
    small = _adamw_small(red, redq, [m.reshape(-1, HD) for m in mats_all], cparts, gab, ws, ms, vs, me)
    big = dict(w_in=(g_w_in, d_w_in, nm_w_in, nv_w_in), w_out=(g_w_out, d_w_out, nm_w_out, nv_w_out),
               ada_w=(g_ada, d_ada, nm_ada, nv_ada))

    loss = red[R_LOSS, 0]
    names = ("c_ctx", "ada_w", "ada_b", "norm_g", "w_in", "conv_w", "conv_b", "lru_wa", "lru_ba", "lru_wx", "lru_bx",
             "lru_lambda", "sgu_ln_g", "sgu_ln_b", "sgu_w", "sgu_b", "w_out", "final_g")
    outs = [loss, gx.reshape(x.shape)]
    for kind in range(4):
        for k in names:
            val = big[k][kind] if k in big else small[kind][k]
            outs.append(val.reshape(args[k].shape))
    return tuple(outs)
```

```python
import functools

import jax
import jax.numpy as jnp
from jax import lax
from jax.experimental import pallas as pl
from jax.experimental.pallas import tpu as pltpu

F32 = jnp.float32
BF16 = jnp.bfloat16

N_DEV = 8
D = 1024
L = 2048
LC = 256
HEADS = 8
HD = 128
CHUNK = 128
D_IN = 5 * D
W_IN_SHARD = D_IN // N_DEV
ROWS = 256
D_MIX = 2 * D
CONV_W = 4
LRU_C = 8.0
NORM_EPS = 1e-6
LN_EPS = 1e-5
ADAM_LR, ADAM_B1, ADAM_B2, ADAM_EPS, ADAM_WD, ADAM_STEP = 0.001, 0.9, 0.999, 1e-08, 0.01, 10

VMEM_LIMIT = 56 * 1024 * 1024

HBM = pl.BlockSpec(memory_space=pltpu.HBM)
VMEM = pl.BlockSpec(memory_space=pltpu.VMEM)
MESH = pl.DeviceIdType.MESH


def _call(body, **kw):
    return pl.pallas_call(body, **kw)


def _params(*sem):
    return pltpu.CompilerParams(dimension_semantics=sem, vmem_limit_bytes=VMEM_LIMIT)


def _sigmoid(x):
    return 0.5 * jnp.tanh(0.5 * x) + 0.5


def _silu_and_grad(x):
    s = _sigmoid(x)
    return x * s, s * (1.0 + x * (1.0 - s))


_G0 = 0.7978845608028654
_G1 = 0.044715


def _gelu_and_grad(x):
    x2 = x * x
    t = jnp.tanh(_G0 * (x + _G1 * x * x2))
    cdf = 0.5 * (1.0 + t)
    return x * cdf, cdf + 0.5 * x * (1.0 - t * t) * (_G0 * (1.0 + 3.0 * _G1 * x2))


def _gelu(x):
    return 0.5 * x * (1.0 + jnp.tanh(_G0 * (x + _G1 * x * x * x)))


def _softplus(z):
    t = jnp.exp(-jnp.abs(z))
    u = 1.0 + t
    log1p = jnp.where(u == 1.0, t, jnp.log(u) * t / jnp.where(u == 1.0, 1.0, u - 1.0))
    return jnp.maximum(z, 0.0) + log1p


def _dot(a, b):
    return jnp.dot(a, b, preferred_element_type=F32)


def _dot_nt(a, b):
    return lax.dot_general(a, b, (((1,), (1,)), ((), ())), preferred_element_type=F32)


def _dot_tn(a, b):
    return lax.dot_general(a, b, (((0,), (0,)), ((), ())), preferred_element_type=F32)


def _rows(shape):
    return lax.broadcasted_iota(jnp.int32, shape, 0)


def _shift_down(x, first):
    y = pltpu.roll(x, 1, 0)
    head = jnp.where(_rows((8, x.shape[1])) == 0, first, y[0:8])
    return jnp.concatenate([head, y[8:]], axis=0)


def _shift_up(x, last):
    n = x.shape[0]
    y = pltpu.roll(x, n - 1, 0)
    tail = jnp.where(_rows((8, x.shape[1])) == 7, last, y[n - 8:])
    return jnp.concatenate([y[:n - 8], tail], axis=0)


def _gather2(arrays, modes, name):
    n = len(arrays)

    def body(*refs):
        start, forward, finish = _gather2_ops(refs[:n], refs[n:2 * n], modes, *refs[2 * n:])
        start()
        forward()
        finish()

    return _call(
        body, name=name, out_shape=_gather2_shapes(arrays, modes), in_specs=[HBM] * n, out_specs=[HBM] * n,
        scratch_shapes=_gather2_sems(n), compiler_params=pltpu.CompilerParams(has_side_effects=True),
    )(*[pltpu.with_memory_space_constraint(a, pltpu.HBM) for a in arrays])


def _gather2_shapes(arrays, modes):
    return [jax.ShapeDtypeStruct((N_DEV,) + a.shape if m == "ag" else (a.shape[0], N_DEV * a.shape[1]), a.dtype)
            for a, m in zip(arrays, modes)]


def _gather2_sems(n):
    return [pltpu.SemaphoreType.DMA((n, N_DEV - 1)), pltpu.SemaphoreType.DMA((n, N_DEV - 1)),
            pltpu.SemaphoreType.DMA((n,))]


def _gather2_ops(ins, outs, modes, send_sems, recv_sems, local_sems):
    n = len(ins)
    x, y, c = lax.axis_index("x"), lax.axis_index("y"), lax.axis_index("c")
    me, sibling = (x, y, c), (x, y, 1 - c)
    chips = [(x ^ (k >> 1), y ^ (k & 1)) for k in (1, 2, 3)]

    def slot(j, px, py, pc):
        dev = 4 * px + 2 * py + pc
        if modes[j] == "agc":
            w = ins[j].shape[1]
            return outs[j].at[:, pl.ds(pl.multiple_of(dev * w, 128), w)]
        return outs[j].at[dev]

    def copy(j, k, block, to, src=None):
        return pltpu.make_async_remote_copy(
            src_ref=slot(j, *block) if src is None else src, dst_ref=slot(j, *block),
            send_sem=send_sems.at[j, k], recv_sem=recv_sems.at[j, k], device_id=to, device_id_type=MESH)

    def own(j):
        return pltpu.make_async_copy(ins[j], slot(j, *me), local_sems.at[j])

    def first(j):
        return [copy(j, 0, me, sibling, src=ins[j])] + [copy(j, 1 + i, me, (*chip, c), src=ins[j])
                                                        for i, chip in enumerate(chips)]

    def passed(j, i):
        return copy(j, 4 + i, (*chips[i], c), sibling)

    def start():
        for j in range(n):
            own(j).start()
            for cp in first(j):
                cp.start()

    def forward():
        for i, chip in enumerate(chips):
            for j in range(n):
                copy(j, 1 + i, (*chip, c), me).wait_recv()
                passed(j, i).start()

    def finish():
        for j in range(n):
            copy(j, 0, sibling, me).wait_recv()
            for i, chip in enumerate(chips):
                copy(j, 4 + i, (*chip, 1 - c), me).wait_recv()
            for cp in first(j) + [passed(j, i) for i in range(3)]:
                cp.wait_send()
            own(j).wait()

    return start, forward, finish


def _reduce2_local(arrays, modes, me, name, counts=None, out_dtype=None):
    n = len(arrays)
    counts = counts or [4] * n
    shapes = [(a.shape[1], a.shape[2]) if m == "a2a" else (a.shape[0], a.shape[1] // (2 * cnt))
              for a, m, cnt in zip(arrays, modes, counts)]
    staged = [jax.ShapeDtypeStruct((cnt,) + s, a.dtype) for s, a, cnt in zip(shapes, arrays, counts)]

    def piece(ref, mode, dev, w):
        return ref.at[dev] if mode == "a2a" else ref.at[:, pl.ds(pl.multiple_of(dev * w, 128), w)]

    def to_sibling(*refs):
        ins, outs = refs[:n], refs[n:2 * n]
        send_sems, recv_sems = refs[2 * n:]
        x, y, c = lax.axis_index("x"), lax.axis_index("y"), lax.axis_index("c")
        copies = []
        for j in range(n):
            for q in range(counts[j]):
                cp = pltpu.make_async_remote_copy(
                    src_ref=piece(ins[j], modes[j], 2 * q + (1 - c), shapes[j][1]), dst_ref=outs[j].at[q],
                    send_sem=send_sems.at[j, q], recv_sem=recv_sems.at[j, q], device_id=(x, y, 1 - c),
                    device_id_type=MESH)
                cp.start()
                copies.append(cp)
        for cp in copies:
            cp.wait()

    stage = _call(
        to_sibling, name=name + "_d2d", out_shape=staged, in_specs=[HBM] * n, out_specs=[HBM] * n,
        scratch_shapes=[pltpu.SemaphoreType.DMA((n, 4)), pltpu.SemaphoreType.DMA((n, 4))],
        compiler_params=pltpu.CompilerParams(has_side_effects=True),
    )(*[pltpu.with_memory_space_constraint(a, pltpu.HBM) for a in arrays])

    def add(me_ref, *refs):
        del me_ref
        own, got, outs = refs[:n], refs[n:2 * n], refs[2 * n:]
        for j in range(n):
            mine = own[j][0] if modes[j] == "a2a" else own[j][...]
            outs[j][0] = (mine.astype(F32) + got[j][0].astype(F32)).astype(outs[j].dtype)

    in_specs, slot_specs = [], []
    for (r, w), m, cnt in zip(shapes, modes, counts):
        if m == "a2a":
            in_specs.append(pl.BlockSpec(
                (1, r, w), lambda q, me_ref, cnt=cnt: (2 * jnp.minimum(q, cnt - 1) + me_ref[0] % 2, 0, 0)))
        else:
            in_specs.append(pl.BlockSpec(
                (r, w), lambda q, me_ref, cnt=cnt: (0, 2 * jnp.minimum(q, cnt - 1) + me_ref[0] % 2)))
        slot_specs.append(pl.BlockSpec((1, r, w), lambda q, me_ref, cnt=cnt: (jnp.minimum(q, cnt - 1), 0, 0)))
    return _call(
        add, name=name + "_add",
        out_shape=[jax.ShapeDtypeStruct(s.shape, out_dtype or s.dtype) for s in staged],
        grid_spec=pltpu.PrefetchScalarGridSpec(num_scalar_prefetch=1, grid=(max(counts),),
                                               in_specs=in_specs + slot_specs, out_specs=slot_specs),
        compiler_params=_params("arbitrary"),
    )(me, *arrays, *stage)


def _chips_sems(n):
    return [pltpu.SemaphoreType.DMA((n, 6)), pltpu.SemaphoreType.DMA((n, 6)), pltpu.SemaphoreType.DMA((n,))]


def _chips_stage_shapes(chip_sums):
    return [jax.ShapeDtypeStruct((2, a.shape[1] // 2, a.shape[2]), a.dtype) for a in chip_sums]


def _chips_ops(ins, outs, stages, send_sems, recv_sems, local_sems, first_chips=None):
    x, y, c = lax.axis_index("x"), lax.axis_index("y"), lax.axis_index("c")
    qm = 2 * x + y
    first_chips = first_chips or [0] * len(ins)

    def owns(j, chip):
        lo, cnt = first_chips[j], ins[j].shape[0]
        if lo == 0 and cnt == 4:
            return None
        return jnp.logical_and(chip >= lo, chip < lo + cnt)

    def guarded(cond, fn):
        if cond is None:
            fn()
        else:
            pl.when(cond)(fn)

    def slot(j, chip):
        return jnp.clip(chip - first_chips[j], 0, ins[j].shape[0] - 1)

    def half(j, i):
        h = ins[j].shape[1] // 2
        return pl.ds(i * h, h)

    def copy(j, sem, src, dst, k):
        return pltpu.make_async_remote_copy(
            src_ref=src, dst_ref=dst, send_sem=send_sems.at[j, sem], recv_sem=recv_sems.at[j, sem],
            device_id=(x ^ (k >> 1), y ^ (k & 1), c), device_id_type=MESH)

    def direct(j, k):
        return copy(j, k - 1, ins[j].at[slot(j, qm ^ k)], outs[j].at[qm], k)

    def first_hop(j, k):
        return copy(j, 1 + k, ins[j].at[slot(j, qm ^ 3), half(j, k - 1)], stages[j].at[k - 1], k)

    def second_hop(j, k):
        return copy(j, 3 + k, stages[j].at[2 - k], outs[j].at[qm ^ (3 - k), half(j, 2 - k)], k)

    def local(j):
        return pltpu.make_async_copy(ins[j].at[slot(j, qm)], outs[j].at[qm], local_sems.at[j])

    def start():
        for j in range(len(ins)):
            for k in (1, 2):
                guarded(owns(j, qm ^ 3), lambda j=j, k=k: first_hop(j, k).start())
        for j in range(len(ins)):
            for k in (1, 2):
                guarded(owns(j, qm ^ k), lambda j=j, k=k: direct(j, k).start())
            guarded(owns(j, qm), lambda j=j: local(j).start())

    def forward():
        for j in range(len(ins)):
            for k in (1, 2):
                def pass_on(j=j, k=k):
                    first_hop(j, 3 - k).wait_recv()
                    second_hop(j, k).start()
                guarded(owns(j, qm ^ k), pass_on)

    def finish():
        for j in range(len(ins)):
            for k in (1, 2):
                guarded(owns(j, qm ^ k), lambda j=j, k=k: direct(j, k).wait_send())
                guarded(owns(j, qm ^ k), lambda j=j, k=k: second_hop(j, k).wait_send())
                guarded(owns(j, qm ^ 3), lambda j=j, k=k: first_hop(j, k).wait_send())
                guarded(owns(j, qm), lambda j=j, k=k: direct(j, k).wait_recv())
                guarded(owns(j, qm), lambda j=j, k=k: second_hop(j, k).wait_recv())
            guarded(owns(j, qm), lambda j=j: local(j).wait())

    return start, forward, finish


def _front(c, c_ctx, ada_w, ada_b, w_in, w_out, me):
    nloc = ada_w.shape[1]

    def body(me_ref, c_ref, cc_ref, aw_ref, ab_ref, win_ref, wout_ref,
             wfull_ref, woutb_ref, modx_ref, modc_ref, call_ref,
             wb_s, part_s, parts_s, w_send, w_recv, w_local, s_send, s_recv):
        x, y, cidx = lax.axis_index("x"), lax.axis_index("y"), lax.axis_index("c")
        me = me_ref[0]
        wb_s[...] = win_ref[...].astype(BF16)
        woutb_ref[...] = wout_ref[...].astype(BF16)
        start, forward, finish = _gather2_ops([wb_s], [wfull_ref], ["agc"], w_send, w_recv, w_local)
        start()

        def small_gather(src, my_slot, stage):
            copies = []
            for k in range(1, N_DEV):
                peer = (x ^ (k >> 2), y ^ ((k >> 1) & 1), cidx ^ (k & 1))
                cp = pltpu.make_async_remote_copy(src_ref=src, dst_ref=my_slot, send_sem=s_send.at[stage, k - 1],
                                                  recv_sem=s_recv.at[stage, k - 1], device_id=peer,
                                                  device_id_type=MESH)
                cp.start()
                copies.append(cp)
            pltpu.sync_copy(src, my_slot)
            for cp in copies:
                cp.wait()

        small_gather(c_ref, call_ref.at[pl.ds(me, 1), :], 0)
        off = pl.multiple_of(me * nloc, 128)
        b = ab_ref[:, pl.ds(off, nloc)]
        w = aw_ref[...]
        sx, _ = _silu_and_grad(call_ref[...])
        sc, _ = _silu_and_grad(jnp.broadcast_to(cc_ref[...], (8, D)))
        part_s[0:8, :] = _dot(sx, w) + b
        part_s[8:16, :] = _dot(sc, w) + b
        small_gather(part_s, parts_s.at[me], 1)
        mine = _rows((16, nloc)) == me
        for j in range(N_DEV):
            pj = parts_s[j]
            modx_ref[:, j * nloc:(j + 1) * nloc] = jnp.sum(jnp.where(mine, pj, 0.0), axis=0, keepdims=True)
            modc_ref[:, j * nloc:(j + 1) * nloc] = pj[8:9, :]
        forward()
        finish()

    return _call(
        body, name="front",
        out_shape=[jax.ShapeDtypeStruct((D, D_IN), BF16), jax.ShapeDtypeStruct(w_out.shape, BF16),
                   jax.ShapeDtypeStruct((1, 3 * D), F32), jax.ShapeDtypeStruct((1, 3 * D), F32),
                   jax.ShapeDtypeStruct((N_DEV, D), F32)],
        in_specs=[pl.BlockSpec(memory_space=pltpu.SMEM)] + [VMEM] * 6, out_specs=[HBM, VMEM, VMEM, VMEM, VMEM],
        scratch_shapes=[pltpu.VMEM(w_in.shape, BF16), pltpu.VMEM((16, nloc), F32),
                        pltpu.VMEM((N_DEV, 16, nloc), F32)] + _gather2_sems(1) +
                       [pltpu.SemaphoreType.DMA((2, N_DEV - 1)), pltpu.SemaphoreType.DMA((2, N_DEV - 1))],
        compiler_params=pltpu.CompilerParams(vmem_limit_bytes=VMEM_LIMIT, has_side_effects=True),
    )(me, c, c_ctx, ada_w, ada_b, w_in, w_out)


ARRIVAL = (0, 1, 2, 4, 3, 5, 6, 7)


def _front_project(xr, c, c_ctx, ada_w, ada_b, ng, w_in, w_out, cw, lam, me):
    nloc = ada_w.shape[1]
    ws = W_IN_SHARD
    order = me[0] ^ jnp.asarray(ARRIVAL, jnp.int32)

    def body(ord_ref, x_ref, c_ref, cc_ref, aw_ref, ab_ref, ng_ref, win_ref, wout_ref, cw_ref, lam_ref,
             z_ref, hn_ref, wfull_ref, woutb_ref, modx_ref, modc_ref, call_ref, cwf_ref, lamf_ref,
             wv, call_s, part_s, parts_s, w_send, w_recv, hbm_sems, s_send, s_recv, g_send, g_recv, g_local):
        t = pl.program_id(0)
        x, y, cidx = lax.axis_index("x"), lax.axis_index("y"), lax.axis_index("c")
        me_i = ord_ref[0]
        sibling = (x, y, 1 - cidx)
        chips = [(x ^ (k >> 1), y ^ (k & 1)) for k in (1, 2, 3)]
        g_start, g_pass, g_finish = _gather2_ops([cw_ref, lam_ref], [cwf_ref, lamf_ref], ["agc", "agc"],
                                                 g_send, g_recv, g_local)

        def shard_copy(k, px, py, pc, to, half=None):
            slot = wv.at[4 * px + 2 * py + pc]
            if half is not None:
                slot = slot.at[pl.ds(half * (D // 2), D // 2), :]
            return pltpu.make_async_remote_copy(src_ref=slot, dst_ref=slot, send_sem=w_send.at[k],
                                                recv_sem=w_recv.at[k], device_id=to, device_id_type=MESH)

        def small_gather(src, my_slot, stage):
            copies = []
            for k in range(1, N_DEV):
                peer = (x ^ (k >> 2), y ^ ((k >> 1) & 1), cidx ^ (k & 1))
                cp = pltpu.make_async_remote_copy(src_ref=src, dst_ref=my_slot, send_sem=s_send.at[stage, k - 1],
                                                  recv_sem=s_recv.at[stage, k - 1], device_id=peer,
                                                  device_id_type=MESH)
                cp.start()
                copies.append(cp)
            pltpu.sync_copy(src, my_slot)
            return copies

        def finish_small(copies):
            for cp in copies:
                cp.wait()

        def to_neighbours(half):
            for i in (0, 1):
                shard_copy(1 + i, x, y, cidx, (*chips[i], cidx), half=half).start()

        @pl.when(t == 0)
        def _():
            g_start()
            wv[me_i] = win_ref[...].astype(BF16)
            woutb_ref[...] = wout_ref[...].astype(BF16)
            shard_copy(0, x, y, cidx, sibling).start()
            finish_small(small_gather(c_ref, call_s.at[pl.ds(me_i, 1), :], 0))
            to_neighbours(0)
            call_ref[...] = call_s[...]
            off = pl.multiple_of(me_i * nloc, 128)
            b = ab_ref[:, pl.ds(off, nloc)]
            w = aw_ref[...]
            sx, _ = _silu_and_grad(call_s[...])
            sc, _ = _silu_and_grad(jnp.broadcast_to(cc_ref[...], (8, D)))
            part_s[0:8, :] = _dot(sx, w) + b
            part_s[8:16, :] = _dot(sc, w) + b
            parts_sent = small_gather(part_s, parts_s.at[me_i], 1)
            to_neighbours(1)
            finish_small(parts_sent)
            mine = _rows((16, nloc)) == me_i
            for j in range(N_DEV):
                pj = parts_s[j]
                modx_ref[:, j * nloc:(j + 1) * nloc] = jnp.sum(jnp.where(mine, pj, 0.0), axis=0, keepdims=True)
                modc_ref[:, j * nloc:(j + 1) * nloc] = pj[8:9, :]
            shift, scale1, ngv = modx_ref[:, 0:D], 1.0 + modx_ref[:, D:2 * D], ng_ref[...]
            for r in range(L // ROWS):
                rsl = slice(r * ROWS, (r + 1) * ROWS)
                xv = x_ref[rsl, :]
                rs = lax.rsqrt(jnp.mean(xv * xv, axis=-1, keepdims=True) + NORM_EPS)
                hn_ref[rsl, :] = ((xv * rs * ngv) * scale1 + shift).astype(BF16)

        @pl.when(t == 1)
        def _():
            shard_copy(0, x, y, 1 - cidx, sibling).wait_recv()
            g_pass()

        for i in (0, 1):
            @pl.when(t == ARRIVAL.index((2, 4)[i]))
            def _(i=i):
                shard_copy(1 + i, *chips[i], cidx, sibling).wait_recv()
                shard_copy(4 + i, *chips[i], cidx, sibling).start()
                shard_copy((7, 3)[i], *chips[i], cidx, (*chips[1 - i], cidx), half=i).start()

        @pl.when(t == ARRIVAL.index(6))
        def _():
            shard_copy(3, *chips[2], cidx, sibling, half=1).wait_recv()
            shard_copy(7, *chips[2], cidx, sibling, half=0).wait_recv()
            shard_copy(6, *chips[2], cidx, sibling).start()

        for i in range(3):
            @pl.when(t == ARRIVAL.index((3, 5, 7)[i]))
            def _(i=i):
                shard_copy(4 + i, *chips[i], 1 - cidx, sibling).wait_recv()

        @pl.when(t == 2)
        def _():
            g_finish()

        dev = ord_ref[t]
        for r in range(L // (2 * ROWS)):
            rsl = slice(r * 2 * ROWS, (r + 1) * 2 * ROWS)
            z_ref[rsl, :] = _dot(hn_ref[rsl, :], wv[dev])
        col = pl.ds(pl.multiple_of(dev * ws, 128), ws)
        pltpu.make_async_copy(wv.at[dev], wfull_ref.at[:, col], hbm_sems.at[t]).start()

        @pl.when(t == N_DEV - 1)
        def _():
            for k in (0, 1, 2, 4, 5, 6):
                shard_copy(k, x, y, cidx, sibling).wait_send()
            for k in (3, 7):
                shard_copy(k, x, y, cidx, sibling, half=0).wait_send()
            for s in range(N_DEV):
                pltpu.make_async_copy(wv.at[0], wfull_ref.at[:, pl.ds(0, ws)], hbm_sems.at[s]).wait()

    const = lambda *shape: pl.BlockSpec(shape, lambda t, o: (0,) * len(shape))
    once = lambda *shape: pl.BlockSpec(shape, lambda t, o: (0,) * len(shape), pipeline_mode=pl.Buffered(1))
    return _call(
        body, name="front_project",
        out_shape=[jax.ShapeDtypeStruct((L, D_IN), F32), jax.ShapeDtypeStruct((L, D), BF16),
                   jax.ShapeDtypeStruct((D, D_IN), BF16), jax.ShapeDtypeStruct(w_out.shape, BF16),
                   jax.ShapeDtypeStruct((1, 3 * D), F32), jax.ShapeDtypeStruct((1, 3 * D), F32),
                   jax.ShapeDtypeStruct((N_DEV, D), F32), jax.ShapeDtypeStruct((CONV_W, D), F32),
                   jax.ShapeDtypeStruct((2, D), F32)],
        grid_spec=pltpu.PrefetchScalarGridSpec(
            num_scalar_prefetch=1, grid=(N_DEV,),
            in_specs=[once(L, D), const(1, D), const(1, D), once(D, nloc), const(1, 3 * D), const(1, D),
                      once(D, ws), once(*w_out.shape), HBM, HBM],
            out_specs=[pl.BlockSpec((L, ws), lambda t, o: (0, o[t])), const(L, D), HBM, const(*w_out.shape),
                       const(1, 3 * D), const(1, 3 * D), const(N_DEV, D), HBM, HBM],
            scratch_shapes=[pltpu.VMEM((N_DEV, D, ws), BF16), pltpu.VMEM((N_DEV, D), F32), pltpu.VMEM((16, nloc), F32),
                            pltpu.VMEM((N_DEV, 16, nloc), F32), pltpu.SemaphoreType.DMA((8,)),
                            pltpu.SemaphoreType.DMA((8,)), pltpu.SemaphoreType.DMA((N_DEV,)),
                            pltpu.SemaphoreType.DMA((2, N_DEV - 1)), pltpu.SemaphoreType.DMA((2, N_DEV - 1))]
            + _gather2_sems(2)),
        compiler_params=pltpu.CompilerParams(dimension_semantics=("arbitrary",), vmem_limit_bytes=VMEM_LIMIT,
                                             has_side_effects=True),
    )(order, xr, c, c_ctx, ada_w, ada_b, ng, w_in, w_out, pltpu.with_memory_space_constraint(cw, pltpu.HBM),
      pltpu.with_memory_space_constraint(lam, pltpu.HBM))


def _project(xr, mod, ng, w, ncols, tm, name, gather=None, gather_modes=()):
    rows = xr.shape[0]
    steps = rows // tm
    ng_ = len(gather or ())

    def body(x_ref, sh_ref, sc_ref, ng_ref, w_ref, *rest):
        z_ref, hn_ref = rest[ng_:ng_ + 2]
        if ng_:
            start, forward, finish = _gather2_ops(rest[:ng_], rest[ng_ + 2:2 * ng_ + 2], gather_modes,
                                                  *rest[2 * ng_ + 2:])
            pl.when(pl.program_id(0) == 0)(start)
            pl.when(pl.program_id(0) == steps // 2)(forward)
        x = x_ref[...]
        rs = lax.rsqrt(jnp.mean(x * x, axis=-1, keepdims=True) + NORM_EPS)
        hn = (x * rs * ng_ref[...]) * (1.0 + sc_ref[...]) + sh_ref[...]
        hb = hn.astype(BF16)
        hn_ref[...] = hb
        for n in range(ncols // D):
            z_ref[:, n * D:(n + 1) * D] = _dot(hb, w_ref[:, n * D:(n + 1) * D])
        if ng_:
            pl.when(pl.program_id(0) == steps - 1)(finish)

    vec = pl.BlockSpec((1, D), lambda i: (0, 0))
    gathered = _gather2_shapes(gather, gather_modes) if ng_ else []
    return _call(
        body, name=name, grid=(steps,),
        out_shape=[jax.ShapeDtypeStruct((rows, ncols), F32), jax.ShapeDtypeStruct((rows, D), BF16)] + gathered,
        in_specs=[pl.BlockSpec((tm, D), lambda i: (i, 0)), vec, pl.BlockSpec((1, D), lambda i: (0, 1)), vec,
                  pl.BlockSpec((D, ncols), lambda i: (0, 0), pipeline_mode=pl.Buffered(1))] + [HBM] * ng_,
        out_specs=[pl.BlockSpec((tm, ncols), lambda i: (i, 0)), pl.BlockSpec((tm, D), lambda i: (i, 0))] + [HBM] * ng_,
        scratch_shapes=_gather2_sems(ng_) if ng_ else [],
        compiler_params=pltpu.CompilerParams(dimension_semantics=("arbitrary",), vmem_limit_bytes=VMEM_LIMIT,
                                             has_side_effects=bool(ng_)),
    )(xr, mod, mod, ng, w, *[pltpu.with_memory_space_constraint(a, pltpu.HBM) for a in gather or ()])


def _scan_pair(af_ref, uf_ref, hf_ref, h0f, ab_ref, ub_ref, hb_ref, h0b, t_len):
    span = 8 * SCAN_BLOCKS
    nit = t_len // span
    rows = _rows((8, HD))

    def local_scan(a, b, forward):
        for s in (1, 2, 4):
            sh = s if forward else 8 - s
            m = rows >= s if forward else rows < 8 - s
            b = a * jnp.where(m, pltpu.roll(b, sh, 0), 0.0) + b
            a = a * jnp.where(m, pltpu.roll(a, sh, 0), 1.0)
        return a, b

    def span_scan(a_ref, u_ref, h_ref, off, carry, forward):
        order = range(SCAN_BLOCKS) if forward else range(SCAN_BLOCKS - 1, -1, -1)
        last = slice(7, 8) if forward else slice(0, 1)
        for q in order:
            rs = pl.ds(off + 8 * q, 8)
            a, b = local_scan(a_ref[rs, :], u_ref[rs, :], forward)
            h_ref[rs, :] = b + a * carry
            carry = a[last, :] * carry + b[last, :]
        return carry

    def body(k, carry):
        cf, cb = carry
        cf = span_scan(af_ref, uf_ref, hf_ref, pl.multiple_of(k * span, span), cf, True)
        cb = span_scan(ab_ref, ub_ref, hb_ref, pl.multiple_of((nit - 1 - k) * span, span), cb, False)
        return cf, cb

    return lax.fori_loop(0, nit, body, (h0f, h0b))


SCAN_BLOCKS = 8


def _shifted(pad_ref, x, offsets, before=0.0, after=0.0):
    n = x.shape[0]
    pad_ref[0:8, :] = jnp.broadcast_to(jnp.asarray(before, F32), (8, x.shape[1]))
    pad_ref[8:8 + n, :] = x
    pad_ref[8 + n:16 + n, :] = jnp.broadcast_to(jnp.asarray(after, F32), (8, x.shape[1]))
    return [pad_ref[8 + o:8 + o + n, :] for o in offsets]


def _conv(xa, cw, cb, pad_ref):
    xm1, xp1, xp2 = _shifted(pad_ref, xa, (-1, 1, 2))
    return xm1 * cw[0:1, :] + xa * cw[1:2, :] + xp1 * cw[2:3, :] + xp2 * cw[3:4, :] + cb


def _gates(xc, wa, wx, ba, bx, nsp):
    xb = xc.astype(BF16)
    r = _sigmoid(_dot(xb, wa) + ba)
    i = _sigmoid(_dot(xb, wx) + bx)
    log_a = r * nsp
    a = jnp.exp(log_a)
    g2 = jnp.tanh(log_a) * (-1.0 - a * a)
    rg = lax.rsqrt(jnp.maximum(g2, 1e-30))
    return r, i, a, g2 * rg, rg


def _lru_param_specs():
    h4 = pl.BlockSpec((2, 1, HD, HD), lambda h: (0, h, 0, 0))
    v2 = pl.BlockSpec((2, HD), lambda h: (0, h))
    b16 = pl.BlockSpec((2 * HEADS, HD), lambda h: (0, 0))
    return dict(
        xa=pl.BlockSpec((L, HD), lambda h: (0, h)), xac=pl.BlockSpec((LC, HD), lambda h: (0, h)),
        cw=pl.BlockSpec((CONV_W, HD), lambda h: (0, h)), cb=pl.BlockSpec((1, HD), lambda h: (0, h)), h4=h4, v2=v2,
        b16=b16)


def _bias_row(ref, d):
    mask = _rows((2 * HEADS, HD)) == d * HEADS + pl.program_id(0)
    return jnp.sum(jnp.where(mask, ref[...], 0.0), axis=0, keepdims=True), mask


def _lru_forward(zx, zc, cw, cb, wa, wx, ba, bx, lam, gather, gather_modes):
    ng_ = len(gather)

    def body(xa_ref, xac_ref, cw_ref, cb_ref, wa_ref, wx_ref, ba_ref, bx_ref, lam_ref, *rest):
        yl_ref = rest[ng_]
        af, uf, hf, ab, ub, hb, pad_s = rest[2 * ng_ + 1:2 * ng_ + 8]
        start, pass_on, finish = _gather2_ops(rest[:ng_], rest[ng_ + 1:2 * ng_ + 1], gather_modes,
                                              *rest[2 * ng_ + 8:])
        pl.when(pl.program_id(0) == 0)(start)
        pl.when(pl.program_id(0) == HEADS // 2)(pass_on)
        pl.when(pl.program_id(0) == HEADS - 1)(finish)
        cwv, cbv = cw_ref[...], cb_ref[...]
        nsp = (-LRU_C) * _softplus(-lam_ref[...])

        def forward(xa, t_len, h0f, h0b):
            xc = _conv(xa, cwv, cbv, pad_s)
            for d, (a_ref, u_ref) in enumerate(((af, uf), (ab, ub))):
                _, i, a, gamma, _ = _gates(xc, wa_ref[d, 0].astype(BF16), wx_ref[d, 0].astype(BF16),
                                           _bias_row(ba_ref, d)[0], _bias_row(bx_ref, d)[0], nsp[d:d + 1, :])
                a_ref[0:t_len, :] = a
                u_ref[0:t_len, :] = gamma * (i * xc)
            return _scan_pair(af, uf, hf, h0f, ab, ub, hb, h0b, t_len)

        z = jnp.zeros((1, HD), F32)
        h0f, h0b = forward(xac_ref[...], LC, z, z)
        forward(xa_ref[...], L, h0f, h0b)
        yl_ref[...] = hf[...] + hb[...]

    s = _lru_param_specs()
    return _call(
        body, name="lru_forward", grid=(HEADS,),
        out_shape=[jax.ShapeDtypeStruct((L, D), F32)] + _gather2_shapes(gather, gather_modes),
        in_specs=[s["xa"], s["xac"], s["cw"], s["cb"], s["h4"], s["h4"], s["b16"], s["b16"], s["v2"]] + [HBM] * ng_,
        out_specs=[pl.BlockSpec((L, HD), lambda h: (0, h))] + [HBM] * ng_,
        scratch_shapes=[pltpu.VMEM((L, HD), F32)] * 6 + [pltpu.VMEM((L + 16, HD), F32)] + _gather2_sems(ng_),
        compiler_params=pltpu.CompilerParams(dimension_semantics=("arbitrary",), vmem_limit_bytes=VMEM_LIMIT,
                                             has_side_effects=True),
    )(zx, zc, cw, cb, wa, wx, ba, bx, lam, *[pltpu.with_memory_space_constraint(a, pltpu.HBM) for a in gather])


def _lru_backward(zx, zc, dyl, dz, cw, cb, wa, wx, ba, bx, lam, chip_sums, first_chips=None):
    nr = len(chip_sums)

    def body(xa_ref, xac_ref, dyl_ref, dz_in, cw_ref, cb_ref, wa_ref, wx_ref, ba_ref, bx_ref, lam_ref, *rest):
        (dxa_ref, dxac_ref, dwa_ref, dwx_ref, dba_ref, dbx_ref, dlam_ref, dcw_ref,
         dcb_ref) = rest[nr:nr + 9]
        main_s, ctx_s, pad_s = rest[3 * nr + 9:3 * nr + 12]
        if nr:
            start, forward, finish = _chips_ops(rest[:nr], rest[nr + 9:2 * nr + 9], rest[2 * nr + 9:3 * nr + 9],
                                                *rest[3 * nr + 12:], first_chips=first_chips)
            pl.when(pl.program_id(0) == 0)(start)
            pl.when(pl.program_id(0) == HEADS // 2)(forward)
            pl.when(pl.program_id(0) == HEADS - 1)(finish)
        del dz_in

        @pl.when(pl.program_id(0) == 0)
        def _():
            dba_ref[...] = jnp.zeros_like(dba_ref)
            dbx_ref[...] = jnp.zeros_like(dbx_ref)

        cwv, cbv = cw_ref[...], cb_ref[...]
        lamv = lam_ref[...]
        sp = _softplus(-lamv)
        nsp = (-LRU_C) * sp
        z = jnp.zeros((1, HD), F32)

        def wmat(ref, d):
            return ref[d, 0].astype(BF16)

        def workspace(s):
            return dict(a=(s.at[0], s.at[1]), u=(s.at[2], s.at[3]), h=(s.at[4], s.at[5]), rho=(s.at[6], s.at[7]),
                        saved=(tuple(s.at[8 + k] for k in range(4)), tuple(s.at[12 + k] for k in range(4))),
                        xc=s.at[16])

        def forward(ws, xa, t_len, h0f, h0b):
            xc = _conv(xa, cwv, cbv, pad_s)
            ws["xc"][...] = xc
            for d in (0, 1):
                vals = _gates(xc, wmat(wa_ref, d), wmat(wx_ref, d), _bias_row(ba_ref, d)[0],
                              _bias_row(bx_ref, d)[0], nsp[d:d + 1, :])
                r, i, a, gamma, rg = vals
                ws["a"][d][...] = a
                ws["u"][d][...] = gamma * (i * xc)
                for ref, val in zip(ws["saved"][d], (r, i, gamma, rg)):
                    ref[...] = val
            return _scan_pair(ws["a"][0], ws["u"][0], ws["h"][0], h0f, ws["a"][1], ws["u"][1], ws["h"][1], h0b,
                              t_len)

        def backward(ws, xa, t_len, h0f, h0b, dhf, dhb, first):
            xc = ws["xc"][...]
            (af, ab), (uf, ub), (hf, hb), (rf, rb) = ws["a"], ws["u"], ws["h"], ws["rho"]
            uf[...] = ab[...] * dhb
            ub[...] = af[...] * dhf
            rho_b_last, rho_f_first = _scan_pair(ab, uf, rb, z, af, ub, rf, z, t_len)
            dxc = jnp.zeros((t_len, HD), F32)
            dsp = []
            for d in (0, 1):
                r, i, gamma, rg = (ref[...] for ref in ws["saved"][d])
                a = ws["a"][d][...]
                if d == 0:
                    lam_t = dhf + _shifted(pad_s, rf[...], (1,))[0]
                    h_prev = _shifted(pad_s, hf[...], (-1,), before=h0f)[0]
                else:
                    lam_t = dhb + _shifted(pad_s, rb[...], (-1,))[0]
                    h_prev = _shifted(pad_s, hb[...], (1,), after=h0b)[0]
                da = lam_t * h_prev
                lx = lam_t * xc
                d_i = lx * gamma
                d_gamma = lx * i
                dxc = dxc + lam_t * (gamma * i)
                d_log_a = a * (da - d_gamma * (a * rg))
                dsp.append(jnp.sum(d_log_a * r, axis=0, keepdims=True) * (-LRU_C))
                d_pre_r = d_log_a * nsp[d:d + 1, :] * (r * (1.0 - r))
                d_pre_i = d_i * (i * (1.0 - i))
                prb, pib, xb = d_pre_r.astype(BF16), d_pre_i.astype(BF16), xc.astype(BF16)
                dxc = dxc + _dot_nt(prb, wmat(wa_ref, d)) + _dot_nt(pib, wmat(wx_ref, d))
                g_wa, g_wx = _dot_tn(xb, prb), _dot_tn(xb, pib)
                g_ba = jnp.sum(d_pre_r, axis=0, keepdims=True)
                g_bx = jnp.sum(d_pre_i, axis=0, keepdims=True)
                mask = _bias_row(ba_ref, d)[1]
                dba_ref[...] += jnp.where(mask, g_ba, 0.0)
                dbx_ref[...] += jnp.where(mask, g_bx, 0.0)
                if first:
                    dwa_ref[d, 0] = g_wa
                    dwx_ref[d, 0] = g_wx
                else:
                    dwa_ref[d, 0] += g_wa
                    dwx_ref[d, 0] += g_wx
            g_lam = jnp.concatenate(dsp, axis=0) * (-_sigmoid(-lamv))
            dm1, dp1, dm2 = _shifted(pad_s, dxc, (-1, 1, -2))
            dxa = dp1 * cwv[0:1, :] + dxc * cwv[1:2, :] + dm1 * cwv[2:3, :] + dm2 * cwv[3:4, :]
            xm1, xp1, xp2 = _shifted(pad_s, xa, (-1, 1, 2))
            g_cw = jnp.concatenate([jnp.sum(dxc * v, axis=0, keepdims=True) for v in (xm1, xa, xp1, xp2)], axis=0)
            g_cb = jnp.sum(dxc, axis=0, keepdims=True)
            if first:
                dlam_ref[...] = g_lam
                dcw_ref[...] = g_cw
                dcb_ref[...] = g_cb
            else:
                dlam_ref[...] += g_lam
                dcw_ref[...] += g_cw
                dcb_ref[...] += g_cb
            return dxa, rho_f_first, rho_b_last

        ws_x, ws_c = workspace(main_s), workspace(ctx_s)
        h0f, h0b = forward(ws_c, xac_ref[...], LC, z, z)
        forward(ws_x, xa_ref[...], L, h0f, h0b)
        dh = dyl_ref[...]
        dxa, dh0f, dh0b = backward(ws_x, xa_ref[...], L, h0f, h0b, dh, dh, True)
        dxa_ref[...] = dxa.astype(BF16)
        rc = _rows((LC, HD))
        dxac, _, _ = backward(ws_c, xac_ref[...], LC, z, z, jnp.where(rc == LC - 1, dh0f, 0.0),
                              jnp.where(rc == 0, dh0b, 0.0), False)
        dxac_ref[...] = dxac.astype(BF16)

    s = _lru_param_specs()
    col = lambda r: pl.BlockSpec((r, HD), lambda h: (0, h))
    return _call(
        body, name="lru_backward", grid=(HEADS,),
        out_shape=[jax.ShapeDtypeStruct((L, D_IN), BF16), jax.ShapeDtypeStruct((LC, D), BF16),
                   jax.ShapeDtypeStruct((2, HEADS, HD, HD), F32), jax.ShapeDtypeStruct((2, HEADS, HD, HD), F32),
                   jax.ShapeDtypeStruct((2 * HEADS, HD), F32), jax.ShapeDtypeStruct((2 * HEADS, HD), F32),
                   jax.ShapeDtypeStruct((2, D), F32), jax.ShapeDtypeStruct((CONV_W, D), F32),
                   jax.ShapeDtypeStruct((1, D), F32)] + [jax.ShapeDtypeStruct((4,) + a.shape[1:], a.dtype)
                                                          for a in chip_sums] + _chips_stage_shapes(chip_sums),
        in_specs=[s["xa"], s["xac"], col(L), pl.BlockSpec(memory_space=pl.ANY), s["cw"], s["cb"], s["h4"], s["h4"],
                  s["b16"], s["b16"], s["v2"]] + [HBM] * nr,
        out_specs=[col(L), col(LC), s["h4"], s["h4"], s["b16"], s["b16"], s["v2"], col(CONV_W), col(1)]
        + [HBM] * (2 * nr),
        scratch_shapes=[pltpu.VMEM((17, L, HD), F32), pltpu.VMEM((17, LC, HD), F32), pltpu.VMEM((L + 16, HD), F32)]
        + (_chips_sems(nr) if nr else []),
        input_output_aliases={3: 0},
        compiler_params=pltpu.CompilerParams(dimension_semantics=("arbitrary",), vmem_limit_bytes=VMEM_LIMIT,
                                             has_side_effects=True),
    )(zx, zc, dyl, dz, cw, cb, wa, wx, ba, bx, lam, *[pltpu.with_memory_space_constraint(a, pltpu.HBM)
                                                       for a in chip_sums])


def _mixer_loss(x, tgt, zx, yl, gx, fg, lng, lnb, ws, wst, bst, wout, tm):
    ncht = tm // CHUNK

    def body(x_ref, t_ref, ga_ref, u_ref, v_ref, gb_ref, yl_ref, gx_ref, fg_ref, lng_ref, lnb_ref, ws_ref, wst_ref,
             bst_ref, wout_ref,
             dz_ref, dyl_ref, dxn_ref, y_s, do_ref, dws_ref, dbst_ref, vec_ref,
             vn_s, mix_s, dm_s, dvn_s):
        step = pl.program_id(0)

        @pl.when(step == 0)
        def _():
            dws_ref[...] = jnp.zeros_like(dws_ref)
            dbst_ref[...] = jnp.zeros_like(dbst_ref)
            vec_ref[...] = jnp.zeros_like(vec_ref)

        u, v = u_ref[...], v_ref[...]
        ug, dug_du = _gelu_and_grad(u)
        vg, dvg_dv = _gelu_and_grad(v)
        mu = jnp.mean(vg, axis=-1, keepdims=True)
        vc = vg - mu
        rstd = lax.rsqrt(jnp.mean(vc * vc, axis=-1, keepdims=True) + LN_EPS)
        vhat = vc * rstd
        lngv = lng_ref[...]
        vn_s[...] = (vhat * lngv + lnb_ref[...]).astype(BF16)
        for ch in range(ncht):
            rs = slice(ch * CHUNK, (ch + 1) * CHUNK)
            for g in range(HEADS):
                cs = slice(g * HD, (g + 1) * HD)
                mix_s[rs, cs] = _dot(ws_ref[g], vn_s[rs, cs]) + bst_ref[:, g:g + 1]
        mixed = mix_s[...]
        ga, gb, yl = ga_ref[...], gb_ref[...], yl_ref[...]
        sga, dsga = _silu_and_grad(ga)
        sgb, dsgb = _silu_and_grad(gb)
        ys = ug * mixed
        y_s[:, 0:D] = (yl * sga).astype(BF16)
        y_s[:, D:D_MIX] = (ys * sgb).astype(BF16)
        o = _dot(y_s[...], wout_ref[...])
        gxv, fgv = gx_ref[...], fg_ref[...]
        xn = x_ref[...] + gxv * o
        rs2 = lax.rsqrt(jnp.mean(xn * xn, axis=-1, keepdims=True) + NORM_EPS)
        xh = xn * rs2
        diff = xh * fgv - t_ref[...]
        vec_ref[R_LOSS:R_LOSS + 1, :] += jnp.full((1, D), jnp.sum(diff * diff) * (0.5 / D), F32)
        dout = diff * (1.0 / D)
        w = dout * fgv
        dxn = rs2 * (w - xh * jnp.mean(w * xh, axis=-1, keepdims=True))
        dxn_ref[...] = dxn
        vec_ref[0:1, :] += jnp.sum(dxn * o, axis=0, keepdims=True)
        vec_ref[1:2, :] += jnp.sum(dout * xh, axis=0, keepdims=True)
        dob = (dxn * gxv).astype(BF16)
        do_ref[...] = dob
        dy = _dot_nt(dob, wout_ref[...])
        dya, dyb = dy[:, 0:D], dy[:, D:D_MIX]
        dyl_ref[...] = dya * sga
        dys = dyb * sgb
        dz_ref[:, 0:D] = jnp.zeros((tm, D), BF16)
        dz_ref[:, D:2 * D] = (dya * yl * dsga).astype(BF16)
        dz_ref[:, 2 * D:3 * D] = (dys * mixed * dug_du).astype(BF16)
        dz_ref[:, 4 * D:5 * D] = (dyb * ys * dsgb).astype(BF16)
        dm = dys * ug
        dm_s[...] = dm.astype(BF16)
        for g in range(HEADS):
            cs = slice(g * HD, (g + 1) * HD)
            dbst_ref[:, g:g + 1] += sum(jnp.sum(dm[ch * CHUNK:(ch + 1) * CHUNK, cs], axis=1, keepdims=True)
                                        for ch in range(ncht))
            for ch in range(ncht):
                rs = slice(ch * CHUNK, (ch + 1) * CHUNK)
                dws_ref[g] += _dot_nt(dm_s[rs, cs], vn_s[rs, cs])
                dvn_s[rs, cs] = _dot(wst_ref[g], dm_s[rs, cs])
        dvn = dvn_s[...]
        vec_ref[2:3, :] += jnp.sum(dvn * vhat, axis=0, keepdims=True)
        vec_ref[3:4, :] += jnp.sum(dvn, axis=0, keepdims=True)
        dvh = dvn * lngv
        dvg = rstd * (dvh - jnp.mean(dvh, axis=-1, keepdims=True) - vhat * jnp.mean(dvh * vhat, axis=-1, keepdims=True))
        dz_ref[:, 3 * D:4 * D] = (dvg * dvg_dv).astype(BF16)

    tile = pl.BlockSpec((tm, D), lambda i: (i, 0))
    zcol = lambda n: pl.BlockSpec((tm, D), lambda i: (i, n))
    vec = pl.BlockSpec((1, D), lambda i: (0, 0))
    full = lambda *s: pl.BlockSpec(s, lambda i: (0,) * len(s))
    return _call(
        body, name="mixer_loss", grid=(L // tm,),
        out_shape=[jax.ShapeDtypeStruct((L, D_IN), BF16), jax.ShapeDtypeStruct((L, D), F32),
                   jax.ShapeDtypeStruct((L, D), F32), jax.ShapeDtypeStruct((L, D_MIX), BF16),
                   jax.ShapeDtypeStruct((L, D), BF16),
                   jax.ShapeDtypeStruct((HEADS, CHUNK, CHUNK), F32), jax.ShapeDtypeStruct((CHUNK, HEADS), F32),
                   jax.ShapeDtypeStruct((8, D), F32)],
        in_specs=[tile, tile, zcol(1), zcol(2), zcol(3), zcol(4), tile, pl.BlockSpec((1, D), lambda i: (0, 2)),
                  vec, vec, vec,
                  full(HEADS, CHUNK, CHUNK), full(HEADS, CHUNK, CHUNK), full(CHUNK, HEADS),
                  pl.BlockSpec((D_MIX, D), lambda i: (0, 0), pipeline_mode=pl.Buffered(1))],
        out_specs=[pl.BlockSpec((tm, D_IN), lambda i: (i, 0)), tile, tile,
                   pl.BlockSpec((tm, D_MIX), lambda i: (i, 0)), tile,
                   full(HEADS, CHUNK, CHUNK), full(CHUNK, HEADS), full(8, D)],
        scratch_shapes=[pltpu.VMEM((tm, D), BF16), pltpu.VMEM((tm, D), F32),
                        pltpu.VMEM((tm, D), BF16), pltpu.VMEM((tm, D), F32)],
        compiler_params=_params("arbitrary"),
    )(x, tgt, zx, zx, zx, zx, yl, gx, fg, lng, lnb, ws, wst, bst, wout)


def _grad_w(a, b, a2, b2, tk, name, bw, first, nblocks, split):
    nk = a.shape[0] // tk
    m = a.shape[1]
    with_ctx = a2 is not None
    if split == "cols":
        slots, r, w = nblocks, m, bw // 2
        piece = lambda q, pc: (slice(None), slice(pc * w, (pc + 1) * w))
    else:
        slots, r, w = 4, m // 8, bw
        piece = lambda q, pc: (slice((2 * q + pc) * r, (2 * q + pc + 1) * r), slice(None))

    def body(*refs):
        a_ref, b_ref = refs[:2]
        a2_ref, b2_ref = refs[2:4] if with_ctx else (None, None)
        sums_ref, acc, mine_v, send_v, stage_v, send_sems, recv_sems = refs[4 if with_ctx else 2:]
        n, k = pl.program_id(0), pl.program_id(1)
        x, y, c = lax.axis_index("x"), lax.axis_index("y"), lax.axis_index("c")

        def to_sibling(s):
            return pltpu.make_async_remote_copy(src_ref=send_v.at[s], dst_ref=stage_v.at[s], send_sem=send_sems.at[s],
                                                recv_sem=recv_sems.at[s], device_id=(x, y, 1 - c),
                                                device_id_type=MESH)

        @pl.when(k == 0)
        def _():
            acc[...] = _dot_tn(a_ref[...], b_ref[...])

        if nk > 1:
            @pl.when(k > 0)
            def _():
                acc[...] += _dot_tn(a_ref[...], b_ref[...])

        if with_ctx:
            @pl.when(jnp.logical_and(k == nk - 1, n == 0))
            def _():
                acc[:, 0:b2_ref.shape[1]] += _dot_tn(a2_ref[...], b2_ref[...])

        def hand_over(s, q):
            for pc in (0, 1):
                @pl.when(c == pc)
                def _(pc=pc):
                    mine_v[s] = acc[piece(q, pc)]
                    send_v[s] = acc[piece(q, 1 - pc)].astype(BF16)
            to_sibling(s).start()

        for i in range(nblocks):
            @pl.when(jnp.logical_and(k == nk - 1, n == i))
            def _(i=i):
                if split == "cols":
                    hand_over(i, 0)
                else:
                    for q in range(4):
                        hand_over(q, q)

        @pl.when(jnp.logical_and(k == nk - 1, n == nblocks - 1))
        def _():
            for s in range(slots):
                to_sibling(s).wait_recv()
                sums_ref[s] = (mine_v[s] + stage_v[s].astype(F32)).astype(BF16)
            for s in range(slots):
                to_sibling(s).wait_send()

    in_specs = [pl.BlockSpec((tk, m), lambda n, k: (k, 0)), pl.BlockSpec((tk, bw), lambda n, k: (k, n + first))]
    args = [a, b]
    if with_ctx:
        in_specs += [pl.BlockSpec(a2.shape, lambda n, k: (0, 0)), pl.BlockSpec(b2.shape, lambda n, k: (0, 0))]
        args += [a2, b2]
    return _call(
        body, name=name, grid=(nblocks, nk), out_shape=jax.ShapeDtypeStruct((slots, r, w), BF16),
        in_specs=in_specs, out_specs=pl.BlockSpec((slots, r, w), lambda n, k: (0, 0, 0)),
        scratch_shapes=[pltpu.VMEM((m, bw), F32), pltpu.VMEM((slots, r, w), F32), pltpu.VMEM((slots, r, w), BF16),
                        pltpu.VMEM((slots, r, w), BF16), pltpu.SemaphoreType.DMA((slots,)),
                        pltpu.SemaphoreType.DMA((slots,))],
        compiler_params=pltpu.CompilerParams(dimension_semantics=("arbitrary", "arbitrary"),
                                             vmem_limit_bytes=VMEM_LIMIT, has_side_effects=True),
    )(*args)


def _grad_rows(xr, dz, w, mod, ng, dres, ncols, tm, name, chip_sums=(), first_chips=None, dests=None):
    rows = xr.shape[0]
    steps = rows // tm
    with_dx = dres is not None
    nr = len(chip_sums)
    dests = [d for d in (dests or [None] * nr)]
    nd = sum(d is not None for d in dests)
    nin = 6 if with_dx else 5
    nout = 2 if with_dx else 1

    def body(*refs):
        if with_dx:
            x_ref, dz_ref, w_ref, sc_ref, ng_ref, dres_ref = refs[:nin]
            dx_ref, vec_ref = refs[nin + nr + nd:nin + nr + nd + nout]
        else:
            x_ref, dz_ref, w_ref, sc_ref, ng_ref = refs[:nin]
            (vec_ref,) = refs[nin + nr + nd:nin + nr + nd + nout]
        if nr:
            o0 = nin + nr + nd + nout
            start, forward, finish = _chips_ops(refs[nin:nin + nr], refs[o0:o0 + nr], refs[o0 + nr:o0 + 2 * nr],
                                                *refs[o0 + 2 * nr:], first_chips=first_chips)
            pl.when(pl.program_id(0) == 0)(start)
            pl.when(pl.program_id(0) == steps // 2)(forward)
            pl.when(pl.program_id(0) == steps - 1)(finish)

        @pl.when(pl.program_id(0) == 0)
        def _():
            vec_ref[...] = jnp.zeros_like(vec_ref)

        dhn = _dot_nt(dz_ref[...], w_ref[...])
        x = x_ref[...]
        rs = lax.rsqrt(jnp.mean(x * x, axis=-1, keepdims=True) + NORM_EPS)
        xh = x * rs
        ngv = ng_ref[...]
        y = xh * ngv
        vec_ref[0:1, :] += jnp.sum(dhn, axis=0, keepdims=True)
        vec_ref[1:2, :] += jnp.sum(dhn * y, axis=0, keepdims=True)
        dy = dhn * (1.0 + sc_ref[...])
        vec_ref[2:3, :] += jnp.sum(dy * xh, axis=0, keepdims=True)
        if with_dx:
            dxh = dy * ngv
            dx_ref[...] = dres_ref[...] + rs * (dxh - xh * jnp.mean(dxh * xh, axis=-1, keepdims=True))

    tile = pl.BlockSpec((tm, D), lambda i: (i, 0))
    vec = pl.BlockSpec((1, D), lambda i: (0, 0))
    in_specs = [tile, pl.BlockSpec((tm, ncols), lambda i: (i, 0)),
                pl.BlockSpec((D, ncols), lambda i: (0, 0), pipeline_mode=pl.Buffered(1)),
                pl.BlockSpec((1, D), lambda i: (0, 1)), vec]
    out_shape = [jax.ShapeDtypeStruct((8, D), F32)]
    out_specs = [pl.BlockSpec((8, D), lambda i: (0, 0))]
    args = [xr, dz, w, mod, ng]
    if with_dx:
        in_specs.append(tile)
        out_shape.insert(0, jax.ShapeDtypeStruct((rows, D), F32))
        out_specs.insert(0, tile)
        args.append(dres)
    aliases = {}
    for j, d in enumerate(dests):
        if d is not None:
            aliases[len(args) + nr + len(aliases)] = len(out_shape) + j
    in_specs += [HBM] * (nr + nd)
    out_specs += [HBM] * (2 * nr)
    out_shape += [jax.ShapeDtypeStruct((4,) + a.shape[1:], a.dtype) for a in chip_sums]
    out_shape += _chips_stage_shapes(chip_sums)
    args += [pltpu.with_memory_space_constraint(a, pltpu.HBM) for a in chip_sums]
    args += [pltpu.with_memory_space_constraint(d, pltpu.HBM) for d in dests if d is not None]
    return _call(body, name=name, grid=(steps,), out_shape=out_shape, in_specs=in_specs, out_specs=out_specs,
                 scratch_shapes=_chips_sems(nr) if nr else [], input_output_aliases=aliases,
                 compiler_params=pltpu.CompilerParams(dimension_semantics=("arbitrary",),
                                                      vmem_limit_bytes=VMEM_LIMIT, has_side_effects=bool(nr)))(*args)


def _adamw(w, g, m, v):
    m = ADAM_B1 * m + (1.0 - ADAM_B1) * g
    v = ADAM_B2 * v + (1.0 - ADAM_B2) * (g * g)
    m_hat = m / (1.0 - ADAM_B1 ** ADAM_STEP)
    v_hat = v / (1.0 - ADAM_B2 ** ADAM_STEP)
    delta = -ADAM_LR * (m_hat / (jnp.sqrt(v_hat) + ADAM_EPS) + ADAM_WD * w)
    return delta, m, v


def _adamw_reduced(parts, w, m, v, tr, name):
    r, n = w.shape
    nparts = parts.shape[0]

    def body(p_ref, w_ref, m_ref, v_ref, g_ref, d_ref, mo_ref, vo_ref):
        g = p_ref[0].astype(F32)
        for i in range(1, nparts):
            g = g + p_ref[i].astype(F32)
        g_ref[...] = g
        d_ref[...], mo_ref[...], vo_ref[...] = _adamw(w_ref[...], g, m_ref[...], v_ref[...])

    tile = pl.BlockSpec((tr, n), lambda i: (i, 0))
    sds = jax.ShapeDtypeStruct((r, n), F32)
    return _call(
        body, name=name, grid=(r // tr,), out_shape=[sds] * 4,
        in_specs=[pl.BlockSpec((nparts, tr, n), lambda i: (0, i, 0)), tile, tile, tile], out_specs=[tile] * 4,
        compiler_params=_params("arbitrary"),
    )(parts, w, m, v)


R_GATE, R_FINAL_G, R_LN_G, R_LN_B, R_LOSS = 0, 1, 2, 3, 4
R_SH_X, R_SC_X, R_NG_X = 5, 6, 7
R_SH_C, R_SC_C, R_NG_C = 8, 9, 10
R_LAM, R_CW, R_CB = 11, 13, 17
PACK_ROWS = 24
Q_BA, Q_BX, Q_SGU_B, PACK128_ROWS = 0, 16, 32, 40


def _reduce_small(vp_all, vq_all, mat_parts, ada_w, me):
    nloc = ada_w.shape[1]
    nm = len(mat_parts)

    def body(me_ref, vp_ref, vq_ref, *refs):
        mp_refs, w_ref = refs[:nm], refs[nm]
        red_ref, redq_ref = refs[nm + 1:nm + 3]
        mat_refs = refs[nm + 3:2 * nm + 3]
        dmod_ref, gab_ref, cpart_ref, dmc_s = refs[2 * nm + 3:]
        red, redq = vp_ref[0], vq_ref[0]
        for i in range(1, N_DEV):
            red = red + vp_ref[i]
            redq = redq + vq_ref[i]
        red_ref[...] = red
        redq_ref[...] = redq
        for mp_ref, mat_ref in zip(mp_refs, mat_refs):
            mat = mp_ref[0].astype(F32)
            for i in range(1, mp_ref.shape[0]):
                mat = mat + mp_ref[i].astype(F32)
            mat_ref[...] = mat
        for e in range(N_DEV):
            dmod_ref[e:e + 1, 0:D] = vp_ref[e, R_SH_X:R_SH_X + 1, :]
            dmod_ref[e:e + 1, D:2 * D] = vp_ref[e, R_SC_X:R_SC_X + 1, :]
            dmod_ref[e:e + 1, 2 * D:3 * D] = vp_ref[e, R_GATE:R_GATE + 1, :]
        dmod_ref[8:9, 0:D] = red[R_SH_C:R_SH_C + 1, :]
        dmod_ref[8:9, D:2 * D] = red[R_SC_C:R_SC_C + 1, :]
        dmod_ref[8:9, 2 * D:3 * D] = jnp.zeros((1, D), F32)
        dmod_ref[9:16, :] = jnp.zeros((7, 3 * D), F32)
        gab_ref[:, 0:D] = red[R_SH_X:R_SH_X + 1, :] + red[R_SH_C:R_SH_C + 1, :]
        gab_ref[:, D:2 * D] = red[R_SC_X:R_SC_X + 1, :] + red[R_SC_C:R_SC_C + 1, :]
        gab_ref[:, 2 * D:3 * D] = red[R_GATE:R_GATE + 1, :]
        dmc_s[...] = jnp.broadcast_to(dmod_ref[8:9, :], (8, 3 * D))
        off = pl.multiple_of(me_ref[0] * nloc, 128)
        cpart_ref[...] = _dot_nt(dmc_s[:, pl.ds(off, nloc)], w_ref[...])

    return _call(
        body, name="reduce_small",
        out_shape=[jax.ShapeDtypeStruct((PACK_ROWS, D), F32), jax.ShapeDtypeStruct((PACK128_ROWS, HD), F32)]
        + [jax.ShapeDtypeStruct(p.shape[1:], F32) for p in mat_parts]
        + [jax.ShapeDtypeStruct((16, 3 * D), F32), jax.ShapeDtypeStruct((1, 3 * D), F32),
           jax.ShapeDtypeStruct((8, D), F32)],
        in_specs=[pl.BlockSpec(memory_space=pltpu.SMEM)] + [VMEM] * (nm + 3), out_specs=[VMEM] * (nm + 5),
        scratch_shapes=[pltpu.VMEM((8, 3 * D), F32)], compiler_params=_params(),
    )(me, vp_all, vq_all, *mat_parts, ada_w)


def _adamw_ada(c_all, c_ctx, dmod, w, m, v, me):
    nloc = w.shape[1]

    def body(me_ref, c_ref, cc_ref, dm_ref, w_ref, m_ref, v_ref, g_ref, d_ref, mo_ref, vo_ref):
        off = pl.multiple_of(me_ref[0] * nloc, 128)
        dm = dm_ref[:, pl.ds(off, nloc)]
        sx, _ = _silu_and_grad(c_ref[...])
        sc, _ = _silu_and_grad(cc_ref[...])
        g = _dot_tn(sx, dm[0:8, :]) + _dot_tn(jnp.broadcast_to(sc, (8, D)), dm[8:16, :])
        g_ref[...] = g
        d_ref[...], mo_ref[...], vo_ref[...] = _adamw(w_ref[...], g, m_ref[...], v_ref[...])

    sds = jax.ShapeDtypeStruct(w.shape, F32)
    return _call(
        body, name="adamw_ada_w", out_shape=[sds] * 4,
        in_specs=[pl.BlockSpec(memory_space=pltpu.SMEM)] + [VMEM] * 6, out_specs=[VMEM] * 4,
        compiler_params=_params(),
    )(me, c_all, c_ctx, dmod, w, m, v)


_SMALL = ("c_ctx", "ada_b", "norm_g", "conv_w", "conv_b", "lru_wa", "lru_ba", "lru_wx", "lru_bx", "lru_lambda",
          "sgu_ln_g", "sgu_ln_b", "sgu_w", "sgu_b", "final_g")


def _adamw_small(red, redq, mats, cparts, gab, ws, ms, vs, me):
    n = len(_SMALL)

    def body(me_ref, red_ref, redq_ref, wa_ref, wx_ref, sw_ref, cp_ref, gab_ref, *refs):
        w_refs, m_refs, v_refs = refs[:n], refs[n:2 * n], refs[2 * n:3 * n]
        outs = refs[3 * n:]
        off = pl.multiple_of(me_ref[0] * HD, 128)

        def row(r, k=1):
            return red_ref[r:r + k, :]

        cc = w_refs[0][...]
        dcc = cp_ref[0, 0:1, :]
        for i in range(1, N_DEV):
            dcc = dcc + cp_ref[i, 0:1, :]
        grads = dict(
            c_ctx=dcc * _silu_and_grad(cc)[1], ada_b=gab_ref[...], norm_g=row(R_NG_X) + row(R_NG_C),
            conv_w=red_ref[R_CW:R_CW + CONV_W, pl.ds(off, HD)], conv_b=row(R_CB),
            lru_wa=wa_ref[...], lru_ba=redq_ref[Q_BA:Q_BA + 2 * HEADS, :], lru_wx=wx_ref[...],
            lru_bx=redq_ref[Q_BX:Q_BX + 2 * HEADS, :], lru_lambda=red_ref[R_LAM:R_LAM + 2, pl.ds(off, HD)],
            sgu_ln_g=row(R_LN_G), sgu_ln_b=row(R_LN_B), sgu_w=sw_ref[...],
            sgu_b=redq_ref[Q_SGU_B:Q_SGU_B + HEADS, :], final_g=row(R_FINAL_G))
        for j, name in enumerate(_SMALL):
            g = grads[name]
            outs[j][...] = g
            outs[n + j][...], outs[2 * n + j][...], outs[3 * n + j][...] = _adamw(w_refs[j][...], g, m_refs[j][...],
                                                                                 v_refs[j][...])

    sds = [jax.ShapeDtypeStruct(ws[k].shape, F32) for k in _SMALL]
    outs = _call(
        body, name="adamw_small", out_shape=sds * 4,
        in_specs=[pl.BlockSpec(memory_space=pltpu.SMEM)] + [VMEM] * (7 + 3 * n), out_specs=[VMEM] * (4 * n),
        compiler_params=_params(),
    )(me, red, redq, *mats, cparts, gab, *[ws[k] for k in _SMALL], *[ms[k] for k in _SMALL],
      *[vs[k] for k in _SMALL])
    return [dict(zip(_SMALL, outs[i * n:(i + 1) * n])) for i in range(4)]


def kernel(x, c, ctx, c_ctx, ada_w, ada_b, norm_g, w_in, conv_w, conv_b, lru_wa, lru_ba, lru_wx, lru_bx, lru_lambda, sgu_ln_g, sgu_ln_b, sgu_w, sgu_b, w_out, final_g, loss_target, m_c_ctx, m_ada_w, m_ada_b, m_norm_g, m_w_in, m_conv_w, m_conv_b, m_lru_wa, m_lru_ba, m_lru_wx, m_lru_bx, m_lru_lambda, m_sgu_ln_g, m_sgu_ln_b, m_sgu_w, m_sgu_b, m_w_out, m_final_g, v_c_ctx, v_ada_w, v_ada_b, v_norm_g, v_w_in, v_conv_w, v_conv_b, v_lru_wa, v_lru_ba, v_lru_wx, v_lru_bx, v_lru_lambda, v_sgu_ln_g, v_sgu_ln_b, v_sgu_w, v_sgu_b, v_w_out, v_final_g):
    args = dict(locals())
    me_s = 4 * lax.axis_index("x") + 2 * lax.axis_index("y") + lax.axis_index("c")
    me = me_s.astype(jnp.int32).reshape(1)
    xr, ctxr, tgt = x[0], ctx[0], loss_target[0]
    cc = c_ctx.reshape(1, D)
    nw = 2 * HEADS * HD
    view = dict(c_ctx=(1, D), ada_b=(1, 3 * D), norm_g=(1, D), conv_w=(CONV_W, HD), conv_b=(1, D), lru_wa=(nw, HD),
                lru_ba=(2 * HEADS, HD), lru_wx=(nw, HD), lru_bx=(2 * HEADS, HD), lru_lambda=(2, HD), sgu_ln_g=(1, D),
                sgu_ln_b=(1, D), sgu_w=(HEADS * CHUNK, CHUNK), sgu_b=(HEADS, CHUNK), final_g=(1, D))

    zx, hn, w_full, w_out_b, modx, modc, c_all, cw_full, lam_full = _front_project(
        xr, c, cc, ada_w[0], ada_b, norm_g, w_in[0], w_out[0], conv_w[0], lru_lambda[0], me)
    zc, hnc = _project(ctxr, modc, norm_g, w_full, D, LC, "project_ctx")
    ba, bx = lru_ba.reshape(view["lru_ba"]), lru_bx.reshape(view["lru_bx"])
    yl, wout_all = _lru_forward(zx, zc, cw_full, conv_b, lru_wa[0], lru_wx[0], ba, bx, lam_full, [w_out_b], ["ag"])
    wout_full = wout_all.reshape(D_MIX, D)
    ws_b = sgu_w[0].astype(BF16)
    dz, dyl, dxn, ycat, dob, dws, dbst, mvec = _mixer_loss(
        xr, tgt, zx, yl, modx, final_g.reshape(1, D), sgu_ln_g, sgu_ln_b, ws_b, jnp.swapaxes(ws_b, 1, 2),
        sgu_b[0].T, wout_full, 256)

    wout_sums = _grad_w(ycat, dob, None, None, L, "grad_w_out", D, 0, 1, "rows")
    rest_sums = _grad_w(hn, dz, None, None, L, "grad_w_in_rest", 2 * W_IN_SHARD, 1, 3, "cols")
    dz, dxac, dwa, dwx, dba, dbx, dlam, dcw, dcb, win_parts, wout_parts, _, _ = _lru_backward(
        zx, zc, dyl, dz, cw_full, conv_b, lru_wa[0], lru_wx[0], ba, bx, lam_full, [rest_sums, wout_sums],
        first_chips=[1, 0])
    first_sums = _grad_w(hn, dz, hnc, dxac, L, "grad_w_in_first", 2 * W_IN_SHARD, 0, 1, "cols")
    mats = [dwa.reshape(N_DEV, nw // N_DEV, HD), dwx.reshape(N_DEV, nw // N_DEV, HD), dws]
    mat_sums = _reduce2_local(mats, ["a2a"] * 3, me, "reduce_mat", out_dtype=BF16)
    gx, xvec, win_parts, *mat_parts = _grad_rows(
        xr, dz, w_full, modx, norm_g, dxn, D_IN, 256, "grad_rows_x", chip_sums=[first_sums, *mat_sums],
        first_chips=[0, 0, 0, 0], dests=[win_parts, None, None, None])[:6]
    (cvec,) = _grad_rows(ctxr, dxac, w_full, modc, norm_g, None, D, LC, "grad_rows_ctx")
    pack = jnp.concatenate([mvec[0:5], xvec[0:3], cvec[0:3], dlam, dcw, dcb,
                            jnp.zeros((PACK_ROWS - R_CB - 1, D), F32)], axis=0)
    pack128 = jnp.concatenate([dba, dbx, dbst.T], axis=0)
    vp_all, vq_all = _gather2([pack, pack128], ["ag", "ag"], "gather_pack")
    red, redq, *rest = _reduce_small(vp_all, vq_all, mat_parts, ada_w[0], me)
    mat_pieces, (dmod, gab, cpart) = rest[:3], rest[3:]
    *mats_all, cparts = _gather2([*mat_pieces, cpart], ["ag"] * 4, "gather_small")

    g_w_in, d_w_in, nm_w_in, nv_w_in = _adamw_reduced(win_parts, w_in[0], m_w_in[0], v_w_in[0], 256, "adamw_w_in")
    g_w_out, d_w_out, nm_w_out, nv_w_out = _adamw_reduced(wout_parts, w_out[0], m_w_out[0], v_w_out[0], 128,
                                                          "adamw_w_out")
    g_ada, d_ada, nm_ada, nv_ada = _adamw_ada(c_all, cc, dmod, ada_w[0], m_ada_w[0], v_ada_w[0], me)
    ws = {k: args[k].reshape(view[k]) for k in _SMALL}
    ms = {k: args["m_" + k].reshape(view[k]) for k in _SMALL}
    vs = {k: args["v_" + k].reshape(view[k]) for k in _SMALL}
    small = _adamw_small(red, redq, [m.reshape(-1, HD) for m in mats_all], cparts, gab, ws, ms, vs, me)
    big = dict(w_in=(g_w_in, d_w_in, nm_w_in, nv_w_in), w_out=(g_w_out, d_w_out, nm_w_out, nv_w_out),
               ada_w=(g_ada, d_ada, nm_ada, nv_ada))

    loss = red[R_LOSS, 0]
    names = ("c_ctx", "ada_w", "ada_b", "norm_g", "w_in", "conv_w", "conv_b", "lru_wa", "lru_ba", "lru_wx", "lru_bx",
             "lru_lambda", "sgu_ln_g", "sgu_ln_b", "sgu_w", "sgu_b", "w_out", "final_g")
    outs = [loss, gx.reshape(x.shape)]
    for kind in range(4):
        for k in names:
            val = big[k][kind] if k in big else small[kind][k]
            outs.append(val.reshape(args[k].shape))
    return tuple(outs)
```

```python
import functools

import jax
import jax.numpy as jnp
from jax import lax
from jax.experimental import pallas as pl
from jax.experimental.pallas import tpu as pltpu

F32 = jnp.float32
BF16 = jnp.bfloat16

N_DEV = 8
D = 1024
L = 2048
LC = 256
HEADS = 8
HD = 128
CHUNK = 128
D_IN = 5 * D
W_IN_SHARD = D_IN // N_DEV
ROWS = 256
D_MIX = 2 * D
CONV_W = 4
LRU_C = 8.0
NORM_EPS = 1e-6
LN_EPS = 1e-5
ADAM_LR, ADAM_B1, ADAM_B2, ADAM_EPS, ADAM_WD, ADAM_STEP = 0.001, 0.9, 0.999, 1e-08, 0.01, 10

VMEM_LIMIT = 56 * 1024 * 1024

HBM = pl.BlockSpec(memory_space=pltpu.HBM)
VMEM = pl.BlockSpec(memory_space=pltpu.VMEM)
MESH = pl.DeviceIdType.MESH


def _call(body, **kw):
    return pl.pallas_call(body, **kw)


def _params(*sem):
    return pltpu.CompilerParams(dimension_semantics=sem, vmem_limit_bytes=VMEM_LIMIT)


def _sigmoid(x):
    return 0.5 * jnp.tanh(0.5 * x) + 0.5


def _silu_and_grad(x):
    s = _sigmoid(x)
    return x * s, s * (1.0 + x * (1.0 - s))


_G0 = 0.7978845608028654
_G1 = 0.044715


def _gelu_and_grad(x):
    x2 = x * x
    t = jnp.tanh(_G0 * (x + _G1 * x * x2))
    cdf = 0.5 * (1.0 + t)
    return x * cdf, cdf + 0.5 * x * (1.0 - t * t) * (_G0 * (1.0 + 3.0 * _G1 * x2))


def _gelu(x):
    return 0.5 * x * (1.0 + jnp.tanh(_G0 * (x + _G1 * x * x * x)))


def _softplus(z):
    t = jnp.exp(-jnp.abs(z))
    u = 1.0 + t
    log1p = jnp.where(u == 1.0, t, jnp.log(u) * t / jnp.where(u == 1.0, 1.0, u - 1.0))
    return jnp.maximum(z, 0.0) + log1p


def _dot(a, b):
    return jnp.dot(a, b, preferred_element_type=F32)


def _dot_nt(a, b):
    return lax.dot_general(a, b, (((1,), (1,)), ((), ())), preferred_element_type=F32)


def _dot_tn(a, b):
    return lax.dot_general(a, b, (((0,), (0,)), ((), ())), preferred_element_type=F32)


def _rows(shape):
    return lax.broadcasted_iota(jnp.int32, shape, 0)


def _shift_down(x, first):
    y = pltpu.roll(x, 1, 0)
    head = jnp.where(_rows((8, x.shape[1])) == 0, first, y[0:8])
    return jnp.concatenate([head, y[8:]], axis=0)


def _shift_up(x, last):
    n = x.shape[0]
    y = pltpu.roll(x, n - 1, 0)
    tail = jnp.where(_rows((8, x.shape[1])) == 7, last, y[n - 8:])
    return jnp.concatenate([y[:n - 8], tail], axis=0)


def _gather2(arrays, modes, name):
    n = len(arrays)

    def body(*refs):
        start, forward, finish = _gather2_ops(refs[:n], refs[n:2 * n], modes, *refs[2 * n:])
        start()
        forward()
        finish()

    return _call(
        body, name=name, out_shape=_gather2_shapes(arrays, modes), in_specs=[HBM] * n, out_specs=[HBM] * n,
        scratch_shapes=_gather2_sems(n), compiler_params=pltpu.CompilerParams(has_side_effects=True),
    )(*[pltpu.with_memory_space_constraint(a, pltpu.HBM) for a in arrays])


def _gather2_shapes(arrays, modes):
    return [jax.ShapeDtypeStruct((N_DEV,) + a.shape if m == "ag" else (a.shape[0], N_DEV * a.shape[1]), a.dtype)
            for a, m in zip(arrays, modes)]


def _gather2_sems(n):
    return [pltpu.SemaphoreType.DMA((n, N_DEV - 1)), pltpu.SemaphoreType.DMA((n, N_DEV - 1)),
            pltpu.SemaphoreType.DMA((n,))]


def _gather2_ops(ins, outs, modes, send_sems, recv_sems, local_sems):
    n = len(ins)
    x, y, c = lax.axis_index("x"), lax.axis_index("y"), lax.axis_index("c")
    me, sibling = (x, y, c), (x, y, 1 - c)
    chips = [(x ^ (k >> 1), y ^ (k & 1)) for k in (1, 2, 3)]

    def slot(j, px, py, pc):
        dev = 4 * px + 2 * py + pc
        if modes[j] == "agc":
            w = ins[j].shape[1]
            return outs[j].at[:, pl.ds(pl.multiple_of(dev * w, 128), w)]
        return outs[j].at[dev]

    def copy(j, k, block, to, src=None):
        return pltpu.make_async_remote_copy(
            src_ref=slot(j, *block) if src is None else src, dst_ref=slot(j, *block),
            send_sem=send_sems.at[j, k], recv_sem=recv_sems.at[j, k], device_id=to, device_id_type=MESH)

    def own(j):
        return pltpu.make_async_copy(ins[j], slot(j, *me), local_sems.at[j])

    def first(j):
        return [copy(j, 0, me, sibling, src=ins[j])] + [copy(j, 1 + i, me, (*chip, c), src=ins[j])
                                                        for i, chip in enumerate(chips)]

    def passed(j, i):
        return copy(j, 4 + i, (*chips[i], c), sibling)

    def start():
        for j in range(n):
            own(j).start()
            for cp in first(j):
                cp.start()

    def forward():
        for i, chip in enumerate(chips):
            for j in range(n):
                copy(j, 1 + i, (*chip, c), me).wait_recv()
                passed(j, i).start()

    def finish():
        for j in range(n):
            copy(j, 0, sibling, me).wait_recv()
            for i, chip in enumerate(chips):
                copy(j, 4 + i, (*chip, 1 - c), me).wait_recv()
            for cp in first(j) + [passed(j, i) for i in range(3)]:
                cp.wait_send()
            own(j).wait()

    return start, forward, finish


def _sibling_barrier():
    sem = pltpu.get_barrier_semaphore()
    sibling = (lax.axis_index("x"), lax.axis_index("y"), 1 - lax.axis_index("c"))
    pl.semaphore_signal(sem, inc=1, device_id=sibling, device_id_type=MESH)
    pl.semaphore_wait(sem, 1)


def _reduce2_local(arrays, modes, me, name, barrier_id, counts=None, out_dtype=None):
    n = len(arrays)
    counts = counts or [4] * n
    shapes = [(a.shape[1], a.shape[2]) if m == "a2a" else (a.shape[0], a.shape[1] // (2 * cnt))
              for a, m, cnt in zip(arrays, modes, counts)]
    staged = [jax.ShapeDtypeStruct((cnt,) + s, a.dtype) for s, a, cnt in zip(shapes, arrays, counts)]

    def piece(ref, mode, dev, w):
        return ref.at[dev] if mode == "a2a" else ref.at[:, pl.ds(pl.multiple_of(dev * w, 128), w)]

    def to_sibling(*refs):
        ins, outs = refs[:n], refs[n:2 * n]
        send_sems, recv_sems = refs[2 * n:]
        x, y, c = lax.axis_index("x"), lax.axis_index("y"), lax.axis_index("c")
        _sibling_barrier()
        copies = []
        for j in range(n):
            for q in range(counts[j]):
                cp = pltpu.make_async_remote_copy(
                    src_ref=piece(ins[j], modes[j], 2 * q + (1 - c), shapes[j][1]), dst_ref=outs[j].at[q],
                    send_sem=send_sems.at[j, q], recv_sem=recv_sems.at[j, q], device_id=(x, y, 1 - c),
                    device_id_type=MESH)
                cp.start()
                copies.append(cp)
        for cp in copies:
            cp.wait()

    stage = _call(
        to_sibling, name=name + "_d2d", out_shape=staged, in_specs=[HBM] * n, out_specs=[HBM] * n,
        scratch_shapes=[pltpu.SemaphoreType.DMA((n, 4)), pltpu.SemaphoreType.DMA((n, 4))],
        compiler_params=pltpu.CompilerParams(has_side_effects=True, collective_id=barrier_id),
    )(*[pltpu.with_memory_space_constraint(a, pltpu.HBM) for a in arrays])

    def add(me_ref, *refs):
        del me_ref
        own, got, outs = refs[:n], refs[n:2 * n], refs[2 * n:]
        for j in range(n):
            mine = own[j][0] if modes[j] == "a2a" else own[j][...]
            outs[j][0] = (mine.astype(F32) + got[j][0].astype(F32)).astype(outs[j].dtype)

    in_specs, slot_specs = [], []
    for (r, w), m, cnt in zip(shapes, modes, counts):
        if m == "a2a":
            in_specs.append(pl.BlockSpec(
                (1, r, w), lambda q, me_ref, cnt=cnt: (2 * jnp.minimum(q, cnt - 1) + me_ref[0] % 2, 0, 0)))
        else:
            in_specs.append(pl.BlockSpec(
                (r, w), lambda q, me_ref, cnt=cnt: (0, 2 * jnp.minimum(q, cnt - 1) + me_ref[0] % 2)))
        slot_specs.append(pl.BlockSpec((1, r, w), lambda q, me_ref, cnt=cnt: (jnp.minimum(q, cnt - 1), 0, 0)))
    return _call(
        add, name=name + "_add",
        out_shape=[jax.ShapeDtypeStruct(s.shape, out_dtype or s.dtype) for s in staged],
        grid_spec=pltpu.PrefetchScalarGridSpec(num_scalar_prefetch=1, grid=(max(counts),),
                                               in_specs=in_specs + slot_specs, out_specs=slot_specs),
        compiler_params=_params("arbitrary"),
    )(me, *arrays, *stage)


def _chips_sems(n):
    return [pltpu.SemaphoreType.DMA((n, 6)), pltpu.SemaphoreType.DMA((n, 6)), pltpu.SemaphoreType.DMA((n,))]


def _chips_stage_shapes(chip_sums):
    return [jax.ShapeDtypeStruct((2, a.shape[1] // 2, a.shape[2]), a.dtype) for a in chip_sums]


def _chips_ops(ins, outs, stages, send_sems, recv_sems, local_sems, first_chips=None):
    x, y, c = lax.axis_index("x"), lax.axis_index("y"), lax.axis_index("c")
    qm = 2 * x + y
    first_chips = first_chips or [0] * len(ins)

    def owns(j, chip):
        lo, cnt = first_chips[j], ins[j].shape[0]
        if lo == 0 and cnt == 4:
            return None
        return jnp.logical_and(chip >= lo, chip < lo + cnt)

    def guarded(cond, fn):
        if cond is None:
            fn()
        else:
            pl.when(cond)(fn)

    def slot(j, chip):
        return jnp.clip(chip - first_chips[j], 0, ins[j].shape[0] - 1)

    def half(j, i):
        h = ins[j].shape[1] // 2
        return pl.ds(i * h, h)

    def copy(j, sem, src, dst, k):
        return pltpu.make_async_remote_copy(
            src_ref=src, dst_ref=dst, send_sem=send_sems.at[j, sem], recv_sem=recv_sems.at[j, sem],
            device_id=(x ^ (k >> 1), y ^ (k & 1), c), device_id_type=MESH)

    def direct(j, k):
        return copy(j, k - 1, ins[j].at[slot(j, qm ^ k)], outs[j].at[qm], k)

    def first_hop(j, k):
        return copy(j, 1 + k, ins[j].at[slot(j, qm ^ 3), half(j, k - 1)], stages[j].at[k - 1], k)

    def second_hop(j, k):
        return copy(j, 3 + k, stages[j].at[2 - k], outs[j].at[qm ^ (3 - k), half(j, 2 - k)], k)

    def local(j):
        return pltpu.make_async_copy(ins[j].at[slot(j, qm)], outs[j].at[qm], local_sems.at[j])

    def start():
        for j in range(len(ins)):
            for k in (1, 2):
                guarded(owns(j, qm ^ 3), lambda j=j, k=k: first_hop(j, k).start())
        for j in range(len(ins)):
            for k in (1, 2):
                guarded(owns(j, qm ^ k), lambda j=j, k=k: direct(j, k).start())
            guarded(owns(j, qm), lambda j=j: local(j).start())

    def forward():
        for j in range(len(ins)):
            for k in (1, 2):
                def pass_on(j=j, k=k):
                    first_hop(j, 3 - k).wait_recv()
                    second_hop(j, k).start()
                guarded(owns(j, qm ^ k), pass_on)

    def finish():
        for j in range(len(ins)):
            for k in (1, 2):
                guarded(owns(j, qm ^ k), lambda j=j, k=k: direct(j, k).wait_send())
                guarded(owns(j, qm ^ k), lambda j=j, k=k: second_hop(j, k).wait_send())
                guarded(owns(j, qm ^ 3), lambda j=j, k=k: first_hop(j, k).wait_send())
                guarded(owns(j, qm), lambda j=j, k=k: direct(j, k).wait_recv())
                guarded(owns(j, qm), lambda j=j, k=k: second_hop(j, k).wait_recv())
            guarded(owns(j, qm), lambda j=j: local(j).wait())

    return start, forward, finish


def _front(c, c_ctx, ada_w, ada_b, w_in, w_out, me):
    nloc = ada_w.shape[1]

    def body(me_ref, c_ref, cc_ref, aw_ref, ab_ref, win_ref, wout_ref,
             wfull_ref, woutb_ref, modx_ref, modc_ref, call_ref,
             wb_s, part_s, parts_s, w_send, w_recv, w_local, s_send, s_recv):
        x, y, cidx = lax.axis_index("x"), lax.axis_index("y"), lax.axis_index("c")
        me = me_ref[0]
        wb_s[...] = win_ref[...].astype(BF16)
        woutb_ref[...] = wout_ref[...].astype(BF16)
        start, forward, finish = _gather2_ops([wb_s], [wfull_ref], ["agc"], w_send, w_recv, w_local)
        start()

        def small_gather(src, my_slot, stage):
            copies = []
            for k in range(1, N_DEV):
                peer = (x ^ (k >> 2), y ^ ((k >> 1) & 1), cidx ^ (k & 1))
                cp = pltpu.make_async_remote_copy(src_ref=src, dst_ref=my_slot, send_sem=s_send.at[stage, k - 1],
                                                  recv_sem=s_recv.at[stage, k - 1], device_id=peer,
                                                  device_id_type=MESH)
                cp.start()
                copies.append(cp)
            pltpu.sync_copy(src, my_slot)
            for cp in copies:
                cp.wait()

        small_gather(c_ref, call_ref.at[pl.ds(me, 1), :], 0)
        off = pl.multiple_of(me * nloc, 128)
        b = ab_ref[:, pl.ds(off, nloc)]
        w = aw_ref[...]
        sx, _ = _silu_and_grad(call_ref[...])
        sc, _ = _silu_and_grad(jnp.broadcast_to(cc_ref[...], (8, D)))
        part_s[0:8, :] = _dot(sx, w) + b
        part_s[8:16, :] = _dot(sc, w) + b
        small_gather(part_s, parts_s.at[me], 1)
        mine = _rows((16, nloc)) == me
        for j in range(N_DEV):
            pj = parts_s[j]
            modx_ref[:, j * nloc:(j + 1) * nloc] = jnp.sum(jnp.where(mine, pj, 0.0), axis=0, keepdims=True)
            modc_ref[:, j * nloc:(j + 1) * nloc] = pj[8:9, :]
        forward()
        finish()

    return _call(
        body, name="front",
        out_shape=[jax.ShapeDtypeStruct((D, D_IN), BF16), jax.ShapeDtypeStruct(w_out.shape, BF16),
                   jax.ShapeDtypeStruct((1, 3 * D), F32), jax.ShapeDtypeStruct((1, 3 * D), F32),
                   jax.ShapeDtypeStruct((N_DEV, D), F32)],
        in_specs=[pl.BlockSpec(memory_space=pltpu.SMEM)] + [VMEM] * 6, out_specs=[HBM, VMEM, VMEM, VMEM, VMEM],
        scratch_shapes=[pltpu.VMEM(w_in.shape, BF16), pltpu.VMEM((16, nloc), F32),
                        pltpu.VMEM((N_DEV, 16, nloc), F32)] + _gather2_sems(1) +
                       [pltpu.SemaphoreType.DMA((2, N_DEV - 1)), pltpu.SemaphoreType.DMA((2, N_DEV - 1))],
        compiler_params=pltpu.CompilerParams(vmem_limit_bytes=VMEM_LIMIT, has_side_effects=True),
    )(me, c, c_ctx, ada_w, ada_b, w_in, w_out)


ARRIVAL = (0, 1, 2, 4, 3, 5, 6, 7)


def _front_project(xr, c, c_ctx, ada_w, ada_b, ng, w_in, w_out, cw, lam, me):
    nloc = ada_w.shape[1]
    ws = W_IN_SHARD
    order = me[0] ^ jnp.asarray(ARRIVAL, jnp.int32)

    def body(ord_ref, x_ref, c_ref, cc_ref, aw_ref, ab_ref, ng_ref, win_ref, wout_ref, cw_ref, lam_ref,
             z_ref, hn_ref, wfull_ref, woutb_ref, modx_ref, modc_ref, call_ref, cwf_ref, lamf_ref,
             wv, call_s, part_s, parts_s, w_send, w_recv, hbm_sems, s_send, s_recv, g_send, g_recv, g_local):
        t = pl.program_id(0)
        x, y, cidx = lax.axis_index("x"), lax.axis_index("y"), lax.axis_index("c")
        me_i = ord_ref[0]
        sibling = (x, y, 1 - cidx)
        chips = [(x ^ (k >> 1), y ^ (k & 1)) for k in (1, 2, 3)]
        g_start, g_pass, g_finish = _gather2_ops([cw_ref, lam_ref], [cwf_ref, lamf_ref], ["agc", "agc"],
                                                 g_send, g_recv, g_local)

        def shard_copy(k, px, py, pc, to, half=None):
            slot = wv.at[4 * px + 2 * py + pc]
            if half is not None:
                slot = slot.at[pl.ds(half * (D // 2), D // 2), :]
            return pltpu.make_async_remote_copy(src_ref=slot, dst_ref=slot, send_sem=w_send.at[k],
                                                recv_sem=w_recv.at[k], device_id=to, device_id_type=MESH)

        def small_gather(src, my_slot, stage):
            copies = []
            for k in range(1, N_DEV):
                peer = (x ^ (k >> 2), y ^ ((k >> 1) & 1), cidx ^ (k & 1))
                cp = pltpu.make_async_remote_copy(src_ref=src, dst_ref=my_slot, send_sem=s_send.at[stage, k - 1],
                                                  recv_sem=s_recv.at[stage, k - 1], device_id=peer,
                                                  device_id_type=MESH)
                cp.start()
                copies.append(cp)
            pltpu.sync_copy(src, my_slot)
            return copies

        def finish_small(copies):
            for cp in copies:
                cp.wait()

        def to_neighbours(half):
            for i in (0, 1):
                shard_copy(1 + i, x, y, cidx, (*chips[i], cidx), half=half).start()

        @pl.when(t == 0)
        def _():
            g_start()
            wv[me_i] = win_ref[...].astype(BF16)
            woutb_ref[...] = wout_ref[...].astype(BF16)
            shard_copy(0, x, y, cidx, sibling).start()
            finish_small(small_gather(c_ref, call_s.at[pl.ds(me_i, 1), :], 0))
            to_neighbours(0)
            call_ref[...] = call_s[...]
            off = pl.multiple_of(me_i * nloc, 128)
            b = ab_ref[:, pl.ds(off, nloc)]
            w = aw_ref[...]
            sx, _ = _silu_and_grad(call_s[...])
            sc, _ = _silu_and_grad(jnp.broadcast_to(cc_ref[...], (8, D)))
            part_s[0:8, :] = _dot(sx, w) + b
            part_s[8:16, :] = _dot(sc, w) + b
            parts_sent = small_gather(part_s, parts_s.at[me_i], 1)
            to_neighbours(1)
            finish_small(parts_sent)
            mine = _rows((16, nloc)) == me_i
            for j in range(N_DEV):
                pj = parts_s[j]
                modx_ref[:, j * nloc:(j + 1) * nloc] = jnp.sum(jnp.where(mine, pj, 0.0), axis=0, keepdims=True)
                modc_ref[:, j * nloc:(j + 1) * nloc] = pj[8:9, :]
            shift, scale1, ngv = modx_ref[:, 0:D], 1.0 + modx_ref[:, D:2 * D], ng_ref[...]
            for r in range(L // ROWS):
                rsl = slice(r * ROWS, (r + 1) * ROWS)
                xv = x_ref[rsl, :]
                rs = lax.rsqrt(jnp.mean(xv * xv, axis=-1, keepdims=True) + NORM_EPS)
                hn_ref[rsl, :] = ((xv * rs * ngv) * scale1 + shift).astype(BF16)

        @pl.when(t == 1)
        def _():
            shard_copy(0, x, y, 1 - cidx, sibling).wait_recv()
            g_pass()

        for i in (0, 1):
            @pl.when(t == ARRIVAL.index((2, 4)[i]))
            def _(i=i):
                shard_copy(1 + i, *chips[i], cidx, sibling).wait_recv()
                shard_copy(4 + i, *chips[i], cidx, sibling).start()
                shard_copy((7, 3)[i], *chips[i], cidx, (*chips[1 - i], cidx), half=i).start()

        @pl.when(t == ARRIVAL.index(6))
        def _():
            shard_copy(3, *chips[2], cidx, sibling, half=1).wait_recv()
            shard_copy(7, *chips[2], cidx, sibling, half=0).wait_recv()
            shard_copy(6, *chips[2], cidx, sibling).start()

        for i in range(3):
            @pl.when(t == ARRIVAL.index((3, 5, 7)[i]))
            def _(i=i):
                shard_copy(4 + i, *chips[i], 1 - cidx, sibling).wait_recv()

        @pl.when(t == 2)
        def _():
            g_finish()

        dev = ord_ref[t]
        for r in range(L // (2 * ROWS)):
            rsl = slice(r * 2 * ROWS, (r + 1) * 2 * ROWS)
            z_ref[rsl, :] = _dot(hn_ref[rsl, :], wv[dev])
        col = pl.ds(pl.multiple_of(dev * ws, 128), ws)
        pltpu.make_async_copy(wv.at[dev], wfull_ref.at[:, col], hbm_sems.at[t]).start()

        @pl.when(t == N_DEV - 1)
        def _():
            for k in (0, 1, 2, 4, 5, 6):
                shard_copy(k, x, y, cidx, sibling).wait_send()
            for k in (3, 7):
                shard_copy(k, x, y, cidx, sibling, half=0).wait_send()
            for s in range(N_DEV):
                pltpu.make_async_copy(wv.at[0], wfull_ref.at[:, pl.ds(0, ws)], hbm_sems.at[s]).wait()

    const = lambda *shape: pl.BlockSpec(shape, lambda t, o: (0,) * len(shape))
    once = lambda *shape: pl.BlockSpec(shape, lambda t, o: (0,) * len(shape), pipeline_mode=pl.Buffered(1))
    return _call(
        body, name="front_project",
        out_shape=[jax.ShapeDtypeStruct((L, D_IN), F32), jax.ShapeDtypeStruct((L, D), BF16),
                   jax.ShapeDtypeStruct((D, D_IN), BF16), jax.ShapeDtypeStruct(w_out.shape, BF16),
                   jax.ShapeDtypeStruct((1, 3 * D), F32), jax.ShapeDtypeStruct((1, 3 * D), F32),
                   jax.ShapeDtypeStruct((N_DEV, D), F32), jax.ShapeDtypeStruct((CONV_W, D), F32),
                   jax.ShapeDtypeStruct((2, D), F32)],
        grid_spec=pltpu.PrefetchScalarGridSpec(
            num_scalar_prefetch=1, grid=(N_DEV,),
            in_specs=[once(L, D), const(1, D), const(1, D), once(D, nloc), const(1, 3 * D), const(1, D),
                      once(D, ws), once(*w_out.shape), HBM, HBM],
            out_specs=[pl.BlockSpec((L, ws), lambda t, o: (0, o[t])), const(L, D), HBM, const(*w_out.shape),
                       const(1, 3 * D), const(1, 3 * D), const(N_DEV, D), HBM, HBM],
            scratch_shapes=[pltpu.VMEM((N_DEV, D, ws), BF16), pltpu.VMEM((N_DEV, D), F32), pltpu.VMEM((16, nloc), F32),
                            pltpu.VMEM((N_DEV, 16, nloc), F32), pltpu.SemaphoreType.DMA((8,)),
                            pltpu.SemaphoreType.DMA((8,)), pltpu.SemaphoreType.DMA((N_DEV,)),
                            pltpu.SemaphoreType.DMA((2, N_DEV - 1)), pltpu.SemaphoreType.DMA((2, N_DEV - 1))]
            + _gather2_sems(2)),
        compiler_params=pltpu.CompilerParams(dimension_semantics=("arbitrary",), vmem_limit_bytes=VMEM_LIMIT,
                                             has_side_effects=True),
    )(order, xr, c, c_ctx, ada_w, ada_b, ng, w_in, w_out, pltpu.with_memory_space_constraint(cw, pltpu.HBM),
      pltpu.with_memory_space_constraint(lam, pltpu.HBM))


def _project(xr, mod, ng, w, ncols, tm, name, gather=None, gather_modes=()):
    rows = xr.shape[0]
    steps = rows // tm
    ng_ = len(gather or ())

    def body(x_ref, sh_ref, sc_ref, ng_ref, w_ref, *rest):
        z_ref, hn_ref = rest[ng_:ng_ + 2]
        if ng_:
            start, forward, finish = _gather2_ops(rest[:ng_], rest[ng_ + 2:2 * ng_ + 2], gather_modes,
                                                  *rest[2 * ng_ + 2:])
            pl.when(pl.program_id(0) == 0)(start)
            pl.when(pl.program_id(0) == steps // 2)(forward)
        x = x_ref[...]
        rs = lax.rsqrt(jnp.mean(x * x, axis=-1, keepdims=True) + NORM_EPS)
        hn = (x * rs * ng_ref[...]) * (1.0 + sc_ref[...]) + sh_ref[...]
        hb = hn.astype(BF16)
        hn_ref[...] = hb
        for n in range(ncols // D):
            z_ref[:, n * D:(n + 1) * D] = _dot(hb, w_ref[:, n * D:(n + 1) * D])
        if ng_:
            pl.when(pl.program_id(0) == steps - 1)(finish)

    vec = pl.BlockSpec((1, D), lambda i: (0, 0))
    gathered = _gather2_shapes(gather, gather_modes) if ng_ else []
    return _call(
        body, name=name, grid=(steps,),
        out_shape=[jax.ShapeDtypeStruct((rows, ncols), F32), jax.ShapeDtypeStruct((rows, D), BF16)] + gathered,
        in_specs=[pl.BlockSpec((tm, D), lambda i: (i, 0)), vec, pl.BlockSpec((1, D), lambda i: (0, 1)), vec,
                  pl.BlockSpec((D, ncols), lambda i: (0, 0), pipeline_mode=pl.Buffered(1))] + [HBM] * ng_,
        out_specs=[pl.BlockSpec((tm, ncols), lambda i: (i, 0)), pl.BlockSpec((tm, D), lambda i: (i, 0))] + [HBM] * ng_,
        scratch_shapes=_gather2_sems(ng_) if ng_ else [],
        compiler_params=pltpu.CompilerParams(dimension_semantics=("arbitrary",), vmem_limit_bytes=VMEM_LIMIT,
                                             has_side_effects=bool(ng_)),
    )(xr, mod, mod, ng, w, *[pltpu.with_memory_space_constraint(a, pltpu.HBM) for a in gather or ()])


def _scan_pair(af_ref, uf_ref, hf_ref, h0f, ab_ref, ub_ref, hb_ref, h0b, t_len):
    span = 8 * SCAN_BLOCKS
    nit = t_len // span
    rows = _rows((8, HD))

    def local_scan(a, b, forward):
        for s in (1, 2, 4):
            sh = s if forward else 8 - s
            m = rows >= s if forward else rows < 8 - s
            b = a * jnp.where(m, pltpu.roll(b, sh, 0), 0.0) + b
            a = a * jnp.where(m, pltpu.roll(a, sh, 0), 1.0)
        return a, b

    def span_scan(a_ref, u_ref, h_ref, off, carry, forward):
        order = range(SCAN_BLOCKS) if forward else range(SCAN_BLOCKS - 1, -1, -1)
        last = slice(7, 8) if forward else slice(0, 1)
        for q in order:
            rs = pl.ds(off + 8 * q, 8)
            a, b = local_scan(a_ref[rs, :], u_ref[rs, :], forward)
            h_ref[rs, :] = b + a * carry
            carry = a[last, :] * carry + b[last, :]
        return carry

    def body(k, carry):
        cf, cb = carry
        cf = span_scan(af_ref, uf_ref, hf_ref, pl.multiple_of(k * span, span), cf, True)
        cb = span_scan(ab_ref, ub_ref, hb_ref, pl.multiple_of((nit - 1 - k) * span, span), cb, False)
        return cf, cb

    return lax.fori_loop(0, nit, body, (h0f, h0b))


SCAN_BLOCKS = 8


def _shifted(pad_ref, x, offsets, before=0.0, after=0.0):
    n = x.shape[0]
    pad_ref[0:8, :] = jnp.broadcast_to(jnp.asarray(before, F32), (8, x.shape[1]))
    pad_ref[8:8 + n, :] = x
    pad_ref[8 + n:16 + n, :] = jnp.broadcast_to(jnp.asarray(after, F32), (8, x.shape[1]))
    return [pad_ref[8 + o:8 + o + n, :] for o in offsets]


def _conv(xa, cw, cb, pad_ref):
    xm1, xp1, xp2 = _shifted(pad_ref, xa, (-1, 1, 2))
    return xm1 * cw[0:1, :] + xa * cw[1:2, :] + xp1 * cw[2:3, :] + xp2 * cw[3:4, :] + cb


def _gates(xc, wa, wx, ba, bx, nsp):
    xb = xc.astype(BF16)
    r = _sigmoid(_dot(xb, wa) + ba)
    i = _sigmoid(_dot(xb, wx) + bx)
    log_a = r * nsp
    a = jnp.exp(log_a)
    g2 = jnp.tanh(log_a) * (-1.0 - a * a)
    rg = lax.rsqrt(jnp.maximum(g2, 1e-30))
    return r, i, a, g2 * rg, rg


def _lru_param_specs():
    h4 = pl.BlockSpec((2, 1, HD, HD), lambda h: (0, h, 0, 0))
    v2 = pl.BlockSpec((2, HD), lambda h: (0, h))
    b16 = pl.BlockSpec((2 * HEADS, HD), lambda h: (0, 0))
    return dict(
        xa=pl.BlockSpec((L, HD), lambda h: (0, h)), xac=pl.BlockSpec((LC, HD), lambda h: (0, h)),
        cw=pl.BlockSpec((CONV_W, HD), lambda h: (0, h)), cb=pl.BlockSpec((1, HD), lambda h: (0, h)), h4=h4, v2=v2,
        b16=b16)


def _bias_row(ref, d):
    mask = _rows((2 * HEADS, HD)) == d * HEADS + pl.program_id(0)
    return jnp.sum(jnp.where(mask, ref[...], 0.0), axis=0, keepdims=True), mask


def _lru_forward(zx, zc, cw, cb, wa, wx, ba, bx, lam, gather, gather_modes):
    ng_ = len(gather)

    def body(xa_ref, xac_ref, cw_ref, cb_ref, wa_ref, wx_ref, ba_ref, bx_ref, lam_ref, *rest):
        yl_ref = rest[ng_]
        af, uf, hf, ab, ub, hb, pad_s = rest[2 * ng_ + 1:2 * ng_ + 8]
        start, pass_on, finish = _gather2_ops(rest[:ng_], rest[ng_ + 1:2 * ng_ + 1], gather_modes,
                                              *rest[2 * ng_ + 8:])
        pl.when(pl.program_id(0) == 0)(start)
        pl.when(pl.program_id(0) == HEADS // 2)(pass_on)
        pl.when(pl.program_id(0) == HEADS - 1)(finish)
        cwv, cbv = cw_ref[...], cb_ref[...]
        nsp = (-LRU_C) * _softplus(-lam_ref[...])

        def forward(xa, t_len, h0f, h0b):
            xc = _conv(xa, cwv, cbv, pad_s)
            for d, (a_ref, u_ref) in enumerate(((af, uf), (ab, ub))):
                _, i, a, gamma, _ = _gates(xc, wa_ref[d, 0].astype(BF16), wx_ref[d, 0].astype(BF16),
                                           _bias_row(ba_ref, d)[0], _bias_row(bx_ref, d)[0], nsp[d:d + 1, :])
                a_ref[0:t_len, :] = a
                u_ref[0:t_len, :] = gamma * (i * xc)
            return _scan_pair(af, uf, hf, h0f, ab, ub, hb, h0b, t_len)

        z = jnp.zeros((1, HD), F32)
        h0f, h0b = forward(xac_ref[...], LC, z, z)
        forward(xa_ref[...], L, h0f, h0b)
        yl_ref[...] = hf[...] + hb[...]

    s = _lru_param_specs()
    return _call(
        body, name="lru_forward", grid=(HEADS,),
        out_shape=[jax.ShapeDtypeStruct((L, D), F32)] + _gather2_shapes(gather, gather_modes),
        in_specs=[s["xa"], s["xac"], s["cw"], s["cb"], s["h4"], s["h4"], s["b16"], s["b16"], s["v2"]] + [HBM] * ng_,
        out_specs=[pl.BlockSpec((L, HD), lambda h: (0, h))] + [HBM] * ng_,
        scratch_shapes=[pltpu.VMEM((L, HD), F32)] * 6 + [pltpu.VMEM((L + 16, HD), F32)] + _gather2_sems(ng_),
        compiler_params=pltpu.CompilerParams(dimension_semantics=("arbitrary",), vmem_limit_bytes=VMEM_LIMIT,
                                             has_side_effects=True),
    )(zx, zc, cw, cb, wa, wx, ba, bx, lam, *[pltpu.with_memory_space_constraint(a, pltpu.HBM) for a in gather])


def _lru_backward(zx, zc, dyl, dz, cw, cb, wa, wx, ba, bx, lam, chip_sums, first_chips=None):
    nr = len(chip_sums)

    def body(xa_ref, xac_ref, dyl_ref, dz_in, cw_ref, cb_ref, wa_ref, wx_ref, ba_ref, bx_ref, lam_ref, *rest):
        (dxa_ref, dxac_ref, dwa_ref, dwx_ref, dba_ref, dbx_ref, dlam_ref, dcw_ref,
         dcb_ref) = rest[nr:nr + 9]
        main_s, ctx_s, pad_s = rest[3 * nr + 9:3 * nr + 12]
        if nr:
            start, forward, finish = _chips_ops(rest[:nr], rest[nr + 9:2 * nr + 9], rest[2 * nr + 9:3 * nr + 9],
                                                *rest[3 * nr + 12:], first_chips=first_chips)
            pl.when(pl.program_id(0) == 0)(start)
            pl.when(pl.program_id(0) == HEADS // 2)(forward)
            pl.when(pl.program_id(0) == HEADS - 1)(finish)
        del dz_in

        @pl.when(pl.program_id(0) == 0)
        def _():
            dba_ref[...] = jnp.zeros_like(dba_ref)
            dbx_ref[...] = jnp.zeros_like(dbx_ref)

        cwv, cbv = cw_ref[...], cb_ref[...]
        lamv = lam_ref[...]
        sp = _softplus(-lamv)
        nsp = (-LRU_C) * sp
        z = jnp.zeros((1, HD), F32)

        def wmat(ref, d):
            return ref[d, 0].astype(BF16)

        def workspace(s):
            return dict(a=(s.at[0], s.at[1]), u=(s.at[2], s.at[3]), h=(s.at[4], s.at[5]), rho=(s.at[6], s.at[7]),
                        saved=(tuple(s.at[8 + k] for k in range(4)), tuple(s.at[12 + k] for k in range(4))),
                        xc=s.at[16])

        def forward(ws, xa, t_len, h0f, h0b):
            xc = _conv(xa, cwv, cbv, pad_s)
            ws["xc"][...] = xc
            for d in (0, 1):
                vals = _gates(xc, wmat(wa_ref, d), wmat(wx_ref, d), _bias_row(ba_ref, d)[0],
                              _bias_row(bx_ref, d)[0], nsp[d:d + 1, :])
                r, i, a, gamma, rg = vals
                ws["a"][d][...] = a
                ws["u"][d][...] = gamma * (i * xc)
                for ref, val in zip(ws["saved"][d], (r, i, gamma, rg)):
                    ref[...] = val
            return _scan_pair(ws["a"][0], ws["u"][0], ws["h"][0], h0f, ws["a"][1], ws["u"][1], ws["h"][1], h0b,
                              t_len)

        def backward(ws, xa, t_len, h0f, h0b, dhf, dhb, first):
            xc = ws["xc"][...]
            (af, ab), (uf, ub), (hf, hb), (rf, rb) = ws["a"], ws["u"], ws["h"], ws["rho"]
            uf[...] = ab[...] * dhb
            ub[...] = af[...] * dhf
            rho_b_last, rho_f_first = _scan_pair(ab, uf, rb, z, af, ub, rf, z, t_len)
            dxc = jnp.zeros((t_len, HD), F32)
            dsp = []
            for d in (0, 1):
                r, i, gamma, rg = (ref[...] for ref in ws["saved"][d])
                a = ws["a"][d][...]
                if d == 0:
                    lam_t = dhf + _shifted(pad_s, rf[...], (1,))[0]
                    h_prev = _shifted(pad_s, hf[...], (-1,), before=h0f)[0]
                else:
                    lam_t = dhb + _shifted(pad_s, rb[...], (-1,))[0]
                    h_prev = _shifted(pad_s, hb[...], (1,), after=h0b)[0]
                da = lam_t * h_prev
                lx = lam_t * xc
                d_i = lx * gamma
                d_gamma = lx * i
                dxc = dxc + lam_t * (gamma * i)
                d_log_a = a * (da - d_gamma * (a * rg))
                dsp.append(jnp.sum(d_log_a * r, axis=0, keepdims=True) * (-LRU_C))
                d_pre_r = d_log_a * nsp[d:d + 1, :] * (r * (1.0 - r))
                d_pre_i = d_i * (i * (1.0 - i))
                prb, pib, xb = d_pre_r.astype(BF16), d_pre_i.astype(BF16), xc.astype(BF16)
                dxc = dxc + _dot_nt(prb, wmat(wa_ref, d)) + _dot_nt(pib, wmat(wx_ref, d))
                g_wa, g_wx = _dot_tn(xb, prb), _dot_tn(xb, pib)
                g_ba = jnp.sum(d_pre_r, axis=0, keepdims=True)
                g_bx = jnp.sum(d_pre_i, axis=0, keepdims=True)
                mask = _bias_row(ba_ref, d)[1]
                dba_ref[...] += jnp.where(mask, g_ba, 0.0)
                dbx_ref[...] += jnp.where(mask, g_bx, 0.0)
                if first:
                    dwa_ref[d, 0] = g_wa
                    dwx_ref[d, 0] = g_wx
                else:
                    dwa_ref[d, 0] += g_wa
                    dwx_ref[d, 0] += g_wx
            g_lam = jnp.concatenate(dsp, axis=0) * (-_sigmoid(-lamv))
            dm1, dp1, dm2 = _shifted(pad_s, dxc, (-1, 1, -2))
            dxa = dp1 * cwv[0:1, :] + dxc * cwv[1:2, :] + dm1 * cwv[2:3, :] + dm2 * cwv[3:4, :]
            xm1, xp1, xp2 = _shifted(pad_s, xa, (-1, 1, 2))
            g_cw = jnp.concatenate([jnp.sum(dxc * v, axis=0, keepdims=True) for v in (xm1, xa, xp1, xp2)], axis=0)
            g_cb = jnp.sum(dxc, axis=0, keepdims=True)
            if first:
                dlam_ref[...] = g_lam
                dcw_ref[...] = g_cw
                dcb_ref[...] = g_cb
            else:
                dlam_ref[...] += g_lam
                dcw_ref[...] += g_cw
                dcb_ref[...] += g_cb
            return dxa, rho_f_first, rho_b_last

        ws_x, ws_c = workspace(main_s), workspace(ctx_s)
        h0f, h0b = forward(ws_c, xac_ref[...], LC, z, z)
        forward(ws_x, xa_ref[...], L, h0f, h0b)
        dh = dyl_ref[...]
        dxa, dh0f, dh0b = backward(ws_x, xa_ref[...], L, h0f, h0b, dh, dh, True)
        dxa_ref[...] = dxa.astype(BF16)
        rc = _rows((LC, HD))
        dxac, _, _ = backward(ws_c, xac_ref[...], LC, z, z, jnp.where(rc == LC - 1, dh0f, 0.0),
                              jnp.where(rc == 0, dh0b, 0.0), False)
        dxac_ref[...] = dxac.astype(BF16)

    s = _lru_param_specs()
    col = lambda r: pl.BlockSpec((r, HD), lambda h: (0, h))
    return _call(
        body, name="lru_backward", grid=(HEADS,),
        out_shape=[jax.ShapeDtypeStruct((L, D_IN), BF16), jax.ShapeDtypeStruct((LC, D), BF16),
                   jax.ShapeDtypeStruct((2, HEADS, HD, HD), F32), jax.ShapeDtypeStruct((2, HEADS, HD, HD), F32),
                   jax.ShapeDtypeStruct((2 * HEADS, HD), F32), jax.ShapeDtypeStruct((2 * HEADS, HD), F32),
                   jax.ShapeDtypeStruct((2, D), F32), jax.ShapeDtypeStruct((CONV_W, D), F32),
                   jax.ShapeDtypeStruct((1, D), F32)] + [jax.ShapeDtypeStruct((4,) + a.shape[1:], a.dtype)
                                                          for a in chip_sums] + _chips_stage_shapes(chip_sums),
        in_specs=[s["xa"], s["xac"], col(L), pl.BlockSpec(memory_space=pl.ANY), s["cw"], s["cb"], s["h4"], s["h4"],
                  s["b16"], s["b16"], s["v2"]] + [HBM] * nr,
        out_specs=[col(L), col(LC), s["h4"], s["h4"], s["b16"], s["b16"], s["v2"], col(CONV_W), col(1)]
        + [HBM] * (2 * nr),
        scratch_shapes=[pltpu.VMEM((17, L, HD), F32), pltpu.VMEM((17, LC, HD), F32), pltpu.VMEM((L + 16, HD), F32)]
        + (_chips_sems(nr) if nr else []),
        input_output_aliases={3: 0},
        compiler_params=pltpu.CompilerParams(dimension_semantics=("arbitrary",), vmem_limit_bytes=VMEM_LIMIT,
                                             has_side_effects=True),
    )(zx, zc, dyl, dz, cw, cb, wa, wx, ba, bx, lam, *[pltpu.with_memory_space_constraint(a, pltpu.HBM)
                                                       for a in chip_sums])


def _mixer_loss(x, tgt, zx, yl, gx, fg, lng, lnb, ws, wst, bst, wout, tm):
    ncht = tm // CHUNK

    def body(x_ref, t_ref, ga_ref, u_ref, v_ref, gb_ref, yl_ref, gx_ref, fg_ref, lng_ref, lnb_ref, ws_ref, wst_ref,
             bst_ref, wout_ref,
             dz_ref, dyl_ref, dxn_ref, y_s, do_ref, dws_ref, dbst_ref, vec_ref,
             vn_s, mix_s, dm_s, dvn_s):
        step = pl.program_id(0)

        @pl.when(step == 0)
        def _():
            dws_ref[...] = jnp.zeros_like(dws_ref)
            dbst_ref[...] = jnp.zeros_like(dbst_ref)
            vec_ref[...] = jnp.zeros_like(vec_ref)

        u, v = u_ref[...], v_ref[...]
        ug, dug_du = _gelu_and_grad(u)
        vg, dvg_dv = _gelu_and_grad(v)
        mu = jnp.mean(vg, axis=-1, keepdims=True)
        vc = vg - mu
        rstd = lax.rsqrt(jnp.mean(vc * vc, axis=-1, keepdims=True) + LN_EPS)
        vhat = vc * rstd
        lngv = lng_ref[...]
        vn_s[...] = (vhat * lngv + lnb_ref[...]).astype(BF16)
        for ch in range(ncht):
            rs = slice(ch * CHUNK, (ch + 1) * CHUNK)
            for g in range(HEADS):
                cs = slice(g * HD, (g + 1) * HD)
                mix_s[rs, cs] = _dot(ws_ref[g], vn_s[rs, cs]) + bst_ref[:, g:g + 1]
        mixed = mix_s[...]
        ga, gb, yl = ga_ref[...], gb_ref[...], yl_ref[...]
        sga, dsga = _silu_and_grad(ga)
        sgb, dsgb = _silu_and_grad(gb)
        ys = ug * mixed
        y_s[:, 0:D] = (yl * sga).astype(BF16)
        y_s[:, D:D_MIX] = (ys * sgb).astype(BF16)
        o = _dot(y_s[...], wout_ref[...])
        gxv, fgv = gx_ref[...], fg_ref[...]
        xn = x_ref[...] + gxv * o
        rs2 = lax.rsqrt(jnp.mean(xn * xn, axis=-1, keepdims=True) + NORM_EPS)
        xh = xn * rs2
        diff = xh * fgv - t_ref[...]
        vec_ref[R_LOSS:R_LOSS + 1, :] += jnp.full((1, D), jnp.sum(diff * diff) * (0.5 / D), F32)
        dout = diff * (1.0 / D)
        w = dout * fgv
        dxn = rs2 * (w - xh * jnp.mean(w * xh, axis=-1, keepdims=True))
        dxn_ref[...] = dxn
        vec_ref[0:1, :] += jnp.sum(dxn * o, axis=0, keepdims=True)
        vec_ref[1:2, :] += jnp.sum(dout * xh, axis=0, keepdims=True)
        dob = (dxn * gxv).astype(BF16)
        do_ref[...] = dob
        dy = _dot_nt(dob, wout_ref[...])
        dya, dyb = dy[:, 0:D], dy[:, D:D_MIX]
        dyl_ref[...] = dya * sga
        dys = dyb * sgb
        dz_ref[:, 0:D] = jnp.zeros((tm, D), BF16)
        dz_ref[:, D:2 * D] = (dya * yl * dsga).astype(BF16)
        dz_ref[:, 2 * D:3 * D] = (dys * mixed * dug_du).astype(BF16)
        dz_ref[:, 4 * D:5 * D] = (dyb * ys * dsgb).astype(BF16)
        dm = dys * ug
        dm_s[...] = dm.astype(BF16)
        for g in range(HEADS):
            cs = slice(g * HD, (g + 1) * HD)
            dbst_ref[:, g:g + 1] += sum(jnp.sum(dm[ch * CHUNK:(ch + 1) * CHUNK, cs], axis=1, keepdims=True)
                                        for ch in range(ncht))
            for ch in range(ncht):
                rs = slice(ch * CHUNK, (ch + 1) * CHUNK)
                dws_ref[g] += _dot_nt(dm_s[rs, cs], vn_s[rs, cs])
                dvn_s[rs, cs] = _dot(wst_ref[g], dm_s[rs, cs])
        dvn = dvn_s[...]
        vec_ref[2:3, :] += jnp.sum(dvn * vhat, axis=0, keepdims=True)
        vec_ref[3:4, :] += jnp.sum(dvn, axis=0, keepdims=True)
        dvh = dvn * lngv
        dvg = rstd * (dvh - jnp.mean(dvh, axis=-1, keepdims=True) - vhat * jnp.mean(dvh * vhat, axis=-1, keepdims=True))
        dz_ref[:, 3 * D:4 * D] = (dvg * dvg_dv).astype(BF16)

    tile = pl.BlockSpec((tm, D), lambda i: (i, 0))
    zcol = lambda n: pl.BlockSpec((tm, D), lambda i: (i, n))
    vec = pl.BlockSpec((1, D), lambda i: (0, 0))
    full = lambda *s: pl.BlockSpec(s, lambda i: (0,) * len(s))
    return _call(
        body, name="mixer_loss", grid=(L // tm,),
        out_shape=[jax.ShapeDtypeStruct((L, D_IN), BF16), jax.ShapeDtypeStruct((L, D), F32),
                   jax.ShapeDtypeStruct((L, D), F32), jax.ShapeDtypeStruct((L, D_MIX), BF16),
                   jax.ShapeDtypeStruct((L, D), BF16),
                   jax.ShapeDtypeStruct((HEADS, CHUNK, CHUNK), F32), jax.ShapeDtypeStruct((CHUNK, HEADS), F32),
                   jax.ShapeDtypeStruct((8, D), F32)],
        in_specs=[tile, tile, zcol(1), zcol(2), zcol(3), zcol(4), tile, pl.BlockSpec((1, D), lambda i: (0, 2)),
                  vec, vec, vec,
                  full(HEADS, CHUNK, CHUNK), full(HEADS, CHUNK, CHUNK), full(CHUNK, HEADS),
                  pl.BlockSpec((D_MIX, D), lambda i: (0, 0), pipeline_mode=pl.Buffered(1))],
        out_specs=[pl.BlockSpec((tm, D_IN), lambda i: (i, 0)), tile, tile,
                   pl.BlockSpec((tm, D_MIX), lambda i: (i, 0)), tile,
                   full(HEADS, CHUNK, CHUNK), full(CHUNK, HEADS), full(8, D)],
        scratch_shapes=[pltpu.VMEM((tm, D), BF16), pltpu.VMEM((tm, D), F32),
                        pltpu.VMEM((tm, D), BF16), pltpu.VMEM((tm, D), F32)],
        compiler_params=_params("arbitrary"),
    )(x, tgt, zx, zx, zx, zx, yl, gx, fg, lng, lnb, ws, wst, bst, wout)


def _grad_w(a, b, a2, b2, tk, name, bw, first, nblocks, split, barrier_id):
    nk = a.shape[0] // tk
    m = a.shape[1]
    with_ctx = a2 is not None
    if split == "cols":
        slots, r, w = nblocks, m, bw // 2
        piece = lambda q, pc: (slice(None), slice(pc * w, (pc + 1) * w))
    else:
        slots, r, w = 4, m // 8, bw
        piece = lambda q, pc: (slice((2 * q + pc) * r, (2 * q + pc + 1) * r), slice(None))

    def body(*refs):
        a_ref, b_ref = refs[:2]
        a2_ref, b2_ref = refs[2:4] if with_ctx else (None, None)
        sums_ref, acc, mine_v, send_v, stage_v, send_sems, recv_sems = refs[4 if with_ctx else 2:]
        n, k = pl.program_id(0), pl.program_id(1)
        x, y, c = lax.axis_index("x"), lax.axis_index("y"), lax.axis_index("c")

        def to_sibling(s):
            return pltpu.make_async_remote_copy(src_ref=send_v.at[s], dst_ref=stage_v.at[s], send_sem=send_sems.at[s],
                                                recv_sem=recv_sems.at[s], device_id=(x, y, 1 - c),
                                                device_id_type=MESH)

        pl.when(jnp.logical_and(n == 0, k == 0))(_sibling_barrier)

        @pl.when(k == 0)
        def _():
            acc[...] = _dot_tn(a_ref[...], b_ref[...])

        if nk > 1:
            @pl.when(k > 0)
            def _():
                acc[...] += _dot_tn(a_ref[...], b_ref[...])

        if with_ctx:
            @pl.when(jnp.logical_and(k == nk - 1, n == 0))
            def _():
                acc[:, 0:b2_ref.shape[1]] += _dot_tn(a2_ref[...], b2_ref[...])

        def hand_over(s, q):
            for pc in (0, 1):
                @pl.when(c == pc)
                def _(pc=pc):
                    mine_v[s] = acc[piece(q, pc)]
                    send_v[s] = acc[piece(q, 1 - pc)].astype(BF16)
            to_sibling(s).start()

        for i in range(nblocks):
            @pl.when(jnp.logical_and(k == nk - 1, n == i))
            def _(i=i):
                if split == "cols":
                    hand_over(i, 0)
                else:
                    for q in range(4):
                        hand_over(q, q)

        @pl.when(jnp.logical_and(k == nk - 1, n == nblocks - 1))
        def _():
            for s in range(slots):
                to_sibling(s).wait_recv()
                sums_ref[s] = (mine_v[s] + stage_v[s].astype(F32)).astype(BF16)
            for s in range(slots):
                to_sibling(s).wait_send()

    in_specs = [pl.BlockSpec((tk, m), lambda n, k: (k, 0)), pl.BlockSpec((tk, bw), lambda n, k: (k, n + first))]
    args = [a, b]
    if with_ctx:
        in_specs += [pl.BlockSpec(a2.shape, lambda n, k: (0, 0)), pl.BlockSpec(b2.shape, lambda n, k: (0, 0))]
        args += [a2, b2]
    return _call(
        body, name=name, grid=(nblocks, nk), out_shape=jax.ShapeDtypeStruct((slots, r, w), BF16),
        in_specs=in_specs, out_specs=pl.BlockSpec((slots, r, w), lambda n, k: (0, 0, 0)),
        scratch_shapes=[pltpu.VMEM((m, bw), F32), pltpu.VMEM((slots, r, w), F32), pltpu.VMEM((slots, r, w), BF16),
                        pltpu.VMEM((slots, r, w), BF16), pltpu.SemaphoreType.DMA((slots,)),
                        pltpu.SemaphoreType.DMA((slots,))],
        compiler_params=pltpu.CompilerParams(dimension_semantics=("arbitrary", "arbitrary"),
                                             vmem_limit_bytes=VMEM_LIMIT, has_side_effects=True,
                                             collective_id=barrier_id),
    )(*args)


def _grad_rows(xr, dz, w, mod, ng, dres, ncols, tm, name, chip_sums=(), first_chips=None, dests=None):
    rows = xr.shape[0]
    steps = rows // tm
    with_dx = dres is not None
    nr = len(chip_sums)
    dests = [d for d in (dests or [None] * nr)]
    nd = sum(d is not None for d in dests)
    nin = 6 if with_dx else 5
    nout = 2 if with_dx else 1

    def body(*refs):
        if with_dx:
            x_ref, dz_ref, w_ref, sc_ref, ng_ref, dres_ref = refs[:nin]
            dx_ref, vec_ref = refs[nin + nr + nd:nin + nr + nd + nout]
        else:
            x_ref, dz_ref, w_ref, sc_ref, ng_ref = refs[:nin]
            (vec_ref,) = refs[nin + nr + nd:nin + nr + nd + nout]
        if nr:
            o0 = nin + nr + nd + nout
            start, forward, finish = _chips_ops(refs[nin:nin + nr], refs[o0:o0 + nr], refs[o0 + nr:o0 + 2 * nr],
                                                *refs[o0 + 2 * nr:], first_chips=first_chips)
            pl.when(pl.program_id(0) == 0)(start)
            pl.when(pl.program_id(0) == steps // 2)(forward)
            pl.when(pl.program_id(0) == steps - 1)(finish)

        @pl.when(pl.program_id(0) == 0)
        def _():
            vec_ref[...] = jnp.zeros_like(vec_ref)

        dhn = _dot_nt(dz_ref[...], w_ref[...])
        x = x_ref[...]
        rs = lax.rsqrt(jnp.mean(x * x, axis=-1, keepdims=True) + NORM_EPS)
        xh = x * rs
        ngv = ng_ref[...]
        y = xh * ngv
        vec_ref[0:1, :] += jnp.sum(dhn, axis=0, keepdims=True)
        vec_ref[1:2, :] += jnp.sum(dhn * y, axis=0, keepdims=True)
        dy = dhn * (1.0 + sc_ref[...])
        vec_ref[2:3, :] += jnp.sum(dy * xh, axis=0, keepdims=True)
        if with_dx:
            dxh = dy * ngv
            dx_ref[...] = dres_ref[...] + rs * (dxh - xh * jnp.mean(dxh * xh, axis=-1, keepdims=True))

    tile = pl.BlockSpec((tm, D), lambda i: (i, 0))
    vec = pl.BlockSpec((1, D), lambda i: (0, 0))
    in_specs = [tile, pl.BlockSpec((tm, ncols), lambda i: (i, 0)),
                pl.BlockSpec((D, ncols), lambda i: (0, 0), pipeline_mode=pl.Buffered(1)),
                pl.BlockSpec((1, D), lambda i: (0, 1)), vec]
    out_shape = [jax.ShapeDtypeStruct((8, D), F32)]
    out_specs = [pl.BlockSpec((8, D), lambda i: (0, 0))]
    args = [xr, dz, w, mod, ng]
    if with_dx:
        in_specs.append(tile)
        out_shape.insert(0, jax.ShapeDtypeStruct((rows, D), F32))
        out_specs.insert(0, tile)
        args.append(dres)
    aliases = {}
    for j, d in enumerate(dests):
        if d is not None:
            aliases[len(args) + nr + len(aliases)] = len(out_shape) + j
    in_specs += [HBM] * (nr + nd)
    out_specs += [HBM] * (2 * nr)
    out_shape += [jax.ShapeDtypeStruct((4,) + a.shape[1:], a.dtype) for a in chip_sums]
    out_shape += _chips_stage_shapes(chip_sums)
    args += [pltpu.with_memory_space_constraint(a, pltpu.HBM) for a in chip_sums]
    args += [pltpu.with_memory_space_constraint(d, pltpu.HBM) for d in dests if d is not None]
    return _call(body, name=name, grid=(steps,), out_shape=out_shape, in_specs=in_specs, out_specs=out_specs,
                 scratch_shapes=_chips_sems(nr) if nr else [], input_output_aliases=aliases,
                 compiler_params=pltpu.CompilerParams(dimension_semantics=("arbitrary",),
                                                      vmem_limit_bytes=VMEM_LIMIT, has_side_effects=bool(nr)))(*args)


def _adamw(w, g, m, v):
    m = ADAM_B1 * m + (1.0 - ADAM_B1) * g
    v = ADAM_B2 * v + (1.0 - ADAM_B2) * (g * g)
    m_hat = m / (1.0 - ADAM_B1 ** ADAM_STEP)
    v_hat = v / (1.0 - ADAM_B2 ** ADAM_STEP)
    delta = -ADAM_LR * (m_hat / (jnp.sqrt(v_hat) + ADAM_EPS) + ADAM_WD * w)
    return delta, m, v


def _adamw_reduced(parts, w, m, v, tr, name):
    r, n = w.shape
    nparts = parts.shape[0]

    def body(p_ref, w_ref, m_ref, v_ref, g_ref, d_ref, mo_ref, vo_ref):
        g = p_ref[0].astype(F32)
        for i in range(1, nparts):
            g = g + p_ref[i].astype(F32)
        g_ref[...] = g
        d_ref[...], mo_ref[...], vo_ref[...] = _adamw(w_ref[...], g, m_ref[...], v_ref[...])

    tile = pl.BlockSpec((tr, n), lambda i: (i, 0))
    sds = jax.ShapeDtypeStruct((r, n), F32)
    return _call(
        body, name=name, grid=(r // tr,), out_shape=[sds] * 4,
        in_specs=[pl.BlockSpec((nparts, tr, n), lambda i: (0, i, 0)), tile, tile, tile], out_specs=[tile] * 4,
        compiler_params=_params("arbitrary"),
    )(parts, w, m, v)


R_GATE, R_FINAL_G, R_LN_G, R_LN_B, R_LOSS = 0, 1, 2, 3, 4
R_SH_X, R_SC_X, R_NG_X = 5, 6, 7
R_SH_C, R_SC_C, R_NG_C = 8, 9, 10
R_LAM, R_CW, R_CB = 11, 13, 17
PACK_ROWS = 24
Q_BA, Q_BX, Q_SGU_B, PACK128_ROWS = 0, 16, 32, 40


def _reduce_small(vp_all, vq_all, mat_parts, ada_w, me):
    nloc = ada_w.shape[1]
    nm = len(mat_parts)

    def body(me_ref, vp_ref, vq_ref, *refs):
        mp_refs, w_ref = refs[:nm], refs[nm]
        red_ref, redq_ref = refs[nm + 1:nm + 3]
        mat_refs = refs[nm + 3:2 * nm + 3]
        dmod_ref, gab_ref, cpart_ref, dmc_s = refs[2 * nm + 3:]
        red, redq = vp_ref[0], vq_ref[0]
        for i in range(1, N_DEV):
            red = red + vp_ref[i]
            redq = redq + vq_ref[i]
        red_ref[...] = red
        redq_ref[...] = redq
        for mp_ref, mat_ref in zip(mp_refs, mat_refs):
            mat = mp_ref[0].astype(F32)
            for i in range(1, mp_ref.shape[0]):
                mat = mat + mp_ref[i].astype(F32)
            mat_ref[...] = mat
        for e in range(N_DEV):
            dmod_ref[e:e + 1, 0:D] = vp_ref[e, R_SH_X:R_SH_X + 1, :]
            dmod_ref[e:e + 1, D:2 * D] = vp_ref[e, R_SC_X:R_SC_X + 1, :]
            dmod_ref[e:e + 1, 2 * D:3 * D] = vp_ref[e, R_GATE:R_GATE + 1, :]
        dmod_ref[8:9, 0:D] = red[R_SH_C:R_SH_C + 1, :]
        dmod_ref[8:9, D:2 * D] = red[R_SC_C:R_SC_C + 1, :]
        dmod_ref[8:9, 2 * D:3 * D] = jnp.zeros((1, D), F32)
        dmod_ref[9:16, :] = jnp.zeros((7, 3 * D), F32)
        gab_ref[:, 0:D] = red[R_SH_X:R_SH_X + 1, :] + red[R_SH_C:R_SH_C + 1, :]
        gab_ref[:, D:2 * D] = red[R_SC_X:R_SC_X + 1, :] + red[R_SC_C:R_SC_C + 1, :]
        gab_ref[:, 2 * D:3 * D] = red[R_GATE:R_GATE + 1, :]
        dmc_s[...] = jnp.broadcast_to(dmod_ref[8:9, :], (8, 3 * D))
        off = pl.multiple_of(me_ref[0] * nloc, 128)
        cpart_ref[...] = _dot_nt(dmc_s[:, pl.ds(off, nloc)], w_ref[...])

    return _call(
        body, name="reduce_small",
        out_shape=[jax.ShapeDtypeStruct((PACK_ROWS, D), F32), jax.ShapeDtypeStruct((PACK128_ROWS, HD), F32)]
        + [jax.ShapeDtypeStruct(p.shape[1:], F32) for p in mat_parts]
        + [jax.ShapeDtypeStruct((16, 3 * D), F32), jax.ShapeDtypeStruct((1, 3 * D), F32),
           jax.ShapeDtypeStruct((8, D), F32)],
        in_specs=[pl.BlockSpec(memory_space=pltpu.SMEM)] + [VMEM] * (nm + 3), out_specs=[VMEM] * (nm + 5),
        scratch_shapes=[pltpu.VMEM((8, 3 * D), F32)], compiler_params=_params(),
    )(me, vp_all, vq_all, *mat_parts, ada_w)


def _adamw_ada(c_all, c_ctx, dmod, w, m, v, me):
    nloc = w.shape[1]

    def body(me_ref, c_ref, cc_ref, dm_ref, w_ref, m_ref, v_ref, g_ref, d_ref, mo_ref, vo_ref):
        off = pl.multiple_of(me_ref[0] * nloc, 128)
        dm = dm_ref[:, pl.ds(off, nloc)]
        sx, _ = _silu_and_grad(c_ref[...])
        sc, _ = _silu_and_grad(cc_ref[...])
        g = _dot_tn(sx, dm[0:8, :]) + _dot_tn(jnp.broadcast_to(sc, (8, D)), dm[8:16, :])
        g_ref[...] = g
        d_ref[...], mo_ref[...], vo_ref[...] = _adamw(w_ref[...], g, m_ref[...], v_ref[...])

    sds = jax.ShapeDtypeStruct(w.shape, F32)
    return _call(
        body, name="adamw_ada_w", out_shape=[sds] * 4,
        in_specs=[pl.BlockSpec(memory_space=pltpu.SMEM)] + [VMEM] * 6, out_specs=[VMEM] * 4,
        compiler_params=_params(),
    )(me, c_all, c_ctx, dmod, w, m, v)


_SMALL = ("c_ctx", "ada_b", "norm_g", "conv_w", "conv_b", "lru_wa", "lru_ba", "lru_wx", "lru_bx", "lru_lambda",
          "sgu_ln_g", "sgu_ln_b", "sgu_w", "sgu_b", "final_g")


def _adamw_small(red, redq, mats, cparts, gab, ws, ms, vs, me):
    n = len(_SMALL)

    def body(me_ref, red_ref, redq_ref, wa_ref, wx_ref, sw_ref, cp_ref, gab_ref, *refs):
        w_refs, m_refs, v_refs = refs[:n], refs[n:2 * n], refs[2 * n:3 * n]
        outs = refs[3 * n:]
        off = pl.multiple_of(me_ref[0] * HD, 128)

        def row(r, k=1):
            return red_ref[r:r + k, :]

        cc = w_refs[0][...]
        dcc = cp_ref[0, 0:1, :]
        for i in range(1, N_DEV):
            dcc = dcc + cp_ref[i, 0:1, :]
        grads = dict(
            c_ctx=dcc * _silu_and_grad(cc)[1], ada_b=gab_ref[...], norm_g=row(R_NG_X) + row(R_NG_C),
            conv_w=red_ref[R_CW:R_CW + CONV_W, pl.ds(off, HD)], conv_b=row(R_CB),
            lru_wa=wa_ref[...], lru_ba=redq_ref[Q_BA:Q_BA + 2 * HEADS, :], lru_wx=wx_ref[...],
            lru_bx=redq_ref[Q_BX:Q_BX + 2 * HEADS, :], lru_lambda=red_ref[R_LAM:R_LAM + 2, pl.ds(off, HD)],
            sgu_ln_g=row(R_LN_G), sgu_ln_b=row(R_LN_B), sgu_w=sw_ref[...],
            sgu_b=redq_ref[Q_SGU_B:Q_SGU_B + HEADS, :], final_g=row(R_FINAL_G))
        for j, name in enumerate(_SMALL):
            g = grads[name]
            outs[j][...] = g
            outs[n + j][...], outs[2 * n + j][...], outs[3 * n + j][...] = _adamw(w_refs[j][...], g, m_refs[j][...],
                                                                                 v_refs[j][...])

    sds = [jax.ShapeDtypeStruct(ws[k].shape, F32) for k in _SMALL]
    outs = _call(
        body, name="adamw_small", out_shape=sds * 4,
        in_specs=[pl.BlockSpec(memory_space=pltpu.SMEM)] + [VMEM] * (7 + 3 * n), out_specs=[VMEM] * (4 * n),
        compiler_params=_params(),
    )(me, red, redq, *mats, cparts, gab, *[ws[k] for k in _SMALL], *[ms[k] for k in _SMALL],
      *[vs[k] for k in _SMALL])
    return [dict(zip(_SMALL, outs[i * n:(i + 1) * n])) for i in range(4)]


def kernel(x, c, ctx, c_ctx, ada_w, ada_b, norm_g, w_in, conv_w, conv_b, lru_wa, lru_ba, lru_wx, lru_bx, lru_lambda, sgu_ln_g, sgu_ln_b, sgu_w, sgu_b, w_out, final_g, loss_target, m_c_ctx, m_ada_w, m_ada_b, m_norm_g, m_w_in, m_conv_w, m_conv_b, m_lru_wa, m_lru_ba, m_lru_wx, m_lru_bx, m_lru_lambda, m_sgu_ln_g, m_sgu_ln_b, m_sgu_w, m_sgu_b, m_w_out, m_final_g, v_c_ctx, v_ada_w, v_ada_b, v_norm_g, v_w_in, v_conv_w, v_conv_b, v_lru_wa, v_lru_ba, v_lru_wx, v_lru_bx, v_lru_lambda, v_sgu_ln_g, v_sgu_ln_b, v_sgu_w, v_sgu_b, v_w_out, v_final_g):
    args = dict(locals())
    me_s = 4 * lax.axis_index("x") + 2 * lax.axis_index("y") + lax.axis_index("c")
    me = me_s.astype(jnp.int32).reshape(1)
    xr, ctxr, tgt = x[0], ctx[0], loss_target[0]
    cc = c_ctx.reshape(1, D)
    nw = 2 * HEADS * HD
    view = dict(c_ctx=(1, D), ada_b=(1, 3 * D), norm_g=(1, D), conv_w=(CONV_W, HD), conv_b=(1, D), lru_wa=(nw, HD),
                lru_ba=(2 * HEADS, HD), lru_wx=(nw, HD), lru_bx=(2 * HEADS, HD), lru_lambda=(2, HD), sgu_ln_g=(1, D),
                sgu_ln_b=(1, D), sgu_w=(HEADS * CHUNK, CHUNK), sgu_b=(HEADS, CHUNK), final_g=(1, D))

    zx, hn, w_full, w_out_b, modx, modc, c_all, cw_full, lam_full = _front_project(
        xr, c, cc, ada_w[0], ada_b, norm_g, w_in[0], w_out[0], conv_w[0], lru_lambda[0], me)
    zc, hnc = _project(ctxr, modc, norm_g, w_full, D, LC, "project_ctx")
    ba, bx = lru_ba.reshape(view["lru_ba"]), lru_bx.reshape(view["lru_bx"])
    yl, wout_all = _lru_forward(zx, zc, cw_full, conv_b, lru_wa[0], lru_wx[0], ba, bx, lam_full, [w_out_b], ["ag"])
    wout_full = wout_all.reshape(D_MIX, D)
    ws_b = sgu_w[0].astype(BF16)
    dz, dyl, dxn, ycat, dob, dws, dbst, mvec = _mixer_loss(
        xr, tgt, zx, yl, modx, final_g.reshape(1, D), sgu_ln_g, sgu_ln_b, ws_b, jnp.swapaxes(ws_b, 1, 2),
        sgu_b[0].T, wout_full, 256)

    wout_sums = _grad_w(ycat, dob, None, None, L, "grad_w_out", D, 0, 1, "rows", 1)
    rest_sums = _grad_w(hn, dz, None, None, L, "grad_w_in_rest", 2 * W_IN_SHARD, 1, 3, "cols", 2)
    dz, dxac, dwa, dwx, dba, dbx, dlam, dcw, dcb, win_parts, wout_parts, _, _ = _lru_backward(
        zx, zc, dyl, dz, cw_full, conv_b, lru_wa[0], lru_wx[0], ba, bx, lam_full, [rest_sums, wout_sums],
        first_chips=[1, 0])
    first_sums = _grad_w(hn, dz, hnc, dxac, L, "grad_w_in_first", 2 * W_IN_SHARD, 0, 1, "cols", 3)
    mats = [dwa.reshape(N_DEV, nw // N_DEV, HD), dwx.reshape(N_DEV, nw // N_DEV, HD), dws]
    mat_sums = _reduce2_local(mats, ["a2a"] * 3, me, "reduce_mat", 4, out_dtype=BF16)
    gx, xvec, win_parts, *mat_parts = _grad_rows(
        xr, dz, w_full, modx, norm_g, dxn, D_IN, 256, "grad_rows_x", chip_sums=[first_sums, *mat_sums],
        first_chips=[0, 0, 0, 0], dests=[win_parts, None, None, None])[:6]
    (cvec,) = _grad_rows(ctxr, dxac, w_full, modc, norm_g, None, D, LC, "grad_rows_ctx")
    pack = jnp.concatenate([mvec[0:5], xvec[0:3], cvec[0:3], dlam, dcw, dcb,
                            jnp.zeros((PACK_ROWS - R_CB - 1, D), F32)], axis=0)
    pack128 = jnp.concatenate([dba, dbx, dbst.T], axis=0)
    vp_all, vq_all = _gather2([pack, pack128], ["ag", "ag"], "gather_pack")
    red, redq, *rest = _reduce_small(vp_all, vq_all, mat_parts, ada_w[0], me)
    mat_pieces, (dmod, gab, cpart) = rest[:3], rest[3:]
    *mats_all, cparts = _gather2([*mat_pieces, cpart], ["ag"] * 4, "gather_small")

    g_w_in, d_w_in, nm_w_in, nv_w_in = _adamw_reduced(win_parts, w_in[0], m_w_in[0], v_w_in[0], 256, "adamw_w_in")
    g_w_out, d_w_out, nm_w_out, nv_w_out = _adamw_reduced(wout_parts, w_out[0], m_w_out[0], v_w_out[0], 128,
                                                          "adamw_w_out")
    g_ada, d_ada, nm_ada, nv_ada = _adamw_ada(c_all, cc, dmod, ada_w[0], m_ada_w[0], v_ada_w[0], me)
    ws = {k: args[k].reshape(view[k]) for k in _SMALL}
    ms = {k: args["m_" + k].reshape(view[k]) for k in _SMALL}
    vs = {k: args["v_" + k].reshape(view[k]) for k in _SMALL}
    small = _adamw_small(red, redq, [m.reshape(-1, HD) for m in mats_all], cparts, gab, ws, ms, vs, me)
    big = dict(w_in=(g_w_in, d_w_in, nm_w_in, nv_w_in), w_out=(g_w_out, d_w_out, nm_w_out, nv_w_out),
               ada_w=(g_ada, d_ada, nm_ada, nv_ada))

    loss = red[R_LOSS, 0]
    names = ("c_ctx", "ada_w", "ada_b", "norm_g", "w_in", "conv_w", "conv_b", "lru_wa", "lru_ba", "lru_wx", "lru_bx",
             "lru_lambda", "sgu_ln_g", "sgu_ln_b", "sgu_w", "sgu_b", "w_out", "final_g")
    outs = [loss, gx.reshape(x.shape)]
    for kind in range(4):
        for k in names:
            val = big[k][kind] if k in big else small[kind][k]
            outs.append(val.reshape(args[k].shape))
    return tuple(outs)
```

```python
import functools

import jax
import jax.numpy as jnp
from jax import lax
from jax.experimental import pallas as pl
from jax.experimental.pallas import tpu as pltpu

F32 = jnp.float32
BF16 = jnp.bfloat16

N_DEV = 8
D = 1024
L = 2048
LC = 256
HEADS = 8
HD = 128
CHUNK = 128
D_IN = 5 * D
W_IN_SHARD = D_IN // N_DEV
ROWS = 256
D_MIX = 2 * D
CONV_W = 4
LRU_C = 8.0
NORM_EPS = 1e-6
LN_EPS = 1e-5
ADAM_LR, ADAM_B1, ADAM_B2, ADAM_EPS, ADAM_WD, ADAM_STEP = 0.001, 0.9, 0.999, 1e-08, 0.01, 10

VMEM_LIMIT = 56 * 1024 * 1024

HBM = pl.BlockSpec(memory_space=pltpu.HBM)
VMEM = pl.BlockSpec(memory_space=pltpu.VMEM)
MESH = pl.DeviceIdType.MESH


def _call(body, **kw):
    return pl.pallas_call(body, **kw)


def _params(*sem):
    return pltpu.CompilerParams(dimension_semantics=sem, vmem_limit_bytes=VMEM_LIMIT)


def _sigmoid(x):
    return 0.5 * jnp.tanh(0.5 * x) + 0.5


def _silu_and_grad(x):
    s = _sigmoid(x)
    return x * s, s * (1.0 + x * (1.0 - s))


_G0 = 0.7978845608028654
_G1 = 0.044715


def _gelu_and_grad(x):
    x2 = x * x
    t = jnp.tanh(_G0 * (x + _G1 * x * x2))
    cdf = 0.5 * (1.0 + t)
    return x * cdf, cdf + 0.5 * x * (1.0 - t * t) * (_G0 * (1.0 + 3.0 * _G1 * x2))


def _gelu(x):
    return 0.5 * x * (1.0 + jnp.tanh(_G0 * (x + _G1 * x * x * x)))


def _softplus(z):
    t = jnp.exp(-jnp.abs(z))
    u = 1.0 + t
    log1p = jnp.where(u == 1.0, t, jnp.log(u) * t / jnp.where(u == 1.0, 1.0, u - 1.0))
    return jnp.maximum(z, 0.0) + log1p


def _dot(a, b):
    return jnp.dot(a, b, preferred_element_type=F32)


def _dot_nt(a, b):
    return lax.dot_general(a, b, (((1,), (1,)), ((), ())), preferred_element_type=F32)


def _dot_tn(a, b):
    return lax.dot_general(a, b, (((0,), (0,)), ((), ())), preferred_element_type=F32)


def _rows(shape):
    return lax.broadcasted_iota(jnp.int32, shape, 0)


def _shift_down(x, first):
    y = pltpu.roll(x, 1, 0)
    head = jnp.where(_rows((8, x.shape[1])) == 0, first, y[0:8])
    return jnp.concatenate([head, y[8:]], axis=0)


def _shift_up(x, last):
    n = x.shape[0]
    y = pltpu.roll(x, n - 1, 0)
    tail = jnp.where(_rows((8, x.shape[1])) == 7, last, y[n - 8:])
    return jnp.concatenate([y[:n - 8], tail], axis=0)


def _gather2(arrays, modes, name, barrier_id):
    n = len(arrays)

    def body(*refs):
        start, forward, finish = _gather2_ops(refs[:n], refs[n:2 * n], modes, *refs[2 * n:], barrier=True)
        start()
        forward()
        finish()

    return _call(
        body, name=name, out_shape=_gather2_shapes(arrays, modes), in_specs=[HBM] * n, out_specs=[HBM] * n,
        scratch_shapes=_gather2_sems(n),
        compiler_params=pltpu.CompilerParams(has_side_effects=True, collective_id=barrier_id),
    )(*[pltpu.with_memory_space_constraint(a, pltpu.HBM) for a in arrays])


def _gather2_shapes(arrays, modes):
    return [jax.ShapeDtypeStruct((N_DEV,) + a.shape if m == "ag" else (a.shape[0], N_DEV * a.shape[1]), a.dtype)
            for a, m in zip(arrays, modes)]


def _gather2_sems(n):
    return [pltpu.SemaphoreType.DMA((n, N_DEV - 1)), pltpu.SemaphoreType.DMA((n, N_DEV - 1)),
            pltpu.SemaphoreType.DMA((n,))]


def _barrier(peers):
    sem = pltpu.get_barrier_semaphore()
    for peer in peers:
        pl.semaphore_signal(sem, inc=1, device_id=peer, device_id_type=MESH)
    pl.semaphore_wait(sem, len(peers))


def _gather2_ops(ins, outs, modes, send_sems, recv_sems, local_sems, barrier=False):
    n = len(ins)
    x, y, c = lax.axis_index("x"), lax.axis_index("y"), lax.axis_index("c")
    me, sibling = (x, y, c), (x, y, 1 - c)
    chips = [(x ^ (k >> 1), y ^ (k & 1)) for k in (1, 2, 3)]

    def slot(j, px, py, pc):
        dev = 4 * px + 2 * py + pc
        if modes[j] == "agc":
            w = ins[j].shape[1]
            return outs[j].at[:, pl.ds(pl.multiple_of(dev * w, 128), w)]
        return outs[j].at[dev]

    def copy(j, k, block, to, src=None):
        return pltpu.make_async_remote_copy(
            src_ref=slot(j, *block) if src is None else src, dst_ref=slot(j, *block),
            send_sem=send_sems.at[j, k], recv_sem=recv_sems.at[j, k], device_id=to, device_id_type=MESH)

    def own(j):
        return pltpu.make_async_copy(ins[j], slot(j, *me), local_sems.at[j])

    def first(j):
        return [copy(j, 0, me, sibling, src=ins[j])] + [copy(j, 1 + i, me, (*chip, c), src=ins[j])
                                                        for i, chip in enumerate(chips)]

    def passed(j, i):
        return copy(j, 4 + i, (*chips[i], c), sibling)

    def start():
        if barrier:
            _barrier([sibling] + [(*chip, c) for chip in chips])
        for j in range(n):
            own(j).start()
            for cp in first(j):
                cp.start()

    def forward():
        for i, chip in enumerate(chips):
            for j in range(n):
                copy(j, 1 + i, (*chip, c), me).wait_recv()
                passed(j, i).start()

    def finish():
        for j in range(n):
            copy(j, 0, sibling, me).wait_recv()
            for i, chip in enumerate(chips):
                copy(j, 4 + i, (*chip, 1 - c), me).wait_recv()
            for cp in first(j) + [passed(j, i) for i in range(3)]:
                cp.wait_send()
            own(j).wait()

    return start, forward, finish


def _sibling_barrier():
    sem = pltpu.get_barrier_semaphore()
    sibling = (lax.axis_index("x"), lax.axis_index("y"), 1 - lax.axis_index("c"))
    pl.semaphore_signal(sem, inc=1, device_id=sibling, device_id_type=MESH)
    pl.semaphore_wait(sem, 1)


def _reduce2_local(arrays, modes, me, name, barrier_id, counts=None, out_dtype=None):
    n = len(arrays)
    counts = counts or [4] * n
    shapes = [(a.shape[1], a.shape[2]) if m == "a2a" else (a.shape[0], a.shape[1] // (2 * cnt))
              for a, m, cnt in zip(arrays, modes, counts)]
    staged = [jax.ShapeDtypeStruct((cnt,) + s, a.dtype) for s, a, cnt in zip(shapes, arrays, counts)]

    def piece(ref, mode, dev, w):
        return ref.at[dev] if mode == "a2a" else ref.at[:, pl.ds(pl.multiple_of(dev * w, 128), w)]

    def to_sibling(*refs):
        ins, outs = refs[:n], refs[n:2 * n]
        send_sems, recv_sems = refs[2 * n:]
        x, y, c = lax.axis_index("x"), lax.axis_index("y"), lax.axis_index("c")
        _sibling_barrier()
        copies = []
        for j in range(n):
            for q in range(counts[j]):
                cp = pltpu.make_async_remote_copy(
                    src_ref=piece(ins[j], modes[j], 2 * q + (1 - c), shapes[j][1]), dst_ref=outs[j].at[q],
                    send_sem=send_sems.at[j, q], recv_sem=recv_sems.at[j, q], device_id=(x, y, 1 - c),
                    device_id_type=MESH)
                cp.start()
                copies.append(cp)
        for cp in copies:
            cp.wait()

    stage = _call(
        to_sibling, name=name + "_d2d", out_shape=staged, in_specs=[HBM] * n, out_specs=[HBM] * n,
        scratch_shapes=[pltpu.SemaphoreType.DMA((n, 4)), pltpu.SemaphoreType.DMA((n, 4))],
        compiler_params=pltpu.CompilerParams(has_side_effects=True, collective_id=barrier_id),
    )(*[pltpu.with_memory_space_constraint(a, pltpu.HBM) for a in arrays])

    def add(me_ref, *refs):
        del me_ref
        own, got, outs = refs[:n], refs[n:2 * n], refs[2 * n:]
        for j in range(n):
            mine = own[j][0] if modes[j] == "a2a" else own[j][...]
            outs[j][0] = (mine.astype(F32) + got[j][0].astype(F32)).astype(outs[j].dtype)

    in_specs, slot_specs = [], []
    for (r, w), m, cnt in zip(shapes, modes, counts):
        if m == "a2a":
            in_specs.append(pl.BlockSpec(
                (1, r, w), lambda q, me_ref, cnt=cnt: (2 * jnp.minimum(q, cnt - 1) + me_ref[0] % 2, 0, 0)))
        else:
            in_specs.append(pl.BlockSpec(
                (r, w), lambda q, me_ref, cnt=cnt: (0, 2 * jnp.minimum(q, cnt - 1) + me_ref[0] % 2)))
        slot_specs.append(pl.BlockSpec((1, r, w), lambda q, me_ref, cnt=cnt: (jnp.minimum(q, cnt - 1), 0, 0)))
    return _call(
        add, name=name + "_add",
        out_shape=[jax.ShapeDtypeStruct(s.shape, out_dtype or s.dtype) for s in staged],
        grid_spec=pltpu.PrefetchScalarGridSpec(num_scalar_prefetch=1, grid=(max(counts),),
                                               in_specs=in_specs + slot_specs, out_specs=slot_specs),
        compiler_params=_params("arbitrary"),
    )(me, *arrays, *stage)


def _chips_sems(n):
    return [pltpu.SemaphoreType.DMA((n, 6)), pltpu.SemaphoreType.DMA((n, 6)), pltpu.SemaphoreType.DMA((n,))]


def _chips_stage_shapes(chip_sums):
    return [jax.ShapeDtypeStruct((2, a.shape[1] // 2, a.shape[2]), a.dtype) for a in chip_sums]


def _chips_ops(ins, outs, stages, send_sems, recv_sems, local_sems, first_chips=None, barrier=False):
    x, y, c = lax.axis_index("x"), lax.axis_index("y"), lax.axis_index("c")
    qm = 2 * x + y
    first_chips = first_chips or [0] * len(ins)

    def owns(j, chip):
        lo, cnt = first_chips[j], ins[j].shape[0]
        if lo == 0 and cnt == 4:
            return None
        return jnp.logical_and(chip >= lo, chip < lo + cnt)

    def guarded(cond, fn):
        if cond is None:
            fn()
        else:
            pl.when(cond)(fn)

    def slot(j, chip):
        return jnp.clip(chip - first_chips[j], 0, ins[j].shape[0] - 1)

    def half(j, i):
        h = ins[j].shape[1] // 2
        return pl.ds(i * h, h)

    def copy(j, sem, src, dst, k):
        return pltpu.make_async_remote_copy(
            src_ref=src, dst_ref=dst, send_sem=send_sems.at[j, sem], recv_sem=recv_sems.at[j, sem],
            device_id=(x ^ (k >> 1), y ^ (k & 1), c), device_id_type=MESH)

    def direct(j, k):
        return copy(j, k - 1, ins[j].at[slot(j, qm ^ k)], outs[j].at[qm], k)

    def first_hop(j, k):
        return copy(j, 1 + k, ins[j].at[slot(j, qm ^ 3), half(j, k - 1)], stages[j].at[k - 1], k)

    def second_hop(j, k):
        return copy(j, 3 + k, stages[j].at[2 - k], outs[j].at[qm ^ (3 - k), half(j, 2 - k)], k)

    def local(j):
        return pltpu.make_async_copy(ins[j].at[slot(j, qm)], outs[j].at[qm], local_sems.at[j])

    def start():
        if barrier:
            _barrier([(x ^ (k >> 1), y ^ (k & 1), c) for k in (1, 2)])
        for j in range(len(ins)):
            for k in (1, 2):
                guarded(owns(j, qm ^ 3), lambda j=j, k=k: first_hop(j, k).start())
        for j in range(len(ins)):
            for k in (1, 2):
                guarded(owns(j, qm ^ k), lambda j=j, k=k: direct(j, k).start())
            guarded(owns(j, qm), lambda j=j: local(j).start())

    def forward():
        for j in range(len(ins)):
            for k in (1, 2):
                def pass_on(j=j, k=k):
                    first_hop(j, 3 - k).wait_recv()
                    second_hop(j, k).start()
                guarded(owns(j, qm ^ k), pass_on)

    def finish():
        for j in range(len(ins)):
            for k in (1, 2):
                guarded(owns(j, qm ^ k), lambda j=j, k=k: direct(j, k).wait_send())
                guarded(owns(j, qm ^ k), lambda j=j, k=k: second_hop(j, k).wait_send())
                guarded(owns(j, qm ^ 3), lambda j=j, k=k: first_hop(j, k).wait_send())
                guarded(owns(j, qm), lambda j=j, k=k: direct(j, k).wait_recv())
                guarded(owns(j, qm), lambda j=j, k=k: second_hop(j, k).wait_recv())
            guarded(owns(j, qm), lambda j=j: local(j).wait())

    return start, forward, finish


def _front(c, c_ctx, ada_w, ada_b, w_in, w_out, me):
    nloc = ada_w.shape[1]

    def body(me_ref, c_ref, cc_ref, aw_ref, ab_ref, win_ref, wout_ref,
             wfull_ref, woutb_ref, modx_ref, modc_ref, call_ref,
             wb_s, part_s, parts_s, w_send, w_recv, w_local, s_send, s_recv):
        x, y, cidx = lax.axis_index("x"), lax.axis_index("y"), lax.axis_index("c")
        me = me_ref[0]
        wb_s[...] = win_ref[...].astype(BF16)
        woutb_ref[...] = wout_ref[...].astype(BF16)
        start, forward, finish = _gather2_ops([wb_s], [wfull_ref], ["agc"], w_send, w_recv, w_local)
        start()

        def small_gather(src, my_slot, stage):
            copies = []
            for k in range(1, N_DEV):
                peer = (x ^ (k >> 2), y ^ ((k >> 1) & 1), cidx ^ (k & 1))
                cp = pltpu.make_async_remote_copy(src_ref=src, dst_ref=my_slot, send_sem=s_send.at[stage, k - 1],
                                                  recv_sem=s_recv.at[stage, k - 1], device_id=peer,
                                                  device_id_type=MESH)
                cp.start()
                copies.append(cp)
            pltpu.sync_copy(src, my_slot)
            for cp in copies:
                cp.wait()

        small_gather(c_ref, call_ref.at[pl.ds(me, 1), :], 0)
        off = pl.multiple_of(me * nloc, 128)
        b = ab_ref[:, pl.ds(off, nloc)]
        w = aw_ref[...]
        sx, _ = _silu_and_grad(call_ref[...])
        sc, _ = _silu_and_grad(jnp.broadcast_to(cc_ref[...], (8, D)))
        part_s[0:8, :] = _dot(sx, w) + b
        part_s[8:16, :] = _dot(sc, w) + b
        small_gather(part_s, parts_s.at[me], 1)
        mine = _rows((16, nloc)) == me
        for j in range(N_DEV):
            pj = parts_s[j]
            modx_ref[:, j * nloc:(j + 1) * nloc] = jnp.sum(jnp.where(mine, pj, 0.0), axis=0, keepdims=True)
            modc_ref[:, j * nloc:(j + 1) * nloc] = pj[8:9, :]
        forward()
        finish()

    return _call(
        body, name="front",
        out_shape=[jax.ShapeDtypeStruct((D, D_IN), BF16), jax.ShapeDtypeStruct(w_out.shape, BF16),
                   jax.ShapeDtypeStruct((1, 3 * D), F32), jax.ShapeDtypeStruct((1, 3 * D), F32),
                   jax.ShapeDtypeStruct((N_DEV, D), F32)],
        in_specs=[pl.BlockSpec(memory_space=pltpu.SMEM)] + [VMEM] * 6, out_specs=[HBM, VMEM, VMEM, VMEM, VMEM],
        scratch_shapes=[pltpu.VMEM(w_in.shape, BF16), pltpu.VMEM((16, nloc), F32),
                        pltpu.VMEM((N_DEV, 16, nloc), F32)] + _gather2_sems(1) +
                       [pltpu.SemaphoreType.DMA((2, N_DEV - 1)), pltpu.SemaphoreType.DMA((2, N_DEV - 1))],
        compiler_params=pltpu.CompilerParams(vmem_limit_bytes=VMEM_LIMIT, has_side_effects=True),
    )(me, c, c_ctx, ada_w, ada_b, w_in, w_out)


ARRIVAL = (0, 1, 2, 4, 3, 5, 6, 7)


def _front_project(xr, c, c_ctx, ada_w, ada_b, ng, w_in, w_out, cw, lam, me):
    nloc = ada_w.shape[1]
    ws = W_IN_SHARD
    order = me[0] ^ jnp.asarray(ARRIVAL, jnp.int32)

    def body(ord_ref, x_ref, c_ref, cc_ref, aw_ref, ab_ref, ng_ref, win_ref, wout_ref, cw_ref, lam_ref,
             z_ref, hn_ref, wfull_ref, woutb_ref, modx_ref, modc_ref, call_ref, cwf_ref, lamf_ref,
             wv, call_s, part_s, parts_s, w_send, w_recv, hbm_sems, s_send, s_recv, g_send, g_recv, g_local):
        t = pl.program_id(0)
        x, y, cidx = lax.axis_index("x"), lax.axis_index("y"), lax.axis_index("c")
        me_i = ord_ref[0]
        sibling = (x, y, 1 - cidx)
        chips = [(x ^ (k >> 1), y ^ (k & 1)) for k in (1, 2, 3)]
        g_start, g_pass, g_finish = _gather2_ops([cw_ref, lam_ref], [cwf_ref, lamf_ref], ["agc", "agc"],
                                                 g_send, g_recv, g_local)

        def shard_copy(k, px, py, pc, to, half=None):
            slot = wv.at[4 * px + 2 * py + pc]
            if half is not None:
                slot = slot.at[pl.ds(half * (D // 2), D // 2), :]
            return pltpu.make_async_remote_copy(src_ref=slot, dst_ref=slot, send_sem=w_send.at[k],
                                                recv_sem=w_recv.at[k], device_id=to, device_id_type=MESH)

        def small_gather(src, my_slot, stage):
            copies = []
            for k in range(1, N_DEV):
                peer = (x ^ (k >> 2), y ^ ((k >> 1) & 1), cidx ^ (k & 1))
                cp = pltpu.make_async_remote_copy(src_ref=src, dst_ref=my_slot, send_sem=s_send.at[stage, k - 1],
                                                  recv_sem=s_recv.at[stage, k - 1], device_id=peer,
                                                  device_id_type=MESH)
                cp.start()
                copies.append(cp)
            pltpu.sync_copy(src, my_slot)
            return copies

        def finish_small(copies):
            for cp in copies:
                cp.wait()

        def to_neighbours(half):
            for i in (0, 1):
                shard_copy(1 + i, x, y, cidx, (*chips[i], cidx), half=half).start()

        @pl.when(t == 0)
        def _():
            _barrier([(x ^ (k >> 2), y ^ ((k >> 1) & 1), cidx ^ (k & 1)) for k in range(1, N_DEV)])
            g_start()
            wv[me_i] = win_ref[...].astype(BF16)
            woutb_ref[...] = wout_ref[...].astype(BF16)
            shard_copy(0, x, y, cidx, sibling).start()
            finish_small(small_gather(c_ref, call_s.at[pl.ds(me_i, 1), :], 0))
            to_neighbours(0)
            call_ref[...] = call_s[...]
            off = pl.multiple_of(me_i * nloc, 128)
            b = ab_ref[:, pl.ds(off, nloc)]
            w = aw_ref[...]
            sx, _ = _silu_and_grad(call_s[...])
            sc, _ = _silu_and_grad(jnp.broadcast_to(cc_ref[...], (8, D)))
            part_s[0:8, :] = _dot(sx, w) + b
            part_s[8:16, :] = _dot(sc, w) + b
            parts_sent = small_gather(part_s, parts_s.at[me_i], 1)
            to_neighbours(1)
            finish_small(parts_sent)
            mine = _rows((16, nloc)) == me_i
            for j in range(N_DEV):
                pj = parts_s[j]
                modx_ref[:, j * nloc:(j + 1) * nloc] = jnp.sum(jnp.where(mine, pj, 0.0), axis=0, keepdims=True)
                modc_ref[:, j * nloc:(j + 1) * nloc] = pj[8:9, :]
            shift, scale1, ngv = modx_ref[:, 0:D], 1.0 + modx_ref[:, D:2 * D], ng_ref[...]
            for r in range(L // ROWS):
                rsl = slice(r * ROWS, (r + 1) * ROWS)
                xv = x_ref[rsl, :]
                rs = lax.rsqrt(jnp.mean(xv * xv, axis=-1, keepdims=True) + NORM_EPS)
                hn_ref[rsl, :] = ((xv * rs * ngv) * scale1 + shift).astype(BF16)

        @pl.when(t == 1)
        def _():
            shard_copy(0, x, y, 1 - cidx, sibling).wait_recv()
            g_pass()

        for i in (0, 1):
            @pl.when(t == ARRIVAL.index((2, 4)[i]))
            def _(i=i):
                shard_copy(1 + i, *chips[i], cidx, sibling).wait_recv()
                shard_copy(4 + i, *chips[i], cidx, sibling).start()
                shard_copy((7, 3)[i], *chips[i], cidx, (*chips[1 - i], cidx), half=i).start()

        @pl.when(t == ARRIVAL.index(6))
        def _():
            shard_copy(3, *chips[2], cidx, sibling, half=1).wait_recv()
            shard_copy(7, *chips[2], cidx, sibling, half=0).wait_recv()
            shard_copy(6, *chips[2], cidx, sibling).start()

        for i in range(3):
            @pl.when(t == ARRIVAL.index((3, 5, 7)[i]))
            def _(i=i):
                shard_copy(4 + i, *chips[i], 1 - cidx, sibling).wait_recv()

        @pl.when(t == 2)
        def _():
            g_finish()

        dev = ord_ref[t]
        for r in range(L // (2 * ROWS)):
            rsl = slice(r * 2 * ROWS, (r + 1) * 2 * ROWS)
            z_ref[rsl, :] = _dot(hn_ref[rsl, :], wv[dev])
        col = pl.ds(pl.multiple_of(dev * ws, 128), ws)
        pltpu.make_async_copy(wv.at[dev], wfull_ref.at[:, col], hbm_sems.at[t]).start()

        @pl.when(t == N_DEV - 1)
        def _():
            for k in (0, 1, 2, 4, 5, 6):
                shard_copy(k, x, y, cidx, sibling).wait_send()
            for k in (3, 7):
                shard_copy(k, x, y, cidx, sibling, half=0).wait_send()
            for s in range(N_DEV):
                pltpu.make_async_copy(wv.at[0], wfull_ref.at[:, pl.ds(0, ws)], hbm_sems.at[s]).wait()

    const = lambda *shape: pl.BlockSpec(shape, lambda t, o: (0,) * len(shape))
    once = lambda *shape: pl.BlockSpec(shape, lambda t, o: (0,) * len(shape), pipeline_mode=pl.Buffered(1))
    return _call(
        body, name="front_project",
        out_shape=[jax.ShapeDtypeStruct((L, D_IN), F32), jax.ShapeDtypeStruct((L, D), BF16),
                   jax.ShapeDtypeStruct((D, D_IN), BF16), jax.ShapeDtypeStruct(w_out.shape, BF16),
                   jax.ShapeDtypeStruct((1, 3 * D), F32), jax.ShapeDtypeStruct((1, 3 * D), F32),
                   jax.ShapeDtypeStruct((N_DEV, D), F32), jax.ShapeDtypeStruct((CONV_W, D), F32),
                   jax.ShapeDtypeStruct((2, D), F32)],
        grid_spec=pltpu.PrefetchScalarGridSpec(
            num_scalar_prefetch=1, grid=(N_DEV,),
            in_specs=[once(L, D), const(1, D), const(1, D), once(D, nloc), const(1, 3 * D), const(1, D),
                      once(D, ws), once(*w_out.shape), HBM, HBM],
            out_specs=[pl.BlockSpec((L, ws), lambda t, o: (0, o[t])), const(L, D), HBM, const(*w_out.shape),
                       const(1, 3 * D), const(1, 3 * D), const(N_DEV, D), HBM, HBM],
            scratch_shapes=[pltpu.VMEM((N_DEV, D, ws), BF16), pltpu.VMEM((N_DEV, D), F32), pltpu.VMEM((16, nloc), F32),
                            pltpu.VMEM((N_DEV, 16, nloc), F32), pltpu.SemaphoreType.DMA((8,)),
                            pltpu.SemaphoreType.DMA((8,)), pltpu.SemaphoreType.DMA((N_DEV,)),
                            pltpu.SemaphoreType.DMA((2, N_DEV - 1)), pltpu.SemaphoreType.DMA((2, N_DEV - 1))]
            + _gather2_sems(2)),
        compiler_params=pltpu.CompilerParams(dimension_semantics=("arbitrary",), vmem_limit_bytes=VMEM_LIMIT,
                                             has_side_effects=True, collective_id=10),
    )(order, xr, c, c_ctx, ada_w, ada_b, ng, w_in, w_out, pltpu.with_memory_space_constraint(cw, pltpu.HBM),
      pltpu.with_memory_space_constraint(lam, pltpu.HBM))


def _project(xr, mod, ng, w, ncols, tm, name, gather=None, gather_modes=()):
    rows = xr.shape[0]
    steps = rows // tm
    ng_ = len(gather or ())

    def body(x_ref, sh_ref, sc_ref, ng_ref, w_ref, *rest):
        z_ref, hn_ref = rest[ng_:ng_ + 2]
        if ng_:
            start, forward, finish = _gather2_ops(rest[:ng_], rest[ng_ + 2:2 * ng_ + 2], gather_modes,
                                                  *rest[2 * ng_ + 2:])
            pl.when(pl.program_id(0) == 0)(start)
            pl.when(pl.program_id(0) == steps // 2)(forward)
        x = x_ref[...]
        rs = lax.rsqrt(jnp.mean(x * x, axis=-1, keepdims=True) + NORM_EPS)
        hn = (x * rs * ng_ref[...]) * (1.0 + sc_ref[...]) + sh_ref[...]
        hb = hn.astype(BF16)
        hn_ref[...] = hb
        for n in range(ncols // D):
            z_ref[:, n * D:(n + 1) * D] = _dot(hb, w_ref[:, n * D:(n + 1) * D])
        if ng_:
            pl.when(pl.program_id(0) == steps - 1)(finish)

    vec = pl.BlockSpec((1, D), lambda i: (0, 0))
    gathered = _gather2_shapes(gather, gather_modes) if ng_ else []
    return _call(
        body, name=name, grid=(steps,),
        out_shape=[jax.ShapeDtypeStruct((rows, ncols), F32), jax.ShapeDtypeStruct((rows, D), BF16)] + gathered,
        in_specs=[pl.BlockSpec((tm, D), lambda i: (i, 0)), vec, pl.BlockSpec((1, D), lambda i: (0, 1)), vec,
                  pl.BlockSpec((D, ncols), lambda i: (0, 0), pipeline_mode=pl.Buffered(1))] + [HBM] * ng_,
        out_specs=[pl.BlockSpec((tm, ncols), lambda i: (i, 0)), pl.BlockSpec((tm, D), lambda i: (i, 0))] + [HBM] * ng_,
        scratch_shapes=_gather2_sems(ng_) if ng_ else [],
        compiler_params=pltpu.CompilerParams(dimension_semantics=("arbitrary",), vmem_limit_bytes=VMEM_LIMIT,
                                             has_side_effects=bool(ng_)),
    )(xr, mod, mod, ng, w, *[pltpu.with_memory_space_constraint(a, pltpu.HBM) for a in gather or ()])


def _scan_pair(af_ref, uf_ref, hf_ref, h0f, ab_ref, ub_ref, hb_ref, h0b, t_len):
    span = 8 * SCAN_BLOCKS
    nit = t_len // span
    rows = _rows((8, HD))

    def local_scan(a, b, forward):
        for s in (1, 2, 4):
            sh = s if forward else 8 - s
            m = rows >= s if forward else rows < 8 - s
            b = a * jnp.where(m, pltpu.roll(b, sh, 0), 0.0) + b
            a = a * jnp.where(m, pltpu.roll(a, sh, 0), 1.0)
        return a, b

    def span_scan(a_ref, u_ref, h_ref, off, carry, forward):
        order = range(SCAN_BLOCKS) if forward else range(SCAN_BLOCKS - 1, -1, -1)
        last = slice(7, 8) if forward else slice(0, 1)
        for q in order:
            rs = pl.ds(off + 8 * q, 8)
            a, b = local_scan(a_ref[rs, :], u_ref[rs, :], forward)
            h_ref[rs, :] = b + a * carry
            carry = a[last, :] * carry + b[last, :]
        return carry

    def body(k, carry):
        cf, cb = carry
        cf = span_scan(af_ref, uf_ref, hf_ref, pl.multiple_of(k * span, span), cf, True)
        cb = span_scan(ab_ref, ub_ref, hb_ref, pl.multiple_of((nit - 1 - k) * span, span), cb, False)
        return cf, cb

    return lax.fori_loop(0, nit, body, (h0f, h0b))


SCAN_BLOCKS = 8


def _shifted(pad_ref, x, offsets, before=0.0, after=0.0):
    n = x.shape[0]
    pad_ref[0:8, :] = jnp.broadcast_to(jnp.asarray(before, F32), (8, x.shape[1]))
    pad_ref[8:8 + n, :] = x
    pad_ref[8 + n:16 + n, :] = jnp.broadcast_to(jnp.asarray(after, F32), (8, x.shape[1]))
    return [pad_ref[8 + o:8 + o + n, :] for o in offsets]


def _conv(xa, cw, cb, pad_ref):
    xm1, xp1, xp2 = _shifted(pad_ref, xa, (-1, 1, 2))
    return xm1 * cw[0:1, :] + xa * cw[1:2, :] + xp1 * cw[2:3, :] + xp2 * cw[3:4, :] + cb


def _gates(xc, wa, wx, ba, bx, nsp):
    xb = xc.astype(BF16)
    r = _sigmoid(_dot(xb, wa) + ba)
    i = _sigmoid(_dot(xb, wx) + bx)
    log_a = r * nsp
    a = jnp.exp(log_a)
    g2 = jnp.tanh(log_a) * (-1.0 - a * a)
    rg = lax.rsqrt(jnp.maximum(g2, 1e-30))
    return r, i, a, g2 * rg, rg


def _lru_param_specs():
    h4 = pl.BlockSpec((2, 1, HD, HD), lambda h: (0, h, 0, 0))
    v2 = pl.BlockSpec((2, HD), lambda h: (0, h))
    b16 = pl.BlockSpec((2 * HEADS, HD), lambda h: (0, 0))
    return dict(
        xa=pl.BlockSpec((L, HD), lambda h: (0, h)), xac=pl.BlockSpec((LC, HD), lambda h: (0, h)),
        cw=pl.BlockSpec((CONV_W, HD), lambda h: (0, h)), cb=pl.BlockSpec((1, HD), lambda h: (0, h)), h4=h4, v2=v2,
        b16=b16)


def _bias_row(ref, d):
    mask = _rows((2 * HEADS, HD)) == d * HEADS + pl.program_id(0)
    return jnp.sum(jnp.where(mask, ref[...], 0.0), axis=0, keepdims=True), mask


def _lru_forward(zx, zc, cw, cb, wa, wx, ba, bx, lam, gather, gather_modes):
    ng_ = len(gather)

    def body(xa_ref, xac_ref, cw_ref, cb_ref, wa_ref, wx_ref, ba_ref, bx_ref, lam_ref, *rest):
        yl_ref = rest[ng_]
        af, uf, hf, ab, ub, hb, pad_s = rest[2 * ng_ + 1:2 * ng_ + 8]
        start, pass_on, finish = _gather2_ops(rest[:ng_], rest[ng_ + 1:2 * ng_ + 1], gather_modes,
                                              *rest[2 * ng_ + 8:], barrier=True)
        pl.when(pl.program_id(0) == 0)(start)
        pl.when(pl.program_id(0) == HEADS // 2)(pass_on)
        pl.when(pl.program_id(0) == HEADS - 1)(finish)
        cwv, cbv = cw_ref[...], cb_ref[...]
        nsp = (-LRU_C) * _softplus(-lam_ref[...])

        def forward(xa, t_len, h0f, h0b):
            xc = _conv(xa, cwv, cbv, pad_s)
            for d, (a_ref, u_ref) in enumerate(((af, uf), (ab, ub))):
                _, i, a, gamma, _ = _gates(xc, wa_ref[d, 0].astype(BF16), wx_ref[d, 0].astype(BF16),
                                           _bias_row(ba_ref, d)[0], _bias_row(bx_ref, d)[0], nsp[d:d + 1, :])
                a_ref[0:t_len, :] = a
                u_ref[0:t_len, :] = gamma * (i * xc)
            return _scan_pair(af, uf, hf, h0f, ab, ub, hb, h0b, t_len)

        z = jnp.zeros((1, HD), F32)
        h0f, h0b = forward(xac_ref[...], LC, z, z)
        forward(xa_ref[...], L, h0f, h0b)
        yl_ref[...] = hf[...] + hb[...]

    s = _lru_param_specs()
    return _call(
        body, name="lru_forward", grid=(HEADS,),
        out_shape=[jax.ShapeDtypeStruct((L, D), F32)] + _gather2_shapes(gather, gather_modes),
        in_specs=[s["xa"], s["xac"], s["cw"], s["cb"], s["h4"], s["h4"], s["b16"], s["b16"], s["v2"]] + [HBM] * ng_,
        out_specs=[pl.BlockSpec((L, HD), lambda h: (0, h))] + [HBM] * ng_,
        scratch_shapes=[pltpu.VMEM((L, HD), F32)] * 6 + [pltpu.VMEM((L + 16, HD), F32)] + _gather2_sems(ng_),
        compiler_params=pltpu.CompilerParams(dimension_semantics=("arbitrary",), vmem_limit_bytes=VMEM_LIMIT,
                                             has_side_effects=True, collective_id=5),
    )(zx, zc, cw, cb, wa, wx, ba, bx, lam, *[pltpu.with_memory_space_constraint(a, pltpu.HBM) for a in gather])


def _lru_backward(zx, zc, dyl, dz, cw, cb, wa, wx, ba, bx, lam, chip_sums, first_chips=None):
    nr = len(chip_sums)

    def body(xa_ref, xac_ref, dyl_ref, dz_in, cw_ref, cb_ref, wa_ref, wx_ref, ba_ref, bx_ref, lam_ref, *rest):
        (dxa_ref, dxac_ref, dwa_ref, dwx_ref, dba_ref, dbx_ref, dlam_ref, dcw_ref,
         dcb_ref) = rest[nr:nr + 9]
        main_s, ctx_s, pad_s = rest[3 * nr + 9:3 * nr + 12]
        if nr:
            start, forward, finish = _chips_ops(rest[:nr], rest[nr + 9:2 * nr + 9], rest[2 * nr + 9:3 * nr + 9],
                                                *rest[3 * nr + 12:], first_chips=first_chips, barrier=True)
            pl.when(pl.program_id(0) == 0)(start)
            pl.when(pl.program_id(0) == HEADS // 2)(forward)
            pl.when(pl.program_id(0) == HEADS - 1)(finish)
        del dz_in

        @pl.when(pl.program_id(0) == 0)
        def _():
            dba_ref[...] = jnp.zeros_like(dba_ref)
            dbx_ref[...] = jnp.zeros_like(dbx_ref)

        cwv, cbv = cw_ref[...], cb_ref[...]
        lamv = lam_ref[...]
        sp = _softplus(-lamv)
        nsp = (-LRU_C) * sp
        z = jnp.zeros((1, HD), F32)

        def wmat(ref, d):
            return ref[d, 0].astype(BF16)

        def workspace(s):
            return dict(a=(s.at[0], s.at[1]), u=(s.at[2], s.at[3]), h=(s.at[4], s.at[5]), rho=(s.at[6], s.at[7]),
                        saved=(tuple(s.at[8 + k] for k in range(4)), tuple(s.at[12 + k] for k in range(4))),
                        xc=s.at[16])

        def forward(ws, xa, t_len, h0f, h0b):
            xc = _conv(xa, cwv, cbv, pad_s)
            ws["xc"][...] = xc
            for d in (0, 1):
                vals = _gates(xc, wmat(wa_ref, d), wmat(wx_ref, d), _bias_row(ba_ref, d)[0],
                              _bias_row(bx_ref, d)[0], nsp[d:d + 1, :])
                r, i, a, gamma, rg = vals
                ws["a"][d][...] = a
                ws["u"][d][...] = gamma * (i * xc)
                for ref, val in zip(ws["saved"][d], (r, i, gamma, rg)):
                    ref[...] = val
            return _scan_pair(ws["a"][0], ws["u"][0], ws["h"][0], h0f, ws["a"][1], ws["u"][1], ws["h"][1], h0b,
                              t_len)

        def backward(ws, xa, t_len, h0f, h0b, dhf, dhb, first):
            xc = ws["xc"][...]
            (af, ab), (uf, ub), (hf, hb), (rf, rb) = ws["a"], ws["u"], ws["h"], ws["rho"]
            uf[...] = ab[...] * dhb
            ub[...] = af[...] * dhf
            rho_b_last, rho_f_first = _scan_pair(ab, uf, rb, z, af, ub, rf, z, t_len)
            dxc = jnp.zeros((t_len, HD), F32)
            dsp = []
            for d in (0, 1):
                r, i, gamma, rg = (ref[...] for ref in ws["saved"][d])
                a = ws["a"][d][...]
                if d == 0:
                    lam_t = dhf + _shifted(pad_s, rf[...], (1,))[0]
                    h_prev = _shifted(pad_s, hf[...], (-1,), before=h0f)[0]
                else:
                    lam_t = dhb + _shifted(pad_s, rb[...], (-1,))[0]
                    h_prev = _shifted(pad_s, hb[...], (1,), after=h0b)[0]
                da = lam_t * h_prev
                lx = lam_t * xc
                d_i = lx * gamma
                d_gamma = lx * i
                dxc = dxc + lam_t * (gamma * i)
                d_log_a = a * (da - d_gamma * (a * rg))
                dsp.append(jnp.sum(d_log_a * r, axis=0, keepdims=True) * (-LRU_C))
                d_pre_r = d_log_a * nsp[d:d + 1, :] * (r * (1.0 - r))
                d_pre_i = d_i * (i * (1.0 - i))
                prb, pib, xb = d_pre_r.astype(BF16), d_pre_i.astype(BF16), xc.astype(BF16)
                dxc = dxc + _dot_nt(prb, wmat(wa_ref, d)) + _dot_nt(pib, wmat(wx_ref, d))
                g_wa, g_wx = _dot_tn(xb, prb), _dot_tn(xb, pib)
                g_ba = jnp.sum(d_pre_r, axis=0, keepdims=True)
                g_bx = jnp.sum(d_pre_i, axis=0, keepdims=True)
                mask = _bias_row(ba_ref, d)[1]
                dba_ref[...] += jnp.where(mask, g_ba, 0.0)
                dbx_ref[...] += jnp.where(mask, g_bx, 0.0)
                if first:
                    dwa_ref[d, 0] = g_wa
                    dwx_ref[d, 0] = g_wx
                else:
                    dwa_ref[d, 0] += g_wa
                    dwx_ref[d, 0] += g_wx
            g_lam = jnp.concatenate(dsp, axis=0) * (-_sigmoid(-lamv))
            dm1, dp1, dm2 = _shifted(pad_s, dxc, (-1, 1, -2))
            dxa = dp1 * cwv[0:1, :] + dxc * cwv[1:2, :] + dm1 * cwv[2:3, :] + dm2 * cwv[3:4, :]
            xm1, xp1, xp2 = _shifted(pad_s, xa, (-1, 1, 2))
            g_cw = jnp.concatenate([jnp.sum(dxc * v, axis=0, keepdims=True) for v in (xm1, xa, xp1, xp2)], axis=0)
            g_cb = jnp.sum(dxc, axis=0, keepdims=True)
            if first:
                dlam_ref[...] = g_lam
                dcw_ref[...] = g_cw
                dcb_ref[...] = g_cb
            else:
                dlam_ref[...] += g_lam
                dcw_ref[...] += g_cw
                dcb_ref[...] += g_cb
            return dxa, rho_f_first, rho_b_last

        ws_x, ws_c = workspace(main_s), workspace(ctx_s)
        h0f, h0b = forward(ws_c, xac_ref[...], LC, z, z)
        forward(ws_x, xa_ref[...], L, h0f, h0b)
        dh = dyl_ref[...]
        dxa, dh0f, dh0b = backward(ws_x, xa_ref[...], L, h0f, h0b, dh, dh, True)
        dxa_ref[...] = dxa.astype(BF16)
        rc = _rows((LC, HD))
        dxac, _, _ = backward(ws_c, xac_ref[...], LC, z, z, jnp.where(rc == LC - 1, dh0f, 0.0),
                              jnp.where(rc == 0, dh0b, 0.0), False)
        dxac_ref[...] = dxac.astype(BF16)

    s = _lru_param_specs()
    col = lambda r: pl.BlockSpec((r, HD), lambda h: (0, h))
    return _call(
        body, name="lru_backward", grid=(HEADS,),
        out_shape=[jax.ShapeDtypeStruct((L, D_IN), BF16), jax.ShapeDtypeStruct((LC, D), BF16),
                   jax.ShapeDtypeStruct((2, HEADS, HD, HD), F32), jax.ShapeDtypeStruct((2, HEADS, HD, HD), F32),
                   jax.ShapeDtypeStruct((2 * HEADS, HD), F32), jax.ShapeDtypeStruct((2 * HEADS, HD), F32),
                   jax.ShapeDtypeStruct((2, D), F32), jax.ShapeDtypeStruct((CONV_W, D), F32),
                   jax.ShapeDtypeStruct((1, D), F32)] + [jax.ShapeDtypeStruct((4,) + a.shape[1:], a.dtype)
                                                          for a in chip_sums] + _chips_stage_shapes(chip_sums),
        in_specs=[s["xa"], s["xac"], col(L), pl.BlockSpec(memory_space=pl.ANY), s["cw"], s["cb"], s["h4"], s["h4"],
                  s["b16"], s["b16"], s["v2"]] + [HBM] * nr,
        out_specs=[col(L), col(LC), s["h4"], s["h4"], s["b16"], s["b16"], s["v2"], col(CONV_W), col(1)]
        + [HBM] * (2 * nr),
        scratch_shapes=[pltpu.VMEM((17, L, HD), F32), pltpu.VMEM((17, LC, HD), F32), pltpu.VMEM((L + 16, HD), F32)]
        + (_chips_sems(nr) if nr else []),
        input_output_aliases={3: 0},
        compiler_params=pltpu.CompilerParams(dimension_semantics=("arbitrary",), vmem_limit_bytes=VMEM_LIMIT,
                                             has_side_effects=True, collective_id=6 if nr else None),
    )(zx, zc, dyl, dz, cw, cb, wa, wx, ba, bx, lam, *[pltpu.with_memory_space_constraint(a, pltpu.HBM)
                                                       for a in chip_sums])


def _mixer_loss(x, tgt, zx, yl, gx, fg, lng, lnb, ws, wst, bst, wout, tm):
    ncht = tm // CHUNK

    def body(x_ref, t_ref, ga_ref, u_ref, v_ref, gb_ref, yl_ref, gx_ref, fg_ref, lng_ref, lnb_ref, ws_ref, wst_ref,
             bst_ref, wout_ref,
             dz_ref, dyl_ref, dxn_ref, y_s, do_ref, dws_ref, dbst_ref, vec_ref,
             vn_s, mix_s, dm_s, dvn_s):
        step = pl.program_id(0)

        @pl.when(step == 0)
        def _():
            dws_ref[...] = jnp.zeros_like(dws_ref)
            dbst_ref[...] = jnp.zeros_like(dbst_ref)
            vec_ref[...] = jnp.zeros_like(vec_ref)

        u, v = u_ref[...], v_ref[...]
        ug, dug_du = _gelu_and_grad(u)
        vg, dvg_dv = _gelu_and_grad(v)
        mu = jnp.mean(vg, axis=-1, keepdims=True)
        vc = vg - mu
        rstd = lax.rsqrt(jnp.mean(vc * vc, axis=-1, keepdims=True) + LN_EPS)
        vhat = vc * rstd
        lngv = lng_ref[...]
        vn_s[...] = (vhat * lngv + lnb_ref[...]).astype(BF16)
        for ch in range(ncht):
            rs = slice(ch * CHUNK, (ch + 1) * CHUNK)
            for g in range(HEADS):
                cs = slice(g * HD, (g + 1) * HD)
                mix_s[rs, cs] = _dot(ws_ref[g], vn_s[rs, cs]) + bst_ref[:, g:g + 1]
        mixed = mix_s[...]
        ga, gb, yl = ga_ref[...], gb_ref[...], yl_ref[...]
        sga, dsga = _silu_and_grad(ga)
        sgb, dsgb = _silu_and_grad(gb)
        ys = ug * mixed
        y_s[:, 0:D] = (yl * sga).astype(BF16)
        y_s[:, D:D_MIX] = (ys * sgb).astype(BF16)
        o = _dot(y_s[...], wout_ref[...])
        gxv, fgv = gx_ref[...], fg_ref[...]
        xn = x_ref[...] + gxv * o
        rs2 = lax.rsqrt(jnp.mean(xn * xn, axis=-1, keepdims=True) + NORM_EPS)
        xh = xn * rs2
        diff = xh * fgv - t_ref[...]
        vec_ref[R_LOSS:R_LOSS + 1, :] += jnp.full((1, D), jnp.sum(diff * diff) * (0.5 / D), F32)
        dout = diff * (1.0 / D)
        w = dout * fgv
        dxn = rs2 * (w - xh * jnp.mean(w * xh, axis=-1, keepdims=True))
        dxn_ref[...] = dxn
        vec_ref[0:1, :] += jnp.sum(dxn * o, axis=0, keepdims=True)
        vec_ref[1:2, :] += jnp.sum(dout * xh, axis=0, keepdims=True)
        dob = (dxn * gxv).astype(BF16)
        do_ref[...] = dob
        dy = _dot_nt(dob, wout_ref[...])
        dya, dyb = dy[:, 0:D], dy[:, D:D_MIX]
        dyl_ref[...] = dya * sga
        dys = dyb * sgb
        dz_ref[:, 0:D] = jnp.zeros((tm, D), BF16)
        dz_ref[:, D:2 * D] = (dya * yl * dsga).astype(BF16)
        dz_ref[:, 2 * D:3 * D] = (dys * mixed * dug_du).astype(BF16)
        dz_ref[:, 4 * D:5 * D] = (dyb * ys * dsgb).astype(BF16)
        dm = dys * ug
        dm_s[...] = dm.astype(BF16)
        for g in range(HEADS):
            cs = slice(g * HD, (g + 1) * HD)
            dbst_ref[:, g:g + 1] += sum(jnp.sum(dm[ch * CHUNK:(ch + 1) * CHUNK, cs], axis=1, keepdims=True)
                                        for ch in range(ncht))
            for ch in range(ncht):
                rs = slice(ch * CHUNK, (ch + 1) * CHUNK)
                dws_ref[g] += _dot_nt(dm_s[rs, cs], vn_s[rs, cs])
                dvn_s[rs, cs] = _dot(wst_ref[g], dm_s[rs, cs])
        dvn = dvn_s[...]
        vec_ref[2:3, :] += jnp.sum(dvn * vhat, axis=0, keepdims=True)
        vec_ref[3:4, :] += jnp.sum(dvn, axis=0, keepdims=True)
        dvh = dvn * lngv
        dvg = rstd * (dvh - jnp.mean(dvh, axis=-1, keepdims=True) - vhat * jnp.mean(dvh * vhat, axis=-1, keepdims=True))
        dz_ref[:, 3 * D:4 * D] = (dvg * dvg_dv).astype(BF16)

    tile = pl.BlockSpec((tm, D), lambda i: (i, 0))
    zcol = lambda n: pl.BlockSpec((tm, D), lambda i: (i, n))
    vec = pl.BlockSpec((1, D), lambda i: (0, 0))
    full = lambda *s: pl.BlockSpec(s, lambda i: (0,) * len(s))
    return _call(
        body, name="mixer_loss", grid=(L // tm,),
        out_shape=[jax.ShapeDtypeStruct((L, D_IN), BF16), jax.ShapeDtypeStruct((L, D), F32),
                   jax.ShapeDtypeStruct((L, D), F32), jax.ShapeDtypeStruct((L, D_MIX), BF16),
                   jax.ShapeDtypeStruct((L, D), BF16),
                   jax.ShapeDtypeStruct((HEADS, CHUNK, CHUNK), F32), jax.ShapeDtypeStruct((CHUNK, HEADS), F32),
                   jax.ShapeDtypeStruct((8, D), F32)],
        in_specs=[tile, tile, zcol(1), zcol(2), zcol(3), zcol(4), tile, pl.BlockSpec((1, D), lambda i: (0, 2)),
                  vec, vec, vec,
                  full(HEADS, CHUNK, CHUNK), full(HEADS, CHUNK, CHUNK), full(CHUNK, HEADS),
                  pl.BlockSpec((D_MIX, D), lambda i: (0, 0), pipeline_mode=pl.Buffered(1))],
        out_specs=[pl.BlockSpec((tm, D_IN), lambda i: (i, 0)), tile, tile,
                   pl.BlockSpec((tm, D_MIX), lambda i: (i, 0)), tile,
                   full(HEADS, CHUNK, CHUNK), full(CHUNK, HEADS), full(8, D)],
        scratch_shapes=[pltpu.VMEM((tm, D), BF16), pltpu.VMEM((tm, D), F32),
                        pltpu.VMEM((tm, D), BF16), pltpu.VMEM((tm, D), F32)],
        compiler_params=_params("arbitrary"),
    )(x, tgt, zx, zx, zx, zx, yl, gx, fg, lng, lnb, ws, wst, bst, wout)


def _grad_w(a, b, a2, b2, tk, name, bw, first, nblocks, split, barrier_id):
    nk = a.shape[0] // tk
    m = a.shape[1]
    with_ctx = a2 is not None
    if split == "cols":
        slots, r, w = nblocks, m, bw // 2
        piece = lambda q, pc: (slice(None), slice(pc * w, (pc + 1) * w))
    else:
        slots, r, w = 4, m // 8, bw
        piece = lambda q, pc: (slice((2 * q + pc) * r, (2 * q + pc + 1) * r), slice(None))

    def body(*refs):
        a_ref, b_ref = refs[:2]
        a2_ref, b2_ref = refs[2:4] if with_ctx else (None, None)
        sums_ref, acc, mine_v, send_v, stage_v, send_sems, recv_sems = refs[4 if with_ctx else 2:]
        n, k = pl.program_id(0), pl.program_id(1)
        x, y, c = lax.axis_index("x"), lax.axis_index("y"), lax.axis_index("c")

        def to_sibling(s):
            return pltpu.make_async_remote_copy(src_ref=send_v.at[s], dst_ref=stage_v.at[s], send_sem=send_sems.at[s],
                                                recv_sem=recv_sems.at[s], device_id=(x, y, 1 - c),
                                                device_id_type=MESH)

        pl.when(jnp.logical_and(n == 0, k == 0))(_sibling_barrier)

        @pl.when(k == 0)
        def _():
            acc[...] = _dot_tn(a_ref[...], b_ref[...])

        if nk > 1:
            @pl.when(k > 0)
            def _():
                acc[...] += _dot_tn(a_ref[...], b_ref[...])

        if with_ctx:
            @pl.when(jnp.logical_and(k == nk - 1, n == 0))
            def _():
                acc[:, 0:b2_ref.shape[1]] += _dot_tn(a2_ref[...], b2_ref[...])

        def hand_over(s, q):
            for pc in (0, 1):
                @pl.when(c == pc)
                def _(pc=pc):
                    mine_v[s] = acc[piece(q, pc)]
                    send_v[s] = acc[piece(q, 1 - pc)].astype(BF16)
            to_sibling(s).start()

        for i in range(nblocks):
            @pl.when(jnp.logical_and(k == nk - 1, n == i))
            def _(i=i):
                if split == "cols":
                    hand_over(i, 0)
                else:
                    for q in range(4):
                        hand_over(q, q)

        @pl.when(jnp.logical_and(k == nk - 1, n == nblocks - 1))
        def _():
            for s in range(slots):
                to_sibling(s).wait_recv()
                sums_ref[s] = (mine_v[s] + stage_v[s].astype(F32)).astype(BF16)
            for s in range(slots):
                to_sibling(s).wait_send()

    in_specs = [pl.BlockSpec((tk, m), lambda n, k: (k, 0)), pl.BlockSpec((tk, bw), lambda n, k: (k, n + first))]
    args = [a, b]
    if with_ctx:
        in_specs += [pl.BlockSpec(a2.shape, lambda n, k: (0, 0)), pl.BlockSpec(b2.shape, lambda n, k: (0, 0))]
        args += [a2, b2]
    return _call(
        body, name=name, grid=(nblocks, nk), out_shape=jax.ShapeDtypeStruct((slots, r, w), BF16),
        in_specs=in_specs, out_specs=pl.BlockSpec((slots, r, w), lambda n, k: (0, 0, 0)),
        scratch_shapes=[pltpu.VMEM((m, bw), F32), pltpu.VMEM((slots, r, w), F32), pltpu.VMEM((slots, r, w), BF16),
                        pltpu.VMEM((slots, r, w), BF16), pltpu.SemaphoreType.DMA((slots,)),
                        pltpu.SemaphoreType.DMA((slots,))],
        compiler_params=pltpu.CompilerParams(dimension_semantics=("arbitrary", "arbitrary"),
                                             vmem_limit_bytes=VMEM_LIMIT, has_side_effects=True,
                                             collective_id=barrier_id),
    )(*args)


def _grad_rows(xr, dz, w, mod, ng, dres, ncols, tm, name, chip_sums=(), first_chips=None, dests=None):
    rows = xr.shape[0]
    steps = rows // tm
    with_dx = dres is not None
    nr = len(chip_sums)
    dests = [d for d in (dests or [None] * nr)]
    nd = sum(d is not None for d in dests)
    nin = 6 if with_dx else 5
    nout = 2 if with_dx else 1

    def body(*refs):
        if with_dx:
            x_ref, dz_ref, w_ref, sc_ref, ng_ref, dres_ref = refs[:nin]
            dx_ref, vec_ref = refs[nin + nr + nd:nin + nr + nd + nout]
        else:
            x_ref, dz_ref, w_ref, sc_ref, ng_ref = refs[:nin]
            (vec_ref,) = refs[nin + nr + nd:nin + nr + nd + nout]
        if nr:
            o0 = nin + nr + nd + nout
            start, forward, finish = _chips_ops(refs[nin:nin + nr], refs[o0:o0 + nr], refs[o0 + nr:o0 + 2 * nr],
                                                *refs[o0 + 2 * nr:], first_chips=first_chips, barrier=True)
            pl.when(pl.program_id(0) == 0)(start)
            pl.when(pl.program_id(0) == steps // 2)(forward)
            pl.when(pl.program_id(0) == steps - 1)(finish)

        @pl.when(pl.program_id(0) == 0)
        def _():
            vec_ref[...] = jnp.zeros_like(vec_ref)

        dhn = _dot_nt(dz_ref[...], w_ref[...])
        x = x_ref[...]
        rs = lax.rsqrt(jnp.mean(x * x, axis=-1, keepdims=True) + NORM_EPS)
        xh = x * rs
        ngv = ng_ref[...]
        y = xh * ngv
        vec_ref[0:1, :] += jnp.sum(dhn, axis=0, keepdims=True)
        vec_ref[1:2, :] += jnp.sum(dhn * y, axis=0, keepdims=True)
        dy = dhn * (1.0 + sc_ref[...])
        vec_ref[2:3, :] += jnp.sum(dy * xh, axis=0, keepdims=True)
        if with_dx:
            dxh = dy * ngv
            dx_ref[...] = dres_ref[...] + rs * (dxh - xh * jnp.mean(dxh * xh, axis=-1, keepdims=True))

    tile = pl.BlockSpec((tm, D), lambda i: (i, 0))
    vec = pl.BlockSpec((1, D), lambda i: (0, 0))
    in_specs = [tile, pl.BlockSpec((tm, ncols), lambda i: (i, 0)),
                pl.BlockSpec((D, ncols), lambda i: (0, 0), pipeline_mode=pl.Buffered(1)),
                pl.BlockSpec((1, D), lambda i: (0, 1)), vec]
    out_shape = [jax.ShapeDtypeStruct((8, D), F32)]
    out_specs = [pl.BlockSpec((8, D), lambda i: (0, 0))]
    args = [xr, dz, w, mod, ng]
    if with_dx:
        in_specs.append(tile)
        out_shape.insert(0, jax.ShapeDtypeStruct((rows, D), F32))
        out_specs.insert(0, tile)
        args.append(dres)
    aliases = {}
    for j, d in enumerate(dests):
        if d is not None:
            aliases[len(args) + nr + len(aliases)] = len(out_shape) + j
    in_specs += [HBM] * (nr + nd)
    out_specs += [HBM] * (2 * nr)
    out_shape += [jax.ShapeDtypeStruct((4,) + a.shape[1:], a.dtype) for a in chip_sums]
    out_shape += _chips_stage_shapes(chip_sums)
    args += [pltpu.with_memory_space_constraint(a, pltpu.HBM) for a in chip_sums]
    args += [pltpu.with_memory_space_constraint(d, pltpu.HBM) for d in dests if d is not None]
    return _call(body, name=name, grid=(steps,), out_shape=out_shape, in_specs=in_specs, out_specs=out_specs,
                 scratch_shapes=_chips_sems(nr) if nr else [], input_output_aliases=aliases,
                 compiler_params=pltpu.CompilerParams(dimension_semantics=("arbitrary",),
                                                      vmem_limit_bytes=VMEM_LIMIT, has_side_effects=bool(nr),
                                                      collective_id=7 if nr else None))(*args)


def _adamw(w, g, m, v):
    m = ADAM_B1 * m + (1.0 - ADAM_B1) * g
    v = ADAM_B2 * v + (1.0 - ADAM_B2) * (g * g)
    m_hat = m / (1.0 - ADAM_B1 ** ADAM_STEP)
    v_hat = v / (1.0 - ADAM_B2 ** ADAM_STEP)
    delta = -ADAM_LR * (m_hat / (jnp.sqrt(v_hat) + ADAM_EPS) + ADAM_WD * w)
    return delta, m, v


def _adamw_reduced(parts, w, m, v, tr, name):
    r, n = w.shape
    nparts = parts.shape[0]

    def body(p_ref, w_ref, m_ref, v_ref, g_ref, d_ref, mo_ref, vo_ref):
        g = p_ref[0].astype(F32)
        for i in range(1, nparts):
            g = g + p_ref[i].astype(F32)
        g_ref[...] = g
        d_ref[...], mo_ref[...], vo_ref[...] = _adamw(w_ref[...], g, m_ref[...], v_ref[...])

    tile = pl.BlockSpec((tr, n), lambda i: (i, 0))
    sds = jax.ShapeDtypeStruct((r, n), F32)
    return _call(
        body, name=name, grid=(r // tr,), out_shape=[sds] * 4,
        in_specs=[pl.BlockSpec((nparts, tr, n), lambda i: (0, i, 0)), tile, tile, tile], out_specs=[tile] * 4,
        compiler_params=_params("arbitrary"),
    )(parts, w, m, v)


R_GATE, R_FINAL_G, R_LN_G, R_LN_B, R_LOSS = 0, 1, 2, 3, 4
R_SH_X, R_SC_X, R_NG_X = 5, 6, 7
R_SH_C, R_SC_C, R_NG_C = 8, 9, 10
R_LAM, R_CW, R_CB = 11, 13, 17
PACK_ROWS = 24
Q_BA, Q_BX, Q_SGU_B, PACK128_ROWS = 0, 16, 32, 40


def _reduce_small(vp_all, vq_all, mat_parts, ada_w, me):
    nloc = ada_w.shape[1]
    nm = len(mat_parts)

    def body(me_ref, vp_ref, vq_ref, *refs):
        mp_refs, w_ref = refs[:nm], refs[nm]
        red_ref, redq_ref = refs[nm + 1:nm + 3]
        mat_refs = refs[nm + 3:2 * nm + 3]
        dmod_ref, gab_ref, cpart_ref, dmc_s = refs[2 * nm + 3:]
        red, redq = vp_ref[0], vq_ref[0]
        for i in range(1, N_DEV):
            red = red + vp_ref[i]
            redq = redq + vq_ref[i]
        red_ref[...] = red
        redq_ref[...] = redq
        for mp_ref, mat_ref in zip(mp_refs, mat_refs):
            mat = mp_ref[0].astype(F32)
            for i in range(1, mp_ref.shape[0]):
                mat = mat + mp_ref[i].astype(F32)
            mat_ref[...] = mat
        for e in range(N_DEV):
            dmod_ref[e:e + 1, 0:D] = vp_ref[e, R_SH_X:R_SH_X + 1, :]
            dmod_ref[e:e + 1, D:2 * D] = vp_ref[e, R_SC_X:R_SC_X + 1, :]
            dmod_ref[e:e + 1, 2 * D:3 * D] = vp_ref[e, R_GATE:R_GATE + 1, :]
        dmod_ref[8:9, 0:D] = red[R_SH_C:R_SH_C + 1, :]
        dmod_ref[8:9, D:2 * D] = red[R_SC_C:R_SC_C + 1, :]
        dmod_ref[8:9, 2 * D:3 * D] = jnp.zeros((1, D), F32)
        dmod_ref[9:16, :] = jnp.zeros((7, 3 * D), F32)
        gab_ref[:, 0:D] = red[R_SH_X:R_SH_X + 1, :] + red[R_SH_C:R_SH_C + 1, :]
        gab_ref[:, D:2 * D] = red[R_SC_X:R_SC_X + 1, :] + red[R_SC_C:R_SC_C + 1, :]
        gab_ref[:, 2 * D:3 * D] = red[R_GATE:R_GATE + 1, :]
        dmc_s[...] = jnp.broadcast_to(dmod_ref[8:9, :], (8, 3 * D))
        off = pl.multiple_of(me_ref[0] * nloc, 128)
        cpart_ref[...] = _dot_nt(dmc_s[:, pl.ds(off, nloc)], w_ref[...])

    return _call(
        body, name="reduce_small",
        out_shape=[jax.ShapeDtypeStruct((PACK_ROWS, D), F32), jax.ShapeDtypeStruct((PACK128_ROWS, HD), F32)]
        + [jax.ShapeDtypeStruct(p.shape[1:], F32) for p in mat_parts]
        + [jax.ShapeDtypeStruct((16, 3 * D), F32), jax.ShapeDtypeStruct((1, 3 * D), F32),
           jax.ShapeDtypeStruct((8, D), F32)],
        in_specs=[pl.BlockSpec(memory_space=pltpu.SMEM)] + [VMEM] * (nm + 3), out_specs=[VMEM] * (nm + 5),
        scratch_shapes=[pltpu.VMEM((8, 3 * D), F32)], compiler_params=_params(),
    )(me, vp_all, vq_all, *mat_parts, ada_w)


def _adamw_ada(c_all, c_ctx, dmod, w, m, v, me):
    nloc = w.shape[1]

    def body(me_ref, c_ref, cc_ref, dm_ref, w_ref, m_ref, v_ref, g_ref, d_ref, mo_ref, vo_ref):
        off = pl.multiple_of(me_ref[0] * nloc, 128)
        dm = dm_ref[:, pl.ds(off, nloc)]
        sx, _ = _silu_and_grad(c_ref[...])
        sc, _ = _silu_and_grad(cc_ref[...])
        g = _dot_tn(sx, dm[0:8, :]) + _dot_tn(jnp.broadcast_to(sc, (8, D)), dm[8:16, :])
        g_ref[...] = g
        d_ref[...], mo_ref[...], vo_ref[...] = _adamw(w_ref[...], g, m_ref[...], v_ref[...])

    sds = jax.ShapeDtypeStruct(w.shape, F32)
    return _call(
        body, name="adamw_ada_w", out_shape=[sds] * 4,
        in_specs=[pl.BlockSpec(memory_space=pltpu.SMEM)] + [VMEM] * 6, out_specs=[VMEM] * 4,
        compiler_params=_params(),
    )(me, c_all, c_ctx, dmod, w, m, v)


_SMALL = ("c_ctx", "ada_b", "norm_g", "conv_w", "conv_b", "lru_wa", "lru_ba", "lru_wx", "lru_bx", "lru_lambda",
          "sgu_ln_g", "sgu_ln_b", "sgu_w", "sgu_b", "final_g")


def _adamw_small(red, redq, mats, cparts, gab, ws, ms, vs, me):
    n = len(_SMALL)

    def body(me_ref, red_ref, redq_ref, wa_ref, wx_ref, sw_ref, cp_ref, gab_ref, *refs):
        w_refs, m_refs, v_refs = refs[:n], refs[n:2 * n], refs[2 * n:3 * n]
        outs = refs[3 * n:]
        off = pl.multiple_of(me_ref[0] * HD, 128)

        def row(r, k=1):
            return red_ref[r:r + k, :]

        cc = w_refs[0][...]
        dcc = cp_ref[0, 0:1, :]
        for i in range(1, N_DEV):
            dcc = dcc + cp_ref[i, 0:1, :]
        grads = dict(
            c_ctx=dcc * _silu_and_grad(cc)[1], ada_b=gab_ref[...], norm_g=row(R_NG_X) + row(R_NG_C),
            conv_w=red_ref[R_CW:R_CW + CONV_W, pl.ds(off, HD)], conv_b=row(R_CB),
            lru_wa=wa_ref[...], lru_ba=redq_ref[Q_BA:Q_BA + 2 * HEADS, :], lru_wx=wx_ref[...],
            lru_bx=redq_ref[Q_BX:Q_BX + 2 * HEADS, :], lru_lambda=red_ref[R_LAM:R_LAM + 2, pl.ds(off, HD)],
            sgu_ln_g=row(R_LN_G), sgu_ln_b=row(R_LN_B), sgu_w=sw_ref[...],
            sgu_b=redq_ref[Q_SGU_B:Q_SGU_B + HEADS, :], final_g=row(R_FINAL_G))
        for j, name in enumerate(_SMALL):
            g = grads[name]
            outs[j][...] = g
            outs[n + j][...], outs[2 * n + j][...], outs[3 * n + j][...] = _adamw(w_refs[j][...], g, m_refs[j][...],
                                                                                 v_refs[j][...])

    sds = [jax.ShapeDtypeStruct(ws[k].shape, F32) for k in _SMALL]
    outs = _call(
        body, name="adamw_small", out_shape=sds * 4,
        in_specs=[pl.BlockSpec(memory_space=pltpu.SMEM)] + [VMEM] * (7 + 3 * n), out_specs=[VMEM] * (4 * n),
        compiler_params=_params(),
    )(me, red, redq, *mats, cparts, gab, *[ws[k] for k in _SMALL], *[ms[k] for k in _SMALL],
      *[vs[k] for k in _SMALL])
    return [dict(zip(_SMALL, outs[i * n:(i + 1) * n])) for i in range(4)]


def kernel(x, c, ctx, c_ctx, ada_w, ada_b, norm_g, w_in, conv_w, conv_b, lru_wa, lru_ba, lru_wx, lru_bx, lru_lambda, sgu_ln_g, sgu_ln_b, sgu_w, sgu_b, w_out, final_g, loss_target, m_c_ctx, m_ada_w, m_ada_b, m_norm_g, m_w_in, m_conv_w, m_conv_b, m_lru_wa, m_lru_ba, m_lru_wx, m_lru_bx, m_lru_lambda, m_sgu_ln_g, m_sgu_ln_b, m_sgu_w, m_sgu_b, m_w_out, m_final_g, v_c_ctx, v_ada_w, v_ada_b, v_norm_g, v_w_in, v_conv_w, v_conv_b, v_lru_wa, v_lru_ba, v_lru_wx, v_lru_bx, v_lru_lambda, v_sgu_ln_g, v_sgu_ln_b, v_sgu_w, v_sgu_b, v_w_out, v_final_g):
    args = dict(locals())
    me_s = 4 * lax.axis_index("x") + 2 * lax.axis_index("y") + lax.axis_index("c")
    me = me_s.astype(jnp.int32).reshape(1)
    xr, ctxr, tgt = x[0], ctx[0], loss_target[0]
    cc = c_ctx.reshape(1, D)
    nw = 2 * HEADS * HD
    view = dict(c_ctx=(1, D), ada_b=(1, 3 * D), norm_g=(1, D), conv_w=(CONV_W, HD), conv_b=(1, D), lru_wa=(nw, HD),
                lru_ba=(2 * HEADS, HD), lru_wx=(nw, HD), lru_bx=(2 * HEADS, HD), lru_lambda=(2, HD), sgu_ln_g=(1, D),
                sgu_ln_b=(1, D), sgu_w=(HEADS * CHUNK, CHUNK), sgu_b=(HEADS, CHUNK), final_g=(1, D))

    zx, hn, w_full, w_out_b, modx, modc, c_all, cw_full, lam_full = _front_project(
        xr, c, cc, ada_w[0], ada_b, norm_g, w_in[0], w_out[0], conv_w[0], lru_lambda[0], me)
    zc, hnc = _project(ctxr, modc, norm_g, w_full, D, LC, "project_ctx")
    ba, bx = lru_ba.reshape(view["lru_ba"]), lru_bx.reshape(view["lru_bx"])
    yl, wout_all = _lru_forward(zx, zc, cw_full, conv_b, lru_wa[0], lru_wx[0], ba, bx, lam_full, [w_out_b], ["ag"])
    wout_full = wout_all.reshape(D_MIX, D)
    ws_b = sgu_w[0].astype(BF16)
    dz, dyl, dxn, ycat, dob, dws, dbst, mvec = _mixer_loss(
        xr, tgt, zx, yl, modx, final_g.reshape(1, D), sgu_ln_g, sgu_ln_b, ws_b, jnp.swapaxes(ws_b, 1, 2),
        sgu_b[0].T, wout_full, 256)

    wout_sums = _grad_w(ycat, dob, None, None, L, "grad_w_out", D, 0, 1, "rows", 1)
    rest_sums = _grad_w(hn, dz, None, None, L, "grad_w_in_rest", 2 * W_IN_SHARD, 1, 3, "cols", 2)
    dz, dxac, dwa, dwx, dba, dbx, dlam, dcw, dcb, win_parts, wout_parts, _, _ = _lru_backward(
        zx, zc, dyl, dz, cw_full, conv_b, lru_wa[0], lru_wx[0], ba, bx, lam_full, [rest_sums, wout_sums],
        first_chips=[1, 0])
    first_sums = _grad_w(hn, dz, hnc, dxac, L, "grad_w_in_first", 2 * W_IN_SHARD, 0, 1, "cols", 3)
    mats = [dwa.reshape(N_DEV, nw // N_DEV, HD), dwx.reshape(N_DEV, nw // N_DEV, HD), dws]
    mat_sums = _reduce2_local(mats, ["a2a"] * 3, me, "reduce_mat", 4, out_dtype=BF16)
    gx, xvec, win_parts, *mat_parts = _grad_rows(
        xr, dz, w_full, modx, norm_g, dxn, D_IN, 256, "grad_rows_x", chip_sums=[first_sums, *mat_sums],
        first_chips=[0, 0, 0, 0], dests=[win_parts, None, None, None])[:6]
    (cvec,) = _grad_rows(ctxr, dxac, w_full, modc, norm_g, None, D, LC, "grad_rows_ctx")
    pack = jnp.concatenate([mvec[0:5], xvec[0:3], cvec[0:3], dlam, dcw, dcb,
                            jnp.zeros((PACK_ROWS - R_CB - 1, D), F32)], axis=0)
    pack128 = jnp.concatenate([dba, dbx, dbst.T], axis=0)
    vp_all, vq_all = _gather2([pack, pack128], ["ag", "ag"], "gather_pack", 8)
    red, redq, *rest = _reduce_small(vp_all, vq_all, mat_parts, ada_w[0], me)
    mat_pieces, (dmod, gab, cpart) = rest[:3], rest[3:]
    *mats_all, cparts = _gather2([*mat_pieces, cpart], ["ag"] * 4, "gather_small", 9)

    g_w_in, d_w_in, nm_w_in, nv_w_in = _adamw_reduced(win_parts, w_in[0], m_w_in[0], v_w_in[0], 256, "adamw_w_in")
    g_w_out, d_w_out, nm_w_out, nv_w_out = _adamw_reduced(wout_parts, w_out[0], m_w_out[0], v_w_out[0], 128,
                                                          "adamw_w_out")
    g_ada, d_ada, nm_ada, nv_ada = _adamw_ada(c_all, cc, dmod, ada_w[0], m_ada_w[0], v_ada_w[0], me)
    ws = {k: args[k].reshape(view[k]) for k in _SMALL}
    ms = {k: args["m_" + k].reshape(view[k]) for k in _SMALL}
    vs = {k: args["v_" + k].reshape(view[k]) for k in _SMALL}
    small = _adamw_small(red, redq, [m.reshape(-1, HD) for m in mats_all], cparts, gab, ws, ms, vs, me)
    big = dict(w_in=(g_w_in, d_w_in, nm_w_in, nv_w_in), w_out=(g_w_out, d_w_out, nm_w_out, nv_w_out),
               ada_w=(g_ada, d_ada, nm_ada, nv_ada))

    loss = red[R_LOSS, 0]
    names = ("c_ctx", "ada_w", "ada_b", "norm_g", "w_in", "conv_w", "conv_b", "lru_wa", "lru_ba", "lru_wx", "lru_bx",
             "lru_lambda", "sgu_ln_g", "sgu_ln_b", "sgu_w", "sgu_b", "w_out", "final_g")
    outs = [loss, gx.reshape(x.shape)]
    for kind in range(4):
        for k in names:
            val = big[k][kind] if k in big else small[kind][k]
            outs.append(val.reshape(args[k].shape))
    return tuple(outs)
```

```python
import functools

import jax
import jax.numpy as jnp
from jax import lax
from jax.experimental import pallas as pl
from jax.experimental.pallas import tpu as pltpu

F32 = jnp.float32
BF16 = jnp.bfloat16

N_DEV = 8
D = 1024
L = 2048
LC = 256
HEADS = 8
HD = 128
CHUNK = 128
D_IN = 5 * D
W_IN_SHARD = D_IN // N_DEV
ROWS = 256
D_MIX = 2 * D
CONV_W = 4
LRU_C = 8.0
NORM_EPS = 1e-6
LN_EPS = 1e-5
ADAM_LR, ADAM_B1, ADAM_B2, ADAM_EPS, ADAM_WD, ADAM_STEP = 0.001, 0.9, 0.999, 1e-08, 0.01, 10

VMEM_LIMIT = 56 * 1024 * 1024

HBM = pl.BlockSpec(memory_space=pltpu.HBM)
VMEM = pl.BlockSpec(memory_space=pltpu.VMEM)
MESH = pl.DeviceIdType.MESH


def _call(body, **kw):
    return pl.pallas_call(body, **kw)


def _params(*sem):
    return pltpu.CompilerParams(dimension_semantics=sem, vmem_limit_bytes=VMEM_LIMIT)


def _sigmoid(x):
    return 0.5 * jnp.tanh(0.5 * x) + 0.5


def _silu_and_grad(x):
    s = _sigmoid(x)
    return x * s, s * (1.0 + x * (1.0 - s))


_G0 = 0.7978845608028654
_G1 = 0.044715


def _gelu_and_grad(x):
    x2 = x * x
    t = jnp.tanh(_G0 * (x + _G1 * x * x2))
    cdf = 0.5 * (1.0 + t)
    return x * cdf, cdf + 0.5 * x * (1.0 - t * t) * (_G0 * (1.0 + 3.0 * _G1 * x2))


def _gelu(x):
    return 0.5 * x * (1.0 + jnp.tanh(_G0 * (x + _G1 * x * x * x)))


def _softplus(z):
    t = jnp.exp(-jnp.abs(z))
    u = 1.0 + t
    log1p = jnp.where(u == 1.0, t, jnp.log(u) * t / jnp.where(u == 1.0, 1.0, u - 1.0))
    return jnp.maximum(z, 0.0) + log1p


def _dot(a, b):
    return jnp.dot(a, b, preferred_element_type=F32)


def _dot_nt(a, b):
    return lax.dot_general(a, b, (((1,), (1,)), ((), ())), preferred_element_type=F32)


def _dot_tn(a, b):
    return lax.dot_general(a, b, (((0,), (0,)), ((), ())), preferred_element_type=F32)


def _rows(shape):
    return lax.broadcasted_iota(jnp.int32, shape, 0)


def _shift_down(x, first):
    y = pltpu.roll(x, 1, 0)
    head = jnp.where(_rows((8, x.shape[1])) == 0, first, y[0:8])
    return jnp.concatenate([head, y[8:]], axis=0)


def _shift_up(x, last):
    n = x.shape[0]
    y = pltpu.roll(x, n - 1, 0)
    tail = jnp.where(_rows((8, x.shape[1])) == 7, last, y[n - 8:])
    return jnp.concatenate([y[:n - 8], tail], axis=0)


def _gather2(arrays, modes, name, barrier_id):
    n = len(arrays)

    def body(*refs):
        start, forward, finish = _gather2_ops(refs[:n], refs[n:2 * n], modes, *refs[2 * n:], barrier=True)
        start()
        forward()
        finish()

    return _call(
        body, name=name, out_shape=_gather2_shapes(arrays, modes), in_specs=[HBM] * n, out_specs=[HBM] * n,
        scratch_shapes=_gather2_sems(n),
        compiler_params=pltpu.CompilerParams(has_side_effects=True, collective_id=barrier_id),
    )(*[pltpu.with_memory_space_constraint(a, pltpu.HBM) for a in arrays])


def _gather2_shapes(arrays, modes):
    return [jax.ShapeDtypeStruct((N_DEV,) + a.shape if m == "ag" else (a.shape[0], N_DEV * a.shape[1]), a.dtype)
            for a, m in zip(arrays, modes)]


def _gather2_sems(n):
    return [pltpu.SemaphoreType.DMA((n, N_DEV - 1)), pltpu.SemaphoreType.DMA((n, N_DEV - 1)),
            pltpu.SemaphoreType.DMA((n,))]


def _barrier(peers):
    sem = pltpu.get_barrier_semaphore()
    for peer in peers:
        pl.semaphore_signal(sem, inc=1, device_id=peer, device_id_type=MESH)
    pl.semaphore_wait(sem, len(peers))


def _gather2_ops(ins, outs, modes, send_sems, recv_sems, local_sems, barrier=False):
    n = len(ins)
    x, y, c = lax.axis_index("x"), lax.axis_index("y"), lax.axis_index("c")
    me, sibling = (x, y, c), (x, y, 1 - c)
    chips = [(x ^ (k >> 1), y ^ (k & 1)) for k in (1, 2, 3)]

    def slot(j, px, py, pc):
        dev = 4 * px + 2 * py + pc
        if modes[j] == "agc":
            w = ins[j].shape[1]
            return outs[j].at[:, pl.ds(pl.multiple_of(dev * w, 128), w)]
        return outs[j].at[dev]

    def copy(j, k, block, to, src=None):
        return pltpu.make_async_remote_copy(
            src_ref=slot(j, *block) if src is None else src, dst_ref=slot(j, *block),
            send_sem=send_sems.at[j, k], recv_sem=recv_sems.at[j, k], device_id=to, device_id_type=MESH)

    def own(j):
        return pltpu.make_async_copy(ins[j], slot(j, *me), local_sems.at[j])

    def first(j):
        return [copy(j, 0, me, sibling, src=ins[j])] + [copy(j, 1 + i, me, (*chip, c), src=ins[j])
                                                        for i, chip in enumerate(chips)]

    def passed(j, i):
        return copy(j, 4 + i, (*chips[i], c), sibling)

    def start():
        if barrier:
            _barrier([sibling] + [(*chip, c) for chip in chips])
        for j in range(n):
            own(j).start()
            for cp in first(j):
                cp.start()

    def forward():
        for i, chip in enumerate(chips):
            for j in range(n):
                copy(j, 1 + i, (*chip, c), me).wait_recv()
                passed(j, i).start()

    def finish():
        for j in range(n):
            copy(j, 0, sibling, me).wait_recv()
            for i, chip in enumerate(chips):
                copy(j, 4 + i, (*chip, 1 - c), me).wait_recv()
            for cp in first(j) + [passed(j, i) for i in range(3)]:
                cp.wait_send()
            own(j).wait()

    return start, forward, finish


def _sibling_barrier():
    sem = pltpu.get_barrier_semaphore()
    sibling = (lax.axis_index("x"), lax.axis_index("y"), 1 - lax.axis_index("c"))
    pl.semaphore_signal(sem, inc=1, device_id=sibling, device_id_type=MESH)
    pl.semaphore_wait(sem, 1)


def _reduce2_local(arrays, modes, me, name, barrier_id, counts=None, out_dtype=None):
    n = len(arrays)
    counts = counts or [4] * n
    shapes = [(a.shape[1], a.shape[2]) if m == "a2a" else (a.shape[0], a.shape[1] // (2 * cnt))
              for a, m, cnt in zip(arrays, modes, counts)]
    staged = [jax.ShapeDtypeStruct((cnt,) + s, a.dtype) for s, a, cnt in zip(shapes, arrays, counts)]

    def piece(ref, mode, dev, w):
        return ref.at[dev] if mode == "a2a" else ref.at[:, pl.ds(pl.multiple_of(dev * w, 128), w)]

    def to_sibling(*refs):
        ins, outs = refs[:n], refs[n:2 * n]
        send_sems, recv_sems = refs[2 * n:]
        x, y, c = lax.axis_index("x"), lax.axis_index("y"), lax.axis_index("c")
        _sibling_barrier()
        copies = []
        for j in range(n):
            for q in range(counts[j]):
                cp = pltpu.make_async_remote_copy(
                    src_ref=piece(ins[j], modes[j], 2 * q + (1 - c), shapes[j][1]), dst_ref=outs[j].at[q],
                    send_sem=send_sems.at[j, q], recv_sem=recv_sems.at[j, q], device_id=(x, y, 1 - c),
                    device_id_type=MESH)
                cp.start()
                copies.append(cp)
        for cp in copies:
            cp.wait()

    stage = _call(
        to_sibling, name=name + "_d2d", out_shape=staged, in_specs=[HBM] * n, out_specs=[HBM] * n,
        scratch_shapes=[pltpu.SemaphoreType.DMA((n, 4)), pltpu.SemaphoreType.DMA((n, 4))],
        compiler_params=pltpu.CompilerParams(has_side_effects=True, collective_id=barrier_id),
    )(*[pltpu.with_memory_space_constraint(a, pltpu.HBM) for a in arrays])

    def add(me_ref, *refs):
        del me_ref
        own, got, outs = refs[:n], refs[n:2 * n], refs[2 * n:]
        for j in range(n):
            mine = own[j][0] if modes[j] == "a2a" else own[j][...]
            outs[j][0] = (mine.astype(F32) + got[j][0].astype(F32)).astype(outs[j].dtype)

    in_specs, slot_specs = [], []
    for (r, w), m, cnt in zip(shapes, modes, counts):
        if m == "a2a":
            in_specs.append(pl.BlockSpec(
                (1, r, w), lambda q, me_ref, cnt=cnt: (2 * jnp.minimum(q, cnt - 1) + me_ref[0] % 2, 0, 0)))
        else:
            in_specs.append(pl.BlockSpec(
                (r, w), lambda q, me_ref, cnt=cnt: (0, 2 * jnp.minimum(q, cnt - 1) + me_ref[0] % 2)))
        slot_specs.append(pl.BlockSpec((1, r, w), lambda q, me_ref, cnt=cnt: (jnp.minimum(q, cnt - 1), 0, 0)))
    return _call(
        add, name=name + "_add",
        out_shape=[jax.ShapeDtypeStruct(s.shape, out_dtype or s.dtype) for s in staged],
        grid_spec=pltpu.PrefetchScalarGridSpec(num_scalar_prefetch=1, grid=(max(counts),),
                                               in_specs=in_specs + slot_specs, out_specs=slot_specs),
        compiler_params=_params("arbitrary"),
    )(me, *arrays, *stage)


def _chips_sems(n):
    return [pltpu.SemaphoreType.DMA((n, 6)), pltpu.SemaphoreType.DMA((n, 6)), pltpu.SemaphoreType.DMA((n,))]


def _chips_stage_shapes(chip_sums):
    return [jax.ShapeDtypeStruct((2, a.shape[1] // 2, a.shape[2]), a.dtype) for a in chip_sums]


def _chips_ops(ins, outs, stages, send_sems, recv_sems, local_sems, first_chips=None, barrier=False):
    x, y, c = lax.axis_index("x"), lax.axis_index("y"), lax.axis_index("c")
    qm = 2 * x + y
    first_chips = first_chips or [0] * len(ins)

    def owns(j, chip):
        lo, cnt = first_chips[j], ins[j].shape[0]
        if lo == 0 and cnt == 4:
            return None
        return jnp.logical_and(chip >= lo, chip < lo + cnt)

    def guarded(cond, fn):
        if cond is None:
            fn()
        else:
            pl.when(cond)(fn)

    def slot(j, chip):
        return jnp.clip(chip - first_chips[j], 0, ins[j].shape[0] - 1)

    def half(j, i):
        h = ins[j].shape[1] // 2
        return pl.ds(i * h, h)

    def copy(j, sem, src, dst, k):
        return pltpu.make_async_remote_copy(
            src_ref=src, dst_ref=dst, send_sem=send_sems.at[j, sem], recv_sem=recv_sems.at[j, sem],
            device_id=(x ^ (k >> 1), y ^ (k & 1), c), device_id_type=MESH)

    def direct(j, k):
        return copy(j, k - 1, ins[j].at[slot(j, qm ^ k)], outs[j].at[qm], k)

    def first_hop(j, k):
        return copy(j, 1 + k, ins[j].at[slot(j, qm ^ 3), half(j, k - 1)], stages[j].at[k - 1], k)

    def second_hop(j, k):
        return copy(j, 3 + k, stages[j].at[2 - k], outs[j].at[qm ^ (3 - k), half(j, 2 - k)], k)

    def local(j):
        return pltpu.make_async_copy(ins[j].at[slot(j, qm)], outs[j].at[qm], local_sems.at[j])

    def start():
        if barrier:
            _barrier([(x ^ (k >> 1), y ^ (k & 1), c) for k in (1, 2)])
        for j in range(len(ins)):
            for k in (1, 2):
                guarded(owns(j, qm ^ 3), lambda j=j, k=k: first_hop(j, k).start())
        for j in range(len(ins)):
            for k in (1, 2):
                guarded(owns(j, qm ^ k), lambda j=j, k=k: direct(j, k).start())
            guarded(owns(j, qm), lambda j=j: local(j).start())

    def forward():
        for j in range(len(ins)):
            for k in (1, 2):
                def pass_on(j=j, k=k):
                    first_hop(j, 3 - k).wait_recv()
                    second_hop(j, k).start()
                guarded(owns(j, qm ^ k), pass_on)

    def finish():
        for j in range(len(ins)):
            for k in (1, 2):
                guarded(owns(j, qm ^ k), lambda j=j, k=k: direct(j, k).wait_send())
                guarded(owns(j, qm ^ k), lambda j=j, k=k: second_hop(j, k).wait_send())
                guarded(owns(j, qm ^ 3), lambda j=j, k=k: first_hop(j, k).wait_send())
                guarded(owns(j, qm), lambda j=j, k=k: direct(j, k).wait_recv())
                guarded(owns(j, qm), lambda j=j, k=k: second_hop(j, k).wait_recv())
            guarded(owns(j, qm), lambda j=j: local(j).wait())

    return start, forward, finish


def _front(c, c_ctx, ada_w, ada_b, w_in, w_out, me):
    nloc = ada_w.shape[1]

    def body(me_ref, c_ref, cc_ref, aw_ref, ab_ref, win_ref, wout_ref,
             wfull_ref, woutb_ref, modx_ref, modc_ref, call_ref,
             wb_s, part_s, parts_s, w_send, w_recv, w_local, s_send, s_recv):
        x, y, cidx = lax.axis_index("x"), lax.axis_index("y"), lax.axis_index("c")
        me = me_ref[0]
        wb_s[...] = win_ref[...].astype(BF16)
        woutb_ref[...] = wout_ref[...].astype(BF16)
        start, forward, finish = _gather2_ops([wb_s], [wfull_ref], ["agc"], w_send, w_recv, w_local)
        start()

        def small_gather(src, my_slot, stage):
            copies = []
            for k in range(1, N_DEV):
                peer = (x ^ (k >> 2), y ^ ((k >> 1) & 1), cidx ^ (k & 1))
                cp = pltpu.make_async_remote_copy(src_ref=src, dst_ref=my_slot, send_sem=s_send.at[stage, k - 1],
                                                  recv_sem=s_recv.at[stage, k - 1], device_id=peer,
                                                  device_id_type=MESH)
                cp.start()
                copies.append(cp)
            pltpu.sync_copy(src, my_slot)
            for cp in copies:
                cp.wait()

        small_gather(c_ref, call_ref.at[pl.ds(me, 1), :], 0)
        off = pl.multiple_of(me * nloc, 128)
        b = ab_ref[:, pl.ds(off, nloc)]
        w = aw_ref[...]
        sx, _ = _silu_and_grad(call_ref[...])
        sc, _ = _silu_and_grad(jnp.broadcast_to(cc_ref[...], (8, D)))
        part_s[0:8, :] = _dot(sx, w) + b
        part_s[8:16, :] = _dot(sc, w) + b
        small_gather(part_s, parts_s.at[me], 1)
        mine = _rows((16, nloc)) == me
        for j in range(N_DEV):
            pj = parts_s[j]
            modx_ref[:, j * nloc:(j + 1) * nloc] = jnp.sum(jnp.where(mine, pj, 0.0), axis=0, keepdims=True)
            modc_ref[:, j * nloc:(j + 1) * nloc] = pj[8:9, :]
        forward()
        finish()

    return _call(
        body, name="front",
        out_shape=[jax.ShapeDtypeStruct((D, D_IN), BF16), jax.ShapeDtypeStruct(w_out.shape, BF16),
                   jax.ShapeDtypeStruct((1, 3 * D), F32), jax.ShapeDtypeStruct((1, 3 * D), F32),
                   jax.ShapeDtypeStruct((N_DEV, D), F32)],
        in_specs=[pl.BlockSpec(memory_space=pltpu.SMEM)] + [VMEM] * 6, out_specs=[HBM, VMEM, VMEM, VMEM, VMEM],
        scratch_shapes=[pltpu.VMEM(w_in.shape, BF16), pltpu.VMEM((16, nloc), F32),
                        pltpu.VMEM((N_DEV, 16, nloc), F32)] + _gather2_sems(1) +
                       [pltpu.SemaphoreType.DMA((2, N_DEV - 1)), pltpu.SemaphoreType.DMA((2, N_DEV - 1))],
        compiler_params=pltpu.CompilerParams(vmem_limit_bytes=VMEM_LIMIT, has_side_effects=True),
    )(me, c, c_ctx, ada_w, ada_b, w_in, w_out)


ARRIVAL = (0, 1, 2, 4, 3, 5, 6, 7)


def _front_project(xr, c, c_ctx, ada_w, ada_b, ng, w_in, w_out, cw, lam, me):
    nloc = ada_w.shape[1]
    ws = W_IN_SHARD
    order = me[0] ^ jnp.asarray(ARRIVAL, jnp.int32)

    def body(ord_ref, x_ref, c_ref, cc_ref, aw_ref, ab_ref, ng_ref, win_ref, wout_ref, cw_ref, lam_ref,
             z_ref, hn_ref, wfull_ref, woutb_ref, modx_ref, modc_ref, call_ref, cwf_ref, lamf_ref,
             wv, call_s, part_s, parts_s, w_send, w_recv, hbm_sems, s_send, s_recv, g_send, g_recv, g_local):
        t = pl.program_id(0)
        x, y, cidx = lax.axis_index("x"), lax.axis_index("y"), lax.axis_index("c")
        me_i = ord_ref[0]
        sibling = (x, y, 1 - cidx)
        chips = [(x ^ (k >> 1), y ^ (k & 1)) for k in (1, 2, 3)]
        g_start, g_pass, g_finish = _gather2_ops([cw_ref, lam_ref], [cwf_ref, lamf_ref], ["agc", "agc"],
                                                 g_send, g_recv, g_local)

        def shard_copy(k, px, py, pc, to, half=None):
            slot = wv.at[4 * px + 2 * py + pc]
            if half is not None:
                slot = slot.at[pl.ds(half * (D // 2), D // 2), :]
            return pltpu.make_async_remote_copy(src_ref=slot, dst_ref=slot, send_sem=w_send.at[k],
                                                recv_sem=w_recv.at[k], device_id=to, device_id_type=MESH)

        def small_gather(src, my_slot, stage):
            copies = []
            for k in range(1, N_DEV):
                peer = (x ^ (k >> 2), y ^ ((k >> 1) & 1), cidx ^ (k & 1))
                cp = pltpu.make_async_remote_copy(src_ref=src, dst_ref=my_slot, send_sem=s_send.at[stage, k - 1],
                                                  recv_sem=s_recv.at[stage, k - 1], device_id=peer,
                                                  device_id_type=MESH)
                cp.start()
                copies.append(cp)
            pltpu.sync_copy(src, my_slot)
            return copies

        def finish_small(copies):
            for cp in copies:
                cp.wait()

        def to_neighbours(half):
            for i in (0, 1):
                shard_copy(1 + i, x, y, cidx, (*chips[i], cidx), half=half).start()

        @pl.when(t == 0)
        def _():
            _barrier([(x ^ (k >> 2), y ^ ((k >> 1) & 1), cidx ^ (k & 1)) for k in range(1, N_DEV)])
            g_start()
            wv[me_i] = win_ref[...].astype(BF16)
            woutb_ref[...] = wout_ref[...].astype(BF16)
            shard_copy(0, x, y, cidx, sibling).start()
            finish_small(small_gather(c_ref, call_s.at[pl.ds(me_i, 1), :], 0))
            to_neighbours(0)
            call_ref[...] = call_s[...]
            off = pl.multiple_of(me_i * nloc, 128)
            b = ab_ref[:, pl.ds(off, nloc)]
            w = aw_ref[...]
            sx, _ = _silu_and_grad(call_s[...])
            sc, _ = _silu_and_grad(jnp.broadcast_to(cc_ref[...], (8, D)))
            part_s[0:8, :] = _dot(sx, w) + b
            part_s[8:16, :] = _dot(sc, w) + b
            parts_sent = small_gather(part_s, parts_s.at[me_i], 1)
            to_neighbours(1)
            finish_small(parts_sent)
            mine = _rows((16, nloc)) == me_i
            for j in range(N_DEV):
                pj = parts_s[j]
                modx_ref[:, j * nloc:(j + 1) * nloc] = jnp.sum(jnp.where(mine, pj, 0.0), axis=0, keepdims=True)
                modc_ref[:, j * nloc:(j + 1) * nloc] = pj[8:9, :]
            shift, scale1, ngv = modx_ref[:, 0:D], 1.0 + modx_ref[:, D:2 * D], ng_ref[...]
            for r in range(L // ROWS):
                rsl = slice(r * ROWS, (r + 1) * ROWS)
                xv = x_ref[rsl, :]
                rs = lax.rsqrt(jnp.mean(xv * xv, axis=-1, keepdims=True) + NORM_EPS)
                hn_ref[rsl, :] = ((xv * rs * ngv) * scale1 + shift).astype(BF16)

        @pl.when(t == 1)
        def _():
            shard_copy(0, x, y, 1 - cidx, sibling).wait_recv()
            g_pass()

        for i in (0, 1):
            @pl.when(t == ARRIVAL.index((2, 4)[i]))
            def _(i=i):
                shard_copy(1 + i, *chips[i], cidx, sibling).wait_recv()
                shard_copy(4 + i, *chips[i], cidx, sibling).start()
                shard_copy((7, 3)[i], *chips[i], cidx, (*chips[1 - i], cidx), half=i).start()

        @pl.when(t == ARRIVAL.index(6))
        def _():
            shard_copy(3, *chips[2], cidx, sibling, half=1).wait_recv()
            shard_copy(7, *chips[2], cidx, sibling, half=0).wait_recv()
            shard_copy(6, *chips[2], cidx, sibling).start()

        for i in range(3):
            @pl.when(t == ARRIVAL.index((3, 5, 7)[i]))
            def _(i=i):
                shard_copy(4 + i, *chips[i], 1 - cidx, sibling).wait_recv()

        @pl.when(t == 2)
        def _():
            g_finish()

        dev = ord_ref[t]
        for r in range(L // (2 * ROWS)):
            rsl = slice(r * 2 * ROWS, (r + 1) * 2 * ROWS)
            z_ref[rsl, :] = _dot(hn_ref[rsl, :], wv[dev])
        col = pl.ds(pl.multiple_of(dev * ws, 128), ws)
        pltpu.make_async_copy(wv.at[dev], wfull_ref.at[:, col], hbm_sems.at[t]).start()

        @pl.when(t == N_DEV - 1)
        def _():
            for k in (0, 1, 2, 4, 5, 6):
                shard_copy(k, x, y, cidx, sibling).wait_send()
            for k in (3, 7):
                shard_copy(k, x, y, cidx, sibling, half=0).wait_send()
            for s in range(N_DEV):
                pltpu.make_async_copy(wv.at[0], wfull_ref.at[:, pl.ds(0, ws)], hbm_sems.at[s]).wait()

    const = lambda *shape: pl.BlockSpec(shape, lambda t, o: (0,) * len(shape))
    once = lambda *shape: pl.BlockSpec(shape, lambda t, o: (0,) * len(shape), pipeline_mode=pl.Buffered(1))
    return _call(
        body, name="front_project",
        out_shape=[jax.ShapeDtypeStruct((L, D_IN), F32), jax.ShapeDtypeStruct((L, D), BF16),
                   jax.ShapeDtypeStruct((D, D_IN), BF16), jax.ShapeDtypeStruct(w_out.shape, BF16),
                   jax.ShapeDtypeStruct((1, 3 * D), F32), jax.ShapeDtypeStruct((1, 3 * D), F32),
                   jax.ShapeDtypeStruct((N_DEV, D), F32), jax.ShapeDtypeStruct((CONV_W, D), F32),
                   jax.ShapeDtypeStruct((2, D), F32)],
        grid_spec=pltpu.PrefetchScalarGridSpec(
            num_scalar_prefetch=1, grid=(N_DEV,),
            in_specs=[once(L, D), const(1, D), const(1, D), once(D, nloc), const(1, 3 * D), const(1, D),
                      once(D, ws), once(*w_out.shape), HBM, HBM],
            out_specs=[pl.BlockSpec((L, ws), lambda t, o: (0, o[t])), const(L, D), HBM, const(*w_out.shape),
                       const(1, 3 * D), const(1, 3 * D), const(N_DEV, D), HBM, HBM],
            scratch_shapes=[pltpu.VMEM((N_DEV, D, ws), BF16), pltpu.VMEM((N_DEV, D), F32), pltpu.VMEM((16, nloc), F32),
                            pltpu.VMEM((N_DEV, 16, nloc), F32), pltpu.SemaphoreType.DMA((8,)),
                            pltpu.SemaphoreType.DMA((8,)), pltpu.SemaphoreType.DMA((N_DEV,)),
                            pltpu.SemaphoreType.DMA((2, N_DEV - 1)), pltpu.SemaphoreType.DMA((2, N_DEV - 1))]
            + _gather2_sems(2)),
        compiler_params=pltpu.CompilerParams(dimension_semantics=("arbitrary",), vmem_limit_bytes=VMEM_LIMIT,
                                             has_side_effects=True, collective_id=10),
    )(order, xr, c, c_ctx, ada_w, ada_b, ng, w_in, w_out, pltpu.with_memory_space_constraint(cw, pltpu.HBM),
      pltpu.with_memory_space_constraint(lam, pltpu.HBM))


def _project(xr, mod, ng, w, ncols, tm, name, gather=None, gather_modes=()):
    rows = xr.shape[0]
    steps = rows // tm
    ng_ = len(gather or ())

    def body(x_ref, sh_ref, sc_ref, ng_ref, w_ref, *rest):
        z_ref, hn_ref = rest[ng_:ng_ + 2]
        if ng_:
            start, forward, finish = _gather2_ops(rest[:ng_], rest[ng_ + 2:2 * ng_ + 2], gather_modes,
                                                  *rest[2 * ng_ + 2:])
            pl.when(pl.program_id(0) == 0)(start)
            pl.when(pl.program_id(0) == steps // 2)(forward)
        x = x_ref[...]
        rs = lax.rsqrt(jnp.mean(x * x, axis=-1, keepdims=True) + NORM_EPS)
        hn = (x * rs * ng_ref[...]) * (1.0 + sc_ref[...]) + sh_ref[...]
        hb = hn.astype(BF16)
        hn_ref[...] = hb
        for n in range(ncols // D):
            z_ref[:, n * D:(n + 1) * D] = _dot(hb, w_ref[:, n * D:(n + 1) * D])
        if ng_:
            pl.when(pl.program_id(0) == steps - 1)(finish)

    vec = pl.BlockSpec((1, D), lambda i: (0, 0))
    gathered = _gather2_shapes(gather, gather_modes) if ng_ else []
    return _call(
        body, name=name, grid=(steps,),
        out_shape=[jax.ShapeDtypeStruct((rows, ncols), F32), jax.ShapeDtypeStruct((rows, D), BF16)] + gathered,
        in_specs=[pl.BlockSpec((tm, D), lambda i: (i, 0)), vec, pl.BlockSpec((1, D), lambda i: (0, 1)), vec,
                  pl.BlockSpec((D, ncols), lambda i: (0, 0), pipeline_mode=pl.Buffered(1))] + [HBM] * ng_,
        out_specs=[pl.BlockSpec((tm, ncols), lambda i: (i, 0)), pl.BlockSpec((tm, D), lambda i: (i, 0))] + [HBM] * ng_,
        scratch_shapes=_gather2_sems(ng_) if ng_ else [],
        compiler_params=pltpu.CompilerParams(dimension_semantics=("arbitrary",), vmem_limit_bytes=VMEM_LIMIT,
                                             has_side_effects=bool(ng_)),
    )(xr, mod, mod, ng, w, *[pltpu.with_memory_space_constraint(a, pltpu.HBM) for a in gather or ()])


def _scan_pair(af_ref, uf_ref, hf_ref, h0f, ab_ref, ub_ref, hb_ref, h0b, t_len):
    span = 8 * SCAN_BLOCKS
    nit = t_len // span
    rows = _rows((8, HD))

    def local_scan(a, b, forward):
        for s in (1, 2, 4):
            sh = s if forward else 8 - s
            m = rows >= s if forward else rows < 8 - s
            b = a * jnp.where(m, pltpu.roll(b, sh, 0), 0.0) + b
            a = a * jnp.where(m, pltpu.roll(a, sh, 0), 1.0)
        return a, b

    def span_scan(a_ref, u_ref, h_ref, off, carry, forward):
        order = range(SCAN_BLOCKS) if forward else range(SCAN_BLOCKS - 1, -1, -1)
        last = slice(7, 8) if forward else slice(0, 1)
        for q in order:
            rs = pl.ds(off + 8 * q, 8)
            a, b = local_scan(a_ref[rs, :], u_ref[rs, :], forward)
            h_ref[rs, :] = b + a * carry
            carry = a[last, :] * carry + b[last, :]
        return carry

    def body(k, carry):
        cf, cb = carry
        cf = span_scan(af_ref, uf_ref, hf_ref, pl.multiple_of(k * span, span), cf, True)
        cb = span_scan(ab_ref, ub_ref, hb_ref, pl.multiple_of((nit - 1 - k) * span, span), cb, False)
        return cf, cb

    return lax.fori_loop(0, nit, body, (h0f, h0b))


SCAN_BLOCKS = 8


def _shifted(pad_ref, x, offsets, before=0.0, after=0.0):
    n = x.shape[0]
    pad_ref[0:8, :] = jnp.broadcast_to(jnp.asarray(before, F32), (8, x.shape[1]))
    pad_ref[8:8 + n, :] = x
    pad_ref[8 + n:16 + n, :] = jnp.broadcast_to(jnp.asarray(after, F32), (8, x.shape[1]))
    return [pad_ref[8 + o:8 + o + n, :] for o in offsets]


def _conv(xa, cw, cb, pad_ref):
    xm1, xp1, xp2 = _shifted(pad_ref, xa, (-1, 1, 2))
    return xm1 * cw[0:1, :] + xa * cw[1:2, :] + xp1 * cw[2:3, :] + xp2 * cw[3:4, :] + cb


def _gates(xc, wa, wx, ba, bx, nsp):
    xb = xc.astype(BF16)
    r = _sigmoid(_dot(xb, wa) + ba)
    i = _sigmoid(_dot(xb, wx) + bx)
    log_a = r * nsp
    a = jnp.exp(log_a)
    g2 = jnp.tanh(log_a) * (-1.0 - a * a)
    rg = lax.rsqrt(jnp.maximum(g2, 1e-30))
    return r, i, a, g2 * rg, rg


def _lru_param_specs():
    h4 = pl.BlockSpec((2, 1, HD, HD), lambda h: (0, h, 0, 0))
    v2 = pl.BlockSpec((2, HD), lambda h: (0, h))
    b16 = pl.BlockSpec((2 * HEADS, HD), lambda h: (0, 0))
    return dict(
        xa=pl.BlockSpec((L, HD), lambda h: (0, h)), xac=pl.BlockSpec((LC, HD), lambda h: (0, h)),
        cw=pl.BlockSpec((CONV_W, HD), lambda h: (0, h)), cb=pl.BlockSpec((1, HD), lambda h: (0, h)), h4=h4, v2=v2,
        b16=b16)


def _bias_row(ref, d):
    mask = _rows((2 * HEADS, HD)) == d * HEADS + pl.program_id(0)
    return jnp.sum(jnp.where(mask, ref[...], 0.0), axis=0, keepdims=True), mask


def _lru_forward(zx, zc, cw, cb, wa, wx, ba, bx, lam, gather, gather_modes):
    ng_ = len(gather)

    def body(xa_ref, xac_ref, cw_ref, cb_ref, wa_ref, wx_ref, ba_ref, bx_ref, lam_ref, *rest):
        yl_ref = rest[ng_]
        af, uf, hf, ab, ub, hb, pad_s = rest[2 * ng_ + 1:2 * ng_ + 8]
        start, pass_on, finish = _gather2_ops(rest[:ng_], rest[ng_ + 1:2 * ng_ + 1], gather_modes,
                                              *rest[2 * ng_ + 8:], barrier=True)
        pl.when(pl.program_id(0) == 0)(start)
        pl.when(pl.program_id(0) == HEADS // 2)(pass_on)
        pl.when(pl.program_id(0) == HEADS - 1)(finish)
        cwv, cbv = cw_ref[...], cb_ref[...]
        nsp = (-LRU_C) * _softplus(-lam_ref[...])

        def forward(xa, t_len, h0f, h0b):
            xc = _conv(xa, cwv, cbv, pad_s)
            for d, (a_ref, u_ref) in enumerate(((af, uf), (ab, ub))):
                _, i, a, gamma, _ = _gates(xc, wa_ref[d, 0].astype(BF16), wx_ref[d, 0].astype(BF16),
                                           _bias_row(ba_ref, d)[0], _bias_row(bx_ref, d)[0], nsp[d:d + 1, :])
                a_ref[0:t_len, :] = a
                u_ref[0:t_len, :] = gamma * (i * xc)
            return _scan_pair(af, uf, hf, h0f, ab, ub, hb, h0b, t_len)

        z = jnp.zeros((1, HD), F32)
        h0f, h0b = forward(xac_ref[...], LC, z, z)
        forward(xa_ref[...], L, h0f, h0b)
        yl_ref[...] = hf[...] + hb[...]

    s = _lru_param_specs()
    return _call(
        body, name="lru_forward", grid=(HEADS,),
        out_shape=[jax.ShapeDtypeStruct((L, D), F32)] + _gather2_shapes(gather, gather_modes),
        in_specs=[s["xa"], s["xac"], s["cw"], s["cb"], s["h4"], s["h4"], s["b16"], s["b16"], s["v2"]] + [HBM] * ng_,
        out_specs=[pl.BlockSpec((L, HD), lambda h: (0, h))] + [HBM] * ng_,
        scratch_shapes=[pltpu.VMEM((L, HD), F32)] * 6 + [pltpu.VMEM((L + 16, HD), F32)] + _gather2_sems(ng_),
        compiler_params=pltpu.CompilerParams(dimension_semantics=("arbitrary",), vmem_limit_bytes=VMEM_LIMIT,
                                             has_side_effects=True, collective_id=5),
    )(zx, zc, cw, cb, wa, wx, ba, bx, lam, *[pltpu.with_memory_space_constraint(a, pltpu.HBM) for a in gather])


def _lru_backward(zx, zc, dyl, dz, cw, cb, wa, wx, ba, bx, lam, chip_sums, first_chips=None):
    nr = len(chip_sums)

    def body(xa_ref, xac_ref, dyl_ref, dz_in, cw_ref, cb_ref, wa_ref, wx_ref, ba_ref, bx_ref, lam_ref, *rest):
        (dxa_ref, dxac_ref, dwa_ref, dwx_ref, dba_ref, dbx_ref, dlam_ref, dcw_ref,
         dcb_ref) = rest[nr:nr + 9]
        main_s, ctx_s, pad_s = rest[3 * nr + 9:3 * nr + 12]
        if nr:
            start, forward, finish = _chips_ops(rest[:nr], rest[nr + 9:2 * nr + 9], rest[2 * nr + 9:3 * nr + 9],
                                                *rest[3 * nr + 12:], first_chips=first_chips, barrier=True)
            pl.when(pl.program_id(0) == 0)(start)
            pl.when(pl.program_id(0) == HEADS // 2)(forward)
            pl.when(pl.program_id(0) == HEADS - 1)(finish)
        del dz_in

        @pl.when(pl.program_id(0) == 0)
        def _():
            dba_ref[...] = jnp.zeros_like(dba_ref)
            dbx_ref[...] = jnp.zeros_like(dbx_ref)

        cwv, cbv = cw_ref[...], cb_ref[...]
        lamv = lam_ref[...]
        sp = _softplus(-lamv)
        nsp = (-LRU_C) * sp
        z = jnp.zeros((1, HD), F32)

        def wmat(ref, d):
            return ref[d, 0].astype(BF16)

        def workspace(s):
            return dict(a=(s.at[0], s.at[1]), u=(s.at[2], s.at[3]), h=(s.at[4], s.at[5]), rho=(s.at[6], s.at[7]),
                        saved=(tuple(s.at[8 + k] for k in range(4)), tuple(s.at[12 + k] for k in range(4))),
                        xc=s.at[16])

        def forward(ws, xa, t_len, h0f, h0b):
            xc = _conv(xa, cwv, cbv, pad_s)
            ws["xc"][...] = xc
            for d in (0, 1):
                vals = _gates(xc, wmat(wa_ref, d), wmat(wx_ref, d), _bias_row(ba_ref, d)[0],
                              _bias_row(bx_ref, d)[0], nsp[d:d + 1, :])
                r, i, a, gamma, rg = vals
                ws["a"][d][...] = a
                ws["u"][d][...] = gamma * (i * xc)
                for ref, val in zip(ws["saved"][d], (r, i, gamma, rg)):
                    ref[...] = val
            return _scan_pair(ws["a"][0], ws["u"][0], ws["h"][0], h0f, ws["a"][1], ws["u"][1], ws["h"][1], h0b,
                              t_len)

        def backward(ws, xa, t_len, h0f, h0b, dhf, dhb, first):
            xc = ws["xc"][...]
            (af, ab), (uf, ub), (hf, hb), (rf, rb) = ws["a"], ws["u"], ws["h"], ws["rho"]
            uf[...] = ab[...] * dhb
            ub[...] = af[...] * dhf
            rho_b_last, rho_f_first = _scan_pair(ab, uf, rb, z, af, ub, rf, z, t_len)
            dxc = jnp.zeros((t_len, HD), F32)
            dsp = []
            for d in (0, 1):
                r, i, gamma, rg = (ref[...] for ref in ws["saved"][d])
                a = ws["a"][d][...]
                if d == 0:
                    lam_t = dhf + _shifted(pad_s, rf[...], (1,))[0]
                    h_prev = _shifted(pad_s, hf[...], (-1,), before=h0f)[0]
                else:
                    lam_t = dhb + _shifted(pad_s, rb[...], (-1,))[0]
                    h_prev = _shifted(pad_s, hb[...], (1,), after=h0b)[0]
                da = lam_t * h_prev
                lx = lam_t * xc
                d_i = lx * gamma
                d_gamma = lx * i
                dxc = dxc + lam_t * (gamma * i)
                d_log_a = a * (da - d_gamma * (a * rg))
                dsp.append(jnp.sum(d_log_a * r, axis=0, keepdims=True) * (-LRU_C))
                d_pre_r = d_log_a * nsp[d:d + 1, :] * (r * (1.0 - r))
                d_pre_i = d_i * (i * (1.0 - i))
                prb, pib, xb = d_pre_r.astype(BF16), d_pre_i.astype(BF16), xc.astype(BF16)
                dxc = dxc + _dot_nt(prb, wmat(wa_ref, d)) + _dot_nt(pib, wmat(wx_ref, d))
                g_wa, g_wx = _dot_tn(xb, prb), _dot_tn(xb, pib)
                g_ba = jnp.sum(d_pre_r, axis=0, keepdims=True)
                g_bx = jnp.sum(d_pre_i, axis=0, keepdims=True)
                mask = _bias_row(ba_ref, d)[1]
                dba_ref[...] += jnp.where(mask, g_ba, 0.0)
                dbx_ref[...] += jnp.where(mask, g_bx, 0.0)
                if first:
                    dwa_ref[d, 0] = g_wa
                    dwx_ref[d, 0] = g_wx
                else:
                    dwa_ref[d, 0] += g_wa
                    dwx_ref[d, 0] += g_wx
            g_lam = jnp.concatenate(dsp, axis=0) * (-_sigmoid(-lamv))
            dm1, dp1, dm2 = _shifted(pad_s, dxc, (-1, 1, -2))
            dxa = dp1 * cwv[0:1, :] + dxc * cwv[1:2, :] + dm1 * cwv[2:3, :] + dm2 * cwv[3:4, :]
            xm1, xp1, xp2 = _shifted(pad_s, xa, (-1, 1, 2))
            g_cw = jnp.concatenate([jnp.sum(dxc * v, axis=0, keepdims=True) for v in (xm1, xa, xp1, xp2)], axis=0)
            g_cb = jnp.sum(dxc, axis=0, keepdims=True)
            if first:
                dlam_ref[...] = g_lam
                dcw_ref[...] = g_cw
                dcb_ref[...] = g_cb
            else:
                dlam_ref[...] += g_lam
                dcw_ref[...] += g_cw
                dcb_ref[...] += g_cb
            return dxa, rho_f_first, rho_b_last

        ws_x, ws_c = workspace(main_s), workspace(ctx_s)
        h0f, h0b = forward(ws_c, xac_ref[...], LC, z, z)
        forward(ws_x, xa_ref[...], L, h0f, h0b)
        dh = dyl_ref[...]
        dxa, dh0f, dh0b = backward(ws_x, xa_ref[...], L, h0f, h0b, dh, dh, True)
        dxa_ref[...] = dxa.astype(BF16)
        rc = _rows((LC, HD))
        dxac, _, _ = backward(ws_c, xac_ref[...], LC, z, z, jnp.where(rc == LC - 1, dh0f, 0.0),
                              jnp.where(rc == 0, dh0b, 0.0), False)
        dxac_ref[...] = dxac.astype(BF16)

    s = _lru_param_specs()
    col = lambda r: pl.BlockSpec((r, HD), lambda h: (0, h))
    return _call(
        body, name="lru_backward", grid=(HEADS,),
        out_shape=[jax.ShapeDtypeStruct((L, D_IN), BF16), jax.ShapeDtypeStruct((LC, D), BF16),
                   jax.ShapeDtypeStruct((2, HEADS, HD, HD), F32), jax.ShapeDtypeStruct((2, HEADS, HD, HD), F32),
                   jax.ShapeDtypeStruct((2 * HEADS, HD), F32), jax.ShapeDtypeStruct((2 * HEADS, HD), F32),
                   jax.ShapeDtypeStruct((2, D), F32), jax.ShapeDtypeStruct((CONV_W, D), F32),
                   jax.ShapeDtypeStruct((1, D), F32)] + [jax.ShapeDtypeStruct((4,) + a.shape[1:], a.dtype)
                                                          for a in chip_sums] + _chips_stage_shapes(chip_sums),
        in_specs=[s["xa"], s["xac"], col(L), pl.BlockSpec(memory_space=pl.ANY), s["cw"], s["cb"], s["h4"], s["h4"],
                  s["b16"], s["b16"], s["v2"]] + [HBM] * nr,
        out_specs=[col(L), col(LC), s["h4"], s["h4"], s["b16"], s["b16"], s["v2"], col(CONV_W), col(1)]
        + [HBM] * (2 * nr),
        scratch_shapes=[pltpu.VMEM((17, L, HD), F32), pltpu.VMEM((17, LC, HD), F32), pltpu.VMEM((L + 16, HD), F32)]
        + (_chips_sems(nr) if nr else []),
        input_output_aliases={3: 0},
        compiler_params=pltpu.CompilerParams(dimension_semantics=("arbitrary",), vmem_limit_bytes=VMEM_LIMIT,
                                             has_side_effects=True, collective_id=6 if nr else None),
    )(zx, zc, dyl, dz, cw, cb, wa, wx, ba, bx, lam, *[pltpu.with_memory_space_constraint(a, pltpu.HBM)
                                                       for a in chip_sums])


def _mixer_loss(x, tgt, zx, yl, gx, fg, lng, lnb, ws, wst, bst, wout, tm):
    ncht = tm // CHUNK

    def body(x_ref, t_ref, ga_ref, u_ref, v_ref, gb_ref, yl_ref, gx_ref, fg_ref, lng_ref, lnb_ref, ws_ref, wst_ref,
             bst_ref, wout_ref,
             dz_ref, dyl_ref, dxn_ref, y_s, do_ref, dws_ref, dbst_ref, vec_ref,
             vn_s, mix_s, dm_s, dvn_s):
        step = pl.program_id(0)

        @pl.when(step == 0)
        def _():
            dws_ref[...] = jnp.zeros_like(dws_ref)
            dbst_ref[...] = jnp.zeros_like(dbst_ref)
            vec_ref[...] = jnp.zeros_like(vec_ref)

        u, v = u_ref[...], v_ref[...]
        ug, dug_du = _gelu_and_grad(u)
        vg, dvg_dv = _gelu_and_grad(v)
        mu = jnp.mean(vg, axis=-1, keepdims=True)
        vc = vg - mu
        rstd = lax.rsqrt(jnp.mean(vc * vc, axis=-1, keepdims=True) + LN_EPS)
        vhat = vc * rstd
        lngv = lng_ref[...]
        vn_s[...] = (vhat * lngv + lnb_ref[...]).astype(BF16)
        for ch in range(ncht):
            rs = slice(ch * CHUNK, (ch + 1) * CHUNK)
            for g in range(HEADS):
                cs = slice(g * HD, (g + 1) * HD)
                mix_s[rs, cs] = _dot(ws_ref[g], vn_s[rs, cs]) + bst_ref[:, g:g + 1]
        mixed = mix_s[...]
        ga, gb, yl = ga_ref[...], gb_ref[...], yl_ref[...]
        sga, dsga = _silu_and_grad(ga)
        sgb, dsgb = _silu_and_grad(gb)
        ys = ug * mixed
        y_s[:, 0:D] = (yl * sga).astype(BF16)
        y_s[:, D:D_MIX] = (ys * sgb).astype(BF16)
        o = _dot(y_s[...], wout_ref[...])
        gxv, fgv = gx_ref[...], fg_ref[...]
        xn = x_ref[...] + gxv * o
        rs2 = lax.rsqrt(jnp.mean(xn * xn, axis=-1, keepdims=True) + NORM_EPS)
        xh = xn * rs2
        diff = xh * fgv - t_ref[...]
        vec_ref[R_LOSS:R_LOSS + 1, :] += jnp.full((1, D), jnp.sum(diff * diff) * (0.5 / D), F32)
        dout = diff * (1.0 / D)
        w = dout * fgv
        dxn = rs2 * (w - xh * jnp.mean(w * xh, axis=-1, keepdims=True))
        dxn_ref[...] = dxn
        vec_ref[0:1, :] += jnp.sum(dxn * o, axis=0, keepdims=True)
        vec_ref[1:2, :] += jnp.sum(dout * xh, axis=0, keepdims=True)
        dob = (dxn * gxv).astype(BF16)
        do_ref[...] = dob
        dy = _dot_nt(dob, wout_ref[...])
        dya, dyb = dy[:, 0:D], dy[:, D:D_MIX]
        dyl_ref[...] = dya * sga
        dys = dyb * sgb
        dz_ref[:, 0:D] = jnp.zeros((tm, D), BF16)
        dz_ref[:, D:2 * D] = (dya * yl * dsga).astype(BF16)
        dz_ref[:, 2 * D:3 * D] = (dys * mixed * dug_du).astype(BF16)
        dz_ref[:, 4 * D:5 * D] = (dyb * ys * dsgb).astype(BF16)
        dm = dys * ug
        dm_s[...] = dm.astype(BF16)
        for g in range(HEADS):
            cs = slice(g * HD, (g + 1) * HD)
            dbst_ref[:, g:g + 1] += sum(jnp.sum(dm[ch * CHUNK:(ch + 1) * CHUNK, cs], axis=1, keepdims=True)
                                        for ch in range(ncht))
            for ch in range(ncht):
                rs = slice(ch * CHUNK, (ch + 1) * CHUNK)
                dws_ref[g] += _dot_nt(dm_s[rs, cs], vn_s[rs, cs])
                dvn_s[rs, cs] = _dot(wst_ref[g], dm_s[rs, cs])
        dvn = dvn_s[...]
        vec_ref[2:3, :] += jnp.sum(dvn * vhat, axis=0, keepdims=True)
        vec_ref[3:4, :] += jnp.sum(dvn, axis=0, keepdims=True)
        dvh = dvn * lngv
        dvg = rstd * (dvh - jnp.mean(dvh, axis=-1, keepdims=True) - vhat * jnp.mean(dvh * vhat, axis=-1, keepdims=True))
        dz_ref[:, 3 * D:4 * D] = (dvg * dvg_dv).astype(BF16)

    tile = pl.BlockSpec((tm, D), lambda i: (i, 0))
    zcol = lambda n: pl.BlockSpec((tm, D), lambda i: (i, n))
    vec = pl.BlockSpec((1, D), lambda i: (0, 0))
    full = lambda *s: pl.BlockSpec(s, lambda i: (0,) * len(s))
    return _call(
        body, name="mixer_loss", grid=(L // tm,),
        out_shape=[jax.ShapeDtypeStruct((L, D_IN), BF16), jax.ShapeDtypeStruct((L, D), F32),
                   jax.ShapeDtypeStruct((L, D), F32), jax.ShapeDtypeStruct((L, D_MIX), BF16),
                   jax.ShapeDtypeStruct((L, D), BF16),
                   jax.ShapeDtypeStruct((HEADS, CHUNK, CHUNK), F32), jax.ShapeDtypeStruct((CHUNK, HEADS), F32),
                   jax.ShapeDtypeStruct((8, D), F32)],
        in_specs=[tile, tile, zcol(1), zcol(2), zcol(3), zcol(4), tile, pl.BlockSpec((1, D), lambda i: (0, 2)),
                  vec, vec, vec,
                  full(HEADS, CHUNK, CHUNK), full(HEADS, CHUNK, CHUNK), full(CHUNK, HEADS),
                  pl.BlockSpec((D_MIX, D), lambda i: (0, 0), pipeline_mode=pl.Buffered(1))],
        out_specs=[pl.BlockSpec((tm, D_IN), lambda i: (i, 0)), tile, tile,
                   pl.BlockSpec((tm, D_MIX), lambda i: (i, 0)), tile,
                   full(HEADS, CHUNK, CHUNK), full(CHUNK, HEADS), full(8, D)],
        scratch_shapes=[pltpu.VMEM((tm, D), BF16), pltpu.VMEM((tm, D), F32),
                        pltpu.VMEM((tm, D), BF16), pltpu.VMEM((tm, D), F32)],
        compiler_params=_params("arbitrary"),
    )(x, tgt, zx, zx, zx, zx, yl, gx, fg, lng, lnb, ws, wst, bst, wout)


def _grad_w(a, b, a2, b2, tk, name, bw, first, nblocks, split, barrier_id, chip_sums=()):
    nk = a.shape[0] // tk
    m = a.shape[1]
    with_ctx = a2 is not None
    if split == "cols":
        slots, r, w = nblocks, m, bw // 2
        piece = lambda q, pc: (slice(None), slice(pc * w, (pc + 1) * w))
    else:
        slots, r, w = 4, m // 8, bw
        piece = lambda q, pc: (slice((2 * q + pc) * r, (2 * q + pc + 1) * r), slice(None))

    nr = len(chip_sums)

    def body(*refs):
        a_ref, b_ref = refs[:2]
        a2_ref, b2_ref = refs[2:4] if with_ctx else (None, None)
        base = 4 if with_ctx else 2
        sums_ref = refs[base + nr]
        acc, mine_v, send_v, stage_v, send_sems, recv_sems = refs[base + 3 * nr + 1:base + 3 * nr + 7]
        n, k = pl.program_id(0), pl.program_id(1)
        x, y, c = lax.axis_index("x"), lax.axis_index("y"), lax.axis_index("c")

        def to_sibling(s):
            return pltpu.make_async_remote_copy(src_ref=send_v.at[s], dst_ref=stage_v.at[s], send_sem=send_sems.at[s],
                                                recv_sem=recv_sems.at[s], device_id=(x, y, 1 - c),
                                                device_id_type=MESH)

        if nr:
            c_start, c_forward, c_finish = _chips_ops(refs[base:base + nr], refs[base + nr + 1:base + 2 * nr + 1],
                                                      refs[base + 2 * nr + 1:base + 3 * nr + 1],
                                                      *refs[base + 3 * nr + 7:])

            @pl.when(jnp.logical_and(n == 0, k == 0))
            def _():
                _barrier([(x, y, 1 - c)] + [(x ^ (j >> 1), y ^ (j & 1), c) for j in (1, 2)])
                c_start()
        else:
            pl.when(jnp.logical_and(n == 0, k == 0))(_sibling_barrier)

        @pl.when(k == 0)
        def _():
            acc[...] = _dot_tn(a_ref[...], b_ref[...])

        if nk > 1:
            @pl.when(k > 0)
            def _():
                acc[...] += _dot_tn(a_ref[...], b_ref[...])

        if with_ctx:
            @pl.when(jnp.logical_and(k == nk - 1, n == 0))
            def _():
                acc[:, 0:b2_ref.shape[1]] += _dot_tn(a2_ref[...], b2_ref[...])

        def hand_over(s, q):
            for pc in (0, 1):
                @pl.when(c == pc)
                def _(pc=pc):
                    mine_v[s] = acc[piece(q, pc)]
                    send_v[s] = acc[piece(q, 1 - pc)].astype(BF16)
            to_sibling(s).start()

        for i in range(nblocks):
            @pl.when(jnp.logical_and(k == nk - 1, n == i))
            def _(i=i):
                if split == "cols":
                    hand_over(i, 0)
                else:
                    for q in range(4):
                        hand_over(q, q)

        @pl.when(jnp.logical_and(k == nk - 1, n == nblocks - 1))
        def _():
            for s in range(slots):
                to_sibling(s).wait_recv()
                sums_ref[s] = (mine_v[s] + stage_v[s].astype(F32)).astype(BF16)
            for s in range(slots):
                to_sibling(s).wait_send()
            if nr:
                c_forward()
                c_finish()

    in_specs = [pl.BlockSpec((tk, m), lambda n, k: (k, 0)), pl.BlockSpec((tk, bw), lambda n, k: (k, n + first))]
    args = [a, b]
    if with_ctx:
        in_specs += [pl.BlockSpec(a2.shape, lambda n, k: (0, 0)), pl.BlockSpec(b2.shape, lambda n, k: (0, 0))]
        args += [a2, b2]
    in_specs += [HBM] * nr
    args += [pltpu.with_memory_space_constraint(s, pltpu.HBM) for s in chip_sums]
    return _call(
        body, name=name, grid=(nblocks, nk),
        out_shape=[jax.ShapeDtypeStruct((slots, r, w), BF16)]
        + [jax.ShapeDtypeStruct((4,) + s.shape[1:], s.dtype) for s in chip_sums] + _chips_stage_shapes(chip_sums),
        in_specs=in_specs, out_specs=[pl.BlockSpec((slots, r, w), lambda n, k: (0, 0, 0))] + [HBM] * (2 * nr),
        scratch_shapes=[pltpu.VMEM((m, bw), F32), pltpu.VMEM((slots, r, w), F32), pltpu.VMEM((slots, r, w), BF16),
                        pltpu.VMEM((slots, r, w), BF16), pltpu.SemaphoreType.DMA((slots,)),
                        pltpu.SemaphoreType.DMA((slots,))] + (_chips_sems(nr) if nr else []),
        compiler_params=pltpu.CompilerParams(dimension_semantics=("arbitrary", "arbitrary"),
                                             vmem_limit_bytes=VMEM_LIMIT, has_side_effects=True,
                                             collective_id=barrier_id),
    )(*args)


def _grad_rows(xr, dz, w, mod, ng, dres, ncols, tm, name, chip_sums=(), first_chips=None, dests=None):
    rows = xr.shape[0]
    steps = rows // tm
    with_dx = dres is not None
    nr = len(chip_sums)
    dests = [d for d in (dests or [None] * nr)]
    nd = sum(d is not None for d in dests)
    nin = 6 if with_dx else 5
    nout = 2 if with_dx else 1

    def body(*refs):
        if with_dx:
            x_ref, dz_ref, w_ref, sc_ref, ng_ref, dres_ref = refs[:nin]
            dx_ref, vec_ref = refs[nin + nr + nd:nin + nr + nd + nout]
        else:
            x_ref, dz_ref, w_ref, sc_ref, ng_ref = refs[:nin]
            (vec_ref,) = refs[nin + nr + nd:nin + nr + nd + nout]
        if nr:
            o0 = nin + nr + nd + nout
            start, forward, finish = _chips_ops(refs[nin:nin + nr], refs[o0:o0 + nr], refs[o0 + nr:o0 + 2 * nr],
                                                *refs[o0 + 2 * nr:], first_chips=first_chips, barrier=True)
            pl.when(pl.program_id(0) == 0)(start)
            pl.when(pl.program_id(0) == steps // 2)(forward)
            pl.when(pl.program_id(0) == steps - 1)(finish)

        @pl.when(pl.program_id(0) == 0)
        def _():
            vec_ref[...] = jnp.zeros_like(vec_ref)

        dhn = _dot_nt(dz_ref[...], w_ref[...])
        x = x_ref[...]
        rs = lax.rsqrt(jnp.mean(x * x, axis=-1, keepdims=True) + NORM_EPS)
        xh = x * rs
        ngv = ng_ref[...]
        y = xh * ngv
        vec_ref[0:1, :] += jnp.sum(dhn, axis=0, keepdims=True)
        vec_ref[1:2, :] += jnp.sum(dhn * y, axis=0, keepdims=True)
        dy = dhn * (1.0 + sc_ref[...])
        vec_ref[2:3, :] += jnp.sum(dy * xh, axis=0, keepdims=True)
        if with_dx:
            dxh = dy * ngv
            dx_ref[...] = dres_ref[...] + rs * (dxh - xh * jnp.mean(dxh * xh, axis=-1, keepdims=True))

    tile = pl.BlockSpec((tm, D), lambda i: (i, 0))
    vec = pl.BlockSpec((1, D), lambda i: (0, 0))
    in_specs = [tile, pl.BlockSpec((tm, ncols), lambda i: (i, 0)),
                pl.BlockSpec((D, ncols), lambda i: (0, 0), pipeline_mode=pl.Buffered(1)),
                pl.BlockSpec((1, D), lambda i: (0, 1)), vec]
    out_shape = [jax.ShapeDtypeStruct((8, D), F32)]
    out_specs = [pl.BlockSpec((8, D), lambda i: (0, 0))]
    args = [xr, dz, w, mod, ng]
    if with_dx:
        in_specs.append(tile)
        out_shape.insert(0, jax.ShapeDtypeStruct((rows, D), F32))
        out_specs.insert(0, tile)
        args.append(dres)
    aliases = {}
    for j, d in enumerate(dests):
        if d is not None:
            aliases[len(args) + nr + len(aliases)] = len(out_shape) + j
    in_specs += [HBM] * (nr + nd)
    out_specs += [HBM] * (2 * nr)
    out_shape += [jax.ShapeDtypeStruct((4,) + a.shape[1:], a.dtype) for a in chip_sums]
    out_shape += _chips_stage_shapes(chip_sums)
    args += [pltpu.with_memory_space_constraint(a, pltpu.HBM) for a in chip_sums]
    args += [pltpu.with_memory_space_constraint(d, pltpu.HBM) for d in dests if d is not None]
    return _call(body, name=name, grid=(steps,), out_shape=out_shape, in_specs=in_specs, out_specs=out_specs,
                 scratch_shapes=_chips_sems(nr) if nr else [], input_output_aliases=aliases,
                 compiler_params=pltpu.CompilerParams(dimension_semantics=("arbitrary",),
                                                      vmem_limit_bytes=VMEM_LIMIT, has_side_effects=bool(nr),
                                                      collective_id=7 if nr else None))(*args)


def _adamw(w, g, m, v):
    m = ADAM_B1 * m + (1.0 - ADAM_B1) * g
    v = ADAM_B2 * v + (1.0 - ADAM_B2) * (g * g)
    m_hat = m / (1.0 - ADAM_B1 ** ADAM_STEP)
    v_hat = v / (1.0 - ADAM_B2 ** ADAM_STEP)
    delta = -ADAM_LR * (m_hat / (jnp.sqrt(v_hat) + ADAM_EPS) + ADAM_WD * w)
    return delta, m, v


def _adamw_reduced(parts, w, m, v, tr, name):
    r, n = w.shape
    nparts = parts.shape[0]

    def body(p_ref, w_ref, m_ref, v_ref, g_ref, d_ref, mo_ref, vo_ref):
        g = p_ref[0].astype(F32)
        for i in range(1, nparts):
            g = g + p_ref[i].astype(F32)
        g_ref[...] = g
        d_ref[...], mo_ref[...], vo_ref[...] = _adamw(w_ref[...], g, m_ref[...], v_ref[...])

    tile = pl.BlockSpec((tr, n), lambda i: (i, 0))
    sds = jax.ShapeDtypeStruct((r, n), F32)
    return _call(
        body, name=name, grid=(r // tr,), out_shape=[sds] * 4,
        in_specs=[pl.BlockSpec((nparts, tr, n), lambda i: (0, i, 0)), tile, tile, tile], out_specs=[tile] * 4,
        compiler_params=_params("arbitrary"),
    )(parts, w, m, v)


R_GATE, R_FINAL_G, R_LN_G, R_LN_B, R_LOSS = 0, 1, 2, 3, 4
R_SH_X, R_SC_X, R_NG_X = 5, 6, 7
R_SH_C, R_SC_C, R_NG_C = 8, 9, 10
R_LAM, R_CW, R_CB = 11, 13, 17
PACK_ROWS = 24
Q_BA, Q_BX, Q_SGU_B, PACK128_ROWS = 0, 16, 32, 40


def _reduce_small(vp_all, vq_all, mat_parts, ada_w, me):
    nloc = ada_w.shape[1]
    nm = len(mat_parts)

    def body(me_ref, vp_ref, vq_ref, *refs):
        mp_refs, w_ref = refs[:nm], refs[nm]
        red_ref, redq_ref = refs[nm + 1:nm + 3]
        mat_refs = refs[nm + 3:2 * nm + 3]
        dmod_ref, gab_ref, cpart_ref, dmc_s = refs[2 * nm + 3:]
        red, redq = vp_ref[0], vq_ref[0]
        for i in range(1, N_DEV):
            red = red + vp_ref[i]
            redq = redq + vq_ref[i]
        red_ref[...] = red
        redq_ref[...] = redq
        for mp_ref, mat_ref in zip(mp_refs, mat_refs):
            mat = mp_ref[0].astype(F32)
            for i in range(1, mp_ref.shape[0]):
                mat = mat + mp_ref[i].astype(F32)
            mat_ref[...] = mat
        for e in range(N_DEV):
            dmod_ref[e:e + 1, 0:D] = vp_ref[e, R_SH_X:R_SH_X + 1, :]
            dmod_ref[e:e + 1, D:2 * D] = vp_ref[e, R_SC_X:R_SC_X + 1, :]
            dmod_ref[e:e + 1, 2 * D:3 * D] = vp_ref[e, R_GATE:R_GATE + 1, :]
        dmod_ref[8:9, 0:D] = red[R_SH_C:R_SH_C + 1, :]
        dmod_ref[8:9, D:2 * D] = red[R_SC_C:R_SC_C + 1, :]
        dmod_ref[8:9, 2 * D:3 * D] = jnp.zeros((1, D), F32)
        dmod_ref[9:16, :] = jnp.zeros((7, 3 * D), F32)
        gab_ref[:, 0:D] = red[R_SH_X:R_SH_X + 1, :] + red[R_SH_C:R_SH_C + 1, :]
        gab_ref[:, D:2 * D] = red[R_SC_X:R_SC_X + 1, :] + red[R_SC_C:R_SC_C + 1, :]
        gab_ref[:, 2 * D:3 * D] = red[R_GATE:R_GATE + 1, :]
        dmc_s[...] = jnp.broadcast_to(dmod_ref[8:9, :], (8, 3 * D))
        off = pl.multiple_of(me_ref[0] * nloc, 128)
        cpart_ref[...] = _dot_nt(dmc_s[:, pl.ds(off, nloc)], w_ref[...])

    return _call(
        body, name="reduce_small",
        out_shape=[jax.ShapeDtypeStruct((PACK_ROWS, D), F32), jax.ShapeDtypeStruct((PACK128_ROWS, HD), F32)]
        + [jax.ShapeDtypeStruct(p.shape[1:], F32) for p in mat_parts]
        + [jax.ShapeDtypeStruct((16, 3 * D), F32), jax.ShapeDtypeStruct((1, 3 * D), F32),
           jax.ShapeDtypeStruct((8, D), F32)],
        in_specs=[pl.BlockSpec(memory_space=pltpu.SMEM)] + [VMEM] * (nm + 3), out_specs=[VMEM] * (nm + 5),
        scratch_shapes=[pltpu.VMEM((8, 3 * D), F32)], compiler_params=_params(),
    )(me, vp_all, vq_all, *mat_parts, ada_w)


def _adamw_ada(c_all, c_ctx, dmod, w, m, v, me):
    nloc = w.shape[1]

    def body(me_ref, c_ref, cc_ref, dm_ref, w_ref, m_ref, v_ref, g_ref, d_ref, mo_ref, vo_ref):
        off = pl.multiple_of(me_ref[0] * nloc, 128)
        dm = dm_ref[:, pl.ds(off, nloc)]
        sx, _ = _silu_and_grad(c_ref[...])
        sc, _ = _silu_and_grad(cc_ref[...])
        g = _dot_tn(sx, dm[0:8, :]) + _dot_tn(jnp.broadcast_to(sc, (8, D)), dm[8:16, :])
        g_ref[...] = g
        d_ref[...], mo_ref[...], vo_ref[...] = _adamw(w_ref[...], g, m_ref[...], v_ref[...])

    sds = jax.ShapeDtypeStruct(w.shape, F32)
    return _call(
        body, name="adamw_ada_w", out_shape=[sds] * 4,
        in_specs=[pl.BlockSpec(memory_space=pltpu.SMEM)] + [VMEM] * 6, out_specs=[VMEM] * 4,
        compiler_params=_params(),
    )(me, c_all, c_ctx, dmod, w, m, v)


_SMALL = ("c_ctx", "ada_b", "norm_g", "conv_w", "conv_b", "lru_wa", "lru_ba", "lru_wx", "lru_bx", "lru_lambda",
          "sgu_ln_g", "sgu_ln_b", "sgu_w", "sgu_b", "final_g")


def _adamw_small(red, redq, mats, cparts, gab, ws, ms, vs, me):
    n = len(_SMALL)

    def body(me_ref, red_ref, redq_ref, wa_ref, wx_ref, sw_ref, cp_ref, gab_ref, *refs):
        w_refs, m_refs, v_refs = refs[:n], refs[n:2 * n], refs[2 * n:3 * n]
        outs = refs[3 * n:]
        off = pl.multiple_of(me_ref[0] * HD, 128)

        def row(r, k=1):
            return red_ref[r:r + k, :]

        cc = w_refs[0][...]
        dcc = cp_ref[0, 0:1, :]
        for i in range(1, N_DEV):
            dcc = dcc + cp_ref[i, 0:1, :]
        grads = dict(
            c_ctx=dcc * _silu_and_grad(cc)[1], ada_b=gab_ref[...], norm_g=row(R_NG_X) + row(R_NG_C),
            conv_w=red_ref[R_CW:R_CW + CONV_W, pl.ds(off, HD)], conv_b=row(R_CB),
            lru_wa=wa_ref[...], lru_ba=redq_ref[Q_BA:Q_BA + 2 * HEADS, :], lru_wx=wx_ref[...],
            lru_bx=redq_ref[Q_BX:Q_BX + 2 * HEADS, :], lru_lambda=red_ref[R_LAM:R_LAM + 2, pl.ds(off, HD)],
            sgu_ln_g=row(R_LN_G), sgu_ln_b=row(R_LN_B), sgu_w=sw_ref[...],
            sgu_b=redq_ref[Q_SGU_B:Q_SGU_B + HEADS, :], final_g=row(R_FINAL_G))
        for j, name in enumerate(_SMALL):
            g = grads[name]
            outs[j][...] = g
            outs[n + j][...], outs[2 * n + j][...], outs[3 * n + j][...] = _adamw(w_refs[j][...], g, m_refs[j][...],
                                                                                 v_refs[j][...])

    sds = [jax.ShapeDtypeStruct(ws[k].shape, F32) for k in _SMALL]
    outs = _call(
        body, name="adamw_small", out_shape=sds * 4,
        in_specs=[pl.BlockSpec(memory_space=pltpu.SMEM)] + [VMEM] * (7 + 3 * n), out_specs=[VMEM] * (4 * n),
        compiler_params=_params(),
    )(me, red, redq, *mats, cparts, gab, *[ws[k] for k in _SMALL], *[ms[k] for k in _SMALL],
      *[vs[k] for k in _SMALL])
    return [dict(zip(_SMALL, outs[i * n:(i + 1) * n])) for i in range(4)]


def kernel(x, c, ctx, c_ctx, ada_w, ada_b, norm_g, w_in, conv_w, conv_b, lru_wa, lru_ba, lru_wx, lru_bx, lru_lambda, sgu_ln_g, sgu_ln_b, sgu_w, sgu_b, w_out, final_g, loss_target, m_c_ctx, m_ada_w, m_ada_b, m_norm_g, m_w_in, m_conv_w, m_conv_b, m_lru_wa, m_lru_ba, m_lru_wx, m_lru_bx, m_lru_lambda, m_sgu_ln_g, m_sgu_ln_b, m_sgu_w, m_sgu_b, m_w_out, m_final_g, v_c_ctx, v_ada_w, v_ada_b, v_norm_g, v_w_in, v_conv_w, v_conv_b, v_lru_wa, v_lru_ba, v_lru_wx, v_lru_bx, v_lru_lambda, v_sgu_ln_g, v_sgu_ln_b, v_sgu_w, v_sgu_b, v_w_out, v_final_g):
    args = dict(locals())
    me_s = 4 * lax.axis_index("x") + 2 * lax.axis_index("y") + lax.axis_index("c")
    me = me_s.astype(jnp.int32).reshape(1)
    xr, ctxr, tgt = x[0], ctx[0], loss_target[0]
    cc = c_ctx.reshape(1, D)
    nw = 2 * HEADS * HD
    view = dict(c_ctx=(1, D), ada_b=(1, 3 * D), norm_g=(1, D), conv_w=(CONV_W, HD), conv_b=(1, D), lru_wa=(nw, HD),
                lru_ba=(2 * HEADS, HD), lru_wx=(nw, HD), lru_bx=(2 * HEADS, HD), lru_lambda=(2, HD), sgu_ln_g=(1, D),
                sgu_ln_b=(1, D), sgu_w=(HEADS * CHUNK, CHUNK), sgu_b=(HEADS, CHUNK), final_g=(1, D))

    zx, hn, w_full, w_out_b, modx, modc, c_all, cw_full, lam_full = _front_project(
        xr, c, cc, ada_w[0], ada_b, norm_g, w_in[0], w_out[0], conv_w[0], lru_lambda[0], me)
    zc, hnc = _project(ctxr, modc, norm_g, w_full, D, LC, "project_ctx")
    ba, bx = lru_ba.reshape(view["lru_ba"]), lru_bx.reshape(view["lru_bx"])
    yl, wout_all = _lru_forward(zx, zc, cw_full, conv_b, lru_wa[0], lru_wx[0], ba, bx, lam_full, [w_out_b], ["ag"])
    wout_full = wout_all.reshape(D_MIX, D)
    ws_b = sgu_w[0].astype(BF16)
    dz, dyl, dxn, ycat, dob, dws, dbst, mvec = _mixer_loss(
        xr, tgt, zx, yl, modx, final_g.reshape(1, D), sgu_ln_g, sgu_ln_b, ws_b, jnp.swapaxes(ws_b, 1, 2),
        sgu_b[0].T, wout_full, 256)

    (wout_sums,) = _grad_w(ycat, dob, None, None, L, "grad_w_out", D, 0, 1, "rows", 1)
    (rest_sums,) = _grad_w(hn, dz, None, None, L, "grad_w_in_rest", 2 * W_IN_SHARD, 1, 3, "cols", 2)
    dz, dxac, dwa, dwx, dba, dbx, dlam, dcw, dcb, win_parts, wout_parts, _, _ = _lru_backward(
        zx, zc, dyl, dz, cw_full, conv_b, lru_wa[0], lru_wx[0], ba, bx, lam_full, [rest_sums, wout_sums],
        first_chips=[1, 0])
    mats = [dwa.reshape(N_DEV, nw // N_DEV, HD), dwx.reshape(N_DEV, nw // N_DEV, HD), dws]
    mat_sums = _reduce2_local(mats, ["a2a"] * 3, me, "reduce_mat", 4, out_dtype=BF16)
    first_sums, *mat_parts = _grad_w(hn, dz, hnc, dxac, L, "grad_w_in_first", 2 * W_IN_SHARD, 0, 1, "cols", 3,
                                     chip_sums=mat_sums)[:4]
    gx, xvec, win_parts = _grad_rows(
        xr, dz, w_full, modx, norm_g, dxn, D_IN, 256, "grad_rows_x", chip_sums=[first_sums], first_chips=[0],
        dests=[win_parts])[:3]
    (cvec,) = _grad_rows(ctxr, dxac, w_full, modc, norm_g, None, D, LC, "grad_rows_ctx")
    pack = jnp.concatenate([mvec[0:5], xvec[0:3], cvec[0:3], dlam, dcw, dcb,
                            jnp.zeros((PACK_ROWS - R_CB - 1, D), F32)], axis=0)
    pack128 = jnp.concatenate([dba, dbx, dbst.T], axis=0)
    vp_all, vq_all = _gather2([pack, pack128], ["ag", "ag"], "gather_pack", 8)
    red, redq, *rest = _reduce_small(vp_all, vq_all, mat_parts, ada_w[0], me)
    mat_pieces, (dmod, gab, cpart) = rest[:3], rest[3:]
    *mats_all, cparts = _gather2([*mat_pieces, cpart], ["ag"] * 4, "gather_small", 9)

    g_w_in, d_w_in, nm_w_in, nv_w_in = _adamw_reduced(win_parts, w_in[0], m_w_in[0], v_w_in[0], 256, "adamw_w_in")
    g_w_out, d_w_out, nm_w_out, nv_w_out = _adamw_reduced(wout_parts, w_out[0], m_w_out[0], v_w_out[0], 128,
                                                          "adamw_w_out")
    g_ada, d_ada, nm_ada, nv_ada = _adamw_ada(c_all, cc, dmod, ada_w[0], m_ada_w[0], v_ada_w[0], me)
    ws = {k: args[k].reshape(view[k]) for k in _SMALL}
    ms = {k: args["m_" + k].reshape(view[k]) for k in _SMALL}
    vs = {k: args["v_" + k].reshape(view[k]) for k in _SMALL}
    small = _adamw_small(red, redq, [m.reshape(-1, HD) for m in mats_all], cparts, gab, ws, ms, vs, me)
    big = dict(w_in=(g_w_in, d_w_in, nm_w_in, nv_w_in), w_out=(g_w_out, d_w_out, nm_w_out, nv_w_out),
               ada_w=(g_ada, d_ada, nm_ada, nv_ada))

    loss = red[R_LOSS, 0]
    names = ("c_ctx", "ada_w", "ada_b", "norm_g", "w_in", "conv_w", "conv_b", "lru_wa", "lru_ba", "lru_wx", "lru_bx",
             "lru_lambda", "sgu_ln_g", "sgu_ln_b", "sgu_w", "sgu_b", "w_out", "final_g")
    outs = [loss, gx.reshape(x.shape)]
    for kind in range(4):
        for k in names:
            val = big[k][kind] if k in big else small[kind][k]
            outs.append(val.reshape(args[k].shape))
    return tuple(outs)
```

```python
import functools

import jax
import jax.numpy as jnp
from jax import lax
from jax.experimental import pallas as pl
from jax.experimental.pallas import tpu as pltpu

F32 = jnp.float32
BF16 = jnp.bfloat16

N_DEV = 8
D = 1024
L = 2048
LC = 256
HEADS = 8
HD = 128
CHUNK = 128
D_IN = 5 * D
W_IN_SHARD = D_IN // N_DEV
ROWS = 256
D_MIX = 2 * D
CONV_W = 4
LRU_C = 8.0
NORM_EPS = 1e-6
LN_EPS = 1e-5
ADAM_LR, ADAM_B1, ADAM_B2, ADAM_EPS, ADAM_WD, ADAM_STEP = 0.001, 0.9, 0.999, 1e-08, 0.01, 10

VMEM_LIMIT = 56 * 1024 * 1024

HBM = pl.BlockSpec(memory_space=pltpu.HBM)
VMEM = pl.BlockSpec(memory_space=pltpu.VMEM)
MESH = pl.DeviceIdType.MESH


def _call(body, **kw):
    return pl.pallas_call(body, **kw)


def _params(*sem):
    return pltpu.CompilerParams(dimension_semantics=sem, vmem_limit_bytes=VMEM_LIMIT)


def _sigmoid(x):
    return 0.5 * jnp.tanh(0.5 * x) + 0.5


def _silu_and_grad(x):
    s = _sigmoid(x)
    return x * s, s * (1.0 + x * (1.0 - s))


_G0 = 0.7978845608028654
_G1 = 0.044715


def _gelu_and_grad(x):
    x2 = x * x
    t = jnp.tanh(_G0 * (x + _G1 * x * x2))
    cdf = 0.5 * (1.0 + t)
    return x * cdf, cdf + 0.5 * x * (1.0 - t * t) * (_G0 * (1.0 + 3.0 * _G1 * x2))


def _gelu(x):
    return 0.5 * x * (1.0 + jnp.tanh(_G0 * (x + _G1 * x * x * x)))


def _softplus(z):
    t = jnp.exp(-jnp.abs(z))
    u = 1.0 + t
    log1p = jnp.where(u == 1.0, t, jnp.log(u) * t / jnp.where(u == 1.0, 1.0, u - 1.0))
    return jnp.maximum(z, 0.0) + log1p


def _dot(a, b):
    return jnp.dot(a, b, preferred_element_type=F32)


def _dot_nt(a, b):
    return lax.dot_general(a, b, (((1,), (1,)), ((), ())), preferred_element_type=F32)


def _dot_tn(a, b):
    return lax.dot_general(a, b, (((0,), (0,)), ((), ())), preferred_element_type=F32)


def _rows(shape):
    return lax.broadcasted_iota(jnp.int32, shape, 0)


def _shift_down(x, first):
    y = pltpu.roll(x, 1, 0)
    head = jnp.where(_rows((8, x.shape[1])) == 0, first, y[0:8])
    return jnp.concatenate([head, y[8:]], axis=0)


def _shift_up(x, last):
    n = x.shape[0]
    y = pltpu.roll(x, n - 1, 0)
    tail = jnp.where(_rows((8, x.shape[1])) == 7, last, y[n - 8:])
    return jnp.concatenate([y[:n - 8], tail], axis=0)


def _gather2(arrays, modes, name, barrier_id):
    n = len(arrays)

    def body(*refs):
        start, forward, finish = _gather2_ops(refs[:n], refs[n:2 * n], modes, *refs[2 * n:], barrier=True)
        start()
        forward()
        finish()

    return _call(
        body, name=name, out_shape=_gather2_shapes(arrays, modes), in_specs=[HBM] * n, out_specs=[HBM] * n,
        scratch_shapes=_gather2_sems(n),
        compiler_params=pltpu.CompilerParams(has_side_effects=True, collective_id=barrier_id),
    )(*[pltpu.with_memory_space_constraint(a, pltpu.HBM) for a in arrays])


def _gather2_shapes(arrays, modes):
    return [jax.ShapeDtypeStruct((N_DEV,) + a.shape if m == "ag" else (a.shape[0], N_DEV * a.shape[1]), a.dtype)
            for a, m in zip(arrays, modes)]


def _gather2_sems(n):
    return [pltpu.SemaphoreType.DMA((n, N_DEV - 1)), pltpu.SemaphoreType.DMA((n, N_DEV - 1)),
            pltpu.SemaphoreType.DMA((n,))]


def _barrier(peers):
    sem = pltpu.get_barrier_semaphore()
    for peer in peers:
        pl.semaphore_signal(sem, inc=1, device_id=peer, device_id_type=MESH)
    pl.semaphore_wait(sem, len(peers))


def _gather2_ops(ins, outs, modes, send_sems, recv_sems, local_sems, barrier=False):
    n = len(ins)
    x, y, c = lax.axis_index("x"), lax.axis_index("y"), lax.axis_index("c")
    me, sibling = (x, y, c), (x, y, 1 - c)
    chips = [(x ^ (k >> 1), y ^ (k & 1)) for k in (1, 2, 3)]

    def slot(j, px, py, pc):
        dev = 4 * px + 2 * py + pc
        if modes[j] == "agc":
            w = ins[j].shape[1]
            return outs[j].at[:, pl.ds(pl.multiple_of(dev * w, 128), w)]
        return outs[j].at[dev]

    def copy(j, k, block, to, src=None):
        return pltpu.make_async_remote_copy(
            src_ref=slot(j, *block) if src is None else src, dst_ref=slot(j, *block),
            send_sem=send_sems.at[j, k], recv_sem=recv_sems.at[j, k], device_id=to, device_id_type=MESH)

    def own(j):
        return pltpu.make_async_copy(ins[j], slot(j, *me), local_sems.at[j])

    def first(j):
        return [copy(j, 0, me, sibling, src=ins[j])] + [copy(j, 1 + i, me, (*chip, c), src=ins[j])
                                                        for i, chip in enumerate(chips)]

    def passed(j, i):
        return copy(j, 4 + i, (*chips[i], c), sibling)

    def start():
        if barrier:
            _barrier([sibling] + [(*chip, c) for chip in chips])
        for j in range(n):
            own(j).start()
            for cp in first(j):
                cp.start()

    def forward():
        for i, chip in enumerate(chips):
            for j in range(n):
                copy(j, 1 + i, (*chip, c), me).wait_recv()
                passed(j, i).start()

    def finish():
        for j in range(n):
            copy(j, 0, sibling, me).wait_recv()
            for i, chip in enumerate(chips):
                copy(j, 4 + i, (*chip, 1 - c), me).wait_recv()
            for cp in first(j) + [passed(j, i) for i in range(3)]:
                cp.wait_send()
            own(j).wait()

    return start, forward, finish


def _sibling_barrier():
    sem = pltpu.get_barrier_semaphore()
    sibling = (lax.axis_index("x"), lax.axis_index("y"), 1 - lax.axis_index("c"))
    pl.semaphore_signal(sem, inc=1, device_id=sibling, device_id_type=MESH)
    pl.semaphore_wait(sem, 1)


def _reduce2_local(arrays, modes, me, name, barrier_id, counts=None, out_dtype=None):
    n = len(arrays)
    counts = counts or [4] * n
    shapes = [(a.shape[1], a.shape[2]) if m == "a2a" else (a.shape[0], a.shape[1] // (2 * cnt))
              for a, m, cnt in zip(arrays, modes, counts)]
    staged = [jax.ShapeDtypeStruct((cnt,) + s, a.dtype) for s, a, cnt in zip(shapes, arrays, counts)]

    def piece(ref, mode, dev, w):
        return ref.at[dev] if mode == "a2a" else ref.at[:, pl.ds(pl.multiple_of(dev * w, 128), w)]

    def to_sibling(*refs):
        ins, outs = refs[:n], refs[n:2 * n]
        send_sems, recv_sems = refs[2 * n:]
        x, y, c = lax.axis_index("x"), lax.axis_index("y"), lax.axis_index("c")
        _sibling_barrier()
        copies = []
        for j in range(n):
            for q in range(counts[j]):
                cp = pltpu.make_async_remote_copy(
                    src_ref=piece(ins[j], modes[j], 2 * q + (1 - c), shapes[j][1]), dst_ref=outs[j].at[q],
                    send_sem=send_sems.at[j, q], recv_sem=recv_sems.at[j, q], device_id=(x, y, 1 - c),
                    device_id_type=MESH)
                cp.start()
                copies.append(cp)
        for cp in copies:
            cp.wait()

    stage = _call(
        to_sibling, name=name + "_d2d", out_shape=staged, in_specs=[HBM] * n, out_specs=[HBM] * n,
        scratch_shapes=[pltpu.SemaphoreType.DMA((n, 4)), pltpu.SemaphoreType.DMA((n, 4))],
        compiler_params=pltpu.CompilerParams(has_side_effects=True, collective_id=barrier_id),
    )(*[pltpu.with_memory_space_constraint(a, pltpu.HBM) for a in arrays])

    def add(me_ref, *refs):
        del me_ref
        own, got, outs = refs[:n], refs[n:2 * n], refs[2 * n:]
        for j in range(n):
            mine = own[j][0] if modes[j] == "a2a" else own[j][...]
            outs[j][0] = (mine.astype(F32) + got[j][0].astype(F32)).astype(outs[j].dtype)

    in_specs, slot_specs = [], []
    for (r, w), m, cnt in zip(shapes, modes, counts):
        if m == "a2a":
            in_specs.append(pl.BlockSpec(
                (1, r, w), lambda q, me_ref, cnt=cnt: (2 * jnp.minimum(q, cnt - 1) + me_ref[0] % 2, 0, 0)))
        else:
            in_specs.append(pl.BlockSpec(
                (r, w), lambda q, me_ref, cnt=cnt: (0, 2 * jnp.minimum(q, cnt - 1) + me_ref[0] % 2)))
        slot_specs.append(pl.BlockSpec((1, r, w), lambda q, me_ref, cnt=cnt: (jnp.minimum(q, cnt - 1), 0, 0)))
    return _call(
        add, name=name + "_add",
        out_shape=[jax.ShapeDtypeStruct(s.shape, out_dtype or s.dtype) for s in staged],
        grid_spec=pltpu.PrefetchScalarGridSpec(num_scalar_prefetch=1, grid=(max(counts),),
                                               in_specs=in_specs + slot_specs, out_specs=slot_specs),
        compiler_params=_params("arbitrary"),
    )(me, *arrays, *stage)


def _chips_sems(n):
    return [pltpu.SemaphoreType.DMA((n, 6)), pltpu.SemaphoreType.DMA((n, 6)), pltpu.SemaphoreType.DMA((n,))]


def _chips_stage_shapes(chip_sums):
    return [jax.ShapeDtypeStruct((2, a.shape[1] // 2, a.shape[2]), a.dtype) for a in chip_sums]


def _chips_ops(ins, outs, stages, send_sems, recv_sems, local_sems, first_chips=None, barrier=False):
    x, y, c = lax.axis_index("x"), lax.axis_index("y"), lax.axis_index("c")
    qm = 2 * x + y
    first_chips = first_chips or [0] * len(ins)

    def owns(j, chip):
        lo, cnt = first_chips[j], ins[j].shape[0]
        if lo == 0 and cnt == 4:
            return None
        return jnp.logical_and(chip >= lo, chip < lo + cnt)

    def guarded(cond, fn):
        if cond is None:
            fn()
        else:
            pl.when(cond)(fn)

    def slot(j, chip):
        return jnp.clip(chip - first_chips[j], 0, ins[j].shape[0] - 1)

    def half(j, i):
        h = ins[j].shape[1] // 2
        return pl.ds(i * h, h)

    def copy(j, sem, src, dst, k):
        return pltpu.make_async_remote_copy(
            src_ref=src, dst_ref=dst, send_sem=send_sems.at[j, sem], recv_sem=recv_sems.at[j, sem],
            device_id=(x ^ (k >> 1), y ^ (k & 1), c), device_id_type=MESH)

    def direct(j, k):
        return copy(j, k - 1, ins[j].at[slot(j, qm ^ k)], outs[j].at[qm], k)

    def first_hop(j, k):
        return copy(j, 1 + k, ins[j].at[slot(j, qm ^ 3), half(j, k - 1)], stages[j].at[k - 1], k)

    def second_hop(j, k):
        return copy(j, 3 + k, stages[j].at[2 - k], outs[j].at[qm ^ (3 - k), half(j, 2 - k)], k)

    def local(j):
        return pltpu.make_async_copy(ins[j].at[slot(j, qm)], outs[j].at[qm], local_sems.at[j])

    def start():
        if barrier:
            _barrier([(x ^ (k >> 1), y ^ (k & 1), c) for k in (1, 2)])
        for j in range(len(ins)):
            for k in (1, 2):
                guarded(owns(j, qm ^ 3), lambda j=j, k=k: first_hop(j, k).start())
        for j in range(len(ins)):
            for k in (1, 2):
                guarded(owns(j, qm ^ k), lambda j=j, k=k: direct(j, k).start())
            guarded(owns(j, qm), lambda j=j: local(j).start())

    def forward():
        for j in range(len(ins)):
            for k in (1, 2):
                def pass_on(j=j, k=k):
                    first_hop(j, 3 - k).wait_recv()
                    second_hop(j, k).start()
                guarded(owns(j, qm ^ k), pass_on)

    def finish():
        for j in range(len(ins)):
            for k in (1, 2):
                guarded(owns(j, qm ^ k), lambda j=j, k=k: direct(j, k).wait_send())
                guarded(owns(j, qm ^ k), lambda j=j, k=k: second_hop(j, k).wait_send())
                guarded(owns(j, qm ^ 3), lambda j=j, k=k: first_hop(j, k).wait_send())
                guarded(owns(j, qm), lambda j=j, k=k: direct(j, k).wait_recv())
                guarded(owns(j, qm), lambda j=j, k=k: second_hop(j, k).wait_recv())
            guarded(owns(j, qm), lambda j=j: local(j).wait())

    return start, forward, finish


def _front(c, c_ctx, ada_w, ada_b, w_in, w_out, me):
    nloc = ada_w.shape[1]

    def body(me_ref, c_ref, cc_ref, aw_ref, ab_ref, win_ref, wout_ref,
             wfull_ref, woutb_ref, modx_ref, modc_ref, call_ref,
             wb_s, part_s, parts_s, w_send, w_recv, w_local, s_send, s_recv):
        x, y, cidx = lax.axis_index("x"), lax.axis_index("y"), lax.axis_index("c")
        me = me_ref[0]
        wb_s[...] = win_ref[...].astype(BF16)
        woutb_ref[...] = wout_ref[...].astype(BF16)
        start, forward, finish = _gather2_ops([wb_s], [wfull_ref], ["agc"], w_send, w_recv, w_local)
        start()

        def small_gather(src, my_slot, stage):
            copies = []
            for k in range(1, N_DEV):
                peer = (x ^ (k >> 2), y ^ ((k >> 1) & 1), cidx ^ (k & 1))
                cp = pltpu.make_async_remote_copy(src_ref=src, dst_ref=my_slot, send_sem=s_send.at[stage, k - 1],
                                                  recv_sem=s_recv.at[stage, k - 1], device_id=peer,
                                                  device_id_type=MESH)
                cp.start()
                copies.append(cp)
            pltpu.sync_copy(src, my_slot)
            for cp in copies:
                cp.wait()

        small_gather(c_ref, call_ref.at[pl.ds(me, 1), :], 0)
        off = pl.multiple_of(me * nloc, 128)
        b = ab_ref[:, pl.ds(off, nloc)]
        w = aw_ref[...]
        sx, _ = _silu_and_grad(call_ref[...])
        sc, _ = _silu_and_grad(jnp.broadcast_to(cc_ref[...], (8, D)))
        part_s[0:8, :] = _dot(sx, w) + b
        part_s[8:16, :] = _dot(sc, w) + b
        small_gather(part_s, parts_s.at[me], 1)
        mine = _rows((16, nloc)) == me
        for j in range(N_DEV):
            pj = parts_s[j]
            modx_ref[:, j * nloc:(j + 1) * nloc] = jnp.sum(jnp.where(mine, pj, 0.0), axis=0, keepdims=True)
            modc_ref[:, j * nloc:(j + 1) * nloc] = pj[8:9, :]
        forward()
        finish()

    return _call(
        body, name="front",
        out_shape=[jax.ShapeDtypeStruct((D, D_IN), BF16), jax.ShapeDtypeStruct(w_out.shape, BF16),
                   jax.ShapeDtypeStruct((1, 3 * D), F32), jax.ShapeDtypeStruct((1, 3 * D), F32),
                   jax.ShapeDtypeStruct((N_DEV, D), F32)],
        in_specs=[pl.BlockSpec(memory_space=pltpu.SMEM)] + [VMEM] * 6, out_specs=[HBM, VMEM, VMEM, VMEM, VMEM],
        scratch_shapes=[pltpu.VMEM(w_in.shape, BF16), pltpu.VMEM((16, nloc), F32),
                        pltpu.VMEM((N_DEV, 16, nloc), F32)] + _gather2_sems(1) +
                       [pltpu.SemaphoreType.DMA((2, N_DEV - 1)), pltpu.SemaphoreType.DMA((2, N_DEV - 1))],
        compiler_params=pltpu.CompilerParams(vmem_limit_bytes=VMEM_LIMIT, has_side_effects=True),
    )(me, c, c_ctx, ada_w, ada_b, w_in, w_out)


ARRIVAL = (0, 1, 2, 4, 3, 5, 6, 7)


def _front_project(xr, c, c_ctx, ada_w, ada_b, ng, w_in, w_out, cw, lam, me):
    nloc = ada_w.shape[1]
    ws = W_IN_SHARD
    order = me[0] ^ jnp.asarray(ARRIVAL, jnp.int32)

    def body(ord_ref, x_ref, c_ref, cc_ref, aw_ref, ab_ref, ng_ref, win_ref, wout_ref, cw_ref, lam_ref,
             z_ref, hn_ref, wfull_ref, woutb_ref, modx_ref, modc_ref, call_ref, cwf_ref, lamf_ref,
             wv, call_s, part_s, parts_s, w_send, w_recv, hbm_sems, s_send, s_recv, g_send, g_recv, g_local):
        t = pl.program_id(0)
        x, y, cidx = lax.axis_index("x"), lax.axis_index("y"), lax.axis_index("c")
        me_i = ord_ref[0]
        sibling = (x, y, 1 - cidx)
        chips = [(x ^ (k >> 1), y ^ (k & 1)) for k in (1, 2, 3)]
        g_start, g_pass, g_finish = _gather2_ops([cw_ref, lam_ref], [cwf_ref, lamf_ref], ["agc", "agc"],
                                                 g_send, g_recv, g_local)

        def shard_copy(k, px, py, pc, to, half=None):
            slot = wv.at[4 * px + 2 * py + pc]
            if half is not None:
                slot = slot.at[pl.ds(half * (D // 2), D // 2), :]
            return pltpu.make_async_remote_copy(src_ref=slot, dst_ref=slot, send_sem=w_send.at[k],
                                                recv_sem=w_recv.at[k], device_id=to, device_id_type=MESH)

        def small_gather(src, my_slot, stage):
            copies = []
            for k in range(1, N_DEV):
                peer = (x ^ (k >> 2), y ^ ((k >> 1) & 1), cidx ^ (k & 1))
                cp = pltpu.make_async_remote_copy(src_ref=src, dst_ref=my_slot, send_sem=s_send.at[stage, k - 1],
                                                  recv_sem=s_recv.at[stage, k - 1], device_id=peer,
                                                  device_id_type=MESH)
                cp.start()
                copies.append(cp)
            pltpu.sync_copy(src, my_slot)
            return copies

        def finish_small(copies):
            for cp in copies:
                cp.wait()

        def to_neighbours(half):
            for i in (0, 1):
                shard_copy(1 + i, x, y, cidx, (*chips[i], cidx), half=half).start()

        @pl.when(t == 0)
        def _():
            _barrier([(x ^ (k >> 2), y ^ ((k >> 1) & 1), cidx ^ (k & 1)) for k in range(1, N_DEV)])
            g_start()
            wv[me_i] = win_ref[...].astype(BF16)
            woutb_ref[...] = wout_ref[...].astype(BF16)
            shard_copy(0, x, y, cidx, sibling).start()
            finish_small(small_gather(c_ref, call_s.at[pl.ds(me_i, 1), :], 0))
            to_neighbours(0)
            call_ref[...] = call_s[...]
            off = pl.multiple_of(me_i * nloc, 128)
            b = ab_ref[:, pl.ds(off, nloc)]
            w = aw_ref[...]
            sx, _ = _silu_and_grad(call_s[...])
            sc, _ = _silu_and_grad(jnp.broadcast_to(cc_ref[...], (8, D)))
            part_s[0:8, :] = _dot(sx, w) + b
            part_s[8:16, :] = _dot(sc, w) + b
            parts_sent = small_gather(part_s, parts_s.at[me_i], 1)
            to_neighbours(1)
            finish_small(parts_sent)
            mine = _rows((16, nloc)) == me_i
            for j in range(N_DEV):
                pj = parts_s[j]
                modx_ref[:, j * nloc:(j + 1) * nloc] = jnp.sum(jnp.where(mine, pj, 0.0), axis=0, keepdims=True)
                modc_ref[:, j * nloc:(j + 1) * nloc] = pj[8:9, :]
            shift, scale1, ngv = modx_ref[:, 0:D], 1.0 + modx_ref[:, D:2 * D], ng_ref[...]
            for r in range(L // ROWS):
                rsl = slice(r * ROWS, (r + 1) * ROWS)
                xv = x_ref[rsl, :]
                rs = lax.rsqrt(jnp.mean(xv * xv, axis=-1, keepdims=True) + NORM_EPS)
                hn_ref[rsl, :] = ((xv * rs * ngv) * scale1 + shift).astype(BF16)

        @pl.when(t == 1)
        def _():
            shard_copy(0, x, y, 1 - cidx, sibling).wait_recv()
            g_pass()

        for i in (0, 1):
            @pl.when(t == ARRIVAL.index((2, 4)[i]))
            def _(i=i):
                shard_copy(1 + i, *chips[i], cidx, sibling).wait_recv()
                shard_copy(4 + i, *chips[i], cidx, sibling).start()
                shard_copy((7, 3)[i], *chips[i], cidx, (*chips[1 - i], cidx), half=i).start()

        @pl.when(t == ARRIVAL.index(6))
        def _():
            shard_copy(3, *chips[2], cidx, sibling, half=1).wait_recv()
            shard_copy(7, *chips[2], cidx, sibling, half=0).wait_recv()
            shard_copy(6, *chips[2], cidx, sibling).start()

        for i in range(3):
            @pl.when(t == ARRIVAL.index((3, 5, 7)[i]))
            def _(i=i):
                shard_copy(4 + i, *chips[i], 1 - cidx, sibling).wait_recv()

        @pl.when(t == 2)
        def _():
            g_finish()

        dev = ord_ref[t]
        for r in range(L // (2 * ROWS)):
            rsl = slice(r * 2 * ROWS, (r + 1) * 2 * ROWS)
            z_ref[rsl, :] = _dot(hn_ref[rsl, :], wv[dev])
        col = pl.ds(pl.multiple_of(dev * ws, 128), ws)
        pltpu.make_async_copy(wv.at[dev], wfull_ref.at[:, col], hbm_sems.at[t]).start()

        @pl.when(t == N_DEV - 1)
        def _():
            for k in (0, 1, 2, 4, 5, 6):
                shard_copy(k, x, y, cidx, sibling).wait_send()
            for k in (3, 7):
                shard_copy(k, x, y, cidx, sibling, half=0).wait_send()
            for s in range(N_DEV):
                pltpu.make_async_copy(wv.at[0], wfull_ref.at[:, pl.ds(0, ws)], hbm_sems.at[s]).wait()

    const = lambda *shape: pl.BlockSpec(shape, lambda t, o: (0,) * len(shape))
    once = lambda *shape: pl.BlockSpec(shape, lambda t, o: (0,) * len(shape), pipeline_mode=pl.Buffered(1))
    return _call(
        body, name="front_project",
        out_shape=[jax.ShapeDtypeStruct((L, D_IN), F32), jax.ShapeDtypeStruct((L, D), BF16),
                   jax.ShapeDtypeStruct((D, D_IN), BF16), jax.ShapeDtypeStruct(w_out.shape, BF16),
                   jax.ShapeDtypeStruct((1, 3 * D), F32), jax.ShapeDtypeStruct((1, 3 * D), F32),
                   jax.ShapeDtypeStruct((N_DEV, D), F32), jax.ShapeDtypeStruct((CONV_W, D), F32),
                   jax.ShapeDtypeStruct((2, D), F32)],
        grid_spec=pltpu.PrefetchScalarGridSpec(
            num_scalar_prefetch=1, grid=(N_DEV,),
            in_specs=[once(L, D), const(1, D), const(1, D), once(D, nloc), const(1, 3 * D), const(1, D),
                      once(D, ws), once(*w_out.shape), HBM, HBM],
            out_specs=[pl.BlockSpec((L, ws), lambda t, o: (0, o[t])), const(L, D), HBM, const(*w_out.shape),
                       const(1, 3 * D), const(1, 3 * D), const(N_DEV, D), HBM, HBM],
            scratch_shapes=[pltpu.VMEM((N_DEV, D, ws), BF16), pltpu.VMEM((N_DEV, D), F32), pltpu.VMEM((16, nloc), F32),
                            pltpu.VMEM((N_DEV, 16, nloc), F32), pltpu.SemaphoreType.DMA((8,)),
                            pltpu.SemaphoreType.DMA((8,)), pltpu.SemaphoreType.DMA((N_DEV,)),
                            pltpu.SemaphoreType.DMA((2, N_DEV - 1)), pltpu.SemaphoreType.DMA((2, N_DEV - 1))]
            + _gather2_sems(2)),
        compiler_params=pltpu.CompilerParams(dimension_semantics=("arbitrary",), vmem_limit_bytes=VMEM_LIMIT,
                                             has_side_effects=True, collective_id=10),
    )(order, xr, c, c_ctx, ada_w, ada_b, ng, w_in, w_out, pltpu.with_memory_space_constraint(cw, pltpu.HBM),
      pltpu.with_memory_space_constraint(lam, pltpu.HBM))


def _project(xr, mod, ng, w, ncols, tm, name, gather=None, gather_modes=()):
    rows = xr.shape[0]
    steps = rows // tm
    ng_ = len(gather or ())

    def body(x_ref, sh_ref, sc_ref, ng_ref, w_ref, *rest):
        z_ref, hn_ref = rest[ng_:ng_ + 2]
        if ng_:
            start, forward, finish = _gather2_ops(rest[:ng_], rest[ng_ + 2:2 * ng_ + 2], gather_modes,
                                                  *rest[2 * ng_ + 2:])
            pl.when(pl.program_id(0) == 0)(start)
            pl.when(pl.program_id(0) == steps // 2)(forward)
        x = x_ref[...]
        rs = lax.rsqrt(jnp.mean(x * x, axis=-1, keepdims=True) + NORM_EPS)
        hn = (x * rs * ng_ref[...]) * (1.0 + sc_ref[...]) + sh_ref[...]
        hb = hn.astype(BF16)
        hn_ref[...] = hb
        for n in range(ncols // D):
            z_ref[:, n * D:(n + 1) * D] = _dot(hb, w_ref[:, n * D:(n + 1) * D])
        if ng_:
            pl.when(pl.program_id(0) == steps - 1)(finish)

    vec = pl.BlockSpec((1, D), lambda i: (0, 0))
    gathered = _gather2_shapes(gather, gather_modes) if ng_ else []
    return _call(
        body, name=name, grid=(steps,),
        out_shape=[jax.ShapeDtypeStruct((rows, ncols), F32), jax.ShapeDtypeStruct((rows, D), BF16)] + gathered,
        in_specs=[pl.BlockSpec((tm, D), lambda i: (i, 0)), vec, pl.BlockSpec((1, D), lambda i: (0, 1)), vec,
                  pl.BlockSpec((D, ncols), lambda i: (0, 0), pipeline_mode=pl.Buffered(1))] + [HBM] * ng_,
        out_specs=[pl.BlockSpec((tm, ncols), lambda i: (i, 0)), pl.BlockSpec((tm, D), lambda i: (i, 0))] + [HBM] * ng_,
        scratch_shapes=_gather2_sems(ng_) if ng_ else [],
        compiler_params=pltpu.CompilerParams(dimension_semantics=("arbitrary",), vmem_limit_bytes=VMEM_LIMIT,
                                             has_side_effects=bool(ng_)),
    )(xr, mod, mod, ng, w, *[pltpu.with_memory_space_constraint(a, pltpu.HBM) for a in gather or ()])


def _scan_pair(af_ref, uf_ref, hf_ref, h0f, ab_ref, ub_ref, hb_ref, h0b, t_len):
    span = 8 * SCAN_BLOCKS
    nit = t_len // span
    rows = _rows((8, HD))

    def local_scan(a, b, forward):
        for s in (1, 2, 4):
            sh = s if forward else 8 - s
            m = rows >= s if forward else rows < 8 - s
            b = a * jnp.where(m, pltpu.roll(b, sh, 0), 0.0) + b
            a = a * jnp.where(m, pltpu.roll(a, sh, 0), 1.0)
        return a, b

    def span_scan(a_ref, u_ref, h_ref, off, carry, forward):
        order = range(SCAN_BLOCKS) if forward else range(SCAN_BLOCKS - 1, -1, -1)
        last = slice(7, 8) if forward else slice(0, 1)
        for q in order:
            rs = pl.ds(off + 8 * q, 8)
            a, b = local_scan(a_ref[rs, :], u_ref[rs, :], forward)
            h_ref[rs, :] = b + a * carry
            carry = a[last, :] * carry + b[last, :]
        return carry

    def body(k, carry):
        cf, cb = carry
        cf = span_scan(af_ref, uf_ref, hf_ref, pl.multiple_of(k * span, span), cf, True)
        cb = span_scan(ab_ref, ub_ref, hb_ref, pl.multiple_of((nit - 1 - k) * span, span), cb, False)
        return cf, cb

    return lax.fori_loop(0, nit, body, (h0f, h0b))


SCAN_BLOCKS = 8


def _shifted(pad_ref, x, offsets, before=0.0, after=0.0):
    n = x.shape[0]
    pad_ref[0:8, :] = jnp.broadcast_to(jnp.asarray(before, F32), (8, x.shape[1]))
    pad_ref[8:8 + n, :] = x
    pad_ref[8 + n:16 + n, :] = jnp.broadcast_to(jnp.asarray(after, F32), (8, x.shape[1]))
    return [pad_ref[8 + o:8 + o + n, :] for o in offsets]


def _conv(xa, cw, cb, pad_ref):
    xm1, xp1, xp2 = _shifted(pad_ref, xa, (-1, 1, 2))
    return xm1 * cw[0:1, :] + xa * cw[1:2, :] + xp1 * cw[2:3, :] + xp2 * cw[3:4, :] + cb


def _gates(xc, wa, wx, ba, bx, nsp):
    xb = xc.astype(BF16)
    r = _sigmoid(_dot(xb, wa) + ba)
    i = _sigmoid(_dot(xb, wx) + bx)
    log_a = r * nsp
    a = jnp.exp(log_a)
    g2 = jnp.tanh(log_a) * (-1.0 - a * a)
    rg = lax.rsqrt(jnp.maximum(g2, 1e-30))
    return r, i, a, g2 * rg, rg


def _lru_param_specs():
    h4 = pl.BlockSpec((2, 1, HD, HD), lambda h: (0, h, 0, 0))
    v2 = pl.BlockSpec((2, HD), lambda h: (0, h))
    b16 = pl.BlockSpec((2 * HEADS, HD), lambda h: (0, 0))
    return dict(
        xa=pl.BlockSpec((L, HD), lambda h: (0, h)), xac=pl.BlockSpec((LC, HD), lambda h: (0, h)),
        cw=pl.BlockSpec((CONV_W, HD), lambda h: (0, h)), cb=pl.BlockSpec((1, HD), lambda h: (0, h)), h4=h4, v2=v2,
        b16=b16)


def _bias_row(ref, d):
    mask = _rows((2 * HEADS, HD)) == d * HEADS + pl.program_id(0)
    return jnp.sum(jnp.where(mask, ref[...], 0.0), axis=0, keepdims=True), mask


def _lru_forward(zx, zc, cw, cb, wa, wx, ba, bx, lam, gather, gather_modes):
    ng_ = len(gather)

    def body(xa_ref, xac_ref, cw_ref, cb_ref, wa_ref, wx_ref, ba_ref, bx_ref, lam_ref, *rest):
        yl_ref = rest[ng_]
        af, uf, hf, ab, ub, hb, pad_s = rest[2 * ng_ + 1:2 * ng_ + 8]
        start, pass_on, finish = _gather2_ops(rest[:ng_], rest[ng_ + 1:2 * ng_ + 1], gather_modes,
                                              *rest[2 * ng_ + 8:], barrier=True)
        pl.when(pl.program_id(0) == 0)(start)
        pl.when(pl.program_id(0) == HEADS // 2)(pass_on)
        pl.when(pl.program_id(0) == HEADS - 1)(finish)
        cwv, cbv = cw_ref[...], cb_ref[...]
        nsp = (-LRU_C) * _softplus(-lam_ref[...])

        def forward(xa, t_len, h0f, h0b):
            xc = _conv(xa, cwv, cbv, pad_s)
            for d, (a_ref, u_ref) in enumerate(((af, uf), (ab, ub))):
                _, i, a, gamma, _ = _gates(xc, wa_ref[d, 0].astype(BF16), wx_ref[d, 0].astype(BF16),
                                           _bias_row(ba_ref, d)[0], _bias_row(bx_ref, d)[0], nsp[d:d + 1, :])
                a_ref[0:t_len, :] = a
                u_ref[0:t_len, :] = gamma * (i * xc)
            return _scan_pair(af, uf, hf, h0f, ab, ub, hb, h0b, t_len)

        z = jnp.zeros((1, HD), F32)
        h0f, h0b = forward(xac_ref[...], LC, z, z)
        forward(xa_ref[...], L, h0f, h0b)
        yl_ref[...] = hf[...] + hb[...]

    s = _lru_param_specs()
    return _call(
        body, name="lru_forward", grid=(HEADS,),
        out_shape=[jax.ShapeDtypeStruct((L, D), F32)] + _gather2_shapes(gather, gather_modes),
        in_specs=[s["xa"], s["xac"], s["cw"], s["cb"], s["h4"], s["h4"], s["b16"], s["b16"], s["v2"]] + [HBM] * ng_,
        out_specs=[pl.BlockSpec((L, HD), lambda h: (0, h))] + [HBM] * ng_,
        scratch_shapes=[pltpu.VMEM((L, HD), F32)] * 6 + [pltpu.VMEM((L + 16, HD), F32)] + _gather2_sems(ng_),
        compiler_params=pltpu.CompilerParams(dimension_semantics=("arbitrary",), vmem_limit_bytes=VMEM_LIMIT,
                                             has_side_effects=True, collective_id=5),
    )(zx, zc, cw, cb, wa, wx, ba, bx, lam, *[pltpu.with_memory_space_constraint(a, pltpu.HBM) for a in gather])


def _lru_backward(zx, zc, dyl, dz, cw, cb, wa, wx, ba, bx, lam, chip_sums, first_chips=None):
    nr = len(chip_sums)

    def body(xa_ref, xac_ref, dyl_ref, dz_in, cw_ref, cb_ref, wa_ref, wx_ref, ba_ref, bx_ref, lam_ref, *rest):
        (dxa_ref, dxac_ref, dwa_ref, dwx_ref, dba_ref, dbx_ref, dlam_ref, dcw_ref,
         dcb_ref) = rest[nr:nr + 9]
        main_s, ctx_s, pad_s = rest[3 * nr + 9:3 * nr + 12]
        if nr:
            start, forward, finish = _chips_ops(rest[:nr], rest[nr + 9:2 * nr + 9], rest[2 * nr + 9:3 * nr + 9],
                                                *rest[3 * nr + 12:], first_chips=first_chips, barrier=True)
            pl.when(pl.program_id(0) == 0)(start)
            pl.when(pl.program_id(0) == HEADS // 2)(forward)
            pl.when(pl.program_id(0) == HEADS - 1)(finish)
        del dz_in

        @pl.when(pl.program_id(0) == 0)
        def _():
            dba_ref[...] = jnp.zeros_like(dba_ref)
            dbx_ref[...] = jnp.zeros_like(dbx_ref)

        cwv, cbv = cw_ref[...], cb_ref[...]
        lamv = lam_ref[...]
        sp = _softplus(-lamv)
        nsp = (-LRU_C) * sp
        z = jnp.zeros((1, HD), F32)

        def wmat(ref, d):
            return ref[d, 0].astype(BF16)

        def workspace(s):
            return dict(a=(s.at[0], s.at[1]), u=(s.at[2], s.at[3]), h=(s.at[4], s.at[5]), rho=(s.at[6], s.at[7]),
                        saved=(tuple(s.at[8 + k] for k in range(4)), tuple(s.at[12 + k] for k in range(4))),
                        xc=s.at[16])

        def forward(ws, xa, t_len, h0f, h0b):
            xc = _conv(xa, cwv, cbv, pad_s)
            ws["xc"][...] = xc
            for d in (0, 1):
                vals = _gates(xc, wmat(wa_ref, d), wmat(wx_ref, d), _bias_row(ba_ref, d)[0],
                              _bias_row(bx_ref, d)[0], nsp[d:d + 1, :])
                r, i, a, gamma, rg = vals
                ws["a"][d][...] = a
                ws["u"][d][...] = gamma * (i * xc)
                for ref, val in zip(ws["saved"][d], (r, i, gamma, rg)):
                    ref[...] = val
            return _scan_pair(ws["a"][0], ws["u"][0], ws["h"][0], h0f, ws["a"][1], ws["u"][1], ws["h"][1], h0b,
                              t_len)

        def backward(ws, xa, t_len, h0f, h0b, dhf, dhb, first):
            xc = ws["xc"][...]
            (af, ab), (uf, ub), (hf, hb), (rf, rb) = ws["a"], ws["u"], ws["h"], ws["rho"]
            uf[...] = ab[...] * dhb
            ub[...] = af[...] * dhf
            rho_b_last, rho_f_first = _scan_pair(ab, uf, rb, z, af, ub, rf, z, t_len)
            dxc = jnp.zeros((t_len, HD), F32)
            dsp = []
            for d in (0, 1):
                r, i, gamma, rg = (ref[...] for ref in ws["saved"][d])
                a = ws["a"][d][...]
                if d == 0:
                    lam_t = dhf + _shifted(pad_s, rf[...], (1,))[0]
                    h_prev = _shifted(pad_s, hf[...], (-1,), before=h0f)[0]
                else:
                    lam_t = dhb + _shifted(pad_s, rb[...], (-1,))[0]
                    h_prev = _shifted(pad_s, hb[...], (1,), after=h0b)[0]
                da = lam_t * h_prev
                lx = lam_t * xc
                d_i = lx * gamma
                d_gamma = lx * i
                dxc = dxc + lam_t * (gamma * i)
                d_log_a = a * (da - d_gamma * (a * rg))
                dsp.append(jnp.sum(d_log_a * r, axis=0, keepdims=True) * (-LRU_C))
                d_pre_r = d_log_a * nsp[d:d + 1, :] * (r * (1.0 - r))
                d_pre_i = d_i * (i * (1.0 - i))
                prb, pib, xb = d_pre_r.astype(BF16), d_pre_i.astype(BF16), xc.astype(BF16)
                dxc = dxc + _dot_nt(prb, wmat(wa_ref, d)) + _dot_nt(pib, wmat(wx_ref, d))
                g_wa, g_wx = _dot_tn(xb, prb), _dot_tn(xb, pib)
                g_ba = jnp.sum(d_pre_r, axis=0, keepdims=True)
                g_bx = jnp.sum(d_pre_i, axis=0, keepdims=True)
                mask = _bias_row(ba_ref, d)[1]
                dba_ref[...] += jnp.where(mask, g_ba, 0.0)
                dbx_ref[...] += jnp.where(mask, g_bx, 0.0)
                if first:
                    dwa_ref[d, 0] = g_wa
                    dwx_ref[d, 0] = g_wx
                else:
                    dwa_ref[d, 0] += g_wa
                    dwx_ref[d, 0] += g_wx
            g_lam = jnp.concatenate(dsp, axis=0) * (-_sigmoid(-lamv))
            dm1, dp1, dm2 = _shifted(pad_s, dxc, (-1, 1, -2))
            dxa = dp1 * cwv[0:1, :] + dxc * cwv[1:2, :] + dm1 * cwv[2:3, :] + dm2 * cwv[3:4, :]
            xm1, xp1, xp2 = _shifted(pad_s, xa, (-1, 1, 2))
            g_cw = jnp.concatenate([jnp.sum(dxc * v, axis=0, keepdims=True) for v in (xm1, xa, xp1, xp2)], axis=0)
            g_cb = jnp.sum(dxc, axis=0, keepdims=True)
            if first:
                dlam_ref[...] = g_lam
                dcw_ref[...] = g_cw
                dcb_ref[...] = g_cb
            else:
                dlam_ref[...] += g_lam
                dcw_ref[...] += g_cw
                dcb_ref[...] += g_cb
            return dxa, rho_f_first, rho_b_last

        ws_x, ws_c = workspace(main_s), workspace(ctx_s)
        h0f, h0b = forward(ws_c, xac_ref[...], LC, z, z)
        forward(ws_x, xa_ref[...], L, h0f, h0b)
        dh = dyl_ref[...]
        dxa, dh0f, dh0b = backward(ws_x, xa_ref[...], L, h0f, h0b, dh, dh, True)
        dxa_ref[...] = dxa.astype(BF16)
        rc = _rows((LC, HD))
        dxac, _, _ = backward(ws_c, xac_ref[...], LC, z, z, jnp.where(rc == LC - 1, dh0f, 0.0),
                              jnp.where(rc == 0, dh0b, 0.0), False)
        dxac_ref[...] = dxac.astype(BF16)

    s = _lru_param_specs()
    col = lambda r: pl.BlockSpec((r, HD), lambda h: (0, h))
    return _call(
        body, name="lru_backward", grid=(HEADS,),
        out_shape=[jax.ShapeDtypeStruct((L, D_IN), BF16), jax.ShapeDtypeStruct((LC, D), BF16),
                   jax.ShapeDtypeStruct((2, HEADS, HD, HD), F32), jax.ShapeDtypeStruct((2, HEADS, HD, HD), F32),
                   jax.ShapeDtypeStruct((2 * HEADS, HD), F32), jax.ShapeDtypeStruct((2 * HEADS, HD), F32),
                   jax.ShapeDtypeStruct((2, D), F32), jax.ShapeDtypeStruct((CONV_W, D), F32),
                   jax.ShapeDtypeStruct((1, D), F32)] + [jax.ShapeDtypeStruct((4,) + a.shape[1:], a.dtype)
                                                          for a in chip_sums] + _chips_stage_shapes(chip_sums),
        in_specs=[s["xa"], s["xac"], col(L), pl.BlockSpec(memory_space=pl.ANY), s["cw"], s["cb"], s["h4"], s["h4"],
                  s["b16"], s["b16"], s["v2"]] + [HBM] * nr,
        out_specs=[col(L), col(LC), s["h4"], s["h4"], s["b16"], s["b16"], s["v2"], col(CONV_W), col(1)]
        + [HBM] * (2 * nr),
        scratch_shapes=[pltpu.VMEM((17, L, HD), F32), pltpu.VMEM((17, LC, HD), F32), pltpu.VMEM((L + 16, HD), F32)]
        + (_chips_sems(nr) if nr else []),
        input_output_aliases={3: 0},
        compiler_params=pltpu.CompilerParams(dimension_semantics=("arbitrary",), vmem_limit_bytes=VMEM_LIMIT,
                                             has_side_effects=True, collective_id=6 if nr else None),
    )(zx, zc, dyl, dz, cw, cb, wa, wx, ba, bx, lam, *[pltpu.with_memory_space_constraint(a, pltpu.HBM)
                                                       for a in chip_sums])


def _mixer_loss(x, tgt, zx, yl, gx, fg, lng, lnb, ws, wst, bst, wout, tm):
    ncht = tm // CHUNK

    def body(x_ref, t_ref, ga_ref, u_ref, v_ref, gb_ref, yl_ref, gx_ref, fg_ref, lng_ref, lnb_ref, ws_ref, wst_ref,
             bst_ref, wout_ref,
             dz_ref, dyl_ref, dxn_ref, y_s, do_ref, dws_ref, dbst_ref, vec_ref,
             vn_s, mix_s, dm_s, dvn_s):
        step = pl.program_id(0)

        @pl.when(step == 0)
        def _():
            dws_ref[...] = jnp.zeros_like(dws_ref)
            dbst_ref[...] = jnp.zeros_like(dbst_ref)
            vec_ref[...] = jnp.zeros_like(vec_ref)

        u, v = u_ref[...], v_ref[...]
        ug, dug_du = _gelu_and_grad(u)
        vg, dvg_dv = _gelu_and_grad(v)
        mu = jnp.mean(vg, axis=-1, keepdims=True)
        vc = vg - mu
        rstd = lax.rsqrt(jnp.mean(vc * vc, axis=-1, keepdims=True) + LN_EPS)
        vhat = vc * rstd
        lngv = lng_ref[...]
        vn_s[...] = (vhat * lngv + lnb_ref[...]).astype(BF16)
        for ch in range(ncht):
            rs = slice(ch * CHUNK, (ch + 1) * CHUNK)
            for g in range(HEADS):
                cs = slice(g * HD, (g + 1) * HD)
                mix_s[rs, cs] = _dot(ws_ref[g], vn_s[rs, cs]) + bst_ref[:, g:g + 1]
        mixed = mix_s[...]
        ga, gb, yl = ga_ref[...], gb_ref[...], yl_ref[...]
        sga, dsga = _silu_and_grad(ga)
        sgb, dsgb = _silu_and_grad(gb)
        ys = ug * mixed
        y_s[:, 0:D] = (yl * sga).astype(BF16)
        y_s[:, D:D_MIX] = (ys * sgb).astype(BF16)
        o = _dot(y_s[...], wout_ref[...])
        gxv, fgv = gx_ref[...], fg_ref[...]
        xn = x_ref[...] + gxv * o
        rs2 = lax.rsqrt(jnp.mean(xn * xn, axis=-1, keepdims=True) + NORM_EPS)
        xh = xn * rs2
        diff = xh * fgv - t_ref[...]
        vec_ref[R_LOSS:R_LOSS + 1, :] += jnp.full((1, D), jnp.sum(diff * diff) * (0.5 / D), F32)
        dout = diff * (1.0 / D)
        w = dout * fgv
        dxn = rs2 * (w - xh * jnp.mean(w * xh, axis=-1, keepdims=True))
        dxn_ref[...] = dxn
        vec_ref[0:1, :] += jnp.sum(dxn * o, axis=0, keepdims=True)
        vec_ref[1:2, :] += jnp.sum(dout * xh, axis=0, keepdims=True)
        dob = (dxn * gxv).astype(BF16)
        do_ref[...] = dob
        dy = _dot_nt(dob, wout_ref[...])
        dya, dyb = dy[:, 0:D], dy[:, D:D_MIX]
        dyl_ref[...] = dya * sga
        dys = dyb * sgb
        dz_ref[:, 0:D] = jnp.zeros((tm, D), BF16)
        dz_ref[:, D:2 * D] = (dya * yl * dsga).astype(BF16)
        dz_ref[:, 2 * D:3 * D] = (dys * mixed * dug_du).astype(BF16)
        dz_ref[:, 4 * D:5 * D] = (dyb * ys * dsgb).astype(BF16)
        dm = dys * ug
        dm_s[...] = dm.astype(BF16)
        for g in range(HEADS):
            cs = slice(g * HD, (g + 1) * HD)
            dbst_ref[:, g:g + 1] += sum(jnp.sum(dm[ch * CHUNK:(ch + 1) * CHUNK, cs], axis=1, keepdims=True)
                                        for ch in range(ncht))
            for ch in range(ncht):
                rs = slice(ch * CHUNK, (ch + 1) * CHUNK)
                dws_ref[g] += _dot_nt(dm_s[rs, cs], vn_s[rs, cs])
                dvn_s[rs, cs] = _dot(wst_ref[g], dm_s[rs, cs])
        dvn = dvn_s[...]
        vec_ref[2:3, :] += jnp.sum(dvn * vhat, axis=0, keepdims=True)
        vec_ref[3:4, :] += jnp.sum(dvn, axis=0, keepdims=True)
        dvh = dvn * lngv
        dvg = rstd * (dvh - jnp.mean(dvh, axis=-1, keepdims=True) - vhat * jnp.mean(dvh * vhat, axis=-1, keepdims=True))
        dz_ref[:, 3 * D:4 * D] = (dvg * dvg_dv).astype(BF16)

    tile = pl.BlockSpec((tm, D), lambda i: (i, 0))
    zcol = lambda n: pl.BlockSpec((tm, D), lambda i: (i, n))
    vec = pl.BlockSpec((1, D), lambda i: (0, 0))
    full = lambda *s: pl.BlockSpec(s, lambda i: (0,) * len(s))
    return _call(
        body, name="mixer_loss", grid=(L // tm,),
        out_shape=[jax.ShapeDtypeStruct((L, D_IN), BF16), jax.ShapeDtypeStruct((L, D), F32),
                   jax.ShapeDtypeStruct((L, D), F32), jax.ShapeDtypeStruct((L, D_MIX), BF16),
                   jax.ShapeDtypeStruct((L, D), BF16),
                   jax.ShapeDtypeStruct((HEADS, CHUNK, CHUNK), F32), jax.ShapeDtypeStruct((CHUNK, HEADS), F32),
                   jax.ShapeDtypeStruct((8, D), F32)],
        in_specs=[tile, tile, zcol(1), zcol(2), zcol(3), zcol(4), tile, pl.BlockSpec((1, D), lambda i: (0, 2)),
                  vec, vec, vec,
                  full(HEADS, CHUNK, CHUNK), full(HEADS, CHUNK, CHUNK), full(CHUNK, HEADS),
                  pl.BlockSpec((D_MIX, D), lambda i: (0, 0), pipeline_mode=pl.Buffered(1))],
        out_specs=[pl.BlockSpec((tm, D_IN), lambda i: (i, 0)), tile, tile,
                   pl.BlockSpec((tm, D_MIX), lambda i: (i, 0)), tile,
                   full(HEADS, CHUNK, CHUNK), full(CHUNK, HEADS), full(8, D)],
        scratch_shapes=[pltpu.VMEM((tm, D), BF16), pltpu.VMEM((tm, D), F32),
                        pltpu.VMEM((tm, D), BF16), pltpu.VMEM((tm, D), F32)],
        compiler_params=_params("arbitrary"),
    )(x, tgt, zx, zx, zx, zx, yl, gx, fg, lng, lnb, ws, wst, bst, wout)


def _grad_w(a, b, a2, b2, tk, name, bw, first, nblocks, split, barrier_id, chip_sums=()):
    nk = a.shape[0] // tk
    m = a.shape[1]
    with_ctx = a2 is not None
    if split == "cols":
        slots, r, w = nblocks, m, bw // 2
        piece = lambda q, pc: (slice(None), slice(pc * w, (pc + 1) * w))
    else:
        slots, r, w = 4, m // 8, bw
        piece = lambda q, pc: (slice((2 * q + pc) * r, (2 * q + pc + 1) * r), slice(None))

    nr = len(chip_sums)

    def body(*refs):
        a_ref, b_ref = refs[:2]
        a2_ref, b2_ref = refs[2:4] if with_ctx else (None, None)
        base = 4 if with_ctx else 2
        sums_ref = refs[base + nr]
        acc, mine_v, send_v, stage_v, send_sems, recv_sems = refs[base + 3 * nr + 1:base + 3 * nr + 7]
        n, k = pl.program_id(0), pl.program_id(1)
        x, y, c = lax.axis_index("x"), lax.axis_index("y"), lax.axis_index("c")

        def to_sibling(s):
            return pltpu.make_async_remote_copy(src_ref=send_v.at[s], dst_ref=stage_v.at[s], send_sem=send_sems.at[s],
                                                recv_sem=recv_sems.at[s], device_id=(x, y, 1 - c),
                                                device_id_type=MESH)

        if nr:
            c_start, c_forward, c_finish = _chips_ops(refs[base:base + nr], refs[base + nr + 1:base + 2 * nr + 1],
                                                      refs[base + 2 * nr + 1:base + 3 * nr + 1],
                                                      *refs[base + 3 * nr + 7:])

            @pl.when(jnp.logical_and(n == 0, k == 0))
            def _():
                _barrier([(x, y, 1 - c)] + [(x ^ (j >> 1), y ^ (j & 1), c) for j in (1, 2)])
                c_start()
        else:
            pl.when(jnp.logical_and(n == 0, k == 0))(_sibling_barrier)

        @pl.when(k == 0)
        def _():
            acc[...] = _dot_tn(a_ref[...], b_ref[...])

        if nk > 1:
            @pl.when(k > 0)
            def _():
                acc[...] += _dot_tn(a_ref[...], b_ref[...])

        if with_ctx:
            @pl.when(jnp.logical_and(k == nk - 1, n == 0))
            def _():
                acc[:, 0:b2_ref.shape[1]] += _dot_tn(a2_ref[...], b2_ref[...])

        if nr:
            pl.when(jnp.logical_and(k == nk - 1, n == nblocks - 1))(c_forward)

        def hand_over(s, q):
            for pc in (0, 1):
                @pl.when(c == pc)
                def _(pc=pc):
                    mine_v[s] = acc[piece(q, pc)]
                    send_v[s] = acc[piece(q, 1 - pc)].astype(BF16)
            to_sibling(s).start()

        for i in range(nblocks):
            @pl.when(jnp.logical_and(k == nk - 1, n == i))
            def _(i=i):
                if split == "cols":
                    hand_over(i, 0)
                else:
                    for q in range(4):
                        hand_over(q, q)

        @pl.when(jnp.logical_and(k == nk - 1, n == nblocks - 1))
        def _():
            for s in range(slots):
                to_sibling(s).wait_recv()
                sums_ref[s] = (mine_v[s] + stage_v[s].astype(F32)).astype(BF16)
            for s in range(slots):
                to_sibling(s).wait_send()
            if nr:
                c_finish()

    in_specs = [pl.BlockSpec((tk, m), lambda n, k: (k, 0)), pl.BlockSpec((tk, bw), lambda n, k: (k, n + first))]
    args = [a, b]
    if with_ctx:
        in_specs += [pl.BlockSpec(a2.shape, lambda n, k: (0, 0)), pl.BlockSpec(b2.shape, lambda n, k: (0, 0))]
        args += [a2, b2]
    in_specs += [HBM] * nr
    args += [pltpu.with_memory_space_constraint(s, pltpu.HBM) for s in chip_sums]
    return _call(
        body, name=name, grid=(nblocks, nk),
        out_shape=[jax.ShapeDtypeStruct((slots, r, w), BF16)]
        + [jax.ShapeDtypeStruct((4,) + s.shape[1:], s.dtype) for s in chip_sums] + _chips_stage_shapes(chip_sums),
        in_specs=in_specs, out_specs=[pl.BlockSpec((slots, r, w), lambda n, k: (0, 0, 0))] + [HBM] * (2 * nr),
        scratch_shapes=[pltpu.VMEM((m, bw), F32), pltpu.VMEM((slots, r, w), F32), pltpu.VMEM((slots, r, w), BF16),
                        pltpu.VMEM((slots, r, w), BF16), pltpu.SemaphoreType.DMA((slots,)),
                        pltpu.SemaphoreType.DMA((slots,))] + (_chips_sems(nr) if nr else []),
        compiler_params=pltpu.CompilerParams(dimension_semantics=("arbitrary", "arbitrary"),
                                             vmem_limit_bytes=VMEM_LIMIT, has_side_effects=True,
                                             collective_id=barrier_id),
    )(*args)


def _grad_rows(xr, dz, w, mod, ng, dres, ncols, tm, name, chip_sums=(), first_chips=None, dests=None):
    rows = xr.shape[0]
    steps = rows // tm
    with_dx = dres is not None
    nr = len(chip_sums)
    dests = [d for d in (dests or [None] * nr)]
    nd = sum(d is not None for d in dests)
    nin = 6 if with_dx else 5
    nout = 2 if with_dx else 1

    def body(*refs):
        if with_dx:
            x_ref, dz_ref, w_ref, sc_ref, ng_ref, dres_ref = refs[:nin]
            dx_ref, vec_ref = refs[nin + nr + nd:nin + nr + nd + nout]
        else:
            x_ref, dz_ref, w_ref, sc_ref, ng_ref = refs[:nin]
            (vec_ref,) = refs[nin + nr + nd:nin + nr + nd + nout]
        if nr:
            o0 = nin + nr + nd + nout
            start, forward, finish = _chips_ops(refs[nin:nin + nr], refs[o0:o0 + nr], refs[o0 + nr:o0 + 2 * nr],
                                                *refs[o0 + 2 * nr:], first_chips=first_chips, barrier=True)
            pl.when(pl.program_id(0) == 0)(start)
            pl.when(pl.program_id(0) == steps // 2)(forward)
            pl.when(pl.program_id(0) == steps - 1)(finish)

        @pl.when(pl.program_id(0) == 0)
        def _():
            vec_ref[...] = jnp.zeros_like(vec_ref)

        dhn = _dot_nt(dz_ref[...], w_ref[...])
        x = x_ref[...]
        rs = lax.rsqrt(jnp.mean(x * x, axis=-1, keepdims=True) + NORM_EPS)
        xh = x * rs
        ngv = ng_ref[...]
        y = xh * ngv
        vec_ref[0:1, :] += jnp.sum(dhn, axis=0, keepdims=True)
        vec_ref[1:2, :] += jnp.sum(dhn * y, axis=0, keepdims=True)
        dy = dhn * (1.0 + sc_ref[...])
        vec_ref[2:3, :] += jnp.sum(dy * xh, axis=0, keepdims=True)
        if with_dx:
            dxh = dy * ngv
            dx_ref[...] = dres_ref[...] + rs * (dxh - xh * jnp.mean(dxh * xh, axis=-1, keepdims=True))

    tile = pl.BlockSpec((tm, D), lambda i: (i, 0))
    vec = pl.BlockSpec((1, D), lambda i: (0, 0))
    in_specs = [tile, pl.BlockSpec((tm, ncols), lambda i: (i, 0)),
                pl.BlockSpec((D, ncols), lambda i: (0, 0), pipeline_mode=pl.Buffered(1)),
                pl.BlockSpec((1, D), lambda i: (0, 1)), vec]
    out_shape = [jax.ShapeDtypeStruct((8, D), F32)]
    out_specs = [pl.BlockSpec((8, D), lambda i: (0, 0))]
    args = [xr, dz, w, mod, ng]
    if with_dx:
        in_specs.append(tile)
        out_shape.insert(0, jax.ShapeDtypeStruct((rows, D), F32))
        out_specs.insert(0, tile)
        args.append(dres)
    aliases = {}
    for j, d in enumerate(dests):
        if d is not None:
            aliases[len(args) + nr + len(aliases)] = len(out_shape) + j
    in_specs += [HBM] * (nr + nd)
    out_specs += [HBM] * (2 * nr)
    out_shape += [jax.ShapeDtypeStruct((4,) + a.shape[1:], a.dtype) for a in chip_sums]
    out_shape += _chips_stage_shapes(chip_sums)
    args += [pltpu.with_memory_space_constraint(a, pltpu.HBM) for a in chip_sums]
    args += [pltpu.with_memory_space_constraint(d, pltpu.HBM) for d in dests if d is not None]
    return _call(body, name=name, grid=(steps,), out_shape=out_shape, in_specs=in_specs, out_specs=out_specs,
                 scratch_shapes=_chips_sems(nr) if nr else [], input_output_aliases=aliases,
                 compiler_params=pltpu.CompilerParams(dimension_semantics=("arbitrary",),
                                                      vmem_limit_bytes=VMEM_LIMIT, has_side_effects=bool(nr),
                                                      collective_id=7 if nr else None))(*args)


def _adamw(w, g, m, v):
    m = ADAM_B1 * m + (1.0 - ADAM_B1) * g
    v = ADAM_B2 * v + (1.0 - ADAM_B2) * (g * g)
    m_hat = m / (1.0 - ADAM_B1 ** ADAM_STEP)
    v_hat = v / (1.0 - ADAM_B2 ** ADAM_STEP)
    delta = -ADAM_LR * (m_hat / (jnp.sqrt(v_hat) + ADAM_EPS) + ADAM_WD * w)
    return delta, m, v


def _adamw_reduced(parts, w, m, v, tr, name):
    r, n = w.shape
    nparts = parts.shape[0]

    def body(p_ref, w_ref, m_ref, v_ref, g_ref, d_ref, mo_ref, vo_ref):
        g = p_ref[0].astype(F32)
        for i in range(1, nparts):
            g = g + p_ref[i].astype(F32)
        g_ref[...] = g
        d_ref[...], mo_ref[...], vo_ref[...] = _adamw(w_ref[...], g, m_ref[...], v_ref[...])

    tile = pl.BlockSpec((tr, n), lambda i: (i, 0))
    sds = jax.ShapeDtypeStruct((r, n), F32)
    return _call(
        body, name=name, grid=(r // tr,), out_shape=[sds] * 4,
        in_specs=[pl.BlockSpec((nparts, tr, n), lambda i: (0, i, 0)), tile, tile, tile], out_specs=[tile] * 4,
        compiler_params=_params("arbitrary"),
    )(parts, w, m, v)


R_GATE, R_FINAL_G, R_LN_G, R_LN_B, R_LOSS = 0, 1, 2, 3, 4
R_SH_X, R_SC_X, R_NG_X = 5, 6, 7
R_SH_C, R_SC_C, R_NG_C = 8, 9, 10
R_LAM, R_CW, R_CB = 11, 13, 17
PACK_ROWS = 24
Q_BA, Q_BX, Q_SGU_B, PACK128_ROWS = 0, 16, 32, 40


def _reduce_small(vp_all, vq_all, mat_parts, ada_w, me):
    nloc = ada_w.shape[1]
    nm = len(mat_parts)

    def body(me_ref, vp_ref, vq_ref, *refs):
        mp_refs, w_ref = refs[:nm], refs[nm]
        red_ref, redq_ref = refs[nm + 1:nm + 3]
        mat_refs = refs[nm + 3:2 * nm + 3]
        dmod_ref, gab_ref, cpart_ref, dmc_s = refs[2 * nm + 3:]
        red, redq = vp_ref[0], vq_ref[0]
        for i in range(1, N_DEV):
            red = red + vp_ref[i]
            redq = redq + vq_ref[i]
        red_ref[...] = red
        redq_ref[...] = redq
        for mp_ref, mat_ref in zip(mp_refs, mat_refs):
            mat = mp_ref[0].astype(F32)
            for i in range(1, mp_ref.shape[0]):
                mat = mat + mp_ref[i].astype(F32)
            mat_ref[...] = mat
        for e in range(N_DEV):
            dmod_ref[e:e + 1, 0:D] = vp_ref[e, R_SH_X:R_SH_X + 1, :]
            dmod_ref[e:e + 1, D:2 * D] = vp_ref[e, R_SC_X:R_SC_X + 1, :]
            dmod_ref[e:e + 1, 2 * D:3 * D] = vp_ref[e, R_GATE:R_GATE + 1, :]
        dmod_ref[8:9, 0:D] = red[R_SH_C:R_SH_C + 1, :]
        dmod_ref[8:9, D:2 * D] = red[R_SC_C:R_SC_C + 1, :]
        dmod_ref[8:9, 2 * D:3 * D] = jnp.zeros((1, D), F32)
        dmod_ref[9:16, :] = jnp.zeros((7, 3 * D), F32)
        gab_ref[:, 0:D] = red[R_SH_X:R_SH_X + 1, :] + red[R_SH_C:R_SH_C + 1, :]
        gab_ref[:, D:2 * D] = red[R_SC_X:R_SC_X + 1, :] + red[R_SC_C:R_SC_C + 1, :]
        gab_ref[:, 2 * D:3 * D] = red[R_GATE:R_GATE + 1, :]
        dmc_s[...] = jnp.broadcast_to(dmod_ref[8:9, :], (8, 3 * D))
        off = pl.multiple_of(me_ref[0] * nloc, 128)
        cpart_ref[...] = _dot_nt(dmc_s[:, pl.ds(off, nloc)], w_ref[...])

    return _call(
        body, name="reduce_small",
        out_shape=[jax.ShapeDtypeStruct((PACK_ROWS, D), F32), jax.ShapeDtypeStruct((PACK128_ROWS, HD), F32)]
        + [jax.ShapeDtypeStruct(p.shape[1:], F32) for p in mat_parts]
        + [jax.ShapeDtypeStruct((16, 3 * D), F32), jax.ShapeDtypeStruct((1, 3 * D), F32),
           jax.ShapeDtypeStruct((8, D), F32)],
        in_specs=[pl.BlockSpec(memory_space=pltpu.SMEM)] + [VMEM] * (nm + 3), out_specs=[VMEM] * (nm + 5),
        scratch_shapes=[pltpu.VMEM((8, 3 * D), F32)], compiler_params=_params(),
    )(me, vp_all, vq_all, *mat_parts, ada_w)


def _adamw_ada(c_all, c_ctx, dmod, w, m, v, me):
    nloc = w.shape[1]

    def body(me_ref, c_ref, cc_ref, dm_ref, w_ref, m_ref, v_ref, g_ref, d_ref, mo_ref, vo_ref):
        off = pl.multiple_of(me_ref[0] * nloc, 128)
        dm = dm_ref[:, pl.ds(off, nloc)]
        sx, _ = _silu_and_grad(c_ref[...])
        sc, _ = _silu_and_grad(cc_ref[...])
        g = _dot_tn(sx, dm[0:8, :]) + _dot_tn(jnp.broadcast_to(sc, (8, D)), dm[8:16, :])
        g_ref[...] = g
        d_ref[...], mo_ref[...], vo_ref[...] = _adamw(w_ref[...], g, m_ref[...], v_ref[...])

    sds = jax.ShapeDtypeStruct(w.shape, F32)
    return _call(
        body, name="adamw_ada_w", out_shape=[sds] * 4,
        in_specs=[pl.BlockSpec(memory_space=pltpu.SMEM)] + [VMEM] * 6, out_specs=[VMEM] * 4,
        compiler_params=_params(),
    )(me, c_all, c_ctx, dmod, w, m, v)


_SMALL = ("c_ctx", "ada_b", "norm_g", "conv_w", "conv_b", "lru_wa", "lru_ba", "lru_wx", "lru_bx", "lru_lambda",
          "sgu_ln_g", "sgu_ln_b", "sgu_w", "sgu_b", "final_g")


def _adamw_small(red, redq, mats, cparts, gab, ws, ms, vs, me):
    n = len(_SMALL)

    def body(me_ref, red_ref, redq_ref, wa_ref, wx_ref, sw_ref, cp_ref, gab_ref, *refs):
        w_refs, m_refs, v_refs = refs[:n], refs[n:2 * n], refs[2 * n:3 * n]
        outs = refs[3 * n:]
        off = pl.multiple_of(me_ref[0] * HD, 128)

        def row(r, k=1):
            return red_ref[r:r + k, :]

        cc = w_refs[0][...]
        dcc = cp_ref[0, 0:1, :]
        for i in range(1, N_DEV):
            dcc = dcc + cp_ref[i, 0:1, :]
        grads = dict(
            c_ctx=dcc * _silu_and_grad(cc)[1], ada_b=gab_ref[...], norm_g=row(R_NG_X) + row(R_NG_C),
            conv_w=red_ref[R_CW:R_CW + CONV_W, pl.ds(off, HD)], conv_b=row(R_CB),
            lru_wa=wa_ref[...], lru_ba=redq_ref[Q_BA:Q_BA + 2 * HEADS, :], lru_wx=wx_ref[...],
            lru_bx=redq_ref[Q_BX:Q_BX + 2 * HEADS, :], lru_lambda=red_ref[R_LAM:R_LAM + 2, pl.ds(off, HD)],
            sgu_ln_g=row(R_LN_G), sgu_ln_b=row(R_LN_B), sgu_w=sw_ref[...],
            sgu_b=redq_ref[Q_SGU_B:Q_SGU_B + HEADS, :], final_g=row(R_FINAL_G))
        for j, name in enumerate(_SMALL):
            g = grads[name]
            outs[j][...] = g
            outs[n + j][...], outs[2 * n + j][...], outs[3 * n + j][...] = _adamw(w_refs[j][...], g, m_refs[j][...],
                                                                                 v_refs[j][...])

    sds = [jax.ShapeDtypeStruct(ws[k].shape, F32) for k in _SMALL]
    outs = _call(
        body, name="adamw_small", out_shape=sds * 4,
        in_specs=[pl.BlockSpec(memory_space=pltpu.SMEM)] + [VMEM] * (7 + 3 * n), out_specs=[VMEM] * (4 * n),
        compiler_params=_params(),
    )(me, red, redq, *mats, cparts, gab, *[ws[k] for k in _SMALL], *[ms[k] for k in _SMALL],
      *[vs[k] for k in _SMALL])
    return [dict(zip(_SMALL, outs[i * n:(i + 1) * n])) for i in range(4)]


def kernel(x, c, ctx, c_ctx, ada_w, ada_b, norm_g, w_in, conv_w, conv_b, lru_wa, lru_ba, lru_wx, lru_bx, lru_lambda, sgu_ln_g, sgu_ln_b, sgu_w, sgu_b, w_out, final_g, loss_target, m_c_ctx, m_ada_w, m_ada_b, m_norm_g, m_w_in, m_conv_w, m_conv_b, m_lru_wa, m_lru_ba, m_lru_wx, m_lru_bx, m_lru_lambda, m_sgu_ln_g, m_sgu_ln_b, m_sgu_w, m_sgu_b, m_w_out, m_final_g, v_c_ctx, v_ada_w, v_ada_b, v_norm_g, v_w_in, v_conv_w, v_conv_b, v_lru_wa, v_lru_ba, v_lru_wx, v_lru_bx, v_lru_lambda, v_sgu_ln_g, v_sgu_ln_b, v_sgu_w, v_sgu_b, v_w_out, v_final_g):
    args = dict(locals())
    me_s = 4 * lax.axis_index("x") + 2 * lax.axis_index("y") + lax.axis_index("c")
    me = me_s.astype(jnp.int32).reshape(1)
    xr, ctxr, tgt = x[0], ctx[0], loss_target[0]
    cc = c_ctx.reshape(1, D)
    nw = 2 * HEADS * HD
    view = dict(c_ctx=(1, D), ada_b=(1, 3 * D), norm_g=(1, D), conv_w=(CONV_W, HD), conv_b=(1, D), lru_wa=(nw, HD),
                lru_ba=(2 * HEADS, HD), lru_wx=(nw, HD), lru_bx=(2 * HEADS, HD), lru_lambda=(2, HD), sgu_ln_g=(1, D),
                sgu_ln_b=(1, D), sgu_w=(HEADS * CHUNK, CHUNK), sgu_b=(HEADS, CHUNK), final_g=(1, D))

    zx, hn, w_full, w_out_b, modx, modc, c_all, cw_full, lam_full = _front_project(
        xr, c, cc, ada_w[0], ada_b, norm_g, w_in[0], w_out[0], conv_w[0], lru_lambda[0], me)
    zc, hnc = _project(ctxr, modc, norm_g, w_full, D, LC, "project_ctx")
    ba, bx = lru_ba.reshape(view["lru_ba"]), lru_bx.reshape(view["lru_bx"])
    yl, wout_all = _lru_forward(zx, zc, cw_full, conv_b, lru_wa[0], lru_wx[0], ba, bx, lam_full, [w_out_b], ["ag"])
    wout_full = wout_all.reshape(D_MIX, D)
    ws_b = sgu_w[0].astype(BF16)
    dz, dyl, dxn, ycat, dob, dws, dbst, mvec = _mixer_loss(
        xr, tgt, zx, yl, modx, final_g.reshape(1, D), sgu_ln_g, sgu_ln_b, ws_b, jnp.swapaxes(ws_b, 1, 2),
        sgu_b[0].T, wout_full, 256)

    (wout_sums,) = _grad_w(ycat, dob, None, None, L, "grad_w_out", D, 0, 1, "rows", 1)
    (rest_sums,) = _grad_w(hn, dz, None, None, L, "grad_w_in_rest", 2 * W_IN_SHARD, 1, 3, "cols", 2)
    dz, dxac, dwa, dwx, dba, dbx, dlam, dcw, dcb, win_parts, wout_parts, _, _ = _lru_backward(
        zx, zc, dyl, dz, cw_full, conv_b, lru_wa[0], lru_wx[0], ba, bx, lam_full, [rest_sums, wout_sums],
        first_chips=[1, 0])
    mats = [dwa.reshape(N_DEV, nw // N_DEV, HD), dwx.reshape(N_DEV, nw // N_DEV, HD), dws]
    mat_sums = _reduce2_local(mats, ["a2a"] * 3, me, "reduce_mat", 4, out_dtype=BF16)
    first_sums, *mat_parts = _grad_w(hn, dz, hnc, dxac, L, "grad_w_in_first", 2 * W_IN_SHARD, 0, 1, "cols", 3,
                                     chip_sums=mat_sums)[:4]
    gx, xvec, win_parts = _grad_rows(
        xr, dz, w_full, modx, norm_g, dxn, D_IN, 256, "grad_rows_x", chip_sums=[first_sums], first_chips=[0],
        dests=[win_parts])[:3]
    (cvec,) = _grad_rows(ctxr, dxac, w_full, modc, norm_g, None, D, LC, "grad_rows_ctx")
    pack = jnp.concatenate([mvec[0:5], xvec[0:3], cvec[0:3], dlam, dcw, dcb,
                            jnp.zeros((PACK_ROWS - R_CB - 1, D), F32)], axis=0)
    pack128 = jnp.concatenate([dba, dbx, dbst.T], axis=0)
    vp_all, vq_all = _gather2([pack, pack128], ["ag", "ag"], "gather_pack", 8)
    red, redq, *rest = _reduce_small(vp_all, vq_all, mat_parts, ada_w[0], me)
    mat_pieces, (dmod, gab, cpart) = rest[:3], rest[3:]
    *mats_all, cparts = _gather2([*mat_pieces, cpart], ["ag"] * 4, "gather_small", 9)

    g_w_in, d_w_in, nm_w_in, nv_w_in = _adamw_reduced(win_parts, w_in[0], m_w_in[0], v_w_in[0], 256, "adamw_w_in")
    g_w_out, d_w_out, nm_w_out, nv_w_out = _adamw_reduced(wout_parts, w_out[0], m_w_out[0], v_w_out[0], 128,
                                                          "adamw_w_out")
    g_ada, d_ada, nm_ada, nv_ada = _adamw_ada(c_all, cc, dmod, ada_w[0], m_ada_w[0], v_ada_w[0], me)
    ws = {k: args[k].reshape(view[k]) for k in _SMALL}
    ms = {k: args["m_" + k].reshape(view[k]) for k in _SMALL}
    vs = {k: args["v_" + k].reshape(view[k]) for k in _SMALL}
    small = _adamw_small(red, redq, [m.reshape(-1, HD) for m in mats_all], cparts, gab, ws, ms, vs, me)
    big = dict(w_in=(g_w_in, d_w_in, nm_w_in, nv_w_in), w_out=(g_w_out, d_w_out, nm_w_out, nv_w_out),
               ada_w=(g_ada, d_ada, nm_ada, nv_ada))

    loss = red[R_LOSS, 0]
    names = ("c_ctx", "ada_w", "ada_b", "norm_g", "w_in", "conv_w", "conv_b", "lru_wa", "lru_ba", "lru_wx", "lru_bx",
             "lru_lambda", "sgu_ln_g", "sgu_ln_b", "sgu_w", "sgu_b", "w_out", "final_g")
    outs = [loss, gx.reshape(x.shape)]
    for kind in range(4):
        for k in names:
            val = big[k][kind] if k in big else small[kind][k]
            outs.append(val.reshape(args[k].shape))
    return tuple(outs)
```

```python
import jax
import jax.numpy as jnp
from jax import lax
from jax.experimental import pallas as pl
from jax.experimental.pallas import tpu as pltpu

F32 = jnp.float32
BF16 = jnp.bfloat16

N_DEV = 8
D = 1024
L = 2048
LC = 256
HEADS = 8
HD = 128
CHUNK = 128
D_IN = 5 * D
W_IN_SHARD = D_IN // N_DEV
ROWS = 256
D_MIX = 2 * D
CONV_W = 4
LRU_C = 8.0
NORM_EPS = 1e-6
LN_EPS = 1e-5
ADAM_LR, ADAM_B1, ADAM_B2, ADAM_EPS, ADAM_WD, ADAM_STEP = 0.001, 0.9, 0.999, 1e-08, 0.01, 10

VMEM_LIMIT = 56 * 1024 * 1024

HBM = pl.BlockSpec(memory_space=pltpu.HBM)
VMEM = pl.BlockSpec(memory_space=pltpu.VMEM)
MESH = pl.DeviceIdType.MESH


def _call(body, **kw):
    return pl.pallas_call(body, **kw)


def _params(*sem):
    return pltpu.CompilerParams(dimension_semantics=sem, vmem_limit_bytes=VMEM_LIMIT)


def _sigmoid(x):
    return 0.5 * jnp.tanh(0.5 * x) + 0.5


def _silu_and_grad(x):
    s = _sigmoid(x)
    return x * s, s * (1.0 + x * (1.0 - s))


_G0 = 0.7978845608028654
_G1 = 0.044715


def _gelu_and_grad(x):
    x2 = x * x
    t = jnp.tanh(_G0 * (x + _G1 * x * x2))
    cdf = 0.5 * (1.0 + t)
    return x * cdf, cdf + 0.5 * x * (1.0 - t * t) * (_G0 * (1.0 + 3.0 * _G1 * x2))


def _softplus(z):
    t = jnp.exp(-jnp.abs(z))
    u = 1.0 + t
    log1p = jnp.where(u == 1.0, t, jnp.log(u) * t / jnp.where(u == 1.0, 1.0, u - 1.0))
    return jnp.maximum(z, 0.0) + log1p


def _dot(a, b):
    return jnp.dot(a, b, preferred_element_type=F32)


def _dot_nt(a, b):
    return lax.dot_general(a, b, (((1,), (1,)), ((), ())), preferred_element_type=F32)


def _dot_tn(a, b):
    return lax.dot_general(a, b, (((0,), (0,)), ((), ())), preferred_element_type=F32)


def _rows(shape):
    return lax.broadcasted_iota(jnp.int32, shape, 0)


def _gather2(arrays, modes, name, barrier_id):
    n = len(arrays)

    def body(*refs):
        start, forward, finish = _gather2_ops(refs[:n], refs[n:2 * n], modes, *refs[2 * n:], barrier=True)
        start()
        forward()
        finish()

    return _call(
        body, name=name, out_shape=_gather2_shapes(arrays, modes), in_specs=[HBM] * n, out_specs=[HBM] * n,
        scratch_shapes=_gather2_sems(n),
        compiler_params=pltpu.CompilerParams(has_side_effects=True, collective_id=barrier_id),
    )(*[pltpu.with_memory_space_constraint(a, pltpu.HBM) for a in arrays])


def _gather2_shapes(arrays, modes):
    return [jax.ShapeDtypeStruct((N_DEV,) + a.shape if m == "ag" else (a.shape[0], N_DEV * a.shape[1]), a.dtype)
            for a, m in zip(arrays, modes)]


def _gather2_sems(n):
    return [pltpu.SemaphoreType.DMA((n, N_DEV - 1)), pltpu.SemaphoreType.DMA((n, N_DEV - 1)),
            pltpu.SemaphoreType.DMA((n,))]


def _barrier(peers):
    sem = pltpu.get_barrier_semaphore()
    for peer in peers:
        pl.semaphore_signal(sem, inc=1, device_id=peer, device_id_type=MESH)
    pl.semaphore_wait(sem, len(peers))


def _gather2_ops(ins, outs, modes, send_sems, recv_sems, local_sems, barrier=False):
    n = len(ins)
    x, y, c = lax.axis_index("x"), lax.axis_index("y"), lax.axis_index("c")
    me, sibling = (x, y, c), (x, y, 1 - c)
    chips = [(x ^ (k >> 1), y ^ (k & 1)) for k in (1, 2, 3)]

    def slot(j, px, py, pc):
        dev = 4 * px + 2 * py + pc
        if modes[j] == "agc":
            w = ins[j].shape[1]
            return outs[j].at[:, pl.ds(pl.multiple_of(dev * w, 128), w)]
        return outs[j].at[dev]

    def copy(j, k, block, to, src=None):
        return pltpu.make_async_remote_copy(
            src_ref=slot(j, *block) if src is None else src, dst_ref=slot(j, *block),
            send_sem=send_sems.at[j, k], recv_sem=recv_sems.at[j, k], device_id=to, device_id_type=MESH)

    def own(j):
        return pltpu.make_async_copy(ins[j], slot(j, *me), local_sems.at[j])

    def first(j):
        return [copy(j, 0, me, sibling, src=ins[j])] + [copy(j, 1 + i, me, (*chip, c), src=ins[j])
                                                        for i, chip in enumerate(chips)]

    def passed(j, i):
        return copy(j, 4 + i, (*chips[i], c), sibling)

    def start():
        if barrier:
            _barrier([sibling] + [(*chip, c) for chip in chips])
        for j in range(n):
            own(j).start()
            for cp in first(j):
                cp.start()

    def forward():
        for i, chip in enumerate(chips):
            for j in range(n):
                copy(j, 1 + i, (*chip, c), me).wait_recv()
                passed(j, i).start()

    def finish():
        for j in range(n):
            copy(j, 0, sibling, me).wait_recv()
            for i, chip in enumerate(chips):
                copy(j, 4 + i, (*chip, 1 - c), me).wait_recv()
            for cp in first(j) + [passed(j, i) for i in range(3)]:
                cp.wait_send()
            own(j).wait()

    return start, forward, finish


def _sibling_barrier():
    sem = pltpu.get_barrier_semaphore()
    sibling = (lax.axis_index("x"), lax.axis_index("y"), 1 - lax.axis_index("c"))
    pl.semaphore_signal(sem, inc=1, device_id=sibling, device_id_type=MESH)
    pl.semaphore_wait(sem, 1)


def _reduce2_local(arrays, me, name, barrier_id, out_dtype):
    n = len(arrays)
    staged = [jax.ShapeDtypeStruct((4,) + a.shape[1:], a.dtype) for a in arrays]

    def to_sibling(*refs):
        ins, outs = refs[:n], refs[n:2 * n]
        send_sems, recv_sems = refs[2 * n:]
        x, y, c = lax.axis_index("x"), lax.axis_index("y"), lax.axis_index("c")
        _sibling_barrier()
        copies = []
        for j in range(n):
            for q in range(4):
                cp = pltpu.make_async_remote_copy(
                    src_ref=ins[j].at[2 * q + (1 - c)], dst_ref=outs[j].at[q], send_sem=send_sems.at[j, q],
                    recv_sem=recv_sems.at[j, q], device_id=(x, y, 1 - c), device_id_type=MESH)
                cp.start()
                copies.append(cp)
        for cp in copies:
            cp.wait()

    stage = _call(
        to_sibling, name=name + "_d2d", out_shape=staged, in_specs=[HBM] * n, out_specs=[HBM] * n,
        scratch_shapes=[pltpu.SemaphoreType.DMA((n, 4)), pltpu.SemaphoreType.DMA((n, 4))],
        compiler_params=pltpu.CompilerParams(has_side_effects=True, collective_id=barrier_id),
    )(*[pltpu.with_memory_space_constraint(a, pltpu.HBM) for a in arrays])

    def add(me_ref, *refs):
        del me_ref
        own, got, outs = refs[:n], refs[n:2 * n], refs[2 * n:]
        for j in range(n):
            outs[j][0] = (own[j][0].astype(F32) + got[j][0].astype(F32)).astype(out_dtype)

    own_specs = [pl.BlockSpec((1,) + a.shape[1:], lambda q, me_ref: (2 * q + me_ref[0] % 2, 0, 0)) for a in arrays]
    slot_specs = [pl.BlockSpec((1,) + a.shape[1:], lambda q, me_ref: (q, 0, 0)) for a in arrays]
    return _call(
        add, name=name + "_add", out_shape=[jax.ShapeDtypeStruct(s.shape, out_dtype) for s in staged],
        grid_spec=pltpu.PrefetchScalarGridSpec(num_scalar_prefetch=1, grid=(4,), in_specs=own_specs + slot_specs,
                                               out_specs=slot_specs),
        compiler_params=_params("arbitrary"),
    )(me, *arrays, *stage)


def _chips_sems(n):
    return [pltpu.SemaphoreType.DMA((n, 6)), pltpu.SemaphoreType.DMA((n, 6)), pltpu.SemaphoreType.DMA((n,))]


def _chips_stage_shapes(chip_sums):
    return [jax.ShapeDtypeStruct((2, a.shape[1] // 2, a.shape[2]), a.dtype) for a in chip_sums]


def _chips_ops(ins, outs, stages, send_sems, recv_sems, local_sems, first_chips=None, barrier=False):
    x, y, c = lax.axis_index("x"), lax.axis_index("y"), lax.axis_index("c")
    qm = 2 * x + y
    first_chips = first_chips or [0] * len(ins)

    def owns(j, chip):
        lo, cnt = first_chips[j], ins[j].shape[0]
        if lo == 0 and cnt == 4:
            return None
        return jnp.logical_and(chip >= lo, chip < lo + cnt)

    def guarded(cond, fn):
        if cond is None:
            fn()
        else:
            pl.when(cond)(fn)

    def slot(j, chip):
        return jnp.clip(chip - first_chips[j], 0, ins[j].shape[0] - 1)

    def half(j, i):
        h = ins[j].shape[1] // 2
        return pl.ds(i * h, h)

    def copy(j, sem, src, dst, k):
        return pltpu.make_async_remote_copy(
            src_ref=src, dst_ref=dst, send_sem=send_sems.at[j, sem], recv_sem=recv_sems.at[j, sem],
            device_id=(x ^ (k >> 1), y ^ (k & 1), c), device_id_type=MESH)

    def direct(j, k):
        return copy(j, k - 1, ins[j].at[slot(j, qm ^ k)], outs[j].at[qm], k)

    def first_hop(j, k):
        return copy(j, 1 + k, ins[j].at[slot(j, qm ^ 3), half(j, k - 1)], stages[j].at[k - 1], k)

    def second_hop(j, k):
        return copy(j, 3 + k, stages[j].at[2 - k], outs[j].at[qm ^ (3 - k), half(j, 2 - k)], k)

    def local(j):
        return pltpu.make_async_copy(ins[j].at[slot(j, qm)], outs[j].at[qm], local_sems.at[j])

    def start():
        if barrier:
            _barrier([(x ^ (k >> 1), y ^ (k & 1), c) for k in (1, 2)])
        for j in range(len(ins)):
            for k in (1, 2):
                guarded(owns(j, qm ^ 3), lambda j=j, k=k: first_hop(j, k).start())
        for j in range(len(ins)):
            for k in (1, 2):
                guarded(owns(j, qm ^ k), lambda j=j, k=k: direct(j, k).start())
            guarded(owns(j, qm), lambda j=j: local(j).start())

    def forward():
        for j in range(len(ins)):
            for k in (1, 2):
                def pass_on(j=j, k=k):
                    first_hop(j, 3 - k).wait_recv()
                    second_hop(j, k).start()
                guarded(owns(j, qm ^ k), pass_on)

    def finish():
        for j in range(len(ins)):
            for k in (1, 2):
                guarded(owns(j, qm ^ k), lambda j=j, k=k: direct(j, k).wait_send())
                guarded(owns(j, qm ^ k), lambda j=j, k=k: second_hop(j, k).wait_send())
                guarded(owns(j, qm ^ 3), lambda j=j, k=k: first_hop(j, k).wait_send())
                guarded(owns(j, qm), lambda j=j, k=k: direct(j, k).wait_recv())
                guarded(owns(j, qm), lambda j=j, k=k: second_hop(j, k).wait_recv())
            guarded(owns(j, qm), lambda j=j: local(j).wait())

    return start, forward, finish


ARRIVAL = (0, 1, 2, 4, 3, 5, 6, 7)


def _front_project(xr, c, c_ctx, ada_w, ada_b, ng, w_in, w_out, cw, lam, me):
    nloc = ada_w.shape[1]
    ws = W_IN_SHARD
    order = me[0] ^ jnp.asarray(ARRIVAL, jnp.int32)

    def body(ord_ref, x_ref, c_ref, cc_ref, aw_ref, ab_ref, ng_ref, win_ref, wout_ref, cw_ref, lam_ref,
             z_ref, hn_ref, wfull_ref, woutb_ref, modx_ref, modc_ref, call_ref, cwf_ref, lamf_ref,
             wv, call_s, part_s, parts_s, w_send, w_recv, hbm_sems, s_send, s_recv, g_send, g_recv, g_local):
        t = pl.program_id(0)
        x, y, cidx = lax.axis_index("x"), lax.axis_index("y"), lax.axis_index("c")
        me_i = ord_ref[0]
        sibling = (x, y, 1 - cidx)
        chips = [(x ^ (k >> 1), y ^ (k & 1)) for k in (1, 2, 3)]
        g_start, g_pass, g_finish = _gather2_ops([cw_ref, lam_ref], [cwf_ref, lamf_ref], ["agc", "agc"],
                                                 g_send, g_recv, g_local)

        def shard_copy(k, px, py, pc, to, half=None):
            slot = wv.at[4 * px + 2 * py + pc]
            if half is not None:
                slot = slot.at[pl.ds(half * (D // 2), D // 2), :]
            return pltpu.make_async_remote_copy(src_ref=slot, dst_ref=slot, send_sem=w_send.at[k],
                                                recv_sem=w_recv.at[k], device_id=to, device_id_type=MESH)

        def small_gather(src, my_slot, stage):
            copies = []
            for k in range(1, N_DEV):
                peer = (x ^ (k >> 2), y ^ ((k >> 1) & 1), cidx ^ (k & 1))
                cp = pltpu.make_async_remote_copy(src_ref=src, dst_ref=my_slot, send_sem=s_send.at[stage, k - 1],
                                                  recv_sem=s_recv.at[stage, k - 1], device_id=peer,
                                                  device_id_type=MESH)
                cp.start()
                copies.append(cp)
            pltpu.sync_copy(src, my_slot)
            return copies

        def finish_small(copies):
            for cp in copies:
                cp.wait()

        def to_neighbours(half):
            for i in (0, 1):
                shard_copy(1 + i, x, y, cidx, (*chips[i], cidx), half=half).start()

        @pl.when(t == 0)
        def _():
            _barrier([(x ^ (k >> 2), y ^ ((k >> 1) & 1), cidx ^ (k & 1)) for k in range(1, N_DEV)])
            g_start()
            wv[me_i] = win_ref[...].astype(BF16)
            woutb_ref[...] = wout_ref[...].astype(BF16)
            shard_copy(0, x, y, cidx, sibling).start()
            finish_small(small_gather(c_ref, call_s.at[pl.ds(me_i, 1), :], 0))
            to_neighbours(0)
            call_ref[...] = call_s[...]
            off = pl.multiple_of(me_i * nloc, 128)
            b = ab_ref[:, pl.ds(off, nloc)]
            w = aw_ref[...]
            sx, _ = _silu_and_grad(call_s[...])
            sc, _ = _silu_and_grad(jnp.broadcast_to(cc_ref[...], (8, D)))
            part_s[0:8, :] = _dot(sx, w) + b
            part_s[8:16, :] = _dot(sc, w) + b
            parts_sent = small_gather(part_s, parts_s.at[me_i], 1)
            to_neighbours(1)
            finish_small(parts_sent)
            mine = _rows((16, nloc)) == me_i
            for j in range(N_DEV):
                pj = parts_s[j]
                modx_ref[:, j * nloc:(j + 1) * nloc] = jnp.sum(jnp.where(mine, pj, 0.0), axis=0, keepdims=True)
                modc_ref[:, j * nloc:(j + 1) * nloc] = pj[8:9, :]
            shift, scale1, ngv = modx_ref[:, 0:D], 1.0 + modx_ref[:, D:2 * D], ng_ref[...]
            for r in range(L // ROWS):
                rsl = slice(r * ROWS, (r + 1) * ROWS)
                xv = x_ref[rsl, :]
                rs = lax.rsqrt(jnp.mean(xv * xv, axis=-1, keepdims=True) + NORM_EPS)
                hn_ref[rsl, :] = ((xv * rs * ngv) * scale1 + shift).astype(BF16)

        @pl.when(t == 1)
        def _():
            shard_copy(0, x, y, 1 - cidx, sibling).wait_recv()
            g_pass()

        for i in (0, 1):
            @pl.when(t == ARRIVAL.index((2, 4)[i]))
            def _(i=i):
                shard_copy(1 + i, *chips[i], cidx, sibling).wait_recv()
                shard_copy(4 + i, *chips[i], cidx, sibling).start()
                shard_copy((7, 3)[i], *chips[i], cidx, (*chips[1 - i], cidx), half=i).start()

        @pl.when(t == ARRIVAL.index(6))
        def _():
            shard_copy(3, *chips[2], cidx, sibling, half=1).wait_recv()
            shard_copy(7, *chips[2], cidx, sibling, half=0).wait_recv()
            shard_copy(6, *chips[2], cidx, sibling).start()

        for i in range(3):
            @pl.when(t == ARRIVAL.index((3, 5, 7)[i]))
            def _(i=i):
                shard_copy(4 + i, *chips[i], 1 - cidx, sibling).wait_recv()

        @pl.when(t == 2)
        def _():
            g_finish()

        dev = ord_ref[t]
        for r in range(L // (2 * ROWS)):
            rsl = slice(r * 2 * ROWS, (r + 1) * 2 * ROWS)
            z_ref[rsl, :] = _dot(hn_ref[rsl, :], wv[dev])
        col = pl.ds(pl.multiple_of(dev * ws, 128), ws)
        pltpu.make_async_copy(wv.at[dev], wfull_ref.at[:, col], hbm_sems.at[t]).start()

        @pl.when(t == N_DEV - 1)
        def _():
            for k in (0, 1, 2, 4, 5, 6):
                shard_copy(k, x, y, cidx, sibling).wait_send()
            for k in (3, 7):
                shard_copy(k, x, y, cidx, sibling, half=0).wait_send()
            for s in range(N_DEV):
                pltpu.make_async_copy(wv.at[0], wfull_ref.at[:, pl.ds(0, ws)], hbm_sems.at[s]).wait()

    const = lambda *shape: pl.BlockSpec(shape, lambda t, o: (0,) * len(shape))
    once = lambda *shape: pl.BlockSpec(shape, lambda t, o: (0,) * len(shape), pipeline_mode=pl.Buffered(1))
    return _call(
        body, name="front_project",
        out_shape=[jax.ShapeDtypeStruct((L, D_IN), F32), jax.ShapeDtypeStruct((L, D), BF16),
                   jax.ShapeDtypeStruct((D, D_IN), BF16), jax.ShapeDtypeStruct(w_out.shape, BF16),
                   jax.ShapeDtypeStruct((1, 3 * D), F32), jax.ShapeDtypeStruct((1, 3 * D), F32),
                   jax.ShapeDtypeStruct((N_DEV, D), F32), jax.ShapeDtypeStruct((CONV_W, D), F32),
                   jax.ShapeDtypeStruct((2, D), F32)],
        grid_spec=pltpu.PrefetchScalarGridSpec(
            num_scalar_prefetch=1, grid=(N_DEV,),
            in_specs=[once(L, D), const(1, D), const(1, D), once(D, nloc), const(1, 3 * D), const(1, D),
                      once(D, ws), once(*w_out.shape), HBM, HBM],
            out_specs=[pl.BlockSpec((L, ws), lambda t, o: (0, o[t])), const(L, D), HBM, const(*w_out.shape),
                       const(1, 3 * D), const(1, 3 * D), const(N_DEV, D), HBM, HBM],
            scratch_shapes=[pltpu.VMEM((N_DEV, D, ws), BF16), pltpu.VMEM((N_DEV, D), F32), pltpu.VMEM((16, nloc), F32),
                            pltpu.VMEM((N_DEV, 16, nloc), F32), pltpu.SemaphoreType.DMA((8,)),
                            pltpu.SemaphoreType.DMA((8,)), pltpu.SemaphoreType.DMA((N_DEV,)),
                            pltpu.SemaphoreType.DMA((2, N_DEV - 1)), pltpu.SemaphoreType.DMA((2, N_DEV - 1))]
            + _gather2_sems(2)),
        compiler_params=pltpu.CompilerParams(dimension_semantics=("arbitrary",), vmem_limit_bytes=VMEM_LIMIT,
                                             has_side_effects=True, collective_id=10),
    )(order, xr, c, c_ctx, ada_w, ada_b, ng, w_in, w_out, pltpu.with_memory_space_constraint(cw, pltpu.HBM),
      pltpu.with_memory_space_constraint(lam, pltpu.HBM))


def _project(xr, mod, ng, w, ncols, tm, name):
    rows = xr.shape[0]

    def body(x_ref, sh_ref, sc_ref, ng_ref, w_ref, z_ref, hn_ref):
        x = x_ref[...]
        rs = lax.rsqrt(jnp.mean(x * x, axis=-1, keepdims=True) + NORM_EPS)
        hn = (x * rs * ng_ref[...]) * (1.0 + sc_ref[...]) + sh_ref[...]
        hb = hn.astype(BF16)
        hn_ref[...] = hb
        for n in range(ncols // D):
            z_ref[:, n * D:(n + 1) * D] = _dot(hb, w_ref[:, n * D:(n + 1) * D])

    vec = pl.BlockSpec((1, D), lambda i: (0, 0))
    return _call(
        body, name=name, grid=(rows // tm,),
        out_shape=[jax.ShapeDtypeStruct((rows, ncols), F32), jax.ShapeDtypeStruct((rows, D), BF16)],
        in_specs=[pl.BlockSpec((tm, D), lambda i: (i, 0)), vec, pl.BlockSpec((1, D), lambda i: (0, 1)), vec,
                  pl.BlockSpec((D, ncols), lambda i: (0, 0), pipeline_mode=pl.Buffered(1))],
        out_specs=[pl.BlockSpec((tm, ncols), lambda i: (i, 0)), pl.BlockSpec((tm, D), lambda i: (i, 0))],
        compiler_params=_params("arbitrary"),
    )(xr, mod, mod, ng, w)


def _scan_pair(af_ref, uf_ref, hf_ref, h0f, ab_ref, ub_ref, hb_ref, h0b, t_len):
    span = 8 * SCAN_BLOCKS
    nit = t_len // span
    rows = _rows((8, HD))

    def local_scan(a, b, forward):
        for s in (1, 2, 4):
            sh = s if forward else 8 - s
            m = rows >= s if forward else rows < 8 - s
            b = a * jnp.where(m, pltpu.roll(b, sh, 0), 0.0) + b
            a = a * jnp.where(m, pltpu.roll(a, sh, 0), 1.0)
        return a, b

    def span_scan(a_ref, u_ref, h_ref, off, carry, forward):
        order = range(SCAN_BLOCKS) if forward else range(SCAN_BLOCKS - 1, -1, -1)
        last = slice(7, 8) if forward else slice(0, 1)
        for q in order:
            rs = pl.ds(off + 8 * q, 8)
            a, b = local_scan(a_ref[rs, :], u_ref[rs, :], forward)
            h_ref[rs, :] = b + a * carry
            carry = a[last, :] * carry + b[last, :]
        return carry

    def body(k, carry):
        cf, cb = carry
        cf = span_scan(af_ref, uf_ref, hf_ref, pl.multiple_of(k * span, span), cf, True)
        cb = span_scan(ab_ref, ub_ref, hb_ref, pl.multiple_of((nit - 1 - k) * span, span), cb, False)
        return cf, cb

    return lax.fori_loop(0, nit, body, (h0f, h0b))


SCAN_BLOCKS = 8


def _shifted(pad_ref, x, offsets, before=0.0, after=0.0):
    n = x.shape[0]
    pad_ref[0:8, :] = jnp.broadcast_to(jnp.asarray(before, F32), (8, x.shape[1]))
    pad_ref[8:8 + n, :] = x
    pad_ref[8 + n:16 + n, :] = jnp.broadcast_to(jnp.asarray(after, F32), (8, x.shape[1]))
    return [pad_ref[8 + o:8 + o + n, :] for o in offsets]


def _conv(xa, cw, cb, pad_ref):
    xm1, xp1, xp2 = _shifted(pad_ref, xa, (-1, 1, 2))
    return xm1 * cw[0:1, :] + xa * cw[1:2, :] + xp1 * cw[2:3, :] + xp2 * cw[3:4, :] + cb


def _gates(xc, wa, wx, ba, bx, nsp):
    xb = xc.astype(BF16)
    r = _sigmoid(_dot(xb, wa) + ba)
    i = _sigmoid(_dot(xb, wx) + bx)
    log_a = r * nsp
    a = jnp.exp(log_a)
    g2 = jnp.tanh(log_a) * (-1.0 - a * a)
    rg = lax.rsqrt(jnp.maximum(g2, 1e-30))
    return r, i, a, g2 * rg, rg


def _lru_param_specs():
    h4 = pl.BlockSpec((2, 1, HD, HD), lambda h: (0, h, 0, 0))
    v2 = pl.BlockSpec((2, HD), lambda h: (0, h))
    b16 = pl.BlockSpec((2 * HEADS, HD), lambda h: (0, 0))
    return dict(
        xa=pl.BlockSpec((L, HD), lambda h: (0, h)), xac=pl.BlockSpec((LC, HD), lambda h: (0, h)),
        cw=pl.BlockSpec((CONV_W, HD), lambda h: (0, h)), cb=pl.BlockSpec((1, HD), lambda h: (0, h)), h4=h4, v2=v2,
        b16=b16)


def _bias_row(ref, d):
    mask = _rows((2 * HEADS, HD)) == d * HEADS + pl.program_id(0)
    return jnp.sum(jnp.where(mask, ref[...], 0.0), axis=0, keepdims=True), mask


def _lru_forward(zx, zc, cw, cb, wa, wx, ba, bx, lam, gather, gather_modes):
    ng_ = len(gather)

    def body(xa_ref, xac_ref, cw_ref, cb_ref, wa_ref, wx_ref, ba_ref, bx_ref, lam_ref, *rest):
        yl_ref = rest[ng_]
        af, uf, hf, ab, ub, hb, pad_s = rest[2 * ng_ + 1:2 * ng_ + 8]
        start, pass_on, finish = _gather2_ops(rest[:ng_], rest[ng_ + 1:2 * ng_ + 1], gather_modes,
                                              *rest[2 * ng_ + 8:], barrier=True)
        pl.when(pl.program_id(0) == 0)(start)
        pl.when(pl.program_id(0) == HEADS // 2)(pass_on)
        pl.when(pl.program_id(0) == HEADS - 1)(finish)
        cwv, cbv = cw_ref[...], cb_ref[...]
        nsp = (-LRU_C) * _softplus(-lam_ref[...])

        def forward(xa, t_len, h0f, h0b):
            xc = _conv(xa, cwv, cbv, pad_s)
            for d, (a_ref, u_ref) in enumerate(((af, uf), (ab, ub))):
                _, i, a, gamma, _ = _gates(xc, wa_ref[d, 0].astype(BF16), wx_ref[d, 0].astype(BF16),
                                           _bias_row(ba_ref, d)[0], _bias_row(bx_ref, d)[0], nsp[d:d + 1, :])
                a_ref[0:t_len, :] = a
                u_ref[0:t_len, :] = gamma * (i * xc)
            return _scan_pair(af, uf, hf, h0f, ab, ub, hb, h0b, t_len)

        z = jnp.zeros((1, HD), F32)
        h0f, h0b = forward(xac_ref[...], LC, z, z)
        forward(xa_ref[...], L, h0f, h0b)
        yl_ref[...] = hf[...] + hb[...]

    s = _lru_param_specs()
    return _call(
        body, name="lru_forward", grid=(HEADS,),
        out_shape=[jax.ShapeDtypeStruct((L, D), F32)] + _gather2_shapes(gather, gather_modes),
        in_specs=[s["xa"], s["xac"], s["cw"], s["cb"], s["h4"], s["h4"], s["b16"], s["b16"], s["v2"]] + [HBM] * ng_,
        out_specs=[pl.BlockSpec((L, HD), lambda h: (0, h))] + [HBM] * ng_,
        scratch_shapes=[pltpu.VMEM((L, HD), F32)] * 6 + [pltpu.VMEM((L + 16, HD), F32)] + _gather2_sems(ng_),
        compiler_params=pltpu.CompilerParams(dimension_semantics=("arbitrary",), vmem_limit_bytes=VMEM_LIMIT,
                                             has_side_effects=True, collective_id=5),
    )(zx, zc, cw, cb, wa, wx, ba, bx, lam, *[pltpu.with_memory_space_constraint(a, pltpu.HBM) for a in gather])


def _lru_backward(zx, zc, dyl, dz, cw, cb, wa, wx, ba, bx, lam, chip_sums, first_chips=None):
    nr = len(chip_sums)

    def body(xa_ref, xac_ref, dyl_ref, dz_in, cw_ref, cb_ref, wa_ref, wx_ref, ba_ref, bx_ref, lam_ref, *rest):
        (dxa_ref, dxac_ref, dwa_ref, dwx_ref, dba_ref, dbx_ref, dlam_ref, dcw_ref,
         dcb_ref) = rest[nr:nr + 9]
        main_s, ctx_s, pad_s = rest[3 * nr + 9:3 * nr + 12]
        if nr:
            start, forward, finish = _chips_ops(rest[:nr], rest[nr + 9:2 * nr + 9], rest[2 * nr + 9:3 * nr + 9],
                                                *rest[3 * nr + 12:], first_chips=first_chips, barrier=True)
            pl.when(pl.program_id(0) == 0)(start)
            pl.when(pl.program_id(0) == HEADS // 2)(forward)
            pl.when(pl.program_id(0) == HEADS - 1)(finish)
        del dz_in

        @pl.when(pl.program_id(0) == 0)
        def _():
            dba_ref[...] = jnp.zeros_like(dba_ref)
            dbx_ref[...] = jnp.zeros_like(dbx_ref)

        cwv, cbv = cw_ref[...], cb_ref[...]
        lamv = lam_ref[...]
        sp = _softplus(-lamv)
        nsp = (-LRU_C) * sp
        z = jnp.zeros((1, HD), F32)

        def wmat(ref, d):
            return ref[d, 0].astype(BF16)

        def workspace(s):
            return dict(a=(s.at[0], s.at[1]), u=(s.at[2], s.at[3]), h=(s.at[4], s.at[5]), rho=(s.at[6], s.at[7]),
                        saved=(tuple(s.at[8 + k] for k in range(4)), tuple(s.at[12 + k] for k in range(4))),
                        xc=s.at[16])

        def forward(ws, xa, t_len, h0f, h0b):
            xc = _conv(xa, cwv, cbv, pad_s)
            ws["xc"][...] = xc
            for d in (0, 1):
                vals = _gates(xc, wmat(wa_ref, d), wmat(wx_ref, d), _bias_row(ba_ref, d)[0],
                              _bias_row(bx_ref, d)[0], nsp[d:d + 1, :])
                r, i, a, gamma, rg = vals
                ws["a"][d][...] = a
                ws["u"][d][...] = gamma * (i * xc)
                for ref, val in zip(ws["saved"][d], (r, i, gamma, rg)):
                    ref[...] = val
            return _scan_pair(ws["a"][0], ws["u"][0], ws["h"][0], h0f, ws["a"][1], ws["u"][1], ws["h"][1], h0b,
                              t_len)

        def backward(ws, xa, t_len, h0f, h0b, dhf, dhb, first):
            xc = ws["xc"][...]
            (af, ab), (uf, ub), (hf, hb), (rf, rb) = ws["a"], ws["u"], ws["h"], ws["rho"]
            uf[...] = ab[...] * dhb
            ub[...] = af[...] * dhf
            rho_b_last, rho_f_first = _scan_pair(ab, uf, rb, z, af, ub, rf, z, t_len)
            dxc = jnp.zeros((t_len, HD), F32)
            dsp = []
            for d in (0, 1):
                r, i, gamma, rg = (ref[...] for ref in ws["saved"][d])
                a = ws["a"][d][...]
                if d == 0:
                    lam_t = dhf + _shifted(pad_s, rf[...], (1,))[0]
                    h_prev = _shifted(pad_s, hf[...], (-1,), before=h0f)[0]
                else:
                    lam_t = dhb + _shifted(pad_s, rb[...], (-1,))[0]
                    h_prev = _shifted(pad_s, hb[...], (1,), after=h0b)[0]
                da = lam_t * h_prev
                lx = lam_t * xc
                d_i = lx * gamma
                d_gamma = lx * i
                dxc = dxc + lam_t * (gamma * i)
                d_log_a = a * (da - d_gamma * (a * rg))
                dsp.append(jnp.sum(d_log_a * r, axis=0, keepdims=True) * (-LRU_C))
                d_pre_r = d_log_a * nsp[d:d + 1, :] * (r * (1.0 - r))
                d_pre_i = d_i * (i * (1.0 - i))
                prb, pib, xb = d_pre_r.astype(BF16), d_pre_i.astype(BF16), xc.astype(BF16)
                dxc = dxc + _dot_nt(prb, wmat(wa_ref, d)) + _dot_nt(pib, wmat(wx_ref, d))
                g_wa, g_wx = _dot_tn(xb, prb), _dot_tn(xb, pib)
                g_ba = jnp.sum(d_pre_r, axis=0, keepdims=True)
                g_bx = jnp.sum(d_pre_i, axis=0, keepdims=True)
                mask = _bias_row(ba_ref, d)[1]
                dba_ref[...] += jnp.where(mask, g_ba, 0.0)
                dbx_ref[...] += jnp.where(mask, g_bx, 0.0)
                if first:
                    dwa_ref[d, 0] = g_wa
                    dwx_ref[d, 0] = g_wx
                else:
                    dwa_ref[d, 0] += g_wa
                    dwx_ref[d, 0] += g_wx
            g_lam = jnp.concatenate(dsp, axis=0) * (-_sigmoid(-lamv))
            dm1, dp1, dm2 = _shifted(pad_s, dxc, (-1, 1, -2))
            dxa = dp1 * cwv[0:1, :] + dxc * cwv[1:2, :] + dm1 * cwv[2:3, :] + dm2 * cwv[3:4, :]
            xm1, xp1, xp2 = _shifted(pad_s, xa, (-1, 1, 2))
            g_cw = jnp.concatenate([jnp.sum(dxc * v, axis=0, keepdims=True) for v in (xm1, xa, xp1, xp2)], axis=0)
            g_cb = jnp.sum(dxc, axis=0, keepdims=True)
            if first:
                dlam_ref[...] = g_lam
                dcw_ref[...] = g_cw
                dcb_ref[...] = g_cb
            else:
                dlam_ref[...] += g_lam
                dcw_ref[...] += g_cw
                dcb_ref[...] += g_cb
            return dxa, rho_f_first, rho_b_last

        ws_x, ws_c = workspace(main_s), workspace(ctx_s)
        h0f, h0b = forward(ws_c, xac_ref[...], LC, z, z)
        forward(ws_x, xa_ref[...], L, h0f, h0b)
        dh = dyl_ref[...]
        dxa, dh0f, dh0b = backward(ws_x, xa_ref[...], L, h0f, h0b, dh, dh, True)
        dxa_ref[...] = dxa.astype(BF16)
        rc = _rows((LC, HD))
        dxac, _, _ = backward(ws_c, xac_ref[...], LC, z, z, jnp.where(rc == LC - 1, dh0f, 0.0),
                              jnp.where(rc == 0, dh0b, 0.0), False)
        dxac_ref[...] = dxac.astype(BF16)

    s = _lru_param_specs()
    col = lambda r: pl.BlockSpec((r, HD), lambda h: (0, h))
    return _call(
        body, name="lru_backward", grid=(HEADS,),
        out_shape=[jax.ShapeDtypeStruct((L, D_IN), BF16), jax.ShapeDtypeStruct((LC, D), BF16),
                   jax.ShapeDtypeStruct((2, HEADS, HD, HD), F32), jax.ShapeDtypeStruct((2, HEADS, HD, HD), F32),
                   jax.ShapeDtypeStruct((2 * HEADS, HD), F32), jax.ShapeDtypeStruct((2 * HEADS, HD), F32),
                   jax.ShapeDtypeStruct((2, D), F32), jax.ShapeDtypeStruct((CONV_W, D), F32),
                   jax.ShapeDtypeStruct((1, D), F32)] + [jax.ShapeDtypeStruct((4,) + a.shape[1:], a.dtype)
                                                          for a in chip_sums] + _chips_stage_shapes(chip_sums),
        in_specs=[s["xa"], s["xac"], col(L), pl.BlockSpec(memory_space=pl.ANY), s["cw"], s["cb"], s["h4"], s["h4"],
                  s["b16"], s["b16"], s["v2"]] + [HBM] * nr,
        out_specs=[col(L), col(LC), s["h4"], s["h4"], s["b16"], s["b16"], s["v2"], col(CONV_W), col(1)]
        + [HBM] * (2 * nr),
        scratch_shapes=[pltpu.VMEM((17, L, HD), F32), pltpu.VMEM((17, LC, HD), F32), pltpu.VMEM((L + 16, HD), F32)]
        + (_chips_sems(nr) if nr else []),
        input_output_aliases={3: 0},
        compiler_params=pltpu.CompilerParams(dimension_semantics=("arbitrary",), vmem_limit_bytes=VMEM_LIMIT,
                                             has_side_effects=True, collective_id=6 if nr else None),
    )(zx, zc, dyl, dz, cw, cb, wa, wx, ba, bx, lam, *[pltpu.with_memory_space_constraint(a, pltpu.HBM)
                                                       for a in chip_sums])


def _mixer_loss(x, tgt, zx, yl, gx, fg, lng, lnb, ws, wst, bst, wout, tm):
    ncht = tm // CHUNK

    def body(x_ref, t_ref, ga_ref, u_ref, v_ref, gb_ref, yl_ref, gx_ref, fg_ref, lng_ref, lnb_ref, ws_ref, wst_ref,
             bst_ref, wout_ref,
             dz_ref, dyl_ref, dxn_ref, y_s, do_ref, dws_ref, dbst_ref, vec_ref,
             vn_s, mix_s, dm_s, dvn_s):
        step = pl.program_id(0)

        @pl.when(step == 0)
        def _():
            dws_ref[...] = jnp.zeros_like(dws_ref)
            dbst_ref[...] = jnp.zeros_like(dbst_ref)
            vec_ref[...] = jnp.zeros_like(vec_ref)

        u, v = u_ref[...], v_ref[...]
        ug, dug_du = _gelu_and_grad(u)
        vg, dvg_dv = _gelu_and_grad(v)
        mu = jnp.mean(vg, axis=-1, keepdims=True)
        vc = vg - mu
        rstd = lax.rsqrt(jnp.mean(vc * vc, axis=-1, keepdims=True) + LN_EPS)
        vhat = vc * rstd
        lngv = lng_ref[...]
        vn_s[...] = (vhat * lngv + lnb_ref[...]).astype(BF16)
        for ch in range(ncht):
            rs = slice(ch * CHUNK, (ch + 1) * CHUNK)
            for g in range(HEADS):
                cs = slice(g * HD, (g + 1) * HD)
                mix_s[rs, cs] = _dot(ws_ref[g], vn_s[rs, cs]) + bst_ref[:, g:g + 1]
        mixed = mix_s[...]
        ga, gb, yl = ga_ref[...], gb_ref[...], yl_ref[...]
        sga, dsga = _silu_and_grad(ga)
        sgb, dsgb = _silu_and_grad(gb)
        ys = ug * mixed
        y_s[:, 0:D] = (yl * sga).astype(BF16)
        y_s[:, D:D_MIX] = (ys * sgb).astype(BF16)
        o = _dot(y_s[...], wout_ref[...])
        gxv, fgv = gx_ref[...], fg_ref[...]
        xn = x_ref[...] + gxv * o
        rs2 = lax.rsqrt(jnp.mean(xn * xn, axis=-1, keepdims=True) + NORM_EPS)
        xh = xn * rs2
        diff = xh * fgv - t_ref[...]
        vec_ref[R_LOSS:R_LOSS + 1, :] += jnp.full((1, D), jnp.sum(diff * diff) * (0.5 / D), F32)
        dout = diff * (1.0 / D)
        w = dout * fgv
        dxn = rs2 * (w - xh * jnp.mean(w * xh, axis=-1, keepdims=True))
        dxn_ref[...] = dxn
        vec_ref[0:1, :] += jnp.sum(dxn * o, axis=0, keepdims=True)
        vec_ref[1:2, :] += jnp.sum(dout * xh, axis=0, keepdims=True)
        dob = (dxn * gxv).astype(BF16)
        do_ref[...] = dob
        dy = _dot_nt(dob, wout_ref[...])
        dya, dyb = dy[:, 0:D], dy[:, D:D_MIX]
        dyl_ref[...] = dya * sga
        dys = dyb * sgb
        dz_ref[:, 0:D] = jnp.zeros((tm, D), BF16)
        dz_ref[:, D:2 * D] = (dya * yl * dsga).astype(BF16)
        dz_ref[:, 2 * D:3 * D] = (dys * mixed * dug_du).astype(BF16)
        dz_ref[:, 4 * D:5 * D] = (dyb * ys * dsgb).astype(BF16)
        dm = dys * ug
        dm_s[...] = dm.astype(BF16)
        for g in range(HEADS):
            cs = slice(g * HD, (g + 1) * HD)
            dbst_ref[:, g:g + 1] += sum(jnp.sum(dm[ch * CHUNK:(ch + 1) * CHUNK, cs], axis=1, keepdims=True)
                                        for ch in range(ncht))
            for ch in range(ncht):
                rs = slice(ch * CHUNK, (ch + 1) * CHUNK)
                dws_ref[g] += _dot_nt(dm_s[rs, cs], vn_s[rs, cs])
                dvn_s[rs, cs] = _dot(wst_ref[g], dm_s[rs, cs])
        dvn = dvn_s[...]
        vec_ref[2:3, :] += jnp.sum(dvn * vhat, axis=0, keepdims=True)
        vec_ref[3:4, :] += jnp.sum(dvn, axis=0, keepdims=True)
        dvh = dvn * lngv
        dvg = rstd * (dvh - jnp.mean(dvh, axis=-1, keepdims=True) - vhat * jnp.mean(dvh * vhat, axis=-1, keepdims=True))
        dz_ref[:, 3 * D:4 * D] = (dvg * dvg_dv).astype(BF16)

    tile = pl.BlockSpec((tm, D), lambda i: (i, 0))
    zcol = lambda n: pl.BlockSpec((tm, D), lambda i: (i, n))
    vec = pl.BlockSpec((1, D), lambda i: (0, 0))
    full = lambda *s: pl.BlockSpec(s, lambda i: (0,) * len(s))
    return _call(
        body, name="mixer_loss", grid=(L // tm,),
        out_shape=[jax.ShapeDtypeStruct((L, D_IN), BF16), jax.ShapeDtypeStruct((L, D), F32),
                   jax.ShapeDtypeStruct((L, D), F32), jax.ShapeDtypeStruct((L, D_MIX), BF16),
                   jax.ShapeDtypeStruct((L, D), BF16),
                   jax.ShapeDtypeStruct((HEADS, CHUNK, CHUNK), F32), jax.ShapeDtypeStruct((CHUNK, HEADS), F32),
                   jax.ShapeDtypeStruct((8, D), F32)],
        in_specs=[tile, tile, zcol(1), zcol(2), zcol(3), zcol(4), tile, pl.BlockSpec((1, D), lambda i: (0, 2)),
                  vec, vec, vec,
                  full(HEADS, CHUNK, CHUNK), full(HEADS, CHUNK, CHUNK), full(CHUNK, HEADS),
                  pl.BlockSpec((D_MIX, D), lambda i: (0, 0), pipeline_mode=pl.Buffered(1))],
        out_specs=[pl.BlockSpec((tm, D_IN), lambda i: (i, 0)), tile, tile,
                   pl.BlockSpec((tm, D_MIX), lambda i: (i, 0)), tile,
                   full(HEADS, CHUNK, CHUNK), full(CHUNK, HEADS), full(8, D)],
        scratch_shapes=[pltpu.VMEM((tm, D), BF16), pltpu.VMEM((tm, D), F32),
                        pltpu.VMEM((tm, D), BF16), pltpu.VMEM((tm, D), F32)],
        compiler_params=_params("arbitrary"),
    )(x, tgt, zx, zx, zx, zx, yl, gx, fg, lng, lnb, ws, wst, bst, wout)


def _grad_w(a, b, a2, b2, tk, name, bw, first, nblocks, split, barrier_id, chip_sums=()):
    nk = a.shape[0] // tk
    m = a.shape[1]
    with_ctx = a2 is not None
    if split == "cols":
        slots, r, w = nblocks, m, bw // 2
        piece = lambda q, pc: (slice(None), slice(pc * w, (pc + 1) * w))
    else:
        slots, r, w = 4, m // 8, bw
        piece = lambda q, pc: (slice((2 * q + pc) * r, (2 * q + pc + 1) * r), slice(None))

    nr = len(chip_sums)

    def body(*refs):
        a_ref, b_ref = refs[:2]
        a2_ref, b2_ref = refs[2:4] if with_ctx else (None, None)
        base = 4 if with_ctx else 2
        sums_ref = refs[base + nr]
        acc, mine_v, send_v, stage_v, send_sems, recv_sems = refs[base + 3 * nr + 1:base + 3 * nr + 7]
        n, k = pl.program_id(0), pl.program_id(1)
        x, y, c = lax.axis_index("x"), lax.axis_index("y"), lax.axis_index("c")

        def to_sibling(s):
            return pltpu.make_async_remote_copy(src_ref=send_v.at[s], dst_ref=stage_v.at[s], send_sem=send_sems.at[s],
                                                recv_sem=recv_sems.at[s], device_id=(x, y, 1 - c),
                                                device_id_type=MESH)

        if nr:
            c_start, c_forward, c_finish = _chips_ops(refs[base:base + nr], refs[base + nr + 1:base + 2 * nr + 1],
                                                      refs[base + 2 * nr + 1:base + 3 * nr + 1],
                                                      *refs[base + 3 * nr + 7:])

            @pl.when(jnp.logical_and(n == 0, k == 0))
            def _():
                _barrier([(x, y, 1 - c)] + [(x ^ (j >> 1), y ^ (j & 1), c) for j in (1, 2)])
                c_start()
        else:
            pl.when(jnp.logical_and(n == 0, k == 0))(_sibling_barrier)

        @pl.when(k == 0)
        def _():
            acc[...] = _dot_tn(a_ref[...], b_ref[...])

        if nk > 1:
            @pl.when(k > 0)
            def _():
                acc[...] += _dot_tn(a_ref[...], b_ref[...])

        if with_ctx:
            @pl.when(jnp.logical_and(k == nk - 1, n == 0))
            def _():
                acc[:, 0:b2_ref.shape[1]] += _dot_tn(a2_ref[...], b2_ref[...])

        if nr:
            pl.when(jnp.logical_and(k == nk - 1, n == nblocks - 1))(c_forward)

        def hand_over(s, q):
            for pc in (0, 1):
                @pl.when(c == pc)
                def _(pc=pc):
                    mine_v[s] = acc[piece(q, pc)]
                    send_v[s] = acc[piece(q, 1 - pc)].astype(BF16)
            to_sibling(s).start()

        for i in range(nblocks):
            @pl.when(jnp.logical_and(k == nk - 1, n == i))
            def _(i=i):
                if split == "cols":
                    hand_over(i, 0)
                else:
                    for q in range(4):
                        hand_over(q, q)

        @pl.when(jnp.logical_and(k == nk - 1, n == nblocks - 1))
        def _():
            for s in range(slots):
                to_sibling(s).wait_recv()
                sums_ref[s] = (mine_v[s] + stage_v[s].astype(F32)).astype(BF16)
            for s in range(slots):
                to_sibling(s).wait_send()
            if nr:
                c_finish()

    in_specs = [pl.BlockSpec((tk, m), lambda n, k: (k, 0)), pl.BlockSpec((tk, bw), lambda n, k: (k, n + first))]
    args = [a, b]
    if with_ctx:
        in_specs += [pl.BlockSpec(a2.shape, lambda n, k: (0, 0)), pl.BlockSpec(b2.shape, lambda n, k: (0, 0))]
        args += [a2, b2]
    in_specs += [HBM] * nr
    args += [pltpu.with_memory_space_constraint(s, pltpu.HBM) for s in chip_sums]
    return _call(
        body, name=name, grid=(nblocks, nk),
        out_shape=[jax.ShapeDtypeStruct((slots, r, w), BF16)]
        + [jax.ShapeDtypeStruct((4,) + s.shape[1:], s.dtype) for s in chip_sums] + _chips_stage_shapes(chip_sums),
        in_specs=in_specs, out_specs=[pl.BlockSpec((slots, r, w), lambda n, k: (0, 0, 0))] + [HBM] * (2 * nr),
        scratch_shapes=[pltpu.VMEM((m, bw), F32), pltpu.VMEM((slots, r, w), F32), pltpu.VMEM((slots, r, w), BF16),
                        pltpu.VMEM((slots, r, w), BF16), pltpu.SemaphoreType.DMA((slots,)),
                        pltpu.SemaphoreType.DMA((slots,))] + (_chips_sems(nr) if nr else []),
        compiler_params=pltpu.CompilerParams(dimension_semantics=("arbitrary", "arbitrary"),
                                             vmem_limit_bytes=VMEM_LIMIT, has_side_effects=True,
                                             collective_id=barrier_id),
    )(*args)


def _grad_rows(xr, dz, w, mod, ng, dres, ncols, tm, name, chip_sums=(), first_chips=None, dests=None):
    rows = xr.shape[0]
    steps = rows // tm
    with_dx = dres is not None
    nr = len(chip_sums)
    dests = [d for d in (dests or [None] * nr)]
    nd = sum(d is not None for d in dests)
    nin = 6 if with_dx else 5
    nout = 2 if with_dx else 1

    def body(*refs):
        if with_dx:
            x_ref, dz_ref, w_ref, sc_ref, ng_ref, dres_ref = refs[:nin]
            dx_ref, vec_ref = refs[nin + nr + nd:nin + nr + nd + nout]
        else:
            x_ref, dz_ref, w_ref, sc_ref, ng_ref = refs[:nin]
            (vec_ref,) = refs[nin + nr + nd:nin + nr + nd + nout]
        if nr:
            o0 = nin + nr + nd + nout
            start, forward, finish = _chips_ops(refs[nin:nin + nr], refs[o0:o0 + nr], refs[o0 + nr:o0 + 2 * nr],
                                                *refs[o0 + 2 * nr:], first_chips=first_chips, barrier=True)
            pl.when(pl.program_id(0) == 0)(start)
            pl.when(pl.program_id(0) == steps // 2)(forward)
            pl.when(pl.program_id(0) == steps - 1)(finish)

        @pl.when(pl.program_id(0) == 0)
        def _():
            vec_ref[...] = jnp.zeros_like(vec_ref)

        dhn = _dot_nt(dz_ref[...], w_ref[...])
        x = x_ref[...]
        rs = lax.rsqrt(jnp.mean(x * x, axis=-1, keepdims=True) + NORM_EPS)
        xh = x * rs
        ngv = ng_ref[...]
        y = xh * ngv
        vec_ref[0:1, :] += jnp.sum(dhn, axis=0, keepdims=True)
        vec_ref[1:2, :] += jnp.sum(dhn * y, axis=0, keepdims=True)
        dy = dhn * (1.0 + sc_ref[...])
        vec_ref[2:3, :] += jnp.sum(dy * xh, axis=0, keepdims=True)
        if with_dx:
            dxh = dy * ngv
            dx_ref[...] = dres_ref[...] + rs * (dxh - xh * jnp.mean(dxh * xh, axis=-1, keepdims=True))

    tile = pl.BlockSpec((tm, D), lambda i: (i, 0))
    vec = pl.BlockSpec((1, D), lambda i: (0, 0))
    in_specs = [tile, pl.BlockSpec((tm, ncols), lambda i: (i, 0)),
                pl.BlockSpec((D, ncols), lambda i: (0, 0), pipeline_mode=pl.Buffered(1)),
                pl.BlockSpec((1, D), lambda i: (0, 1)), vec]
    out_shape = [jax.ShapeDtypeStruct((8, D), F32)]
    out_specs = [pl.BlockSpec((8, D), lambda i: (0, 0))]
    args = [xr, dz, w, mod, ng]
    if with_dx:
        in_specs.append(tile)
        out_shape.insert(0, jax.ShapeDtypeStruct((rows, D), F32))
        out_specs.insert(0, tile)
        args.append(dres)
    aliases = {}
    for j, d in enumerate(dests):
        if d is not None:
            aliases[len(args) + nr + len(aliases)] = len(out_shape) + j
    in_specs += [HBM] * (nr + nd)
    out_specs += [HBM] * (2 * nr)
    out_shape += [jax.ShapeDtypeStruct((4,) + a.shape[1:], a.dtype) for a in chip_sums]
    out_shape += _chips_stage_shapes(chip_sums)
    args += [pltpu.with_memory_space_constraint(a, pltpu.HBM) for a in chip_sums]
    args += [pltpu.with_memory_space_constraint(d, pltpu.HBM) for d in dests if d is not None]
    return _call(body, name=name, grid=(steps,), out_shape=out_shape, in_specs=in_specs, out_specs=out_specs,
                 scratch_shapes=_chips_sems(nr) if nr else [], input_output_aliases=aliases,
                 compiler_params=pltpu.CompilerParams(dimension_semantics=("arbitrary",),
                                                      vmem_limit_bytes=VMEM_LIMIT, has_side_effects=bool(nr),
                                                      collective_id=7 if nr else None))(*args)


def _adamw(w, g, m, v):
    m = ADAM_B1 * m + (1.0 - ADAM_B1) * g
    v = ADAM_B2 * v + (1.0 - ADAM_B2) * (g * g)
    m_hat = m / (1.0 - ADAM_B1 ** ADAM_STEP)
    v_hat = v / (1.0 - ADAM_B2 ** ADAM_STEP)
    delta = -ADAM_LR * (m_hat / (jnp.sqrt(v_hat) + ADAM_EPS) + ADAM_WD * w)
    return delta, m, v


def _adamw_reduced(parts, w, m, v, tr, name):
    r, n = w.shape
    nparts = parts.shape[0]

    def body(p_ref, w_ref, m_ref, v_ref, g_ref, d_ref, mo_ref, vo_ref):
        g = p_ref[0].astype(F32)
        for i in range(1, nparts):
            g = g + p_ref[i].astype(F32)
        g_ref[...] = g
        d_ref[...], mo_ref[...], vo_ref[...] = _adamw(w_ref[...], g, m_ref[...], v_ref[...])

    tile = pl.BlockSpec((tr, n), lambda i: (i, 0))
    sds = jax.ShapeDtypeStruct((r, n), F32)
    return _call(
        body, name=name, grid=(r // tr,), out_shape=[sds] * 4,
        in_specs=[pl.BlockSpec((nparts, tr, n), lambda i: (0, i, 0)), tile, tile, tile], out_specs=[tile] * 4,
        compiler_params=_params("arbitrary"),
    )(parts, w, m, v)


R_GATE, R_FINAL_G, R_LN_G, R_LN_B, R_LOSS = 0, 1, 2, 3, 4
R_SH_X, R_SC_X, R_NG_X = 5, 6, 7
R_SH_C, R_SC_C, R_NG_C = 8, 9, 10
R_LAM, R_CW, R_CB = 11, 13, 17
PACK_ROWS = 24
Q_BA, Q_BX, Q_SGU_B, PACK128_ROWS = 0, 16, 32, 40


def _reduce_small(vp_all, vq_all, mat_parts, ada_w, me):
    nloc = ada_w.shape[1]
    nm = len(mat_parts)

    def body(me_ref, vp_ref, vq_ref, *refs):
        mp_refs, w_ref = refs[:nm], refs[nm]
        red_ref, redq_ref = refs[nm + 1:nm + 3]
        mat_refs = refs[nm + 3:2 * nm + 3]
        dmod_ref, gab_ref, cpart_ref, dmc_s = refs[2 * nm + 3:]
        red, redq = vp_ref[0], vq_ref[0]
        for i in range(1, N_DEV):
            red = red + vp_ref[i]
            redq = redq + vq_ref[i]
        red_ref[...] = red
        redq_ref[...] = redq
        for mp_ref, mat_ref in zip(mp_refs, mat_refs):
            mat = mp_ref[0].astype(F32)
            for i in range(1, mp_ref.shape[0]):
                mat = mat + mp_ref[i].astype(F32)
            mat_ref[...] = mat
        for e in range(N_DEV):
            dmod_ref[e:e + 1, 0:D] = vp_ref[e, R_SH_X:R_SH_X + 1, :]
            dmod_ref[e:e + 1, D:2 * D] = vp_ref[e, R_SC_X:R_SC_X + 1, :]
            dmod_ref[e:e + 1, 2 * D:3 * D] = vp_ref[e, R_GATE:R_GATE + 1, :]
        dmod_ref[8:9, 0:D] = red[R_SH_C:R_SH_C + 1, :]
        dmod_ref[8:9, D:2 * D] = red[R_SC_C:R_SC_C + 1, :]
        dmod_ref[8:9, 2 * D:3 * D] = jnp.zeros((1, D), F32)
        dmod_ref[9:16, :] = jnp.zeros((7, 3 * D), F32)
        gab_ref[:, 0:D] = red[R_SH_X:R_SH_X + 1, :] + red[R_SH_C:R_SH_C + 1, :]
        gab_ref[:, D:2 * D] = red[R_SC_X:R_SC_X + 1, :] + red[R_SC_C:R_SC_C + 1, :]
        gab_ref[:, 2 * D:3 * D] = red[R_GATE:R_GATE + 1, :]
        dmc_s[...] = jnp.broadcast_to(dmod_ref[8:9, :], (8, 3 * D))
        off = pl.multiple_of(me_ref[0] * nloc, 128)
        cpart_ref[...] = _dot_nt(dmc_s[:, pl.ds(off, nloc)], w_ref[...])

    return _call(
        body, name="reduce_small",
        out_shape=[jax.ShapeDtypeStruct((PACK_ROWS, D), F32), jax.ShapeDtypeStruct((PACK128_ROWS, HD), F32)]
        + [jax.ShapeDtypeStruct(p.shape[1:], F32) for p in mat_parts]
        + [jax.ShapeDtypeStruct((16, 3 * D), F32), jax.ShapeDtypeStruct((1, 3 * D), F32),
           jax.ShapeDtypeStruct((8, D), F32)],
        in_specs=[pl.BlockSpec(memory_space=pltpu.SMEM)] + [VMEM] * (nm + 3), out_specs=[VMEM] * (nm + 5),
        scratch_shapes=[pltpu.VMEM((8, 3 * D), F32)], compiler_params=_params(),
    )(me, vp_all, vq_all, *mat_parts, ada_w)


def _adamw_ada(c_all, c_ctx, dmod, w, m, v, me):
    nloc = w.shape[1]

    def body(me_ref, c_ref, cc_ref, dm_ref, w_ref, m_ref, v_ref, g_ref, d_ref, mo_ref, vo_ref):
        off = pl.multiple_of(me_ref[0] * nloc, 128)
        dm = dm_ref[:, pl.ds(off, nloc)]
        sx, _ = _silu_and_grad(c_ref[...])
        sc, _ = _silu_and_grad(cc_ref[...])
        g = _dot_tn(sx, dm[0:8, :]) + _dot_tn(jnp.broadcast_to(sc, (8, D)), dm[8:16, :])
        g_ref[...] = g
        d_ref[...], mo_ref[...], vo_ref[...] = _adamw(w_ref[...], g, m_ref[...], v_ref[...])

    sds = jax.ShapeDtypeStruct(w.shape, F32)
    return _call(
        body, name="adamw_ada_w", out_shape=[sds] * 4,
        in_specs=[pl.BlockSpec(memory_space=pltpu.SMEM)] + [VMEM] * 6, out_specs=[VMEM] * 4,
        compiler_params=_params(),
    )(me, c_all, c_ctx, dmod, w, m, v)


_SMALL = ("c_ctx", "ada_b", "norm_g", "conv_w", "conv_b", "lru_wa", "lru_ba", "lru_wx", "lru_bx", "lru_lambda",
          "sgu_ln_g", "sgu_ln_b", "sgu_w", "sgu_b", "final_g")


def _adamw_small(red, redq, mats, cparts, gab, ws, ms, vs, me):
    n = len(_SMALL)

    def body(me_ref, red_ref, redq_ref, wa_ref, wx_ref, sw_ref, cp_ref, gab_ref, *refs):
        w_refs, m_refs, v_refs = refs[:n], refs[n:2 * n], refs[2 * n:3 * n]
        outs = refs[3 * n:]
        off = pl.multiple_of(me_ref[0] * HD, 128)

        def row(r, k=1):
            return red_ref[r:r + k, :]

        cc = w_refs[0][...]
        dcc = cp_ref[0, 0:1, :]
        for i in range(1, N_DEV):
            dcc = dcc + cp_ref[i, 0:1, :]
        grads = dict(
            c_ctx=dcc * _silu_and_grad(cc)[1], ada_b=gab_ref[...], norm_g=row(R_NG_X) + row(R_NG_C),
            conv_w=red_ref[R_CW:R_CW + CONV_W, pl.ds(off, HD)], conv_b=row(R_CB),
            lru_wa=wa_ref[...], lru_ba=redq_ref[Q_BA:Q_BA + 2 * HEADS, :], lru_wx=wx_ref[...],
            lru_bx=redq_ref[Q_BX:Q_BX + 2 * HEADS, :], lru_lambda=red_ref[R_LAM:R_LAM + 2, pl.ds(off, HD)],
            sgu_ln_g=row(R_LN_G), sgu_ln_b=row(R_LN_B), sgu_w=sw_ref[...],
            sgu_b=redq_ref[Q_SGU_B:Q_SGU_B + HEADS, :], final_g=row(R_FINAL_G))
        for j, name in enumerate(_SMALL):
            g = grads[name]
            outs[j][...] = g
            outs[n + j][...], outs[2 * n + j][...], outs[3 * n + j][...] = _adamw(w_refs[j][...], g, m_refs[j][...],
                                                                                 v_refs[j][...])

    sds = [jax.ShapeDtypeStruct(ws[k].shape, F32) for k in _SMALL]
    outs = _call(
        body, name="adamw_small", out_shape=sds * 4,
        in_specs=[pl.BlockSpec(memory_space=pltpu.SMEM)] + [VMEM] * (7 + 3 * n), out_specs=[VMEM] * (4 * n),
        compiler_params=_params(),
    )(me, red, redq, *mats, cparts, gab, *[ws[k] for k in _SMALL], *[ms[k] for k in _SMALL],
      *[vs[k] for k in _SMALL])
    return [dict(zip(_SMALL, outs[i * n:(i + 1) * n])) for i in range(4)]


def kernel(x, c, ctx, c_ctx, ada_w, ada_b, norm_g, w_in, conv_w, conv_b, lru_wa, lru_ba, lru_wx, lru_bx, lru_lambda, sgu_ln_g, sgu_ln_b, sgu_w, sgu_b, w_out, final_g, loss_target, m_c_ctx, m_ada_w, m_ada_b, m_norm_g, m_w_in, m_conv_w, m_conv_b, m_lru_wa, m_lru_ba, m_lru_wx, m_lru_bx, m_lru_lambda, m_sgu_ln_g, m_sgu_ln_b, m_sgu_w, m_sgu_b, m_w_out, m_final_g, v_c_ctx, v_ada_w, v_ada_b, v_norm_g, v_w_in, v_conv_w, v_conv_b, v_lru_wa, v_lru_ba, v_lru_wx, v_lru_bx, v_lru_lambda, v_sgu_ln_g, v_sgu_ln_b, v_sgu_w, v_sgu_b, v_w_out, v_final_g):
    args = dict(locals())
    me = (4 * lax.axis_index("x") + 2 * lax.axis_index("y") + lax.axis_index("c")).astype(jnp.int32).reshape(1)
    xr, ctxr, tgt = x[0], ctx[0], loss_target[0]
    cc = c_ctx.reshape(1, D)
    nw = 2 * HEADS * HD
    view = dict(c_ctx=(1, D), ada_b=(1, 3 * D), norm_g=(1, D), conv_w=(CONV_W, HD), conv_b=(1, D), lru_wa=(nw, HD),
                lru_ba=(2 * HEADS, HD), lru_wx=(nw, HD), lru_bx=(2 * HEADS, HD), lru_lambda=(2, HD), sgu_ln_g=(1, D),
                sgu_ln_b=(1, D), sgu_w=(HEADS * CHUNK, CHUNK), sgu_b=(HEADS, CHUNK), final_g=(1, D))

    zx, hn, w_full, w_out_b, modx, modc, c_all, cw_full, lam_full = _front_project(
        xr, c, cc, ada_w[0], ada_b, norm_g, w_in[0], w_out[0], conv_w[0], lru_lambda[0], me)
    zc, hnc = _project(ctxr, modc, norm_g, w_full, D, LC, "project_ctx")
    ba, bx = lru_ba.reshape(view["lru_ba"]), lru_bx.reshape(view["lru_bx"])
    yl, wout_all = _lru_forward(zx, zc, cw_full, conv_b, lru_wa[0], lru_wx[0], ba, bx, lam_full, [w_out_b], ["ag"])
    wout_full = wout_all.reshape(D_MIX, D)
    ws_b = sgu_w[0].astype(BF16)
    dz, dyl, dxn, ycat, dob, dws, dbst, mvec = _mixer_loss(
        xr, tgt, zx, yl, modx, final_g.reshape(1, D), sgu_ln_g, sgu_ln_b, ws_b, jnp.swapaxes(ws_b, 1, 2),
        sgu_b[0].T, wout_full, ROWS)

    (wout_sums,) = _grad_w(ycat, dob, None, None, L, "grad_w_out", D, 0, 1, "rows", 1)
    (rest_sums,) = _grad_w(hn, dz, None, None, L, "grad_w_in_rest", 2 * W_IN_SHARD, 1, 3, "cols", 2)
    dz, dxac, dwa, dwx, dba, dbx, dlam, dcw, dcb, win_parts, wout_parts, _, _ = _lru_backward(
        zx, zc, dyl, dz, cw_full, conv_b, lru_wa[0], lru_wx[0], ba, bx, lam_full, [rest_sums, wout_sums],
        first_chips=[1, 0])
    mats = [dwa.reshape(N_DEV, nw // N_DEV, HD), dwx.reshape(N_DEV, nw // N_DEV, HD), dws]
    mat_sums = _reduce2_local(mats, me, "reduce_mat", 4, BF16)
    first_sums, *mat_parts = _grad_w(hn, dz, hnc, dxac, L, "grad_w_in_first", 2 * W_IN_SHARD, 0, 1, "cols", 3,
                                     chip_sums=mat_sums)[:4]
    gx, xvec, win_parts = _grad_rows(
        xr, dz, w_full, modx, norm_g, dxn, D_IN, ROWS, "grad_rows_x", chip_sums=[first_sums], first_chips=[0],
        dests=[win_parts])[:3]
    (cvec,) = _grad_rows(ctxr, dxac, w_full, modc, norm_g, None, D, LC, "grad_rows_ctx")
    pack = jnp.concatenate([mvec[0:5], xvec[0:3], cvec[0:3], dlam, dcw, dcb,
                            jnp.zeros((PACK_ROWS - R_CB - 1, D), F32)], axis=0)
    pack128 = jnp.concatenate([dba, dbx, dbst.T], axis=0)
    vp_all, vq_all = _gather2([pack, pack128], ["ag", "ag"], "gather_pack", 8)
    red, redq, *rest = _reduce_small(vp_all, vq_all, mat_parts, ada_w[0], me)
    mat_pieces, (dmod, gab, cpart) = rest[:3], rest[3:]
    *mats_all, cparts = _gather2([*mat_pieces, cpart], ["ag"] * 4, "gather_small", 9)

    g_w_in, d_w_in, nm_w_in, nv_w_in = _adamw_reduced(win_parts, w_in[0], m_w_in[0], v_w_in[0], ROWS, "adamw_w_in")
    g_w_out, d_w_out, nm_w_out, nv_w_out = _adamw_reduced(wout_parts, w_out[0], m_w_out[0], v_w_out[0], ROWS // 2,
                                                          "adamw_w_out")
    g_ada, d_ada, nm_ada, nv_ada = _adamw_ada(c_all, cc, dmod, ada_w[0], m_ada_w[0], v_ada_w[0], me)
    ws = {k: args[k].reshape(view[k]) for k in _SMALL}
    ms = {k: args["m_" + k].reshape(view[k]) for k in _SMALL}
    vs = {k: args["v_" + k].reshape(view[k]) for k in _SMALL}
    small = _adamw_small(red, redq, [m.reshape(-1, HD) for m in mats_all], cparts, gab, ws, ms, vs, me)
    big = dict(w_in=(g_w_in, d_w_in, nm_w_in, nv_w_in), w_out=(g_w_out, d_w_out, nm_w_out, nv_w_out),
               ada_w=(g_ada, d_ada, nm_ada, nv_ada))

    loss = red[R_LOSS, 0]
    names = ("c_ctx", "ada_w", "ada_b", "norm_g", "w_in", "conv_w", "conv_b", "lru_wa", "lru_ba", "lru_wx", "lru_bx",
             "lru_lambda", "sgu_ln_g", "sgu_ln_b", "sgu_w", "sgu_b", "w_out", "final_g")
    outs = [loss, gx.reshape(x.shape)]
    for kind in range(4):
        for k in names:
            val = big[k][kind] if k in big else small[kind][k]
            outs.append(val.reshape(args[k].shape))
    return tuple(outs)
```

```python
import jax
import jax.numpy as jnp
from jax import lax
from jax.experimental import pallas as pl
from jax.experimental.pallas import tpu as pltpu

F32 = jnp.float32
BF16 = jnp.bfloat16

N_DEV = 8
D = 1024
L = 2048
LC = 256
HEADS = 8
HD = 128
CHUNK = 128
D_IN = 5 * D
W_IN_SHARD = D_IN // N_DEV
ROWS = 256
D_MIX = 2 * D
CONV_W = 4
LRU_C = 8.0
NORM_EPS = 1e-6
LN_EPS = 1e-5
ADAM_LR, ADAM_B1, ADAM_B2, ADAM_EPS, ADAM_WD, ADAM_STEP = 0.001, 0.9, 0.999, 1e-08, 0.01, 10

VMEM_LIMIT = 56 * 1024 * 1024

HBM = pl.BlockSpec(memory_space=pltpu.HBM)
VMEM = pl.BlockSpec(memory_space=pltpu.VMEM)
MESH = pl.DeviceIdType.MESH


def _call(body, **kw):
    return pl.pallas_call(body, **kw)


def _params(*sem):
    return pltpu.CompilerParams(dimension_semantics=sem, vmem_limit_bytes=VMEM_LIMIT)


def _sigmoid(x):
    return 0.5 * jnp.tanh(0.5 * x) + 0.5


def _silu_and_grad(x):
    s = _sigmoid(x)
    return x * s, s * (1.0 + x * (1.0 - s))


_G0 = 0.7978845608028654
_G1 = 0.044715


def _gelu_and_grad(x):
    x2 = x * x
    t = jnp.tanh(_G0 * (x + _G1 * x * x2))
    cdf = 0.5 * (1.0 + t)
    return x * cdf, cdf + 0.5 * x * (1.0 - t * t) * (_G0 * (1.0 + 3.0 * _G1 * x2))


def _softplus(z):
    t = jnp.exp(-jnp.abs(z))
    u = 1.0 + t
    log1p = jnp.where(u == 1.0, t, jnp.log(u) * t / jnp.where(u == 1.0, 1.0, u - 1.0))
    return jnp.maximum(z, 0.0) + log1p


def _dot(a, b):
    return jnp.dot(a, b, preferred_element_type=F32)


def _dot_nt(a, b):
    return lax.dot_general(a, b, (((1,), (1,)), ((), ())), preferred_element_type=F32)


def _dot_tn(a, b):
    return lax.dot_general(a, b, (((0,), (0,)), ((), ())), preferred_element_type=F32)


def _rows(shape):
    return lax.broadcasted_iota(jnp.int32, shape, 0)


def _gather2(arrays, modes, name, barrier_id):
    n = len(arrays)

    def body(*refs):
        start, forward, finish = _gather2_ops(refs[:n], refs[n:2 * n], modes, *refs[2 * n:], barrier=True)
        start()
        forward()
        finish()

    return _call(
        body, name=name, out_shape=_gather2_shapes(arrays, modes), in_specs=[HBM] * n, out_specs=[HBM] * n,
        scratch_shapes=_gather2_sems(n),
        compiler_params=pltpu.CompilerParams(has_side_effects=True, collective_id=barrier_id),
    )(*[pltpu.with_memory_space_constraint(a, pltpu.HBM) for a in arrays])


def _gather2_shapes(arrays, modes):
    return [jax.ShapeDtypeStruct((N_DEV,) + a.shape if m == "ag" else (a.shape[0], N_DEV * a.shape[1]), a.dtype)
            for a, m in zip(arrays, modes)]


def _gather2_sems(n):
    return [pltpu.SemaphoreType.DMA((n, N_DEV - 1)), pltpu.SemaphoreType.DMA((n, N_DEV - 1)),
            pltpu.SemaphoreType.DMA((n,))]


def _barrier(peers):
    sem = pltpu.get_barrier_semaphore()
    for peer in peers:
        pl.semaphore_signal(sem, inc=1, device_id=peer, device_id_type=MESH)
    pl.semaphore_wait(sem, len(peers))


def _gather2_ops(ins, outs, modes, send_sems, recv_sems, local_sems, barrier=False):
    n = len(ins)
    x, y, c = lax.axis_index("x"), lax.axis_index("y"), lax.axis_index("c")
    me, sibling = (x, y, c), (x, y, 1 - c)
    chips = [(x ^ (k >> 1), y ^ (k & 1)) for k in (1, 2, 3)]

    def slot(j, px, py, pc):
        dev = 4 * px + 2 * py + pc
        if modes[j] == "agc":
            w = ins[j].shape[1]
            return outs[j].at[:, pl.ds(pl.multiple_of(dev * w, 128), w)]
        return outs[j].at[dev]

    def copy(j, k, block, to, src=None):
        return pltpu.make_async_remote_copy(
            src_ref=slot(j, *block) if src is None else src, dst_ref=slot(j, *block),
            send_sem=send_sems.at[j, k], recv_sem=recv_sems.at[j, k], device_id=to, device_id_type=MESH)

    def own(j):
        return pltpu.make_async_copy(ins[j], slot(j, *me), local_sems.at[j])

    def first(j):
        return [copy(j, 0, me, sibling, src=ins[j])] + [copy(j, 1 + i, me, (*chip, c), src=ins[j])
                                                        for i, chip in enumerate(chips)]

    def passed(j, i):
        return copy(j, 4 + i, (*chips[i], c), sibling)

    def start():
        if barrier:
            _barrier([sibling] + [(*chip, c) for chip in chips])
        for j in range(n):
            own(j).start()
            for cp in first(j):
                cp.start()

    def forward():
        for i, chip in enumerate(chips):
            for j in range(n):
                copy(j, 1 + i, (*chip, c), me).wait_recv()
                passed(j, i).start()

    def finish():
        for j in range(n):
            copy(j, 0, sibling, me).wait_recv()
            for i, chip in enumerate(chips):
                copy(j, 4 + i, (*chip, 1 - c), me).wait_recv()
            for cp in first(j) + [passed(j, i) for i in range(3)]:
                cp.wait_send()
            own(j).wait()

    return start, forward, finish


def _sibling_barrier():
    sem = pltpu.get_barrier_semaphore()
    sibling = (lax.axis_index("x"), lax.axis_index("y"), 1 - lax.axis_index("c"))
    pl.semaphore_signal(sem, inc=1, device_id=sibling, device_id_type=MESH)
    pl.semaphore_wait(sem, 1)


def _reduce2_local(arrays, me, name, barrier_id, out_dtype):
    n = len(arrays)
    staged = [jax.ShapeDtypeStruct((4,) + a.shape[1:], a.dtype) for a in arrays]

    def to_sibling(*refs):
        ins, outs = refs[:n], refs[n:2 * n]
        send_sems, recv_sems = refs[2 * n:]
        x, y, c = lax.axis_index("x"), lax.axis_index("y"), lax.axis_index("c")
        _sibling_barrier()
        copies = []
        for j in range(n):
            for q in range(4):
                cp = pltpu.make_async_remote_copy(
                    src_ref=ins[j].at[2 * q + (1 - c)], dst_ref=outs[j].at[q], send_sem=send_sems.at[j, q],
                    recv_sem=recv_sems.at[j, q], device_id=(x, y, 1 - c), device_id_type=MESH)
                cp.start()
                copies.append(cp)
        for cp in copies:
            cp.wait()

    stage = _call(
        to_sibling, name=name + "_d2d", out_shape=staged, in_specs=[HBM] * n, out_specs=[HBM] * n,
        scratch_shapes=[pltpu.SemaphoreType.DMA((n, 4)), pltpu.SemaphoreType.DMA((n, 4))],
        compiler_params=pltpu.CompilerParams(has_side_effects=True, collective_id=barrier_id),
    )(*[pltpu.with_memory_space_constraint(a, pltpu.HBM) for a in arrays])

    def add(me_ref, *refs):
        del me_ref
        own, got, outs = refs[:n], refs[n:2 * n], refs[2 * n:]
        for j in range(n):
            outs[j][0] = (own[j][0].astype(F32) + got[j][0].astype(F32)).astype(out_dtype)

    own_specs = [pl.BlockSpec((1,) + a.shape[1:], lambda q, me_ref: (2 * q + me_ref[0] % 2, 0, 0)) for a in arrays]
    slot_specs = [pl.BlockSpec((1,) + a.shape[1:], lambda q, me_ref: (q, 0, 0)) for a in arrays]
    return _call(
        add, name=name + "_add", out_shape=[jax.ShapeDtypeStruct(s.shape, out_dtype) for s in staged],
        grid_spec=pltpu.PrefetchScalarGridSpec(num_scalar_prefetch=1, grid=(4,), in_specs=own_specs + slot_specs,
                                               out_specs=slot_specs),
        compiler_params=_params("arbitrary"),
    )(me, *arrays, *stage)


def _chips_sems(n):
    return [pltpu.SemaphoreType.DMA((n, 6)), pltpu.SemaphoreType.DMA((n, 6)), pltpu.SemaphoreType.DMA((n,))]


def _chips_stage_shapes(chip_sums):
    return [jax.ShapeDtypeStruct((2, a.shape[1] // 2, a.shape[2]), a.dtype) for a in chip_sums]


def _chips_ops(ins, outs, stages, send_sems, recv_sems, local_sems, first_chips=None, barrier=False):
    x, y, c = lax.axis_index("x"), lax.axis_index("y"), lax.axis_index("c")
    qm = 2 * x + y
    first_chips = first_chips or [0] * len(ins)

    def owns(j, chip):
        lo, cnt = first_chips[j], ins[j].shape[0]
        if lo == 0 and cnt == 4:
            return None
        return jnp.logical_and(chip >= lo, chip < lo + cnt)

    def guarded(cond, fn):
        if cond is None:
            fn()
        else:
            pl.when(cond)(fn)

    def slot(j, chip):
        return jnp.clip(chip - first_chips[j], 0, ins[j].shape[0] - 1)

    def half(j, i):
        h = ins[j].shape[1] // 2
        return pl.ds(i * h, h)

    def copy(j, sem, src, dst, k):
        return pltpu.make_async_remote_copy(
            src_ref=src, dst_ref=dst, send_sem=send_sems.at[j, sem], recv_sem=recv_sems.at[j, sem],
            device_id=(x ^ (k >> 1), y ^ (k & 1), c), device_id_type=MESH)

    def direct(j, k):
        return copy(j, k - 1, ins[j].at[slot(j, qm ^ k)], outs[j].at[qm], k)

    def first_hop(j, k):
        return copy(j, 1 + k, ins[j].at[slot(j, qm ^ 3), half(j, k - 1)], stages[j].at[k - 1], k)

    def second_hop(j, k):
        return copy(j, 3 + k, stages[j].at[2 - k], outs[j].at[qm ^ (3 - k), half(j, 2 - k)], k)

    def local(j):
        return pltpu.make_async_copy(ins[j].at[slot(j, qm)], outs[j].at[qm], local_sems.at[j])

    def start():
        if barrier:
            _barrier([(x ^ (k >> 1), y ^ (k & 1), c) for k in (1, 2)])
        for j in range(len(ins)):
            for k in (1, 2):
                guarded(owns(j, qm ^ 3), lambda j=j, k=k: first_hop(j, k).start())
        for j in range(len(ins)):
            for k in (1, 2):
                guarded(owns(j, qm ^ k), lambda j=j, k=k: direct(j, k).start())
            guarded(owns(j, qm), lambda j=j: local(j).start())

    def forward():
        for j in range(len(ins)):
            for k in (1, 2):
                def pass_on(j=j, k=k):
                    first_hop(j, 3 - k).wait_recv()
                    second_hop(j, k).start()
                guarded(owns(j, qm ^ k), pass_on)

    def finish():
        for j in range(len(ins)):
            for k in (1, 2):
                guarded(owns(j, qm ^ k), lambda j=j, k=k: direct(j, k).wait_send())
                guarded(owns(j, qm ^ k), lambda j=j, k=k: second_hop(j, k).wait_send())
                guarded(owns(j, qm ^ 3), lambda j=j, k=k: first_hop(j, k).wait_send())
                guarded(owns(j, qm), lambda j=j, k=k: direct(j, k).wait_recv())
                guarded(owns(j, qm), lambda j=j, k=k: second_hop(j, k).wait_recv())
            guarded(owns(j, qm), lambda j=j: local(j).wait())

    return start, forward, finish


ARRIVAL = (0, 1, 2, 4, 3, 5, 6, 7)


def _front_project(xr, c, c_ctx, ada_w, ada_b, ng, w_in, w_out, cw, lam, me):
    nloc = ada_w.shape[1]
    ws = W_IN_SHARD
    order = me[0] ^ jnp.asarray(ARRIVAL, jnp.int32)

    def body(ord_ref, x_ref, c_ref, cc_ref, aw_ref, ab_ref, ng_ref, win_ref, wout_ref, cw_ref, lam_ref,
             z_ref, hn_ref, wfull_ref, woutb_ref, modx_ref, modc_ref, call_ref, cwf_ref, lamf_ref,
             wv, call_s, part_s, parts_s, w_send, w_recv, hbm_sems, s_send, s_recv, g_send, g_recv, g_local):
        t = pl.program_id(0)
        x, y, cidx = lax.axis_index("x"), lax.axis_index("y"), lax.axis_index("c")
        me_i = ord_ref[0]
        sibling = (x, y, 1 - cidx)
        chips = [(x ^ (k >> 1), y ^ (k & 1)) for k in (1, 2, 3)]
        g_start, g_pass, g_finish = _gather2_ops([cw_ref, lam_ref], [cwf_ref, lamf_ref], ["agc", "agc"],
                                                 g_send, g_recv, g_local)

        def shard_copy(k, px, py, pc, to, half=None):
            slot = wv.at[4 * px + 2 * py + pc]
            if half is not None:
                slot = slot.at[pl.ds(half * (D // 2), D // 2), :]
            return pltpu.make_async_remote_copy(src_ref=slot, dst_ref=slot, send_sem=w_send.at[k],
                                                recv_sem=w_recv.at[k], device_id=to, device_id_type=MESH)

        def small_gather(src, my_slot, stage):
            copies = []
            for k in range(1, N_DEV):
                peer = (x ^ (k >> 2), y ^ ((k >> 1) & 1), cidx ^ (k & 1))
                cp = pltpu.make_async_remote_copy(src_ref=src, dst_ref=my_slot, send_sem=s_send.at[stage, k - 1],
                                                  recv_sem=s_recv.at[stage, k - 1], device_id=peer,
                                                  device_id_type=MESH)
                cp.start()
                copies.append(cp)
            pltpu.sync_copy(src, my_slot)
            return copies

        def finish_small(copies):
            for cp in copies:
                cp.wait()

        def to_neighbours(half):
            for i in (0, 1):
                shard_copy(1 + i, x, y, cidx, (*chips[i], cidx), half=half).start()

        @pl.when(t == 0)
        def _():
            _barrier([(x ^ (k >> 2), y ^ ((k >> 1) & 1), cidx ^ (k & 1)) for k in range(1, N_DEV)])
            g_start()
            wv[me_i] = win_ref[...].astype(BF16)
            woutb_ref[...] = wout_ref[...].astype(BF16)
            shard_copy(0, x, y, cidx, sibling).start()
            finish_small(small_gather(c_ref, call_s.at[pl.ds(me_i, 1), :], 0))
            to_neighbours(0)
            call_ref[...] = call_s[...]
            off = pl.multiple_of(me_i * nloc, 128)
            b = ab_ref[:, pl.ds(off, nloc)]
            w = aw_ref[...]
            sx, _ = _silu_and_grad(call_s[...])
            sc, _ = _silu_and_grad(jnp.broadcast_to(cc_ref[...], (8, D)))
            part_s[0:8, :] = _dot(sx, w) + b
            part_s[8:16, :] = _dot(sc, w) + b
            parts_sent = small_gather(part_s, parts_s.at[me_i], 1)
            to_neighbours(1)
            finish_small(parts_sent)
            mine = _rows((16, nloc)) == me_i
            for j in range(N_DEV):
                pj = parts_s[j]
                modx_ref[:, j * nloc:(j + 1) * nloc] = jnp.sum(jnp.where(mine, pj, 0.0), axis=0, keepdims=True)
                modc_ref[:, j * nloc:(j + 1) * nloc] = pj[8:9, :]
            shift, scale1, ngv = modx_ref[:, 0:D], 1.0 + modx_ref[:, D:2 * D], ng_ref[...]
            for r in range(L // ROWS):
                rsl = slice(r * ROWS, (r + 1) * ROWS)
                xv = x_ref[rsl, :]
                rs = lax.rsqrt(jnp.mean(xv * xv, axis=-1, keepdims=True) + NORM_EPS)
                hn_ref[rsl, :] = ((xv * rs * ngv) * scale1 + shift).astype(BF16)

        @pl.when(t == 1)
        def _():
            shard_copy(0, x, y, 1 - cidx, sibling).wait_recv()
            g_pass()

        for i in (0, 1):
            @pl.when(t == ARRIVAL.index((2, 4)[i]))
            def _(i=i):
                shard_copy(1 + i, *chips[i], cidx, sibling).wait_recv()
                shard_copy(4 + i, *chips[i], cidx, sibling).start()
                shard_copy((7, 3)[i], *chips[i], cidx, (*chips[1 - i], cidx), half=i).start()

        @pl.when(t == ARRIVAL.index(6))
        def _():
            shard_copy(3, *chips[2], cidx, sibling, half=1).wait_recv()
            shard_copy(7, *chips[2], cidx, sibling, half=0).wait_recv()
            shard_copy(6, *chips[2], cidx, sibling).start()

        for i in range(3):
            @pl.when(t == ARRIVAL.index((3, 5, 7)[i]))
            def _(i=i):
                shard_copy(4 + i, *chips[i], 1 - cidx, sibling).wait_recv()

        @pl.when(t == 2)
        def _():
            g_finish()

        dev = ord_ref[t]
        for r in range(L // (2 * ROWS)):
            rsl = slice(r * 2 * ROWS, (r + 1) * 2 * ROWS)
            z_ref[rsl, :] = _dot(hn_ref[rsl, :], wv[dev])
        col = pl.ds(pl.multiple_of(dev * ws, 128), ws)
        pltpu.make_async_copy(wv.at[dev], wfull_ref.at[:, col], hbm_sems.at[t]).start()

        @pl.when(t == N_DEV - 1)
        def _():
            for k in (0, 1, 2, 4, 5, 6):
                shard_copy(k, x, y, cidx, sibling).wait_send()
            for k in (3, 7):
                shard_copy(k, x, y, cidx, sibling, half=0).wait_send()
            for s in range(N_DEV):
                pltpu.make_async_copy(wv.at[0], wfull_ref.at[:, pl.ds(0, ws)], hbm_sems.at[s]).wait()

    const = lambda *shape: pl.BlockSpec(shape, lambda t, o: (0,) * len(shape))
    once = lambda *shape: pl.BlockSpec(shape, lambda t, o: (0,) * len(shape), pipeline_mode=pl.Buffered(1))
    return _call(
        body, name="front_project",
        out_shape=[jax.ShapeDtypeStruct((L, D_IN), F32), jax.ShapeDtypeStruct((L, D), BF16),
                   jax.ShapeDtypeStruct((D, D_IN), BF16), jax.ShapeDtypeStruct(w_out.shape, BF16),
                   jax.ShapeDtypeStruct((1, 3 * D), F32), jax.ShapeDtypeStruct((1, 3 * D), F32),
                   jax.ShapeDtypeStruct((N_DEV, D), F32), jax.ShapeDtypeStruct((CONV_W, D), F32),
                   jax.ShapeDtypeStruct((2, D), F32)],
        grid_spec=pltpu.PrefetchScalarGridSpec(
            num_scalar_prefetch=1, grid=(N_DEV,),
            in_specs=[once(L, D), const(1, D), const(1, D), once(D, nloc), const(1, 3 * D), const(1, D),
                      once(D, ws), once(*w_out.shape), HBM, HBM],
            out_specs=[pl.BlockSpec((L, ws), lambda t, o: (0, o[t])), const(L, D), HBM, const(*w_out.shape),
                       const(1, 3 * D), const(1, 3 * D), const(N_DEV, D), HBM, HBM],
            scratch_shapes=[pltpu.VMEM((N_DEV, D, ws), BF16), pltpu.VMEM((N_DEV, D), F32), pltpu.VMEM((16, nloc), F32),
                            pltpu.VMEM((N_DEV, 16, nloc), F32), pltpu.SemaphoreType.DMA((8,)),
                            pltpu.SemaphoreType.DMA((8,)), pltpu.SemaphoreType.DMA((N_DEV,)),
                            pltpu.SemaphoreType.DMA((2, N_DEV - 1)), pltpu.SemaphoreType.DMA((2, N_DEV - 1))]
            + _gather2_sems(2)),
        compiler_params=pltpu.CompilerParams(dimension_semantics=("arbitrary",), vmem_limit_bytes=VMEM_LIMIT,
                                             has_side_effects=True, collective_id=10),
    )(order, xr, c, c_ctx, ada_w, ada_b, ng, w_in, w_out, pltpu.with_memory_space_constraint(cw, pltpu.HBM),
      pltpu.with_memory_space_constraint(lam, pltpu.HBM))


def _project(xr, mod, ng, w, ncols, tm, name):
    rows = xr.shape[0]

    def body(x_ref, sh_ref, sc_ref, ng_ref, w_ref, z_ref, hn_ref):
        x = x_ref[...]
        rs = lax.rsqrt(jnp.mean(x * x, axis=-1, keepdims=True) + NORM_EPS)
        hn = (x * rs * ng_ref[...]) * (1.0 + sc_ref[...]) + sh_ref[...]
        hb = hn.astype(BF16)
        hn_ref[...] = hb
        for n in range(ncols // D):
            z_ref[:, n * D:(n + 1) * D] = _dot(hb, w_ref[:, n * D:(n + 1) * D])

    vec = pl.BlockSpec((1, D), lambda i: (0, 0))
    return _call(
        body, name=name, grid=(rows // tm,),
        out_shape=[jax.ShapeDtypeStruct((rows, ncols), F32), jax.ShapeDtypeStruct((rows, D), BF16)],
        in_specs=[pl.BlockSpec((tm, D), lambda i: (i, 0)), vec, pl.BlockSpec((1, D), lambda i: (0, 1)), vec,
                  pl.BlockSpec((D, ncols), lambda i: (0, 0), pipeline_mode=pl.Buffered(1))],
        out_specs=[pl.BlockSpec((tm, ncols), lambda i: (i, 0)), pl.BlockSpec((tm, D), lambda i: (i, 0))],
        compiler_params=_params("arbitrary"),
    )(xr, mod, mod, ng, w)


def _scan_pair(af_ref, uf_ref, hf_ref, h0f, ab_ref, ub_ref, hb_ref, h0b, t_len):
    span = 8 * SCAN_BLOCKS
    nit = t_len // span
    rows = _rows((8, HD))

    def local_scan(a, b, forward):
        for s in (1, 2, 4):
            sh = s if forward else 8 - s
            m = rows >= s if forward else rows < 8 - s
            b = a * jnp.where(m, pltpu.roll(b, sh, 0), 0.0) + b
            a = a * jnp.where(m, pltpu.roll(a, sh, 0), 1.0)
        return a, b

    def span_scan(a_ref, u_ref, h_ref, off, carry, forward):
        order = range(SCAN_BLOCKS) if forward else range(SCAN_BLOCKS - 1, -1, -1)
        last = slice(7, 8) if forward else slice(0, 1)
        for q in order:
            rs = pl.ds(off + 8 * q, 8)
            a, b = local_scan(a_ref[rs, :], u_ref[rs, :], forward)
            h_ref[rs, :] = b + a * carry
            carry = a[last, :] * carry + b[last, :]
        return carry

    def body(k, carry):
        cf, cb = carry
        cf = span_scan(af_ref, uf_ref, hf_ref, pl.multiple_of(k * span, span), cf, True)
        cb = span_scan(ab_ref, ub_ref, hb_ref, pl.multiple_of((nit - 1 - k) * span, span), cb, False)
        return cf, cb

    return lax.fori_loop(0, nit, body, (h0f, h0b))


SCAN_BLOCKS = 8


def _shifted(pad_ref, x, offsets, before=0.0, after=0.0):
    n = x.shape[0]
    pad_ref[0:8, :] = jnp.broadcast_to(jnp.asarray(before, F32), (8, x.shape[1]))
    pad_ref[8:8 + n, :] = x
    pad_ref[8 + n:16 + n, :] = jnp.broadcast_to(jnp.asarray(after, F32), (8, x.shape[1]))
    return [pad_ref[8 + o:8 + o + n, :] for o in offsets]


def _conv(xa, cw, cb, pad_ref):
    xm1, xp1, xp2 = _shifted(pad_ref, xa, (-1, 1, 2))
    return xm1 * cw[0:1, :] + xa * cw[1:2, :] + xp1 * cw[2:3, :] + xp2 * cw[3:4, :] + cb


def _gates(xc, wa, wx, ba, bx, nsp):
    xb = xc.astype(BF16)
    r = _sigmoid(_dot(xb, wa) + ba)
    i = _sigmoid(_dot(xb, wx) + bx)
    log_a = r * nsp
    a = jnp.exp(log_a)
    g2 = jnp.tanh(log_a) * (-1.0 - a * a)
    rg = lax.rsqrt(jnp.maximum(g2, 1e-30))
    return r, i, a, g2 * rg, rg


def _lru_param_specs():
    h4 = pl.BlockSpec((2, 1, HD, HD), lambda h: (0, h, 0, 0))
    v2 = pl.BlockSpec((2, HD), lambda h: (0, h))
    b16 = pl.BlockSpec((2 * HEADS, HD), lambda h: (0, 0))
    return dict(
        xa=pl.BlockSpec((L, HD), lambda h: (0, h)), xac=pl.BlockSpec((LC, HD), lambda h: (0, h)),
        cw=pl.BlockSpec((CONV_W, HD), lambda h: (0, h)), cb=pl.BlockSpec((1, HD), lambda h: (0, h)), h4=h4, v2=v2,
        b16=b16)


def _bias_row(ref, d):
    mask = _rows((2 * HEADS, HD)) == d * HEADS + pl.program_id(0)
    return jnp.sum(jnp.where(mask, ref[...], 0.0), axis=0, keepdims=True), mask


def _lru_forward(zx, zc, cw, cb, wa, wx, ba, bx, lam, gather, gather_modes):
    ng_ = len(gather)

    def body(xa_ref, xac_ref, cw_ref, cb_ref, wa_ref, wx_ref, ba_ref, bx_ref, lam_ref, *rest):
        yl_ref = rest[ng_]
        af, uf, hf, ab, ub, hb, pad_s = rest[2 * ng_ + 1:2 * ng_ + 8]
        start, pass_on, finish = _gather2_ops(rest[:ng_], rest[ng_ + 1:2 * ng_ + 1], gather_modes,
                                              *rest[2 * ng_ + 8:], barrier=True)
        pl.when(pl.program_id(0) == 0)(start)
        pl.when(pl.program_id(0) == HEADS // 2)(pass_on)
        pl.when(pl.program_id(0) == HEADS - 1)(finish)
        cwv, cbv = cw_ref[...], cb_ref[...]
        nsp = (-LRU_C) * _softplus(-lam_ref[...])

        def forward(xa, t_len, h0f, h0b):
            xc = _conv(xa, cwv, cbv, pad_s)
            for d, (a_ref, u_ref) in enumerate(((af, uf), (ab, ub))):
                _, i, a, gamma, _ = _gates(xc, wa_ref[d, 0].astype(BF16), wx_ref[d, 0].astype(BF16),
                                           _bias_row(ba_ref, d)[0], _bias_row(bx_ref, d)[0], nsp[d:d + 1, :])
                a_ref[0:t_len, :] = a
                u_ref[0:t_len, :] = gamma * (i * xc)
            return _scan_pair(af, uf, hf, h0f, ab, ub, hb, h0b, t_len)

        z = jnp.zeros((1, HD), F32)
        h0f, h0b = forward(xac_ref[...], LC, z, z)
        forward(xa_ref[...], L, h0f, h0b)
        yl_ref[...] = hf[...] + hb[...]

    s = _lru_param_specs()
    return _call(
        body, name="lru_forward", grid=(HEADS,),
        out_shape=[jax.ShapeDtypeStruct((L, D), F32)] + _gather2_shapes(gather, gather_modes),
        in_specs=[s["xa"], s["xac"], s["cw"], s["cb"], s["h4"], s["h4"], s["b16"], s["b16"], s["v2"]] + [HBM] * ng_,
        out_specs=[pl.BlockSpec((L, HD), lambda h: (0, h))] + [HBM] * ng_,
        scratch_shapes=[pltpu.VMEM((L, HD), F32)] * 6 + [pltpu.VMEM((L + 16, HD), F32)] + _gather2_sems(ng_),
        compiler_params=pltpu.CompilerParams(dimension_semantics=("arbitrary",), vmem_limit_bytes=VMEM_LIMIT,
                                             has_side_effects=True, collective_id=5),
    )(zx, zc, cw, cb, wa, wx, ba, bx, lam, *[pltpu.with_memory_space_constraint(a, pltpu.HBM) for a in gather])


def _lru_backward(zx, zc, dyl, dz, cw, cb, wa, wx, ba, bx, lam, chip_sums, first_chips=None):
    nr = len(chip_sums)

    def body(xa_ref, xac_ref, dyl_ref, dz_in, cw_ref, cb_ref, wa_ref, wx_ref, ba_ref, bx_ref, lam_ref, *rest):
        (dxa_ref, dxac_ref, dwa_ref, dwx_ref, dba_ref, dbx_ref, dlam_ref, dcw_ref,
         dcb_ref) = rest[nr:nr + 9]
        main_s, ctx_s, pad_s = rest[3 * nr + 9:3 * nr + 12]
        if nr:
            start, forward, finish = _chips_ops(rest[:nr], rest[nr + 9:2 * nr + 9], rest[2 * nr + 9:3 * nr + 9],
                                                *rest[3 * nr + 12:], first_chips=first_chips, barrier=True)
            pl.when(pl.program_id(0) == 0)(start)
            pl.when(pl.program_id(0) == HEADS // 2)(forward)
            pl.when(pl.program_id(0) == HEADS - 1)(finish)
        del dz_in

        @pl.when(pl.program_id(0) == 0)
        def _():
            dba_ref[...] = jnp.zeros_like(dba_ref)
            dbx_ref[...] = jnp.zeros_like(dbx_ref)

        cwv, cbv = cw_ref[...], cb_ref[...]
        lamv = lam_ref[...]
        sp = _softplus(-lamv)
        nsp = (-LRU_C) * sp
        z = jnp.zeros((1, HD), F32)

        def wmat(ref, d):
            return ref[d, 0].astype(BF16)

        def workspace(s):
            return dict(a=(s.at[0], s.at[1]), u=(s.at[2], s.at[3]), h=(s.at[4], s.at[5]), rho=(s.at[6], s.at[7]),
                        saved=(tuple(s.at[8 + k] for k in range(4)), tuple(s.at[12 + k] for k in range(4))),
                        xc=s.at[16])

        def forward(ws, xa, t_len, h0f, h0b):
            xc = _conv(xa, cwv, cbv, pad_s)
            ws["xc"][...] = xc
            for d in (0, 1):
                vals = _gates(xc, wmat(wa_ref, d), wmat(wx_ref, d), _bias_row(ba_ref, d)[0],
                              _bias_row(bx_ref, d)[0], nsp[d:d + 1, :])
                r, i, a, gamma, rg = vals
                ws["a"][d][...] = a
                ws["u"][d][...] = gamma * (i * xc)
                for ref, val in zip(ws["saved"][d], (r, i, gamma, rg)):
                    ref[...] = val
            return _scan_pair(ws["a"][0], ws["u"][0], ws["h"][0], h0f, ws["a"][1], ws["u"][1], ws["h"][1], h0b,
                              t_len)

        def backward(ws, xa, t_len, h0f, h0b, dhf, dhb, first):
            xc = ws["xc"][...]
            (af, ab), (uf, ub), (hf, hb), (rf, rb) = ws["a"], ws["u"], ws["h"], ws["rho"]
            uf[...] = ab[...] * dhb
            ub[...] = af[...] * dhf
            rho_b_last, rho_f_first = _scan_pair(ab, uf, rb, z, af, ub, rf, z, t_len)
            dxc = jnp.zeros((t_len, HD), F32)
            dsp = []
            for d in (0, 1):
                r, i, gamma, rg = (ref[...] for ref in ws["saved"][d])
                a = ws["a"][d][...]
                if d == 0:
                    lam_t = dhf + _shifted(pad_s, rf[...], (1,))[0]
                    h_prev = _shifted(pad_s, hf[...], (-1,), before=h0f)[0]
                else:
                    lam_t = dhb + _shifted(pad_s, rb[...], (-1,))[0]
                    h_prev = _shifted(pad_s, hb[...], (1,), after=h0b)[0]
                da = lam_t * h_prev
                lx = lam_t * xc
                d_i = lx * gamma
                d_gamma = lx * i
                dxc = dxc + lam_t * (gamma * i)
                d_log_a = a * (da - d_gamma * (a * rg))
                dsp.append(jnp.sum(d_log_a * r, axis=0, keepdims=True) * (-LRU_C))
                d_pre_r = d_log_a * nsp[d:d + 1, :] * (r * (1.0 - r))
                d_pre_i = d_i * (i * (1.0 - i))
                prb, pib, xb = d_pre_r.astype(BF16), d_pre_i.astype(BF16), xc.astype(BF16)
                dxc = dxc + _dot_nt(prb, wmat(wa_ref, d)) + _dot_nt(pib, wmat(wx_ref, d))
                g_wa, g_wx = _dot_tn(xb, prb), _dot_tn(xb, pib)
                g_ba = jnp.sum(d_pre_r, axis=0, keepdims=True)
                g_bx = jnp.sum(d_pre_i, axis=0, keepdims=True)
                mask = _bias_row(ba_ref, d)[1]
                dba_ref[...] += jnp.where(mask, g_ba, 0.0)
                dbx_ref[...] += jnp.where(mask, g_bx, 0.0)
                if first:
                    dwa_ref[d, 0] = g_wa
                    dwx_ref[d, 0] = g_wx
                else:
                    dwa_ref[d, 0] += g_wa
                    dwx_ref[d, 0] += g_wx
            g_lam = jnp.concatenate(dsp, axis=0) * (-_sigmoid(-lamv))
            dm1, dp1, dm2 = _shifted(pad_s, dxc, (-1, 1, -2))
            dxa = dp1 * cwv[0:1, :] + dxc * cwv[1:2, :] + dm1 * cwv[2:3, :] + dm2 * cwv[3:4, :]
            xm1, xp1, xp2 = _shifted(pad_s, xa, (-1, 1, 2))
            g_cw = jnp.concatenate([jnp.sum(dxc * v, axis=0, keepdims=True) for v in (xm1, xa, xp1, xp2)], axis=0)
            g_cb = jnp.sum(dxc, axis=0, keepdims=True)
            if first:
                dlam_ref[...] = g_lam
                dcw_ref[...] = g_cw
                dcb_ref[...] = g_cb
            else:
                dlam_ref[...] += g_lam
                dcw_ref[...] += g_cw
                dcb_ref[...] += g_cb
            return dxa, rho_f_first, rho_b_last

        ws_x, ws_c = workspace(main_s), workspace(ctx_s)
        h0f, h0b = forward(ws_c, xac_ref[...], LC, z, z)
        forward(ws_x, xa_ref[...], L, h0f, h0b)
        dh = dyl_ref[...]
        dxa, dh0f, dh0b = backward(ws_x, xa_ref[...], L, h0f, h0b, dh, dh, True)
        dxa_ref[...] = dxa.astype(BF16)
        rc = _rows((LC, HD))
        dxac, _, _ = backward(ws_c, xac_ref[...], LC, z, z, jnp.where(rc == LC - 1, dh0f, 0.0),
                              jnp.where(rc == 0, dh0b, 0.0), False)
        dxac_ref[...] = dxac.astype(BF16)

    s = _lru_param_specs()
    col = lambda r: pl.BlockSpec((r, HD), lambda h: (0, h))
    return _call(
        body, name="lru_backward", grid=(HEADS,),
        out_shape=[jax.ShapeDtypeStruct((L, D_IN), BF16), jax.ShapeDtypeStruct((LC, D), BF16),
                   jax.ShapeDtypeStruct((2, HEADS, HD, HD), F32), jax.ShapeDtypeStruct((2, HEADS, HD, HD), F32),
                   jax.ShapeDtypeStruct((2 * HEADS, HD), F32), jax.ShapeDtypeStruct((2 * HEADS, HD), F32),
                   jax.ShapeDtypeStruct((2, D), F32), jax.ShapeDtypeStruct((CONV_W, D), F32),
                   jax.ShapeDtypeStruct((1, D), F32)] + [jax.ShapeDtypeStruct((4,) + a.shape[1:], a.dtype)
                                                          for a in chip_sums] + _chips_stage_shapes(chip_sums),
        in_specs=[s["xa"], s["xac"], col(L), pl.BlockSpec(memory_space=pl.ANY), s["cw"], s["cb"], s["h4"], s["h4"],
                  s["b16"], s["b16"], s["v2"]] + [HBM] * nr,
        out_specs=[col(L), col(LC), s["h4"], s["h4"], s["b16"], s["b16"], s["v2"], col(CONV_W), col(1)]
        + [HBM] * (2 * nr),
        scratch_shapes=[pltpu.VMEM((17, L, HD), F32), pltpu.VMEM((17, LC, HD), F32), pltpu.VMEM((L + 16, HD), F32)]
        + (_chips_sems(nr) if nr else []),
        input_output_aliases={3: 0},
        compiler_params=pltpu.CompilerParams(dimension_semantics=("arbitrary",), vmem_limit_bytes=VMEM_LIMIT,
                                             has_side_effects=True, collective_id=6 if nr else None),
    )(zx, zc, dyl, dz, cw, cb, wa, wx, ba, bx, lam, *[pltpu.with_memory_space_constraint(a, pltpu.HBM)
                                                       for a in chip_sums])


def _mixer_loss(x, tgt, zx, yl, gx, fg, lng, lnb, ws, wst, bst, wout, tm):
    ncht = tm // CHUNK

    def body(x_ref, t_ref, ga_ref, u_ref, v_ref, gb_ref, yl_ref, gx_ref, fg_ref, lng_ref, lnb_ref, ws_ref, wst_ref,
             bst_ref, wout_ref,
             dz_ref, dyl_ref, dxn_ref, y_s, do_ref, dws_ref, dbst_ref, vec_ref,
             vn_s, mix_s, dm_s, dvn_s):
        step = pl.program_id(0)

        @pl.when(step == 0)
        def _():
            dws_ref[...] = jnp.zeros_like(dws_ref)
            dbst_ref[...] = jnp.zeros_like(dbst_ref)
            vec_ref[...] = jnp.zeros_like(vec_ref)

        u, v = u_ref[...], v_ref[...]
        ug, dug_du = _gelu_and_grad(u)
        vg, dvg_dv = _gelu_and_grad(v)
        mu = jnp.mean(vg, axis=-1, keepdims=True)
        vc = vg - mu
        rstd = lax.rsqrt(jnp.mean(vc * vc, axis=-1, keepdims=True) + LN_EPS)
        vhat = vc * rstd
        lngv = lng_ref[...]
        vn_s[...] = (vhat * lngv + lnb_ref[...]).astype(BF16)
        for ch in range(ncht):
            rs = slice(ch * CHUNK, (ch + 1) * CHUNK)
            for g in range(HEADS):
                cs = slice(g * HD, (g + 1) * HD)
                mix_s[rs, cs] = _dot(ws_ref[g], vn_s[rs, cs]) + bst_ref[:, g:g + 1]
        mixed = mix_s[...]
        ga, gb, yl = ga_ref[...], gb_ref[...], yl_ref[...]
        sga, dsga = _silu_and_grad(ga)
        sgb, dsgb = _silu_and_grad(gb)
        ys = ug * mixed
        y_s[:, 0:D] = (yl * sga).astype(BF16)
        y_s[:, D:D_MIX] = (ys * sgb).astype(BF16)
        o = _dot(y_s[...], wout_ref[...])
        gxv, fgv = gx_ref[...], fg_ref[...]
        xn = x_ref[...] + gxv * o
        rs2 = lax.rsqrt(jnp.mean(xn * xn, axis=-1, keepdims=True) + NORM_EPS)
        xh = xn * rs2
        diff = xh * fgv - t_ref[...]
        vec_ref[R_LOSS:R_LOSS + 1, :] += jnp.full((1, D), jnp.sum(diff * diff) * (0.5 / D), F32)
        dout = diff * (1.0 / D)
        w = dout * fgv
        dxn = rs2 * (w - xh * jnp.mean(w * xh, axis=-1, keepdims=True))
        dxn_ref[...] = dxn
        vec_ref[0:1, :] += jnp.sum(dxn * o, axis=0, keepdims=True)
        vec_ref[1:2, :] += jnp.sum(dout * xh, axis=0, keepdims=True)
        dob = (dxn * gxv).astype(BF16)
        do_ref[...] = dob
        dy = _dot_nt(dob, wout_ref[...])
        dya, dyb = dy[:, 0:D], dy[:, D:D_MIX]
        dyl_ref[...] = dya * sga
        dys = dyb * sgb
        dz_ref[:, 0:D] = jnp.zeros((tm, D), BF16)
        dz_ref[:, D:2 * D] = (dya * yl * dsga).astype(BF16)
        dz_ref[:, 2 * D:3 * D] = (dys * mixed * dug_du).astype(BF16)
        dz_ref[:, 4 * D:5 * D] = (dyb * ys * dsgb).astype(BF16)
        dm = dys * ug
        dm_s[...] = dm.astype(BF16)
        for g in range(HEADS):
            cs = slice(g * HD, (g + 1) * HD)
            dbst_ref[:, g:g + 1] += sum(jnp.sum(dm[ch * CHUNK:(ch + 1) * CHUNK, cs], axis=1, keepdims=True)
                                        for ch in range(ncht))
            for ch in range(ncht):
                rs = slice(ch * CHUNK, (ch + 1) * CHUNK)
                dws_ref[g] += _dot_nt(dm_s[rs, cs], vn_s[rs, cs])
                dvn_s[rs, cs] = _dot(wst_ref[g], dm_s[rs, cs])
        dvn = dvn_s[...]
        vec_ref[2:3, :] += jnp.sum(dvn * vhat, axis=0, keepdims=True)
        vec_ref[3:4, :] += jnp.sum(dvn, axis=0, keepdims=True)
        dvh = dvn * lngv
        dvg = rstd * (dvh - jnp.mean(dvh, axis=-1, keepdims=True) - vhat * jnp.mean(dvh * vhat, axis=-1, keepdims=True))
        dz_ref[:, 3 * D:4 * D] = (dvg * dvg_dv).astype(BF16)

    tile = pl.BlockSpec((tm, D), lambda i: (i, 0))
    zcol = lambda n: pl.BlockSpec((tm, D), lambda i: (i, n))
    vec = pl.BlockSpec((1, D), lambda i: (0, 0))
    full = lambda *s: pl.BlockSpec(s, lambda i: (0,) * len(s))
    return _call(
        body, name="mixer_loss", grid=(L // tm,),
        out_shape=[jax.ShapeDtypeStruct((L, D_IN), BF16), jax.ShapeDtypeStruct((L, D), F32),
                   jax.ShapeDtypeStruct((L, D), F32), jax.ShapeDtypeStruct((L, D_MIX), BF16),
                   jax.ShapeDtypeStruct((L, D), BF16),
                   jax.ShapeDtypeStruct((HEADS, CHUNK, CHUNK), F32), jax.ShapeDtypeStruct((CHUNK, HEADS), F32),
                   jax.ShapeDtypeStruct((8, D), F32)],
        in_specs=[tile, tile, zcol(1), zcol(2), zcol(3), zcol(4), tile, pl.BlockSpec((1, D), lambda i: (0, 2)),
                  vec, vec, vec,
                  full(HEADS, CHUNK, CHUNK), full(HEADS, CHUNK, CHUNK), full(CHUNK, HEADS),
                  pl.BlockSpec((D_MIX, D), lambda i: (0, 0), pipeline_mode=pl.Buffered(1))],
        out_specs=[pl.BlockSpec((tm, D_IN), lambda i: (i, 0)), tile, tile,
                   pl.BlockSpec((tm, D_MIX), lambda i: (i, 0)), tile,
                   full(HEADS, CHUNK, CHUNK), full(CHUNK, HEADS), full(8, D)],
        scratch_shapes=[pltpu.VMEM((tm, D), BF16), pltpu.VMEM((tm, D), F32),
                        pltpu.VMEM((tm, D), BF16), pltpu.VMEM((tm, D), F32)],
        compiler_params=_params("arbitrary"),
    )(x, tgt, zx, zx, zx, zx, yl, gx, fg, lng, lnb, ws, wst, bst, wout)


def _grad_w(a, b, a2, b2, tk, name, bw, first, nblocks, split, barrier_id, chip_sums=()):
    nk = a.shape[0] // tk
    m = a.shape[1]
    with_ctx = a2 is not None
    if split == "cols":
        slots, r, w = nblocks, m, bw // 2
        piece = lambda q, pc: (slice(None), slice(pc * w, (pc + 1) * w))
    else:
        slots, r, w = 4, m // 8, bw
        piece = lambda q, pc: (slice((2 * q + pc) * r, (2 * q + pc + 1) * r), slice(None))

    nr = len(chip_sums)

    def body(*refs):
        a_ref, b_ref = refs[:2]
        a2_ref, b2_ref = refs[2:4] if with_ctx else (None, None)
        base = 4 if with_ctx else 2
        sums_ref = refs[base + nr]
        acc, mine_v, send_v, stage_v, send_sems, recv_sems = refs[base + 3 * nr + 1:base + 3 * nr + 7]
        n, k = pl.program_id(0), pl.program_id(1)
        x, y, c = lax.axis_index("x"), lax.axis_index("y"), lax.axis_index("c")

        def to_sibling(s):
            return pltpu.make_async_remote_copy(src_ref=send_v.at[s], dst_ref=stage_v.at[s], send_sem=send_sems.at[s],
                                                recv_sem=recv_sems.at[s], device_id=(x, y, 1 - c),
                                                device_id_type=MESH)

        if nr:
            c_start, c_forward, c_finish = _chips_ops(refs[base:base + nr], refs[base + nr + 1:base + 2 * nr + 1],
                                                      refs[base + 2 * nr + 1:base + 3 * nr + 1],
                                                      *refs[base + 3 * nr + 7:])

            @pl.when(jnp.logical_and(n == 0, k == 0))
            def _():
                _barrier([(x, y, 1 - c)] + [(x ^ (j >> 1), y ^ (j & 1), c) for j in (1, 2)])
                c_start()
        else:
            pl.when(jnp.logical_and(n == 0, k == 0))(_sibling_barrier)

        @pl.when(k == 0)
        def _():
            acc[...] = _dot_tn(a_ref[...], b_ref[...])

        if nk > 1:
            @pl.when(k > 0)
            def _():
                acc[...] += _dot_tn(a_ref[...], b_ref[...])

        if with_ctx:
            @pl.when(jnp.logical_and(k == nk - 1, n == 0))
            def _():
                acc[:, 0:b2_ref.shape[1]] += _dot_tn(a2_ref[...], b2_ref[...])

        if nr:
            pl.when(jnp.logical_and(k == nk - 1, n == nblocks - 1))(c_forward)

        def hand_over(s, q):
            for pc in (0, 1):
                @pl.when(c == pc)
                def _(pc=pc):
                    mine_v[s] = acc[piece(q, pc)]
                    send_v[s] = acc[piece(q, 1 - pc)].astype(BF16)
            to_sibling(s).start()

        for i in range(nblocks):
            @pl.when(jnp.logical_and(k == nk - 1, n == i))
            def _(i=i):
                if split == "cols":
                    hand_over(i, 0)
                else:
                    for q in range(4):
                        hand_over(q, q)

        @pl.when(jnp.logical_and(k == nk - 1, n == nblocks - 1))
        def _():
            for s in range(slots):
                to_sibling(s).wait_recv()
                sums_ref[s] = (mine_v[s] + stage_v[s].astype(F32)).astype(BF16)
            for s in range(slots):
                to_sibling(s).wait_send()
            if nr:
                c_finish()

    in_specs = [pl.BlockSpec((tk, m), lambda n, k: (k, 0)), pl.BlockSpec((tk, bw), lambda n, k: (k, n + first))]
    args = [a, b]
    if with_ctx:
        in_specs += [pl.BlockSpec(a2.shape, lambda n, k: (0, 0)), pl.BlockSpec(b2.shape, lambda n, k: (0, 0))]
        args += [a2, b2]
    in_specs += [HBM] * nr
    args += [pltpu.with_memory_space_constraint(s, pltpu.HBM) for s in chip_sums]
    return _call(
        body, name=name, grid=(nblocks, nk),
        out_shape=[jax.ShapeDtypeStruct((slots, r, w), BF16)]
        + [jax.ShapeDtypeStruct((4,) + s.shape[1:], s.dtype) for s in chip_sums] + _chips_stage_shapes(chip_sums),
        in_specs=in_specs, out_specs=[pl.BlockSpec((slots, r, w), lambda n, k: (0, 0, 0))] + [HBM] * (2 * nr),
        scratch_shapes=[pltpu.VMEM((m, bw), F32), pltpu.VMEM((slots, r, w), F32), pltpu.VMEM((slots, r, w), BF16),
                        pltpu.VMEM((slots, r, w), BF16), pltpu.SemaphoreType.DMA((slots,)),
                        pltpu.SemaphoreType.DMA((slots,))] + (_chips_sems(nr) if nr else []),
        compiler_params=pltpu.CompilerParams(dimension_semantics=("arbitrary", "arbitrary"),
                                             vmem_limit_bytes=VMEM_LIMIT, has_side_effects=True,
                                             collective_id=barrier_id),
    )(*args)


def _grad_rows(xr, dz, w, mod, ng, dres, ncols, tm, name, chip_sums=(), first_chips=None, dests=None):
    rows = xr.shape[0]
    steps = rows // tm
    with_dx = dres is not None
    nr = len(chip_sums)
    dests = [d for d in (dests or [None] * nr)]
    nd = sum(d is not None for d in dests)
    nin = 6 if with_dx else 5
    nout = 2 if with_dx else 1

    def body(*refs):
        if with_dx:
            x_ref, dz_ref, w_ref, sc_ref, ng_ref, dres_ref = refs[:nin]
            dx_ref, vec_ref = refs[nin + nr + nd:nin + nr + nd + nout]
        else:
            x_ref, dz_ref, w_ref, sc_ref, ng_ref = refs[:nin]
            (vec_ref,) = refs[nin + nr + nd:nin + nr + nd + nout]
        if nr:
            o0 = nin + nr + nd + nout
            start, forward, finish = _chips_ops(refs[nin:nin + nr], refs[o0:o0 + nr], refs[o0 + nr:o0 + 2 * nr],
                                                *refs[o0 + 2 * nr:], first_chips=first_chips, barrier=True)
            pl.when(pl.program_id(0) == 0)(start)
            pl.when(pl.program_id(0) == steps // 2)(forward)
            pl.when(pl.program_id(0) == steps - 1)(finish)

        @pl.when(pl.program_id(0) == 0)
        def _():
            vec_ref[...] = jnp.zeros_like(vec_ref)

        dhn = _dot_nt(dz_ref[...], w_ref[...])
        x = x_ref[...]
        rs = lax.rsqrt(jnp.mean(x * x, axis=-1, keepdims=True) + NORM_EPS)
        xh = x * rs
        ngv = ng_ref[...]
        y = xh * ngv
        vec_ref[0:1, :] += jnp.sum(dhn, axis=0, keepdims=True)
        vec_ref[1:2, :] += jnp.sum(dhn * y, axis=0, keepdims=True)
        dy = dhn * (1.0 + sc_ref[...])
        vec_ref[2:3, :] += jnp.sum(dy * xh, axis=0, keepdims=True)
        if with_dx:
            dxh = dy * ngv
            dx_ref[...] = dres_ref[...] + rs * (dxh - xh * jnp.mean(dxh * xh, axis=-1, keepdims=True))

    tile = pl.BlockSpec((tm, D), lambda i: (i, 0))
    vec = pl.BlockSpec((1, D), lambda i: (0, 0))
    in_specs = [tile, pl.BlockSpec((tm, ncols), lambda i: (i, 0)),
                pl.BlockSpec((D, ncols), lambda i: (0, 0), pipeline_mode=pl.Buffered(1)),
                pl.BlockSpec((1, D), lambda i: (0, 1)), vec]
    out_shape = [jax.ShapeDtypeStruct((8, D), F32)]
    out_specs = [pl.BlockSpec((8, D), lambda i: (0, 0))]
    args = [xr, dz, w, mod, ng]
    if with_dx:
        in_specs.append(tile)
        out_shape.insert(0, jax.ShapeDtypeStruct((rows, D), F32))
        out_specs.insert(0, tile)
        args.append(dres)
    aliases = {}
    for j, d in enumerate(dests):
        if d is not None:
            aliases[len(args) + nr + len(aliases)] = len(out_shape) + j
    in_specs += [HBM] * (nr + nd)
    out_specs += [HBM] * (2 * nr)
    out_shape += [jax.ShapeDtypeStruct((4,) + a.shape[1:], a.dtype) for a in chip_sums]
    out_shape += _chips_stage_shapes(chip_sums)
    args += [pltpu.with_memory_space_constraint(a, pltpu.HBM) for a in chip_sums]
    args += [pltpu.with_memory_space_constraint(d, pltpu.HBM) for d in dests if d is not None]
    return _call(body, name=name, grid=(steps,), out_shape=out_shape, in_specs=in_specs, out_specs=out_specs,
                 scratch_shapes=_chips_sems(nr) if nr else [], input_output_aliases=aliases,
                 compiler_params=pltpu.CompilerParams(dimension_semantics=("arbitrary",),
                                                      vmem_limit_bytes=VMEM_LIMIT, has_side_effects=bool(nr),
                                                      collective_id=7 if nr else None))(*args)


def _adamw(w, g, m, v):
    m = ADAM_B1 * m + (1.0 - ADAM_B1) * g
    v = ADAM_B2 * v + (1.0 - ADAM_B2) * (g * g)
    m_hat = m / (1.0 - ADAM_B1 ** ADAM_STEP)
    v_hat = v / (1.0 - ADAM_B2 ** ADAM_STEP)
    delta = -ADAM_LR * (m_hat / (jnp.sqrt(v_hat) + ADAM_EPS) + ADAM_WD * w)
    return delta, m, v


def _adamw_reduced(parts, w, m, v, tr, name):
    r, n = w.shape
    nparts = parts.shape[0]

    def body(p_ref, w_ref, m_ref, v_ref, g_ref, d_ref, mo_ref, vo_ref):
        g = p_ref[0].astype(F32)
        for i in range(1, nparts):
            g = g + p_ref[i].astype(F32)
        g_ref[...] = g
        d_ref[...], mo_ref[...], vo_ref[...] = _adamw(w_ref[...], g, m_ref[...], v_ref[...])

    tile = pl.BlockSpec((tr, n), lambda i: (i, 0))
    sds = jax.ShapeDtypeStruct((r, n), F32)
    return _call(
        body, name=name, grid=(r // tr,), out_shape=[sds] * 4,
        in_specs=[pl.BlockSpec((nparts, tr, n), lambda i: (0, i, 0)), tile, tile, tile], out_specs=[tile] * 4,
        compiler_params=_params("arbitrary"),
    )(parts, w, m, v)


R_GATE, R_FINAL_G, R_LN_G, R_LN_B, R_LOSS = 0, 1, 2, 3, 4
R_SH_X, R_SC_X, R_NG_X = 5, 6, 7
R_SH_C, R_SC_C, R_NG_C = 8, 9, 10
R_LAM, R_CW, R_CB = 11, 13, 17
PACK_ROWS = 24
Q_BA, Q_BX, Q_SGU_B, PACK128_ROWS = 0, 16, 32, 40


def _reduce_small(pack, pack128, mat_parts, ada_w, me):
    nloc = ada_w.shape[1]
    nm = len(mat_parts)

    def body(me_ref, pack_ref, packq_ref, *refs):
        mp_refs, w_ref = refs[:nm], refs[nm]
        red_ref, redq_ref = refs[nm + 1:nm + 3]
        mats_all = refs[nm + 3:2 * nm + 3]
        cparts_ref, dmod_ref, gab_ref = refs[2 * nm + 3:2 * nm + 6]
        vp_ref, vq_ref = refs[2 * nm + 6:2 * nm + 8]
        mat_refs = refs[2 * nm + 8:3 * nm + 8]
        cpart_ref, dmc_s = refs[3 * nm + 8:3 * nm + 10]
        sems = refs[3 * nm + 10:]
        g1 = _gather2_ops([pack_ref, packq_ref], [vp_ref, vq_ref], ["ag", "ag"], *sems[:3], barrier=True)
        g2 = _gather2_ops([*mat_refs, cpart_ref], [*mats_all, cparts_ref], ["ag"] * (nm + 1), *sems[3:])
        for stage in g1:
            stage()
        red, redq = vp_ref[0], vq_ref[0]
        for i in range(1, N_DEV):
            red = red + vp_ref[i]
            redq = redq + vq_ref[i]
        red_ref[...] = red
        redq_ref[...] = redq
        for mp_ref, mat_ref in zip(mp_refs, mat_refs):
            mat = mp_ref[0].astype(F32)
            for i in range(1, mp_ref.shape[0]):
                mat = mat + mp_ref[i].astype(F32)
            mat_ref[...] = mat
        for e in range(N_DEV):
            dmod_ref[e:e + 1, 0:D] = vp_ref[e, R_SH_X:R_SH_X + 1, :]
            dmod_ref[e:e + 1, D:2 * D] = vp_ref[e, R_SC_X:R_SC_X + 1, :]
            dmod_ref[e:e + 1, 2 * D:3 * D] = vp_ref[e, R_GATE:R_GATE + 1, :]
        dmod_ref[8:9, 0:D] = red[R_SH_C:R_SH_C + 1, :]
        dmod_ref[8:9, D:2 * D] = red[R_SC_C:R_SC_C + 1, :]
        dmod_ref[8:9, 2 * D:3 * D] = jnp.zeros((1, D), F32)
        dmod_ref[9:16, :] = jnp.zeros((7, 3 * D), F32)
        gab_ref[:, 0:D] = red[R_SH_X:R_SH_X + 1, :] + red[R_SH_C:R_SH_C + 1, :]
        gab_ref[:, D:2 * D] = red[R_SC_X:R_SC_X + 1, :] + red[R_SC_C:R_SC_C + 1, :]
        gab_ref[:, 2 * D:3 * D] = red[R_GATE:R_GATE + 1, :]
        dmc_s[...] = jnp.broadcast_to(dmod_ref[8:9, :], (8, 3 * D))
        off = pl.multiple_of(me_ref[0] * nloc, 128)
        cpart_ref[...] = _dot_nt(dmc_s[:, pl.ds(off, nloc)], w_ref[...])
        for stage in g2:
            stage()

    return _call(
        body, name="reduce_small",
        out_shape=[jax.ShapeDtypeStruct((PACK_ROWS, D), F32), jax.ShapeDtypeStruct((PACK128_ROWS, HD), F32)]
        + [jax.ShapeDtypeStruct((N_DEV,) + p.shape[1:], F32) for p in mat_parts]
        + [jax.ShapeDtypeStruct((N_DEV, 8, D), F32), jax.ShapeDtypeStruct((16, 3 * D), F32),
           jax.ShapeDtypeStruct((1, 3 * D), F32)],
        in_specs=[pl.BlockSpec(memory_space=pltpu.SMEM)] + [VMEM] * (nm + 3), out_specs=[VMEM] * (nm + 5),
        scratch_shapes=[pltpu.VMEM((N_DEV, PACK_ROWS, D), F32), pltpu.VMEM((N_DEV, PACK128_ROWS, HD), F32)]
        + [pltpu.VMEM(p.shape[1:], F32) for p in mat_parts]
        + [pltpu.VMEM((8, D), F32), pltpu.VMEM((8, 3 * D), F32)] + _gather2_sems(2) + _gather2_sems(nm + 1),
        compiler_params=pltpu.CompilerParams(vmem_limit_bytes=VMEM_LIMIT, has_side_effects=True, collective_id=8),
    )(me, pack, pack128, *mat_parts, ada_w)


def _adamw_ada(c_all, c_ctx, dmod, w, m, v, me):
    nloc = w.shape[1]

    def body(me_ref, c_ref, cc_ref, dm_ref, w_ref, m_ref, v_ref, g_ref, d_ref, mo_ref, vo_ref):
        off = pl.multiple_of(me_ref[0] * nloc, 128)
        dm = dm_ref[:, pl.ds(off, nloc)]
        sx, _ = _silu_and_grad(c_ref[...])
        sc, _ = _silu_and_grad(cc_ref[...])
        g = _dot_tn(sx, dm[0:8, :]) + _dot_tn(jnp.broadcast_to(sc, (8, D)), dm[8:16, :])
        g_ref[...] = g
        d_ref[...], mo_ref[...], vo_ref[...] = _adamw(w_ref[...], g, m_ref[...], v_ref[...])

    sds = jax.ShapeDtypeStruct(w.shape, F32)
    return _call(
        body, name="adamw_ada_w", out_shape=[sds] * 4,
        in_specs=[pl.BlockSpec(memory_space=pltpu.SMEM)] + [VMEM] * 6, out_specs=[VMEM] * 4,
        compiler_params=_params(),
    )(me, c_all, c_ctx, dmod, w, m, v)


_SMALL = ("c_ctx", "ada_b", "norm_g", "conv_w", "conv_b", "lru_wa", "lru_ba", "lru_wx", "lru_bx", "lru_lambda",
          "sgu_ln_g", "sgu_ln_b", "sgu_w", "sgu_b", "final_g")


def _adamw_small(red, redq, mats, cparts, gab, ws, ms, vs, me):
    n = len(_SMALL)

    def body(me_ref, red_ref, redq_ref, wa_ref, wx_ref, sw_ref, cp_ref, gab_ref, *refs):
        w_refs, m_refs, v_refs = refs[:n], refs[n:2 * n], refs[2 * n:3 * n]
        outs = refs[3 * n:]
        off = pl.multiple_of(me_ref[0] * HD, 128)

        def row(r, k=1):
            return red_ref[r:r + k, :]

        cc = w_refs[0][...]
        dcc = cp_ref[0, 0:1, :]
        for i in range(1, N_DEV):
            dcc = dcc + cp_ref[i, 0:1, :]
        grads = dict(
            c_ctx=dcc * _silu_and_grad(cc)[1], ada_b=gab_ref[...], norm_g=row(R_NG_X) + row(R_NG_C),
            conv_w=red_ref[R_CW:R_CW + CONV_W, pl.ds(off, HD)], conv_b=row(R_CB),
            lru_wa=wa_ref[...], lru_ba=redq_ref[Q_BA:Q_BA + 2 * HEADS, :], lru_wx=wx_ref[...],
            lru_bx=redq_ref[Q_BX:Q_BX + 2 * HEADS, :], lru_lambda=red_ref[R_LAM:R_LAM + 2, pl.ds(off, HD)],
            sgu_ln_g=row(R_LN_G), sgu_ln_b=row(R_LN_B), sgu_w=sw_ref[...],
            sgu_b=redq_ref[Q_SGU_B:Q_SGU_B + HEADS, :], final_g=row(R_FINAL_G))
        for j, name in enumerate(_SMALL):
            g = grads[name]
            outs[j][...] = g
            outs[n + j][...], outs[2 * n + j][...], outs[3 * n + j][...] = _adamw(w_refs[j][...], g, m_refs[j][...],
                                                                                 v_refs[j][...])

    sds = [jax.ShapeDtypeStruct(ws[k].shape, F32) for k in _SMALL]
    outs = _call(
        body, name="adamw_small", out_shape=sds * 4,
        in_specs=[pl.BlockSpec(memory_space=pltpu.SMEM)] + [VMEM] * (7 + 3 * n), out_specs=[VMEM] * (4 * n),
        compiler_params=_params(),
    )(me, red, redq, *mats, cparts, gab, *[ws[k] for k in _SMALL], *[ms[k] for k in _SMALL],
      *[vs[k] for k in _SMALL])
    return [dict(zip(_SMALL, outs[i * n:(i + 1) * n])) for i in range(4)]


def kernel(x, c, ctx, c_ctx, ada_w, ada_b, norm_g, w_in, conv_w, conv_b, lru_wa, lru_ba, lru_wx, lru_bx, lru_lambda, sgu_ln_g, sgu_ln_b, sgu_w, sgu_b, w_out, final_g, loss_target, m_c_ctx, m_ada_w, m_ada_b, m_norm_g, m_w_in, m_conv_w, m_conv_b, m_lru_wa, m_lru_ba, m_lru_wx, m_lru_bx, m_lru_lambda, m_sgu_ln_g, m_sgu_ln_b, m_sgu_w, m_sgu_b, m_w_out, m_final_g, v_c_ctx, v_ada_w, v_ada_b, v_norm_g, v_w_in, v_conv_w, v_conv_b, v_lru_wa, v_lru_ba, v_lru_wx, v_lru_bx, v_lru_lambda, v_sgu_ln_g, v_sgu_ln_b, v_sgu_w, v_sgu_b, v_w_out, v_final_g):
    args = dict(locals())
    me = (4 * lax.axis_index("x") + 2 * lax.axis_index("y") + lax.axis_index("c")).astype(jnp.int32).reshape(1)
    xr, ctxr, tgt = x[0], ctx[0], loss_target[0]
    cc = c_ctx.reshape(1, D)
    nw = 2 * HEADS * HD
    view = dict(c_ctx=(1, D), ada_b=(1, 3 * D), norm_g=(1, D), conv_w=(CONV_W, HD), conv_b=(1, D), lru_wa=(nw, HD),
                lru_ba=(2 * HEADS, HD), lru_wx=(nw, HD), lru_bx=(2 * HEADS, HD), lru_lambda=(2, HD), sgu_ln_g=(1, D),
                sgu_ln_b=(1, D), sgu_w=(HEADS * CHUNK, CHUNK), sgu_b=(HEADS, CHUNK), final_g=(1, D))

    zx, hn, w_full, w_out_b, modx, modc, c_all, cw_full, lam_full = _front_project(
        xr, c, cc, ada_w[0], ada_b, norm_g, w_in[0], w_out[0], conv_w[0], lru_lambda[0], me)
    zc, hnc = _project(ctxr, modc, norm_g, w_full, D, LC, "project_ctx")
    ba, bx = lru_ba.reshape(view["lru_ba"]), lru_bx.reshape(view["lru_bx"])
    yl, wout_all = _lru_forward(zx, zc, cw_full, conv_b, lru_wa[0], lru_wx[0], ba, bx, lam_full, [w_out_b], ["ag"])
    wout_full = wout_all.reshape(D_MIX, D)
    ws_b = sgu_w[0].astype(BF16)
    dz, dyl, dxn, ycat, dob, dws, dbst, mvec = _mixer_loss(
        xr, tgt, zx, yl, modx, final_g.reshape(1, D), sgu_ln_g, sgu_ln_b, ws_b, jnp.swapaxes(ws_b, 1, 2),
        sgu_b[0].T, wout_full, ROWS)

    (wout_sums,) = _grad_w(ycat, dob, None, None, L, "grad_w_out", D, 0, 1, "rows", 1)
    (rest_sums,) = _grad_w(hn, dz, None, None, L, "grad_w_in_rest", 2 * W_IN_SHARD, 1, 3, "cols", 2)
    dz, dxac, dwa, dwx, dba, dbx, dlam, dcw, dcb, win_parts, wout_parts, _, _ = _lru_backward(
        zx, zc, dyl, dz, cw_full, conv_b, lru_wa[0], lru_wx[0], ba, bx, lam_full, [rest_sums, wout_sums],
        first_chips=[1, 0])
    mats = [dwa.reshape(N_DEV, nw // N_DEV, HD), dwx.reshape(N_DEV, nw // N_DEV, HD), dws]
    mat_sums = _reduce2_local(mats, me, "reduce_mat", 4, BF16)
    first_sums, *mat_parts = _grad_w(hn, dz, hnc, dxac, L, "grad_w_in_first", 2 * W_IN_SHARD, 0, 1, "cols", 3,
                                     chip_sums=mat_sums)[:4]
    gx, xvec, win_parts = _grad_rows(
        xr, dz, w_full, modx, norm_g, dxn, D_IN, ROWS, "grad_rows_x", chip_sums=[first_sums], first_chips=[0],
        dests=[win_parts])[:3]
    (cvec,) = _grad_rows(ctxr, dxac, w_full, modc, norm_g, None, D, LC, "grad_rows_ctx")
    pack = jnp.concatenate([mvec[0:5], xvec[0:3], cvec[0:3], dlam, dcw, dcb,
                            jnp.zeros((PACK_ROWS - R_CB - 1, D), F32)], axis=0)
    pack128 = jnp.concatenate([dba, dbx, dbst.T], axis=0)
    red, redq, *rest = _reduce_small(pack, pack128, mat_parts, ada_w[0], me)
    mats_all, (cparts, dmod, gab) = rest[:3], rest[3:]

    g_w_in, d_w_in, nm_w_in, nv_w_in = _adamw_reduced(win_parts, w_in[0], m_w_in[0], v_w_in[0], ROWS, "adamw_w_in")
    g_w_out, d_w_out, nm_w_out, nv_w_out = _adamw_reduced(wout_parts, w_out[0], m_w_out[0], v_w_out[0], ROWS // 2,
                                                          "adamw_w_out")
    g_ada, d_ada, nm_ada, nv_ada = _adamw_ada(c_all, cc, dmod, ada_w[0], m_ada_w[0], v_ada_w[0], me)
    ws = {k: args[k].reshape(view[k]) for k in _SMALL}
    ms = {k: args["m_" + k].reshape(view[k]) for k in _SMALL}
    vs = {k: args["v_" + k].reshape(view[k]) for k in _SMALL}
    small = _adamw_small(red, redq, [m.reshape(-1, HD) for m in mats_all], cparts, gab, ws, ms, vs, me)
    big = dict(w_in=(g_w_in, d_w_in, nm_w_in, nv_w_in), w_out=(g_w_out, d_w_out, nm_w_out, nv_w_out),
               ada_w=(g_ada, d_ada, nm_ada, nv_ada))

    loss = red[R_LOSS, 0]
    names = ("c_ctx", "ada_w", "ada_b", "norm_g", "w_in", "conv_w", "conv_b", "lru_wa", "lru_ba", "lru_wx", "lru_bx",
             "lru_lambda", "sgu_ln_g", "sgu_ln_b", "sgu_w", "sgu_b", "w_out", "final_g")
    outs = [loss, gx.reshape(x.shape)]
    for kind in range(4):
        for k in names:
            val = big[k][kind] if k in big else small[kind][k]
            outs.append(val.reshape(args[k].shape))
    return tuple(outs)
```

```python
import jax
import jax.numpy as jnp
from jax import lax
from jax.experimental import pallas as pl
from jax.experimental.pallas import tpu as pltpu

F32 = jnp.float32
BF16 = jnp.bfloat16

N_DEV = 8
D = 1024
L = 2048
LC = 256
HEADS = 8
HD = 128
CHUNK = 128
D_IN = 5 * D
W_IN_SHARD = D_IN // N_DEV
ROWS = 256
D_MIX = 2 * D
CONV_W = 4
LRU_C = 8.0
NORM_EPS = 1e-6
LN_EPS = 1e-5
ADAM_LR, ADAM_B1, ADAM_B2, ADAM_EPS, ADAM_WD, ADAM_STEP = 0.001, 0.9, 0.999, 1e-08, 0.01, 10

VMEM_LIMIT = 56 * 1024 * 1024

HBM = pl.BlockSpec(memory_space=pltpu.HBM)
VMEM = pl.BlockSpec(memory_space=pltpu.VMEM)
MESH = pl.DeviceIdType.MESH


def _call(body, **kw):
    return pl.pallas_call(body, **kw)


def _params(*sem):
    return pltpu.CompilerParams(dimension_semantics=sem, vmem_limit_bytes=VMEM_LIMIT)


def _sigmoid(x):
    return 0.5 * jnp.tanh(0.5 * x) + 0.5


def _silu_and_grad(x):
    s = _sigmoid(x)
    return x * s, s * (1.0 + x * (1.0 - s))


_G0 = 0.7978845608028654
_G1 = 0.044715


def _gelu_and_grad(x):
    x2 = x * x
    t = jnp.tanh(_G0 * (x + _G1 * x * x2))
    cdf = 0.5 * (1.0 + t)
    return x * cdf, cdf + 0.5 * x * (1.0 - t * t) * (_G0 * (1.0 + 3.0 * _G1 * x2))


def _softplus(z):
    t = jnp.exp(-jnp.abs(z))
    u = 1.0 + t
    log1p = jnp.where(u == 1.0, t, jnp.log(u) * t / jnp.where(u == 1.0, 1.0, u - 1.0))
    return jnp.maximum(z, 0.0) + log1p


def _dot(a, b):
    return jnp.dot(a, b, preferred_element_type=F32)


def _dot_nt(a, b):
    return lax.dot_general(a, b, (((1,), (1,)), ((), ())), preferred_element_type=F32)


def _dot_tn(a, b):
    return lax.dot_general(a, b, (((0,), (0,)), ((), ())), preferred_element_type=F32)


def _rows(shape):
    return lax.broadcasted_iota(jnp.int32, shape, 0)


def _gather2_shapes(arrays, modes):
    return [jax.ShapeDtypeStruct((N_DEV,) + a.shape if m == "ag" else (a.shape[0], N_DEV * a.shape[1]), a.dtype)
            for a, m in zip(arrays, modes)]


def _gather2_sems(n):
    return [pltpu.SemaphoreType.DMA((n, N_DEV - 1)), pltpu.SemaphoreType.DMA((n, N_DEV - 1)),
            pltpu.SemaphoreType.DMA((n,))]


def _barrier(peers):
    sem = pltpu.get_barrier_semaphore()
    for peer in peers:
        pl.semaphore_signal(sem, inc=1, device_id=peer, device_id_type=MESH)
    pl.semaphore_wait(sem, len(peers))


def _gather2_ops(ins, outs, modes, send_sems, recv_sems, local_sems, barrier=False):
    n = len(ins)
    x, y, c = lax.axis_index("x"), lax.axis_index("y"), lax.axis_index("c")
    me, sibling = (x, y, c), (x, y, 1 - c)
    chips = [(x ^ (k >> 1), y ^ (k & 1)) for k in (1, 2, 3)]

    def slot(j, px, py, pc):
        dev = 4 * px + 2 * py + pc
        if modes[j] == "agc":
            w = ins[j].shape[1]
            return outs[j].at[:, pl.ds(pl.multiple_of(dev * w, 128), w)]
        return outs[j].at[dev]

    def copy(j, k, block, to, src=None):
        return pltpu.make_async_remote_copy(
            src_ref=slot(j, *block) if src is None else src, dst_ref=slot(j, *block),
            send_sem=send_sems.at[j, k], recv_sem=recv_sems.at[j, k], device_id=to, device_id_type=MESH)

    def own(j):
        return pltpu.make_async_copy(ins[j], slot(j, *me), local_sems.at[j])

    def first(j):
        return [copy(j, 0, me, sibling, src=ins[j])] + [copy(j, 1 + i, me, (*chip, c), src=ins[j])
                                                        for i, chip in enumerate(chips)]

    def passed(j, i):
        return copy(j, 4 + i, (*chips[i], c), sibling)

    def start():
        if barrier:
            _barrier([sibling] + [(*chip, c) for chip in chips])
        for j in range(n):
            own(j).start()
            for cp in first(j):
                cp.start()

    def forward():
        for i, chip in enumerate(chips):
            for j in range(n):
                copy(j, 1 + i, (*chip, c), me).wait_recv()
                passed(j, i).start()

    def finish():
        for j in range(n):
            copy(j, 0, sibling, me).wait_recv()
            for i, chip in enumerate(chips):
                copy(j, 4 + i, (*chip, 1 - c), me).wait_recv()
            for cp in first(j) + [passed(j, i) for i in range(3)]:
                cp.wait_send()
            own(j).wait()

    return start, forward, finish


def _sibling_barrier():
    sem = pltpu.get_barrier_semaphore()
    sibling = (lax.axis_index("x"), lax.axis_index("y"), 1 - lax.axis_index("c"))
    pl.semaphore_signal(sem, inc=1, device_id=sibling, device_id_type=MESH)
    pl.semaphore_wait(sem, 1)


def _reduce2_local(arrays, me, name, barrier_id, out_dtype):
    n = len(arrays)
    staged = [jax.ShapeDtypeStruct((4,) + a.shape[1:], a.dtype) for a in arrays]

    def to_sibling(*refs):
        ins, outs = refs[:n], refs[n:2 * n]
        send_sems, recv_sems = refs[2 * n:]
        x, y, c = lax.axis_index("x"), lax.axis_index("y"), lax.axis_index("c")
        _sibling_barrier()
        copies = []
        for j in range(n):
            for q in range(4):
                cp = pltpu.make_async_remote_copy(
                    src_ref=ins[j].at[2 * q + (1 - c)], dst_ref=outs[j].at[q], send_sem=send_sems.at[j, q],
                    recv_sem=recv_sems.at[j, q], device_id=(x, y, 1 - c), device_id_type=MESH)
                cp.start()
                copies.append(cp)
        for cp in copies:
            cp.wait()

    stage = _call(
        to_sibling, name=name + "_d2d", out_shape=staged, in_specs=[HBM] * n, out_specs=[HBM] * n,
        scratch_shapes=[pltpu.SemaphoreType.DMA((n, 4)), pltpu.SemaphoreType.DMA((n, 4))],
        compiler_params=pltpu.CompilerParams(has_side_effects=True, collective_id=barrier_id),
    )(*[pltpu.with_memory_space_constraint(a, pltpu.HBM) for a in arrays])

    def add(me_ref, *refs):
        del me_ref
        own, got, outs = refs[:n], refs[n:2 * n], refs[2 * n:]
        for j in range(n):
            outs[j][0] = (own[j][0].astype(F32) + got[j][0].astype(F32)).astype(out_dtype)

    own_specs = [pl.BlockSpec((1,) + a.shape[1:], lambda q, me_ref: (2 * q + me_ref[0] % 2, 0, 0)) for a in arrays]
    slot_specs = [pl.BlockSpec((1,) + a.shape[1:], lambda q, me_ref: (q, 0, 0)) for a in arrays]
    return _call(
        add, name=name + "_add", out_shape=[jax.ShapeDtypeStruct(s.shape, out_dtype) for s in staged],
        grid_spec=pltpu.PrefetchScalarGridSpec(num_scalar_prefetch=1, grid=(4,), in_specs=own_specs + slot_specs,
                                               out_specs=slot_specs),
        compiler_params=_params("arbitrary"),
    )(me, *arrays, *stage)


def _chips_sems(n):
    return [pltpu.SemaphoreType.DMA((n, 6)), pltpu.SemaphoreType.DMA((n, 6)), pltpu.SemaphoreType.DMA((n,))]


def _chips_stage_shapes(chip_sums):
    return [jax.ShapeDtypeStruct((2, a.shape[1] // 2, a.shape[2]), a.dtype) for a in chip_sums]


def _chips_ops(ins, outs, stages, send_sems, recv_sems, local_sems, first_chips=None, barrier=False):
    x, y, c = lax.axis_index("x"), lax.axis_index("y"), lax.axis_index("c")
    qm = 2 * x + y
    first_chips = first_chips or [0] * len(ins)

    def owns(j, chip):
        lo, cnt = first_chips[j], ins[j].shape[0]
        if lo == 0 and cnt == 4:
            return None
        return jnp.logical_and(chip >= lo, chip < lo + cnt)

    def guarded(cond, fn):
        if cond is None:
            fn()
        else:
            pl.when(cond)(fn)

    def slot(j, chip):
        return jnp.clip(chip - first_chips[j], 0, ins[j].shape[0] - 1)

    def half(j, i):
        h = ins[j].shape[1] // 2
        return pl.ds(i * h, h)

    def copy(j, sem, src, dst, k):
        return pltpu.make_async_remote_copy(
            src_ref=src, dst_ref=dst, send_sem=send_sems.at[j, sem], recv_sem=recv_sems.at[j, sem],
            device_id=(x ^ (k >> 1), y ^ (k & 1), c), device_id_type=MESH)

    def direct(j, k):
        return copy(j, k - 1, ins[j].at[slot(j, qm ^ k)], outs[j].at[qm], k)

    def first_hop(j, k):
        return copy(j, 1 + k, ins[j].at[slot(j, qm ^ 3), half(j, k - 1)], stages[j].at[k - 1], k)

    def second_hop(j, k):
        return copy(j, 3 + k, stages[j].at[2 - k], outs[j].at[qm ^ (3 - k), half(j, 2 - k)], k)

    def local(j):
        return pltpu.make_async_copy(ins[j].at[slot(j, qm)], outs[j].at[qm], local_sems.at[j])

    def start():
        if barrier:
            _barrier([(x ^ (k >> 1), y ^ (k & 1), c) for k in (1, 2)])
        for j in range(len(ins)):
            for k in (1, 2):
                guarded(owns(j, qm ^ 3), lambda j=j, k=k: first_hop(j, k).start())
        for j in range(len(ins)):
            for k in (1, 2):
                guarded(owns(j, qm ^ k), lambda j=j, k=k: direct(j, k).start())
            guarded(owns(j, qm), lambda j=j: local(j).start())

    def forward():
        for j in range(len(ins)):
            for k in (1, 2):
                def pass_on(j=j, k=k):
                    first_hop(j, 3 - k).wait_recv()
                    second_hop(j, k).start()
                guarded(owns(j, qm ^ k), pass_on)

    def finish():
        for j in range(len(ins)):
            for k in (1, 2):
                guarded(owns(j, qm ^ k), lambda j=j, k=k: direct(j, k).wait_send())
                guarded(owns(j, qm ^ k), lambda j=j, k=k: second_hop(j, k).wait_send())
                guarded(owns(j, qm ^ 3), lambda j=j, k=k: first_hop(j, k).wait_send())
                guarded(owns(j, qm), lambda j=j, k=k: direct(j, k).wait_recv())
                guarded(owns(j, qm), lambda j=j, k=k: second_hop(j, k).wait_recv())
            guarded(owns(j, qm), lambda j=j: local(j).wait())

    return start, forward, finish


ARRIVAL = (0, 1, 2, 4, 3, 5, 6, 7)


def _front_project(xr, c, c_ctx, ada_w, ada_b, ng, w_in, w_out, cw, lam, me):
    nloc = ada_w.shape[1]
    ws = W_IN_SHARD
    order = me[0] ^ jnp.asarray(ARRIVAL, jnp.int32)

    def body(ord_ref, x_ref, c_ref, cc_ref, aw_ref, ab_ref, ng_ref, win_ref, wout_ref, cw_ref, lam_ref,
             z_ref, hn_ref, wfull_ref, woutb_ref, modx_ref, modc_ref, call_ref, cwf_ref, lamf_ref,
             wv, call_s, part_s, parts_s, w_send, w_recv, hbm_sems, s_send, s_recv, g_send, g_recv, g_local):
        t = pl.program_id(0)
        x, y, cidx = lax.axis_index("x"), lax.axis_index("y"), lax.axis_index("c")
        me_i = ord_ref[0]
        sibling = (x, y, 1 - cidx)
        chips = [(x ^ (k >> 1), y ^ (k & 1)) for k in (1, 2, 3)]
        g_start, g_pass, g_finish = _gather2_ops([cw_ref, lam_ref], [cwf_ref, lamf_ref], ["agc", "agc"],
                                                 g_send, g_recv, g_local)

        def shard_copy(k, px, py, pc, to, half=None):
            slot = wv.at[4 * px + 2 * py + pc]
            if half is not None:
                slot = slot.at[pl.ds(half * (D // 2), D // 2), :]
            return pltpu.make_async_remote_copy(src_ref=slot, dst_ref=slot, send_sem=w_send.at[k],
                                                recv_sem=w_recv.at[k], device_id=to, device_id_type=MESH)

        def small_gather(src, my_slot, stage):
            copies = []
            for k in range(1, N_DEV):
                peer = (x ^ (k >> 2), y ^ ((k >> 1) & 1), cidx ^ (k & 1))
                cp = pltpu.make_async_remote_copy(src_ref=src, dst_ref=my_slot, send_sem=s_send.at[stage, k - 1],
                                                  recv_sem=s_recv.at[stage, k - 1], device_id=peer,
                                                  device_id_type=MESH)
                cp.start()
                copies.append(cp)
            pltpu.sync_copy(src, my_slot)
            return copies

        def finish_small(copies):
            for cp in copies:
                cp.wait()

        def to_neighbours(half):
            for i in (0, 1):
                shard_copy(1 + i, x, y, cidx, (*chips[i], cidx), half=half).start()

        @pl.when(t == 0)
        def _():
            _barrier([(x ^ (k >> 2), y ^ ((k >> 1) & 1), cidx ^ (k & 1)) for k in range(1, N_DEV)])
            g_start()
            wv[me_i] = win_ref[...].astype(BF16)
            woutb_ref[...] = wout_ref[...].astype(BF16)
            shard_copy(0, x, y, cidx, sibling).start()
            finish_small(small_gather(c_ref, call_s.at[pl.ds(me_i, 1), :], 0))
            to_neighbours(0)
            call_ref[...] = call_s[...]
            off = pl.multiple_of(me_i * nloc, 128)
            b = ab_ref[:, pl.ds(off, nloc)]
            w = aw_ref[...]
            sx, _ = _silu_and_grad(call_s[...])
            sc, _ = _silu_and_grad(jnp.broadcast_to(cc_ref[...], (8, D)))
            part_s[0:8, :] = _dot(sx, w) + b
            part_s[8:16, :] = _dot(sc, w) + b
            parts_sent = small_gather(part_s, parts_s.at[me_i], 1)
            to_neighbours(1)
            finish_small(parts_sent)
            mine = _rows((16, nloc)) == me_i
            for j in range(N_DEV):
                pj = parts_s[j]
                modx_ref[:, j * nloc:(j + 1) * nloc] = jnp.sum(jnp.where(mine, pj, 0.0), axis=0, keepdims=True)
                modc_ref[:, j * nloc:(j + 1) * nloc] = pj[8:9, :]
            shift, scale1, ngv = modx_ref[:, 0:D], 1.0 + modx_ref[:, D:2 * D], ng_ref[...]
            for r in range(L // ROWS):
                rsl = slice(r * ROWS, (r + 1) * ROWS)
                xv = x_ref[rsl, :]
                rs = lax.rsqrt(jnp.mean(xv * xv, axis=-1, keepdims=True) + NORM_EPS)
                hn_ref[rsl, :] = ((xv * rs * ngv) * scale1 + shift).astype(BF16)

        @pl.when(t == 1)
        def _():
            shard_copy(0, x, y, 1 - cidx, sibling).wait_recv()
            g_pass()

        for i in (0, 1):
            @pl.when(t == ARRIVAL.index((2, 4)[i]))
            def _(i=i):
                shard_copy(1 + i, *chips[i], cidx, sibling).wait_recv()
                shard_copy(4 + i, *chips[i], cidx, sibling).start()
                shard_copy((7, 3)[i], *chips[i], cidx, (*chips[1 - i], cidx), half=i).start()

        @pl.when(t == ARRIVAL.index(6))
        def _():
            shard_copy(3, *chips[2], cidx, sibling, half=1).wait_recv()
            shard_copy(7, *chips[2], cidx, sibling, half=0).wait_recv()
            shard_copy(6, *chips[2], cidx, sibling).start()

        for i in range(3):
            @pl.when(t == ARRIVAL.index((3, 5, 7)[i]))
            def _(i=i):
                shard_copy(4 + i, *chips[i], 1 - cidx, sibling).wait_recv()

        @pl.when(t == 2)
        def _():
            g_finish()

        dev = ord_ref[t]
        for r in range(L // (2 * ROWS)):
            rsl = slice(r * 2 * ROWS, (r + 1) * 2 * ROWS)
            z_ref[rsl, :] = _dot(hn_ref[rsl, :], wv[dev])
        col = pl.ds(pl.multiple_of(dev * ws, 128), ws)
        pltpu.make_async_copy(wv.at[dev], wfull_ref.at[:, col], hbm_sems.at[t]).start()

        @pl.when(t == N_DEV - 1)
        def _():
            for k in (0, 1, 2, 4, 5, 6):
                shard_copy(k, x, y, cidx, sibling).wait_send()
            for k in (3, 7):
                shard_copy(k, x, y, cidx, sibling, half=0).wait_send()
            for s in range(N_DEV):
                pltpu.make_async_copy(wv.at[0], wfull_ref.at[:, pl.ds(0, ws)], hbm_sems.at[s]).wait()

    const = lambda *shape: pl.BlockSpec(shape, lambda t, o: (0,) * len(shape))
    once = lambda *shape: pl.BlockSpec(shape, lambda t, o: (0,) * len(shape), pipeline_mode=pl.Buffered(1))
    return _call(
        body, name="front_project",
        out_shape=[jax.ShapeDtypeStruct((L, D_IN), F32), jax.ShapeDtypeStruct((L, D), BF16),
                   jax.ShapeDtypeStruct((D, D_IN), BF16), jax.ShapeDtypeStruct(w_out.shape, BF16),
                   jax.ShapeDtypeStruct((1, 3 * D), F32), jax.ShapeDtypeStruct((1, 3 * D), F32),
                   jax.ShapeDtypeStruct((N_DEV, D), F32), jax.ShapeDtypeStruct((CONV_W, D), F32),
                   jax.ShapeDtypeStruct((2, D), F32)],
        grid_spec=pltpu.PrefetchScalarGridSpec(
            num_scalar_prefetch=1, grid=(N_DEV,),
            in_specs=[once(L, D), const(1, D), const(1, D), once(D, nloc), const(1, 3 * D), const(1, D),
                      once(D, ws), once(*w_out.shape), HBM, HBM],
            out_specs=[pl.BlockSpec((L, ws), lambda t, o: (0, o[t])), const(L, D), HBM, const(*w_out.shape),
                       const(1, 3 * D), const(1, 3 * D), const(N_DEV, D), HBM, HBM],
            scratch_shapes=[pltpu.VMEM((N_DEV, D, ws), BF16), pltpu.VMEM((N_DEV, D), F32), pltpu.VMEM((16, nloc), F32),
                            pltpu.VMEM((N_DEV, 16, nloc), F32), pltpu.SemaphoreType.DMA((8,)),
                            pltpu.SemaphoreType.DMA((8,)), pltpu.SemaphoreType.DMA((N_DEV,)),
                            pltpu.SemaphoreType.DMA((2, N_DEV - 1)), pltpu.SemaphoreType.DMA((2, N_DEV - 1))]
            + _gather2_sems(2)),
        compiler_params=pltpu.CompilerParams(dimension_semantics=("arbitrary",), vmem_limit_bytes=VMEM_LIMIT,
                                             has_side_effects=True, collective_id=10),
    )(order, xr, c, c_ctx, ada_w, ada_b, ng, w_in, w_out, pltpu.with_memory_space_constraint(cw, pltpu.HBM),
      pltpu.with_memory_space_constraint(lam, pltpu.HBM))


def _project(xr, mod, ng, w, ncols, tm, name):
    rows = xr.shape[0]

    def body(x_ref, sh_ref, sc_ref, ng_ref, w_ref, z_ref, hn_ref):
        x = x_ref[...]
        rs = lax.rsqrt(jnp.mean(x * x, axis=-1, keepdims=True) + NORM_EPS)
        hn = (x * rs * ng_ref[...]) * (1.0 + sc_ref[...]) + sh_ref[...]
        hb = hn.astype(BF16)
        hn_ref[...] = hb
        for n in range(ncols // D):
            z_ref[:, n * D:(n + 1) * D] = _dot(hb, w_ref[:, n * D:(n + 1) * D])

    vec = pl.BlockSpec((1, D), lambda i: (0, 0))
    return _call(
        body, name=name, grid=(rows // tm,),
        out_shape=[jax.ShapeDtypeStruct((rows, ncols), F32), jax.ShapeDtypeStruct((rows, D), BF16)],
        in_specs=[pl.BlockSpec((tm, D), lambda i: (i, 0)), vec, pl.BlockSpec((1, D), lambda i: (0, 1)), vec,
                  pl.BlockSpec((D, ncols), lambda i: (0, 0), pipeline_mode=pl.Buffered(1))],
        out_specs=[pl.BlockSpec((tm, ncols), lambda i: (i, 0)), pl.BlockSpec((tm, D), lambda i: (i, 0))],
        compiler_params=_params("arbitrary"),
    )(xr, mod, mod, ng, w)


def _scan_pair(af_ref, uf_ref, hf_ref, h0f, ab_ref, ub_ref, hb_ref, h0b, t_len):
    span = 8 * SCAN_BLOCKS
    nit = t_len // span
    rows = _rows((8, HD))

    def local_scan(a, b, forward):
        for s in (1, 2, 4):
            sh = s if forward else 8 - s
            m = rows >= s if forward else rows < 8 - s
            b = a * jnp.where(m, pltpu.roll(b, sh, 0), 0.0) + b
            a = a * jnp.where(m, pltpu.roll(a, sh, 0), 1.0)
        return a, b

    def span_scan(a_ref, u_ref, h_ref, off, carry, forward):
        order = range(SCAN_BLOCKS) if forward else range(SCAN_BLOCKS - 1, -1, -1)
        last = slice(7, 8) if forward else slice(0, 1)
        for q in order:
            rs = pl.ds(off + 8 * q, 8)
            a, b = local_scan(a_ref[rs, :], u_ref[rs, :], forward)
            h_ref[rs, :] = b + a * carry
            carry = a[last, :] * carry + b[last, :]
        return carry

    def body(k, carry):
        cf, cb = carry
        cf = span_scan(af_ref, uf_ref, hf_ref, pl.multiple_of(k * span, span), cf, True)
        cb = span_scan(ab_ref, ub_ref, hb_ref, pl.multiple_of((nit - 1 - k) * span, span), cb, False)
        return cf, cb

    return lax.fori_loop(0, nit, body, (h0f, h0b))


SCAN_BLOCKS = 8


def _shifted(pad_ref, x, offsets, before=0.0, after=0.0):
    n = x.shape[0]
    pad_ref[0:8, :] = jnp.broadcast_to(jnp.asarray(before, F32), (8, x.shape[1]))
    pad_ref[8:8 + n, :] = x
    pad_ref[8 + n:16 + n, :] = jnp.broadcast_to(jnp.asarray(after, F32), (8, x.shape[1]))
    return [pad_ref[8 + o:8 + o + n, :] for o in offsets]


def _conv(xa, cw, cb, pad_ref):
    xm1, xp1, xp2 = _shifted(pad_ref, xa, (-1, 1, 2))
    return xm1 * cw[0:1, :] + xa * cw[1:2, :] + xp1 * cw[2:3, :] + xp2 * cw[3:4, :] + cb


def _gates(xc, wa, wx, ba, bx, nsp):
    xb = xc.astype(BF16)
    r = _sigmoid(_dot(xb, wa) + ba)
    i = _sigmoid(_dot(xb, wx) + bx)
    log_a = r * nsp
    a = jnp.exp(log_a)
    g2 = jnp.tanh(log_a) * (-1.0 - a * a)
    rg = lax.rsqrt(jnp.maximum(g2, 1e-30))
    return r, i, a, g2 * rg, rg


def _lru_param_specs():
    h4 = pl.BlockSpec((2, 1, HD, HD), lambda h: (0, h, 0, 0))
    v2 = pl.BlockSpec((2, HD), lambda h: (0, h))
    b16 = pl.BlockSpec((2 * HEADS, HD), lambda h: (0, 0))
    return dict(
        xa=pl.BlockSpec((L, HD), lambda h: (0, h)), xac=pl.BlockSpec((LC, HD), lambda h: (0, h)),
        cw=pl.BlockSpec((CONV_W, HD), lambda h: (0, h)), cb=pl.BlockSpec((1, HD), lambda h: (0, h)), h4=h4, v2=v2,
        b16=b16)


def _bias_row(ref, d):
    mask = _rows((2 * HEADS, HD)) == d * HEADS + pl.program_id(0)
    return jnp.sum(jnp.where(mask, ref[...], 0.0), axis=0, keepdims=True), mask


def _lru_forward(zx, zc, cw, cb, wa, wx, ba, bx, lam, gather, gather_modes):
    ng_ = len(gather)

    def body(xa_ref, xac_ref, cw_ref, cb_ref, wa_ref, wx_ref, ba_ref, bx_ref, lam_ref, *rest):
        yl_ref = rest[ng_]
        af, uf, hf, ab, ub, hb, pad_s = rest[2 * ng_ + 1:2 * ng_ + 8]
        start, pass_on, finish = _gather2_ops(rest[:ng_], rest[ng_ + 1:2 * ng_ + 1], gather_modes,
                                              *rest[2 * ng_ + 8:], barrier=True)
        pl.when(pl.program_id(0) == 0)(start)
        pl.when(pl.program_id(0) == HEADS // 2)(pass_on)
        pl.when(pl.program_id(0) == HEADS - 1)(finish)
        cwv, cbv = cw_ref[...], cb_ref[...]
        nsp = (-LRU_C) * _softplus(-lam_ref[...])

        def forward(xa, t_len, h0f, h0b):
            xc = _conv(xa, cwv, cbv, pad_s)
            for d, (a_ref, u_ref) in enumerate(((af, uf), (ab, ub))):
                _, i, a, gamma, _ = _gates(xc, wa_ref[d, 0].astype(BF16), wx_ref[d, 0].astype(BF16),
                                           _bias_row(ba_ref, d)[0], _bias_row(bx_ref, d)[0], nsp[d:d + 1, :])
                a_ref[0:t_len, :] = a
                u_ref[0:t_len, :] = gamma * (i * xc)
            return _scan_pair(af, uf, hf, h0f, ab, ub, hb, h0b, t_len)

        z = jnp.zeros((1, HD), F32)
        h0f, h0b = forward(xac_ref[...], LC, z, z)
        forward(xa_ref[...], L, h0f, h0b)
        yl_ref[...] = hf[...] + hb[...]

    s = _lru_param_specs()
    return _call(
        body, name="lru_forward", grid=(HEADS,),
        out_shape=[jax.ShapeDtypeStruct((L, D), F32)] + _gather2_shapes(gather, gather_modes),
        in_specs=[s["xa"], s["xac"], s["cw"], s["cb"], s["h4"], s["h4"], s["b16"], s["b16"], s["v2"]] + [HBM] * ng_,
        out_specs=[pl.BlockSpec((L, HD), lambda h: (0, h))] + [HBM] * ng_,
        scratch_shapes=[pltpu.VMEM((L, HD), F32)] * 6 + [pltpu.VMEM((L + 16, HD), F32)] + _gather2_sems(ng_),
        compiler_params=pltpu.CompilerParams(dimension_semantics=("arbitrary",), vmem_limit_bytes=VMEM_LIMIT,
                                             has_side_effects=True, collective_id=5),
    )(zx, zc, cw, cb, wa, wx, ba, bx, lam, *[pltpu.with_memory_space_constraint(a, pltpu.HBM) for a in gather])


def _lru_backward(zx, zc, dyl, dz, cw, cb, wa, wx, ba, bx, lam, chip_sums, first_chips=None):
    nr = len(chip_sums)

    def body(xa_ref, xac_ref, dyl_ref, dz_in, cw_ref, cb_ref, wa_ref, wx_ref, ba_ref, bx_ref, lam_ref, *rest):
        (dxa_ref, dxac_ref, dwa_ref, dwx_ref, dba_ref, dbx_ref, dlam_ref, dcw_ref,
         dcb_ref) = rest[nr:nr + 9]
        main_s, ctx_s, pad_s = rest[3 * nr + 9:3 * nr + 12]
        if nr:
            start, forward, finish = _chips_ops(rest[:nr], rest[nr + 9:2 * nr + 9], rest[2 * nr + 9:3 * nr + 9],
                                                *rest[3 * nr + 12:], first_chips=first_chips, barrier=True)
            pl.when(pl.program_id(0) == 0)(start)
            pl.when(pl.program_id(0) == HEADS // 2)(forward)
            pl.when(pl.program_id(0) == HEADS - 1)(finish)
        del dz_in

        @pl.when(pl.program_id(0) == 0)
        def _():
            dba_ref[...] = jnp.zeros_like(dba_ref)
            dbx_ref[...] = jnp.zeros_like(dbx_ref)

        cwv, cbv = cw_ref[...], cb_ref[...]
        lamv = lam_ref[...]
        sp = _softplus(-lamv)
        nsp = (-LRU_C) * sp
        z = jnp.zeros((1, HD), F32)

        def wmat(ref, d):
            return ref[d, 0].astype(BF16)

        def workspace(s):
            return dict(a=(s.at[0], s.at[1]), u=(s.at[2], s.at[3]), h=(s.at[4], s.at[5]), rho=(s.at[6], s.at[7]),
                        saved=(tuple(s.at[8 + k] for k in range(4)), tuple(s.at[12 + k] for k in range(4))),
                        xc=s.at[16])

        def forward(ws, xa, t_len, h0f, h0b):
            xc = _conv(xa, cwv, cbv, pad_s)
            ws["xc"][...] = xc
            for d in (0, 1):
                vals = _gates(xc, wmat(wa_ref, d), wmat(wx_ref, d), _bias_row(ba_ref, d)[0],
                              _bias_row(bx_ref, d)[0], nsp[d:d + 1, :])
                r, i, a, gamma, rg = vals
                ws["a"][d][...] = a
                ws["u"][d][...] = gamma * (i * xc)
                for ref, val in zip(ws["saved"][d], (r, i, gamma, rg)):
                    ref[...] = val
            return _scan_pair(ws["a"][0], ws["u"][0], ws["h"][0], h0f, ws["a"][1], ws["u"][1], ws["h"][1], h0b,
                              t_len)

        def backward(ws, xa, t_len, h0f, h0b, dhf, dhb, first):
            xc = ws["xc"][...]
            (af, ab), (uf, ub), (hf, hb), (rf, rb) = ws["a"], ws["u"], ws["h"], ws["rho"]
            uf[...] = ab[...] * dhb
            ub[...] = af[...] * dhf
            rho_b_last, rho_f_first = _scan_pair(ab, uf, rb, z, af, ub, rf, z, t_len)
            dxc = jnp.zeros((t_len, HD), F32)
            dsp = []
            for d in (0, 1):
                r, i, gamma, rg = (ref[...] for ref in ws["saved"][d])
                a = ws["a"][d][...]
                if d == 0:
                    lam_t = dhf + _shifted(pad_s, rf[...], (1,))[0]
                    h_prev = _shifted(pad_s, hf[...], (-1,), before=h0f)[0]
                else:
                    lam_t = dhb + _shifted(pad_s, rb[...], (-1,))[0]
                    h_prev = _shifted(pad_s, hb[...], (1,), after=h0b)[0]
                da = lam_t * h_prev
                lx = lam_t * xc
                d_i = lx * gamma
                d_gamma = lx * i
                dxc = dxc + lam_t * (gamma * i)
                d_log_a = a * (da - d_gamma * (a * rg))
                dsp.append(jnp.sum(d_log_a * r, axis=0, keepdims=True) * (-LRU_C))
                d_pre_r = d_log_a * nsp[d:d + 1, :] * (r * (1.0 - r))
                d_pre_i = d_i * (i * (1.0 - i))
                prb, pib, xb = d_pre_r.astype(BF16), d_pre_i.astype(BF16), xc.astype(BF16)
                dxc = dxc + _dot_nt(prb, wmat(wa_ref, d)) + _dot_nt(pib, wmat(wx_ref, d))
                g_wa, g_wx = _dot_tn(xb, prb), _dot_tn(xb, pib)
                g_ba = jnp.sum(d_pre_r, axis=0, keepdims=True)
                g_bx = jnp.sum(d_pre_i, axis=0, keepdims=True)
                mask = _bias_row(ba_ref, d)[1]
                dba_ref[...] += jnp.where(mask, g_ba, 0.0)
                dbx_ref[...] += jnp.where(mask, g_bx, 0.0)
                if first:
                    dwa_ref[d, 0] = g_wa
                    dwx_ref[d, 0] = g_wx
                else:
                    dwa_ref[d, 0] += g_wa
                    dwx_ref[d, 0] += g_wx
            g_lam = jnp.concatenate(dsp, axis=0) * (-_sigmoid(-lamv))
            dm1, dp1, dm2 = _shifted(pad_s, dxc, (-1, 1, -2))
            dxa = dp1 * cwv[0:1, :] + dxc * cwv[1:2, :] + dm1 * cwv[2:3, :] + dm2 * cwv[3:4, :]
            xm1, xp1, xp2 = _shifted(pad_s, xa, (-1, 1, 2))
            g_cw = jnp.concatenate([jnp.sum(dxc * v, axis=0, keepdims=True) for v in (xm1, xa, xp1, xp2)], axis=0)
            g_cb = jnp.sum(dxc, axis=0, keepdims=True)
            if first:
                dlam_ref[...] = g_lam
                dcw_ref[...] = g_cw
                dcb_ref[...] = g_cb
            else:
                dlam_ref[...] += g_lam
                dcw_ref[...] += g_cw
                dcb_ref[...] += g_cb
            return dxa, rho_f_first, rho_b_last

        ws_x, ws_c = workspace(main_s), workspace(ctx_s)
        h0f, h0b = forward(ws_c, xac_ref[...], LC, z, z)
        forward(ws_x, xa_ref[...], L, h0f, h0b)
        dh = dyl_ref[...]
        dxa, dh0f, dh0b = backward(ws_x, xa_ref[...], L, h0f, h0b, dh, dh, True)
        dxa_ref[...] = dxa.astype(BF16)
        rc = _rows((LC, HD))
        dxac, _, _ = backward(ws_c, xac_ref[...], LC, z, z, jnp.where(rc == LC - 1, dh0f, 0.0),
                              jnp.where(rc == 0, dh0b, 0.0), False)
        dxac_ref[...] = dxac.astype(BF16)

    s = _lru_param_specs()
    col = lambda r: pl.BlockSpec((r, HD), lambda h: (0, h))
    return _call(
        body, name="lru_backward", grid=(HEADS,),
        out_shape=[jax.ShapeDtypeStruct((L, D_IN), BF16), jax.ShapeDtypeStruct((LC, D), BF16),
                   jax.ShapeDtypeStruct((2, HEADS, HD, HD), F32), jax.ShapeDtypeStruct((2, HEADS, HD, HD), F32),
                   jax.ShapeDtypeStruct((2 * HEADS, HD), F32), jax.ShapeDtypeStruct((2 * HEADS, HD), F32),
                   jax.ShapeDtypeStruct((2, D), F32), jax.ShapeDtypeStruct((CONV_W, D), F32),
                   jax.ShapeDtypeStruct((1, D), F32)] + [jax.ShapeDtypeStruct((4,) + a.shape[1:], a.dtype)
                                                          for a in chip_sums] + _chips_stage_shapes(chip_sums),
        in_specs=[s["xa"], s["xac"], col(L), pl.BlockSpec(memory_space=pl.ANY), s["cw"], s["cb"], s["h4"], s["h4"],
                  s["b16"], s["b16"], s["v2"]] + [HBM] * nr,
        out_specs=[col(L), col(LC), s["h4"], s["h4"], s["b16"], s["b16"], s["v2"], col(CONV_W), col(1)]
        + [HBM] * (2 * nr),
        scratch_shapes=[pltpu.VMEM((17, L, HD), F32), pltpu.VMEM((17, LC, HD), F32), pltpu.VMEM((L + 16, HD), F32)]
        + (_chips_sems(nr) if nr else []),
        input_output_aliases={3: 0},
        compiler_params=pltpu.CompilerParams(dimension_semantics=("arbitrary",), vmem_limit_bytes=VMEM_LIMIT,
                                             has_side_effects=True, collective_id=6 if nr else None),
    )(zx, zc, dyl, dz, cw, cb, wa, wx, ba, bx, lam, *[pltpu.with_memory_space_constraint(a, pltpu.HBM)
                                                       for a in chip_sums])


def _mixer_loss(x, tgt, zx, yl, gx, fg, lng, lnb, ws, wst, bst, wout, tm):
    ncht = tm // CHUNK

    def body(x_ref, t_ref, ga_ref, u_ref, v_ref, gb_ref, yl_ref, gx_ref, fg_ref, lng_ref, lnb_ref, ws_ref, wst_ref,
             bst_ref, wout_ref,
             dz_ref, dyl_ref, dxn_ref, y_s, do_ref, dws_ref, dbst_ref, vec_ref,
             vn_s, mix_s, dm_s, dvn_s):
        step = pl.program_id(0)

        @pl.when(step == 0)
        def _():
            dws_ref[...] = jnp.zeros_like(dws_ref)
            dbst_ref[...] = jnp.zeros_like(dbst_ref)
            vec_ref[...] = jnp.zeros_like(vec_ref)

        u, v = u_ref[...], v_ref[...]
        ug, dug_du = _gelu_and_grad(u)
        vg, dvg_dv = _gelu_and_grad(v)
        mu = jnp.mean(vg, axis=-1, keepdims=True)
        vc = vg - mu
        rstd = lax.rsqrt(jnp.mean(vc * vc, axis=-1, keepdims=True) + LN_EPS)
        vhat = vc * rstd
        lngv = lng_ref[...]
        vn_s[...] = (vhat * lngv + lnb_ref[...]).astype(BF16)
        for ch in range(ncht):
            rs = slice(ch * CHUNK, (ch + 1) * CHUNK)
            for g in range(HEADS):
                cs = slice(g * HD, (g + 1) * HD)
                mix_s[rs, cs] = _dot(ws_ref[g], vn_s[rs, cs]) + bst_ref[:, g:g + 1]
        mixed = mix_s[...]
        ga, gb, yl = ga_ref[...], gb_ref[...], yl_ref[...]
        sga, dsga = _silu_and_grad(ga)
        sgb, dsgb = _silu_and_grad(gb)
        ys = ug * mixed
        y_s[:, 0:D] = (yl * sga).astype(BF16)
        y_s[:, D:D_MIX] = (ys * sgb).astype(BF16)
        o = _dot(y_s[...], wout_ref[...])
        gxv, fgv = gx_ref[...], fg_ref[...]
        xn = x_ref[...] + gxv * o
        rs2 = lax.rsqrt(jnp.mean(xn * xn, axis=-1, keepdims=True) + NORM_EPS)
        xh = xn * rs2
        diff = xh * fgv - t_ref[...]
        vec_ref[R_LOSS:R_LOSS + 1, :] += jnp.full((1, D), jnp.sum(diff * diff) * (0.5 / D), F32)
        dout = diff * (1.0 / D)
        w = dout * fgv
        dxn = rs2 * (w - xh * jnp.mean(w * xh, axis=-1, keepdims=True))
        dxn_ref[...] = dxn
        vec_ref[0:1, :] += jnp.sum(dxn * o, axis=0, keepdims=True)
        vec_ref[1:2, :] += jnp.sum(dout * xh, axis=0, keepdims=True)
        dob = (dxn * gxv).astype(BF16)
        do_ref[...] = dob
        dy = _dot_nt(dob, wout_ref[...])
        dya, dyb = dy[:, 0:D], dy[:, D:D_MIX]
        dyl_ref[...] = dya * sga
        dys = dyb * sgb
        dz_ref[:, 0:D] = jnp.zeros((tm, D), BF16)
        dz_ref[:, D:2 * D] = (dya * yl * dsga).astype(BF16)
        dz_ref[:, 2 * D:3 * D] = (dys * mixed * dug_du).astype(BF16)
        dz_ref[:, 4 * D:5 * D] = (dyb * ys * dsgb).astype(BF16)
        dm = dys * ug
        dm_s[...] = dm.astype(BF16)
        for g in range(HEADS):
            cs = slice(g * HD, (g + 1) * HD)
            dbst_ref[:, g:g + 1] += sum(jnp.sum(dm[ch * CHUNK:(ch + 1) * CHUNK, cs], axis=1, keepdims=True)
                                        for ch in range(ncht))
            for ch in range(ncht):
                rs = slice(ch * CHUNK, (ch + 1) * CHUNK)
                dws_ref[g] += _dot_nt(dm_s[rs, cs], vn_s[rs, cs])
                dvn_s[rs, cs] = _dot(wst_ref[g], dm_s[rs, cs])
        dvn = dvn_s[...]
        vec_ref[2:3, :] += jnp.sum(dvn * vhat, axis=0, keepdims=True)
        vec_ref[3:4, :] += jnp.sum(dvn, axis=0, keepdims=True)
        dvh = dvn * lngv
        dvg = rstd * (dvh - jnp.mean(dvh, axis=-1, keepdims=True) - vhat * jnp.mean(dvh * vhat, axis=-1, keepdims=True))
        dz_ref[:, 3 * D:4 * D] = (dvg * dvg_dv).astype(BF16)

    tile = pl.BlockSpec((tm, D), lambda i: (i, 0))
    zcol = lambda n: pl.BlockSpec((tm, D), lambda i: (i, n))
    vec = pl.BlockSpec((1, D), lambda i: (0, 0))
    full = lambda *s: pl.BlockSpec(s, lambda i: (0,) * len(s))
    return _call(
        body, name="mixer_loss", grid=(L // tm,),
        out_shape=[jax.ShapeDtypeStruct((L, D_IN), BF16), jax.ShapeDtypeStruct((L, D), F32),
                   jax.ShapeDtypeStruct((L, D), F32), jax.ShapeDtypeStruct((L, D_MIX), BF16),
                   jax.ShapeDtypeStruct((L, D), BF16),
                   jax.ShapeDtypeStruct((HEADS, CHUNK, CHUNK), F32), jax.ShapeDtypeStruct((CHUNK, HEADS), F32),
                   jax.ShapeDtypeStruct((8, D), F32)],
        in_specs=[tile, tile, zcol(1), zcol(2), zcol(3), zcol(4), tile, pl.BlockSpec((1, D), lambda i: (0, 2)),
                  vec, vec, vec,
                  full(HEADS, CHUNK, CHUNK), full(HEADS, CHUNK, CHUNK), full(CHUNK, HEADS),
                  pl.BlockSpec((D_MIX, D), lambda i: (0, 0), pipeline_mode=pl.Buffered(1))],
        out_specs=[pl.BlockSpec((tm, D_IN), lambda i: (i, 0)), tile, tile,
                   pl.BlockSpec((tm, D_MIX), lambda i: (i, 0)), tile,
                   full(HEADS, CHUNK, CHUNK), full(CHUNK, HEADS), full(8, D)],
        scratch_shapes=[pltpu.VMEM((tm, D), BF16), pltpu.VMEM((tm, D), F32),
                        pltpu.VMEM((tm, D), BF16), pltpu.VMEM((tm, D), F32)],
        compiler_params=_params("arbitrary"),
    )(x, tgt, zx, zx, zx, zx, yl, gx, fg, lng, lnb, ws, wst, bst, wout)


def _grad_w(a, b, a2, b2, tk, name, bw, first, nblocks, split, barrier_id, chip_sums=()):
    nk = a.shape[0] // tk
    m = a.shape[1]
    with_ctx = a2 is not None
    if split == "cols":
        slots, r, w = nblocks, m, bw // 2
        piece = lambda q, pc: (slice(None), slice(pc * w, (pc + 1) * w))
    else:
        slots, r, w = 4, m // 8, bw
        piece = lambda q, pc: (slice((2 * q + pc) * r, (2 * q + pc + 1) * r), slice(None))

    nr = len(chip_sums)

    def body(*refs):
        a_ref, b_ref = refs[:2]
        a2_ref, b2_ref = refs[2:4] if with_ctx else (None, None)
        base = 4 if with_ctx else 2
        sums_ref = refs[base + nr]
        acc, mine_v, send_v, stage_v, send_sems, recv_sems = refs[base + 3 * nr + 1:base + 3 * nr + 7]
        n, k = pl.program_id(0), pl.program_id(1)
        x, y, c = lax.axis_index("x"), lax.axis_index("y"), lax.axis_index("c")

        def to_sibling(s):
            return pltpu.make_async_remote_copy(src_ref=send_v.at[s], dst_ref=stage_v.at[s], send_sem=send_sems.at[s],
                                                recv_sem=recv_sems.at[s], device_id=(x, y, 1 - c),
                                                device_id_type=MESH)

        if nr:
            c_start, c_forward, c_finish = _chips_ops(refs[base:base + nr], refs[base + nr + 1:base + 2 * nr + 1],
                                                      refs[base + 2 * nr + 1:base + 3 * nr + 1],
                                                      *refs[base + 3 * nr + 7:])

            @pl.when(jnp.logical_and(n == 0, k == 0))
            def _():
                _barrier([(x, y, 1 - c)] + [(x ^ (j >> 1), y ^ (j & 1), c) for j in (1, 2)])
                c_start()
        else:
            pl.when(jnp.logical_and(n == 0, k == 0))(_sibling_barrier)

        @pl.when(k == 0)
        def _():
            acc[...] = _dot_tn(a_ref[...], b_ref[...])

        if nk > 1:
            @pl.when(k > 0)
            def _():
                acc[...] += _dot_tn(a_ref[...], b_ref[...])

        if with_ctx:
            @pl.when(jnp.logical_and(k == nk - 1, n == 0))
            def _():
                acc[:, 0:b2_ref.shape[1]] += _dot_tn(a2_ref[...], b2_ref[...])

        if nr:
            pl.when(jnp.logical_and(k == nk - 1, n == nblocks - 1))(c_forward)

        def hand_over(s, q):
            for pc in (0, 1):
                @pl.when(c == pc)
                def _(pc=pc):
                    mine_v[s] = acc[piece(q, pc)]
                    send_v[s] = acc[piece(q, 1 - pc)].astype(BF16)
            to_sibling(s).start()

        for i in range(nblocks):
            @pl.when(jnp.logical_and(k == nk - 1, n == i))
            def _(i=i):
                if split == "cols":
                    hand_over(i, 0)
                else:
                    for q in range(4):
                        hand_over(q, q)

        @pl.when(jnp.logical_and(k == nk - 1, n == nblocks - 1))
        def _():
            for s in range(slots):
                to_sibling(s).wait_recv()
                sums_ref[s] = (mine_v[s] + stage_v[s].astype(F32)).astype(BF16)
            for s in range(slots):
                to_sibling(s).wait_send()
            if nr:
                c_finish()

    in_specs = [pl.BlockSpec((tk, m), lambda n, k: (k, 0)), pl.BlockSpec((tk, bw), lambda n, k: (k, n + first))]
    args = [a, b]
    if with_ctx:
        in_specs += [pl.BlockSpec(a2.shape, lambda n, k: (0, 0)), pl.BlockSpec(b2.shape, lambda n, k: (0, 0))]
        args += [a2, b2]
    in_specs += [HBM] * nr
    args += [pltpu.with_memory_space_constraint(s, pltpu.HBM) for s in chip_sums]
    return _call(
        body, name=name, grid=(nblocks, nk),
        out_shape=[jax.ShapeDtypeStruct((slots, r, w), BF16)]
        + [jax.ShapeDtypeStruct((4,) + s.shape[1:], s.dtype) for s in chip_sums] + _chips_stage_shapes(chip_sums),
        in_specs=in_specs, out_specs=[pl.BlockSpec((slots, r, w), lambda n, k: (0, 0, 0))] + [HBM] * (2 * nr),
        scratch_shapes=[pltpu.VMEM((m, bw), F32), pltpu.VMEM((slots, r, w), F32), pltpu.VMEM((slots, r, w), BF16),
                        pltpu.VMEM((slots, r, w), BF16), pltpu.SemaphoreType.DMA((slots,)),
                        pltpu.SemaphoreType.DMA((slots,))] + (_chips_sems(nr) if nr else []),
        compiler_params=pltpu.CompilerParams(dimension_semantics=("arbitrary", "arbitrary"),
                                             vmem_limit_bytes=VMEM_LIMIT, has_side_effects=True,
                                             collective_id=barrier_id),
    )(*args)


def _grad_rows(xr, dz, w, mod, ng, dres, ncols, tm, name, chip_sums=(), first_chips=None, dests=None):
    rows = xr.shape[0]
    steps = rows // tm
    with_dx = dres is not None
    nr = len(chip_sums)
    dests = [d for d in (dests or [None] * nr)]
    nd = sum(d is not None for d in dests)
    nin = 6 if with_dx else 5
    nout = 2 if with_dx else 1

    def body(*refs):
        if with_dx:
            x_ref, dz_ref, w_ref, sc_ref, ng_ref, dres_ref = refs[:nin]
            dx_ref, vec_ref = refs[nin + nr + nd:nin + nr + nd + nout]
        else:
            x_ref, dz_ref, w_ref, sc_ref, ng_ref = refs[:nin]
            (vec_ref,) = refs[nin + nr + nd:nin + nr + nd + nout]
        if nr:
            o0 = nin + nr + nd + nout
            start, forward, finish = _chips_ops(refs[nin:nin + nr], refs[o0:o0 + nr], refs[o0 + nr:o0 + 2 * nr],
                                                *refs[o0 + 2 * nr:], first_chips=first_chips, barrier=True)
            pl.when(pl.program_id(0) == 0)(start)
            pl.when(pl.program_id(0) == steps // 2)(forward)
            pl.when(pl.program_id(0) == steps - 1)(finish)

        @pl.when(pl.program_id(0) == 0)
        def _():
            vec_ref[...] = jnp.zeros_like(vec_ref)

        dhn = _dot_nt(dz_ref[...], w_ref[...])
        x = x_ref[...]
        rs = lax.rsqrt(jnp.mean(x * x, axis=-1, keepdims=True) + NORM_EPS)
        xh = x * rs
        ngv = ng_ref[...]
        y = xh * ngv
        vec_ref[0:1, :] += jnp.sum(dhn, axis=0, keepdims=True)
        vec_ref[1:2, :] += jnp.sum(dhn * y, axis=0, keepdims=True)
        dy = dhn * (1.0 + sc_ref[...])
        vec_ref[2:3, :] += jnp.sum(dy * xh, axis=0, keepdims=True)
        if with_dx:
            dxh = dy * ngv
            dx_ref[...] = dres_ref[...] + rs * (dxh - xh * jnp.mean(dxh * xh, axis=-1, keepdims=True))

    tile = pl.BlockSpec((tm, D), lambda i: (i, 0))
    vec = pl.BlockSpec((1, D), lambda i: (0, 0))
    in_specs = [tile, pl.BlockSpec((tm, ncols), lambda i: (i, 0)),
                pl.BlockSpec((D, ncols), lambda i: (0, 0), pipeline_mode=pl.Buffered(1)),
                pl.BlockSpec((1, D), lambda i: (0, 1)), vec]
    out_shape = [jax.ShapeDtypeStruct((8, D), F32)]
    out_specs = [pl.BlockSpec((8, D), lambda i: (0, 0))]
    args = [xr, dz, w, mod, ng]
    if with_dx:
        in_specs.append(tile)
        out_shape.insert(0, jax.ShapeDtypeStruct((rows, D), F32))
        out_specs.insert(0, tile)
        args.append(dres)
    aliases = {}
    for j, d in enumerate(dests):
        if d is not None:
            aliases[len(args) + nr + len(aliases)] = len(out_shape) + j
    in_specs += [HBM] * (nr + nd)
    out_specs += [HBM] * (2 * nr)
    out_shape += [jax.ShapeDtypeStruct((4,) + a.shape[1:], a.dtype) for a in chip_sums]
    out_shape += _chips_stage_shapes(chip_sums)
    args += [pltpu.with_memory_space_constraint(a, pltpu.HBM) for a in chip_sums]
    args += [pltpu.with_memory_space_constraint(d, pltpu.HBM) for d in dests if d is not None]
    return _call(body, name=name, grid=(steps,), out_shape=out_shape, in_specs=in_specs, out_specs=out_specs,
                 scratch_shapes=_chips_sems(nr) if nr else [], input_output_aliases=aliases,
                 compiler_params=pltpu.CompilerParams(dimension_semantics=("arbitrary",),
                                                      vmem_limit_bytes=VMEM_LIMIT, has_side_effects=bool(nr),
                                                      collective_id=7 if nr else None))(*args)


def _adamw(w, g, m, v):
    m = ADAM_B1 * m + (1.0 - ADAM_B1) * g
    v = ADAM_B2 * v + (1.0 - ADAM_B2) * (g * g)
    m_hat = m / (1.0 - ADAM_B1 ** ADAM_STEP)
    v_hat = v / (1.0 - ADAM_B2 ** ADAM_STEP)
    delta = -ADAM_LR * (m_hat / (jnp.sqrt(v_hat) + ADAM_EPS) + ADAM_WD * w)
    return delta, m, v


def _adamw_reduced(parts, w, m, v, tr, name):
    r, n = w.shape
    nparts = parts.shape[0]

    def body(p_ref, w_ref, m_ref, v_ref, g_ref, d_ref, mo_ref, vo_ref):
        g = p_ref[0].astype(F32)
        for i in range(1, nparts):
            g = g + p_ref[i].astype(F32)
        g_ref[...] = g
        d_ref[...], mo_ref[...], vo_ref[...] = _adamw(w_ref[...], g, m_ref[...], v_ref[...])

    tile = pl.BlockSpec((tr, n), lambda i: (i, 0))
    sds = jax.ShapeDtypeStruct((r, n), F32)
    return _call(
        body, name=name, grid=(r // tr,), out_shape=[sds] * 4,
        in_specs=[pl.BlockSpec((nparts, tr, n), lambda i: (0, i, 0)), tile, tile, tile], out_specs=[tile] * 4,
        compiler_params=_params("arbitrary"),
    )(parts, w, m, v)


R_GATE, R_FINAL_G, R_LN_G, R_LN_B, R_LOSS = 0, 1, 2, 3, 4
R_SH_X, R_SC_X, R_NG_X = 5, 6, 7
R_SH_C, R_SC_C, R_NG_C = 8, 9, 10
R_LAM, R_CW, R_CB = 11, 13, 17
PACK_ROWS = 24
Q_BA, Q_BX, Q_SGU_B, PACK128_ROWS = 0, 16, 32, 40


def _reduce_small(pack, pack128, mat_parts, ada_w, me):
    nloc = ada_w.shape[1]
    nm = len(mat_parts)

    def body(me_ref, pack_ref, packq_ref, *refs):
        mp_refs, w_ref = refs[:nm], refs[nm]
        red_ref, redq_ref = refs[nm + 1:nm + 3]
        mats_all = refs[nm + 3:2 * nm + 3]
        cparts_ref, dmod_ref, gab_ref = refs[2 * nm + 3:2 * nm + 6]
        vp_ref, vq_ref = refs[2 * nm + 6:2 * nm + 8]
        mat_refs = refs[2 * nm + 8:3 * nm + 8]
        cpart_ref, dmc_s = refs[3 * nm + 8:3 * nm + 10]
        sems = refs[3 * nm + 10:]
        p_start, p_forward, p_finish = _gather2_ops([pack_ref, packq_ref], [vp_ref, vq_ref], ["ag", "ag"], *sems[:3],
                                                    barrier=True)
        m_start, m_forward, m_finish = _gather2_ops(mat_refs, mats_all, ["ag"] * nm, *sems[3:6])
        c_start, c_forward, c_finish = _gather2_ops([cpart_ref], [cparts_ref], ["ag"], *sems[6:])
        p_start()
        for mp_ref, mat_ref in zip(mp_refs, mat_refs):
            mat = mp_ref[0].astype(F32)
            for i in range(1, mp_ref.shape[0]):
                mat = mat + mp_ref[i].astype(F32)
            mat_ref[...] = mat
        m_start()
        p_forward()
        p_finish()
        m_forward()
        red, redq = vp_ref[0], vq_ref[0]
        for i in range(1, N_DEV):
            red = red + vp_ref[i]
            redq = redq + vq_ref[i]
        red_ref[...] = red
        redq_ref[...] = redq
        for e in range(N_DEV):
            dmod_ref[e:e + 1, 0:D] = vp_ref[e, R_SH_X:R_SH_X + 1, :]
            dmod_ref[e:e + 1, D:2 * D] = vp_ref[e, R_SC_X:R_SC_X + 1, :]
            dmod_ref[e:e + 1, 2 * D:3 * D] = vp_ref[e, R_GATE:R_GATE + 1, :]
        dmod_ref[8:9, 0:D] = red[R_SH_C:R_SH_C + 1, :]
        dmod_ref[8:9, D:2 * D] = red[R_SC_C:R_SC_C + 1, :]
        dmod_ref[8:9, 2 * D:3 * D] = jnp.zeros((1, D), F32)
        dmod_ref[9:16, :] = jnp.zeros((7, 3 * D), F32)
        gab_ref[:, 0:D] = red[R_SH_X:R_SH_X + 1, :] + red[R_SH_C:R_SH_C + 1, :]
        gab_ref[:, D:2 * D] = red[R_SC_X:R_SC_X + 1, :] + red[R_SC_C:R_SC_C + 1, :]
        gab_ref[:, 2 * D:3 * D] = red[R_GATE:R_GATE + 1, :]
        dmc_s[...] = jnp.broadcast_to(dmod_ref[8:9, :], (8, 3 * D))
        off = pl.multiple_of(me_ref[0] * nloc, 128)
        cpart_ref[...] = _dot_nt(dmc_s[:, pl.ds(off, nloc)], w_ref[...])
        c_start()
        c_forward()
        c_finish()
        m_finish()

    return _call(
        body, name="reduce_small",
        out_shape=[jax.ShapeDtypeStruct((PACK_ROWS, D), F32), jax.ShapeDtypeStruct((PACK128_ROWS, HD), F32)]
        + [jax.ShapeDtypeStruct((N_DEV,) + p.shape[1:], F32) for p in mat_parts]
        + [jax.ShapeDtypeStruct((N_DEV, 8, D), F32), jax.ShapeDtypeStruct((16, 3 * D), F32),
           jax.ShapeDtypeStruct((1, 3 * D), F32)],
        in_specs=[pl.BlockSpec(memory_space=pltpu.SMEM)] + [VMEM] * (nm + 3), out_specs=[VMEM] * (nm + 5),
        scratch_shapes=[pltpu.VMEM((N_DEV, PACK_ROWS, D), F32), pltpu.VMEM((N_DEV, PACK128_ROWS, HD), F32)]
        + [pltpu.VMEM(p.shape[1:], F32) for p in mat_parts]
        + [pltpu.VMEM((8, D), F32), pltpu.VMEM((8, 3 * D), F32)] + _gather2_sems(2) + _gather2_sems(nm)
        + _gather2_sems(1),
        compiler_params=pltpu.CompilerParams(vmem_limit_bytes=VMEM_LIMIT, has_side_effects=True, collective_id=8),
    )(me, pack, pack128, *mat_parts, ada_w)


def _adamw_ada(c_all, c_ctx, dmod, w, m, v, me):
    nloc = w.shape[1]

    def body(me_ref, c_ref, cc_ref, dm_ref, w_ref, m_ref, v_ref, g_ref, d_ref, mo_ref, vo_ref):
        off = pl.multiple_of(me_ref[0] * nloc, 128)
        dm = dm_ref[:, pl.ds(off, nloc)]
        sx, _ = _silu_and_grad(c_ref[...])
        sc, _ = _silu_and_grad(cc_ref[...])
        g = _dot_tn(sx, dm[0:8, :]) + _dot_tn(jnp.broadcast_to(sc, (8, D)), dm[8:16, :])
        g_ref[...] = g
        d_ref[...], mo_ref[...], vo_ref[...] = _adamw(w_ref[...], g, m_ref[...], v_ref[...])

    sds = jax.ShapeDtypeStruct(w.shape, F32)
    return _call(
        body, name="adamw_ada_w", out_shape=[sds] * 4,
        in_specs=[pl.BlockSpec(memory_space=pltpu.SMEM)] + [VMEM] * 6, out_specs=[VMEM] * 4,
        compiler_params=_params(),
    )(me, c_all, c_ctx, dmod, w, m, v)


_SMALL = ("c_ctx", "ada_b", "norm_g", "conv_w", "conv_b", "lru_wa", "lru_ba", "lru_wx", "lru_bx", "lru_lambda",
          "sgu_ln_g", "sgu_ln_b", "sgu_w", "sgu_b", "final_g")


def _adamw_small(red, redq, mats, cparts, gab, ws, ms, vs, me):
    n = len(_SMALL)

    def body(me_ref, red_ref, redq_ref, wa_ref, wx_ref, sw_ref, cp_ref, gab_ref, *refs):
        w_refs, m_refs, v_refs = refs[:n], refs[n:2 * n], refs[2 * n:3 * n]
        outs = refs[3 * n:]
        off = pl.multiple_of(me_ref[0] * HD, 128)

        def row(r, k=1):
            return red_ref[r:r + k, :]

        cc = w_refs[0][...]
        dcc = cp_ref[0, 0:1, :]
        for i in range(1, N_DEV):
            dcc = dcc + cp_ref[i, 0:1, :]
        grads = dict(
            c_ctx=dcc * _silu_and_grad(cc)[1], ada_b=gab_ref[...], norm_g=row(R_NG_X) + row(R_NG_C),
            conv_w=red_ref[R_CW:R_CW + CONV_W, pl.ds(off, HD)], conv_b=row(R_CB),
            lru_wa=wa_ref[...], lru_ba=redq_ref[Q_BA:Q_BA + 2 * HEADS, :], lru_wx=wx_ref[...],
            lru_bx=redq_ref[Q_BX:Q_BX + 2 * HEADS, :], lru_lambda=red_ref[R_LAM:R_LAM + 2, pl.ds(off, HD)],
            sgu_ln_g=row(R_LN_G), sgu_ln_b=row(R_LN_B), sgu_w=sw_ref[...],
            sgu_b=redq_ref[Q_SGU_B:Q_SGU_B + HEADS, :], final_g=row(R_FINAL_G))
        for j, name in enumerate(_SMALL):
            g = grads[name]
            outs[j][...] = g
            outs[n + j][...], outs[2 * n + j][...], outs[3 * n + j][...] = _adamw(w_refs[j][...], g, m_refs[j][...],
                                                                                 v_refs[j][...])

    sds = [jax.ShapeDtypeStruct(ws[k].shape, F32) for k in _SMALL]
    outs = _call(
        body, name="adamw_small", out_shape=sds * 4,
        in_specs=[pl.BlockSpec(memory_space=pltpu.SMEM)] + [VMEM] * (7 + 3 * n), out_specs=[VMEM] * (4 * n),
        compiler_params=_params(),
    )(me, red, redq, *mats, cparts, gab, *[ws[k] for k in _SMALL], *[ms[k] for k in _SMALL],
      *[vs[k] for k in _SMALL])
    return [dict(zip(_SMALL, outs[i * n:(i + 1) * n])) for i in range(4)]


def kernel(x, c, ctx, c_ctx, ada_w, ada_b, norm_g, w_in, conv_w, conv_b, lru_wa, lru_ba, lru_wx, lru_bx, lru_lambda, sgu_ln_g, sgu_ln_b, sgu_w, sgu_b, w_out, final_g, loss_target, m_c_ctx, m_ada_w, m_ada_b, m_norm_g, m_w_in, m_conv_w, m_conv_b, m_lru_wa, m_lru_ba, m_lru_wx, m_lru_bx, m_lru_lambda, m_sgu_ln_g, m_sgu_ln_b, m_sgu_w, m_sgu_b, m_w_out, m_final_g, v_c_ctx, v_ada_w, v_ada_b, v_norm_g, v_w_in, v_conv_w, v_conv_b, v_lru_wa, v_lru_ba, v_lru_wx, v_lru_bx, v_lru_lambda, v_sgu_ln_g, v_sgu_ln_b, v_sgu_w, v_sgu_b, v_w_out, v_final_g):
    args = dict(locals())
    me = (4 * lax.axis_index("x") + 2 * lax.axis_index("y") + lax.axis_index("c")).astype(jnp.int32).reshape(1)
    xr, ctxr, tgt = x[0], ctx[0], loss_target[0]
    cc = c_ctx.reshape(1, D)
    nw = 2 * HEADS * HD
    view = dict(c_ctx=(1, D), ada_b=(1, 3 * D), norm_g=(1, D), conv_w=(CONV_W, HD), conv_b=(1, D), lru_wa=(nw, HD),
                lru_ba=(2 * HEADS, HD), lru_wx=(nw, HD), lru_bx=(2 * HEADS, HD), lru_lambda=(2, HD), sgu_ln_g=(1, D),
                sgu_ln_b=(1, D), sgu_w=(HEADS * CHUNK, CHUNK), sgu_b=(HEADS, CHUNK), final_g=(1, D))

    zx, hn, w_full, w_out_b, modx, modc, c_all, cw_full, lam_full = _front_project(
        xr, c, cc, ada_w[0], ada_b, norm_g, w_in[0], w_out[0], conv_w[0], lru_lambda[0], me)
    zc, hnc = _project(ctxr, modc, norm_g, w_full, D, LC, "project_ctx")
    ba, bx = lru_ba.reshape(view["lru_ba"]), lru_bx.reshape(view["lru_bx"])
    yl, wout_all = _lru_forward(zx, zc, cw_full, conv_b, lru_wa[0], lru_wx[0], ba, bx, lam_full, [w_out_b], ["ag"])
    wout_full = wout_all.reshape(D_MIX, D)
    ws_b = sgu_w[0].astype(BF16)
    dz, dyl, dxn, ycat, dob, dws, dbst, mvec = _mixer_loss(
        xr, tgt, zx, yl, modx, final_g.reshape(1, D), sgu_ln_g, sgu_ln_b, ws_b, jnp.swapaxes(ws_b, 1, 2),
        sgu_b[0].T, wout_full, ROWS)

    (wout_sums,) = _grad_w(ycat, dob, None, None, L, "grad_w_out", D, 0, 1, "rows", 1)
    (rest_sums,) = _grad_w(hn, dz, None, None, L, "grad_w_in_rest", 2 * W_IN_SHARD, 1, 3, "cols", 2)
    dz, dxac, dwa, dwx, dba, dbx, dlam, dcw, dcb, win_parts, wout_parts, _, _ = _lru_backward(
        zx, zc, dyl, dz, cw_full, conv_b, lru_wa[0], lru_wx[0], ba, bx, lam_full, [rest_sums, wout_sums],
        first_chips=[1, 0])
    mats = [dwa.reshape(N_DEV, nw // N_DEV, HD), dwx.reshape(N_DEV, nw // N_DEV, HD), dws]
    mat_sums = _reduce2_local(mats, me, "reduce_mat", 4, BF16)
    first_sums, *mat_parts = _grad_w(hn, dz, hnc, dxac, L, "grad_w_in_first", 2 * W_IN_SHARD, 0, 1, "cols", 3,
                                     chip_sums=mat_sums)[:4]
    gx, xvec, win_parts = _grad_rows(
        xr, dz, w_full, modx, norm_g, dxn, D_IN, ROWS, "grad_rows_x", chip_sums=[first_sums], first_chips=[0],
        dests=[win_parts])[:3]
    (cvec,) = _grad_rows(ctxr, dxac, w_full, modc, norm_g, None, D, LC, "grad_rows_ctx")
    pack = jnp.concatenate([mvec[0:5], xvec[0:3], cvec[0:3], dlam, dcw, dcb,
                            jnp.zeros((PACK_ROWS - R_CB - 1, D), F32)], axis=0)
    pack128 = jnp.concatenate([dba, dbx, dbst.T], axis=0)
    red, redq, *rest = _reduce_small(pack, pack128, mat_parts, ada_w[0], me)
    mats_all, (cparts, dmod, gab) = rest[:3], rest[3:]

    g_w_in, d_w_in, nm_w_in, nv_w_in = _adamw_reduced(win_parts, w_in[0], m_w_in[0], v_w_in[0], ROWS, "adamw_w_in")
    g_w_out, d_w_out, nm_w_out, nv_w_out = _adamw_reduced(wout_parts, w_out[0], m_w_out[0], v_w_out[0], ROWS // 2,
                                                          "adamw_w_out")
    g_ada, d_ada, nm_ada, nv_ada = _adamw_ada(c_all, cc, dmod, ada_w[0], m_ada_w[0], v_ada_w[0], me)
    ws = {k: args[k].reshape(view[k]) for k in _SMALL}
    ms = {k: args["m_" + k].reshape(view[k]) for k in _SMALL}
    vs = {k: args["v_" + k].reshape(view[k]) for k in _SMALL}
    small = _adamw_small(red, redq, [m.reshape(-1, HD) for m in mats_all], cparts, gab, ws, ms, vs, me)
    big = dict(w_in=(g_w_in, d_w_in, nm_w_in, nv_w_in), w_out=(g_w_out, d_w_out, nm_w_out, nv_w_out),
               ada_w=(g_ada, d_ada, nm_ada, nv_ada))

    loss = red[R_LOSS, 0]
    names = ("c_ctx", "ada_w", "ada_b", "norm_g", "w_in", "conv_w", "conv_b", "lru_wa", "lru_ba", "lru_wx", "lru_bx",
             "lru_lambda", "sgu_ln_g", "sgu_ln_b", "sgu_w", "sgu_b", "w_out", "final_g")
    outs = [loss, gx.reshape(x.shape)]
    for kind in range(4):
        for k in names:
            val = big[k][kind] if k in big else small[kind][k]
            outs.append(val.reshape(args[k].shape))
    return tuple(outs)
```

```python
import jax
import jax.numpy as jnp
from jax import lax
from jax.experimental import pallas as pl
from jax.experimental.pallas import tpu as pltpu

F32 = jnp.float32
BF16 = jnp.bfloat16

N_DEV = 8
D = 1024
L = 2048
LC = 256
HEADS = 8
HD = 128
CHUNK = 128
D_IN = 5 * D
W_IN_SHARD = D_IN // N_DEV
ROWS = 256
D_MIX = 2 * D
CONV_W = 4
LRU_C = 8.0
NORM_EPS = 1e-6
LN_EPS = 1e-5
ADAM_LR, ADAM_B1, ADAM_B2, ADAM_EPS, ADAM_WD, ADAM_STEP = 0.001, 0.9, 0.999, 1e-08, 0.01, 10

VMEM_LIMIT = 56 * 1024 * 1024

HBM = pl.BlockSpec(memory_space=pltpu.HBM)
VMEM = pl.BlockSpec(memory_space=pltpu.VMEM)
MESH = pl.DeviceIdType.MESH


def _call(body, **kw):
    return pl.pallas_call(body, **kw)


def _params(*sem):
    return pltpu.CompilerParams(dimension_semantics=sem, vmem_limit_bytes=VMEM_LIMIT)


def _sigmoid(x):
    return 0.5 * jnp.tanh(0.5 * x) + 0.5


def _silu_and_grad(x):
    s = _sigmoid(x)
    return x * s, s * (1.0 + x * (1.0 - s))


_G0 = 0.7978845608028654
_G1 = 0.044715


def _gelu_and_grad(x):
    x2 = x * x
    t = jnp.tanh(_G0 * (x + _G1 * x * x2))
    cdf = 0.5 * (1.0 + t)
    return x * cdf, cdf + 0.5 * x * (1.0 - t * t) * (_G0 * (1.0 + 3.0 * _G1 * x2))


def _softplus(z):
    t = jnp.exp(-jnp.abs(z))
    u = 1.0 + t
    log1p = jnp.where(u == 1.0, t, jnp.log(u) * t / jnp.where(u == 1.0, 1.0, u - 1.0))
    return jnp.maximum(z, 0.0) + log1p


def _dot(a, b):
    return jnp.dot(a, b, preferred_element_type=F32)


def _dot_nt(a, b):
    return lax.dot_general(a, b, (((1,), (1,)), ((), ())), preferred_element_type=F32)


def _dot_tn(a, b):
    return lax.dot_general(a, b, (((0,), (0,)), ((), ())), preferred_element_type=F32)


def _rows(shape):
    return lax.broadcasted_iota(jnp.int32, shape, 0)


def _gather2_shapes(arrays, modes):
    return [jax.ShapeDtypeStruct((N_DEV,) + a.shape if m == "ag" else (a.shape[0], N_DEV * a.shape[1]), a.dtype)
            for a, m in zip(arrays, modes)]


def _gather2_sems(n):
    return [pltpu.SemaphoreType.DMA((n, N_DEV - 1)), pltpu.SemaphoreType.DMA((n, N_DEV - 1)),
            pltpu.SemaphoreType.DMA((n,))]


def _barrier(peers):
    sem = pltpu.get_barrier_semaphore()
    for peer in peers:
        pl.semaphore_signal(sem, inc=1, device_id=peer, device_id_type=MESH)
    pl.semaphore_wait(sem, len(peers))


def _gather2_ops(ins, outs, modes, send_sems, recv_sems, local_sems, barrier=False):
    n = len(ins)
    x, y, c = lax.axis_index("x"), lax.axis_index("y"), lax.axis_index("c")
    me, sibling = (x, y, c), (x, y, 1 - c)
    chips = [(x ^ (k >> 1), y ^ (k & 1)) for k in (1, 2, 3)]

    def slot(j, px, py, pc):
        dev = 4 * px + 2 * py + pc
        if modes[j] == "agc":
            w = ins[j].shape[1]
            return outs[j].at[:, pl.ds(pl.multiple_of(dev * w, 128), w)]
        return outs[j].at[dev]

    def copy(j, k, block, to, src=None):
        return pltpu.make_async_remote_copy(
            src_ref=slot(j, *block) if src is None else src, dst_ref=slot(j, *block),
            send_sem=send_sems.at[j, k], recv_sem=recv_sems.at[j, k], device_id=to, device_id_type=MESH)

    def own(j):
        return pltpu.make_async_copy(ins[j], slot(j, *me), local_sems.at[j])

    def first(j):
        return [copy(j, 0, me, sibling, src=ins[j])] + [copy(j, 1 + i, me, (*chip, c), src=ins[j])
                                                        for i, chip in enumerate(chips)]

    def passed(j, i):
        return copy(j, 4 + i, (*chips[i], c), sibling)

    def start():
        if barrier:
            _barrier([sibling] + [(*chip, c) for chip in chips])
        for j in range(n):
            own(j).start()
            for cp in first(j):
                cp.start()

    def forward():
        for i, chip in enumerate(chips):
            for j in range(n):
                copy(j, 1 + i, (*chip, c), me).wait_recv()
                passed(j, i).start()

    def finish():
        for j in range(n):
            copy(j, 0, sibling, me).wait_recv()
            for i, chip in enumerate(chips):
                copy(j, 4 + i, (*chip, 1 - c), me).wait_recv()
            for cp in first(j) + [passed(j, i) for i in range(3)]:
                cp.wait_send()
            own(j).wait()

    return start, forward, finish


def _sibling_barrier():
    sem = pltpu.get_barrier_semaphore()
    sibling = (lax.axis_index("x"), lax.axis_index("y"), 1 - lax.axis_index("c"))
    pl.semaphore_signal(sem, inc=1, device_id=sibling, device_id_type=MESH)
    pl.semaphore_wait(sem, 1)


def _reduce2_local(arrays, me, name, barrier_id, out_dtype):
    n = len(arrays)
    staged = [jax.ShapeDtypeStruct((4,) + a.shape[1:], a.dtype) for a in arrays]

    def to_sibling(*refs):
        ins, outs = refs[:n], refs[n:2 * n]
        send_sems, recv_sems = refs[2 * n:]
        x, y, c = lax.axis_index("x"), lax.axis_index("y"), lax.axis_index("c")
        _sibling_barrier()
        copies = []
        for j in range(n):
            for q in range(4):
                cp = pltpu.make_async_remote_copy(
                    src_ref=ins[j].at[2 * q + (1 - c)], dst_ref=outs[j].at[q], send_sem=send_sems.at[j, q],
                    recv_sem=recv_sems.at[j, q], device_id=(x, y, 1 - c), device_id_type=MESH)
                cp.start()
                copies.append(cp)
        for cp in copies:
            cp.wait()

    stage = _call(
        to_sibling, name=name + "_d2d", out_shape=staged, in_specs=[HBM] * n, out_specs=[HBM] * n,
        scratch_shapes=[pltpu.SemaphoreType.DMA((n, 4)), pltpu.SemaphoreType.DMA((n, 4))],
        compiler_params=pltpu.CompilerParams(has_side_effects=True, collective_id=barrier_id),
    )(*[pltpu.with_memory_space_constraint(a, pltpu.HBM) for a in arrays])

    def add(me_ref, *refs):
        del me_ref
        own, got, outs = refs[:n], refs[n:2 * n], refs[2 * n:]
        for j in range(n):
            outs[j][0] = (own[j][0].astype(F32) + got[j][0].astype(F32)).astype(out_dtype)

    own_specs = [pl.BlockSpec((1,) + a.shape[1:], lambda q, me_ref: (2 * q + me_ref[0] % 2, 0, 0)) for a in arrays]
    slot_specs = [pl.BlockSpec((1,) + a.shape[1:], lambda q, me_ref: (q, 0, 0)) for a in arrays]
    return _call(
        add, name=name + "_add", out_shape=[jax.ShapeDtypeStruct(s.shape, out_dtype) for s in staged],
        grid_spec=pltpu.PrefetchScalarGridSpec(num_scalar_prefetch=1, grid=(4,), in_specs=own_specs + slot_specs,
                                               out_specs=slot_specs),
        compiler_params=_params("arbitrary"),
    )(me, *arrays, *stage)


def _chips_sems(n):
    return [pltpu.SemaphoreType.DMA((n, 6)), pltpu.SemaphoreType.DMA((n, 6)), pltpu.SemaphoreType.DMA((n,))]


def _chips_stage_shapes(chip_sums):
    return [jax.ShapeDtypeStruct((2, a.shape[1] // 2, a.shape[2]), a.dtype) for a in chip_sums]


def _chips_ops(ins, outs, stages, send_sems, recv_sems, local_sems, first_chips=None, barrier=False):
    x, y, c = lax.axis_index("x"), lax.axis_index("y"), lax.axis_index("c")
    qm = 2 * x + y
    first_chips = first_chips or [0] * len(ins)

    def owns(j, chip):
        lo, cnt = first_chips[j], ins[j].shape[0]
        if lo == 0 and cnt == 4:
            return None
        return jnp.logical_and(chip >= lo, chip < lo + cnt)

    def guarded(cond, fn):
        if cond is None:
            fn()
        else:
            pl.when(cond)(fn)

    def slot(j, chip):
        return jnp.clip(chip - first_chips[j], 0, ins[j].shape[0] - 1)

    def half(j, i):
        h = ins[j].shape[1] // 2
        return pl.ds(i * h, h)

    def copy(j, sem, src, dst, k):
        return pltpu.make_async_remote_copy(
            src_ref=src, dst_ref=dst, send_sem=send_sems.at[j, sem], recv_sem=recv_sems.at[j, sem],
            device_id=(x ^ (k >> 1), y ^ (k & 1), c), device_id_type=MESH)

    def direct(j, k):
        return copy(j, k - 1, ins[j].at[slot(j, qm ^ k)], outs[j].at[qm], k)

    def first_hop(j, k):
        return copy(j, 1 + k, ins[j].at[slot(j, qm ^ 3), half(j, k - 1)], stages[j].at[k - 1], k)

    def second_hop(j, k):
        return copy(j, 3 + k, stages[j].at[2 - k], outs[j].at[qm ^ (3 - k), half(j, 2 - k)], k)

    def local(j):
        return pltpu.make_async_copy(ins[j].at[slot(j, qm)], outs[j].at[qm], local_sems.at[j])

    def start():
        if barrier:
            _barrier([(x ^ (k >> 1), y ^ (k & 1), c) for k in (1, 2)])
        for j in range(len(ins)):
            for k in (1, 2):
                guarded(owns(j, qm ^ 3), lambda j=j, k=k: first_hop(j, k).start())
        for j in range(len(ins)):
            for k in (1, 2):
                guarded(owns(j, qm ^ k), lambda j=j, k=k: direct(j, k).start())
            guarded(owns(j, qm), lambda j=j: local(j).start())

    def forward():
        for j in range(len(ins)):
            for k in (1, 2):
                def pass_on(j=j, k=k):
                    first_hop(j, 3 - k).wait_recv()
                    second_hop(j, k).start()
                guarded(owns(j, qm ^ k), pass_on)

    def finish():
        for j in range(len(ins)):
            for k in (1, 2):
                guarded(owns(j, qm ^ k), lambda j=j, k=k: direct(j, k).wait_send())
                guarded(owns(j, qm ^ k), lambda j=j, k=k: second_hop(j, k).wait_send())
                guarded(owns(j, qm ^ 3), lambda j=j, k=k: first_hop(j, k).wait_send())
                guarded(owns(j, qm), lambda j=j, k=k: direct(j, k).wait_recv())
                guarded(owns(j, qm), lambda j=j, k=k: second_hop(j, k).wait_recv())
            guarded(owns(j, qm), lambda j=j: local(j).wait())

    return start, forward, finish


ARRIVAL = (0, 1, 2, 4, 3, 5, 6, 7)


def _front_project(xr, c, c_ctx, ada_w, ada_b, ng, w_in, w_out, cw, lam, me):
    nloc = ada_w.shape[1]
    ws = W_IN_SHARD
    order = me[0] ^ jnp.asarray(ARRIVAL, jnp.int32)

    def body(ord_ref, x_ref, c_ref, cc_ref, aw_ref, ab_ref, ng_ref, win_ref, wout_ref, cw_ref, lam_ref,
             z_ref, hn_ref, wfull_ref, woutb_ref, modx_ref, modc_ref, call_ref, cwf_ref, lamf_ref,
             wv, call_s, part_s, parts_s, w_send, w_recv, hbm_sems, s_send, s_recv, g_send, g_recv, g_local):
        t = pl.program_id(0)
        x, y, cidx = lax.axis_index("x"), lax.axis_index("y"), lax.axis_index("c")
        me_i = ord_ref[0]
        sibling = (x, y, 1 - cidx)
        chips = [(x ^ (k >> 1), y ^ (k & 1)) for k in (1, 2, 3)]
        g_start, g_pass, g_finish = _gather2_ops([cw_ref, lam_ref], [cwf_ref, lamf_ref], ["agc", "agc"],
                                                 g_send, g_recv, g_local)

        def shard_copy(k, px, py, pc, to, half=None):
            slot = wv.at[4 * px + 2 * py + pc]
            if half is not None:
                slot = slot.at[pl.ds(half * (D // 2), D // 2), :]
            return pltpu.make_async_remote_copy(src_ref=slot, dst_ref=slot, send_sem=w_send.at[k],
                                                recv_sem=w_recv.at[k], device_id=to, device_id_type=MESH)

        def small_gather(src, my_slot, stage):
            copies = []
            for k in range(1, N_DEV):
                peer = (x ^ (k >> 2), y ^ ((k >> 1) & 1), cidx ^ (k & 1))
                cp = pltpu.make_async_remote_copy(src_ref=src, dst_ref=my_slot, send_sem=s_send.at[stage, k - 1],
                                                  recv_sem=s_recv.at[stage, k - 1], device_id=peer,
                                                  device_id_type=MESH)
                cp.start()
                copies.append(cp)
            pltpu.sync_copy(src, my_slot)
            return copies

        def finish_small(copies):
            for cp in copies:
                cp.wait()

        def to_neighbours(half):
            for i in (0, 1):
                shard_copy(1 + i, x, y, cidx, (*chips[i], cidx), half=half).start()

        @pl.when(t == 0)
        def _():
            _barrier([(x ^ (k >> 2), y ^ ((k >> 1) & 1), cidx ^ (k & 1)) for k in range(1, N_DEV)])
            g_start()
            wv[me_i] = win_ref[...].astype(BF16)
            woutb_ref[...] = wout_ref[...].astype(BF16)
            shard_copy(0, x, y, cidx, sibling).start()
            finish_small(small_gather(c_ref, call_s.at[pl.ds(me_i, 1), :], 0))
            to_neighbours(0)
            call_ref[...] = call_s[...]
            off = pl.multiple_of(me_i * nloc, 128)
            b = ab_ref[:, pl.ds(off, nloc)]
            w = aw_ref[...]
            sx, _ = _silu_and_grad(call_s[...])
            sc, _ = _silu_and_grad(jnp.broadcast_to(cc_ref[...], (8, D)))
            part_s[0:8, :] = _dot(sx, w) + b
            part_s[8:16, :] = _dot(sc, w) + b
            parts_sent = small_gather(part_s, parts_s.at[me_i], 1)
            to_neighbours(1)
            finish_small(parts_sent)
            mine = _rows((16, nloc)) == me_i
            for j in range(N_DEV):
                pj = parts_s[j]
                modx_ref[:, j * nloc:(j + 1) * nloc] = jnp.sum(jnp.where(mine, pj, 0.0), axis=0, keepdims=True)
                modc_ref[:, j * nloc:(j + 1) * nloc] = pj[8:9, :]
            shift, scale1, ngv = modx_ref[:, 0:D], 1.0 + modx_ref[:, D:2 * D], ng_ref[...]
            for r in range(L // ROWS):
                rsl = slice(r * ROWS, (r + 1) * ROWS)
                xv = x_ref[rsl, :]
                rs = lax.rsqrt(jnp.mean(xv * xv, axis=-1, keepdims=True) + NORM_EPS)
                hn_ref[rsl, :] = ((xv * rs * ngv) * scale1 + shift).astype(BF16)

        @pl.when(t == 1)
        def _():
            shard_copy(0, x, y, 1 - cidx, sibling).wait_recv()
            g_pass()

        for i in (0, 1):
            @pl.when(t == ARRIVAL.index((2, 4)[i]))
            def _(i=i):
                shard_copy(1 + i, *chips[i], cidx, sibling).wait_recv()
                shard_copy(4 + i, *chips[i], cidx, sibling).start()
                shard_copy((7, 3)[i], *chips[i], cidx, (*chips[1 - i], cidx), half=i).start()

        @pl.when(t == ARRIVAL.index(6))
        def _():
            shard_copy(3, *chips[2], cidx, sibling, half=1).wait_recv()
            shard_copy(7, *chips[2], cidx, sibling, half=0).wait_recv()
            shard_copy(6, *chips[2], cidx, sibling).start()

        for i in range(3):
            @pl.when(t == ARRIVAL.index((3, 5, 7)[i]))
            def _(i=i):
                shard_copy(4 + i, *chips[i], 1 - cidx, sibling).wait_recv()

        @pl.when(t == 2)
        def _():
            g_finish()

        dev = ord_ref[t]
        for r in range(L // (2 * ROWS)):
            rsl = slice(r * 2 * ROWS, (r + 1) * 2 * ROWS)
            z_ref[rsl, :] = _dot(hn_ref[rsl, :], wv[dev])
        col = pl.ds(pl.multiple_of(dev * ws, 128), ws)
        pltpu.make_async_copy(wv.at[dev], wfull_ref.at[:, col], hbm_sems.at[t]).start()

        @pl.when(t == N_DEV - 1)
        def _():
            for k in (0, 1, 2, 4, 5, 6):
                shard_copy(k, x, y, cidx, sibling).wait_send()
            for k in (3, 7):
                shard_copy(k, x, y, cidx, sibling, half=0).wait_send()
            for s in range(N_DEV):
                pltpu.make_async_copy(wv.at[0], wfull_ref.at[:, pl.ds(0, ws)], hbm_sems.at[s]).wait()

    const = lambda *shape: pl.BlockSpec(shape, lambda t, o: (0,) * len(shape))
    once = lambda *shape: pl.BlockSpec(shape, lambda t, o: (0,) * len(shape), pipeline_mode=pl.Buffered(1))
    return _call(
        body, name="front_project",
        out_shape=[jax.ShapeDtypeStruct((L, D_IN), F32), jax.ShapeDtypeStruct((L, D), BF16),
                   jax.ShapeDtypeStruct((D, D_IN), BF16), jax.ShapeDtypeStruct(w_out.shape, BF16),
                   jax.ShapeDtypeStruct((1, 3 * D), F32), jax.ShapeDtypeStruct((1, 3 * D), F32),
                   jax.ShapeDtypeStruct((N_DEV, D), F32), jax.ShapeDtypeStruct((CONV_W, D), F32),
                   jax.ShapeDtypeStruct((2, D), F32)],
        grid_spec=pltpu.PrefetchScalarGridSpec(
            num_scalar_prefetch=1, grid=(N_DEV,),
            in_specs=[once(L, D), const(1, D), const(1, D), once(D, nloc), const(1, 3 * D), const(1, D),
                      once(D, ws), once(*w_out.shape), HBM, HBM],
            out_specs=[pl.BlockSpec((L, ws), lambda t, o: (0, o[t])), const(L, D), HBM, const(*w_out.shape),
                       const(1, 3 * D), const(1, 3 * D), const(N_DEV, D), HBM, HBM],
            scratch_shapes=[pltpu.VMEM((N_DEV, D, ws), BF16), pltpu.VMEM((N_DEV, D), F32), pltpu.VMEM((16, nloc), F32),
                            pltpu.VMEM((N_DEV, 16, nloc), F32), pltpu.SemaphoreType.DMA((8,)),
                            pltpu.SemaphoreType.DMA((8,)), pltpu.SemaphoreType.DMA((N_DEV,)),
                            pltpu.SemaphoreType.DMA((2, N_DEV - 1)), pltpu.SemaphoreType.DMA((2, N_DEV - 1))]
            + _gather2_sems(2)),
        compiler_params=pltpu.CompilerParams(dimension_semantics=("arbitrary",), vmem_limit_bytes=VMEM_LIMIT,
                                             has_side_effects=True, collective_id=10),
    )(order, xr, c, c_ctx, ada_w, ada_b, ng, w_in, w_out, pltpu.with_memory_space_constraint(cw, pltpu.HBM),
      pltpu.with_memory_space_constraint(lam, pltpu.HBM))


def _project(xr, mod, ng, w, ncols, tm, name):
    rows = xr.shape[0]

    def body(x_ref, sh_ref, sc_ref, ng_ref, w_ref, z_ref, hn_ref):
        x = x_ref[...]
        rs = lax.rsqrt(jnp.mean(x * x, axis=-1, keepdims=True) + NORM_EPS)
        hn = (x * rs * ng_ref[...]) * (1.0 + sc_ref[...]) + sh_ref[...]
        hb = hn.astype(BF16)
        hn_ref[...] = hb
        for n in range(ncols // D):
            z_ref[:, n * D:(n + 1) * D] = _dot(hb, w_ref[:, n * D:(n + 1) * D])

    vec = pl.BlockSpec((1, D), lambda i: (0, 0))
    return _call(
        body, name=name, grid=(rows // tm,),
        out_shape=[jax.ShapeDtypeStruct((rows, ncols), F32), jax.ShapeDtypeStruct((rows, D), BF16)],
        in_specs=[pl.BlockSpec((tm, D), lambda i: (i, 0)), vec, pl.BlockSpec((1, D), lambda i: (0, 1)), vec,
                  pl.BlockSpec((D, ncols), lambda i: (0, 0), pipeline_mode=pl.Buffered(1))],
        out_specs=[pl.BlockSpec((tm, ncols), lambda i: (i, 0)), pl.BlockSpec((tm, D), lambda i: (i, 0))],
        compiler_params=_params("arbitrary"),
    )(xr, mod, mod, ng, w)


def _scan_pair(af_ref, uf_ref, hf_ref, h0f, ab_ref, ub_ref, hb_ref, h0b, t_len):
    span = 8 * SCAN_BLOCKS
    nit = t_len // span
    rows = _rows((8, HD))

    def local_scan(a, b, forward):
        for s in (1, 2, 4):
            sh = s if forward else 8 - s
            m = rows >= s if forward else rows < 8 - s
            b = a * jnp.where(m, pltpu.roll(b, sh, 0), 0.0) + b
            a = a * jnp.where(m, pltpu.roll(a, sh, 0), 1.0)
        return a, b

    def span_scan(a_ref, u_ref, h_ref, off, carry, forward):
        order = range(SCAN_BLOCKS) if forward else range(SCAN_BLOCKS - 1, -1, -1)
        last = slice(7, 8) if forward else slice(0, 1)
        for q in order:
            rs = pl.ds(off + 8 * q, 8)
            a, b = local_scan(a_ref[rs, :], u_ref[rs, :], forward)
            h_ref[rs, :] = b + a * carry
            carry = a[last, :] * carry + b[last, :]
        return carry

    def body(k, carry):
        cf, cb = carry
        cf = span_scan(af_ref, uf_ref, hf_ref, pl.multiple_of(k * span, span), cf, True)
        cb = span_scan(ab_ref, ub_ref, hb_ref, pl.multiple_of((nit - 1 - k) * span, span), cb, False)
        return cf, cb

    return lax.fori_loop(0, nit, body, (h0f, h0b))


SCAN_BLOCKS = 8


def _shifted(pad_ref, x, offsets, before=0.0, after=0.0):
    n = x.shape[0]
    pad_ref[0:8, :] = jnp.broadcast_to(jnp.asarray(before, F32), (8, x.shape[1]))
    pad_ref[8:8 + n, :] = x
    pad_ref[8 + n:16 + n, :] = jnp.broadcast_to(jnp.asarray(after, F32), (8, x.shape[1]))
    return [pad_ref[8 + o:8 + o + n, :] for o in offsets]


def _conv(xa, cw, cb, pad_ref):
    xm1, xp1, xp2 = _shifted(pad_ref, xa, (-1, 1, 2))
    return xm1 * cw[0:1, :] + xa * cw[1:2, :] + xp1 * cw[2:3, :] + xp2 * cw[3:4, :] + cb


def _gates(xc, wa, wx, ba, bx, nsp):
    xb = xc.astype(BF16)
    r = _sigmoid(_dot(xb, wa) + ba)
    i = _sigmoid(_dot(xb, wx) + bx)
    log_a = r * nsp
    a = jnp.exp(log_a)
    g2 = jnp.tanh(log_a) * (-1.0 - a * a)
    rg = lax.rsqrt(jnp.maximum(g2, 1e-30))
    return r, i, a, g2 * rg, rg


def _lru_param_specs():
    h4 = pl.BlockSpec((2, 1, HD, HD), lambda h: (0, h, 0, 0))
    v2 = pl.BlockSpec((2, HD), lambda h: (0, h))
    b16 = pl.BlockSpec((2 * HEADS, HD), lambda h: (0, 0))
    return dict(
        xa=pl.BlockSpec((L, HD), lambda h: (0, h)), xac=pl.BlockSpec((LC, HD), lambda h: (0, h)),
        cw=pl.BlockSpec((CONV_W, HD), lambda h: (0, h)), cb=pl.BlockSpec((1, HD), lambda h: (0, h)), h4=h4, v2=v2,
        b16=b16)


def _bias_row(ref, d):
    mask = _rows((2 * HEADS, HD)) == d * HEADS + pl.program_id(0)
    return jnp.sum(jnp.where(mask, ref[...], 0.0), axis=0, keepdims=True), mask


def _lru_forward(zx, zc, cw, cb, wa, wx, ba, bx, lam, gather, gather_modes):
    ng_ = len(gather)

    def body(xa_ref, xac_ref, cw_ref, cb_ref, wa_ref, wx_ref, ba_ref, bx_ref, lam_ref, *rest):
        yl_ref = rest[ng_]
        af, uf, hf, ab, ub, hb, pad_s = rest[2 * ng_ + 1:2 * ng_ + 8]
        start, pass_on, finish = _gather2_ops(rest[:ng_], rest[ng_ + 1:2 * ng_ + 1], gather_modes,
                                              *rest[2 * ng_ + 8:], barrier=True)
        pl.when(pl.program_id(0) == 0)(start)
        pl.when(pl.program_id(0) == HEADS // 2)(pass_on)
        pl.when(pl.program_id(0) == HEADS - 1)(finish)
        cwv, cbv = cw_ref[...], cb_ref[...]
        nsp = (-LRU_C) * _softplus(-lam_ref[...])

        def forward(xa, t_len, h0f, h0b):
            xc = _conv(xa, cwv, cbv, pad_s)
            for d, (a_ref, u_ref) in enumerate(((af, uf), (ab, ub))):
                _, i, a, gamma, _ = _gates(xc, wa_ref[d, 0].astype(BF16), wx_ref[d, 0].astype(BF16),
                                           _bias_row(ba_ref, d)[0], _bias_row(bx_ref, d)[0], nsp[d:d + 1, :])
                a_ref[0:t_len, :] = a
                u_ref[0:t_len, :] = gamma * (i * xc)
            return _scan_pair(af, uf, hf, h0f, ab, ub, hb, h0b, t_len)

        z = jnp.zeros((1, HD), F32)
        h0f, h0b = forward(xac_ref[...], LC, z, z)
        forward(xa_ref[...], L, h0f, h0b)
        yl_ref[...] = hf[...] + hb[...]

    s = _lru_param_specs()
    return _call(
        body, name="lru_forward", grid=(HEADS,),
        out_shape=[jax.ShapeDtypeStruct((L, D), F32)] + _gather2_shapes(gather, gather_modes),
        in_specs=[s["xa"], s["xac"], s["cw"], s["cb"], s["h4"], s["h4"], s["b16"], s["b16"], s["v2"]] + [HBM] * ng_,
        out_specs=[pl.BlockSpec((L, HD), lambda h: (0, h))] + [HBM] * ng_,
        scratch_shapes=[pltpu.VMEM((L, HD), F32)] * 6 + [pltpu.VMEM((L + 16, HD), F32)] + _gather2_sems(ng_),
        compiler_params=pltpu.CompilerParams(dimension_semantics=("arbitrary",), vmem_limit_bytes=VMEM_LIMIT,
                                             has_side_effects=True, collective_id=5),
    )(zx, zc, cw, cb, wa, wx, ba, bx, lam, *[pltpu.with_memory_space_constraint(a, pltpu.HBM) for a in gather])


def _lru_backward(zx, zc, dyl, dz, cw, cb, wa, wx, ba, bx, lam, chip_sums, first_chips=None):
    nr = len(chip_sums)

    def body(xa_ref, xac_ref, dyl_ref, dz_in, cw_ref, cb_ref, wa_ref, wx_ref, ba_ref, bx_ref, lam_ref, *rest):
        (dxa_ref, dxac_ref, dwa_ref, dwx_ref, dba_ref, dbx_ref, dlam_ref, dcw_ref,
         dcb_ref) = rest[nr:nr + 9]
        main_s, ctx_s, pad_s = rest[3 * nr + 9:3 * nr + 12]
        if nr:
            start, forward, finish = _chips_ops(rest[:nr], rest[nr + 9:2 * nr + 9], rest[2 * nr + 9:3 * nr + 9],
                                                *rest[3 * nr + 12:], first_chips=first_chips, barrier=True)
            pl.when(pl.program_id(0) == 0)(start)
            pl.when(pl.program_id(0) == HEADS // 2)(forward)
            pl.when(pl.program_id(0) == HEADS - 1)(finish)
        del dz_in

        @pl.when(pl.program_id(0) == 0)
        def _():
            dba_ref[...] = jnp.zeros_like(dba_ref)
            dbx_ref[...] = jnp.zeros_like(dbx_ref)

        cwv, cbv = cw_ref[...], cb_ref[...]
        lamv = lam_ref[...]
        sp = _softplus(-lamv)
        nsp = (-LRU_C) * sp
        z = jnp.zeros((1, HD), F32)

        def wmat(ref, d):
            return ref[d, 0].astype(BF16)

        def workspace(s):
            return dict(a=(s.at[0], s.at[1]), u=(s.at[2], s.at[3]), h=(s.at[4], s.at[5]), rho=(s.at[6], s.at[7]),
                        saved=(tuple(s.at[8 + k] for k in range(4)), tuple(s.at[12 + k] for k in range(4))),
                        xc=s.at[16])

        def forward(ws, xa, t_len, h0f, h0b):
            xc = _conv(xa, cwv, cbv, pad_s)
            ws["xc"][...] = xc
            for d in (0, 1):
                vals = _gates(xc, wmat(wa_ref, d), wmat(wx_ref, d), _bias_row(ba_ref, d)[0],
                              _bias_row(bx_ref, d)[0], nsp[d:d + 1, :])
                r, i, a, gamma, rg = vals
                ws["a"][d][...] = a
                ws["u"][d][...] = gamma * (i * xc)
                for ref, val in zip(ws["saved"][d], (r, i, gamma, rg)):
                    ref[...] = val
            return _scan_pair(ws["a"][0], ws["u"][0], ws["h"][0], h0f, ws["a"][1], ws["u"][1], ws["h"][1], h0b,
                              t_len)

        def backward(ws, xa, t_len, h0f, h0b, dhf, dhb, first):
            xc = ws["xc"][...]
            (af, ab), (uf, ub), (hf, hb), (rf, rb) = ws["a"], ws["u"], ws["h"], ws["rho"]
            uf[...] = ab[...] * dhb
            ub[...] = af[...] * dhf
            rho_b_last, rho_f_first = _scan_pair(ab, uf, rb, z, af, ub, rf, z, t_len)
            dxc = jnp.zeros((t_len, HD), F32)
            dsp = []
            for d in (0, 1):
                r, i, gamma, rg = (ref[...] for ref in ws["saved"][d])
                a = ws["a"][d][...]
                if d == 0:
                    lam_t = dhf + _shifted(pad_s, rf[...], (1,))[0]
                    h_prev = _shifted(pad_s, hf[...], (-1,), before=h0f)[0]
                else:
                    lam_t = dhb + _shifted(pad_s, rb[...], (-1,))[0]
                    h_prev = _shifted(pad_s, hb[...], (1,), after=h0b)[0]
                da = lam_t * h_prev
                lx = lam_t * xc
                d_i = lx * gamma
                d_gamma = lx * i
                dxc = dxc + lam_t * (gamma * i)
                d_log_a = a * (da - d_gamma * (a * rg))
                dsp.append(jnp.sum(d_log_a * r, axis=0, keepdims=True) * (-LRU_C))
                d_pre_r = d_log_a * nsp[d:d + 1, :] * (r * (1.0 - r))
                d_pre_i = d_i * (i * (1.0 - i))
                prb, pib, xb = d_pre_r.astype(BF16), d_pre_i.astype(BF16), xc.astype(BF16)
                dxc = dxc + _dot_nt(prb, wmat(wa_ref, d)) + _dot_nt(pib, wmat(wx_ref, d))
                g_wa, g_wx = _dot_tn(xb, prb), _dot_tn(xb, pib)
                g_ba = jnp.sum(d_pre_r, axis=0, keepdims=True)
                g_bx = jnp.sum(d_pre_i, axis=0, keepdims=True)
                mask = _bias_row(ba_ref, d)[1]
                dba_ref[...] += jnp.where(mask, g_ba, 0.0)
                dbx_ref[...] += jnp.where(mask, g_bx, 0.0)
                if first:
                    dwa_ref[d, 0] = g_wa
                    dwx_ref[d, 0] = g_wx
                else:
                    dwa_ref[d, 0] += g_wa
                    dwx_ref[d, 0] += g_wx
            g_lam = jnp.concatenate(dsp, axis=0) * (-_sigmoid(-lamv))
            dm1, dp1, dm2 = _shifted(pad_s, dxc, (-1, 1, -2))
            dxa = dp1 * cwv[0:1, :] + dxc * cwv[1:2, :] + dm1 * cwv[2:3, :] + dm2 * cwv[3:4, :]
            xm1, xp1, xp2 = _shifted(pad_s, xa, (-1, 1, 2))
            g_cw = jnp.concatenate([jnp.sum(dxc * v, axis=0, keepdims=True) for v in (xm1, xa, xp1, xp2)], axis=0)
            g_cb = jnp.sum(dxc, axis=0, keepdims=True)
            if first:
                dlam_ref[...] = g_lam
                dcw_ref[...] = g_cw
                dcb_ref[...] = g_cb
            else:
                dlam_ref[...] += g_lam
                dcw_ref[...] += g_cw
                dcb_ref[...] += g_cb
            return dxa, rho_f_first, rho_b_last

        ws_x, ws_c = workspace(main_s), workspace(ctx_s)
        h0f, h0b = forward(ws_c, xac_ref[...], LC, z, z)
        forward(ws_x, xa_ref[...], L, h0f, h0b)
        dh = dyl_ref[...]
        dxa, dh0f, dh0b = backward(ws_x, xa_ref[...], L, h0f, h0b, dh, dh, True)
        dxa_ref[...] = dxa.astype(BF16)
        rc = _rows((LC, HD))
        dxac, _, _ = backward(ws_c, xac_ref[...], LC, z, z, jnp.where(rc == LC - 1, dh0f, 0.0),
                              jnp.where(rc == 0, dh0b, 0.0), False)
        dxac_ref[...] = dxac.astype(BF16)

    s = _lru_param_specs()
    col = lambda r: pl.BlockSpec((r, HD), lambda h: (0, h))
    return _call(
        body, name="lru_backward", grid=(HEADS,),
        out_shape=[jax.ShapeDtypeStruct((L, D_IN), BF16), jax.ShapeDtypeStruct((LC, D), BF16),
                   jax.ShapeDtypeStruct((2, HEADS, HD, HD), F32), jax.ShapeDtypeStruct((2, HEADS, HD, HD), F32),
                   jax.ShapeDtypeStruct((2 * HEADS, HD), F32), jax.ShapeDtypeStruct((2 * HEADS, HD), F32),
                   jax.ShapeDtypeStruct((2, D), F32), jax.ShapeDtypeStruct((CONV_W, D), F32),
                   jax.ShapeDtypeStruct((1, D), F32)] + [jax.ShapeDtypeStruct((4,) + a.shape[1:], a.dtype)
                                                          for a in chip_sums] + _chips_stage_shapes(chip_sums),
        in_specs=[s["xa"], s["xac"], col(L), pl.BlockSpec(memory_space=pl.ANY), s["cw"], s["cb"], s["h4"], s["h4"],
                  s["b16"], s["b16"], s["v2"]] + [HBM] * nr,
        out_specs=[col(L), col(LC), s["h4"], s["h4"], s["b16"], s["b16"], s["v2"], col(CONV_W), col(1)]
        + [HBM] * (2 * nr),
        scratch_shapes=[pltpu.VMEM((17, L, HD), F32), pltpu.VMEM((17, LC, HD), F32), pltpu.VMEM((L + 16, HD), F32)]
        + (_chips_sems(nr) if nr else []),
        input_output_aliases={3: 0},
        compiler_params=pltpu.CompilerParams(dimension_semantics=("arbitrary",), vmem_limit_bytes=VMEM_LIMIT,
                                             has_side_effects=True, collective_id=6 if nr else None),
    )(zx, zc, dyl, dz, cw, cb, wa, wx, ba, bx, lam, *[pltpu.with_memory_space_constraint(a, pltpu.HBM)
                                                       for a in chip_sums])


def _mixer_loss(x, tgt, zx, yl, gx, fg, lng, lnb, ws, wst, bst, wout, tm):
    ncht = tm // CHUNK

    def body(x_ref, t_ref, ga_ref, u_ref, v_ref, gb_ref, yl_ref, gx_ref, fg_ref, lng_ref, lnb_ref, ws_ref, wst_ref,
             bst_ref, wout_ref,
             dz_ref, dyl_ref, dxn_ref, y_s, do_ref, dws_ref, dbst_ref, vec_ref,
             vn_s, mix_s, dm_s, dvn_s):
        step = pl.program_id(0)

        @pl.when(step == 0)
        def _():
            dws_ref[...] = jnp.zeros_like(dws_ref)
            dbst_ref[...] = jnp.zeros_like(dbst_ref)
            vec_ref[...] = jnp.zeros_like(vec_ref)

        u, v = u_ref[...], v_ref[...]
        ug, dug_du = _gelu_and_grad(u)
        vg, dvg_dv = _gelu_and_grad(v)
        mu = jnp.mean(vg, axis=-1, keepdims=True)
        vc = vg - mu
        rstd = lax.rsqrt(jnp.mean(vc * vc, axis=-1, keepdims=True) + LN_EPS)
        vhat = vc * rstd
        lngv = lng_ref[...]
        vn_s[...] = (vhat * lngv + lnb_ref[...]).astype(BF16)
        for ch in range(ncht):
            rs = slice(ch * CHUNK, (ch + 1) * CHUNK)
            for g in range(HEADS):
                cs = slice(g * HD, (g + 1) * HD)
                mix_s[rs, cs] = _dot(ws_ref[g], vn_s[rs, cs]) + bst_ref[:, g:g + 1]
        mixed = mix_s[...]
        ga, gb, yl = ga_ref[...], gb_ref[...], yl_ref[...]
        sga, dsga = _silu_and_grad(ga)
        sgb, dsgb = _silu_and_grad(gb)
        ys = ug * mixed
        y_s[:, 0:D] = (yl * sga).astype(BF16)
        y_s[:, D:D_MIX] = (ys * sgb).astype(BF16)
        o = _dot(y_s[...], wout_ref[...])
        gxv, fgv = gx_ref[...], fg_ref[...]
        xn = x_ref[...] + gxv * o
        rs2 = lax.rsqrt(jnp.mean(xn * xn, axis=-1, keepdims=True) + NORM_EPS)
        xh = xn * rs2
        diff = xh * fgv - t_ref[...]
        vec_ref[R_LOSS:R_LOSS + 1, :] += jnp.full((1, D), jnp.sum(diff * diff) * (0.5 / D), F32)
        dout = diff * (1.0 / D)
        w = dout * fgv
        dxn = rs2 * (w - xh * jnp.mean(w * xh, axis=-1, keepdims=True))
        dxn_ref[...] = dxn
        vec_ref[0:1, :] += jnp.sum(dxn * o, axis=0, keepdims=True)
        vec_ref[1:2, :] += jnp.sum(dout * xh, axis=0, keepdims=True)
        dob = (dxn * gxv).astype(BF16)
        do_ref[...] = dob
        dy = _dot_nt(dob, wout_ref[...])
        dya, dyb = dy[:, 0:D], dy[:, D:D_MIX]
        dyl_ref[...] = dya * sga
        dys = dyb * sgb
        dz_ref[:, 0:D] = jnp.zeros((tm, D), BF16)
        dz_ref[:, D:2 * D] = (dya * yl * dsga).astype(BF16)
        dz_ref[:, 2 * D:3 * D] = (dys * mixed * dug_du).astype(BF16)
        dz_ref[:, 4 * D:5 * D] = (dyb * ys * dsgb).astype(BF16)
        dm = dys * ug
        dm_s[...] = dm.astype(BF16)
        for g in range(HEADS):
            cs = slice(g * HD, (g + 1) * HD)
            dbst_ref[:, g:g + 1] += sum(jnp.sum(dm[ch * CHUNK:(ch + 1) * CHUNK, cs], axis=1, keepdims=True)
                                        for ch in range(ncht))
            for ch in range(ncht):
                rs = slice(ch * CHUNK, (ch + 1) * CHUNK)
                dws_ref[g] += _dot_nt(dm_s[rs, cs], vn_s[rs, cs])
                dvn_s[rs, cs] = _dot(wst_ref[g], dm_s[rs, cs])
        dvn = dvn_s[...]
        vec_ref[2:3, :] += jnp.sum(dvn * vhat, axis=0, keepdims=True)
        vec_ref[3:4, :] += jnp.sum(dvn, axis=0, keepdims=True)
        dvh = dvn * lngv
        dvg = rstd * (dvh - jnp.mean(dvh, axis=-1, keepdims=True) - vhat * jnp.mean(dvh * vhat, axis=-1, keepdims=True))
        dz_ref[:, 3 * D:4 * D] = (dvg * dvg_dv).astype(BF16)

    tile = pl.BlockSpec((tm, D), lambda i: (i, 0))
    zcol = lambda n: pl.BlockSpec((tm, D), lambda i: (i, n))
    vec = pl.BlockSpec((1, D), lambda i: (0, 0))
    full = lambda *s: pl.BlockSpec(s, lambda i: (0,) * len(s))
    return _call(
        body, name="mixer_loss", grid=(L // tm,),
        out_shape=[jax.ShapeDtypeStruct((L, D_IN), BF16), jax.ShapeDtypeStruct((L, D), F32),
                   jax.ShapeDtypeStruct((L, D), F32), jax.ShapeDtypeStruct((L, D_MIX), BF16),
                   jax.ShapeDtypeStruct((L, D), BF16),
                   jax.ShapeDtypeStruct((HEADS, CHUNK, CHUNK), F32), jax.ShapeDtypeStruct((CHUNK, HEADS), F32),
                   jax.ShapeDtypeStruct((8, D), F32)],
        in_specs=[tile, tile, zcol(1), zcol(2), zcol(3), zcol(4), tile, pl.BlockSpec((1, D), lambda i: (0, 2)),
                  vec, vec, vec,
                  full(HEADS, CHUNK, CHUNK), full(HEADS, CHUNK, CHUNK), full(CHUNK, HEADS),
                  pl.BlockSpec((D_MIX, D), lambda i: (0, 0), pipeline_mode=pl.Buffered(1))],
        out_specs=[pl.BlockSpec((tm, D_IN), lambda i: (i, 0)), tile, tile,
                   pl.BlockSpec((tm, D_MIX), lambda i: (i, 0)), tile,
                   full(HEADS, CHUNK, CHUNK), full(CHUNK, HEADS), full(8, D)],
        scratch_shapes=[pltpu.VMEM((tm, D), BF16), pltpu.VMEM((tm, D), F32),
                        pltpu.VMEM((tm, D), BF16), pltpu.VMEM((tm, D), F32)],
        compiler_params=_params("arbitrary"),
    )(x, tgt, zx, zx, zx, zx, yl, gx, fg, lng, lnb, ws, wst, bst, wout)


def _grad_w(a, b, a2, b2, tk, name, bw, first, nblocks, split, barrier_id, chip_sums=()):
    nk = a.shape[0] // tk
    m = a.shape[1]
    with_ctx = a2 is not None
    if split == "cols":
        slots, r, w = nblocks, m, bw // 2
        piece = lambda q, pc: (slice(None), slice(pc * w, (pc + 1) * w))
    else:
        slots, r, w = 4, m // 8, bw
        piece = lambda q, pc: (slice((2 * q + pc) * r, (2 * q + pc + 1) * r), slice(None))

    nr = len(chip_sums)

    def body(*refs):
        a_ref, b_ref = refs[:2]
        a2_ref, b2_ref = refs[2:4] if with_ctx else (None, None)
        base = 4 if with_ctx else 2
        sums_ref = refs[base + nr]
        acc, mine_v, send_v, stage_v, send_sems, recv_sems = refs[base + 3 * nr + 1:base + 3 * nr + 7]
        n, k = pl.program_id(0), pl.program_id(1)
        x, y, c = lax.axis_index("x"), lax.axis_index("y"), lax.axis_index("c")

        def to_sibling(s):
            return pltpu.make_async_remote_copy(src_ref=send_v.at[s], dst_ref=stage_v.at[s], send_sem=send_sems.at[s],
                                                recv_sem=recv_sems.at[s], device_id=(x, y, 1 - c),
                                                device_id_type=MESH)

        if nr:
            c_start, c_forward, c_finish = _chips_ops(refs[base:base + nr], refs[base + nr + 1:base + 2 * nr + 1],
                                                      refs[base + 2 * nr + 1:base + 3 * nr + 1],
                                                      *refs[base + 3 * nr + 7:])

            @pl.when(jnp.logical_and(n == 0, k == 0))
            def _():
                _barrier([(x, y, 1 - c)] + [(x ^ (j >> 1), y ^ (j & 1), c) for j in (1, 2)])
                c_start()
        else:
            pl.when(jnp.logical_and(n == 0, k == 0))(_sibling_barrier)

        @pl.when(k == 0)
        def _():
            acc[...] = _dot_tn(a_ref[...], b_ref[...])

        if nk > 1:
            @pl.when(k > 0)
            def _():
                acc[...] += _dot_tn(a_ref[...], b_ref[...])

        if with_ctx:
            @pl.when(jnp.logical_and(k == nk - 1, n == 0))
            def _():
                acc[:, 0:b2_ref.shape[1]] += _dot_tn(a2_ref[...], b2_ref[...])

        if nr:
            pl.when(jnp.logical_and(k == nk - 1, n == nblocks - 1))(c_forward)

        def hand_over(s, q):
            for pc in (0, 1):
                @pl.when(c == pc)
                def _(pc=pc):
                    mine_v[s] = acc[piece(q, pc)]
                    send_v[s] = acc[piece(q, 1 - pc)].astype(BF16)
            to_sibling(s).start()

        for i in range(nblocks):
            @pl.when(jnp.logical_and(k == nk - 1, n == i))
            def _(i=i):
                if split == "cols":
                    hand_over(i, 0)
                else:
                    for q in range(4):
                        hand_over(q, q)

        @pl.when(jnp.logical_and(k == nk - 1, n == nblocks - 1))
        def _():
            for s in range(slots):
                to_sibling(s).wait_recv()
                sums_ref[s] = (mine_v[s] + stage_v[s].astype(F32)).astype(BF16)
            for s in range(slots):
                to_sibling(s).wait_send()
            if nr:
                c_finish()

    in_specs = [pl.BlockSpec((tk, m), lambda n, k: (k, 0)), pl.BlockSpec((tk, bw), lambda n, k: (k, n + first))]
    args = [a, b]
    if with_ctx:
        in_specs += [pl.BlockSpec(a2.shape, lambda n, k: (0, 0)), pl.BlockSpec(b2.shape, lambda n, k: (0, 0))]
        args += [a2, b2]
    in_specs += [HBM] * nr
    args += [pltpu.with_memory_space_constraint(s, pltpu.HBM) for s in chip_sums]
    return _call(
        body, name=name, grid=(nblocks, nk),
        out_shape=[jax.ShapeDtypeStruct((slots, r, w), BF16)]
        + [jax.ShapeDtypeStruct((4,) + s.shape[1:], s.dtype) for s in chip_sums] + _chips_stage_shapes(chip_sums),
        in_specs=in_specs, out_specs=[pl.BlockSpec((slots, r, w), lambda n, k: (0, 0, 0))] + [HBM] * (2 * nr),
        scratch_shapes=[pltpu.VMEM((m, bw), F32), pltpu.VMEM((slots, r, w), F32), pltpu.VMEM((slots, r, w), BF16),
                        pltpu.VMEM((slots, r, w), BF16), pltpu.SemaphoreType.DMA((slots,)),
                        pltpu.SemaphoreType.DMA((slots,))] + (_chips_sems(nr) if nr else []),
        compiler_params=pltpu.CompilerParams(dimension_semantics=("arbitrary", "arbitrary"),
                                             vmem_limit_bytes=VMEM_LIMIT, has_side_effects=True,
                                             collective_id=barrier_id),
    )(*args)


def _grad_rows(xr, dz, w, mod, ng, dres, ncols, tm, name, chip_sums=(), first_chips=None, dests=None):
    rows = xr.shape[0]
    steps = rows // tm
    with_dx = dres is not None
    nr = len(chip_sums)
    dests = [d for d in (dests or [None] * nr)]
    nd = sum(d is not None for d in dests)
    nin = 6 if with_dx else 5
    nout = 2 if with_dx else 1

    def body(*refs):
        if with_dx:
            x_ref, dz_ref, w_ref, sc_ref, ng_ref, dres_ref = refs[:nin]
            dx_ref, vec_ref = refs[nin + nr + nd:nin + nr + nd + nout]
        else:
            x_ref, dz_ref, w_ref, sc_ref, ng_ref = refs[:nin]
            (vec_ref,) = refs[nin + nr + nd:nin + nr + nd + nout]
        if nr:
            o0 = nin + nr + nd + nout
            start, forward, finish = _chips_ops(refs[nin:nin + nr], refs[o0:o0 + nr], refs[o0 + nr:o0 + 2 * nr],
                                                *refs[o0 + 2 * nr:], first_chips=first_chips, barrier=True)
            pl.when(pl.program_id(0) == 0)(start)
            pl.when(pl.program_id(0) == steps // 2)(forward)
            pl.when(pl.program_id(0) == steps - 1)(finish)

        @pl.when(pl.program_id(0) == 0)
        def _():
            vec_ref[...] = jnp.zeros_like(vec_ref)

        dhn = _dot_nt(dz_ref[...], w_ref[...])
        x = x_ref[...]
        rs = lax.rsqrt(jnp.mean(x * x, axis=-1, keepdims=True) + NORM_EPS)
        xh = x * rs
        ngv = ng_ref[...]
        y = xh * ngv
        vec_ref[0:1, :] += jnp.sum(dhn, axis=0, keepdims=True)
        vec_ref[1:2, :] += jnp.sum(dhn * y, axis=0, keepdims=True)
        dy = dhn * (1.0 + sc_ref[...])
        vec_ref[2:3, :] += jnp.sum(dy * xh, axis=0, keepdims=True)
        if with_dx:
            dxh = dy * ngv
            dx_ref[...] = dres_ref[...] + rs * (dxh - xh * jnp.mean(dxh * xh, axis=-1, keepdims=True))

    tile = pl.BlockSpec((tm, D), lambda i: (i, 0))
    vec = pl.BlockSpec((1, D), lambda i: (0, 0))
    in_specs = [tile, pl.BlockSpec((tm, ncols), lambda i: (i, 0)),
                pl.BlockSpec((D, ncols), lambda i: (0, 0), pipeline_mode=pl.Buffered(1)),
                pl.BlockSpec((1, D), lambda i: (0, 1)), vec]
    out_shape = [jax.ShapeDtypeStruct((8, D), F32)]
    out_specs = [pl.BlockSpec((8, D), lambda i: (0, 0))]
    args = [xr, dz, w, mod, ng]
    if with_dx:
        in_specs.append(tile)
        out_shape.insert(0, jax.ShapeDtypeStruct((rows, D), F32))
        out_specs.insert(0, tile)
        args.append(dres)
    aliases = {}
    for j, d in enumerate(dests):
        if d is not None:
            aliases[len(args) + nr + len(aliases)] = len(out_shape) + j
    in_specs += [HBM] * (nr + nd)
    out_specs += [HBM] * (2 * nr)
    out_shape += [jax.ShapeDtypeStruct((4,) + a.shape[1:], a.dtype) for a in chip_sums]
    out_shape += _chips_stage_shapes(chip_sums)
    args += [pltpu.with_memory_space_constraint(a, pltpu.HBM) for a in chip_sums]
    args += [pltpu.with_memory_space_constraint(d, pltpu.HBM) for d in dests if d is not None]
    return _call(body, name=name, grid=(steps,), out_shape=out_shape, in_specs=in_specs, out_specs=out_specs,
                 scratch_shapes=_chips_sems(nr) if nr else [], input_output_aliases=aliases,
                 compiler_params=pltpu.CompilerParams(dimension_semantics=("arbitrary",),
                                                      vmem_limit_bytes=VMEM_LIMIT, has_side_effects=bool(nr),
                                                      collective_id=7 if nr else None))(*args)


def _adamw(w, g, m, v):
    m = ADAM_B1 * m + (1.0 - ADAM_B1) * g
    v = ADAM_B2 * v + (1.0 - ADAM_B2) * (g * g)
    m_hat = m / (1.0 - ADAM_B1 ** ADAM_STEP)
    v_hat = v / (1.0 - ADAM_B2 ** ADAM_STEP)
    delta = -ADAM_LR * (m_hat / (jnp.sqrt(v_hat) + ADAM_EPS) + ADAM_WD * w)
    return delta, m, v


def _adamw_reduced(parts, w, m, v, tr, name):
    r, n = w.shape
    nparts = parts.shape[0]

    def body(p_ref, w_ref, m_ref, v_ref, g_ref, d_ref, mo_ref, vo_ref):
        g = p_ref[0].astype(F32)
        for i in range(1, nparts):
            g = g + p_ref[i].astype(F32)
        g_ref[...] = g
        d_ref[...], mo_ref[...], vo_ref[...] = _adamw(w_ref[...], g, m_ref[...], v_ref[...])

    tile = pl.BlockSpec((tr, n), lambda i: (i, 0))
    sds = jax.ShapeDtypeStruct((r, n), F32)
    return _call(
        body, name=name, grid=(r // tr,), out_shape=[sds] * 4,
        in_specs=[pl.BlockSpec((nparts, tr, n), lambda i: (0, i, 0)), tile, tile, tile], out_specs=[tile] * 4,
        compiler_params=_params("arbitrary"),
    )(parts, w, m, v)


R_GATE, R_FINAL_G, R_LN_G, R_LN_B, R_LOSS = 0, 1, 2, 3, 4
R_SH_X, R_SC_X, R_NG_X = 5, 6, 7
R_SH_C, R_SC_C, R_NG_C = 8, 9, 10
R_LAM, R_CW, R_CB = 11, 13, 17
PACK_ROWS = 24
Q_BA, Q_BX, Q_SGU_B, PACK128_ROWS = 0, 16, 32, 40


def _reduce_small(vec_pieces, q_pieces, mat_parts, ada_w, me):
    nloc = ada_w.shape[1]
    nm = len(mat_parts)
    pieces = list(vec_pieces) + list(q_pieces)

    def body(me_ref, *refs):
        piece_refs, refs = refs[:len(pieces)], refs[len(pieces):]
        mp_refs, w_ref = refs[:nm], refs[nm]
        red_ref, redq_ref = refs[nm + 1:nm + 3]
        mats_all = refs[nm + 3:2 * nm + 3]
        cparts_ref, dmod_ref, gab_ref = refs[2 * nm + 3:2 * nm + 6]
        pack_ref, packq_ref, vp_ref, vq_ref = refs[2 * nm + 6:2 * nm + 10]
        mat_refs = refs[2 * nm + 10:3 * nm + 10]
        cpart_ref, dmc_s = refs[3 * nm + 10:3 * nm + 12]
        sems = refs[3 * nm + 12:]
        for dst, group in ((pack_ref, vec_pieces), (packq_ref, q_pieces)):
            row = 0
            for _, nrows in group:
                dst[row:row + nrows, :] = piece_refs[0][0:nrows, :]
                piece_refs, row = piece_refs[1:], row + nrows
            if row < dst.shape[0]:
                dst[row:, :] = jnp.zeros((dst.shape[0] - row, dst.shape[1]), F32)
        p_start, p_forward, p_finish = _gather2_ops([pack_ref, packq_ref], [vp_ref, vq_ref], ["ag", "ag"], *sems[:3],
                                                    barrier=True)
        m_start, m_forward, m_finish = _gather2_ops(mat_refs, mats_all, ["ag"] * nm, *sems[3:6])
        c_start, c_forward, c_finish = _gather2_ops([cpart_ref], [cparts_ref], ["ag"], *sems[6:])
        p_start()
        for mp_ref, mat_ref in zip(mp_refs, mat_refs):
            mat = mp_ref[0].astype(F32)
            for i in range(1, mp_ref.shape[0]):
                mat = mat + mp_ref[i].astype(F32)
            mat_ref[...] = mat
        m_start()
        p_forward()
        p_finish()
        m_forward()
        red, redq = vp_ref[0], vq_ref[0]
        for i in range(1, N_DEV):
            red = red + vp_ref[i]
            redq = redq + vq_ref[i]
        red_ref[...] = red
        redq_ref[...] = redq
        for e in range(N_DEV):
            dmod_ref[e:e + 1, 0:D] = vp_ref[e, R_SH_X:R_SH_X + 1, :]
            dmod_ref[e:e + 1, D:2 * D] = vp_ref[e, R_SC_X:R_SC_X + 1, :]
            dmod_ref[e:e + 1, 2 * D:3 * D] = vp_ref[e, R_GATE:R_GATE + 1, :]
        dmod_ref[8:9, 0:D] = red[R_SH_C:R_SH_C + 1, :]
        dmod_ref[8:9, D:2 * D] = red[R_SC_C:R_SC_C + 1, :]
        dmod_ref[8:9, 2 * D:3 * D] = jnp.zeros((1, D), F32)
        dmod_ref[9:16, :] = jnp.zeros((7, 3 * D), F32)
        gab_ref[:, 0:D] = red[R_SH_X:R_SH_X + 1, :] + red[R_SH_C:R_SH_C + 1, :]
        gab_ref[:, D:2 * D] = red[R_SC_X:R_SC_X + 1, :] + red[R_SC_C:R_SC_C + 1, :]
        gab_ref[:, 2 * D:3 * D] = red[R_GATE:R_GATE + 1, :]
        dmc_s[...] = jnp.broadcast_to(dmod_ref[8:9, :], (8, 3 * D))
        off = pl.multiple_of(me_ref[0] * nloc, 128)
        cpart_ref[...] = _dot_nt(dmc_s[:, pl.ds(off, nloc)], w_ref[...])
        c_start()
        c_forward()
        c_finish()
        m_finish()

    return _call(
        body, name="reduce_small",
        out_shape=[jax.ShapeDtypeStruct((PACK_ROWS, D), F32), jax.ShapeDtypeStruct((PACK128_ROWS, HD), F32)]
        + [jax.ShapeDtypeStruct((N_DEV,) + p.shape[1:], F32) for p in mat_parts]
        + [jax.ShapeDtypeStruct((N_DEV, 8, D), F32), jax.ShapeDtypeStruct((16, 3 * D), F32),
           jax.ShapeDtypeStruct((1, 3 * D), F32)],
        in_specs=[pl.BlockSpec(memory_space=pltpu.SMEM)] + [VMEM] * (len(pieces) + nm + 1),
        out_specs=[VMEM] * (nm + 5),
        scratch_shapes=[pltpu.VMEM((PACK_ROWS, D), F32), pltpu.VMEM((PACK128_ROWS, HD), F32),
                        pltpu.VMEM((N_DEV, PACK_ROWS, D), F32), pltpu.VMEM((N_DEV, PACK128_ROWS, HD), F32)]
        + [pltpu.VMEM(p.shape[1:], F32) for p in mat_parts]
        + [pltpu.VMEM((8, D), F32), pltpu.VMEM((8, 3 * D), F32)] + _gather2_sems(2) + _gather2_sems(nm)
        + _gather2_sems(1),
        compiler_params=pltpu.CompilerParams(vmem_limit_bytes=VMEM_LIMIT, has_side_effects=True, collective_id=8),
    )(me, *[a for a, _ in pieces], *mat_parts, ada_w)


def _adamw_ada(c_all, c_ctx, dmod, w, m, v, me):
    nloc = w.shape[1]

    def body(me_ref, c_ref, cc_ref, dm_ref, w_ref, m_ref, v_ref, g_ref, d_ref, mo_ref, vo_ref):
        off = pl.multiple_of(me_ref[0] * nloc, 128)
        dm = dm_ref[:, pl.ds(off, nloc)]
        sx, _ = _silu_and_grad(c_ref[...])
        sc, _ = _silu_and_grad(cc_ref[...])
        g = _dot_tn(sx, dm[0:8, :]) + _dot_tn(jnp.broadcast_to(sc, (8, D)), dm[8:16, :])
        g_ref[...] = g
        d_ref[...], mo_ref[...], vo_ref[...] = _adamw(w_ref[...], g, m_ref[...], v_ref[...])

    sds = jax.ShapeDtypeStruct(w.shape, F32)
    return _call(
        body, name="adamw_ada_w", out_shape=[sds] * 4,
        in_specs=[pl.BlockSpec(memory_space=pltpu.SMEM)] + [VMEM] * 6, out_specs=[VMEM] * 4,
        compiler_params=_params(),
    )(me, c_all, c_ctx, dmod, w, m, v)


_SMALL = ("c_ctx", "ada_b", "norm_g", "conv_w", "conv_b", "lru_wa", "lru_ba", "lru_wx", "lru_bx", "lru_lambda",
          "sgu_ln_g", "sgu_ln_b", "sgu_w", "sgu_b", "final_g")


def _adamw_small(red, redq, mats, cparts, gab, ws, ms, vs, me):
    n = len(_SMALL)

    def body(me_ref, red_ref, redq_ref, wa_ref, wx_ref, sw_ref, cp_ref, gab_ref, *refs):
        w_refs, m_refs, v_refs = refs[:n], refs[n:2 * n], refs[2 * n:3 * n]
        outs = refs[3 * n:]
        off = pl.multiple_of(me_ref[0] * HD, 128)

        def row(r, k=1):
            return red_ref[r:r + k, :]

        cc = w_refs[0][...]
        dcc = cp_ref[0, 0:1, :]
        for i in range(1, N_DEV):
            dcc = dcc + cp_ref[i, 0:1, :]
        grads = dict(
            c_ctx=dcc * _silu_and_grad(cc)[1], ada_b=gab_ref[...], norm_g=row(R_NG_X) + row(R_NG_C),
            conv_w=red_ref[R_CW:R_CW + CONV_W, pl.ds(off, HD)], conv_b=row(R_CB),
            lru_wa=wa_ref[...], lru_ba=redq_ref[Q_BA:Q_BA + 2 * HEADS, :], lru_wx=wx_ref[...],
            lru_bx=redq_ref[Q_BX:Q_BX + 2 * HEADS, :], lru_lambda=red_ref[R_LAM:R_LAM + 2, pl.ds(off, HD)],
            sgu_ln_g=row(R_LN_G), sgu_ln_b=row(R_LN_B), sgu_w=sw_ref[...],
            sgu_b=redq_ref[Q_SGU_B:Q_SGU_B + HEADS, :], final_g=row(R_FINAL_G))
        for j, name in enumerate(_SMALL):
            g = grads[name]
            outs[j][...] = g
            outs[n + j][...], outs[2 * n + j][...], outs[3 * n + j][...] = _adamw(w_refs[j][...], g, m_refs[j][...],
                                                                                 v_refs[j][...])

    sds = [jax.ShapeDtypeStruct(ws[k].shape, F32) for k in _SMALL]
    outs = _call(
        body, name="adamw_small", out_shape=sds * 4,
        in_specs=[pl.BlockSpec(memory_space=pltpu.SMEM)] + [VMEM] * (7 + 3 * n), out_specs=[VMEM] * (4 * n),
        compiler_params=_params(),
    )(me, red, redq, *mats, cparts, gab, *[ws[k] for k in _SMALL], *[ms[k] for k in _SMALL],
      *[vs[k] for k in _SMALL])
    return [dict(zip(_SMALL, outs[i * n:(i + 1) * n])) for i in range(4)]


def kernel(x, c, ctx, c_ctx, ada_w, ada_b, norm_g, w_in, conv_w, conv_b, lru_wa, lru_ba, lru_wx, lru_bx, lru_lambda, sgu_ln_g, sgu_ln_b, sgu_w, sgu_b, w_out, final_g, loss_target, m_c_ctx, m_ada_w, m_ada_b, m_norm_g, m_w_in, m_conv_w, m_conv_b, m_lru_wa, m_lru_ba, m_lru_wx, m_lru_bx, m_lru_lambda, m_sgu_ln_g, m_sgu_ln_b, m_sgu_w, m_sgu_b, m_w_out, m_final_g, v_c_ctx, v_ada_w, v_ada_b, v_norm_g, v_w_in, v_conv_w, v_conv_b, v_lru_wa, v_lru_ba, v_lru_wx, v_lru_bx, v_lru_lambda, v_sgu_ln_g, v_sgu_ln_b, v_sgu_w, v_sgu_b, v_w_out, v_final_g):
    args = dict(locals())
    me = (4 * lax.axis_index("x") + 2 * lax.axis_index("y") + lax.axis_index("c")).astype(jnp.int32).reshape(1)
    xr, ctxr, tgt = x[0], ctx[0], loss_target[0]
    cc = c_ctx.reshape(1, D)
    nw = 2 * HEADS * HD
    view = dict(c_ctx=(1, D), ada_b=(1, 3 * D), norm_g=(1, D), conv_w=(CONV_W, HD), conv_b=(1, D), lru_wa=(nw, HD),
                lru_ba=(2 * HEADS, HD), lru_wx=(nw, HD), lru_bx=(2 * HEADS, HD), lru_lambda=(2, HD), sgu_ln_g=(1, D),
                sgu_ln_b=(1, D), sgu_w=(HEADS * CHUNK, CHUNK), sgu_b=(HEADS, CHUNK), final_g=(1, D))

    zx, hn, w_full, w_out_b, modx, modc, c_all, cw_full, lam_full = _front_project(
        xr, c, cc, ada_w[0], ada_b, norm_g, w_in[0], w_out[0], conv_w[0], lru_lambda[0], me)
    zc, hnc = _project(ctxr, modc, norm_g, w_full, D, LC, "project_ctx")
    ba, bx = lru_ba.reshape(view["lru_ba"]), lru_bx.reshape(view["lru_bx"])
    yl, wout_all = _lru_forward(zx, zc, cw_full, conv_b, lru_wa[0], lru_wx[0], ba, bx, lam_full, [w_out_b], ["ag"])
    wout_full = wout_all.reshape(D_MIX, D)
    ws_b = sgu_w[0].astype(BF16)
    dz, dyl, dxn, ycat, dob, dws, dbst, mvec = _mixer_loss(
        xr, tgt, zx, yl, modx, final_g.reshape(1, D), sgu_ln_g, sgu_ln_b, ws_b, jnp.swapaxes(ws_b, 1, 2),
        sgu_b[0].T, wout_full, ROWS)

    (wout_sums,) = _grad_w(ycat, dob, None, None, L, "grad_w_out", D, 0, 1, "rows", 1)
    (rest_sums,) = _grad_w(hn, dz, None, None, L, "grad_w_in_rest", 2 * W_IN_SHARD, 1, 3, "cols", 2)
    dz, dxac, dwa, dwx, dba, dbx, dlam, dcw, dcb, win_parts, wout_parts, _, _ = _lru_backward(
        zx, zc, dyl, dz, cw_full, conv_b, lru_wa[0], lru_wx[0], ba, bx, lam_full, [rest_sums, wout_sums],
        first_chips=[1, 0])
    mats = [dwa.reshape(N_DEV, nw // N_DEV, HD), dwx.reshape(N_DEV, nw // N_DEV, HD), dws]
    mat_sums = _reduce2_local(mats, me, "reduce_mat", 4, BF16)
    first_sums, *mat_parts = _grad_w(hn, dz, hnc, dxac, L, "grad_w_in_first", 2 * W_IN_SHARD, 0, 1, "cols", 3,
                                     chip_sums=mat_sums)[:4]
    gx, xvec, win_parts = _grad_rows(
        xr, dz, w_full, modx, norm_g, dxn, D_IN, ROWS, "grad_rows_x", chip_sums=[first_sums], first_chips=[0],
        dests=[win_parts])[:3]
    (cvec,) = _grad_rows(ctxr, dxac, w_full, modc, norm_g, None, D, LC, "grad_rows_ctx")
    red, redq, *rest = _reduce_small(
        [(mvec, 5), (xvec, 3), (cvec, 3), (dlam, 2), (dcw, CONV_W), (dcb, 1)],
        [(dba, 2 * HEADS), (dbx, 2 * HEADS), (dbst.T, HEADS)], mat_parts, ada_w[0], me)
    mats_all, (cparts, dmod, gab) = rest[:3], rest[3:]

    g_w_in, d_w_in, nm_w_in, nv_w_in = _adamw_reduced(win_parts, w_in[0], m_w_in[0], v_w_in[0], ROWS, "adamw_w_in")
    g_w_out, d_w_out, nm_w_out, nv_w_out = _adamw_reduced(wout_parts, w_out[0], m_w_out[0], v_w_out[0], ROWS // 2,
                                                          "adamw_w_out")
    g_ada, d_ada, nm_ada, nv_ada = _adamw_ada(c_all, cc, dmod, ada_w[0], m_ada_w[0], v_ada_w[0], me)
    ws = {k: args[k].reshape(view[k]) for k in _SMALL}
    ms = {k: args["m_" + k].reshape(view[k]) for k in _SMALL}
    vs = {k: args["v_" + k].reshape(view[k]) for k in _SMALL}
    small = _adamw_small(red, redq, [m.reshape(-1, HD) for m in mats_all], cparts, gab, ws, ms, vs, me)
    big = dict(w_in=(g_w_in, d_w_in, nm_w_in, nv_w_in), w_out=(g_w_out, d_w_out, nm_w_out, nv_w_out),
               ada_w=(g_ada, d_ada, nm_ada, nv_ada))

    loss = red[R_LOSS, 0]
    names = ("c_ctx", "ada_w", "ada_b", "norm_g", "w_in", "conv_w", "conv_b", "lru_wa", "lru_ba", "lru_wx", "lru_bx",
             "lru_lambda", "sgu_ln_g", "sgu_ln_b", "sgu_w", "sgu_b", "w_out", "final_g")
    outs = [loss, gx.reshape(x.shape)]
    for kind in range(4):
        for k in names:
            val = big[k][kind] if k in big else small[kind][k]
            outs.append(val.reshape(args[k].shape))
    return tuple(outs)
```

```python
import jax
import jax.numpy as jnp
from jax import lax
from jax.experimental import pallas as pl
from jax.experimental.pallas import tpu as pltpu

F32 = jnp.float32
BF16 = jnp.bfloat16

N_DEV = 8
D = 1024
L = 2048
LC = 256
HEADS = 8
HD = 128
CHUNK = 128
D_IN = 5 * D
W_IN_SHARD = D_IN // N_DEV
ROWS = 256
D_MIX = 2 * D
CONV_W = 4
LRU_C = 8.0
NORM_EPS = 1e-6
LN_EPS = 1e-5
ADAM_LR, ADAM_B1, ADAM_B2, ADAM_EPS, ADAM_WD, ADAM_STEP = 0.001, 0.9, 0.999, 1e-08, 0.01, 10

VMEM_LIMIT = 56 * 1024 * 1024

HBM = pl.BlockSpec(memory_space=pltpu.HBM)
VMEM = pl.BlockSpec(memory_space=pltpu.VMEM)
MESH = pl.DeviceIdType.MESH


def _call(body, **kw):
    return pl.pallas_call(body, **kw)


def _params(*sem):
    return pltpu.CompilerParams(dimension_semantics=sem, vmem_limit_bytes=VMEM_LIMIT)


def _sigmoid(x):
    return 0.5 * jnp.tanh(0.5 * x) + 0.5


def _silu_and_grad(x):
    s = _sigmoid(x)
    return x * s, s * (1.0 + x * (1.0 - s))


_G0 = 0.7978845608028654
_G1 = 0.044715


def _gelu_and_grad(x):
    x2 = x * x
    t = jnp.tanh(_G0 * (x + _G1 * x * x2))
    cdf = 0.5 * (1.0 + t)
    return x * cdf, cdf + 0.5 * x * (1.0 - t * t) * (_G0 * (1.0 + 3.0 * _G1 * x2))


def _softplus(z):
    t = jnp.exp(-jnp.abs(z))
    u = 1.0 + t
    log1p = jnp.where(u == 1.0, t, jnp.log(u) * t / jnp.where(u == 1.0, 1.0, u - 1.0))
    return jnp.maximum(z, 0.0) + log1p


def _dot(a, b):
    return jnp.dot(a, b, preferred_element_type=F32)


def _dot_nt(a, b):
    return lax.dot_general(a, b, (((1,), (1,)), ((), ())), preferred_element_type=F32)


def _dot_tn(a, b):
    return lax.dot_general(a, b, (((0,), (0,)), ((), ())), preferred_element_type=F32)


def _rows(shape):
    return lax.broadcasted_iota(jnp.int32, shape, 0)


def _gather2_shapes(arrays, modes):
    return [jax.ShapeDtypeStruct((N_DEV,) + a.shape if m == "ag" else (a.shape[0], N_DEV * a.shape[1]), a.dtype)
            for a, m in zip(arrays, modes)]


def _gather2_sems(n):
    return [pltpu.SemaphoreType.DMA((n, N_DEV - 1)), pltpu.SemaphoreType.DMA((n, N_DEV - 1)),
            pltpu.SemaphoreType.DMA((n,))]


def _barrier(peers):
    sem = pltpu.get_barrier_semaphore()
    for peer in peers:
        pl.semaphore_signal(sem, inc=1, device_id=peer, device_id_type=MESH)
    pl.semaphore_wait(sem, len(peers))


def _gather2_ops(ins, outs, modes, send_sems, recv_sems, local_sems, barrier=False):
    n = len(ins)
    x, y, c = lax.axis_index("x"), lax.axis_index("y"), lax.axis_index("c")
    me, sibling = (x, y, c), (x, y, 1 - c)
    chips = [(x ^ (k >> 1), y ^ (k & 1)) for k in (1, 2, 3)]

    def slot(j, px, py, pc):
        dev = 4 * px + 2 * py + pc
        if modes[j] == "agc":
            w = ins[j].shape[1]
            return outs[j].at[:, pl.ds(pl.multiple_of(dev * w, 128), w)]
        return outs[j].at[dev]

    def copy(j, k, block, to, src=None):
        return pltpu.make_async_remote_copy(
            src_ref=slot(j, *block) if src is None else src, dst_ref=slot(j, *block),
            send_sem=send_sems.at[j, k], recv_sem=recv_sems.at[j, k], device_id=to, device_id_type=MESH)

    def own(j):
        return pltpu.make_async_copy(ins[j], slot(j, *me), local_sems.at[j])

    def first(j):
        return [copy(j, 0, me, sibling, src=ins[j])] + [copy(j, 1 + i, me, (*chip, c), src=ins[j])
                                                        for i, chip in enumerate(chips)]

    def passed(j, i):
        return copy(j, 4 + i, (*chips[i], c), sibling)

    def start():
        if barrier:
            _barrier([sibling] + [(*chip, c) for chip in chips])
        for j in range(n):
            own(j).start()
            for cp in first(j):
                cp.start()

    def forward():
        for i, chip in enumerate(chips):
            for j in range(n):
                copy(j, 1 + i, (*chip, c), me).wait_recv()
                passed(j, i).start()

    def finish():
        for j in range(n):
            copy(j, 0, sibling, me).wait_recv()
            for i, chip in enumerate(chips):
                copy(j, 4 + i, (*chip, 1 - c), me).wait_recv()
            for cp in first(j) + [passed(j, i) for i in range(3)]:
                cp.wait_send()
            own(j).wait()

    return start, forward, finish


def _sibling_barrier():
    sem = pltpu.get_barrier_semaphore()
    sibling = (lax.axis_index("x"), lax.axis_index("y"), 1 - lax.axis_index("c"))
    pl.semaphore_signal(sem, inc=1, device_id=sibling, device_id_type=MESH)
    pl.semaphore_wait(sem, 1)


def _reduce2_local(arrays, me, name, barrier_id, out_dtype):
    n = len(arrays)
    staged = [jax.ShapeDtypeStruct((4,) + a.shape[1:], a.dtype) for a in arrays]

    def to_sibling(*refs):
        ins, outs = refs[:n], refs[n:2 * n]
        send_sems, recv_sems = refs[2 * n:]
        x, y, c = lax.axis_index("x"), lax.axis_index("y"), lax.axis_index("c")
        _sibling_barrier()
        copies = []
        for j in range(n):
            for q in range(4):
                cp = pltpu.make_async_remote_copy(
                    src_ref=ins[j].at[2 * q + (1 - c)], dst_ref=outs[j].at[q], send_sem=send_sems.at[j, q],
                    recv_sem=recv_sems.at[j, q], device_id=(x, y, 1 - c), device_id_type=MESH)
                cp.start()
                copies.append(cp)
        for cp in copies:
            cp.wait()

    stage = _call(
        to_sibling, name=name + "_d2d", out_shape=staged, in_specs=[HBM] * n, out_specs=[HBM] * n,
        scratch_shapes=[pltpu.SemaphoreType.DMA((n, 4)), pltpu.SemaphoreType.DMA((n, 4))],
        compiler_params=pltpu.CompilerParams(has_side_effects=True, collective_id=barrier_id),
    )(*[pltpu.with_memory_space_constraint(a, pltpu.HBM) for a in arrays])

    def add(me_ref, *refs):
        del me_ref
        own, got, outs = refs[:n], refs[n:2 * n], refs[2 * n:]
        for j in range(n):
            outs[j][0] = (own[j][0].astype(F32) + got[j][0].astype(F32)).astype(out_dtype)

    own_specs = [pl.BlockSpec((1,) + a.shape[1:], lambda q, me_ref: (2 * q + me_ref[0] % 2, 0, 0)) for a in arrays]
    slot_specs = [pl.BlockSpec((1,) + a.shape[1:], lambda q, me_ref: (q, 0, 0)) for a in arrays]
    return _call(
        add, name=name + "_add", out_shape=[jax.ShapeDtypeStruct(s.shape, out_dtype) for s in staged],
        grid_spec=pltpu.PrefetchScalarGridSpec(num_scalar_prefetch=1, grid=(4,), in_specs=own_specs + slot_specs,
                                               out_specs=slot_specs),
        compiler_params=_params("arbitrary"),
    )(me, *arrays, *stage)


def _chips_sems(n):
    return [pltpu.SemaphoreType.DMA((n, 6)), pltpu.SemaphoreType.DMA((n, 6)), pltpu.SemaphoreType.DMA((n,))]


def _chips_stage_shapes(chip_sums):
    return [jax.ShapeDtypeStruct((2, a.shape[1] // 2, a.shape[2]), a.dtype) for a in chip_sums]


def _chips_ops(ins, outs, stages, send_sems, recv_sems, local_sems, first_chips=None, barrier=False):
    x, y, c = lax.axis_index("x"), lax.axis_index("y"), lax.axis_index("c")
    qm = 2 * x + y
    first_chips = first_chips or [0] * len(ins)

    def owns(j, chip):
        lo, cnt = first_chips[j], ins[j].shape[0]
        if lo == 0 and cnt == 4:
            return None
        return jnp.logical_and(chip >= lo, chip < lo + cnt)

    def guarded(cond, fn):
        if cond is None:
            fn()
        else:
            pl.when(cond)(fn)

    def slot(j, chip):
        return jnp.clip(chip - first_chips[j], 0, ins[j].shape[0] - 1)

    def half(j, i):
        h = ins[j].shape[1] // 2
        return pl.ds(i * h, h)

    def copy(j, sem, src, dst, k):
        return pltpu.make_async_remote_copy(
            src_ref=src, dst_ref=dst, send_sem=send_sems.at[j, sem], recv_sem=recv_sems.at[j, sem],
            device_id=(x ^ (k >> 1), y ^ (k & 1), c), device_id_type=MESH)

    def direct(j, k):
        return copy(j, k - 1, ins[j].at[slot(j, qm ^ k)], outs[j].at[qm], k)

    def first_hop(j, k):
        return copy(j, 1 + k, ins[j].at[slot(j, qm ^ 3), half(j, k - 1)], stages[j].at[k - 1], k)

    def second_hop(j, k):
        return copy(j, 3 + k, stages[j].at[2 - k], outs[j].at[qm ^ (3 - k), half(j, 2 - k)], k)

    def local(j):
        return pltpu.make_async_copy(ins[j].at[slot(j, qm)], outs[j].at[qm], local_sems.at[j])

    def start():
        if barrier:
            _barrier([(x ^ (k >> 1), y ^ (k & 1), c) for k in (1, 2)])
        for j in range(len(ins)):
            for k in (1, 2):
                guarded(owns(j, qm ^ 3), lambda j=j, k=k: first_hop(j, k).start())
        for j in range(len(ins)):
            for k in (1, 2):
                guarded(owns(j, qm ^ k), lambda j=j, k=k: direct(j, k).start())
            guarded(owns(j, qm), lambda j=j: local(j).start())

    def forward():
        for j in range(len(ins)):
            for k in (1, 2):
                def pass_on(j=j, k=k):
                    first_hop(j, 3 - k).wait_recv()
                    second_hop(j, k).start()
                guarded(owns(j, qm ^ k), pass_on)

    def finish():
        for j in range(len(ins)):
            for k in (1, 2):
                guarded(owns(j, qm ^ k), lambda j=j, k=k: direct(j, k).wait_send())
                guarded(owns(j, qm ^ k), lambda j=j, k=k: second_hop(j, k).wait_send())
                guarded(owns(j, qm ^ 3), lambda j=j, k=k: first_hop(j, k).wait_send())
                guarded(owns(j, qm), lambda j=j, k=k: direct(j, k).wait_recv())
                guarded(owns(j, qm), lambda j=j, k=k: second_hop(j, k).wait_recv())
            guarded(owns(j, qm), lambda j=j: local(j).wait())

    return start, forward, finish


ARRIVAL = (0, 1, 2, 4, 3, 5, 6, 7)


def _front_project(xr, c, c_ctx, ada_w, ada_b, ng, w_in, w_out, cw, lam, me):
    nloc = ada_w.shape[1]
    ws = W_IN_SHARD
    arrival = jnp.asarray(ARRIVAL, jnp.int32)

    def body(me_ref, arr_ref, x_ref, c_ref, cc_ref, aw_ref, ab_ref, ng_ref, win_ref, wout_ref, cw_ref, lam_ref,
             z_ref, hn_ref, wfull_ref, woutb_ref, modx_ref, modc_ref, call_ref, cwf_ref, lamf_ref,
             wv, call_s, part_s, parts_s, w_send, w_recv, hbm_sems, s_send, s_recv, g_send, g_recv, g_local):
        t = pl.program_id(0)
        x, y, cidx = lax.axis_index("x"), lax.axis_index("y"), lax.axis_index("c")
        me_i = me_ref[0]
        sibling = (x, y, 1 - cidx)
        chips = [(x ^ (k >> 1), y ^ (k & 1)) for k in (1, 2, 3)]
        g_start, g_pass, g_finish = _gather2_ops([cw_ref, lam_ref], [cwf_ref, lamf_ref], ["agc", "agc"],
                                                 g_send, g_recv, g_local)

        def shard_copy(k, px, py, pc, to, half=None):
            slot = wv.at[4 * px + 2 * py + pc]
            if half is not None:
                slot = slot.at[pl.ds(half * (D // 2), D // 2), :]
            return pltpu.make_async_remote_copy(src_ref=slot, dst_ref=slot, send_sem=w_send.at[k],
                                                recv_sem=w_recv.at[k], device_id=to, device_id_type=MESH)

        def small_gather(src, my_slot, stage):
            copies = []
            for k in range(1, N_DEV):
                peer = (x ^ (k >> 2), y ^ ((k >> 1) & 1), cidx ^ (k & 1))
                cp = pltpu.make_async_remote_copy(src_ref=src, dst_ref=my_slot, send_sem=s_send.at[stage, k - 1],
                                                  recv_sem=s_recv.at[stage, k - 1], device_id=peer,
                                                  device_id_type=MESH)
                cp.start()
                copies.append(cp)
            pltpu.sync_copy(src, my_slot)
            return copies

        def finish_small(copies):
            for cp in copies:
                cp.wait()

        def to_neighbours(half):
            for i in (0, 1):
                shard_copy(1 + i, x, y, cidx, (*chips[i], cidx), half=half).start()

        @pl.when(t == 0)
        def _():
            _barrier([(x ^ (k >> 2), y ^ ((k >> 1) & 1), cidx ^ (k & 1)) for k in range(1, N_DEV)])
            g_start()
            wv[me_i] = win_ref[...].astype(BF16)
            woutb_ref[...] = wout_ref[...].astype(BF16)
            shard_copy(0, x, y, cidx, sibling).start()
            finish_small(small_gather(c_ref, call_s.at[pl.ds(me_i, 1), :], 0))
            to_neighbours(0)
            call_ref[...] = call_s[...]
            off = pl.multiple_of(me_i * nloc, 128)
            b = ab_ref[:, pl.ds(off, nloc)]
            w = aw_ref[...]
            sx, _ = _silu_and_grad(call_s[...])
            sc, _ = _silu_and_grad(jnp.broadcast_to(cc_ref[...], (8, D)))
            part_s[0:8, :] = _dot(sx, w) + b
            part_s[8:16, :] = _dot(sc, w) + b
            parts_sent = small_gather(part_s, parts_s.at[me_i], 1)
            to_neighbours(1)
            finish_small(parts_sent)
            mine = _rows((16, nloc)) == me_i
            for j in range(N_DEV):
                pj = parts_s[j]
                modx_ref[:, j * nloc:(j + 1) * nloc] = jnp.sum(jnp.where(mine, pj, 0.0), axis=0, keepdims=True)
                modc_ref[:, j * nloc:(j + 1) * nloc] = pj[8:9, :]
            shift, scale1, ngv = modx_ref[:, 0:D], 1.0 + modx_ref[:, D:2 * D], ng_ref[...]
            for r in range(L // ROWS):
                rsl = slice(r * ROWS, (r + 1) * ROWS)
                xv = x_ref[rsl, :]
                rs = lax.rsqrt(jnp.mean(xv * xv, axis=-1, keepdims=True) + NORM_EPS)
                hn_ref[rsl, :] = ((xv * rs * ngv) * scale1 + shift).astype(BF16)

        @pl.when(t == 1)
        def _():
            shard_copy(0, x, y, 1 - cidx, sibling).wait_recv()
            g_pass()

        for i in (0, 1):
            @pl.when(t == ARRIVAL.index((2, 4)[i]))
            def _(i=i):
                shard_copy(1 + i, *chips[i], cidx, sibling).wait_recv()
                shard_copy(4 + i, *chips[i], cidx, sibling).start()
                shard_copy((7, 3)[i], *chips[i], cidx, (*chips[1 - i], cidx), half=i).start()

        @pl.when(t == ARRIVAL.index(6))
        def _():
            shard_copy(3, *chips[2], cidx, sibling, half=1).wait_recv()
            shard_copy(7, *chips[2], cidx, sibling, half=0).wait_recv()
            shard_copy(6, *chips[2], cidx, sibling).start()

        for i in range(3):
            @pl.when(t == ARRIVAL.index((3, 5, 7)[i]))
            def _(i=i):
                shard_copy(4 + i, *chips[i], 1 - cidx, sibling).wait_recv()

        @pl.when(t == 2)
        def _():
            g_finish()

        dev = me_i ^ arr_ref[t]
        for r in range(L // (2 * ROWS)):
            rsl = slice(r * 2 * ROWS, (r + 1) * 2 * ROWS)
            z_ref[rsl, :] = _dot(hn_ref[rsl, :], wv[dev])
        col = pl.ds(pl.multiple_of(dev * ws, 128), ws)
        pltpu.make_async_copy(wv.at[dev], wfull_ref.at[:, col], hbm_sems.at[t]).start()

        @pl.when(t == N_DEV - 1)
        def _():
            for k in (0, 1, 2, 4, 5, 6):
                shard_copy(k, x, y, cidx, sibling).wait_send()
            for k in (3, 7):
                shard_copy(k, x, y, cidx, sibling, half=0).wait_send()
            for s in range(N_DEV):
                pltpu.make_async_copy(wv.at[0], wfull_ref.at[:, pl.ds(0, ws)], hbm_sems.at[s]).wait()

    const = lambda *shape: pl.BlockSpec(shape, lambda t, m, a: (0,) * len(shape))
    once = lambda *shape: pl.BlockSpec(shape, lambda t, m, a: (0,) * len(shape), pipeline_mode=pl.Buffered(1))
    return _call(
        body, name="front_project",
        out_shape=[jax.ShapeDtypeStruct((L, D_IN), F32), jax.ShapeDtypeStruct((L, D), BF16),
                   jax.ShapeDtypeStruct((D, D_IN), BF16), jax.ShapeDtypeStruct(w_out.shape, BF16),
                   jax.ShapeDtypeStruct((1, 3 * D), F32), jax.ShapeDtypeStruct((1, 3 * D), F32),
                   jax.ShapeDtypeStruct((N_DEV, D), F32), jax.ShapeDtypeStruct((CONV_W, D), F32),
                   jax.ShapeDtypeStruct((2, D), F32)],
        grid_spec=pltpu.PrefetchScalarGridSpec(
            num_scalar_prefetch=2, grid=(N_DEV,),
            in_specs=[once(L, D), const(1, D), const(1, D), once(D, nloc), const(1, 3 * D), const(1, D),
                      once(D, ws), once(*w_out.shape), HBM, HBM],
            out_specs=[pl.BlockSpec((L, ws), lambda t, m, a: (0, m[0] ^ a[t])), const(L, D), HBM,
                       const(*w_out.shape),
                       const(1, 3 * D), const(1, 3 * D), const(N_DEV, D), HBM, HBM],
            scratch_shapes=[pltpu.VMEM((N_DEV, D, ws), BF16), pltpu.VMEM((N_DEV, D), F32), pltpu.VMEM((16, nloc), F32),
                            pltpu.VMEM((N_DEV, 16, nloc), F32), pltpu.SemaphoreType.DMA((8,)),
                            pltpu.SemaphoreType.DMA((8,)), pltpu.SemaphoreType.DMA((N_DEV,)),
                            pltpu.SemaphoreType.DMA((2, N_DEV - 1)), pltpu.SemaphoreType.DMA((2, N_DEV - 1))]
            + _gather2_sems(2)),
        compiler_params=pltpu.CompilerParams(dimension_semantics=("arbitrary",), vmem_limit_bytes=VMEM_LIMIT,
                                             has_side_effects=True, collective_id=10),
    )(me, arrival, xr, c, c_ctx, ada_w, ada_b, ng, w_in, w_out, pltpu.with_memory_space_constraint(cw, pltpu.HBM),
      pltpu.with_memory_space_constraint(lam, pltpu.HBM))


def _project(xr, mod, ng, w, ncols, tm, name):
    rows = xr.shape[0]

    def body(x_ref, sh_ref, sc_ref, ng_ref, w_ref, z_ref, hn_ref):
        x = x_ref[...]
        rs = lax.rsqrt(jnp.mean(x * x, axis=-1, keepdims=True) + NORM_EPS)
        hn = (x * rs * ng_ref[...]) * (1.0 + sc_ref[...]) + sh_ref[...]
        hb = hn.astype(BF16)
        hn_ref[...] = hb
        for n in range(ncols // D):
            z_ref[:, n * D:(n + 1) * D] = _dot(hb, w_ref[:, n * D:(n + 1) * D])

    vec = pl.BlockSpec((1, D), lambda i: (0, 0))
    return _call(
        body, name=name, grid=(rows // tm,),
        out_shape=[jax.ShapeDtypeStruct((rows, ncols), F32), jax.ShapeDtypeStruct((rows, D), BF16)],
        in_specs=[pl.BlockSpec((tm, D), lambda i: (i, 0)), vec, pl.BlockSpec((1, D), lambda i: (0, 1)), vec,
                  pl.BlockSpec((D, ncols), lambda i: (0, 0), pipeline_mode=pl.Buffered(1))],
        out_specs=[pl.BlockSpec((tm, ncols), lambda i: (i, 0)), pl.BlockSpec((tm, D), lambda i: (i, 0))],
        compiler_params=_params("arbitrary"),
    )(xr, mod, mod, ng, w)


def _scan_pair(af_ref, uf_ref, hf_ref, h0f, ab_ref, ub_ref, hb_ref, h0b, t_len):
    span = 8 * SCAN_BLOCKS
    nit = t_len // span
    rows = _rows((8, HD))

    def local_scan(a, b, forward):
        for s in (1, 2, 4):
            sh = s if forward else 8 - s
            m = rows >= s if forward else rows < 8 - s
            b = a * jnp.where(m, pltpu.roll(b, sh, 0), 0.0) + b
            a = a * jnp.where(m, pltpu.roll(a, sh, 0), 1.0)
        return a, b

    def span_scan(a_ref, u_ref, h_ref, off, carry, forward):
        order = range(SCAN_BLOCKS) if forward else range(SCAN_BLOCKS - 1, -1, -1)
        last = slice(7, 8) if forward else slice(0, 1)
        for q in order:
            rs = pl.ds(off + 8 * q, 8)
            a, b = local_scan(a_ref[rs, :], u_ref[rs, :], forward)
            h_ref[rs, :] = b + a * carry
            carry = a[last, :] * carry + b[last, :]
        return carry

    def body(k, carry):
        cf, cb = carry
        cf = span_scan(af_ref, uf_ref, hf_ref, pl.multiple_of(k * span, span), cf, True)
        cb = span_scan(ab_ref, ub_ref, hb_ref, pl.multiple_of((nit - 1 - k) * span, span), cb, False)
        return cf, cb

    return lax.fori_loop(0, nit, body, (h0f, h0b))


SCAN_BLOCKS = 16


def _shifted(pad_ref, x, offsets, before=0.0, after=0.0):
    n = x.shape[0]
    pad_ref[0:8, :] = jnp.broadcast_to(jnp.asarray(before, F32), (8, x.shape[1]))
    pad_ref[8:8 + n, :] = x
    pad_ref[8 + n:16 + n, :] = jnp.broadcast_to(jnp.asarray(after, F32), (8, x.shape[1]))
    return [pad_ref[8 + o:8 + o + n, :] for o in offsets]


def _conv(xa, cw, cb, pad_ref):
    xm1, xp1, xp2 = _shifted(pad_ref, xa, (-1, 1, 2))
    return xm1 * cw[0:1, :] + xa * cw[1:2, :] + xp1 * cw[2:3, :] + xp2 * cw[3:4, :] + cb


def _gates(xc, wa, wx, ba, bx, nsp):
    xb = xc.astype(BF16)
    r = _sigmoid(_dot(xb, wa) + ba)
    i = _sigmoid(_dot(xb, wx) + bx)
    log_a = r * nsp
    a = jnp.exp(log_a)
    g2 = jnp.tanh(log_a) * (-1.0 - a * a)
    rg = lax.rsqrt(jnp.maximum(g2, 1e-30))
    return r, i, a, g2 * rg, rg


def _lru_param_specs():
    h4 = pl.BlockSpec((2, 1, HD, HD), lambda h: (0, h, 0, 0))
    v2 = pl.BlockSpec((2, HD), lambda h: (0, h))
    b16 = pl.BlockSpec((2 * HEADS, HD), lambda h: (0, 0))
    return dict(
        xa=pl.BlockSpec((L, HD), lambda h: (0, h)), xac=pl.BlockSpec((LC, HD), lambda h: (0, h)),
        cw=pl.BlockSpec((CONV_W, HD), lambda h: (0, h)), cb=pl.BlockSpec((1, HD), lambda h: (0, h)), h4=h4, v2=v2,
        b16=b16)


def _bias_row(ref, d):
    mask = _rows((2 * HEADS, HD)) == d * HEADS + pl.program_id(0)
    return jnp.sum(jnp.where(mask, ref[...], 0.0), axis=0, keepdims=True), mask


def _lru_forward(zx, zc, cw, cb, wa, wx, ba, bx, lam, gather, gather_modes):
    ng_ = len(gather)

    def body(xa_ref, xac_ref, cw_ref, cb_ref, wa_ref, wx_ref, ba_ref, bx_ref, lam_ref, *rest):
        yl_ref = rest[ng_]
        af, uf, hf, ab, ub, hb, pad_s = rest[2 * ng_ + 1:2 * ng_ + 8]
        start, pass_on, finish = _gather2_ops(rest[:ng_], rest[ng_ + 1:2 * ng_ + 1], gather_modes,
                                              *rest[2 * ng_ + 8:], barrier=True)
        pl.when(pl.program_id(0) == 0)(start)
        pl.when(pl.program_id(0) == HEADS // 2)(pass_on)
        pl.when(pl.program_id(0) == HEADS - 1)(finish)
        cwv, cbv = cw_ref[...], cb_ref[...]
        nsp = (-LRU_C) * _softplus(-lam_ref[...])

        def forward(xa, t_len, h0f, h0b):
            xc = _conv(xa, cwv, cbv, pad_s)
            for d, (a_ref, u_ref) in enumerate(((af, uf), (ab, ub))):
                _, i, a, gamma, _ = _gates(xc, wa_ref[d, 0].astype(BF16), wx_ref[d, 0].astype(BF16),
                                           _bias_row(ba_ref, d)[0], _bias_row(bx_ref, d)[0], nsp[d:d + 1, :])
                a_ref[0:t_len, :] = a
                u_ref[0:t_len, :] = gamma * (i * xc)
            return _scan_pair(af, uf, hf, h0f, ab, ub, hb, h0b, t_len)

        z = jnp.zeros((1, HD), F32)
        h0f, h0b = forward(xac_ref[...], LC, z, z)
        forward(xa_ref[...], L, h0f, h0b)
        yl_ref[...] = hf[...] + hb[...]

    s = _lru_param_specs()
    return _call(
        body, name="lru_forward", grid=(HEADS,),
        out_shape=[jax.ShapeDtypeStruct((L, D), F32)] + _gather2_shapes(gather, gather_modes),
        in_specs=[s["xa"], s["xac"], s["cw"], s["cb"], s["h4"], s["h4"], s["b16"], s["b16"], s["v2"]] + [HBM] * ng_,
        out_specs=[pl.BlockSpec((L, HD), lambda h: (0, h))] + [HBM] * ng_,
        scratch_shapes=[pltpu.VMEM((L, HD), F32)] * 6 + [pltpu.VMEM((L + 16, HD), F32)] + _gather2_sems(ng_),
        compiler_params=pltpu.CompilerParams(dimension_semantics=("arbitrary",), vmem_limit_bytes=VMEM_LIMIT,
                                             has_side_effects=True, collective_id=5),
    )(zx, zc, cw, cb, wa, wx, ba, bx, lam, *[pltpu.with_memory_space_constraint(a, pltpu.HBM) for a in gather])


def _lru_backward(zx, zc, dyl, dz, cw, cb, wa, wx, ba, bx, lam, chip_sums, first_chips=None):
    nr = len(chip_sums)

    def body(xa_ref, xac_ref, dyl_ref, dz_in, cw_ref, cb_ref, wa_ref, wx_ref, ba_ref, bx_ref, lam_ref, *rest):
        (dxa_ref, dxac_ref, dwa_ref, dwx_ref, dba_ref, dbx_ref, dlam_ref, dcw_ref,
         dcb_ref) = rest[nr:nr + 9]
        main_s, ctx_s, pad_s = rest[3 * nr + 9:3 * nr + 12]
        if nr:
            start, forward, finish = _chips_ops(rest[:nr], rest[nr + 9:2 * nr + 9], rest[2 * nr + 9:3 * nr + 9],
                                                *rest[3 * nr + 12:], first_chips=first_chips, barrier=True)
            pl.when(pl.program_id(0) == 0)(start)
            pl.when(pl.program_id(0) == HEADS // 2)(forward)
            pl.when(pl.program_id(0) == HEADS - 1)(finish)
        del dz_in

        @pl.when(pl.program_id(0) == 0)
        def _():
            dba_ref[...] = jnp.zeros_like(dba_ref)
            dbx_ref[...] = jnp.zeros_like(dbx_ref)

        cwv, cbv = cw_ref[...], cb_ref[...]
        lamv = lam_ref[...]
        sp = _softplus(-lamv)
        nsp = (-LRU_C) * sp
        z = jnp.zeros((1, HD), F32)

        def wmat(ref, d):
            return ref[d, 0].astype(BF16)

        def workspace(s):
            return dict(a=(s.at[0], s.at[1]), u=(s.at[2], s.at[3]), h=(s.at[4], s.at[5]), rho=(s.at[6], s.at[7]),
                        saved=(tuple(s.at[8 + k] for k in range(4)), tuple(s.at[12 + k] for k in range(4))),
                        xc=s.at[16])

        def forward(ws, xa, t_len, h0f, h0b):
            xc = _conv(xa, cwv, cbv, pad_s)
            ws["xc"][...] = xc
            for d in (0, 1):
                vals = _gates(xc, wmat(wa_ref, d), wmat(wx_ref, d), _bias_row(ba_ref, d)[0],
                              _bias_row(bx_ref, d)[0], nsp[d:d + 1, :])
                r, i, a, gamma, rg = vals
                ws["a"][d][...] = a
                ws["u"][d][...] = gamma * (i * xc)
                for ref, val in zip(ws["saved"][d], (r, i, gamma, rg)):
                    ref[...] = val
            return _scan_pair(ws["a"][0], ws["u"][0], ws["h"][0], h0f, ws["a"][1], ws["u"][1], ws["h"][1], h0b,
                              t_len)

        def backward(ws, xa, t_len, h0f, h0b, dhf, dhb, first):
            xc = ws["xc"][...]
            (af, ab), (uf, ub), (hf, hb), (rf, rb) = ws["a"], ws["u"], ws["h"], ws["rho"]
            uf[...] = ab[...] * dhb
            ub[...] = af[...] * dhf
            rho_b_last, rho_f_first = _scan_pair(ab, uf, rb, z, af, ub, rf, z, t_len)
            dxc = jnp.zeros((t_len, HD), F32)
            dsp = []
            for d in (0, 1):
                r, i, gamma, rg = (ref[...] for ref in ws["saved"][d])
                a = ws["a"][d][...]
                if d == 0:
                    lam_t = dhf + _shifted(pad_s, rf[...], (1,))[0]
                    h_prev = _shifted(pad_s, hf[...], (-1,), before=h0f)[0]
                else:
                    lam_t = dhb + _shifted(pad_s, rb[...], (-1,))[0]
                    h_prev = _shifted(pad_s, hb[...], (1,), after=h0b)[0]
                da = lam_t * h_prev
                lx = lam_t * xc
                d_i = lx * gamma
                d_gamma = lx * i
                dxc = dxc + lam_t * (gamma * i)
                d_log_a = a * (da - d_gamma * (a * rg))
                dsp.append(jnp.sum(d_log_a * r, axis=0, keepdims=True) * (-LRU_C))
                d_pre_r = d_log_a * nsp[d:d + 1, :] * (r * (1.0 - r))
                d_pre_i = d_i * (i * (1.0 - i))
                prb, pib, xb = d_pre_r.astype(BF16), d_pre_i.astype(BF16), xc.astype(BF16)
                dxc = dxc + _dot_nt(prb, wmat(wa_ref, d)) + _dot_nt(pib, wmat(wx_ref, d))
                g_wa, g_wx = _dot_tn(xb, prb), _dot_tn(xb, pib)
                g_ba = jnp.sum(d_pre_r, axis=0, keepdims=True)
                g_bx = jnp.sum(d_pre_i, axis=0, keepdims=True)
                mask = _bias_row(ba_ref, d)[1]
                dba_ref[...] += jnp.where(mask, g_ba, 0.0)
                dbx_ref[...] += jnp.where(mask, g_bx, 0.0)
                if first:
                    dwa_ref[d, 0] = g_wa
                    dwx_ref[d, 0] = g_wx
                else:
                    dwa_ref[d, 0] += g_wa
                    dwx_ref[d, 0] += g_wx
            g_lam = jnp.concatenate(dsp, axis=0) * (-_sigmoid(-lamv))
            dm1, dp1, dm2 = _shifted(pad_s, dxc, (-1, 1, -2))
            dxa = dp1 * cwv[0:1, :] + dxc * cwv[1:2, :] + dm1 * cwv[2:3, :] + dm2 * cwv[3:4, :]
            xm1, xp1, xp2 = _shifted(pad_s, xa, (-1, 1, 2))
            g_cw = jnp.concatenate([jnp.sum(dxc * v, axis=0, keepdims=True) for v in (xm1, xa, xp1, xp2)], axis=0)
            g_cb = jnp.sum(dxc, axis=0, keepdims=True)
            if first:
                dlam_ref[...] = g_lam
                dcw_ref[...] = g_cw
                dcb_ref[...] = g_cb
            else:
                dlam_ref[...] += g_lam
                dcw_ref[...] += g_cw
                dcb_ref[...] += g_cb
            return dxa, rho_f_first, rho_b_last

        ws_x, ws_c = workspace(main_s), workspace(ctx_s)
        h0f, h0b = forward(ws_c, xac_ref[...], LC, z, z)
        forward(ws_x, xa_ref[...], L, h0f, h0b)
        dh = dyl_ref[...]
        dxa, dh0f, dh0b = backward(ws_x, xa_ref[...], L, h0f, h0b, dh, dh, True)
        dxa_ref[...] = dxa.astype(BF16)
        rc = _rows((LC, HD))
        dxac, _, _ = backward(ws_c, xac_ref[...], LC, z, z, jnp.where(rc == LC - 1, dh0f, 0.0),
                              jnp.where(rc == 0, dh0b, 0.0), False)
        dxac_ref[...] = dxac.astype(BF16)

    s = _lru_param_specs()
    col = lambda r: pl.BlockSpec((r, HD), lambda h: (0, h))
    return _call(
        body, name="lru_backward", grid=(HEADS,),
        out_shape=[jax.ShapeDtypeStruct((L, D_IN), BF16), jax.ShapeDtypeStruct((LC, D), BF16),
                   jax.ShapeDtypeStruct((2, HEADS, HD, HD), F32), jax.ShapeDtypeStruct((2, HEADS, HD, HD), F32),
                   jax.ShapeDtypeStruct((2 * HEADS, HD), F32), jax.ShapeDtypeStruct((2 * HEADS, HD), F32),
                   jax.ShapeDtypeStruct((2, D), F32), jax.ShapeDtypeStruct((CONV_W, D), F32),
                   jax.ShapeDtypeStruct((1, D), F32)] + [jax.ShapeDtypeStruct((4,) + a.shape[1:], a.dtype)
                                                          for a in chip_sums] + _chips_stage_shapes(chip_sums),
        in_specs=[s["xa"], s["xac"], col(L), pl.BlockSpec(memory_space=pl.ANY), s["cw"], s["cb"], s["h4"], s["h4"],
                  s["b16"], s["b16"], s["v2"]] + [HBM] * nr,
        out_specs=[col(L), col(LC), s["h4"], s["h4"], s["b16"], s["b16"], s["v2"], col(CONV_W), col(1)]
        + [HBM] * (2 * nr),
        scratch_shapes=[pltpu.VMEM((17, L, HD), F32), pltpu.VMEM((17, LC, HD), F32), pltpu.VMEM((L + 16, HD), F32)]
        + (_chips_sems(nr) if nr else []),
        input_output_aliases={3: 0},
        compiler_params=pltpu.CompilerParams(dimension_semantics=("arbitrary",), vmem_limit_bytes=VMEM_LIMIT,
                                             has_side_effects=True, collective_id=6 if nr else None),
    )(zx, zc, dyl, dz, cw, cb, wa, wx, ba, bx, lam, *[pltpu.with_memory_space_constraint(a, pltpu.HBM)
                                                       for a in chip_sums])


def _mixer_loss(x, tgt, zx, yl, gx, fg, lng, lnb, ws, bst, wout, tm):
    ncht = tm // CHUNK

    def body(x_ref, t_ref, ga_ref, u_ref, v_ref, gb_ref, yl_ref, gx_ref, fg_ref, lng_ref, lnb_ref, ws_ref,
             bst_ref, wout_ref,
             dz_ref, dyl_ref, dxn_ref, y_s, do_ref, dws_ref, dbst_ref, vec_ref,
             vn_s, mix_s, dm_s, dvn_s):
        step = pl.program_id(0)

        @pl.when(step == 0)
        def _():
            dws_ref[...] = jnp.zeros_like(dws_ref)
            dbst_ref[...] = jnp.zeros_like(dbst_ref)
            vec_ref[...] = jnp.zeros_like(vec_ref)

        u, v = u_ref[...], v_ref[...]
        ug, dug_du = _gelu_and_grad(u)
        vg, dvg_dv = _gelu_and_grad(v)
        mu = jnp.mean(vg, axis=-1, keepdims=True)
        vc = vg - mu
        rstd = lax.rsqrt(jnp.mean(vc * vc, axis=-1, keepdims=True) + LN_EPS)
        vhat = vc * rstd
        lngv = lng_ref[...]
        vn_s[...] = (vhat * lngv + lnb_ref[...]).astype(BF16)
        for ch in range(ncht):
            rs = slice(ch * CHUNK, (ch + 1) * CHUNK)
            for g in range(HEADS):
                cs = slice(g * HD, (g + 1) * HD)
                mix_s[rs, cs] = _dot(ws_ref[g].astype(BF16), vn_s[rs, cs]) + bst_ref[:, g:g + 1]
        mixed = mix_s[...]
        ga, gb, yl = ga_ref[...], gb_ref[...], yl_ref[...]
        sga, dsga = _silu_and_grad(ga)
        sgb, dsgb = _silu_and_grad(gb)
        ys = ug * mixed
        y_s[:, 0:D] = (yl * sga).astype(BF16)
        y_s[:, D:D_MIX] = (ys * sgb).astype(BF16)
        o = _dot(y_s[...], wout_ref[...])
        gxv, fgv = gx_ref[...], fg_ref[...]
        xn = x_ref[...] + gxv * o
        rs2 = lax.rsqrt(jnp.mean(xn * xn, axis=-1, keepdims=True) + NORM_EPS)
        xh = xn * rs2
        diff = xh * fgv - t_ref[...]
        vec_ref[R_LOSS:R_LOSS + 1, :] += jnp.full((1, D), jnp.sum(diff * diff) * (0.5 / D), F32)
        dout = diff * (1.0 / D)
        w = dout * fgv
        dxn = rs2 * (w - xh * jnp.mean(w * xh, axis=-1, keepdims=True))
        dxn_ref[...] = dxn
        vec_ref[0:1, :] += jnp.sum(dxn * o, axis=0, keepdims=True)
        vec_ref[1:2, :] += jnp.sum(dout * xh, axis=0, keepdims=True)
        dob = (dxn * gxv).astype(BF16)
        do_ref[...] = dob
        dy = _dot_nt(dob, wout_ref[...])
        dya, dyb = dy[:, 0:D], dy[:, D:D_MIX]
        dyl_ref[...] = dya * sga
        dys = dyb * sgb
        dz_ref[:, 0:D] = jnp.zeros((tm, D), BF16)
        dz_ref[:, D:2 * D] = (dya * yl * dsga).astype(BF16)
        dz_ref[:, 2 * D:3 * D] = (dys * mixed * dug_du).astype(BF16)
        dz_ref[:, 4 * D:5 * D] = (dyb * ys * dsgb).astype(BF16)
        dm = dys * ug
        dm_s[...] = dm.astype(BF16)
        for g in range(HEADS):
            cs = slice(g * HD, (g + 1) * HD)
            dbst_ref[:, g:g + 1] += sum(jnp.sum(dm[ch * CHUNK:(ch + 1) * CHUNK, cs], axis=1, keepdims=True)
                                        for ch in range(ncht))
            for ch in range(ncht):
                rs = slice(ch * CHUNK, (ch + 1) * CHUNK)
                dws_ref[g] += _dot_nt(dm_s[rs, cs], vn_s[rs, cs])
                dvn_s[rs, cs] = _dot_tn(ws_ref[g].astype(BF16), dm_s[rs, cs])
        dvn = dvn_s[...]
        vec_ref[2:3, :] += jnp.sum(dvn * vhat, axis=0, keepdims=True)
        vec_ref[3:4, :] += jnp.sum(dvn, axis=0, keepdims=True)
        dvh = dvn * lngv
        dvg = rstd * (dvh - jnp.mean(dvh, axis=-1, keepdims=True) - vhat * jnp.mean(dvh * vhat, axis=-1, keepdims=True))
        dz_ref[:, 3 * D:4 * D] = (dvg * dvg_dv).astype(BF16)

    tile = pl.BlockSpec((tm, D), lambda i: (i, 0))
    zcol = lambda n: pl.BlockSpec((tm, D), lambda i: (i, n))
    vec = pl.BlockSpec((1, D), lambda i: (0, 0))
    full = lambda *s: pl.BlockSpec(s, lambda i: (0,) * len(s))
    return _call(
        body, name="mixer_loss", grid=(L // tm,),
        out_shape=[jax.ShapeDtypeStruct((L, D_IN), BF16), jax.ShapeDtypeStruct((L, D), F32),
                   jax.ShapeDtypeStruct((L, D), F32), jax.ShapeDtypeStruct((L, D_MIX), BF16),
                   jax.ShapeDtypeStruct((L, D), BF16),
                   jax.ShapeDtypeStruct((HEADS, CHUNK, CHUNK), F32), jax.ShapeDtypeStruct((CHUNK, HEADS), F32),
                   jax.ShapeDtypeStruct((8, D), F32)],
        in_specs=[tile, tile, zcol(1), zcol(2), zcol(3), zcol(4), tile, pl.BlockSpec((1, D), lambda i: (0, 2)),
                  vec, vec, vec,
                  full(HEADS, CHUNK, CHUNK), full(CHUNK, HEADS),
                  pl.BlockSpec((D_MIX, D), lambda i: (0, 0), pipeline_mode=pl.Buffered(1))],
        out_specs=[pl.BlockSpec((tm, D_IN), lambda i: (i, 0)), tile, tile,
                   pl.BlockSpec((tm, D_MIX), lambda i: (i, 0)), tile,
                   full(HEADS, CHUNK, CHUNK), full(CHUNK, HEADS), full(8, D)],
        scratch_shapes=[pltpu.VMEM((tm, D), BF16), pltpu.VMEM((tm, D), F32),
                        pltpu.VMEM((tm, D), BF16), pltpu.VMEM((tm, D), F32)],
        compiler_params=_params("arbitrary"),
    )(x, tgt, zx, zx, zx, zx, yl, gx, fg, lng, lnb, ws, bst, wout)


def _grad_w(a, b, a2, b2, tk, name, bw, first, nblocks, split, barrier_id, chip_sums=()):
    nk = a.shape[0] // tk
    m = a.shape[1]
    with_ctx = a2 is not None
    if split == "cols":
        slots, r, w = nblocks, m, bw // 2
        piece = lambda q, pc: (slice(None), slice(pc * w, (pc + 1) * w))
    else:
        slots, r, w = 4, m // 8, bw
        piece = lambda q, pc: (slice((2 * q + pc) * r, (2 * q + pc + 1) * r), slice(None))

    nr = len(chip_sums)

    def body(*refs):
        a_ref, b_ref = refs[:2]
        a2_ref, b2_ref = refs[2:4] if with_ctx else (None, None)
        base = 4 if with_ctx else 2
        sums_ref = refs[base + nr]
        acc, mine_v, send_v, stage_v, send_sems, recv_sems = refs[base + 3 * nr + 1:base + 3 * nr + 7]
        n, k = pl.program_id(0), pl.program_id(1)
        x, y, c = lax.axis_index("x"), lax.axis_index("y"), lax.axis_index("c")

        def to_sibling(s):
            return pltpu.make_async_remote_copy(src_ref=send_v.at[s], dst_ref=stage_v.at[s], send_sem=send_sems.at[s],
                                                recv_sem=recv_sems.at[s], device_id=(x, y, 1 - c),
                                                device_id_type=MESH)

        if nr:
            c_start, c_forward, c_finish = _chips_ops(refs[base:base + nr], refs[base + nr + 1:base + 2 * nr + 1],
                                                      refs[base + 2 * nr + 1:base + 3 * nr + 1],
                                                      *refs[base + 3 * nr + 7:])

            @pl.when(jnp.logical_and(n == 0, k == 0))
            def _():
                _barrier([(x, y, 1 - c)] + [(x ^ (j >> 1), y ^ (j & 1), c) for j in (1, 2)])
                c_start()
        else:
            pl.when(jnp.logical_and(n == 0, k == 0))(_sibling_barrier)

        @pl.when(k == 0)
        def _():
            acc[...] = _dot_tn(a_ref[...], b_ref[...])

        if nk > 1:
            @pl.when(k > 0)
            def _():
                acc[...] += _dot_tn(a_ref[...], b_ref[...])

        if with_ctx:
            @pl.when(jnp.logical_and(k == nk - 1, n == 0))
            def _():
                acc[:, 0:b2_ref.shape[1]] += _dot_tn(a2_ref[...], b2_ref[...])

        if nr:
            pl.when(jnp.logical_and(k == nk - 1, n == nblocks - 1))(c_forward)

        def hand_over(s, q):
            for pc in (0, 1):
                @pl.when(c == pc)
                def _(pc=pc):
                    mine_v[s] = acc[piece(q, pc)]
                    send_v[s] = acc[piece(q, 1 - pc)].astype(BF16)
            to_sibling(s).start()

        for i in range(nblocks):
            @pl.when(jnp.logical_and(k == nk - 1, n == i))
            def _(i=i):
                if split == "cols":
                    hand_over(i, 0)
                else:
                    for q in range(4):
                        hand_over(q, q)

        @pl.when(jnp.logical_and(k == nk - 1, n == nblocks - 1))
        def _():
            for s in range(slots):
                to_sibling(s).wait_recv()
                sums_ref[s] = (mine_v[s] + stage_v[s].astype(F32)).astype(BF16)
            for s in range(slots):
                to_sibling(s).wait_send()
            if nr:
                c_finish()

    in_specs = [pl.BlockSpec((tk, m), lambda n, k: (k, 0)), pl.BlockSpec((tk, bw), lambda n, k: (k, n + first))]
    args = [a, b]
    if with_ctx:
        in_specs += [pl.BlockSpec(a2.shape, lambda n, k: (0, 0)), pl.BlockSpec(b2.shape, lambda n, k: (0, 0))]
        args += [a2, b2]
    in_specs += [HBM] * nr
    args += [pltpu.with_memory_space_constraint(s, pltpu.HBM) for s in chip_sums]
    return _call(
        body, name=name, grid=(nblocks, nk),
        out_shape=[jax.ShapeDtypeStruct((slots, r, w), BF16)]
        + [jax.ShapeDtypeStruct((4,) + s.shape[1:], s.dtype) for s in chip_sums] + _chips_stage_shapes(chip_sums),
        in_specs=in_specs, out_specs=[pl.BlockSpec((slots, r, w), lambda n, k: (0, 0, 0))] + [HBM] * (2 * nr),
        scratch_shapes=[pltpu.VMEM((m, bw), F32), pltpu.VMEM((slots, r, w), F32), pltpu.VMEM((slots, r, w), BF16),
                        pltpu.VMEM((slots, r, w), BF16), pltpu.SemaphoreType.DMA((slots,)),
                        pltpu.SemaphoreType.DMA((slots,))] + (_chips_sems(nr) if nr else []),
        compiler_params=pltpu.CompilerParams(dimension_semantics=("arbitrary", "arbitrary"),
                                             vmem_limit_bytes=VMEM_LIMIT, has_side_effects=True,
                                             collective_id=barrier_id),
    )(*args)


def _grad_rows(xr, dz, w, mod, ng, dres, ncols, tm, name, chip_sums=(), first_chips=None, dests=None):
    rows = xr.shape[0]
    steps = rows // tm
    with_dx = dres is not None
    nr = len(chip_sums)
    dests = [d for d in (dests or [None] * nr)]
    nd = sum(d is not None for d in dests)
    nin = 6 if with_dx else 5
    nout = 2 if with_dx else 1

    def body(*refs):
        if with_dx:
            x_ref, dz_ref, w_ref, sc_ref, ng_ref, dres_ref = refs[:nin]
            dx_ref, vec_ref = refs[nin + nr + nd:nin + nr + nd + nout]
        else:
            x_ref, dz_ref, w_ref, sc_ref, ng_ref = refs[:nin]
            (vec_ref,) = refs[nin + nr + nd:nin + nr + nd + nout]
        if nr:
            o0 = nin + nr + nd + nout
            start, forward, finish = _chips_ops(refs[nin:nin + nr], refs[o0:o0 + nr], refs[o0 + nr:o0 + 2 * nr],
                                                *refs[o0 + 2 * nr:], first_chips=first_chips, barrier=True)
            pl.when(pl.program_id(0) == 0)(start)
            pl.when(pl.program_id(0) == steps // 2)(forward)
            pl.when(pl.program_id(0) == steps - 1)(finish)

        @pl.when(pl.program_id(0) == 0)
        def _():
            vec_ref[...] = jnp.zeros_like(vec_ref)

        dhn = _dot_nt(dz_ref[...], w_ref[...])
        x = x_ref[...]
        rs = lax.rsqrt(jnp.mean(x * x, axis=-1, keepdims=True) + NORM_EPS)
        xh = x * rs
        ngv = ng_ref[...]
        y = xh * ngv
        vec_ref[0:1, :] += jnp.sum(dhn, axis=0, keepdims=True)
        vec_ref[1:2, :] += jnp.sum(dhn * y, axis=0, keepdims=True)
        dy = dhn * (1.0 + sc_ref[...])
        vec_ref[2:3, :] += jnp.sum(dy * xh, axis=0, keepdims=True)
        if with_dx:
            dxh = dy * ngv
            dx_ref[...] = dres_ref[...] + rs * (dxh - xh * jnp.mean(dxh * xh, axis=-1, keepdims=True))

    tile = pl.BlockSpec((tm, D), lambda i: (i, 0))
    vec = pl.BlockSpec((1, D), lambda i: (0, 0))
    in_specs = [tile, pl.BlockSpec((tm, ncols), lambda i: (i, 0)),
                pl.BlockSpec((D, ncols), lambda i: (0, 0), pipeline_mode=pl.Buffered(1)),
                pl.BlockSpec((1, D), lambda i: (0, 1)), vec]
    out_shape = [jax.ShapeDtypeStruct((8, D), F32)]
    out_specs = [pl.BlockSpec((8, D), lambda i: (0, 0))]
    args = [xr, dz, w, mod, ng]
    if with_dx:
        in_specs.append(tile)
        out_shape.insert(0, jax.ShapeDtypeStruct((rows, D), F32))
        out_specs.insert(0, tile)
        args.append(dres)
    aliases = {}
    for j, d in enumerate(dests):
        if d is not None:
            aliases[len(args) + nr + len(aliases)] = len(out_shape) + j
    in_specs += [HBM] * (nr + nd)
    out_specs += [HBM] * (2 * nr)
    out_shape += [jax.ShapeDtypeStruct((4,) + a.shape[1:], a.dtype) for a in chip_sums]
    out_shape += _chips_stage_shapes(chip_sums)
    args += [pltpu.with_memory_space_constraint(a, pltpu.HBM) for a in chip_sums]
    args += [pltpu.with_memory_space_constraint(d, pltpu.HBM) for d in dests if d is not None]
    return _call(body, name=name, grid=(steps,), out_shape=out_shape, in_specs=in_specs, out_specs=out_specs,
                 scratch_shapes=_chips_sems(nr) if nr else [], input_output_aliases=aliases,
                 compiler_params=pltpu.CompilerParams(dimension_semantics=("arbitrary",),
                                                      vmem_limit_bytes=VMEM_LIMIT, has_side_effects=bool(nr),
                                                      collective_id=7 if nr else None))(*args)


def _adamw(w, g, m, v):
    m = ADAM_B1 * m + (1.0 - ADAM_B1) * g
    v = ADAM_B2 * v + (1.0 - ADAM_B2) * (g * g)
    m_hat = m / (1.0 - ADAM_B1 ** ADAM_STEP)
    v_hat = v / (1.0 - ADAM_B2 ** ADAM_STEP)
    delta = -ADAM_LR * (m_hat / (jnp.sqrt(v_hat) + ADAM_EPS) + ADAM_WD * w)
    return delta, m, v


def _adamw_reduced(parts, w, m, v, tr, name):
    r, n = w.shape
    nparts = parts.shape[0]

    def body(p_ref, w_ref, m_ref, v_ref, g_ref, d_ref, mo_ref, vo_ref):
        g = p_ref[0].astype(F32)
        for i in range(1, nparts):
            g = g + p_ref[i].astype(F32)
        g_ref[...] = g
        d_ref[...], mo_ref[...], vo_ref[...] = _adamw(w_ref[...], g, m_ref[...], v_ref[...])

    tile = pl.BlockSpec((tr, n), lambda i: (i, 0))
    sds = jax.ShapeDtypeStruct((r, n), F32)
    return _call(
        body, name=name, grid=(r // tr,), out_shape=[sds] * 4,
        in_specs=[pl.BlockSpec((nparts, tr, n), lambda i: (0, i, 0)), tile, tile, tile], out_specs=[tile] * 4,
        compiler_params=_params("arbitrary"),
    )(parts, w, m, v)


R_GATE, R_FINAL_G, R_LN_G, R_LN_B, R_LOSS = 0, 1, 2, 3, 4
R_SH_X, R_SC_X, R_NG_X = 5, 6, 7
R_SH_C, R_SC_C, R_NG_C = 8, 9, 10
R_LAM, R_CW, R_CB = 11, 13, 17
PACK_ROWS = 24
Q_BA, Q_BX, Q_SGU_B, PACK128_ROWS = 0, 16, 32, 40


def _reduce_small(vec_pieces, q_pieces, mat_parts, ada_w, me):
    nloc = ada_w.shape[1]
    nm = len(mat_parts)
    pieces = list(vec_pieces) + list(q_pieces)

    def body(me_ref, *refs):
        piece_refs, refs = refs[:len(pieces)], refs[len(pieces):]
        mp_refs, w_ref = refs[:nm], refs[nm]
        red_ref, redq_ref = refs[nm + 1:nm + 3]
        mats_all = refs[nm + 3:2 * nm + 3]
        cparts_ref, dmod_ref, gab_ref = refs[2 * nm + 3:2 * nm + 6]
        pack_ref, packq_ref, vp_ref, vq_ref = refs[2 * nm + 6:2 * nm + 10]
        mat_refs = refs[2 * nm + 10:3 * nm + 10]
        cpart_ref, dmc_s = refs[3 * nm + 10:3 * nm + 12]
        sems = refs[3 * nm + 12:]
        for dst, group in ((pack_ref, vec_pieces), (packq_ref, q_pieces)):
            row = 0
            for _, nrows in group:
                dst[row:row + nrows, :] = piece_refs[0][0:nrows, :]
                piece_refs, row = piece_refs[1:], row + nrows
            if row < dst.shape[0]:
                dst[row:, :] = jnp.zeros((dst.shape[0] - row, dst.shape[1]), F32)
        p_start, p_forward, p_finish = _gather2_ops([pack_ref, packq_ref], [vp_ref, vq_ref], ["ag", "ag"], *sems[:3],
                                                    barrier=True)
        m_start, m_forward, m_finish = _gather2_ops(mat_refs, mats_all, ["ag"] * nm, *sems[3:6])
        c_start, c_forward, c_finish = _gather2_ops([cpart_ref], [cparts_ref], ["ag"], *sems[6:])
        p_start()
        for mp_ref, mat_ref in zip(mp_refs, mat_refs):
            mat = mp_ref[0].astype(F32)
            for i in range(1, mp_ref.shape[0]):
                mat = mat + mp_ref[i].astype(F32)
            mat_ref[...] = mat
        m_start()
        p_forward()
        p_finish()
        m_forward()
        red, redq = vp_ref[0], vq_ref[0]
        for i in range(1, N_DEV):
            red = red + vp_ref[i]
            redq = redq + vq_ref[i]
        red_ref[...] = red
        redq_ref[...] = redq
        for e in range(N_DEV):
            dmod_ref[e:e + 1, 0:D] = vp_ref[e, R_SH_X:R_SH_X + 1, :]
            dmod_ref[e:e + 1, D:2 * D] = vp_ref[e, R_SC_X:R_SC_X + 1, :]
            dmod_ref[e:e + 1, 2 * D:3 * D] = vp_ref[e, R_GATE:R_GATE + 1, :]
        dmod_ref[8:9, 0:D] = red[R_SH_C:R_SH_C + 1, :]
        dmod_ref[8:9, D:2 * D] = red[R_SC_C:R_SC_C + 1, :]
        dmod_ref[8:9, 2 * D:3 * D] = jnp.zeros((1, D), F32)
        dmod_ref[9:16, :] = jnp.zeros((7, 3 * D), F32)
        gab_ref[:, 0:D] = red[R_SH_X:R_SH_X + 1, :] + red[R_SH_C:R_SH_C + 1, :]
        gab_ref[:, D:2 * D] = red[R_SC_X:R_SC_X + 1, :] + red[R_SC_C:R_SC_C + 1, :]
        gab_ref[:, 2 * D:3 * D] = red[R_GATE:R_GATE + 1, :]
        dmc_s[...] = jnp.broadcast_to(dmod_ref[8:9, :], (8, 3 * D))
        off = pl.multiple_of(me_ref[0] * nloc, 128)
        cpart_ref[...] = _dot_nt(dmc_s[:, pl.ds(off, nloc)], w_ref[...])
        c_start()
        c_forward()
        c_finish()
        m_finish()

    return _call(
        body, name="reduce_small",
        out_shape=[jax.ShapeDtypeStruct((PACK_ROWS, D), F32), jax.ShapeDtypeStruct((PACK128_ROWS, HD), F32)]
        + [jax.ShapeDtypeStruct((N_DEV,) + p.shape[1:], F32) for p in mat_parts]
        + [jax.ShapeDtypeStruct((N_DEV, 8, D), F32), jax.ShapeDtypeStruct((16, 3 * D), F32),
           jax.ShapeDtypeStruct((1, 3 * D), F32)],
        in_specs=[pl.BlockSpec(memory_space=pltpu.SMEM)] + [VMEM] * (len(pieces) + nm + 1),
        out_specs=[VMEM] * (nm + 5),
        scratch_shapes=[pltpu.VMEM((PACK_ROWS, D), F32), pltpu.VMEM((PACK128_ROWS, HD), F32),
                        pltpu.VMEM((N_DEV, PACK_ROWS, D), F32), pltpu.VMEM((N_DEV, PACK128_ROWS, HD), F32)]
        + [pltpu.VMEM(p.shape[1:], F32) for p in mat_parts]
        + [pltpu.VMEM((8, D), F32), pltpu.VMEM((8, 3 * D), F32)] + _gather2_sems(2) + _gather2_sems(nm)
        + _gather2_sems(1),
        compiler_params=pltpu.CompilerParams(vmem_limit_bytes=VMEM_LIMIT, has_side_effects=True, collective_id=8),
    )(me, *[a for a, _ in pieces], *mat_parts, ada_w)


def _adamw_ada(c_all, c_ctx, dmod, w, m, v, me):
    nloc = w.shape[1]

    def body(me_ref, c_ref, cc_ref, dm_ref, w_ref, m_ref, v_ref, g_ref, d_ref, mo_ref, vo_ref):
        off = pl.multiple_of(me_ref[0] * nloc, 128)
        dm = dm_ref[:, pl.ds(off, nloc)]
        sx, _ = _silu_and_grad(c_ref[...])
        sc, _ = _silu_and_grad(cc_ref[...])
        g = _dot_tn(sx, dm[0:8, :]) + _dot_tn(jnp.broadcast_to(sc, (8, D)), dm[8:16, :])
        g_ref[...] = g
        d_ref[...], mo_ref[...], vo_ref[...] = _adamw(w_ref[...], g, m_ref[...], v_ref[...])

    sds = jax.ShapeDtypeStruct(w.shape, F32)
    return _call(
        body, name="adamw_ada_w", out_shape=[sds] * 4,
        in_specs=[pl.BlockSpec(memory_space=pltpu.SMEM)] + [VMEM] * 6, out_specs=[VMEM] * 4,
        compiler_params=_params(),
    )(me, c_all, c_ctx, dmod, w, m, v)


_SMALL = ("c_ctx", "ada_b", "norm_g", "conv_w", "conv_b", "lru_wa", "lru_ba", "lru_wx", "lru_bx", "lru_lambda",
          "sgu_ln_g", "sgu_ln_b", "sgu_w", "sgu_b", "final_g")


def _adamw_small(red, redq, mats, cparts, gab, ws, ms, vs, me):
    n = len(_SMALL)

    def body(me_ref, red_ref, redq_ref, wa_ref, wx_ref, sw_ref, cp_ref, gab_ref, *refs):
        w_refs, m_refs, v_refs = refs[:n], refs[n:2 * n], refs[2 * n:3 * n]
        outs = refs[3 * n:]
        off = pl.multiple_of(me_ref[0] * HD, 128)

        def row(r, k=1):
            return red_ref[r:r + k, :]

        cc = w_refs[0][...]
        dcc = cp_ref[0, 0:1, :]
        for i in range(1, N_DEV):
            dcc = dcc + cp_ref[i, 0:1, :]
        grads = dict(
            c_ctx=dcc * _silu_and_grad(cc)[1], ada_b=gab_ref[...], norm_g=row(R_NG_X) + row(R_NG_C),
            conv_w=red_ref[R_CW:R_CW + CONV_W, pl.ds(off, HD)], conv_b=row(R_CB),
            lru_wa=wa_ref[...], lru_ba=redq_ref[Q_BA:Q_BA + 2 * HEADS, :], lru_wx=wx_ref[...],
            lru_bx=redq_ref[Q_BX:Q_BX + 2 * HEADS, :], lru_lambda=red_ref[R_LAM:R_LAM + 2, pl.ds(off, HD)],
            sgu_ln_g=row(R_LN_G), sgu_ln_b=row(R_LN_B), sgu_w=sw_ref[...],
            sgu_b=redq_ref[Q_SGU_B:Q_SGU_B + HEADS, :], final_g=row(R_FINAL_G))
        for j, name in enumerate(_SMALL):
            g = grads[name]
            outs[j][...] = g
            outs[n + j][...], outs[2 * n + j][...], outs[3 * n + j][...] = _adamw(w_refs[j][...], g, m_refs[j][...],
                                                                                 v_refs[j][...])

    sds = [jax.ShapeDtypeStruct(ws[k].shape, F32) for k in _SMALL]
    outs = _call(
        body, name="adamw_small", out_shape=sds * 4,
        in_specs=[pl.BlockSpec(memory_space=pltpu.SMEM)] + [VMEM] * (7 + 3 * n), out_specs=[VMEM] * (4 * n),
        compiler_params=_params(),
    )(me, red, redq, *mats, cparts, gab, *[ws[k] for k in _SMALL], *[ms[k] for k in _SMALL],
      *[vs[k] for k in _SMALL])
    return [dict(zip(_SMALL, outs[i * n:(i + 1) * n])) for i in range(4)]


def kernel(x, c, ctx, c_ctx, ada_w, ada_b, norm_g, w_in, conv_w, conv_b, lru_wa, lru_ba, lru_wx, lru_bx, lru_lambda, sgu_ln_g, sgu_ln_b, sgu_w, sgu_b, w_out, final_g, loss_target, m_c_ctx, m_ada_w, m_ada_b, m_norm_g, m_w_in, m_conv_w, m_conv_b, m_lru_wa, m_lru_ba, m_lru_wx, m_lru_bx, m_lru_lambda, m_sgu_ln_g, m_sgu_ln_b, m_sgu_w, m_sgu_b, m_w_out, m_final_g, v_c_ctx, v_ada_w, v_ada_b, v_norm_g, v_w_in, v_conv_w, v_conv_b, v_lru_wa, v_lru_ba, v_lru_wx, v_lru_bx, v_lru_lambda, v_sgu_ln_g, v_sgu_ln_b, v_sgu_w, v_sgu_b, v_w_out, v_final_g):
    args = dict(locals())
    me = (4 * lax.axis_index("x") + 2 * lax.axis_index("y") + lax.axis_index("c")).astype(jnp.int32).reshape(1)
    xr, ctxr, tgt = x[0], ctx[0], loss_target[0]
    cc = c_ctx.reshape(1, D)
    nw = 2 * HEADS * HD
    view = dict(c_ctx=(1, D), ada_b=(1, 3 * D), norm_g=(1, D), conv_w=(CONV_W, HD), conv_b=(1, D), lru_wa=(nw, HD),
                lru_ba=(2 * HEADS, HD), lru_wx=(nw, HD), lru_bx=(2 * HEADS, HD), lru_lambda=(2, HD), sgu_ln_g=(1, D),
                sgu_ln_b=(1, D), sgu_w=(HEADS * CHUNK, CHUNK), sgu_b=(HEADS, CHUNK), final_g=(1, D))

    zx, hn, w_full, w_out_b, modx, modc, c_all, cw_full, lam_full = _front_project(
        xr, c, cc, ada_w[0], ada_b, norm_g, w_in[0], w_out[0], conv_w[0], lru_lambda[0], me)
    zc, hnc = _project(ctxr, modc, norm_g, w_full, D, LC, "project_ctx")
    ba, bx = lru_ba.reshape(view["lru_ba"]), lru_bx.reshape(view["lru_bx"])
    yl, wout_all = _lru_forward(zx, zc, cw_full, conv_b, lru_wa[0], lru_wx[0], ba, bx, lam_full, [w_out_b], ["ag"])
    wout_full = wout_all.reshape(D_MIX, D)
    dz, dyl, dxn, ycat, dob, dws, dbst, mvec = _mixer_loss(
        xr, tgt, zx, yl, modx, final_g.reshape(1, D), sgu_ln_g, sgu_ln_b, sgu_w[0], sgu_b[0].T, wout_full, ROWS)

    (wout_sums,) = _grad_w(ycat, dob, None, None, L, "grad_w_out", D, 0, 1, "rows", 1)
    (rest_sums,) = _grad_w(hn, dz, None, None, L, "grad_w_in_rest", 2 * W_IN_SHARD, 1, 3, "cols", 2)
    dz, dxac, dwa, dwx, dba, dbx, dlam, dcw, dcb, win_parts, wout_parts, _, _ = _lru_backward(
        zx, zc, dyl, dz, cw_full, conv_b, lru_wa[0], lru_wx[0], ba, bx, lam_full, [rest_sums, wout_sums],
        first_chips=[1, 0])
    mats = [dwa.reshape(N_DEV, nw // N_DEV, HD), dwx.reshape(N_DEV, nw // N_DEV, HD), dws]
    mat_sums = _reduce2_local(mats, me, "reduce_mat", 4, BF16)
    first_sums, *mat_parts = _grad_w(hn, dz, hnc, dxac, L, "grad_w_in_first", 2 * W_IN_SHARD, 0, 1, "cols", 3,
                                     chip_sums=mat_sums)[:4]
    gx, xvec, win_parts = _grad_rows(
        xr, dz, w_full, modx, norm_g, dxn, D_IN, ROWS, "grad_rows_x", chip_sums=[first_sums], first_chips=[0],
        dests=[win_parts])[:3]
    (cvec,) = _grad_rows(ctxr, dxac, w_full, modc, norm_g, None, D, LC, "grad_rows_ctx")
    red, redq, *rest = _reduce_small(
        [(mvec, 5), (xvec, 3), (cvec, 3), (dlam, 2), (dcw, CONV_W), (dcb, 1)],
        [(dba, 2 * HEADS), (dbx, 2 * HEADS), (dbst.T, HEADS)], mat_parts, ada_w[0], me)
    mats_all, (cparts, dmod, gab) = rest[:3], rest[3:]

    g_w_in, d_w_in, nm_w_in, nv_w_in = _adamw_reduced(win_parts, w_in[0], m_w_in[0], v_w_in[0], ROWS, "adamw_w_in")
    g_w_out, d_w_out, nm_w_out, nv_w_out = _adamw_reduced(wout_parts, w_out[0], m_w_out[0], v_w_out[0], ROWS // 2,
                                                          "adamw_w_out")
    g_ada, d_ada, nm_ada, nv_ada = _adamw_ada(c_all, cc, dmod, ada_w[0], m_ada_w[0], v_ada_w[0], me)
    ws = {k: args[k].reshape(view[k]) for k in _SMALL}
    ms = {k: args["m_" + k].reshape(view[k]) for k in _SMALL}
    vs = {k: args["v_" + k].reshape(view[k]) for k in _SMALL}
    small = _adamw_small(red, redq, [m.reshape(-1, HD) for m in mats_all], cparts, gab, ws, ms, vs, me)
    big = dict(w_in=(g_w_in, d_w_in, nm_w_in, nv_w_in), w_out=(g_w_out, d_w_out, nm_w_out, nv_w_out),
               ada_w=(g_ada, d_ada, nm_ada, nv_ada))

    loss = red[R_LOSS, 0]
    names = ("c_ctx", "ada_w", "ada_b", "norm_g", "w_in", "conv_w", "conv_b", "lru_wa", "lru_ba", "lru_wx", "lru_bx",
             "lru_lambda", "sgu_ln_g", "sgu_ln_b", "sgu_w", "sgu_b", "w_out", "final_g")
    outs = [loss, gx.reshape(x.shape)]
    for kind in range(4):
        for k in names:
            val = big[k][kind] if k in big else small[kind][k]
            outs.append(val.reshape(args[k].shape))
    return tuple(outs)
```

```python
import jax
import jax.numpy as jnp
from jax import lax
from jax.experimental import pallas as pl
from jax.experimental.pallas import tpu as pltpu

F32 = jnp.float32
BF16 = jnp.bfloat16

N_DEV = 8
D = 1024
L = 2048
LC = 256
HEADS = 8
HD = 128
CHUNK = 128
D_IN = 5 * D
W_IN_SHARD = D_IN // N_DEV
ROWS = 256
D_MIX = 2 * D
CONV_W = 4
LRU_C = 8.0
NORM_EPS = 1e-6
LN_EPS = 1e-5
ADAM_LR, ADAM_B1, ADAM_B2, ADAM_EPS, ADAM_WD, ADAM_STEP = 0.001, 0.9, 0.999, 1e-08, 0.01, 10

VMEM_LIMIT = 56 * 1024 * 1024

HBM = pl.BlockSpec(memory_space=pltpu.HBM)
VMEM = pl.BlockSpec(memory_space=pltpu.VMEM)
MESH = pl.DeviceIdType.MESH


def _call(body, **kw):
    return pl.pallas_call(body, **kw)


def _params(*sem):
    return pltpu.CompilerParams(dimension_semantics=sem, vmem_limit_bytes=VMEM_LIMIT)


def _sigmoid(x):
    return 0.5 * jnp.tanh(0.5 * x) + 0.5


def _silu_and_grad(x):
    s = _sigmoid(x)
    return x * s, s * (1.0 + x * (1.0 - s))


_G0 = 0.7978845608028654
_G1 = 0.044715


def _gelu_and_grad(x):
    x2 = x * x
    t = jnp.tanh(_G0 * (x + _G1 * x * x2))
    cdf = 0.5 * (1.0 + t)
    return x * cdf, cdf + 0.5 * x * (1.0 - t * t) * (_G0 * (1.0 + 3.0 * _G1 * x2))


def _softplus(z):
    t = jnp.exp(-jnp.abs(z))
    u = 1.0 + t
    log1p = jnp.where(u == 1.0, t, jnp.log(u) * t / jnp.where(u == 1.0, 1.0, u - 1.0))
    return jnp.maximum(z, 0.0) + log1p


def _dot(a, b):
    return jnp.dot(a, b, preferred_element_type=F32)


def _dot_nt(a, b):
    return lax.dot_general(a, b, (((1,), (1,)), ((), ())), preferred_element_type=F32)


def _dot_tn(a, b):
    return lax.dot_general(a, b, (((0,), (0,)), ((), ())), preferred_element_type=F32)


def _rows(shape):
    return lax.broadcasted_iota(jnp.int32, shape, 0)


def _gather2_shapes(arrays, modes):
    return [jax.ShapeDtypeStruct((N_DEV,) + a.shape if m == "ag" else (a.shape[0], N_DEV * a.shape[1]), a.dtype)
            for a, m in zip(arrays, modes)]


def _gather2_sems(n):
    return [pltpu.SemaphoreType.DMA((n, N_DEV - 1)), pltpu.SemaphoreType.DMA((n, N_DEV - 1)),
            pltpu.SemaphoreType.DMA((n,))]


def _barrier(peers):
    sem = pltpu.get_barrier_semaphore()
    for peer in peers:
        pl.semaphore_signal(sem, inc=1, device_id=peer, device_id_type=MESH)
    pl.semaphore_wait(sem, len(peers))


def _gather2_ops(ins, outs, modes, send_sems, recv_sems, local_sems, barrier=False):
    n = len(ins)
    x, y, c = lax.axis_index("x"), lax.axis_index("y"), lax.axis_index("c")
    me, sibling = (x, y, c), (x, y, 1 - c)
    chips = [(x ^ (k >> 1), y ^ (k & 1)) for k in (1, 2, 3)]

    def slot(j, px, py, pc):
        dev = 4 * px + 2 * py + pc
        if modes[j] == "agc":
            w = ins[j].shape[1]
            return outs[j].at[:, pl.ds(pl.multiple_of(dev * w, 128), w)]
        return outs[j].at[dev]

    def copy(j, k, block, to, src=None):
        return pltpu.make_async_remote_copy(
            src_ref=slot(j, *block) if src is None else src, dst_ref=slot(j, *block),
            send_sem=send_sems.at[j, k], recv_sem=recv_sems.at[j, k], device_id=to, device_id_type=MESH)

    def own(j):
        return pltpu.make_async_copy(ins[j], slot(j, *me), local_sems.at[j])

    def first(j):
        return [copy(j, 0, me, sibling, src=ins[j])] + [copy(j, 1 + i, me, (*chip, c), src=ins[j])
                                                        for i, chip in enumerate(chips)]

    def passed(j, i):
        return copy(j, 4 + i, (*chips[i], c), sibling)

    def start():
        if barrier:
            _barrier([sibling] + [(*chip, c) for chip in chips])
        for j in range(n):
            own(j).start()
            for cp in first(j):
                cp.start()

    def forward():
        for i, chip in enumerate(chips):
            for j in range(n):
                copy(j, 1 + i, (*chip, c), me).wait_recv()
                passed(j, i).start()

    def finish():
        for j in range(n):
            copy(j, 0, sibling, me).wait_recv()
            for i, chip in enumerate(chips):
                copy(j, 4 + i, (*chip, 1 - c), me).wait_recv()
            for cp in first(j) + [passed(j, i) for i in range(3)]:
                cp.wait_send()
            own(j).wait()

    return start, forward, finish


def _sibling_barrier():
    sem = pltpu.get_barrier_semaphore()
    sibling = (lax.axis_index("x"), lax.axis_index("y"), 1 - lax.axis_index("c"))
    pl.semaphore_signal(sem, inc=1, device_id=sibling, device_id_type=MESH)
    pl.semaphore_wait(sem, 1)


def _reduce2_local(arrays, me, name, barrier_id, out_dtype):
    n = len(arrays)
    staged = [jax.ShapeDtypeStruct((4,) + a.shape[1:], a.dtype) for a in arrays]

    def to_sibling(*refs):
        ins, outs = refs[:n], refs[n:2 * n]
        send_sems, recv_sems = refs[2 * n:]
        x, y, c = lax.axis_index("x"), lax.axis_index("y"), lax.axis_index("c")
        _sibling_barrier()
        copies = []
        for j in range(n):
            for q in range(4):
                cp = pltpu.make_async_remote_copy(
                    src_ref=ins[j].at[2 * q + (1 - c)], dst_ref=outs[j].at[q], send_sem=send_sems.at[j, q],
                    recv_sem=recv_sems.at[j, q], device_id=(x, y, 1 - c), device_id_type=MESH)
                cp.start()
                copies.append(cp)
        for cp in copies:
            cp.wait()

    stage = _call(
        to_sibling, name=name + "_d2d", out_shape=staged, in_specs=[HBM] * n, out_specs=[HBM] * n,
        scratch_shapes=[pltpu.SemaphoreType.DMA((n, 4)), pltpu.SemaphoreType.DMA((n, 4))],
        compiler_params=pltpu.CompilerParams(has_side_effects=True, collective_id=barrier_id),
    )(*[pltpu.with_memory_space_constraint(a, pltpu.HBM) for a in arrays])

    def add(me_ref, *refs):
        del me_ref
        own, got, outs = refs[:n], refs[n:2 * n], refs[2 * n:]
        for j in range(n):
            outs[j][0] = (own[j][0].astype(F32) + got[j][0].astype(F32)).astype(out_dtype)

    own_specs = [pl.BlockSpec((1,) + a.shape[1:], lambda q, me_ref: (2 * q + me_ref[0] % 2, 0, 0)) for a in arrays]
    slot_specs = [pl.BlockSpec((1,) + a.shape[1:], lambda q, me_ref: (q, 0, 0)) for a in arrays]
    return _call(
        add, name=name + "_add", out_shape=[jax.ShapeDtypeStruct(s.shape, out_dtype) for s in staged],
        grid_spec=pltpu.PrefetchScalarGridSpec(num_scalar_prefetch=1, grid=(4,), in_specs=own_specs + slot_specs,
                                               out_specs=slot_specs),
        compiler_params=_params("arbitrary"),
    )(me, *arrays, *stage)


def _chips_sems(n):
    return [pltpu.SemaphoreType.DMA((n, 6)), pltpu.SemaphoreType.DMA((n, 6)), pltpu.SemaphoreType.DMA((n,))]


def _chips_stage_shapes(chip_sums):
    return [jax.ShapeDtypeStruct((2, a.shape[1] // 2, a.shape[2]), a.dtype) for a in chip_sums]


def _chips_ops(ins, outs, stages, send_sems, recv_sems, local_sems, first_chips=None, barrier=False):
    x, y, c = lax.axis_index("x"), lax.axis_index("y"), lax.axis_index("c")
    qm = 2 * x + y
    first_chips = first_chips or [0] * len(ins)

    def owns(j, chip):
        lo, cnt = first_chips[j], ins[j].shape[0]
        if lo == 0 and cnt == 4:
            return None
        return jnp.logical_and(chip >= lo, chip < lo + cnt)

    def guarded(cond, fn):
        if cond is None:
            fn()
        else:
            pl.when(cond)(fn)

    def slot(j, chip):
        return jnp.clip(chip - first_chips[j], 0, ins[j].shape[0] - 1)

    def half(j, i):
        h = ins[j].shape[1] // 2
        return pl.ds(i * h, h)

    def copy(j, sem, src, dst, k):
        return pltpu.make_async_remote_copy(
            src_ref=src, dst_ref=dst, send_sem=send_sems.at[j, sem], recv_sem=recv_sems.at[j, sem],
            device_id=(x ^ (k >> 1), y ^ (k & 1), c), device_id_type=MESH)

    def direct(j, k):
        return copy(j, k - 1, ins[j].at[slot(j, qm ^ k)], outs[j].at[qm], k)

    def first_hop(j, k):
        return copy(j, 1 + k, ins[j].at[slot(j, qm ^ 3), half(j, k - 1)], stages[j].at[k - 1], k)

    def second_hop(j, k):
        return copy(j, 3 + k, stages[j].at[2 - k], outs[j].at[qm ^ (3 - k), half(j, 2 - k)], k)

    def local(j):
        return pltpu.make_async_copy(ins[j].at[slot(j, qm)], outs[j].at[qm], local_sems.at[j])

    def start():
        if barrier:
            _barrier([(x ^ (k >> 1), y ^ (k & 1), c) for k in (1, 2)])
        for j in range(len(ins)):
            for k in (1, 2):
                guarded(owns(j, qm ^ 3), lambda j=j, k=k: first_hop(j, k).start())
        for j in range(len(ins)):
            for k in (1, 2):
                guarded(owns(j, qm ^ k), lambda j=j, k=k: direct(j, k).start())
            guarded(owns(j, qm), lambda j=j: local(j).start())

    def forward():
        for j in range(len(ins)):
            for k in (1, 2):
                def pass_on(j=j, k=k):
                    first_hop(j, 3 - k).wait_recv()
                    second_hop(j, k).start()
                guarded(owns(j, qm ^ k), pass_on)

    def finish():
        for j in range(len(ins)):
            for k in (1, 2):
                guarded(owns(j, qm ^ k), lambda j=j, k=k: direct(j, k).wait_send())
                guarded(owns(j, qm ^ k), lambda j=j, k=k: second_hop(j, k).wait_send())
                guarded(owns(j, qm ^ 3), lambda j=j, k=k: first_hop(j, k).wait_send())
                guarded(owns(j, qm), lambda j=j, k=k: direct(j, k).wait_recv())
                guarded(owns(j, qm), lambda j=j, k=k: second_hop(j, k).wait_recv())
            guarded(owns(j, qm), lambda j=j: local(j).wait())

    return start, forward, finish


ARRIVAL = (0, 1, 2, 4, 3, 5, 6, 7)


def _front_project(xr, c, c_ctx, ada_w, ada_b, ng, w_in, w_out, cw, lam, me):
    nloc = ada_w.shape[1]
    ws = W_IN_SHARD
    arrival = jnp.asarray(ARRIVAL, jnp.int32)

    def body(me_ref, arr_ref, x_ref, c_ref, cc_ref, aw_ref, ab_ref, ng_ref, win_ref, wout_ref, cw_ref, lam_ref,
             z_ref, hn_ref, wfull_ref, woutb_ref, modx_ref, modc_ref, call_ref, cwf_ref, lamf_ref,
             wv, call_s, part_s, parts_s, w_send, w_recv, hbm_sems, s_send, s_recv, g_send, g_recv, g_local):
        t = pl.program_id(0)
        x, y, cidx = lax.axis_index("x"), lax.axis_index("y"), lax.axis_index("c")
        me_i = me_ref[0]
        sibling = (x, y, 1 - cidx)
        chips = [(x ^ (k >> 1), y ^ (k & 1)) for k in (1, 2, 3)]
        g_start, g_pass, g_finish = _gather2_ops([cw_ref, lam_ref], [cwf_ref, lamf_ref], ["agc", "agc"],
                                                 g_send, g_recv, g_local)

        def shard_copy(k, px, py, pc, to, half=None):
            slot = wv.at[4 * px + 2 * py + pc]
            if half is not None:
                slot = slot.at[pl.ds(half * (D // 2), D // 2), :]
            return pltpu.make_async_remote_copy(src_ref=slot, dst_ref=slot, send_sem=w_send.at[k],
                                                recv_sem=w_recv.at[k], device_id=to, device_id_type=MESH)

        def small_gather(src, my_slot, stage):
            copies = []
            for k in range(1, N_DEV):
                peer = (x ^ (k >> 2), y ^ ((k >> 1) & 1), cidx ^ (k & 1))
                cp = pltpu.make_async_remote_copy(src_ref=src, dst_ref=my_slot, send_sem=s_send.at[stage, k - 1],
                                                  recv_sem=s_recv.at[stage, k - 1], device_id=peer,
                                                  device_id_type=MESH)
                cp.start()
                copies.append(cp)
            pltpu.sync_copy(src, my_slot)
            return copies

        def finish_small(copies):
            for cp in copies:
                cp.wait()

        def to_neighbours(half):
            for i in (0, 1):
                shard_copy(1 + i, x, y, cidx, (*chips[i], cidx), half=half).start()

        @pl.when(t == 0)
        def _():
            _barrier([(x ^ (k >> 2), y ^ ((k >> 1) & 1), cidx ^ (k & 1)) for k in range(1, N_DEV)])
            g_start()
            wv[me_i] = win_ref[...].astype(BF16)
            woutb_ref[...] = wout_ref[...].astype(BF16)
            shard_copy(0, x, y, cidx, sibling).start()
            finish_small(small_gather(c_ref, call_s.at[pl.ds(me_i, 1), :], 0))
            to_neighbours(0)
            call_ref[...] = call_s[...]
            off = pl.multiple_of(me_i * nloc, 128)
            b = ab_ref[:, pl.ds(off, nloc)]
            w = aw_ref[...]
            sx, _ = _silu_and_grad(call_s[...])
            sc, _ = _silu_and_grad(jnp.broadcast_to(cc_ref[...], (8, D)))
            part_s[0:8, :] = _dot(sx, w) + b
            part_s[8:16, :] = _dot(sc, w) + b
            parts_sent = small_gather(part_s, parts_s.at[me_i], 1)
            to_neighbours(1)
            finish_small(parts_sent)
            mine = _rows((16, nloc)) == me_i
            for j in range(N_DEV):
                pj = parts_s[j]
                modx_ref[:, j * nloc:(j + 1) * nloc] = jnp.sum(jnp.where(mine, pj, 0.0), axis=0, keepdims=True)
                modc_ref[:, j * nloc:(j + 1) * nloc] = pj[8:9, :]
            shift, scale1, ngv = modx_ref[:, 0:D], 1.0 + modx_ref[:, D:2 * D], ng_ref[...]
            for r in range(L // ROWS):
                rsl = slice(r * ROWS, (r + 1) * ROWS)
                xv = x_ref[rsl, :]
                rs = lax.rsqrt(jnp.mean(xv * xv, axis=-1, keepdims=True) + NORM_EPS)
                hn_ref[rsl, :] = ((xv * rs * ngv) * scale1 + shift).astype(BF16)

        @pl.when(t == 1)
        def _():
            shard_copy(0, x, y, 1 - cidx, sibling).wait_recv()
            g_pass()

        for i in (0, 1):
            @pl.when(t == ARRIVAL.index((2, 4)[i]))
            def _(i=i):
                shard_copy(1 + i, *chips[i], cidx, sibling).wait_recv()
                shard_copy(4 + i, *chips[i], cidx, sibling).start()
                shard_copy((7, 3)[i], *chips[i], cidx, (*chips[1 - i], cidx), half=i).start()

        @pl.when(t == ARRIVAL.index(6))
        def _():
            shard_copy(3, *chips[2], cidx, sibling, half=1).wait_recv()
            shard_copy(7, *chips[2], cidx, sibling, half=0).wait_recv()
            shard_copy(6, *chips[2], cidx, sibling).start()

        for i in range(3):
            @pl.when(t == ARRIVAL.index((3, 5, 7)[i]))
            def _(i=i):
                shard_copy(4 + i, *chips[i], 1 - cidx, sibling).wait_recv()

        @pl.when(t == 2)
        def _():
            g_finish()

        dev = me_i ^ arr_ref[t]
        for r in range(L // (2 * ROWS)):
            rsl = slice(r * 2 * ROWS, (r + 1) * 2 * ROWS)
            z_ref[rsl, :] = _dot(hn_ref[rsl, :], wv[dev])
        col = pl.ds(pl.multiple_of(dev * ws, 128), ws)
        pltpu.make_async_copy(wv.at[dev], wfull_ref.at[:, col], hbm_sems.at[t]).start()

        @pl.when(t == N_DEV - 1)
        def _():
            for k in (0, 1, 2, 4, 5, 6):
                shard_copy(k, x, y, cidx, sibling).wait_send()
            for k in (3, 7):
                shard_copy(k, x, y, cidx, sibling, half=0).wait_send()
            for s in range(N_DEV):
                pltpu.make_async_copy(wv.at[0], wfull_ref.at[:, pl.ds(0, ws)], hbm_sems.at[s]).wait()

    const = lambda *shape: pl.BlockSpec(shape, lambda t, m, a: (0,) * len(shape))
    once = lambda *shape: pl.BlockSpec(shape, lambda t, m, a: (0,) * len(shape), pipeline_mode=pl.Buffered(1))
    return _call(
        body, name="front_project",
        out_shape=[jax.ShapeDtypeStruct((L, D_IN), F32), jax.ShapeDtypeStruct((L, D), BF16),
                   jax.ShapeDtypeStruct((D, D_IN), BF16), jax.ShapeDtypeStruct(w_out.shape, BF16),
                   jax.ShapeDtypeStruct((1, 3 * D), F32), jax.ShapeDtypeStruct((1, 3 * D), F32),
                   jax.ShapeDtypeStruct((N_DEV, D), F32), jax.ShapeDtypeStruct((CONV_W, D), F32),
                   jax.ShapeDtypeStruct((2, D), F32)],
        grid_spec=pltpu.PrefetchScalarGridSpec(
            num_scalar_prefetch=2, grid=(N_DEV,),
            in_specs=[once(L, D), const(1, D), const(1, D), once(D, nloc), const(1, 3 * D), const(1, D),
                      once(D, ws), once(*w_out.shape), HBM, HBM],
            out_specs=[pl.BlockSpec((L, ws), lambda t, m, a: (0, m[0] ^ a[t])), const(L, D), HBM,
                       const(*w_out.shape),
                       const(1, 3 * D), const(1, 3 * D), const(N_DEV, D), HBM, HBM],
            scratch_shapes=[pltpu.VMEM((N_DEV, D, ws), BF16), pltpu.VMEM((N_DEV, D), F32), pltpu.VMEM((16, nloc), F32),
                            pltpu.VMEM((N_DEV, 16, nloc), F32), pltpu.SemaphoreType.DMA((8,)),
                            pltpu.SemaphoreType.DMA((8,)), pltpu.SemaphoreType.DMA((N_DEV,)),
                            pltpu.SemaphoreType.DMA((2, N_DEV - 1)), pltpu.SemaphoreType.DMA((2, N_DEV - 1))]
            + _gather2_sems(2)),
        compiler_params=pltpu.CompilerParams(dimension_semantics=("arbitrary",), vmem_limit_bytes=VMEM_LIMIT,
                                             has_side_effects=True, collective_id=10),
    )(me, arrival, xr, c, c_ctx, ada_w, ada_b, ng, w_in, w_out, pltpu.with_memory_space_constraint(cw, pltpu.HBM),
      pltpu.with_memory_space_constraint(lam, pltpu.HBM))


def _project(xr, mod, ng, w, ncols, tm, name):
    rows = xr.shape[0]

    def body(x_ref, sh_ref, sc_ref, ng_ref, w_ref, z_ref, hn_ref):
        x = x_ref[...]
        rs = lax.rsqrt(jnp.mean(x * x, axis=-1, keepdims=True) + NORM_EPS)
        hn = (x * rs * ng_ref[...]) * (1.0 + sc_ref[...]) + sh_ref[...]
        hb = hn.astype(BF16)
        hn_ref[...] = hb
        for n in range(ncols // D):
            z_ref[:, n * D:(n + 1) * D] = _dot(hb, w_ref[:, n * D:(n + 1) * D])

    vec = pl.BlockSpec((1, D), lambda i: (0, 0))
    return _call(
        body, name=name, grid=(rows // tm,),
        out_shape=[jax.ShapeDtypeStruct((rows, ncols), F32), jax.ShapeDtypeStruct((rows, D), BF16)],
        in_specs=[pl.BlockSpec((tm, D), lambda i: (i, 0)), vec, pl.BlockSpec((1, D), lambda i: (0, 1)), vec,
                  pl.BlockSpec((D, ncols), lambda i: (0, 0), pipeline_mode=pl.Buffered(1))],
        out_specs=[pl.BlockSpec((tm, ncols), lambda i: (i, 0)), pl.BlockSpec((tm, D), lambda i: (i, 0))],
        compiler_params=_params("arbitrary"),
    )(xr, mod, mod, ng, w)


def _scan_pair(af_ref, uf_ref, hf_ref, h0f, ab_ref, ub_ref, hb_ref, h0b, t_len):
    span = 8 * SCAN_BLOCKS
    nit = t_len // span
    rows = _rows((8, HD))

    def local_scan(a, b, forward):
        for s in (1, 2, 4):
            sh = s if forward else 8 - s
            m = rows >= s if forward else rows < 8 - s
            b = a * jnp.where(m, pltpu.roll(b, sh, 0), 0.0) + b
            a = a * jnp.where(m, pltpu.roll(a, sh, 0), 1.0)
        return a, b

    def span_scan(a_ref, u_ref, h_ref, off, carry, forward):
        order = range(SCAN_BLOCKS) if forward else range(SCAN_BLOCKS - 1, -1, -1)
        last = slice(7, 8) if forward else slice(0, 1)
        for q in order:
            rs = pl.ds(off + 8 * q, 8)
            a, b = local_scan(a_ref[rs, :], u_ref[rs, :], forward)
            h_ref[rs, :] = b + a * carry
            carry = a[last, :] * carry + b[last, :]
        return carry

    def body(k, carry):
        cf, cb = carry
        cf = span_scan(af_ref, uf_ref, hf_ref, pl.multiple_of(k * span, span), cf, True)
        cb = span_scan(ab_ref, ub_ref, hb_ref, pl.multiple_of((nit - 1 - k) * span, span), cb, False)
        return cf, cb

    return lax.fori_loop(0, nit, body, (h0f, h0b))


SCAN_BLOCKS = 16


def _shifted(pad_ref, x, offsets, before=0.0, after=0.0):
    n = x.shape[0]
    pad_ref[0:8, :] = jnp.broadcast_to(jnp.asarray(before, F32), (8, x.shape[1]))
    pad_ref[8:8 + n, :] = x
    pad_ref[8 + n:16 + n, :] = jnp.broadcast_to(jnp.asarray(after, F32), (8, x.shape[1]))
    return [pad_ref[8 + o:8 + o + n, :] for o in offsets]


def _conv(xa, cw, cb, pad_ref):
    xm1, xp1, xp2 = _shifted(pad_ref, xa, (-1, 1, 2))
    return xm1 * cw[0:1, :] + xa * cw[1:2, :] + xp1 * cw[2:3, :] + xp2 * cw[3:4, :] + cb


def _gates(xc, wa, wx, ba, bx, nsp):
    xb = xc.astype(BF16)
    r = _sigmoid(_dot(xb, wa) + ba)
    i = _sigmoid(_dot(xb, wx) + bx)
    log_a = r * nsp
    a = jnp.exp(log_a)
    g2 = jnp.tanh(log_a) * (-1.0 - a * a)
    rg = lax.rsqrt(jnp.maximum(g2, 1e-30))
    return r, i, a, g2 * rg, rg


def _lru_param_specs():
    h4 = pl.BlockSpec((2, 1, HD, HD), lambda h: (0, h, 0, 0))
    v2 = pl.BlockSpec((2, HD), lambda h: (0, h))
    b16 = pl.BlockSpec((2 * HEADS, HD), lambda h: (0, 0))
    return dict(
        xa=pl.BlockSpec((L, HD), lambda h: (0, h)), xac=pl.BlockSpec((LC, HD), lambda h: (0, h)),
        cw=pl.BlockSpec((CONV_W, HD), lambda h: (0, h)), cb=pl.BlockSpec((1, HD), lambda h: (0, h)), h4=h4, v2=v2,
        b16=b16)


def _bias_row(ref, d):
    mask = _rows((2 * HEADS, HD)) == d * HEADS + pl.program_id(0)
    return jnp.sum(jnp.where(mask, ref[...], 0.0), axis=0, keepdims=True), mask


def _lru_forward(zx, zc, cw, cb, wa, wx, ba, bx, lam, gather, gather_modes):
    ng_ = len(gather)

    def body(xa_ref, xac_ref, cw_ref, cb_ref, wa_ref, wx_ref, ba_ref, bx_ref, lam_ref, *rest):
        yl_ref = rest[ng_]
        af, uf, hf, ab, ub, hb, pad_s = rest[2 * ng_ + 1:2 * ng_ + 8]
        start, pass_on, finish = _gather2_ops(rest[:ng_], rest[ng_ + 1:2 * ng_ + 1], gather_modes,
                                              *rest[2 * ng_ + 8:], barrier=True)
        pl.when(pl.program_id(0) == 0)(start)
        pl.when(pl.program_id(0) == HEADS // 2)(pass_on)
        pl.when(pl.program_id(0) == HEADS - 1)(finish)
        cwv, cbv = cw_ref[...], cb_ref[...]
        nsp = (-LRU_C) * _softplus(-lam_ref[...])

        def forward(xa, t_len, h0f, h0b):
            xc = _conv(xa, cwv, cbv, pad_s)
            for d, (a_ref, u_ref) in enumerate(((af, uf), (ab, ub))):
                _, i, a, gamma, _ = _gates(xc, wa_ref[d, 0].astype(BF16), wx_ref[d, 0].astype(BF16),
                                           _bias_row(ba_ref, d)[0], _bias_row(bx_ref, d)[0], nsp[d:d + 1, :])
                a_ref[0:t_len, :] = a
                u_ref[0:t_len, :] = gamma * (i * xc)
            return _scan_pair(af, uf, hf, h0f, ab, ub, hb, h0b, t_len)

        z = jnp.zeros((1, HD), F32)
        h0f, h0b = forward(xac_ref[...], LC, z, z)
        forward(xa_ref[...], L, h0f, h0b)
        yl_ref[...] = hf[...] + hb[...]

    s = _lru_param_specs()
    return _call(
        body, name="lru_forward", grid=(HEADS,),
        out_shape=[jax.ShapeDtypeStruct((L, D), F32)] + _gather2_shapes(gather, gather_modes),
        in_specs=[s["xa"], s["xac"], s["cw"], s["cb"], s["h4"], s["h4"], s["b16"], s["b16"], s["v2"]] + [HBM] * ng_,
        out_specs=[pl.BlockSpec((L, HD), lambda h: (0, h))] + [HBM] * ng_,
        scratch_shapes=[pltpu.VMEM((L, HD), F32)] * 6 + [pltpu.VMEM((L + 16, HD), F32)] + _gather2_sems(ng_),
        compiler_params=pltpu.CompilerParams(dimension_semantics=("arbitrary",), vmem_limit_bytes=VMEM_LIMIT,
                                             has_side_effects=True, collective_id=5),
    )(zx, zc, cw, cb, wa, wx, ba, bx, lam, *[pltpu.with_memory_space_constraint(a, pltpu.HBM) for a in gather])


def _lru_backward(zx, zc, dyl, dz, cw, cb, wa, wx, ba, bx, lam, chip_sums, first_chips=None):
    nr = len(chip_sums)

    def body(xa_ref, xac_ref, dyl_ref, dz_in, cw_ref, cb_ref, wa_ref, wx_ref, ba_ref, bx_ref, lam_ref, *rest):
        (dxa_ref, dxac_ref, dwa_ref, dwx_ref, dba_ref, dbx_ref, dlam_ref, dcw_ref,
         dcb_ref) = rest[nr:nr + 9]
        main_s, ctx_s, pad_s = rest[3 * nr + 9:3 * nr + 12]
        if nr:
            start, forward, finish = _chips_ops(rest[:nr], rest[nr + 9:2 * nr + 9], rest[2 * nr + 9:3 * nr + 9],
                                                *rest[3 * nr + 12:], first_chips=first_chips, barrier=True)
            pl.when(pl.program_id(0) == 0)(start)
            pl.when(pl.program_id(0) == HEADS // 2)(forward)
            pl.when(pl.program_id(0) == HEADS - 1)(finish)
        del dz_in

        @pl.when(pl.program_id(0) == 0)
        def _():
            dba_ref[...] = jnp.zeros_like(dba_ref)
            dbx_ref[...] = jnp.zeros_like(dbx_ref)

        cwv, cbv = cw_ref[...], cb_ref[...]
        lamv = lam_ref[...]
        sp = _softplus(-lamv)
        nsp = (-LRU_C) * sp
        z = jnp.zeros((1, HD), F32)

        def wmat(ref, d):
            return ref[d, 0].astype(BF16)

        def workspace(s):
            return dict(a=(s.at[0], s.at[1]), u=(s.at[2], s.at[3]), h=(s.at[4], s.at[5]), rho=(s.at[6], s.at[7]),
                        saved=(tuple(s.at[8 + k] for k in range(4)), tuple(s.at[12 + k] for k in range(4))),
                        xc=s.at[16])

        def forward(ws, xa, t_len, h0f, h0b):
            xc = _conv(xa, cwv, cbv, pad_s)
            ws["xc"][...] = xc
            for d in (0, 1):
                vals = _gates(xc, wmat(wa_ref, d), wmat(wx_ref, d), _bias_row(ba_ref, d)[0],
                              _bias_row(bx_ref, d)[0], nsp[d:d + 1, :])
                r, i, a, gamma, rg = vals
                ws["a"][d][...] = a
                ws["u"][d][...] = gamma * (i * xc)
                for ref, val in zip(ws["saved"][d], (r, i, gamma, rg)):
                    ref[...] = val
            return _scan_pair(ws["a"][0], ws["u"][0], ws["h"][0], h0f, ws["a"][1], ws["u"][1], ws["h"][1], h0b,
                              t_len)

        def backward(ws, xa, t_len, h0f, h0b, dhf, dhb, first):
            xc = ws["xc"][...]
            (af, ab), (uf, ub), (hf, hb), (rf, rb) = ws["a"], ws["u"], ws["h"], ws["rho"]
            uf[...] = ab[...] * dhb
            ub[...] = af[...] * dhf
            rho_b_last, rho_f_first = _scan_pair(ab, uf, rb, z, af, ub, rf, z, t_len)
            dxc = jnp.zeros((t_len, HD), F32)
            dsp = []
            for d in (0, 1):
                r, i, gamma, rg = (ref[...] for ref in ws["saved"][d])
                a = ws["a"][d][...]
                if d == 0:
                    lam_t = dhf + _shifted(pad_s, rf[...], (1,))[0]
                    h_prev = _shifted(pad_s, hf[...], (-1,), before=h0f)[0]
                else:
                    lam_t = dhb + _shifted(pad_s, rb[...], (-1,))[0]
                    h_prev = _shifted(pad_s, hb[...], (1,), after=h0b)[0]
                da = lam_t * h_prev
                lx = lam_t * xc
                d_i = lx * gamma
                d_gamma = lx * i
                dxc = dxc + lam_t * (gamma * i)
                d_log_a = a * (da - d_gamma * (a * rg))
                dsp.append(jnp.sum(d_log_a * r, axis=0, keepdims=True) * (-LRU_C))
                d_pre_r = d_log_a * nsp[d:d + 1, :] * (r * (1.0 - r))
                d_pre_i = d_i * (i * (1.0 - i))
                prb, pib, xb = d_pre_r.astype(BF16), d_pre_i.astype(BF16), xc.astype(BF16)
                dxc = dxc + _dot_nt(prb, wmat(wa_ref, d)) + _dot_nt(pib, wmat(wx_ref, d))
                g_wa, g_wx = _dot_tn(xb, prb), _dot_tn(xb, pib)
                g_ba = jnp.sum(d_pre_r, axis=0, keepdims=True)
                g_bx = jnp.sum(d_pre_i, axis=0, keepdims=True)
                mask = _bias_row(ba_ref, d)[1]
                dba_ref[...] += jnp.where(mask, g_ba, 0.0)
                dbx_ref[...] += jnp.where(mask, g_bx, 0.0)
                if first:
                    dwa_ref[d, 0] = g_wa
                    dwx_ref[d, 0] = g_wx
                else:
                    dwa_ref[d, 0] += g_wa
                    dwx_ref[d, 0] += g_wx
            g_lam = jnp.concatenate(dsp, axis=0) * (-_sigmoid(-lamv))
            dm1, dp1, dm2 = _shifted(pad_s, dxc, (-1, 1, -2))
            dxa = dp1 * cwv[0:1, :] + dxc * cwv[1:2, :] + dm1 * cwv[2:3, :] + dm2 * cwv[3:4, :]
            xm1, xp1, xp2 = _shifted(pad_s, xa, (-1, 1, 2))
            g_cw = jnp.concatenate([jnp.sum(dxc * v, axis=0, keepdims=True) for v in (xm1, xa, xp1, xp2)], axis=0)
            g_cb = jnp.sum(dxc, axis=0, keepdims=True)
            if first:
                dlam_ref[...] = g_lam
                dcw_ref[...] = g_cw
                dcb_ref[...] = g_cb
            else:
                dlam_ref[...] += g_lam
                dcw_ref[...] += g_cw
                dcb_ref[...] += g_cb
            return dxa, rho_f_first, rho_b_last

        ws_x, ws_c = workspace(main_s), workspace(ctx_s)
        h0f, h0b = forward(ws_c, xac_ref[...], LC, z, z)
        forward(ws_x, xa_ref[...], L, h0f, h0b)
        dh = dyl_ref[...]
        dxa, dh0f, dh0b = backward(ws_x, xa_ref[...], L, h0f, h0b, dh, dh, True)
        dxa_ref[...] = dxa.astype(BF16)
        rc = _rows((LC, HD))
        dxac, _, _ = backward(ws_c, xac_ref[...], LC, z, z, jnp.where(rc == LC - 1, dh0f, 0.0),
                              jnp.where(rc == 0, dh0b, 0.0), False)
        dxac_ref[...] = dxac.astype(BF16)

    s = _lru_param_specs()
    col = lambda r: pl.BlockSpec((r, HD), lambda h: (0, h))
    return _call(
        body, name="lru_backward", grid=(HEADS,),
        out_shape=[jax.ShapeDtypeStruct((L, D_IN), BF16), jax.ShapeDtypeStruct((LC, D), BF16),
                   jax.ShapeDtypeStruct((2, HEADS, HD, HD), F32), jax.ShapeDtypeStruct((2, HEADS, HD, HD), F32),
                   jax.ShapeDtypeStruct((2 * HEADS, HD), F32), jax.ShapeDtypeStruct((2 * HEADS, HD), F32),
                   jax.ShapeDtypeStruct((2, D), F32), jax.ShapeDtypeStruct((CONV_W, D), F32),
                   jax.ShapeDtypeStruct((1, D), F32)] + [jax.ShapeDtypeStruct((4,) + a.shape[1:], a.dtype)
                                                          for a in chip_sums] + _chips_stage_shapes(chip_sums),
        in_specs=[s["xa"], s["xac"], col(L), pl.BlockSpec(memory_space=pl.ANY), s["cw"], s["cb"], s["h4"], s["h4"],
                  s["b16"], s["b16"], s["v2"]] + [HBM] * nr,
        out_specs=[col(L), col(LC), s["h4"], s["h4"], s["b16"], s["b16"], s["v2"], col(CONV_W), col(1)]
        + [HBM] * (2 * nr),
        scratch_shapes=[pltpu.VMEM((17, L, HD), F32), pltpu.VMEM((17, LC, HD), F32), pltpu.VMEM((L + 16, HD), F32)]
        + (_chips_sems(nr) if nr else []),
        input_output_aliases={3: 0},
        compiler_params=pltpu.CompilerParams(dimension_semantics=("arbitrary",), vmem_limit_bytes=VMEM_LIMIT,
                                             has_side_effects=True, collective_id=6 if nr else None),
    )(zx, zc, dyl, dz, cw, cb, wa, wx, ba, bx, lam, *[pltpu.with_memory_space_constraint(a, pltpu.HBM)
                                                       for a in chip_sums])


def _mixer_loss(x, tgt, zx, yl, gx, fg, lng, lnb, ws, bst, wout, tm):
    ncht = tm // CHUNK

    def body(x_ref, t_ref, ga_ref, u_ref, v_ref, gb_ref, yl_ref, gx_ref, fg_ref, lng_ref, lnb_ref, ws_ref,
             bst_ref, wout_ref,
             dz_ref, dyl_ref, dxn_ref, y_s, do_ref, dws_ref, dbst_ref, vec_ref,
             vn_s, mix_s, dm_s, dvn_s):
        step = pl.program_id(0)

        @pl.when(step == 0)
        def _():
            dws_ref[...] = jnp.zeros_like(dws_ref)
            dbst_ref[...] = jnp.zeros_like(dbst_ref)
            vec_ref[...] = jnp.zeros_like(vec_ref)

        u, v = u_ref[...], v_ref[...]
        ug, dug_du = _gelu_and_grad(u)
        vg, dvg_dv = _gelu_and_grad(v)
        mu = jnp.mean(vg, axis=-1, keepdims=True)
        vc = vg - mu
        rstd = lax.rsqrt(jnp.mean(vc * vc, axis=-1, keepdims=True) + LN_EPS)
        vhat = vc * rstd
        lngv = lng_ref[...]
        vn_s[...] = (vhat * lngv + lnb_ref[...]).astype(BF16)
        for ch in range(ncht):
            rs = slice(ch * CHUNK, (ch + 1) * CHUNK)
            for g in range(HEADS):
                cs = slice(g * HD, (g + 1) * HD)
                mix_s[rs, cs] = _dot(ws_ref[g].astype(BF16), vn_s[rs, cs]) + bst_ref[:, g:g + 1]
        mixed = mix_s[...]
        ga, gb, yl = ga_ref[...], gb_ref[...], yl_ref[...]
        sga, dsga = _silu_and_grad(ga)
        sgb, dsgb = _silu_and_grad(gb)
        ys = ug * mixed
        y_s[:, 0:D] = (yl * sga).astype(BF16)
        y_s[:, D:D_MIX] = (ys * sgb).astype(BF16)
        o = _dot(y_s[...], wout_ref[...])
        gxv, fgv = gx_ref[...], fg_ref[...]
        xn = x_ref[...] + gxv * o
        rs2 = lax.rsqrt(jnp.mean(xn * xn, axis=-1, keepdims=True) + NORM_EPS)
        xh = xn * rs2
        diff = xh * fgv - t_ref[...]
        vec_ref[R_LOSS:R_LOSS + 1, :] += jnp.full((1, D), jnp.sum(diff * diff) * (0.5 / D), F32)
        dout = diff * (1.0 / D)
        w = dout * fgv
        dxn = rs2 * (w - xh * jnp.mean(w * xh, axis=-1, keepdims=True))
        dxn_ref[...] = dxn
        vec_ref[0:1, :] += jnp.sum(dxn * o, axis=0, keepdims=True)
        vec_ref[1:2, :] += jnp.sum(dout * xh, axis=0, keepdims=True)
        dob = (dxn * gxv).astype(BF16)
        do_ref[...] = dob
        dy = _dot_nt(dob, wout_ref[...])
        dya, dyb = dy[:, 0:D], dy[:, D:D_MIX]
        dyl_ref[...] = dya * sga
        dys = dyb * sgb
        dz_ref[:, 0:D] = jnp.zeros((tm, D), BF16)
        dz_ref[:, D:2 * D] = (dya * yl * dsga).astype(BF16)
        dz_ref[:, 2 * D:3 * D] = (dys * mixed * dug_du).astype(BF16)
        dz_ref[:, 4 * D:5 * D] = (dyb * ys * dsgb).astype(BF16)
        dm = dys * ug
        dm_s[...] = dm.astype(BF16)
        for g in range(HEADS):
            cs = slice(g * HD, (g + 1) * HD)
            dbst_ref[:, g:g + 1] += sum(jnp.sum(dm[ch * CHUNK:(ch + 1) * CHUNK, cs], axis=1, keepdims=True)
                                        for ch in range(ncht))
            for ch in range(ncht):
                rs = slice(ch * CHUNK, (ch + 1) * CHUNK)
                dws_ref[g] += _dot_nt(dm_s[rs, cs], vn_s[rs, cs])
                dvn_s[rs, cs] = _dot_tn(ws_ref[g].astype(BF16), dm_s[rs, cs])
        dvn = dvn_s[...]
        vec_ref[2:3, :] += jnp.sum(dvn * vhat, axis=0, keepdims=True)
        vec_ref[3:4, :] += jnp.sum(dvn, axis=0, keepdims=True)
        dvh = dvn * lngv
        dvg = rstd * (dvh - jnp.mean(dvh, axis=-1, keepdims=True) - vhat * jnp.mean(dvh * vhat, axis=-1, keepdims=True))
        dz_ref[:, 3 * D:4 * D] = (dvg * dvg_dv).astype(BF16)

    tile = pl.BlockSpec((tm, D), lambda i: (i, 0))
    zcol = lambda n: pl.BlockSpec((tm, D), lambda i: (i, n))
    vec = pl.BlockSpec((1, D), lambda i: (0, 0))
    full = lambda *s: pl.BlockSpec(s, lambda i: (0,) * len(s))
    return _call(
        body, name="mixer_loss", grid=(L // tm,),
        out_shape=[jax.ShapeDtypeStruct((L, D_IN), BF16), jax.ShapeDtypeStruct((L, D), F32),
                   jax.ShapeDtypeStruct((L, D), F32), jax.ShapeDtypeStruct((L, D_MIX), BF16),
                   jax.ShapeDtypeStruct((L, D), BF16),
                   jax.ShapeDtypeStruct((HEADS, CHUNK, CHUNK), F32), jax.ShapeDtypeStruct((CHUNK, HEADS), F32),
                   jax.ShapeDtypeStruct((8, D), F32)],
        in_specs=[tile, tile, zcol(1), zcol(2), zcol(3), zcol(4), tile, pl.BlockSpec((1, D), lambda i: (0, 2)),
                  vec, vec, vec,
                  full(HEADS, CHUNK, CHUNK), full(CHUNK, HEADS),
                  pl.BlockSpec((D_MIX, D), lambda i: (0, 0), pipeline_mode=pl.Buffered(1))],
        out_specs=[pl.BlockSpec((tm, D_IN), lambda i: (i, 0)), tile, tile,
                   pl.BlockSpec((tm, D_MIX), lambda i: (i, 0)), tile,
                   full(HEADS, CHUNK, CHUNK), full(CHUNK, HEADS), full(8, D)],
        scratch_shapes=[pltpu.VMEM((tm, D), BF16), pltpu.VMEM((tm, D), F32),
                        pltpu.VMEM((tm, D), BF16), pltpu.VMEM((tm, D), F32)],
        compiler_params=_params("arbitrary"),
    )(x, tgt, zx, zx, zx, zx, yl, gx, fg, lng, lnb, ws, bst, wout)


def _grad_w(a, b, a2, b2, tk, name, bw, first, nblocks, split, barrier_id, chip_sums=()):
    nk = a.shape[0] // tk
    m = a.shape[1]
    with_ctx = a2 is not None
    if split == "cols":
        slots, r, w = nblocks, m, bw // 2
        piece = lambda q, pc: (slice(None), slice(pc * w, (pc + 1) * w))
    else:
        slots, r, w = 4, m // 8, bw
        piece = lambda q, pc: (slice((2 * q + pc) * r, (2 * q + pc + 1) * r), slice(None))

    nr = len(chip_sums)

    def body(*refs):
        a_ref, b_ref = refs[:2]
        a2_ref, b2_ref = refs[2:4] if with_ctx else (None, None)
        base = 4 if with_ctx else 2
        sums_ref = refs[base + nr]
        acc, mine_v, send_v, stage_v, send_sems, recv_sems = refs[base + 3 * nr + 1:base + 3 * nr + 7]
        n, k = pl.program_id(0), pl.program_id(1)
        x, y, c = lax.axis_index("x"), lax.axis_index("y"), lax.axis_index("c")

        def to_sibling(s):
            return pltpu.make_async_remote_copy(src_ref=send_v.at[s], dst_ref=stage_v.at[s], send_sem=send_sems.at[s],
                                                recv_sem=recv_sems.at[s], device_id=(x, y, 1 - c),
                                                device_id_type=MESH)

        if nr:
            c_start, c_forward, c_finish = _chips_ops(refs[base:base + nr], refs[base + nr + 1:base + 2 * nr + 1],
                                                      refs[base + 2 * nr + 1:base + 3 * nr + 1],
                                                      *refs[base + 3 * nr + 7:])

            @pl.when(jnp.logical_and(n == 0, k == 0))
            def _():
                _barrier([(x, y, 1 - c)] + [(x ^ (j >> 1), y ^ (j & 1), c) for j in (1, 2)])
                c_start()
        else:
            pl.when(jnp.logical_and(n == 0, k == 0))(_sibling_barrier)

        @pl.when(k == 0)
        def _():
            acc[...] = _dot_tn(a_ref[...], b_ref[...])

        if nk > 1:
            @pl.when(k > 0)
            def _():
                acc[...] += _dot_tn(a_ref[...], b_ref[...])

        if with_ctx:
            @pl.when(jnp.logical_and(k == nk - 1, n == 0))
            def _():
                acc[:, 0:b2_ref.shape[1]] += _dot_tn(a2_ref[...], b2_ref[...])

        if nr:
            pl.when(jnp.logical_and(k == nk - 1, n == nblocks - 1))(c_forward)

        def hand_over(s, q):
            for pc in (0, 1):
                @pl.when(c == pc)
                def _(pc=pc):
                    mine_v[s] = acc[piece(q, pc)]
                    send_v[s] = acc[piece(q, 1 - pc)].astype(BF16)
            to_sibling(s).start()

        for i in range(nblocks):
            @pl.when(jnp.logical_and(k == nk - 1, n == i))
            def _(i=i):
                if split == "cols":
                    hand_over(i, 0)
                else:
                    for q in range(4):
                        hand_over(q, q)

        @pl.when(jnp.logical_and(k == nk - 1, n == nblocks - 1))
        def _():
            for s in range(slots):
                to_sibling(s).wait_recv()
                sums_ref[s] = (mine_v[s] + stage_v[s].astype(F32)).astype(BF16)
            for s in range(slots):
                to_sibling(s).wait_send()
            if nr:
                c_finish()

    in_specs = [pl.BlockSpec((tk, m), lambda n, k: (k, 0)), pl.BlockSpec((tk, bw), lambda n, k: (k, n + first))]
    args = [a, b]
    if with_ctx:
        in_specs += [pl.BlockSpec(a2.shape, lambda n, k: (0, 0)), pl.BlockSpec(b2.shape, lambda n, k: (0, 0))]
        args += [a2, b2]
    in_specs += [HBM] * nr
    args += [pltpu.with_memory_space_constraint(s, pltpu.HBM) for s in chip_sums]
    return _call(
        body, name=name, grid=(nblocks, nk),
        out_shape=[jax.ShapeDtypeStruct((slots, r, w), BF16)]
        + [jax.ShapeDtypeStruct((4,) + s.shape[1:], s.dtype) for s in chip_sums] + _chips_stage_shapes(chip_sums),
        in_specs=in_specs, out_specs=[pl.BlockSpec((slots, r, w), lambda n, k: (0, 0, 0))] + [HBM] * (2 * nr),
        scratch_shapes=[pltpu.VMEM((m, bw), F32), pltpu.VMEM((slots, r, w), F32), pltpu.VMEM((slots, r, w), BF16),
                        pltpu.VMEM((slots, r, w), BF16), pltpu.SemaphoreType.DMA((slots,)),
                        pltpu.SemaphoreType.DMA((slots,))] + (_chips_sems(nr) if nr else []),
        compiler_params=pltpu.CompilerParams(dimension_semantics=("arbitrary", "arbitrary"),
                                             vmem_limit_bytes=VMEM_LIMIT, has_side_effects=True,
                                             collective_id=barrier_id),
    )(*args)


def _grad_rows(xr, dz, w, mod, ng, dres, ncols, tm, name, chip_sums=(), first_chips=None, dests=None):
    rows = xr.shape[0]
    steps = rows // tm
    with_dx = dres is not None
    nr = len(chip_sums)
    dests = [d for d in (dests or [None] * nr)]
    nd = sum(d is not None for d in dests)
    nin = 6 if with_dx else 5
    nout = 2 if with_dx else 1

    def body(*refs):
        if with_dx:
            x_ref, dz_ref, w_ref, sc_ref, ng_ref, dres_ref = refs[:nin]
            dx_ref, vec_ref = refs[nin + nr + nd:nin + nr + nd + nout]
        else:
            x_ref, dz_ref, w_ref, sc_ref, ng_ref = refs[:nin]
            (vec_ref,) = refs[nin + nr + nd:nin + nr + nd + nout]
        if nr:
            o0 = nin + nr + nd + nout
            start, forward, finish = _chips_ops(refs[nin:nin + nr], refs[o0:o0 + nr], refs[o0 + nr:o0 + 2 * nr],
                                                *refs[o0 + 2 * nr:], first_chips=first_chips, barrier=True)
            pl.when(pl.program_id(0) == 0)(start)
            pl.when(pl.program_id(0) == steps // 2)(forward)
            pl.when(pl.program_id(0) == steps - 1)(finish)

        @pl.when(pl.program_id(0) == 0)
        def _():
            vec_ref[...] = jnp.zeros_like(vec_ref)

        dhn = _dot_nt(dz_ref[...], w_ref[...])
        x = x_ref[...]
        rs = lax.rsqrt(jnp.mean(x * x, axis=-1, keepdims=True) + NORM_EPS)
        xh = x * rs
        ngv = ng_ref[...]
        y = xh * ngv
        vec_ref[0:1, :] += jnp.sum(dhn, axis=0, keepdims=True)
        vec_ref[1:2, :] += jnp.sum(dhn * y, axis=0, keepdims=True)
        dy = dhn * (1.0 + sc_ref[...])
        vec_ref[2:3, :] += jnp.sum(dy * xh, axis=0, keepdims=True)
        if with_dx:
            dxh = dy * ngv
            dx_ref[...] = dres_ref[...] + rs * (dxh - xh * jnp.mean(dxh * xh, axis=-1, keepdims=True))

    tile = pl.BlockSpec((tm, D), lambda i: (i, 0))
    vec = pl.BlockSpec((1, D), lambda i: (0, 0))
    in_specs = [tile, pl.BlockSpec((tm, ncols), lambda i: (i, 0)),
                pl.BlockSpec((D, ncols), lambda i: (0, 0), pipeline_mode=pl.Buffered(1)),
                pl.BlockSpec((1, D), lambda i: (0, 1)), vec]
    out_shape = [jax.ShapeDtypeStruct((8, D), F32)]
    out_specs = [pl.BlockSpec((8, D), lambda i: (0, 0))]
    args = [xr, dz, w, mod, ng]
    if with_dx:
        in_specs.append(tile)
        out_shape.insert(0, jax.ShapeDtypeStruct((rows, D), F32))
        out_specs.insert(0, tile)
        args.append(dres)
    aliases = {}
    for j, d in enumerate(dests):
        if d is not None:
            aliases[len(args) + nr + len(aliases)] = len(out_shape) + j
    in_specs += [HBM] * (nr + nd)
    out_specs += [HBM] * (2 * nr)
    out_shape += [jax.ShapeDtypeStruct((4,) + a.shape[1:], a.dtype) for a in chip_sums]
    out_shape += _chips_stage_shapes(chip_sums)
    args += [pltpu.with_memory_space_constraint(a, pltpu.HBM) for a in chip_sums]
    args += [pltpu.with_memory_space_constraint(d, pltpu.HBM) for d in dests if d is not None]
    return _call(body, name=name, grid=(steps,), out_shape=out_shape, in_specs=in_specs, out_specs=out_specs,
                 scratch_shapes=_chips_sems(nr) if nr else [], input_output_aliases=aliases,
                 compiler_params=pltpu.CompilerParams(dimension_semantics=("arbitrary",),
                                                      vmem_limit_bytes=VMEM_LIMIT, has_side_effects=bool(nr),
                                                      collective_id=7 if nr else None))(*args)


def _adamw(w, g, m, v):
    m = ADAM_B1 * m + (1.0 - ADAM_B1) * g
    v = ADAM_B2 * v + (1.0 - ADAM_B2) * (g * g)
    m_hat = m / (1.0 - ADAM_B1 ** ADAM_STEP)
    v_hat = v / (1.0 - ADAM_B2 ** ADAM_STEP)
    delta = -ADAM_LR * (m_hat / (jnp.sqrt(v_hat) + ADAM_EPS) + ADAM_WD * w)
    return delta, m, v


def _adamw_reduced(parts, w, m, v, tr, name):
    r, n = w.shape
    nparts = parts.shape[0]

    def body(p_ref, w_ref, m_ref, v_ref, g_ref, d_ref, mo_ref, vo_ref):
        g = p_ref[0].astype(F32)
        for i in range(1, nparts):
            g = g + p_ref[i].astype(F32)
        g_ref[...] = g
        d_ref[...], mo_ref[...], vo_ref[...] = _adamw(w_ref[...], g, m_ref[...], v_ref[...])

    tile = pl.BlockSpec((tr, n), lambda i: (i, 0))
    sds = jax.ShapeDtypeStruct((r, n), F32)
    return _call(
        body, name=name, grid=(r // tr,), out_shape=[sds] * 4,
        in_specs=[pl.BlockSpec((nparts, tr, n), lambda i: (0, i, 0)), tile, tile, tile], out_specs=[tile] * 4,
        compiler_params=_params("arbitrary"),
    )(parts, w, m, v)


R_GATE, R_FINAL_G, R_LN_G, R_LN_B, R_LOSS = 0, 1, 2, 3, 4
R_SH_X, R_SC_X, R_NG_X = 5, 6, 7
R_SH_C, R_SC_C, R_NG_C = 8, 9, 10
R_LAM, R_CW, R_CB = 11, 13, 17
PACK_ROWS = 24
Q_BA, Q_BX, Q_SGU_B, PACK128_ROWS = 0, 16, 32, 40


def _reduce_small(vec_pieces, q_pieces, mat_parts, ada_w, me):
    nloc = ada_w.shape[1]
    nm = len(mat_parts)
    pieces = list(vec_pieces) + list(q_pieces)

    def body(me_ref, *refs):
        piece_refs, refs = refs[:len(pieces)], refs[len(pieces):]
        mp_refs, w_ref = refs[:nm], refs[nm]
        red_ref, redq_ref = refs[nm + 1:nm + 3]
        mats_all = refs[nm + 3:2 * nm + 3]
        cparts_ref, dmod_ref, gab_ref, loss_ref = refs[2 * nm + 3:2 * nm + 7]
        pack_ref, packq_ref, vp_ref, vq_ref = refs[2 * nm + 7:2 * nm + 11]
        mat_refs = refs[2 * nm + 11:3 * nm + 11]
        cpart_ref, dmc_s = refs[3 * nm + 11:3 * nm + 13]
        sems = refs[3 * nm + 13:]
        for dst, group in ((pack_ref, vec_pieces), (packq_ref, q_pieces)):
            row = 0
            for _, nrows in group:
                dst[row:row + nrows, :] = piece_refs[0][0:nrows, :]
                piece_refs, row = piece_refs[1:], row + nrows
            if row < dst.shape[0]:
                dst[row:, :] = jnp.zeros((dst.shape[0] - row, dst.shape[1]), F32)
        p_start, p_forward, p_finish = _gather2_ops([pack_ref, packq_ref], [vp_ref, vq_ref], ["ag", "ag"], *sems[:3],
                                                    barrier=True)
        m_start, m_forward, m_finish = _gather2_ops(mat_refs, mats_all, ["ag"] * nm, *sems[3:6])
        c_start, c_forward, c_finish = _gather2_ops([cpart_ref], [cparts_ref], ["ag"], *sems[6:])
        p_start()
        for mp_ref, mat_ref in zip(mp_refs, mat_refs):
            mat = mp_ref[0].astype(F32)
            for i in range(1, mp_ref.shape[0]):
                mat = mat + mp_ref[i].astype(F32)
            mat_ref[...] = mat
        m_start()
        p_forward()
        p_finish()
        m_forward()
        red, redq = vp_ref[0], vq_ref[0]
        for i in range(1, N_DEV):
            red = red + vp_ref[i]
            redq = redq + vq_ref[i]
        red_ref[...] = red
        redq_ref[...] = redq
        loss_ref[...] = red_ref[R_LOSS:R_LOSS + 1, 0:1]
        for e in range(N_DEV):
            dmod_ref[e:e + 1, 0:D] = vp_ref[e, R_SH_X:R_SH_X + 1, :]
            dmod_ref[e:e + 1, D:2 * D] = vp_ref[e, R_SC_X:R_SC_X + 1, :]
            dmod_ref[e:e + 1, 2 * D:3 * D] = vp_ref[e, R_GATE:R_GATE + 1, :]
        dmod_ref[8:9, 0:D] = red[R_SH_C:R_SH_C + 1, :]
        dmod_ref[8:9, D:2 * D] = red[R_SC_C:R_SC_C + 1, :]
        dmod_ref[8:9, 2 * D:3 * D] = jnp.zeros((1, D), F32)
        dmod_ref[9:16, :] = jnp.zeros((7, 3 * D), F32)
        gab_ref[:, 0:D] = red[R_SH_X:R_SH_X + 1, :] + red[R_SH_C:R_SH_C + 1, :]
        gab_ref[:, D:2 * D] = red[R_SC_X:R_SC_X + 1, :] + red[R_SC_C:R_SC_C + 1, :]
        gab_ref[:, 2 * D:3 * D] = red[R_GATE:R_GATE + 1, :]
        dmc_s[...] = jnp.broadcast_to(dmod_ref[8:9, :], (8, 3 * D))
        off = pl.multiple_of(me_ref[0] * nloc, 128)
        cpart_ref[...] = _dot_nt(dmc_s[:, pl.ds(off, nloc)], w_ref[...])
        c_start()
        c_forward()
        c_finish()
        m_finish()

    return _call(
        body, name="reduce_small",
        out_shape=[jax.ShapeDtypeStruct((PACK_ROWS, D), F32), jax.ShapeDtypeStruct((PACK128_ROWS, HD), F32)]
        + [jax.ShapeDtypeStruct((N_DEV,) + p.shape[1:], F32) for p in mat_parts]
        + [jax.ShapeDtypeStruct((N_DEV, 8, D), F32), jax.ShapeDtypeStruct((16, 3 * D), F32),
           jax.ShapeDtypeStruct((1, 3 * D), F32), jax.ShapeDtypeStruct((1, 1), F32)],
        in_specs=[pl.BlockSpec(memory_space=pltpu.SMEM)] + [VMEM] * (len(pieces) + nm + 1),
        out_specs=[VMEM] * (nm + 6),
        scratch_shapes=[pltpu.VMEM((PACK_ROWS, D), F32), pltpu.VMEM((PACK128_ROWS, HD), F32),
                        pltpu.VMEM((N_DEV, PACK_ROWS, D), F32), pltpu.VMEM((N_DEV, PACK128_ROWS, HD), F32)]
        + [pltpu.VMEM(p.shape[1:], F32) for p in mat_parts]
        + [pltpu.VMEM((8, D), F32), pltpu.VMEM((8, 3 * D), F32)] + _gather2_sems(2) + _gather2_sems(nm)
        + _gather2_sems(1),
        compiler_params=pltpu.CompilerParams(vmem_limit_bytes=VMEM_LIMIT, has_side_effects=True, collective_id=8),
    )(me, *[a for a, _ in pieces], *mat_parts, ada_w)


def _adamw_ada(c_all, c_ctx, dmod, w, m, v, me):
    nloc = w.shape[1]

    def body(me_ref, c_ref, cc_ref, dm_ref, w_ref, m_ref, v_ref, g_ref, d_ref, mo_ref, vo_ref):
        off = pl.multiple_of(me_ref[0] * nloc, 128)
        dm = dm_ref[:, pl.ds(off, nloc)]
        sx, _ = _silu_and_grad(c_ref[...])
        sc, _ = _silu_and_grad(cc_ref[...])
        g = _dot_tn(sx, dm[0:8, :]) + _dot_tn(jnp.broadcast_to(sc, (8, D)), dm[8:16, :])
        g_ref[...] = g
        d_ref[...], mo_ref[...], vo_ref[...] = _adamw(w_ref[...], g, m_ref[...], v_ref[...])

    sds = jax.ShapeDtypeStruct(w.shape, F32)
    return _call(
        body, name="adamw_ada_w", out_shape=[sds] * 4,
        in_specs=[pl.BlockSpec(memory_space=pltpu.SMEM)] + [VMEM] * 6, out_specs=[VMEM] * 4,
        compiler_params=_params(),
    )(me, c_all, c_ctx, dmod, w, m, v)


_SMALL = ("c_ctx", "ada_b", "norm_g", "conv_w", "conv_b", "lru_wa", "lru_ba", "lru_wx", "lru_bx", "lru_lambda",
          "sgu_ln_g", "sgu_ln_b", "sgu_w", "sgu_b", "final_g")


def _adamw_small(red, redq, mats, cparts, gab, ws, ms, vs, me):
    n = len(_SMALL)

    def body(me_ref, red_ref, redq_ref, wa_ref, wx_ref, sw_ref, cp_ref, gab_ref, *refs):
        w_refs, m_refs, v_refs = refs[:n], refs[n:2 * n], refs[2 * n:3 * n]
        outs = refs[3 * n:]
        off = pl.multiple_of(me_ref[0] * HD, 128)

        def row(r, k=1):
            return red_ref[r:r + k, :]

        cc = w_refs[0][...]
        dcc = cp_ref[0, 0:1, :]
        for i in range(1, N_DEV):
            dcc = dcc + cp_ref[i, 0:1, :]
        grads = dict(
            c_ctx=dcc * _silu_and_grad(cc)[1], ada_b=gab_ref[...], norm_g=row(R_NG_X) + row(R_NG_C),
            conv_w=red_ref[R_CW:R_CW + CONV_W, pl.ds(off, HD)], conv_b=row(R_CB),
            lru_wa=wa_ref[...], lru_ba=redq_ref[Q_BA:Q_BA + 2 * HEADS, :], lru_wx=wx_ref[...],
            lru_bx=redq_ref[Q_BX:Q_BX + 2 * HEADS, :], lru_lambda=red_ref[R_LAM:R_LAM + 2, pl.ds(off, HD)],
            sgu_ln_g=row(R_LN_G), sgu_ln_b=row(R_LN_B), sgu_w=sw_ref[...],
            sgu_b=redq_ref[Q_SGU_B:Q_SGU_B + HEADS, :], final_g=row(R_FINAL_G))
        for j, name in enumerate(_SMALL):
            g = grads[name]
            outs[j][...] = g
            outs[n + j][...], outs[2 * n + j][...], outs[3 * n + j][...] = _adamw(w_refs[j][...], g, m_refs[j][...],
                                                                                 v_refs[j][...])

    sds = [jax.ShapeDtypeStruct(ws[k].shape, F32) for k in _SMALL]
    outs = _call(
        body, name="adamw_small", out_shape=sds * 4,
        in_specs=[pl.BlockSpec(memory_space=pltpu.SMEM)] + [VMEM] * (7 + 3 * n), out_specs=[VMEM] * (4 * n),
        compiler_params=_params(),
    )(me, red, redq, *mats, cparts, gab, *[ws[k] for k in _SMALL], *[ms[k] for k in _SMALL],
      *[vs[k] for k in _SMALL])
    return [dict(zip(_SMALL, outs[i * n:(i + 1) * n])) for i in range(4)]


def kernel(x, c, ctx, c_ctx, ada_w, ada_b, norm_g, w_in, conv_w, conv_b, lru_wa, lru_ba, lru_wx, lru_bx, lru_lambda, sgu_ln_g, sgu_ln_b, sgu_w, sgu_b, w_out, final_g, loss_target, m_c_ctx, m_ada_w, m_ada_b, m_norm_g, m_w_in, m_conv_w, m_conv_b, m_lru_wa, m_lru_ba, m_lru_wx, m_lru_bx, m_lru_lambda, m_sgu_ln_g, m_sgu_ln_b, m_sgu_w, m_sgu_b, m_w_out, m_final_g, v_c_ctx, v_ada_w, v_ada_b, v_norm_g, v_w_in, v_conv_w, v_conv_b, v_lru_wa, v_lru_ba, v_lru_wx, v_lru_bx, v_lru_lambda, v_sgu_ln_g, v_sgu_ln_b, v_sgu_w, v_sgu_b, v_w_out, v_final_g):
    args = dict(locals())
    me = (4 * lax.axis_index("x") + 2 * lax.axis_index("y") + lax.axis_index("c")).astype(jnp.int32).reshape(1)
    xr, ctxr, tgt = x[0], ctx[0], loss_target[0]
    cc = c_ctx.reshape(1, D)
    nw = 2 * HEADS * HD
    view = dict(c_ctx=(1, D), ada_b=(1, 3 * D), norm_g=(1, D), conv_w=(CONV_W, HD), conv_b=(1, D), lru_wa=(nw, HD),
                lru_ba=(2 * HEADS, HD), lru_wx=(nw, HD), lru_bx=(2 * HEADS, HD), lru_lambda=(2, HD), sgu_ln_g=(1, D),
                sgu_ln_b=(1, D), sgu_w=(HEADS * CHUNK, CHUNK), sgu_b=(HEADS, CHUNK), final_g=(1, D))

    zx, hn, w_full, w_out_b, modx, modc, c_all, cw_full, lam_full = _front_project(
        xr, c, cc, ada_w[0], ada_b, norm_g, w_in[0], w_out[0], conv_w[0], lru_lambda[0], me)
    zc, hnc = _project(ctxr, modc, norm_g, w_full, D, LC, "project_ctx")
    ba, bx = lru_ba.reshape(view["lru_ba"]), lru_bx.reshape(view["lru_bx"])
    yl, wout_all = _lru_forward(zx, zc, cw_full, conv_b, lru_wa[0], lru_wx[0], ba, bx, lam_full, [w_out_b], ["ag"])
    wout_full = wout_all.reshape(D_MIX, D)
    dz, dyl, dxn, ycat, dob, dws, dbst, mvec = _mixer_loss(
        xr, tgt, zx, yl, modx, final_g.reshape(1, D), sgu_ln_g, sgu_ln_b, sgu_w[0], sgu_b[0].T, wout_full, ROWS)

    (wout_sums,) = _grad_w(ycat, dob, None, None, L, "grad_w_out", D, 0, 1, "rows", 1)
    (rest_sums,) = _grad_w(hn, dz, None, None, L, "grad_w_in_rest", 2 * W_IN_SHARD, 1, 3, "cols", 2)
    dz, dxac, dwa, dwx, dba, dbx, dlam, dcw, dcb, win_parts, wout_parts, _, _ = _lru_backward(
        zx, zc, dyl, dz, cw_full, conv_b, lru_wa[0], lru_wx[0], ba, bx, lam_full, [rest_sums, wout_sums],
        first_chips=[1, 0])
    mats = [dwa.reshape(N_DEV, nw // N_DEV, HD), dwx.reshape(N_DEV, nw // N_DEV, HD), dws]
    mat_sums = _reduce2_local(mats, me, "reduce_mat", 4, BF16)
    first_sums, *mat_parts = _grad_w(hn, dz, hnc, dxac, L, "grad_w_in_first", 2 * W_IN_SHARD, 0, 1, "cols", 3,
                                     chip_sums=mat_sums)[:4]
    gx, xvec, win_parts = _grad_rows(
        xr, dz, w_full, modx, norm_g, dxn, D_IN, ROWS, "grad_rows_x", chip_sums=[first_sums], first_chips=[0],
        dests=[win_parts])[:3]
    (cvec,) = _grad_rows(ctxr, dxac, w_full, modc, norm_g, None, D, LC, "grad_rows_ctx")
    red, redq, *rest = _reduce_small(
        [(mvec, 5), (xvec, 3), (cvec, 3), (dlam, 2), (dcw, CONV_W), (dcb, 1)],
        [(dba, 2 * HEADS), (dbx, 2 * HEADS), (dbst.T, HEADS)], mat_parts, ada_w[0], me)
    mats_all, (cparts, dmod, gab, loss) = rest[:3], rest[3:]

    g_w_in, d_w_in, nm_w_in, nv_w_in = _adamw_reduced(win_parts, w_in[0], m_w_in[0], v_w_in[0], ROWS, "adamw_w_in")
    g_w_out, d_w_out, nm_w_out, nv_w_out = _adamw_reduced(wout_parts, w_out[0], m_w_out[0], v_w_out[0], ROWS // 2,
                                                          "adamw_w_out")
    g_ada, d_ada, nm_ada, nv_ada = _adamw_ada(c_all, cc, dmod, ada_w[0], m_ada_w[0], v_ada_w[0], me)
    ws = {k: args[k].reshape(view[k]) for k in _SMALL}
    ms = {k: args["m_" + k].reshape(view[k]) for k in _SMALL}
    vs = {k: args["v_" + k].reshape(view[k]) for k in _SMALL}
    small = _adamw_small(red, redq, [m.reshape(-1, HD) for m in mats_all], cparts, gab, ws, ms, vs, me)
    big = dict(w_in=(g_w_in, d_w_in, nm_w_in, nv_w_in), w_out=(g_w_out, d_w_out, nm_w_out, nv_w_out),
               ada_w=(g_ada, d_ada, nm_ada, nv_ada))

    loss = loss.reshape(())
    names = ("c_ctx", "ada_w", "ada_b", "norm_g", "w_in", "conv_w", "conv_b", "lru_wa", "lru_ba", "lru_wx", "lru_bx",
             "lru_lambda", "sgu_ln_g", "sgu_ln_b", "sgu_w", "sgu_b", "w_out", "final_g")
    outs = [loss, gx.reshape(x.shape)]
    for kind in range(4):
        for k in names:
            val = big[k][kind] if k in big else small[kind][k]
            outs.append(val.reshape(args[k].shape))
    return tuple(outs)
```

```python
import jax
import jax.numpy as jnp
from jax import lax
from jax.experimental import pallas as pl
from jax.experimental.pallas import tpu as pltpu

F32 = jnp.float32
BF16 = jnp.bfloat16

N_DEV = 8
D = 1024
L = 2048
LC = 256
HEADS = 8
HD = 128
CHUNK = 128
D_IN = 5 * D
W_IN_SHARD = D_IN // N_DEV
ROWS = 256
D_MIX = 2 * D
CONV_W = 4
LRU_C = 8.0
NORM_EPS = 1e-6
LN_EPS = 1e-5
ADAM_LR, ADAM_B1, ADAM_B2, ADAM_EPS, ADAM_WD, ADAM_STEP = 0.001, 0.9, 0.999, 1e-08, 0.01, 10

VMEM_LIMIT = 56 * 1024 * 1024

HBM = pl.BlockSpec(memory_space=pltpu.HBM)
VMEM = pl.BlockSpec(memory_space=pltpu.VMEM)
MESH = pl.DeviceIdType.MESH


def _call(body, **kw):
    return pl.pallas_call(body, **kw)


def _params(*sem):
    return pltpu.CompilerParams(dimension_semantics=sem, vmem_limit_bytes=VMEM_LIMIT)


def _sigmoid(x):
    return 0.5 * jnp.tanh(0.5 * x) + 0.5


def _silu_and_grad(x):
    s = _sigmoid(x)
    return x * s, s * (1.0 + x * (1.0 - s))


_G0 = 0.7978845608028654
_G1 = 0.044715


def _gelu_and_grad(x):
    x2 = x * x
    t = jnp.tanh(_G0 * (x + _G1 * x * x2))
    cdf = 0.5 * (1.0 + t)
    return x * cdf, cdf + 0.5 * x * (1.0 - t * t) * (_G0 * (1.0 + 3.0 * _G1 * x2))


def _softplus(z):
    t = jnp.exp(-jnp.abs(z))
    u = 1.0 + t
    log1p = jnp.where(u == 1.0, t, jnp.log(u) * t / jnp.where(u == 1.0, 1.0, u - 1.0))
    return jnp.maximum(z, 0.0) + log1p


def _dot(a, b):
    return jnp.dot(a, b, preferred_element_type=F32)


def _dot_nt(a, b):
    return lax.dot_general(a, b, (((1,), (1,)), ((), ())), preferred_element_type=F32)


def _dot_tn(a, b):
    return lax.dot_general(a, b, (((0,), (0,)), ((), ())), preferred_element_type=F32)


def _rows(shape):
    return lax.broadcasted_iota(jnp.int32, shape, 0)


def _gather2_shapes(arrays, modes):
    return [jax.ShapeDtypeStruct((N_DEV,) + a.shape if m == "ag" else (a.shape[0], N_DEV * a.shape[1]), a.dtype)
            for a, m in zip(arrays, modes)]


def _gather2_sems(n):
    return [pltpu.SemaphoreType.DMA((n, N_DEV - 1)), pltpu.SemaphoreType.DMA((n, N_DEV - 1)),
            pltpu.SemaphoreType.DMA((n,))]


def _barrier(peers):
    sem = pltpu.get_barrier_semaphore()
    for peer in peers:
        pl.semaphore_signal(sem, inc=1, device_id=peer, device_id_type=MESH)
    pl.semaphore_wait(sem, len(peers))


def _gather2_ops(ins, outs, modes, send_sems, recv_sems, local_sems, barrier=False):
    n = len(ins)
    x, y, c = lax.axis_index("x"), lax.axis_index("y"), lax.axis_index("c")
    me, sibling = (x, y, c), (x, y, 1 - c)
    chips = [(x ^ (k >> 1), y ^ (k & 1)) for k in (1, 2, 3)]

    def slot(j, px, py, pc):
        dev = 4 * px + 2 * py + pc
        if modes[j] == "agc":
            w = ins[j].shape[1]
            return outs[j].at[:, pl.ds(pl.multiple_of(dev * w, 128), w)]
        return outs[j].at[dev]

    def copy(j, k, block, to, src=None):
        return pltpu.make_async_remote_copy(
            src_ref=slot(j, *block) if src is None else src, dst_ref=slot(j, *block),
            send_sem=send_sems.at[j, k], recv_sem=recv_sems.at[j, k], device_id=to, device_id_type=MESH)

    def own(j):
        return pltpu.make_async_copy(ins[j], slot(j, *me), local_sems.at[j])

    def first(j):
        return [copy(j, 0, me, sibling, src=ins[j])] + [copy(j, 1 + i, me, (*chip, c), src=ins[j])
                                                        for i, chip in enumerate(chips)]

    def passed(j, i):
        return copy(j, 4 + i, (*chips[i], c), sibling)

    def start():
        if barrier:
            _barrier([sibling] + [(*chip, c) for chip in chips])
        for j in range(n):
            own(j).start()
            for cp in first(j):
                cp.start()

    def forward():
        for i, chip in enumerate(chips):
            for j in range(n):
                copy(j, 1 + i, (*chip, c), me).wait_recv()
                passed(j, i).start()

    def finish():
        for j in range(n):
            copy(j, 0, sibling, me).wait_recv()
            for i, chip in enumerate(chips):
                copy(j, 4 + i, (*chip, 1 - c), me).wait_recv()
            for cp in first(j) + [passed(j, i) for i in range(3)]:
                cp.wait_send()
            own(j).wait()

    return start, forward, finish


def _sibling_barrier():
    sem = pltpu.get_barrier_semaphore()
    sibling = (lax.axis_index("x"), lax.axis_index("y"), 1 - lax.axis_index("c"))
    pl.semaphore_signal(sem, inc=1, device_id=sibling, device_id_type=MESH)
    pl.semaphore_wait(sem, 1)


def _chips_sems(n):
    return [pltpu.SemaphoreType.DMA((n, 6)), pltpu.SemaphoreType.DMA((n, 6)), pltpu.SemaphoreType.DMA((n,))]


def _chips_stage_shapes(chip_sums):
    return [jax.ShapeDtypeStruct((2, a.shape[1] // 2, a.shape[2]), a.dtype) for a in chip_sums]


def _chips_ops(ins, outs, stages, send_sems, recv_sems, local_sems, first_chips=None, barrier=False):
    x, y, c = lax.axis_index("x"), lax.axis_index("y"), lax.axis_index("c")
    qm = 2 * x + y
    first_chips = first_chips or [0] * len(ins)

    def owns(j, chip):
        lo, cnt = first_chips[j], ins[j].shape[0]
        if lo == 0 and cnt == 4:
            return None
        return jnp.logical_and(chip >= lo, chip < lo + cnt)

    def guarded(cond, fn):
        if cond is None:
            fn()
        else:
            pl.when(cond)(fn)

    def slot(j, chip):
        return jnp.clip(chip - first_chips[j], 0, ins[j].shape[0] - 1)

    def half(j, i):
        h = ins[j].shape[1] // 2
        return pl.ds(i * h, h)

    def copy(j, sem, src, dst, k):
        return pltpu.make_async_remote_copy(
            src_ref=src, dst_ref=dst, send_sem=send_sems.at[j, sem], recv_sem=recv_sems.at[j, sem],
            device_id=(x ^ (k >> 1), y ^ (k & 1), c), device_id_type=MESH)

    def direct(j, k):
        return copy(j, k - 1, ins[j].at[slot(j, qm ^ k)], outs[j].at[qm], k)

    def first_hop(j, k):
        return copy(j, 1 + k, ins[j].at[slot(j, qm ^ 3), half(j, k - 1)], stages[j].at[k - 1], k)

    def second_hop(j, k):
        return copy(j, 3 + k, stages[j].at[2 - k], outs[j].at[qm ^ (3 - k), half(j, 2 - k)], k)

    def local(j):
        return pltpu.make_async_copy(ins[j].at[slot(j, qm)], outs[j].at[qm], local_sems.at[j])

    def start():
        if barrier:
            _barrier([(x ^ (k >> 1), y ^ (k & 1), c) for k in (1, 2)])
        for j in range(len(ins)):
            for k in (1, 2):
                guarded(owns(j, qm ^ 3), lambda j=j, k=k: first_hop(j, k).start())
        for j in range(len(ins)):
            for k in (1, 2):
                guarded(owns(j, qm ^ k), lambda j=j, k=k: direct(j, k).start())
            guarded(owns(j, qm), lambda j=j: local(j).start())

    def forward():
        for j in range(len(ins)):
            for k in (1, 2):
                def pass_on(j=j, k=k):
                    first_hop(j, 3 - k).wait_recv()
                    second_hop(j, k).start()
                guarded(owns(j, qm ^ k), pass_on)

    def finish():
        for j in range(len(ins)):
            for k in (1, 2):
                guarded(owns(j, qm ^ k), lambda j=j, k=k: direct(j, k).wait_send())
                guarded(owns(j, qm ^ k), lambda j=j, k=k: second_hop(j, k).wait_send())
                guarded(owns(j, qm ^ 3), lambda j=j, k=k: first_hop(j, k).wait_send())
                guarded(owns(j, qm), lambda j=j, k=k: direct(j, k).wait_recv())
                guarded(owns(j, qm), lambda j=j, k=k: second_hop(j, k).wait_recv())
            guarded(owns(j, qm), lambda j=j: local(j).wait())

    return start, forward, finish


ARRIVAL = (0, 1, 2, 4, 3, 5, 6, 7)


def _front_project(xr, c, c_ctx, ada_w, ada_b, ng, w_in, w_out, cw, lam, me):
    nloc = ada_w.shape[1]
    ws = W_IN_SHARD
    arrival = jnp.asarray(ARRIVAL, jnp.int32)

    def body(me_ref, arr_ref, x_ref, c_ref, cc_ref, aw_ref, ab_ref, ng_ref, win_ref, wout_ref, cw_ref, lam_ref,
             z_ref, hn_ref, wfull_ref, woutb_ref, modx_ref, modc_ref, call_ref, cwf_ref, lamf_ref,
             wv, call_s, part_s, parts_s, w_send, w_recv, hbm_sems, s_send, s_recv, g_send, g_recv, g_local):
        t = pl.program_id(0)
        x, y, cidx = lax.axis_index("x"), lax.axis_index("y"), lax.axis_index("c")
        me_i = me_ref[0]
        sibling = (x, y, 1 - cidx)
        chips = [(x ^ (k >> 1), y ^ (k & 1)) for k in (1, 2, 3)]
        g_start, g_pass, g_finish = _gather2_ops([cw_ref, lam_ref], [cwf_ref, lamf_ref], ["agc", "agc"],
                                                 g_send, g_recv, g_local)

        def shard_copy(k, px, py, pc, to, half=None):
            slot = wv.at[4 * px + 2 * py + pc]
            if half is not None:
                slot = slot.at[pl.ds(half * (D // 2), D // 2), :]
            return pltpu.make_async_remote_copy(src_ref=slot, dst_ref=slot, send_sem=w_send.at[k],
                                                recv_sem=w_recv.at[k], device_id=to, device_id_type=MESH)

        def small_gather(src, my_slot, stage):
            copies = []
            for k in range(1, N_DEV):
                peer = (x ^ (k >> 2), y ^ ((k >> 1) & 1), cidx ^ (k & 1))
                cp = pltpu.make_async_remote_copy(src_ref=src, dst_ref=my_slot, send_sem=s_send.at[stage, k - 1],
                                                  recv_sem=s_recv.at[stage, k - 1], device_id=peer,
                                                  device_id_type=MESH)
                cp.start()
                copies.append(cp)
            pltpu.sync_copy(src, my_slot)
            return copies

        def finish_small(copies):
            for cp in copies:
                cp.wait()

        def to_neighbours(half):
            for i in (0, 1):
                shard_copy(1 + i, x, y, cidx, (*chips[i], cidx), half=half).start()

        @pl.when(t == 0)
        def _():
            _barrier([(x ^ (k >> 2), y ^ ((k >> 1) & 1), cidx ^ (k & 1)) for k in range(1, N_DEV)])
            g_start()
            wv[me_i] = win_ref[...].astype(BF16)
            woutb_ref[...] = wout_ref[...].astype(BF16)
            shard_copy(0, x, y, cidx, sibling).start()
            finish_small(small_gather(c_ref, call_s.at[pl.ds(me_i, 1), :], 0))
            to_neighbours(0)
            call_ref[...] = call_s[...]
            off = pl.multiple_of(me_i * nloc, 128)
            b = ab_ref[:, pl.ds(off, nloc)]
            w = aw_ref[...]
            sx, _ = _silu_and_grad(call_s[...])
            sc, _ = _silu_and_grad(jnp.broadcast_to(cc_ref[...], (8, D)))
            part_s[0:8, :] = _dot(sx, w) + b
            part_s[8:16, :] = _dot(sc, w) + b
            parts_sent = small_gather(part_s, parts_s.at[me_i], 1)
            to_neighbours(1)
            finish_small(parts_sent)
            mine = _rows((16, nloc)) == me_i
            for j in range(N_DEV):
                pj = parts_s[j]
                modx_ref[:, j * nloc:(j + 1) * nloc] = jnp.sum(jnp.where(mine, pj, 0.0), axis=0, keepdims=True)
                modc_ref[:, j * nloc:(j + 1) * nloc] = pj[8:9, :]
            shift, scale1, ngv = modx_ref[:, 0:D], 1.0 + modx_ref[:, D:2 * D], ng_ref[...]
            for r in range(L // ROWS):
                rsl = slice(r * ROWS, (r + 1) * ROWS)
                xv = x_ref[rsl, :]
                rs = lax.rsqrt(jnp.mean(xv * xv, axis=-1, keepdims=True) + NORM_EPS)
                hn_ref[rsl, :] = ((xv * rs * ngv) * scale1 + shift).astype(BF16)

        @pl.when(t == 1)
        def _():
            shard_copy(0, x, y, 1 - cidx, sibling).wait_recv()
            g_pass()

        for i in (0, 1):
            @pl.when(t == ARRIVAL.index((2, 4)[i]))
            def _(i=i):
                shard_copy(1 + i, *chips[i], cidx, sibling).wait_recv()
                shard_copy(4 + i, *chips[i], cidx, sibling).start()
                shard_copy((7, 3)[i], *chips[i], cidx, (*chips[1 - i], cidx), half=i).start()

        @pl.when(t == ARRIVAL.index(6))
        def _():
            shard_copy(3, *chips[2], cidx, sibling, half=1).wait_recv()
            shard_copy(7, *chips[2], cidx, sibling, half=0).wait_recv()
            shard_copy(6, *chips[2], cidx, sibling).start()

        for i in range(3):
            @pl.when(t == ARRIVAL.index((3, 5, 7)[i]))
            def _(i=i):
                shard_copy(4 + i, *chips[i], 1 - cidx, sibling).wait_recv()

        @pl.when(t == 2)
        def _():
            g_finish()

        dev = me_i ^ arr_ref[t]
        for r in range(L // (2 * ROWS)):
            rsl = slice(r * 2 * ROWS, (r + 1) * 2 * ROWS)
            z_ref[rsl, :] = _dot(hn_ref[rsl, :], wv[dev])
        col = pl.ds(pl.multiple_of(dev * ws, 128), ws)
        pltpu.make_async_copy(wv.at[dev], wfull_ref.at[:, col], hbm_sems.at[t]).start()

        @pl.when(t == N_DEV - 1)
        def _():
            for k in (0, 1, 2, 4, 5, 6):
                shard_copy(k, x, y, cidx, sibling).wait_send()
            for k in (3, 7):
                shard_copy(k, x, y, cidx, sibling, half=0).wait_send()
            for s in range(N_DEV):
                pltpu.make_async_copy(wv.at[0], wfull_ref.at[:, pl.ds(0, ws)], hbm_sems.at[s]).wait()

    const = lambda *shape: pl.BlockSpec(shape, lambda t, m, a: (0,) * len(shape))
    once = lambda *shape: pl.BlockSpec(shape, lambda t, m, a: (0,) * len(shape), pipeline_mode=pl.Buffered(1))
    return _call(
        body, name="front_project",
        out_shape=[jax.ShapeDtypeStruct((L, D_IN), F32), jax.ShapeDtypeStruct((L, D), BF16),
                   jax.ShapeDtypeStruct((D, D_IN), BF16), jax.ShapeDtypeStruct(w_out.shape, BF16),
                   jax.ShapeDtypeStruct((1, 3 * D), F32), jax.ShapeDtypeStruct((1, 3 * D), F32),
                   jax.ShapeDtypeStruct((N_DEV, D), F32), jax.ShapeDtypeStruct((CONV_W, D), F32),
                   jax.ShapeDtypeStruct((2, D), F32)],
        grid_spec=pltpu.PrefetchScalarGridSpec(
            num_scalar_prefetch=2, grid=(N_DEV,),
            in_specs=[once(L, D), const(1, D), const(1, D), once(D, nloc), const(1, 3 * D), const(1, D),
                      once(D, ws), once(*w_out.shape), HBM, HBM],
            out_specs=[pl.BlockSpec((L, ws), lambda t, m, a: (0, m[0] ^ a[t])), const(L, D), HBM,
                       const(*w_out.shape),
                       const(1, 3 * D), const(1, 3 * D), const(N_DEV, D), HBM, HBM],
            scratch_shapes=[pltpu.VMEM((N_DEV, D, ws), BF16), pltpu.VMEM((N_DEV, D), F32), pltpu.VMEM((16, nloc), F32),
                            pltpu.VMEM((N_DEV, 16, nloc), F32), pltpu.SemaphoreType.DMA((8,)),
                            pltpu.SemaphoreType.DMA((8,)), pltpu.SemaphoreType.DMA((N_DEV,)),
                            pltpu.SemaphoreType.DMA((2, N_DEV - 1)), pltpu.SemaphoreType.DMA((2, N_DEV - 1))]
            + _gather2_sems(2)),
        compiler_params=pltpu.CompilerParams(dimension_semantics=("arbitrary",), vmem_limit_bytes=VMEM_LIMIT,
                                             has_side_effects=True, collective_id=10),
    )(me, arrival, xr, c, c_ctx, ada_w, ada_b, ng, w_in, w_out, pltpu.with_memory_space_constraint(cw, pltpu.HBM),
      pltpu.with_memory_space_constraint(lam, pltpu.HBM))


def _project(xr, mod, ng, w, ncols, tm, name):
    rows = xr.shape[0]

    def body(x_ref, sh_ref, sc_ref, ng_ref, w_ref, z_ref, hn_ref):
        x = x_ref[...]
        rs = lax.rsqrt(jnp.mean(x * x, axis=-1, keepdims=True) + NORM_EPS)
        hn = (x * rs * ng_ref[...]) * (1.0 + sc_ref[...]) + sh_ref[...]
        hb = hn.astype(BF16)
        hn_ref[...] = hb
        for n in range(ncols // D):
            z_ref[:, n * D:(n + 1) * D] = _dot(hb, w_ref[:, n * D:(n + 1) * D])

    vec = pl.BlockSpec((1, D), lambda i: (0, 0))
    return _call(
        body, name=name, grid=(rows // tm,),
        out_shape=[jax.ShapeDtypeStruct((rows, ncols), F32), jax.ShapeDtypeStruct((rows, D), BF16)],
        in_specs=[pl.BlockSpec((tm, D), lambda i: (i, 0)), vec, pl.BlockSpec((1, D), lambda i: (0, 1)), vec,
                  pl.BlockSpec((D, ncols), lambda i: (0, 0), pipeline_mode=pl.Buffered(1))],
        out_specs=[pl.BlockSpec((tm, ncols), lambda i: (i, 0)), pl.BlockSpec((tm, D), lambda i: (i, 0))],
        compiler_params=_params("arbitrary"),
    )(xr, mod, mod, ng, w)


def _scan_pair(af_ref, uf_ref, hf_ref, h0f, ab_ref, ub_ref, hb_ref, h0b, t_len):
    span = 8 * SCAN_BLOCKS
    nit = t_len // span
    rows = _rows((8, HD))

    def local_scan(a, b, forward):
        for s in (1, 2, 4):
            sh = s if forward else 8 - s
            m = rows >= s if forward else rows < 8 - s
            b = a * jnp.where(m, pltpu.roll(b, sh, 0), 0.0) + b
            a = a * jnp.where(m, pltpu.roll(a, sh, 0), 1.0)
        return a, b

    def span_scan(a_ref, u_ref, h_ref, off, carry, forward):
        order = range(SCAN_BLOCKS) if forward else range(SCAN_BLOCKS - 1, -1, -1)
        last = slice(7, 8) if forward else slice(0, 1)
        for q in order:
            rs = pl.ds(off + 8 * q, 8)
            a, b = local_scan(a_ref[rs, :], u_ref[rs, :], forward)
            h_ref[rs, :] = b + a * carry
            carry = a[last, :] * carry + b[last, :]
        return carry

    def body(k, carry):
        cf, cb = carry
        cf = span_scan(af_ref, uf_ref, hf_ref, pl.multiple_of(k * span, span), cf, True)
        cb = span_scan(ab_ref, ub_ref, hb_ref, pl.multiple_of((nit - 1 - k) * span, span), cb, False)
        return cf, cb

    return lax.fori_loop(0, nit, body, (h0f, h0b))


SCAN_BLOCKS = 16


def _shifted(pad_ref, x, offsets, before=0.0, after=0.0):
    n = x.shape[0]
    pad_ref[0:8, :] = jnp.broadcast_to(jnp.asarray(before, F32), (8, x.shape[1]))
    pad_ref[8:8 + n, :] = x
    pad_ref[8 + n:16 + n, :] = jnp.broadcast_to(jnp.asarray(after, F32), (8, x.shape[1]))
    return [pad_ref[8 + o:8 + o + n, :] for o in offsets]


def _conv(xa, cw, cb, pad_ref):
    xm1, xp1, xp2 = _shifted(pad_ref, xa, (-1, 1, 2))
    return xm1 * cw[0:1, :] + xa * cw[1:2, :] + xp1 * cw[2:3, :] + xp2 * cw[3:4, :] + cb


def _gates(xc, wa, wx, ba, bx, nsp):
    xb = xc.astype(BF16)
    r = _sigmoid(_dot(xb, wa) + ba)
    i = _sigmoid(_dot(xb, wx) + bx)
    log_a = r * nsp
    a = jnp.exp(log_a)
    g2 = jnp.tanh(log_a) * (-1.0 - a * a)
    rg = lax.rsqrt(jnp.maximum(g2, 1e-30))
    return r, i, a, g2 * rg, rg


def _lru_param_specs():
    h4 = pl.BlockSpec((2, 1, HD, HD), lambda h: (0, h, 0, 0))
    v2 = pl.BlockSpec((2, HD), lambda h: (0, h))
    b16 = pl.BlockSpec((2 * HEADS, HD), lambda h: (0, 0))
    return dict(
        xa=pl.BlockSpec((L, HD), lambda h: (0, h)), xac=pl.BlockSpec((LC, HD), lambda h: (0, h)),
        cw=pl.BlockSpec((CONV_W, HD), lambda h: (0, h)), cb=pl.BlockSpec((1, HD), lambda h: (0, h)), h4=h4, v2=v2,
        b16=b16)


def _bias_row(ref, d):
    mask = _rows((2 * HEADS, HD)) == d * HEADS + pl.program_id(0)
    return jnp.sum(jnp.where(mask, ref[...], 0.0), axis=0, keepdims=True), mask


def _lru_forward(zx, zc, cw, cb, wa, wx, ba, bx, lam, gather, gather_modes):
    ng_ = len(gather)

    def body(xa_ref, xac_ref, cw_ref, cb_ref, wa_ref, wx_ref, ba_ref, bx_ref, lam_ref, *rest):
        yl_ref = rest[ng_]
        af, uf, hf, ab, ub, hb, pad_s = rest[2 * ng_ + 1:2 * ng_ + 8]
        start, pass_on, finish = _gather2_ops(rest[:ng_], rest[ng_ + 1:2 * ng_ + 1], gather_modes,
                                              *rest[2 * ng_ + 8:], barrier=True)
        pl.when(pl.program_id(0) == 0)(start)
        pl.when(pl.program_id(0) == HEADS // 2)(pass_on)
        pl.when(pl.program_id(0) == HEADS - 1)(finish)
        cwv, cbv = cw_ref[...], cb_ref[...]
        nsp = (-LRU_C) * _softplus(-lam_ref[...])

        def forward(xa, t_len, h0f, h0b):
            xc = _conv(xa, cwv, cbv, pad_s)
            for d, (a_ref, u_ref) in enumerate(((af, uf), (ab, ub))):
                _, i, a, gamma, _ = _gates(xc, wa_ref[d, 0].astype(BF16), wx_ref[d, 0].astype(BF16),
                                           _bias_row(ba_ref, d)[0], _bias_row(bx_ref, d)[0], nsp[d:d + 1, :])
                a_ref[0:t_len, :] = a
                u_ref[0:t_len, :] = gamma * (i * xc)
            return _scan_pair(af, uf, hf, h0f, ab, ub, hb, h0b, t_len)

        z = jnp.zeros((1, HD), F32)
        h0f, h0b = forward(xac_ref[...], LC, z, z)
        forward(xa_ref[...], L, h0f, h0b)
        yl_ref[...] = hf[...] + hb[...]

    s = _lru_param_specs()
    return _call(
        body, name="lru_forward", grid=(HEADS,),
        out_shape=[jax.ShapeDtypeStruct((L, D), F32)] + _gather2_shapes(gather, gather_modes),
        in_specs=[s["xa"], s["xac"], s["cw"], s["cb"], s["h4"], s["h4"], s["b16"], s["b16"], s["v2"]] + [HBM] * ng_,
        out_specs=[pl.BlockSpec((L, HD), lambda h: (0, h))] + [HBM] * ng_,
        scratch_shapes=[pltpu.VMEM((L, HD), F32)] * 6 + [pltpu.VMEM((L + 16, HD), F32)] + _gather2_sems(ng_),
        compiler_params=pltpu.CompilerParams(dimension_semantics=("arbitrary",), vmem_limit_bytes=VMEM_LIMIT,
                                             has_side_effects=True, collective_id=5),
    )(zx, zc, cw, cb, wa, wx, ba, bx, lam, *[pltpu.with_memory_space_constraint(a, pltpu.HBM) for a in gather])


def _lru_backward(zx, zc, dyl, dz, cw, cb, wa, wx, ba, bx, lam, chip_sums, first_chips=None):
    nr = len(chip_sums)

    def body(xa_ref, xac_ref, dyl_ref, dz_in, cw_ref, cb_ref, wa_ref, wx_ref, ba_ref, bx_ref, lam_ref, *rest):
        (dxa_ref, dxac_ref, dwa_ref, dwx_ref, dba_ref, dbx_ref, dlam_ref, dcw_ref,
         dcb_ref) = rest[nr:nr + 9]
        main_s, ctx_s, pad_s = rest[3 * nr + 9:3 * nr + 12]
        if nr:
            start, forward, finish = _chips_ops(rest[:nr], rest[nr + 9:2 * nr + 9], rest[2 * nr + 9:3 * nr + 9],
                                                *rest[3 * nr + 12:], first_chips=first_chips, barrier=True)
            pl.when(pl.program_id(0) == 0)(start)
            pl.when(pl.program_id(0) == HEADS // 2)(forward)
            pl.when(pl.program_id(0) == HEADS - 1)(finish)
        del dz_in

        @pl.when(pl.program_id(0) == 0)
        def _():
            dba_ref[...] = jnp.zeros_like(dba_ref)
            dbx_ref[...] = jnp.zeros_like(dbx_ref)

        cwv, cbv = cw_ref[...], cb_ref[...]
        lamv = lam_ref[...]
        sp = _softplus(-lamv)
        nsp = (-LRU_C) * sp
        z = jnp.zeros((1, HD), F32)

        def wmat(ref, d):
            return ref[d, 0].astype(BF16)

        def workspace(s):
            return dict(a=(s.at[0], s.at[1]), u=(s.at[2], s.at[3]), h=(s.at[4], s.at[5]), rho=(s.at[6], s.at[7]),
                        saved=(tuple(s.at[8 + k] for k in range(4)), tuple(s.at[12 + k] for k in range(4))),
                        xc=s.at[16])

        def forward(ws, xa, t_len, h0f, h0b):
            xc = _conv(xa, cwv, cbv, pad_s)
            ws["xc"][...] = xc
            for d in (0, 1):
                vals = _gates(xc, wmat(wa_ref, d), wmat(wx_ref, d), _bias_row(ba_ref, d)[0],
                              _bias_row(bx_ref, d)[0], nsp[d:d + 1, :])
                r, i, a, gamma, rg = vals
                ws["a"][d][...] = a
                ws["u"][d][...] = gamma * (i * xc)
                for ref, val in zip(ws["saved"][d], (r, i, gamma, rg)):
                    ref[...] = val
            return _scan_pair(ws["a"][0], ws["u"][0], ws["h"][0], h0f, ws["a"][1], ws["u"][1], ws["h"][1], h0b,
                              t_len)

        def backward(ws, xa, t_len, h0f, h0b, dhf, dhb, first):
            xc = ws["xc"][...]
            (af, ab), (uf, ub), (hf, hb), (rf, rb) = ws["a"], ws["u"], ws["h"], ws["rho"]
            uf[...] = ab[...] * dhb
            ub[...] = af[...] * dhf
            rho_b_last, rho_f_first = _scan_pair(ab, uf, rb, z, af, ub, rf, z, t_len)
            dxc = jnp.zeros((t_len, HD), F32)
            dsp = []
            for d in (0, 1):
                r, i, gamma, rg = (ref[...] for ref in ws["saved"][d])
                a = ws["a"][d][...]
                if d == 0:
                    lam_t = dhf + _shifted(pad_s, rf[...], (1,))[0]
                    h_prev = _shifted(pad_s, hf[...], (-1,), before=h0f)[0]
                else:
                    lam_t = dhb + _shifted(pad_s, rb[...], (-1,))[0]
                    h_prev = _shifted(pad_s, hb[...], (1,), after=h0b)[0]
                da = lam_t * h_prev
                lx = lam_t * xc
                d_i = lx * gamma
                d_gamma = lx * i
                dxc = dxc + lam_t * (gamma * i)
                d_log_a = a * (da - d_gamma * (a * rg))
                dsp.append(jnp.sum(d_log_a * r, axis=0, keepdims=True) * (-LRU_C))
                d_pre_r = d_log_a * nsp[d:d + 1, :] * (r * (1.0 - r))
                d_pre_i = d_i * (i * (1.0 - i))
                prb, pib, xb = d_pre_r.astype(BF16), d_pre_i.astype(BF16), xc.astype(BF16)
                dxc = dxc + _dot_nt(prb, wmat(wa_ref, d)) + _dot_nt(pib, wmat(wx_ref, d))
                g_wa, g_wx = _dot_tn(xb, prb), _dot_tn(xb, pib)
                g_ba = jnp.sum(d_pre_r, axis=0, keepdims=True)
                g_bx = jnp.sum(d_pre_i, axis=0, keepdims=True)
                mask = _bias_row(ba_ref, d)[1]
                dba_ref[...] += jnp.where(mask, g_ba, 0.0)
                dbx_ref[...] += jnp.where(mask, g_bx, 0.0)
                if first:
                    dwa_ref[d, 0] = g_wa
                    dwx_ref[d, 0] = g_wx
                else:
                    dwa_ref[d, 0] += g_wa
                    dwx_ref[d, 0] += g_wx
            g_lam = jnp.concatenate(dsp, axis=0) * (-_sigmoid(-lamv))
            dm1, dp1, dm2 = _shifted(pad_s, dxc, (-1, 1, -2))
            dxa = dp1 * cwv[0:1, :] + dxc * cwv[1:2, :] + dm1 * cwv[2:3, :] + dm2 * cwv[3:4, :]
            xm1, xp1, xp2 = _shifted(pad_s, xa, (-1, 1, 2))
            g_cw = jnp.concatenate([jnp.sum(dxc * v, axis=0, keepdims=True) for v in (xm1, xa, xp1, xp2)], axis=0)
            g_cb = jnp.sum(dxc, axis=0, keepdims=True)
            if first:
                dlam_ref[...] = g_lam
                dcw_ref[...] = g_cw
                dcb_ref[...] = g_cb
            else:
                dlam_ref[...] += g_lam
                dcw_ref[...] += g_cw
                dcb_ref[...] += g_cb
            return dxa, rho_f_first, rho_b_last

        ws_x, ws_c = workspace(main_s), workspace(ctx_s)
        h0f, h0b = forward(ws_c, xac_ref[...], LC, z, z)
        forward(ws_x, xa_ref[...], L, h0f, h0b)
        dh = dyl_ref[...]
        dxa, dh0f, dh0b = backward(ws_x, xa_ref[...], L, h0f, h0b, dh, dh, True)
        dxa_ref[...] = dxa.astype(BF16)
        rc = _rows((LC, HD))
        dxac, _, _ = backward(ws_c, xac_ref[...], LC, z, z, jnp.where(rc == LC - 1, dh0f, 0.0),
                              jnp.where(rc == 0, dh0b, 0.0), False)
        dxac_ref[...] = dxac.astype(BF16)

    s = _lru_param_specs()
    col = lambda r: pl.BlockSpec((r, HD), lambda h: (0, h))
    return _call(
        body, name="lru_backward", grid=(HEADS,),
        out_shape=[jax.ShapeDtypeStruct((L, D_IN), BF16), jax.ShapeDtypeStruct((LC, D), BF16),
                   jax.ShapeDtypeStruct((2, HEADS, HD, HD), F32), jax.ShapeDtypeStruct((2, HEADS, HD, HD), F32),
                   jax.ShapeDtypeStruct((2 * HEADS, HD), F32), jax.ShapeDtypeStruct((2 * HEADS, HD), F32),
                   jax.ShapeDtypeStruct((2, D), F32), jax.ShapeDtypeStruct((CONV_W, D), F32),
                   jax.ShapeDtypeStruct((1, D), F32)] + [jax.ShapeDtypeStruct((4,) + a.shape[1:], a.dtype)
                                                          for a in chip_sums] + _chips_stage_shapes(chip_sums),
        in_specs=[s["xa"], s["xac"], col(L), pl.BlockSpec(memory_space=pl.ANY), s["cw"], s["cb"], s["h4"], s["h4"],
                  s["b16"], s["b16"], s["v2"]] + [HBM] * nr,
        out_specs=[col(L), col(LC), s["h4"], s["h4"], s["b16"], s["b16"], s["v2"], col(CONV_W), col(1)]
        + [HBM] * (2 * nr),
        scratch_shapes=[pltpu.VMEM((17, L, HD), F32), pltpu.VMEM((17, LC, HD), F32), pltpu.VMEM((L + 16, HD), F32)]
        + (_chips_sems(nr) if nr else []),
        input_output_aliases={3: 0},
        compiler_params=pltpu.CompilerParams(dimension_semantics=("arbitrary",), vmem_limit_bytes=VMEM_LIMIT,
                                             has_side_effects=True, collective_id=6 if nr else None),
    )(zx, zc, dyl, dz, cw, cb, wa, wx, ba, bx, lam, *[pltpu.with_memory_space_constraint(a, pltpu.HBM)
                                                       for a in chip_sums])


def _mixer_loss(x, tgt, zx, yl, gx, fg, lng, lnb, ws, bst, wout, tm):
    ncht = tm // CHUNK

    def body(x_ref, t_ref, ga_ref, u_ref, v_ref, gb_ref, yl_ref, gx_ref, fg_ref, lng_ref, lnb_ref, ws_ref,
             bst_ref, wout_ref,
             dz_ref, dyl_ref, dxn_ref, y_s, do_ref, dws_ref, dbst_ref, vec_ref,
             vn_s, mix_s, dm_s, dvn_s):
        step = pl.program_id(0)

        @pl.when(step == 0)
        def _():
            dws_ref[...] = jnp.zeros_like(dws_ref)
            dbst_ref[...] = jnp.zeros_like(dbst_ref)
            vec_ref[...] = jnp.zeros_like(vec_ref)

        u, v = u_ref[...], v_ref[...]
        ug, dug_du = _gelu_and_grad(u)
        vg, dvg_dv = _gelu_and_grad(v)
        mu = jnp.mean(vg, axis=-1, keepdims=True)
        vc = vg - mu
        rstd = lax.rsqrt(jnp.mean(vc * vc, axis=-1, keepdims=True) + LN_EPS)
        vhat = vc * rstd
        lngv = lng_ref[...]
        vn_s[...] = (vhat * lngv + lnb_ref[...]).astype(BF16)
        for ch in range(ncht):
            rs = slice(ch * CHUNK, (ch + 1) * CHUNK)
            for g in range(HEADS):
                cs = slice(g * HD, (g + 1) * HD)
                mix_s[rs, cs] = _dot(ws_ref[g].astype(BF16), vn_s[rs, cs]) + bst_ref[:, g:g + 1]
        mixed = mix_s[...]
        ga, gb, yl = ga_ref[...], gb_ref[...], yl_ref[...]
        sga, dsga = _silu_and_grad(ga)
        sgb, dsgb = _silu_and_grad(gb)
        ys = ug * mixed
        y_s[:, 0:D] = (yl * sga).astype(BF16)
        y_s[:, D:D_MIX] = (ys * sgb).astype(BF16)
        o = _dot(y_s[...], wout_ref[...])
        gxv, fgv = gx_ref[...], fg_ref[...]
        xn = x_ref[...] + gxv * o
        rs2 = lax.rsqrt(jnp.mean(xn * xn, axis=-1, keepdims=True) + NORM_EPS)
        xh = xn * rs2
        diff = xh * fgv - t_ref[...]
        vec_ref[R_LOSS:R_LOSS + 1, :] += jnp.full((1, D), jnp.sum(diff * diff) * (0.5 / D), F32)
        dout = diff * (1.0 / D)
        w = dout * fgv
        dxn = rs2 * (w - xh * jnp.mean(w * xh, axis=-1, keepdims=True))
        dxn_ref[...] = dxn
        vec_ref[0:1, :] += jnp.sum(dxn * o, axis=0, keepdims=True)
        vec_ref[1:2, :] += jnp.sum(dout * xh, axis=0, keepdims=True)
        dob = (dxn * gxv).astype(BF16)
        do_ref[...] = dob
        dy = _dot_nt(dob, wout_ref[...])
        dya, dyb = dy[:, 0:D], dy[:, D:D_MIX]
        dyl_ref[...] = dya * sga
        dys = dyb * sgb
        dz_ref[:, 0:D] = jnp.zeros((tm, D), BF16)
        dz_ref[:, D:2 * D] = (dya * yl * dsga).astype(BF16)
        dz_ref[:, 2 * D:3 * D] = (dys * mixed * dug_du).astype(BF16)
        dz_ref[:, 4 * D:5 * D] = (dyb * ys * dsgb).astype(BF16)
        dm = dys * ug
        dm_s[...] = dm.astype(BF16)
        for g in range(HEADS):
            cs = slice(g * HD, (g + 1) * HD)
            dbst_ref[:, g:g + 1] += sum(jnp.sum(dm[ch * CHUNK:(ch + 1) * CHUNK, cs], axis=1, keepdims=True)
                                        for ch in range(ncht))
            for ch in range(ncht):
                rs = slice(ch * CHUNK, (ch + 1) * CHUNK)
                dws_ref[g] += _dot_nt(dm_s[rs, cs], vn_s[rs, cs])
                dvn_s[rs, cs] = _dot_tn(ws_ref[g].astype(BF16), dm_s[rs, cs])
        dvn = dvn_s[...]
        vec_ref[2:3, :] += jnp.sum(dvn * vhat, axis=0, keepdims=True)
        vec_ref[3:4, :] += jnp.sum(dvn, axis=0, keepdims=True)
        dvh = dvn * lngv
        dvg = rstd * (dvh - jnp.mean(dvh, axis=-1, keepdims=True) - vhat * jnp.mean(dvh * vhat, axis=-1, keepdims=True))
        dz_ref[:, 3 * D:4 * D] = (dvg * dvg_dv).astype(BF16)

    tile = pl.BlockSpec((tm, D), lambda i: (i, 0))
    zcol = lambda n: pl.BlockSpec((tm, D), lambda i: (i, n))
    vec = pl.BlockSpec((1, D), lambda i: (0, 0))
    full = lambda *s: pl.BlockSpec(s, lambda i: (0,) * len(s))
    return _call(
        body, name="mixer_loss", grid=(L // tm,),
        out_shape=[jax.ShapeDtypeStruct((L, D_IN), BF16), jax.ShapeDtypeStruct((L, D), F32),
                   jax.ShapeDtypeStruct((L, D), F32), jax.ShapeDtypeStruct((L, D_MIX), BF16),
                   jax.ShapeDtypeStruct((L, D), BF16),
                   jax.ShapeDtypeStruct((HEADS, CHUNK, CHUNK), F32), jax.ShapeDtypeStruct((CHUNK, HEADS), F32),
                   jax.ShapeDtypeStruct((8, D), F32)],
        in_specs=[tile, tile, zcol(1), zcol(2), zcol(3), zcol(4), tile, pl.BlockSpec((1, D), lambda i: (0, 2)),
                  vec, vec, vec,
                  full(HEADS, CHUNK, CHUNK), full(CHUNK, HEADS),
                  pl.BlockSpec((D_MIX, D), lambda i: (0, 0), pipeline_mode=pl.Buffered(1))],
        out_specs=[pl.BlockSpec((tm, D_IN), lambda i: (i, 0)), tile, tile,
                   pl.BlockSpec((tm, D_MIX), lambda i: (i, 0)), tile,
                   full(HEADS, CHUNK, CHUNK), full(CHUNK, HEADS), full(8, D)],
        scratch_shapes=[pltpu.VMEM((tm, D), BF16), pltpu.VMEM((tm, D), F32),
                        pltpu.VMEM((tm, D), BF16), pltpu.VMEM((tm, D), F32)],
        compiler_params=_params("arbitrary"),
    )(x, tgt, zx, zx, zx, zx, yl, gx, fg, lng, lnb, ws, bst, wout)


def _grad_w(a, b, a2, b2, tk, name, bw, first, nblocks, split, barrier_id, riders=()):
    nk = a.shape[0] // tk
    m = a.shape[1]
    with_ctx = a2 is not None
    if split == "cols":
        slots, r, w = nblocks, m, bw // 2
        piece = lambda q, pc: (slice(None), slice(pc * w, (pc + 1) * w))
    else:
        slots, r, w = 4, m // 8, bw
        piece = lambda q, pc: (slice((2 * q + pc) * r, (2 * q + pc + 1) * r), slice(None))

    nr = len(riders)

    def body(*refs):
        a_ref, b_ref = refs[:2]
        a2_ref, b2_ref = refs[2:4] if with_ctx else (None, None)
        base = 4 if with_ctx else 2
        sums_ref = refs[base + nr]
        s0 = base + 3 * nr + 1
        acc, mine_v, send_v, stage_v, send_sems, recv_sems = refs[s0:s0 + 6]
        n, k = pl.program_id(0), pl.program_id(1)
        x, y, c = lax.axis_index("x"), lax.axis_index("y"), lax.axis_index("c")

        def to_sibling(s):
            return pltpu.make_async_remote_copy(src_ref=send_v.at[s], dst_ref=stage_v.at[s], send_sem=send_sems.at[s],
                                                recv_sem=recv_sems.at[s], device_id=(x, y, 1 - c),
                                                device_id_type=MESH)

        if nr:
            rider_in = refs[base:base + nr]
            own_v, got_v, psum_v = (refs[s0 + 9 + i * nr:s0 + 9 + (i + 1) * nr] for i in range(3))
            p_send, p_recv, p_local = refs[s0 + 9 + 3 * nr:s0 + 12 + 3 * nr]
            c_start, c_forward, c_finish = _chips_ops(psum_v, refs[base + nr + 1:base + 2 * nr + 1],
                                                      refs[base + 2 * nr + 1:base + 3 * nr + 1], *refs[s0 + 6:s0 + 9])

            @pl.when(jnp.logical_and(n == 0, k == 0))
            def _():
                _barrier([(x, y, 1 - c)] + [(x ^ (j >> 1), y ^ (j & 1), c) for j in (1, 2)])
                copies = []
                for j in range(nr):
                    for q in range(4):
                        copies.append(pltpu.make_async_remote_copy(
                            src_ref=rider_in[j].at[2 * q + 1 - c], dst_ref=got_v[j].at[q], send_sem=p_send.at[j, q],
                            recv_sem=p_recv.at[j, q], device_id=(x, y, 1 - c), device_id_type=MESH))
                        copies.append(pltpu.make_async_copy(rider_in[j].at[2 * q + c], own_v[j].at[q],
                                                            p_local.at[j, q]))
                for cp in copies:
                    cp.start()
                for cp in copies:
                    cp.wait()
                for j in range(nr):
                    psum_v[j][...] = (own_v[j][...] + got_v[j][...]).astype(BF16)
                c_start()
        else:
            pl.when(jnp.logical_and(n == 0, k == 0))(_sibling_barrier)

        @pl.when(k == 0)
        def _():
            acc[...] = _dot_tn(a_ref[...], b_ref[...])

        if nk > 1:
            @pl.when(k > 0)
            def _():
                acc[...] += _dot_tn(a_ref[...], b_ref[...])

        if with_ctx:
            @pl.when(jnp.logical_and(k == nk - 1, n == 0))
            def _():
                acc[:, 0:b2_ref.shape[1]] += _dot_tn(a2_ref[...], b2_ref[...])

        if nr:
            pl.when(jnp.logical_and(k == nk - 1, n == nblocks - 1))(c_forward)

        def hand_over(s, q):
            for pc in (0, 1):
                @pl.when(c == pc)
                def _(pc=pc):
                    mine_v[s] = acc[piece(q, pc)]
                    send_v[s] = acc[piece(q, 1 - pc)].astype(BF16)
            to_sibling(s).start()

        for i in range(nblocks):
            @pl.when(jnp.logical_and(k == nk - 1, n == i))
            def _(i=i):
                if split == "cols":
                    hand_over(i, 0)
                else:
                    for q in range(4):
                        hand_over(q, q)

        @pl.when(jnp.logical_and(k == nk - 1, n == nblocks - 1))
        def _():
            for s in range(slots):
                to_sibling(s).wait_recv()
                sums_ref[s] = (mine_v[s] + stage_v[s].astype(F32)).astype(BF16)
            for s in range(slots):
                to_sibling(s).wait_send()
            if nr:
                c_finish()

    in_specs = [pl.BlockSpec((tk, m), lambda n, k: (k, 0)), pl.BlockSpec((tk, bw), lambda n, k: (k, n + first))]
    args = [a, b]
    if with_ctx:
        in_specs += [pl.BlockSpec(a2.shape, lambda n, k: (0, 0)), pl.BlockSpec(b2.shape, lambda n, k: (0, 0))]
        args += [a2, b2]
    in_specs += [HBM] * nr
    args += [pltpu.with_memory_space_constraint(s, pltpu.HBM) for s in riders]
    rider_sums = [jax.ShapeDtypeStruct((4,) + s.shape[1:], BF16) for s in riders]
    rider_scratch = []
    if nr:
        rider_scratch = (_chips_sems(nr) + [pltpu.VMEM(s.shape, F32) for s in rider_sums] * 2
                         + [pltpu.VMEM(s.shape, BF16) for s in rider_sums]
                         + [pltpu.SemaphoreType.DMA((nr, 4))] * 3)
    return _call(
        body, name=name, grid=(nblocks, nk),
        out_shape=[jax.ShapeDtypeStruct((slots, r, w), BF16)] + rider_sums + _chips_stage_shapes(rider_sums),
        in_specs=in_specs, out_specs=[pl.BlockSpec((slots, r, w), lambda n, k: (0, 0, 0))] + [HBM] * (2 * nr),
        scratch_shapes=[pltpu.VMEM((m, bw), F32), pltpu.VMEM((slots, r, w), F32), pltpu.VMEM((slots, r, w), BF16),
                        pltpu.VMEM((slots, r, w), BF16), pltpu.SemaphoreType.DMA((slots,)),
                        pltpu.SemaphoreType.DMA((slots,))] + rider_scratch,
        compiler_params=pltpu.CompilerParams(dimension_semantics=("arbitrary", "arbitrary"),
                                             vmem_limit_bytes=VMEM_LIMIT, has_side_effects=True,
                                             collective_id=barrier_id),
    )(*args)


def _grad_rows(xr, dz, w, mod, ng, dres, ncols, tm, name, chip_sums=(), first_chips=None, dests=None):
    rows = xr.shape[0]
    steps = rows // tm
    with_dx = dres is not None
    nr = len(chip_sums)
    dests = [d for d in (dests or [None] * nr)]
    nd = sum(d is not None for d in dests)
    nin = 6 if with_dx else 5
    nout = 2 if with_dx else 1

    def body(*refs):
        if with_dx:
            x_ref, dz_ref, w_ref, sc_ref, ng_ref, dres_ref = refs[:nin]
            dx_ref, vec_ref = refs[nin + nr + nd:nin + nr + nd + nout]
        else:
            x_ref, dz_ref, w_ref, sc_ref, ng_ref = refs[:nin]
            (vec_ref,) = refs[nin + nr + nd:nin + nr + nd + nout]
        if nr:
            o0 = nin + nr + nd + nout
            start, forward, finish = _chips_ops(refs[nin:nin + nr], refs[o0:o0 + nr], refs[o0 + nr:o0 + 2 * nr],
                                                *refs[o0 + 2 * nr:], first_chips=first_chips, barrier=True)
            pl.when(pl.program_id(0) == 0)(start)
            pl.when(pl.program_id(0) == steps // 2)(forward)
            pl.when(pl.program_id(0) == steps - 1)(finish)

        @pl.when(pl.program_id(0) == 0)
        def _():
            vec_ref[...] = jnp.zeros_like(vec_ref)

        dhn = _dot_nt(dz_ref[...], w_ref[...])
        x = x_ref[...]
        rs = lax.rsqrt(jnp.mean(x * x, axis=-1, keepdims=True) + NORM_EPS)
        xh = x * rs
        ngv = ng_ref[...]
        y = xh * ngv
        vec_ref[0:1, :] += jnp.sum(dhn, axis=0, keepdims=True)
        vec_ref[1:2, :] += jnp.sum(dhn * y, axis=0, keepdims=True)
        dy = dhn * (1.0 + sc_ref[...])
        vec_ref[2:3, :] += jnp.sum(dy * xh, axis=0, keepdims=True)
        if with_dx:
            dxh = dy * ngv
            dx_ref[...] = dres_ref[...] + rs * (dxh - xh * jnp.mean(dxh * xh, axis=-1, keepdims=True))

    tile = pl.BlockSpec((tm, D), lambda i: (i, 0))
    vec = pl.BlockSpec((1, D), lambda i: (0, 0))
    in_specs = [tile, pl.BlockSpec((tm, ncols), lambda i: (i, 0)),
                pl.BlockSpec((D, ncols), lambda i: (0, 0), pipeline_mode=pl.Buffered(1)),
                pl.BlockSpec((1, D), lambda i: (0, 1)), vec]
    out_shape = [jax.ShapeDtypeStruct((8, D), F32)]
    out_specs = [pl.BlockSpec((8, D), lambda i: (0, 0))]
    args = [xr, dz, w, mod, ng]
    if with_dx:
        in_specs.append(tile)
        out_shape.insert(0, jax.ShapeDtypeStruct((rows, D), F32))
        out_specs.insert(0, tile)
        args.append(dres)
    aliases = {}
    for j, d in enumerate(dests):
        if d is not None:
            aliases[len(args) + nr + len(aliases)] = len(out_shape) + j
    in_specs += [HBM] * (nr + nd)
    out_specs += [HBM] * (2 * nr)
    out_shape += [jax.ShapeDtypeStruct((4,) + a.shape[1:], a.dtype) for a in chip_sums]
    out_shape += _chips_stage_shapes(chip_sums)
    args += [pltpu.with_memory_space_constraint(a, pltpu.HBM) for a in chip_sums]
    args += [pltpu.with_memory_space_constraint(d, pltpu.HBM) for d in dests if d is not None]
    return _call(body, name=name, grid=(steps,), out_shape=out_shape, in_specs=in_specs, out_specs=out_specs,
                 scratch_shapes=_chips_sems(nr) if nr else [], input_output_aliases=aliases,
                 compiler_params=pltpu.CompilerParams(dimension_semantics=("arbitrary",),
                                                      vmem_limit_bytes=VMEM_LIMIT, has_side_effects=bool(nr),
                                                      collective_id=7 if nr else None))(*args)


def _adamw(w, g, m, v):
    m = ADAM_B1 * m + (1.0 - ADAM_B1) * g
    v = ADAM_B2 * v + (1.0 - ADAM_B2) * (g * g)
    m_hat = m / (1.0 - ADAM_B1 ** ADAM_STEP)
    v_hat = v / (1.0 - ADAM_B2 ** ADAM_STEP)
    delta = -ADAM_LR * (m_hat / (jnp.sqrt(v_hat) + ADAM_EPS) + ADAM_WD * w)
    return delta, m, v


def _adamw_reduced(parts, w, m, v, tr, name):
    r, n = w.shape
    nparts = parts.shape[0]

    def body(p_ref, w_ref, m_ref, v_ref, g_ref, d_ref, mo_ref, vo_ref):
        g = p_ref[0].astype(F32)
        for i in range(1, nparts):
            g = g + p_ref[i].astype(F32)
        g_ref[...] = g
        d_ref[...], mo_ref[...], vo_ref[...] = _adamw(w_ref[...], g, m_ref[...], v_ref[...])

    tile = pl.BlockSpec((tr, n), lambda i: (i, 0))
    sds = jax.ShapeDtypeStruct((r, n), F32)
    return _call(
        body, name=name, grid=(r // tr,), out_shape=[sds] * 4,
        in_specs=[pl.BlockSpec((nparts, tr, n), lambda i: (0, i, 0)), tile, tile, tile], out_specs=[tile] * 4,
        compiler_params=_params("arbitrary"),
    )(parts, w, m, v)


R_GATE, R_FINAL_G, R_LN_G, R_LN_B, R_LOSS = 0, 1, 2, 3, 4
R_SH_X, R_SC_X, R_NG_X = 5, 6, 7
R_SH_C, R_SC_C, R_NG_C = 8, 9, 10
R_LAM, R_CW, R_CB = 11, 13, 17
PACK_ROWS = 24
Q_BA, Q_BX, Q_SGU_B, PACK128_ROWS = 0, 16, 32, 40


def _reduce_small(vec_pieces, q_pieces, mat_parts, ada_w, me):
    nloc = ada_w.shape[1]
    nm = len(mat_parts)
    pieces = list(vec_pieces) + list(q_pieces)

    def body(me_ref, *refs):
        piece_refs, refs = refs[:len(pieces)], refs[len(pieces):]
        mp_refs, w_ref = refs[:nm], refs[nm]
        red_ref, redq_ref = refs[nm + 1:nm + 3]
        mats_all = refs[nm + 3:2 * nm + 3]
        cparts_ref, dmod_ref, gab_ref, loss_ref = refs[2 * nm + 3:2 * nm + 7]
        pack_ref, packq_ref, vp_ref, vq_ref = refs[2 * nm + 7:2 * nm + 11]
        mat_refs = refs[2 * nm + 11:3 * nm + 11]
        cpart_ref, dmc_s = refs[3 * nm + 11:3 * nm + 13]
        sems = refs[3 * nm + 13:]
        for dst, group in ((pack_ref, vec_pieces), (packq_ref, q_pieces)):
            row = 0
            for _, nrows in group:
                dst[row:row + nrows, :] = piece_refs[0][0:nrows, :]
                piece_refs, row = piece_refs[1:], row + nrows
            if row < dst.shape[0]:
                dst[row:, :] = jnp.zeros((dst.shape[0] - row, dst.shape[1]), F32)
        p_start, p_forward, p_finish = _gather2_ops([pack_ref, packq_ref], [vp_ref, vq_ref], ["ag", "ag"], *sems[:3],
                                                    barrier=True)
        m_start, m_forward, m_finish = _gather2_ops(mat_refs, mats_all, ["ag"] * nm, *sems[3:6])
        c_start, c_forward, c_finish = _gather2_ops([cpart_ref], [cparts_ref], ["ag"], *sems[6:])
        p_start()
        for mp_ref, mat_ref in zip(mp_refs, mat_refs):
            mat = mp_ref[0].astype(F32)
            for i in range(1, mp_ref.shape[0]):
                mat = mat + mp_ref[i].astype(F32)
            mat_ref[...] = mat
        m_start()
        p_forward()
        p_finish()
        m_forward()
        red, redq = vp_ref[0], vq_ref[0]
        for i in range(1, N_DEV):
            red = red + vp_ref[i]
            redq = redq + vq_ref[i]
        red_ref[...] = red
        redq_ref[...] = redq
        loss_ref[...] = red_ref[R_LOSS:R_LOSS + 1, 0:1]
        for e in range(N_DEV):
            dmod_ref[e:e + 1, 0:D] = vp_ref[e, R_SH_X:R_SH_X + 1, :]
            dmod_ref[e:e + 1, D:2 * D] = vp_ref[e, R_SC_X:R_SC_X + 1, :]
            dmod_ref[e:e + 1, 2 * D:3 * D] = vp_ref[e, R_GATE:R_GATE + 1, :]
        dmod_ref[8:9, 0:D] = red[R_SH_C:R_SH_C + 1, :]
        dmod_ref[8:9, D:2 * D] = red[R_SC_C:R_SC_C + 1, :]
        dmod_ref[8:9, 2 * D:3 * D] = jnp.zeros((1, D), F32)
        dmod_ref[9:16, :] = jnp.zeros((7, 3 * D), F32)
        gab_ref[:, 0:D] = red[R_SH_X:R_SH_X + 1, :] + red[R_SH_C:R_SH_C + 1, :]
        gab_ref[:, D:2 * D] = red[R_SC_X:R_SC_X + 1, :] + red[R_SC_C:R_SC_C + 1, :]
        gab_ref[:, 2 * D:3 * D] = red[R_GATE:R_GATE + 1, :]
        dmc_s[...] = jnp.broadcast_to(dmod_ref[8:9, :], (8, 3 * D))
        off = pl.multiple_of(me_ref[0] * nloc, 128)
        cpart_ref[...] = _dot_nt(dmc_s[:, pl.ds(off, nloc)], w_ref[...])
        c_start()
        c_forward()
        c_finish()
        m_finish()

    return _call(
        body, name="reduce_small",
        out_shape=[jax.ShapeDtypeStruct((PACK_ROWS, D), F32), jax.ShapeDtypeStruct((PACK128_ROWS, HD), F32)]
        + [jax.ShapeDtypeStruct((N_DEV,) + p.shape[1:], F32) for p in mat_parts]
        + [jax.ShapeDtypeStruct((N_DEV, 8, D), F32), jax.ShapeDtypeStruct((16, 3 * D), F32),
           jax.ShapeDtypeStruct((1, 3 * D), F32), jax.ShapeDtypeStruct((1, 1), F32)],
        in_specs=[pl.BlockSpec(memory_space=pltpu.SMEM)] + [VMEM] * (len(pieces) + nm + 1),
        out_specs=[VMEM] * (nm + 6),
        scratch_shapes=[pltpu.VMEM((PACK_ROWS, D), F32), pltpu.VMEM((PACK128_ROWS, HD), F32),
                        pltpu.VMEM((N_DEV, PACK_ROWS, D), F32), pltpu.VMEM((N_DEV, PACK128_ROWS, HD), F32)]
        + [pltpu.VMEM(p.shape[1:], F32) for p in mat_parts]
        + [pltpu.VMEM((8, D), F32), pltpu.VMEM((8, 3 * D), F32)] + _gather2_sems(2) + _gather2_sems(nm)
        + _gather2_sems(1),
        compiler_params=pltpu.CompilerParams(vmem_limit_bytes=VMEM_LIMIT, has_side_effects=True, collective_id=8),
    )(me, *[a for a, _ in pieces], *mat_parts, ada_w)


def _adamw_ada(c_all, c_ctx, dmod, w, m, v, me):
    nloc = w.shape[1]

    def body(me_ref, c_ref, cc_ref, dm_ref, w_ref, m_ref, v_ref, g_ref, d_ref, mo_ref, vo_ref):
        off = pl.multiple_of(me_ref[0] * nloc, 128)
        dm = dm_ref[:, pl.ds(off, nloc)]
        sx, _ = _silu_and_grad(c_ref[...])
        sc, _ = _silu_and_grad(cc_ref[...])
        g = _dot_tn(sx, dm[0:8, :]) + _dot_tn(jnp.broadcast_to(sc, (8, D)), dm[8:16, :])
        g_ref[...] = g
        d_ref[...], mo_ref[...], vo_ref[...] = _adamw(w_ref[...], g, m_ref[...], v_ref[...])

    sds = jax.ShapeDtypeStruct(w.shape, F32)
    return _call(
        body, name="adamw_ada_w", out_shape=[sds] * 4,
        in_specs=[pl.BlockSpec(memory_space=pltpu.SMEM)] + [VMEM] * 6, out_specs=[VMEM] * 4,
        compiler_params=_params(),
    )(me, c_all, c_ctx, dmod, w, m, v)


_SMALL = ("c_ctx", "ada_b", "norm_g", "conv_w", "conv_b", "lru_wa", "lru_ba", "lru_wx", "lru_bx", "lru_lambda",
          "sgu_ln_g", "sgu_ln_b", "sgu_w", "sgu_b", "final_g")


def _adamw_small(red, redq, mats, cparts, gab, ws, ms, vs, me):
    n = len(_SMALL)

    def body(me_ref, red_ref, redq_ref, wa_ref, wx_ref, sw_ref, cp_ref, gab_ref, *refs):
        w_refs, m_refs, v_refs = refs[:n], refs[n:2 * n], refs[2 * n:3 * n]
        outs = refs[3 * n:]
        off = pl.multiple_of(me_ref[0] * HD, 128)

        def row(r, k=1):
            return red_ref[r:r + k, :]

        cc = w_refs[0][...]
        dcc = cp_ref[0, 0:1, :]
        for i in range(1, N_DEV):
            dcc = dcc + cp_ref[i, 0:1, :]
        grads = dict(
            c_ctx=dcc * _silu_and_grad(cc)[1], ada_b=gab_ref[...], norm_g=row(R_NG_X) + row(R_NG_C),
            conv_w=red_ref[R_CW:R_CW + CONV_W, pl.ds(off, HD)], conv_b=row(R_CB),
            lru_wa=wa_ref[...], lru_ba=redq_ref[Q_BA:Q_BA + 2 * HEADS, :], lru_wx=wx_ref[...],
            lru_bx=redq_ref[Q_BX:Q_BX + 2 * HEADS, :], lru_lambda=red_ref[R_LAM:R_LAM + 2, pl.ds(off, HD)],
            sgu_ln_g=row(R_LN_G), sgu_ln_b=row(R_LN_B), sgu_w=sw_ref[...],
            sgu_b=redq_ref[Q_SGU_B:Q_SGU_B + HEADS, :], final_g=row(R_FINAL_G))
        for j, name in enumerate(_SMALL):
            g = grads[name]
            outs[j][...] = g
            outs[n + j][...], outs[2 * n + j][...], outs[3 * n + j][...] = _adamw(w_refs[j][...], g, m_refs[j][...],
                                                                                 v_refs[j][...])

    sds = [jax.ShapeDtypeStruct(ws[k].shape, F32) for k in _SMALL]
    outs = _call(
        body, name="adamw_small", out_shape=sds * 4,
        in_specs=[pl.BlockSpec(memory_space=pltpu.SMEM)] + [VMEM] * (7 + 3 * n), out_specs=[VMEM] * (4 * n),
        compiler_params=_params(),
    )(me, red, redq, *mats, cparts, gab, *[ws[k] for k in _SMALL], *[ms[k] for k in _SMALL],
      *[vs[k] for k in _SMALL])
    return [dict(zip(_SMALL, outs[i * n:(i + 1) * n])) for i in range(4)]


def kernel(x, c, ctx, c_ctx, ada_w, ada_b, norm_g, w_in, conv_w, conv_b, lru_wa, lru_ba, lru_wx, lru_bx, lru_lambda, sgu_ln_g, sgu_ln_b, sgu_w, sgu_b, w_out, final_g, loss_target, m_c_ctx, m_ada_w, m_ada_b, m_norm_g, m_w_in, m_conv_w, m_conv_b, m_lru_wa, m_lru_ba, m_lru_wx, m_lru_bx, m_lru_lambda, m_sgu_ln_g, m_sgu_ln_b, m_sgu_w, m_sgu_b, m_w_out, m_final_g, v_c_ctx, v_ada_w, v_ada_b, v_norm_g, v_w_in, v_conv_w, v_conv_b, v_lru_wa, v_lru_ba, v_lru_wx, v_lru_bx, v_lru_lambda, v_sgu_ln_g, v_sgu_ln_b, v_sgu_w, v_sgu_b, v_w_out, v_final_g):
    args = dict(locals())
    me = (4 * lax.axis_index("x") + 2 * lax.axis_index("y") + lax.axis_index("c")).astype(jnp.int32).reshape(1)
    xr, ctxr, tgt = x[0], ctx[0], loss_target[0]
    cc = c_ctx.reshape(1, D)
    nw = 2 * HEADS * HD
    view = dict(c_ctx=(1, D), ada_b=(1, 3 * D), norm_g=(1, D), conv_w=(CONV_W, HD), conv_b=(1, D), lru_wa=(nw, HD),
                lru_ba=(2 * HEADS, HD), lru_wx=(nw, HD), lru_bx=(2 * HEADS, HD), lru_lambda=(2, HD), sgu_ln_g=(1, D),
                sgu_ln_b=(1, D), sgu_w=(HEADS * CHUNK, CHUNK), sgu_b=(HEADS, CHUNK), final_g=(1, D))

    zx, hn, w_full, w_out_b, modx, modc, c_all, cw_full, lam_full = _front_project(
        xr, c, cc, ada_w[0], ada_b, norm_g, w_in[0], w_out[0], conv_w[0], lru_lambda[0], me)
    zc, hnc = _project(ctxr, modc, norm_g, w_full, D, LC, "project_ctx")
    ba, bx = lru_ba.reshape(view["lru_ba"]), lru_bx.reshape(view["lru_bx"])
    yl, wout_all = _lru_forward(zx, zc, cw_full, conv_b, lru_wa[0], lru_wx[0], ba, bx, lam_full, [w_out_b], ["ag"])
    wout_full = wout_all.reshape(D_MIX, D)
    dz, dyl, dxn, ycat, dob, dws, dbst, mvec = _mixer_loss(
        xr, tgt, zx, yl, modx, final_g.reshape(1, D), sgu_ln_g, sgu_ln_b, sgu_w[0], sgu_b[0].T, wout_full, ROWS)

    (wout_sums,) = _grad_w(ycat, dob, None, None, L, "grad_w_out", D, 0, 1, "rows", 1)
    (rest_sums,) = _grad_w(hn, dz, None, None, L, "grad_w_in_rest", 2 * W_IN_SHARD, 1, 3, "cols", 2)
    dz, dxac, dwa, dwx, dba, dbx, dlam, dcw, dcb, win_parts, wout_parts, _, _ = _lru_backward(
        zx, zc, dyl, dz, cw_full, conv_b, lru_wa[0], lru_wx[0], ba, bx, lam_full, [rest_sums, wout_sums],
        first_chips=[1, 0])
    mats = [dwa.reshape(N_DEV, nw // N_DEV, HD), dwx.reshape(N_DEV, nw // N_DEV, HD), dws]
    first_sums, *mat_parts = _grad_w(hn, dz, hnc, dxac, L, "grad_w_in_first", 2 * W_IN_SHARD, 0, 1, "cols", 3,
                                     riders=mats)[:4]
    gx, xvec, win_parts = _grad_rows(
        xr, dz, w_full, modx, norm_g, dxn, D_IN, ROWS, "grad_rows_x", chip_sums=[first_sums], first_chips=[0],
        dests=[win_parts])[:3]
    (cvec,) = _grad_rows(ctxr, dxac, w_full, modc, norm_g, None, D, LC, "grad_rows_ctx")
    red, redq, *rest = _reduce_small(
        [(mvec, 5), (xvec, 3), (cvec, 3), (dlam, 2), (dcw, CONV_W), (dcb, 1)],
        [(dba, 2 * HEADS), (dbx, 2 * HEADS), (dbst.T, HEADS)], mat_parts, ada_w[0], me)
    mats_all, (cparts, dmod, gab, loss) = rest[:3], rest[3:]

    g_w_in, d_w_in, nm_w_in, nv_w_in = _adamw_reduced(win_parts, w_in[0], m_w_in[0], v_w_in[0], ROWS, "adamw_w_in")
    g_w_out, d_w_out, nm_w_out, nv_w_out = _adamw_reduced(wout_parts, w_out[0], m_w_out[0], v_w_out[0], ROWS // 2,
                                                          "adamw_w_out")
    g_ada, d_ada, nm_ada, nv_ada = _adamw_ada(c_all, cc, dmod, ada_w[0], m_ada_w[0], v_ada_w[0], me)
    ws = {k: args[k].reshape(view[k]) for k in _SMALL}
    ms = {k: args["m_" + k].reshape(view[k]) for k in _SMALL}
    vs = {k: args["v_" + k].reshape(view[k]) for k in _SMALL}
    small = _adamw_small(red, redq, [m.reshape(-1, HD) for m in mats_all], cparts, gab, ws, ms, vs, me)
    big = dict(w_in=(g_w_in, d_w_in, nm_w_in, nv_w_in), w_out=(g_w_out, d_w_out, nm_w_out, nv_w_out),
               ada_w=(g_ada, d_ada, nm_ada, nv_ada))

    loss = loss.reshape(())
    names = ("c_ctx", "ada_w", "ada_b", "norm_g", "w_in", "conv_w", "conv_b", "lru_wa", "lru_ba", "lru_wx", "lru_bx",
             "lru_lambda", "sgu_ln_g", "sgu_ln_b", "sgu_w", "sgu_b", "w_out", "final_g")
    outs = [loss, gx.reshape(x.shape)]
    for kind in range(4):
        for k in names:
            val = big[k][kind] if k in big else small[kind][k]
            outs.append(val.reshape(args[k].shape))
    return tuple(outs)
```

```python
import jax
import jax.numpy as jnp
from jax import lax
from jax.experimental import pallas as pl
from jax.experimental.pallas import tpu as pltpu

F32 = jnp.float32
BF16 = jnp.bfloat16

N_DEV = 8
D = 1024
L = 2048
LC = 256
HEADS = 8
HD = 128
CHUNK = 128
D_IN = 5 * D
W_IN_SHARD = D_IN // N_DEV
ROWS = 256
D_MIX = 2 * D
CONV_W = 4
LRU_C = 8.0
NORM_EPS = 1e-6
LN_EPS = 1e-5
ADAM_LR, ADAM_B1, ADAM_B2, ADAM_EPS, ADAM_WD, ADAM_STEP = 0.001, 0.9, 0.999, 1e-08, 0.01, 10

VMEM_LIMIT = 56 * 1024 * 1024

HBM = pl.BlockSpec(memory_space=pltpu.HBM)
VMEM = pl.BlockSpec(memory_space=pltpu.VMEM)
MESH = pl.DeviceIdType.MESH


def _call(body, **kw):
    return pl.pallas_call(body, **kw)


def _params(*sem):
    return pltpu.CompilerParams(dimension_semantics=sem, vmem_limit_bytes=VMEM_LIMIT)


def _sigmoid(x):
    return 0.5 * jnp.tanh(0.5 * x) + 0.5


def _silu_and_grad(x):
    s = _sigmoid(x)
    return x * s, s * (1.0 + x * (1.0 - s))


_G0 = 0.7978845608028654
_G1 = 0.044715


def _gelu_and_grad(x):
    x2 = x * x
    t = jnp.tanh(_G0 * (x + _G1 * x * x2))
    cdf = 0.5 * (1.0 + t)
    return x * cdf, cdf + 0.5 * x * (1.0 - t * t) * (_G0 * (1.0 + 3.0 * _G1 * x2))


def _softplus(z):
    t = jnp.exp(-jnp.abs(z))
    u = 1.0 + t
    log1p = jnp.where(u == 1.0, t, jnp.log(u) * t / jnp.where(u == 1.0, 1.0, u - 1.0))
    return jnp.maximum(z, 0.0) + log1p


def _dot(a, b):
    return jnp.dot(a, b, preferred_element_type=F32)


def _dot_nt(a, b):
    return lax.dot_general(a, b, (((1,), (1,)), ((), ())), preferred_element_type=F32)


def _dot_tn(a, b):
    return lax.dot_general(a, b, (((0,), (0,)), ((), ())), preferred_element_type=F32)


def _rows(shape):
    return lax.broadcasted_iota(jnp.int32, shape, 0)


def _gather2_shapes(arrays, modes):
    return [jax.ShapeDtypeStruct((N_DEV,) + a.shape if m == "ag" else (a.shape[0], N_DEV * a.shape[1]), a.dtype)
            for a, m in zip(arrays, modes)]


def _gather2_sems(n):
    return [pltpu.SemaphoreType.DMA((n, N_DEV - 1)), pltpu.SemaphoreType.DMA((n, N_DEV - 1)),
            pltpu.SemaphoreType.DMA((n,))]


def _barrier(peers):
    sem = pltpu.get_barrier_semaphore()
    for peer in peers:
        pl.semaphore_signal(sem, inc=1, device_id=peer, device_id_type=MESH)
    pl.semaphore_wait(sem, len(peers))


def _gather2_ops(ins, outs, modes, send_sems, recv_sems, local_sems, barrier=False):
    n = len(ins)
    x, y, c = lax.axis_index("x"), lax.axis_index("y"), lax.axis_index("c")
    me, sibling = (x, y, c), (x, y, 1 - c)
    chips = [(x ^ (k >> 1), y ^ (k & 1)) for k in (1, 2, 3)]

    def slot(j, px, py, pc):
        dev = 4 * px + 2 * py + pc
        if modes[j] == "agc":
            w = ins[j].shape[1]
            return outs[j].at[:, pl.ds(pl.multiple_of(dev * w, 128), w)]
        return outs[j].at[dev]

    def copy(j, k, block, to, src=None):
        return pltpu.make_async_remote_copy(
            src_ref=slot(j, *block) if src is None else src, dst_ref=slot(j, *block),
            send_sem=send_sems.at[j, k], recv_sem=recv_sems.at[j, k], device_id=to, device_id_type=MESH)

    def own(j):
        return pltpu.make_async_copy(ins[j], slot(j, *me), local_sems.at[j])

    def first(j):
        return [copy(j, 0, me, sibling, src=ins[j])] + [copy(j, 1 + i, me, (*chip, c), src=ins[j])
                                                        for i, chip in enumerate(chips)]

    def passed(j, i):
        return copy(j, 4 + i, (*chips[i], c), sibling)

    def start():
        if barrier:
            _barrier([sibling] + [(*chip, c) for chip in chips])
        for j in range(n):
            own(j).start()
            for cp in first(j):
                cp.start()

    def forward():
        for i, chip in enumerate(chips):
            for j in range(n):
                copy(j, 1 + i, (*chip, c), me).wait_recv()
                passed(j, i).start()

    def finish():
        for j in range(n):
            copy(j, 0, sibling, me).wait_recv()
            for i, chip in enumerate(chips):
                copy(j, 4 + i, (*chip, 1 - c), me).wait_recv()
            for cp in first(j) + [passed(j, i) for i in range(3)]:
                cp.wait_send()
            own(j).wait()

    return start, forward, finish


def _sibling_barrier():
    sem = pltpu.get_barrier_semaphore()
    sibling = (lax.axis_index("x"), lax.axis_index("y"), 1 - lax.axis_index("c"))
    pl.semaphore_signal(sem, inc=1, device_id=sibling, device_id_type=MESH)
    pl.semaphore_wait(sem, 1)


def _chips_sems(n):
    return [pltpu.SemaphoreType.DMA((n, 6)), pltpu.SemaphoreType.DMA((n, 6)), pltpu.SemaphoreType.DMA((n,))]


def _chips_stage_shapes(chip_sums):
    return [jax.ShapeDtypeStruct((2, a.shape[1] // 2, a.shape[2]), a.dtype) for a in chip_sums]


def _chips_ops(ins, outs, stages, send_sems, recv_sems, local_sems, first_chips=None, barrier=False):
    x, y, c = lax.axis_index("x"), lax.axis_index("y"), lax.axis_index("c")
    qm = 2 * x + y
    first_chips = first_chips or [0] * len(ins)

    def owns(j, chip):
        lo, cnt = first_chips[j], ins[j].shape[0]
        if lo == 0 and cnt == 4:
            return None
        return jnp.logical_and(chip >= lo, chip < lo + cnt)

    def guarded(cond, fn):
        if cond is None:
            fn()
        else:
            pl.when(cond)(fn)

    def slot(j, chip):
        return jnp.clip(chip - first_chips[j], 0, ins[j].shape[0] - 1)

    def half(j, i):
        h = ins[j].shape[1] // 2
        return pl.ds(i * h, h)

    def copy(j, sem, src, dst, k):
        return pltpu.make_async_remote_copy(
            src_ref=src, dst_ref=dst, send_sem=send_sems.at[j, sem], recv_sem=recv_sems.at[j, sem],
            device_id=(x ^ (k >> 1), y ^ (k & 1), c), device_id_type=MESH)

    def direct(j, k):
        return copy(j, k - 1, ins[j].at[slot(j, qm ^ k)], outs[j].at[qm], k)

    def first_hop(j, k):
        return copy(j, 1 + k, ins[j].at[slot(j, qm ^ 3), half(j, k - 1)], stages[j].at[k - 1], k)

    def second_hop(j, k):
        return copy(j, 3 + k, stages[j].at[2 - k], outs[j].at[qm ^ (3 - k), half(j, 2 - k)], k)

    def local(j):
        return pltpu.make_async_copy(ins[j].at[slot(j, qm)], outs[j].at[qm], local_sems.at[j])

    def start():
        if barrier:
            _barrier([(x ^ (k >> 1), y ^ (k & 1), c) for k in (1, 2)])
        for j in range(len(ins)):
            for k in (1, 2):
                guarded(owns(j, qm ^ 3), lambda j=j, k=k: first_hop(j, k).start())
        for j in range(len(ins)):
            for k in (1, 2):
                guarded(owns(j, qm ^ k), lambda j=j, k=k: direct(j, k).start())
            guarded(owns(j, qm), lambda j=j: local(j).start())

    def forward():
        for j in range(len(ins)):
            for k in (1, 2):
                def pass_on(j=j, k=k):
                    first_hop(j, 3 - k).wait_recv()
                    second_hop(j, k).start()
                guarded(owns(j, qm ^ k), pass_on)

    def finish():
        for j in range(len(ins)):
            for k in (1, 2):
                guarded(owns(j, qm ^ k), lambda j=j, k=k: direct(j, k).wait_send())
                guarded(owns(j, qm ^ k), lambda j=j, k=k: second_hop(j, k).wait_send())
                guarded(owns(j, qm ^ 3), lambda j=j, k=k: first_hop(j, k).wait_send())
                guarded(owns(j, qm), lambda j=j, k=k: direct(j, k).wait_recv())
                guarded(owns(j, qm), lambda j=j, k=k: second_hop(j, k).wait_recv())
            guarded(owns(j, qm), lambda j=j: local(j).wait())

    return start, forward, finish


ARRIVAL = (0, 1, 2, 4, 3, 5, 6, 7)


def _front_project(xr, c, c_ctx, ada_w, ada_b, ng, w_in, w_out, cw, lam, me):
    nloc = ada_w.shape[1]
    ws = W_IN_SHARD
    arrival = jnp.asarray(ARRIVAL, jnp.int32)

    def body(me_ref, arr_ref, x_ref, c_ref, cc_ref, aw_ref, ab_ref, ng_ref, win_ref, wout_ref, cw_ref, lam_ref,
             z_ref, hn_ref, wfull_ref, woutb_ref, modx_ref, modc_ref, call_ref, cwf_ref, lamf_ref,
             wv, call_s, part_s, parts_s, w_send, w_recv, hbm_sems, s_send, s_recv, g_send, g_recv, g_local):
        t = pl.program_id(0)
        x, y, cidx = lax.axis_index("x"), lax.axis_index("y"), lax.axis_index("c")
        me_i = me_ref[0]
        sibling = (x, y, 1 - cidx)
        chips = [(x ^ (k >> 1), y ^ (k & 1)) for k in (1, 2, 3)]
        g_start, g_pass, g_finish = _gather2_ops([cw_ref, lam_ref], [cwf_ref, lamf_ref], ["agc", "agc"],
                                                 g_send, g_recv, g_local)

        def shard_copy(k, px, py, pc, to, half=None):
            slot = wv.at[4 * px + 2 * py + pc]
            if half is not None:
                slot = slot.at[pl.ds(half * (D // 2), D // 2), :]
            return pltpu.make_async_remote_copy(src_ref=slot, dst_ref=slot, send_sem=w_send.at[k],
                                                recv_sem=w_recv.at[k], device_id=to, device_id_type=MESH)

        def small_gather(src, my_slot, stage):
            copies = []
            for k in range(1, N_DEV):
                peer = (x ^ (k >> 2), y ^ ((k >> 1) & 1), cidx ^ (k & 1))
                cp = pltpu.make_async_remote_copy(src_ref=src, dst_ref=my_slot, send_sem=s_send.at[stage, k - 1],
                                                  recv_sem=s_recv.at[stage, k - 1], device_id=peer,
                                                  device_id_type=MESH)
                cp.start()
                copies.append(cp)
            pltpu.sync_copy(src, my_slot)
            return copies

        def finish_small(copies):
            for cp in copies:
                cp.wait()

        def to_neighbours(half):
            for i in (0, 1):
                shard_copy(1 + i, x, y, cidx, (*chips[i], cidx), half=half).start()

        @pl.when(t == 0)
        def _():
            _barrier([(x ^ (k >> 2), y ^ ((k >> 1) & 1), cidx ^ (k & 1)) for k in range(1, N_DEV)])
            g_start()
            wv[me_i] = win_ref[...].astype(BF16)
            woutb_ref[...] = wout_ref[...].astype(BF16)
            shard_copy(0, x, y, cidx, sibling).start()
            conds_sent = small_gather(c_ref, call_s.at[pl.ds(me_i, 1), :], 0)
            to_neighbours(0)
            finish_small(conds_sent)
            call_ref[...] = call_s[...]
            off = pl.multiple_of(me_i * nloc, 128)
            b = ab_ref[:, pl.ds(off, nloc)]
            w = aw_ref[...]
            sx, _ = _silu_and_grad(call_s[...])
            sc, _ = _silu_and_grad(jnp.broadcast_to(cc_ref[...], (8, D)))
            part_s[0:8, :] = _dot(sx, w) + b
            part_s[8:16, :] = _dot(sc, w) + b
            parts_sent = small_gather(part_s, parts_s.at[me_i], 1)
            to_neighbours(1)
            finish_small(parts_sent)
            mine = _rows((16, nloc)) == me_i
            for j in range(N_DEV):
                pj = parts_s[j]
                modx_ref[:, j * nloc:(j + 1) * nloc] = jnp.sum(jnp.where(mine, pj, 0.0), axis=0, keepdims=True)
                modc_ref[:, j * nloc:(j + 1) * nloc] = pj[8:9, :]
            shift, scale1, ngv = modx_ref[:, 0:D], 1.0 + modx_ref[:, D:2 * D], ng_ref[...]
            for r in range(L // ROWS):
                rsl = slice(r * ROWS, (r + 1) * ROWS)
                xv = x_ref[rsl, :]
                rs = lax.rsqrt(jnp.mean(xv * xv, axis=-1, keepdims=True) + NORM_EPS)
                hn_ref[rsl, :] = ((xv * rs * ngv) * scale1 + shift).astype(BF16)

        @pl.when(t == 1)
        def _():
            shard_copy(0, x, y, 1 - cidx, sibling).wait_recv()
            g_pass()

        for i in (0, 1):
            @pl.when(t == ARRIVAL.index((2, 4)[i]))
            def _(i=i):
                shard_copy(1 + i, *chips[i], cidx, sibling).wait_recv()
                shard_copy(4 + i, *chips[i], cidx, sibling).start()
                shard_copy((7, 3)[i], *chips[i], cidx, (*chips[1 - i], cidx), half=i).start()

        @pl.when(t == ARRIVAL.index(6))
        def _():
            shard_copy(3, *chips[2], cidx, sibling, half=1).wait_recv()
            shard_copy(7, *chips[2], cidx, sibling, half=0).wait_recv()
            shard_copy(6, *chips[2], cidx, sibling).start()

        for i in range(3):
            @pl.when(t == ARRIVAL.index((3, 5, 7)[i]))
            def _(i=i):
                shard_copy(4 + i, *chips[i], 1 - cidx, sibling).wait_recv()

        @pl.when(t == 2)
        def _():
            g_finish()

        dev = me_i ^ arr_ref[t]
        for r in range(L // (2 * ROWS)):
            rsl = slice(r * 2 * ROWS, (r + 1) * 2 * ROWS)
            z_ref[rsl, :] = _dot(hn_ref[rsl, :], wv[dev])
        col = pl.ds(pl.multiple_of(dev * ws, 128), ws)
        pltpu.make_async_copy(wv.at[dev], wfull_ref.at[:, col], hbm_sems.at[t]).start()

        @pl.when(t == N_DEV - 1)
        def _():
            for k in (0, 1, 2, 4, 5, 6):
                shard_copy(k, x, y, cidx, sibling).wait_send()
            for k in (3, 7):
                shard_copy(k, x, y, cidx, sibling, half=0).wait_send()
            for s in range(N_DEV):
                pltpu.make_async_copy(wv.at[0], wfull_ref.at[:, pl.ds(0, ws)], hbm_sems.at[s]).wait()

    const = lambda *shape: pl.BlockSpec(shape, lambda t, m, a: (0,) * len(shape))
    once = lambda *shape: pl.BlockSpec(shape, lambda t, m, a: (0,) * len(shape), pipeline_mode=pl.Buffered(1))
    return _call(
        body, name="front_project",
        out_shape=[jax.ShapeDtypeStruct((L, D_IN), F32), jax.ShapeDtypeStruct((L, D), BF16),
                   jax.ShapeDtypeStruct((D, D_IN), BF16), jax.ShapeDtypeStruct(w_out.shape, BF16),
                   jax.ShapeDtypeStruct((1, 3 * D), F32), jax.ShapeDtypeStruct((1, 3 * D), F32),
                   jax.ShapeDtypeStruct((N_DEV, D), F32), jax.ShapeDtypeStruct((CONV_W, D), F32),
                   jax.ShapeDtypeStruct((2, D), F32)],
        grid_spec=pltpu.PrefetchScalarGridSpec(
            num_scalar_prefetch=2, grid=(N_DEV,),
            in_specs=[once(L, D), const(1, D), const(1, D), once(D, nloc), const(1, 3 * D), const(1, D),
                      once(D, ws), once(*w_out.shape), HBM, HBM],
            out_specs=[pl.BlockSpec((L, ws), lambda t, m, a: (0, m[0] ^ a[t])), const(L, D), HBM,
                       const(*w_out.shape),
                       const(1, 3 * D), const(1, 3 * D), const(N_DEV, D), HBM, HBM],
            scratch_shapes=[pltpu.VMEM((N_DEV, D, ws), BF16), pltpu.VMEM((N_DEV, D), F32), pltpu.VMEM((16, nloc), F32),
                            pltpu.VMEM((N_DEV, 16, nloc), F32), pltpu.SemaphoreType.DMA((8,)),
                            pltpu.SemaphoreType.DMA((8,)), pltpu.SemaphoreType.DMA((N_DEV,)),
                            pltpu.SemaphoreType.DMA((2, N_DEV - 1)), pltpu.SemaphoreType.DMA((2, N_DEV - 1))]
            + _gather2_sems(2)),
        compiler_params=pltpu.CompilerParams(dimension_semantics=("arbitrary",), vmem_limit_bytes=VMEM_LIMIT,
                                             has_side_effects=True, collective_id=10),
    )(me, arrival, xr, c, c_ctx, ada_w, ada_b, ng, w_in, w_out, pltpu.with_memory_space_constraint(cw, pltpu.HBM),
      pltpu.with_memory_space_constraint(lam, pltpu.HBM))


def _project(xr, mod, ng, w, ncols, tm, name):
    rows = xr.shape[0]

    def body(x_ref, sh_ref, sc_ref, ng_ref, w_ref, z_ref, hn_ref):
        x = x_ref[...]
        rs = lax.rsqrt(jnp.mean(x * x, axis=-1, keepdims=True) + NORM_EPS)
        hn = (x * rs * ng_ref[...]) * (1.0 + sc_ref[...]) + sh_ref[...]
        hb = hn.astype(BF16)
        hn_ref[...] = hb
        for n in range(ncols // D):
            z_ref[:, n * D:(n + 1) * D] = _dot(hb, w_ref[:, n * D:(n + 1) * D])

    vec = pl.BlockSpec((1, D), lambda i: (0, 0))
    return _call(
        body, name=name, grid=(rows // tm,),
        out_shape=[jax.ShapeDtypeStruct((rows, ncols), F32), jax.ShapeDtypeStruct((rows, D), BF16)],
        in_specs=[pl.BlockSpec((tm, D), lambda i: (i, 0)), vec, pl.BlockSpec((1, D), lambda i: (0, 1)), vec,
                  pl.BlockSpec((D, ncols), lambda i: (0, 0), pipeline_mode=pl.Buffered(1))],
        out_specs=[pl.BlockSpec((tm, ncols), lambda i: (i, 0)), pl.BlockSpec((tm, D), lambda i: (i, 0))],
        compiler_params=_params("arbitrary"),
    )(xr, mod, mod, ng, w)


def _scan_pair(af_ref, uf_ref, hf_ref, h0f, ab_ref, ub_ref, hb_ref, h0b, t_len):
    span = 8 * SCAN_BLOCKS
    nit = t_len // span
    rows = _rows((8, HD))

    def local_scan(a, b, forward):
        for s in (1, 2, 4):
            sh = s if forward else 8 - s
            m = rows >= s if forward else rows < 8 - s
            b = a * jnp.where(m, pltpu.roll(b, sh, 0), 0.0) + b
            a = a * jnp.where(m, pltpu.roll(a, sh, 0), 1.0)
        return a, b

    def span_scan(a_ref, u_ref, h_ref, off, carry, forward):
        order = range(SCAN_BLOCKS) if forward else range(SCAN_BLOCKS - 1, -1, -1)
        last = slice(7, 8) if forward else slice(0, 1)
        for q in order:
            rs = pl.ds(off + 8 * q, 8)
            a, b = local_scan(a_ref[rs, :], u_ref[rs, :], forward)
            h_ref[rs, :] = b + a * carry
            carry = a[last, :] * carry + b[last, :]
        return carry

    def body(k, carry):
        cf, cb = carry
        cf = span_scan(af_ref, uf_ref, hf_ref, pl.multiple_of(k * span, span), cf, True)
        cb = span_scan(ab_ref, ub_ref, hb_ref, pl.multiple_of((nit - 1 - k) * span, span), cb, False)
        return cf, cb

    return lax.fori_loop(0, nit, body, (h0f, h0b))


SCAN_BLOCKS = 16


def _shifted(pad_ref, x, offsets, before=0.0, after=0.0):
    n = x.shape[0]
    pad_ref[0:8, :] = jnp.broadcast_to(jnp.asarray(before, F32), (8, x.shape[1]))
    pad_ref[8:8 + n, :] = x
    pad_ref[8 + n:16 + n, :] = jnp.broadcast_to(jnp.asarray(after, F32), (8, x.shape[1]))
    return [pad_ref[8 + o:8 + o + n, :] for o in offsets]


def _conv(xa, cw, cb, pad_ref):
    xm1, xp1, xp2 = _shifted(pad_ref, xa, (-1, 1, 2))
    return xm1 * cw[0:1, :] + xa * cw[1:2, :] + xp1 * cw[2:3, :] + xp2 * cw[3:4, :] + cb


def _gates(xc, wa, wx, ba, bx, nsp):
    xb = xc.astype(BF16)
    r = _sigmoid(_dot(xb, wa) + ba)
    i = _sigmoid(_dot(xb, wx) + bx)
    log_a = r * nsp
    a = jnp.exp(log_a)
    g2 = jnp.tanh(log_a) * (-1.0 - a * a)
    rg = lax.rsqrt(jnp.maximum(g2, 1e-30))
    return r, i, a, g2 * rg, rg


def _lru_param_specs():
    h4 = pl.BlockSpec((2, 1, HD, HD), lambda h: (0, h, 0, 0))
    v2 = pl.BlockSpec((2, HD), lambda h: (0, h))
    b16 = pl.BlockSpec((2 * HEADS, HD), lambda h: (0, 0))
    return dict(
        xa=pl.BlockSpec((L, HD), lambda h: (0, h)), xac=pl.BlockSpec((LC, HD), lambda h: (0, h)),
        cw=pl.BlockSpec((CONV_W, HD), lambda h: (0, h)), cb=pl.BlockSpec((1, HD), lambda h: (0, h)), h4=h4, v2=v2,
        b16=b16)


def _bias_row(ref, d):
    mask = _rows((2 * HEADS, HD)) == d * HEADS + pl.program_id(0)
    return jnp.sum(jnp.where(mask, ref[...], 0.0), axis=0, keepdims=True), mask


def _lru_forward(zx, zc, cw, cb, wa, wx, ba, bx, lam, gather, gather_modes):
    ng_ = len(gather)

    def body(xa_ref, xac_ref, cw_ref, cb_ref, wa_ref, wx_ref, ba_ref, bx_ref, lam_ref, *rest):
        yl_ref = rest[ng_]
        af, uf, hf, ab, ub, hb, pad_s = rest[2 * ng_ + 1:2 * ng_ + 8]
        start, pass_on, finish = _gather2_ops(rest[:ng_], rest[ng_ + 1:2 * ng_ + 1], gather_modes,
                                              *rest[2 * ng_ + 8:], barrier=True)
        pl.when(pl.program_id(0) == 0)(start)
        pl.when(pl.program_id(0) == HEADS // 2)(pass_on)
        pl.when(pl.program_id(0) == HEADS - 1)(finish)
        cwv, cbv = cw_ref[...], cb_ref[...]
        nsp = (-LRU_C) * _softplus(-lam_ref[...])

        def forward(xa, t_len, h0f, h0b):
            xc = _conv(xa, cwv, cbv, pad_s)
            for d, (a_ref, u_ref) in enumerate(((af, uf), (ab, ub))):
                _, i, a, gamma, _ = _gates(xc, wa_ref[d, 0].astype(BF16), wx_ref[d, 0].astype(BF16),
                                           _bias_row(ba_ref, d)[0], _bias_row(bx_ref, d)[0], nsp[d:d + 1, :])
                a_ref[0:t_len, :] = a
                u_ref[0:t_len, :] = gamma * (i * xc)
            return _scan_pair(af, uf, hf, h0f, ab, ub, hb, h0b, t_len)

        z = jnp.zeros((1, HD), F32)
        h0f, h0b = forward(xac_ref[...], LC, z, z)
        forward(xa_ref[...], L, h0f, h0b)
        yl_ref[...] = hf[...] + hb[...]

    s = _lru_param_specs()
    return _call(
        body, name="lru_forward", grid=(HEADS,),
        out_shape=[jax.ShapeDtypeStruct((L, D), F32)] + _gather2_shapes(gather, gather_modes),
        in_specs=[s["xa"], s["xac"], s["cw"], s["cb"], s["h4"], s["h4"], s["b16"], s["b16"], s["v2"]] + [HBM] * ng_,
        out_specs=[pl.BlockSpec((L, HD), lambda h: (0, h))] + [HBM] * ng_,
        scratch_shapes=[pltpu.VMEM((L, HD), F32)] * 6 + [pltpu.VMEM((L + 16, HD), F32)] + _gather2_sems(ng_),
        compiler_params=pltpu.CompilerParams(dimension_semantics=("arbitrary",), vmem_limit_bytes=VMEM_LIMIT,
                                             has_side_effects=True, collective_id=5),
    )(zx, zc, cw, cb, wa, wx, ba, bx, lam, *[pltpu.with_memory_space_constraint(a, pltpu.HBM) for a in gather])


def _lru_backward(zx, zc, dyl, dz, cw, cb, wa, wx, ba, bx, lam, chip_sums, first_chips=None):
    nr = len(chip_sums)

    def body(xa_ref, xac_ref, dyl_ref, dz_in, cw_ref, cb_ref, wa_ref, wx_ref, ba_ref, bx_ref, lam_ref, *rest):
        (dxa_ref, dxac_ref, dwa_ref, dwx_ref, dba_ref, dbx_ref, dlam_ref, dcw_ref,
         dcb_ref) = rest[nr:nr + 9]
        main_s, ctx_s, pad_s = rest[3 * nr + 9:3 * nr + 12]
        if nr:
            start, forward, finish = _chips_ops(rest[:nr], rest[nr + 9:2 * nr + 9], rest[2 * nr + 9:3 * nr + 9],
                                                *rest[3 * nr + 12:], first_chips=first_chips, barrier=True)
            pl.when(pl.program_id(0) == 0)(start)
            pl.when(pl.program_id(0) == HEADS // 2)(forward)
            pl.when(pl.program_id(0) == HEADS - 1)(finish)
        del dz_in

        @pl.when(pl.program_id(0) == 0)
        def _():
            dba_ref[...] = jnp.zeros_like(dba_ref)
            dbx_ref[...] = jnp.zeros_like(dbx_ref)

        cwv, cbv = cw_ref[...], cb_ref[...]
        lamv = lam_ref[...]
        sp = _softplus(-lamv)
        nsp = (-LRU_C) * sp
        z = jnp.zeros((1, HD), F32)

        def wmat(ref, d):
            return ref[d, 0].astype(BF16)

        def workspace(s):
            return dict(a=(s.at[0], s.at[1]), u=(s.at[2], s.at[3]), h=(s.at[4], s.at[5]), rho=(s.at[6], s.at[7]),
                        saved=(tuple(s.at[8 + k] for k in range(4)), tuple(s.at[12 + k] for k in range(4))),
                        xc=s.at[16])

        def forward(ws, xa, t_len, h0f, h0b):
            xc = _conv(xa, cwv, cbv, pad_s)
            ws["xc"][...] = xc
            for d in (0, 1):
                vals = _gates(xc, wmat(wa_ref, d), wmat(wx_ref, d), _bias_row(ba_ref, d)[0],
                              _bias_row(bx_ref, d)[0], nsp[d:d + 1, :])
                r, i, a, gamma, rg = vals
                ws["a"][d][...] = a
                ws["u"][d][...] = gamma * (i * xc)
                for ref, val in zip(ws["saved"][d], (r, i, gamma, rg)):
                    ref[...] = val
            return _scan_pair(ws["a"][0], ws["u"][0], ws["h"][0], h0f, ws["a"][1], ws["u"][1], ws["h"][1], h0b,
                              t_len)

        def backward(ws, xa, t_len, h0f, h0b, dhf, dhb, first):
            xc = ws["xc"][...]
            (af, ab), (uf, ub), (hf, hb), (rf, rb) = ws["a"], ws["u"], ws["h"], ws["rho"]
            uf[...] = ab[...] * dhb
            ub[...] = af[...] * dhf
            rho_b_last, rho_f_first = _scan_pair(ab, uf, rb, z, af, ub, rf, z, t_len)
            dxc = jnp.zeros((t_len, HD), F32)
            dsp = []
            for d in (0, 1):
                r, i, gamma, rg = (ref[...] for ref in ws["saved"][d])
                a = ws["a"][d][...]
                if d == 0:
                    lam_t = dhf + _shifted(pad_s, rf[...], (1,))[0]
                    h_prev = _shifted(pad_s, hf[...], (-1,), before=h0f)[0]
                else:
                    lam_t = dhb + _shifted(pad_s, rb[...], (-1,))[0]
                    h_prev = _shifted(pad_s, hb[...], (1,), after=h0b)[0]
                da = lam_t * h_prev
                lx = lam_t * xc
                d_i = lx * gamma
                d_gamma = lx * i
                dxc = dxc + lam_t * (gamma * i)
                d_log_a = a * (da - d_gamma * (a * rg))
                dsp.append(jnp.sum(d_log_a * r, axis=0, keepdims=True) * (-LRU_C))
                d_pre_r = d_log_a * nsp[d:d + 1, :] * (r * (1.0 - r))
                d_pre_i = d_i * (i * (1.0 - i))
                prb, pib, xb = d_pre_r.astype(BF16), d_pre_i.astype(BF16), xc.astype(BF16)
                dxc = dxc + _dot_nt(prb, wmat(wa_ref, d)) + _dot_nt(pib, wmat(wx_ref, d))
                g_wa, g_wx = _dot_tn(xb, prb), _dot_tn(xb, pib)
                g_ba = jnp.sum(d_pre_r, axis=0, keepdims=True)
                g_bx = jnp.sum(d_pre_i, axis=0, keepdims=True)
                mask = _bias_row(ba_ref, d)[1]
                dba_ref[...] += jnp.where(mask, g_ba, 0.0)
                dbx_ref[...] += jnp.where(mask, g_bx, 0.0)
                if first:
                    dwa_ref[d, 0] = g_wa
                    dwx_ref[d, 0] = g_wx
                else:
                    dwa_ref[d, 0] += g_wa
                    dwx_ref[d, 0] += g_wx
            g_lam = jnp.concatenate(dsp, axis=0) * (-_sigmoid(-lamv))
            dm1, dp1, dm2 = _shifted(pad_s, dxc, (-1, 1, -2))
            dxa = dp1 * cwv[0:1, :] + dxc * cwv[1:2, :] + dm1 * cwv[2:3, :] + dm2 * cwv[3:4, :]
            xm1, xp1, xp2 = _shifted(pad_s, xa, (-1, 1, 2))
            g_cw = jnp.concatenate([jnp.sum(dxc * v, axis=0, keepdims=True) for v in (xm1, xa, xp1, xp2)], axis=0)
            g_cb = jnp.sum(dxc, axis=0, keepdims=True)
            if first:
                dlam_ref[...] = g_lam
                dcw_ref[...] = g_cw
                dcb_ref[...] = g_cb
            else:
                dlam_ref[...] += g_lam
                dcw_ref[...] += g_cw
                dcb_ref[...] += g_cb
            return dxa, rho_f_first, rho_b_last

        ws_x, ws_c = workspace(main_s), workspace(ctx_s)
        h0f, h0b = forward(ws_c, xac_ref[...], LC, z, z)
        forward(ws_x, xa_ref[...], L, h0f, h0b)
        dh = dyl_ref[...]
        dxa, dh0f, dh0b = backward(ws_x, xa_ref[...], L, h0f, h0b, dh, dh, True)
        dxa_ref[...] = dxa.astype(BF16)
        rc = _rows((LC, HD))
        dxac, _, _ = backward(ws_c, xac_ref[...], LC, z, z, jnp.where(rc == LC - 1, dh0f, 0.0),
                              jnp.where(rc == 0, dh0b, 0.0), False)
        dxac_ref[...] = dxac.astype(BF16)

    s = _lru_param_specs()
    col = lambda r: pl.BlockSpec((r, HD), lambda h: (0, h))
    return _call(
        body, name="lru_backward", grid=(HEADS,),
        out_shape=[jax.ShapeDtypeStruct((L, D_IN), BF16), jax.ShapeDtypeStruct((LC, D), BF16),
                   jax.ShapeDtypeStruct((2, HEADS, HD, HD), F32), jax.ShapeDtypeStruct((2, HEADS, HD, HD), F32),
                   jax.ShapeDtypeStruct((2 * HEADS, HD), F32), jax.ShapeDtypeStruct((2 * HEADS, HD), F32),
                   jax.ShapeDtypeStruct((2, D), F32), jax.ShapeDtypeStruct((CONV_W, D), F32),
                   jax.ShapeDtypeStruct((1, D), F32)] + [jax.ShapeDtypeStruct((4,) + a.shape[1:], a.dtype)
                                                          for a in chip_sums] + _chips_stage_shapes(chip_sums),
        in_specs=[s["xa"], s["xac"], col(L), pl.BlockSpec(memory_space=pl.ANY), s["cw"], s["cb"], s["h4"], s["h4"],
                  s["b16"], s["b16"], s["v2"]] + [HBM] * nr,
        out_specs=[col(L), col(LC), s["h4"], s["h4"], s["b16"], s["b16"], s["v2"], col(CONV_W), col(1)]
        + [HBM] * (2 * nr),
        scratch_shapes=[pltpu.VMEM((17, L, HD), F32), pltpu.VMEM((17, LC, HD), F32), pltpu.VMEM((L + 16, HD), F32)]
        + (_chips_sems(nr) if nr else []),
        input_output_aliases={3: 0},
        compiler_params=pltpu.CompilerParams(dimension_semantics=("arbitrary",), vmem_limit_bytes=VMEM_LIMIT,
                                             has_side_effects=True, collective_id=6 if nr else None),
    )(zx, zc, dyl, dz, cw, cb, wa, wx, ba, bx, lam, *[pltpu.with_memory_space_constraint(a, pltpu.HBM)
                                                       for a in chip_sums])


def _mixer_loss(x, tgt, zx, yl, gx, fg, lng, lnb, ws, bst, wout, tm):
    ncht = tm // CHUNK

    def body(x_ref, t_ref, ga_ref, u_ref, v_ref, gb_ref, yl_ref, gx_ref, fg_ref, lng_ref, lnb_ref, ws_ref,
             bst_ref, wout_ref,
             dz_ref, dyl_ref, dxn_ref, y_s, do_ref, dws_ref, dbst_ref, vec_ref,
             vn_s, mix_s, dm_s, dvn_s):
        step = pl.program_id(0)

        @pl.when(step == 0)
        def _():
            dws_ref[...] = jnp.zeros_like(dws_ref)
            dbst_ref[...] = jnp.zeros_like(dbst_ref)
            vec_ref[...] = jnp.zeros_like(vec_ref)

        u, v = u_ref[...], v_ref[...]
        ug, dug_du = _gelu_and_grad(u)
        vg, dvg_dv = _gelu_and_grad(v)
        mu = jnp.mean(vg, axis=-1, keepdims=True)
        vc = vg - mu
        rstd = lax.rsqrt(jnp.mean(vc * vc, axis=-1, keepdims=True) + LN_EPS)
        vhat = vc * rstd
        lngv = lng_ref[...]
        vn_s[...] = (vhat * lngv + lnb_ref[...]).astype(BF16)
        for ch in range(ncht):
            rs = slice(ch * CHUNK, (ch + 1) * CHUNK)
            for g in range(HEADS):
                cs = slice(g * HD, (g + 1) * HD)
                mix_s[rs, cs] = _dot(ws_ref[g].astype(BF16), vn_s[rs, cs]) + bst_ref[:, g:g + 1]
        mixed = mix_s[...]
        ga, gb, yl = ga_ref[...], gb_ref[...], yl_ref[...]
        sga, dsga = _silu_and_grad(ga)
        sgb, dsgb = _silu_and_grad(gb)
        ys = ug * mixed
        y_s[:, 0:D] = (yl * sga).astype(BF16)
        y_s[:, D:D_MIX] = (ys * sgb).astype(BF16)
        o = _dot(y_s[...], wout_ref[...])
        gxv, fgv = gx_ref[...], fg_ref[...]
        xn = x_ref[...] + gxv * o
        rs2 = lax.rsqrt(jnp.mean(xn * xn, axis=-1, keepdims=True) + NORM_EPS)
        xh = xn * rs2
        diff = xh * fgv - t_ref[...]
        vec_ref[R_LOSS:R_LOSS + 1, :] += jnp.full((1, D), jnp.sum(diff * diff) * (0.5 / D), F32)
        dout = diff * (1.0 / D)
        w = dout * fgv
        dxn = rs2 * (w - xh * jnp.mean(w * xh, axis=-1, keepdims=True))
        dxn_ref[...] = dxn
        vec_ref[0:1, :] += jnp.sum(dxn * o, axis=0, keepdims=True)
        vec_ref[1:2, :] += jnp.sum(dout * xh, axis=0, keepdims=True)
        dob = (dxn * gxv).astype(BF16)
        do_ref[...] = dob
        dy = _dot_nt(dob, wout_ref[...])
        dya, dyb = dy[:, 0:D], dy[:, D:D_MIX]
        dyl_ref[...] = dya * sga
        dys = dyb * sgb
        dz_ref[:, 0:D] = jnp.zeros((tm, D), BF16)
        dz_ref[:, D:2 * D] = (dya * yl * dsga).astype(BF16)
        dz_ref[:, 2 * D:3 * D] = (dys * mixed * dug_du).astype(BF16)
        dz_ref[:, 4 * D:5 * D] = (dyb * ys * dsgb).astype(BF16)
        dm = dys * ug
        dm_s[...] = dm.astype(BF16)
        for g in range(HEADS):
            cs = slice(g * HD, (g + 1) * HD)
            dbst_ref[:, g:g + 1] += sum(jnp.sum(dm[ch * CHUNK:(ch + 1) * CHUNK, cs], axis=1, keepdims=True)
                                        for ch in range(ncht))
            for ch in range(ncht):
                rs = slice(ch * CHUNK, (ch + 1) * CHUNK)
                dws_ref[g] += _dot_nt(dm_s[rs, cs], vn_s[rs, cs])
                dvn_s[rs, cs] = _dot_tn(ws_ref[g].astype(BF16), dm_s[rs, cs])
        dvn = dvn_s[...]
        vec_ref[2:3, :] += jnp.sum(dvn * vhat, axis=0, keepdims=True)
        vec_ref[3:4, :] += jnp.sum(dvn, axis=0, keepdims=True)
        dvh = dvn * lngv
        dvg = rstd * (dvh - jnp.mean(dvh, axis=-1, keepdims=True) - vhat * jnp.mean(dvh * vhat, axis=-1, keepdims=True))
        dz_ref[:, 3 * D:4 * D] = (dvg * dvg_dv).astype(BF16)

    tile = pl.BlockSpec((tm, D), lambda i: (i, 0))
    zcol = lambda n: pl.BlockSpec((tm, D), lambda i: (i, n))
    vec = pl.BlockSpec((1, D), lambda i: (0, 0))
    full = lambda *s: pl.BlockSpec(s, lambda i: (0,) * len(s))
    return _call(
        body, name="mixer_loss", grid=(L // tm,),
        out_shape=[jax.ShapeDtypeStruct((L, D_IN), BF16), jax.ShapeDtypeStruct((L, D), F32),
                   jax.ShapeDtypeStruct((L, D), F32), jax.ShapeDtypeStruct((L, D_MIX), BF16),
                   jax.ShapeDtypeStruct((L, D), BF16),
                   jax.ShapeDtypeStruct((HEADS, CHUNK, CHUNK), F32), jax.ShapeDtypeStruct((CHUNK, HEADS), F32),
                   jax.ShapeDtypeStruct((8, D), F32)],
        in_specs=[tile, tile, zcol(1), zcol(2), zcol(3), zcol(4), tile, pl.BlockSpec((1, D), lambda i: (0, 2)),
                  vec, vec, vec,
                  full(HEADS, CHUNK, CHUNK), full(CHUNK, HEADS),
                  pl.BlockSpec((D_MIX, D), lambda i: (0, 0), pipeline_mode=pl.Buffered(1))],
        out_specs=[pl.BlockSpec((tm, D_IN), lambda i: (i, 0)), tile, tile,
                   pl.BlockSpec((tm, D_MIX), lambda i: (i, 0)), tile,
                   full(HEADS, CHUNK, CHUNK), full(CHUNK, HEADS), full(8, D)],
        scratch_shapes=[pltpu.VMEM((tm, D), BF16), pltpu.VMEM((tm, D), F32),
                        pltpu.VMEM((tm, D), BF16), pltpu.VMEM((tm, D), F32)],
        compiler_params=_params("arbitrary"),
    )(x, tgt, zx, zx, zx, zx, yl, gx, fg, lng, lnb, ws, bst, wout)


def _grad_w(a, b, a2, b2, tk, name, bw, first, nblocks, split, barrier_id, riders=()):
    nk = a.shape[0] // tk
    m = a.shape[1]
    with_ctx = a2 is not None
    if split == "cols":
        slots, r, w = nblocks, m, bw // 2
        piece = lambda q, pc: (slice(None), slice(pc * w, (pc + 1) * w))
    else:
        slots, r, w = 4, m // 8, bw
        piece = lambda q, pc: (slice((2 * q + pc) * r, (2 * q + pc + 1) * r), slice(None))

    nr = len(riders)

    def body(*refs):
        a_ref, b_ref = refs[:2]
        a2_ref, b2_ref = refs[2:4] if with_ctx else (None, None)
        base = 4 if with_ctx else 2
        sums_ref = refs[base + nr]
        s0 = base + 3 * nr + 1
        acc, mine_v, send_v, stage_v, send_sems, recv_sems = refs[s0:s0 + 6]
        n, k = pl.program_id(0), pl.program_id(1)
        x, y, c = lax.axis_index("x"), lax.axis_index("y"), lax.axis_index("c")

        def to_sibling(s):
            return pltpu.make_async_remote_copy(src_ref=send_v.at[s], dst_ref=stage_v.at[s], send_sem=send_sems.at[s],
                                                recv_sem=recv_sems.at[s], device_id=(x, y, 1 - c),
                                                device_id_type=MESH)

        if nr:
            rider_in = refs[base:base + nr]
            own_v, got_v, psum_v = (refs[s0 + 9 + i * nr:s0 + 9 + (i + 1) * nr] for i in range(3))
            p_send, p_recv, p_local = refs[s0 + 9 + 3 * nr:s0 + 12 + 3 * nr]
            c_start, c_forward, c_finish = _chips_ops(psum_v, refs[base + nr + 1:base + 2 * nr + 1],
                                                      refs[base + 2 * nr + 1:base + 3 * nr + 1], *refs[s0 + 6:s0 + 9])

            @pl.when(jnp.logical_and(n == 0, k == 0))
            def _():
                _barrier([(x, y, 1 - c)] + [(x ^ (j >> 1), y ^ (j & 1), c) for j in (1, 2)])
                copies = []
                for j in range(nr):
                    for q in range(4):
                        copies.append(pltpu.make_async_remote_copy(
                            src_ref=rider_in[j].at[2 * q + 1 - c], dst_ref=got_v[j].at[q], send_sem=p_send.at[j, q],
                            recv_sem=p_recv.at[j, q], device_id=(x, y, 1 - c), device_id_type=MESH))
                        copies.append(pltpu.make_async_copy(rider_in[j].at[2 * q + c], own_v[j].at[q],
                                                            p_local.at[j, q]))
                for cp in copies:
                    cp.start()
                for cp in copies:
                    cp.wait()
                for j in range(nr):
                    psum_v[j][...] = (own_v[j][...] + got_v[j][...]).astype(BF16)
                c_start()
        else:
            pl.when(jnp.logical_and(n == 0, k == 0))(_sibling_barrier)

        @pl.when(k == 0)
        def _():
            acc[...] = _dot_tn(a_ref[...], b_ref[...])

        if nk > 1:
            @pl.when(k > 0)
            def _():
                acc[...] += _dot_tn(a_ref[...], b_ref[...])

        if with_ctx:
            @pl.when(jnp.logical_and(k == nk - 1, n == 0))
            def _():
                acc[:, 0:b2_ref.shape[1]] += _dot_tn(a2_ref[...], b2_ref[...])

        if nr:
            pl.when(jnp.logical_and(k == nk - 1, n == nblocks - 1))(c_forward)

        def hand_over(s, q):
            for pc in (0, 1):
                @pl.when(c == pc)
                def _(pc=pc):
                    mine_v[s] = acc[piece(q, pc)]
                    send_v[s] = acc[piece(q, 1 - pc)].astype(BF16)
            to_sibling(s).start()

        for i in range(nblocks):
            @pl.when(jnp.logical_and(k == nk - 1, n == i))
            def _(i=i):
                if split == "cols":
                    hand_over(i, 0)
                else:
                    for q in range(4):
                        hand_over(q, q)

        @pl.when(jnp.logical_and(k == nk - 1, n == nblocks - 1))
        def _():
            for s in range(slots):
                to_sibling(s).wait_recv()
                sums_ref[s] = (mine_v[s] + stage_v[s].astype(F32)).astype(BF16)
            for s in range(slots):
                to_sibling(s).wait_send()
            if nr:
                c_finish()

    in_specs = [pl.BlockSpec((tk, m), lambda n, k: (k, 0)), pl.BlockSpec((tk, bw), lambda n, k: (k, n + first))]
    args = [a, b]
    if with_ctx:
        in_specs += [pl.BlockSpec(a2.shape, lambda n, k: (0, 0)), pl.BlockSpec(b2.shape, lambda n, k: (0, 0))]
        args += [a2, b2]
    in_specs += [HBM] * nr
    args += [pltpu.with_memory_space_constraint(s, pltpu.HBM) for s in riders]
    rider_sums = [jax.ShapeDtypeStruct((4,) + s.shape[1:], BF16) for s in riders]
    rider_scratch = []
    if nr:
        rider_scratch = (_chips_sems(nr) + [pltpu.VMEM(s.shape, F32) for s in rider_sums] * 2
                         + [pltpu.VMEM(s.shape, BF16) for s in rider_sums]
                         + [pltpu.SemaphoreType.DMA((nr, 4))] * 3)
    return _call(
        body, name=name, grid=(nblocks, nk),
        out_shape=[jax.ShapeDtypeStruct((slots, r, w), BF16)] + rider_sums + _chips_stage_shapes(rider_sums),
        in_specs=in_specs, out_specs=[pl.BlockSpec((slots, r, w), lambda n, k: (0, 0, 0))] + [HBM] * (2 * nr),
        scratch_shapes=[pltpu.VMEM((m, bw), F32), pltpu.VMEM((slots, r, w), F32), pltpu.VMEM((slots, r, w), BF16),
                        pltpu.VMEM((slots, r, w), BF16), pltpu.SemaphoreType.DMA((slots,)),
                        pltpu.SemaphoreType.DMA((slots,))] + rider_scratch,
        compiler_params=pltpu.CompilerParams(dimension_semantics=("arbitrary", "arbitrary"),
                                             vmem_limit_bytes=VMEM_LIMIT, has_side_effects=True,
                                             collective_id=barrier_id),
    )(*args)


def _grad_rows(xr, dz, w, mod, ng, dres, ncols, tm, name, chip_sums=(), first_chips=None, dests=None):
    rows = xr.shape[0]
    steps = rows // tm
    with_dx = dres is not None
    nr = len(chip_sums)
    dests = [d for d in (dests or [None] * nr)]
    nd = sum(d is not None for d in dests)
    nin = 6 if with_dx else 5
    nout = 2 if with_dx else 1

    def body(*refs):
        if with_dx:
            x_ref, dz_ref, w_ref, sc_ref, ng_ref, dres_ref = refs[:nin]
            dx_ref, vec_ref = refs[nin + nr + nd:nin + nr + nd + nout]
        else:
            x_ref, dz_ref, w_ref, sc_ref, ng_ref = refs[:nin]
            (vec_ref,) = refs[nin + nr + nd:nin + nr + nd + nout]
        if nr:
            o0 = nin + nr + nd + nout
            start, forward, finish = _chips_ops(refs[nin:nin + nr], refs[o0:o0 + nr], refs[o0 + nr:o0 + 2 * nr],
                                                *refs[o0 + 2 * nr:], first_chips=first_chips, barrier=True)
            pl.when(pl.program_id(0) == 0)(start)
            pl.when(pl.program_id(0) == steps // 2)(forward)
            pl.when(pl.program_id(0) == steps - 1)(finish)

        @pl.when(pl.program_id(0) == 0)
        def _():
            vec_ref[...] = jnp.zeros_like(vec_ref)

        dhn = _dot_nt(dz_ref[...], w_ref[...])
        x = x_ref[...]
        rs = lax.rsqrt(jnp.mean(x * x, axis=-1, keepdims=True) + NORM_EPS)
        xh = x * rs
        ngv = ng_ref[...]
        y = xh * ngv
        vec_ref[0:1, :] += jnp.sum(dhn, axis=0, keepdims=True)
        vec_ref[1:2, :] += jnp.sum(dhn * y, axis=0, keepdims=True)
        dy = dhn * (1.0 + sc_ref[...])
        vec_ref[2:3, :] += jnp.sum(dy * xh, axis=0, keepdims=True)
        if with_dx:
            dxh = dy * ngv
            dx_ref[...] = dres_ref[...] + rs * (dxh - xh * jnp.mean(dxh * xh, axis=-1, keepdims=True))

    tile = pl.BlockSpec((tm, D), lambda i: (i, 0))
    vec = pl.BlockSpec((1, D), lambda i: (0, 0))
    in_specs = [tile, pl.BlockSpec((tm, ncols), lambda i: (i, 0)),
                pl.BlockSpec((D, ncols), lambda i: (0, 0), pipeline_mode=pl.Buffered(1)),
                pl.BlockSpec((1, D), lambda i: (0, 1)), vec]
    out_shape = [jax.ShapeDtypeStruct((8, D), F32)]
    out_specs = [pl.BlockSpec((8, D), lambda i: (0, 0))]
    args = [xr, dz, w, mod, ng]
    if with_dx:
        in_specs.append(tile)
        out_shape.insert(0, jax.ShapeDtypeStruct((rows, D), F32))
        out_specs.insert(0, tile)
        args.append(dres)
    aliases = {}
    for j, d in enumerate(dests):
        if d is not None:
            aliases[len(args) + nr + len(aliases)] = len(out_shape) + j
    in_specs += [HBM] * (nr + nd)
    out_specs += [HBM] * (2 * nr)
    out_shape += [jax.ShapeDtypeStruct((4,) + a.shape[1:], a.dtype) for a in chip_sums]
    out_shape += _chips_stage_shapes(chip_sums)
    args += [pltpu.with_memory_space_constraint(a, pltpu.HBM) for a in chip_sums]
    args += [pltpu.with_memory_space_constraint(d, pltpu.HBM) for d in dests if d is not None]
    return _call(body, name=name, grid=(steps,), out_shape=out_shape, in_specs=in_specs, out_specs=out_specs,
                 scratch_shapes=_chips_sems(nr) if nr else [], input_output_aliases=aliases,
                 compiler_params=pltpu.CompilerParams(dimension_semantics=("arbitrary",),
                                                      vmem_limit_bytes=VMEM_LIMIT, has_side_effects=bool(nr),
                                                      collective_id=7 if nr else None))(*args)


def _adamw(w, g, m, v):
    m = ADAM_B1 * m + (1.0 - ADAM_B1) * g
    v = ADAM_B2 * v + (1.0 - ADAM_B2) * (g * g)
    m_hat = m / (1.0 - ADAM_B1 ** ADAM_STEP)
    v_hat = v / (1.0 - ADAM_B2 ** ADAM_STEP)
    delta = -ADAM_LR * (m_hat / (jnp.sqrt(v_hat) + ADAM_EPS) + ADAM_WD * w)
    return delta, m, v


def _adamw_reduced(parts, w, m, v, tr, name):
    r, n = w.shape
    nparts = parts.shape[0]

    def body(p_ref, w_ref, m_ref, v_ref, g_ref, d_ref, mo_ref, vo_ref):
        g = p_ref[0].astype(F32)
        for i in range(1, nparts):
            g = g + p_ref[i].astype(F32)
        g_ref[...] = g
        d_ref[...], mo_ref[...], vo_ref[...] = _adamw(w_ref[...], g, m_ref[...], v_ref[...])

    tile = pl.BlockSpec((tr, n), lambda i: (i, 0))
    sds = jax.ShapeDtypeStruct((r, n), F32)
    return _call(
        body, name=name, grid=(r // tr,), out_shape=[sds] * 4,
        in_specs=[pl.BlockSpec((nparts, tr, n), lambda i: (0, i, 0)), tile, tile, tile], out_specs=[tile] * 4,
        compiler_params=_params("arbitrary"),
    )(parts, w, m, v)


R_GATE, R_FINAL_G, R_LN_G, R_LN_B, R_LOSS = 0, 1, 2, 3, 4
R_SH_X, R_SC_X, R_NG_X = 5, 6, 7
R_SH_C, R_SC_C, R_NG_C = 8, 9, 10
R_LAM, R_CW, R_CB = 11, 13, 17
PACK_ROWS = 24
Q_BA, Q_BX, Q_SGU_B, PACK128_ROWS = 0, 16, 32, 40


def _reduce_small(vec_pieces, q_pieces, mat_parts, ada_w, me):
    nloc = ada_w.shape[1]
    nm = len(mat_parts)
    pieces = list(vec_pieces) + list(q_pieces)

    def body(me_ref, *refs):
        piece_refs, refs = refs[:len(pieces)], refs[len(pieces):]
        mp_refs, w_ref = refs[:nm], refs[nm]
        red_ref, redq_ref = refs[nm + 1:nm + 3]
        mats_all = refs[nm + 3:2 * nm + 3]
        cparts_ref, dmod_ref, gab_ref, loss_ref = refs[2 * nm + 3:2 * nm + 7]
        pack_ref, packq_ref, vp_ref, vq_ref = refs[2 * nm + 7:2 * nm + 11]
        mat_refs = refs[2 * nm + 11:3 * nm + 11]
        cpart_ref, dmc_s = refs[3 * nm + 11:3 * nm + 13]
        sems = refs[3 * nm + 13:]
        for dst, group in ((pack_ref, vec_pieces), (packq_ref, q_pieces)):
            row = 0
            for _, nrows in group:
                dst[row:row + nrows, :] = piece_refs[0][0:nrows, :]
                piece_refs, row = piece_refs[1:], row + nrows
            if row < dst.shape[0]:
                dst[row:, :] = jnp.zeros((dst.shape[0] - row, dst.shape[1]), F32)
        p_start, p_forward, p_finish = _gather2_ops([pack_ref, packq_ref], [vp_ref, vq_ref], ["ag", "ag"], *sems[:3],
                                                    barrier=True)
        m_start, m_forward, m_finish = _gather2_ops(mat_refs, mats_all, ["ag"] * nm, *sems[3:6])
        c_start, c_forward, c_finish = _gather2_ops([cpart_ref], [cparts_ref], ["ag"], *sems[6:])
        p_start()
        for mp_ref, mat_ref in zip(mp_refs, mat_refs):
            mat = mp_ref[0].astype(F32)
            for i in range(1, mp_ref.shape[0]):
                mat = mat + mp_ref[i].astype(F32)
            mat_ref[...] = mat
        m_start()
        p_forward()
        p_finish()
        m_forward()
        red, redq = vp_ref[0], vq_ref[0]
        for i in range(1, N_DEV):
            red = red + vp_ref[i]
            redq = redq + vq_ref[i]
        red_ref[...] = red
        redq_ref[...] = redq
        loss_ref[...] = red_ref[R_LOSS:R_LOSS + 1, 0:1]
        for e in range(N_DEV):
            dmod_ref[e:e + 1, 0:D] = vp_ref[e, R_SH_X:R_SH_X + 1, :]
            dmod_ref[e:e + 1, D:2 * D] = vp_ref[e, R_SC_X:R_SC_X + 1, :]
            dmod_ref[e:e + 1, 2 * D:3 * D] = vp_ref[e, R_GATE:R_GATE + 1, :]
        dmod_ref[8:9, 0:D] = red[R_SH_C:R_SH_C + 1, :]
        dmod_ref[8:9, D:2 * D] = red[R_SC_C:R_SC_C + 1, :]
        dmod_ref[8:9, 2 * D:3 * D] = jnp.zeros((1, D), F32)
        dmod_ref[9:16, :] = jnp.zeros((7, 3 * D), F32)
        gab_ref[:, 0:D] = red[R_SH_X:R_SH_X + 1, :] + red[R_SH_C:R_SH_C + 1, :]
        gab_ref[:, D:2 * D] = red[R_SC_X:R_SC_X + 1, :] + red[R_SC_C:R_SC_C + 1, :]
        gab_ref[:, 2 * D:3 * D] = red[R_GATE:R_GATE + 1, :]
        dmc_s[...] = jnp.broadcast_to(dmod_ref[8:9, :], (8, 3 * D))
        off = pl.multiple_of(me_ref[0] * nloc, 128)
        cpart_ref[...] = _dot_nt(dmc_s[:, pl.ds(off, nloc)], w_ref[...])
        c_start()
        c_forward()
        c_finish()
        m_finish()

    return _call(
        body, name="reduce_small",
        out_shape=[jax.ShapeDtypeStruct((PACK_ROWS, D), F32), jax.ShapeDtypeStruct((PACK128_ROWS, HD), F32)]
        + [jax.ShapeDtypeStruct((N_DEV,) + p.shape[1:], F32) for p in mat_parts]
        + [jax.ShapeDtypeStruct((N_DEV, 8, D), F32), jax.ShapeDtypeStruct((16, 3 * D), F32),
           jax.ShapeDtypeStruct((1, 3 * D), F32), jax.ShapeDtypeStruct((1, 1), F32)],
        in_specs=[pl.BlockSpec(memory_space=pltpu.SMEM)] + [VMEM] * (len(pieces) + nm + 1),
        out_specs=[VMEM] * (nm + 6),
        scratch_shapes=[pltpu.VMEM((PACK_ROWS, D), F32), pltpu.VMEM((PACK128_ROWS, HD), F32),
                        pltpu.VMEM((N_DEV, PACK_ROWS, D), F32), pltpu.VMEM((N_DEV, PACK128_ROWS, HD), F32)]
        + [pltpu.VMEM(p.shape[1:], F32) for p in mat_parts]
        + [pltpu.VMEM((8, D), F32), pltpu.VMEM((8, 3 * D), F32)] + _gather2_sems(2) + _gather2_sems(nm)
        + _gather2_sems(1),
        compiler_params=pltpu.CompilerParams(vmem_limit_bytes=VMEM_LIMIT, has_side_effects=True, collective_id=8),
    )(me, *[a for a, _ in pieces], *mat_parts, ada_w)


def _adamw_ada(c_all, c_ctx, dmod, w, m, v, me):
    nloc = w.shape[1]

    def body(me_ref, c_ref, cc_ref, dm_ref, w_ref, m_ref, v_ref, g_ref, d_ref, mo_ref, vo_ref):
        off = pl.multiple_of(me_ref[0] * nloc, 128)
        dm = dm_ref[:, pl.ds(off, nloc)]
        sx, _ = _silu_and_grad(c_ref[...])
        sc, _ = _silu_and_grad(cc_ref[...])
        g = _dot_tn(sx, dm[0:8, :]) + _dot_tn(jnp.broadcast_to(sc, (8, D)), dm[8:16, :])
        g_ref[...] = g
        d_ref[...], mo_ref[...], vo_ref[...] = _adamw(w_ref[...], g, m_ref[...], v_ref[...])

    sds = jax.ShapeDtypeStruct(w.shape, F32)
    return _call(
        body, name="adamw_ada_w", out_shape=[sds] * 4,
        in_specs=[pl.BlockSpec(memory_space=pltpu.SMEM)] + [VMEM] * 6, out_specs=[VMEM] * 4,
        compiler_params=_params(),
    )(me, c_all, c_ctx, dmod, w, m, v)


_SMALL = ("c_ctx", "ada_b", "norm_g", "conv_w", "conv_b", "lru_wa", "lru_ba", "lru_wx", "lru_bx", "lru_lambda",
          "sgu_ln_g", "sgu_ln_b", "sgu_w", "sgu_b", "final_g")


def _adamw_small(red, redq, mats, cparts, gab, ws, ms, vs, me):
    n = len(_SMALL)

    def body(me_ref, red_ref, redq_ref, wa_ref, wx_ref, sw_ref, cp_ref, gab_ref, *refs):
        w_refs, m_refs, v_refs = refs[:n], refs[n:2 * n], refs[2 * n:3 * n]
        outs = refs[3 * n:]
        off = pl.multiple_of(me_ref[0] * HD, 128)

        def row(r, k=1):
            return red_ref[r:r + k, :]

        cc = w_refs[0][...]
        dcc = cp_ref[0, 0:1, :]
        for i in range(1, N_DEV):
            dcc = dcc + cp_ref[i, 0:1, :]
        grads = dict(
            c_ctx=dcc * _silu_and_grad(cc)[1], ada_b=gab_ref[...], norm_g=row(R_NG_X) + row(R_NG_C),
            conv_w=red_ref[R_CW:R_CW + CONV_W, pl.ds(off, HD)], conv_b=row(R_CB),
            lru_wa=wa_ref[...], lru_ba=redq_ref[Q_BA:Q_BA + 2 * HEADS, :], lru_wx=wx_ref[...],
            lru_bx=redq_ref[Q_BX:Q_BX + 2 * HEADS, :], lru_lambda=red_ref[R_LAM:R_LAM + 2, pl.ds(off, HD)],
            sgu_ln_g=row(R_LN_G), sgu_ln_b=row(R_LN_B), sgu_w=sw_ref[...],
            sgu_b=redq_ref[Q_SGU_B:Q_SGU_B + HEADS, :], final_g=row(R_FINAL_G))
        for j, name in enumerate(_SMALL):
            g = grads[name]
            outs[j][...] = g
            outs[n + j][...], outs[2 * n + j][...], outs[3 * n + j][...] = _adamw(w_refs[j][...], g, m_refs[j][...],
                                                                                 v_refs[j][...])

    sds = [jax.ShapeDtypeStruct(ws[k].shape, F32) for k in _SMALL]
    outs = _call(
        body, name="adamw_small", out_shape=sds * 4,
        in_specs=[pl.BlockSpec(memory_space=pltpu.SMEM)] + [VMEM] * (7 + 3 * n), out_specs=[VMEM] * (4 * n),
        compiler_params=_params(),
    )(me, red, redq, *mats, cparts, gab, *[ws[k] for k in _SMALL], *[ms[k] for k in _SMALL],
      *[vs[k] for k in _SMALL])
    return [dict(zip(_SMALL, outs[i * n:(i + 1) * n])) for i in range(4)]


def kernel(x, c, ctx, c_ctx, ada_w, ada_b, norm_g, w_in, conv_w, conv_b, lru_wa, lru_ba, lru_wx, lru_bx, lru_lambda, sgu_ln_g, sgu_ln_b, sgu_w, sgu_b, w_out, final_g, loss_target, m_c_ctx, m_ada_w, m_ada_b, m_norm_g, m_w_in, m_conv_w, m_conv_b, m_lru_wa, m_lru_ba, m_lru_wx, m_lru_bx, m_lru_lambda, m_sgu_ln_g, m_sgu_ln_b, m_sgu_w, m_sgu_b, m_w_out, m_final_g, v_c_ctx, v_ada_w, v_ada_b, v_norm_g, v_w_in, v_conv_w, v_conv_b, v_lru_wa, v_lru_ba, v_lru_wx, v_lru_bx, v_lru_lambda, v_sgu_ln_g, v_sgu_ln_b, v_sgu_w, v_sgu_b, v_w_out, v_final_g):
    args = dict(locals())
    me = (4 * lax.axis_index("x") + 2 * lax.axis_index("y") + lax.axis_index("c")).astype(jnp.int32).reshape(1)
    xr, ctxr, tgt = x[0], ctx[0], loss_target[0]
    cc = c_ctx.reshape(1, D)
    nw = 2 * HEADS * HD
    view = dict(c_ctx=(1, D), ada_b=(1, 3 * D), norm_g=(1, D), conv_w=(CONV_W, HD), conv_b=(1, D), lru_wa=(nw, HD),
                lru_ba=(2 * HEADS, HD), lru_wx=(nw, HD), lru_bx=(2 * HEADS, HD), lru_lambda=(2, HD), sgu_ln_g=(1, D),
                sgu_ln_b=(1, D), sgu_w=(HEADS * CHUNK, CHUNK), sgu_b=(HEADS, CHUNK), final_g=(1, D))

    zx, hn, w_full, w_out_b, modx, modc, c_all, cw_full, lam_full = _front_project(
        xr, c, cc, ada_w[0], ada_b, norm_g, w_in[0], w_out[0], conv_w[0], lru_lambda[0], me)
    zc, hnc = _project(ctxr, modc, norm_g, w_full, D, LC, "project_ctx")
    ba, bx = lru_ba.reshape(view["lru_ba"]), lru_bx.reshape(view["lru_bx"])
    yl, wout_all = _lru_forward(zx, zc, cw_full, conv_b, lru_wa[0], lru_wx[0], ba, bx, lam_full, [w_out_b], ["ag"])
    wout_full = wout_all.reshape(D_MIX, D)
    dz, dyl, dxn, ycat, dob, dws, dbst, mvec = _mixer_loss(
        xr, tgt, zx, yl, modx, final_g.reshape(1, D), sgu_ln_g, sgu_ln_b, sgu_w[0], sgu_b[0].T, wout_full, ROWS)

    (wout_sums,) = _grad_w(ycat, dob, None, None, L, "grad_w_out", D, 0, 1, "rows", 1)
    (rest_sums,) = _grad_w(hn, dz, None, None, L, "grad_w_in_rest", 2 * W_IN_SHARD, 1, 3, "cols", 2)
    dz, dxac, dwa, dwx, dba, dbx, dlam, dcw, dcb, win_parts, wout_parts, _, _ = _lru_backward(
        zx, zc, dyl, dz, cw_full, conv_b, lru_wa[0], lru_wx[0], ba, bx, lam_full, [rest_sums, wout_sums],
        first_chips=[1, 0])
    mats = [dwa.reshape(N_DEV, nw // N_DEV, HD), dwx.reshape(N_DEV, nw // N_DEV, HD), dws]
    first_sums, *mat_parts = _grad_w(hn, dz, hnc, dxac, L, "grad_w_in_first", 2 * W_IN_SHARD, 0, 1, "cols", 3,
                                     riders=mats)[:4]
    gx, xvec, win_parts = _grad_rows(
        xr, dz, w_full, modx, norm_g, dxn, D_IN, ROWS, "grad_rows_x", chip_sums=[first_sums], first_chips=[0],
        dests=[win_parts])[:3]
    (cvec,) = _grad_rows(ctxr, dxac, w_full, modc, norm_g, None, D, LC, "grad_rows_ctx")
    red, redq, *rest = _reduce_small(
        [(mvec, 5), (xvec, 3), (cvec, 3), (dlam, 2), (dcw, CONV_W), (dcb, 1)],
        [(dba, 2 * HEADS), (dbx, 2 * HEADS), (dbst.T, HEADS)], mat_parts, ada_w[0], me)
    mats_all, (cparts, dmod, gab, loss) = rest[:3], rest[3:]

    g_w_in, d_w_in, nm_w_in, nv_w_in = _adamw_reduced(win_parts, w_in[0], m_w_in[0], v_w_in[0], 2 * ROWS, "adamw_w_in")
    g_w_out, d_w_out, nm_w_out, nv_w_out = _adamw_reduced(wout_parts, w_out[0], m_w_out[0], v_w_out[0], ROWS,
                                                          "adamw_w_out")
    g_ada, d_ada, nm_ada, nv_ada = _adamw_ada(c_all, cc, dmod, ada_w[0], m_ada_w[0], v_ada_w[0], me)
    ws = {k: args[k].reshape(view[k]) for k in _SMALL}
    ms = {k: args["m_" + k].reshape(view[k]) for k in _SMALL}
    vs = {k: args["v_" + k].reshape(view[k]) for k in _SMALL}
    small = _adamw_small(red, redq, [m.reshape(-1, HD) for m in mats_all], cparts, gab, ws, ms, vs, me)
    big = dict(w_in=(g_w_in, d_w_in, nm_w_in, nv_w_in), w_out=(g_w_out, d_w_out, nm_w_out, nv_w_out),
               ada_w=(g_ada, d_ada, nm_ada, nv_ada))

    loss = loss.reshape(())
    names = ("c_ctx", "ada_w", "ada_b", "norm_g", "w_in", "conv_w", "conv_b", "lru_wa", "lru_ba", "lru_wx", "lru_bx",
             "lru_lambda", "sgu_ln_g", "sgu_ln_b", "sgu_w", "sgu_b", "w_out", "final_g")
    outs = [loss, gx.reshape(x.shape)]
    for kind in range(4):
        for k in names:
            val = big[k][kind] if k in big else small[kind][k]
            outs.append(val.reshape(args[k].shape))
    return tuple(outs)
```

```python
import jax
import jax.numpy as jnp
from jax import lax
from jax.experimental import pallas as pl
from jax.experimental.pallas import tpu as pltpu

F32 = jnp.float32
BF16 = jnp.bfloat16

N_DEV = 8
D = 1024
L = 2048
LC = 256
HEADS = 8
HD = 128
CHUNK = 128
D_IN = 5 * D
W_IN_SHARD = D_IN // N_DEV
ROWS = 256
D_MIX = 2 * D
CONV_W = 4
LRU_C = 8.0
NORM_EPS = 1e-6
LN_EPS = 1e-5
ADAM_LR, ADAM_B1, ADAM_B2, ADAM_EPS, ADAM_WD, ADAM_STEP = 0.001, 0.9, 0.999, 1e-08, 0.01, 10

VMEM_LIMIT = 56 * 1024 * 1024

HBM = pl.BlockSpec(memory_space=pltpu.HBM)
VMEM = pl.BlockSpec(memory_space=pltpu.VMEM)
MESH = pl.DeviceIdType.MESH


def _call(body, **kw):
    return pl.pallas_call(body, **kw)


def _params(*sem):
    return pltpu.CompilerParams(dimension_semantics=sem, vmem_limit_bytes=VMEM_LIMIT)


def _sigmoid(x):
    return 0.5 * jnp.tanh(0.5 * x) + 0.5


def _silu_and_grad(x):
    s = _sigmoid(x)
    return x * s, s * (1.0 + x * (1.0 - s))


_G0 = 0.7978845608028654
_G1 = 0.044715


def _gelu_and_grad(x):
    x2 = x * x
    t = jnp.tanh(_G0 * (x + _G1 * x * x2))
    cdf = 0.5 * (1.0 + t)
    return x * cdf, cdf + 0.5 * x * (1.0 - t * t) * (_G0 * (1.0 + 3.0 * _G1 * x2))


def _softplus(z):
    t = jnp.exp(-jnp.abs(z))
    u = 1.0 + t
    log1p = jnp.where(u == 1.0, t, jnp.log(u) * t / jnp.where(u == 1.0, 1.0, u - 1.0))
    return jnp.maximum(z, 0.0) + log1p


def _dot(a, b):
    return jnp.dot(a, b, preferred_element_type=F32)


def _dot_nt(a, b):
    return lax.dot_general(a, b, (((1,), (1,)), ((), ())), preferred_element_type=F32)


def _dot_tn(a, b):
    return lax.dot_general(a, b, (((0,), (0,)), ((), ())), preferred_element_type=F32)


def _rows(shape):
    return lax.broadcasted_iota(jnp.int32, shape, 0)


def _gather2_shapes(arrays, modes):
    return [jax.ShapeDtypeStruct((N_DEV,) + a.shape if m == "ag" else (a.shape[0], N_DEV * a.shape[1]), a.dtype)
            for a, m in zip(arrays, modes)]


def _gather2_sems(n):
    return [pltpu.SemaphoreType.DMA((n, N_DEV - 1)), pltpu.SemaphoreType.DMA((n, N_DEV - 1)),
            pltpu.SemaphoreType.DMA((n,))]


def _barrier(peers):
    sem = pltpu.get_barrier_semaphore()
    for peer in peers:
        pl.semaphore_signal(sem, inc=1, device_id=peer, device_id_type=MESH)
    pl.semaphore_wait(sem, len(peers))


def _gather2_ops(ins, outs, modes, send_sems, recv_sems, local_sems, barrier=False):
    n = len(ins)
    x, y, c = lax.axis_index("x"), lax.axis_index("y"), lax.axis_index("c")
    me, sibling = (x, y, c), (x, y, 1 - c)
    chips = [(x ^ (k >> 1), y ^ (k & 1)) for k in (1, 2, 3)]

    def slot(j, px, py, pc):
        dev = 4 * px + 2 * py + pc
        if modes[j] == "agc":
            w = ins[j].shape[1]
            return outs[j].at[:, pl.ds(pl.multiple_of(dev * w, 128), w)]
        return outs[j].at[dev]

    def copy(j, k, block, to, src=None):
        return pltpu.make_async_remote_copy(
            src_ref=slot(j, *block) if src is None else src, dst_ref=slot(j, *block),
            send_sem=send_sems.at[j, k], recv_sem=recv_sems.at[j, k], device_id=to, device_id_type=MESH)

    def own(j):
        return pltpu.make_async_copy(ins[j], slot(j, *me), local_sems.at[j])

    def first(j):
        return [copy(j, 0, me, sibling, src=ins[j])] + [copy(j, 1 + i, me, (*chip, c), src=ins[j])
                                                        for i, chip in enumerate(chips)]

    def passed(j, i):
        return copy(j, 4 + i, (*chips[i], c), sibling)

    def start():
        if barrier:
            _barrier([sibling] + [(*chip, c) for chip in chips])
        for j in range(n):
            own(j).start()
            for cp in first(j):
                cp.start()

    def forward():
        for i, chip in enumerate(chips):
            for j in range(n):
                copy(j, 1 + i, (*chip, c), me).wait_recv()
                passed(j, i).start()

    def finish():
        for j in range(n):
            copy(j, 0, sibling, me).wait_recv()
            for i, chip in enumerate(chips):
                copy(j, 4 + i, (*chip, 1 - c), me).wait_recv()
            for cp in first(j) + [passed(j, i) for i in range(3)]:
                cp.wait_send()
            own(j).wait()

    return start, forward, finish


def _sibling_barrier():
    sem = pltpu.get_barrier_semaphore()
    sibling = (lax.axis_index("x"), lax.axis_index("y"), 1 - lax.axis_index("c"))
    pl.semaphore_signal(sem, inc=1, device_id=sibling, device_id_type=MESH)
    pl.semaphore_wait(sem, 1)


def _chips_sems(n):
    return [pltpu.SemaphoreType.DMA((n, 6)), pltpu.SemaphoreType.DMA((n, 6)), pltpu.SemaphoreType.DMA((n,))]


def _chips_stage_shapes(chip_sums):
    return [jax.ShapeDtypeStruct((2, a.shape[1] // 2, a.shape[2]), a.dtype) for a in chip_sums]


def _chips_ops(ins, outs, stages, send_sems, recv_sems, local_sems, first_chips=None, barrier=False):
    x, y, c = lax.axis_index("x"), lax.axis_index("y"), lax.axis_index("c")
    qm = 2 * x + y
    first_chips = first_chips or [0] * len(ins)

    def owns(j, chip):
        lo, cnt = first_chips[j], ins[j].shape[0]
        if lo == 0 and cnt == 4:
            return None
        return jnp.logical_and(chip >= lo, chip < lo + cnt)

    def guarded(cond, fn):
        if cond is None:
            fn()
        else:
            pl.when(cond)(fn)

    def slot(j, chip):
        return jnp.clip(chip - first_chips[j], 0, ins[j].shape[0] - 1)

    def half(j, i):
        h = ins[j].shape[1] // 2
        return pl.ds(i * h, h)

    def copy(j, sem, src, dst, k):
        return pltpu.make_async_remote_copy(
            src_ref=src, dst_ref=dst, send_sem=send_sems.at[j, sem], recv_sem=recv_sems.at[j, sem],
            device_id=(x ^ (k >> 1), y ^ (k & 1), c), device_id_type=MESH)

    def direct(j, k):
        return copy(j, k - 1, ins[j].at[slot(j, qm ^ k)], outs[j].at[qm], k)

    def first_hop(j, k):
        return copy(j, 1 + k, ins[j].at[slot(j, qm ^ 3), half(j, k - 1)], stages[j].at[k - 1], k)

    def second_hop(j, k):
        return copy(j, 3 + k, stages[j].at[2 - k], outs[j].at[qm ^ (3 - k), half(j, 2 - k)], k)

    def local(j):
        return pltpu.make_async_copy(ins[j].at[slot(j, qm)], outs[j].at[qm], local_sems.at[j])

    def start():
        if barrier:
            _barrier([(x ^ (k >> 1), y ^ (k & 1), c) for k in (1, 2)])
        for j in range(len(ins)):
            for k in (1, 2):
                guarded(owns(j, qm ^ 3), lambda j=j, k=k: first_hop(j, k).start())
        for j in range(len(ins)):
            for k in (1, 2):
                guarded(owns(j, qm ^ k), lambda j=j, k=k: direct(j, k).start())
            guarded(owns(j, qm), lambda j=j: local(j).start())

    def forward():
        for j in range(len(ins)):
            for k in (1, 2):
                def pass_on(j=j, k=k):
                    first_hop(j, 3 - k).wait_recv()
                    second_hop(j, k).start()
                guarded(owns(j, qm ^ k), pass_on)

    def finish():
        for j in range(len(ins)):
            for k in (1, 2):
                guarded(owns(j, qm ^ k), lambda j=j, k=k: direct(j, k).wait_send())
                guarded(owns(j, qm ^ k), lambda j=j, k=k: second_hop(j, k).wait_send())
                guarded(owns(j, qm ^ 3), lambda j=j, k=k: first_hop(j, k).wait_send())
                guarded(owns(j, qm), lambda j=j, k=k: direct(j, k).wait_recv())
                guarded(owns(j, qm), lambda j=j, k=k: second_hop(j, k).wait_recv())
            guarded(owns(j, qm), lambda j=j: local(j).wait())

    return start, forward, finish


ARRIVAL = (0, 1, 2, 4, 3, 5, 6, 7)


def _front_project(xr, c, c_ctx, ada_w, ada_b, ng, w_in, w_out, cw, lam, me):
    nloc = ada_w.shape[1]
    ws = W_IN_SHARD
    arrival = jnp.asarray(ARRIVAL, jnp.int32)

    def body(me_ref, arr_ref, x_ref, c_ref, cc_ref, aw_ref, ab_ref, ng_ref, win_ref, wout_ref, cw_ref, lam_ref,
             z_ref, hn_ref, wfull_ref, woutb_ref, modx_ref, modc_ref, call_ref, cwf_ref, lamf_ref,
             wv, call_s, part_s, parts_s, w_send, w_recv, hbm_sems, s_send, s_recv, g_send, g_recv, g_local,
             x_v, x_sem):
        t = pl.program_id(0)
        x_load = pltpu.make_async_copy(x_ref, x_v, x_sem)
        x, y, cidx = lax.axis_index("x"), lax.axis_index("y"), lax.axis_index("c")
        me_i = me_ref[0]
        sibling = (x, y, 1 - cidx)
        chips = [(x ^ (k >> 1), y ^ (k & 1)) for k in (1, 2, 3)]
        g_start, g_pass, g_finish = _gather2_ops([cw_ref, lam_ref], [cwf_ref, lamf_ref], ["agc", "agc"],
                                                 g_send, g_recv, g_local)

        def shard_copy(k, px, py, pc, to, half=None):
            slot = wv.at[4 * px + 2 * py + pc]
            if half is not None:
                slot = slot.at[pl.ds(half * (D // 2), D // 2), :]
            return pltpu.make_async_remote_copy(src_ref=slot, dst_ref=slot, send_sem=w_send.at[k],
                                                recv_sem=w_recv.at[k], device_id=to, device_id_type=MESH)

        def small_gather(src, my_slot, stage):
            copies = []
            for k in range(1, N_DEV):
                peer = (x ^ (k >> 2), y ^ ((k >> 1) & 1), cidx ^ (k & 1))
                cp = pltpu.make_async_remote_copy(src_ref=src, dst_ref=my_slot, send_sem=s_send.at[stage, k - 1],
                                                  recv_sem=s_recv.at[stage, k - 1], device_id=peer,
                                                  device_id_type=MESH)
                cp.start()
                copies.append(cp)
            pltpu.sync_copy(src, my_slot)
            return copies

        def finish_small(copies):
            for cp in copies:
                cp.wait()

        def to_neighbours(half):
            for i in (0, 1):
                shard_copy(1 + i, x, y, cidx, (*chips[i], cidx), half=half).start()

        @pl.when(t == 0)
        def _():
            _barrier([(x ^ (k >> 2), y ^ ((k >> 1) & 1), cidx ^ (k & 1)) for k in range(1, N_DEV)])
            g_start()
            x_load.start()
            wv[me_i] = win_ref[...].astype(BF16)
            woutb_ref[...] = wout_ref[...].astype(BF16)
            shard_copy(0, x, y, cidx, sibling).start()
            conds_sent = small_gather(c_ref, call_s.at[pl.ds(me_i, 1), :], 0)
            to_neighbours(0)
            finish_small(conds_sent)
            call_ref[...] = call_s[...]
            off = pl.multiple_of(me_i * nloc, 128)
            b = ab_ref[:, pl.ds(off, nloc)]
            w = aw_ref[...]
            sx, _ = _silu_and_grad(call_s[...])
            sc, _ = _silu_and_grad(jnp.broadcast_to(cc_ref[...], (8, D)))
            part_s[0:8, :] = _dot(sx, w) + b
            part_s[8:16, :] = _dot(sc, w) + b
            parts_sent = small_gather(part_s, parts_s.at[me_i], 1)
            to_neighbours(1)
            finish_small(parts_sent)
            mine = _rows((16, nloc)) == me_i
            for j in range(N_DEV):
                pj = parts_s[j]
                modx_ref[:, j * nloc:(j + 1) * nloc] = jnp.sum(jnp.where(mine, pj, 0.0), axis=0, keepdims=True)
                modc_ref[:, j * nloc:(j + 1) * nloc] = pj[8:9, :]
            shift, scale1, ngv = modx_ref[:, 0:D], 1.0 + modx_ref[:, D:2 * D], ng_ref[...]
            x_load.wait()
            for r in range(L // ROWS):
                rsl = slice(r * ROWS, (r + 1) * ROWS)
                xv = x_v[rsl, :]
                rs = lax.rsqrt(jnp.mean(xv * xv, axis=-1, keepdims=True) + NORM_EPS)
                hn_ref[rsl, :] = ((xv * rs * ngv) * scale1 + shift).astype(BF16)

        @pl.when(t == 1)
        def _():
            shard_copy(0, x, y, 1 - cidx, sibling).wait_recv()
            g_pass()

        for i in (0, 1):
            @pl.when(t == ARRIVAL.index((2, 4)[i]))
            def _(i=i):
                shard_copy(1 + i, *chips[i], cidx, sibling).wait_recv()
                shard_copy(4 + i, *chips[i], cidx, sibling).start()
                shard_copy((7, 3)[i], *chips[i], cidx, (*chips[1 - i], cidx), half=i).start()

        @pl.when(t == ARRIVAL.index(6))
        def _():
            shard_copy(3, *chips[2], cidx, sibling, half=1).wait_recv()
            shard_copy(7, *chips[2], cidx, sibling, half=0).wait_recv()
            shard_copy(6, *chips[2], cidx, sibling).start()

        for i in range(3):
            @pl.when(t == ARRIVAL.index((3, 5, 7)[i]))
            def _(i=i):
                shard_copy(4 + i, *chips[i], 1 - cidx, sibling).wait_recv()

        @pl.when(t == 2)
        def _():
            g_finish()

        dev = me_i ^ arr_ref[t]
        for r in range(L // (2 * ROWS)):
            rsl = slice(r * 2 * ROWS, (r + 1) * 2 * ROWS)
            z_ref[rsl, :] = _dot(hn_ref[rsl, :], wv[dev])
        col = pl.ds(pl.multiple_of(dev * ws, 128), ws)
        pltpu.make_async_copy(wv.at[dev], wfull_ref.at[:, col], hbm_sems.at[t]).start()

        @pl.when(t == N_DEV - 1)
        def _():
            for k in (0, 1, 2, 4, 5, 6):
                shard_copy(k, x, y, cidx, sibling).wait_send()
            for k in (3, 7):
                shard_copy(k, x, y, cidx, sibling, half=0).wait_send()
            for s in range(N_DEV):
                pltpu.make_async_copy(wv.at[0], wfull_ref.at[:, pl.ds(0, ws)], hbm_sems.at[s]).wait()

    const = lambda *shape: pl.BlockSpec(shape, lambda t, m, a: (0,) * len(shape))
    once = lambda *shape: pl.BlockSpec(shape, lambda t, m, a: (0,) * len(shape), pipeline_mode=pl.Buffered(1))
    return _call(
        body, name="front_project",
        out_shape=[jax.ShapeDtypeStruct((L, D_IN), F32), jax.ShapeDtypeStruct((L, D), BF16),
                   jax.ShapeDtypeStruct((D, D_IN), BF16), jax.ShapeDtypeStruct(w_out.shape, BF16),
                   jax.ShapeDtypeStruct((1, 3 * D), F32), jax.ShapeDtypeStruct((1, 3 * D), F32),
                   jax.ShapeDtypeStruct((N_DEV, D), F32), jax.ShapeDtypeStruct((CONV_W, D), F32),
                   jax.ShapeDtypeStruct((2, D), F32)],
        grid_spec=pltpu.PrefetchScalarGridSpec(
            num_scalar_prefetch=2, grid=(N_DEV,),
            in_specs=[HBM, const(1, D), const(1, D), once(D, nloc), const(1, 3 * D), const(1, D),
                      once(D, ws), once(*w_out.shape), HBM, HBM],
            out_specs=[pl.BlockSpec((L, ws), lambda t, m, a: (0, m[0] ^ a[t])), const(L, D), HBM,
                       const(*w_out.shape),
                       const(1, 3 * D), const(1, 3 * D), const(N_DEV, D), HBM, HBM],
            scratch_shapes=[pltpu.VMEM((N_DEV, D, ws), BF16), pltpu.VMEM((N_DEV, D), F32), pltpu.VMEM((16, nloc), F32),
                            pltpu.VMEM((N_DEV, 16, nloc), F32), pltpu.SemaphoreType.DMA((8,)),
                            pltpu.SemaphoreType.DMA((8,)), pltpu.SemaphoreType.DMA((N_DEV,)),
                            pltpu.SemaphoreType.DMA((2, N_DEV - 1)), pltpu.SemaphoreType.DMA((2, N_DEV - 1))]
            + _gather2_sems(2) + [pltpu.VMEM((L, D), F32), pltpu.SemaphoreType.DMA(())]),
        compiler_params=pltpu.CompilerParams(dimension_semantics=("arbitrary",), vmem_limit_bytes=VMEM_LIMIT,
                                             has_side_effects=True, collective_id=10),
    )(me, arrival, pltpu.with_memory_space_constraint(xr, pltpu.HBM), c, c_ctx, ada_w, ada_b, ng, w_in, w_out,
      pltpu.with_memory_space_constraint(cw, pltpu.HBM), pltpu.with_memory_space_constraint(lam, pltpu.HBM))


def _project(xr, mod, ng, w, ncols, tm, name):
    rows = xr.shape[0]

    def body(x_ref, sh_ref, sc_ref, ng_ref, w_ref, z_ref, hn_ref):
        x = x_ref[...]
        rs = lax.rsqrt(jnp.mean(x * x, axis=-1, keepdims=True) + NORM_EPS)
        hn = (x * rs * ng_ref[...]) * (1.0 + sc_ref[...]) + sh_ref[...]
        hb = hn.astype(BF16)
        hn_ref[...] = hb
        for n in range(ncols // D):
            z_ref[:, n * D:(n + 1) * D] = _dot(hb, w_ref[:, n * D:(n + 1) * D])

    vec = pl.BlockSpec((1, D), lambda i: (0, 0))
    return _call(
        body, name=name, grid=(rows // tm,),
        out_shape=[jax.ShapeDtypeStruct((rows, ncols), F32), jax.ShapeDtypeStruct((rows, D), BF16)],
        in_specs=[pl.BlockSpec((tm, D), lambda i: (i, 0)), vec, pl.BlockSpec((1, D), lambda i: (0, 1)), vec,
                  pl.BlockSpec((D, ncols), lambda i: (0, 0), pipeline_mode=pl.Buffered(1))],
        out_specs=[pl.BlockSpec((tm, ncols), lambda i: (i, 0)), pl.BlockSpec((tm, D), lambda i: (i, 0))],
        compiler_params=_params("arbitrary"),
    )(xr, mod, mod, ng, w)


def _scan_pair(af_ref, uf_ref, hf_ref, h0f, ab_ref, ub_ref, hb_ref, h0b, t_len):
    span = 8 * SCAN_BLOCKS
    nit = t_len // span
    rows = _rows((8, HD))

    def local_scan(a, b, forward):
        for s in (1, 2, 4):
            sh = s if forward else 8 - s
            m = rows >= s if forward else rows < 8 - s
            b = a * jnp.where(m, pltpu.roll(b, sh, 0), 0.0) + b
            a = a * jnp.where(m, pltpu.roll(a, sh, 0), 1.0)
        return a, b

    def span_scan(a_ref, u_ref, h_ref, off, carry, forward):
        order = range(SCAN_BLOCKS) if forward else range(SCAN_BLOCKS - 1, -1, -1)
        last = slice(7, 8) if forward else slice(0, 1)
        for q in order:
            rs = pl.ds(off + 8 * q, 8)
            a, b = local_scan(a_ref[rs, :], u_ref[rs, :], forward)
            h_ref[rs, :] = b + a * carry
            carry = a[last, :] * carry + b[last, :]
        return carry

    def body(k, carry):
        cf, cb = carry
        cf = span_scan(af_ref, uf_ref, hf_ref, pl.multiple_of(k * span, span), cf, True)
        cb = span_scan(ab_ref, ub_ref, hb_ref, pl.multiple_of((nit - 1 - k) * span, span), cb, False)
        return cf, cb

    return lax.fori_loop(0, nit, body, (h0f, h0b))


SCAN_BLOCKS = 16


def _shifted(pad_ref, x, offsets, before=0.0, after=0.0):
    n = x.shape[0]
    pad_ref[0:8, :] = jnp.broadcast_to(jnp.asarray(before, F32), (8, x.shape[1]))
    pad_ref[8:8 + n, :] = x
    pad_ref[8 + n:16 + n, :] = jnp.broadcast_to(jnp.asarray(after, F32), (8, x.shape[1]))
    return [pad_ref[8 + o:8 + o + n, :] for o in offsets]


def _conv(xa, cw, cb, pad_ref):
    xm1, xp1, xp2 = _shifted(pad_ref, xa, (-1, 1, 2))
    return xm1 * cw[0:1, :] + xa * cw[1:2, :] + xp1 * cw[2:3, :] + xp2 * cw[3:4, :] + cb


def _gates(xc, wa, wx, ba, bx, nsp):
    xb = xc.astype(BF16)
    r = _sigmoid(_dot(xb, wa) + ba)
    i = _sigmoid(_dot(xb, wx) + bx)
    log_a = r * nsp
    a = jnp.exp(log_a)
    g2 = jnp.tanh(log_a) * (-1.0 - a * a)
    rg = lax.rsqrt(jnp.maximum(g2, 1e-30))
    return r, i, a, g2 * rg, rg


def _lru_param_specs():
    h4 = pl.BlockSpec((2, 1, HD, HD), lambda h: (0, h, 0, 0))
    v2 = pl.BlockSpec((2, HD), lambda h: (0, h))
    b16 = pl.BlockSpec((2 * HEADS, HD), lambda h: (0, 0))
    return dict(
        xa=pl.BlockSpec((L, HD), lambda h: (0, h)), xac=pl.BlockSpec((LC, HD), lambda h: (0, h)),
        cw=pl.BlockSpec((CONV_W, HD), lambda h: (0, h)), cb=pl.BlockSpec((1, HD), lambda h: (0, h)), h4=h4, v2=v2,
        b16=b16)


def _bias_row(ref, d):
    mask = _rows((2 * HEADS, HD)) == d * HEADS + pl.program_id(0)
    return jnp.sum(jnp.where(mask, ref[...], 0.0), axis=0, keepdims=True), mask


def _lru_forward(zx, zc, cw, cb, wa, wx, ba, bx, lam, gather, gather_modes):
    ng_ = len(gather)

    def body(xa_ref, xac_ref, cw_ref, cb_ref, wa_ref, wx_ref, ba_ref, bx_ref, lam_ref, *rest):
        yl_ref = rest[ng_]
        af, uf, hf, ab, ub, hb, pad_s = rest[2 * ng_ + 1:2 * ng_ + 8]
        start, pass_on, finish = _gather2_ops(rest[:ng_], rest[ng_ + 1:2 * ng_ + 1], gather_modes,
                                              *rest[2 * ng_ + 8:], barrier=True)
        pl.when(pl.program_id(0) == 0)(start)
        pl.when(pl.program_id(0) == HEADS // 2)(pass_on)
        pl.when(pl.program_id(0) == HEADS - 1)(finish)
        cwv, cbv = cw_ref[...], cb_ref[...]
        nsp = (-LRU_C) * _softplus(-lam_ref[...])

        def forward(xa, t_len, h0f, h0b):
            xc = _conv(xa, cwv, cbv, pad_s)
            for d, (a_ref, u_ref) in enumerate(((af, uf), (ab, ub))):
                _, i, a, gamma, _ = _gates(xc, wa_ref[d, 0].astype(BF16), wx_ref[d, 0].astype(BF16),
                                           _bias_row(ba_ref, d)[0], _bias_row(bx_ref, d)[0], nsp[d:d + 1, :])
                a_ref[0:t_len, :] = a
                u_ref[0:t_len, :] = gamma * (i * xc)
            return _scan_pair(af, uf, hf, h0f, ab, ub, hb, h0b, t_len)

        z = jnp.zeros((1, HD), F32)
        h0f, h0b = forward(xac_ref[...], LC, z, z)
        forward(xa_ref[...], L, h0f, h0b)
        yl_ref[...] = hf[...] + hb[...]

    s = _lru_param_specs()
    return _call(
        body, name="lru_forward", grid=(HEADS,),
        out_shape=[jax.ShapeDtypeStruct((L, D), F32)] + _gather2_shapes(gather, gather_modes),
        in_specs=[s["xa"], s["xac"], s["cw"], s["cb"], s["h4"], s["h4"], s["b16"], s["b16"], s["v2"]] + [HBM] * ng_,
        out_specs=[pl.BlockSpec((L, HD), lambda h: (0, h))] + [HBM] * ng_,
        scratch_shapes=[pltpu.VMEM((L, HD), F32)] * 6 + [pltpu.VMEM((L + 16, HD), F32)] + _gather2_sems(ng_),
        compiler_params=pltpu.CompilerParams(dimension_semantics=("arbitrary",), vmem_limit_bytes=VMEM_LIMIT,
                                             has_side_effects=True, collective_id=5),
    )(zx, zc, cw, cb, wa, wx, ba, bx, lam, *[pltpu.with_memory_space_constraint(a, pltpu.HBM) for a in gather])


def _lru_backward(zx, zc, dyl, dz, cw, cb, wa, wx, ba, bx, lam, chip_sums, first_chips=None):
    nr = len(chip_sums)

    def body(xa_ref, xac_ref, dyl_ref, dz_in, cw_ref, cb_ref, wa_ref, wx_ref, ba_ref, bx_ref, lam_ref, *rest):
        (dxa_ref, dxac_ref, dwa_ref, dwx_ref, dba_ref, dbx_ref, dlam_ref, dcw_ref,
         dcb_ref) = rest[nr:nr + 9]
        main_s, ctx_s, pad_s = rest[3 * nr + 9:3 * nr + 12]
        if nr:
            start, forward, finish = _chips_ops(rest[:nr], rest[nr + 9:2 * nr + 9], rest[2 * nr + 9:3 * nr + 9],
                                                *rest[3 * nr + 12:], first_chips=first_chips, barrier=True)
            pl.when(pl.program_id(0) == 0)(start)
            pl.when(pl.program_id(0) == HEADS // 2)(forward)
            pl.when(pl.program_id(0) == HEADS - 1)(finish)
        del dz_in

        @pl.when(pl.program_id(0) == 0)
        def _():
            dba_ref[...] = jnp.zeros_like(dba_ref)
            dbx_ref[...] = jnp.zeros_like(dbx_ref)

        cwv, cbv = cw_ref[...], cb_ref[...]
        lamv = lam_ref[...]
        sp = _softplus(-lamv)
        nsp = (-LRU_C) * sp
        z = jnp.zeros((1, HD), F32)

        def wmat(ref, d):
            return ref[d, 0].astype(BF16)

        def workspace(s):
            return dict(a=(s.at[0], s.at[1]), u=(s.at[2], s.at[3]), h=(s.at[4], s.at[5]), rho=(s.at[6], s.at[7]),
                        saved=(tuple(s.at[8 + k] for k in range(4)), tuple(s.at[12 + k] for k in range(4))),
                        xc=s.at[16])

        def forward(ws, xa, t_len, h0f, h0b):
            xc = _conv(xa, cwv, cbv, pad_s)
            ws["xc"][...] = xc
            for d in (0, 1):
                vals = _gates(xc, wmat(wa_ref, d), wmat(wx_ref, d), _bias_row(ba_ref, d)[0],
                              _bias_row(bx_ref, d)[0], nsp[d:d + 1, :])
                r, i, a, gamma, rg = vals
                ws["a"][d][...] = a
                ws["u"][d][...] = gamma * (i * xc)
                for ref, val in zip(ws["saved"][d], (r, i, gamma, rg)):
                    ref[...] = val
            return _scan_pair(ws["a"][0], ws["u"][0], ws["h"][0], h0f, ws["a"][1], ws["u"][1], ws["h"][1], h0b,
                              t_len)

        def backward(ws, xa, t_len, h0f, h0b, dhf, dhb, first):
            xc = ws["xc"][...]
            (af, ab), (uf, ub), (hf, hb), (rf, rb) = ws["a"], ws["u"], ws["h"], ws["rho"]
            uf[...] = ab[...] * dhb
            ub[...] = af[...] * dhf
            rho_b_last, rho_f_first = _scan_pair(ab, uf, rb, z, af, ub, rf, z, t_len)
            dxc = jnp.zeros((t_len, HD), F32)
            dsp = []
            for d in (0, 1):
                r, i, gamma, rg = (ref[...] for ref in ws["saved"][d])
                a = ws["a"][d][...]
                if d == 0:
                    lam_t = dhf + _shifted(pad_s, rf[...], (1,))[0]
                    h_prev = _shifted(pad_s, hf[...], (-1,), before=h0f)[0]
                else:
                    lam_t = dhb + _shifted(pad_s, rb[...], (-1,))[0]
                    h_prev = _shifted(pad_s, hb[...], (1,), after=h0b)[0]
                da = lam_t * h_prev
                lx = lam_t * xc
                d_i = lx * gamma
                d_gamma = lx * i
                dxc = dxc + lam_t * (gamma * i)
                d_log_a = a * (da - d_gamma * (a * rg))
                dsp.append(jnp.sum(d_log_a * r, axis=0, keepdims=True) * (-LRU_C))
                d_pre_r = d_log_a * nsp[d:d + 1, :] * (r * (1.0 - r))
                d_pre_i = d_i * (i * (1.0 - i))
                prb, pib, xb = d_pre_r.astype(BF16), d_pre_i.astype(BF16), xc.astype(BF16)
                dxc = dxc + _dot_nt(prb, wmat(wa_ref, d)) + _dot_nt(pib, wmat(wx_ref, d))
                g_wa, g_wx = _dot_tn(xb, prb), _dot_tn(xb, pib)
                g_ba = jnp.sum(d_pre_r, axis=0, keepdims=True)
                g_bx = jnp.sum(d_pre_i, axis=0, keepdims=True)
                mask = _bias_row(ba_ref, d)[1]
                dba_ref[...] += jnp.where(mask, g_ba, 0.0)
                dbx_ref[...] += jnp.where(mask, g_bx, 0.0)
                if first:
                    dwa_ref[d, 0] = g_wa
                    dwx_ref[d, 0] = g_wx
                else:
                    dwa_ref[d, 0] += g_wa
                    dwx_ref[d, 0] += g_wx
            g_lam = jnp.concatenate(dsp, axis=0) * (-_sigmoid(-lamv))
            dm1, dp1, dm2 = _shifted(pad_s, dxc, (-1, 1, -2))
            dxa = dp1 * cwv[0:1, :] + dxc * cwv[1:2, :] + dm1 * cwv[2:3, :] + dm2 * cwv[3:4, :]
            xm1, xp1, xp2 = _shifted(pad_s, xa, (-1, 1, 2))
            g_cw = jnp.concatenate([jnp.sum(dxc * v, axis=0, keepdims=True) for v in (xm1, xa, xp1, xp2)], axis=0)
            g_cb = jnp.sum(dxc, axis=0, keepdims=True)
            if first:
                dlam_ref[...] = g_lam
                dcw_ref[...] = g_cw
                dcb_ref[...] = g_cb
            else:
                dlam_ref[...] += g_lam
                dcw_ref[...] += g_cw
                dcb_ref[...] += g_cb
            return dxa, rho_f_first, rho_b_last

        ws_x, ws_c = workspace(main_s), workspace(ctx_s)
        h0f, h0b = forward(ws_c, xac_ref[...], LC, z, z)
        forward(ws_x, xa_ref[...], L, h0f, h0b)
        dh = dyl_ref[...]
        dxa, dh0f, dh0b = backward(ws_x, xa_ref[...], L, h0f, h0b, dh, dh, True)
        dxa_ref[...] = dxa.astype(BF16)
        rc = _rows((LC, HD))
        dxac, _, _ = backward(ws_c, xac_ref[...], LC, z, z, jnp.where(rc == LC - 1, dh0f, 0.0),
                              jnp.where(rc == 0, dh0b, 0.0), False)
        dxac_ref[...] = dxac.astype(BF16)

    s = _lru_param_specs()
    col = lambda r: pl.BlockSpec((r, HD), lambda h: (0, h))
    return _call(
        body, name="lru_backward", grid=(HEADS,),
        out_shape=[jax.ShapeDtypeStruct((L, D_IN), BF16), jax.ShapeDtypeStruct((LC, D), BF16),
                   jax.ShapeDtypeStruct((2, HEADS, HD, HD), F32), jax.ShapeDtypeStruct((2, HEADS, HD, HD), F32),
                   jax.ShapeDtypeStruct((2 * HEADS, HD), F32), jax.ShapeDtypeStruct((2 * HEADS, HD), F32),
                   jax.ShapeDtypeStruct((2, D), F32), jax.ShapeDtypeStruct((CONV_W, D), F32),
                   jax.ShapeDtypeStruct((1, D), F32)] + [jax.ShapeDtypeStruct((4,) + a.shape[1:], a.dtype)
                                                          for a in chip_sums] + _chips_stage_shapes(chip_sums),
        in_specs=[s["xa"], s["xac"], col(L), pl.BlockSpec(memory_space=pl.ANY), s["cw"], s["cb"], s["h4"], s["h4"],
                  s["b16"], s["b16"], s["v2"]] + [HBM] * nr,
        out_specs=[col(L), col(LC), s["h4"], s["h4"], s["b16"], s["b16"], s["v2"], col(CONV_W), col(1)]
        + [HBM] * (2 * nr),
        scratch_shapes=[pltpu.VMEM((17, L, HD), F32), pltpu.VMEM((17, LC, HD), F32), pltpu.VMEM((L + 16, HD), F32)]
        + (_chips_sems(nr) if nr else []),
        input_output_aliases={3: 0},
        compiler_params=pltpu.CompilerParams(dimension_semantics=("arbitrary",), vmem_limit_bytes=VMEM_LIMIT,
                                             has_side_effects=True, collective_id=6 if nr else None),
    )(zx, zc, dyl, dz, cw, cb, wa, wx, ba, bx, lam, *[pltpu.with_memory_space_constraint(a, pltpu.HBM)
                                                       for a in chip_sums])


def _mixer_loss(x, tgt, zx, yl, gx, fg, lng, lnb, ws, bst, wout, tm):
    ncht = tm // CHUNK

    def body(x_ref, t_ref, ga_ref, u_ref, v_ref, gb_ref, yl_ref, gx_ref, fg_ref, lng_ref, lnb_ref, ws_ref,
             bst_ref, wout_ref,
             dz_ref, dyl_ref, dxn_ref, y_s, do_ref, dws_ref, dbst_ref, vec_ref,
             vn_s, mix_s, dm_s, dvn_s):
        step = pl.program_id(0)

        @pl.when(step == 0)
        def _():
            dws_ref[...] = jnp.zeros_like(dws_ref)
            dbst_ref[...] = jnp.zeros_like(dbst_ref)
            vec_ref[...] = jnp.zeros_like(vec_ref)

        u, v = u_ref[...], v_ref[...]
        ug, dug_du = _gelu_and_grad(u)
        vg, dvg_dv = _gelu_and_grad(v)
        mu = jnp.mean(vg, axis=-1, keepdims=True)
        vc = vg - mu
        rstd = lax.rsqrt(jnp.mean(vc * vc, axis=-1, keepdims=True) + LN_EPS)
        vhat = vc * rstd
        lngv = lng_ref[...]
        vn_s[...] = (vhat * lngv + lnb_ref[...]).astype(BF16)
        for ch in range(ncht):
            rs = slice(ch * CHUNK, (ch + 1) * CHUNK)
            for g in range(HEADS):
                cs = slice(g * HD, (g + 1) * HD)
                mix_s[rs, cs] = _dot(ws_ref[g].astype(BF16), vn_s[rs, cs]) + bst_ref[:, g:g + 1]
        mixed = mix_s[...]
        ga, gb, yl = ga_ref[...], gb_ref[...], yl_ref[...]
        sga, dsga = _silu_and_grad(ga)
        sgb, dsgb = _silu_and_grad(gb)
        ys = ug * mixed
        y_s[:, 0:D] = (yl * sga).astype(BF16)
        y_s[:, D:D_MIX] = (ys * sgb).astype(BF16)
        o = _dot(y_s[...], wout_ref[...])
        gxv, fgv = gx_ref[...], fg_ref[...]
        xn = x_ref[...] + gxv * o
        rs2 = lax.rsqrt(jnp.mean(xn * xn, axis=-1, keepdims=True) + NORM_EPS)
        xh = xn * rs2
        diff = xh * fgv - t_ref[...]
        vec_ref[R_LOSS:R_LOSS + 1, :] += jnp.full((1, D), jnp.sum(diff * diff) * (0.5 / D), F32)
        dout = diff * (1.0 / D)
        w = dout * fgv
        dxn = rs2 * (w - xh * jnp.mean(w * xh, axis=-1, keepdims=True))
        dxn_ref[...] = dxn
        vec_ref[0:1, :] += jnp.sum(dxn * o, axis=0, keepdims=True)
        vec_ref[1:2, :] += jnp.sum(dout * xh, axis=0, keepdims=True)
        dob = (dxn * gxv).astype(BF16)
        do_ref[...] = dob
        dy = _dot_nt(dob, wout_ref[...])
        dya, dyb = dy[:, 0:D], dy[:, D:D_MIX]
        dyl_ref[...] = dya * sga
        dys = dyb * sgb
        dz_ref[:, 0:D] = jnp.zeros((tm, D), BF16)
        dz_ref[:, D:2 * D] = (dya * yl * dsga).astype(BF16)
        dz_ref[:, 2 * D:3 * D] = (dys * mixed * dug_du).astype(BF16)
        dz_ref[:, 4 * D:5 * D] = (dyb * ys * dsgb).astype(BF16)
        dm = dys * ug
        dm_s[...] = dm.astype(BF16)
        for g in range(HEADS):
            cs = slice(g * HD, (g + 1) * HD)
            dbst_ref[:, g:g + 1] += sum(jnp.sum(dm[ch * CHUNK:(ch + 1) * CHUNK, cs], axis=1, keepdims=True)
                                        for ch in range(ncht))
            for ch in range(ncht):
                rs = slice(ch * CHUNK, (ch + 1) * CHUNK)
                dws_ref[g] += _dot_nt(dm_s[rs, cs], vn_s[rs, cs])
                dvn_s[rs, cs] = _dot_tn(ws_ref[g].astype(BF16), dm_s[rs, cs])
        dvn = dvn_s[...]
        vec_ref[2:3, :] += jnp.sum(dvn * vhat, axis=0, keepdims=True)
        vec_ref[3:4, :] += jnp.sum(dvn, axis=0, keepdims=True)
        dvh = dvn * lngv
        dvg = rstd * (dvh - jnp.mean(dvh, axis=-1, keepdims=True) - vhat * jnp.mean(dvh * vhat, axis=-1, keepdims=True))
        dz_ref[:, 3 * D:4 * D] = (dvg * dvg_dv).astype(BF16)

    tile = pl.BlockSpec((tm, D), lambda i: (i, 0))
    zcol = lambda n: pl.BlockSpec((tm, D), lambda i: (i, n))
    vec = pl.BlockSpec((1, D), lambda i: (0, 0))
    full = lambda *s: pl.BlockSpec(s, lambda i: (0,) * len(s))
    return _call(
        body, name="mixer_loss", grid=(L // tm,),
        out_shape=[jax.ShapeDtypeStruct((L, D_IN), BF16), jax.ShapeDtypeStruct((L, D), F32),
                   jax.ShapeDtypeStruct((L, D), F32), jax.ShapeDtypeStruct((L, D_MIX), BF16),
                   jax.ShapeDtypeStruct((L, D), BF16),
                   jax.ShapeDtypeStruct((HEADS, CHUNK, CHUNK), F32), jax.ShapeDtypeStruct((CHUNK, HEADS), F32),
                   jax.ShapeDtypeStruct((8, D), F32)],
        in_specs=[tile, tile, zcol(1), zcol(2), zcol(3), zcol(4), tile, pl.BlockSpec((1, D), lambda i: (0, 2)),
                  vec, vec, vec,
                  full(HEADS, CHUNK, CHUNK), full(CHUNK, HEADS),
                  pl.BlockSpec((D_MIX, D), lambda i: (0, 0), pipeline_mode=pl.Buffered(1))],
        out_specs=[pl.BlockSpec((tm, D_IN), lambda i: (i, 0)), tile, tile,
                   pl.BlockSpec((tm, D_MIX), lambda i: (i, 0)), tile,
                   full(HEADS, CHUNK, CHUNK), full(CHUNK, HEADS), full(8, D)],
        scratch_shapes=[pltpu.VMEM((tm, D), BF16), pltpu.VMEM((tm, D), F32),
                        pltpu.VMEM((tm, D), BF16), pltpu.VMEM((tm, D), F32)],
        compiler_params=_params("arbitrary"),
    )(x, tgt, zx, zx, zx, zx, yl, gx, fg, lng, lnb, ws, bst, wout)


def _grad_w(a, b, a2, b2, tk, name, bw, first, nblocks, split, barrier_id, riders=()):
    nk = a.shape[0] // tk
    m = a.shape[1]
    with_ctx = a2 is not None
    if split == "cols":
        slots, r, w = nblocks, m, bw // 2
        piece = lambda q, pc: (slice(None), slice(pc * w, (pc + 1) * w))
    else:
        slots, r, w = 4, m // 8, bw
        piece = lambda q, pc: (slice((2 * q + pc) * r, (2 * q + pc + 1) * r), slice(None))

    nr = len(riders)

    def body(*refs):
        a_ref, b_ref = refs[:2]
        a2_ref, b2_ref = refs[2:4] if with_ctx else (None, None)
        base = 4 if with_ctx else 2
        sums_ref = refs[base + nr]
        s0 = base + 3 * nr + 1
        acc, mine_v, send_v, stage_v, send_sems, recv_sems = refs[s0:s0 + 6]
        n, k = pl.program_id(0), pl.program_id(1)
        x, y, c = lax.axis_index("x"), lax.axis_index("y"), lax.axis_index("c")

        def to_sibling(s):
            return pltpu.make_async_remote_copy(src_ref=send_v.at[s], dst_ref=stage_v.at[s], send_sem=send_sems.at[s],
                                                recv_sem=recv_sems.at[s], device_id=(x, y, 1 - c),
                                                device_id_type=MESH)

        if nr:
            rider_in = refs[base:base + nr]
            own_v, got_v, psum_v = (refs[s0 + 9 + i * nr:s0 + 9 + (i + 1) * nr] for i in range(3))
            p_send, p_recv, p_local = refs[s0 + 9 + 3 * nr:s0 + 12 + 3 * nr]
            c_start, c_forward, c_finish = _chips_ops(psum_v, refs[base + nr + 1:base + 2 * nr + 1],
                                                      refs[base + 2 * nr + 1:base + 3 * nr + 1], *refs[s0 + 6:s0 + 9])

            @pl.when(jnp.logical_and(n == 0, k == 0))
            def _():
                _barrier([(x, y, 1 - c)] + [(x ^ (j >> 1), y ^ (j & 1), c) for j in (1, 2)])
                copies = []
                for j in range(nr):
                    for q in range(4):
                        copies.append(pltpu.make_async_remote_copy(
                            src_ref=rider_in[j].at[2 * q + 1 - c], dst_ref=got_v[j].at[q], send_sem=p_send.at[j, q],
                            recv_sem=p_recv.at[j, q], device_id=(x, y, 1 - c), device_id_type=MESH))
                        copies.append(pltpu.make_async_copy(rider_in[j].at[2 * q + c], own_v[j].at[q],
                                                            p_local.at[j, q]))
                for cp in copies:
                    cp.start()
                for cp in copies:
                    cp.wait()
                for j in range(nr):
                    psum_v[j][...] = (own_v[j][...] + got_v[j][...]).astype(BF16)
                c_start()
        else:
            pl.when(jnp.logical_and(n == 0, k == 0))(_sibling_barrier)

        @pl.when(k == 0)
        def _():
            acc[...] = _dot_tn(a_ref[...], b_ref[...])

        if nk > 1:
            @pl.when(k > 0)
            def _():
                acc[...] += _dot_tn(a_ref[...], b_ref[...])

        if with_ctx:
            @pl.when(jnp.logical_and(k == nk - 1, n == 0))
            def _():
                acc[:, 0:b2_ref.shape[1]] += _dot_tn(a2_ref[...], b2_ref[...])

        if nr:
            pl.when(jnp.logical_and(k == nk - 1, n == nblocks - 1))(c_forward)

        def hand_over(s, q):
            for pc in (0, 1):
                @pl.when(c == pc)
                def _(pc=pc):
                    mine_v[s] = acc[piece(q, pc)]
                    send_v[s] = acc[piece(q, 1 - pc)].astype(BF16)
            to_sibling(s).start()

        for i in range(nblocks):
            @pl.when(jnp.logical_and(k == nk - 1, n == i))
            def _(i=i):
                if split == "cols":
                    hand_over(i, 0)
                else:
                    for q in range(4):
                        hand_over(q, q)

        @pl.when(jnp.logical_and(k == nk - 1, n == nblocks - 1))
        def _():
            for s in range(slots):
                to_sibling(s).wait_recv()
                sums_ref[s] = (mine_v[s] + stage_v[s].astype(F32)).astype(BF16)
            for s in range(slots):
                to_sibling(s).wait_send()
            if nr:
                c_finish()

    in_specs = [pl.BlockSpec((tk, m), lambda n, k: (k, 0)), pl.BlockSpec((tk, bw), lambda n, k: (k, n + first))]
    args = [a, b]
    if with_ctx:
        in_specs += [pl.BlockSpec(a2.shape, lambda n, k: (0, 0)), pl.BlockSpec(b2.shape, lambda n, k: (0, 0))]
        args += [a2, b2]
    in_specs += [HBM] * nr
    args += [pltpu.with_memory_space_constraint(s, pltpu.HBM) for s in riders]
    rider_sums = [jax.ShapeDtypeStruct((4,) + s.shape[1:], BF16) for s in riders]
    rider_scratch = []
    if nr:
        rider_scratch = (_chips_sems(nr) + [pltpu.VMEM(s.shape, F32) for s in rider_sums] * 2
                         + [pltpu.VMEM(s.shape, BF16) for s in rider_sums]
                         + [pltpu.SemaphoreType.DMA((nr, 4))] * 3)
    return _call(
        body, name=name, grid=(nblocks, nk),
        out_shape=[jax.ShapeDtypeStruct((slots, r, w), BF16)] + rider_sums + _chips_stage_shapes(rider_sums),
        in_specs=in_specs, out_specs=[pl.BlockSpec((slots, r, w), lambda n, k: (0, 0, 0))] + [HBM] * (2 * nr),
        scratch_shapes=[pltpu.VMEM((m, bw), F32), pltpu.VMEM((slots, r, w), F32), pltpu.VMEM((slots, r, w), BF16),
                        pltpu.VMEM((slots, r, w), BF16), pltpu.SemaphoreType.DMA((slots,)),
                        pltpu.SemaphoreType.DMA((slots,))] + rider_scratch,
        compiler_params=pltpu.CompilerParams(dimension_semantics=("arbitrary", "arbitrary"),
                                             vmem_limit_bytes=VMEM_LIMIT, has_side_effects=True,
                                             collective_id=barrier_id),
    )(*args)


def _grad_rows(xr, dz, w, mod, ng, dres, ncols, tm, name, chip_sums=(), first_chips=None, dests=None):
    rows = xr.shape[0]
    steps = rows // tm
    with_dx = dres is not None
    nr = len(chip_sums)
    dests = [d for d in (dests or [None] * nr)]
    nd = sum(d is not None for d in dests)
    nin = 6 if with_dx else 5
    nout = 2 if with_dx else 1

    def body(*refs):
        if with_dx:
            x_ref, dz_ref, w_ref, sc_ref, ng_ref, dres_ref = refs[:nin]
            dx_ref, vec_ref = refs[nin + nr + nd:nin + nr + nd + nout]
        else:
            x_ref, dz_ref, w_ref, sc_ref, ng_ref = refs[:nin]
            (vec_ref,) = refs[nin + nr + nd:nin + nr + nd + nout]
        if nr:
            o0 = nin + nr + nd + nout
            start, forward, finish = _chips_ops(refs[nin:nin + nr], refs[o0:o0 + nr], refs[o0 + nr:o0 + 2 * nr],
                                                *refs[o0 + 2 * nr:], first_chips=first_chips, barrier=True)
            pl.when(pl.program_id(0) == 0)(start)
            pl.when(pl.program_id(0) == steps // 2)(forward)
            pl.when(pl.program_id(0) == steps - 1)(finish)

        @pl.when(pl.program_id(0) == 0)
        def _():
            vec_ref[...] = jnp.zeros_like(vec_ref)

        dhn = _dot_nt(dz_ref[...], w_ref[...])
        x = x_ref[...]
        rs = lax.rsqrt(jnp.mean(x * x, axis=-1, keepdims=True) + NORM_EPS)
        xh = x * rs
        ngv = ng_ref[...]
        y = xh * ngv
        vec_ref[0:1, :] += jnp.sum(dhn, axis=0, keepdims=True)
        vec_ref[1:2, :] += jnp.sum(dhn * y, axis=0, keepdims=True)
        dy = dhn * (1.0 + sc_ref[...])
        vec_ref[2:3, :] += jnp.sum(dy * xh, axis=0, keepdims=True)
        if with_dx:
            dxh = dy * ngv
            dx_ref[...] = dres_ref[...] + rs * (dxh - xh * jnp.mean(dxh * xh, axis=-1, keepdims=True))

    tile = pl.BlockSpec((tm, D), lambda i: (i, 0))
    vec = pl.BlockSpec((1, D), lambda i: (0, 0))
    in_specs = [tile, pl.BlockSpec((tm, ncols), lambda i: (i, 0)),
                pl.BlockSpec((D, ncols), lambda i: (0, 0), pipeline_mode=pl.Buffered(1)),
                pl.BlockSpec((1, D), lambda i: (0, 1)), vec]
    out_shape = [jax.ShapeDtypeStruct((8, D), F32)]
    out_specs = [pl.BlockSpec((8, D), lambda i: (0, 0))]
    args = [xr, dz, w, mod, ng]
    if with_dx:
        in_specs.append(tile)
        out_shape.insert(0, jax.ShapeDtypeStruct((rows, D), F32))
        out_specs.insert(0, tile)
        args.append(dres)
    aliases = {}
    for j, d in enumerate(dests):
        if d is not None:
            aliases[len(args) + nr + len(aliases)] = len(out_shape) + j
    in_specs += [HBM] * (nr + nd)
    out_specs += [HBM] * (2 * nr)
    out_shape += [jax.ShapeDtypeStruct((4,) + a.shape[1:], a.dtype) for a in chip_sums]
    out_shape += _chips_stage_shapes(chip_sums)
    args += [pltpu.with_memory_space_constraint(a, pltpu.HBM) for a in chip_sums]
    args += [pltpu.with_memory_space_constraint(d, pltpu.HBM) for d in dests if d is not None]
    return _call(body, name=name, grid=(steps,), out_shape=out_shape, in_specs=in_specs, out_specs=out_specs,
                 scratch_shapes=_chips_sems(nr) if nr else [], input_output_aliases=aliases,
                 compiler_params=pltpu.CompilerParams(dimension_semantics=("arbitrary",),
                                                      vmem_limit_bytes=VMEM_LIMIT, has_side_effects=bool(nr),
                                                      collective_id=7 if nr else None))(*args)


def _adamw(w, g, m, v):
    m = ADAM_B1 * m + (1.0 - ADAM_B1) * g
    v = ADAM_B2 * v + (1.0 - ADAM_B2) * (g * g)
    m_hat = m / (1.0 - ADAM_B1 ** ADAM_STEP)
    v_hat = v / (1.0 - ADAM_B2 ** ADAM_STEP)
    delta = -ADAM_LR * (m_hat / (jnp.sqrt(v_hat) + ADAM_EPS) + ADAM_WD * w)
    return delta, m, v


def _adamw_reduced(parts, w, m, v, tr, name):
    r, n = w.shape
    nparts = parts.shape[0]

    def body(p_ref, w_ref, m_ref, v_ref, g_ref, d_ref, mo_ref, vo_ref):
        g = p_ref[0].astype(F32)
        for i in range(1, nparts):
            g = g + p_ref[i].astype(F32)
        g_ref[...] = g
        d_ref[...], mo_ref[...], vo_ref[...] = _adamw(w_ref[...], g, m_ref[...], v_ref[...])

    tile = pl.BlockSpec((tr, n), lambda i: (i, 0))
    sds = jax.ShapeDtypeStruct((r, n), F32)
    return _call(
        body, name=name, grid=(r // tr,), out_shape=[sds] * 4,
        in_specs=[pl.BlockSpec((nparts, tr, n), lambda i: (0, i, 0)), tile, tile, tile], out_specs=[tile] * 4,
        compiler_params=_params("arbitrary"),
    )(parts, w, m, v)


R_GATE, R_FINAL_G, R_LN_G, R_LN_B, R_LOSS = 0, 1, 2, 3, 4
R_SH_X, R_SC_X, R_NG_X = 5, 6, 7
R_SH_C, R_SC_C, R_NG_C = 8, 9, 10
R_LAM, R_CW, R_CB = 11, 13, 17
PACK_ROWS = 24
Q_BA, Q_BX, Q_SGU_B, PACK128_ROWS = 0, 16, 32, 40


def _reduce_small(vec_pieces, q_pieces, mat_parts, ada_w, me):
    nloc = ada_w.shape[1]
    nm = len(mat_parts)
    pieces = list(vec_pieces) + list(q_pieces)

    def body(me_ref, *refs):
        piece_refs, refs = refs[:len(pieces)], refs[len(pieces):]
        mp_refs, w_ref = refs[:nm], refs[nm]
        red_ref, redq_ref = refs[nm + 1:nm + 3]
        mats_all = refs[nm + 3:2 * nm + 3]
        cparts_ref, dmod_ref, gab_ref, loss_ref = refs[2 * nm + 3:2 * nm + 7]
        pack_ref, packq_ref, vp_ref, vq_ref = refs[2 * nm + 7:2 * nm + 11]
        mat_refs = refs[2 * nm + 11:3 * nm + 11]
        cpart_ref, dmc_s = refs[3 * nm + 11:3 * nm + 13]
        sems = refs[3 * nm + 13:]
        for dst, group in ((pack_ref, vec_pieces), (packq_ref, q_pieces)):
            row = 0
            for _, nrows in group:
                dst[row:row + nrows, :] = piece_refs[0][0:nrows, :]
                piece_refs, row = piece_refs[1:], row + nrows
            if row < dst.shape[0]:
                dst[row:, :] = jnp.zeros((dst.shape[0] - row, dst.shape[1]), F32)
        p_start, p_forward, p_finish = _gather2_ops([pack_ref, packq_ref], [vp_ref, vq_ref], ["ag", "ag"], *sems[:3],
                                                    barrier=True)
        m_start, m_forward, m_finish = _gather2_ops(mat_refs, mats_all, ["ag"] * nm, *sems[3:6])
        c_start, c_forward, c_finish = _gather2_ops([cpart_ref], [cparts_ref], ["ag"], *sems[6:])
        p_start()
        for mp_ref, mat_ref in zip(mp_refs, mat_refs):
            mat = mp_ref[0].astype(F32)
            for i in range(1, mp_ref.shape[0]):
                mat = mat + mp_ref[i].astype(F32)
            mat_ref[...] = mat
        m_start()
        p_forward()
        p_finish()
        red, redq = vp_ref[0], vq_ref[0]
        for i in range(1, N_DEV):
            red = red + vp_ref[i]
            redq = redq + vq_ref[i]
        red_ref[...] = red
        redq_ref[...] = redq
        loss_ref[...] = red_ref[R_LOSS:R_LOSS + 1, 0:1]
        for e in range(N_DEV):
            dmod_ref[e:e + 1, 0:D] = vp_ref[e, R_SH_X:R_SH_X + 1, :]
            dmod_ref[e:e + 1, D:2 * D] = vp_ref[e, R_SC_X:R_SC_X + 1, :]
            dmod_ref[e:e + 1, 2 * D:3 * D] = vp_ref[e, R_GATE:R_GATE + 1, :]
        dmod_ref[8:9, 0:D] = red[R_SH_C:R_SH_C + 1, :]
        dmod_ref[8:9, D:2 * D] = red[R_SC_C:R_SC_C + 1, :]
        dmod_ref[8:9, 2 * D:3 * D] = jnp.zeros((1, D), F32)
        dmod_ref[9:16, :] = jnp.zeros((7, 3 * D), F32)
        gab_ref[:, 0:D] = red[R_SH_X:R_SH_X + 1, :] + red[R_SH_C:R_SH_C + 1, :]
        gab_ref[:, D:2 * D] = red[R_SC_X:R_SC_X + 1, :] + red[R_SC_C:R_SC_C + 1, :]
        gab_ref[:, 2 * D:3 * D] = red[R_GATE:R_GATE + 1, :]
        dmc_s[...] = jnp.broadcast_to(dmod_ref[8:9, :], (8, 3 * D))
        off = pl.multiple_of(me_ref[0] * nloc, 128)
        cpart_ref[...] = _dot_nt(dmc_s[:, pl.ds(off, nloc)], w_ref[...])
        c_start()
        m_forward()
        c_forward()
        c_finish()
        m_finish()

    return _call(
        body, name="reduce_small",
        out_shape=[jax.ShapeDtypeStruct((PACK_ROWS, D), F32), jax.ShapeDtypeStruct((PACK128_ROWS, HD), F32)]
        + [jax.ShapeDtypeStruct((N_DEV,) + p.shape[1:], F32) for p in mat_parts]
        + [jax.ShapeDtypeStruct((N_DEV, 8, D), F32), jax.ShapeDtypeStruct((16, 3 * D), F32),
           jax.ShapeDtypeStruct((1, 3 * D), F32), jax.ShapeDtypeStruct((1, 1), F32)],
        in_specs=[pl.BlockSpec(memory_space=pltpu.SMEM)] + [VMEM] * (len(pieces) + nm + 1),
        out_specs=[VMEM] * (nm + 6),
        scratch_shapes=[pltpu.VMEM((PACK_ROWS, D), F32), pltpu.VMEM((PACK128_ROWS, HD), F32),
                        pltpu.VMEM((N_DEV, PACK_ROWS, D), F32), pltpu.VMEM((N_DEV, PACK128_ROWS, HD), F32)]
        + [pltpu.VMEM(p.shape[1:], F32) for p in mat_parts]
        + [pltpu.VMEM((8, D), F32), pltpu.VMEM((8, 3 * D), F32)] + _gather2_sems(2) + _gather2_sems(nm)
        + _gather2_sems(1),
        compiler_params=pltpu.CompilerParams(vmem_limit_bytes=VMEM_LIMIT, has_side_effects=True, collective_id=8),
    )(me, *[a for a, _ in pieces], *mat_parts, ada_w)


def _adamw_ada(c_all, c_ctx, dmod, w, m, v, me):
    nloc = w.shape[1]

    def body(me_ref, c_ref, cc_ref, dm_ref, w_ref, m_ref, v_ref, g_ref, d_ref, mo_ref, vo_ref):
        off = pl.multiple_of(me_ref[0] * nloc, 128)
        dm = dm_ref[:, pl.ds(off, nloc)]
        sx, _ = _silu_and_grad(c_ref[...])
        sc, _ = _silu_and_grad(cc_ref[...])
        g = _dot_tn(sx, dm[0:8, :]) + _dot_tn(jnp.broadcast_to(sc, (8, D)), dm[8:16, :])
        g_ref[...] = g
        d_ref[...], mo_ref[...], vo_ref[...] = _adamw(w_ref[...], g, m_ref[...], v_ref[...])

    sds = jax.ShapeDtypeStruct(w.shape, F32)
    return _call(
        body, name="adamw_ada_w", out_shape=[sds] * 4,
        in_specs=[pl.BlockSpec(memory_space=pltpu.SMEM)] + [VMEM] * 6, out_specs=[VMEM] * 4,
        compiler_params=_params(),
    )(me, c_all, c_ctx, dmod, w, m, v)


_SMALL = ("c_ctx", "ada_b", "norm_g", "conv_w", "conv_b", "lru_wa", "lru_ba", "lru_wx", "lru_bx", "lru_lambda",
          "sgu_ln_g", "sgu_ln_b", "sgu_w", "sgu_b", "final_g")


def _adamw_small(red, redq, mats, cparts, gab, ws, ms, vs, me):
    n = len(_SMALL)

    def body(me_ref, red_ref, redq_ref, wa_ref, wx_ref, sw_ref, cp_ref, gab_ref, *refs):
        w_refs, m_refs, v_refs = refs[:n], refs[n:2 * n], refs[2 * n:3 * n]
        outs = refs[3 * n:]
        off = pl.multiple_of(me_ref[0] * HD, 128)

        def row(r, k=1):
            return red_ref[r:r + k, :]

        cc = w_refs[0][...]
        dcc = cp_ref[0, 0:1, :]
        for i in range(1, N_DEV):
            dcc = dcc + cp_ref[i, 0:1, :]
        grads = dict(
            c_ctx=dcc * _silu_and_grad(cc)[1], ada_b=gab_ref[...], norm_g=row(R_NG_X) + row(R_NG_C),
            conv_w=red_ref[R_CW:R_CW + CONV_W, pl.ds(off, HD)], conv_b=row(R_CB),
            lru_wa=wa_ref[...], lru_ba=redq_ref[Q_BA:Q_BA + 2 * HEADS, :], lru_wx=wx_ref[...],
            lru_bx=redq_ref[Q_BX:Q_BX + 2 * HEADS, :], lru_lambda=red_ref[R_LAM:R_LAM + 2, pl.ds(off, HD)],
            sgu_ln_g=row(R_LN_G), sgu_ln_b=row(R_LN_B), sgu_w=sw_ref[...],
            sgu_b=redq_ref[Q_SGU_B:Q_SGU_B + HEADS, :], final_g=row(R_FINAL_G))
        for j, name in enumerate(_SMALL):
            g = grads[name]
            outs[j][...] = g
            outs[n + j][...], outs[2 * n + j][...], outs[3 * n + j][...] = _adamw(w_refs[j][...], g, m_refs[j][...],
                                                                                 v_refs[j][...])

    sds = [jax.ShapeDtypeStruct(ws[k].shape, F32) for k in _SMALL]
    outs = _call(
        body, name="adamw_small", out_shape=sds * 4,
        in_specs=[pl.BlockSpec(memory_space=pltpu.SMEM)] + [VMEM] * (7 + 3 * n), out_specs=[VMEM] * (4 * n),
        compiler_params=_params(),
    )(me, red, redq, *mats, cparts, gab, *[ws[k] for k in _SMALL], *[ms[k] for k in _SMALL],
      *[vs[k] for k in _SMALL])
    return [dict(zip(_SMALL, outs[i * n:(i + 1) * n])) for i in range(4)]


def kernel(x, c, ctx, c_ctx, ada_w, ada_b, norm_g, w_in, conv_w, conv_b, lru_wa, lru_ba, lru_wx, lru_bx, lru_lambda, sgu_ln_g, sgu_ln_b, sgu_w, sgu_b, w_out, final_g, loss_target, m_c_ctx, m_ada_w, m_ada_b, m_norm_g, m_w_in, m_conv_w, m_conv_b, m_lru_wa, m_lru_ba, m_lru_wx, m_lru_bx, m_lru_lambda, m_sgu_ln_g, m_sgu_ln_b, m_sgu_w, m_sgu_b, m_w_out, m_final_g, v_c_ctx, v_ada_w, v_ada_b, v_norm_g, v_w_in, v_conv_w, v_conv_b, v_lru_wa, v_lru_ba, v_lru_wx, v_lru_bx, v_lru_lambda, v_sgu_ln_g, v_sgu_ln_b, v_sgu_w, v_sgu_b, v_w_out, v_final_g):
    args = dict(locals())
    me = (4 * lax.axis_index("x") + 2 * lax.axis_index("y") + lax.axis_index("c")).astype(jnp.int32).reshape(1)
    xr, ctxr, tgt = x[0], ctx[0], loss_target[0]
    cc = c_ctx.reshape(1, D)
    nw = 2 * HEADS * HD
    view = dict(c_ctx=(1, D), ada_b=(1, 3 * D), norm_g=(1, D), conv_w=(CONV_W, HD), conv_b=(1, D), lru_wa=(nw, HD),
                lru_ba=(2 * HEADS, HD), lru_wx=(nw, HD), lru_bx=(2 * HEADS, HD), lru_lambda=(2, HD), sgu_ln_g=(1, D),
                sgu_ln_b=(1, D), sgu_w=(HEADS * CHUNK, CHUNK), sgu_b=(HEADS, CHUNK), final_g=(1, D))

    zx, hn, w_full, w_out_b, modx, modc, c_all, cw_full, lam_full = _front_project(
        xr, c, cc, ada_w[0], ada_b, norm_g, w_in[0], w_out[0], conv_w[0], lru_lambda[0], me)
    zc, hnc = _project(ctxr, modc, norm_g, w_full, D, LC, "project_ctx")
    ba, bx = lru_ba.reshape(view["lru_ba"]), lru_bx.reshape(view["lru_bx"])
    yl, wout_all = _lru_forward(zx, zc, cw_full, conv_b, lru_wa[0], lru_wx[0], ba, bx, lam_full, [w_out_b], ["ag"])
    wout_full = wout_all.reshape(D_MIX, D)
    dz, dyl, dxn, ycat, dob, dws, dbst, mvec = _mixer_loss(
        xr, tgt, zx, yl, modx, final_g.reshape(1, D), sgu_ln_g, sgu_ln_b, sgu_w[0], sgu_b[0].T, wout_full, ROWS)

    (wout_sums,) = _grad_w(ycat, dob, None, None, L, "grad_w_out", D, 0, 1, "rows", 1)
    (rest_sums,) = _grad_w(hn, dz, None, None, L, "grad_w_in_rest", 2 * W_IN_SHARD, 1, 3, "cols", 2)
    dz, dxac, dwa, dwx, dba, dbx, dlam, dcw, dcb, win_parts, wout_parts, _, _ = _lru_backward(
        zx, zc, dyl, dz, cw_full, conv_b, lru_wa[0], lru_wx[0], ba, bx, lam_full, [rest_sums, wout_sums],
        first_chips=[1, 0])
    mats = [dwa.reshape(N_DEV, nw // N_DEV, HD), dwx.reshape(N_DEV, nw // N_DEV, HD), dws]
    first_sums, *mat_parts = _grad_w(hn, dz, hnc, dxac, L, "grad_w_in_first", 2 * W_IN_SHARD, 0, 1, "cols", 3,
                                     riders=mats)[:4]
    gx, xvec, win_parts = _grad_rows(
        xr, dz, w_full, modx, norm_g, dxn, D_IN, ROWS, "grad_rows_x", chip_sums=[first_sums], first_chips=[0],
        dests=[win_parts])[:3]
    (cvec,) = _grad_rows(ctxr, dxac, w_full, modc, norm_g, None, D, LC, "grad_rows_ctx")
    red, redq, *rest = _reduce_small(
        [(mvec, 5), (xvec, 3), (cvec, 3), (dlam, 2), (dcw, CONV_W), (dcb, 1)],
        [(dba, 2 * HEADS), (dbx, 2 * HEADS), (dbst.T, HEADS)], mat_parts, ada_w[0], me)
    mats_all, (cparts, dmod, gab, loss) = rest[:3], rest[3:]

    g_w_in, d_w_in, nm_w_in, nv_w_in = _adamw_reduced(win_parts, w_in[0], m_w_in[0], v_w_in[0], 2 * ROWS, "adamw_w_in")
    g_w_out, d_w_out, nm_w_out, nv_w_out = _adamw_reduced(wout_parts, w_out[0], m_w_out[0], v_w_out[0], ROWS // 2,
                                                          "adamw_w_out")
    g_ada, d_ada, nm_ada, nv_ada = _adamw_ada(c_all, cc, dmod, ada_w[0], m_ada_w[0], v_ada_w[0], me)
    ws = {k: args[k].reshape(view[k]) for k in _SMALL}
    ms = {k: args["m_" + k].reshape(view[k]) for k in _SMALL}
    vs = {k: args["v_" + k].reshape(view[k]) for k in _SMALL}
    small = _adamw_small(red, redq, [m.reshape(-1, HD) for m in mats_all], cparts, gab, ws, ms, vs, me)
    big = dict(w_in=(g_w_in, d_w_in, nm_w_in, nv_w_in), w_out=(g_w_out, d_w_out, nm_w_out, nv_w_out),
               ada_w=(g_ada, d_ada, nm_ada, nv_ada))

    loss = loss.reshape(())
    names = ("c_ctx", "ada_w", "ada_b", "norm_g", "w_in", "conv_w", "conv_b", "lru_wa", "lru_ba", "lru_wx", "lru_bx",
             "lru_lambda", "sgu_ln_g", "sgu_ln_b", "sgu_w", "sgu_b", "w_out", "final_g")
    outs = [loss, gx.reshape(x.shape)]
    for kind in range(4):
        for k in names:
            val = big[k][kind] if k in big else small[kind][k]
            outs.append(val.reshape(args[k].shape))
    return tuple(outs)
```

```python
import jax
import jax.numpy as jnp
from jax import lax
from jax.experimental import pallas as pl
from jax.experimental.pallas import tpu as pltpu

F32 = jnp.float32
BF16 = jnp.bfloat16

N_DEV = 8
D = 1024
L = 2048
LC = 256
HEADS = 8
HD = 128
CHUNK = 128
D_IN = 5 * D
W_IN_SHARD = D_IN // N_DEV
ROWS = 256
D_MIX = 2 * D
CONV_W = 4
LRU_C = 8.0
NORM_EPS = 1e-6
LN_EPS = 1e-5
ADAM_LR, ADAM_B1, ADAM_B2, ADAM_EPS, ADAM_WD, ADAM_STEP = 0.001, 0.9, 0.999, 1e-08, 0.01, 10

VMEM_LIMIT = 56 * 1024 * 1024

HBM = pl.BlockSpec(memory_space=pltpu.HBM)
VMEM = pl.BlockSpec(memory_space=pltpu.VMEM)
MESH = pl.DeviceIdType.MESH


def _call(body, **kw):
    return pl.pallas_call(body, **kw)


def _params(*sem):
    return pltpu.CompilerParams(dimension_semantics=sem, vmem_limit_bytes=VMEM_LIMIT)


def _sigmoid(x):
    return 0.5 * jnp.tanh(0.5 * x) + 0.5


def _silu_and_grad(x):
    s = _sigmoid(x)
    return x * s, s * (1.0 + x * (1.0 - s))


_G0 = 0.7978845608028654
_G1 = 0.044715


def _gelu_and_grad(x):
    x2 = x * x
    t = jnp.tanh(_G0 * (x + _G1 * x * x2))
    cdf = 0.5 * (1.0 + t)
    return x * cdf, cdf + 0.5 * x * (1.0 - t * t) * (_G0 * (1.0 + 3.0 * _G1 * x2))


def _softplus(z):
    t = jnp.exp(-jnp.abs(z))
    u = 1.0 + t
    log1p = jnp.where(u == 1.0, t, jnp.log(u) * t / jnp.where(u == 1.0, 1.0, u - 1.0))
    return jnp.maximum(z, 0.0) + log1p


def _dot(a, b):
    return jnp.dot(a, b, preferred_element_type=F32)


def _dot_nt(a, b):
    return lax.dot_general(a, b, (((1,), (1,)), ((), ())), preferred_element_type=F32)


def _dot_tn(a, b):
    return lax.dot_general(a, b, (((0,), (0,)), ((), ())), preferred_element_type=F32)


def _rows(shape):
    return lax.broadcasted_iota(jnp.int32, shape, 0)


def _gather2_shapes(arrays, modes):
    return [jax.ShapeDtypeStruct((N_DEV,) + a.shape if m == "ag" else (a.shape[0], N_DEV * a.shape[1]), a.dtype)
            for a, m in zip(arrays, modes)]


def _gather2_sems(n):
    return [pltpu.SemaphoreType.DMA((n, N_DEV - 1)), pltpu.SemaphoreType.DMA((n, N_DEV - 1)),
            pltpu.SemaphoreType.DMA((n,))]


def _barrier(peers):
    sem = pltpu.get_barrier_semaphore()
    for peer in peers:
        pl.semaphore_signal(sem, inc=1, device_id=peer, device_id_type=MESH)
    pl.semaphore_wait(sem, len(peers))


def _gather2_ops(ins, outs, modes, send_sems, recv_sems, local_sems, barrier=False):
    n = len(ins)
    x, y, c = lax.axis_index("x"), lax.axis_index("y"), lax.axis_index("c")
    me, sibling = (x, y, c), (x, y, 1 - c)
    chips = [(x ^ (k >> 1), y ^ (k & 1)) for k in (1, 2, 3)]

    def slot(j, px, py, pc):
        dev = 4 * px + 2 * py + pc
        if modes[j] == "agc":
            w = ins[j].shape[1]
            return outs[j].at[:, pl.ds(pl.multiple_of(dev * w, 128), w)]
        return outs[j].at[dev]

    def copy(j, k, block, to, src=None):
        return pltpu.make_async_remote_copy(
            src_ref=slot(j, *block) if src is None else src, dst_ref=slot(j, *block),
            send_sem=send_sems.at[j, k], recv_sem=recv_sems.at[j, k], device_id=to, device_id_type=MESH)

    def own(j):
        return pltpu.make_async_copy(ins[j], slot(j, *me), local_sems.at[j])

    def first(j):
        return [copy(j, 0, me, sibling, src=ins[j])] + [copy(j, 1 + i, me, (*chip, c), src=ins[j])
                                                        for i, chip in enumerate(chips)]

    def passed(j, i):
        return copy(j, 4 + i, (*chips[i], c), sibling)

    def start():
        if barrier:
            _barrier([sibling] + [(*chip, c) for chip in chips])
        for j in range(n):
            own(j).start()
            for cp in first(j):
                cp.start()

    def forward():
        for i, chip in enumerate(chips):
            for j in range(n):
                copy(j, 1 + i, (*chip, c), me).wait_recv()
                passed(j, i).start()

    def finish():
        for j in range(n):
            copy(j, 0, sibling, me).wait_recv()
            for i, chip in enumerate(chips):
                copy(j, 4 + i, (*chip, 1 - c), me).wait_recv()
            for cp in first(j) + [passed(j, i) for i in range(3)]:
                cp.wait_send()
            own(j).wait()

    return start, forward, finish


def _sibling_barrier():
    sem = pltpu.get_barrier_semaphore()
    sibling = (lax.axis_index("x"), lax.axis_index("y"), 1 - lax.axis_index("c"))
    pl.semaphore_signal(sem, inc=1, device_id=sibling, device_id_type=MESH)
    pl.semaphore_wait(sem, 1)


def _chips_sems(n):
    return [pltpu.SemaphoreType.DMA((n, 6)), pltpu.SemaphoreType.DMA((n, 6)), pltpu.SemaphoreType.DMA((n,))]


def _chips_stage_shapes(chip_sums):
    return [jax.ShapeDtypeStruct((2, a.shape[1] // 2, a.shape[2]), a.dtype) for a in chip_sums]


def _chips_ops(ins, outs, stages, send_sems, recv_sems, local_sems, first_chips=None, barrier=False):
    x, y, c = lax.axis_index("x"), lax.axis_index("y"), lax.axis_index("c")
    qm = 2 * x + y
    first_chips = first_chips or [0] * len(ins)

    def owns(j, chip):
        lo, cnt = first_chips[j], ins[j].shape[0]
        if lo == 0 and cnt == 4:
            return None
        return jnp.logical_and(chip >= lo, chip < lo + cnt)

    def guarded(cond, fn):
        if cond is None:
            fn()
        else:
            pl.when(cond)(fn)

    def slot(j, chip):
        return jnp.clip(chip - first_chips[j], 0, ins[j].shape[0] - 1)

    def half(j, i):
        h = ins[j].shape[1] // 2
        return pl.ds(i * h, h)

    def copy(j, sem, src, dst, k):
        return pltpu.make_async_remote_copy(
            src_ref=src, dst_ref=dst, send_sem=send_sems.at[j, sem], recv_sem=recv_sems.at[j, sem],
            device_id=(x ^ (k >> 1), y ^ (k & 1), c), device_id_type=MESH)

    def direct(j, k):
        return copy(j, k - 1, ins[j].at[slot(j, qm ^ k)], outs[j].at[qm], k)

    def first_hop(j, k):
        return copy(j, 1 + k, ins[j].at[slot(j, qm ^ 3), half(j, k - 1)], stages[j].at[k - 1], k)

    def second_hop(j, k):
        return copy(j, 3 + k, stages[j].at[2 - k], outs[j].at[qm ^ (3 - k), half(j, 2 - k)], k)

    def local(j):
        return pltpu.make_async_copy(ins[j].at[slot(j, qm)], outs[j].at[qm], local_sems.at[j])

    def start():
        if barrier:
            _barrier([(x ^ (k >> 1), y ^ (k & 1), c) for k in (1, 2)])
        for j in range(len(ins)):
            for k in (1, 2):
                guarded(owns(j, qm ^ 3), lambda j=j, k=k: first_hop(j, k).start())
        for j in range(len(ins)):
            for k in (1, 2):
                guarded(owns(j, qm ^ k), lambda j=j, k=k: direct(j, k).start())
            guarded(owns(j, qm), lambda j=j: local(j).start())

    def forward():
        for j in range(len(ins)):
            for k in (1, 2):
                def pass_on(j=j, k=k):
                    first_hop(j, 3 - k).wait_recv()
                    second_hop(j, k).start()
                guarded(owns(j, qm ^ k), pass_on)

    def finish():
        for j in range(len(ins)):
            for k in (1, 2):
                guarded(owns(j, qm ^ k), lambda j=j, k=k: direct(j, k).wait_send())
                guarded(owns(j, qm ^ k), lambda j=j, k=k: second_hop(j, k).wait_send())
                guarded(owns(j, qm ^ 3), lambda j=j, k=k: first_hop(j, k).wait_send())
                guarded(owns(j, qm), lambda j=j, k=k: direct(j, k).wait_recv())
                guarded(owns(j, qm), lambda j=j, k=k: second_hop(j, k).wait_recv())
            guarded(owns(j, qm), lambda j=j: local(j).wait())

    return start, forward, finish


ARRIVAL = (0, 1, 2, 4, 3, 5, 6, 7)


def _front_project(xr, c, c_ctx, ada_w, ada_b, ng, w_in, w_out, cw, lam, me):
    nloc = ada_w.shape[1]
    ws = W_IN_SHARD
    arrival = jnp.asarray(ARRIVAL, jnp.int32)

    def body(me_ref, arr_ref, x_ref, c_ref, cc_ref, aw_ref, ab_ref, ng_ref, win_ref, wout_ref, cw_ref, lam_ref,
             z_ref, hn_ref, wfull_ref, woutb_ref, modx_ref, modc_ref, call_ref, cwf_ref, lamf_ref,
             wv, call_s, part_s, parts_s, w_send, w_recv, hbm_sems, s_send, s_recv, g_send, g_recv, g_local,
             x_v, x_sem):
        t = pl.program_id(0)
        x_load = pltpu.make_async_copy(x_ref, x_v, x_sem)
        x, y, cidx = lax.axis_index("x"), lax.axis_index("y"), lax.axis_index("c")
        me_i = me_ref[0]
        sibling = (x, y, 1 - cidx)
        chips = [(x ^ (k >> 1), y ^ (k & 1)) for k in (1, 2, 3)]
        g_start, g_pass, g_finish = _gather2_ops([cw_ref, lam_ref], [cwf_ref, lamf_ref], ["agc", "agc"],
                                                 g_send, g_recv, g_local)

        def shard_copy(k, px, py, pc, to, half=None):
            slot = wv.at[4 * px + 2 * py + pc]
            if half is not None:
                slot = slot.at[pl.ds(half * (D // 2), D // 2), :]
            return pltpu.make_async_remote_copy(src_ref=slot, dst_ref=slot, send_sem=w_send.at[k],
                                                recv_sem=w_recv.at[k], device_id=to, device_id_type=MESH)

        def small_gather(src, my_slot, stage):
            copies = []
            for k in range(1, N_DEV):
                peer = (x ^ (k >> 2), y ^ ((k >> 1) & 1), cidx ^ (k & 1))
                cp = pltpu.make_async_remote_copy(src_ref=src, dst_ref=my_slot, send_sem=s_send.at[stage, k - 1],
                                                  recv_sem=s_recv.at[stage, k - 1], device_id=peer,
                                                  device_id_type=MESH)
                cp.start()
                copies.append(cp)
            pltpu.sync_copy(src, my_slot)
            return copies

        def finish_small(copies):
            for cp in copies:
                cp.wait()

        def to_neighbours(half):
            for i in (0, 1):
                shard_copy(1 + i, x, y, cidx, (*chips[i], cidx), half=half).start()

        @pl.when(t == 0)
        def _():
            _barrier([(x ^ (k >> 2), y ^ ((k >> 1) & 1), cidx ^ (k & 1)) for k in range(1, N_DEV)])
            g_start()
            x_load.start()
            wv[me_i] = win_ref[...].astype(BF16)
            woutb_ref[...] = wout_ref[...].astype(BF16)
            shard_copy(0, x, y, cidx, sibling).start()
            conds_sent = small_gather(c_ref, call_s.at[pl.ds(me_i, 1), :], 0)
            to_neighbours(0)
            finish_small(conds_sent)
            call_ref[...] = call_s[...]
            off = pl.multiple_of(me_i * nloc, 128)
            b = ab_ref[:, pl.ds(off, nloc)]
            w = aw_ref[...]
            sx, _ = _silu_and_grad(call_s[...])
            sc, _ = _silu_and_grad(jnp.broadcast_to(cc_ref[...], (8, D)))
            part_s[0:8, :] = _dot(sx, w) + b
            part_s[8:16, :] = _dot(sc, w) + b
            parts_sent = small_gather(part_s, parts_s.at[me_i], 1)
            to_neighbours(1)
            finish_small(parts_sent)
            mine = _rows((16, nloc)) == me_i
            for j in range(N_DEV):
                pj = parts_s[j]
                modx_ref[:, j * nloc:(j + 1) * nloc] = jnp.sum(jnp.where(mine, pj, 0.0), axis=0, keepdims=True)
                modc_ref[:, j * nloc:(j + 1) * nloc] = pj[8:9, :]
            shift, scale1, ngv = modx_ref[:, 0:D], 1.0 + modx_ref[:, D:2 * D], ng_ref[...]
            x_load.wait()
            for r in range(L // ROWS):
                rsl = slice(r * ROWS, (r + 1) * ROWS)
                xv = x_v[rsl, :]
                rs = lax.rsqrt(jnp.mean(xv * xv, axis=-1, keepdims=True) + NORM_EPS)
                hn_ref[rsl, :] = ((xv * rs * ngv) * scale1 + shift).astype(BF16)

        @pl.when(t == 1)
        def _():
            shard_copy(0, x, y, 1 - cidx, sibling).wait_recv()
            g_pass()

        for i in (0, 1):
            @pl.when(t == ARRIVAL.index((2, 4)[i]))
            def _(i=i):
                shard_copy(1 + i, *chips[i], cidx, sibling).wait_recv()
                shard_copy(4 + i, *chips[i], cidx, sibling).start()
                shard_copy((7, 3)[i], *chips[i], cidx, (*chips[1 - i], cidx), half=i).start()

        @pl.when(t == ARRIVAL.index(6))
        def _():
            shard_copy(3, *chips[2], cidx, sibling, half=1).wait_recv()
            shard_copy(7, *chips[2], cidx, sibling, half=0).wait_recv()
            shard_copy(6, *chips[2], cidx, sibling).start()

        for i in range(3):
            @pl.when(t == ARRIVAL.index((3, 5, 7)[i]))
            def _(i=i):
                shard_copy(4 + i, *chips[i], 1 - cidx, sibling).wait_recv()

        @pl.when(t == 2)
        def _():
            g_finish()

        dev = me_i ^ arr_ref[t]
        for r in range(L // (2 * ROWS)):
            rsl = slice(r * 2 * ROWS, (r + 1) * 2 * ROWS)
            z_ref[rsl, :] = _dot(hn_ref[rsl, :], wv[dev])
        col = pl.ds(pl.multiple_of(dev * ws, 128), ws)
        pltpu.make_async_copy(wv.at[dev], wfull_ref.at[:, col], hbm_sems.at[t]).start()

        @pl.when(t == N_DEV - 1)
        def _():
            for k in (0, 1, 2, 4, 5, 6):
                shard_copy(k, x, y, cidx, sibling).wait_send()
            for k in (3, 7):
                shard_copy(k, x, y, cidx, sibling, half=0).wait_send()
            for s in range(N_DEV):
                pltpu.make_async_copy(wv.at[0], wfull_ref.at[:, pl.ds(0, ws)], hbm_sems.at[s]).wait()

    const = lambda *shape: pl.BlockSpec(shape, lambda t, m, a: (0,) * len(shape))
    once = lambda *shape: pl.BlockSpec(shape, lambda t, m, a: (0,) * len(shape), pipeline_mode=pl.Buffered(1))
    return _call(
        body, name="front_project",
        out_shape=[jax.ShapeDtypeStruct((L, D_IN), F32), jax.ShapeDtypeStruct((L, D), BF16),
                   jax.ShapeDtypeStruct((D, D_IN), BF16), jax.ShapeDtypeStruct(w_out.shape, BF16),
                   jax.ShapeDtypeStruct((1, 3 * D), F32), jax.ShapeDtypeStruct((1, 3 * D), F32),
                   jax.ShapeDtypeStruct((N_DEV, D), F32), jax.ShapeDtypeStruct((CONV_W, D), F32),
                   jax.ShapeDtypeStruct((2, D), F32)],
        grid_spec=pltpu.PrefetchScalarGridSpec(
            num_scalar_prefetch=2, grid=(N_DEV,),
            in_specs=[HBM, const(1, D), const(1, D), once(D, nloc), const(1, 3 * D), const(1, D),
                      once(D, ws), once(*w_out.shape), HBM, HBM],
            out_specs=[pl.BlockSpec((L, ws), lambda t, m, a: (0, m[0] ^ a[t])), const(L, D), HBM,
                       const(*w_out.shape),
                       const(1, 3 * D), const(1, 3 * D), const(N_DEV, D), HBM, HBM],
            scratch_shapes=[pltpu.VMEM((N_DEV, D, ws), BF16), pltpu.VMEM((N_DEV, D), F32), pltpu.VMEM((16, nloc), F32),
                            pltpu.VMEM((N_DEV, 16, nloc), F32), pltpu.SemaphoreType.DMA((8,)),
                            pltpu.SemaphoreType.DMA((8,)), pltpu.SemaphoreType.DMA((N_DEV,)),
                            pltpu.SemaphoreType.DMA((2, N_DEV - 1)), pltpu.SemaphoreType.DMA((2, N_DEV - 1))]
            + _gather2_sems(2) + [pltpu.VMEM((L, D), F32), pltpu.SemaphoreType.DMA(())]),
        compiler_params=pltpu.CompilerParams(dimension_semantics=("arbitrary",), vmem_limit_bytes=VMEM_LIMIT,
                                             has_side_effects=True, collective_id=10),
    )(me, arrival, pltpu.with_memory_space_constraint(xr, pltpu.HBM), c, c_ctx, ada_w, ada_b, ng, w_in, w_out,
      pltpu.with_memory_space_constraint(cw, pltpu.HBM), pltpu.with_memory_space_constraint(lam, pltpu.HBM))


def _project(xr, mod, ng, w, ncols, tm, name):
    rows = xr.shape[0]

    def body(x_ref, sh_ref, sc_ref, ng_ref, w_ref, z_ref, hn_ref):
        x = x_ref[...]
        rs = lax.rsqrt(jnp.mean(x * x, axis=-1, keepdims=True) + NORM_EPS)
        hn = (x * rs * ng_ref[...]) * (1.0 + sc_ref[...]) + sh_ref[...]
        hb = hn.astype(BF16)
        hn_ref[...] = hb
        for n in range(ncols // D):
            z_ref[:, n * D:(n + 1) * D] = _dot(hb, w_ref[:, n * D:(n + 1) * D])

    vec = pl.BlockSpec((1, D), lambda i: (0, 0))
    return _call(
        body, name=name, grid=(rows // tm,),
        out_shape=[jax.ShapeDtypeStruct((rows, ncols), F32), jax.ShapeDtypeStruct((rows, D), BF16)],
        in_specs=[pl.BlockSpec((tm, D), lambda i: (i, 0)), vec, pl.BlockSpec((1, D), lambda i: (0, 1)), vec,
                  pl.BlockSpec((D, ncols), lambda i: (0, 0), pipeline_mode=pl.Buffered(1))],
        out_specs=[pl.BlockSpec((tm, ncols), lambda i: (i, 0)), pl.BlockSpec((tm, D), lambda i: (i, 0))],
        compiler_params=_params("arbitrary"),
    )(xr, mod, mod, ng, w)


def _scan_pair(af_ref, uf_ref, hf_ref, h0f, ab_ref, ub_ref, hb_ref, h0b, t_len):
    span = 8 * SCAN_BLOCKS
    nit = t_len // span
    rows = _rows((8, HD))

    def local_scan(a, b, forward):
        for s in (1, 2, 4):
            sh = s if forward else 8 - s
            m = rows >= s if forward else rows < 8 - s
            b = a * jnp.where(m, pltpu.roll(b, sh, 0), 0.0) + b
            a = a * jnp.where(m, pltpu.roll(a, sh, 0), 1.0)
        return a, b

    def span_scan(a_ref, u_ref, h_ref, off, carry, forward):
        order = range(SCAN_BLOCKS) if forward else range(SCAN_BLOCKS - 1, -1, -1)
        last = slice(7, 8) if forward else slice(0, 1)
        for q in order:
            rs = pl.ds(off + 8 * q, 8)
            a, b = local_scan(a_ref[rs, :], u_ref[rs, :], forward)
            h_ref[rs, :] = b + a * carry
            carry = a[last, :] * carry + b[last, :]
        return carry

    def body(k, carry):
        cf, cb = carry
        cf = span_scan(af_ref, uf_ref, hf_ref, pl.multiple_of(k * span, span), cf, True)
        cb = span_scan(ab_ref, ub_ref, hb_ref, pl.multiple_of((nit - 1 - k) * span, span), cb, False)
        return cf, cb

    return lax.fori_loop(0, nit, body, (h0f, h0b))


SCAN_BLOCKS = 16


def _shifted(pad_ref, x, offsets, before=0.0, after=0.0):
    n = x.shape[0]
    pad_ref[0:8, :] = jnp.broadcast_to(jnp.asarray(before, F32), (8, x.shape[1]))
    pad_ref[8:8 + n, :] = x
    pad_ref[8 + n:16 + n, :] = jnp.broadcast_to(jnp.asarray(after, F32), (8, x.shape[1]))
    return [pad_ref[8 + o:8 + o + n, :] for o in offsets]


def _conv(xa, cw, cb, pad_ref):
    xm1, xp1, xp2 = _shifted(pad_ref, xa, (-1, 1, 2))
    return xm1 * cw[0:1, :] + xa * cw[1:2, :] + xp1 * cw[2:3, :] + xp2 * cw[3:4, :] + cb


def _gates(xc, wa, wx, ba, bx, nsp):
    xb = xc.astype(BF16)
    r = _sigmoid(_dot(xb, wa) + ba)
    i = _sigmoid(_dot(xb, wx) + bx)
    log_a = r * nsp
    a = jnp.exp(log_a)
    g2 = jnp.tanh(log_a) * (-1.0 - a * a)
    rg = lax.rsqrt(jnp.maximum(g2, 1e-30))
    return r, i, a, g2 * rg, rg


def _lru_param_specs():
    h4 = pl.BlockSpec((2, 1, HD, HD), lambda h: (0, h, 0, 0))
    v2 = pl.BlockSpec((2, HD), lambda h: (0, h))
    b16 = pl.BlockSpec((2 * HEADS, HD), lambda h: (0, 0))
    return dict(
        xa=pl.BlockSpec((L, HD), lambda h: (0, h)), xac=pl.BlockSpec((LC, HD), lambda h: (0, h)),
        cw=pl.BlockSpec((CONV_W, HD), lambda h: (0, h)), cb=pl.BlockSpec((1, HD), lambda h: (0, h)), h4=h4, v2=v2,
        b16=b16)


def _bias_row(ref, d):
    mask = _rows((2 * HEADS, HD)) == d * HEADS + pl.program_id(0)
    return jnp.sum(jnp.where(mask, ref[...], 0.0), axis=0, keepdims=True), mask


def _lru_forward(zx, zc, cw, cb, wa, wx, ba, bx, lam, gather, gather_modes):
    ng_ = len(gather)

    def body(xa_ref, xac_ref, cw_ref, cb_ref, wa_ref, wx_ref, ba_ref, bx_ref, lam_ref, *rest):
        yl_ref = rest[ng_]
        af, uf, hf, ab, ub, hb, pad_s = rest[2 * ng_ + 1:2 * ng_ + 8]
        start, pass_on, finish = _gather2_ops(rest[:ng_], rest[ng_ + 1:2 * ng_ + 1], gather_modes,
                                              *rest[2 * ng_ + 8:], barrier=True)
        pl.when(pl.program_id(0) == 0)(start)
        pl.when(pl.program_id(0) == HEADS // 2)(pass_on)
        pl.when(pl.program_id(0) == HEADS - 1)(finish)
        cwv, cbv = cw_ref[...], cb_ref[...]
        nsp = (-LRU_C) * _softplus(-lam_ref[...])

        def forward(xa, t_len, h0f, h0b):
            xc = _conv(xa, cwv, cbv, pad_s)
            for d, (a_ref, u_ref) in enumerate(((af, uf), (ab, ub))):
                _, i, a, gamma, _ = _gates(xc, wa_ref[d, 0].astype(BF16), wx_ref[d, 0].astype(BF16),
                                           _bias_row(ba_ref, d)[0], _bias_row(bx_ref, d)[0], nsp[d:d + 1, :])
                a_ref[0:t_len, :] = a
                u_ref[0:t_len, :] = gamma * (i * xc)
            return _scan_pair(af, uf, hf, h0f, ab, ub, hb, h0b, t_len)

        z = jnp.zeros((1, HD), F32)
        h0f, h0b = forward(xac_ref[...], LC, z, z)
        forward(xa_ref[...], L, h0f, h0b)
        yl_ref[...] = hf[...] + hb[...]

    s = _lru_param_specs()
    return _call(
        body, name="lru_forward", grid=(HEADS,),
        out_shape=[jax.ShapeDtypeStruct((L, D), F32)] + _gather2_shapes(gather, gather_modes),
        in_specs=[s["xa"], s["xac"], s["cw"], s["cb"], s["h4"], s["h4"], s["b16"], s["b16"], s["v2"]] + [HBM] * ng_,
        out_specs=[pl.BlockSpec((L, HD), lambda h: (0, h))] + [HBM] * ng_,
        scratch_shapes=[pltpu.VMEM((L, HD), F32)] * 6 + [pltpu.VMEM((L + 16, HD), F32)] + _gather2_sems(ng_),
        compiler_params=pltpu.CompilerParams(dimension_semantics=("arbitrary",), vmem_limit_bytes=VMEM_LIMIT,
                                             has_side_effects=True, collective_id=5),
    )(zx, zc, cw, cb, wa, wx, ba, bx, lam, *[pltpu.with_memory_space_constraint(a, pltpu.HBM) for a in gather])


def _lru_backward(zx, zc, dyl, dz, cw, cb, wa, wx, ba, bx, lam, chip_sums, first_chips=None):
    nr = len(chip_sums)

    def body(xa_ref, xac_ref, dyl_ref, dz_in, cw_ref, cb_ref, wa_ref, wx_ref, ba_ref, bx_ref, lam_ref, *rest):
        (dxa_ref, dxac_ref, dwa_ref, dwx_ref, dba_ref, dbx_ref, dlam_ref, dcw_ref,
         dcb_ref) = rest[nr:nr + 9]
        main_s, ctx_s, pad_s = rest[3 * nr + 9:3 * nr + 12]
        if nr:
            start, forward, finish = _chips_ops(rest[:nr], rest[nr + 9:2 * nr + 9], rest[2 * nr + 9:3 * nr + 9],
                                                *rest[3 * nr + 12:], first_chips=first_chips, barrier=True)
            pl.when(pl.program_id(0) == 0)(start)
            pl.when(pl.program_id(0) == HEADS // 2)(forward)
            pl.when(pl.program_id(0) == HEADS - 1)(finish)
        del dz_in

        @pl.when(pl.program_id(0) == 0)
        def _():
            dba_ref[...] = jnp.zeros_like(dba_ref)
            dbx_ref[...] = jnp.zeros_like(dbx_ref)

        cwv, cbv = cw_ref[...], cb_ref[...]
        lamv = lam_ref[...]
        sp = _softplus(-lamv)
        nsp = (-LRU_C) * sp
        z = jnp.zeros((1, HD), F32)

        def wmat(ref, d):
            return ref[d, 0].astype(BF16)

        def workspace(s):
            return dict(a=(s.at[0], s.at[1]), u=(s.at[2], s.at[3]), h=(s.at[4], s.at[5]), rho=(s.at[6], s.at[7]),
                        saved=(tuple(s.at[8 + k] for k in range(4)), tuple(s.at[12 + k] for k in range(4))),
                        xc=s.at[16])

        def forward(ws, xa, t_len, h0f, h0b):
            xc = _conv(xa, cwv, cbv, pad_s)
            ws["xc"][...] = xc
            for d in (0, 1):
                vals = _gates(xc, wmat(wa_ref, d), wmat(wx_ref, d), _bias_row(ba_ref, d)[0],
                              _bias_row(bx_ref, d)[0], nsp[d:d + 1, :])
                r, i, a, gamma, rg = vals
                ws["a"][d][...] = a
                ws["u"][d][...] = gamma * (i * xc)
                for ref, val in zip(ws["saved"][d], (r, i, gamma, rg)):
                    ref[...] = val
            return _scan_pair(ws["a"][0], ws["u"][0], ws["h"][0], h0f, ws["a"][1], ws["u"][1], ws["h"][1], h0b,
                              t_len)

        def backward(ws, xa, t_len, h0f, h0b, dhf, dhb, first):
            xc = ws["xc"][...]
            (af, ab), (uf, ub), (hf, hb), (rf, rb) = ws["a"], ws["u"], ws["h"], ws["rho"]
            uf[...] = ab[...] * dhb
            ub[...] = af[...] * dhf
            rho_b_last, rho_f_first = _scan_pair(ab, uf, rb, z, af, ub, rf, z, t_len)
            dxc = jnp.zeros((t_len, HD), F32)
            dsp = []
            for d in (0, 1):
                r, i, gamma, rg = (ref[...] for ref in ws["saved"][d])
                a = ws["a"][d][...]
                if d == 0:
                    lam_t = dhf + _shifted(pad_s, rf[...], (1,))[0]
                    h_prev = _shifted(pad_s, hf[...], (-1,), before=h0f)[0]
                else:
                    lam_t = dhb + _shifted(pad_s, rb[...], (-1,))[0]
                    h_prev = _shifted(pad_s, hb[...], (1,), after=h0b)[0]
                da = lam_t * h_prev
                lx = lam_t * xc
                d_i = lx * gamma
                d_gamma = lx * i
                dxc = dxc + lam_t * (gamma * i)
                d_log_a = a * (da - d_gamma * (a * rg))
                dsp.append(jnp.sum(d_log_a * r, axis=0, keepdims=True) * (-LRU_C))
                d_pre_r = d_log_a * nsp[d:d + 1, :] * (r * (1.0 - r))
                d_pre_i = d_i * (i * (1.0 - i))
                prb, pib, xb = d_pre_r.astype(BF16), d_pre_i.astype(BF16), xc.astype(BF16)
                dxc = dxc + _dot_nt(prb, wmat(wa_ref, d)) + _dot_nt(pib, wmat(wx_ref, d))
                g_wa, g_wx = _dot_tn(xb, prb), _dot_tn(xb, pib)
                g_ba = jnp.sum(d_pre_r, axis=0, keepdims=True)
                g_bx = jnp.sum(d_pre_i, axis=0, keepdims=True)
                mask = _bias_row(ba_ref, d)[1]
                dba_ref[...] += jnp.where(mask, g_ba, 0.0)
                dbx_ref[...] += jnp.where(mask, g_bx, 0.0)
                if first:
                    dwa_ref[d, 0] = g_wa
                    dwx_ref[d, 0] = g_wx
                else:
                    dwa_ref[d, 0] += g_wa
                    dwx_ref[d, 0] += g_wx
            g_lam = jnp.concatenate(dsp, axis=0) * (-_sigmoid(-lamv))
            dm1, dp1, dm2 = _shifted(pad_s, dxc, (-1, 1, -2))
            dxa = dp1 * cwv[0:1, :] + dxc * cwv[1:2, :] + dm1 * cwv[2:3, :] + dm2 * cwv[3:4, :]
            xm1, xp1, xp2 = _shifted(pad_s, xa, (-1, 1, 2))
            g_cw = jnp.concatenate([jnp.sum(dxc * v, axis=0, keepdims=True) for v in (xm1, xa, xp1, xp2)], axis=0)
            g_cb = jnp.sum(dxc, axis=0, keepdims=True)
            if first:
                dlam_ref[...] = g_lam
                dcw_ref[...] = g_cw
                dcb_ref[...] = g_cb
            else:
                dlam_ref[...] += g_lam
                dcw_ref[...] += g_cw
                dcb_ref[...] += g_cb
            return dxa, rho_f_first, rho_b_last

        ws_x, ws_c = workspace(main_s), workspace(ctx_s)
        h0f, h0b = forward(ws_c, xac_ref[...], LC, z, z)
        forward(ws_x, xa_ref[...], L, h0f, h0b)
        dh = dyl_ref[...]
        dxa, dh0f, dh0b = backward(ws_x, xa_ref[...], L, h0f, h0b, dh, dh, True)
        dxa_ref[...] = dxa.astype(BF16)
        rc = _rows((LC, HD))
        dxac, _, _ = backward(ws_c, xac_ref[...], LC, z, z, jnp.where(rc == LC - 1, dh0f, 0.0),
                              jnp.where(rc == 0, dh0b, 0.0), False)
        dxac_ref[...] = dxac.astype(BF16)

    s = _lru_param_specs()
    col = lambda r: pl.BlockSpec((r, HD), lambda h: (0, h))
    return _call(
        body, name="lru_backward", grid=(HEADS,),
        out_shape=[jax.ShapeDtypeStruct((L, D_IN), BF16), jax.ShapeDtypeStruct((LC, D), BF16),
                   jax.ShapeDtypeStruct((2, HEADS, HD, HD), F32), jax.ShapeDtypeStruct((2, HEADS, HD, HD), F32),
                   jax.ShapeDtypeStruct((2 * HEADS, HD), F32), jax.ShapeDtypeStruct((2 * HEADS, HD), F32),
                   jax.ShapeDtypeStruct((2, D), F32), jax.ShapeDtypeStruct((CONV_W, D), F32),
                   jax.ShapeDtypeStruct((1, D), F32)] + [jax.ShapeDtypeStruct((4,) + a.shape[1:], a.dtype)
                                                          for a in chip_sums] + _chips_stage_shapes(chip_sums),
        in_specs=[s["xa"], s["xac"], col(L), pl.BlockSpec(memory_space=pl.ANY), s["cw"], s["cb"], s["h4"], s["h4"],
                  s["b16"], s["b16"], s["v2"]] + [HBM] * nr,
        out_specs=[col(L), col(LC), s["h4"], s["h4"], s["b16"], s["b16"], s["v2"], col(CONV_W), col(1)]
        + [HBM] * (2 * nr),
        scratch_shapes=[pltpu.VMEM((17, L, HD), F32), pltpu.VMEM((17, LC, HD), F32), pltpu.VMEM((L + 16, HD), F32)]
        + (_chips_sems(nr) if nr else []),
        input_output_aliases={3: 0},
        compiler_params=pltpu.CompilerParams(dimension_semantics=("arbitrary",), vmem_limit_bytes=VMEM_LIMIT,
                                             has_side_effects=True, collective_id=6 if nr else None),
    )(zx, zc, dyl, dz, cw, cb, wa, wx, ba, bx, lam, *[pltpu.with_memory_space_constraint(a, pltpu.HBM)
                                                       for a in chip_sums])


def _mixer_loss(x, tgt, zx, yl, gx, fg, lng, lnb, ws, bst, wout, tm):
    ncht = tm // CHUNK

    def body(x_ref, t_ref, ga_ref, u_ref, v_ref, gb_ref, yl_ref, gx_ref, fg_ref, lng_ref, lnb_ref, ws_ref,
             bst_ref, wout_ref,
             dz_ref, dyl_ref, dxn_ref, y_s, do_ref, dws_ref, dbst_ref, vec_ref,
             vn_s, mix_s, dm_s, dvn_s):
        step = pl.program_id(0)

        @pl.when(step == 0)
        def _():
            dws_ref[...] = jnp.zeros_like(dws_ref)
            dbst_ref[...] = jnp.zeros_like(dbst_ref)
            vec_ref[...] = jnp.zeros_like(vec_ref)

        u, v = u_ref[...], v_ref[...]
        ug, dug_du = _gelu_and_grad(u)
        vg, dvg_dv = _gelu_and_grad(v)
        mu = jnp.mean(vg, axis=-1, keepdims=True)
        vc = vg - mu
        rstd = lax.rsqrt(jnp.mean(vc * vc, axis=-1, keepdims=True) + LN_EPS)
        vhat = vc * rstd
        lngv = lng_ref[...]
        vn_s[...] = (vhat * lngv + lnb_ref[...]).astype(BF16)
        for ch in range(ncht):
            rs = slice(ch * CHUNK, (ch + 1) * CHUNK)
            for g in range(HEADS):
                cs = slice(g * HD, (g + 1) * HD)
                mix_s[rs, cs] = _dot(ws_ref[g].astype(BF16), vn_s[rs, cs]) + bst_ref[:, g:g + 1]
        mixed = mix_s[...]
        ga, gb, yl = ga_ref[...], gb_ref[...], yl_ref[...]
        sga, dsga = _silu_and_grad(ga)
        sgb, dsgb = _silu_and_grad(gb)
        ys = ug * mixed
        y_s[:, 0:D] = (yl * sga).astype(BF16)
        y_s[:, D:D_MIX] = (ys * sgb).astype(BF16)
        o = _dot(y_s[...], wout_ref[...])
        gxv, fgv = gx_ref[...], fg_ref[...]
        xn = x_ref[...] + gxv * o
        rs2 = lax.rsqrt(jnp.mean(xn * xn, axis=-1, keepdims=True) + NORM_EPS)
        xh = xn * rs2
        diff = xh * fgv - t_ref[...]
        vec_ref[R_LOSS:R_LOSS + 1, :] += jnp.full((1, D), jnp.sum(diff * diff) * (0.5 / D), F32)
        dout = diff * (1.0 / D)
        w = dout * fgv
        dxn = rs2 * (w - xh * jnp.mean(w * xh, axis=-1, keepdims=True))
        dxn_ref[...] = dxn
        vec_ref[0:1, :] += jnp.sum(dxn * o, axis=0, keepdims=True)
        vec_ref[1:2, :] += jnp.sum(dout * xh, axis=0, keepdims=True)
        dob = (dxn * gxv).astype(BF16)
        do_ref[...] = dob
        dy = _dot_nt(dob, wout_ref[...])
        dya, dyb = dy[:, 0:D], dy[:, D:D_MIX]
        dyl_ref[...] = dya * sga
        dys = dyb * sgb
        dz_ref[:, 0:D] = jnp.zeros((tm, D), BF16)
        dz_ref[:, D:2 * D] = (dya * yl * dsga).astype(BF16)
        dz_ref[:, 2 * D:3 * D] = (dys * mixed * dug_du).astype(BF16)
        dz_ref[:, 4 * D:5 * D] = (dyb * ys * dsgb).astype(BF16)
        dm = dys * ug
        dm_s[...] = dm.astype(BF16)
        for g in range(HEADS):
            cs = slice(g * HD, (g + 1) * HD)
            dbst_ref[:, g:g + 1] += sum(jnp.sum(dm[ch * CHUNK:(ch + 1) * CHUNK, cs], axis=1, keepdims=True)
                                        for ch in range(ncht))
            for ch in range(ncht):
                rs = slice(ch * CHUNK, (ch + 1) * CHUNK)
                dws_ref[g] += _dot_nt(dm_s[rs, cs], vn_s[rs, cs])
                dvn_s[rs, cs] = _dot_tn(ws_ref[g].astype(BF16), dm_s[rs, cs])
        dvn = dvn_s[...]
        vec_ref[2:3, :] += jnp.sum(dvn * vhat, axis=0, keepdims=True)
        vec_ref[3:4, :] += jnp.sum(dvn, axis=0, keepdims=True)
        dvh = dvn * lngv
        dvg = rstd * (dvh - jnp.mean(dvh, axis=-1, keepdims=True) - vhat * jnp.mean(dvh * vhat, axis=-1, keepdims=True))
        dz_ref[:, 3 * D:4 * D] = (dvg * dvg_dv).astype(BF16)

    tile = pl.BlockSpec((tm, D), lambda i: (i, 0))
    zcol = lambda n: pl.BlockSpec((tm, D), lambda i: (i, n))
    vec = pl.BlockSpec((1, D), lambda i: (0, 0))
    full = lambda *s: pl.BlockSpec(s, lambda i: (0,) * len(s))
    return _call(
        body, name="mixer_loss", grid=(L // tm,),
        out_shape=[jax.ShapeDtypeStruct((L, D_IN), BF16), jax.ShapeDtypeStruct((L, D), F32),
                   jax.ShapeDtypeStruct((L, D), F32), jax.ShapeDtypeStruct((L, D_MIX), BF16),
                   jax.ShapeDtypeStruct((L, D), BF16),
                   jax.ShapeDtypeStruct((HEADS, CHUNK, CHUNK), F32), jax.ShapeDtypeStruct((CHUNK, HEADS), F32),
                   jax.ShapeDtypeStruct((8, D), F32)],
        in_specs=[tile, tile, zcol(1), zcol(2), zcol(3), zcol(4), tile, pl.BlockSpec((1, D), lambda i: (0, 2)),
                  vec, vec, vec,
                  full(HEADS, CHUNK, CHUNK), full(CHUNK, HEADS),
                  pl.BlockSpec((D_MIX, D), lambda i: (0, 0), pipeline_mode=pl.Buffered(1))],
        out_specs=[pl.BlockSpec((tm, D_IN), lambda i: (i, 0)), tile, tile,
                   pl.BlockSpec((tm, D_MIX), lambda i: (i, 0)), tile,
                   full(HEADS, CHUNK, CHUNK), full(CHUNK, HEADS), full(8, D)],
        scratch_shapes=[pltpu.VMEM((tm, D), BF16), pltpu.VMEM((tm, D), F32),
                        pltpu.VMEM((tm, D), BF16), pltpu.VMEM((tm, D), F32)],
        compiler_params=_params("arbitrary"),
    )(x, tgt, zx, zx, zx, zx, yl, gx, fg, lng, lnb, ws, bst, wout)


def _grad_w(a, b, a2, b2, tk, name, bw, first, nblocks, split, barrier_id, riders=()):
    nk = a.shape[0] // tk
    m = a.shape[1]
    with_ctx = a2 is not None
    if split == "cols":
        slots, r, w = nblocks, m, bw // 2
        piece = lambda q, pc: (slice(None), slice(pc * w, (pc + 1) * w))
    else:
        slots, r, w = 4, m // 8, bw
        piece = lambda q, pc: (slice((2 * q + pc) * r, (2 * q + pc + 1) * r), slice(None))

    nr = len(riders)

    def body(*refs):
        a_ref, b_ref = refs[:2]
        a2_ref, b2_ref = refs[2:4] if with_ctx else (None, None)
        base = 4 if with_ctx else 2
        sums_ref = refs[base + nr]
        s0 = base + 3 * nr + 1
        acc, mine_v, send_v, stage_v, send_sems, recv_sems = refs[s0:s0 + 6]
        n, k = pl.program_id(0), pl.program_id(1)
        x, y, c = lax.axis_index("x"), lax.axis_index("y"), lax.axis_index("c")

        def to_sibling(s):
            return pltpu.make_async_remote_copy(src_ref=send_v.at[s], dst_ref=stage_v.at[s], send_sem=send_sems.at[s],
                                                recv_sem=recv_sems.at[s], device_id=(x, y, 1 - c),
                                                device_id_type=MESH)

        if nr:
            rider_in = refs[base:base + nr]
            own_v, got_v, psum_v = (refs[s0 + 9 + i * nr:s0 + 9 + (i + 1) * nr] for i in range(3))
            p_send, p_recv, p_local = refs[s0 + 9 + 3 * nr:s0 + 12 + 3 * nr]
            c_start, c_forward, c_finish = _chips_ops(psum_v, refs[base + nr + 1:base + 2 * nr + 1],
                                                      refs[base + 2 * nr + 1:base + 3 * nr + 1], *refs[s0 + 6:s0 + 9])

            @pl.when(jnp.logical_and(n == 0, k == 0))
            def _():
                _barrier([(x, y, 1 - c)] + [(x ^ (j >> 1), y ^ (j & 1), c) for j in (1, 2)])
                copies = []
                for j in range(nr):
                    for q in range(4):
                        copies.append(pltpu.make_async_remote_copy(
                            src_ref=rider_in[j].at[2 * q + 1 - c], dst_ref=got_v[j].at[q], send_sem=p_send.at[j, q],
                            recv_sem=p_recv.at[j, q], device_id=(x, y, 1 - c), device_id_type=MESH))
                        copies.append(pltpu.make_async_copy(rider_in[j].at[2 * q + c], own_v[j].at[q],
                                                            p_local.at[j, q]))
                for cp in copies:
                    cp.start()
                for cp in copies:
                    cp.wait()
                for j in range(nr):
                    psum_v[j][...] = (own_v[j][...] + got_v[j][...]).astype(BF16)
                c_start()
        else:
            pl.when(jnp.logical_and(n == 0, k == 0))(_sibling_barrier)

        @pl.when(k == 0)
        def _():
            acc[...] = _dot_tn(a_ref[...], b_ref[...])

        if nk > 1:
            @pl.when(k > 0)
            def _():
                acc[...] += _dot_tn(a_ref[...], b_ref[...])

        if with_ctx:
            @pl.when(jnp.logical_and(k == nk - 1, n == 0))
            def _():
                acc[:, 0:b2_ref.shape[1]] += _dot_tn(a2_ref[...], b2_ref[...])

        if nr:
            pl.when(jnp.logical_and(k == nk - 1, n == nblocks - 1))(c_forward)

        def hand_over(s, q):
            for pc in (0, 1):
                @pl.when(c == pc)
                def _(pc=pc):
                    mine_v[s] = acc[piece(q, pc)]
                    send_v[s] = acc[piece(q, 1 - pc)].astype(BF16)
            to_sibling(s).start()

        for i in range(nblocks):
            @pl.when(jnp.logical_and(k == nk - 1, n == i))
            def _(i=i):
                if split == "cols":
                    hand_over(i, 0)
                else:
                    for q in range(4):
                        hand_over(q, q)

        @pl.when(jnp.logical_and(k == nk - 1, n == nblocks - 1))
        def _():
            for s in range(slots):
                to_sibling(s).wait_recv()
                sums_ref[s] = (mine_v[s] + stage_v[s].astype(F32)).astype(BF16)
            for s in range(slots):
                to_sibling(s).wait_send()
            if nr:
                c_finish()

    in_specs = [pl.BlockSpec((tk, m), lambda n, k: (k, 0)), pl.BlockSpec((tk, bw), lambda n, k: (k, n + first))]
    args = [a, b]
    if with_ctx:
        in_specs += [pl.BlockSpec(a2.shape, lambda n, k: (0, 0)), pl.BlockSpec(b2.shape, lambda n, k: (0, 0))]
        args += [a2, b2]
    in_specs += [HBM] * nr
    args += [pltpu.with_memory_space_constraint(s, pltpu.HBM) for s in riders]
    rider_sums = [jax.ShapeDtypeStruct((4,) + s.shape[1:], BF16) for s in riders]
    rider_scratch = []
    if nr:
        rider_scratch = (_chips_sems(nr) + [pltpu.VMEM(s.shape, F32) for s in rider_sums] * 2
                         + [pltpu.VMEM(s.shape, BF16) for s in rider_sums]
                         + [pltpu.SemaphoreType.DMA((nr, 4))] * 3)
    return _call(
        body, name=name, grid=(nblocks, nk),
        out_shape=[jax.ShapeDtypeStruct((slots, r, w), BF16)] + rider_sums + _chips_stage_shapes(rider_sums),
        in_specs=in_specs, out_specs=[pl.BlockSpec((slots, r, w), lambda n, k: (0, 0, 0))] + [HBM] * (2 * nr),
        scratch_shapes=[pltpu.VMEM((m, bw), F32), pltpu.VMEM((slots, r, w), F32), pltpu.VMEM((slots, r, w), BF16),
                        pltpu.VMEM((slots, r, w), BF16), pltpu.SemaphoreType.DMA((slots,)),
                        pltpu.SemaphoreType.DMA((slots,))] + rider_scratch,
        compiler_params=pltpu.CompilerParams(dimension_semantics=("arbitrary", "arbitrary"),
                                             vmem_limit_bytes=VMEM_LIMIT, has_side_effects=True,
                                             collective_id=barrier_id),
    )(*args)


def _grad_rows(xr, dz, w, mod, ng, dres, ncols, tm, name, chip_sums=(), first_chips=None, dests=None):
    rows = xr.shape[0]
    steps = rows // tm
    with_dx = dres is not None
    nr = len(chip_sums)
    dests = [d for d in (dests or [None] * nr)]
    nd = sum(d is not None for d in dests)
    nin = 6 if with_dx else 5
    nout = 2 if with_dx else 1

    def body(*refs):
        if with_dx:
            x_ref, dz_ref, w_ref, sc_ref, ng_ref, dres_ref = refs[:nin]
            dx_ref, vec_ref = refs[nin + nr + nd:nin + nr + nd + nout]
        else:
            x_ref, dz_ref, w_ref, sc_ref, ng_ref = refs[:nin]
            (vec_ref,) = refs[nin + nr + nd:nin + nr + nd + nout]
        if nr:
            o0 = nin + nr + nd + nout
            start, forward, finish = _chips_ops(refs[nin:nin + nr], refs[o0:o0 + nr], refs[o0 + nr:o0 + 2 * nr],
                                                *refs[o0 + 2 * nr:o0 + 2 * nr + 3], first_chips=first_chips,
                                                barrier=True)
            w_hbm, w_ref, w_sem = w_ref, refs[o0 + 2 * nr + 3], refs[o0 + 2 * nr + 4]

            @pl.when(pl.program_id(0) == 0)
            def _():
                start()
                w_load = pltpu.make_async_copy(w_hbm, w_ref, w_sem)
                w_load.start()
                w_load.wait()

            pl.when(pl.program_id(0) == steps // 2)(forward)
            pl.when(pl.program_id(0) == steps - 1)(finish)

        @pl.when(pl.program_id(0) == 0)
        def _():
            vec_ref[...] = jnp.zeros_like(vec_ref)

        dhn = _dot_nt(dz_ref[...], w_ref[...])
        x = x_ref[...]
        rs = lax.rsqrt(jnp.mean(x * x, axis=-1, keepdims=True) + NORM_EPS)
        xh = x * rs
        ngv = ng_ref[...]
        y = xh * ngv
        vec_ref[0:1, :] += jnp.sum(dhn, axis=0, keepdims=True)
        vec_ref[1:2, :] += jnp.sum(dhn * y, axis=0, keepdims=True)
        dy = dhn * (1.0 + sc_ref[...])
        vec_ref[2:3, :] += jnp.sum(dy * xh, axis=0, keepdims=True)
        if with_dx:
            dxh = dy * ngv
            dx_ref[...] = dres_ref[...] + rs * (dxh - xh * jnp.mean(dxh * xh, axis=-1, keepdims=True))

    tile = pl.BlockSpec((tm, D), lambda i: (i, 0))
    vec = pl.BlockSpec((1, D), lambda i: (0, 0))
    w_spec = HBM if nr else pl.BlockSpec((D, ncols), lambda i: (0, 0), pipeline_mode=pl.Buffered(1))
    in_specs = [tile, pl.BlockSpec((tm, ncols), lambda i: (i, 0)), w_spec, pl.BlockSpec((1, D), lambda i: (0, 1)), vec]
    out_shape = [jax.ShapeDtypeStruct((8, D), F32)]
    out_specs = [pl.BlockSpec((8, D), lambda i: (0, 0))]
    args = [xr, dz, pltpu.with_memory_space_constraint(w, pltpu.HBM) if nr else w, mod, ng]
    if with_dx:
        in_specs.append(tile)
        out_shape.insert(0, jax.ShapeDtypeStruct((rows, D), F32))
        out_specs.insert(0, tile)
        args.append(dres)
    aliases = {}
    for j, d in enumerate(dests):
        if d is not None:
            aliases[len(args) + nr + len(aliases)] = len(out_shape) + j
    in_specs += [HBM] * (nr + nd)
    out_specs += [HBM] * (2 * nr)
    out_shape += [jax.ShapeDtypeStruct((4,) + a.shape[1:], a.dtype) for a in chip_sums]
    out_shape += _chips_stage_shapes(chip_sums)
    args += [pltpu.with_memory_space_constraint(a, pltpu.HBM) for a in chip_sums]
    args += [pltpu.with_memory_space_constraint(d, pltpu.HBM) for d in dests if d is not None]
    return _call(body, name=name, grid=(steps,), out_shape=out_shape, in_specs=in_specs, out_specs=out_specs,
                 scratch_shapes=(_chips_sems(nr) + [pltpu.VMEM((D, ncols), BF16), pltpu.SemaphoreType.DMA(())])
                 if nr else [], input_output_aliases=aliases,
                 compiler_params=pltpu.CompilerParams(dimension_semantics=("arbitrary",),
                                                      vmem_limit_bytes=VMEM_LIMIT, has_side_effects=bool(nr),
                                                      collective_id=7 if nr else None))(*args)


def _adamw(w, g, m, v):
    m = ADAM_B1 * m + (1.0 - ADAM_B1) * g
    v = ADAM_B2 * v + (1.0 - ADAM_B2) * (g * g)
    m_hat = m / (1.0 - ADAM_B1 ** ADAM_STEP)
    v_hat = v / (1.0 - ADAM_B2 ** ADAM_STEP)
    delta = -ADAM_LR * (m_hat / (jnp.sqrt(v_hat) + ADAM_EPS) + ADAM_WD * w)
    return delta, m, v


def _adamw_reduced(parts, w, m, v, tr, name):
    r, n = w.shape
    nparts = parts.shape[0]

    def body(p_ref, w_ref, m_ref, v_ref, g_ref, d_ref, mo_ref, vo_ref):
        g = p_ref[0].astype(F32)
        for i in range(1, nparts):
            g = g + p_ref[i].astype(F32)
        g_ref[...] = g
        d_ref[...], mo_ref[...], vo_ref[...] = _adamw(w_ref[...], g, m_ref[...], v_ref[...])

    tile = pl.BlockSpec((tr, n), lambda i: (i, 0))
    sds = jax.ShapeDtypeStruct((r, n), F32)
    return _call(
        body, name=name, grid=(r // tr,), out_shape=[sds] * 4,
        in_specs=[pl.BlockSpec((nparts, tr, n), lambda i: (0, i, 0)), tile, tile, tile], out_specs=[tile] * 4,
        compiler_params=_params("arbitrary"),
    )(parts, w, m, v)


R_GATE, R_FINAL_G, R_LN_G, R_LN_B, R_LOSS = 0, 1, 2, 3, 4
R_SH_X, R_SC_X, R_NG_X = 5, 6, 7
R_SH_C, R_SC_C, R_NG_C = 8, 9, 10
R_LAM, R_CW, R_CB = 11, 13, 17
PACK_ROWS = 24
Q_BA, Q_BX, Q_SGU_B, PACK128_ROWS = 0, 16, 32, 40


def _reduce_small(vec_pieces, q_pieces, mat_parts, ada_w, me):
    nloc = ada_w.shape[1]
    nm = len(mat_parts)
    pieces = list(vec_pieces) + list(q_pieces)

    def body(me_ref, *refs):
        piece_refs, refs = refs[:len(pieces)], refs[len(pieces):]
        mp_refs, w_ref = refs[:nm], refs[nm]
        red_ref, redq_ref = refs[nm + 1:nm + 3]
        mats_all = refs[nm + 3:2 * nm + 3]
        cparts_ref, dmod_ref, gab_ref, loss_ref = refs[2 * nm + 3:2 * nm + 7]
        pack_ref, packq_ref, vp_ref, vq_ref = refs[2 * nm + 7:2 * nm + 11]
        mat_refs = refs[2 * nm + 11:3 * nm + 11]
        cpart_ref, dmc_s = refs[3 * nm + 11:3 * nm + 13]
        sems = refs[3 * nm + 13:]
        for dst, group in ((pack_ref, vec_pieces), (packq_ref, q_pieces)):
            row = 0
            for _, nrows in group:
                dst[row:row + nrows, :] = piece_refs[0][0:nrows, :]
                piece_refs, row = piece_refs[1:], row + nrows
            if row < dst.shape[0]:
                dst[row:, :] = jnp.zeros((dst.shape[0] - row, dst.shape[1]), F32)
        p_start, p_forward, p_finish = _gather2_ops([pack_ref, packq_ref], [vp_ref, vq_ref], ["ag", "ag"], *sems[:3],
                                                    barrier=True)
        m_start, m_forward, m_finish = _gather2_ops(mat_refs, mats_all, ["ag"] * nm, *sems[3:6])
        c_start, c_forward, c_finish = _gather2_ops([cpart_ref], [cparts_ref], ["ag"], *sems[6:])
        p_start()
        for mp_ref, mat_ref in zip(mp_refs, mat_refs):
            mat = mp_ref[0].astype(F32)
            for i in range(1, mp_ref.shape[0]):
                mat = mat + mp_ref[i].astype(F32)
            mat_ref[...] = mat
        m_start()
        p_forward()
        p_finish()
        red, redq = vp_ref[0], vq_ref[0]
        for i in range(1, N_DEV):
            red = red + vp_ref[i]
            redq = redq + vq_ref[i]
        red_ref[...] = red
        redq_ref[...] = redq
        loss_ref[...] = red_ref[R_LOSS:R_LOSS + 1, 0:1]
        for e in range(N_DEV):
            dmod_ref[e:e + 1, 0:D] = vp_ref[e, R_SH_X:R_SH_X + 1, :]
            dmod_ref[e:e + 1, D:2 * D] = vp_ref[e, R_SC_X:R_SC_X + 1, :]
            dmod_ref[e:e + 1, 2 * D:3 * D] = vp_ref[e, R_GATE:R_GATE + 1, :]
        dmod_ref[8:9, 0:D] = red[R_SH_C:R_SH_C + 1, :]
        dmod_ref[8:9, D:2 * D] = red[R_SC_C:R_SC_C + 1, :]
        dmod_ref[8:9, 2 * D:3 * D] = jnp.zeros((1, D), F32)
        dmod_ref[9:16, :] = jnp.zeros((7, 3 * D), F32)
        gab_ref[:, 0:D] = red[R_SH_X:R_SH_X + 1, :] + red[R_SH_C:R_SH_C + 1, :]
        gab_ref[:, D:2 * D] = red[R_SC_X:R_SC_X + 1, :] + red[R_SC_C:R_SC_C + 1, :]
        gab_ref[:, 2 * D:3 * D] = red[R_GATE:R_GATE + 1, :]
        dmc_s[...] = jnp.broadcast_to(dmod_ref[8:9, :], (8, 3 * D))
        off = pl.multiple_of(me_ref[0] * nloc, 128)
        cpart_ref[...] = _dot_nt(dmc_s[:, pl.ds(off, nloc)], w_ref[...])
        c_start()
        m_forward()
        c_forward()
        c_finish()
        m_finish()

    return _call(
        body, name="reduce_small",
        out_shape=[jax.ShapeDtypeStruct((PACK_ROWS, D), F32), jax.ShapeDtypeStruct((PACK128_ROWS, HD), F32)]
        + [jax.ShapeDtypeStruct((N_DEV,) + p.shape[1:], F32) for p in mat_parts]
        + [jax.ShapeDtypeStruct((N_DEV, 8, D), F32), jax.ShapeDtypeStruct((16, 3 * D), F32),
           jax.ShapeDtypeStruct((1, 3 * D), F32), jax.ShapeDtypeStruct((1, 1), F32)],
        in_specs=[pl.BlockSpec(memory_space=pltpu.SMEM)] + [VMEM] * (len(pieces) + nm + 1),
        out_specs=[VMEM] * (nm + 6),
        scratch_shapes=[pltpu.VMEM((PACK_ROWS, D), F32), pltpu.VMEM((PACK128_ROWS, HD), F32),
                        pltpu.VMEM((N_DEV, PACK_ROWS, D), F32), pltpu.VMEM((N_DEV, PACK128_ROWS, HD), F32)]
        + [pltpu.VMEM(p.shape[1:], F32) for p in mat_parts]
        + [pltpu.VMEM((8, D), F32), pltpu.VMEM((8, 3 * D), F32)] + _gather2_sems(2) + _gather2_sems(nm)
        + _gather2_sems(1),
        compiler_params=pltpu.CompilerParams(vmem_limit_bytes=VMEM_LIMIT, has_side_effects=True, collective_id=8),
    )(me, *[a for a, _ in pieces], *mat_parts, ada_w)


def _adamw_ada(c_all, c_ctx, dmod, w, m, v, me):
    nloc = w.shape[1]

    def body(me_ref, c_ref, cc_ref, dm_ref, w_ref, m_ref, v_ref, g_ref, d_ref, mo_ref, vo_ref):
        off = pl.multiple_of(me_ref[0] * nloc, 128)
        dm = dm_ref[:, pl.ds(off, nloc)]
        sx, _ = _silu_and_grad(c_ref[...])
        sc, _ = _silu_and_grad(cc_ref[...])
        g = _dot_tn(sx, dm[0:8, :]) + _dot_tn(jnp.broadcast_to(sc, (8, D)), dm[8:16, :])
        g_ref[...] = g
        d_ref[...], mo_ref[...], vo_ref[...] = _adamw(w_ref[...], g, m_ref[...], v_ref[...])

    sds = jax.ShapeDtypeStruct(w.shape, F32)
    return _call(
        body, name="adamw_ada_w", out_shape=[sds] * 4,
        in_specs=[pl.BlockSpec(memory_space=pltpu.SMEM)] + [VMEM] * 6, out_specs=[VMEM] * 4,
        compiler_params=_params(),
    )(me, c_all, c_ctx, dmod, w, m, v)


_SMALL = ("c_ctx", "ada_b", "norm_g", "conv_w", "conv_b", "lru_wa", "lru_ba", "lru_wx", "lru_bx", "lru_lambda",
          "sgu_ln_g", "sgu_ln_b", "sgu_w", "sgu_b", "final_g")


def _adamw_small(red, redq, mats, cparts, gab, ws, ms, vs, me):
    n = len(_SMALL)

    def body(me_ref, red_ref, redq_ref, wa_ref, wx_ref, sw_ref, cp_ref, gab_ref, *refs):
        w_refs, m_refs, v_refs = refs[:n], refs[n:2 * n], refs[2 * n:3 * n]
        outs = refs[3 * n:]
        off = pl.multiple_of(me_ref[0] * HD, 128)

        def row(r, k=1):
            return red_ref[r:r + k, :]

        cc = w_refs[0][...]
        dcc = cp_ref[0, 0:1, :]
        for i in range(1, N_DEV):
            dcc = dcc + cp_ref[i, 0:1, :]
        grads = dict(
            c_ctx=dcc * _silu_and_grad(cc)[1], ada_b=gab_ref[...], norm_g=row(R_NG_X) + row(R_NG_C),
            conv_w=red_ref[R_CW:R_CW + CONV_W, pl.ds(off, HD)], conv_b=row(R_CB),
            lru_wa=wa_ref[...], lru_ba=redq_ref[Q_BA:Q_BA + 2 * HEADS, :], lru_wx=wx_ref[...],
            lru_bx=redq_ref[Q_BX:Q_BX + 2 * HEADS, :], lru_lambda=red_ref[R_LAM:R_LAM + 2, pl.ds(off, HD)],
            sgu_ln_g=row(R_LN_G), sgu_ln_b=row(R_LN_B), sgu_w=sw_ref[...],
            sgu_b=redq_ref[Q_SGU_B:Q_SGU_B + HEADS, :], final_g=row(R_FINAL_G))
        for j, name in enumerate(_SMALL):
            g = grads[name]
            outs[j][...] = g
            outs[n + j][...], outs[2 * n + j][...], outs[3 * n + j][...] = _adamw(w_refs[j][...], g, m_refs[j][...],
                                                                                 v_refs[j][...])

    sds = [jax.ShapeDtypeStruct(ws[k].shape, F32) for k in _SMALL]
    outs = _call(
        body, name="adamw_small", out_shape=sds * 4,
        in_specs=[pl.BlockSpec(memory_space=pltpu.SMEM)] + [VMEM] * (7 + 3 * n), out_specs=[VMEM] * (4 * n),
        compiler_params=_params(),
    )(me, red, redq, *mats, cparts, gab, *[ws[k] for k in _SMALL], *[ms[k] for k in _SMALL],
      *[vs[k] for k in _SMALL])
    return [dict(zip(_SMALL, outs[i * n:(i + 1) * n])) for i in range(4)]


def kernel(x, c, ctx, c_ctx, ada_w, ada_b, norm_g, w_in, conv_w, conv_b, lru_wa, lru_ba, lru_wx, lru_bx, lru_lambda, sgu_ln_g, sgu_ln_b, sgu_w, sgu_b, w_out, final_g, loss_target, m_c_ctx, m_ada_w, m_ada_b, m_norm_g, m_w_in, m_conv_w, m_conv_b, m_lru_wa, m_lru_ba, m_lru_wx, m_lru_bx, m_lru_lambda, m_sgu_ln_g, m_sgu_ln_b, m_sgu_w, m_sgu_b, m_w_out, m_final_g, v_c_ctx, v_ada_w, v_ada_b, v_norm_g, v_w_in, v_conv_w, v_conv_b, v_lru_wa, v_lru_ba, v_lru_wx, v_lru_bx, v_lru_lambda, v_sgu_ln_g, v_sgu_ln_b, v_sgu_w, v_sgu_b, v_w_out, v_final_g):
    args = dict(locals())
    me = (4 * lax.axis_index("x") + 2 * lax.axis_index("y") + lax.axis_index("c")).astype(jnp.int32).reshape(1)
    xr, ctxr, tgt = x[0], ctx[0], loss_target[0]
    cc = c_ctx.reshape(1, D)
    nw = 2 * HEADS * HD
    view = dict(c_ctx=(1, D), ada_b=(1, 3 * D), norm_g=(1, D), conv_w=(CONV_W, HD), conv_b=(1, D), lru_wa=(nw, HD),
                lru_ba=(2 * HEADS, HD), lru_wx=(nw, HD), lru_bx=(2 * HEADS, HD), lru_lambda=(2, HD), sgu_ln_g=(1, D),
                sgu_ln_b=(1, D), sgu_w=(HEADS * CHUNK, CHUNK), sgu_b=(HEADS, CHUNK), final_g=(1, D))

    zx, hn, w_full, w_out_b, modx, modc, c_all, cw_full, lam_full = _front_project(
        xr, c, cc, ada_w[0], ada_b, norm_g, w_in[0], w_out[0], conv_w[0], lru_lambda[0], me)
    zc, hnc = _project(ctxr, modc, norm_g, w_full, D, LC, "project_ctx")
    ba, bx = lru_ba.reshape(view["lru_ba"]), lru_bx.reshape(view["lru_bx"])
    yl, wout_all = _lru_forward(zx, zc, cw_full, conv_b, lru_wa[0], lru_wx[0], ba, bx, lam_full, [w_out_b], ["ag"])
    wout_full = wout_all.reshape(D_MIX, D)
    dz, dyl, dxn, ycat, dob, dws, dbst, mvec = _mixer_loss(
        xr, tgt, zx, yl, modx, final_g.reshape(1, D), sgu_ln_g, sgu_ln_b, sgu_w[0], sgu_b[0].T, wout_full, ROWS)

    (wout_sums,) = _grad_w(ycat, dob, None, None, L, "grad_w_out", D, 0, 1, "rows", 1)
    (rest_sums,) = _grad_w(hn, dz, None, None, L, "grad_w_in_rest", 2 * W_IN_SHARD, 1, 3, "cols", 2)
    dz, dxac, dwa, dwx, dba, dbx, dlam, dcw, dcb, win_parts, wout_parts, _, _ = _lru_backward(
        zx, zc, dyl, dz, cw_full, conv_b, lru_wa[0], lru_wx[0], ba, bx, lam_full, [rest_sums, wout_sums],
        first_chips=[1, 0])
    mats = [dwa.reshape(N_DEV, nw // N_DEV, HD), dwx.reshape(N_DEV, nw // N_DEV, HD), dws]
    first_sums, *mat_parts = _grad_w(hn, dz, hnc, dxac, L, "grad_w_in_first", 2 * W_IN_SHARD, 0, 1, "cols", 3,
                                     riders=mats)[:4]
    gx, xvec, win_parts = _grad_rows(
        xr, dz, w_full, modx, norm_g, dxn, D_IN, ROWS, "grad_rows_x", chip_sums=[first_sums], first_chips=[0],
        dests=[win_parts])[:3]
    (cvec,) = _grad_rows(ctxr, dxac, w_full, modc, norm_g, None, D, LC, "grad_rows_ctx")
    red, redq, *rest = _reduce_small(
        [(mvec, 5), (xvec, 3), (cvec, 3), (dlam, 2), (dcw, CONV_W), (dcb, 1)],
        [(dba, 2 * HEADS), (dbx, 2 * HEADS), (dbst.T, HEADS)], mat_parts, ada_w[0], me)
    mats_all, (cparts, dmod, gab, loss) = rest[:3], rest[3:]

    g_w_in, d_w_in, nm_w_in, nv_w_in = _adamw_reduced(win_parts, w_in[0], m_w_in[0], v_w_in[0], 2 * ROWS, "adamw_w_in")
    g_w_out, d_w_out, nm_w_out, nv_w_out = _adamw_reduced(wout_parts, w_out[0], m_w_out[0], v_w_out[0], ROWS // 2,
                                                          "adamw_w_out")
    g_ada, d_ada, nm_ada, nv_ada = _adamw_ada(c_all, cc, dmod, ada_w[0], m_ada_w[0], v_ada_w[0], me)
    ws = {k: args[k].reshape(view[k]) for k in _SMALL}
    ms = {k: args["m_" + k].reshape(view[k]) for k in _SMALL}
    vs = {k: args["v_" + k].reshape(view[k]) for k in _SMALL}
    small = _adamw_small(red, redq, [m.reshape(-1, HD) for m in mats_all], cparts, gab, ws, ms, vs, me)
    big = dict(w_in=(g_w_in, d_w_in, nm_w_in, nv_w_in), w_out=(g_w_out, d_w_out, nm_w_out, nv_w_out),
               ada_w=(g_ada, d_ada, nm_ada, nv_ada))

    loss = loss.reshape(())
    names = ("c_ctx", "ada_w", "ada_b", "norm_g", "w_in", "conv_w", "conv_b", "lru_wa", "lru_ba", "lru_wx", "lru_bx",
             "lru_lambda", "sgu_ln_g", "sgu_ln_b", "sgu_w", "sgu_b", "w_out", "final_g")
    outs = [loss, gx.reshape(x.shape)]
    for kind in range(4):
        for k in names:
            val = big[k][kind] if k in big else small[kind][k]
            outs.append(val.reshape(args[k].shape))
    return tuple(outs)
```

```python
import jax
import jax.numpy as jnp
from jax import lax
from jax.experimental import pallas as pl
from jax.experimental.pallas import tpu as pltpu

F32 = jnp.float32
BF16 = jnp.bfloat16

N_DEV = 8
D = 1024
L = 2048
LC = 256
HEADS = 8
HD = 128
CHUNK = 128
D_IN = 5 * D
W_IN_SHARD = D_IN // N_DEV
ROWS = 256
D_MIX = 2 * D
CONV_W = 4
LRU_C = 8.0
NORM_EPS = 1e-6
LN_EPS = 1e-5
ADAM_LR, ADAM_B1, ADAM_B2, ADAM_EPS, ADAM_WD, ADAM_STEP = 0.001, 0.9, 0.999, 1e-08, 0.01, 10

VMEM_LIMIT = 56 * 1024 * 1024

HBM = pl.BlockSpec(memory_space=pltpu.HBM)
VMEM = pl.BlockSpec(memory_space=pltpu.VMEM)
MESH = pl.DeviceIdType.MESH


def _call(body, **kw):
    return pl.pallas_call(body, **kw)


def _params(*sem):
    return pltpu.CompilerParams(dimension_semantics=sem, vmem_limit_bytes=VMEM_LIMIT)


def _sigmoid(x):
    return 0.5 * jnp.tanh(0.5 * x) + 0.5


def _silu_and_grad(x):
    s = _sigmoid(x)
    return x * s, s * (1.0 + x * (1.0 - s))


_G0 = 0.7978845608028654
_G1 = 0.044715


def _gelu_and_grad(x):
    x2 = x * x
    t = jnp.tanh(_G0 * (x + _G1 * x * x2))
    cdf = 0.5 * (1.0 + t)
    return x * cdf, cdf + 0.5 * x * (1.0 - t * t) * (_G0 * (1.0 + 3.0 * _G1 * x2))


def _softplus(z):
    t = jnp.exp(-jnp.abs(z))
    u = 1.0 + t
    log1p = jnp.where(u == 1.0, t, jnp.log(u) * t / jnp.where(u == 1.0, 1.0, u - 1.0))
    return jnp.maximum(z, 0.0) + log1p


def _dot(a, b):
    return jnp.dot(a, b, preferred_element_type=F32)


def _dot_nt(a, b):
    return lax.dot_general(a, b, (((1,), (1,)), ((), ())), preferred_element_type=F32)


def _dot_tn(a, b):
    return lax.dot_general(a, b, (((0,), (0,)), ((), ())), preferred_element_type=F32)


def _rows(shape):
    return lax.broadcasted_iota(jnp.int32, shape, 0)


def _gather2_shapes(arrays, modes):
    return [jax.ShapeDtypeStruct((N_DEV,) + a.shape if m == "ag" else (a.shape[0], N_DEV * a.shape[1]), a.dtype)
            for a, m in zip(arrays, modes)]


def _gather2_sems(n):
    return [pltpu.SemaphoreType.DMA((n, N_DEV - 1)), pltpu.SemaphoreType.DMA((n, N_DEV - 1)),
            pltpu.SemaphoreType.DMA((n,))]


def _barrier(peers):
    sem = pltpu.get_barrier_semaphore()
    for peer in peers:
        pl.semaphore_signal(sem, inc=1, device_id=peer, device_id_type=MESH)
    pl.semaphore_wait(sem, len(peers))


def _gather2_ops(ins, outs, modes, send_sems, recv_sems, local_sems, barrier=False):
    n = len(ins)
    x, y, c = lax.axis_index("x"), lax.axis_index("y"), lax.axis_index("c")
    me, sibling = (x, y, c), (x, y, 1 - c)
    chips = [(x ^ (k >> 1), y ^ (k & 1)) for k in (1, 2, 3)]

    def slot(j, px, py, pc):
        dev = 4 * px + 2 * py + pc
        if modes[j] == "agc":
            w = ins[j].shape[1]
            return outs[j].at[:, pl.ds(pl.multiple_of(dev * w, 128), w)]
        return outs[j].at[dev]

    def copy(j, k, block, to, src=None):
        return pltpu.make_async_remote_copy(
            src_ref=slot(j, *block) if src is None else src, dst_ref=slot(j, *block),
            send_sem=send_sems.at[j, k], recv_sem=recv_sems.at[j, k], device_id=to, device_id_type=MESH)

    def own(j):
        return pltpu.make_async_copy(ins[j], slot(j, *me), local_sems.at[j])

    def first(j):
        return [copy(j, 0, me, sibling, src=ins[j])] + [copy(j, 1 + i, me, (*chip, c), src=ins[j])
                                                        for i, chip in enumerate(chips)]

    def passed(j, i):
        return copy(j, 4 + i, (*chips[i], c), sibling)

    def start():
        if barrier:
            _barrier([sibling] + [(*chip, c) for chip in chips])
        for j in range(n):
            own(j).start()
            for cp in first(j):
                cp.start()

    def forward():
        for i, chip in enumerate(chips):
            for j in range(n):
                copy(j, 1 + i, (*chip, c), me).wait_recv()
                passed(j, i).start()

    def finish():
        for j in range(n):
            copy(j, 0, sibling, me).wait_recv()
            for i, chip in enumerate(chips):
                copy(j, 4 + i, (*chip, 1 - c), me).wait_recv()
            for cp in first(j) + [passed(j, i) for i in range(3)]:
                cp.wait_send()
            own(j).wait()

    return start, forward, finish


def _sibling_barrier():
    sem = pltpu.get_barrier_semaphore()
    sibling = (lax.axis_index("x"), lax.axis_index("y"), 1 - lax.axis_index("c"))
    pl.semaphore_signal(sem, inc=1, device_id=sibling, device_id_type=MESH)
    pl.semaphore_wait(sem, 1)


def _chips_sems(n):
    return [pltpu.SemaphoreType.DMA((n, 6)), pltpu.SemaphoreType.DMA((n, 6)), pltpu.SemaphoreType.DMA((n,))]


def _chips_stage_shapes(chip_sums):
    return [jax.ShapeDtypeStruct((2, a.shape[1] // 2, a.shape[2]), a.dtype) for a in chip_sums]


def _chips_ops(ins, outs, stages, send_sems, recv_sems, local_sems, first_chips=None, barrier=False):
    x, y, c = lax.axis_index("x"), lax.axis_index("y"), lax.axis_index("c")
    qm = 2 * x + y
    first_chips = first_chips or [0] * len(ins)

    def owns(j, chip):
        lo, cnt = first_chips[j], ins[j].shape[0]
        if lo == 0 and cnt == 4:
            return None
        return jnp.logical_and(chip >= lo, chip < lo + cnt)

    def guarded(cond, fn):
        if cond is None:
            fn()
        else:
            pl.when(cond)(fn)

    def slot(j, chip):
        return jnp.clip(chip - first_chips[j], 0, ins[j].shape[0] - 1)

    def half(j, i):
        h = ins[j].shape[1] // 2
        return pl.ds(i * h, h)

    def copy(j, sem, src, dst, k):
        return pltpu.make_async_remote_copy(
            src_ref=src, dst_ref=dst, send_sem=send_sems.at[j, sem], recv_sem=recv_sems.at[j, sem],
            device_id=(x ^ (k >> 1), y ^ (k & 1), c), device_id_type=MESH)

    def direct(j, k):
        return copy(j, k - 1, ins[j].at[slot(j, qm ^ k)], outs[j].at[qm], k)

    def first_hop(j, k):
        return copy(j, 1 + k, ins[j].at[slot(j, qm ^ 3), half(j, k - 1)], stages[j].at[k - 1], k)

    def second_hop(j, k):
        return copy(j, 3 + k, stages[j].at[2 - k], outs[j].at[qm ^ (3 - k), half(j, 2 - k)], k)

    def local(j):
        return pltpu.make_async_copy(ins[j].at[slot(j, qm)], outs[j].at[qm], local_sems.at[j])

    def start():
        if barrier:
            _barrier([(x ^ (k >> 1), y ^ (k & 1), c) for k in (1, 2)])
        for j in range(len(ins)):
            for k in (1, 2):
                guarded(owns(j, qm ^ 3), lambda j=j, k=k: first_hop(j, k).start())
        for j in range(len(ins)):
            for k in (1, 2):
                guarded(owns(j, qm ^ k), lambda j=j, k=k: direct(j, k).start())
            guarded(owns(j, qm), lambda j=j: local(j).start())

    def forward():
        for j in range(len(ins)):
            for k in (1, 2):
                def pass_on(j=j, k=k):
                    first_hop(j, 3 - k).wait_recv()
                    second_hop(j, k).start()
                guarded(owns(j, qm ^ k), pass_on)

    def finish():
        for j in range(len(ins)):
            for k in (1, 2):
                guarded(owns(j, qm ^ k), lambda j=j, k=k: direct(j, k).wait_send())
                guarded(owns(j, qm ^ k), lambda j=j, k=k: second_hop(j, k).wait_send())
                guarded(owns(j, qm ^ 3), lambda j=j, k=k: first_hop(j, k).wait_send())
                guarded(owns(j, qm), lambda j=j, k=k: direct(j, k).wait_recv())
                guarded(owns(j, qm), lambda j=j, k=k: second_hop(j, k).wait_recv())
            guarded(owns(j, qm), lambda j=j: local(j).wait())

    return start, forward, finish


ARRIVAL = (0, 1, 2, 4, 3, 5, 6, 7)


def _front_project(xr, c, c_ctx, ada_w, ada_b, ng, w_in, w_out, cw, lam, me):
    nloc = ada_w.shape[1]
    ws = W_IN_SHARD
    arrival = jnp.asarray(ARRIVAL, jnp.int32)

    def body(me_ref, arr_ref, x_ref, c_ref, cc_ref, aw_ref, ab_ref, ng_ref, win_ref, wout_ref, cw_ref, lam_ref,
             z_ref, hn_ref, wfull_ref, woutb_ref, modx_ref, modc_ref, call_ref, cwf_ref, lamf_ref,
             wv, call_s, part_s, parts_s, w_send, w_recv, hbm_sems, s_send, s_recv, g_send, g_recv, g_local,
             x_v, x_sem):
        t = pl.program_id(0)
        x_load = pltpu.make_async_copy(x_ref, x_v, x_sem)
        x, y, cidx = lax.axis_index("x"), lax.axis_index("y"), lax.axis_index("c")
        me_i = me_ref[0]
        sibling = (x, y, 1 - cidx)
        chips = [(x ^ (k >> 1), y ^ (k & 1)) for k in (1, 2, 3)]
        g_start, g_pass, g_finish = _gather2_ops([cw_ref, lam_ref], [cwf_ref, lamf_ref], ["agc", "agc"],
                                                 g_send, g_recv, g_local)

        def shard_copy(k, px, py, pc, to, half=None):
            slot = wv.at[4 * px + 2 * py + pc]
            if half is not None:
                slot = slot.at[pl.ds(half * (D // 2), D // 2), :]
            return pltpu.make_async_remote_copy(src_ref=slot, dst_ref=slot, send_sem=w_send.at[k],
                                                recv_sem=w_recv.at[k], device_id=to, device_id_type=MESH)

        def small_gather(src, my_slot, stage):
            copies = []
            for k in range(1, N_DEV):
                peer = (x ^ (k >> 2), y ^ ((k >> 1) & 1), cidx ^ (k & 1))
                cp = pltpu.make_async_remote_copy(src_ref=src, dst_ref=my_slot, send_sem=s_send.at[stage, k - 1],
                                                  recv_sem=s_recv.at[stage, k - 1], device_id=peer,
                                                  device_id_type=MESH)
                cp.start()
                copies.append(cp)
            pltpu.sync_copy(src, my_slot)
            return copies

        def finish_small(copies):
            for cp in copies:
                cp.wait()

        def to_neighbours(half):
            for i in (0, 1):
                shard_copy(1 + i, x, y, cidx, (*chips[i], cidx), half=half).start()

        @pl.when(t == 0)
        def _():
            _barrier([(x ^ (k >> 2), y ^ ((k >> 1) & 1), cidx ^ (k & 1)) for k in range(1, N_DEV)])
            g_start()
            x_load.start()
            wv[me_i] = win_ref[...].astype(BF16)
            woutb_ref[...] = wout_ref[...].astype(BF16)
            shard_copy(0, x, y, cidx, sibling).start()
            conds_sent = small_gather(c_ref, call_s.at[pl.ds(me_i, 1), :], 0)
            to_neighbours(0)
            finish_small(conds_sent)
            call_ref[...] = call_s[...]
            off = pl.multiple_of(me_i * nloc, 128)
            b = ab_ref[:, pl.ds(off, nloc)]
            w = aw_ref[...]
            sx, _ = _silu_and_grad(call_s[...])
            sc, _ = _silu_and_grad(jnp.broadcast_to(cc_ref[...], (8, D)))
            part_s[0:8, :] = _dot(sx, w) + b
            part_s[8:16, :] = _dot(sc, w) + b
            parts_sent = small_gather(part_s, parts_s.at[me_i], 1)
            to_neighbours(1)
            finish_small(parts_sent)
            mine = _rows((16, nloc)) == me_i
            for j in range(N_DEV):
                pj = parts_s[j]
                modx_ref[:, j * nloc:(j + 1) * nloc] = jnp.sum(jnp.where(mine, pj, 0.0), axis=0, keepdims=True)
                modc_ref[:, j * nloc:(j + 1) * nloc] = pj[8:9, :]
            shift, scale1, ngv = modx_ref[:, 0:D], 1.0 + modx_ref[:, D:2 * D], ng_ref[...]
            x_load.wait()
            for r in range(L // ROWS):
                rsl = slice(r * ROWS, (r + 1) * ROWS)
                xv = x_v[rsl, :]
                rs = lax.rsqrt(jnp.mean(xv * xv, axis=-1, keepdims=True) + NORM_EPS)
                hn_ref[rsl, :] = ((xv * rs * ngv) * scale1 + shift).astype(BF16)

        @pl.when(t == 1)
        def _():
            shard_copy(0, x, y, 1 - cidx, sibling).wait_recv()
            g_pass()

        for i in (0, 1):
            @pl.when(t == ARRIVAL.index((2, 4)[i]))
            def _(i=i):
                shard_copy(1 + i, *chips[i], cidx, sibling).wait_recv()
                shard_copy(4 + i, *chips[i], cidx, sibling).start()
                shard_copy((7, 3)[i], *chips[i], cidx, (*chips[1 - i], cidx), half=i).start()

        @pl.when(t == ARRIVAL.index(6))
        def _():
            shard_copy(3, *chips[2], cidx, sibling, half=1).wait_recv()
            shard_copy(7, *chips[2], cidx, sibling, half=0).wait_recv()
            shard_copy(6, *chips[2], cidx, sibling).start()

        for i in range(3):
            @pl.when(t == ARRIVAL.index((3, 5, 7)[i]))
            def _(i=i):
                shard_copy(4 + i, *chips[i], 1 - cidx, sibling).wait_recv()

        @pl.when(t == 2)
        def _():
            g_finish()

        dev = me_i ^ arr_ref[t]
        for r in range(L // (2 * ROWS)):
            rsl = slice(r * 2 * ROWS, (r + 1) * 2 * ROWS)
            z_ref[rsl, :] = _dot(hn_ref[rsl, :], wv[dev])
        col = pl.ds(pl.multiple_of(dev * ws, 128), ws)
        pltpu.make_async_copy(wv.at[dev], wfull_ref.at[:, col], hbm_sems.at[t]).start()

        @pl.when(t == N_DEV - 1)
        def _():
            for k in (0, 1, 2, 4, 5, 6):
                shard_copy(k, x, y, cidx, sibling).wait_send()
            for k in (3, 7):
                shard_copy(k, x, y, cidx, sibling, half=0).wait_send()
            for s in range(N_DEV):
                pltpu.make_async_copy(wv.at[0], wfull_ref.at[:, pl.ds(0, ws)], hbm_sems.at[s]).wait()

    const = lambda *shape: pl.BlockSpec(shape, lambda t, m, a: (0,) * len(shape))
    once = lambda *shape: pl.BlockSpec(shape, lambda t, m, a: (0,) * len(shape), pipeline_mode=pl.Buffered(1))
    return _call(
        body, name="front_project",
        out_shape=[jax.ShapeDtypeStruct((L, D_IN), F32), jax.ShapeDtypeStruct((L, D), BF16),
                   jax.ShapeDtypeStruct((D, D_IN), BF16), jax.ShapeDtypeStruct(w_out.shape, BF16),
                   jax.ShapeDtypeStruct((1, 3 * D), F32), jax.ShapeDtypeStruct((1, 3 * D), F32),
                   jax.ShapeDtypeStruct((N_DEV, D), F32), jax.ShapeDtypeStruct((CONV_W, D), F32),
                   jax.ShapeDtypeStruct((2, D), F32)],
        grid_spec=pltpu.PrefetchScalarGridSpec(
            num_scalar_prefetch=2, grid=(N_DEV,),
            in_specs=[HBM, const(1, D), const(1, D), once(D, nloc), const(1, 3 * D), const(1, D),
                      once(D, ws), once(*w_out.shape), HBM, HBM],
            out_specs=[pl.BlockSpec((L, ws), lambda t, m, a: (0, m[0] ^ a[t])), const(L, D), HBM,
                       const(*w_out.shape),
                       const(1, 3 * D), const(1, 3 * D), const(N_DEV, D), HBM, HBM],
            scratch_shapes=[pltpu.VMEM((N_DEV, D, ws), BF16), pltpu.VMEM((N_DEV, D), F32), pltpu.VMEM((16, nloc), F32),
                            pltpu.VMEM((N_DEV, 16, nloc), F32), pltpu.SemaphoreType.DMA((8,)),
                            pltpu.SemaphoreType.DMA((8,)), pltpu.SemaphoreType.DMA((N_DEV,)),
                            pltpu.SemaphoreType.DMA((2, N_DEV - 1)), pltpu.SemaphoreType.DMA((2, N_DEV - 1))]
            + _gather2_sems(2) + [pltpu.VMEM((L, D), F32), pltpu.SemaphoreType.DMA(())]),
        compiler_params=pltpu.CompilerParams(dimension_semantics=("arbitrary",), vmem_limit_bytes=VMEM_LIMIT,
                                             has_side_effects=True, collective_id=10),
    )(me, arrival, pltpu.with_memory_space_constraint(xr, pltpu.HBM), c, c_ctx, ada_w, ada_b, ng, w_in, w_out,
      pltpu.with_memory_space_constraint(cw, pltpu.HBM), pltpu.with_memory_space_constraint(lam, pltpu.HBM))


def _project(xr, mod, ng, w, ncols, tm, name):
    rows = xr.shape[0]

    def body(x_ref, sh_ref, sc_ref, ng_ref, w_ref, z_ref, hn_ref):
        x = x_ref[...]
        rs = lax.rsqrt(jnp.mean(x * x, axis=-1, keepdims=True) + NORM_EPS)
        hn = (x * rs * ng_ref[...]) * (1.0 + sc_ref[...]) + sh_ref[...]
        hb = hn.astype(BF16)
        hn_ref[...] = hb
        for n in range(ncols // D):
            z_ref[:, n * D:(n + 1) * D] = _dot(hb, w_ref[:, n * D:(n + 1) * D])

    vec = pl.BlockSpec((1, D), lambda i: (0, 0))
    return _call(
        body, name=name, grid=(rows // tm,),
        out_shape=[jax.ShapeDtypeStruct((rows, ncols), F32), jax.ShapeDtypeStruct((rows, D), BF16)],
        in_specs=[pl.BlockSpec((tm, D), lambda i: (i, 0)), vec, pl.BlockSpec((1, D), lambda i: (0, 1)), vec,
                  pl.BlockSpec((D, ncols), lambda i: (0, 0), pipeline_mode=pl.Buffered(1))],
        out_specs=[pl.BlockSpec((tm, ncols), lambda i: (i, 0)), pl.BlockSpec((tm, D), lambda i: (i, 0))],
        compiler_params=_params("arbitrary"),
    )(xr, mod, mod, ng, w)


def _scan_pair(af_ref, uf_ref, hf_ref, h0f, ab_ref, ub_ref, hb_ref, h0b, t_len):
    span = 8 * SCAN_BLOCKS
    nit = t_len // span
    rows = _rows((8, HD))

    def local_scan(a, b, forward):
        for s in (1, 2, 4):
            sh = s if forward else 8 - s
            m = rows >= s if forward else rows < 8 - s
            b = a * jnp.where(m, pltpu.roll(b, sh, 0), 0.0) + b
            a = a * jnp.where(m, pltpu.roll(a, sh, 0), 1.0)
        return a, b

    def span_scan(a_ref, u_ref, h_ref, off, carry, forward):
        order = range(SCAN_BLOCKS) if forward else range(SCAN_BLOCKS - 1, -1, -1)
        last = slice(7, 8) if forward else slice(0, 1)
        for q in order:
            rs = pl.ds(off + 8 * q, 8)
            a, b = local_scan(a_ref[rs, :], u_ref[rs, :], forward)
            h_ref[rs, :] = b + a * carry
            carry = a[last, :] * carry + b[last, :]
        return carry

    def body(k, carry):
        cf, cb = carry
        cf = span_scan(af_ref, uf_ref, hf_ref, pl.multiple_of(k * span, span), cf, True)
        cb = span_scan(ab_ref, ub_ref, hb_ref, pl.multiple_of((nit - 1 - k) * span, span), cb, False)
        return cf, cb

    return lax.fori_loop(0, nit, body, (h0f, h0b))


SCAN_BLOCKS = 16


def _shifted(pad_ref, x, offsets, before=0.0, after=0.0):
    n = x.shape[0]
    pad_ref[0:8, :] = jnp.broadcast_to(jnp.asarray(before, F32), (8, x.shape[1]))
    pad_ref[8:8 + n, :] = x
    pad_ref[8 + n:16 + n, :] = jnp.broadcast_to(jnp.asarray(after, F32), (8, x.shape[1]))
    return [pad_ref[8 + o:8 + o + n, :] for o in offsets]


def _conv(xa, cw, cb, pad_ref):
    xm1, xp1, xp2 = _shifted(pad_ref, xa, (-1, 1, 2))
    return xm1 * cw[0:1, :] + xa * cw[1:2, :] + xp1 * cw[2:3, :] + xp2 * cw[3:4, :] + cb


def _gates(xc, wa, wx, ba, bx, nsp):
    xb = xc.astype(BF16)
    r = _sigmoid(_dot(xb, wa) + ba)
    i = _sigmoid(_dot(xb, wx) + bx)
    log_a = r * nsp
    a = jnp.exp(log_a)
    g2 = jnp.tanh(log_a) * (-1.0 - a * a)
    rg = lax.rsqrt(jnp.maximum(g2, 1e-30))
    return r, i, a, g2 * rg, rg


def _lru_param_specs():
    h4 = pl.BlockSpec((2, 1, HD, HD), lambda h: (0, h, 0, 0))
    v2 = pl.BlockSpec((2, HD), lambda h: (0, h))
    b16 = pl.BlockSpec((2 * HEADS, HD), lambda h: (0, 0))
    return dict(
        xa=pl.BlockSpec((L, HD), lambda h: (0, h)), xac=pl.BlockSpec((LC, HD), lambda h: (0, h)),
        cw=pl.BlockSpec((CONV_W, HD), lambda h: (0, h)), cb=pl.BlockSpec((1, HD), lambda h: (0, h)), h4=h4, v2=v2,
        b16=b16)


def _bias_row(ref, d):
    mask = _rows((2 * HEADS, HD)) == d * HEADS + pl.program_id(0)
    return jnp.sum(jnp.where(mask, ref[...], 0.0), axis=0, keepdims=True), mask


def _lru_forward(zx, zc, cw, cb, wa, wx, ba, bx, lam, gather, gather_modes):
    ng_ = len(gather)

    def body(xa_ref, xac_ref, cw_ref, cb_ref, wa_ref, wx_ref, ba_ref, bx_ref, lam_ref, *rest):
        yl_ref = rest[ng_]
        af, uf, hf, ab, ub, hb, pad_s = rest[2 * ng_ + 1:2 * ng_ + 8]
        start, pass_on, finish = _gather2_ops(rest[:ng_], rest[ng_ + 1:2 * ng_ + 1], gather_modes,
                                              *rest[2 * ng_ + 8:], barrier=True)
        pl.when(pl.program_id(0) == 0)(start)
        pl.when(pl.program_id(0) == HEADS // 2)(pass_on)
        pl.when(pl.program_id(0) == HEADS - 1)(finish)
        cwv, cbv = cw_ref[...], cb_ref[...]
        nsp = (-LRU_C) * _softplus(-lam_ref[...])

        def forward(xa, t_len, h0f, h0b):
            xc = _conv(xa, cwv, cbv, pad_s)
            for d, (a_ref, u_ref) in enumerate(((af, uf), (ab, ub))):
                _, i, a, gamma, _ = _gates(xc, wa_ref[d, 0].astype(BF16), wx_ref[d, 0].astype(BF16),
                                           _bias_row(ba_ref, d)[0], _bias_row(bx_ref, d)[0], nsp[d:d + 1, :])
                a_ref[0:t_len, :] = a
                u_ref[0:t_len, :] = gamma * (i * xc)
            return _scan_pair(af, uf, hf, h0f, ab, ub, hb, h0b, t_len)

        z = jnp.zeros((1, HD), F32)
        h0f, h0b = forward(xac_ref[...], LC, z, z)
        forward(xa_ref[...], L, h0f, h0b)
        yl_ref[...] = hf[...] + hb[...]

    s = _lru_param_specs()
    return _call(
        body, name="lru_forward", grid=(HEADS,),
        out_shape=[jax.ShapeDtypeStruct((L, D), F32)] + _gather2_shapes(gather, gather_modes),
        in_specs=[s["xa"], s["xac"], s["cw"], s["cb"], s["h4"], s["h4"], s["b16"], s["b16"], s["v2"]] + [HBM] * ng_,
        out_specs=[pl.BlockSpec((L, HD), lambda h: (0, h))] + [HBM] * ng_,
        scratch_shapes=[pltpu.VMEM((L, HD), F32)] * 6 + [pltpu.VMEM((L + 16, HD), F32)] + _gather2_sems(ng_),
        compiler_params=pltpu.CompilerParams(dimension_semantics=("arbitrary",), vmem_limit_bytes=VMEM_LIMIT,
                                             has_side_effects=True, collective_id=5),
    )(zx, zc, cw, cb, wa, wx, ba, bx, lam, *[pltpu.with_memory_space_constraint(a, pltpu.HBM) for a in gather])


def _lru_backward(zx, zc, dyl, dz, cw, cb, wa, wx, ba, bx, lam, chip_sums, first_chips=None):
    nr = len(chip_sums)

    def body(xa_ref, xac_ref, dyl_ref, dz_in, cw_ref, cb_ref, wa_ref, wx_ref, ba_ref, bx_ref, lam_ref, *rest):
        (dxa_ref, dxac_ref, dwa_ref, dwx_ref, dba_ref, dbx_ref, dlam_ref, dcw_ref,
         dcb_ref) = rest[nr:nr + 9]
        main_s, ctx_s, pad_s = rest[3 * nr + 9:3 * nr + 12]
        if nr:
            start, forward, finish = _chips_ops(rest[:nr], rest[nr + 9:2 * nr + 9], rest[2 * nr + 9:3 * nr + 9],
                                                *rest[3 * nr + 12:], first_chips=first_chips, barrier=True)
            pl.when(pl.program_id(0) == 0)(start)
            pl.when(pl.program_id(0) == HEADS // 2)(forward)
            pl.when(pl.program_id(0) == HEADS - 1)(finish)
        del dz_in

        @pl.when(pl.program_id(0) == 0)
        def _():
            dba_ref[...] = jnp.zeros_like(dba_ref)
            dbx_ref[...] = jnp.zeros_like(dbx_ref)

        cwv, cbv = cw_ref[...], cb_ref[...]
        lamv = lam_ref[...]
        sp = _softplus(-lamv)
        nsp = (-LRU_C) * sp
        z = jnp.zeros((1, HD), F32)

        def wmat(ref, d):
            return ref[d, 0].astype(BF16)

        def workspace(s):
            return dict(a=(s.at[0], s.at[1]), u=(s.at[2], s.at[3]), h=(s.at[4], s.at[5]), rho=(s.at[6], s.at[7]),
                        saved=(tuple(s.at[8 + k] for k in range(4)), tuple(s.at[12 + k] for k in range(4))),
                        xc=s.at[16])

        def forward(ws, xa, t_len, h0f, h0b):
            xc = _conv(xa, cwv, cbv, pad_s)
            ws["xc"][...] = xc
            for d in (0, 1):
                vals = _gates(xc, wmat(wa_ref, d), wmat(wx_ref, d), _bias_row(ba_ref, d)[0],
                              _bias_row(bx_ref, d)[0], nsp[d:d + 1, :])
                r, i, a, gamma, rg = vals
                ws["a"][d][...] = a
                ws["u"][d][...] = gamma * (i * xc)
                for ref, val in zip(ws["saved"][d], (r, i, gamma, rg)):
                    ref[...] = val
            return _scan_pair(ws["a"][0], ws["u"][0], ws["h"][0], h0f, ws["a"][1], ws["u"][1], ws["h"][1], h0b,
                              t_len)

        def backward(ws, xa, t_len, h0f, h0b, dhf, dhb, first):
            xc = ws["xc"][...]
            (af, ab), (uf, ub), (hf, hb), (rf, rb) = ws["a"], ws["u"], ws["h"], ws["rho"]
            uf[...] = ab[...] * dhb
            ub[...] = af[...] * dhf
            rho_b_last, rho_f_first = _scan_pair(ab, uf, rb, z, af, ub, rf, z, t_len)
            dxc = jnp.zeros((t_len, HD), F32)
            dsp = []
            for d in (0, 1):
                r, i, gamma, rg = (ref[...] for ref in ws["saved"][d])
                a = ws["a"][d][...]
                if d == 0:
                    lam_t = dhf + _shifted(pad_s, rf[...], (1,))[0]
                    h_prev = _shifted(pad_s, hf[...], (-1,), before=h0f)[0]
                else:
                    lam_t = dhb + _shifted(pad_s, rb[...], (-1,))[0]
                    h_prev = _shifted(pad_s, hb[...], (1,), after=h0b)[0]
                da = lam_t * h_prev
                lx = lam_t * xc
                d_i = lx * gamma
                d_gamma = lx * i
                dxc = dxc + lam_t * (gamma * i)
                d_log_a = a * (da - d_gamma * (a * rg))
                dsp.append(jnp.sum(d_log_a * r, axis=0, keepdims=True) * (-LRU_C))
                d_pre_r = d_log_a * nsp[d:d + 1, :] * (r * (1.0 - r))
                d_pre_i = d_i * (i * (1.0 - i))
                prb, pib, xb = d_pre_r.astype(BF16), d_pre_i.astype(BF16), xc.astype(BF16)
                dxc = dxc + _dot_nt(prb, wmat(wa_ref, d)) + _dot_nt(pib, wmat(wx_ref, d))
                g_wa, g_wx = _dot_tn(xb, prb), _dot_tn(xb, pib)
                g_ba = jnp.sum(d_pre_r, axis=0, keepdims=True)
                g_bx = jnp.sum(d_pre_i, axis=0, keepdims=True)
                mask = _bias_row(ba_ref, d)[1]
                dba_ref[...] += jnp.where(mask, g_ba, 0.0)
                dbx_ref[...] += jnp.where(mask, g_bx, 0.0)
                if first:
                    dwa_ref[d, 0] = g_wa
                    dwx_ref[d, 0] = g_wx
                else:
                    dwa_ref[d, 0] += g_wa
                    dwx_ref[d, 0] += g_wx
            g_lam = jnp.concatenate(dsp, axis=0) * (-_sigmoid(-lamv))
            dm1, dp1, dm2 = _shifted(pad_s, dxc, (-1, 1, -2))
            dxa = dp1 * cwv[0:1, :] + dxc * cwv[1:2, :] + dm1 * cwv[2:3, :] + dm2 * cwv[3:4, :]
            xm1, xp1, xp2 = _shifted(pad_s, xa, (-1, 1, 2))
            g_cw = jnp.concatenate([jnp.sum(dxc * v, axis=0, keepdims=True) for v in (xm1, xa, xp1, xp2)], axis=0)
            g_cb = jnp.sum(dxc, axis=0, keepdims=True)
            if first:
                dlam_ref[...] = g_lam
                dcw_ref[...] = g_cw
                dcb_ref[...] = g_cb
            else:
                dlam_ref[...] += g_lam
                dcw_ref[...] += g_cw
                dcb_ref[...] += g_cb
            return dxa, rho_f_first, rho_b_last

        ws_x, ws_c = workspace(main_s), workspace(ctx_s)
        h0f, h0b = forward(ws_c, xac_ref[...], LC, z, z)
        forward(ws_x, xa_ref[...], L, h0f, h0b)
        dh = dyl_ref[...]
        dxa, dh0f, dh0b = backward(ws_x, xa_ref[...], L, h0f, h0b, dh, dh, True)
        dxa_ref[...] = dxa.astype(BF16)
        rc = _rows((LC, HD))
        dxac, _, _ = backward(ws_c, xac_ref[...], LC, z, z, jnp.where(rc == LC - 1, dh0f, 0.0),
                              jnp.where(rc == 0, dh0b, 0.0), False)
        dxac_ref[...] = dxac.astype(BF16)

    s = _lru_param_specs()
    col = lambda r: pl.BlockSpec((r, HD), lambda h: (0, h))
    return _call(
        body, name="lru_backward", grid=(HEADS,),
        out_shape=[jax.ShapeDtypeStruct((L, D_IN), BF16), jax.ShapeDtypeStruct((LC, D), BF16),
                   jax.ShapeDtypeStruct((2, HEADS, HD, HD), F32), jax.ShapeDtypeStruct((2, HEADS, HD, HD), F32),
                   jax.ShapeDtypeStruct((2 * HEADS, HD), F32), jax.ShapeDtypeStruct((2 * HEADS, HD), F32),
                   jax.ShapeDtypeStruct((2, D), F32), jax.ShapeDtypeStruct((CONV_W, D), F32),
                   jax.ShapeDtypeStruct((1, D), F32)] + [jax.ShapeDtypeStruct((4,) + a.shape[1:], a.dtype)
                                                          for a in chip_sums] + _chips_stage_shapes(chip_sums),
        in_specs=[s["xa"], s["xac"], col(L), pl.BlockSpec(memory_space=pl.ANY), s["cw"], s["cb"], s["h4"], s["h4"],
                  s["b16"], s["b16"], s["v2"]] + [HBM] * nr,
        out_specs=[col(L), col(LC), s["h4"], s["h4"], s["b16"], s["b16"], s["v2"], col(CONV_W), col(1)]
        + [HBM] * (2 * nr),
        scratch_shapes=[pltpu.VMEM((17, L, HD), F32), pltpu.VMEM((17, LC, HD), F32), pltpu.VMEM((L + 16, HD), F32)]
        + (_chips_sems(nr) if nr else []),
        input_output_aliases={3: 0},
        compiler_params=pltpu.CompilerParams(dimension_semantics=("arbitrary",), vmem_limit_bytes=VMEM_LIMIT,
                                             has_side_effects=True, collective_id=6 if nr else None),
    )(zx, zc, dyl, dz, cw, cb, wa, wx, ba, bx, lam, *[pltpu.with_memory_space_constraint(a, pltpu.HBM)
                                                       for a in chip_sums])


def _mixer_loss(x, tgt, zx, yl, gx, fg, lng, lnb, ws, bst, wout, tm):
    ncht = tm // CHUNK

    def body(x_ref, t_ref, ga_ref, u_ref, v_ref, gb_ref, yl_ref, gx_ref, fg_ref, lng_ref, lnb_ref, ws_ref,
             bst_ref, wout_ref,
             dz_ref, dyl_ref, dxn_ref, y_s, do_ref, dws_ref, dbst_ref, vec_ref,
             vn_s, mix_s, dm_s, dvn_s):
        step = pl.program_id(0)

        @pl.when(step == 0)
        def _():
            dws_ref[...] = jnp.zeros_like(dws_ref)
            dbst_ref[...] = jnp.zeros_like(dbst_ref)
            vec_ref[...] = jnp.zeros_like(vec_ref)

        u, v = u_ref[...], v_ref[...]
        ug, dug_du = _gelu_and_grad(u)
        vg, dvg_dv = _gelu_and_grad(v)
        mu = jnp.mean(vg, axis=-1, keepdims=True)
        vc = vg - mu
        rstd = lax.rsqrt(jnp.mean(vc * vc, axis=-1, keepdims=True) + LN_EPS)
        vhat = vc * rstd
        lngv = lng_ref[...]
        vn_s[...] = (vhat * lngv + lnb_ref[...]).astype(BF16)
        for ch in range(ncht):
            rs = slice(ch * CHUNK, (ch + 1) * CHUNK)
            for g in range(HEADS):
                cs = slice(g * HD, (g + 1) * HD)
                mix_s[rs, cs] = _dot(ws_ref[g].astype(BF16), vn_s[rs, cs]) + bst_ref[:, g:g + 1]
        mixed = mix_s[...]
        ga, gb, yl = ga_ref[...], gb_ref[...], yl_ref[...]
        sga, dsga = _silu_and_grad(ga)
        sgb, dsgb = _silu_and_grad(gb)
        ys = ug * mixed
        y_s[:, 0:D] = (yl * sga).astype(BF16)
        y_s[:, D:D_MIX] = (ys * sgb).astype(BF16)
        o = _dot(y_s[...], wout_ref[...])
        gxv, fgv = gx_ref[...], fg_ref[...]
        xn = x_ref[...] + gxv * o
        rs2 = lax.rsqrt(jnp.mean(xn * xn, axis=-1, keepdims=True) + NORM_EPS)
        xh = xn * rs2
        diff = xh * fgv - t_ref[...]
        vec_ref[R_LOSS:R_LOSS + 1, :] += jnp.full((1, D), jnp.sum(diff * diff) * (0.5 / D), F32)
        dout = diff * (1.0 / D)
        w = dout * fgv
        dxn = rs2 * (w - xh * jnp.mean(w * xh, axis=-1, keepdims=True))
        dxn_ref[...] = dxn
        vec_ref[0:1, :] += jnp.sum(dxn * o, axis=0, keepdims=True)
        vec_ref[1:2, :] += jnp.sum(dout * xh, axis=0, keepdims=True)
        dob = (dxn * gxv).astype(BF16)
        do_ref[...] = dob
        dy = _dot_nt(dob, wout_ref[...])
        dya, dyb = dy[:, 0:D], dy[:, D:D_MIX]
        dyl_ref[...] = dya * sga
        dys = dyb * sgb
        dz_ref[:, 0:D] = jnp.zeros((tm, D), BF16)
        dz_ref[:, D:2 * D] = (dya * yl * dsga).astype(BF16)
        dz_ref[:, 2 * D:3 * D] = (dys * mixed * dug_du).astype(BF16)
        dz_ref[:, 4 * D:5 * D] = (dyb * ys * dsgb).astype(BF16)
        dm = dys * ug
        dm_s[...] = dm.astype(BF16)
        for g in range(HEADS):
            cs = slice(g * HD, (g + 1) * HD)
            dbst_ref[:, g:g + 1] += sum(jnp.sum(dm[ch * CHUNK:(ch + 1) * CHUNK, cs], axis=1, keepdims=True)
                                        for ch in range(ncht))
            for ch in range(ncht):
                rs = slice(ch * CHUNK, (ch + 1) * CHUNK)
                dws_ref[g] += _dot_nt(dm_s[rs, cs], vn_s[rs, cs])
                dvn_s[rs, cs] = _dot_tn(ws_ref[g].astype(BF16), dm_s[rs, cs])
        dvn = dvn_s[...]
        vec_ref[2:3, :] += jnp.sum(dvn * vhat, axis=0, keepdims=True)
        vec_ref[3:4, :] += jnp.sum(dvn, axis=0, keepdims=True)
        dvh = dvn * lngv
        dvg = rstd * (dvh - jnp.mean(dvh, axis=-1, keepdims=True) - vhat * jnp.mean(dvh * vhat, axis=-1, keepdims=True))
        dz_ref[:, 3 * D:4 * D] = (dvg * dvg_dv).astype(BF16)

    tile = pl.BlockSpec((tm, D), lambda i: (i, 0))
    zcol = lambda n: pl.BlockSpec((tm, D), lambda i: (i, n))
    vec = pl.BlockSpec((1, D), lambda i: (0, 0))
    full = lambda *s: pl.BlockSpec(s, lambda i: (0,) * len(s))
    return _call(
        body, name="mixer_loss", grid=(L // tm,),
        out_shape=[jax.ShapeDtypeStruct((L, D_IN), BF16), jax.ShapeDtypeStruct((L, D), F32),
                   jax.ShapeDtypeStruct((L, D), F32), jax.ShapeDtypeStruct((L, D_MIX), BF16),
                   jax.ShapeDtypeStruct((L, D), BF16),
                   jax.ShapeDtypeStruct((HEADS, CHUNK, CHUNK), F32), jax.ShapeDtypeStruct((CHUNK, HEADS), F32),
                   jax.ShapeDtypeStruct((8, D), F32)],
        in_specs=[tile, tile, zcol(1), zcol(2), zcol(3), zcol(4), tile, pl.BlockSpec((1, D), lambda i: (0, 2)),
                  vec, vec, vec,
                  full(HEADS, CHUNK, CHUNK), full(CHUNK, HEADS),
                  pl.BlockSpec((D_MIX, D), lambda i: (0, 0), pipeline_mode=pl.Buffered(1))],
        out_specs=[pl.BlockSpec((tm, D_IN), lambda i: (i, 0)), tile, tile,
                   pl.BlockSpec((tm, D_MIX), lambda i: (i, 0)), tile,
                   full(HEADS, CHUNK, CHUNK), full(CHUNK, HEADS), full(8, D)],
        scratch_shapes=[pltpu.VMEM((tm, D), BF16), pltpu.VMEM((tm, D), F32),
                        pltpu.VMEM((tm, D), BF16), pltpu.VMEM((tm, D), F32)],
        compiler_params=_params("arbitrary"),
    )(x, tgt, zx, zx, zx, zx, yl, gx, fg, lng, lnb, ws, bst, wout)


def _grad_w(a, b, a2, b2, tk, name, bw, first, nblocks, split, barrier_id, riders=()):
    nk = a.shape[0] // tk
    m = a.shape[1]
    with_ctx = a2 is not None
    if split == "cols":
        slots, r, w = nblocks, m, bw // 2
        piece = lambda q, pc: (slice(None), slice(pc * w, (pc + 1) * w))
    else:
        slots, r, w = 4, m // 8, bw
        piece = lambda q, pc: (slice((2 * q + pc) * r, (2 * q + pc + 1) * r), slice(None))

    nr = len(riders)

    def body(*refs):
        a_ref, b_ref = refs[:2]
        a2_ref, b2_ref = refs[2:4] if with_ctx else (None, None)
        base = 4 if with_ctx else 2
        sums_ref = refs[base + nr]
        s0 = base + 3 * nr + 1
        acc, mine_v, send_v, stage_v, send_sems, recv_sems = refs[s0:s0 + 6]
        n, k = pl.program_id(0), pl.program_id(1)
        x, y, c = lax.axis_index("x"), lax.axis_index("y"), lax.axis_index("c")

        def to_sibling(s):
            return pltpu.make_async_remote_copy(src_ref=send_v.at[s], dst_ref=stage_v.at[s], send_sem=send_sems.at[s],
                                                recv_sem=recv_sems.at[s], device_id=(x, y, 1 - c),
                                                device_id_type=MESH)

        if nr:
            rider_in = refs[base:base + nr]
            own_v, got_v, psum_v = (refs[s0 + 9 + i * nr:s0 + 9 + (i + 1) * nr] for i in range(3))
            p_send, p_recv, p_local = refs[s0 + 9 + 3 * nr:s0 + 12 + 3 * nr]
            c_start, c_forward, c_finish = _chips_ops(psum_v, refs[base + nr + 1:base + 2 * nr + 1],
                                                      refs[base + 2 * nr + 1:base + 3 * nr + 1], *refs[s0 + 6:s0 + 9])

            @pl.when(jnp.logical_and(n == 0, k == 0))
            def _():
                _barrier([(x, y, 1 - c)] + [(x ^ (j >> 1), y ^ (j & 1), c) for j in (1, 2)])
                copies = []
                for j in range(nr):
                    for q in range(4):
                        copies.append(pltpu.make_async_remote_copy(
                            src_ref=rider_in[j].at[2 * q + 1 - c], dst_ref=got_v[j].at[q], send_sem=p_send.at[j, q],
                            recv_sem=p_recv.at[j, q], device_id=(x, y, 1 - c), device_id_type=MESH))
                        copies.append(pltpu.make_async_copy(rider_in[j].at[2 * q + c], own_v[j].at[q],
                                                            p_local.at[j, q]))
                for cp in copies:
                    cp.start()
                for cp in copies:
                    cp.wait()
                for j in range(nr):
                    psum_v[j][...] = (own_v[j][...] + got_v[j][...]).astype(BF16)
                c_start()
        else:
            pl.when(jnp.logical_and(n == 0, k == 0))(_sibling_barrier)

        @pl.when(k == 0)
        def _():
            acc[...] = _dot_tn(a_ref[...], b_ref[...])

        if nk > 1:
            @pl.when(k > 0)
            def _():
                acc[...] += _dot_tn(a_ref[...], b_ref[...])

        if with_ctx:
            @pl.when(jnp.logical_and(k == nk - 1, n == 0))
            def _():
                acc[:, 0:b2_ref.shape[1]] += _dot_tn(a2_ref[...], b2_ref[...])

        if nr:
            pl.when(jnp.logical_and(k == nk - 1, n == nblocks - 1))(c_forward)

        def hand_over(s, q):
            for pc in (0, 1):
                @pl.when(c == pc)
                def _(pc=pc):
                    mine_v[s] = acc[piece(q, pc)]
                    send_v[s] = acc[piece(q, 1 - pc)].astype(BF16)
            to_sibling(s).start()

        for i in range(nblocks):
            @pl.when(jnp.logical_and(k == nk - 1, n == i))
            def _(i=i):
                if split == "cols":
                    hand_over(i, 0)
                else:
                    for q in range(4):
                        hand_over(q, q)

        @pl.when(jnp.logical_and(k == nk - 1, n == nblocks - 1))
        def _():
            for s in range(slots):
                to_sibling(s).wait_recv()
                sums_ref[s] = (mine_v[s] + stage_v[s].astype(F32)).astype(BF16)
            for s in range(slots):
                to_sibling(s).wait_send()
            if nr:
                c_finish()

    in_specs = [pl.BlockSpec((tk, m), lambda n, k: (k, 0)), pl.BlockSpec((tk, bw), lambda n, k: (k, n + first))]
    args = [a, b]
    if with_ctx:
        in_specs += [pl.BlockSpec(a2.shape, lambda n, k: (0, 0)), pl.BlockSpec(b2.shape, lambda n, k: (0, 0))]
        args += [a2, b2]
    in_specs += [HBM] * nr
    args += [pltpu.with_memory_space_constraint(s, pltpu.HBM) for s in riders]
    rider_sums = [jax.ShapeDtypeStruct((4,) + s.shape[1:], BF16) for s in riders]
    rider_scratch = []
    if nr:
        rider_scratch = (_chips_sems(nr) + [pltpu.VMEM(s.shape, F32) for s in rider_sums] * 2
                         + [pltpu.VMEM(s.shape, BF16) for s in rider_sums]
                         + [pltpu.SemaphoreType.DMA((nr, 4))] * 3)
    return _call(
        body, name=name, grid=(nblocks, nk),
        out_shape=[jax.ShapeDtypeStruct((slots, r, w), BF16)] + rider_sums + _chips_stage_shapes(rider_sums),
        in_specs=in_specs, out_specs=[pl.BlockSpec((slots, r, w), lambda n, k: (0, 0, 0))] + [HBM] * (2 * nr),
        scratch_shapes=[pltpu.VMEM((m, bw), F32), pltpu.VMEM((slots, r, w), F32), pltpu.VMEM((slots, r, w), BF16),
                        pltpu.VMEM((slots, r, w), BF16), pltpu.SemaphoreType.DMA((slots,)),
                        pltpu.SemaphoreType.DMA((slots,))] + rider_scratch,
        compiler_params=pltpu.CompilerParams(dimension_semantics=("arbitrary", "arbitrary"),
                                             vmem_limit_bytes=VMEM_LIMIT, has_side_effects=True,
                                             collective_id=barrier_id),
    )(*args)


def _grad_rows(xr, dz, w, mod, ng, dres, ncols, tm, name, chip_sums=(), first_chips=None, dests=None):
    rows = xr.shape[0]
    steps = rows // tm
    with_dx = dres is not None
    nr = len(chip_sums)
    dests = [d for d in (dests or [None] * nr)]
    nd = sum(d is not None for d in dests)
    nin = 6 if with_dx else 5
    nout = 2 if with_dx else 1

    def body(*refs):
        if with_dx:
            x_ref, dz_ref, w_ref, sc_ref, ng_ref, dres_ref = refs[:nin]
            dx_ref, vec_ref = refs[nin + nr + nd:nin + nr + nd + nout]
        else:
            x_ref, dz_ref, w_ref, sc_ref, ng_ref = refs[:nin]
            (vec_ref,) = refs[nin + nr + nd:nin + nr + nd + nout]
        if nr:
            o0 = nin + nr + nd + nout
            start, forward, finish = _chips_ops(refs[nin:nin + nr], refs[o0:o0 + nr], refs[o0 + nr:o0 + 2 * nr],
                                                *refs[o0 + 2 * nr:o0 + 2 * nr + 3], first_chips=first_chips,
                                                barrier=True)
            w_hbm, w_ref, w_sem = w_ref, refs[o0 + 2 * nr + 3], refs[o0 + 2 * nr + 4]

            @pl.when(pl.program_id(0) == 0)
            def _():
                start()
                w_load = pltpu.make_async_copy(w_hbm, w_ref, w_sem)
                w_load.start()
                w_load.wait()

            pl.when(pl.program_id(0) == steps // 2)(forward)
            pl.when(pl.program_id(0) == steps - 1)(finish)

        @pl.when(pl.program_id(0) == 0)
        def _():
            vec_ref[...] = jnp.zeros_like(vec_ref)

        dhn = _dot_nt(dz_ref[...], w_ref[...])
        x = x_ref[...]
        rs = lax.rsqrt(jnp.mean(x * x, axis=-1, keepdims=True) + NORM_EPS)
        xh = x * rs
        ngv = ng_ref[...]
        y = xh * ngv
        vec_ref[0:1, :] += jnp.sum(dhn, axis=0, keepdims=True)
        vec_ref[1:2, :] += jnp.sum(dhn * y, axis=0, keepdims=True)
        dy = dhn * (1.0 + sc_ref[...])
        vec_ref[2:3, :] += jnp.sum(dy * xh, axis=0, keepdims=True)
        if with_dx:
            dxh = dy * ngv
            dx_ref[...] = dres_ref[...] + rs * (dxh - xh * jnp.mean(dxh * xh, axis=-1, keepdims=True))

    tile = pl.BlockSpec((tm, D), lambda i: (i, 0))
    vec = pl.BlockSpec((1, D), lambda i: (0, 0))
    w_spec = HBM if nr else pl.BlockSpec((D, ncols), lambda i: (0, 0), pipeline_mode=pl.Buffered(1))
    in_specs = [tile, pl.BlockSpec((tm, ncols), lambda i: (i, 0)), w_spec, pl.BlockSpec((1, D), lambda i: (0, 1)), vec]
    out_shape = [jax.ShapeDtypeStruct((8, D), F32)]
    out_specs = [pl.BlockSpec((8, D), lambda i: (0, 0))]
    args = [xr, dz, pltpu.with_memory_space_constraint(w, pltpu.HBM) if nr else w, mod, ng]
    if with_dx:
        in_specs.append(tile)
        out_shape.insert(0, jax.ShapeDtypeStruct((rows, D), F32))
        out_specs.insert(0, tile)
        args.append(dres)
    aliases = {}
    for j, d in enumerate(dests):
        if d is not None:
            aliases[len(args) + nr + len(aliases)] = len(out_shape) + j
    in_specs += [HBM] * (nr + nd)
    out_specs += [HBM] * (2 * nr)
    out_shape += [jax.ShapeDtypeStruct((4,) + a.shape[1:], a.dtype) for a in chip_sums]
    out_shape += _chips_stage_shapes(chip_sums)
    args += [pltpu.with_memory_space_constraint(a, pltpu.HBM) for a in chip_sums]
    args += [pltpu.with_memory_space_constraint(d, pltpu.HBM) for d in dests if d is not None]
    return _call(body, name=name, grid=(steps,), out_shape=out_shape, in_specs=in_specs, out_specs=out_specs,
                 scratch_shapes=(_chips_sems(nr) + [pltpu.VMEM((D, ncols), BF16), pltpu.SemaphoreType.DMA(())])
                 if nr else [], input_output_aliases=aliases,
                 compiler_params=pltpu.CompilerParams(dimension_semantics=("arbitrary",),
                                                      vmem_limit_bytes=VMEM_LIMIT, has_side_effects=bool(nr),
                                                      collective_id=7 if nr else None))(*args)


def _adamw(w, g, m, v):
    m = ADAM_B1 * m + (1.0 - ADAM_B1) * g
    v = ADAM_B2 * v + (1.0 - ADAM_B2) * (g * g)
    m_hat = m / (1.0 - ADAM_B1 ** ADAM_STEP)
    v_hat = v / (1.0 - ADAM_B2 ** ADAM_STEP)
    delta = -ADAM_LR * (m_hat / (jnp.sqrt(v_hat) + ADAM_EPS) + ADAM_WD * w)
    return delta, m, v


def _adamw_reduced(parts, w, m, v, tr, name):
    r, n = w.shape
    nparts = parts.shape[0]

    def body(p_ref, w_ref, m_ref, v_ref, g_ref, d_ref, mo_ref, vo_ref):
        g = p_ref[0].astype(F32)
        for i in range(1, nparts):
            g = g + p_ref[i].astype(F32)
        g_ref[...] = g
        d_ref[...], mo_ref[...], vo_ref[...] = _adamw(w_ref[...], g, m_ref[...], v_ref[...])

    tile = pl.BlockSpec((tr, n), lambda i: (i, 0))
    sds = jax.ShapeDtypeStruct((r, n), F32)
    return _call(
        body, name=name, grid=(r // tr,), out_shape=[sds] * 4,
        in_specs=[pl.BlockSpec((nparts, tr, n), lambda i: (0, i, 0)), tile, tile, tile], out_specs=[tile] * 4,
        compiler_params=_params("arbitrary"),
    )(parts, w, m, v)


R_GATE, R_FINAL_G, R_LN_G, R_LN_B, R_LOSS = 0, 1, 2, 3, 4
R_SH_X, R_SC_X, R_NG_X = 5, 6, 7
R_SH_C, R_SC_C, R_NG_C = 8, 9, 10
R_LAM, R_CW, R_CB = 11, 13, 17
PACK_ROWS = 24
Q_BA, Q_BX, Q_SGU_B, PACK128_ROWS = 0, 16, 32, 40


def _reduce_small(vec_pieces, q_pieces, mat_parts, ada_w, me):
    nloc = ada_w.shape[1]
    nm = len(mat_parts)
    pieces = list(vec_pieces) + list(q_pieces)

    def body(me_ref, *refs):
        piece_refs, refs = refs[:len(pieces)], refs[len(pieces):]
        mp_refs, w_ref = refs[:nm], refs[nm]
        red_ref, redq_ref = refs[nm + 1:nm + 3]
        mats_all = refs[nm + 3:2 * nm + 3]
        cparts_ref, dmod_ref, gab_ref, loss_ref = refs[2 * nm + 3:2 * nm + 7]
        pack_ref, packq_ref, vp_ref, vq_ref = refs[2 * nm + 7:2 * nm + 11]
        mat_refs = refs[2 * nm + 11:3 * nm + 11]
        cpart_ref, dmc_s = refs[3 * nm + 11:3 * nm + 13]
        sems = refs[3 * nm + 13:]
        for dst, group in ((pack_ref, vec_pieces), (packq_ref, q_pieces)):
            row = 0
            for _, nrows in group:
                dst[row:row + nrows, :] = piece_refs[0][0:nrows, :]
                piece_refs, row = piece_refs[1:], row + nrows
            if row < dst.shape[0]:
                dst[row:, :] = jnp.zeros((dst.shape[0] - row, dst.shape[1]), F32)
        p_start, p_forward, p_finish = _gather2_ops([pack_ref, packq_ref], [vp_ref, vq_ref], ["ag", "ag"], *sems[:3],
                                                    barrier=True)
        m_start, m_forward, m_finish = _gather2_ops(mat_refs, mats_all, ["ag"] * nm, *sems[3:6])
        c_start, c_forward, c_finish = _gather2_ops([cpart_ref], [cparts_ref], ["ag"], *sems[6:])
        p_start()
        for mp_ref, mat_ref in zip(mp_refs, mat_refs):
            mat = mp_ref[0].astype(F32)
            for i in range(1, mp_ref.shape[0]):
                mat = mat + mp_ref[i].astype(F32)
            mat_ref[...] = mat
        m_start()
        p_forward()
        p_finish()
        red, redq = vp_ref[0], vq_ref[0]
        for i in range(1, N_DEV):
            red = red + vp_ref[i]
            redq = redq + vq_ref[i]
        red_ref[...] = red
        redq_ref[...] = redq
        loss_ref[...] = red_ref[R_LOSS:R_LOSS + 1, 0:1]
        for e in range(N_DEV):
            dmod_ref[e:e + 1, 0:D] = vp_ref[e, R_SH_X:R_SH_X + 1, :]
            dmod_ref[e:e + 1, D:2 * D] = vp_ref[e, R_SC_X:R_SC_X + 1, :]
            dmod_ref[e:e + 1, 2 * D:3 * D] = vp_ref[e, R_GATE:R_GATE + 1, :]
        dmod_ref[8:9, 0:D] = red[R_SH_C:R_SH_C + 1, :]
        dmod_ref[8:9, D:2 * D] = red[R_SC_C:R_SC_C + 1, :]
        dmod_ref[8:9, 2 * D:3 * D] = jnp.zeros((1, D), F32)
        dmod_ref[9:16, :] = jnp.zeros((7, 3 * D), F32)
        gab_ref[:, 0:D] = red[R_SH_X:R_SH_X + 1, :] + red[R_SH_C:R_SH_C + 1, :]
        gab_ref[:, D:2 * D] = red[R_SC_X:R_SC_X + 1, :] + red[R_SC_C:R_SC_C + 1, :]
        gab_ref[:, 2 * D:3 * D] = red[R_GATE:R_GATE + 1, :]
        dmc_s[...] = jnp.broadcast_to(dmod_ref[8:9, :], (8, 3 * D))
        off = pl.multiple_of(me_ref[0] * nloc, 128)
        cpart_ref[...] = _dot_nt(dmc_s[:, pl.ds(off, nloc)], w_ref[...])
        c_start()
        m_forward()
        c_forward()
        c_finish()
        m_finish()

    return _call(
        body, name="reduce_small",
        out_shape=[jax.ShapeDtypeStruct((PACK_ROWS, D), F32), jax.ShapeDtypeStruct((PACK128_ROWS, HD), F32)]
        + [jax.ShapeDtypeStruct((N_DEV,) + p.shape[1:], F32) for p in mat_parts]
        + [jax.ShapeDtypeStruct((N_DEV, 8, D), F32), jax.ShapeDtypeStruct((16, 3 * D), F32),
           jax.ShapeDtypeStruct((1, 3 * D), F32), jax.ShapeDtypeStruct((1, 1), F32)],
        in_specs=[pl.BlockSpec(memory_space=pltpu.SMEM)] + [VMEM] * (len(pieces) + nm + 1),
        out_specs=[VMEM] * (nm + 6),
        scratch_shapes=[pltpu.VMEM((PACK_ROWS, D), F32), pltpu.VMEM((PACK128_ROWS, HD), F32),
                        pltpu.VMEM((N_DEV, PACK_ROWS, D), F32), pltpu.VMEM((N_DEV, PACK128_ROWS, HD), F32)]
        + [pltpu.VMEM(p.shape[1:], F32) for p in mat_parts]
        + [pltpu.VMEM((8, D), F32), pltpu.VMEM((8, 3 * D), F32)] + _gather2_sems(2) + _gather2_sems(nm)
        + _gather2_sems(1),
        compiler_params=pltpu.CompilerParams(vmem_limit_bytes=VMEM_LIMIT, has_side_effects=True, collective_id=8),
    )(me, *[a for a, _ in pieces], *mat_parts, ada_w)


def _adamw_ada(c_all, c_ctx, dmod, w, m, v, me):
    nloc = w.shape[1]

    def body(me_ref, c_ref, cc_ref, dm_ref, w_ref, m_ref, v_ref, g_ref, d_ref, mo_ref, vo_ref):
        off = pl.multiple_of(me_ref[0] * nloc, 128)
        dm = dm_ref[:, pl.ds(off, nloc)]
        sx, _ = _silu_and_grad(c_ref[...])
        sc, _ = _silu_and_grad(cc_ref[...])
        g = _dot_tn(sx, dm[0:8, :]) + _dot_tn(jnp.broadcast_to(sc, (8, D)), dm[8:16, :])
        g_ref[...] = g
        d_ref[...], mo_ref[...], vo_ref[...] = _adamw(w_ref[...], g, m_ref[...], v_ref[...])

    sds = jax.ShapeDtypeStruct(w.shape, F32)
    return _call(
        body, name="adamw_ada_w", out_shape=[sds] * 4,
        in_specs=[pl.BlockSpec(memory_space=pltpu.SMEM)] + [VMEM] * 6, out_specs=[VMEM] * 4,
        compiler_params=_params(),
    )(me, c_all, c_ctx, dmod, w, m, v)


_SMALL = ("c_ctx", "ada_b", "norm_g", "conv_w", "conv_b", "lru_wa", "lru_ba", "lru_wx", "lru_bx", "lru_lambda",
          "sgu_ln_g", "sgu_ln_b", "sgu_w", "sgu_b", "final_g")


def _adamw_small(red, redq, mats, cparts, gab, ws, ms, vs, me):
    n = len(_SMALL)

    def body(me_ref, red_ref, redq_ref, wa_ref, wx_ref, sw_ref, cp_ref, gab_ref, *refs):
        w_refs, m_refs, v_refs = refs[:n], refs[n:2 * n], refs[2 * n:3 * n]
        outs = refs[3 * n:]
        off = pl.multiple_of(me_ref[0] * HD, 128)

        def row(r, k=1):
            return red_ref[r:r + k, :]

        cc = w_refs[0][...]
        dcc = cp_ref[0, 0:1, :]
        for i in range(1, N_DEV):
            dcc = dcc + cp_ref[i, 0:1, :]
        grads = dict(
            c_ctx=dcc * _silu_and_grad(cc)[1], ada_b=gab_ref[...], norm_g=row(R_NG_X) + row(R_NG_C),
            conv_w=red_ref[R_CW:R_CW + CONV_W, pl.ds(off, HD)], conv_b=row(R_CB),
            lru_wa=wa_ref[...], lru_ba=redq_ref[Q_BA:Q_BA + 2 * HEADS, :], lru_wx=wx_ref[...],
            lru_bx=redq_ref[Q_BX:Q_BX + 2 * HEADS, :], lru_lambda=red_ref[R_LAM:R_LAM + 2, pl.ds(off, HD)],
            sgu_ln_g=row(R_LN_G), sgu_ln_b=row(R_LN_B), sgu_w=sw_ref[...],
            sgu_b=redq_ref[Q_SGU_B:Q_SGU_B + HEADS, :], final_g=row(R_FINAL_G))
        for j, name in enumerate(_SMALL):
            g = grads[name]
            outs[j][...] = g
            outs[n + j][...], outs[2 * n + j][...], outs[3 * n + j][...] = _adamw(w_refs[j][...], g, m_refs[j][...],
                                                                                 v_refs[j][...])

    sds = [jax.ShapeDtypeStruct(ws[k].shape, F32) for k in _SMALL]
    outs = _call(
        body, name="adamw_small", out_shape=sds * 4,
        in_specs=[pl.BlockSpec(memory_space=pltpu.SMEM)] + [VMEM] * (7 + 3 * n), out_specs=[VMEM] * (4 * n),
        compiler_params=_params(),
    )(me, red, redq, *mats, cparts, gab, *[ws[k] for k in _SMALL], *[ms[k] for k in _SMALL],
      *[vs[k] for k in _SMALL])
    return [dict(zip(_SMALL, outs[i * n:(i + 1) * n])) for i in range(4)]


def kernel(x, c, ctx, c_ctx, ada_w, ada_b, norm_g, w_in, conv_w, conv_b, lru_wa, lru_ba, lru_wx, lru_bx, lru_lambda, sgu_ln_g, sgu_ln_b, sgu_w, sgu_b, w_out, final_g, loss_target, m_c_ctx, m_ada_w, m_ada_b, m_norm_g, m_w_in, m_conv_w, m_conv_b, m_lru_wa, m_lru_ba, m_lru_wx, m_lru_bx, m_lru_lambda, m_sgu_ln_g, m_sgu_ln_b, m_sgu_w, m_sgu_b, m_w_out, m_final_g, v_c_ctx, v_ada_w, v_ada_b, v_norm_g, v_w_in, v_conv_w, v_conv_b, v_lru_wa, v_lru_ba, v_lru_wx, v_lru_bx, v_lru_lambda, v_sgu_ln_g, v_sgu_ln_b, v_sgu_w, v_sgu_b, v_w_out, v_final_g):
    args = dict(locals())
    me = (4 * lax.axis_index("x") + 2 * lax.axis_index("y") + lax.axis_index("c")).astype(jnp.int32).reshape(1)
    xr, ctxr, tgt = x[0], ctx[0], loss_target[0]
    cc = c_ctx.reshape(1, D)
    nw = 2 * HEADS * HD
    view = dict(c_ctx=(1, D), ada_b=(1, 3 * D), norm_g=(1, D), conv_w=(CONV_W, HD), conv_b=(1, D), lru_wa=(nw, HD),
                lru_ba=(2 * HEADS, HD), lru_wx=(nw, HD), lru_bx=(2 * HEADS, HD), lru_lambda=(2, HD), sgu_ln_g=(1, D),
                sgu_ln_b=(1, D), sgu_w=(HEADS * CHUNK, CHUNK), sgu_b=(HEADS, CHUNK), final_g=(1, D))

    zx, hn, w_full, w_out_b, modx, modc, c_all, cw_full, lam_full = _front_project(
        xr, c, cc, ada_w[0], ada_b, norm_g, w_in[0], w_out[0], conv_w[0], lru_lambda[0], me)
    zc, hnc = _project(ctxr, modc, norm_g, w_full, D, LC, "project_ctx")
    ba, bx = lru_ba.reshape(view["lru_ba"]), lru_bx.reshape(view["lru_bx"])
    yl, wout_all = _lru_forward(zx, zc, cw_full, conv_b, lru_wa[0], lru_wx[0], ba, bx, lam_full, [w_out_b], ["ag"])
    wout_full = wout_all.reshape(D_MIX, D)
    dz, dyl, dxn, ycat, dob, dws, dbst, mvec = _mixer_loss(
        xr, tgt, zx, yl, modx, final_g.reshape(1, D), sgu_ln_g, sgu_ln_b, sgu_w[0], sgu_b[0].T, wout_full, ROWS)

    (wout_sums,) = _grad_w(ycat, dob, None, None, L, "grad_w_out", D, 0, 1, "rows", 1)
    (rest_sums,) = _grad_w(hn, dz, None, None, L, "grad_w_in_rest", 2 * W_IN_SHARD, 1, 3, "cols", 2)
    dz, dxac, dwa, dwx, dba, dbx, dlam, dcw, dcb, win_parts, wout_parts, _, _ = _lru_backward(
        zx, zc, dyl, dz, cw_full, conv_b, lru_wa[0], lru_wx[0], ba, bx, lam_full, [rest_sums, wout_sums],
        first_chips=[1, 0])
    mats = [dwa.reshape(N_DEV, nw // N_DEV, HD), dwx.reshape(N_DEV, nw // N_DEV, HD), dws]
    first_sums, *mat_parts = _grad_w(hn, dz, hnc, dxac, 2 * ROWS, "grad_w_in_first", 2 * W_IN_SHARD, 0, 1, "cols", 3,
                                     riders=mats)[:4]
    gx, xvec, win_parts = _grad_rows(
        xr, dz, w_full, modx, norm_g, dxn, D_IN, ROWS, "grad_rows_x", chip_sums=[first_sums], first_chips=[0],
        dests=[win_parts])[:3]
    (cvec,) = _grad_rows(ctxr, dxac, w_full, modc, norm_g, None, D, LC, "grad_rows_ctx")
    red, redq, *rest = _reduce_small(
        [(mvec, 5), (xvec, 3), (cvec, 3), (dlam, 2), (dcw, CONV_W), (dcb, 1)],
        [(dba, 2 * HEADS), (dbx, 2 * HEADS), (dbst.T, HEADS)], mat_parts, ada_w[0], me)
    mats_all, (cparts, dmod, gab, loss) = rest[:3], rest[3:]

    g_w_in, d_w_in, nm_w_in, nv_w_in = _adamw_reduced(win_parts, w_in[0], m_w_in[0], v_w_in[0], 2 * ROWS, "adamw_w_in")
    g_w_out, d_w_out, nm_w_out, nv_w_out = _adamw_reduced(wout_parts, w_out[0], m_w_out[0], v_w_out[0], ROWS // 2,
                                                          "adamw_w_out")
    g_ada, d_ada, nm_ada, nv_ada = _adamw_ada(c_all, cc, dmod, ada_w[0], m_ada_w[0], v_ada_w[0], me)
    ws = {k: args[k].reshape(view[k]) for k in _SMALL}
    ms = {k: args["m_" + k].reshape(view[k]) for k in _SMALL}
    vs = {k: args["v_" + k].reshape(view[k]) for k in _SMALL}
    small = _adamw_small(red, redq, [m.reshape(-1, HD) for m in mats_all], cparts, gab, ws, ms, vs, me)
    big = dict(w_in=(g_w_in, d_w_in, nm_w_in, nv_w_in), w_out=(g_w_out, d_w_out, nm_w_out, nv_w_out),
               ada_w=(g_ada, d_ada, nm_ada, nv_ada))

    loss = loss.reshape(())
    names = ("c_ctx", "ada_w", "ada_b", "norm_g", "w_in", "conv_w", "conv_b", "lru_wa", "lru_ba", "lru_wx", "lru_bx",
             "lru_lambda", "sgu_ln_g", "sgu_ln_b", "sgu_w", "sgu_b", "w_out", "final_g")
    outs = [loss, gx.reshape(x.shape)]
    for kind in range(4):
        for k in names:
            val = big[k][kind] if k in big else small[kind][k]
            outs.append(val.reshape(args[k].shape))
    return tuple(outs)
```

```python
import jax
import jax.numpy as jnp
from jax import lax
from jax.experimental import pallas as pl
from jax.experimental.pallas import tpu as pltpu

F32 = jnp.float32
BF16 = jnp.bfloat16

N_DEV = 8
D = 1024
L = 2048
LC = 256
HEADS = 8
HD = 128
CHUNK = 128
D_IN = 5 * D
W_IN_SHARD = D_IN // N_DEV
ROWS = 256
D_MIX = 2 * D
CONV_W = 4
LRU_C = 8.0
NORM_EPS = 1e-6
LN_EPS = 1e-5
ADAM_LR, ADAM_B1, ADAM_B2, ADAM_EPS, ADAM_WD, ADAM_STEP = 0.001, 0.9, 0.999, 1e-08, 0.01, 10

VMEM_LIMIT = 56 * 1024 * 1024

HBM = pl.BlockSpec(memory_space=pltpu.HBM)
VMEM = pl.BlockSpec(memory_space=pltpu.VMEM)
MESH = pl.DeviceIdType.MESH


def _call(body, **kw):
    return pl.pallas_call(body, **kw)


def _params(*sem):
    return pltpu.CompilerParams(dimension_semantics=sem, vmem_limit_bytes=VMEM_LIMIT)


def _sigmoid(x):
    return 0.5 * jnp.tanh(0.5 * x) + 0.5


def _silu_and_grad(x):
    s = _sigmoid(x)
    return x * s, s * (1.0 + x * (1.0 - s))


_G0 = 0.7978845608028654
_G1 = 0.044715


def _gelu_and_grad(x):
    x2 = x * x
    t = jnp.tanh(_G0 * (x + _G1 * x * x2))
    cdf = 0.5 * (1.0 + t)
    return x * cdf, cdf + 0.5 * x * (1.0 - t * t) * (_G0 * (1.0 + 3.0 * _G1 * x2))


def _softplus(z):
    t = jnp.exp(-jnp.abs(z))
    u = 1.0 + t
    log1p = jnp.where(u == 1.0, t, jnp.log(u) * t / jnp.where(u == 1.0, 1.0, u - 1.0))
    return jnp.maximum(z, 0.0) + log1p


def _dot(a, b):
    return jnp.dot(a, b, preferred_element_type=F32)


def _dot_nt(a, b):
    return lax.dot_general(a, b, (((1,), (1,)), ((), ())), preferred_element_type=F32)


def _dot_tn(a, b):
    return lax.dot_general(a, b, (((0,), (0,)), ((), ())), preferred_element_type=F32)


def _rows(shape):
    return lax.broadcasted_iota(jnp.int32, shape, 0)


def _gather2_shapes(arrays, modes):
    return [jax.ShapeDtypeStruct((N_DEV,) + a.shape if m == "ag" else (a.shape[0], N_DEV * a.shape[1]), a.dtype)
            for a, m in zip(arrays, modes)]


def _gather2_sems(n):
    return [pltpu.SemaphoreType.DMA((n, N_DEV - 1)), pltpu.SemaphoreType.DMA((n, N_DEV - 1)),
            pltpu.SemaphoreType.DMA((n,))]


def _barrier(peers):
    sem = pltpu.get_barrier_semaphore()
    for peer in peers:
        pl.semaphore_signal(sem, inc=1, device_id=peer, device_id_type=MESH)
    pl.semaphore_wait(sem, len(peers))


def _gather2_ops(ins, outs, modes, send_sems, recv_sems, local_sems, barrier=False):
    n = len(ins)
    x, y, c = lax.axis_index("x"), lax.axis_index("y"), lax.axis_index("c")
    me, sibling = (x, y, c), (x, y, 1 - c)
    chips = [(x ^ (k >> 1), y ^ (k & 1)) for k in (1, 2, 3)]

    def slot(j, px, py, pc):
        dev = 4 * px + 2 * py + pc
        if modes[j] == "agc":
            w = ins[j].shape[1]
            return outs[j].at[:, pl.ds(pl.multiple_of(dev * w, 128), w)]
        return outs[j].at[dev]

    def copy(j, k, block, to, src=None):
        return pltpu.make_async_remote_copy(
            src_ref=slot(j, *block) if src is None else src, dst_ref=slot(j, *block),
            send_sem=send_sems.at[j, k], recv_sem=recv_sems.at[j, k], device_id=to, device_id_type=MESH)

    def own(j):
        return pltpu.make_async_copy(ins[j], slot(j, *me), local_sems.at[j])

    def first(j):
        return [copy(j, 0, me, sibling, src=ins[j])] + [copy(j, 1 + i, me, (*chip, c), src=ins[j])
                                                        for i, chip in enumerate(chips)]

    def passed(j, i):
        return copy(j, 4 + i, (*chips[i], c), sibling)

    def start():
        if barrier:
            _barrier([sibling] + [(*chip, c) for chip in chips])
        for j in range(n):
            own(j).start()
            for cp in first(j):
                cp.start()

    def forward():
        for i, chip in enumerate(chips):
            for j in range(n):
                copy(j, 1 + i, (*chip, c), me).wait_recv()
                passed(j, i).start()

    def finish():
        for j in range(n):
            copy(j, 0, sibling, me).wait_recv()
            for i, chip in enumerate(chips):
                copy(j, 4 + i, (*chip, 1 - c), me).wait_recv()
            for cp in first(j) + [passed(j, i) for i in range(3)]:
                cp.wait_send()
            own(j).wait()

    return start, forward, finish


def _sibling_barrier():
    sem = pltpu.get_barrier_semaphore()
    sibling = (lax.axis_index("x"), lax.axis_index("y"), 1 - lax.axis_index("c"))
    pl.semaphore_signal(sem, inc=1, device_id=sibling, device_id_type=MESH)
    pl.semaphore_wait(sem, 1)


def _chips_sems(n):
    return [pltpu.SemaphoreType.DMA((n, 6)), pltpu.SemaphoreType.DMA((n, 6)), pltpu.SemaphoreType.DMA((n,))]


def _chips_stage_shapes(chip_sums):
    return [jax.ShapeDtypeStruct((2, a.shape[1] // 2, a.shape[2]), a.dtype) for a in chip_sums]


def _chips_ops(ins, outs, stages, send_sems, recv_sems, local_sems, first_chips=None, barrier=False):
    x, y, c = lax.axis_index("x"), lax.axis_index("y"), lax.axis_index("c")
    qm = 2 * x + y
    first_chips = first_chips or [0] * len(ins)

    def owns(j, chip):
        lo, cnt = first_chips[j], ins[j].shape[0]
        if lo == 0 and cnt == 4:
            return None
        return jnp.logical_and(chip >= lo, chip < lo + cnt)

    def guarded(cond, fn):
        if cond is None:
            fn()
        else:
            pl.when(cond)(fn)

    def slot(j, chip):
        return jnp.clip(chip - first_chips[j], 0, ins[j].shape[0] - 1)

    def half(j, i):
        h = ins[j].shape[1] // 2
        return pl.ds(i * h, h)

    def copy(j, sem, src, dst, k):
        return pltpu.make_async_remote_copy(
            src_ref=src, dst_ref=dst, send_sem=send_sems.at[j, sem], recv_sem=recv_sems.at[j, sem],
            device_id=(x ^ (k >> 1), y ^ (k & 1), c), device_id_type=MESH)

    def direct(j, k):
        return copy(j, k - 1, ins[j].at[slot(j, qm ^ k)], outs[j].at[qm], k)

    def first_hop(j, k):
        return copy(j, 1 + k, ins[j].at[slot(j, qm ^ 3), half(j, k - 1)], stages[j].at[k - 1], k)

    def second_hop(j, k):
        return copy(j, 3 + k, stages[j].at[2 - k], outs[j].at[qm ^ (3 - k), half(j, 2 - k)], k)

    def local(j):
        return pltpu.make_async_copy(ins[j].at[slot(j, qm)], outs[j].at[qm], local_sems.at[j])

    def start():
        if barrier:
            _barrier([(x ^ (k >> 1), y ^ (k & 1), c) for k in (1, 2)])
        for j in range(len(ins)):
            for k in (1, 2):
                guarded(owns(j, qm ^ 3), lambda j=j, k=k: first_hop(j, k).start())
        for j in range(len(ins)):
            for k in (1, 2):
                guarded(owns(j, qm ^ k), lambda j=j, k=k: direct(j, k).start())
            guarded(owns(j, qm), lambda j=j: local(j).start())

    def forward():
        for j in range(len(ins)):
            for k in (1, 2):
                def pass_on(j=j, k=k):
                    first_hop(j, 3 - k).wait_recv()
                    second_hop(j, k).start()
                guarded(owns(j, qm ^ k), pass_on)

    def finish():
        for j in range(len(ins)):
            for k in (1, 2):
                guarded(owns(j, qm ^ k), lambda j=j, k=k: direct(j, k).wait_send())
                guarded(owns(j, qm ^ k), lambda j=j, k=k: second_hop(j, k).wait_send())
                guarded(owns(j, qm ^ 3), lambda j=j, k=k: first_hop(j, k).wait_send())
                guarded(owns(j, qm), lambda j=j, k=k: direct(j, k).wait_recv())
                guarded(owns(j, qm), lambda j=j, k=k: second_hop(j, k).wait_recv())
            guarded(owns(j, qm), lambda j=j: local(j).wait())

    return start, forward, finish


ARRIVAL = (0, 1, 2, 4, 3, 5, 6, 7)


def _front_project(xr, c, c_ctx, ada_w, ada_b, ng, w_in, w_out, cw, lam, me):
    nloc = ada_w.shape[1]
    ws = W_IN_SHARD
    arrival = jnp.asarray(ARRIVAL, jnp.int32)

    def body(me_ref, arr_ref, x_ref, c_ref, cc_ref, aw_ref, ab_ref, ng_ref, win_ref, wout_ref, cw_ref, lam_ref,
             z_ref, hn_ref, wfull_ref, woutb_ref, modx_ref, modc_ref, call_ref, cwf_ref, lamf_ref,
             wv, call_s, part_s, parts_s, w_send, w_recv, hbm_sems, s_send, s_recv, g_send, g_recv, g_local,
             x_v, x_sem):
        t = pl.program_id(0)
        x_load = pltpu.make_async_copy(x_ref, x_v, x_sem)
        x, y, cidx = lax.axis_index("x"), lax.axis_index("y"), lax.axis_index("c")
        me_i = me_ref[0]
        sibling = (x, y, 1 - cidx)
        chips = [(x ^ (k >> 1), y ^ (k & 1)) for k in (1, 2, 3)]
        g_start, g_pass, g_finish = _gather2_ops([cw_ref, lam_ref], [cwf_ref, lamf_ref], ["agc", "agc"],
                                                 g_send, g_recv, g_local)

        def shard_copy(k, px, py, pc, to, half=None):
            slot = wv.at[4 * px + 2 * py + pc]
            if half is not None:
                slot = slot.at[pl.ds(half * (D // 2), D // 2), :]
            return pltpu.make_async_remote_copy(src_ref=slot, dst_ref=slot, send_sem=w_send.at[k],
                                                recv_sem=w_recv.at[k], device_id=to, device_id_type=MESH)

        def small_gather(src, my_slot, stage):
            copies = []
            for k in range(1, N_DEV):
                peer = (x ^ (k >> 2), y ^ ((k >> 1) & 1), cidx ^ (k & 1))
                cp = pltpu.make_async_remote_copy(src_ref=src, dst_ref=my_slot, send_sem=s_send.at[stage, k - 1],
                                                  recv_sem=s_recv.at[stage, k - 1], device_id=peer,
                                                  device_id_type=MESH)
                cp.start()
                copies.append(cp)
            pltpu.sync_copy(src, my_slot)
            return copies

        def finish_small(copies):
            for cp in copies:
                cp.wait()

        def to_neighbours(half):
            for i in (0, 1):
                shard_copy(1 + i, x, y, cidx, (*chips[i], cidx), half=half).start()

        @pl.when(t == 0)
        def _():
            _barrier([(x ^ (k >> 2), y ^ ((k >> 1) & 1), cidx ^ (k & 1)) for k in range(1, N_DEV)])
            g_start()
            x_load.start()
            wv[me_i] = win_ref[...].astype(BF16)
            woutb_ref[...] = wout_ref[...].astype(BF16)
            shard_copy(0, x, y, cidx, sibling).start()
            conds_sent = small_gather(c_ref, call_s.at[pl.ds(me_i, 1), :], 0)
            to_neighbours(0)
            finish_small(conds_sent)
            call_ref[...] = call_s[...]
            off = pl.multiple_of(me_i * nloc, 128)
            b = ab_ref[:, pl.ds(off, nloc)]
            w = aw_ref[...]
            sx, _ = _silu_and_grad(call_s[...])
            sc, _ = _silu_and_grad(jnp.broadcast_to(cc_ref[...], (8, D)))
            part_s[0:8, :] = _dot(sx, w) + b
            part_s[8:16, :] = _dot(sc, w) + b
            parts_sent = small_gather(part_s, parts_s.at[me_i], 1)
            to_neighbours(1)
            finish_small(parts_sent)
            mine = _rows((16, nloc)) == me_i
            for j in range(N_DEV):
                pj = parts_s[j]
                modx_ref[:, j * nloc:(j + 1) * nloc] = jnp.sum(jnp.where(mine, pj, 0.0), axis=0, keepdims=True)
                modc_ref[:, j * nloc:(j + 1) * nloc] = pj[8:9, :]
            shift, scale1, ngv = modx_ref[:, 0:D], 1.0 + modx_ref[:, D:2 * D], ng_ref[...]
            x_load.wait()
            for r in range(L // ROWS):
                rsl = slice(r * ROWS, (r + 1) * ROWS)
                xv = x_v[rsl, :]
                rs = lax.rsqrt(jnp.mean(xv * xv, axis=-1, keepdims=True) + NORM_EPS)
                hn_ref[rsl, :] = ((xv * rs * ngv) * scale1 + shift).astype(BF16)

        @pl.when(t == 1)
        def _():
            shard_copy(0, x, y, 1 - cidx, sibling).wait_recv()
            g_pass()

        for i in (0, 1):
            @pl.when(t == ARRIVAL.index((2, 4)[i]))
            def _(i=i):
                shard_copy(1 + i, *chips[i], cidx, sibling).wait_recv()
                shard_copy(4 + i, *chips[i], cidx, sibling).start()
                shard_copy((7, 3)[i], *chips[i], cidx, (*chips[1 - i], cidx), half=i).start()

        @pl.when(t == ARRIVAL.index(6))
        def _():
            shard_copy(3, *chips[2], cidx, sibling, half=1).wait_recv()
            shard_copy(7, *chips[2], cidx, sibling, half=0).wait_recv()
            shard_copy(6, *chips[2], cidx, sibling).start()

        for i in range(3):
            @pl.when(t == ARRIVAL.index((3, 5, 7)[i]))
            def _(i=i):
                shard_copy(4 + i, *chips[i], 1 - cidx, sibling).wait_recv()

        @pl.when(t == 2)
        def _():
            g_finish()

        dev = me_i ^ arr_ref[t]
        for r in range(L // (2 * ROWS)):
            rsl = slice(r * 2 * ROWS, (r + 1) * 2 * ROWS)
            z_ref[rsl, :] = _dot(hn_ref[rsl, :], wv[dev])
        col = pl.ds(pl.multiple_of(dev * ws, 128), ws)
        pltpu.make_async_copy(wv.at[dev], wfull_ref.at[:, col], hbm_sems.at[t]).start()

        @pl.when(t == N_DEV - 1)
        def _():
            for k in (0, 1, 2, 4, 5, 6):
                shard_copy(k, x, y, cidx, sibling).wait_send()
            for k in (3, 7):
                shard_copy(k, x, y, cidx, sibling, half=0).wait_send()
            for s in range(N_DEV):
                pltpu.make_async_copy(wv.at[0], wfull_ref.at[:, pl.ds(0, ws)], hbm_sems.at[s]).wait()

    const = lambda *shape: pl.BlockSpec(shape, lambda t, m, a: (0,) * len(shape))
    once = lambda *shape: pl.BlockSpec(shape, lambda t, m, a: (0,) * len(shape), pipeline_mode=pl.Buffered(1))
    return _call(
        body, name="front_project",
        out_shape=[jax.ShapeDtypeStruct((L, D_IN), F32), jax.ShapeDtypeStruct((L, D), BF16),
                   jax.ShapeDtypeStruct((D, D_IN), BF16), jax.ShapeDtypeStruct(w_out.shape, BF16),
                   jax.ShapeDtypeStruct((1, 3 * D), F32), jax.ShapeDtypeStruct((1, 3 * D), F32),
                   jax.ShapeDtypeStruct((N_DEV, D), F32), jax.ShapeDtypeStruct((CONV_W, D), F32),
                   jax.ShapeDtypeStruct((2, D), F32)],
        grid_spec=pltpu.PrefetchScalarGridSpec(
            num_scalar_prefetch=2, grid=(N_DEV,),
            in_specs=[HBM, const(1, D), const(1, D), once(D, nloc), const(1, 3 * D), const(1, D),
                      once(D, ws), once(*w_out.shape), HBM, HBM],
            out_specs=[pl.BlockSpec((L, ws), lambda t, m, a: (0, m[0] ^ a[t])), const(L, D), HBM,
                       const(*w_out.shape),
                       const(1, 3 * D), const(1, 3 * D), const(N_DEV, D), HBM, HBM],
            scratch_shapes=[pltpu.VMEM((N_DEV, D, ws), BF16), pltpu.VMEM((N_DEV, D), F32), pltpu.VMEM((16, nloc), F32),
                            pltpu.VMEM((N_DEV, 16, nloc), F32), pltpu.SemaphoreType.DMA((8,)),
                            pltpu.SemaphoreType.DMA((8,)), pltpu.SemaphoreType.DMA((N_DEV,)),
                            pltpu.SemaphoreType.DMA((2, N_DEV - 1)), pltpu.SemaphoreType.DMA((2, N_DEV - 1))]
            + _gather2_sems(2) + [pltpu.VMEM((L, D), F32), pltpu.SemaphoreType.DMA(())]),
        compiler_params=pltpu.CompilerParams(dimension_semantics=("arbitrary",), vmem_limit_bytes=VMEM_LIMIT,
                                             has_side_effects=True, collective_id=10),
    )(me, arrival, pltpu.with_memory_space_constraint(xr, pltpu.HBM), c, c_ctx, ada_w, ada_b, ng, w_in, w_out,
      pltpu.with_memory_space_constraint(cw, pltpu.HBM), pltpu.with_memory_space_constraint(lam, pltpu.HBM))


def _project(xr, mod, ng, w, ncols, tm, name):
    rows = xr.shape[0]

    def body(x_ref, sh_ref, sc_ref, ng_ref, w_ref, z_ref, hn_ref):
        x = x_ref[...]
        rs = lax.rsqrt(jnp.mean(x * x, axis=-1, keepdims=True) + NORM_EPS)
        hn = (x * rs * ng_ref[...]) * (1.0 + sc_ref[...]) + sh_ref[...]
        hb = hn.astype(BF16)
        hn_ref[...] = hb
        for n in range(ncols // D):
            z_ref[:, n * D:(n + 1) * D] = _dot(hb, w_ref[:, n * D:(n + 1) * D])

    vec = pl.BlockSpec((1, D), lambda i: (0, 0))
    return _call(
        body, name=name, grid=(rows // tm,),
        out_shape=[jax.ShapeDtypeStruct((rows, ncols), F32), jax.ShapeDtypeStruct((rows, D), BF16)],
        in_specs=[pl.BlockSpec((tm, D), lambda i: (i, 0)), vec, pl.BlockSpec((1, D), lambda i: (0, 1)), vec,
                  pl.BlockSpec((D, ncols), lambda i: (0, 0), pipeline_mode=pl.Buffered(1))],
        out_specs=[pl.BlockSpec((tm, ncols), lambda i: (i, 0)), pl.BlockSpec((tm, D), lambda i: (i, 0))],
        compiler_params=_params("arbitrary"),
    )(xr, mod, mod, ng, w)


def _scan_pair(af_ref, uf_ref, hf_ref, h0f, ab_ref, ub_ref, hb_ref, h0b, t_len):
    span = 8 * SCAN_BLOCKS
    nit = t_len // span
    rows = _rows((8, HD))

    def local_scan(a, b, forward):
        for s in (1, 2, 4):
            sh = s if forward else 8 - s
            m = rows >= s if forward else rows < 8 - s
            b = a * jnp.where(m, pltpu.roll(b, sh, 0), 0.0) + b
            a = a * jnp.where(m, pltpu.roll(a, sh, 0), 1.0)
        return a, b

    def span_scan(a_ref, u_ref, h_ref, off, carry, forward):
        order = range(SCAN_BLOCKS) if forward else range(SCAN_BLOCKS - 1, -1, -1)
        last = slice(7, 8) if forward else slice(0, 1)
        for q in order:
            rs = pl.ds(off + 8 * q, 8)
            a, b = local_scan(a_ref[rs, :], u_ref[rs, :], forward)
            h_ref[rs, :] = b + a * carry
            carry = a[last, :] * carry + b[last, :]
        return carry

    def body(k, carry):
        cf, cb = carry
        cf = span_scan(af_ref, uf_ref, hf_ref, pl.multiple_of(k * span, span), cf, True)
        cb = span_scan(ab_ref, ub_ref, hb_ref, pl.multiple_of((nit - 1 - k) * span, span), cb, False)
        return cf, cb

    return lax.fori_loop(0, nit, body, (h0f, h0b))


SCAN_BLOCKS = 16


def _shifted(pad_ref, x, offsets, before=0.0, after=0.0):
    n = x.shape[0]
    pad_ref[0:8, :] = jnp.broadcast_to(jnp.asarray(before, F32), (8, x.shape[1]))
    pad_ref[8:8 + n, :] = x
    pad_ref[8 + n:16 + n, :] = jnp.broadcast_to(jnp.asarray(after, F32), (8, x.shape[1]))
    return [pad_ref[8 + o:8 + o + n, :] for o in offsets]


def _conv(xa, cw, cb, pad_ref):
    xm1, xp1, xp2 = _shifted(pad_ref, xa, (-1, 1, 2))
    return xm1 * cw[0:1, :] + xa * cw[1:2, :] + xp1 * cw[2:3, :] + xp2 * cw[3:4, :] + cb


def _gates(xc, wa, wx, ba, bx, nsp):
    xb = xc.astype(BF16)
    r = _sigmoid(_dot(xb, wa) + ba)
    i = _sigmoid(_dot(xb, wx) + bx)
    log_a = r * nsp
    a = jnp.exp(log_a)
    g2 = jnp.tanh(log_a) * (-1.0 - a * a)
    rg = lax.rsqrt(jnp.maximum(g2, 1e-30))
    return r, i, a, g2 * rg, rg


def _lru_param_specs():
    h4 = pl.BlockSpec((2, 1, HD, HD), lambda h: (0, h, 0, 0))
    v2 = pl.BlockSpec((2, HD), lambda h: (0, h))
    b16 = pl.BlockSpec((2 * HEADS, HD), lambda h: (0, 0))
    return dict(
        xa=pl.BlockSpec((L, HD), lambda h: (0, h)), xac=pl.BlockSpec((LC, HD), lambda h: (0, h)),
        cw=pl.BlockSpec((CONV_W, HD), lambda h: (0, h)), cb=pl.BlockSpec((1, HD), lambda h: (0, h)), h4=h4, v2=v2,
        b16=b16)


def _bias_row(ref, d):
    mask = _rows((2 * HEADS, HD)) == d * HEADS + pl.program_id(0)
    return jnp.sum(jnp.where(mask, ref[...], 0.0), axis=0, keepdims=True), mask


def _lru_forward(zx, zc, cw, cb, wa, wx, ba, bx, lam, gather, gather_modes):
    ng_ = len(gather)

    def body(xa_ref, xac_ref, cw_ref, cb_ref, wa_ref, wx_ref, ba_ref, bx_ref, lam_ref, *rest):
        yl_ref = rest[ng_]
        af, uf, hf, ab, ub, hb, pad_s = rest[2 * ng_ + 1:2 * ng_ + 8]
        start, pass_on, finish = _gather2_ops(rest[:ng_], rest[ng_ + 1:2 * ng_ + 1], gather_modes,
                                              *rest[2 * ng_ + 8:], barrier=True)
        pl.when(pl.program_id(0) == 0)(start)
        pl.when(pl.program_id(0) == HEADS // 2)(pass_on)
        pl.when(pl.program_id(0) == HEADS - 1)(finish)
        cwv, cbv = cw_ref[...], cb_ref[...]
        nsp = (-LRU_C) * _softplus(-lam_ref[...])

        def forward(xa, t_len, h0f, h0b):
            xc = _conv(xa, cwv, cbv, pad_s)
            for d, (a_ref, u_ref) in enumerate(((af, uf), (ab, ub))):
                _, i, a, gamma, _ = _gates(xc, wa_ref[d, 0].astype(BF16), wx_ref[d, 0].astype(BF16),
                                           _bias_row(ba_ref, d)[0], _bias_row(bx_ref, d)[0], nsp[d:d + 1, :])
                a_ref[0:t_len, :] = a
                u_ref[0:t_len, :] = gamma * (i * xc)
            return _scan_pair(af, uf, hf, h0f, ab, ub, hb, h0b, t_len)

        z = jnp.zeros((1, HD), F32)
        h0f, h0b = forward(xac_ref[...], LC, z, z)
        forward(xa_ref[...], L, h0f, h0b)
        yl_ref[...] = hf[...] + hb[...]

    s = _lru_param_specs()
    return _call(
        body, name="lru_forward", grid=(HEADS,),
        out_shape=[jax.ShapeDtypeStruct((L, D), F32)] + _gather2_shapes(gather, gather_modes),
        in_specs=[s["xa"], s["xac"], s["cw"], s["cb"], s["h4"], s["h4"], s["b16"], s["b16"], s["v2"]] + [HBM] * ng_,
        out_specs=[pl.BlockSpec((L, HD), lambda h: (0, h))] + [HBM] * ng_,
        scratch_shapes=[pltpu.VMEM((L, HD), F32)] * 6 + [pltpu.VMEM((L + 16, HD), F32)] + _gather2_sems(ng_),
        compiler_params=pltpu.CompilerParams(dimension_semantics=("arbitrary",), vmem_limit_bytes=VMEM_LIMIT,
                                             has_side_effects=True, collective_id=5),
    )(zx, zc, cw, cb, wa, wx, ba, bx, lam, *[pltpu.with_memory_space_constraint(a, pltpu.HBM) for a in gather])


def _lru_backward(zx, zc, dyl, dz, cw, cb, wa, wx, ba, bx, lam, chip_sums, first_chips=None):
    nr = len(chip_sums)

    def body(xa_ref, xac_ref, dyl_ref, dz_in, cw_ref, cb_ref, wa_ref, wx_ref, ba_ref, bx_ref, lam_ref, *rest):
        (dxa_ref, dxac_ref, dwa_ref, dwx_ref, dba_ref, dbx_ref, dlam_ref, dcw_ref,
         dcb_ref) = rest[nr:nr + 9]
        main_s, ctx_s, pad_s = rest[3 * nr + 9:3 * nr + 12]
        if nr:
            start, forward, finish = _chips_ops(rest[:nr], rest[nr + 9:2 * nr + 9], rest[2 * nr + 9:3 * nr + 9],
                                                *rest[3 * nr + 12:], first_chips=first_chips, barrier=True)
            pl.when(pl.program_id(0) == 0)(start)
            pl.when(pl.program_id(0) == HEADS // 2)(forward)
            pl.when(pl.program_id(0) == HEADS - 1)(finish)
        del dz_in

        @pl.when(pl.program_id(0) == 0)
        def _():
            dba_ref[...] = jnp.zeros_like(dba_ref)
            dbx_ref[...] = jnp.zeros_like(dbx_ref)

        cwv, cbv = cw_ref[...], cb_ref[...]
        lamv = lam_ref[...]
        sp = _softplus(-lamv)
        nsp = (-LRU_C) * sp
        z = jnp.zeros((1, HD), F32)

        def wmat(ref, d):
            return ref[d, 0].astype(BF16)

        def workspace(s):
            return dict(a=(s.at[0], s.at[1]), u=(s.at[2], s.at[3]), h=(s.at[4], s.at[5]), rho=(s.at[6], s.at[7]),
                        saved=(tuple(s.at[8 + k] for k in range(4)), tuple(s.at[12 + k] for k in range(4))),
                        xc=s.at[16])

        def forward(ws, xa, t_len, h0f, h0b):
            xc = _conv(xa, cwv, cbv, pad_s)
            ws["xc"][...] = xc
            for d in (0, 1):
                vals = _gates(xc, wmat(wa_ref, d), wmat(wx_ref, d), _bias_row(ba_ref, d)[0],
                              _bias_row(bx_ref, d)[0], nsp[d:d + 1, :])
                r, i, a, gamma, rg = vals
                ws["a"][d][...] = a
                ws["u"][d][...] = gamma * (i * xc)
                for ref, val in zip(ws["saved"][d], (r, i, gamma, rg)):
                    ref[...] = val
            return _scan_pair(ws["a"][0], ws["u"][0], ws["h"][0], h0f, ws["a"][1], ws["u"][1], ws["h"][1], h0b,
                              t_len)

        def backward(ws, xa, t_len, h0f, h0b, dhf, dhb, first):
            xc = ws["xc"][...]
            (af, ab), (uf, ub), (hf, hb), (rf, rb) = ws["a"], ws["u"], ws["h"], ws["rho"]
            uf[...] = ab[...] * dhb
            ub[...] = af[...] * dhf
            rho_b_last, rho_f_first = _scan_pair(ab, uf, rb, z, af, ub, rf, z, t_len)
            dxc = jnp.zeros((t_len, HD), F32)
            dsp = []
            for d in (0, 1):
                r, i, gamma, rg = (ref[...] for ref in ws["saved"][d])
                a = ws["a"][d][...]
                if d == 0:
                    lam_t = dhf + _shifted(pad_s, rf[...], (1,))[0]
                    h_prev = _shifted(pad_s, hf[...], (-1,), before=h0f)[0]
                else:
                    lam_t = dhb + _shifted(pad_s, rb[...], (-1,))[0]
                    h_prev = _shifted(pad_s, hb[...], (1,), after=h0b)[0]
                da = lam_t * h_prev
                lx = lam_t * xc
                d_i = lx * gamma
                d_gamma = lx * i
                dxc = dxc + lam_t * (gamma * i)
                d_log_a = a * (da - d_gamma * (a * rg))
                dsp.append(jnp.sum(d_log_a * r, axis=0, keepdims=True) * (-LRU_C))
                d_pre_r = d_log_a * nsp[d:d + 1, :] * (r * (1.0 - r))
                d_pre_i = d_i * (i * (1.0 - i))
                prb, pib, xb = d_pre_r.astype(BF16), d_pre_i.astype(BF16), xc.astype(BF16)
                dxc = dxc + _dot_nt(prb, wmat(wa_ref, d)) + _dot_nt(pib, wmat(wx_ref, d))
                g_wa, g_wx = _dot_tn(xb, prb), _dot_tn(xb, pib)
                g_ba = jnp.sum(d_pre_r, axis=0, keepdims=True)
                g_bx = jnp.sum(d_pre_i, axis=0, keepdims=True)
                mask = _bias_row(ba_ref, d)[1]
                dba_ref[...] += jnp.where(mask, g_ba, 0.0)
                dbx_ref[...] += jnp.where(mask, g_bx, 0.0)
                if first:
                    dwa_ref[d, 0] = g_wa
                    dwx_ref[d, 0] = g_wx
                else:
                    dwa_ref[d, 0] += g_wa
                    dwx_ref[d, 0] += g_wx
            g_lam = jnp.concatenate(dsp, axis=0) * (-_sigmoid(-lamv))
            dm1, dp1, dm2 = _shifted(pad_s, dxc, (-1, 1, -2))
            dxa = dp1 * cwv[0:1, :] + dxc * cwv[1:2, :] + dm1 * cwv[2:3, :] + dm2 * cwv[3:4, :]
            xm1, xp1, xp2 = _shifted(pad_s, xa, (-1, 1, 2))
            g_cw = jnp.concatenate([jnp.sum(dxc * v, axis=0, keepdims=True) for v in (xm1, xa, xp1, xp2)], axis=0)
            g_cb = jnp.sum(dxc, axis=0, keepdims=True)
            if first:
                dlam_ref[...] = g_lam
                dcw_ref[...] = g_cw
                dcb_ref[...] = g_cb
            else:
                dlam_ref[...] += g_lam
                dcw_ref[...] += g_cw
                dcb_ref[...] += g_cb
            return dxa, rho_f_first, rho_b_last

        ws_x, ws_c = workspace(main_s), workspace(ctx_s)
        h0f, h0b = forward(ws_c, xac_ref[...], LC, z, z)
        forward(ws_x, xa_ref[...], L, h0f, h0b)
        dh = dyl_ref[...]
        dxa, dh0f, dh0b = backward(ws_x, xa_ref[...], L, h0f, h0b, dh, dh, True)
        dxa_ref[...] = dxa.astype(BF16)
        rc = _rows((LC, HD))
        dxac, _, _ = backward(ws_c, xac_ref[...], LC, z, z, jnp.where(rc == LC - 1, dh0f, 0.0),
                              jnp.where(rc == 0, dh0b, 0.0), False)
        dxac_ref[...] = dxac.astype(BF16)

    s = _lru_param_specs()
    col = lambda r: pl.BlockSpec((r, HD), lambda h: (0, h))
    return _call(
        body, name="lru_backward", grid=(HEADS,),
        out_shape=[jax.ShapeDtypeStruct((L, D_IN), BF16), jax.ShapeDtypeStruct((LC, D), BF16),
                   jax.ShapeDtypeStruct((2, HEADS, HD, HD), F32), jax.ShapeDtypeStruct((2, HEADS, HD, HD), F32),
                   jax.ShapeDtypeStruct((2 * HEADS, HD), F32), jax.ShapeDtypeStruct((2 * HEADS, HD), F32),
                   jax.ShapeDtypeStruct((2, D), F32), jax.ShapeDtypeStruct((CONV_W, D), F32),
                   jax.ShapeDtypeStruct((1, D), F32)] + [jax.ShapeDtypeStruct((4,) + a.shape[1:], a.dtype)
                                                          for a in chip_sums] + _chips_stage_shapes(chip_sums),
        in_specs=[s["xa"], s["xac"], col(L), pl.BlockSpec(memory_space=pl.ANY), s["cw"], s["cb"], s["h4"], s["h4"],
                  s["b16"], s["b16"], s["v2"]] + [HBM] * nr,
        out_specs=[col(L), col(LC), s["h4"], s["h4"], s["b16"], s["b16"], s["v2"], col(CONV_W), col(1)]
        + [HBM] * (2 * nr),
        scratch_shapes=[pltpu.VMEM((17, L, HD), F32), pltpu.VMEM((17, LC, HD), F32), pltpu.VMEM((L + 16, HD), F32)]
        + (_chips_sems(nr) if nr else []),
        input_output_aliases={3: 0},
        compiler_params=pltpu.CompilerParams(dimension_semantics=("arbitrary",), vmem_limit_bytes=VMEM_LIMIT,
                                             has_side_effects=True, collective_id=6 if nr else None),
    )(zx, zc, dyl, dz, cw, cb, wa, wx, ba, bx, lam, *[pltpu.with_memory_space_constraint(a, pltpu.HBM)
                                                       for a in chip_sums])


def _mixer_loss(x, tgt, zx, yl, gx, fg, lng, lnb, ws, bst, wout, tm):
    ncht = tm // CHUNK

    def body(x_ref, t_ref, ga_ref, u_ref, v_ref, gb_ref, yl_ref, gx_ref, fg_ref, lng_ref, lnb_ref, ws_ref,
             bst_ref, wout_ref,
             dz_ref, dyl_ref, dxn_ref, y_s, do_ref, dws_ref, dbst_ref, vec_ref,
             vn_s, mix_s, dm_s, dvn_s, dbst_s):
        step = pl.program_id(0)

        @pl.when(step == 0)
        def _():
            dws_ref[...] = jnp.zeros_like(dws_ref)
            dbst_s[...] = jnp.zeros_like(dbst_s)
            vec_ref[...] = jnp.zeros_like(vec_ref)

        bst = bst_ref[...].T

        u, v = u_ref[...], v_ref[...]
        ug, dug_du = _gelu_and_grad(u)
        vg, dvg_dv = _gelu_and_grad(v)
        mu = jnp.mean(vg, axis=-1, keepdims=True)
        vc = vg - mu
        rstd = lax.rsqrt(jnp.mean(vc * vc, axis=-1, keepdims=True) + LN_EPS)
        vhat = vc * rstd
        lngv = lng_ref[...]
        vn_s[...] = (vhat * lngv + lnb_ref[...]).astype(BF16)
        for ch in range(ncht):
            rs = slice(ch * CHUNK, (ch + 1) * CHUNK)
            for g in range(HEADS):
                cs = slice(g * HD, (g + 1) * HD)
                mix_s[rs, cs] = _dot(ws_ref[g].astype(BF16), vn_s[rs, cs]) + bst[:, g:g + 1]
        mixed = mix_s[...]
        ga, gb, yl = ga_ref[...], gb_ref[...], yl_ref[...]
        sga, dsga = _silu_and_grad(ga)
        sgb, dsgb = _silu_and_grad(gb)
        ys = ug * mixed
        y_s[:, 0:D] = (yl * sga).astype(BF16)
        y_s[:, D:D_MIX] = (ys * sgb).astype(BF16)
        o = _dot(y_s[...], wout_ref[...])
        gxv, fgv = gx_ref[...], fg_ref[...]
        xn = x_ref[...] + gxv * o
        rs2 = lax.rsqrt(jnp.mean(xn * xn, axis=-1, keepdims=True) + NORM_EPS)
        xh = xn * rs2
        diff = xh * fgv - t_ref[...]
        vec_ref[R_LOSS:R_LOSS + 1, :] += jnp.full((1, D), jnp.sum(diff * diff) * (0.5 / D), F32)
        dout = diff * (1.0 / D)
        w = dout * fgv
        dxn = rs2 * (w - xh * jnp.mean(w * xh, axis=-1, keepdims=True))
        dxn_ref[...] = dxn
        vec_ref[0:1, :] += jnp.sum(dxn * o, axis=0, keepdims=True)
        vec_ref[1:2, :] += jnp.sum(dout * xh, axis=0, keepdims=True)
        dob = (dxn * gxv).astype(BF16)
        do_ref[...] = dob
        dy = _dot_nt(dob, wout_ref[...])
        dya, dyb = dy[:, 0:D], dy[:, D:D_MIX]
        dyl_ref[...] = dya * sga
        dys = dyb * sgb
        dz_ref[:, 0:D] = jnp.zeros((tm, D), BF16)
        dz_ref[:, D:2 * D] = (dya * yl * dsga).astype(BF16)
        dz_ref[:, 2 * D:3 * D] = (dys * mixed * dug_du).astype(BF16)
        dz_ref[:, 4 * D:5 * D] = (dyb * ys * dsgb).astype(BF16)
        dm = dys * ug
        dm_s[...] = dm.astype(BF16)
        for g in range(HEADS):
            cs = slice(g * HD, (g + 1) * HD)
            dbst_s[:, g:g + 1] += sum(jnp.sum(dm[ch * CHUNK:(ch + 1) * CHUNK, cs], axis=1, keepdims=True)
                                      for ch in range(ncht))
            for ch in range(ncht):
                rs = slice(ch * CHUNK, (ch + 1) * CHUNK)
                dws_ref[g] += _dot_nt(dm_s[rs, cs], vn_s[rs, cs])
                dvn_s[rs, cs] = _dot_tn(ws_ref[g].astype(BF16), dm_s[rs, cs])
        dvn = dvn_s[...]
        vec_ref[2:3, :] += jnp.sum(dvn * vhat, axis=0, keepdims=True)
        vec_ref[3:4, :] += jnp.sum(dvn, axis=0, keepdims=True)
        dvh = dvn * lngv
        dvg = rstd * (dvh - jnp.mean(dvh, axis=-1, keepdims=True) - vhat * jnp.mean(dvh * vhat, axis=-1, keepdims=True))
        dz_ref[:, 3 * D:4 * D] = (dvg * dvg_dv).astype(BF16)

        @pl.when(step == pl.num_programs(0) - 1)
        def _():
            dbst_ref[...] = dbst_s[...].T

    tile = pl.BlockSpec((tm, D), lambda i: (i, 0))
    zcol = lambda n: pl.BlockSpec((tm, D), lambda i: (i, n))
    vec = pl.BlockSpec((1, D), lambda i: (0, 0))
    full = lambda *s: pl.BlockSpec(s, lambda i: (0,) * len(s))
    return _call(
        body, name="mixer_loss", grid=(L // tm,),
        out_shape=[jax.ShapeDtypeStruct((L, D_IN), BF16), jax.ShapeDtypeStruct((L, D), F32),
                   jax.ShapeDtypeStruct((L, D), F32), jax.ShapeDtypeStruct((L, D_MIX), BF16),
                   jax.ShapeDtypeStruct((L, D), BF16),
                   jax.ShapeDtypeStruct((HEADS, CHUNK, CHUNK), F32), jax.ShapeDtypeStruct((HEADS, CHUNK), F32),
                   jax.ShapeDtypeStruct((8, D), F32)],
        in_specs=[tile, tile, zcol(1), zcol(2), zcol(3), zcol(4), tile, pl.BlockSpec((1, D), lambda i: (0, 2)),
                  vec, vec, vec,
                  full(HEADS, CHUNK, CHUNK), full(HEADS, CHUNK),
                  pl.BlockSpec((D_MIX, D), lambda i: (0, 0), pipeline_mode=pl.Buffered(1))],
        out_specs=[pl.BlockSpec((tm, D_IN), lambda i: (i, 0)), tile, tile,
                   pl.BlockSpec((tm, D_MIX), lambda i: (i, 0)), tile,
                   full(HEADS, CHUNK, CHUNK), full(HEADS, CHUNK), full(8, D)],
        scratch_shapes=[pltpu.VMEM((tm, D), BF16), pltpu.VMEM((tm, D), F32),
                        pltpu.VMEM((tm, D), BF16), pltpu.VMEM((tm, D), F32), pltpu.VMEM((CHUNK, HEADS), F32)],
        compiler_params=_params("arbitrary"),
    )(x, tgt, zx, zx, zx, zx, yl, gx, fg, lng, lnb, ws, bst, wout)


def _grad_w(a, b, a2, b2, tk, name, bw, first, nblocks, split, barrier_id, riders=()):
    nk = a.shape[0] // tk
    m = a.shape[1]
    with_ctx = a2 is not None
    if split == "cols":
        slots, r, w = nblocks, m, bw // 2
        piece = lambda q, pc: (slice(None), slice(pc * w, (pc + 1) * w))
    else:
        slots, r, w = 4, m // 8, bw
        piece = lambda q, pc: (slice((2 * q + pc) * r, (2 * q + pc + 1) * r), slice(None))

    nr = len(riders)

    def body(*refs):
        a_ref, b_ref = refs[:2]
        a2_ref, b2_ref = refs[2:4] if with_ctx else (None, None)
        base = 4 if with_ctx else 2
        sums_ref = refs[base + nr]
        s0 = base + 3 * nr + 1
        acc, mine_v, send_v, stage_v, send_sems, recv_sems = refs[s0:s0 + 6]
        n, k = pl.program_id(0), pl.program_id(1)
        x, y, c = lax.axis_index("x"), lax.axis_index("y"), lax.axis_index("c")

        def to_sibling(s):
            return pltpu.make_async_remote_copy(src_ref=send_v.at[s], dst_ref=stage_v.at[s], send_sem=send_sems.at[s],
                                                recv_sem=recv_sems.at[s], device_id=(x, y, 1 - c),
                                                device_id_type=MESH)

        if nr:
            rider_in = refs[base:base + nr]
            own_v, got_v, psum_v = (refs[s0 + 9 + i * nr:s0 + 9 + (i + 1) * nr] for i in range(3))
            p_send, p_recv, p_local = refs[s0 + 9 + 3 * nr:s0 + 12 + 3 * nr]
            c_start, c_forward, c_finish = _chips_ops(psum_v, refs[base + nr + 1:base + 2 * nr + 1],
                                                      refs[base + 2 * nr + 1:base + 3 * nr + 1], *refs[s0 + 6:s0 + 9])

            @pl.when(jnp.logical_and(n == 0, k == 0))
            def _():
                _barrier([(x, y, 1 - c)] + [(x ^ (j >> 1), y ^ (j & 1), c) for j in (1, 2)])
                copies = []
                for j in range(nr):
                    for q in range(4):
                        copies.append(pltpu.make_async_remote_copy(
                            src_ref=rider_in[j].at[2 * q + 1 - c], dst_ref=got_v[j].at[q], send_sem=p_send.at[j, q],
                            recv_sem=p_recv.at[j, q], device_id=(x, y, 1 - c), device_id_type=MESH))
                        copies.append(pltpu.make_async_copy(rider_in[j].at[2 * q + c], own_v[j].at[q],
                                                            p_local.at[j, q]))
                for cp in copies:
                    cp.start()
                for cp in copies:
                    cp.wait()
                for j in range(nr):
                    psum_v[j][...] = (own_v[j][...] + got_v[j][...]).astype(BF16)
                c_start()
        else:
            pl.when(jnp.logical_and(n == 0, k == 0))(_sibling_barrier)

        @pl.when(k == 0)
        def _():
            acc[...] = _dot_tn(a_ref[...], b_ref[...])

        if nk > 1:
            @pl.when(k > 0)
            def _():
                acc[...] += _dot_tn(a_ref[...], b_ref[...])

        if with_ctx:
            @pl.when(jnp.logical_and(k == nk - 1, n == 0))
            def _():
                acc[:, 0:b2_ref.shape[1]] += _dot_tn(a2_ref[...], b2_ref[...])

        if nr:
            pl.when(jnp.logical_and(k == nk - 1, n == nblocks - 1))(c_forward)

        def hand_over(s, q):
            for pc in (0, 1):
                @pl.when(c == pc)
                def _(pc=pc):
                    mine_v[s] = acc[piece(q, pc)]
                    send_v[s] = acc[piece(q, 1 - pc)].astype(BF16)
            to_sibling(s).start()

        for i in range(nblocks):
            @pl.when(jnp.logical_and(k == nk - 1, n == i))
            def _(i=i):
                if split == "cols":
                    hand_over(i, 0)
                else:
                    for q in range(4):
                        hand_over(q, q)

        @pl.when(jnp.logical_and(k == nk - 1, n == nblocks - 1))
        def _():
            for s in range(slots):
                to_sibling(s).wait_recv()
                sums_ref[s] = (mine_v[s] + stage_v[s].astype(F32)).astype(BF16)
            for s in range(slots):
                to_sibling(s).wait_send()
            if nr:
                c_finish()

    in_specs = [pl.BlockSpec((tk, m), lambda n, k: (k, 0)), pl.BlockSpec((tk, bw), lambda n, k: (k, n + first))]
    args = [a, b]
    if with_ctx:
        in_specs += [pl.BlockSpec(a2.shape, lambda n, k: (0, 0)), pl.BlockSpec(b2.shape, lambda n, k: (0, 0))]
        args += [a2, b2]
    in_specs += [HBM] * nr
    args += [pltpu.with_memory_space_constraint(s, pltpu.HBM) for s in riders]
    rider_sums = [jax.ShapeDtypeStruct((4,) + s.shape[1:], BF16) for s in riders]
    rider_scratch = []
    if nr:
        rider_scratch = (_chips_sems(nr) + [pltpu.VMEM(s.shape, F32) for s in rider_sums] * 2
                         + [pltpu.VMEM(s.shape, BF16) for s in rider_sums]
                         + [pltpu.SemaphoreType.DMA((nr, 4))] * 3)
    return _call(
        body, name=name, grid=(nblocks, nk),
        out_shape=[jax.ShapeDtypeStruct((slots, r, w), BF16)] + rider_sums + _chips_stage_shapes(rider_sums),
        in_specs=in_specs, out_specs=[pl.BlockSpec((slots, r, w), lambda n, k: (0, 0, 0))] + [HBM] * (2 * nr),
        scratch_shapes=[pltpu.VMEM((m, bw), F32), pltpu.VMEM((slots, r, w), F32), pltpu.VMEM((slots, r, w), BF16),
                        pltpu.VMEM((slots, r, w), BF16), pltpu.SemaphoreType.DMA((slots,)),
                        pltpu.SemaphoreType.DMA((slots,))] + rider_scratch,
        compiler_params=pltpu.CompilerParams(dimension_semantics=("arbitrary", "arbitrary"),
                                             vmem_limit_bytes=VMEM_LIMIT, has_side_effects=True,
                                             collective_id=barrier_id),
    )(*args)


def _grad_rows(xr, dz, w, mod, ng, dres, ncols, tm, name, chip_sums=(), first_chips=None, dests=None):
    rows = xr.shape[0]
    steps = rows // tm
    with_dx = dres is not None
    nr = len(chip_sums)
    dests = [d for d in (dests or [None] * nr)]
    nd = sum(d is not None for d in dests)
    nin = 6 if with_dx else 5
    nout = 2 if with_dx else 1

    def body(*refs):
        if with_dx:
            x_ref, dz_ref, w_ref, sc_ref, ng_ref, dres_ref = refs[:nin]
            dx_ref, vec_ref = refs[nin + nr + nd:nin + nr + nd + nout]
        else:
            x_ref, dz_ref, w_ref, sc_ref, ng_ref = refs[:nin]
            (vec_ref,) = refs[nin + nr + nd:nin + nr + nd + nout]
        if nr:
            o0 = nin + nr + nd + nout
            start, forward, finish = _chips_ops(refs[nin:nin + nr], refs[o0:o0 + nr], refs[o0 + nr:o0 + 2 * nr],
                                                *refs[o0 + 2 * nr:o0 + 2 * nr + 3], first_chips=first_chips,
                                                barrier=True)
            w_hbm, w_ref, w_sem = w_ref, refs[o0 + 2 * nr + 3], refs[o0 + 2 * nr + 4]

            @pl.when(pl.program_id(0) == 0)
            def _():
                start()
                w_load = pltpu.make_async_copy(w_hbm, w_ref, w_sem)
                w_load.start()
                w_load.wait()

            pl.when(pl.program_id(0) == steps // 2)(forward)
            pl.when(pl.program_id(0) == steps - 1)(finish)

        @pl.when(pl.program_id(0) == 0)
        def _():
            vec_ref[...] = jnp.zeros_like(vec_ref)

        dhn = _dot_nt(dz_ref[...], w_ref[...])
        x = x_ref[...]
        rs = lax.rsqrt(jnp.mean(x * x, axis=-1, keepdims=True) + NORM_EPS)
        xh = x * rs
        ngv = ng_ref[...]
        y = xh * ngv
        vec_ref[0:1, :] += jnp.sum(dhn, axis=0, keepdims=True)
        vec_ref[1:2, :] += jnp.sum(dhn * y, axis=0, keepdims=True)
        dy = dhn * (1.0 + sc_ref[...])
        vec_ref[2:3, :] += jnp.sum(dy * xh, axis=0, keepdims=True)
        if with_dx:
            dxh = dy * ngv
            dx_ref[...] = dres_ref[...] + rs * (dxh - xh * jnp.mean(dxh * xh, axis=-1, keepdims=True))

    tile = pl.BlockSpec((tm, D), lambda i: (i, 0))
    vec = pl.BlockSpec((1, D), lambda i: (0, 0))
    w_spec = HBM if nr else pl.BlockSpec((D, ncols), lambda i: (0, 0), pipeline_mode=pl.Buffered(1))
    in_specs = [tile, pl.BlockSpec((tm, ncols), lambda i: (i, 0)), w_spec, pl.BlockSpec((1, D), lambda i: (0, 1)), vec]
    out_shape = [jax.ShapeDtypeStruct((8, D), F32)]
    out_specs = [pl.BlockSpec((8, D), lambda i: (0, 0))]
    args = [xr, dz, pltpu.with_memory_space_constraint(w, pltpu.HBM) if nr else w, mod, ng]
    if with_dx:
        in_specs.append(tile)
        out_shape.insert(0, jax.ShapeDtypeStruct((rows, D), F32))
        out_specs.insert(0, tile)
        args.append(dres)
    aliases = {}
    for j, d in enumerate(dests):
        if d is not None:
            aliases[len(args) + nr + len(aliases)] = len(out_shape) + j
    in_specs += [HBM] * (nr + nd)
    out_specs += [HBM] * (2 * nr)
    out_shape += [jax.ShapeDtypeStruct((4,) + a.shape[1:], a.dtype) for a in chip_sums]
    out_shape += _chips_stage_shapes(chip_sums)
    args += [pltpu.with_memory_space_constraint(a, pltpu.HBM) for a in chip_sums]
    args += [pltpu.with_memory_space_constraint(d, pltpu.HBM) for d in dests if d is not None]
    return _call(body, name=name, grid=(steps,), out_shape=out_shape, in_specs=in_specs, out_specs=out_specs,
                 scratch_shapes=(_chips_sems(nr) + [pltpu.VMEM((D, ncols), BF16), pltpu.SemaphoreType.DMA(())])
                 if nr else [], input_output_aliases=aliases,
                 compiler_params=pltpu.CompilerParams(dimension_semantics=("arbitrary",),
                                                      vmem_limit_bytes=VMEM_LIMIT, has_side_effects=bool(nr),
                                                      collective_id=7 if nr else None))(*args)


def _adamw(w, g, m, v):
    m = ADAM_B1 * m + (1.0 - ADAM_B1) * g
    v = ADAM_B2 * v + (1.0 - ADAM_B2) * (g * g)
    m_hat = m / (1.0 - ADAM_B1 ** ADAM_STEP)
    v_hat = v / (1.0 - ADAM_B2 ** ADAM_STEP)
    delta = -ADAM_LR * (m_hat / (jnp.sqrt(v_hat) + ADAM_EPS) + ADAM_WD * w)
    return delta, m, v


def _adamw_reduced(parts, w, m, v, tr, name):
    r, n = w.shape
    nparts = parts.shape[0]

    def body(p_ref, w_ref, m_ref, v_ref, g_ref, d_ref, mo_ref, vo_ref):
        g = p_ref[0].astype(F32)
        for i in range(1, nparts):
            g = g + p_ref[i].astype(F32)
        g_ref[...] = g
        d_ref[...], mo_ref[...], vo_ref[...] = _adamw(w_ref[...], g, m_ref[...], v_ref[...])

    tile = pl.BlockSpec((tr, n), lambda i: (i, 0))
    sds = jax.ShapeDtypeStruct((r, n), F32)
    return _call(
        body, name=name, grid=(r // tr,), out_shape=[sds] * 4,
        in_specs=[pl.BlockSpec((nparts, tr, n), lambda i: (0, i, 0)), tile, tile, tile], out_specs=[tile] * 4,
        compiler_params=_params("arbitrary"),
    )(parts, w, m, v)


R_GATE, R_FINAL_G, R_LN_G, R_LN_B, R_LOSS = 0, 1, 2, 3, 4
R_SH_X, R_SC_X, R_NG_X = 5, 6, 7
R_SH_C, R_SC_C, R_NG_C = 8, 9, 10
R_LAM, R_CW, R_CB = 11, 13, 17
PACK_ROWS = 24
Q_BA, Q_BX, Q_SGU_B, PACK128_ROWS = 0, 16, 32, 40


def _reduce_small(vec_pieces, q_pieces, mat_parts, ada_w, me):
    nloc = ada_w.shape[1]
    nm = len(mat_parts)
    pieces = list(vec_pieces) + list(q_pieces)

    def body(me_ref, *refs):
        piece_refs, refs = refs[:len(pieces)], refs[len(pieces):]
        mp_refs, w_ref = refs[:nm], refs[nm]
        red_ref, redq_ref = refs[nm + 1:nm + 3]
        mats_all = refs[nm + 3:2 * nm + 3]
        cparts_ref, dmod_ref, gab_ref, loss_ref = refs[2 * nm + 3:2 * nm + 7]
        pack_ref, packq_ref, vp_ref, vq_ref = refs[2 * nm + 7:2 * nm + 11]
        mat_refs = refs[2 * nm + 11:3 * nm + 11]
        cpart_ref, dmc_s = refs[3 * nm + 11:3 * nm + 13]
        sems = refs[3 * nm + 13:]
        for dst, group in ((pack_ref, vec_pieces), (packq_ref, q_pieces)):
            row = 0
            for _, nrows in group:
                dst[row:row + nrows, :] = piece_refs[0][0:nrows, :]
                piece_refs, row = piece_refs[1:], row + nrows
            if row < dst.shape[0]:
                dst[row:, :] = jnp.zeros((dst.shape[0] - row, dst.shape[1]), F32)
        p_start, p_forward, p_finish = _gather2_ops([pack_ref, packq_ref], [vp_ref, vq_ref], ["ag", "ag"], *sems[:3],
                                                    barrier=True)
        m_start, m_forward, m_finish = _gather2_ops(mat_refs, mats_all, ["ag"] * nm, *sems[3:6])
        c_start, c_forward, c_finish = _gather2_ops([cpart_ref], [cparts_ref], ["ag"], *sems[6:])
        p_start()
        for mp_ref, mat_ref in zip(mp_refs, mat_refs):
            mat = mp_ref[0].astype(F32)
            for i in range(1, mp_ref.shape[0]):
                mat = mat + mp_ref[i].astype(F32)
            mat_ref[...] = mat
        m_start()
        p_forward()
        p_finish()
        red, redq = vp_ref[0], vq_ref[0]
        for i in range(1, N_DEV):
            red = red + vp_ref[i]
            redq = redq + vq_ref[i]
        red_ref[...] = red
        redq_ref[...] = redq
        loss_ref[...] = red_ref[R_LOSS:R_LOSS + 1, 0:1]
        for e in range(N_DEV):
            dmod_ref[e:e + 1, 0:D] = vp_ref[e, R_SH_X:R_SH_X + 1, :]
            dmod_ref[e:e + 1, D:2 * D] = vp_ref[e, R_SC_X:R_SC_X + 1, :]
            dmod_ref[e:e + 1, 2 * D:3 * D] = vp_ref[e, R_GATE:R_GATE + 1, :]
        dmod_ref[8:9, 0:D] = red[R_SH_C:R_SH_C + 1, :]
        dmod_ref[8:9, D:2 * D] = red[R_SC_C:R_SC_C + 1, :]
        dmod_ref[8:9, 2 * D:3 * D] = jnp.zeros((1, D), F32)
        dmod_ref[9:16, :] = jnp.zeros((7, 3 * D), F32)
        gab_ref[:, 0:D] = red[R_SH_X:R_SH_X + 1, :] + red[R_SH_C:R_SH_C + 1, :]
        gab_ref[:, D:2 * D] = red[R_SC_X:R_SC_X + 1, :] + red[R_SC_C:R_SC_C + 1, :]
        gab_ref[:, 2 * D:3 * D] = red[R_GATE:R_GATE + 1, :]
        dmc_s[...] = jnp.broadcast_to(dmod_ref[8:9, :], (8, 3 * D))
        off = pl.multiple_of(me_ref[0] * nloc, 128)
        cpart_ref[...] = _dot_nt(dmc_s[:, pl.ds(off, nloc)], w_ref[...])
        c_start()
        m_forward()
        c_forward()
        c_finish()
        m_finish()

    return _call(
        body, name="reduce_small",
        out_shape=[jax.ShapeDtypeStruct((PACK_ROWS, D), F32), jax.ShapeDtypeStruct((PACK128_ROWS, HD), F32)]
        + [jax.ShapeDtypeStruct((N_DEV,) + p.shape[1:], F32) for p in mat_parts]
        + [jax.ShapeDtypeStruct((N_DEV, 8, D), F32), jax.ShapeDtypeStruct((16, 3 * D), F32),
           jax.ShapeDtypeStruct((1, 3 * D), F32), jax.ShapeDtypeStruct((1, 1), F32)],
        in_specs=[pl.BlockSpec(memory_space=pltpu.SMEM)] + [VMEM] * (len(pieces) + nm + 1),
        out_specs=[VMEM] * (nm + 6),
        scratch_shapes=[pltpu.VMEM((PACK_ROWS, D), F32), pltpu.VMEM((PACK128_ROWS, HD), F32),
                        pltpu.VMEM((N_DEV, PACK_ROWS, D), F32), pltpu.VMEM((N_DEV, PACK128_ROWS, HD), F32)]
        + [pltpu.VMEM(p.shape[1:], F32) for p in mat_parts]
        + [pltpu.VMEM((8, D), F32), pltpu.VMEM((8, 3 * D), F32)] + _gather2_sems(2) + _gather2_sems(nm)
        + _gather2_sems(1),
        compiler_params=pltpu.CompilerParams(vmem_limit_bytes=VMEM_LIMIT, has_side_effects=True, collective_id=8),
    )(me, *[a for a, _ in pieces], *mat_parts, ada_w)


def _adamw_ada(c_all, c_ctx, dmod, w, m, v, me):
    nloc = w.shape[1]

    def body(me_ref, c_ref, cc_ref, dm_ref, w_ref, m_ref, v_ref, g_ref, d_ref, mo_ref, vo_ref):
        off = pl.multiple_of(me_ref[0] * nloc, 128)
        dm = dm_ref[:, pl.ds(off, nloc)]
        sx, _ = _silu_and_grad(c_ref[...])
        sc, _ = _silu_and_grad(cc_ref[...])
        g = _dot_tn(sx, dm[0:8, :]) + _dot_tn(jnp.broadcast_to(sc, (8, D)), dm[8:16, :])
        g_ref[...] = g
        d_ref[...], mo_ref[...], vo_ref[...] = _adamw(w_ref[...], g, m_ref[...], v_ref[...])

    sds = jax.ShapeDtypeStruct(w.shape, F32)
    return _call(
        body, name="adamw_ada_w", out_shape=[sds] * 4,
        in_specs=[pl.BlockSpec(memory_space=pltpu.SMEM)] + [VMEM] * 6, out_specs=[VMEM] * 4,
        compiler_params=_params(),
    )(me, c_all, c_ctx, dmod, w, m, v)


_SMALL = ("c_ctx", "ada_b", "norm_g", "conv_w", "conv_b", "lru_wa", "lru_ba", "lru_wx", "lru_bx", "lru_lambda",
          "sgu_ln_g", "sgu_ln_b", "sgu_w", "sgu_b", "final_g")


def _adamw_small(red, redq, mats, cparts, gab, ws, ms, vs, me):
    n = len(_SMALL)

    def body(me_ref, red_ref, redq_ref, wa_ref, wx_ref, sw_ref, cp_ref, gab_ref, *refs):
        w_refs, m_refs, v_refs = refs[:n], refs[n:2 * n], refs[2 * n:3 * n]
        outs = refs[3 * n:]
        off = pl.multiple_of(me_ref[0] * HD, 128)

        def row(r, k=1):
            return red_ref[r:r + k, :]

        cc = w_refs[0][...]
        dcc = cp_ref[0, 0:1, :]
        for i in range(1, N_DEV):
            dcc = dcc + cp_ref[i, 0:1, :]
        grads = dict(
            c_ctx=dcc * _silu_and_grad(cc)[1], ada_b=gab_ref[...], norm_g=row(R_NG_X) + row(R_NG_C),
            conv_w=red_ref[R_CW:R_CW + CONV_W, pl.ds(off, HD)], conv_b=row(R_CB),
            lru_wa=wa_ref[...], lru_ba=redq_ref[Q_BA:Q_BA + 2 * HEADS, :], lru_wx=wx_ref[...],
            lru_bx=redq_ref[Q_BX:Q_BX + 2 * HEADS, :], lru_lambda=red_ref[R_LAM:R_LAM + 2, pl.ds(off, HD)],
            sgu_ln_g=row(R_LN_G), sgu_ln_b=row(R_LN_B), sgu_w=sw_ref[...],
            sgu_b=redq_ref[Q_SGU_B:Q_SGU_B + HEADS, :], final_g=row(R_FINAL_G))
        for j, name in enumerate(_SMALL):
            g = grads[name]
            outs[j][...] = g
            outs[n + j][...], outs[2 * n + j][...], outs[3 * n + j][...] = _adamw(w_refs[j][...], g, m_refs[j][...],
                                                                                 v_refs[j][...])

    sds = [jax.ShapeDtypeStruct(ws[k].shape, F32) for k in _SMALL]
    outs = _call(
        body, name="adamw_small", out_shape=sds * 4,
        in_specs=[pl.BlockSpec(memory_space=pltpu.SMEM)] + [VMEM] * (7 + 3 * n), out_specs=[VMEM] * (4 * n),
        compiler_params=_params(),
    )(me, red, redq, *mats, cparts, gab, *[ws[k] for k in _SMALL], *[ms[k] for k in _SMALL],
      *[vs[k] for k in _SMALL])
    return [dict(zip(_SMALL, outs[i * n:(i + 1) * n])) for i in range(4)]


def kernel(x, c, ctx, c_ctx, ada_w, ada_b, norm_g, w_in, conv_w, conv_b, lru_wa, lru_ba, lru_wx, lru_bx, lru_lambda, sgu_ln_g, sgu_ln_b, sgu_w, sgu_b, w_out, final_g, loss_target, m_c_ctx, m_ada_w, m_ada_b, m_norm_g, m_w_in, m_conv_w, m_conv_b, m_lru_wa, m_lru_ba, m_lru_wx, m_lru_bx, m_lru_lambda, m_sgu_ln_g, m_sgu_ln_b, m_sgu_w, m_sgu_b, m_w_out, m_final_g, v_c_ctx, v_ada_w, v_ada_b, v_norm_g, v_w_in, v_conv_w, v_conv_b, v_lru_wa, v_lru_ba, v_lru_wx, v_lru_bx, v_lru_lambda, v_sgu_ln_g, v_sgu_ln_b, v_sgu_w, v_sgu_b, v_w_out, v_final_g):
    args = dict(locals())
    me = (4 * lax.axis_index("x") + 2 * lax.axis_index("y") + lax.axis_index("c")).astype(jnp.int32).reshape(1)
    xr, ctxr, tgt = x[0], ctx[0], loss_target[0]
    cc = c_ctx.reshape(1, D)
    nw = 2 * HEADS * HD
    view = dict(c_ctx=(1, D), ada_b=(1, 3 * D), norm_g=(1, D), conv_w=(CONV_W, HD), conv_b=(1, D), lru_wa=(nw, HD),
                lru_ba=(2 * HEADS, HD), lru_wx=(nw, HD), lru_bx=(2 * HEADS, HD), lru_lambda=(2, HD), sgu_ln_g=(1, D),
                sgu_ln_b=(1, D), sgu_w=(HEADS * CHUNK, CHUNK), sgu_b=(HEADS, CHUNK), final_g=(1, D))

    zx, hn, w_full, w_out_b, modx, modc, c_all, cw_full, lam_full = _front_project(
        xr, c, cc, ada_w[0], ada_b, norm_g, w_in[0], w_out[0], conv_w[0], lru_lambda[0], me)
    zc, hnc = _project(ctxr, modc, norm_g, w_full, D, LC, "project_ctx")
    ba, bx = lru_ba.reshape(view["lru_ba"]), lru_bx.reshape(view["lru_bx"])
    yl, wout_all = _lru_forward(zx, zc, cw_full, conv_b, lru_wa[0], lru_wx[0], ba, bx, lam_full, [w_out_b], ["ag"])
    wout_full = wout_all.reshape(D_MIX, D)
    dz, dyl, dxn, ycat, dob, dws, dbst, mvec = _mixer_loss(
        xr, tgt, zx, yl, modx, final_g.reshape(1, D), sgu_ln_g, sgu_ln_b, sgu_w[0], sgu_b[0], wout_full, ROWS)

    (wout_sums,) = _grad_w(ycat, dob, None, None, L, "grad_w_out", D, 0, 1, "rows", 1)
    (rest_sums,) = _grad_w(hn, dz, None, None, L, "grad_w_in_rest", 2 * W_IN_SHARD, 1, 3, "cols", 2)
    dz, dxac, dwa, dwx, dba, dbx, dlam, dcw, dcb, win_parts, wout_parts, _, _ = _lru_backward(
        zx, zc, dyl, dz, cw_full, conv_b, lru_wa[0], lru_wx[0], ba, bx, lam_full, [rest_sums, wout_sums],
        first_chips=[1, 0])
    mats = [dwa.reshape(N_DEV, nw // N_DEV, HD), dwx.reshape(N_DEV, nw // N_DEV, HD), dws]
    first_sums, *mat_parts = _grad_w(hn, dz, hnc, dxac, L, "grad_w_in_first", 2 * W_IN_SHARD, 0, 1, "cols", 3,
                                     riders=mats)[:4]
    gx, xvec, win_parts = _grad_rows(
        xr, dz, w_full, modx, norm_g, dxn, D_IN, ROWS, "grad_rows_x", chip_sums=[first_sums], first_chips=[0],
        dests=[win_parts])[:3]
    (cvec,) = _grad_rows(ctxr, dxac, w_full, modc, norm_g, None, D, LC, "grad_rows_ctx")
    red, redq, *rest = _reduce_small(
        [(mvec, 5), (xvec, 3), (cvec, 3), (dlam, 2), (dcw, CONV_W), (dcb, 1)],
        [(dba, 2 * HEADS), (dbx, 2 * HEADS), (dbst, HEADS)], mat_parts, ada_w[0], me)
    mats_all, (cparts, dmod, gab, loss) = rest[:3], rest[3:]

    g_w_in, d_w_in, nm_w_in, nv_w_in = _adamw_reduced(win_parts, w_in[0], m_w_in[0], v_w_in[0], 2 * ROWS, "adamw_w_in")
    g_w_out, d_w_out, nm_w_out, nv_w_out = _adamw_reduced(wout_parts, w_out[0], m_w_out[0], v_w_out[0], ROWS // 2,
                                                          "adamw_w_out")
    g_ada, d_ada, nm_ada, nv_ada = _adamw_ada(c_all, cc, dmod, ada_w[0], m_ada_w[0], v_ada_w[0], me)
    ws = {k: args[k].reshape(view[k]) for k in _SMALL}
    ms = {k: args["m_" + k].reshape(view[k]) for k in _SMALL}
    vs = {k: args["v_" + k].reshape(view[k]) for k in _SMALL}
    small = _adamw_small(red, redq, [m.reshape(-1, HD) for m in mats_all], cparts, gab, ws, ms, vs, me)
    big = dict(w_in=(g_w_in, d_w_in, nm_w_in, nv_w_in), w_out=(g_w_out, d_w_out, nm_w_out, nv_w_out),
               ada_w=(g_ada, d_ada, nm_ada, nv_ada))

    loss = loss.reshape(())
    names = ("c_ctx", "ada_w", "ada_b", "norm_g", "w_in", "conv_w", "conv_b", "lru_wa", "lru_ba", "lru_wx", "lru_bx",
             "lru_lambda", "sgu_ln_g", "sgu_ln_b", "sgu_w", "sgu_b", "w_out", "final_g")
    outs = [loss, gx.reshape(x.shape)]
    for kind in range(4):
        for k in names:
            val = big[k][kind] if k in big else small[kind][k]
            outs.append(val.reshape(args[k].shape))
    return tuple(outs)
```

```python
import jax
import jax.numpy as jnp
from jax import lax
from jax.experimental import pallas as pl
from jax.experimental.pallas import tpu as pltpu

F32 = jnp.float32
BF16 = jnp.bfloat16

N_DEV = 8
D = 1024
L = 2048
LC = 256
HEADS = 8
HD = 128
CHUNK = 128
D_IN = 5 * D
W_IN_SHARD = D_IN // N_DEV
ROWS = 256
D_MIX = 2 * D
CONV_W = 4
LRU_C = 8.0
NORM_EPS = 1e-6
LN_EPS = 1e-5
ADAM_LR, ADAM_B1, ADAM_B2, ADAM_EPS, ADAM_WD, ADAM_STEP = 0.001, 0.9, 0.999, 1e-08, 0.01, 10

VMEM_LIMIT = 56 * 1024 * 1024

HBM = pl.BlockSpec(memory_space=pltpu.HBM)
VMEM = pl.BlockSpec(memory_space=pltpu.VMEM)
MESH = pl.DeviceIdType.MESH


def _call(body, **kw):
    return pl.pallas_call(body, **kw)


def _params(*sem):
    return pltpu.CompilerParams(dimension_semantics=sem, vmem_limit_bytes=VMEM_LIMIT)


def _sigmoid(x):
    return 0.5 * jnp.tanh(0.5 * x) + 0.5


def _silu_and_grad(x):
    s = _sigmoid(x)
    return x * s, s * (1.0 + x * (1.0 - s))


_G0 = 0.7978845608028654
_G1 = 0.044715


def _gelu_and_grad(x):
    x2 = x * x
    t = jnp.tanh(_G0 * (x + _G1 * x * x2))
    cdf = 0.5 * (1.0 + t)
    return x * cdf, cdf + 0.5 * x * (1.0 - t * t) * (_G0 * (1.0 + 3.0 * _G1 * x2))


def _softplus(z):
    t = jnp.exp(-jnp.abs(z))
    u = 1.0 + t
    log1p = jnp.where(u == 1.0, t, jnp.log(u) * t / jnp.where(u == 1.0, 1.0, u - 1.0))
    return jnp.maximum(z, 0.0) + log1p


def _dot(a, b):
    return jnp.dot(a, b, preferred_element_type=F32)


def _dot_nt(a, b):
    return lax.dot_general(a, b, (((1,), (1,)), ((), ())), preferred_element_type=F32)


def _dot_tn(a, b):
    return lax.dot_general(a, b, (((0,), (0,)), ((), ())), preferred_element_type=F32)


def _rows(shape):
    return lax.broadcasted_iota(jnp.int32, shape, 0)


def _gather2_shapes(arrays, modes):
    return [jax.ShapeDtypeStruct((N_DEV,) + a.shape if m == "ag" else (a.shape[0], N_DEV * a.shape[1]), a.dtype)
            for a, m in zip(arrays, modes)]


def _gather2_sems(n):
    return [pltpu.SemaphoreType.DMA((n, N_DEV - 1)), pltpu.SemaphoreType.DMA((n, N_DEV - 1)),
            pltpu.SemaphoreType.DMA((n,))]


def _barrier(peers):
    sem = pltpu.get_barrier_semaphore()
    for peer in peers:
        pl.semaphore_signal(sem, inc=1, device_id=peer, device_id_type=MESH)
    pl.semaphore_wait(sem, len(peers))


def _gather2_ops(ins, outs, modes, send_sems, recv_sems, local_sems, barrier=False):
    n = len(ins)
    x, y, c = lax.axis_index("x"), lax.axis_index("y"), lax.axis_index("c")
    me, sibling = (x, y, c), (x, y, 1 - c)
    chips = [(x ^ (k >> 1), y ^ (k & 1)) for k in (1, 2, 3)]

    def slot(j, px, py, pc):
        dev = 4 * px + 2 * py + pc
        if modes[j] == "agc":
            w = ins[j].shape[1]
            return outs[j].at[:, pl.ds(pl.multiple_of(dev * w, 128), w)]
        return outs[j].at[dev]

    def copy(j, k, block, to, src=None):
        return pltpu.make_async_remote_copy(
            src_ref=slot(j, *block) if src is None else src, dst_ref=slot(j, *block),
            send_sem=send_sems.at[j, k], recv_sem=recv_sems.at[j, k], device_id=to, device_id_type=MESH)

    def own(j):
        return pltpu.make_async_copy(ins[j], slot(j, *me), local_sems.at[j])

    def first(j):
        return [copy(j, 0, me, sibling, src=ins[j])] + [copy(j, 1 + i, me, (*chip, c), src=ins[j])
                                                        for i, chip in enumerate(chips)]

    def passed(j, i):
        return copy(j, 4 + i, (*chips[i], c), sibling)

    def start():
        if barrier:
            _barrier([sibling] + [(*chip, c) for chip in chips])
        for j in range(n):
            own(j).start()
            for cp in first(j):
                cp.start()

    def forward():
        for i, chip in enumerate(chips):
            for j in range(n):
                copy(j, 1 + i, (*chip, c), me).wait_recv()
                passed(j, i).start()

    def finish():
        for j in range(n):
            copy(j, 0, sibling, me).wait_recv()
            for i, chip in enumerate(chips):
                copy(j, 4 + i, (*chip, 1 - c), me).wait_recv()
            for cp in first(j) + [passed(j, i) for i in range(3)]:
                cp.wait_send()
            own(j).wait()

    return start, forward, finish


def _sibling_barrier():
    sem = pltpu.get_barrier_semaphore()
    sibling = (lax.axis_index("x"), lax.axis_index("y"), 1 - lax.axis_index("c"))
    pl.semaphore_signal(sem, inc=1, device_id=sibling, device_id_type=MESH)
    pl.semaphore_wait(sem, 1)


def _chips_sems(n):
    return [pltpu.SemaphoreType.DMA((n, 6)), pltpu.SemaphoreType.DMA((n, 6)), pltpu.SemaphoreType.DMA((n,))]


def _chips_stage_shapes(chip_sums):
    return [jax.ShapeDtypeStruct((2, a.shape[1] // 2, a.shape[2]), a.dtype) for a in chip_sums]


def _chips_ops(ins, outs, stages, send_sems, recv_sems, local_sems, first_chips=None, barrier=False):
    x, y, c = lax.axis_index("x"), lax.axis_index("y"), lax.axis_index("c")
    qm = 2 * x + y
    first_chips = first_chips or [0] * len(ins)

    def owns(j, chip):
        lo, cnt = first_chips[j], ins[j].shape[0]
        if lo == 0 and cnt == 4:
            return None
        return jnp.logical_and(chip >= lo, chip < lo + cnt)

    def guarded(cond, fn):
        if cond is None:
            fn()
        else:
            pl.when(cond)(fn)

    def slot(j, chip):
        return jnp.clip(chip - first_chips[j], 0, ins[j].shape[0] - 1)

    def half(j, i):
        h = ins[j].shape[1] // 2
        return pl.ds(i * h, h)

    def copy(j, sem, src, dst, k):
        return pltpu.make_async_remote_copy(
            src_ref=src, dst_ref=dst, send_sem=send_sems.at[j, sem], recv_sem=recv_sems.at[j, sem],
            device_id=(x ^ (k >> 1), y ^ (k & 1), c), device_id_type=MESH)

    def direct(j, k):
        return copy(j, k - 1, ins[j].at[slot(j, qm ^ k)], outs[j].at[qm], k)

    def first_hop(j, k):
        return copy(j, 1 + k, ins[j].at[slot(j, qm ^ 3), half(j, k - 1)], stages[j].at[k - 1], k)

    def second_hop(j, k):
        return copy(j, 3 + k, stages[j].at[2 - k], outs[j].at[qm ^ (3 - k), half(j, 2 - k)], k)

    def local(j):
        return pltpu.make_async_copy(ins[j].at[slot(j, qm)], outs[j].at[qm], local_sems.at[j])

    def start():
        if barrier:
            _barrier([(x ^ (k >> 1), y ^ (k & 1), c) for k in (1, 2)])
        for j in range(len(ins)):
            for k in (1, 2):
                guarded(owns(j, qm ^ 3), lambda j=j, k=k: first_hop(j, k).start())
        for j in range(len(ins)):
            for k in (1, 2):
                guarded(owns(j, qm ^ k), lambda j=j, k=k: direct(j, k).start())
            guarded(owns(j, qm), lambda j=j: local(j).start())

    def forward():
        for j in range(len(ins)):
            for k in (1, 2):
                def pass_on(j=j, k=k):
                    first_hop(j, 3 - k).wait_recv()
                    second_hop(j, k).start()
                guarded(owns(j, qm ^ k), pass_on)

    def finish():
        for j in range(len(ins)):
            for k in (1, 2):
                guarded(owns(j, qm ^ k), lambda j=j, k=k: direct(j, k).wait_send())
                guarded(owns(j, qm ^ k), lambda j=j, k=k: second_hop(j, k).wait_send())
                guarded(owns(j, qm ^ 3), lambda j=j, k=k: first_hop(j, k).wait_send())
                guarded(owns(j, qm), lambda j=j, k=k: direct(j, k).wait_recv())
                guarded(owns(j, qm), lambda j=j, k=k: second_hop(j, k).wait_recv())
            guarded(owns(j, qm), lambda j=j: local(j).wait())

    return start, forward, finish


ARRIVAL = (0, 1, 2, 4, 3, 5, 6, 7)


def _front_project(xr, c, c_ctx, ada_w, ada_b, ng, w_in, w_out, cw, lam, me):
    nloc = ada_w.shape[1]
    ws = W_IN_SHARD
    arrival = jnp.asarray(ARRIVAL, jnp.int32)

    def body(me_ref, arr_ref, x_ref, c_ref, cc_ref, aw_ref, ab_ref, ng_ref, win_ref, wout_ref, cw_ref, lam_ref,
             z_ref, hn_ref, wfull_ref, woutb_ref, modx_ref, modc_ref, call_ref, cwf_ref, lamf_ref,
             wv, call_s, part_s, parts_s, w_send, w_recv, hbm_sems, s_send, s_recv, g_send, g_recv, g_local,
             x_v, x_sem):
        t = pl.program_id(0)
        x_load = pltpu.make_async_copy(x_ref, x_v, x_sem)
        x, y, cidx = lax.axis_index("x"), lax.axis_index("y"), lax.axis_index("c")
        me_i = me_ref[0]
        sibling = (x, y, 1 - cidx)
        chips = [(x ^ (k >> 1), y ^ (k & 1)) for k in (1, 2, 3)]
        g_start, g_pass, g_finish = _gather2_ops([cw_ref, lam_ref], [cwf_ref, lamf_ref], ["agc", "agc"],
                                                 g_send, g_recv, g_local)

        def shard_copy(k, px, py, pc, to, half=None):
            slot = wv.at[4 * px + 2 * py + pc]
            if half is not None:
                slot = slot.at[pl.ds(half * (D // 2), D // 2), :]
            return pltpu.make_async_remote_copy(src_ref=slot, dst_ref=slot, send_sem=w_send.at[k],
                                                recv_sem=w_recv.at[k], device_id=to, device_id_type=MESH)

        def small_gather(src, my_slot, stage):
            copies = []
            for k in range(1, N_DEV):
                peer = (x ^ (k >> 2), y ^ ((k >> 1) & 1), cidx ^ (k & 1))
                cp = pltpu.make_async_remote_copy(src_ref=src, dst_ref=my_slot, send_sem=s_send.at[stage, k - 1],
                                                  recv_sem=s_recv.at[stage, k - 1], device_id=peer,
                                                  device_id_type=MESH)
                cp.start()
                copies.append(cp)
            pltpu.sync_copy(src, my_slot)
            return copies

        def finish_small(copies):
            for cp in copies:
                cp.wait()

        def to_neighbours(half):
            for i in (0, 1):
                shard_copy(1 + i, x, y, cidx, (*chips[i], cidx), half=half).start()

        @pl.when(t == 0)
        def _():
            _barrier([(x ^ (k >> 2), y ^ ((k >> 1) & 1), cidx ^ (k & 1)) for k in range(1, N_DEV)])
            g_start()
            x_load.start()
            wv[me_i] = win_ref[...].astype(BF16)
            woutb_ref[...] = wout_ref[...].astype(BF16)
            shard_copy(0, x, y, cidx, sibling).start()
            conds_sent = small_gather(c_ref, call_s.at[pl.ds(me_i, 1), :], 0)
            to_neighbours(0)
            finish_small(conds_sent)
            call_ref[...] = call_s[...]
            off = pl.multiple_of(me_i * nloc, 128)
            b = ab_ref[:, pl.ds(off, nloc)]
            w = aw_ref[...]
            sx, _ = _silu_and_grad(call_s[...])
            sc, _ = _silu_and_grad(jnp.broadcast_to(cc_ref[...], (8, D)))
            part_s[0:8, :] = _dot(sx, w) + b
            part_s[8:16, :] = _dot(sc, w) + b
            parts_sent = small_gather(part_s, parts_s.at[me_i], 1)
            to_neighbours(1)
            finish_small(parts_sent)
            mine = _rows((16, nloc)) == me_i
            for j in range(N_DEV):
                pj = parts_s[j]
                modx_ref[:, j * nloc:(j + 1) * nloc] = jnp.sum(jnp.where(mine, pj, 0.0), axis=0, keepdims=True)
                modc_ref[:, j * nloc:(j + 1) * nloc] = pj[8:9, :]
            shift, scale1, ngv = modx_ref[:, 0:D], 1.0 + modx_ref[:, D:2 * D], ng_ref[...]
            x_load.wait()
            for r in range(L // ROWS):
                rsl = slice(r * ROWS, (r + 1) * ROWS)
                xv = x_v[rsl, :]
                rs = lax.rsqrt(jnp.mean(xv * xv, axis=-1, keepdims=True) + NORM_EPS)
                hn_ref[rsl, :] = ((xv * rs * ngv) * scale1 + shift).astype(BF16)

        @pl.when(t == 1)
        def _():
            shard_copy(0, x, y, 1 - cidx, sibling).wait_recv()
            g_pass()

        for i in (0, 1):
            @pl.when(t == ARRIVAL.index((2, 4)[i]))
            def _(i=i):
                shard_copy(1 + i, *chips[i], cidx, sibling).wait_recv()
                shard_copy(4 + i, *chips[i], cidx, sibling).start()
                shard_copy((7, 3)[i], *chips[i], cidx, (*chips[1 - i], cidx), half=i).start()

        @pl.when(t == ARRIVAL.index(6))
        def _():
            shard_copy(3, *chips[2], cidx, sibling, half=1).wait_recv()
            shard_copy(7, *chips[2], cidx, sibling, half=0).wait_recv()
            shard_copy(6, *chips[2], cidx, sibling).start()

        for i in range(3):
            @pl.when(t == ARRIVAL.index((3, 5, 7)[i]))
            def _(i=i):
                shard_copy(4 + i, *chips[i], 1 - cidx, sibling).wait_recv()

        @pl.when(t == 2)
        def _():
            g_finish()

        dev = me_i ^ arr_ref[t]
        for r in range(L // (2 * ROWS)):
            rsl = slice(r * 2 * ROWS, (r + 1) * 2 * ROWS)
            z_ref[rsl, :] = _dot(hn_ref[rsl, :], wv[dev])
        col = pl.ds(pl.multiple_of(dev * ws, 128), ws)
        pltpu.make_async_copy(wv.at[dev], wfull_ref.at[:, col], hbm_sems.at[t]).start()

        @pl.when(t == N_DEV - 1)
        def _():
            for k in (0, 1, 2, 4, 5, 6):
                shard_copy(k, x, y, cidx, sibling).wait_send()
            for k in (3, 7):
                shard_copy(k, x, y, cidx, sibling, half=0).wait_send()
            for s in range(N_DEV):
                pltpu.make_async_copy(wv.at[0], wfull_ref.at[:, pl.ds(0, ws)], hbm_sems.at[s]).wait()

    const = lambda *shape: pl.BlockSpec(shape, lambda t, m, a: (0,) * len(shape))
    once = lambda *shape: pl.BlockSpec(shape, lambda t, m, a: (0,) * len(shape), pipeline_mode=pl.Buffered(1))
    return _call(
        body, name="front_project",
        out_shape=[jax.ShapeDtypeStruct((L, D_IN), F32), jax.ShapeDtypeStruct((L, D), BF16),
                   jax.ShapeDtypeStruct((D, D_IN), BF16), jax.ShapeDtypeStruct(w_out.shape, BF16),
                   jax.ShapeDtypeStruct((1, 3 * D), F32), jax.ShapeDtypeStruct((1, 3 * D), F32),
                   jax.ShapeDtypeStruct((N_DEV, D), F32), jax.ShapeDtypeStruct((CONV_W, D), F32),
                   jax.ShapeDtypeStruct((2, D), F32)],
        grid_spec=pltpu.PrefetchScalarGridSpec(
            num_scalar_prefetch=2, grid=(N_DEV,),
            in_specs=[HBM, const(1, D), const(1, D), once(D, nloc), const(1, 3 * D), const(1, D),
                      once(D, ws), once(*w_out.shape), HBM, HBM],
            out_specs=[pl.BlockSpec((L, ws), lambda t, m, a: (0, m[0] ^ a[t])), const(L, D), HBM,
                       const(*w_out.shape),
                       const(1, 3 * D), const(1, 3 * D), const(N_DEV, D), HBM, HBM],
            scratch_shapes=[pltpu.VMEM((N_DEV, D, ws), BF16), pltpu.VMEM((N_DEV, D), F32), pltpu.VMEM((16, nloc), F32),
                            pltpu.VMEM((N_DEV, 16, nloc), F32), pltpu.SemaphoreType.DMA((8,)),
                            pltpu.SemaphoreType.DMA((8,)), pltpu.SemaphoreType.DMA((N_DEV,)),
                            pltpu.SemaphoreType.DMA((2, N_DEV - 1)), pltpu.SemaphoreType.DMA((2, N_DEV - 1))]
            + _gather2_sems(2) + [pltpu.VMEM((L, D), F32), pltpu.SemaphoreType.DMA(())]),
        compiler_params=pltpu.CompilerParams(dimension_semantics=("arbitrary",), vmem_limit_bytes=VMEM_LIMIT,
                                             has_side_effects=True, collective_id=10),
    )(me, arrival, pltpu.with_memory_space_constraint(xr, pltpu.HBM), c, c_ctx, ada_w, ada_b, ng, w_in, w_out,
      pltpu.with_memory_space_constraint(cw, pltpu.HBM), pltpu.with_memory_space_constraint(lam, pltpu.HBM))


def _project(xr, mod, ng, w, ncols, tm, name):
    rows = xr.shape[0]

    def body(x_ref, sh_ref, sc_ref, ng_ref, w_ref, z_ref, hn_ref):
        x = x_ref[...]
        rs = lax.rsqrt(jnp.mean(x * x, axis=-1, keepdims=True) + NORM_EPS)
        hn = (x * rs * ng_ref[...]) * (1.0 + sc_ref[...]) + sh_ref[...]
        hb = hn.astype(BF16)
        hn_ref[...] = hb
        for n in range(ncols // D):
            z_ref[:, n * D:(n + 1) * D] = _dot(hb, w_ref[:, n * D:(n + 1) * D])

    vec = pl.BlockSpec((1, D), lambda i: (0, 0))
    return _call(
        body, name=name, grid=(rows // tm,),
        out_shape=[jax.ShapeDtypeStruct((rows, ncols), F32), jax.ShapeDtypeStruct((rows, D), BF16)],
        in_specs=[pl.BlockSpec((tm, D), lambda i: (i, 0)), vec, pl.BlockSpec((1, D), lambda i: (0, 1)), vec,
                  pl.BlockSpec((D, ncols), lambda i: (0, 0), pipeline_mode=pl.Buffered(1))],
        out_specs=[pl.BlockSpec((tm, ncols), lambda i: (i, 0)), pl.BlockSpec((tm, D), lambda i: (i, 0))],
        compiler_params=_params("arbitrary"),
    )(xr, mod, mod, ng, w)


def _scan_pair(af_ref, uf_ref, hf_ref, h0f, ab_ref, ub_ref, hb_ref, h0b, t_len):
    span = 8 * SCAN_BLOCKS
    nit = t_len // span
    rows = _rows((8, HD))

    def local_scan(a, b, forward):
        for s in (1, 2, 4):
            sh = s if forward else 8 - s
            m = rows >= s if forward else rows < 8 - s
            b = a * jnp.where(m, pltpu.roll(b, sh, 0), 0.0) + b
            a = a * jnp.where(m, pltpu.roll(a, sh, 0), 1.0)
        return a, b

    def span_scan(a_ref, u_ref, h_ref, off, carry, forward):
        order = range(SCAN_BLOCKS) if forward else range(SCAN_BLOCKS - 1, -1, -1)
        last = slice(7, 8) if forward else slice(0, 1)
        for q in order:
            rs = pl.ds(off + 8 * q, 8)
            a, b = local_scan(a_ref[rs, :], u_ref[rs, :], forward)
            h_ref[rs, :] = b + a * carry
            carry = a[last, :] * carry + b[last, :]
        return carry

    def body(k, carry):
        cf, cb = carry
        cf = span_scan(af_ref, uf_ref, hf_ref, pl.multiple_of(k * span, span), cf, True)
        cb = span_scan(ab_ref, ub_ref, hb_ref, pl.multiple_of((nit - 1 - k) * span, span), cb, False)
        return cf, cb

    return lax.fori_loop(0, nit, body, (h0f, h0b))


SCAN_BLOCKS = 16


def _shifted(pad_ref, x, offsets, before=0.0, after=0.0):
    n = x.shape[0]
    pad_ref[0:8, :] = jnp.broadcast_to(jnp.asarray(before, F32), (8, x.shape[1]))
    pad_ref[8:8 + n, :] = x
    pad_ref[8 + n:16 + n, :] = jnp.broadcast_to(jnp.asarray(after, F32), (8, x.shape[1]))
    return [pad_ref[8 + o:8 + o + n, :] for o in offsets]


def _conv(xa, cw, cb, pad_ref):
    xm1, xp1, xp2 = _shifted(pad_ref, xa, (-1, 1, 2))
    return xm1 * cw[0:1, :] + xa * cw[1:2, :] + xp1 * cw[2:3, :] + xp2 * cw[3:4, :] + cb


def _gates(xc, wa, wx, ba, bx, nsp):
    xb = xc.astype(BF16)
    r = _sigmoid(_dot(xb, wa) + ba)
    i = _sigmoid(_dot(xb, wx) + bx)
    log_a = r * nsp
    a = jnp.exp(log_a)
    g2 = jnp.tanh(log_a) * (-1.0 - a * a)
    rg = lax.rsqrt(jnp.maximum(g2, 1e-30))
    return r, i, a, g2 * rg, rg


def _lru_param_specs():
    h4 = pl.BlockSpec((2, 1, HD, HD), lambda h: (0, h, 0, 0))
    v2 = pl.BlockSpec((2, HD), lambda h: (0, h))
    b16 = pl.BlockSpec((2 * HEADS, HD), lambda h: (0, 0))
    return dict(
        xa=pl.BlockSpec((L, HD), lambda h: (0, h)), xac=pl.BlockSpec((LC, HD), lambda h: (0, h)),
        cw=pl.BlockSpec((CONV_W, HD), lambda h: (0, h)), cb=pl.BlockSpec((1, HD), lambda h: (0, h)), h4=h4, v2=v2,
        b16=b16)


def _bias_row(ref, d):
    mask = _rows((2 * HEADS, HD)) == d * HEADS + pl.program_id(0)
    return jnp.sum(jnp.where(mask, ref[...], 0.0), axis=0, keepdims=True), mask


def _lru_forward(zx, zc, cw, cb, wa, wx, ba, bx, lam, gather, gather_modes):
    ng_ = len(gather)

    def body(xa_ref, xac_ref, cw_ref, cb_ref, wa_ref, wx_ref, ba_ref, bx_ref, lam_ref, *rest):
        yl_ref = rest[ng_]
        af, uf, hf, ab, ub, hb, pad_s = rest[2 * ng_ + 1:2 * ng_ + 8]
        start, pass_on, finish = _gather2_ops(rest[:ng_], rest[ng_ + 1:2 * ng_ + 1], gather_modes,
                                              *rest[2 * ng_ + 8:], barrier=True)
        pl.when(pl.program_id(0) == 0)(start)
        pl.when(pl.program_id(0) == HEADS // 2)(pass_on)
        pl.when(pl.program_id(0) == HEADS - 1)(finish)
        cwv, cbv = cw_ref[...], cb_ref[...]
        nsp = (-LRU_C) * _softplus(-lam_ref[...])

        def forward(xa, t_len, h0f, h0b):
            xc = _conv(xa, cwv, cbv, pad_s)
            for d, (a_ref, u_ref) in enumerate(((af, uf), (ab, ub))):
                _, i, a, gamma, _ = _gates(xc, wa_ref[d, 0].astype(BF16), wx_ref[d, 0].astype(BF16),
                                           _bias_row(ba_ref, d)[0], _bias_row(bx_ref, d)[0], nsp[d:d + 1, :])
                a_ref[0:t_len, :] = a
                u_ref[0:t_len, :] = gamma * (i * xc)
            return _scan_pair(af, uf, hf, h0f, ab, ub, hb, h0b, t_len)

        z = jnp.zeros((1, HD), F32)
        h0f, h0b = forward(xac_ref[...], LC, z, z)
        forward(xa_ref[...], L, h0f, h0b)
        yl_ref[...] = hf[...] + hb[...]

    s = _lru_param_specs()
    return _call(
        body, name="lru_forward", grid=(HEADS,),
        out_shape=[jax.ShapeDtypeStruct((L, D), F32)] + _gather2_shapes(gather, gather_modes),
        in_specs=[s["xa"], s["xac"], s["cw"], s["cb"], s["h4"], s["h4"], s["b16"], s["b16"], s["v2"]] + [HBM] * ng_,
        out_specs=[pl.BlockSpec((L, HD), lambda h: (0, h))] + [HBM] * ng_,
        scratch_shapes=[pltpu.VMEM((L, HD), F32)] * 6 + [pltpu.VMEM((L + 16, HD), F32)] + _gather2_sems(ng_),
        compiler_params=pltpu.CompilerParams(dimension_semantics=("arbitrary",), vmem_limit_bytes=VMEM_LIMIT,
                                             has_side_effects=True, collective_id=5),
    )(zx, zc, cw, cb, wa, wx, ba, bx, lam, *[pltpu.with_memory_space_constraint(a, pltpu.HBM) for a in gather])


def _lru_backward(zx, zc, dyl, dz, cw, cb, wa, wx, ba, bx, lam, chip_sums, first_chips=None):
    nr = len(chip_sums)

    def body(xa_ref, xac_ref, dyl_ref, dz_in, cw_ref, cb_ref, wa_ref, wx_ref, ba_ref, bx_ref, lam_ref, *rest):
        (dxa_ref, dxac_ref, dwa_ref, dwx_ref, dba_ref, dbx_ref, dlam_ref, dcw_ref,
         dcb_ref) = rest[nr:nr + 9]
        main_s, ctx_s, pad_s = rest[3 * nr + 9:3 * nr + 12]
        if nr:
            start, forward, finish = _chips_ops(rest[:nr], rest[nr + 9:2 * nr + 9], rest[2 * nr + 9:3 * nr + 9],
                                                *rest[3 * nr + 12:], first_chips=first_chips, barrier=True)
            pl.when(pl.program_id(0) == 0)(start)
            pl.when(pl.program_id(0) == HEADS // 2)(forward)
            pl.when(pl.program_id(0) == HEADS - 1)(finish)
        del dz_in

        @pl.when(pl.program_id(0) == 0)
        def _():
            dba_ref[...] = jnp.zeros_like(dba_ref)
            dbx_ref[...] = jnp.zeros_like(dbx_ref)

        cwv, cbv = cw_ref[...], cb_ref[...]
        lamv = lam_ref[...]
        sp = _softplus(-lamv)
        nsp = (-LRU_C) * sp
        z = jnp.zeros((1, HD), F32)

        def wmat(ref, d):
            return ref[d, 0].astype(BF16)

        def workspace(s):
            return dict(a=(s.at[0], s.at[1]), u=(s.at[2], s.at[3]), h=(s.at[4], s.at[5]), rho=(s.at[6], s.at[7]),
                        saved=(tuple(s.at[8 + k] for k in range(4)), tuple(s.at[12 + k] for k in range(4))),
                        xc=s.at[16])

        def forward(ws, xa, t_len, h0f, h0b):
            xc = _conv(xa, cwv, cbv, pad_s)
            ws["xc"][...] = xc
            for d in (0, 1):
                vals = _gates(xc, wmat(wa_ref, d), wmat(wx_ref, d), _bias_row(ba_ref, d)[0],
                              _bias_row(bx_ref, d)[0], nsp[d:d + 1, :])
                r, i, a, gamma, rg = vals
                ws["a"][d][...] = a
                ws["u"][d][...] = gamma * (i * xc)
                for ref, val in zip(ws["saved"][d], (r, i, gamma, rg)):
                    ref[...] = val
            return _scan_pair(ws["a"][0], ws["u"][0], ws["h"][0], h0f, ws["a"][1], ws["u"][1], ws["h"][1], h0b,
                              t_len)

        def backward(ws, xa, t_len, h0f, h0b, dhf, dhb, first):
            xc = ws["xc"][...]
            (af, ab), (uf, ub), (hf, hb), (rf, rb) = ws["a"], ws["u"], ws["h"], ws["rho"]
            uf[...] = ab[...] * dhb
            ub[...] = af[...] * dhf
            rho_b_last, rho_f_first = _scan_pair(ab, uf, rb, z, af, ub, rf, z, t_len)
            dxc = jnp.zeros((t_len, HD), F32)
            dsp = []
            for d in (0, 1):
                r, i, gamma, rg = (ref[...] for ref in ws["saved"][d])
                a = ws["a"][d][...]
                if d == 0:
                    lam_t = dhf + _shifted(pad_s, rf[...], (1,))[0]
                    h_prev = _shifted(pad_s, hf[...], (-1,), before=h0f)[0]
                else:
                    lam_t = dhb + _shifted(pad_s, rb[...], (-1,))[0]
                    h_prev = _shifted(pad_s, hb[...], (1,), after=h0b)[0]
                da = lam_t * h_prev
                lx = lam_t * xc
                d_i = lx * gamma
                d_gamma = lx * i
                dxc = dxc + lam_t * (gamma * i)
                d_log_a = a * (da - d_gamma * (a * rg))
                dsp.append(jnp.sum(d_log_a * r, axis=0, keepdims=True) * (-LRU_C))
                d_pre_r = d_log_a * nsp[d:d + 1, :] * (r * (1.0 - r))
                d_pre_i = d_i * (i * (1.0 - i))
                prb, pib, xb = d_pre_r.astype(BF16), d_pre_i.astype(BF16), xc.astype(BF16)
                dxc = dxc + _dot_nt(prb, wmat(wa_ref, d)) + _dot_nt(pib, wmat(wx_ref, d))
                g_wa, g_wx = _dot_tn(xb, prb), _dot_tn(xb, pib)
                g_ba = jnp.sum(d_pre_r, axis=0, keepdims=True)
                g_bx = jnp.sum(d_pre_i, axis=0, keepdims=True)
                mask = _bias_row(ba_ref, d)[1]
                dba_ref[...] += jnp.where(mask, g_ba, 0.0)
                dbx_ref[...] += jnp.where(mask, g_bx, 0.0)
                if first:
                    dwa_ref[d, 0] = g_wa
                    dwx_ref[d, 0] = g_wx
                else:
                    dwa_ref[d, 0] += g_wa
                    dwx_ref[d, 0] += g_wx
            g_lam = jnp.concatenate(dsp, axis=0) * (-_sigmoid(-lamv))
            dm1, dp1, dm2 = _shifted(pad_s, dxc, (-1, 1, -2))
            dxa = dp1 * cwv[0:1, :] + dxc * cwv[1:2, :] + dm1 * cwv[2:3, :] + dm2 * cwv[3:4, :]
            xm1, xp1, xp2 = _shifted(pad_s, xa, (-1, 1, 2))
            g_cw = jnp.concatenate([jnp.sum(dxc * v, axis=0, keepdims=True) for v in (xm1, xa, xp1, xp2)], axis=0)
            g_cb = jnp.sum(dxc, axis=0, keepdims=True)
            if first:
                dlam_ref[...] = g_lam
                dcw_ref[...] = g_cw
                dcb_ref[...] = g_cb
            else:
                dlam_ref[...] += g_lam
                dcw_ref[...] += g_cw
                dcb_ref[...] += g_cb
            return dxa, rho_f_first, rho_b_last

        ws_x, ws_c = workspace(main_s), workspace(ctx_s)
        h0f, h0b = forward(ws_c, xac_ref[...], LC, z, z)
        forward(ws_x, xa_ref[...], L, h0f, h0b)
        dh = dyl_ref[...]
        dxa, dh0f, dh0b = backward(ws_x, xa_ref[...], L, h0f, h0b, dh, dh, True)
        dxa_ref[...] = dxa.astype(BF16)
        rc = _rows((LC, HD))
        dxac, _, _ = backward(ws_c, xac_ref[...], LC, z, z, jnp.where(rc == LC - 1, dh0f, 0.0),
                              jnp.where(rc == 0, dh0b, 0.0), False)
        dxac_ref[...] = dxac.astype(BF16)

    s = _lru_param_specs()
    col = lambda r: pl.BlockSpec((r, HD), lambda h: (0, h))
    return _call(
        body, name="lru_backward", grid=(HEADS,),
        out_shape=[jax.ShapeDtypeStruct((L, D_IN), BF16), jax.ShapeDtypeStruct((LC, D), BF16),
                   jax.ShapeDtypeStruct((2, HEADS, HD, HD), F32), jax.ShapeDtypeStruct((2, HEADS, HD, HD), F32),
                   jax.ShapeDtypeStruct((2 * HEADS, HD), F32), jax.ShapeDtypeStruct((2 * HEADS, HD), F32),
                   jax.ShapeDtypeStruct((2, D), F32), jax.ShapeDtypeStruct((CONV_W, D), F32),
                   jax.ShapeDtypeStruct((1, D), F32)] + [jax.ShapeDtypeStruct((4,) + a.shape[1:], a.dtype)
                                                          for a in chip_sums] + _chips_stage_shapes(chip_sums),
        in_specs=[s["xa"], s["xac"], col(L), pl.BlockSpec(memory_space=pl.ANY), s["cw"], s["cb"], s["h4"], s["h4"],
                  s["b16"], s["b16"], s["v2"]] + [HBM] * nr,
        out_specs=[col(L), col(LC), s["h4"], s["h4"], s["b16"], s["b16"], s["v2"], col(CONV_W), col(1)]
        + [HBM] * (2 * nr),
        scratch_shapes=[pltpu.VMEM((17, L, HD), F32), pltpu.VMEM((17, LC, HD), F32), pltpu.VMEM((L + 16, HD), F32)]
        + (_chips_sems(nr) if nr else []),
        input_output_aliases={3: 0},
        compiler_params=pltpu.CompilerParams(dimension_semantics=("arbitrary",), vmem_limit_bytes=VMEM_LIMIT,
                                             has_side_effects=True, collective_id=6 if nr else None),
    )(zx, zc, dyl, dz, cw, cb, wa, wx, ba, bx, lam, *[pltpu.with_memory_space_constraint(a, pltpu.HBM)
                                                       for a in chip_sums])


def _mixer_loss(x, tgt, zx, yl, gx, fg, lng, lnb, ws, bst, wout, tm):
    ncht = tm // CHUNK

    def body(x_ref, t_ref, ga_ref, u_ref, v_ref, gb_ref, yl_ref, gx_ref, fg_ref, lng_ref, lnb_ref, ws_ref,
             bst_ref, wout_ref,
             dz_ref, dyl_ref, dxn_ref, y_s, do_ref, dws_ref, dbst_ref, vec_ref,
             vn_s, mix_s, dm_s, dvn_s, bst_s, dbst_s):
        step = pl.program_id(0)

        @pl.when(step == 0)
        def _():
            dws_ref[...] = jnp.zeros_like(dws_ref)
            dbst_s[...] = jnp.zeros_like(dbst_s)
            vec_ref[...] = jnp.zeros_like(vec_ref)
            bst_s[...] = bst_ref[...].T

        u, v = u_ref[...], v_ref[...]
        ug, dug_du = _gelu_and_grad(u)
        vg, dvg_dv = _gelu_and_grad(v)
        mu = jnp.mean(vg, axis=-1, keepdims=True)
        vc = vg - mu
        rstd = lax.rsqrt(jnp.mean(vc * vc, axis=-1, keepdims=True) + LN_EPS)
        vhat = vc * rstd
        lngv = lng_ref[...]
        vn_s[...] = (vhat * lngv + lnb_ref[...]).astype(BF16)
        for ch in range(ncht):
            rs = slice(ch * CHUNK, (ch + 1) * CHUNK)
            for g in range(HEADS):
                cs = slice(g * HD, (g + 1) * HD)
                mix_s[rs, cs] = _dot(ws_ref[g].astype(BF16), vn_s[rs, cs]) + bst_s[:, g:g + 1]
        mixed = mix_s[...]
        ga, gb, yl = ga_ref[...], gb_ref[...], yl_ref[...]
        sga, dsga = _silu_and_grad(ga)
        sgb, dsgb = _silu_and_grad(gb)
        ys = ug * mixed
        y_s[:, 0:D] = (yl * sga).astype(BF16)
        y_s[:, D:D_MIX] = (ys * sgb).astype(BF16)
        o = _dot(y_s[...], wout_ref[...])
        gxv, fgv = gx_ref[...], fg_ref[...]
        xn = x_ref[...] + gxv * o
        rs2 = lax.rsqrt(jnp.mean(xn * xn, axis=-1, keepdims=True) + NORM_EPS)
        xh = xn * rs2
        diff = xh * fgv - t_ref[...]
        vec_ref[R_LOSS:R_LOSS + 1, :] += jnp.full((1, D), jnp.sum(diff * diff) * (0.5 / D), F32)
        dout = diff * (1.0 / D)
        w = dout * fgv
        dxn = rs2 * (w - xh * jnp.mean(w * xh, axis=-1, keepdims=True))
        dxn_ref[...] = dxn
        vec_ref[0:1, :] += jnp.sum(dxn * o, axis=0, keepdims=True)
        vec_ref[1:2, :] += jnp.sum(dout * xh, axis=0, keepdims=True)
        dob = (dxn * gxv).astype(BF16)
        do_ref[...] = dob
        dy = _dot_nt(dob, wout_ref[...])
        dya, dyb = dy[:, 0:D], dy[:, D:D_MIX]
        dyl_ref[...] = dya * sga
        dys = dyb * sgb
        dz_ref[:, 0:D] = jnp.zeros((tm, D), BF16)
        dz_ref[:, D:2 * D] = (dya * yl * dsga).astype(BF16)
        dz_ref[:, 2 * D:3 * D] = (dys * mixed * dug_du).astype(BF16)
        dz_ref[:, 4 * D:5 * D] = (dyb * ys * dsgb).astype(BF16)
        dm = dys * ug
        dm_s[...] = dm.astype(BF16)
        for g in range(HEADS):
            cs = slice(g * HD, (g + 1) * HD)
            dbst_s[:, g:g + 1] += sum(jnp.sum(dm[ch * CHUNK:(ch + 1) * CHUNK, cs], axis=1, keepdims=True)
                                      for ch in range(ncht))
            for ch in range(ncht):
                rs = slice(ch * CHUNK, (ch + 1) * CHUNK)
                dws_ref[g] += _dot_nt(dm_s[rs, cs], vn_s[rs, cs])
                dvn_s[rs, cs] = _dot_tn(ws_ref[g].astype(BF16), dm_s[rs, cs])
        dvn = dvn_s[...]
        vec_ref[2:3, :] += jnp.sum(dvn * vhat, axis=0, keepdims=True)
        vec_ref[3:4, :] += jnp.sum(dvn, axis=0, keepdims=True)
        dvh = dvn * lngv
        dvg = rstd * (dvh - jnp.mean(dvh, axis=-1, keepdims=True) - vhat * jnp.mean(dvh * vhat, axis=-1, keepdims=True))
        dz_ref[:, 3 * D:4 * D] = (dvg * dvg_dv).astype(BF16)

        @pl.when(step == pl.num_programs(0) - 1)
        def _():
            dbst_ref[...] = dbst_s[...].T

    tile = pl.BlockSpec((tm, D), lambda i: (i, 0))
    zcol = lambda n: pl.BlockSpec((tm, D), lambda i: (i, n))
    vec = pl.BlockSpec((1, D), lambda i: (0, 0))
    full = lambda *s: pl.BlockSpec(s, lambda i: (0,) * len(s))
    return _call(
        body, name="mixer_loss", grid=(L // tm,),
        out_shape=[jax.ShapeDtypeStruct((L, D_IN), BF16), jax.ShapeDtypeStruct((L, D), F32),
                   jax.ShapeDtypeStruct((L, D), F32), jax.ShapeDtypeStruct((L, D_MIX), BF16),
                   jax.ShapeDtypeStruct((L, D), BF16),
                   jax.ShapeDtypeStruct((HEADS, CHUNK, CHUNK), F32), jax.ShapeDtypeStruct((HEADS, CHUNK), F32),
                   jax.ShapeDtypeStruct((8, D), F32)],
        in_specs=[tile, tile, zcol(1), zcol(2), zcol(3), zcol(4), tile, pl.BlockSpec((1, D), lambda i: (0, 2)),
                  vec, vec, vec,
                  full(HEADS, CHUNK, CHUNK), full(HEADS, CHUNK),
                  pl.BlockSpec((D_MIX, D), lambda i: (0, 0), pipeline_mode=pl.Buffered(1))],
        out_specs=[pl.BlockSpec((tm, D_IN), lambda i: (i, 0)), tile, tile,
                   pl.BlockSpec((tm, D_MIX), lambda i: (i, 0)), tile,
                   full(HEADS, CHUNK, CHUNK), full(HEADS, CHUNK), full(8, D)],
        scratch_shapes=[pltpu.VMEM((tm, D), BF16), pltpu.VMEM((tm, D), F32),
                        pltpu.VMEM((tm, D), BF16), pltpu.VMEM((tm, D), F32)] + [pltpu.VMEM((CHUNK, HEADS), F32)] * 2,
        compiler_params=_params("arbitrary"),
    )(x, tgt, zx, zx, zx, zx, yl, gx, fg, lng, lnb, ws, bst, wout)


def _grad_w(a, b, a2, b2, tk, name, bw, first, nblocks, split, barrier_id, riders=()):
    nk = a.shape[0] // tk
    m = a.shape[1]
    with_ctx = a2 is not None
    if split == "cols":
        slots, r, w = nblocks, m, bw // 2
        piece = lambda q, pc: (slice(None), slice(pc * w, (pc + 1) * w))
    else:
        slots, r, w = 4, m // 8, bw
        piece = lambda q, pc: (slice((2 * q + pc) * r, (2 * q + pc + 1) * r), slice(None))

    nr = len(riders)

    def body(*refs):
        a_ref, b_ref = refs[:2]
        a2_ref, b2_ref = refs[2:4] if with_ctx else (None, None)
        base = 4 if with_ctx else 2
        sums_ref = refs[base + nr]
        s0 = base + 3 * nr + 1
        acc, mine_v, send_v, stage_v, send_sems, recv_sems = refs[s0:s0 + 6]
        n, k = pl.program_id(0), pl.program_id(1)
        x, y, c = lax.axis_index("x"), lax.axis_index("y"), lax.axis_index("c")

        def to_sibling(s):
            return pltpu.make_async_remote_copy(src_ref=send_v.at[s], dst_ref=stage_v.at[s], send_sem=send_sems.at[s],
                                                recv_sem=recv_sems.at[s], device_id=(x, y, 1 - c),
                                                device_id_type=MESH)

        if nr:
            rider_in = refs[base:base + nr]
            own_v, got_v, psum_v = (refs[s0 + 9 + i * nr:s0 + 9 + (i + 1) * nr] for i in range(3))
            p_send, p_recv, p_local = refs[s0 + 9 + 3 * nr:s0 + 12 + 3 * nr]
            c_start, c_forward, c_finish = _chips_ops(psum_v, refs[base + nr + 1:base + 2 * nr + 1],
                                                      refs[base + 2 * nr + 1:base + 3 * nr + 1], *refs[s0 + 6:s0 + 9])

            @pl.when(jnp.logical_and(n == 0, k == 0))
            def _():
                _barrier([(x, y, 1 - c)] + [(x ^ (j >> 1), y ^ (j & 1), c) for j in (1, 2)])
                copies = []
                for j in range(nr):
                    for q in range(4):
                        copies.append(pltpu.make_async_remote_copy(
                            src_ref=rider_in[j].at[2 * q + 1 - c], dst_ref=got_v[j].at[q], send_sem=p_send.at[j, q],
                            recv_sem=p_recv.at[j, q], device_id=(x, y, 1 - c), device_id_type=MESH))
                        copies.append(pltpu.make_async_copy(rider_in[j].at[2 * q + c], own_v[j].at[q],
                                                            p_local.at[j, q]))
                for cp in copies:
                    cp.start()
                for cp in copies:
                    cp.wait()
                for j in range(nr):
                    psum_v[j][...] = (own_v[j][...] + got_v[j][...]).astype(BF16)
                c_start()
        else:
            pl.when(jnp.logical_and(n == 0, k == 0))(_sibling_barrier)

        @pl.when(k == 0)
        def _():
            acc[...] = _dot_tn(a_ref[...], b_ref[...])

        if nk > 1:
            @pl.when(k > 0)
            def _():
                acc[...] += _dot_tn(a_ref[...], b_ref[...])

        if with_ctx:
            @pl.when(jnp.logical_and(k == nk - 1, n == 0))
            def _():
                acc[:, 0:b2_ref.shape[1]] += _dot_tn(a2_ref[...], b2_ref[...])

        if nr:
            pl.when(jnp.logical_and(k == nk - 1, n == nblocks - 1))(c_forward)

        def hand_over(s, q):
            for pc in (0, 1):
                @pl.when(c == pc)
                def _(pc=pc):
                    mine_v[s] = acc[piece(q, pc)]
                    send_v[s] = acc[piece(q, 1 - pc)].astype(BF16)
            to_sibling(s).start()

        for i in range(nblocks):
            @pl.when(jnp.logical_and(k == nk - 1, n == i))
            def _(i=i):
                if split == "cols":
                    hand_over(i, 0)
                else:
                    for q in range(4):
                        hand_over(q, q)

        @pl.when(jnp.logical_and(k == nk - 1, n == nblocks - 1))
        def _():
            for s in range(slots):
                to_sibling(s).wait_recv()
                sums_ref[s] = (mine_v[s] + stage_v[s].astype(F32)).astype(BF16)
            for s in range(slots):
                to_sibling(s).wait_send()
            if nr:
                c_finish()

    in_specs = [pl.BlockSpec((tk, m), lambda n, k: (k, 0)), pl.BlockSpec((tk, bw), lambda n, k: (k, n + first))]
    args = [a, b]
    if with_ctx:
        in_specs += [pl.BlockSpec(a2.shape, lambda n, k: (0, 0)), pl.BlockSpec(b2.shape, lambda n, k: (0, 0))]
        args += [a2, b2]
    in_specs += [HBM] * nr
    args += [pltpu.with_memory_space_constraint(s, pltpu.HBM) for s in riders]
    rider_sums = [jax.ShapeDtypeStruct((4,) + s.shape[1:], BF16) for s in riders]
    rider_scratch = []
    if nr:
        rider_scratch = (_chips_sems(nr) + [pltpu.VMEM(s.shape, F32) for s in rider_sums] * 2
                         + [pltpu.VMEM(s.shape, BF16) for s in rider_sums]
                         + [pltpu.SemaphoreType.DMA((nr, 4))] * 3)
    return _call(
        body, name=name, grid=(nblocks, nk),
        out_shape=[jax.ShapeDtypeStruct((slots, r, w), BF16)] + rider_sums + _chips_stage_shapes(rider_sums),
        in_specs=in_specs, out_specs=[pl.BlockSpec((slots, r, w), lambda n, k: (0, 0, 0))] + [HBM] * (2 * nr),
        scratch_shapes=[pltpu.VMEM((m, bw), F32), pltpu.VMEM((slots, r, w), F32), pltpu.VMEM((slots, r, w), BF16),
                        pltpu.VMEM((slots, r, w), BF16), pltpu.SemaphoreType.DMA((slots,)),
                        pltpu.SemaphoreType.DMA((slots,))] + rider_scratch,
        compiler_params=pltpu.CompilerParams(dimension_semantics=("arbitrary", "arbitrary"),
                                             vmem_limit_bytes=VMEM_LIMIT, has_side_effects=True,
                                             collective_id=barrier_id),
    )(*args)


def _grad_rows(xr, dz, w, mod, ng, dres, ncols, tm, name, chip_sums=(), first_chips=None, dests=None):
    rows = xr.shape[0]
    steps = rows // tm
    with_dx = dres is not None
    nr = len(chip_sums)
    dests = [d for d in (dests or [None] * nr)]
    nd = sum(d is not None for d in dests)
    nin = 6 if with_dx else 5
    nout = 2 if with_dx else 1

    def body(*refs):
        if with_dx:
            x_ref, dz_ref, w_ref, sc_ref, ng_ref, dres_ref = refs[:nin]
            dx_ref, vec_ref = refs[nin + nr + nd:nin + nr + nd + nout]
        else:
            x_ref, dz_ref, w_ref, sc_ref, ng_ref = refs[:nin]
            (vec_ref,) = refs[nin + nr + nd:nin + nr + nd + nout]
        if nr:
            o0 = nin + nr + nd + nout
            start, forward, finish = _chips_ops(refs[nin:nin + nr], refs[o0:o0 + nr], refs[o0 + nr:o0 + 2 * nr],
                                                *refs[o0 + 2 * nr:o0 + 2 * nr + 3], first_chips=first_chips,
                                                barrier=True)
            w_hbm, w_ref, w_sem = w_ref, refs[o0 + 2 * nr + 3], refs[o0 + 2 * nr + 4]

            @pl.when(pl.program_id(0) == 0)
            def _():
                start()
                w_load = pltpu.make_async_copy(w_hbm, w_ref, w_sem)
                w_load.start()
                w_load.wait()

            pl.when(pl.program_id(0) == steps // 2)(forward)
            pl.when(pl.program_id(0) == steps - 1)(finish)

        @pl.when(pl.program_id(0) == 0)
        def _():
            vec_ref[...] = jnp.zeros_like(vec_ref)

        dhn = _dot_nt(dz_ref[...], w_ref[...])
        x = x_ref[...]
        rs = lax.rsqrt(jnp.mean(x * x, axis=-1, keepdims=True) + NORM_EPS)
        xh = x * rs
        ngv = ng_ref[...]
        y = xh * ngv
        vec_ref[0:1, :] += jnp.sum(dhn, axis=0, keepdims=True)
        vec_ref[1:2, :] += jnp.sum(dhn * y, axis=0, keepdims=True)
        dy = dhn * (1.0 + sc_ref[...])
        vec_ref[2:3, :] += jnp.sum(dy * xh, axis=0, keepdims=True)
        if with_dx:
            dxh = dy * ngv
            dx_ref[...] = dres_ref[...] + rs * (dxh - xh * jnp.mean(dxh * xh, axis=-1, keepdims=True))

    tile = pl.BlockSpec((tm, D), lambda i: (i, 0))
    vec = pl.BlockSpec((1, D), lambda i: (0, 0))
    w_spec = HBM if nr else pl.BlockSpec((D, ncols), lambda i: (0, 0), pipeline_mode=pl.Buffered(1))
    in_specs = [tile, pl.BlockSpec((tm, ncols), lambda i: (i, 0)), w_spec, pl.BlockSpec((1, D), lambda i: (0, 1)), vec]
    out_shape = [jax.ShapeDtypeStruct((8, D), F32)]
    out_specs = [pl.BlockSpec((8, D), lambda i: (0, 0))]
    args = [xr, dz, pltpu.with_memory_space_constraint(w, pltpu.HBM) if nr else w, mod, ng]
    if with_dx:
        in_specs.append(tile)
        out_shape.insert(0, jax.ShapeDtypeStruct((rows, D), F32))
        out_specs.insert(0, tile)
        args.append(dres)
    aliases = {}
    for j, d in enumerate(dests):
        if d is not None:
            aliases[len(args) + nr + len(aliases)] = len(out_shape) + j
    in_specs += [HBM] * (nr + nd)
    out_specs += [HBM] * (2 * nr)
    out_shape += [jax.ShapeDtypeStruct((4,) + a.shape[1:], a.dtype) for a in chip_sums]
    out_shape += _chips_stage_shapes(chip_sums)
    args += [pltpu.with_memory_space_constraint(a, pltpu.HBM) for a in chip_sums]
    args += [pltpu.with_memory_space_constraint(d, pltpu.HBM) for d in dests if d is not None]
    return _call(body, name=name, grid=(steps,), out_shape=out_shape, in_specs=in_specs, out_specs=out_specs,
                 scratch_shapes=(_chips_sems(nr) + [pltpu.VMEM((D, ncols), BF16), pltpu.SemaphoreType.DMA(())])
                 if nr else [], input_output_aliases=aliases,
                 compiler_params=pltpu.CompilerParams(dimension_semantics=("arbitrary",),
                                                      vmem_limit_bytes=VMEM_LIMIT, has_side_effects=bool(nr),
                                                      collective_id=7 if nr else None))(*args)


def _adamw(w, g, m, v):
    m = ADAM_B1 * m + (1.0 - ADAM_B1) * g
    v = ADAM_B2 * v + (1.0 - ADAM_B2) * (g * g)
    m_hat = m / (1.0 - ADAM_B1 ** ADAM_STEP)
    v_hat = v / (1.0 - ADAM_B2 ** ADAM_STEP)
    delta = -ADAM_LR * (m_hat / (jnp.sqrt(v_hat) + ADAM_EPS) + ADAM_WD * w)
    return delta, m, v


def _adamw_reduced(parts, w, m, v, tr, name):
    r, n = w.shape
    nparts = parts.shape[0]

    def body(p_ref, w_ref, m_ref, v_ref, g_ref, d_ref, mo_ref, vo_ref):
        g = p_ref[0].astype(F32)
        for i in range(1, nparts):
            g = g + p_ref[i].astype(F32)
        g_ref[...] = g
        d_ref[...], mo_ref[...], vo_ref[...] = _adamw(w_ref[...], g, m_ref[...], v_ref[...])

    tile = pl.BlockSpec((tr, n), lambda i: (i, 0))
    sds = jax.ShapeDtypeStruct((r, n), F32)
    return _call(
        body, name=name, grid=(r // tr,), out_shape=[sds] * 4,
        in_specs=[pl.BlockSpec((nparts, tr, n), lambda i: (0, i, 0)), tile, tile, tile], out_specs=[tile] * 4,
        compiler_params=_params("arbitrary"),
    )(parts, w, m, v)


R_GATE, R_FINAL_G, R_LN_G, R_LN_B, R_LOSS = 0, 1, 2, 3, 4
R_SH_X, R_SC_X, R_NG_X = 5, 6, 7
R_SH_C, R_SC_C, R_NG_C = 8, 9, 10
R_LAM, R_CW, R_CB = 11, 13, 17
PACK_ROWS = 24
Q_BA, Q_BX, Q_SGU_B, PACK128_ROWS = 0, 16, 32, 40


def _reduce_small(vec_pieces, q_pieces, mat_parts, ada_w, me):
    nloc = ada_w.shape[1]
    nm = len(mat_parts)
    pieces = list(vec_pieces) + list(q_pieces)

    def body(me_ref, *refs):
        piece_refs, refs = refs[:len(pieces)], refs[len(pieces):]
        mp_refs, w_ref = refs[:nm], refs[nm]
        red_ref, redq_ref = refs[nm + 1:nm + 3]
        mats_all = refs[nm + 3:2 * nm + 3]
        cparts_ref, dmod_ref, gab_ref, loss_ref = refs[2 * nm + 3:2 * nm + 7]
        pack_ref, packq_ref, vp_ref, vq_ref = refs[2 * nm + 7:2 * nm + 11]
        mat_refs = refs[2 * nm + 11:3 * nm + 11]
        cpart_ref, dmc_s = refs[3 * nm + 11:3 * nm + 13]
        sems = refs[3 * nm + 13:]
        for dst, group in ((pack_ref, vec_pieces), (packq_ref, q_pieces)):
            row = 0
            for _, nrows in group:
                dst[row:row + nrows, :] = piece_refs[0][0:nrows, :]
                piece_refs, row = piece_refs[1:], row + nrows
            if row < dst.shape[0]:
                dst[row:, :] = jnp.zeros((dst.shape[0] - row, dst.shape[1]), F32)
        p_start, p_forward, p_finish = _gather2_ops([pack_ref, packq_ref], [vp_ref, vq_ref], ["ag", "ag"], *sems[:3],
                                                    barrier=True)
        m_start, m_forward, m_finish = _gather2_ops(mat_refs, mats_all, ["ag"] * nm, *sems[3:6])
        c_start, c_forward, c_finish = _gather2_ops([cpart_ref], [cparts_ref], ["ag"], *sems[6:])
        p_start()
        for mp_ref, mat_ref in zip(mp_refs, mat_refs):
            mat = mp_ref[0].astype(F32)
            for i in range(1, mp_ref.shape[0]):
                mat = mat + mp_ref[i].astype(F32)
            mat_ref[...] = mat
        m_start()
        p_forward()
        p_finish()
        red, redq = vp_ref[0], vq_ref[0]
        for i in range(1, N_DEV):
            red = red + vp_ref[i]
            redq = redq + vq_ref[i]
        red_ref[...] = red
        redq_ref[...] = redq
        loss_ref[...] = red_ref[R_LOSS:R_LOSS + 1, 0:1]
        for e in range(N_DEV):
            dmod_ref[e:e + 1, 0:D] = vp_ref[e, R_SH_X:R_SH_X + 1, :]
            dmod_ref[e:e + 1, D:2 * D] = vp_ref[e, R_SC_X:R_SC_X + 1, :]
            dmod_ref[e:e + 1, 2 * D:3 * D] = vp_ref[e, R_GATE:R_GATE + 1, :]
        dmod_ref[8:9, 0:D] = red[R_SH_C:R_SH_C + 1, :]
        dmod_ref[8:9, D:2 * D] = red[R_SC_C:R_SC_C + 1, :]
        dmod_ref[8:9, 2 * D:3 * D] = jnp.zeros((1, D), F32)
        dmod_ref[9:16, :] = jnp.zeros((7, 3 * D), F32)
        gab_ref[:, 0:D] = red[R_SH_X:R_SH_X + 1, :] + red[R_SH_C:R_SH_C + 1, :]
        gab_ref[:, D:2 * D] = red[R_SC_X:R_SC_X + 1, :] + red[R_SC_C:R_SC_C + 1, :]
        gab_ref[:, 2 * D:3 * D] = red[R_GATE:R_GATE + 1, :]
        dmc_s[...] = jnp.broadcast_to(dmod_ref[8:9, :], (8, 3 * D))
        off = pl.multiple_of(me_ref[0] * nloc, 128)
        cpart_ref[...] = _dot_nt(dmc_s[:, pl.ds(off, nloc)], w_ref[...])
        c_start()
        m_forward()
        c_forward()
        c_finish()
        m_finish()

    return _call(
        body, name="reduce_small",
        out_shape=[jax.ShapeDtypeStruct((PACK_ROWS, D), F32), jax.ShapeDtypeStruct((PACK128_ROWS, HD), F32)]
        + [jax.ShapeDtypeStruct((N_DEV,) + p.shape[1:], F32) for p in mat_parts]
        + [jax.ShapeDtypeStruct((N_DEV, 8, D), F32), jax.ShapeDtypeStruct((16, 3 * D), F32),
           jax.ShapeDtypeStruct((1, 3 * D), F32), jax.ShapeDtypeStruct((1, 1), F32)],
        in_specs=[pl.BlockSpec(memory_space=pltpu.SMEM)] + [VMEM] * (len(pieces) + nm + 1),
        out_specs=[VMEM] * (nm + 6),
        scratch_shapes=[pltpu.VMEM((PACK_ROWS, D), F32), pltpu.VMEM((PACK128_ROWS, HD), F32),
                        pltpu.VMEM((N_DEV, PACK_ROWS, D), F32), pltpu.VMEM((N_DEV, PACK128_ROWS, HD), F32)]
        + [pltpu.VMEM(p.shape[1:], F32) for p in mat_parts]
        + [pltpu.VMEM((8, D), F32), pltpu.VMEM((8, 3 * D), F32)] + _gather2_sems(2) + _gather2_sems(nm)
        + _gather2_sems(1),
        compiler_params=pltpu.CompilerParams(vmem_limit_bytes=VMEM_LIMIT, has_side_effects=True, collective_id=8),
    )(me, *[a for a, _ in pieces], *mat_parts, ada_w)


def _adamw_ada(c_all, c_ctx, dmod, w, m, v, me):
    nloc = w.shape[1]

    def body(me_ref, c_ref, cc_ref, dm_ref, w_ref, m_ref, v_ref, g_ref, d_ref, mo_ref, vo_ref):
        off = pl.multiple_of(me_ref[0] * nloc, 128)
        dm = dm_ref[:, pl.ds(off, nloc)]
        sx, _ = _silu_and_grad(c_ref[...])
        sc, _ = _silu_and_grad(cc_ref[...])
        g = _dot_tn(sx, dm[0:8, :]) + _dot_tn(jnp.broadcast_to(sc, (8, D)), dm[8:16, :])
        g_ref[...] = g
        d_ref[...], mo_ref[...], vo_ref[...] = _adamw(w_ref[...], g, m_ref[...], v_ref[...])

    sds = jax.ShapeDtypeStruct(w.shape, F32)
    return _call(
        body, name="adamw_ada_w", out_shape=[sds] * 4,
        in_specs=[pl.BlockSpec(memory_space=pltpu.SMEM)] + [VMEM] * 6, out_specs=[VMEM] * 4,
        compiler_params=_params(),
    )(me, c_all, c_ctx, dmod, w, m, v)


_SMALL = ("c_ctx", "ada_b", "norm_g", "conv_w", "conv_b", "lru_wa", "lru_ba", "lru_wx", "lru_bx", "lru_lambda",
          "sgu_ln_g", "sgu_ln_b", "sgu_w", "sgu_b", "final_g")


def _adamw_small(red, redq, mats, cparts, gab, ws, ms, vs, me):
    n = len(_SMALL)

    def body(me_ref, red_ref, redq_ref, wa_ref, wx_ref, sw_ref, cp_ref, gab_ref, *refs):
        w_refs, m_refs, v_refs = refs[:n], refs[n:2 * n], refs[2 * n:3 * n]
        outs = refs[3 * n:]
        off = pl.multiple_of(me_ref[0] * HD, 128)

        def row(r, k=1):
            return red_ref[r:r + k, :]

        cc = w_refs[0][...]
        dcc = cp_ref[0, 0:1, :]
        for i in range(1, N_DEV):
            dcc = dcc + cp_ref[i, 0:1, :]
        grads = dict(
            c_ctx=dcc * _silu_and_grad(cc)[1], ada_b=gab_ref[...], norm_g=row(R_NG_X) + row(R_NG_C),
            conv_w=red_ref[R_CW:R_CW + CONV_W, pl.ds(off, HD)], conv_b=row(R_CB),
            lru_wa=wa_ref[...], lru_ba=redq_ref[Q_BA:Q_BA + 2 * HEADS, :], lru_wx=wx_ref[...],
            lru_bx=redq_ref[Q_BX:Q_BX + 2 * HEADS, :], lru_lambda=red_ref[R_LAM:R_LAM + 2, pl.ds(off, HD)],
            sgu_ln_g=row(R_LN_G), sgu_ln_b=row(R_LN_B), sgu_w=sw_ref[...],
            sgu_b=redq_ref[Q_SGU_B:Q_SGU_B + HEADS, :], final_g=row(R_FINAL_G))
        for j, name in enumerate(_SMALL):
            g = grads[name]
            outs[j][...] = g
            outs[n + j][...], outs[2 * n + j][...], outs[3 * n + j][...] = _adamw(w_refs[j][...], g, m_refs[j][...],
                                                                                 v_refs[j][...])

    sds = [jax.ShapeDtypeStruct(ws[k].shape, F32) for k in _SMALL]
    outs = _call(
        body, name="adamw_small", out_shape=sds * 4,
        in_specs=[pl.BlockSpec(memory_space=pltpu.SMEM)] + [VMEM] * (7 + 3 * n), out_specs=[VMEM] * (4 * n),
        compiler_params=_params(),
    )(me, red, redq, *mats, cparts, gab, *[ws[k] for k in _SMALL], *[ms[k] for k in _SMALL],
      *[vs[k] for k in _SMALL])
    return [dict(zip(_SMALL, outs[i * n:(i + 1) * n])) for i in range(4)]


def kernel(x, c, ctx, c_ctx, ada_w, ada_b, norm_g, w_in, conv_w, conv_b, lru_wa, lru_ba, lru_wx, lru_bx, lru_lambda, sgu_ln_g, sgu_ln_b, sgu_w, sgu_b, w_out, final_g, loss_target, m_c_ctx, m_ada_w, m_ada_b, m_norm_g, m_w_in, m_conv_w, m_conv_b, m_lru_wa, m_lru_ba, m_lru_wx, m_lru_bx, m_lru_lambda, m_sgu_ln_g, m_sgu_ln_b, m_sgu_w, m_sgu_b, m_w_out, m_final_g, v_c_ctx, v_ada_w, v_ada_b, v_norm_g, v_w_in, v_conv_w, v_conv_b, v_lru_wa, v_lru_ba, v_lru_wx, v_lru_bx, v_lru_lambda, v_sgu_ln_g, v_sgu_ln_b, v_sgu_w, v_sgu_b, v_w_out, v_final_g):
    args = dict(locals())
    me = (4 * lax.axis_index("x") + 2 * lax.axis_index("y") + lax.axis_index("c")).astype(jnp.int32).reshape(1)
    xr, ctxr, tgt = x[0], ctx[0], loss_target[0]
    cc = c_ctx.reshape(1, D)
    nw = 2 * HEADS * HD
    view = dict(c_ctx=(1, D), ada_b=(1, 3 * D), norm_g=(1, D), conv_w=(CONV_W, HD), conv_b=(1, D), lru_wa=(nw, HD),
                lru_ba=(2 * HEADS, HD), lru_wx=(nw, HD), lru_bx=(2 * HEADS, HD), lru_lambda=(2, HD), sgu_ln_g=(1, D),
                sgu_ln_b=(1, D), sgu_w=(HEADS * CHUNK, CHUNK), sgu_b=(HEADS, CHUNK), final_g=(1, D))

    zx, hn, w_full, w_out_b, modx, modc, c_all, cw_full, lam_full = _front_project(
        xr, c, cc, ada_w[0], ada_b, norm_g, w_in[0], w_out[0], conv_w[0], lru_lambda[0], me)
    zc, hnc = _project(ctxr, modc, norm_g, w_full, D, LC, "project_ctx")
    ba, bx = lru_ba.reshape(view["lru_ba"]), lru_bx.reshape(view["lru_bx"])
    yl, wout_all = _lru_forward(zx, zc, cw_full, conv_b, lru_wa[0], lru_wx[0], ba, bx, lam_full, [w_out_b], ["ag"])
    wout_full = wout_all.reshape(D_MIX, D)
    dz, dyl, dxn, ycat, dob, dws, dbst, mvec = _mixer_loss(
        xr, tgt, zx, yl, modx, final_g.reshape(1, D), sgu_ln_g, sgu_ln_b, sgu_w[0], sgu_b[0], wout_full, ROWS)

    (wout_sums,) = _grad_w(ycat, dob, None, None, L, "grad_w_out", D, 0, 1, "rows", 1)
    (rest_sums,) = _grad_w(hn, dz, None, None, L, "grad_w_in_rest", 2 * W_IN_SHARD, 1, 3, "cols", 2)
    dz, dxac, dwa, dwx, dba, dbx, dlam, dcw, dcb, win_parts, wout_parts, _, _ = _lru_backward(
        zx, zc, dyl, dz, cw_full, conv_b, lru_wa[0], lru_wx[0], ba, bx, lam_full, [rest_sums, wout_sums],
        first_chips=[1, 0])
    mats = [dwa.reshape(N_DEV, nw // N_DEV, HD), dwx.reshape(N_DEV, nw // N_DEV, HD), dws]
    first_sums, *mat_parts = _grad_w(hn, dz, hnc, dxac, L, "grad_w_in_first", 2 * W_IN_SHARD, 0, 1, "cols", 3,
                                     riders=mats)[:4]
    gx, xvec, win_parts = _grad_rows(
        xr, dz, w_full, modx, norm_g, dxn, D_IN, ROWS, "grad_rows_x", chip_sums=[first_sums], first_chips=[0],
        dests=[win_parts])[:3]
    (cvec,) = _grad_rows(ctxr, dxac, w_full, modc, norm_g, None, D, LC, "grad_rows_ctx")
    red, redq, *rest = _reduce_small(
        [(mvec, 5), (xvec, 3), (cvec, 3), (dlam, 2), (dcw, CONV_W), (dcb, 1)],
        [(dba, 2 * HEADS), (dbx, 2 * HEADS), (dbst, HEADS)], mat_parts, ada_w[0], me)
    mats_all, (cparts, dmod, gab, loss) = rest[:3], rest[3:]

    g_w_in, d_w_in, nm_w_in, nv_w_in = _adamw_reduced(win_parts, w_in[0], m_w_in[0], v_w_in[0], 2 * ROWS, "adamw_w_in")
    g_w_out, d_w_out, nm_w_out, nv_w_out = _adamw_reduced(wout_parts, w_out[0], m_w_out[0], v_w_out[0], ROWS // 2,
                                                          "adamw_w_out")
    g_ada, d_ada, nm_ada, nv_ada = _adamw_ada(c_all, cc, dmod, ada_w[0], m_ada_w[0], v_ada_w[0], me)
    ws = {k: args[k].reshape(view[k]) for k in _SMALL}
    ms = {k: args["m_" + k].reshape(view[k]) for k in _SMALL}
    vs = {k: args["v_" + k].reshape(view[k]) for k in _SMALL}
    small = _adamw_small(red, redq, [m.reshape(-1, HD) for m in mats_all], cparts, gab, ws, ms, vs, me)
    big = dict(w_in=(g_w_in, d_w_in, nm_w_in, nv_w_in), w_out=(g_w_out, d_w_out, nm_w_out, nv_w_out),
               ada_w=(g_ada, d_ada, nm_ada, nv_ada))

    loss = loss.reshape(())
    names = ("c_ctx", "ada_w", "ada_b", "norm_g", "w_in", "conv_w", "conv_b", "lru_wa", "lru_ba", "lru_wx", "lru_bx",
             "lru_lambda", "sgu_ln_g", "sgu_ln_b", "sgu_w", "sgu_b", "w_out", "final_g")
    outs = [loss, gx.reshape(x.shape)]
    for kind in range(4):
        for k in names:
            val = big[k][kind] if k in big else small[kind][k]
            outs.append(val.reshape(args[k].shape))
    return tuple(outs)
```

```python
import jax
import jax.numpy as jnp
from jax import lax
from jax.experimental import pallas as pl
from jax.experimental.pallas import tpu as pltpu

F32 = jnp.float32
BF16 = jnp.bfloat16

N_DEV = 8
D = 1024
L = 2048
LC = 256
HEADS = 8
HD = 128
CHUNK = 128
D_IN = 5 * D
W_IN_SHARD = D_IN // N_DEV
ROWS = 256
D_MIX = 2 * D
CONV_W = 4
LRU_C = 8.0
NORM_EPS = 1e-6
LN_EPS = 1e-5
ADAM_LR, ADAM_B1, ADAM_B2, ADAM_EPS, ADAM_WD, ADAM_STEP = 0.001, 0.9, 0.999, 1e-08, 0.01, 10

VMEM_LIMIT = 56 * 1024 * 1024

HBM = pl.BlockSpec(memory_space=pltpu.HBM)
VMEM = pl.BlockSpec(memory_space=pltpu.VMEM)
MESH = pl.DeviceIdType.MESH


def _call(body, **kw):
    return pl.pallas_call(body, **kw)


def _params(*sem):
    return pltpu.CompilerParams(dimension_semantics=sem, vmem_limit_bytes=VMEM_LIMIT)


def _sigmoid(x):
    return 0.5 * jnp.tanh(0.5 * x) + 0.5


def _silu_and_grad(x):
    s = _sigmoid(x)
    return x * s, s * (1.0 + x * (1.0 - s))


_G0 = 0.7978845608028654
_G1 = 0.044715


def _gelu_and_grad(x):
    x2 = x * x
    t = jnp.tanh(_G0 * (x + _G1 * x * x2))
    cdf = 0.5 * (1.0 + t)
    return x * cdf, cdf + 0.5 * x * (1.0 - t * t) * (_G0 * (1.0 + 3.0 * _G1 * x2))


def _softplus(z):
    t = jnp.exp(-jnp.abs(z))
    u = 1.0 + t
    log1p = jnp.where(u == 1.0, t, jnp.log(u) * t / jnp.where(u == 1.0, 1.0, u - 1.0))
    return jnp.maximum(z, 0.0) + log1p


def _dot(a, b):
    return jnp.dot(a, b, preferred_element_type=F32)


def _dot_nt(a, b):
    return lax.dot_general(a, b, (((1,), (1,)), ((), ())), preferred_element_type=F32)


def _dot_tn(a, b):
    return lax.dot_general(a, b, (((0,), (0,)), ((), ())), preferred_element_type=F32)


def _rows(shape):
    return lax.broadcasted_iota(jnp.int32, shape, 0)


def _gather2_shapes(arrays, modes):
    return [jax.ShapeDtypeStruct((N_DEV,) + a.shape if m == "ag" else (a.shape[0], N_DEV * a.shape[1]), a.dtype)
            for a, m in zip(arrays, modes)]


def _gather2_sems(n):
    return [pltpu.SemaphoreType.DMA((n, N_DEV - 1)), pltpu.SemaphoreType.DMA((n, N_DEV - 1)),
            pltpu.SemaphoreType.DMA((n,))]


def _barrier(peers):
    sem = pltpu.get_barrier_semaphore()
    for peer in peers:
        pl.semaphore_signal(sem, inc=1, device_id=peer, device_id_type=MESH)
    pl.semaphore_wait(sem, len(peers))


def _gather2_ops(ins, outs, modes, send_sems, recv_sems, local_sems, barrier=False):
    n = len(ins)
    x, y, c = lax.axis_index("x"), lax.axis_index("y"), lax.axis_index("c")
    me, sibling = (x, y, c), (x, y, 1 - c)
    chips = [(x ^ (k >> 1), y ^ (k & 1)) for k in (1, 2, 3)]

    def slot(j, px, py, pc):
        dev = 4 * px + 2 * py + pc
        if modes[j] == "agc":
            w = ins[j].shape[1]
            return outs[j].at[:, pl.ds(pl.multiple_of(dev * w, 128), w)]
        return outs[j].at[dev]

    def copy(j, k, block, to, src=None):
        return pltpu.make_async_remote_copy(
            src_ref=slot(j, *block) if src is None else src, dst_ref=slot(j, *block),
            send_sem=send_sems.at[j, k], recv_sem=recv_sems.at[j, k], device_id=to, device_id_type=MESH)

    def own(j):
        return pltpu.make_async_copy(ins[j], slot(j, *me), local_sems.at[j])

    def first(j):
        return [copy(j, 0, me, sibling, src=ins[j])] + [copy(j, 1 + i, me, (*chip, c), src=ins[j])
                                                        for i, chip in enumerate(chips)]

    def passed(j, i):
        return copy(j, 4 + i, (*chips[i], c), sibling)

    def start():
        if barrier:
            _barrier([sibling] + [(*chip, c) for chip in chips])
        for j in range(n):
            own(j).start()
            for cp in first(j):
                cp.start()

    def forward():
        for i, chip in enumerate(chips):
            for j in range(n):
                copy(j, 1 + i, (*chip, c), me).wait_recv()
                passed(j, i).start()

    def finish():
        for j in range(n):
            copy(j, 0, sibling, me).wait_recv()
            for i, chip in enumerate(chips):
                copy(j, 4 + i, (*chip, 1 - c), me).wait_recv()
            for cp in first(j) + [passed(j, i) for i in range(3)]:
                cp.wait_send()
            own(j).wait()

    return start, forward, finish


def _sibling_barrier():
    sem = pltpu.get_barrier_semaphore()
    sibling = (lax.axis_index("x"), lax.axis_index("y"), 1 - lax.axis_index("c"))
    pl.semaphore_signal(sem, inc=1, device_id=sibling, device_id_type=MESH)
    pl.semaphore_wait(sem, 1)


def _chips_sems(n):
    return [pltpu.SemaphoreType.DMA((n, 6)), pltpu.SemaphoreType.DMA((n, 6)), pltpu.SemaphoreType.DMA((n,))]


def _chips_stage_shapes(chip_sums):
    return [jax.ShapeDtypeStruct((2, a.shape[1] // 2, a.shape[2]), a.dtype) for a in chip_sums]


def _chips_ops(ins, outs, stages, send_sems, recv_sems, local_sems, first_chips=None, barrier=False):
    x, y, c = lax.axis_index("x"), lax.axis_index("y"), lax.axis_index("c")
    qm = 2 * x + y
    first_chips = first_chips or [0] * len(ins)

    def owns(j, chip):
        lo, cnt = first_chips[j], ins[j].shape[0]
        if lo == 0 and cnt == 4:
            return None
        return jnp.logical_and(chip >= lo, chip < lo + cnt)

    def guarded(cond, fn):
        if cond is None:
            fn()
        else:
            pl.when(cond)(fn)

    def slot(j, chip):
        return jnp.clip(chip - first_chips[j], 0, ins[j].shape[0] - 1)

    def half(j, i):
        h = ins[j].shape[1] // 2
        return pl.ds(i * h, h)

    def copy(j, sem, src, dst, k):
        return pltpu.make_async_remote_copy(
            src_ref=src, dst_ref=dst, send_sem=send_sems.at[j, sem], recv_sem=recv_sems.at[j, sem],
            device_id=(x ^ (k >> 1), y ^ (k & 1), c), device_id_type=MESH)

    def direct(j, k):
        return copy(j, k - 1, ins[j].at[slot(j, qm ^ k)], outs[j].at[qm], k)

    def first_hop(j, k):
        return copy(j, 1 + k, ins[j].at[slot(j, qm ^ 3), half(j, k - 1)], stages[j].at[k - 1], k)

    def second_hop(j, k):
        return copy(j, 3 + k, stages[j].at[2 - k], outs[j].at[qm ^ (3 - k), half(j, 2 - k)], k)

    def local(j):
        return pltpu.make_async_copy(ins[j].at[slot(j, qm)], outs[j].at[qm], local_sems.at[j])

    def start():
        if barrier:
            _barrier([(x ^ (k >> 1), y ^ (k & 1), c) for k in (1, 2)])
        for j in range(len(ins)):
            for k in (1, 2):
                guarded(owns(j, qm ^ 3), lambda j=j, k=k: first_hop(j, k).start())
        for j in range(len(ins)):
            for k in (1, 2):
                guarded(owns(j, qm ^ k), lambda j=j, k=k: direct(j, k).start())
            guarded(owns(j, qm), lambda j=j: local(j).start())

    def forward():
        for j in range(len(ins)):
            for k in (1, 2):
                def pass_on(j=j, k=k):
                    first_hop(j, 3 - k).wait_recv()
                    second_hop(j, k).start()
                guarded(owns(j, qm ^ k), pass_on)

    def finish():
        for j in range(len(ins)):
            for k in (1, 2):
                guarded(owns(j, qm ^ k), lambda j=j, k=k: direct(j, k).wait_send())
                guarded(owns(j, qm ^ k), lambda j=j, k=k: second_hop(j, k).wait_send())
                guarded(owns(j, qm ^ 3), lambda j=j, k=k: first_hop(j, k).wait_send())
                guarded(owns(j, qm), lambda j=j, k=k: direct(j, k).wait_recv())
                guarded(owns(j, qm), lambda j=j, k=k: second_hop(j, k).wait_recv())
            guarded(owns(j, qm), lambda j=j: local(j).wait())

    return start, forward, finish


ARRIVAL = (0, 1, 2, 4, 3, 5, 6, 7)


def _front_project(xr, c, c_ctx, ada_w, ada_b, ng, w_in, w_out, cw, lam, me):
    nloc = ada_w.shape[1]
    ws = W_IN_SHARD
    arrival = jnp.asarray(ARRIVAL, jnp.int32)

    def body(me_ref, arr_ref, x_ref, c_ref, cc_ref, aw_ref, ab_ref, ng_ref, win_ref, wout_ref, cw_ref, lam_ref,
             z_ref, hn_ref, wfull_ref, woutb_ref, modx_ref, modc_ref, call_ref, cwf_ref, lamf_ref,
             wv, call_s, part_s, parts_s, w_send, w_recv, hbm_sems, s_send, s_recv, g_send, g_recv, g_local,
             x_v, x_sem):
        t = pl.program_id(0)
        x_load = pltpu.make_async_copy(x_ref, x_v, x_sem)
        x, y, cidx = lax.axis_index("x"), lax.axis_index("y"), lax.axis_index("c")
        me_i = me_ref[0]
        sibling = (x, y, 1 - cidx)
        chips = [(x ^ (k >> 1), y ^ (k & 1)) for k in (1, 2, 3)]
        g_start, g_pass, g_finish = _gather2_ops([cw_ref, lam_ref], [cwf_ref, lamf_ref], ["agc", "agc"],
                                                 g_send, g_recv, g_local)

        def shard_copy(k, px, py, pc, to, half=None):
            slot = wv.at[4 * px + 2 * py + pc]
            if half is not None:
                slot = slot.at[pl.ds(half * (D // 2), D // 2), :]
            return pltpu.make_async_remote_copy(src_ref=slot, dst_ref=slot, send_sem=w_send.at[k],
                                                recv_sem=w_recv.at[k], device_id=to, device_id_type=MESH)

        def small_gather(src, my_slot, stage):
            copies = []
            for k in range(1, N_DEV):
                peer = (x ^ (k >> 2), y ^ ((k >> 1) & 1), cidx ^ (k & 1))
                cp = pltpu.make_async_remote_copy(src_ref=src, dst_ref=my_slot, send_sem=s_send.at[stage, k - 1],
                                                  recv_sem=s_recv.at[stage, k - 1], device_id=peer,
                                                  device_id_type=MESH)
                cp.start()
                copies.append(cp)
            pltpu.sync_copy(src, my_slot)
            return copies

        def finish_small(copies):
            for cp in copies:
                cp.wait()

        def to_neighbours(half):
            for i in (0, 1):
                shard_copy(1 + i, x, y, cidx, (*chips[i], cidx), half=half).start()

        @pl.when(t == 0)
        def _():
            _barrier([(x ^ (k >> 2), y ^ ((k >> 1) & 1), cidx ^ (k & 1)) for k in range(1, N_DEV)])
            g_start()
            x_load.start()
            wv[me_i] = win_ref[...].astype(BF16)
            woutb_ref[...] = wout_ref[...].astype(BF16)
            shard_copy(0, x, y, cidx, sibling).start()
            conds_sent = small_gather(c_ref, call_s.at[pl.ds(me_i, 1), :], 0)
            to_neighbours(0)
            finish_small(conds_sent)
            call_ref[...] = call_s[...]
            off = pl.multiple_of(me_i * nloc, 128)
            b = ab_ref[:, pl.ds(off, nloc)]
            w = aw_ref[...]
            sx, _ = _silu_and_grad(call_s[...])
            sc, _ = _silu_and_grad(jnp.broadcast_to(cc_ref[...], (8, D)))
            part_s[0:8, :] = _dot(sx, w) + b
            part_s[8:16, :] = _dot(sc, w) + b
            parts_sent = small_gather(part_s, parts_s.at[me_i], 1)
            to_neighbours(1)
            finish_small(parts_sent)
            mine = _rows((16, nloc)) == me_i
            for j in range(N_DEV):
                pj = parts_s[j]
                modx_ref[:, j * nloc:(j + 1) * nloc] = jnp.sum(jnp.where(mine, pj, 0.0), axis=0, keepdims=True)
                modc_ref[:, j * nloc:(j + 1) * nloc] = pj[8:9, :]
            shift, scale1, ngv = modx_ref[:, 0:D], 1.0 + modx_ref[:, D:2 * D], ng_ref[...]
            x_load.wait()
            for r in range(L // ROWS):
                rsl = slice(r * ROWS, (r + 1) * ROWS)
                xv = x_v[rsl, :]
                rs = lax.rsqrt(jnp.mean(xv * xv, axis=-1, keepdims=True) + NORM_EPS)
                hn_ref[rsl, :] = ((xv * rs * ngv) * scale1 + shift).astype(BF16)

        @pl.when(t == 1)
        def _():
            shard_copy(0, x, y, 1 - cidx, sibling).wait_recv()
            g_pass()

        for i in (0, 1):
            @pl.when(t == ARRIVAL.index((2, 4)[i]))
            def _(i=i):
                shard_copy(1 + i, *chips[i], cidx, sibling).wait_recv()
                shard_copy(4 + i, *chips[i], cidx, sibling).start()
                shard_copy((7, 3)[i], *chips[i], cidx, (*chips[1 - i], cidx), half=i).start()

        @pl.when(t == ARRIVAL.index(6))
        def _():
            shard_copy(3, *chips[2], cidx, sibling, half=1).wait_recv()
            shard_copy(7, *chips[2], cidx, sibling, half=0).wait_recv()
            shard_copy(6, *chips[2], cidx, sibling).start()

        for i in range(3):
            @pl.when(t == ARRIVAL.index((3, 5, 7)[i]))
            def _(i=i):
                shard_copy(4 + i, *chips[i], 1 - cidx, sibling).wait_recv()

        @pl.when(t == 2)
        def _():
            g_finish()

        dev = me_i ^ arr_ref[t]
        for r in range(L // (2 * ROWS)):
            rsl = slice(r * 2 * ROWS, (r + 1) * 2 * ROWS)
            z_ref[rsl, :] = _dot(hn_ref[rsl, :], wv[dev])
        col = pl.ds(pl.multiple_of(dev * ws, 128), ws)
        pltpu.make_async_copy(wv.at[dev], wfull_ref.at[:, col], hbm_sems.at[t]).start()

        @pl.when(t == N_DEV - 1)
        def _():
            for k in (0, 1, 2, 4, 5, 6):
                shard_copy(k, x, y, cidx, sibling).wait_send()
            for k in (3, 7):
                shard_copy(k, x, y, cidx, sibling, half=0).wait_send()
            for s in range(N_DEV):
                pltpu.make_async_copy(wv.at[0], wfull_ref.at[:, pl.ds(0, ws)], hbm_sems.at[s]).wait()

    const = lambda *shape: pl.BlockSpec(shape, lambda t, m, a: (0,) * len(shape))
    once = lambda *shape: pl.BlockSpec(shape, lambda t, m, a: (0,) * len(shape), pipeline_mode=pl.Buffered(1))
    return _call(
        body, name="front_project",
        out_shape=[jax.ShapeDtypeStruct((L, D_IN), F32), jax.ShapeDtypeStruct((L, D), BF16),
                   jax.ShapeDtypeStruct((D, D_IN), BF16), jax.ShapeDtypeStruct(w_out.shape, BF16),
                   jax.ShapeDtypeStruct((1, 3 * D), F32), jax.ShapeDtypeStruct((1, 3 * D), F32),
                   jax.ShapeDtypeStruct((N_DEV, D), F32), jax.ShapeDtypeStruct((CONV_W, D), F32),
                   jax.ShapeDtypeStruct((2, D), F32)],
        grid_spec=pltpu.PrefetchScalarGridSpec(
            num_scalar_prefetch=2, grid=(N_DEV,),
            in_specs=[HBM, const(1, D), const(1, D), once(D, nloc), const(1, 3 * D), const(1, D),
                      once(D, ws), once(*w_out.shape), HBM, HBM],
            out_specs=[pl.BlockSpec((L, ws), lambda t, m, a: (0, m[0] ^ a[t])), const(L, D), HBM,
                       const(*w_out.shape),
                       const(1, 3 * D), const(1, 3 * D), const(N_DEV, D), HBM, HBM],
            scratch_shapes=[pltpu.VMEM((N_DEV, D, ws), BF16), pltpu.VMEM((N_DEV, D), F32), pltpu.VMEM((16, nloc), F32),
                            pltpu.VMEM((N_DEV, 16, nloc), F32), pltpu.SemaphoreType.DMA((8,)),
                            pltpu.SemaphoreType.DMA((8,)), pltpu.SemaphoreType.DMA((N_DEV,)),
                            pltpu.SemaphoreType.DMA((2, N_DEV - 1)), pltpu.SemaphoreType.DMA((2, N_DEV - 1))]
            + _gather2_sems(2) + [pltpu.VMEM((L, D), F32), pltpu.SemaphoreType.DMA(())]),
        compiler_params=pltpu.CompilerParams(dimension_semantics=("arbitrary",), vmem_limit_bytes=VMEM_LIMIT,
                                             has_side_effects=True, collective_id=10),
    )(me, arrival, pltpu.with_memory_space_constraint(xr, pltpu.HBM), c, c_ctx, ada_w, ada_b, ng, w_in, w_out,
      pltpu.with_memory_space_constraint(cw, pltpu.HBM), pltpu.with_memory_space_constraint(lam, pltpu.HBM))


def _project(xr, mod, ng, w, ncols, tm, name):
    rows = xr.shape[0]

    def body(x_ref, sh_ref, sc_ref, ng_ref, w_ref, z_ref, hn_ref):
        x = x_ref[...]
        rs = lax.rsqrt(jnp.mean(x * x, axis=-1, keepdims=True) + NORM_EPS)
        hn = (x * rs * ng_ref[...]) * (1.0 + sc_ref[...]) + sh_ref[...]
        hb = hn.astype(BF16)
        hn_ref[...] = hb
        for n in range(ncols // D):
            z_ref[:, n * D:(n + 1) * D] = _dot(hb, w_ref[:, n * D:(n + 1) * D])

    vec = pl.BlockSpec((1, D), lambda i: (0, 0))
    return _call(
        body, name=name, grid=(rows // tm,),
        out_shape=[jax.ShapeDtypeStruct((rows, ncols), F32), jax.ShapeDtypeStruct((rows, D), BF16)],
        in_specs=[pl.BlockSpec((tm, D), lambda i: (i, 0)), vec, pl.BlockSpec((1, D), lambda i: (0, 1)), vec,
                  pl.BlockSpec((D, ncols), lambda i: (0, 0), pipeline_mode=pl.Buffered(1))],
        out_specs=[pl.BlockSpec((tm, ncols), lambda i: (i, 0)), pl.BlockSpec((tm, D), lambda i: (i, 0))],
        compiler_params=_params("arbitrary"),
    )(xr, mod, mod, ng, w)


def _scan_pair(af_ref, uf_ref, hf_ref, h0f, ab_ref, ub_ref, hb_ref, h0b, t_len):
    span = 8 * SCAN_BLOCKS
    nit = t_len // span
    rows = _rows((8, HD))

    def local_scan(a, b, forward):
        for s in (1, 2, 4):
            sh = s if forward else 8 - s
            m = rows >= s if forward else rows < 8 - s
            b = a * jnp.where(m, pltpu.roll(b, sh, 0), 0.0) + b
            a = a * jnp.where(m, pltpu.roll(a, sh, 0), 1.0)
        return a, b

    def span_scan(a_ref, u_ref, h_ref, off, carry, forward):
        order = range(SCAN_BLOCKS) if forward else range(SCAN_BLOCKS - 1, -1, -1)
        last = slice(7, 8) if forward else slice(0, 1)
        for q in order:
            rs = pl.ds(off + 8 * q, 8)
            a, b = local_scan(a_ref[rs, :], u_ref[rs, :], forward)
            h_ref[rs, :] = b + a * carry
            carry = a[last, :] * carry + b[last, :]
        return carry

    def body(k, carry):
        cf, cb = carry
        cf = span_scan(af_ref, uf_ref, hf_ref, pl.multiple_of(k * span, span), cf, True)
        cb = span_scan(ab_ref, ub_ref, hb_ref, pl.multiple_of((nit - 1 - k) * span, span), cb, False)
        return cf, cb

    return lax.fori_loop(0, nit, body, (h0f, h0b))


SCAN_BLOCKS = 16


def _shifted(pad_ref, x, offsets, before=0.0, after=0.0):
    n = x.shape[0]
    pad_ref[0:8, :] = jnp.broadcast_to(jnp.asarray(before, F32), (8, x.shape[1]))
    pad_ref[8:8 + n, :] = x
    pad_ref[8 + n:16 + n, :] = jnp.broadcast_to(jnp.asarray(after, F32), (8, x.shape[1]))
    return [pad_ref[8 + o:8 + o + n, :] for o in offsets]


def _conv(xa, cw, cb, pad_ref):
    xm1, xp1, xp2 = _shifted(pad_ref, xa, (-1, 1, 2))
    return xm1 * cw[0:1, :] + xa * cw[1:2, :] + xp1 * cw[2:3, :] + xp2 * cw[3:4, :] + cb


def _gates(xc, wa, wx, ba, bx, nsp):
    xb = xc.astype(BF16)
    r = _sigmoid(_dot(xb, wa) + ba)
    i = _sigmoid(_dot(xb, wx) + bx)
    log_a = r * nsp
    a = jnp.exp(log_a)
    g2 = jnp.tanh(log_a) * (-1.0 - a * a)
    rg = lax.rsqrt(jnp.maximum(g2, 1e-30))
    return r, i, a, g2 * rg, rg


def _lru_param_specs():
    h4 = pl.BlockSpec((2, 1, HD, HD), lambda h: (0, h, 0, 0))
    v2 = pl.BlockSpec((2, HD), lambda h: (0, h))
    b16 = pl.BlockSpec((2 * HEADS, HD), lambda h: (0, 0))
    return dict(
        xa=pl.BlockSpec((L, HD), lambda h: (0, h)), xac=pl.BlockSpec((LC, HD), lambda h: (0, h)),
        cw=pl.BlockSpec((CONV_W, HD), lambda h: (0, h)), cb=pl.BlockSpec((1, HD), lambda h: (0, h)), h4=h4, v2=v2,
        b16=b16)


def _bias_row(ref, d):
    mask = _rows((2 * HEADS, HD)) == d * HEADS + pl.program_id(0)
    return jnp.sum(jnp.where(mask, ref[...], 0.0), axis=0, keepdims=True), mask


def _lru_forward(zx, zc, cw, cb, wa, wx, ba, bx, lam, gather, gather_modes):
    ng_ = len(gather)

    def body(xa_ref, xac_ref, cw_ref, cb_ref, wa_ref, wx_ref, ba_ref, bx_ref, lam_ref, *rest):
        yl_ref = rest[ng_]
        af, uf, hf, ab, ub, hb, pad_s = rest[2 * ng_ + 1:2 * ng_ + 8]
        start, pass_on, finish = _gather2_ops(rest[:ng_], rest[ng_ + 1:2 * ng_ + 1], gather_modes,
                                              *rest[2 * ng_ + 8:], barrier=True)
        pl.when(pl.program_id(0) == 0)(start)
        pl.when(pl.program_id(0) == HEADS // 2)(pass_on)
        pl.when(pl.program_id(0) == HEADS - 1)(finish)
        cwv, cbv = cw_ref[...], cb_ref[...]
        nsp = (-LRU_C) * _softplus(-lam_ref[...])

        def forward(xa, t_len, h0f, h0b):
            xc = _conv(xa, cwv, cbv, pad_s)
            for d, (a_ref, u_ref) in enumerate(((af, uf), (ab, ub))):
                _, i, a, gamma, _ = _gates(xc, wa_ref[d, 0].astype(BF16), wx_ref[d, 0].astype(BF16),
                                           _bias_row(ba_ref, d)[0], _bias_row(bx_ref, d)[0], nsp[d:d + 1, :])
                a_ref[0:t_len, :] = a
                u_ref[0:t_len, :] = gamma * (i * xc)
            return _scan_pair(af, uf, hf, h0f, ab, ub, hb, h0b, t_len)

        z = jnp.zeros((1, HD), F32)
        h0f, h0b = forward(xac_ref[...], LC, z, z)
        forward(xa_ref[...], L, h0f, h0b)
        yl_ref[...] = hf[...] + hb[...]

    s = _lru_param_specs()
    return _call(
        body, name="lru_forward", grid=(HEADS,),
        out_shape=[jax.ShapeDtypeStruct((L, D), F32)] + _gather2_shapes(gather, gather_modes),
        in_specs=[s["xa"], s["xac"], s["cw"], s["cb"], s["h4"], s["h4"], s["b16"], s["b16"], s["v2"]] + [HBM] * ng_,
        out_specs=[pl.BlockSpec((L, HD), lambda h: (0, h))] + [HBM] * ng_,
        scratch_shapes=[pltpu.VMEM((L, HD), F32)] * 6 + [pltpu.VMEM((L + 16, HD), F32)] + _gather2_sems(ng_),
        compiler_params=pltpu.CompilerParams(dimension_semantics=("arbitrary",), vmem_limit_bytes=VMEM_LIMIT,
                                             has_side_effects=True, collective_id=5),
    )(zx, zc, cw, cb, wa, wx, ba, bx, lam, *[pltpu.with_memory_space_constraint(a, pltpu.HBM) for a in gather])


def _lru_backward(zx, zc, dyl, dz, cw, cb, wa, wx, ba, bx, lam, chip_sums, first_chips=None):
    nr = len(chip_sums)

    def body(xa_ref, xac_ref, dyl_ref, dz_in, cw_ref, cb_ref, wa_ref, wx_ref, ba_ref, bx_ref, lam_ref, *rest):
        (dxa_ref, dxac_ref, dwa_ref, dwx_ref, dba_ref, dbx_ref, dlam_ref, dcw_ref,
         dcb_ref) = rest[nr:nr + 9]
        main_s, ctx_s, pad_s = rest[3 * nr + 9:3 * nr + 12]
        if nr:
            start, forward, finish = _chips_ops(rest[:nr], rest[nr + 9:2 * nr + 9], rest[2 * nr + 9:3 * nr + 9],
                                                *rest[3 * nr + 12:], first_chips=first_chips, barrier=True)
            pl.when(pl.program_id(0) == 0)(start)
            pl.when(pl.program_id(0) == HEADS // 2)(forward)
            pl.when(pl.program_id(0) == HEADS - 1)(finish)
        del dz_in

        @pl.when(pl.program_id(0) == 0)
        def _():
            dba_ref[...] = jnp.zeros_like(dba_ref)
            dbx_ref[...] = jnp.zeros_like(dbx_ref)

        cwv, cbv = cw_ref[...], cb_ref[...]
        lamv = lam_ref[...]
        sp = _softplus(-lamv)
        nsp = (-LRU_C) * sp
        z = jnp.zeros((1, HD), F32)

        def wmat(ref, d):
            return ref[d, 0].astype(BF16)

        def workspace(s):
            return dict(a=(s.at[0], s.at[1]), u=(s.at[2], s.at[3]), h=(s.at[4], s.at[5]), rho=(s.at[6], s.at[7]),
                        saved=(tuple(s.at[8 + k] for k in range(4)), tuple(s.at[12 + k] for k in range(4))),
                        xc=s.at[16])

        def forward(ws, xa, t_len, h0f, h0b):
            xc = _conv(xa, cwv, cbv, pad_s)
            ws["xc"][...] = xc
            for d in (0, 1):
                vals = _gates(xc, wmat(wa_ref, d), wmat(wx_ref, d), _bias_row(ba_ref, d)[0],
                              _bias_row(bx_ref, d)[0], nsp[d:d + 1, :])
                r, i, a, gamma, rg = vals
                ws["a"][d][...] = a
                ws["u"][d][...] = gamma * (i * xc)
                for ref, val in zip(ws["saved"][d], (r, i, gamma, rg)):
                    ref[...] = val
            return _scan_pair(ws["a"][0], ws["u"][0], ws["h"][0], h0f, ws["a"][1], ws["u"][1], ws["h"][1], h0b,
                              t_len)

        def backward(ws, xa, t_len, h0f, h0b, dhf, dhb, first):
            xc = ws["xc"][...]
            (af, ab), (uf, ub), (hf, hb), (rf, rb) = ws["a"], ws["u"], ws["h"], ws["rho"]
            uf[...] = ab[...] * dhb
            ub[...] = af[...] * dhf
            rho_b_last, rho_f_first = _scan_pair(ab, uf, rb, z, af, ub, rf, z, t_len)
            dxc = jnp.zeros((t_len, HD), F32)
            dsp = []
            for d in (0, 1):
                r, i, gamma, rg = (ref[...] for ref in ws["saved"][d])
                a = ws["a"][d][...]
                if d == 0:
                    lam_t = dhf + _shifted(pad_s, rf[...], (1,))[0]
                    h_prev = _shifted(pad_s, hf[...], (-1,), before=h0f)[0]
                else:
                    lam_t = dhb + _shifted(pad_s, rb[...], (-1,))[0]
                    h_prev = _shifted(pad_s, hb[...], (1,), after=h0b)[0]
                da = lam_t * h_prev
                lx = lam_t * xc
                d_i = lx * gamma
                d_gamma = lx * i
                dxc = dxc + lam_t * (gamma * i)
                d_log_a = a * (da - d_gamma * (a * rg))
                dsp.append(jnp.sum(d_log_a * r, axis=0, keepdims=True) * (-LRU_C))
                d_pre_r = d_log_a * nsp[d:d + 1, :] * (r * (1.0 - r))
                d_pre_i = d_i * (i * (1.0 - i))
                prb, pib, xb = d_pre_r.astype(BF16), d_pre_i.astype(BF16), xc.astype(BF16)
                dxc = dxc + _dot_nt(prb, wmat(wa_ref, d)) + _dot_nt(pib, wmat(wx_ref, d))
                g_wa, g_wx = _dot_tn(xb, prb), _dot_tn(xb, pib)
                g_ba = jnp.sum(d_pre_r, axis=0, keepdims=True)
                g_bx = jnp.sum(d_pre_i, axis=0, keepdims=True)
                mask = _bias_row(ba_ref, d)[1]
                dba_ref[...] += jnp.where(mask, g_ba, 0.0)
                dbx_ref[...] += jnp.where(mask, g_bx, 0.0)
                if first:
                    dwa_ref[d, 0] = g_wa
                    dwx_ref[d, 0] = g_wx
                else:
                    dwa_ref[d, 0] += g_wa
                    dwx_ref[d, 0] += g_wx
            g_lam = jnp.concatenate(dsp, axis=0) * (-_sigmoid(-lamv))
            dm1, dp1, dm2 = _shifted(pad_s, dxc, (-1, 1, -2))
            dxa = dp1 * cwv[0:1, :] + dxc * cwv[1:2, :] + dm1 * cwv[2:3, :] + dm2 * cwv[3:4, :]
            xm1, xp1, xp2 = _shifted(pad_s, xa, (-1, 1, 2))
            g_cw = jnp.concatenate([jnp.sum(dxc * v, axis=0, keepdims=True) for v in (xm1, xa, xp1, xp2)], axis=0)
            g_cb = jnp.sum(dxc, axis=0, keepdims=True)
            if first:
                dlam_ref[...] = g_lam
                dcw_ref[...] = g_cw
                dcb_ref[...] = g_cb
            else:
                dlam_ref[...] += g_lam
                dcw_ref[...] += g_cw
                dcb_ref[...] += g_cb
            return dxa, rho_f_first, rho_b_last

        ws_x, ws_c = workspace(main_s), workspace(ctx_s)
        h0f, h0b = forward(ws_c, xac_ref[...], LC, z, z)
        forward(ws_x, xa_ref[...], L, h0f, h0b)
        dh = dyl_ref[...]
        dxa, dh0f, dh0b = backward(ws_x, xa_ref[...], L, h0f, h0b, dh, dh, True)
        dxa_ref[...] = dxa.astype(BF16)
        rc = _rows((LC, HD))
        dxac, _, _ = backward(ws_c, xac_ref[...], LC, z, z, jnp.where(rc == LC - 1, dh0f, 0.0),
                              jnp.where(rc == 0, dh0b, 0.0), False)
        dxac_ref[...] = dxac.astype(BF16)

    s = _lru_param_specs()
    col = lambda r: pl.BlockSpec((r, HD), lambda h: (0, h))
    return _call(
        body, name="lru_backward", grid=(HEADS,),
        out_shape=[jax.ShapeDtypeStruct((L, D_IN), BF16), jax.ShapeDtypeStruct((LC, D), BF16),
                   jax.ShapeDtypeStruct((2, HEADS, HD, HD), F32), jax.ShapeDtypeStruct((2, HEADS, HD, HD), F32),
                   jax.ShapeDtypeStruct((2 * HEADS, HD), F32), jax.ShapeDtypeStruct((2 * HEADS, HD), F32),
                   jax.ShapeDtypeStruct((2, D), F32), jax.ShapeDtypeStruct((CONV_W, D), F32),
                   jax.ShapeDtypeStruct((1, D), F32)] + [jax.ShapeDtypeStruct((4,) + a.shape[1:], a.dtype)
                                                          for a in chip_sums] + _chips_stage_shapes(chip_sums),
        in_specs=[s["xa"], s["xac"], col(L), pl.BlockSpec(memory_space=pl.ANY), s["cw"], s["cb"], s["h4"], s["h4"],
                  s["b16"], s["b16"], s["v2"]] + [HBM] * nr,
        out_specs=[col(L), col(LC), s["h4"], s["h4"], s["b16"], s["b16"], s["v2"], col(CONV_W), col(1)]
        + [HBM] * (2 * nr),
        scratch_shapes=[pltpu.VMEM((17, L, HD), F32), pltpu.VMEM((17, LC, HD), F32), pltpu.VMEM((L + 16, HD), F32)]
        + (_chips_sems(nr) if nr else []),
        input_output_aliases={3: 0},
        compiler_params=pltpu.CompilerParams(dimension_semantics=("arbitrary",), vmem_limit_bytes=VMEM_LIMIT,
                                             has_side_effects=True, collective_id=6 if nr else None),
    )(zx, zc, dyl, dz, cw, cb, wa, wx, ba, bx, lam, *[pltpu.with_memory_space_constraint(a, pltpu.HBM)
                                                       for a in chip_sums])


def _mixer_loss(x, tgt, zx, yl, gx, fg, lng, lnb, ws, bst, wout, tm):
    ncht = tm // CHUNK

    def body(x_ref, t_ref, ga_ref, u_ref, v_ref, gb_ref, yl_ref, gx_ref, fg_ref, lng_ref, lnb_ref, ws_ref,
             bst_ref, wout_ref,
             dz_ref, dyl_ref, dxn_ref, y_s, do_ref, dws_ref, dbst_ref, vec_ref,
             vn_s, mix_s, dm_s, dvn_s, bst_s, dbst_s):
        step = pl.program_id(0)

        @pl.when(step == 0)
        def _():
            dws_ref[...] = jnp.zeros_like(dws_ref)
            dbst_s[...] = jnp.zeros_like(dbst_s)
            vec_ref[...] = jnp.zeros_like(vec_ref)
            bst_s[...] = bst_ref[...].T

        u, v = u_ref[...], v_ref[...]
        ug, dug_du = _gelu_and_grad(u)
        vg, dvg_dv = _gelu_and_grad(v)
        mu = jnp.mean(vg, axis=-1, keepdims=True)
        vc = vg - mu
        rstd = lax.rsqrt(jnp.mean(vc * vc, axis=-1, keepdims=True) + LN_EPS)
        vhat = vc * rstd
        lngv = lng_ref[...]
        vn_s[...] = (vhat * lngv + lnb_ref[...]).astype(BF16)
        for ch in range(ncht):
            rs = slice(ch * CHUNK, (ch + 1) * CHUNK)
            for g in range(HEADS):
                cs = slice(g * HD, (g + 1) * HD)
                mix_s[rs, cs] = _dot(ws_ref[g].astype(BF16), vn_s[rs, cs]) + bst_s[:, g:g + 1]
        mixed = mix_s[...]
        ga, gb, yl = ga_ref[...], gb_ref[...], yl_ref[...]
        sga, dsga = _silu_and_grad(ga)
        sgb, dsgb = _silu_and_grad(gb)
        ys = ug * mixed
        y_s[:, 0:D] = (yl * sga).astype(BF16)
        y_s[:, D:D_MIX] = (ys * sgb).astype(BF16)
        o = _dot(y_s[...], wout_ref[...])
        gxv, fgv = gx_ref[...], fg_ref[...]
        xn = x_ref[...] + gxv * o
        rs2 = lax.rsqrt(jnp.mean(xn * xn, axis=-1, keepdims=True) + NORM_EPS)
        xh = xn * rs2
        diff = xh * fgv - t_ref[...]
        vec_ref[R_LOSS:R_LOSS + 1, :] += jnp.full((1, D), jnp.sum(diff * diff) * (0.5 / D), F32)
        dout = diff * (1.0 / D)
        w = dout * fgv
        dxn = rs2 * (w - xh * jnp.mean(w * xh, axis=-1, keepdims=True))
        dxn_ref[...] = dxn
        vec_ref[0:1, :] += jnp.sum(dxn * o, axis=0, keepdims=True)
        vec_ref[1:2, :] += jnp.sum(dout * xh, axis=0, keepdims=True)
        dob = (dxn * gxv).astype(BF16)
        do_ref[...] = dob
        dy = _dot_nt(dob, wout_ref[...])
        dya, dyb = dy[:, 0:D], dy[:, D:D_MIX]
        dyl_ref[...] = dya * sga
        dys = dyb * sgb
        dz_ref[:, 0:D] = jnp.zeros((tm, D), BF16)
        dz_ref[:, D:2 * D] = (dya * yl * dsga).astype(BF16)
        dz_ref[:, 2 * D:3 * D] = (dys * mixed * dug_du).astype(BF16)
        dz_ref[:, 4 * D:5 * D] = (dyb * ys * dsgb).astype(BF16)
        dm = dys * ug
        dm_s[...] = dm.astype(BF16)
        for g in range(HEADS):
            cs = slice(g * HD, (g + 1) * HD)
            dbst_s[:, g:g + 1] += sum(jnp.sum(dm[ch * CHUNK:(ch + 1) * CHUNK, cs], axis=1, keepdims=True)
                                      for ch in range(ncht))
            for ch in range(ncht):
                rs = slice(ch * CHUNK, (ch + 1) * CHUNK)
                dws_ref[g] += _dot_nt(dm_s[rs, cs], vn_s[rs, cs])
                dvn_s[rs, cs] = _dot_tn(ws_ref[g].astype(BF16), dm_s[rs, cs])
        dvn = dvn_s[...]
        vec_ref[2:3, :] += jnp.sum(dvn * vhat, axis=0, keepdims=True)
        vec_ref[3:4, :] += jnp.sum(dvn, axis=0, keepdims=True)
        dvh = dvn * lngv
        dvg = rstd * (dvh - jnp.mean(dvh, axis=-1, keepdims=True) - vhat * jnp.mean(dvh * vhat, axis=-1, keepdims=True))
        dz_ref[:, 3 * D:4 * D] = (dvg * dvg_dv).astype(BF16)

        @pl.when(step == pl.num_programs(0) - 1)
        def _():
            dbst_ref[...] = dbst_s[...].T

    tile = pl.BlockSpec((tm, D), lambda i: (i, 0))
    zcol = lambda n: pl.BlockSpec((tm, D), lambda i: (i, n))
    vec = pl.BlockSpec((1, D), lambda i: (0, 0))
    full = lambda *s: pl.BlockSpec(s, lambda i: (0,) * len(s))
    return _call(
        body, name="mixer_loss", grid=(L // tm,),
        out_shape=[jax.ShapeDtypeStruct((L, D_IN), BF16), jax.ShapeDtypeStruct((L, D), F32),
                   jax.ShapeDtypeStruct((L, D), F32), jax.ShapeDtypeStruct((L, D_MIX), BF16),
                   jax.ShapeDtypeStruct((L, D), BF16),
                   jax.ShapeDtypeStruct((HEADS, CHUNK, CHUNK), F32), jax.ShapeDtypeStruct((HEADS, CHUNK), F32),
                   jax.ShapeDtypeStruct((8, D), F32)],
        in_specs=[tile, tile, zcol(1), zcol(2), zcol(3), zcol(4), tile, pl.BlockSpec((1, D), lambda i: (0, 2)),
                  vec, vec, vec,
                  full(HEADS, CHUNK, CHUNK), full(HEADS, CHUNK),
                  pl.BlockSpec((D_MIX, D), lambda i: (0, 0), pipeline_mode=pl.Buffered(1))],
        out_specs=[pl.BlockSpec((tm, D_IN), lambda i: (i, 0)), tile, tile,
                   pl.BlockSpec((tm, D_MIX), lambda i: (i, 0)), tile,
                   full(HEADS, CHUNK, CHUNK), full(HEADS, CHUNK), full(8, D)],
        scratch_shapes=[pltpu.VMEM((tm, D), BF16), pltpu.VMEM((tm, D), F32),
                        pltpu.VMEM((tm, D), BF16), pltpu.VMEM((tm, D), F32)] + [pltpu.VMEM((CHUNK, HEADS), F32)] * 2,
        compiler_params=_params("arbitrary"),
    )(x, tgt, zx, zx, zx, zx, yl, gx, fg, lng, lnb, ws, bst, wout)


def _grad_w(a, b, a2, b2, tk, name, bw, first, nblocks, split, barrier_id, riders=()):
    nk = a.shape[0] // tk
    m = a.shape[1]
    with_ctx = a2 is not None
    if split == "cols":
        slots, r, w = nblocks, m, bw // 2
        piece = lambda q, pc: (slice(None), slice(pc * w, (pc + 1) * w))
    else:
        slots, r, w = 4, m // 8, bw
        piece = lambda q, pc: (slice((2 * q + pc) * r, (2 * q + pc + 1) * r), slice(None))

    nr = len(riders)

    def body(*refs):
        a_ref, b_ref = refs[:2]
        a2_ref, b2_ref = refs[2:4] if with_ctx else (None, None)
        base = 4 if with_ctx else 2
        sums_ref = refs[base + nr]
        s0 = base + 3 * nr + 1
        acc, mine_v, send_v, stage_v, send_sems, recv_sems = refs[s0:s0 + 6]
        n, k = pl.program_id(0), pl.program_id(1)
        x, y, c = lax.axis_index("x"), lax.axis_index("y"), lax.axis_index("c")

        def to_sibling(s):
            return pltpu.make_async_remote_copy(src_ref=send_v.at[s], dst_ref=stage_v.at[s], send_sem=send_sems.at[s],
                                                recv_sem=recv_sems.at[s], device_id=(x, y, 1 - c),
                                                device_id_type=MESH)

        if nr:
            rider_in = refs[base:base + nr]
            own_v, got_v, psum_v = (refs[s0 + 9 + i * nr:s0 + 9 + (i + 1) * nr] for i in range(3))
            p_send, p_recv, p_local = refs[s0 + 9 + 3 * nr:s0 + 12 + 3 * nr]
            c_start, c_forward, c_finish = _chips_ops(psum_v, refs[base + nr + 1:base + 2 * nr + 1],
                                                      refs[base + 2 * nr + 1:base + 3 * nr + 1], *refs[s0 + 6:s0 + 9])

            @pl.when(jnp.logical_and(n == 0, k == 0))
            def _():
                _barrier([(x, y, 1 - c)] + [(x ^ (j >> 1), y ^ (j & 1), c) for j in (1, 2)])
                copies = []
                for j in range(nr):
                    for q in range(4):
                        copies.append(pltpu.make_async_remote_copy(
                            src_ref=rider_in[j].at[2 * q + 1 - c], dst_ref=got_v[j].at[q], send_sem=p_send.at[j, q],
                            recv_sem=p_recv.at[j, q], device_id=(x, y, 1 - c), device_id_type=MESH))
                        copies.append(pltpu.make_async_copy(rider_in[j].at[2 * q + c], own_v[j].at[q],
                                                            p_local.at[j, q]))
                for cp in copies:
                    cp.start()
                for cp in copies:
                    cp.wait()
                for j in range(nr):
                    psum_v[j][...] = (own_v[j][...] + got_v[j][...]).astype(BF16)
                c_start()
        else:
            pl.when(jnp.logical_and(n == 0, k == 0))(_sibling_barrier)

        @pl.when(k == 0)
        def _():
            acc[...] = _dot_tn(a_ref[...], b_ref[...])

        if nk > 1:
            @pl.when(k > 0)
            def _():
                acc[...] += _dot_tn(a_ref[...], b_ref[...])

        if with_ctx:
            @pl.when(jnp.logical_and(k == nk - 1, n == 0))
            def _():
                acc[:, 0:b2_ref.shape[1]] += _dot_tn(a2_ref[...], b2_ref[...])

        if nr:
            pl.when(jnp.logical_and(k == nk - 1, n == nblocks - 1))(c_forward)

        def hand_over(s, q):
            for pc in (0, 1):
                @pl.when(c == pc)
                def _(pc=pc):
                    mine_v[s] = acc[piece(q, pc)]
                    send_v[s] = acc[piece(q, 1 - pc)].astype(BF16)
            to_sibling(s).start()

        for i in range(nblocks):
            @pl.when(jnp.logical_and(k == nk - 1, n == i))
            def _(i=i):
                if split == "cols":
                    hand_over(i, 0)
                else:
                    for q in range(4):
                        hand_over(q, q)

        @pl.when(jnp.logical_and(k == nk - 1, n == nblocks - 1))
        def _():
            for s in range(slots):
                to_sibling(s).wait_recv()
                sums_ref[s] = (mine_v[s] + stage_v[s].astype(F32)).astype(BF16)
            for s in range(slots):
                to_sibling(s).wait_send()
            if nr:
                c_finish()

    in_specs = [pl.BlockSpec((tk, m), lambda n, k: (k, 0)), pl.BlockSpec((tk, bw), lambda n, k: (k, n + first))]
    args = [a, b]
    if with_ctx:
        in_specs += [pl.BlockSpec(a2.shape, lambda n, k: (0, 0)), pl.BlockSpec(b2.shape, lambda n, k: (0, 0))]
        args += [a2, b2]
    in_specs += [HBM] * nr
    args += [pltpu.with_memory_space_constraint(s, pltpu.HBM) for s in riders]
    rider_sums = [jax.ShapeDtypeStruct((4,) + s.shape[1:], BF16) for s in riders]
    rider_scratch = []
    if nr:
        rider_scratch = (_chips_sems(nr) + [pltpu.VMEM(s.shape, F32) for s in rider_sums] * 2
                         + [pltpu.VMEM(s.shape, BF16) for s in rider_sums]
                         + [pltpu.SemaphoreType.DMA((nr, 4))] * 3)
    return _call(
        body, name=name, grid=(nblocks, nk),
        out_shape=[jax.ShapeDtypeStruct((slots, r, w), BF16)] + rider_sums + _chips_stage_shapes(rider_sums),
        in_specs=in_specs, out_specs=[pl.BlockSpec((slots, r, w), lambda n, k: (0, 0, 0))] + [HBM] * (2 * nr),
        scratch_shapes=[pltpu.VMEM((m, bw), F32), pltpu.VMEM((slots, r, w), F32), pltpu.VMEM((slots, r, w), BF16),
                        pltpu.VMEM((slots, r, w), BF16), pltpu.SemaphoreType.DMA((slots,)),
                        pltpu.SemaphoreType.DMA((slots,))] + rider_scratch,
        compiler_params=pltpu.CompilerParams(dimension_semantics=("arbitrary", "arbitrary"),
                                             vmem_limit_bytes=VMEM_LIMIT, has_side_effects=True,
                                             collective_id=barrier_id),
    )(*args)


def _grad_rows(xr, dz, w, mod, ng, dres, ncols, tm, name, chip_sums=(), first_chips=None, dests=None):
    rows = xr.shape[0]
    steps = rows // tm
    with_dx = dres is not None
    nr = len(chip_sums)
    dests = [d for d in (dests or [None] * nr)]
    nd = sum(d is not None for d in dests)
    nin = 6 if with_dx else 5
    nout = 2 if with_dx else 1

    def body(*refs):
        if with_dx:
            x_ref, dz_ref, w_ref, sc_ref, ng_ref, dres_ref = refs[:nin]
            dx_ref, vec_ref = refs[nin + nr + nd:nin + nr + nd + nout]
        else:
            x_ref, dz_ref, w_ref, sc_ref, ng_ref = refs[:nin]
            (vec_ref,) = refs[nin + nr + nd:nin + nr + nd + nout]
        if nr:
            o0 = nin + nr + nd + nout
            start, forward, finish = _chips_ops(refs[nin:nin + nr], refs[o0:o0 + nr], refs[o0 + nr:o0 + 2 * nr],
                                                *refs[o0 + 2 * nr:o0 + 2 * nr + 3], first_chips=first_chips,
                                                barrier=True)
            w_hbm, w_ref, w_sem = w_ref, refs[o0 + 2 * nr + 3], refs[o0 + 2 * nr + 4]

            @pl.when(pl.program_id(0) == 0)
            def _():
                start()
                w_load = pltpu.make_async_copy(w_hbm, w_ref, w_sem)
                w_load.start()
                w_load.wait()

            pl.when(pl.program_id(0) == steps // 2)(forward)
            pl.when(pl.program_id(0) == steps - 1)(finish)

        @pl.when(pl.program_id(0) == 0)
        def _():
            vec_ref[...] = jnp.zeros_like(vec_ref)

        dhn = _dot_nt(dz_ref[...], w_ref[...])
        x = x_ref[...]
        rs = lax.rsqrt(jnp.mean(x * x, axis=-1, keepdims=True) + NORM_EPS)
        xh = x * rs
        ngv = ng_ref[...]
        y = xh * ngv
        vec_ref[0:1, :] += jnp.sum(dhn, axis=0, keepdims=True)
        vec_ref[1:2, :] += jnp.sum(dhn * y, axis=0, keepdims=True)
        dy = dhn * (1.0 + sc_ref[...])
        vec_ref[2:3, :] += jnp.sum(dy * xh, axis=0, keepdims=True)
        if with_dx:
            dxh = dy * ngv
            dx_ref[...] = dres_ref[...] + rs * (dxh - xh * jnp.mean(dxh * xh, axis=-1, keepdims=True))

    tile = pl.BlockSpec((tm, D), lambda i: (i, 0))
    vec = pl.BlockSpec((1, D), lambda i: (0, 0))
    w_spec = HBM if nr else pl.BlockSpec((D, ncols), lambda i: (0, 0), pipeline_mode=pl.Buffered(1))
    in_specs = [tile, pl.BlockSpec((tm, ncols), lambda i: (i, 0)), w_spec, pl.BlockSpec((1, D), lambda i: (0, 1)), vec]
    out_shape = [jax.ShapeDtypeStruct((8, D), F32)]
    out_specs = [pl.BlockSpec((8, D), lambda i: (0, 0))]
    args = [xr, dz, pltpu.with_memory_space_constraint(w, pltpu.HBM) if nr else w, mod, ng]
    if with_dx:
        in_specs.append(tile)
        out_shape.insert(0, jax.ShapeDtypeStruct((rows, D), F32))
        out_specs.insert(0, tile)
        args.append(dres)
    aliases = {}
    for j, d in enumerate(dests):
        if d is not None:
            aliases[len(args) + nr + len(aliases)] = len(out_shape) + j
    in_specs += [HBM] * (nr + nd)
    out_specs += [HBM] * (2 * nr)
    out_shape += [jax.ShapeDtypeStruct((4,) + a.shape[1:], a.dtype) for a in chip_sums]
    out_shape += _chips_stage_shapes(chip_sums)
    args += [pltpu.with_memory_space_constraint(a, pltpu.HBM) for a in chip_sums]
    args += [pltpu.with_memory_space_constraint(d, pltpu.HBM) for d in dests if d is not None]
    return _call(body, name=name, grid=(steps,), out_shape=out_shape, in_specs=in_specs, out_specs=out_specs,
                 scratch_shapes=(_chips_sems(nr) + [pltpu.VMEM((D, ncols), BF16), pltpu.SemaphoreType.DMA(())])
                 if nr else [], input_output_aliases=aliases,
                 compiler_params=pltpu.CompilerParams(dimension_semantics=("arbitrary",),
                                                      vmem_limit_bytes=VMEM_LIMIT, has_side_effects=bool(nr),
                                                      collective_id=7 if nr else None))(*args)


def _adamw(w, g, m, v):
    m = ADAM_B1 * m + (1.0 - ADAM_B1) * g
    v = ADAM_B2 * v + (1.0 - ADAM_B2) * (g * g)
    m_hat = m / (1.0 - ADAM_B1 ** ADAM_STEP)
    v_hat = v / (1.0 - ADAM_B2 ** ADAM_STEP)
    delta = -ADAM_LR * (m_hat / (jnp.sqrt(v_hat) + ADAM_EPS) + ADAM_WD * w)
    return delta, m, v


def _adamw_reduced(parts, w, m, v, tr, name):
    r, n = w.shape
    nparts = parts.shape[0]

    def body(p_ref, w_ref, m_ref, v_ref, g_ref, d_ref, mo_ref, vo_ref):
        g = p_ref[0].astype(F32)
        for i in range(1, nparts):
            g = g + p_ref[i].astype(F32)
        g_ref[...] = g
        d_ref[...], mo_ref[...], vo_ref[...] = _adamw(w_ref[...], g, m_ref[...], v_ref[...])

    tile = pl.BlockSpec((tr, n), lambda i: (i, 0))
    sds = jax.ShapeDtypeStruct((r, n), F32)
    return _call(
        body, name=name, grid=(r // tr,), out_shape=[sds] * 4,
        in_specs=[pl.BlockSpec((nparts, tr, n), lambda i: (0, i, 0)), tile, tile, tile], out_specs=[tile] * 4,
        compiler_params=_params("arbitrary"),
    )(parts, w, m, v)


R_GATE, R_FINAL_G, R_LN_G, R_LN_B, R_LOSS = 0, 1, 2, 3, 4
R_SH_X, R_SC_X, R_NG_X = 5, 6, 7
R_SH_C, R_SC_C, R_NG_C = 8, 9, 10
R_LAM, R_CW, R_CB = 11, 13, 17
PACK_ROWS = 24
Q_BA, Q_BX, Q_SGU_B, PACK128_ROWS = 0, 16, 32, 40


def _reduce_small(vec_pieces, q_pieces, mat_parts, ada_w, me):
    nloc = ada_w.shape[1]
    nm = len(mat_parts)
    pieces = list(vec_pieces) + list(q_pieces)

    def body(me_ref, *refs):
        piece_refs, refs = refs[:len(pieces)], refs[len(pieces):]
        mp_refs, w_ref = refs[:nm], refs[nm]
        red_ref, redq_ref = refs[nm + 1:nm + 3]
        mats_all = refs[nm + 3:2 * nm + 3]
        cparts_ref, dmod_ref, gab_ref, loss_ref = refs[2 * nm + 3:2 * nm + 7]
        pack_ref, packq_ref, vp_ref, vq_ref = refs[2 * nm + 7:2 * nm + 11]
        mat_refs = refs[2 * nm + 11:3 * nm + 11]
        cpart_ref, dmc_s = refs[3 * nm + 11:3 * nm + 13]
        sems = refs[3 * nm + 13:]
        for dst, group in ((pack_ref, vec_pieces), (packq_ref, q_pieces)):
            row = 0
            for _, nrows in group:
                dst[row:row + nrows, :] = piece_refs[0][0:nrows, :]
                piece_refs, row = piece_refs[1:], row + nrows
            if row < dst.shape[0]:
                dst[row:, :] = jnp.zeros((dst.shape[0] - row, dst.shape[1]), F32)
        p_start, p_forward, p_finish = _gather2_ops([pack_ref, packq_ref], [vp_ref, vq_ref], ["ag", "ag"], *sems[:3],
                                                    barrier=True)
        m_start, m_forward, m_finish = _gather2_ops(mat_refs, mats_all, ["ag"] * nm, *sems[3:6])
        c_start, c_forward, c_finish = _gather2_ops([cpart_ref], [cparts_ref], ["ag"], *sems[6:])
        p_start()
        for mp_ref, mat_ref in zip(mp_refs, mat_refs):
            mat = mp_ref[0].astype(F32)
            for i in range(1, mp_ref.shape[0]):
                mat = mat + mp_ref[i].astype(F32)
            mat_ref[...] = mat
        m_start()
        p_forward()
        p_finish()
        red, redq = vp_ref[0], vq_ref[0]
        for i in range(1, N_DEV):
            red = red + vp_ref[i]
            redq = redq + vq_ref[i]
        red_ref[...] = red
        redq_ref[...] = redq
        loss_ref[...] = red_ref[R_LOSS:R_LOSS + 1, 0:1]
        for e in range(N_DEV):
            dmod_ref[e:e + 1, 0:D] = vp_ref[e, R_SH_X:R_SH_X + 1, :]
            dmod_ref[e:e + 1, D:2 * D] = vp_ref[e, R_SC_X:R_SC_X + 1, :]
            dmod_ref[e:e + 1, 2 * D:3 * D] = vp_ref[e, R_GATE:R_GATE + 1, :]
        dmod_ref[8:9, 0:D] = red[R_SH_C:R_SH_C + 1, :]
        dmod_ref[8:9, D:2 * D] = red[R_SC_C:R_SC_C + 1, :]
        dmod_ref[8:9, 2 * D:3 * D] = jnp.zeros((1, D), F32)
        dmod_ref[9:16, :] = jnp.zeros((7, 3 * D), F32)
        gab_ref[:, 0:D] = red[R_SH_X:R_SH_X + 1, :] + red[R_SH_C:R_SH_C + 1, :]
        gab_ref[:, D:2 * D] = red[R_SC_X:R_SC_X + 1, :] + red[R_SC_C:R_SC_C + 1, :]
        gab_ref[:, 2 * D:3 * D] = red[R_GATE:R_GATE + 1, :]
        dmc_s[...] = jnp.broadcast_to(dmod_ref[8:9, :], (8, 3 * D))
        off = pl.multiple_of(me_ref[0] * nloc, 128)
        cpart_ref[...] = _dot_nt(dmc_s[:, pl.ds(off, nloc)], w_ref[...])
        c_start()
        m_forward()
        c_forward()
        c_finish()
        m_finish()

    return _call(
        body, name="reduce_small",
        out_shape=[jax.ShapeDtypeStruct((PACK_ROWS, D), F32), jax.ShapeDtypeStruct((PACK128_ROWS, HD), F32)]
        + [jax.ShapeDtypeStruct((N_DEV,) + p.shape[1:], F32) for p in mat_parts]
        + [jax.ShapeDtypeStruct((N_DEV, 8, D), F32), jax.ShapeDtypeStruct((16, 3 * D), F32),
           jax.ShapeDtypeStruct((1, 3 * D), F32), jax.ShapeDtypeStruct((1, 1), F32)],
        in_specs=[pl.BlockSpec(memory_space=pltpu.SMEM)] + [VMEM] * (len(pieces) + nm + 1),
        out_specs=[VMEM] * (nm + 6),
        scratch_shapes=[pltpu.VMEM((PACK_ROWS, D), F32), pltpu.VMEM((PACK128_ROWS, HD), F32),
                        pltpu.VMEM((N_DEV, PACK_ROWS, D), F32), pltpu.VMEM((N_DEV, PACK128_ROWS, HD), F32)]
        + [pltpu.VMEM(p.shape[1:], F32) for p in mat_parts]
        + [pltpu.VMEM((8, D), F32), pltpu.VMEM((8, 3 * D), F32)] + _gather2_sems(2) + _gather2_sems(nm)
        + _gather2_sems(1),
        compiler_params=pltpu.CompilerParams(vmem_limit_bytes=VMEM_LIMIT, has_side_effects=True, collective_id=8),
    )(me, *[a for a, _ in pieces], *mat_parts, ada_w)


def _adamw_ada(c_all, c_ctx, dmod, w, m, v, me):
    nloc = w.shape[1]

    def body(me_ref, c_ref, cc_ref, dm_ref, w_ref, m_ref, v_ref, g_ref, d_ref, mo_ref, vo_ref):
        off = pl.multiple_of(me_ref[0] * nloc, 128)
        dm = dm_ref[:, pl.ds(off, nloc)]
        sx, _ = _silu_and_grad(c_ref[...])
        sc, _ = _silu_and_grad(cc_ref[...])
        g = _dot_tn(sx, dm[0:8, :]) + _dot_tn(jnp.broadcast_to(sc, (8, D)), dm[8:16, :])
        g_ref[...] = g
        d_ref[...], mo_ref[...], vo_ref[...] = _adamw(w_ref[...], g, m_ref[...], v_ref[...])

    sds = jax.ShapeDtypeStruct(w.shape, F32)
    return _call(
        body, name="adamw_ada_w", out_shape=[sds] * 4,
        in_specs=[pl.BlockSpec(memory_space=pltpu.SMEM)] + [VMEM] * 6, out_specs=[VMEM] * 4,
        compiler_params=_params(),
    )(me, c_all, c_ctx, dmod, w, m, v)


_SMALL = ("c_ctx", "ada_b", "norm_g", "conv_w", "conv_b", "lru_wa", "lru_ba", "lru_wx", "lru_bx", "lru_lambda",
          "sgu_ln_g", "sgu_ln_b", "sgu_w", "sgu_b", "final_g")


def _adamw_small(red, redq, mats, cparts, gab, ws, ms, vs, me):
    n = len(_SMALL)

    def body(me_ref, red_ref, redq_ref, wa_ref, wx_ref, sw_ref, cp_ref, gab_ref, *refs):
        w_refs, m_refs, v_refs = refs[:n], refs[n:2 * n], refs[2 * n:3 * n]
        outs = refs[3 * n:]
        off = pl.multiple_of(me_ref[0] * HD, 128)

        def row(r, k=1):
            return red_ref[r:r + k, :]

        cc = w_refs[0][...]
        dcc = cp_ref[0, 0:1, :]
        for i in range(1, N_DEV):
            dcc = dcc + cp_ref[i, 0:1, :]
        grads = dict(
            c_ctx=dcc * _silu_and_grad(cc)[1], ada_b=gab_ref[...], norm_g=row(R_NG_X) + row(R_NG_C),
            conv_w=red_ref[R_CW:R_CW + CONV_W, pl.ds(off, HD)], conv_b=row(R_CB),
            lru_wa=wa_ref[...], lru_ba=redq_ref[Q_BA:Q_BA + 2 * HEADS, :], lru_wx=wx_ref[...],
            lru_bx=redq_ref[Q_BX:Q_BX + 2 * HEADS, :], lru_lambda=red_ref[R_LAM:R_LAM + 2, pl.ds(off, HD)],
            sgu_ln_g=row(R_LN_G), sgu_ln_b=row(R_LN_B), sgu_w=sw_ref[...],
            sgu_b=redq_ref[Q_SGU_B:Q_SGU_B + HEADS, :], final_g=row(R_FINAL_G))
        for j, name in enumerate(_SMALL):
            g = grads[name]
            outs[j][...] = g
            outs[n + j][...], outs[2 * n + j][...], outs[3 * n + j][...] = _adamw(w_refs[j][...], g, m_refs[j][...],
                                                                                 v_refs[j][...])

    sds = [jax.ShapeDtypeStruct(ws[k].shape, F32) for k in _SMALL]
    outs = _call(
        body, name="adamw_small", out_shape=sds * 4,
        in_specs=[pl.BlockSpec(memory_space=pltpu.SMEM)] + [VMEM] * (7 + 3 * n), out_specs=[VMEM] * (4 * n),
        compiler_params=_params(),
    )(me, red, redq, *mats, cparts, gab, *[ws[k] for k in _SMALL], *[ms[k] for k in _SMALL],
      *[vs[k] for k in _SMALL])
    return [dict(zip(_SMALL, outs[i * n:(i + 1) * n])) for i in range(4)]


def kernel(x, c, ctx, c_ctx, ada_w, ada_b, norm_g, w_in, conv_w, conv_b, lru_wa, lru_ba, lru_wx, lru_bx, lru_lambda, sgu_ln_g, sgu_ln_b, sgu_w, sgu_b, w_out, final_g, loss_target, m_c_ctx, m_ada_w, m_ada_b, m_norm_g, m_w_in, m_conv_w, m_conv_b, m_lru_wa, m_lru_ba, m_lru_wx, m_lru_bx, m_lru_lambda, m_sgu_ln_g, m_sgu_ln_b, m_sgu_w, m_sgu_b, m_w_out, m_final_g, v_c_ctx, v_ada_w, v_ada_b, v_norm_g, v_w_in, v_conv_w, v_conv_b, v_lru_wa, v_lru_ba, v_lru_wx, v_lru_bx, v_lru_lambda, v_sgu_ln_g, v_sgu_ln_b, v_sgu_w, v_sgu_b, v_w_out, v_final_g):
    args = dict(locals())
    me = (4 * lax.axis_index("x") + 2 * lax.axis_index("y") + lax.axis_index("c")).astype(jnp.int32).reshape(1)
    xr, ctxr, tgt = x[0], ctx[0], loss_target[0]
    cc = c_ctx.reshape(1, D)
    nw = 2 * HEADS * HD
    view = dict(c_ctx=(1, D), ada_b=(1, 3 * D), norm_g=(1, D), conv_w=(CONV_W, HD), conv_b=(1, D), lru_wa=(nw, HD),
                lru_ba=(2 * HEADS, HD), lru_wx=(nw, HD), lru_bx=(2 * HEADS, HD), lru_lambda=(2, HD), sgu_ln_g=(1, D),
                sgu_ln_b=(1, D), sgu_w=(HEADS * CHUNK, CHUNK), sgu_b=(HEADS, CHUNK), final_g=(1, D))

    zx, hn, w_full, w_out_b, modx, modc, c_all, cw_full, lam_full = _front_project(
        xr, c, cc, ada_w[0], ada_b, norm_g, w_in[0], w_out[0], conv_w[0], lru_lambda[0], me)
    zc, hnc = _project(ctxr, modc, norm_g, w_full, D, LC, "project_ctx")
    ba, bx = lru_ba.reshape(view["lru_ba"]), lru_bx.reshape(view["lru_bx"])
    yl, wout_all = _lru_forward(zx, zc, cw_full, conv_b, lru_wa[0], lru_wx[0], ba, bx, lam_full, [w_out_b], ["ag"])
    wout_full = wout_all.reshape(D_MIX, D)
    dz, dyl, dxn, ycat, dob, dws, dbst, mvec = _mixer_loss(
        xr, tgt, zx, yl, modx, final_g.reshape(1, D), sgu_ln_g, sgu_ln_b, sgu_w[0], sgu_b[0], wout_full, ROWS)

    (wout_sums,) = _grad_w(ycat, dob, None, None, L // 2, "grad_w_out", D, 0, 1, "rows", 1)
    (rest_sums,) = _grad_w(hn, dz, None, None, L, "grad_w_in_rest", 2 * W_IN_SHARD, 1, 3, "cols", 2)
    dz, dxac, dwa, dwx, dba, dbx, dlam, dcw, dcb, win_parts, wout_parts, _, _ = _lru_backward(
        zx, zc, dyl, dz, cw_full, conv_b, lru_wa[0], lru_wx[0], ba, bx, lam_full, [rest_sums, wout_sums],
        first_chips=[1, 0])
    mats = [dwa.reshape(N_DEV, nw // N_DEV, HD), dwx.reshape(N_DEV, nw // N_DEV, HD), dws]
    first_sums, *mat_parts = _grad_w(hn, dz, hnc, dxac, L, "grad_w_in_first", 2 * W_IN_SHARD, 0, 1, "cols", 3,
                                     riders=mats)[:4]
    gx, xvec, win_parts = _grad_rows(
        xr, dz, w_full, modx, norm_g, dxn, D_IN, ROWS, "grad_rows_x", chip_sums=[first_sums], first_chips=[0],
        dests=[win_parts])[:3]
    (cvec,) = _grad_rows(ctxr, dxac, w_full, modc, norm_g, None, D, LC, "grad_rows_ctx")
    red, redq, *rest = _reduce_small(
        [(mvec, 5), (xvec, 3), (cvec, 3), (dlam, 2), (dcw, CONV_W), (dcb, 1)],
        [(dba, 2 * HEADS), (dbx, 2 * HEADS), (dbst, HEADS)], mat_parts, ada_w[0], me)
    mats_all, (cparts, dmod, gab, loss) = rest[:3], rest[3:]

    g_w_in, d_w_in, nm_w_in, nv_w_in = _adamw_reduced(win_parts, w_in[0], m_w_in[0], v_w_in[0], 2 * ROWS, "adamw_w_in")
    g_w_out, d_w_out, nm_w_out, nv_w_out = _adamw_reduced(wout_parts, w_out[0], m_w_out[0], v_w_out[0], ROWS // 2,
                                                          "adamw_w_out")
    g_ada, d_ada, nm_ada, nv_ada = _adamw_ada(c_all, cc, dmod, ada_w[0], m_ada_w[0], v_ada_w[0], me)
    ws = {k: args[k].reshape(view[k]) for k in _SMALL}
    ms = {k: args["m_" + k].reshape(view[k]) for k in _SMALL}
    vs = {k: args["v_" + k].reshape(view[k]) for k in _SMALL}
    small = _adamw_small(red, redq, [m.reshape(-1, HD) for m in mats_all], cparts, gab, ws, ms, vs, me)
    big = dict(w_in=(g_w_in, d_w_in, nm_w_in, nv_w_in), w_out=(g_w_out, d_w_out, nm_w_out, nv_w_out),
               ada_w=(g_ada, d_ada, nm_ada, nv_ada))

    loss = loss.reshape(())
    names = ("c_ctx", "ada_w", "ada_b", "norm_g", "w_in", "conv_w", "conv_b", "lru_wa", "lru_ba", "lru_wx", "lru_bx",
             "lru_lambda", "sgu_ln_g", "sgu_ln_b", "sgu_w", "sgu_b", "w_out", "final_g")
    outs = [loss, gx.reshape(x.shape)]
    for kind in range(4):
        for k in names:
            val = big[k][kind] if k in big else small[kind][k]
            outs.append(val.reshape(args[k].shape))
    return tuple(outs)
```

```python
import jax
import jax.numpy as jnp
from jax import lax
from jax.experimental import pallas as pl
from jax.experimental.pallas import tpu as pltpu

F32 = jnp.float32
BF16 = jnp.bfloat16

N_DEV = 8
D = 1024
L = 2048
LC = 256
HEADS = 8
HD = 128
CHUNK = 128
D_IN = 5 * D
W_IN_SHARD = D_IN // N_DEV
ROWS = 256
D_MIX = 2 * D
CONV_W = 4
LRU_C = 8.0
NORM_EPS = 1e-6
LN_EPS = 1e-5
ADAM_LR, ADAM_B1, ADAM_B2, ADAM_EPS, ADAM_WD, ADAM_STEP = 0.001, 0.9, 0.999, 1e-08, 0.01, 10

VMEM_LIMIT = 56 * 1024 * 1024

HBM = pl.BlockSpec(memory_space=pltpu.HBM)
VMEM = pl.BlockSpec(memory_space=pltpu.VMEM)
MESH = pl.DeviceIdType.MESH


def _call(body, **kw):
    return pl.pallas_call(body, **kw)


def _params(*sem):
    return pltpu.CompilerParams(dimension_semantics=sem, vmem_limit_bytes=VMEM_LIMIT)


def _sigmoid(x):
    return 0.5 * jnp.tanh(0.5 * x) + 0.5


def _silu_and_grad(x):
    s = _sigmoid(x)
    return x * s, s * (1.0 + x * (1.0 - s))


_G0 = 0.7978845608028654
_G1 = 0.044715


def _gelu_and_grad(x):
    x2 = x * x
    t = jnp.tanh(_G0 * (x + _G1 * x * x2))
    cdf = 0.5 * (1.0 + t)
    return x * cdf, cdf + 0.5 * x * (1.0 - t * t) * (_G0 * (1.0 + 3.0 * _G1 * x2))


def _softplus(z):
    t = jnp.exp(-jnp.abs(z))
    u = 1.0 + t
    log1p = jnp.where(u == 1.0, t, jnp.log(u) * t / jnp.where(u == 1.0, 1.0, u - 1.0))
    return jnp.maximum(z, 0.0) + log1p


def _dot(a, b):
    return jnp.dot(a, b, preferred_element_type=F32)


def _dot_nt(a, b):
    return lax.dot_general(a, b, (((1,), (1,)), ((), ())), preferred_element_type=F32)


def _dot_tn(a, b):
    return lax.dot_general(a, b, (((0,), (0,)), ((), ())), preferred_element_type=F32)


def _rows(shape):
    return lax.broadcasted_iota(jnp.int32, shape, 0)


def _gather2_shapes(arrays, modes):
    return [jax.ShapeDtypeStruct((N_DEV,) + a.shape if m == "ag" else (a.shape[0], N_DEV * a.shape[1]), a.dtype)
            for a, m in zip(arrays, modes)]


def _gather2_sems(n):
    return [pltpu.SemaphoreType.DMA((n, N_DEV - 1)), pltpu.SemaphoreType.DMA((n, N_DEV - 1)),
            pltpu.SemaphoreType.DMA((n,))]


def _barrier(peers):
    sem = pltpu.get_barrier_semaphore()
    for peer in peers:
        pl.semaphore_signal(sem, inc=1, device_id=peer, device_id_type=MESH)
    pl.semaphore_wait(sem, len(peers))


def _gather2_ops(ins, outs, modes, send_sems, recv_sems, local_sems, barrier=False):
    n = len(ins)
    x, y, c = lax.axis_index("x"), lax.axis_index("y"), lax.axis_index("c")
    me, sibling = (x, y, c), (x, y, 1 - c)
    chips = [(x ^ (k >> 1), y ^ (k & 1)) for k in (1, 2, 3)]

    def slot(j, px, py, pc):
        dev = 4 * px + 2 * py + pc
        if modes[j] == "agc":
            w = ins[j].shape[1]
            return outs[j].at[:, pl.ds(pl.multiple_of(dev * w, 128), w)]
        return outs[j].at[dev]

    def copy(j, k, block, to, src=None):
        return pltpu.make_async_remote_copy(
            src_ref=slot(j, *block) if src is None else src, dst_ref=slot(j, *block),
            send_sem=send_sems.at[j, k], recv_sem=recv_sems.at[j, k], device_id=to, device_id_type=MESH)

    def own(j):
        return pltpu.make_async_copy(ins[j], slot(j, *me), local_sems.at[j])

    def first(j):
        return [copy(j, 0, me, sibling, src=ins[j])] + [copy(j, 1 + i, me, (*chip, c), src=ins[j])
                                                        for i, chip in enumerate(chips)]

    def passed(j, i):
        return copy(j, 4 + i, (*chips[i], c), sibling)

    def start():
        if barrier:
            _barrier([sibling] + [(*chip, c) for chip in chips])
        for j in range(n):
            own(j).start()
            for cp in first(j):
                cp.start()

    def forward():
        for i, chip in enumerate(chips):
            for j in range(n):
                copy(j, 1 + i, (*chip, c), me).wait_recv()
                passed(j, i).start()

    def finish():
        for j in range(n):
            copy(j, 0, sibling, me).wait_recv()
            for i, chip in enumerate(chips):
                copy(j, 4 + i, (*chip, 1 - c), me).wait_recv()
            for cp in first(j) + [passed(j, i) for i in range(3)]:
                cp.wait_send()
            own(j).wait()

    return start, forward, finish


def _sibling_barrier():
    sem = pltpu.get_barrier_semaphore()
    sibling = (lax.axis_index("x"), lax.axis_index("y"), 1 - lax.axis_index("c"))
    pl.semaphore_signal(sem, inc=1, device_id=sibling, device_id_type=MESH)
    pl.semaphore_wait(sem, 1)


def _chips_sems(n):
    return [pltpu.SemaphoreType.DMA((n, 6)), pltpu.SemaphoreType.DMA((n, 6)), pltpu.SemaphoreType.DMA((n,))]


def _chips_stage_shapes(chip_sums):
    return [jax.ShapeDtypeStruct((2, a.shape[1] // 2, a.shape[2]), a.dtype) for a in chip_sums]


def _chips_ops(ins, outs, stages, send_sems, recv_sems, local_sems, first_chips=None, barrier=False):
    x, y, c = lax.axis_index("x"), lax.axis_index("y"), lax.axis_index("c")
    qm = 2 * x + y
    first_chips = first_chips or [0] * len(ins)

    def owns(j, chip):
        lo, cnt = first_chips[j], ins[j].shape[0]
        if lo == 0 and cnt == 4:
            return None
        return jnp.logical_and(chip >= lo, chip < lo + cnt)

    def guarded(cond, fn):
        if cond is None:
            fn()
        else:
            pl.when(cond)(fn)

    def slot(j, chip):
        return jnp.clip(chip - first_chips[j], 0, ins[j].shape[0] - 1)

    def half(j, i):
        h = ins[j].shape[1] // 2
        return pl.ds(i * h, h)

    def copy(j, sem, src, dst, k):
        return pltpu.make_async_remote_copy(
            src_ref=src, dst_ref=dst, send_sem=send_sems.at[j, sem], recv_sem=recv_sems.at[j, sem],
            device_id=(x ^ (k >> 1), y ^ (k & 1), c), device_id_type=MESH)

    def direct(j, k):
        return copy(j, k - 1, ins[j].at[slot(j, qm ^ k)], outs[j].at[qm], k)

    def first_hop(j, k):
        return copy(j, 1 + k, ins[j].at[slot(j, qm ^ 3), half(j, k - 1)], stages[j].at[k - 1], k)

    def second_hop(j, k):
        return copy(j, 3 + k, stages[j].at[2 - k], outs[j].at[qm ^ (3 - k), half(j, 2 - k)], k)

    def local(j):
        return pltpu.make_async_copy(ins[j].at[slot(j, qm)], outs[j].at[qm], local_sems.at[j])

    def start():
        if barrier:
            _barrier([(x ^ (k >> 1), y ^ (k & 1), c) for k in (1, 2)])
        for j in range(len(ins)):
            for k in (1, 2):
                guarded(owns(j, qm ^ 3), lambda j=j, k=k: first_hop(j, k).start())
        for j in range(len(ins)):
            for k in (1, 2):
                guarded(owns(j, qm ^ k), lambda j=j, k=k: direct(j, k).start())
            guarded(owns(j, qm), lambda j=j: local(j).start())

    def forward():
        for j in range(len(ins)):
            for k in (1, 2):
                def pass_on(j=j, k=k):
                    first_hop(j, 3 - k).wait_recv()
                    second_hop(j, k).start()
                guarded(owns(j, qm ^ k), pass_on)

    def finish():
        for j in range(len(ins)):
            for k in (1, 2):
                guarded(owns(j, qm ^ k), lambda j=j, k=k: direct(j, k).wait_send())
                guarded(owns(j, qm ^ k), lambda j=j, k=k: second_hop(j, k).wait_send())
                guarded(owns(j, qm ^ 3), lambda j=j, k=k: first_hop(j, k).wait_send())
                guarded(owns(j, qm), lambda j=j, k=k: direct(j, k).wait_recv())
                guarded(owns(j, qm), lambda j=j, k=k: second_hop(j, k).wait_recv())
            guarded(owns(j, qm), lambda j=j: local(j).wait())

    return start, forward, finish


ARRIVAL = (0, 1, 2, 4, 3, 5, 6, 7)


def _front_project(xr, c, c_ctx, ada_w, ada_b, ng, w_in, w_out, cw, lam, me):
    nloc = ada_w.shape[1]
    ws = W_IN_SHARD
    arrival = jnp.asarray(ARRIVAL, jnp.int32)

    def body(me_ref, arr_ref, x_ref, c_ref, cc_ref, aw_ref, ab_ref, ng_ref, win_ref, wout_ref, cw_ref, lam_ref,
             z_ref, hn_ref, wfull_ref, woutb_ref, modx_ref, modc_ref, call_ref, cwf_ref, lamf_ref,
             wv, call_s, part_s, parts_s, w_send, w_recv, hbm_sems, s_send, s_recv, g_send, g_recv, g_local,
             x_v, x_sem):
        t = pl.program_id(0)
        x_load = pltpu.make_async_copy(x_ref, x_v, x_sem)
        x, y, cidx = lax.axis_index("x"), lax.axis_index("y"), lax.axis_index("c")
        me_i = me_ref[0]
        sibling = (x, y, 1 - cidx)
        chips = [(x ^ (k >> 1), y ^ (k & 1)) for k in (1, 2, 3)]
        g_start, g_pass, g_finish = _gather2_ops([cw_ref, lam_ref], [cwf_ref, lamf_ref], ["agc", "agc"],
                                                 g_send, g_recv, g_local)

        def shard_copy(k, px, py, pc, to, half=None):
            slot = wv.at[4 * px + 2 * py + pc]
            if half is not None:
                slot = slot.at[pl.ds(half * (D // 2), D // 2), :]
            return pltpu.make_async_remote_copy(src_ref=slot, dst_ref=slot, send_sem=w_send.at[k],
                                                recv_sem=w_recv.at[k], device_id=to, device_id_type=MESH)

        def small_gather(src, my_slot, stage):
            copies = []
            for k in range(1, N_DEV):
                peer = (x ^ (k >> 2), y ^ ((k >> 1) & 1), cidx ^ (k & 1))
                cp = pltpu.make_async_remote_copy(src_ref=src, dst_ref=my_slot, send_sem=s_send.at[stage, k - 1],
                                                  recv_sem=s_recv.at[stage, k - 1], device_id=peer,
                                                  device_id_type=MESH)
                cp.start()
                copies.append(cp)
            pltpu.sync_copy(src, my_slot)
            return copies

        def finish_small(copies):
            for cp in copies:
                cp.wait()

        def to_neighbours(half):
            for i in (0, 1):
                shard_copy(1 + i, x, y, cidx, (*chips[i], cidx), half=half).start()

        @pl.when(t == 0)
        def _():
            _barrier([(x ^ (k >> 2), y ^ ((k >> 1) & 1), cidx ^ (k & 1)) for k in range(1, N_DEV)])
            g_start()
            x_load.start(priority=1)
            wv[me_i] = win_ref[...].astype(BF16)
            woutb_ref[...] = wout_ref[...].astype(BF16)
            shard_copy(0, x, y, cidx, sibling).start()
            conds_sent = small_gather(c_ref, call_s.at[pl.ds(me_i, 1), :], 0)
            to_neighbours(0)
            finish_small(conds_sent)
            call_ref[...] = call_s[...]
            off = pl.multiple_of(me_i * nloc, 128)
            b = ab_ref[:, pl.ds(off, nloc)]
            w = aw_ref[...]
            sx, _ = _silu_and_grad(call_s[...])
            sc, _ = _silu_and_grad(jnp.broadcast_to(cc_ref[...], (8, D)))
            part_s[0:8, :] = _dot(sx, w) + b
            part_s[8:16, :] = _dot(sc, w) + b
            parts_sent = small_gather(part_s, parts_s.at[me_i], 1)
            to_neighbours(1)
            finish_small(parts_sent)
            mine = _rows((16, nloc)) == me_i
            for j in range(N_DEV):
                pj = parts_s[j]
                modx_ref[:, j * nloc:(j + 1) * nloc] = jnp.sum(jnp.where(mine, pj, 0.0), axis=0, keepdims=True)
                modc_ref[:, j * nloc:(j + 1) * nloc] = pj[8:9, :]
            shift, scale1, ngv = modx_ref[:, 0:D], 1.0 + modx_ref[:, D:2 * D], ng_ref[...]
            x_load.wait()
            for r in range(L // ROWS):
                rsl = slice(r * ROWS, (r + 1) * ROWS)
                xv = x_v[rsl, :]
                rs = lax.rsqrt(jnp.mean(xv * xv, axis=-1, keepdims=True) + NORM_EPS)
                hn_ref[rsl, :] = ((xv * rs * ngv) * scale1 + shift).astype(BF16)

        @pl.when(t == 1)
        def _():
            shard_copy(0, x, y, 1 - cidx, sibling).wait_recv()
            g_pass()

        for i in (0, 1):
            @pl.when(t == ARRIVAL.index((2, 4)[i]))
            def _(i=i):
                shard_copy(1 + i, *chips[i], cidx, sibling).wait_recv()
                shard_copy(4 + i, *chips[i], cidx, sibling).start()
                shard_copy((7, 3)[i], *chips[i], cidx, (*chips[1 - i], cidx), half=i).start()

        @pl.when(t == ARRIVAL.index(6))
        def _():
            shard_copy(3, *chips[2], cidx, sibling, half=1).wait_recv()
            shard_copy(7, *chips[2], cidx, sibling, half=0).wait_recv()
            shard_copy(6, *chips[2], cidx, sibling).start()

        for i in range(3):
            @pl.when(t == ARRIVAL.index((3, 5, 7)[i]))
            def _(i=i):
                shard_copy(4 + i, *chips[i], 1 - cidx, sibling).wait_recv()

        @pl.when(t == 2)
        def _():
            g_finish()

        dev = me_i ^ arr_ref[t]
        for r in range(L // (2 * ROWS)):
            rsl = slice(r * 2 * ROWS, (r + 1) * 2 * ROWS)
            z_ref[rsl, :] = _dot(hn_ref[rsl, :], wv[dev])
        col = pl.ds(pl.multiple_of(dev * ws, 128), ws)
        pltpu.make_async_copy(wv.at[dev], wfull_ref.at[:, col], hbm_sems.at[t]).start(priority=1)

        @pl.when(t == N_DEV - 1)
        def _():
            for k in (0, 1, 2, 4, 5, 6):
                shard_copy(k, x, y, cidx, sibling).wait_send()
            for k in (3, 7):
                shard_copy(k, x, y, cidx, sibling, half=0).wait_send()
            for s in range(N_DEV):
                pltpu.make_async_copy(wv.at[0], wfull_ref.at[:, pl.ds(0, ws)], hbm_sems.at[s]).wait()

    const = lambda *shape: pl.BlockSpec(shape, lambda t, m, a: (0,) * len(shape))
    once = lambda *shape: pl.BlockSpec(shape, lambda t, m, a: (0,) * len(shape), pipeline_mode=pl.Buffered(1))
    return _call(
        body, name="front_project",
        out_shape=[jax.ShapeDtypeStruct((L, D_IN), F32), jax.ShapeDtypeStruct((L, D), BF16),
                   jax.ShapeDtypeStruct((D, D_IN), BF16), jax.ShapeDtypeStruct(w_out.shape, BF16),
                   jax.ShapeDtypeStruct((1, 3 * D), F32), jax.ShapeDtypeStruct((1, 3 * D), F32),
                   jax.ShapeDtypeStruct((N_DEV, D), F32), jax.ShapeDtypeStruct((CONV_W, D), F32),
                   jax.ShapeDtypeStruct((2, D), F32)],
        grid_spec=pltpu.PrefetchScalarGridSpec(
            num_scalar_prefetch=2, grid=(N_DEV,),
            in_specs=[HBM, const(1, D), const(1, D), once(D, nloc), const(1, 3 * D), const(1, D),
                      once(D, ws), once(*w_out.shape), HBM, HBM],
            out_specs=[pl.BlockSpec((L, ws), lambda t, m, a: (0, m[0] ^ a[t])), const(L, D), HBM,
                       const(*w_out.shape),
                       const(1, 3 * D), const(1, 3 * D), const(N_DEV, D), HBM, HBM],
            scratch_shapes=[pltpu.VMEM((N_DEV, D, ws), BF16), pltpu.VMEM((N_DEV, D), F32), pltpu.VMEM((16, nloc), F32),
                            pltpu.VMEM((N_DEV, 16, nloc), F32), pltpu.SemaphoreType.DMA((8,)),
                            pltpu.SemaphoreType.DMA((8,)), pltpu.SemaphoreType.DMA((N_DEV,)),
                            pltpu.SemaphoreType.DMA((2, N_DEV - 1)), pltpu.SemaphoreType.DMA((2, N_DEV - 1))]
            + _gather2_sems(2) + [pltpu.VMEM((L, D), F32), pltpu.SemaphoreType.DMA(())]),
        compiler_params=pltpu.CompilerParams(dimension_semantics=("arbitrary",), vmem_limit_bytes=VMEM_LIMIT,
                                             has_side_effects=True, collective_id=10),
    )(me, arrival, pltpu.with_memory_space_constraint(xr, pltpu.HBM), c, c_ctx, ada_w, ada_b, ng, w_in, w_out,
      pltpu.with_memory_space_constraint(cw, pltpu.HBM), pltpu.with_memory_space_constraint(lam, pltpu.HBM))


def _project(xr, mod, ng, w, ncols, tm, name):
    rows = xr.shape[0]

    def body(x_ref, sh_ref, sc_ref, ng_ref, w_ref, z_ref, hn_ref):
        x = x_ref[...]
        rs = lax.rsqrt(jnp.mean(x * x, axis=-1, keepdims=True) + NORM_EPS)
        hn = (x * rs * ng_ref[...]) * (1.0 + sc_ref[...]) + sh_ref[...]
        hb = hn.astype(BF16)
        hn_ref[...] = hb
        for n in range(ncols // D):
            z_ref[:, n * D:(n + 1) * D] = _dot(hb, w_ref[:, n * D:(n + 1) * D])

    vec = pl.BlockSpec((1, D), lambda i: (0, 0))
    return _call(
        body, name=name, grid=(rows // tm,),
        out_shape=[jax.ShapeDtypeStruct((rows, ncols), F32), jax.ShapeDtypeStruct((rows, D), BF16)],
        in_specs=[pl.BlockSpec((tm, D), lambda i: (i, 0)), vec, pl.BlockSpec((1, D), lambda i: (0, 1)), vec,
                  pl.BlockSpec((D, ncols), lambda i: (0, 0), pipeline_mode=pl.Buffered(1))],
        out_specs=[pl.BlockSpec((tm, ncols), lambda i: (i, 0)), pl.BlockSpec((tm, D), lambda i: (i, 0))],
        compiler_params=_params("arbitrary"),
    )(xr, mod, mod, ng, w)


def _scan_pair(af_ref, uf_ref, hf_ref, h0f, ab_ref, ub_ref, hb_ref, h0b, t_len):
    span = 8 * SCAN_BLOCKS
    nit = t_len // span
    rows = _rows((8, HD))

    def local_scan(a, b, forward):
        for s in (1, 2, 4):
            sh = s if forward else 8 - s
            m = rows >= s if forward else rows < 8 - s
            b = a * jnp.where(m, pltpu.roll(b, sh, 0), 0.0) + b
            a = a * jnp.where(m, pltpu.roll(a, sh, 0), 1.0)
        return a, b

    def span_scan(a_ref, u_ref, h_ref, off, carry, forward):
        order = range(SCAN_BLOCKS) if forward else range(SCAN_BLOCKS - 1, -1, -1)
        last = slice(7, 8) if forward else slice(0, 1)
        for q in order:
            rs = pl.ds(off + 8 * q, 8)
            a, b = local_scan(a_ref[rs, :], u_ref[rs, :], forward)
            h_ref[rs, :] = b + a * carry
            carry = a[last, :] * carry + b[last, :]
        return carry

    def body(k, carry):
        cf, cb = carry
        cf = span_scan(af_ref, uf_ref, hf_ref, pl.multiple_of(k * span, span), cf, True)
        cb = span_scan(ab_ref, ub_ref, hb_ref, pl.multiple_of((nit - 1 - k) * span, span), cb, False)
        return cf, cb

    return lax.fori_loop(0, nit, body, (h0f, h0b))


SCAN_BLOCKS = 16


def _shifted(pad_ref, x, offsets, before=0.0, after=0.0):
    n = x.shape[0]
    pad_ref[0:8, :] = jnp.broadcast_to(jnp.asarray(before, F32), (8, x.shape[1]))
    pad_ref[8:8 + n, :] = x
    pad_ref[8 + n:16 + n, :] = jnp.broadcast_to(jnp.asarray(after, F32), (8, x.shape[1]))
    return [pad_ref[8 + o:8 + o + n, :] for o in offsets]


def _conv(xa, cw, cb, pad_ref):
    xm1, xp1, xp2 = _shifted(pad_ref, xa, (-1, 1, 2))
    return xm1 * cw[0:1, :] + xa * cw[1:2, :] + xp1 * cw[2:3, :] + xp2 * cw[3:4, :] + cb


def _gates(xc, wa, wx, ba, bx, nsp):
    xb = xc.astype(BF16)
    r = _sigmoid(_dot(xb, wa) + ba)
    i = _sigmoid(_dot(xb, wx) + bx)
    log_a = r * nsp
    a = jnp.exp(log_a)
    g2 = jnp.tanh(log_a) * (-1.0 - a * a)
    rg = lax.rsqrt(jnp.maximum(g2, 1e-30))
    return r, i, a, g2 * rg, rg


def _lru_param_specs():
    h4 = pl.BlockSpec((2, 1, HD, HD), lambda h: (0, h, 0, 0))
    v2 = pl.BlockSpec((2, HD), lambda h: (0, h))
    b16 = pl.BlockSpec((2 * HEADS, HD), lambda h: (0, 0))
    return dict(
        xa=pl.BlockSpec((L, HD), lambda h: (0, h)), xac=pl.BlockSpec((LC, HD), lambda h: (0, h)),
        cw=pl.BlockSpec((CONV_W, HD), lambda h: (0, h)), cb=pl.BlockSpec((1, HD), lambda h: (0, h)), h4=h4, v2=v2,
        b16=b16)


def _bias_row(ref, d):
    mask = _rows((2 * HEADS, HD)) == d * HEADS + pl.program_id(0)
    return jnp.sum(jnp.where(mask, ref[...], 0.0), axis=0, keepdims=True), mask


def _lru_forward(zx, zc, cw, cb, wa, wx, ba, bx, lam, gather, gather_modes):
    ng_ = len(gather)

    def body(xa_ref, xac_ref, cw_ref, cb_ref, wa_ref, wx_ref, ba_ref, bx_ref, lam_ref, *rest):
        yl_ref = rest[ng_]
        af, uf, hf, ab, ub, hb, pad_s = rest[2 * ng_ + 1:2 * ng_ + 8]
        start, pass_on, finish = _gather2_ops(rest[:ng_], rest[ng_ + 1:2 * ng_ + 1], gather_modes,
                                              *rest[2 * ng_ + 8:], barrier=True)
        pl.when(pl.program_id(0) == 0)(start)
        pl.when(pl.program_id(0) == HEADS // 2)(pass_on)
        pl.when(pl.program_id(0) == HEADS - 1)(finish)
        cwv, cbv = cw_ref[...], cb_ref[...]
        nsp = (-LRU_C) * _softplus(-lam_ref[...])

        def forward(xa, t_len, h0f, h0b):
            xc = _conv(xa, cwv, cbv, pad_s)
            for d, (a_ref, u_ref) in enumerate(((af, uf), (ab, ub))):
                _, i, a, gamma, _ = _gates(xc, wa_ref[d, 0].astype(BF16), wx_ref[d, 0].astype(BF16),
                                           _bias_row(ba_ref, d)[0], _bias_row(bx_ref, d)[0], nsp[d:d + 1, :])
                a_ref[0:t_len, :] = a
                u_ref[0:t_len, :] = gamma * (i * xc)
            return _scan_pair(af, uf, hf, h0f, ab, ub, hb, h0b, t_len)

        z = jnp.zeros((1, HD), F32)
        h0f, h0b = forward(xac_ref[...], LC, z, z)
        forward(xa_ref[...], L, h0f, h0b)
        yl_ref[...] = hf[...] + hb[...]

    s = _lru_param_specs()
    return _call(
        body, name="lru_forward", grid=(HEADS,),
        out_shape=[jax.ShapeDtypeStruct((L, D), F32)] + _gather2_shapes(gather, gather_modes),
        in_specs=[s["xa"], s["xac"], s["cw"], s["cb"], s["h4"], s["h4"], s["b16"], s["b16"], s["v2"]] + [HBM] * ng_,
        out_specs=[pl.BlockSpec((L, HD), lambda h: (0, h))] + [HBM] * ng_,
        scratch_shapes=[pltpu.VMEM((L, HD), F32)] * 6 + [pltpu.VMEM((L + 16, HD), F32)] + _gather2_sems(ng_),
        compiler_params=pltpu.CompilerParams(dimension_semantics=("arbitrary",), vmem_limit_bytes=VMEM_LIMIT,
                                             has_side_effects=True, collective_id=5),
    )(zx, zc, cw, cb, wa, wx, ba, bx, lam, *[pltpu.with_memory_space_constraint(a, pltpu.HBM) for a in gather])


def _lru_backward(zx, zc, dyl, dz, cw, cb, wa, wx, ba, bx, lam, chip_sums, first_chips=None):
    nr = len(chip_sums)

    def body(xa_ref, xac_ref, dyl_ref, dz_in, cw_ref, cb_ref, wa_ref, wx_ref, ba_ref, bx_ref, lam_ref, *rest):
        (dxa_ref, dxac_ref, dwa_ref, dwx_ref, dba_ref, dbx_ref, dlam_ref, dcw_ref,
         dcb_ref) = rest[nr:nr + 9]
        main_s, ctx_s, pad_s = rest[3 * nr + 9:3 * nr + 12]
        if nr:
            start, forward, finish = _chips_ops(rest[:nr], rest[nr + 9:2 * nr + 9], rest[2 * nr + 9:3 * nr + 9],
                                                *rest[3 * nr + 12:], first_chips=first_chips, barrier=True)
            pl.when(pl.program_id(0) == 0)(start)
            pl.when(pl.program_id(0) == HEADS // 2)(forward)
            pl.when(pl.program_id(0) == HEADS - 1)(finish)
        del dz_in

        @pl.when(pl.program_id(0) == 0)
        def _():
            dba_ref[...] = jnp.zeros_like(dba_ref)
            dbx_ref[...] = jnp.zeros_like(dbx_ref)

        cwv, cbv = cw_ref[...], cb_ref[...]
        lamv = lam_ref[...]
        sp = _softplus(-lamv)
        nsp = (-LRU_C) * sp
        z = jnp.zeros((1, HD), F32)

        def wmat(ref, d):
            return ref[d, 0].astype(BF16)

        def workspace(s):
            return dict(a=(s.at[0], s.at[1]), u=(s.at[2], s.at[3]), h=(s.at[4], s.at[5]), rho=(s.at[6], s.at[7]),
                        saved=(tuple(s.at[8 + k] for k in range(4)), tuple(s.at[12 + k] for k in range(4))),
                        xc=s.at[16])

        def forward(ws, xa, t_len, h0f, h0b):
            xc = _conv(xa, cwv, cbv, pad_s)
            ws["xc"][...] = xc
            for d in (0, 1):
                vals = _gates(xc, wmat(wa_ref, d), wmat(wx_ref, d), _bias_row(ba_ref, d)[0],
                              _bias_row(bx_ref, d)[0], nsp[d:d + 1, :])
                r, i, a, gamma, rg = vals
                ws["a"][d][...] = a
                ws["u"][d][...] = gamma * (i * xc)
                for ref, val in zip(ws["saved"][d], (r, i, gamma, rg)):
                    ref[...] = val
            return _scan_pair(ws["a"][0], ws["u"][0], ws["h"][0], h0f, ws["a"][1], ws["u"][1], ws["h"][1], h0b,
                              t_len)

        def backward(ws, xa, t_len, h0f, h0b, dhf, dhb, first):
            xc = ws["xc"][...]
            (af, ab), (uf, ub), (hf, hb), (rf, rb) = ws["a"], ws["u"], ws["h"], ws["rho"]
            uf[...] = ab[...] * dhb
            ub[...] = af[...] * dhf
            rho_b_last, rho_f_first = _scan_pair(ab, uf, rb, z, af, ub, rf, z, t_len)
            dxc = jnp.zeros((t_len, HD), F32)
            dsp = []
            for d in (0, 1):
                r, i, gamma, rg = (ref[...] for ref in ws["saved"][d])
                a = ws["a"][d][...]
                if d == 0:
                    lam_t = dhf + _shifted(pad_s, rf[...], (1,))[0]
                    h_prev = _shifted(pad_s, hf[...], (-1,), before=h0f)[0]
                else:
                    lam_t = dhb + _shifted(pad_s, rb[...], (-1,))[0]
                    h_prev = _shifted(pad_s, hb[...], (1,), after=h0b)[0]
                da = lam_t * h_prev
                lx = lam_t * xc
                d_i = lx * gamma
                d_gamma = lx * i
                dxc = dxc + lam_t * (gamma * i)
                d_log_a = a * (da - d_gamma * (a * rg))
                dsp.append(jnp.sum(d_log_a * r, axis=0, keepdims=True) * (-LRU_C))
                d_pre_r = d_log_a * nsp[d:d + 1, :] * (r * (1.0 - r))
                d_pre_i = d_i * (i * (1.0 - i))
                prb, pib, xb = d_pre_r.astype(BF16), d_pre_i.astype(BF16), xc.astype(BF16)
                dxc = dxc + _dot_nt(prb, wmat(wa_ref, d)) + _dot_nt(pib, wmat(wx_ref, d))
                g_wa, g_wx = _dot_tn(xb, prb), _dot_tn(xb, pib)
                g_ba = jnp.sum(d_pre_r, axis=0, keepdims=True)
                g_bx = jnp.sum(d_pre_i, axis=0, keepdims=True)
                mask = _bias_row(ba_ref, d)[1]
                dba_ref[...] += jnp.where(mask, g_ba, 0.0)
                dbx_ref[...] += jnp.where(mask, g_bx, 0.0)
                if first:
                    dwa_ref[d, 0] = g_wa
                    dwx_ref[d, 0] = g_wx
                else:
                    dwa_ref[d, 0] += g_wa
                    dwx_ref[d, 0] += g_wx
            g_lam = jnp.concatenate(dsp, axis=0) * (-_sigmoid(-lamv))
            dm1, dp1, dm2 = _shifted(pad_s, dxc, (-1, 1, -2))
            dxa = dp1 * cwv[0:1, :] + dxc * cwv[1:2, :] + dm1 * cwv[2:3, :] + dm2 * cwv[3:4, :]
            xm1, xp1, xp2 = _shifted(pad_s, xa, (-1, 1, 2))
            g_cw = jnp.concatenate([jnp.sum(dxc * v, axis=0, keepdims=True) for v in (xm1, xa, xp1, xp2)], axis=0)
            g_cb = jnp.sum(dxc, axis=0, keepdims=True)
            if first:
                dlam_ref[...] = g_lam
                dcw_ref[...] = g_cw
                dcb_ref[...] = g_cb
            else:
                dlam_ref[...] += g_lam
                dcw_ref[...] += g_cw
                dcb_ref[...] += g_cb
            return dxa, rho_f_first, rho_b_last

        ws_x, ws_c = workspace(main_s), workspace(ctx_s)
        h0f, h0b = forward(ws_c, xac_ref[...], LC, z, z)
        forward(ws_x, xa_ref[...], L, h0f, h0b)
        dh = dyl_ref[...]
        dxa, dh0f, dh0b = backward(ws_x, xa_ref[...], L, h0f, h0b, dh, dh, True)
        dxa_ref[...] = dxa.astype(BF16)
        rc = _rows((LC, HD))
        dxac, _, _ = backward(ws_c, xac_ref[...], LC, z, z, jnp.where(rc == LC - 1, dh0f, 0.0),
                              jnp.where(rc == 0, dh0b, 0.0), False)
        dxac_ref[...] = dxac.astype(BF16)

    s = _lru_param_specs()
    col = lambda r: pl.BlockSpec((r, HD), lambda h: (0, h))
    return _call(
        body, name="lru_backward", grid=(HEADS,),
        out_shape=[jax.ShapeDtypeStruct((L, D_IN), BF16), jax.ShapeDtypeStruct((LC, D), BF16),
                   jax.ShapeDtypeStruct((2, HEADS, HD, HD), F32), jax.ShapeDtypeStruct((2, HEADS, HD, HD), F32),
                   jax.ShapeDtypeStruct((2 * HEADS, HD), F32), jax.ShapeDtypeStruct((2 * HEADS, HD), F32),
                   jax.ShapeDtypeStruct((2, D), F32), jax.ShapeDtypeStruct((CONV_W, D), F32),
                   jax.ShapeDtypeStruct((1, D), F32)] + [jax.ShapeDtypeStruct((4,) + a.shape[1:], a.dtype)
                                                          for a in chip_sums] + _chips_stage_shapes(chip_sums),
        in_specs=[s["xa"], s["xac"], col(L), pl.BlockSpec(memory_space=pl.ANY), s["cw"], s["cb"], s["h4"], s["h4"],
                  s["b16"], s["b16"], s["v2"]] + [HBM] * nr,
        out_specs=[col(L), col(LC), s["h4"], s["h4"], s["b16"], s["b16"], s["v2"], col(CONV_W), col(1)]
        + [HBM] * (2 * nr),
        scratch_shapes=[pltpu.VMEM((17, L, HD), F32), pltpu.VMEM((17, LC, HD), F32), pltpu.VMEM((L + 16, HD), F32)]
        + (_chips_sems(nr) if nr else []),
        input_output_aliases={3: 0},
        compiler_params=pltpu.CompilerParams(dimension_semantics=("arbitrary",), vmem_limit_bytes=VMEM_LIMIT,
                                             has_side_effects=True, collective_id=6 if nr else None),
    )(zx, zc, dyl, dz, cw, cb, wa, wx, ba, bx, lam, *[pltpu.with_memory_space_constraint(a, pltpu.HBM)
                                                       for a in chip_sums])


def _mixer_loss(x, tgt, zx, yl, gx, fg, lng, lnb, ws, bst, wout, tm):
    ncht = tm // CHUNK

    def body(x_ref, t_ref, ga_ref, u_ref, v_ref, gb_ref, yl_ref, gx_ref, fg_ref, lng_ref, lnb_ref, ws_ref,
             bst_ref, wout_ref,
             dz_ref, dyl_ref, dxn_ref, y_s, do_ref, dws_ref, dbst_ref, vec_ref,
             vn_s, mix_s, dm_s, dvn_s, bst_s, dbst_s):
        step = pl.program_id(0)

        @pl.when(step == 0)
        def _():
            dws_ref[...] = jnp.zeros_like(dws_ref)
            dbst_s[...] = jnp.zeros_like(dbst_s)
            vec_ref[...] = jnp.zeros_like(vec_ref)
            bst_s[...] = bst_ref[...].T

        u, v = u_ref[...], v_ref[...]
        ug, dug_du = _gelu_and_grad(u)
        vg, dvg_dv = _gelu_and_grad(v)
        mu = jnp.mean(vg, axis=-1, keepdims=True)
        vc = vg - mu
        rstd = lax.rsqrt(jnp.mean(vc * vc, axis=-1, keepdims=True) + LN_EPS)
        vhat = vc * rstd
        lngv = lng_ref[...]
        vn_s[...] = (vhat * lngv + lnb_ref[...]).astype(BF16)
        for ch in range(ncht):
            rs = slice(ch * CHUNK, (ch + 1) * CHUNK)
            for g in range(HEADS):
                cs = slice(g * HD, (g + 1) * HD)
                mix_s[rs, cs] = _dot(ws_ref[g].astype(BF16), vn_s[rs, cs]) + bst_s[:, g:g + 1]
        mixed = mix_s[...]
        ga, gb, yl = ga_ref[...], gb_ref[...], yl_ref[...]
        sga, dsga = _silu_and_grad(ga)
        sgb, dsgb = _silu_and_grad(gb)
        ys = ug * mixed
        y_s[:, 0:D] = (yl * sga).astype(BF16)
        y_s[:, D:D_MIX] = (ys * sgb).astype(BF16)
        o = _dot(y_s[...], wout_ref[...])
        gxv, fgv = gx_ref[...], fg_ref[...]
        xn = x_ref[...] + gxv * o
        rs2 = lax.rsqrt(jnp.mean(xn * xn, axis=-1, keepdims=True) + NORM_EPS)
        xh = xn * rs2
        diff = xh * fgv - t_ref[...]
        vec_ref[R_LOSS:R_LOSS + 1, :] += jnp.full((1, D), jnp.sum(diff * diff) * (0.5 / D), F32)
        dout = diff * (1.0 / D)
        w = dout * fgv
        dxn = rs2 * (w - xh * jnp.mean(w * xh, axis=-1, keepdims=True))
        dxn_ref[...] = dxn
        vec_ref[0:1, :] += jnp.sum(dxn * o, axis=0, keepdims=True)
        vec_ref[1:2, :] += jnp.sum(dout * xh, axis=0, keepdims=True)
        dob = (dxn * gxv).astype(BF16)
        do_ref[...] = dob
        dy = _dot_nt(dob, wout_ref[...])
        dya, dyb = dy[:, 0:D], dy[:, D:D_MIX]
        dyl_ref[...] = dya * sga
        dys = dyb * sgb
        dz_ref[:, 0:D] = jnp.zeros((tm, D), BF16)
        dz_ref[:, D:2 * D] = (dya * yl * dsga).astype(BF16)
        dz_ref[:, 2 * D:3 * D] = (dys * mixed * dug_du).astype(BF16)
        dz_ref[:, 4 * D:5 * D] = (dyb * ys * dsgb).astype(BF16)
        dm = dys * ug
        dm_s[...] = dm.astype(BF16)
        for g in range(HEADS):
            cs = slice(g * HD, (g + 1) * HD)
            dbst_s[:, g:g + 1] += sum(jnp.sum(dm[ch * CHUNK:(ch + 1) * CHUNK, cs], axis=1, keepdims=True)
                                      for ch in range(ncht))
            for ch in range(ncht):
                rs = slice(ch * CHUNK, (ch + 1) * CHUNK)
                dws_ref[g] += _dot_nt(dm_s[rs, cs], vn_s[rs, cs])
                dvn_s[rs, cs] = _dot_tn(ws_ref[g].astype(BF16), dm_s[rs, cs])
        dvn = dvn_s[...]
        vec_ref[2:3, :] += jnp.sum(dvn * vhat, axis=0, keepdims=True)
        vec_ref[3:4, :] += jnp.sum(dvn, axis=0, keepdims=True)
        dvh = dvn * lngv
        dvg = rstd * (dvh - jnp.mean(dvh, axis=-1, keepdims=True) - vhat * jnp.mean(dvh * vhat, axis=-1, keepdims=True))
        dz_ref[:, 3 * D:4 * D] = (dvg * dvg_dv).astype(BF16)

        @pl.when(step == pl.num_programs(0) - 1)
        def _():
            dbst_ref[...] = dbst_s[...].T

    tile = pl.BlockSpec((tm, D), lambda i: (i, 0))
    zcol = lambda n: pl.BlockSpec((tm, D), lambda i: (i, n))
    vec = pl.BlockSpec((1, D), lambda i: (0, 0))
    full = lambda *s: pl.BlockSpec(s, lambda i: (0,) * len(s))
    return _call(
        body, name="mixer_loss", grid=(L // tm,),
        out_shape=[jax.ShapeDtypeStruct((L, D_IN), BF16), jax.ShapeDtypeStruct((L, D), F32),
                   jax.ShapeDtypeStruct((L, D), F32), jax.ShapeDtypeStruct((L, D_MIX), BF16),
                   jax.ShapeDtypeStruct((L, D), BF16),
                   jax.ShapeDtypeStruct((HEADS, CHUNK, CHUNK), F32), jax.ShapeDtypeStruct((HEADS, CHUNK), F32),
                   jax.ShapeDtypeStruct((8, D), F32)],
        in_specs=[tile, tile, zcol(1), zcol(2), zcol(3), zcol(4), tile, pl.BlockSpec((1, D), lambda i: (0, 2)),
                  vec, vec, vec,
                  full(HEADS, CHUNK, CHUNK), full(HEADS, CHUNK),
                  pl.BlockSpec((D_MIX, D), lambda i: (0, 0), pipeline_mode=pl.Buffered(1))],
        out_specs=[pl.BlockSpec((tm, D_IN), lambda i: (i, 0)), tile, tile,
                   pl.BlockSpec((tm, D_MIX), lambda i: (i, 0)), tile,
                   full(HEADS, CHUNK, CHUNK), full(HEADS, CHUNK), full(8, D)],
        scratch_shapes=[pltpu.VMEM((tm, D), BF16), pltpu.VMEM((tm, D), F32),
                        pltpu.VMEM((tm, D), BF16), pltpu.VMEM((tm, D), F32)] + [pltpu.VMEM((CHUNK, HEADS), F32)] * 2,
        compiler_params=_params("arbitrary"),
    )(x, tgt, zx, zx, zx, zx, yl, gx, fg, lng, lnb, ws, bst, wout)


def _grad_w(a, b, a2, b2, tk, name, bw, first, nblocks, split, barrier_id, riders=()):
    nk = a.shape[0] // tk
    m = a.shape[1]
    with_ctx = a2 is not None
    if split == "cols":
        slots, r, w = nblocks, m, bw // 2
        piece = lambda q, pc: (slice(None), slice(pc * w, (pc + 1) * w))
    else:
        slots, r, w = 4, m // 8, bw
        piece = lambda q, pc: (slice((2 * q + pc) * r, (2 * q + pc + 1) * r), slice(None))

    nr = len(riders)

    def body(*refs):
        a_ref, b_ref = refs[:2]
        a2_ref, b2_ref = refs[2:4] if with_ctx else (None, None)
        base = 4 if with_ctx else 2
        sums_ref = refs[base + nr]
        s0 = base + 3 * nr + 1
        acc, mine_v, send_v, stage_v, send_sems, recv_sems = refs[s0:s0 + 6]
        n, k = pl.program_id(0), pl.program_id(1)
        x, y, c = lax.axis_index("x"), lax.axis_index("y"), lax.axis_index("c")

        def to_sibling(s):
            return pltpu.make_async_remote_copy(src_ref=send_v.at[s], dst_ref=stage_v.at[s], send_sem=send_sems.at[s],
                                                recv_sem=recv_sems.at[s], device_id=(x, y, 1 - c),
                                                device_id_type=MESH)

        if nr:
            rider_in = refs[base:base + nr]
            own_v, got_v, psum_v = (refs[s0 + 9 + i * nr:s0 + 9 + (i + 1) * nr] for i in range(3))
            p_send, p_recv, p_local = refs[s0 + 9 + 3 * nr:s0 + 12 + 3 * nr]
            c_start, c_forward, c_finish = _chips_ops(psum_v, refs[base + nr + 1:base + 2 * nr + 1],
                                                      refs[base + 2 * nr + 1:base + 3 * nr + 1], *refs[s0 + 6:s0 + 9])

            @pl.when(jnp.logical_and(n == 0, k == 0))
            def _():
                _barrier([(x, y, 1 - c)] + [(x ^ (j >> 1), y ^ (j & 1), c) for j in (1, 2)])
                copies = []
                for j in range(nr):
                    for q in range(4):
                        copies.append(pltpu.make_async_remote_copy(
                            src_ref=rider_in[j].at[2 * q + 1 - c], dst_ref=got_v[j].at[q], send_sem=p_send.at[j, q],
                            recv_sem=p_recv.at[j, q], device_id=(x, y, 1 - c), device_id_type=MESH))
                        copies.append(pltpu.make_async_copy(rider_in[j].at[2 * q + c], own_v[j].at[q],
                                                            p_local.at[j, q]))
                for cp in copies:
                    cp.start()
                for cp in copies:
                    cp.wait()
                for j in range(nr):
                    psum_v[j][...] = (own_v[j][...] + got_v[j][...]).astype(BF16)
                c_start()
        else:
            pl.when(jnp.logical_and(n == 0, k == 0))(_sibling_barrier)

        @pl.when(k == 0)
        def _():
            acc[...] = _dot_tn(a_ref[...], b_ref[...])

        if nk > 1:
            @pl.when(k > 0)
            def _():
                acc[...] += _dot_tn(a_ref[...], b_ref[...])

        if with_ctx:
            @pl.when(jnp.logical_and(k == nk - 1, n == 0))
            def _():
                acc[:, 0:b2_ref.shape[1]] += _dot_tn(a2_ref[...], b2_ref[...])

        if nr:
            pl.when(jnp.logical_and(k == nk - 1, n == nblocks - 1))(c_forward)

        def hand_over(s, q):
            for pc in (0, 1):
                @pl.when(c == pc)
                def _(pc=pc):
                    mine_v[s] = acc[piece(q, pc)]
                    send_v[s] = acc[piece(q, 1 - pc)].astype(BF16)
            to_sibling(s).start()

        for i in range(nblocks):
            @pl.when(jnp.logical_and(k == nk - 1, n == i))
            def _(i=i):
                if split == "cols":
                    hand_over(i, 0)
                else:
                    for q in range(4):
                        hand_over(q, q)

        @pl.when(jnp.logical_and(k == nk - 1, n == nblocks - 1))
        def _():
            for s in range(slots):
                to_sibling(s).wait_recv()
                sums_ref[s] = (mine_v[s] + stage_v[s].astype(F32)).astype(BF16)
            for s in range(slots):
                to_sibling(s).wait_send()
            if nr:
                c_finish()

    in_specs = [pl.BlockSpec((tk, m), lambda n, k: (k, 0)), pl.BlockSpec((tk, bw), lambda n, k: (k, n + first))]
    args = [a, b]
    if with_ctx:
        in_specs += [pl.BlockSpec(a2.shape, lambda n, k: (0, 0)), pl.BlockSpec(b2.shape, lambda n, k: (0, 0))]
        args += [a2, b2]
    in_specs += [HBM] * nr
    args += [pltpu.with_memory_space_constraint(s, pltpu.HBM) for s in riders]
    rider_sums = [jax.ShapeDtypeStruct((4,) + s.shape[1:], BF16) for s in riders]
    rider_scratch = []
    if nr:
        rider_scratch = (_chips_sems(nr) + [pltpu.VMEM(s.shape, F32) for s in rider_sums] * 2
                         + [pltpu.VMEM(s.shape, BF16) for s in rider_sums]
                         + [pltpu.SemaphoreType.DMA((nr, 4))] * 3)
    return _call(
        body, name=name, grid=(nblocks, nk),
        out_shape=[jax.ShapeDtypeStruct((slots, r, w), BF16)] + rider_sums + _chips_stage_shapes(rider_sums),
        in_specs=in_specs, out_specs=[pl.BlockSpec((slots, r, w), lambda n, k: (0, 0, 0))] + [HBM] * (2 * nr),
        scratch_shapes=[pltpu.VMEM((m, bw), F32), pltpu.VMEM((slots, r, w), F32), pltpu.VMEM((slots, r, w), BF16),
                        pltpu.VMEM((slots, r, w), BF16), pltpu.SemaphoreType.DMA((slots,)),
                        pltpu.SemaphoreType.DMA((slots,))] + rider_scratch,
        compiler_params=pltpu.CompilerParams(dimension_semantics=("arbitrary", "arbitrary"),
                                             vmem_limit_bytes=VMEM_LIMIT, has_side_effects=True,
                                             collective_id=barrier_id),
    )(*args)


def _grad_rows(xr, dz, w, mod, ng, dres, ncols, tm, name, chip_sums=(), first_chips=None, dests=None):
    rows = xr.shape[0]
    steps = rows // tm
    with_dx = dres is not None
    nr = len(chip_sums)
    dests = [d for d in (dests or [None] * nr)]
    nd = sum(d is not None for d in dests)
    nin = 6 if with_dx else 5
    nout = 2 if with_dx else 1

    def body(*refs):
        if with_dx:
            x_ref, dz_ref, w_ref, sc_ref, ng_ref, dres_ref = refs[:nin]
            dx_ref, vec_ref = refs[nin + nr + nd:nin + nr + nd + nout]
        else:
            x_ref, dz_ref, w_ref, sc_ref, ng_ref = refs[:nin]
            (vec_ref,) = refs[nin + nr + nd:nin + nr + nd + nout]
        if nr:
            o0 = nin + nr + nd + nout
            start, forward, finish = _chips_ops(refs[nin:nin + nr], refs[o0:o0 + nr], refs[o0 + nr:o0 + 2 * nr],
                                                *refs[o0 + 2 * nr:o0 + 2 * nr + 3], first_chips=first_chips,
                                                barrier=True)
            w_hbm, w_ref, w_sem = w_ref, refs[o0 + 2 * nr + 3], refs[o0 + 2 * nr + 4]

            @pl.when(pl.program_id(0) == 0)
            def _():
                start()
                w_load = pltpu.make_async_copy(w_hbm, w_ref, w_sem)
                w_load.start()
                w_load.wait()

            pl.when(pl.program_id(0) == steps // 2)(forward)
            pl.when(pl.program_id(0) == steps - 1)(finish)

        @pl.when(pl.program_id(0) == 0)
        def _():
            vec_ref[...] = jnp.zeros_like(vec_ref)

        dhn = _dot_nt(dz_ref[...], w_ref[...])
        x = x_ref[...]
        rs = lax.rsqrt(jnp.mean(x * x, axis=-1, keepdims=True) + NORM_EPS)
        xh = x * rs
        ngv = ng_ref[...]
        y = xh * ngv
        vec_ref[0:1, :] += jnp.sum(dhn, axis=0, keepdims=True)
        vec_ref[1:2, :] += jnp.sum(dhn * y, axis=0, keepdims=True)
        dy = dhn * (1.0 + sc_ref[...])
        vec_ref[2:3, :] += jnp.sum(dy * xh, axis=0, keepdims=True)
        if with_dx:
            dxh = dy * ngv
            dx_ref[...] = dres_ref[...] + rs * (dxh - xh * jnp.mean(dxh * xh, axis=-1, keepdims=True))

    tile = pl.BlockSpec((tm, D), lambda i: (i, 0))
    vec = pl.BlockSpec((1, D), lambda i: (0, 0))
    w_spec = HBM if nr else pl.BlockSpec((D, ncols), lambda i: (0, 0), pipeline_mode=pl.Buffered(1))
    in_specs = [tile, pl.BlockSpec((tm, ncols), lambda i: (i, 0)), w_spec, pl.BlockSpec((1, D), lambda i: (0, 1)), vec]
    out_shape = [jax.ShapeDtypeStruct((8, D), F32)]
    out_specs = [pl.BlockSpec((8, D), lambda i: (0, 0))]
    args = [xr, dz, pltpu.with_memory_space_constraint(w, pltpu.HBM) if nr else w, mod, ng]
    if with_dx:
        in_specs.append(tile)
        out_shape.insert(0, jax.ShapeDtypeStruct((rows, D), F32))
        out_specs.insert(0, tile)
        args.append(dres)
    aliases = {}
    for j, d in enumerate(dests):
        if d is not None:
            aliases[len(args) + nr + len(aliases)] = len(out_shape) + j
    in_specs += [HBM] * (nr + nd)
    out_specs += [HBM] * (2 * nr)
    out_shape += [jax.ShapeDtypeStruct((4,) + a.shape[1:], a.dtype) for a in chip_sums]
    out_shape += _chips_stage_shapes(chip_sums)
    args += [pltpu.with_memory_space_constraint(a, pltpu.HBM) for a in chip_sums]
    args += [pltpu.with_memory_space_constraint(d, pltpu.HBM) for d in dests if d is not None]
    return _call(body, name=name, grid=(steps,), out_shape=out_shape, in_specs=in_specs, out_specs=out_specs,
                 scratch_shapes=(_chips_sems(nr) + [pltpu.VMEM((D, ncols), BF16), pltpu.SemaphoreType.DMA(())])
                 if nr else [], input_output_aliases=aliases,
                 compiler_params=pltpu.CompilerParams(dimension_semantics=("arbitrary",),
                                                      vmem_limit_bytes=VMEM_LIMIT, has_side_effects=bool(nr),
                                                      collective_id=7 if nr else None))(*args)


def _adamw(w, g, m, v):
    m = ADAM_B1 * m + (1.0 - ADAM_B1) * g
    v = ADAM_B2 * v + (1.0 - ADAM_B2) * (g * g)
    m_hat = m / (1.0 - ADAM_B1 ** ADAM_STEP)
    v_hat = v / (1.0 - ADAM_B2 ** ADAM_STEP)
    delta = -ADAM_LR * (m_hat / (jnp.sqrt(v_hat) + ADAM_EPS) + ADAM_WD * w)
    return delta, m, v


def _adamw_reduced(parts, w, m, v, tr, name):
    r, n = w.shape
    nparts = parts.shape[0]

    def body(p_ref, w_ref, m_ref, v_ref, g_ref, d_ref, mo_ref, vo_ref):
        g = p_ref[0].astype(F32)
        for i in range(1, nparts):
            g = g + p_ref[i].astype(F32)
        g_ref[...] = g
        d_ref[...], mo_ref[...], vo_ref[...] = _adamw(w_ref[...], g, m_ref[...], v_ref[...])

    tile = pl.BlockSpec((tr, n), lambda i: (i, 0))
    sds = jax.ShapeDtypeStruct((r, n), F32)
    return _call(
        body, name=name, grid=(r // tr,), out_shape=[sds] * 4,
        in_specs=[pl.BlockSpec((nparts, tr, n), lambda i: (0, i, 0)), tile, tile, tile], out_specs=[tile] * 4,
        compiler_params=_params("arbitrary"),
    )(parts, w, m, v)


R_GATE, R_FINAL_G, R_LN_G, R_LN_B, R_LOSS = 0, 1, 2, 3, 4
R_SH_X, R_SC_X, R_NG_X = 5, 6, 7
R_SH_C, R_SC_C, R_NG_C = 8, 9, 10
R_LAM, R_CW, R_CB = 11, 13, 17
PACK_ROWS = 24
Q_BA, Q_BX, Q_SGU_B, PACK128_ROWS = 0, 16, 32, 40


def _reduce_small(vec_pieces, q_pieces, mat_parts, ada_w, me):
    nloc = ada_w.shape[1]
    nm = len(mat_parts)
    pieces = list(vec_pieces) + list(q_pieces)

    def body(me_ref, *refs):
        piece_refs, refs = refs[:len(pieces)], refs[len(pieces):]
        mp_refs, w_ref = refs[:nm], refs[nm]
        red_ref, redq_ref = refs[nm + 1:nm + 3]
        mats_all = refs[nm + 3:2 * nm + 3]
        cparts_ref, dmod_ref, gab_ref, loss_ref = refs[2 * nm + 3:2 * nm + 7]
        pack_ref, packq_ref, vp_ref, vq_ref = refs[2 * nm + 7:2 * nm + 11]
        mat_refs = refs[2 * nm + 11:3 * nm + 11]
        cpart_ref, dmc_s = refs[3 * nm + 11:3 * nm + 13]
        sems = refs[3 * nm + 13:]
        for dst, group in ((pack_ref, vec_pieces), (packq_ref, q_pieces)):
            row = 0
            for _, nrows in group:
                dst[row:row + nrows, :] = piece_refs[0][0:nrows, :]
                piece_refs, row = piece_refs[1:], row + nrows
            if row < dst.shape[0]:
                dst[row:, :] = jnp.zeros((dst.shape[0] - row, dst.shape[1]), F32)
        p_start, p_forward, p_finish = _gather2_ops([pack_ref, packq_ref], [vp_ref, vq_ref], ["ag", "ag"], *sems[:3],
                                                    barrier=True)
        m_start, m_forward, m_finish = _gather2_ops(mat_refs, mats_all, ["ag"] * nm, *sems[3:6])
        c_start, c_forward, c_finish = _gather2_ops([cpart_ref], [cparts_ref], ["ag"], *sems[6:])
        p_start()
        for mp_ref, mat_ref in zip(mp_refs, mat_refs):
            mat = mp_ref[0].astype(F32)
            for i in range(1, mp_ref.shape[0]):
                mat = mat + mp_ref[i].astype(F32)
            mat_ref[...] = mat
        m_start()
        p_forward()
        p_finish()
        red, redq = vp_ref[0], vq_ref[0]
        for i in range(1, N_DEV):
            red = red + vp_ref[i]
            redq = redq + vq_ref[i]
        red_ref[...] = red
        redq_ref[...] = redq
        loss_ref[...] = red_ref[R_LOSS:R_LOSS + 1, 0:1]
        for e in range(N_DEV):
            dmod_ref[e:e + 1, 0:D] = vp_ref[e, R_SH_X:R_SH_X + 1, :]
            dmod_ref[e:e + 1, D:2 * D] = vp_ref[e, R_SC_X:R_SC_X + 1, :]
            dmod_ref[e:e + 1, 2 * D:3 * D] = vp_ref[e, R_GATE:R_GATE + 1, :]
        dmod_ref[8:9, 0:D] = red[R_SH_C:R_SH_C + 1, :]
        dmod_ref[8:9, D:2 * D] = red[R_SC_C:R_SC_C + 1, :]
        dmod_ref[8:9, 2 * D:3 * D] = jnp.zeros((1, D), F32)
        dmod_ref[9:16, :] = jnp.zeros((7, 3 * D), F32)
        gab_ref[:, 0:D] = red[R_SH_X:R_SH_X + 1, :] + red[R_SH_C:R_SH_C + 1, :]
        gab_ref[:, D:2 * D] = red[R_SC_X:R_SC_X + 1, :] + red[R_SC_C:R_SC_C + 1, :]
        gab_ref[:, 2 * D:3 * D] = red[R_GATE:R_GATE + 1, :]
        dmc_s[...] = jnp.broadcast_to(dmod_ref[8:9, :], (8, 3 * D))
        off = pl.multiple_of(me_ref[0] * nloc, 128)
        cpart_ref[...] = _dot_nt(dmc_s[:, pl.ds(off, nloc)], w_ref[...])
        c_start()
        m_forward()
        c_forward()
        c_finish()
        m_finish()

    return _call(
        body, name="reduce_small",
        out_shape=[jax.ShapeDtypeStruct((PACK_ROWS, D), F32), jax.ShapeDtypeStruct((PACK128_ROWS, HD), F32)]
        + [jax.ShapeDtypeStruct((N_DEV,) + p.shape[1:], F32) for p in mat_parts]
        + [jax.ShapeDtypeStruct((N_DEV, 8, D), F32), jax.ShapeDtypeStruct((16, 3 * D), F32),
           jax.ShapeDtypeStruct((1, 3 * D), F32), jax.ShapeDtypeStruct((1, 1), F32)],
        in_specs=[pl.BlockSpec(memory_space=pltpu.SMEM)] + [VMEM] * (len(pieces) + nm + 1),
        out_specs=[VMEM] * (nm + 6),
        scratch_shapes=[pltpu.VMEM((PACK_ROWS, D), F32), pltpu.VMEM((PACK128_ROWS, HD), F32),
                        pltpu.VMEM((N_DEV, PACK_ROWS, D), F32), pltpu.VMEM((N_DEV, PACK128_ROWS, HD), F32)]
        + [pltpu.VMEM(p.shape[1:], F32) for p in mat_parts]
        + [pltpu.VMEM((8, D), F32), pltpu.VMEM((8, 3 * D), F32)] + _gather2_sems(2) + _gather2_sems(nm)
        + _gather2_sems(1),
        compiler_params=pltpu.CompilerParams(vmem_limit_bytes=VMEM_LIMIT, has_side_effects=True, collective_id=8),
    )(me, *[a for a, _ in pieces], *mat_parts, ada_w)


def _adamw_ada(c_all, c_ctx, dmod, w, m, v, me):
    nloc = w.shape[1]

    def body(me_ref, c_ref, cc_ref, dm_ref, w_ref, m_ref, v_ref, g_ref, d_ref, mo_ref, vo_ref):
        off = pl.multiple_of(me_ref[0] * nloc, 128)
        dm = dm_ref[:, pl.ds(off, nloc)]
        sx, _ = _silu_and_grad(c_ref[...])
        sc, _ = _silu_and_grad(cc_ref[...])
        g = _dot_tn(sx, dm[0:8, :]) + _dot_tn(jnp.broadcast_to(sc, (8, D)), dm[8:16, :])
        g_ref[...] = g
        d_ref[...], mo_ref[...], vo_ref[...] = _adamw(w_ref[...], g, m_ref[...], v_ref[...])

    sds = jax.ShapeDtypeStruct(w.shape, F32)
    return _call(
        body, name="adamw_ada_w", out_shape=[sds] * 4,
        in_specs=[pl.BlockSpec(memory_space=pltpu.SMEM)] + [VMEM] * 6, out_specs=[VMEM] * 4,
        compiler_params=_params(),
    )(me, c_all, c_ctx, dmod, w, m, v)


_SMALL = ("c_ctx", "ada_b", "norm_g", "conv_w", "conv_b", "lru_wa", "lru_ba", "lru_wx", "lru_bx", "lru_lambda",
          "sgu_ln_g", "sgu_ln_b", "sgu_w", "sgu_b", "final_g")


def _adamw_small(red, redq, mats, cparts, gab, ws, ms, vs, me):
    n = len(_SMALL)

    def body(me_ref, red_ref, redq_ref, wa_ref, wx_ref, sw_ref, cp_ref, gab_ref, *refs):
        w_refs, m_refs, v_refs = refs[:n], refs[n:2 * n], refs[2 * n:3 * n]
        outs = refs[3 * n:]
        off = pl.multiple_of(me_ref[0] * HD, 128)

        def row(r, k=1):
            return red_ref[r:r + k, :]

        cc = w_refs[0][...]
        dcc = cp_ref[0, 0:1, :]
        for i in range(1, N_DEV):
            dcc = dcc + cp_ref[i, 0:1, :]
        grads = dict(
            c_ctx=dcc * _silu_and_grad(cc)[1], ada_b=gab_ref[...], norm_g=row(R_NG_X) + row(R_NG_C),
            conv_w=red_ref[R_CW:R_CW + CONV_W, pl.ds(off, HD)], conv_b=row(R_CB),
            lru_wa=wa_ref[...], lru_ba=redq_ref[Q_BA:Q_BA + 2 * HEADS, :], lru_wx=wx_ref[...],
            lru_bx=redq_ref[Q_BX:Q_BX + 2 * HEADS, :], lru_lambda=red_ref[R_LAM:R_LAM + 2, pl.ds(off, HD)],
            sgu_ln_g=row(R_LN_G), sgu_ln_b=row(R_LN_B), sgu_w=sw_ref[...],
            sgu_b=redq_ref[Q_SGU_B:Q_SGU_B + HEADS, :], final_g=row(R_FINAL_G))
        for j, name in enumerate(_SMALL):
            g = grads[name]
            outs[j][...] = g
            outs[n + j][...], outs[2 * n + j][...], outs[3 * n + j][...] = _adamw(w_refs[j][...], g, m_refs[j][...],
                                                                                 v_refs[j][...])

    sds = [jax.ShapeDtypeStruct(ws[k].shape, F32) for k in _SMALL]
    outs = _call(
        body, name="adamw_small", out_shape=sds * 4,
        in_specs=[pl.BlockSpec(memory_space=pltpu.SMEM)] + [VMEM] * (7 + 3 * n), out_specs=[VMEM] * (4 * n),
        compiler_params=_params(),
    )(me, red, redq, *mats, cparts, gab, *[ws[k] for k in _SMALL], *[ms[k] for k in _SMALL],
      *[vs[k] for k in _SMALL])
    return [dict(zip(_SMALL, outs[i * n:(i + 1) * n])) for i in range(4)]


def kernel(x, c, ctx, c_ctx, ada_w, ada_b, norm_g, w_in, conv_w, conv_b, lru_wa, lru_ba, lru_wx, lru_bx, lru_lambda, sgu_ln_g, sgu_ln_b, sgu_w, sgu_b, w_out, final_g, loss_target, m_c_ctx, m_ada_w, m_ada_b, m_norm_g, m_w_in, m_conv_w, m_conv_b, m_lru_wa, m_lru_ba, m_lru_wx, m_lru_bx, m_lru_lambda, m_sgu_ln_g, m_sgu_ln_b, m_sgu_w, m_sgu_b, m_w_out, m_final_g, v_c_ctx, v_ada_w, v_ada_b, v_norm_g, v_w_in, v_conv_w, v_conv_b, v_lru_wa, v_lru_ba, v_lru_wx, v_lru_bx, v_lru_lambda, v_sgu_ln_g, v_sgu_ln_b, v_sgu_w, v_sgu_b, v_w_out, v_final_g):
    args = dict(locals())
    me = (4 * lax.axis_index("x") + 2 * lax.axis_index("y") + lax.axis_index("c")).astype(jnp.int32).reshape(1)
    xr, ctxr, tgt = x[0], ctx[0], loss_target[0]
    cc = c_ctx.reshape(1, D)
    nw = 2 * HEADS * HD
    view = dict(c_ctx=(1, D), ada_b=(1, 3 * D), norm_g=(1, D), conv_w=(CONV_W, HD), conv_b=(1, D), lru_wa=(nw, HD),
                lru_ba=(2 * HEADS, HD), lru_wx=(nw, HD), lru_bx=(2 * HEADS, HD), lru_lambda=(2, HD), sgu_ln_g=(1, D),
                sgu_ln_b=(1, D), sgu_w=(HEADS * CHUNK, CHUNK), sgu_b=(HEADS, CHUNK), final_g=(1, D))

    zx, hn, w_full, w_out_b, modx, modc, c_all, cw_full, lam_full = _front_project(
        xr, c, cc, ada_w[0], ada_b, norm_g, w_in[0], w_out[0], conv_w[0], lru_lambda[0], me)
    zc, hnc = _project(ctxr, modc, norm_g, w_full, D, LC, "project_ctx")
    ba, bx = lru_ba.reshape(view["lru_ba"]), lru_bx.reshape(view["lru_bx"])
    yl, wout_all = _lru_forward(zx, zc, cw_full, conv_b, lru_wa[0], lru_wx[0], ba, bx, lam_full, [w_out_b], ["ag"])
    wout_full = wout_all.reshape(D_MIX, D)
    dz, dyl, dxn, ycat, dob, dws, dbst, mvec = _mixer_loss(
        xr, tgt, zx, yl, modx, final_g.reshape(1, D), sgu_ln_g, sgu_ln_b, sgu_w[0], sgu_b[0], wout_full, ROWS)

    (wout_sums,) = _grad_w(ycat, dob, None, None, L // 2, "grad_w_out", D, 0, 1, "rows", 1)
    (rest_sums,) = _grad_w(hn, dz, None, None, L, "grad_w_in_rest", 2 * W_IN_SHARD, 1, 3, "cols", 2)
    dz, dxac, dwa, dwx, dba, dbx, dlam, dcw, dcb, win_parts, wout_parts, _, _ = _lru_backward(
        zx, zc, dyl, dz, cw_full, conv_b, lru_wa[0], lru_wx[0], ba, bx, lam_full, [rest_sums, wout_sums],
        first_chips=[1, 0])
    mats = [dwa.reshape(N_DEV, nw // N_DEV, HD), dwx.reshape(N_DEV, nw // N_DEV, HD), dws]
    first_sums, *mat_parts = _grad_w(hn, dz, hnc, dxac, L, "grad_w_in_first", 2 * W_IN_SHARD, 0, 1, "cols", 3,
                                     riders=mats)[:4]
    gx, xvec, win_parts = _grad_rows(
        xr, dz, w_full, modx, norm_g, dxn, D_IN, ROWS, "grad_rows_x", chip_sums=[first_sums], first_chips=[0],
        dests=[win_parts])[:3]
    (cvec,) = _grad_rows(ctxr, dxac, w_full, modc, norm_g, None, D, LC, "grad_rows_ctx")
    red, redq, *rest = _reduce_small(
        [(mvec, 5), (xvec, 3), (cvec, 3), (dlam, 2), (dcw, CONV_W), (dcb, 1)],
        [(dba, 2 * HEADS), (dbx, 2 * HEADS), (dbst, HEADS)], mat_parts, ada_w[0], me)
    mats_all, (cparts, dmod, gab, loss) = rest[:3], rest[3:]

    g_w_in, d_w_in, nm_w_in, nv_w_in = _adamw_reduced(win_parts, w_in[0], m_w_in[0], v_w_in[0], 2 * ROWS, "adamw_w_in")
    g_w_out, d_w_out, nm_w_out, nv_w_out = _adamw_reduced(wout_parts, w_out[0], m_w_out[0], v_w_out[0], ROWS // 2,
                                                          "adamw_w_out")
    g_ada, d_ada, nm_ada, nv_ada = _adamw_ada(c_all, cc, dmod, ada_w[0], m_ada_w[0], v_ada_w[0], me)
    ws = {k: args[k].reshape(view[k]) for k in _SMALL}
    ms = {k: args["m_" + k].reshape(view[k]) for k in _SMALL}
    vs = {k: args["v_" + k].reshape(view[k]) for k in _SMALL}
    small = _adamw_small(red, redq, [m.reshape(-1, HD) for m in mats_all], cparts, gab, ws, ms, vs, me)
    big = dict(w_in=(g_w_in, d_w_in, nm_w_in, nv_w_in), w_out=(g_w_out, d_w_out, nm_w_out, nv_w_out),
               ada_w=(g_ada, d_ada, nm_ada, nv_ada))

    loss = loss.reshape(())
    names = ("c_ctx", "ada_w", "ada_b", "norm_g", "w_in", "conv_w", "conv_b", "lru_wa", "lru_ba", "lru_wx", "lru_bx",
             "lru_lambda", "sgu_ln_g", "sgu_ln_b", "sgu_w", "sgu_b", "w_out", "final_g")
    outs = [loss, gx.reshape(x.shape)]
    for kind in range(4):
        for k in names:
            val = big[k][kind] if k in big else small[kind][k]
            outs.append(val.reshape(args[k].shape))
    return tuple(outs)
```

```python
import functools

import jax
import jax.numpy as jnp
from jax import lax
from jax.experimental import pallas as pl
from jax.experimental.pallas import tpu as pltpu

F32 = jnp.float32
BF16 = jnp.bfloat16

N_DEV = 8
D = 1024
L = 2048
LC = 256
HEADS = 8
HD = 128
CHUNK = 128
D_IN = 5 * D
W_IN_SHARD = D_IN // N_DEV
ROWS = 256
D_MIX = 2 * D
CONV_W = 4
LRU_C = 8.0
NORM_EPS = 1e-6
LN_EPS = 1e-5
ADAM_LR, ADAM_B1, ADAM_B2, ADAM_EPS, ADAM_WD, ADAM_STEP = 0.001, 0.9, 0.999, 1e-08, 0.01, 10

VMEM_LIMIT = 56 * 1024 * 1024

HBM = pl.BlockSpec(memory_space=pltpu.HBM)
VMEM = pl.BlockSpec(memory_space=pltpu.VMEM)
MESH = pl.DeviceIdType.MESH


def _call(body, **kw):
    return pl.pallas_call(body, **kw)


def _params(*sem):
    return pltpu.CompilerParams(dimension_semantics=sem, vmem_limit_bytes=VMEM_LIMIT)


def _sigmoid(x):
    return 0.5 * jnp.tanh(0.5 * x) + 0.5


def _silu_and_grad(x):
    s = _sigmoid(x)
    return x * s, s * (1.0 + x * (1.0 - s))


_G0 = 0.7978845608028654
_G1 = 0.044715


def _gelu_and_grad(x):
    x2 = x * x
    t = jnp.tanh(_G0 * (x + _G1 * x * x2))
    cdf = 0.5 * (1.0 + t)
    return x * cdf, cdf + 0.5 * x * (1.0 - t * t) * (_G0 * (1.0 + 3.0 * _G1 * x2))


def _softplus(z):
    t = jnp.exp(-jnp.abs(z))
    u = 1.0 + t
    log1p = jnp.where(u == 1.0, t, jnp.log(u) * t / jnp.where(u == 1.0, 1.0, u - 1.0))
    return jnp.maximum(z, 0.0) + log1p


def _dot(a, b):
    return jnp.dot(a, b, preferred_element_type=F32)


def _dot_nt(a, b):
    return lax.dot_general(a, b, (((1,), (1,)), ((), ())), preferred_element_type=F32)


def _dot_tn(a, b):
    return lax.dot_general(a, b, (((0,), (0,)), ((), ())), preferred_element_type=F32)


def _rows(shape):
    return lax.broadcasted_iota(jnp.int32, shape, 0)


def _gather2_shapes(arrays, modes):
    return [jax.ShapeDtypeStruct((N_DEV,) + a.shape if m == "ag" else (a.shape[0], N_DEV * a.shape[1]), a.dtype)
            for a, m in zip(arrays, modes)]


def _gather2_sems(n):
    return [pltpu.SemaphoreType.DMA((n, N_DEV - 1)), pltpu.SemaphoreType.DMA((n, N_DEV - 1)),
            pltpu.SemaphoreType.DMA((n,))]


def _barrier(peers):
    sem = pltpu.get_barrier_semaphore()
    for peer in peers:
        pl.semaphore_signal(sem, inc=1, device_id=peer, device_id_type=MESH)
    pl.semaphore_wait(sem, len(peers))


def _gather2_ops(ins, outs, modes, send_sems, recv_sems, local_sems, barrier=False):
    n = len(ins)
    x, y, c = lax.axis_index("x"), lax.axis_index("y"), lax.axis_index("c")
    me, sibling = (x, y, c), (x, y, 1 - c)
    chips = [(x ^ (k >> 1), y ^ (k & 1)) for k in (1, 2, 3)]

    def slot(j, px, py, pc):
        dev = 4 * px + 2 * py + pc
        if modes[j] == "agc":
            w = ins[j].shape[1]
            return outs[j].at[:, pl.ds(pl.multiple_of(dev * w, 128), w)]
        return outs[j].at[dev]

    def copy(j, k, block, to, src=None):
        return pltpu.make_async_remote_copy(
            src_ref=slot(j, *block) if src is None else src, dst_ref=slot(j, *block),
            send_sem=send_sems.at[j, k], recv_sem=recv_sems.at[j, k], device_id=to, device_id_type=MESH)

    def own(j):
        return pltpu.make_async_copy(ins[j], slot(j, *me), local_sems.at[j])

    def first(j):
        return [copy(j, 0, me, sibling, src=ins[j])] + [copy(j, 1 + i, me, (*chip, c), src=ins[j])
                                                        for i, chip in enumerate(chips)]

    def passed(j, i):
        return copy(j, 4 + i, (*chips[i], c), sibling)

    def start():
        if barrier:
            _barrier([sibling] + [(*chip, c) for chip in chips])
        for j in range(n):
            own(j).start()
            for cp in first(j):
                cp.start()

    def forward():
        for i, chip in enumerate(chips):
            for j in range(n):
                copy(j, 1 + i, (*chip, c), me).wait_recv()
                passed(j, i).start()

    def finish():
        for j in range(n):
            copy(j, 0, sibling, me).wait_recv()
            for i, chip in enumerate(chips):
                copy(j, 4 + i, (*chip, 1 - c), me).wait_recv()
            for cp in first(j) + [passed(j, i) for i in range(3)]:
                cp.wait_send()
            own(j).wait()

    return start, forward, finish


def _sibling_barrier(signal=True, wait=True):
    sem = pltpu.get_barrier_semaphore()
    sibling = (lax.axis_index("x"), lax.axis_index("y"), 1 - lax.axis_index("c"))
    if signal:
        pl.semaphore_signal(sem, inc=1, device_id=sibling, device_id_type=MESH)
    if wait:
        pl.semaphore_wait(sem, 1)


def _chips_sems(n):
    return [pltpu.SemaphoreType.DMA((n, 6)), pltpu.SemaphoreType.DMA((n, 6)), pltpu.SemaphoreType.DMA((n,))]


def _chips_stage_shapes(chip_sums):
    return [jax.ShapeDtypeStruct((2, a.shape[1] // 2, a.shape[2]), a.dtype) for a in chip_sums]


def _chips_ops(ins, outs, stages, send_sems, recv_sems, local_sems, first_chips=None, barrier=False):
    x, y, c = lax.axis_index("x"), lax.axis_index("y"), lax.axis_index("c")
    qm = 2 * x + y
    first_chips = first_chips or [0] * len(ins)

    def owns(j, chip):
        lo, cnt = first_chips[j], ins[j].shape[0]
        if lo == 0 and cnt == 4:
            return None
        return jnp.logical_and(chip >= lo, chip < lo + cnt)

    def guarded(cond, fn):
        if cond is None:
            fn()
        else:
            pl.when(cond)(fn)

    def slot(j, chip):
        return jnp.clip(chip - first_chips[j], 0, ins[j].shape[0] - 1)

    def half(j, i):
        h = ins[j].shape[1] // 2
        return pl.ds(i * h, h)

    def copy(j, sem, src, dst, k):
        return pltpu.make_async_remote_copy(
            src_ref=src, dst_ref=dst, send_sem=send_sems.at[j, sem], recv_sem=recv_sems.at[j, sem],
            device_id=(x ^ (k >> 1), y ^ (k & 1), c), device_id_type=MESH)

    def direct(j, k):
        return copy(j, k - 1, ins[j].at[slot(j, qm ^ k)], outs[j].at[qm], k)

    def first_hop(j, k):
        return copy(j, 1 + k, ins[j].at[slot(j, qm ^ 3), half(j, k - 1)], stages[j].at[k - 1], k)

    def second_hop(j, k):
        return copy(j, 3 + k, stages[j].at[2 - k], outs[j].at[qm ^ (3 - k), half(j, 2 - k)], k)

    def local(j):
        return pltpu.make_async_copy(ins[j].at[slot(j, qm)], outs[j].at[qm], local_sems.at[j])

    def start():
        if barrier:
            _barrier([(x ^ (k >> 1), y ^ (k & 1), c) for k in (1, 2)])
        for j in range(len(ins)):
            for k in (1, 2):
                guarded(owns(j, qm ^ 3), lambda j=j, k=k: first_hop(j, k).start())
        for j in range(len(ins)):
            for k in (1, 2):
                guarded(owns(j, qm ^ k), lambda j=j, k=k: direct(j, k).start())
            guarded(owns(j, qm), lambda j=j: local(j).start())

    def forward():
        for j in range(len(ins)):
            for k in (1, 2):
                def pass_on(j=j, k=k):
                    first_hop(j, 3 - k).wait_recv()
                    second_hop(j, k).start()
                guarded(owns(j, qm ^ k), pass_on)

    def finish():
        for j in range(len(ins)):
            for k in (1, 2):
                guarded(owns(j, qm ^ k), lambda j=j, k=k: direct(j, k).wait_send())
                guarded(owns(j, qm ^ k), lambda j=j, k=k: second_hop(j, k).wait_send())
                guarded(owns(j, qm ^ 3), lambda j=j, k=k: first_hop(j, k).wait_send())
                guarded(owns(j, qm), lambda j=j, k=k: direct(j, k).wait_recv())
                guarded(owns(j, qm), lambda j=j, k=k: second_hop(j, k).wait_recv())
            guarded(owns(j, qm), lambda j=j: local(j).wait())

    return start, forward, finish


ARRIVAL = (0, 1, 2, 4, 3, 5, 6, 7)


def _front_project(xr, c, c_ctx, ada_w, ada_b, ng, w_in, w_out, cw, lam, me):
    nloc = ada_w.shape[1]
    ws = W_IN_SHARD
    arrival = jnp.asarray(ARRIVAL, jnp.int32)

    def body(me_ref, arr_ref, x_ref, c_ref, cc_ref, aw_ref, ab_ref, ng_ref, win_ref, wout_ref, cw_ref, lam_ref,
             z_ref, hn_ref, wfull_ref, woutb_ref, modx_ref, modc_ref, call_ref, cwf_ref, lamf_ref,
             wv, call_s, part_s, parts_s, w_send, w_recv, hbm_sems, s_send, s_recv, g_send, g_recv, g_local,
             x_v, x_sem):
        t = pl.program_id(0)
        x_load = pltpu.make_async_copy(x_ref, x_v, x_sem)
        x, y, cidx = lax.axis_index("x"), lax.axis_index("y"), lax.axis_index("c")
        me_i = me_ref[0]
        sibling = (x, y, 1 - cidx)
        chips = [(x ^ (k >> 1), y ^ (k & 1)) for k in (1, 2, 3)]
        g_start, g_pass, g_finish = _gather2_ops([cw_ref, lam_ref], [cwf_ref, lamf_ref], ["agc", "agc"],
                                                 g_send, g_recv, g_local)

        def shard_copy(k, px, py, pc, to, half=None):
            slot = wv.at[4 * px + 2 * py + pc]
            if half is not None:
                slot = slot.at[pl.ds(half * (D // 2), D // 2), :]
            return pltpu.make_async_remote_copy(src_ref=slot, dst_ref=slot, send_sem=w_send.at[k],
                                                recv_sem=w_recv.at[k], device_id=to, device_id_type=MESH)

        def small_gather(src, my_slot, stage):
            copies = []
            for k in range(1, N_DEV):
                peer = (x ^ (k >> 2), y ^ ((k >> 1) & 1), cidx ^ (k & 1))
                cp = pltpu.make_async_remote_copy(src_ref=src, dst_ref=my_slot, send_sem=s_send.at[stage, k - 1],
                                                  recv_sem=s_recv.at[stage, k - 1], device_id=peer,
                                                  device_id_type=MESH)
                cp.start()
                copies.append(cp)
            pltpu.sync_copy(src, my_slot)
            return copies

        def finish_small(copies):
            for cp in copies:
                cp.wait()

        def to_neighbours(half):
            for i in (0, 1):
                shard_copy(1 + i, x, y, cidx, (*chips[i], cidx), half=half).start()

        @pl.when(t == 0)
        def _():
            _barrier([(x ^ (k >> 2), y ^ ((k >> 1) & 1), cidx ^ (k & 1)) for k in range(1, N_DEV)])
            g_start()
            x_load.start()
            wv[me_i] = win_ref[...].astype(BF16)
            woutb_ref[...] = wout_ref[...].astype(BF16)
            shard_copy(0, x, y, cidx, sibling).start()
            conds_sent = small_gather(c_ref, call_s.at[pl.ds(me_i, 1), :], 0)
            to_neighbours(0)
            finish_small(conds_sent)
            call_ref[...] = call_s[...]
            off = pl.multiple_of(me_i * nloc, 128)
            b = ab_ref[:, pl.ds(off, nloc)]
            w = aw_ref[...]
            sx, _ = _silu_and_grad(call_s[...])
            sc, _ = _silu_and_grad(jnp.broadcast_to(cc_ref[...], (8, D)))
            part_s[0:8, :] = _dot(sx, w) + b
            part_s[8:16, :] = _dot(sc, w) + b
            parts_sent = small_gather(part_s, parts_s.at[me_i], 1)
            to_neighbours(1)
            finish_small(parts_sent)
            mine = _rows((16, nloc)) == me_i
            for j in range(N_DEV):
                pj = parts_s[j]
                modx_ref[:, j * nloc:(j + 1) * nloc] = jnp.sum(jnp.where(mine, pj, 0.0), axis=0, keepdims=True)
                modc_ref[:, j * nloc:(j + 1) * nloc] = pj[8:9, :]
            shift, scale1, ngv = modx_ref[:, 0:D], 1.0 + modx_ref[:, D:2 * D], ng_ref[...]
            x_load.wait()
            for r in range(L // ROWS):
                rsl = slice(r * ROWS, (r + 1) * ROWS)
                xv = x_v[rsl, :]
                rs = lax.rsqrt(jnp.mean(xv * xv, axis=-1, keepdims=True) + NORM_EPS)
                hn_ref[rsl, :] = ((xv * rs * ngv) * scale1 + shift).astype(BF16)

        @pl.when(t == 1)
        def _():
            shard_copy(0, x, y, 1 - cidx, sibling).wait_recv()
            g_pass()

        for i in (0, 1):
            @pl.when(t == ARRIVAL.index((2, 4)[i]))
            def _(i=i):
                shard_copy(1 + i, *chips[i], cidx, sibling).wait_recv()
                shard_copy(4 + i, *chips[i], cidx, sibling).start()
                shard_copy((7, 3)[i], *chips[i], cidx, (*chips[1 - i], cidx), half=i).start()

        @pl.when(t == ARRIVAL.index(6))
        def _():
            shard_copy(3, *chips[2], cidx, sibling, half=1).wait_recv()
            shard_copy(7, *chips[2], cidx, sibling, half=0).wait_recv()
            shard_copy(6, *chips[2], cidx, sibling).start()

        for i in range(3):
            @pl.when(t == ARRIVAL.index((3, 5, 7)[i]))
            def _(i=i):
                shard_copy(4 + i, *chips[i], 1 - cidx, sibling).wait_recv()

        @pl.when(t == 2)
        def _():
            g_finish()

        dev = me_i ^ arr_ref[t]
        for r in range(L // (2 * ROWS)):
            rsl = slice(r * 2 * ROWS, (r + 1) * 2 * ROWS)
            z_ref[rsl, :] = _dot(hn_ref[rsl, :], wv[dev])
        col = pl.ds(pl.multiple_of(dev * ws, 128), ws)
        pltpu.make_async_copy(wv.at[dev], wfull_ref.at[:, col], hbm_sems.at[t]).start()

        @pl.when(t == N_DEV - 1)
        def _():
            for k in (0, 1, 2, 4, 5, 6):
                shard_copy(k, x, y, cidx, sibling).wait_send()
            for k in (3, 7):
                shard_copy(k, x, y, cidx, sibling, half=0).wait_send()
            for s in range(N_DEV):
                pltpu.make_async_copy(wv.at[0], wfull_ref.at[:, pl.ds(0, ws)], hbm_sems.at[s]).wait()

    const = lambda *shape: pl.BlockSpec(shape, lambda t, m, a: (0,) * len(shape))
    once = lambda *shape: pl.BlockSpec(shape, lambda t, m, a: (0,) * len(shape), pipeline_mode=pl.Buffered(1))
    return _call(
        body, name="front_project",
        out_shape=[jax.ShapeDtypeStruct((L, D_IN), F32), jax.ShapeDtypeStruct((L, D), BF16),
                   jax.ShapeDtypeStruct((D, D_IN), BF16), jax.ShapeDtypeStruct(w_out.shape, BF16),
                   jax.ShapeDtypeStruct((1, 3 * D), F32), jax.ShapeDtypeStruct((1, 3 * D), F32),
                   jax.ShapeDtypeStruct((N_DEV, D), F32), jax.ShapeDtypeStruct((CONV_W, D), F32),
                   jax.ShapeDtypeStruct((2, D), F32)],
        grid_spec=pltpu.PrefetchScalarGridSpec(
            num_scalar_prefetch=2, grid=(N_DEV,),
            in_specs=[HBM, const(1, D), const(1, D), once(D, nloc), const(1, 3 * D), const(1, D),
                      once(D, ws), once(*w_out.shape), HBM, HBM],
            out_specs=[pl.BlockSpec((L, ws), lambda t, m, a: (0, m[0] ^ a[t])), const(L, D), HBM,
                       const(*w_out.shape),
                       const(1, 3 * D), const(1, 3 * D), const(N_DEV, D), HBM, HBM],
            scratch_shapes=[pltpu.VMEM((N_DEV, D, ws), BF16), pltpu.VMEM((N_DEV, D), F32), pltpu.VMEM((16, nloc), F32),
                            pltpu.VMEM((N_DEV, 16, nloc), F32), pltpu.SemaphoreType.DMA((8,)),
                            pltpu.SemaphoreType.DMA((8,)), pltpu.SemaphoreType.DMA((N_DEV,)),
                            pltpu.SemaphoreType.DMA((2, N_DEV - 1)), pltpu.SemaphoreType.DMA((2, N_DEV - 1))]
            + _gather2_sems(2) + [pltpu.VMEM((L, D), F32), pltpu.SemaphoreType.DMA(())]),
        compiler_params=pltpu.CompilerParams(dimension_semantics=("arbitrary",), vmem_limit_bytes=VMEM_LIMIT,
                                             has_side_effects=True, collective_id=10),
    )(me, arrival, pltpu.with_memory_space_constraint(xr, pltpu.HBM), c, c_ctx, ada_w, ada_b, ng, w_in, w_out,
      pltpu.with_memory_space_constraint(cw, pltpu.HBM), pltpu.with_memory_space_constraint(lam, pltpu.HBM))


def _project(xr, mod, ng, w, ncols, tm, name):
    rows = xr.shape[0]

    def body(x_ref, sh_ref, sc_ref, ng_ref, w_ref, z_ref, hn_ref):
        x = x_ref[...]
        rs = lax.rsqrt(jnp.mean(x * x, axis=-1, keepdims=True) + NORM_EPS)
        hn = (x * rs * ng_ref[...]) * (1.0 + sc_ref[...]) + sh_ref[...]
        hb = hn.astype(BF16)
        hn_ref[...] = hb
        for n in range(ncols // D):
            z_ref[:, n * D:(n + 1) * D] = _dot(hb, w_ref[:, n * D:(n + 1) * D])

    vec = pl.BlockSpec((1, D), lambda i: (0, 0))
    return _call(
        body, name=name, grid=(rows // tm,),
        out_shape=[jax.ShapeDtypeStruct((rows, ncols), F32), jax.ShapeDtypeStruct((rows, D), BF16)],
        in_specs=[pl.BlockSpec((tm, D), lambda i: (i, 0)), vec, pl.BlockSpec((1, D), lambda i: (0, 1)), vec,
                  pl.BlockSpec((D, ncols), lambda i: (0, 0), pipeline_mode=pl.Buffered(1))],
        out_specs=[pl.BlockSpec((tm, ncols), lambda i: (i, 0)), pl.BlockSpec((tm, D), lambda i: (i, 0))],
        compiler_params=_params("arbitrary"),
    )(xr, mod, mod, ng, w)


def _scan_pair(af_ref, uf_ref, hf_ref, h0f, ab_ref, ub_ref, hb_ref, h0b, t_len):
    span = 8 * SCAN_BLOCKS
    nit = t_len // span
    rows = _rows((8, HD))

    def local_scan(a, b, forward):
        for s in (1, 2, 4):
            sh = s if forward else 8 - s
            m = rows >= s if forward else rows < 8 - s
            b = a * jnp.where(m, pltpu.roll(b, sh, 0), 0.0) + b
            a = a * jnp.where(m, pltpu.roll(a, sh, 0), 1.0)
        return a, b

    def span_scan(a_ref, u_ref, h_ref, off, carry, forward):
        order = range(SCAN_BLOCKS) if forward else range(SCAN_BLOCKS - 1, -1, -1)
        last = slice(7, 8) if forward else slice(0, 1)
        for q in order:
            rs = pl.ds(off + 8 * q, 8)
            a, b = local_scan(a_ref[rs, :], u_ref[rs, :], forward)
            h_ref[rs, :] = b + a * carry
            carry = a[last, :] * carry + b[last, :]
        return carry

    def body(k, carry):
        cf, cb = carry
        cf = span_scan(af_ref, uf_ref, hf_ref, pl.multiple_of(k * span, span), cf, True)
        cb = span_scan(ab_ref, ub_ref, hb_ref, pl.multiple_of((nit - 1 - k) * span, span), cb, False)
        return cf, cb

    return lax.fori_loop(0, nit, body, (h0f, h0b))


SCAN_BLOCKS = 16


def _shifted(pad_ref, x, offsets, before=0.0, after=0.0):
    n = x.shape[0]
    pad_ref[0:8, :] = jnp.broadcast_to(jnp.asarray(before, F32), (8, x.shape[1]))
    pad_ref[8:8 + n, :] = x
    pad_ref[8 + n:16 + n, :] = jnp.broadcast_to(jnp.asarray(after, F32), (8, x.shape[1]))
    return [pad_ref[8 + o:8 + o + n, :] for o in offsets]


def _conv(xa, cw, cb, pad_ref):
    xm1, xp1, xp2 = _shifted(pad_ref, xa, (-1, 1, 2))
    return xm1 * cw[0:1, :] + xa * cw[1:2, :] + xp1 * cw[2:3, :] + xp2 * cw[3:4, :] + cb


def _gates(xc, wa, wx, ba, bx, nsp):
    xb = xc.astype(BF16)
    r = _sigmoid(_dot(xb, wa) + ba)
    i = _sigmoid(_dot(xb, wx) + bx)
    log_a = r * nsp
    a = jnp.exp(log_a)
    g2 = jnp.tanh(log_a) * (-1.0 - a * a)
    rg = lax.rsqrt(jnp.maximum(g2, 1e-30))
    return r, i, a, g2 * rg, rg


def _lru_param_specs():
    h4 = pl.BlockSpec((2, 1, HD, HD), lambda h: (0, h, 0, 0))
    v2 = pl.BlockSpec((2, HD), lambda h: (0, h))
    b16 = pl.BlockSpec((2 * HEADS, HD), lambda h: (0, 0))
    return dict(
        xa=pl.BlockSpec((L, HD), lambda h: (0, h)), xac=pl.BlockSpec((LC, HD), lambda h: (0, h)),
        cw=pl.BlockSpec((CONV_W, HD), lambda h: (0, h)), cb=pl.BlockSpec((1, HD), lambda h: (0, h)), h4=h4, v2=v2,
        b16=b16)


def _bias_row(ref, d):
    mask = _rows((2 * HEADS, HD)) == d * HEADS + pl.program_id(0)
    return jnp.sum(jnp.where(mask, ref[...], 0.0), axis=0, keepdims=True), mask


def _lru_forward(zx, zc, cw, cb, wa, wx, ba, bx, lam, gather, gather_modes):
    ng_ = len(gather)

    def body(xa_ref, xac_ref, cw_ref, cb_ref, wa_ref, wx_ref, ba_ref, bx_ref, lam_ref, *rest):
        yl_ref = rest[ng_]
        af, uf, hf, ab, ub, hb, pad_s = rest[2 * ng_ + 1:2 * ng_ + 8]
        start, pass_on, finish = _gather2_ops(rest[:ng_], rest[ng_ + 1:2 * ng_ + 1], gather_modes,
                                              *rest[2 * ng_ + 8:], barrier=True)
        pl.when(pl.program_id(0) == 0)(start)
        pl.when(pl.program_id(0) == HEADS // 2)(pass_on)
        pl.when(pl.program_id(0) == HEADS - 1)(finish)
        cwv, cbv = cw_ref[...], cb_ref[...]
        nsp = (-LRU_C) * _softplus(-lam_ref[...])

        def forward(xa, t_len, h0f, h0b):
            xc = _conv(xa, cwv, cbv, pad_s)
            for d, (a_ref, u_ref) in enumerate(((af, uf), (ab, ub))):
                _, i, a, gamma, _ = _gates(xc, wa_ref[d, 0].astype(BF16), wx_ref[d, 0].astype(BF16),
                                           _bias_row(ba_ref, d)[0], _bias_row(bx_ref, d)[0], nsp[d:d + 1, :])
                a_ref[0:t_len, :] = a
                u_ref[0:t_len, :] = gamma * (i * xc)
            return _scan_pair(af, uf, hf, h0f, ab, ub, hb, h0b, t_len)

        z = jnp.zeros((1, HD), F32)
        h0f, h0b = forward(xac_ref[...], LC, z, z)
        forward(xa_ref[...], L, h0f, h0b)
        yl_ref[...] = hf[...] + hb[...]

    s = _lru_param_specs()
    return _call(
        body, name="lru_forward", grid=(HEADS,),
        out_shape=[jax.ShapeDtypeStruct((L, D), F32)] + _gather2_shapes(gather, gather_modes),
        in_specs=[s["xa"], s["xac"], s["cw"], s["cb"], s["h4"], s["h4"], s["b16"], s["b16"], s["v2"]] + [HBM] * ng_,
        out_specs=[pl.BlockSpec((L, HD), lambda h: (0, h))] + [HBM] * ng_,
        scratch_shapes=[pltpu.VMEM((L, HD), F32)] * 6 + [pltpu.VMEM((L + 16, HD), F32)] + _gather2_sems(ng_),
        compiler_params=pltpu.CompilerParams(dimension_semantics=("arbitrary",), vmem_limit_bytes=VMEM_LIMIT,
                                             has_side_effects=True, collective_id=5),
    )(zx, zc, cw, cb, wa, wx, ba, bx, lam, *[pltpu.with_memory_space_constraint(a, pltpu.HBM) for a in gather])


def _lru_backward(zx, zc, dyl, dz, cw, cb, wa, wx, ba, bx, lam, chip_sums, first_chips=None):
    nr = len(chip_sums)

    def body(xa_ref, xac_ref, dyl_ref, dz_in, cw_ref, cb_ref, wa_ref, wx_ref, ba_ref, bx_ref, lam_ref, *rest):
        (dxa_ref, dxac_ref, dwa_ref, dwx_ref, dba_ref, dbx_ref, dlam_ref, dcw_ref,
         dcb_ref) = rest[nr:nr + 9]
        main_s, ctx_s, pad_s = rest[3 * nr + 9:3 * nr + 12]
        if nr:
            start, forward, finish = _chips_ops(rest[:nr], rest[nr + 9:2 * nr + 9], rest[2 * nr + 9:3 * nr + 9],
                                                *rest[3 * nr + 12:], first_chips=first_chips, barrier=True)
            pl.when(pl.program_id(0) == 0)(start)
            pl.when(pl.program_id(0) == HEADS // 2)(forward)
            pl.when(pl.program_id(0) == HEADS - 1)(finish)
        del dz_in

        @pl.when(pl.program_id(0) == 0)
        def _():
            dba_ref[...] = jnp.zeros_like(dba_ref)
            dbx_ref[...] = jnp.zeros_like(dbx_ref)

        cwv, cbv = cw_ref[...], cb_ref[...]
        lamv = lam_ref[...]
        sp = _softplus(-lamv)
        nsp = (-LRU_C) * sp
        z = jnp.zeros((1, HD), F32)

        def wmat(ref, d):
            return ref[d, 0].astype(BF16)

        def workspace(s):
            return dict(a=(s.at[0], s.at[1]), u=(s.at[2], s.at[3]), h=(s.at[4], s.at[5]), rho=(s.at[6], s.at[7]),
                        saved=(tuple(s.at[8 + k] for k in range(4)), tuple(s.at[12 + k] for k in range(4))),
                        xc=s.at[16])

        def forward(ws, xa, t_len, h0f, h0b):
            xc = _conv(xa, cwv, cbv, pad_s)
            ws["xc"][...] = xc
            for d in (0, 1):
                vals = _gates(xc, wmat(wa_ref, d), wmat(wx_ref, d), _bias_row(ba_ref, d)[0],
                              _bias_row(bx_ref, d)[0], nsp[d:d + 1, :])
                r, i, a, gamma, rg = vals
                ws["a"][d][...] = a
                ws["u"][d][...] = gamma * (i * xc)
                for ref, val in zip(ws["saved"][d], (r, i, gamma, rg)):
                    ref[...] = val
            return _scan_pair(ws["a"][0], ws["u"][0], ws["h"][0], h0f, ws["a"][1], ws["u"][1], ws["h"][1], h0b,
                              t_len)

        def backward(ws, xa, t_len, h0f, h0b, dhf, dhb, first):
            xc = ws["xc"][...]
            (af, ab), (uf, ub), (hf, hb), (rf, rb) = ws["a"], ws["u"], ws["h"], ws["rho"]
            uf[...] = ab[...] * dhb
            ub[...] = af[...] * dhf
            rho_b_last, rho_f_first = _scan_pair(ab, uf, rb, z, af, ub, rf, z, t_len)
            dxc = jnp.zeros((t_len, HD), F32)
            dsp = []
            for d in (0, 1):
                r, i, gamma, rg = (ref[...] for ref in ws["saved"][d])
                a = ws["a"][d][...]
                if d == 0:
                    lam_t = dhf + _shifted(pad_s, rf[...], (1,))[0]
                    h_prev = _shifted(pad_s, hf[...], (-1,), before=h0f)[0]
                else:
                    lam_t = dhb + _shifted(pad_s, rb[...], (-1,))[0]
                    h_prev = _shifted(pad_s, hb[...], (1,), after=h0b)[0]
                da = lam_t * h_prev
                lx = lam_t * xc
                d_i = lx * gamma
                d_gamma = lx * i
                dxc = dxc + lam_t * (gamma * i)
                d_log_a = a * (da - d_gamma * (a * rg))
                dsp.append(jnp.sum(d_log_a * r, axis=0, keepdims=True) * (-LRU_C))
                d_pre_r = d_log_a * nsp[d:d + 1, :] * (r * (1.0 - r))
                d_pre_i = d_i * (i * (1.0 - i))
                prb, pib, xb = d_pre_r.astype(BF16), d_pre_i.astype(BF16), xc.astype(BF16)
                dxc = dxc + _dot_nt(prb, wmat(wa_ref, d)) + _dot_nt(pib, wmat(wx_ref, d))
                g_wa, g_wx = _dot_tn(xb, prb), _dot_tn(xb, pib)
                g_ba = jnp.sum(d_pre_r, axis=0, keepdims=True)
                g_bx = jnp.sum(d_pre_i, axis=0, keepdims=True)
                mask = _bias_row(ba_ref, d)[1]
                dba_ref[...] += jnp.where(mask, g_ba, 0.0)
                dbx_ref[...] += jnp.where(mask, g_bx, 0.0)
                if first:
                    dwa_ref[d, 0] = g_wa
                    dwx_ref[d, 0] = g_wx
                else:
                    dwa_ref[d, 0] += g_wa
                    dwx_ref[d, 0] += g_wx
            g_lam = jnp.concatenate(dsp, axis=0) * (-_sigmoid(-lamv))
            dm1, dp1, dm2 = _shifted(pad_s, dxc, (-1, 1, -2))
            dxa = dp1 * cwv[0:1, :] + dxc * cwv[1:2, :] + dm1 * cwv[2:3, :] + dm2 * cwv[3:4, :]
            xm1, xp1, xp2 = _shifted(pad_s, xa, (-1, 1, 2))
            g_cw = jnp.concatenate([jnp.sum(dxc * v, axis=0, keepdims=True) for v in (xm1, xa, xp1, xp2)], axis=0)
            g_cb = jnp.sum(dxc, axis=0, keepdims=True)
            if first:
                dlam_ref[...] = g_lam
                dcw_ref[...] = g_cw
                dcb_ref[...] = g_cb
            else:
                dlam_ref[...] += g_lam
                dcw_ref[...] += g_cw
                dcb_ref[...] += g_cb
            return dxa, rho_f_first, rho_b_last

        ws_x, ws_c = workspace(main_s), workspace(ctx_s)
        h0f, h0b = forward(ws_c, xac_ref[...], LC, z, z)
        forward(ws_x, xa_ref[...], L, h0f, h0b)
        dh = dyl_ref[...]
        dxa, dh0f, dh0b = backward(ws_x, xa_ref[...], L, h0f, h0b, dh, dh, True)
        dxa_ref[...] = dxa.astype(BF16)
        rc = _rows((LC, HD))
        dxac, _, _ = backward(ws_c, xac_ref[...], LC, z, z, jnp.where(rc == LC - 1, dh0f, 0.0),
                              jnp.where(rc == 0, dh0b, 0.0), False)
        dxac_ref[...] = dxac.astype(BF16)

    s = _lru_param_specs()
    col = lambda r: pl.BlockSpec((r, HD), lambda h: (0, h))
    return _call(
        body, name="lru_backward", grid=(HEADS,),
        out_shape=[jax.ShapeDtypeStruct((L, D_IN), BF16), jax.ShapeDtypeStruct((LC, D), BF16),
                   jax.ShapeDtypeStruct((2, HEADS, HD, HD), F32), jax.ShapeDtypeStruct((2, HEADS, HD, HD), F32),
                   jax.ShapeDtypeStruct((2 * HEADS, HD), F32), jax.ShapeDtypeStruct((2 * HEADS, HD), F32),
                   jax.ShapeDtypeStruct((2, D), F32), jax.ShapeDtypeStruct((CONV_W, D), F32),
                   jax.ShapeDtypeStruct((1, D), F32)] + [jax.ShapeDtypeStruct((4,) + a.shape[1:], a.dtype)
                                                          for a in chip_sums] + _chips_stage_shapes(chip_sums),
        in_specs=[s["xa"], s["xac"], col(L), pl.BlockSpec(memory_space=pl.ANY), s["cw"], s["cb"], s["h4"], s["h4"],
                  s["b16"], s["b16"], s["v2"]] + [HBM] * nr,
        out_specs=[col(L), col(LC), s["h4"], s["h4"], s["b16"], s["b16"], s["v2"], col(CONV_W), col(1)]
        + [HBM] * (2 * nr),
        scratch_shapes=[pltpu.VMEM((17, L, HD), F32), pltpu.VMEM((17, LC, HD), F32), pltpu.VMEM((L + 16, HD), F32)]
        + (_chips_sems(nr) if nr else []),
        input_output_aliases={3: 0},
        compiler_params=pltpu.CompilerParams(dimension_semantics=("arbitrary",), vmem_limit_bytes=VMEM_LIMIT,
                                             has_side_effects=True, collective_id=6 if nr else None),
    )(zx, zc, dyl, dz, cw, cb, wa, wx, ba, bx, lam, *[pltpu.with_memory_space_constraint(a, pltpu.HBM)
                                                       for a in chip_sums])


def _mixer_loss(x, tgt, zx, yl, gx, fg, lng, lnb, ws, bst, wout, tm):
    ncht = tm // CHUNK

    def body(x_ref, t_ref, ga_ref, u_ref, v_ref, gb_ref, yl_ref, gx_ref, fg_ref, lng_ref, lnb_ref, ws_ref,
             bst_ref, wout_ref,
             dz_ref, dyl_ref, dxn_ref, y_s, do_ref, dws_ref, dbst_ref, vec_ref,
             vn_s, mix_s, dm_s, dvn_s, bst_s, dbst_s):
        step = pl.program_id(0)

        @pl.when(step == 0)
        def _():
            dws_ref[...] = jnp.zeros_like(dws_ref)
            dbst_s[...] = jnp.zeros_like(dbst_s)
            vec_ref[...] = jnp.zeros_like(vec_ref)
            bst_s[...] = bst_ref[...].T

        u, v = u_ref[...], v_ref[...]
        ug, dug_du = _gelu_and_grad(u)
        vg, dvg_dv = _gelu_and_grad(v)
        mu = jnp.mean(vg, axis=-1, keepdims=True)
        vc = vg - mu
        rstd = lax.rsqrt(jnp.mean(vc * vc, axis=-1, keepdims=True) + LN_EPS)
        vhat = vc * rstd
        lngv = lng_ref[...]
        vn_s[...] = (vhat * lngv + lnb_ref[...]).astype(BF16)
        for ch in range(ncht):
            rs = slice(ch * CHUNK, (ch + 1) * CHUNK)
            for g in range(HEADS):
                cs = slice(g * HD, (g + 1) * HD)
                mix_s[rs, cs] = _dot(ws_ref[g].astype(BF16), vn_s[rs, cs]) + bst_s[:, g:g + 1]
        mixed = mix_s[...]
        ga, gb, yl = ga_ref[...], gb_ref[...], yl_ref[...]
        sga, dsga = _silu_and_grad(ga)
        sgb, dsgb = _silu_and_grad(gb)
        ys = ug * mixed
        y_s[:, 0:D] = (yl * sga).astype(BF16)
        y_s[:, D:D_MIX] = (ys * sgb).astype(BF16)
        o = _dot(y_s[...], wout_ref[...])
        gxv, fgv = gx_ref[...], fg_ref[...]
        xn = x_ref[...] + gxv * o
        rs2 = lax.rsqrt(jnp.mean(xn * xn, axis=-1, keepdims=True) + NORM_EPS)
        xh = xn * rs2
        diff = xh * fgv - t_ref[...]
        vec_ref[R_LOSS:R_LOSS + 1, :] += jnp.full((1, D), jnp.sum(diff * diff) * (0.5 / D), F32)
        dout = diff * (1.0 / D)
        w = dout * fgv
        dxn = rs2 * (w - xh * jnp.mean(w * xh, axis=-1, keepdims=True))
        dxn_ref[...] = dxn
        vec_ref[0:1, :] += jnp.sum(dxn * o, axis=0, keepdims=True)
        vec_ref[1:2, :] += jnp.sum(dout * xh, axis=0, keepdims=True)
        dob = (dxn * gxv).astype(BF16)
        do_ref[...] = dob
        dy = _dot_nt(dob, wout_ref[...])
        dya, dyb = dy[:, 0:D], dy[:, D:D_MIX]
        dyl_ref[...] = dya * sga
        dys = dyb * sgb
        dz_ref[:, 0:D] = jnp.zeros((tm, D), BF16)
        dz_ref[:, D:2 * D] = (dya * yl * dsga).astype(BF16)
        dz_ref[:, 2 * D:3 * D] = (dys * mixed * dug_du).astype(BF16)
        dz_ref[:, 4 * D:5 * D] = (dyb * ys * dsgb).astype(BF16)
        dm = dys * ug
        dm_s[...] = dm.astype(BF16)
        for g in range(HEADS):
            cs = slice(g * HD, (g + 1) * HD)
            dbst_s[:, g:g + 1] += sum(jnp.sum(dm[ch * CHUNK:(ch + 1) * CHUNK, cs], axis=1, keepdims=True)
                                      for ch in range(ncht))
            for ch in range(ncht):
                rs = slice(ch * CHUNK, (ch + 1) * CHUNK)
                dws_ref[g] += _dot_nt(dm_s[rs, cs], vn_s[rs, cs])
                dvn_s[rs, cs] = _dot_tn(ws_ref[g].astype(BF16), dm_s[rs, cs])
        dvn = dvn_s[...]
        vec_ref[2:3, :] += jnp.sum(dvn * vhat, axis=0, keepdims=True)
        vec_ref[3:4, :] += jnp.sum(dvn, axis=0, keepdims=True)
        dvh = dvn * lngv
        dvg = rstd * (dvh - jnp.mean(dvh, axis=-1, keepdims=True) - vhat * jnp.mean(dvh * vhat, axis=-1, keepdims=True))
        dz_ref[:, 3 * D:4 * D] = (dvg * dvg_dv).astype(BF16)

        @pl.when(step == pl.num_programs(0) - 1)
        def _():
            dbst_ref[...] = dbst_s[...].T

    tile = pl.BlockSpec((tm, D), lambda i: (i, 0))
    zcol = lambda n: pl.BlockSpec((tm, D), lambda i: (i, n))
    vec = pl.BlockSpec((1, D), lambda i: (0, 0))
    full = lambda *s: pl.BlockSpec(s, lambda i: (0,) * len(s))
    return _call(
        body, name="mixer_loss", grid=(L // tm,),
        out_shape=[jax.ShapeDtypeStruct((L, D_IN), BF16), jax.ShapeDtypeStruct((L, D), F32),
                   jax.ShapeDtypeStruct((L, D), F32), jax.ShapeDtypeStruct((L, D_MIX), BF16),
                   jax.ShapeDtypeStruct((L, D), BF16),
                   jax.ShapeDtypeStruct((HEADS, CHUNK, CHUNK), F32), jax.ShapeDtypeStruct((HEADS, CHUNK), F32),
                   jax.ShapeDtypeStruct((8, D), F32)],
        in_specs=[tile, tile, zcol(1), zcol(2), zcol(3), zcol(4), tile, pl.BlockSpec((1, D), lambda i: (0, 2)),
                  vec, vec, vec,
                  full(HEADS, CHUNK, CHUNK), full(HEADS, CHUNK),
                  pl.BlockSpec((D_MIX, D), lambda i: (0, 0), pipeline_mode=pl.Buffered(1))],
        out_specs=[pl.BlockSpec((tm, D_IN), lambda i: (i, 0)), tile, tile,
                   pl.BlockSpec((tm, D_MIX), lambda i: (i, 0)), tile,
                   full(HEADS, CHUNK, CHUNK), full(HEADS, CHUNK), full(8, D)],
        scratch_shapes=[pltpu.VMEM((tm, D), BF16), pltpu.VMEM((tm, D), F32),
                        pltpu.VMEM((tm, D), BF16), pltpu.VMEM((tm, D), F32)] + [pltpu.VMEM((CHUNK, HEADS), F32)] * 2,
        compiler_params=_params("arbitrary"),
    )(x, tgt, zx, zx, zx, zx, yl, gx, fg, lng, lnb, ws, bst, wout)


def _grad_w(a, b, a2, b2, tk, name, bw, first, nblocks, split, barrier_id, riders=()):
    nk = a.shape[0] // tk
    m = a.shape[1]
    with_ctx = a2 is not None
    if split == "cols":
        slots, r, w = nblocks, m, bw // 2
        piece = lambda q, pc: (slice(None), slice(pc * w, (pc + 1) * w))
    else:
        slots, r, w = 4, m // 8, bw
        piece = lambda q, pc: (slice((2 * q + pc) * r, (2 * q + pc + 1) * r), slice(None))

    nr = len(riders)

    def body(*refs):
        a_ref, b_ref = refs[:2]
        a2_ref, b2_ref = refs[2:4] if with_ctx else (None, None)
        base = 4 if with_ctx else 2
        sums_ref = refs[base + nr]
        s0 = base + 3 * nr + 1
        acc, mine_v, send_v, stage_v, send_sems, recv_sems = refs[s0:s0 + 6]
        n, k = pl.program_id(0), pl.program_id(1)
        x, y, c = lax.axis_index("x"), lax.axis_index("y"), lax.axis_index("c")

        def to_sibling(s):
            return pltpu.make_async_remote_copy(src_ref=send_v.at[s], dst_ref=stage_v.at[s], send_sem=send_sems.at[s],
                                                recv_sem=recv_sems.at[s], device_id=(x, y, 1 - c),
                                                device_id_type=MESH)

        if nr:
            rider_in = refs[base:base + nr]
            own_v, got_v, psum_v = (refs[s0 + 9 + i * nr:s0 + 9 + (i + 1) * nr] for i in range(3))
            p_send, p_recv, p_local = refs[s0 + 9 + 3 * nr:s0 + 12 + 3 * nr]
            c_start, c_forward, c_finish = _chips_ops(psum_v, refs[base + nr + 1:base + 2 * nr + 1],
                                                      refs[base + 2 * nr + 1:base + 3 * nr + 1], *refs[s0 + 6:s0 + 9])

            @pl.when(jnp.logical_and(n == 0, k == 0))
            def _():
                _barrier([(x, y, 1 - c)] + [(x ^ (j >> 1), y ^ (j & 1), c) for j in (1, 2)])
                copies = []
                for j in range(nr):
                    for q in range(4):
                        copies.append(pltpu.make_async_remote_copy(
                            src_ref=rider_in[j].at[2 * q + 1 - c], dst_ref=got_v[j].at[q], send_sem=p_send.at[j, q],
                            recv_sem=p_recv.at[j, q], device_id=(x, y, 1 - c), device_id_type=MESH))
                        copies.append(pltpu.make_async_copy(rider_in[j].at[2 * q + c], own_v[j].at[q],
                                                            p_local.at[j, q]))
                for cp in copies:
                    cp.start()
                for cp in copies:
                    cp.wait()
                for j in range(nr):
                    psum_v[j][...] = (own_v[j][...] + got_v[j][...]).astype(BF16)
                c_start()
        else:
            pl.when(jnp.logical_and(n == 0, k == 0))(functools.partial(_sibling_barrier, wait=False))

        @pl.when(k == 0)
        def _():
            acc[...] = _dot_tn(a_ref[...], b_ref[...])

        if nk > 1:
            @pl.when(k > 0)
            def _():
                acc[...] += _dot_tn(a_ref[...], b_ref[...])

        if with_ctx:
            @pl.when(jnp.logical_and(k == nk - 1, n == 0))
            def _():
                acc[:, 0:b2_ref.shape[1]] += _dot_tn(a2_ref[...], b2_ref[...])

        if nr:
            pl.when(jnp.logical_and(k == nk - 1, n == nblocks - 1))(c_forward)

        def hand_over(s, q):
            for pc in (0, 1):
                @pl.when(c == pc)
                def _(pc=pc):
                    mine_v[s] = acc[piece(q, pc)]
                    send_v[s] = acc[piece(q, 1 - pc)].astype(BF16)
            to_sibling(s).start()

        for i in range(nblocks):
            @pl.when(jnp.logical_and(k == nk - 1, n == i))
            def _(i=i):
                if i == 0 and not nr:
                    _sibling_barrier(signal=False)
                if split == "cols":
                    hand_over(i, 0)
                else:
                    for q in range(4):
                        hand_over(q, q)

        @pl.when(jnp.logical_and(k == nk - 1, n == nblocks - 1))
        def _():
            for s in range(slots):
                to_sibling(s).wait_recv()
                sums_ref[s] = (mine_v[s] + stage_v[s].astype(F32)).astype(BF16)
            for s in range(slots):
                to_sibling(s).wait_send()
            if nr:
                c_finish()

    in_specs = [pl.BlockSpec((tk, m), lambda n, k: (k, 0)), pl.BlockSpec((tk, bw), lambda n, k: (k, n + first))]
    args = [a, b]
    if with_ctx:
        in_specs += [pl.BlockSpec(a2.shape, lambda n, k: (0, 0)), pl.BlockSpec(b2.shape, lambda n, k: (0, 0))]
        args += [a2, b2]
    in_specs += [HBM] * nr
    args += [pltpu.with_memory_space_constraint(s, pltpu.HBM) for s in riders]
    rider_sums = [jax.ShapeDtypeStruct((4,) + s.shape[1:], BF16) for s in riders]
    rider_scratch = []
    if nr:
        rider_scratch = (_chips_sems(nr) + [pltpu.VMEM(s.shape, F32) for s in rider_sums] * 2
                         + [pltpu.VMEM(s.shape, BF16) for s in rider_sums]
                         + [pltpu.SemaphoreType.DMA((nr, 4))] * 3)
    return _call(
        body, name=name, grid=(nblocks, nk),
        out_shape=[jax.ShapeDtypeStruct((slots, r, w), BF16)] + rider_sums + _chips_stage_shapes(rider_sums),
        in_specs=in_specs, out_specs=[pl.BlockSpec((slots, r, w), lambda n, k: (0, 0, 0))] + [HBM] * (2 * nr),
        scratch_shapes=[pltpu.VMEM((m, bw), F32), pltpu.VMEM((slots, r, w), F32), pltpu.VMEM((slots, r, w), BF16),
                        pltpu.VMEM((slots, r, w), BF16), pltpu.SemaphoreType.DMA((slots,)),
                        pltpu.SemaphoreType.DMA((slots,))] + rider_scratch,
        compiler_params=pltpu.CompilerParams(dimension_semantics=("arbitrary", "arbitrary"),
                                             vmem_limit_bytes=VMEM_LIMIT, has_side_effects=True,
                                             collective_id=barrier_id),
    )(*args)


def _grad_rows(xr, dz, w, mod, ng, dres, ncols, tm, name, chip_sums=(), first_chips=None, dests=None):
    rows = xr.shape[0]
    steps = rows // tm
    with_dx = dres is not None
    nr = len(chip_sums)
    dests = [d for d in (dests or [None] * nr)]
    nd = sum(d is not None for d in dests)
    nin = 6 if with_dx else 5
    nout = 2 if with_dx else 1

    def body(*refs):
        if with_dx:
            x_ref, dz_ref, w_ref, sc_ref, ng_ref, dres_ref = refs[:nin]
            dx_ref, vec_ref = refs[nin + nr + nd:nin + nr + nd + nout]
        else:
            x_ref, dz_ref, w_ref, sc_ref, ng_ref = refs[:nin]
            (vec_ref,) = refs[nin + nr + nd:nin + nr + nd + nout]
        if nr:
            o0 = nin + nr + nd + nout
            start, forward, finish = _chips_ops(refs[nin:nin + nr], refs[o0:o0 + nr], refs[o0 + nr:o0 + 2 * nr],
                                                *refs[o0 + 2 * nr:o0 + 2 * nr + 3], first_chips=first_chips,
                                                barrier=True)
            w_hbm, w_ref, w_sem = w_ref, refs[o0 + 2 * nr + 3], refs[o0 + 2 * nr + 4]

            @pl.when(pl.program_id(0) == 0)
            def _():
                start()
                w_load = pltpu.make_async_copy(w_hbm, w_ref, w_sem)
                w_load.start()
                w_load.wait()

            pl.when(pl.program_id(0) == steps // 2)(forward)
            pl.when(pl.program_id(0) == steps - 1)(finish)

        @pl.when(pl.program_id(0) == 0)
        def _():
            vec_ref[...] = jnp.zeros_like(vec_ref)

        dhn = _dot_nt(dz_ref[...], w_ref[...])
        x = x_ref[...]
        rs = lax.rsqrt(jnp.mean(x * x, axis=-1, keepdims=True) + NORM_EPS)
        xh = x * rs
        ngv = ng_ref[...]
        y = xh * ngv
        vec_ref[0:1, :] += jnp.sum(dhn, axis=0, keepdims=True)
        vec_ref[1:2, :] += jnp.sum(dhn * y, axis=0, keepdims=True)
        dy = dhn * (1.0 + sc_ref[...])
        vec_ref[2:3, :] += jnp.sum(dy * xh, axis=0, keepdims=True)
        if with_dx:
            dxh = dy * ngv
            dx_ref[...] = dres_ref[...] + rs * (dxh - xh * jnp.mean(dxh * xh, axis=-1, keepdims=True))

    tile = pl.BlockSpec((tm, D), lambda i: (i, 0))
    vec = pl.BlockSpec((1, D), lambda i: (0, 0))
    w_spec = HBM if nr else pl.BlockSpec((D, ncols), lambda i: (0, 0), pipeline_mode=pl.Buffered(1))
    in_specs = [tile, pl.BlockSpec((tm, ncols), lambda i: (i, 0)), w_spec, pl.BlockSpec((1, D), lambda i: (0, 1)), vec]
    out_shape = [jax.ShapeDtypeStruct((8, D), F32)]
    out_specs = [pl.BlockSpec((8, D), lambda i: (0, 0))]
    args = [xr, dz, pltpu.with_memory_space_constraint(w, pltpu.HBM) if nr else w, mod, ng]
    if with_dx:
        in_specs.append(tile)
        out_shape.insert(0, jax.ShapeDtypeStruct((rows, D), F32))
        out_specs.insert(0, tile)
        args.append(dres)
    aliases = {}
    for j, d in enumerate(dests):
        if d is not None:
            aliases[len(args) + nr + len(aliases)] = len(out_shape) + j
    in_specs += [HBM] * (nr + nd)
    out_specs += [HBM] * (2 * nr)
    out_shape += [jax.ShapeDtypeStruct((4,) + a.shape[1:], a.dtype) for a in chip_sums]
    out_shape += _chips_stage_shapes(chip_sums)
    args += [pltpu.with_memory_space_constraint(a, pltpu.HBM) for a in chip_sums]
    args += [pltpu.with_memory_space_constraint(d, pltpu.HBM) for d in dests if d is not None]
    return _call(body, name=name, grid=(steps,), out_shape=out_shape, in_specs=in_specs, out_specs=out_specs,
                 scratch_shapes=(_chips_sems(nr) + [pltpu.VMEM((D, ncols), BF16), pltpu.SemaphoreType.DMA(())])
                 if nr else [], input_output_aliases=aliases,
                 compiler_params=pltpu.CompilerParams(dimension_semantics=("arbitrary",),
                                                      vmem_limit_bytes=VMEM_LIMIT, has_side_effects=bool(nr),
                                                      collective_id=7 if nr else None))(*args)


def _adamw(w, g, m, v):
    m = ADAM_B1 * m + (1.0 - ADAM_B1) * g
    v = ADAM_B2 * v + (1.0 - ADAM_B2) * (g * g)
    m_hat = m / (1.0 - ADAM_B1 ** ADAM_STEP)
    v_hat = v / (1.0 - ADAM_B2 ** ADAM_STEP)
    delta = -ADAM_LR * (m_hat / (jnp.sqrt(v_hat) + ADAM_EPS) + ADAM_WD * w)
    return delta, m, v


def _adamw_reduced(parts, w, m, v, tr, name):
    r, n = w.shape
    nparts = parts.shape[0]

    def body(p_ref, w_ref, m_ref, v_ref, g_ref, d_ref, mo_ref, vo_ref):
        g = p_ref[0].astype(F32)
        for i in range(1, nparts):
            g = g + p_ref[i].astype(F32)
        g_ref[...] = g
        d_ref[...], mo_ref[...], vo_ref[...] = _adamw(w_ref[...], g, m_ref[...], v_ref[...])

    tile = pl.BlockSpec((tr, n), lambda i: (i, 0))
    sds = jax.ShapeDtypeStruct((r, n), F32)
    return _call(
        body, name=name, grid=(r // tr,), out_shape=[sds] * 4,
        in_specs=[pl.BlockSpec((nparts, tr, n), lambda i: (0, i, 0)), tile, tile, tile], out_specs=[tile] * 4,
        compiler_params=_params("arbitrary"),
    )(parts, w, m, v)


R_GATE, R_FINAL_G, R_LN_G, R_LN_B, R_LOSS = 0, 1, 2, 3, 4
R_SH_X, R_SC_X, R_NG_X = 5, 6, 7
R_SH_C, R_SC_C, R_NG_C = 8, 9, 10
R_LAM, R_CW, R_CB = 11, 13, 17
PACK_ROWS = 24
Q_BA, Q_BX, Q_SGU_B, PACK128_ROWS = 0, 16, 32, 40


def _reduce_small(vec_pieces, q_pieces, mat_parts, ada_w, me):
    nloc = ada_w.shape[1]
    nm = len(mat_parts)
    pieces = list(vec_pieces) + list(q_pieces)

    def body(me_ref, *refs):
        piece_refs, refs = refs[:len(pieces)], refs[len(pieces):]
        mp_refs, w_ref = refs[:nm], refs[nm]
        red_ref, redq_ref = refs[nm + 1:nm + 3]
        mats_all = refs[nm + 3:2 * nm + 3]
        cparts_ref, dmod_ref, gab_ref, loss_ref = refs[2 * nm + 3:2 * nm + 7]
        pack_ref, packq_ref, vp_ref, vq_ref = refs[2 * nm + 7:2 * nm + 11]
        mat_refs = refs[2 * nm + 11:3 * nm + 11]
        cpart_ref, dmc_s = refs[3 * nm + 11:3 * nm + 13]
        sems = refs[3 * nm + 13:]
        for dst, group in ((pack_ref, vec_pieces), (packq_ref, q_pieces)):
            row = 0
            for _, nrows in group:
                dst[row:row + nrows, :] = piece_refs[0][0:nrows, :]
                piece_refs, row = piece_refs[1:], row + nrows
            if row < dst.shape[0]:
                dst[row:, :] = jnp.zeros((dst.shape[0] - row, dst.shape[1]), F32)
        p_start, p_forward, p_finish = _gather2_ops([pack_ref, packq_ref], [vp_ref, vq_ref], ["ag", "ag"], *sems[:3],
                                                    barrier=True)
        m_start, m_forward, m_finish = _gather2_ops(mat_refs, mats_all, ["ag"] * nm, *sems[3:6])
        c_start, c_forward, c_finish = _gather2_ops([cpart_ref], [cparts_ref], ["ag"], *sems[6:])
        p_start()
        for mp_ref, mat_ref in zip(mp_refs, mat_refs):
            mat = mp_ref[0].astype(F32)
            for i in range(1, mp_ref.shape[0]):
                mat = mat + mp_ref[i].astype(F32)
            mat_ref[...] = mat
        m_start()
        p_forward()
        p_finish()
        red, redq = vp_ref[0], vq_ref[0]
        for i in range(1, N_DEV):
            red = red + vp_ref[i]
            redq = redq + vq_ref[i]
        red_ref[...] = red
        redq_ref[...] = redq
        loss_ref[...] = red_ref[R_LOSS:R_LOSS + 1, 0:1]
        for e in range(N_DEV):
            dmod_ref[e:e + 1, 0:D] = vp_ref[e, R_SH_X:R_SH_X + 1, :]
            dmod_ref[e:e + 1, D:2 * D] = vp_ref[e, R_SC_X:R_SC_X + 1, :]
            dmod_ref[e:e + 1, 2 * D:3 * D] = vp_ref[e, R_GATE:R_GATE + 1, :]
        dmod_ref[8:9, 0:D] = red[R_SH_C:R_SH_C + 1, :]
        dmod_ref[8:9, D:2 * D] = red[R_SC_C:R_SC_C + 1, :]
        dmod_ref[8:9, 2 * D:3 * D] = jnp.zeros((1, D), F32)
        dmod_ref[9:16, :] = jnp.zeros((7, 3 * D), F32)
        gab_ref[:, 0:D] = red[R_SH_X:R_SH_X + 1, :] + red[R_SH_C:R_SH_C + 1, :]
        gab_ref[:, D:2 * D] = red[R_SC_X:R_SC_X + 1, :] + red[R_SC_C:R_SC_C + 1, :]
        gab_ref[:, 2 * D:3 * D] = red[R_GATE:R_GATE + 1, :]
        dmc_s[...] = jnp.broadcast_to(dmod_ref[8:9, :], (8, 3 * D))
        off = pl.multiple_of(me_ref[0] * nloc, 128)
        cpart_ref[...] = _dot_nt(dmc_s[:, pl.ds(off, nloc)], w_ref[...])
        c_start()
        m_forward()
        c_forward()
        c_finish()
        m_finish()

    return _call(
        body, name="reduce_small",
        out_shape=[jax.ShapeDtypeStruct((PACK_ROWS, D), F32), jax.ShapeDtypeStruct((PACK128_ROWS, HD), F32)]
        + [jax.ShapeDtypeStruct((N_DEV,) + p.shape[1:], F32) for p in mat_parts]
        + [jax.ShapeDtypeStruct((N_DEV, 8, D), F32), jax.ShapeDtypeStruct((16, 3 * D), F32),
           jax.ShapeDtypeStruct((1, 3 * D), F32), jax.ShapeDtypeStruct((1, 1), F32)],
        in_specs=[pl.BlockSpec(memory_space=pltpu.SMEM)] + [VMEM] * (len(pieces) + nm + 1),
        out_specs=[VMEM] * (nm + 6),
        scratch_shapes=[pltpu.VMEM((PACK_ROWS, D), F32), pltpu.VMEM((PACK128_ROWS, HD), F32),
                        pltpu.VMEM((N_DEV, PACK_ROWS, D), F32), pltpu.VMEM((N_DEV, PACK128_ROWS, HD), F32)]
        + [pltpu.VMEM(p.shape[1:], F32) for p in mat_parts]
        + [pltpu.VMEM((8, D), F32), pltpu.VMEM((8, 3 * D), F32)] + _gather2_sems(2) + _gather2_sems(nm)
        + _gather2_sems(1),
        compiler_params=pltpu.CompilerParams(vmem_limit_bytes=VMEM_LIMIT, has_side_effects=True, collective_id=8),
    )(me, *[a for a, _ in pieces], *mat_parts, ada_w)


def _adamw_ada(c_all, c_ctx, dmod, w, m, v, me):
    nloc = w.shape[1]

    def body(me_ref, c_ref, cc_ref, dm_ref, w_ref, m_ref, v_ref, g_ref, d_ref, mo_ref, vo_ref):
        off = pl.multiple_of(me_ref[0] * nloc, 128)
        dm = dm_ref[:, pl.ds(off, nloc)]
        sx, _ = _silu_and_grad(c_ref[...])
        sc, _ = _silu_and_grad(cc_ref[...])
        g = _dot_tn(sx, dm[0:8, :]) + _dot_tn(jnp.broadcast_to(sc, (8, D)), dm[8:16, :])
        g_ref[...] = g
        d_ref[...], mo_ref[...], vo_ref[...] = _adamw(w_ref[...], g, m_ref[...], v_ref[...])

    sds = jax.ShapeDtypeStruct(w.shape, F32)
    return _call(
        body, name="adamw_ada_w", out_shape=[sds] * 4,
        in_specs=[pl.BlockSpec(memory_space=pltpu.SMEM)] + [VMEM] * 6, out_specs=[VMEM] * 4,
        compiler_params=_params(),
    )(me, c_all, c_ctx, dmod, w, m, v)


_SMALL = ("c_ctx", "ada_b", "norm_g", "conv_w", "conv_b", "lru_wa", "lru_ba", "lru_wx", "lru_bx", "lru_lambda",
          "sgu_ln_g", "sgu_ln_b", "sgu_w", "sgu_b", "final_g")


def _adamw_small(red, redq, mats, cparts, gab, ws, ms, vs, me):
    n = len(_SMALL)

    def body(me_ref, red_ref, redq_ref, wa_ref, wx_ref, sw_ref, cp_ref, gab_ref, *refs):
        w_refs, m_refs, v_refs = refs[:n], refs[n:2 * n], refs[2 * n:3 * n]
        outs = refs[3 * n:]
        off = pl.multiple_of(me_ref[0] * HD, 128)

        def row(r, k=1):
            return red_ref[r:r + k, :]

        cc = w_refs[0][...]
        dcc = cp_ref[0, 0:1, :]
        for i in range(1, N_DEV):
            dcc = dcc + cp_ref[i, 0:1, :]
        grads = dict(
            c_ctx=dcc * _silu_and_grad(cc)[1], ada_b=gab_ref[...], norm_g=row(R_NG_X) + row(R_NG_C),
            conv_w=red_ref[R_CW:R_CW + CONV_W, pl.ds(off, HD)], conv_b=row(R_CB),
            lru_wa=wa_ref[...], lru_ba=redq_ref[Q_BA:Q_BA + 2 * HEADS, :], lru_wx=wx_ref[...],
            lru_bx=redq_ref[Q_BX:Q_BX + 2 * HEADS, :], lru_lambda=red_ref[R_LAM:R_LAM + 2, pl.ds(off, HD)],
            sgu_ln_g=row(R_LN_G), sgu_ln_b=row(R_LN_B), sgu_w=sw_ref[...],
            sgu_b=redq_ref[Q_SGU_B:Q_SGU_B + HEADS, :], final_g=row(R_FINAL_G))
        for j, name in enumerate(_SMALL):
            g = grads[name]
            outs[j][...] = g
            outs[n + j][...], outs[2 * n + j][...], outs[3 * n + j][...] = _adamw(w_refs[j][...], g, m_refs[j][...],
                                                                                 v_refs[j][...])

    sds = [jax.ShapeDtypeStruct(ws[k].shape, F32) for k in _SMALL]
    outs = _call(
        body, name="adamw_small", out_shape=sds * 4,
        in_specs=[pl.BlockSpec(memory_space=pltpu.SMEM)] + [VMEM] * (7 + 3 * n), out_specs=[VMEM] * (4 * n),
        compiler_params=_params(),
    )(me, red, redq, *mats, cparts, gab, *[ws[k] for k in _SMALL], *[ms[k] for k in _SMALL],
      *[vs[k] for k in _SMALL])
    return [dict(zip(_SMALL, outs[i * n:(i + 1) * n])) for i in range(4)]


def kernel(x, c, ctx, c_ctx, ada_w, ada_b, norm_g, w_in, conv_w, conv_b, lru_wa, lru_ba, lru_wx, lru_bx, lru_lambda, sgu_ln_g, sgu_ln_b, sgu_w, sgu_b, w_out, final_g, loss_target, m_c_ctx, m_ada_w, m_ada_b, m_norm_g, m_w_in, m_conv_w, m_conv_b, m_lru_wa, m_lru_ba, m_lru_wx, m_lru_bx, m_lru_lambda, m_sgu_ln_g, m_sgu_ln_b, m_sgu_w, m_sgu_b, m_w_out, m_final_g, v_c_ctx, v_ada_w, v_ada_b, v_norm_g, v_w_in, v_conv_w, v_conv_b, v_lru_wa, v_lru_ba, v_lru_wx, v_lru_bx, v_lru_lambda, v_sgu_ln_g, v_sgu_ln_b, v_sgu_w, v_sgu_b, v_w_out, v_final_g):
    args = dict(locals())
    me = (4 * lax.axis_index("x") + 2 * lax.axis_index("y") + lax.axis_index("c")).astype(jnp.int32).reshape(1)
    xr, ctxr, tgt = x[0], ctx[0], loss_target[0]
    cc = c_ctx.reshape(1, D)
    nw = 2 * HEADS * HD
    view = dict(c_ctx=(1, D), ada_b=(1, 3 * D), norm_g=(1, D), conv_w=(CONV_W, HD), conv_b=(1, D), lru_wa=(nw, HD),
                lru_ba=(2 * HEADS, HD), lru_wx=(nw, HD), lru_bx=(2 * HEADS, HD), lru_lambda=(2, HD), sgu_ln_g=(1, D),
                sgu_ln_b=(1, D), sgu_w=(HEADS * CHUNK, CHUNK), sgu_b=(HEADS, CHUNK), final_g=(1, D))

    zx, hn, w_full, w_out_b, modx, modc, c_all, cw_full, lam_full = _front_project(
        xr, c, cc, ada_w[0], ada_b, norm_g, w_in[0], w_out[0], conv_w[0], lru_lambda[0], me)
    zc, hnc = _project(ctxr, modc, norm_g, w_full, D, LC, "project_ctx")
    ba, bx = lru_ba.reshape(view["lru_ba"]), lru_bx.reshape(view["lru_bx"])
    yl, wout_all = _lru_forward(zx, zc, cw_full, conv_b, lru_wa[0], lru_wx[0], ba, bx, lam_full, [w_out_b], ["ag"])
    wout_full = wout_all.reshape(D_MIX, D)
    dz, dyl, dxn, ycat, dob, dws, dbst, mvec = _mixer_loss(
        xr, tgt, zx, yl, modx, final_g.reshape(1, D), sgu_ln_g, sgu_ln_b, sgu_w[0], sgu_b[0], wout_full, ROWS)

    (wout_sums,) = _grad_w(ycat, dob, None, None, L // 2, "grad_w_out", D, 0, 1, "rows", 1)
    (rest_sums,) = _grad_w(hn, dz, None, None, L, "grad_w_in_rest", 2 * W_IN_SHARD, 1, 3, "cols", 2)
    dz, dxac, dwa, dwx, dba, dbx, dlam, dcw, dcb, win_parts, wout_parts, _, _ = _lru_backward(
        zx, zc, dyl, dz, cw_full, conv_b, lru_wa[0], lru_wx[0], ba, bx, lam_full, [rest_sums, wout_sums],
        first_chips=[1, 0])
    mats = [dwa.reshape(N_DEV, nw // N_DEV, HD), dwx.reshape(N_DEV, nw // N_DEV, HD), dws]
    first_sums, *mat_parts = _grad_w(hn, dz, hnc, dxac, L, "grad_w_in_first", 2 * W_IN_SHARD, 0, 1, "cols", 3,
                                     riders=mats)[:4]
    gx, xvec, win_parts = _grad_rows(
        xr, dz, w_full, modx, norm_g, dxn, D_IN, ROWS, "grad_rows_x", chip_sums=[first_sums], first_chips=[0],
        dests=[win_parts])[:3]
    (cvec,) = _grad_rows(ctxr, dxac, w_full, modc, norm_g, None, D, LC, "grad_rows_ctx")
    red, redq, *rest = _reduce_small(
        [(mvec, 5), (xvec, 3), (cvec, 3), (dlam, 2), (dcw, CONV_W), (dcb, 1)],
        [(dba, 2 * HEADS), (dbx, 2 * HEADS), (dbst, HEADS)], mat_parts, ada_w[0], me)
    mats_all, (cparts, dmod, gab, loss) = rest[:3], rest[3:]

    g_w_in, d_w_in, nm_w_in, nv_w_in = _adamw_reduced(win_parts, w_in[0], m_w_in[0], v_w_in[0], 2 * ROWS, "adamw_w_in")
    g_w_out, d_w_out, nm_w_out, nv_w_out = _adamw_reduced(wout_parts, w_out[0], m_w_out[0], v_w_out[0], ROWS // 2,
                                                          "adamw_w_out")
    g_ada, d_ada, nm_ada, nv_ada = _adamw_ada(c_all, cc, dmod, ada_w[0], m_ada_w[0], v_ada_w[0], me)
    ws = {k: args[k].reshape(view[k]) for k in _SMALL}
    ms = {k: args["m_" + k].reshape(view[k]) for k in _SMALL}
    vs = {k: args["v_" + k].reshape(view[k]) for k in _SMALL}
    small = _adamw_small(red, redq, [m.reshape(-1, HD) for m in mats_all], cparts, gab, ws, ms, vs, me)
    big = dict(w_in=(g_w_in, d_w_in, nm_w_in, nv_w_in), w_out=(g_w_out, d_w_out, nm_w_out, nv_w_out),
               ada_w=(g_ada, d_ada, nm_ada, nv_ada))

    loss = loss.reshape(())
    names = ("c_ctx", "ada_w", "ada_b", "norm_g", "w_in", "conv_w", "conv_b", "lru_wa", "lru_ba", "lru_wx", "lru_bx",
             "lru_lambda", "sgu_ln_g", "sgu_ln_b", "sgu_w", "sgu_b", "w_out", "final_g")
    outs = [loss, gx.reshape(x.shape)]
    for kind in range(4):
        for k in names:
            val = big[k][kind] if k in big else small[kind][k]
            outs.append(val.reshape(args[k].shape))
    return tuple(outs)
```
